```python
import jax, jax.numpy as jnp
from jax import lax
import numpy as np

D_MODEL = 1024
BATCH = 8
SEQ = 2048
DEPTH = 4

MIX_WIDTH = D_MODEL
POOL_WIDTH = MIX_WIDTH // 4
POOL_WINDOWS = (2, 4, 8, 16)
POOL_GROUPS = len(POOL_WINDOWS)
POOL_GROUP_DIM = POOL_WIDTH // POOL_GROUPS
HEAD_DIM = 64
ATTN_WIDTH = MIX_WIDTH - POOL_WIDTH
N_HEADS = ATTN_WIDTH // HEAD_DIM
DILATED_PATTERNS = ((128, 1), (512, 4), (2048, 16))
ROPE_THETA = 500000.0
ROPE_DIM = HEAD_DIM // 4
D_FF = 2816
IN_PROJ_WIDTH = POOL_WIDTH + 3 * ATTN_WIDTH
NORM_EPS = 1e-6
MASK_VALUE = -1e30

kernel_name = "hybrid_pool_dilated_attn_macaron_encoder"


def rmsnorm(x, g):
    xf = x.astype(jnp.float32)
    y = xf * lax.rsqrt(jnp.mean(xf * xf, axis=-1, keepdims=True) + NORM_EPS)
    return (y * g.astype(jnp.float32)).astype(x.dtype)


def swiglu(h, w_gate, w_up, w_down):
    return (jax.nn.silu(h @ w_gate) * (h @ w_up)) @ w_down


def rope_tables(positions):
    inv_freq = ROPE_THETA ** (-jnp.arange(0, ROPE_DIM, 2, dtype=jnp.float32) / ROPE_DIM)
    ang = positions.astype(jnp.float32)[..., None] * inv_freq
    return jnp.cos(ang)[:, :, None, :], jnp.sin(ang)[:, :, None, :]


def apply_partial_rope(t, cos, sin):
    tf = t.astype(jnp.float32)
    half = ROPE_DIM // 2
    t1, t2, rest = tf[..., :half], tf[..., half:ROPE_DIM], tf[..., ROPE_DIM:]
    rot = jnp.concatenate([t1 * cos - t2 * sin, t2 * cos + t1 * sin, rest], axis=-1)
    return rot.astype(t.dtype)


def multiscale_pool(v, pool_w, pool_scale):
    B, S, _ = v.shape
    vf = v.astype(jnp.float32).reshape(B, S, POOL_GROUPS, POOL_GROUP_DIM)
    cs = jnp.pad(lax.cumsum(vf, axis=1), ((0, 0), (1, 0), (0, 0), (0, 0)))
    pos = jnp.arange(S)
    means = []
    for g, w in enumerate(POOL_WINDOWS):
        lo = jnp.maximum(pos - w // 2, 0)
        hi = jnp.minimum(pos + w - 1 - w // 2, S - 1)
        cnt = (hi - lo + 1).astype(jnp.float32)
        means.append((cs[:, hi + 1, g] - cs[:, lo, g]) / cnt[None, :, None])
    pooled = jnp.stack(means, axis=2)
    diff = (pooled - vf).astype(v.dtype)
    y = jnp.einsum('bsgc,gcd->bsgd', diff, pool_w).reshape(B, S, POOL_WIDTH)
    return y * pool_scale


def dilated_branch(q, k, v, window, dilation):
    B, S, H, Dh = q.shape
    half = window // (2 * dilation)
    blk = half
    L = S // dilation
    nb = -(-L // blk)
    Lp = nb * blk

    def to_compressed(t):
        t = t.astype(jnp.float32).reshape(B, L, dilation, H, Dh)
        return jnp.pad(t, ((0, 0), (0, Lp - L), (0, 0), (0, 0), (0, 0)))

    def band(t):
        tp = jnp.pad(t, ((0, 0), (blk, blk), (0, 0), (0, 0), (0, 0)))
        tp = tp.reshape(B, nb + 2, blk, dilation, H, Dh)
        return jnp.concatenate([tp[:, :-2], tp[:, 1:-1], tp[:, 2:]], axis=2)

    qb = to_compressed(q).reshape(B, nb, blk, dilation, H, Dh)
    kb = band(to_compressed(k))
    vb = band(to_compressed(v))

    t_idx = jnp.arange(nb)[:, None] * blk + jnp.arange(blk)[None, :]
    j_idx = jnp.arange(nb)[:, None] * blk - blk + jnp.arange(3 * blk)[None, :]
    jj = j_idx[:, None, :]
    valid = (jnp.abs(jj - t_idx[:, :, None]) <= half) & (jj >= 0) & (jj < L)

    scale = 1.0 / np.sqrt(Dh)
    s = jnp.einsum('bnqrhd,bnkrhd->bnrhqk', qb, kb) * scale
    s = jnp.where(valid[None, :, None, None], s, MASK_VALUE)
    m = jnp.max(s, axis=-1, keepdims=True)
    p = jnp.exp(s - m)
    denom = jnp.sum(p, axis=-1)
    num = jnp.einsum('bnrhqk,bnkrhd->bnqrhd', p, vb)

    num = num.reshape(B, Lp, dilation, H, Dh)[:, :L].reshape(B, S, H, Dh)

    def stat_back(t):
        t = jnp.transpose(t, (0, 1, 4, 2, 3)).reshape(B, Lp, dilation, H)
        return t[:, :L].reshape(B, S, H)

    return num, stat_back(m[..., 0]), stat_back(denom)


def dilated_mixture_attention(q, k, v):
    branches = [dilated_branch(q, k, v, w, d) for (w, d) in DILATED_PATTERNS]
    m_all = jnp.stack([b[1] for b in branches], axis=0)
    wts = jnp.exp(m_all - jnp.max(m_all, axis=0, keepdims=True))
    num = sum(wts[i][..., None] * branches[i][0] for i in range(len(branches)))
    den = sum(wts[i] * branches[i][2] for i in range(len(branches)))
    return (num / den[..., None]).astype(q.dtype)


def _fwd_setup_inputs(seed: int = 0) -> dict:
    key = jax.random.key(seed)
    ks = jax.random.split(key, 20)
    f32 = jnp.float32

    def normal(k, shape, fan_in):
        return jax.random.normal(k, shape, f32) * (fan_in ** -0.5)

    def gain(k, shape):
        return jnp.ones(shape, f32) + 0.02 * jax.random.normal(k, shape, f32)

    x = jax.random.normal(ks[0], (BATCH, SEQ, D_MODEL), f32)
    start = jax.random.randint(ks[1], (BATCH, 1), 0, 4096, dtype=jnp.int32)
    positions = start + jnp.arange(SEQ, dtype=jnp.int32)[None, :]
    return {
        "x": x,
        "positions": positions,
        "ffn1_norm": gain(ks[2], (DEPTH, D_MODEL)),
        "ffn1_w_gate": normal(ks[3], (DEPTH, D_MODEL, D_FF), D_MODEL),
        "ffn1_w_up": normal(ks[4], (DEPTH, D_MODEL, D_FF), D_MODEL),
        "ffn1_w_down": normal(ks[5], (DEPTH, D_FF, D_MODEL), D_FF),
        "mix_norm": gain(ks[6], (DEPTH, D_MODEL)),
        "w_in": normal(ks[7], (DEPTH, D_MODEL, IN_PROJ_WIDTH), D_MODEL),
        "pool_w": normal(ks[8], (DEPTH, POOL_GROUPS, POOL_GROUP_DIM, POOL_GROUP_DIM), POOL_GROUP_DIM),
        "pool_scale": gain(ks[9], (DEPTH, POOL_WIDTH)),
        "w_out": normal(ks[10], (DEPTH, MIX_WIDTH, D_MODEL), MIX_WIDTH),
        "ffn2_norm": gain(ks[11], (DEPTH, D_MODEL)),
        "ffn2_w_gate": normal(ks[12], (DEPTH, D_MODEL, D_FF), D_MODEL),
        "ffn2_w_up": normal(ks[13], (DEPTH, D_MODEL, D_FF), D_MODEL),
        "ffn2_w_down": normal(ks[14], (DEPTH, D_FF, D_MODEL), D_FF),
        "final_norm": gain(ks[15], (D_MODEL,)),
    }


def _fwd_reference(x, positions, ffn1_norm, ffn1_w_gate, ffn1_w_up, ffn1_w_down, mix_norm, w_in,
              pool_w, pool_scale, w_out, ffn2_norm, ffn2_w_gate, ffn2_w_up, ffn2_w_down, final_norm):
    B, S, _ = x.shape
    cos, sin = rope_tables(positions)
    for l in range(DEPTH):
        x = x + 0.5 * swiglu(rmsnorm(x, ffn1_norm[l]), ffn1_w_gate[l], ffn1_w_up[l], ffn1_w_down[l])

        h = rmsnorm(x, mix_norm[l])
        proj = h @ w_in[l]
        v_pool = proj[..., :POOL_WIDTH]
        q = proj[..., POOL_WIDTH:POOL_WIDTH + ATTN_WIDTH].reshape(B, S, N_HEADS, HEAD_DIM)
        k = proj[..., POOL_WIDTH + ATTN_WIDTH:POOL_WIDTH + 2 * ATTN_WIDTH].reshape(B, S, N_HEADS, HEAD_DIM)
        v = proj[..., POOL_WIDTH + 2 * ATTN_WIDTH:].reshape(B, S, N_HEADS, HEAD_DIM)

        y_pool = multiscale_pool(v_pool, pool_w[l], pool_scale[l])
        q = apply_partial_rope(q, cos, sin)
        k = apply_partial_rope(k, cos, sin)
        y_attn = dilated_mixture_attention(q, k, v).reshape(B, S, ATTN_WIDTH)

        mixed = jnp.concatenate([y_pool.astype(x.dtype), y_attn.astype(x.dtype)], axis=-1)
        x = x + mixed @ w_out[l]

        x = x + 0.5 * swiglu(rmsnorm(x, ffn2_norm[l]), ffn2_w_gate[l], ffn2_w_up[l], ffn2_w_down[l])
    return rmsnorm(x, final_norm)


import jax as _jax
import jax.numpy as _jnp

TWIN_FORMAT = 'train_step'
FWD_PARAMS = ['x', 'positions', 'ffn1_norm', 'ffn1_w_gate', 'ffn1_w_up', 'ffn1_w_down', 'mix_norm', 'w_in', 'pool_w', 'pool_scale', 'w_out', 'ffn2_norm', 'ffn2_w_gate', 'ffn2_w_up', 'ffn2_w_down', 'final_norm']
TWIN_WEIGHTS = ['ffn1_norm', 'ffn1_w_gate', 'ffn1_w_up', 'ffn1_w_down', 'mix_norm', 'w_in', 'pool_w', 'pool_scale', 'w_out', 'ffn2_norm', 'ffn2_w_gate', 'ffn2_w_up', 'ffn2_w_down', 'final_norm']
TWIN_DIFF_INPUT = 'x'
TWIN_INPUTS = ['x', 'positions', 'ffn1_norm', 'ffn1_w_gate', 'ffn1_w_up', 'ffn1_w_down', 'mix_norm', 'w_in', 'pool_w', 'pool_scale', 'w_out', 'ffn2_norm', 'ffn2_w_gate', 'ffn2_w_up', 'ffn2_w_down', 'final_norm', 'loss_target', 'm_ffn1_norm', 'm_ffn1_w_gate', 'm_ffn1_w_up', 'm_ffn1_w_down', 'm_mix_norm', 'm_w_in', 'm_pool_w', 'm_pool_scale', 'm_w_out', 'm_ffn2_norm', 'm_ffn2_w_gate', 'm_ffn2_w_up', 'm_ffn2_w_down', 'm_final_norm', 'v_ffn1_norm', 'v_ffn1_w_gate', 'v_ffn1_w_up', 'v_ffn1_w_down', 'v_mix_norm', 'v_w_in', 'v_pool_w', 'v_pool_scale', 'v_w_out', 'v_ffn2_norm', 'v_ffn2_w_gate', 'v_ffn2_w_up', 'v_ffn2_w_down', 'v_final_norm']
TWIN_OUTPUTS = ['loss', 'grad_x', 'grad_ffn1_norm', 'grad_ffn1_w_gate', 'grad_ffn1_w_up', 'grad_ffn1_w_down', 'grad_mix_norm', 'grad_w_in', 'grad_pool_w', 'grad_pool_scale', 'grad_w_out', 'grad_ffn2_norm', 'grad_ffn2_w_gate', 'grad_ffn2_w_up', 'grad_ffn2_w_down', 'grad_final_norm', 'delta_ffn1_norm', 'delta_ffn1_w_gate', 'delta_ffn1_w_up', 'delta_ffn1_w_down', 'delta_mix_norm', 'delta_w_in', 'delta_pool_w', 'delta_pool_scale', 'delta_w_out', 'delta_ffn2_norm', 'delta_ffn2_w_gate', 'delta_ffn2_w_up', 'delta_ffn2_w_down', 'delta_final_norm', 'new_m_ffn1_norm', 'new_m_ffn1_w_gate', 'new_m_ffn1_w_up', 'new_m_ffn1_w_down', 'new_m_mix_norm', 'new_m_w_in', 'new_m_pool_w', 'new_m_pool_scale', 'new_m_w_out', 'new_m_ffn2_norm', 'new_m_ffn2_w_gate', 'new_m_ffn2_w_up', 'new_m_ffn2_w_down', 'new_m_final_norm', 'new_v_ffn1_norm', 'new_v_ffn1_w_gate', 'new_v_ffn1_w_up', 'new_v_ffn1_w_down', 'new_v_mix_norm', 'new_v_w_in', 'new_v_pool_w', 'new_v_pool_scale', 'new_v_w_out', 'new_v_ffn2_norm', 'new_v_ffn2_w_gate', 'new_v_ffn2_w_up', 'new_v_ffn2_w_down', 'new_v_final_norm']
TWIN_LEAF_KINDS = {'loss': 'loss', 'grad_x': 'grad_x', 'grad_ffn1_norm': 'grad_w', 'grad_ffn1_w_gate': 'grad_w', 'grad_ffn1_w_up': 'grad_w', 'grad_ffn1_w_down': 'grad_w', 'grad_mix_norm': 'grad_w', 'grad_w_in': 'grad_w', 'grad_pool_w': 'grad_w', 'grad_pool_scale': 'grad_w', 'grad_w_out': 'grad_w', 'grad_ffn2_norm': 'grad_w', 'grad_ffn2_w_gate': 'grad_w', 'grad_ffn2_w_up': 'grad_w', 'grad_ffn2_w_down': 'grad_w', 'grad_final_norm': 'grad_w', 'delta_ffn1_norm': 'delta_w', 'delta_ffn1_w_gate': 'delta_w', 'delta_ffn1_w_up': 'delta_w', 'delta_ffn1_w_down': 'delta_w', 'delta_mix_norm': 'delta_w', 'delta_w_in': 'delta_w', 'delta_pool_w': 'delta_w', 'delta_pool_scale': 'delta_w', 'delta_w_out': 'delta_w', 'delta_ffn2_norm': 'delta_w', 'delta_ffn2_w_gate': 'delta_w', 'delta_ffn2_w_up': 'delta_w', 'delta_ffn2_w_down': 'delta_w', 'delta_final_norm': 'delta_w', 'new_m_ffn1_norm': 'new_m', 'new_m_ffn1_w_gate': 'new_m', 'new_m_ffn1_w_up': 'new_m', 'new_m_ffn1_w_down': 'new_m', 'new_m_mix_norm': 'new_m', 'new_m_w_in': 'new_m', 'new_m_pool_w': 'new_m', 'new_m_pool_scale': 'new_m', 'new_m_w_out': 'new_m', 'new_m_ffn2_norm': 'new_m', 'new_m_ffn2_w_gate': 'new_m', 'new_m_ffn2_w_up': 'new_m', 'new_m_ffn2_w_down': 'new_m', 'new_m_final_norm': 'new_m', 'new_v_ffn1_norm': 'new_v', 'new_v_ffn1_w_gate': 'new_v', 'new_v_ffn1_w_up': 'new_v', 'new_v_ffn1_w_down': 'new_v', 'new_v_mix_norm': 'new_v', 'new_v_w_in': 'new_v', 'new_v_pool_w': 'new_v', 'new_v_pool_scale': 'new_v', 'new_v_w_out': 'new_v', 'new_v_ffn2_norm': 'new_v', 'new_v_ffn2_w_gate': 'new_v', 'new_v_ffn2_w_up': 'new_v', 'new_v_ffn2_w_down': 'new_v', 'new_v_final_norm': 'new_v'}


def _forward(args):
    return _fwd_reference(*[args[k] for k in FWD_PARAMS])


def _output_shape():
    out = _jax.eval_shape(lambda: _forward(_fwd_setup_inputs(0)))
    return out.shape, out.dtype

N_MICROBATCH = 1
ADAM_LR = 0.001
ADAM_B1 = 0.9
ADAM_B2 = 0.999
ADAM_EPS = 1e-08
ADAM_WD = 0.01
ADAM_STEP = 10
PER_EXAMPLE_BATCH_AXIS = {'x': 0, 'positions': 0, 'loss_target': 0}
SHARED_INPUTS = []
_WEIGHT_DTYPES = {'ffn1_norm': _jnp.float32, 'ffn1_w_gate': _jnp.float32, 'ffn1_w_up': _jnp.float32, 'ffn1_w_down': _jnp.float32, 'mix_norm': _jnp.float32, 'w_in': _jnp.float32, 'pool_w': _jnp.float32, 'pool_scale': _jnp.float32, 'w_out': _jnp.float32, 'ffn2_norm': _jnp.float32, 'ffn2_w_gate': _jnp.float32, 'ffn2_w_up': _jnp.float32, 'ffn2_w_down': _jnp.float32, 'final_norm': _jnp.float32}
MOMENT_SCALE = {'ffn1_norm': 5.106777e-02, 'ffn1_w_gate': 2.134816e-02, 'ffn1_w_up': 2.071001e-02, 'ffn1_w_down': 3.424778e-02, 'mix_norm': 5.326200e-02, 'w_in': 3.319917e-02, 'pool_w': 9.407404e-02, 'pool_scale': 9.686750e-02, 'w_out': 4.917773e-02, 'ffn2_norm': 4.291703e-02, 'ffn2_w_gate': 1.870099e-02, 'ffn2_w_up': 1.810563e-02, 'ffn2_w_down': 3.001515e-02, 'final_norm': 1.599345e+01}


def _to_microbatches(a, axis):
    t = _jnp.moveaxis(a, axis, 0)
    t = t.reshape((N_MICROBATCH, t.shape[0] // N_MICROBATCH) + t.shape[1:])
    return _jnp.moveaxis(t, 1, axis + 1)


def setup_inputs(seed: int = 0) -> dict:
    inp = _fwd_setup_inputs(seed)
    key = _jax.random.fold_in(_jax.random.key(seed), 7919)
    shape, _ = _output_shape()
    out = dict(inp)
    out["loss_target"] = _jax.random.normal(_jax.random.fold_in(key, 0), shape, _jnp.float32)
    for i, name in enumerate(TWIN_WEIGHTS):
        w = inp[name].astype(_jnp.float32)
        if MOMENT_SCALE is None:
            s = _jnp.sqrt(_jnp.mean(_jnp.square(w)) + 1e-30)
        else:
            s = MOMENT_SCALE[name]
        km, kv = _jax.random.split(_jax.random.fold_in(key, i + 1))
        out[name] = w
        out["m_" + name] = s * _jax.random.normal(km, w.shape, _jnp.float32)
        out["v_" + name] = (s * s) * _jax.random.uniform(kv, w.shape, _jnp.float32, 0.5, 1.5)
    if N_MICROBATCH > 1:
        for name, axis in PER_EXAMPLE_BATCH_AXIS.items():
            out[name] = _to_microbatches(out[name], axis)
    return {'x': out['x'], 'positions': out['positions'], 'ffn1_norm': out['ffn1_norm'], 'ffn1_w_gate': out['ffn1_w_gate'], 'ffn1_w_up': out['ffn1_w_up'], 'ffn1_w_down': out['ffn1_w_down'], 'mix_norm': out['mix_norm'], 'w_in': out['w_in'], 'pool_w': out['pool_w'], 'pool_scale': out['pool_scale'], 'w_out': out['w_out'], 'ffn2_norm': out['ffn2_norm'], 'ffn2_w_gate': out['ffn2_w_gate'], 'ffn2_w_up': out['ffn2_w_up'], 'ffn2_w_down': out['ffn2_w_down'], 'final_norm': out['final_norm'], 'loss_target': out['loss_target'], 'm_ffn1_norm': out['m_ffn1_norm'], 'm_ffn1_w_gate': out['m_ffn1_w_gate'], 'm_ffn1_w_up': out['m_ffn1_w_up'], 'm_ffn1_w_down': out['m_ffn1_w_down'], 'm_mix_norm': out['m_mix_norm'], 'm_w_in': out['m_w_in'], 'm_pool_w': out['m_pool_w'], 'm_pool_scale': out['m_pool_scale'], 'm_w_out': out['m_w_out'], 'm_ffn2_norm': out['m_ffn2_norm'], 'm_ffn2_w_gate': out['m_ffn2_w_gate'], 'm_ffn2_w_up': out['m_ffn2_w_up'], 'm_ffn2_w_down': out['m_ffn2_w_down'], 'm_final_norm': out['m_final_norm'], 'v_ffn1_norm': out['v_ffn1_norm'], 'v_ffn1_w_gate': out['v_ffn1_w_gate'], 'v_ffn1_w_up': out['v_ffn1_w_up'], 'v_ffn1_w_down': out['v_ffn1_w_down'], 'v_mix_norm': out['v_mix_norm'], 'v_w_in': out['v_w_in'], 'v_pool_w': out['v_pool_w'], 'v_pool_scale': out['v_pool_scale'], 'v_w_out': out['v_w_out'], 'v_ffn2_norm': out['v_ffn2_norm'], 'v_ffn2_w_gate': out['v_ffn2_w_gate'], 'v_ffn2_w_up': out['v_ffn2_w_up'], 'v_ffn2_w_down': out['v_ffn2_w_down'], 'v_final_norm': out['v_final_norm']}


def _loss(weights, diff, rest, loss_target):
    with _jax.named_scope("forward"):
        args = {**rest, TWIN_DIFF_INPUT: diff, **{k: w.astype(_WEIGHT_DTYPES[k]) for k, w in weights.items()}}
        y = _forward(args)
    with _jax.named_scope("loss_head"):
        err = _jnp.square(y.astype(_jnp.float32) - loss_target)
        return 0.5 * _jnp.sum(_jnp.mean(err, axis=-1)) if err.ndim else 0.5 * err


def _adamw(w, g, m, v):
    m = ADAM_B1 * m + (1.0 - ADAM_B1) * g
    v = ADAM_B2 * v + (1.0 - ADAM_B2) * _jnp.square(g)
    m_hat = m / (1.0 - ADAM_B1 ** ADAM_STEP)
    v_hat = v / (1.0 - ADAM_B2 ** ADAM_STEP)
    delta = -ADAM_LR * (m_hat / (_jnp.sqrt(v_hat) + ADAM_EPS) + ADAM_WD * w)
    return delta, m, v


def reference(x, positions, ffn1_norm, ffn1_w_gate, ffn1_w_up, ffn1_w_down, mix_norm, w_in, pool_w, pool_scale, w_out, ffn2_norm, ffn2_w_gate, ffn2_w_up, ffn2_w_down, final_norm, loss_target, m_ffn1_norm, m_ffn1_w_gate, m_ffn1_w_up, m_ffn1_w_down, m_mix_norm, m_w_in, m_pool_w, m_pool_scale, m_w_out, m_ffn2_norm, m_ffn2_w_gate, m_ffn2_w_up, m_ffn2_w_down, m_final_norm, v_ffn1_norm, v_ffn1_w_gate, v_ffn1_w_up, v_ffn1_w_down, v_mix_norm, v_w_in, v_pool_w, v_pool_scale, v_w_out, v_ffn2_norm, v_ffn2_w_gate, v_ffn2_w_up, v_ffn2_w_down, v_final_norm):
    given = dict(x=x, positions=positions, ffn1_norm=ffn1_norm, ffn1_w_gate=ffn1_w_gate, ffn1_w_up=ffn1_w_up, ffn1_w_down=ffn1_w_down, mix_norm=mix_norm, w_in=w_in, pool_w=pool_w, pool_scale=pool_scale, w_out=w_out, ffn2_norm=ffn2_norm, ffn2_w_gate=ffn2_w_gate, ffn2_w_up=ffn2_w_up, ffn2_w_down=ffn2_w_down, final_norm=final_norm, loss_target=loss_target, m_ffn1_norm=m_ffn1_norm, m_ffn1_w_gate=m_ffn1_w_gate, m_ffn1_w_up=m_ffn1_w_up, m_ffn1_w_down=m_ffn1_w_down, m_mix_norm=m_mix_norm, m_w_in=m_w_in, m_pool_w=m_pool_w, m_pool_scale=m_pool_scale, m_w_out=m_w_out, m_ffn2_norm=m_ffn2_norm, m_ffn2_w_gate=m_ffn2_w_gate, m_ffn2_w_up=m_ffn2_w_up, m_ffn2_w_down=m_ffn2_w_down, m_final_norm=m_final_norm, v_ffn1_norm=v_ffn1_norm, v_ffn1_w_gate=v_ffn1_w_gate, v_ffn1_w_up=v_ffn1_w_up, v_ffn1_w_down=v_ffn1_w_down, v_mix_norm=v_mix_norm, v_w_in=v_w_in, v_pool_w=v_pool_w, v_pool_scale=v_pool_scale, v_w_out=v_w_out, v_ffn2_norm=v_ffn2_norm, v_ffn2_w_gate=v_ffn2_w_gate, v_ffn2_w_up=v_ffn2_w_up, v_ffn2_w_down=v_ffn2_w_down, v_final_norm=v_final_norm)
    weights = {n: given[n] for n in TWIN_WEIGHTS}
    shared = {n: given[n] for n in SHARED_INPUTS}
    per_example = {n: given[n] for n in ['x', 'positions']}
    grad_fn = _jax.value_and_grad(_loss, argnums=(0, 1))

    def one_microbatch(ex, loss_target):
        ex = dict(ex)
        diff = ex.pop(TWIN_DIFF_INPUT)
        return grad_fn(weights, diff, {**shared, **ex}, loss_target)

    if N_MICROBATCH == 1:
        loss, (grad_w, grad_x) = one_microbatch(per_example, given["loss_target"])
    else:
        def body(carry, xs):
            loss_sum, grad_sum = carry
            l_k, (gw_k, gx_k) = one_microbatch(xs[0], xs[1])
            with _jax.named_scope("update"):
                return (loss_sum + l_k, _jax.tree.map(_jnp.add, grad_sum, gw_k)), gx_k

        init = (_jnp.zeros((), _jnp.float32), _jax.tree.map(_jnp.zeros_like, weights))
        (loss, grad_w), grad_x = _jax.lax.scan(body, init, (per_example, given["loss_target"]))
    with _jax.named_scope("update"):
        delta_w, new_m, new_v = {}, {}, {}
        for n in TWIN_WEIGHTS:
            delta_w[n], new_m[n], new_v[n] = _adamw(weights[n], grad_w[n], given["m_" + n], given["v_" + n])
    return (loss, grad_x, *[grad_w[n] for n in TWIN_WEIGHTS], *[delta_w[n] for n in TWIN_WEIGHTS],
            *[new_m[n] for n in TWIN_WEIGHTS], *[new_v[n] for n in TWIN_WEIGHTS])
```

```python
import jax
import jax.numpy as jnp
from jax import lax
from jax.experimental import pallas as pl
from jax.experimental.pallas import tpu as pltpu

f32 = jnp.float32
bf16 = jnp.bfloat16
SDS = jax.ShapeDtypeStruct

D = 1024
S = 2048
F = 2816
L = 4
PW = 256
AW = 768
PROJ = PW + 3 * AW
NDEV = 8
FS, PS, OS = F // NDEV, PROJ // NDEV, D // NDEV
TM = 256
QB = 128
HALF = 64
NG = AW // 128
NORM_EPS = 1e-6
MASK_VALUE = -1e30
ROPE_THETA = 500000.0
ADAM_LR, ADAM_B1, ADAM_B2, ADAM_EPS, ADAM_WD, ADAM_STEP = 0.001, 0.9, 0.999, 1e-08, 0.01, 10
POOL_WINDOWS = (2, 4, 8, 16)
PAD = 8
SMALL_ROWS = 80
VMEM_LIMIT = 56 * 1024 * 1024

_CP = pltpu.CompilerParams(vmem_limit_bytes=VMEM_LIMIT)
_ANY = pl.BlockSpec(memory_space=pl.ANY)
_MESH = pl.DeviceIdType.MESH


def _dot_nn(a, b):
    return lax.dot_general(a, b, (((1,), (0,)), ((), ())), preferred_element_type=f32)


def _dot_nt(a, b):
    return lax.dot_general(a, b, (((1,), (1,)), ((), ())), preferred_element_type=f32)


def _dot_tn(a, b):
    return lax.dot_general(a, b, (((0,), (0,)), ((), ())), preferred_element_type=f32)


def _rms(x, g):
    r = lax.rsqrt(jnp.mean(x * x, axis=-1, keepdims=True) + NORM_EPS)
    xh = x * r
    return r, xh, xh * g


def _rms_bwd(dh, r, xh, g):
    dxh = dh * g
    return r * (dxh - xh * jnp.mean(dxh * xh, axis=-1, keepdims=True))


def _tile(cols):
    return pl.BlockSpec((TM, cols), lambda i: (i, 0))


def _const(shape):
    return pl.BlockSpec(shape, lambda i: (0,) * len(shape))


def _layer(rows, cols, l):
    return pl.BlockSpec((None, rows, cols), lambda i: (l, 0, 0), pipeline_mode=pl.Buffered(1))


def _p4():
    return pl.BlockSpec((4, TM // 4, AW), lambda i: (0, i, 0))


def _p16():
    return pl.BlockSpec((16, TM // 16, AW), lambda i: (0, i, 0))


def _cols(j):
    return slice(128 * j, 128 * (j + 1))


def _ffn_fwd(x, g, gt, ut, dn, l):
    def body(x_ref, g_ref, gt_ref, ut_ref, dn_ref, xo_ref, gate_ref, up_ref):
        x = x_ref[...]
        _, _, hn = _rms(x, g_ref[...])
        h = hn.astype(bf16)
        gate = _dot_nt(h, gt_ref[...])
        up = _dot_nt(h, ut_ref[...])
        gate_ref[...] = gate.astype(bf16)
        up_ref[...] = up.astype(bf16)
        a = (gate * jax.nn.sigmoid(gate) * up).astype(bf16)
        xo_ref[...] = x + 0.5 * _dot_nn(a, dn_ref[...])

    return pl.pallas_call(
        body, grid=(S // TM,),
        in_specs=[_tile(D), _layer(1, D, l), _layer(F, D, l), _layer(F, D, l), _layer(F, D, l)],
        out_specs=[_tile(D), _tile(F), _tile(F)],
        out_shape=[SDS((S, D), f32), SDS((S, F), bf16), SDS((S, F), bf16)],
        compiler_params=_CP, name="ffn_fwd")(x, g, gt, ut, dn)


def _ffn_bwd_d(x, g, gate, up, dxo, gt, ut, dn, l):
    def body(x_ref, g_ref, gate_ref, up_ref, dxo_ref, gt_ref, ut_ref, dn_ref,
             dx_ref, dgate_ref, dup_ref, h_ref, dy_ref, dg_ref):
        x = x_ref[...]
        g = g_ref[...]
        r, xh, hn = _rms(x, g)
        h_ref[...] = hn.astype(bf16)
        dxo = dxo_ref[...]
        dy = (0.5 * dxo).astype(bf16)
        dy_ref[...] = dy
        da = _dot_nt(dy, dn_ref[...])
        gate = gate_ref[...].astype(f32)
        up = up_ref[...].astype(f32)
        sg = jax.nn.sigmoid(gate)
        dgate = (da * up * (sg * (1.0 + gate * (1.0 - sg)))).astype(bf16)
        dup = (da * (gate * sg)).astype(bf16)
        dgate_ref[...] = dgate
        dup_ref[...] = dup
        dh = _dot_nn(dgate, gt_ref[...]) + _dot_nn(dup, ut_ref[...])

        @pl.when(pl.program_id(0) == 0)
        def _():
            dg_ref[...] = jnp.zeros_like(dg_ref)

        dg_ref[...] += jnp.sum(dh * xh, axis=0, keepdims=True)
        dx_ref[...] = dxo + _rms_bwd(dh, r, xh, g)

    return pl.pallas_call(
        body, grid=(S // TM,),
        in_specs=[_tile(D), _layer(1, D, l), _tile(F), _tile(F), _tile(D),
                  _layer(F, D, l), _layer(F, D, l), _layer(F, D, l)],
        out_specs=[_tile(D), _tile(F), _tile(F), _tile(D), _tile(D), _const((1, D))],
        out_shape=[SDS((S, D), f32), SDS((S, F), bf16), SDS((S, F), bf16), SDS((S, D), bf16),
                   SDS((S, D), bf16), SDS((1, D), f32)],
        compiler_params=_CP, name="ffn_bwd_d")(x, g, gate, up, dxo, gt, ut, dn)


def _ffn_bwd_w(h, dy, gate, up, dgate, dup):
    fc = 256

    def body(h_ref, dy_ref, gate_ref, up_ref, dgate_ref, dup_ref, dgt_ref, dut_ref, ddn_ref):
        gate = gate_ref[...].astype(f32)
        a = (gate * jax.nn.sigmoid(gate) * up_ref[...].astype(f32)).astype(bf16)
        ddn_ref[...] = _dot_tn(a, dy_ref[...]).astype(bf16)
        h = h_ref[...]
        dgt_ref[...] = _dot_tn(dgate_ref[...], h).astype(bf16)
        dut_ref[...] = _dot_tn(dup_ref[...], h).astype(bf16)

    col = pl.BlockSpec((S, fc), lambda j: (0, j))
    row = pl.BlockSpec((fc, D), lambda j: (j, 0))
    full = pl.BlockSpec((S, D), lambda j: (0, 0))
    return pl.pallas_call(
        body, grid=(F // fc,),
        in_specs=[full, full, col, col, col, col],
        out_specs=[row, row, row],
        out_shape=[SDS((F, D), bf16)] * 3,
        compiler_params=_CP, name="ffn_bwd_w")(h, dy, gate, up, dgate, dup)


def _wgrad(a, b):
    m, n = a.shape[1], b.shape[1]
    mc = 256

    def body(a_ref, b_ref, o_ref):
        o_ref[...] = _dot_tn(a_ref[...], b_ref[...]).astype(bf16)

    return pl.pallas_call(
        body, grid=(m // mc,),
        in_specs=[pl.BlockSpec((S, mc), lambda j: (0, j)), pl.BlockSpec((S, n), lambda j: (0, 0))],
        out_specs=pl.BlockSpec((mc, n), lambda j: (j, 0)),
        out_shape=SDS((m, n), bf16),
        compiler_params=_CP, name="wgrad")(a, b)


def _rope(t, c, sn, sp):
    return t * c + pltpu.roll(t, 120, 1) * sn + pltpu.roll(t, 8, 1) * sp


def _rope_bwd(d, c, sn, sp):
    return d * c + pltpu.roll(d * sn, 8, 1) + pltpu.roll(d * sp, 120, 1)


def _rope_tables(positions):
    inv_freq = ROPE_THETA ** (-jnp.arange(0, 16, 2, dtype=f32) / 16)
    ang = positions.reshape(S, 1).astype(f32) * inv_freq
    cos, sin = jnp.cos(ang), jnp.sin(ang)
    one = jnp.ones((S, 48), f32)
    zero8 = jnp.zeros((S, 8), f32)
    zero48 = jnp.zeros((S, 48), f32)
    c = jnp.concatenate([cos, cos, one], axis=1)
    sn = jnp.concatenate([-sin, zero8, zero48], axis=1)
    sp = jnp.concatenate([zero8, sin, zero48], axis=1)
    return tuple(jnp.concatenate([t, t], axis=1) for t in (c, sn, sp))


def _mix_in_fwd(x, g, wint, tabs, l):
    def body(x_ref, g_ref, w_ref, c_ref, sn_ref, sp_ref,
             h_ref, vp_ref, q1, k1, v1, q4, k4, v4, q16, k16, v16, scr):
        _, _, hn = _rms(x_ref[...], g_ref[...])
        h = hn.astype(bf16)
        h_ref[...] = h
        proj = _dot_nt(h, w_ref[...])
        vp_ref[...] = proj[:, :PW]
        c, sn, sp = c_ref[...], sn_ref[...], sp_ref[...]
        for kind, (o1, o4, o16) in enumerate(((q1, q4, q16), (k1, k4, k16), (v1, v4, v16))):
            for j in range(NG):
                t = proj[:, PW + kind * AW + 128 * j: PW + kind * AW + 128 * (j + 1)]
                if kind == 0:
                    t = _rope(t, c, sn, sp) * 0.125
                elif kind == 1:
                    t = _rope(t, c, sn, sp)
                scr[j] = t
                o1[:, _cols(j)] = t.astype(bf16)
            for r in range(4):
                for j in range(NG):
                    o4[r, :, _cols(j)] = scr[j, pl.ds(r, TM // 4, stride=4), :].astype(bf16)
            for r in range(16):
                for j in range(NG):
                    o16[r, :, _cols(j)] = scr[j, pl.ds(r, TM // 16, stride=16), :].astype(bf16)

    nat, d4, d16 = SDS((S, AW), bf16), SDS((4, S // 4, AW), bf16), SDS((16, S // 16, AW), bf16)
    return pl.pallas_call(
        body, grid=(S // TM,),
        in_specs=[_tile(D), _layer(1, D, l), _layer(PROJ, D, l), _tile(128), _tile(128), _tile(128)],
        out_specs=[_tile(D), _tile(PW)] + [_tile(AW)] * 3 + [_p4()] * 3 + [_p16()] * 3,
        out_shape=[SDS((S, D), bf16), SDS((S, PW), f32)] + [nat] * 3 + [d4] * 3 + [d16] * 3,
        scratch_shapes=[pltpu.VMEM((NG, TM, 128), f32)],
        compiler_params=_CP, name="mix_in_fwd")(x, g, wint, *tabs)


def _mix_in_bwd(dxo, x, g, wint, tabs, dvp, d1, d4, d16, l):
    def body(dxo_ref, x_ref, g_ref, w_ref, c_ref, sn_ref, sp_ref, dvp_ref,
             dq1, dk1, dv1, dq4, dk4, dv4, dq16, dk16, dv16,
             dx_ref, dproj_ref, dg_ref, s4, s16):
        c, sn, sp = c_ref[...], sn_ref[...], sp_ref[...]
        dproj_ref[:, :PW] = dvp_ref[...].astype(bf16)
        for kind, (a1, a4, a16) in enumerate(((dq1, dq4, dq16), (dk1, dk4, dk16), (dv1, dv4, dv16))):
            for r in range(4):
                for j in range(NG):
                    s4[j, pl.ds(r, TM // 4, stride=4), :] = a4[r, :, _cols(j)]
            for r in range(16):
                for j in range(NG):
                    s16[j, pl.ds(r, TM // 16, stride=16), :] = a16[r, :, _cols(j)]
            for j in range(NG):
                t = a1[:, _cols(j)] + s4[j] + s16[j]
                if kind == 0:
                    t = _rope_bwd(t * 0.125, c, sn, sp)
                elif kind == 1:
                    t = _rope_bwd(t, c, sn, sp)
                dproj_ref[:, PW + kind * AW + 128 * j: PW + kind * AW + 128 * (j + 1)] = t.astype(bf16)
        g = g_ref[...]
        r_, xh, _ = _rms(x_ref[...], g)
        dh = _dot_nn(dproj_ref[...], w_ref[...])

        @pl.when(pl.program_id(0) == 0)
        def _():
            dg_ref[...] = jnp.zeros_like(dg_ref)

        dg_ref[...] += jnp.sum(dh * xh, axis=0, keepdims=True)
        dx_ref[...] = dxo_ref[...] + _rms_bwd(dh, r_, xh, g)

    return pl.pallas_call(
        body, grid=(S // TM,),
        in_specs=[_tile(D), _tile(D), _layer(1, D, l), _layer(PROJ, D, l), _tile(128), _tile(128), _tile(128),
                  _tile(PW)] + [_tile(AW)] * 3 + [_p4()] * 3 + [_p16()] * 3,
        out_specs=[_tile(D), _tile(PROJ), _const((1, D))],
        out_shape=[SDS((S, D), f32), SDS((S, PROJ), bf16), SDS((1, D), f32)],
        scratch_shapes=[pltpu.VMEM((NG, TM, 128), f32), pltpu.VMEM((NG, TM, 128), f32)],
        compiler_params=_CP, name="mix_in_bwd")(dxo, x, g, wint, *tabs, dvp, *d1, *d4, *d16)


def _pool_sums(pad_ref, base, rows, adjoint):
    lane_group = lax.broadcasted_iota(jnp.int32, (rows, PW), 1) // 64
    sign = -1 if adjoint else 1

    def sh(o):
        return pad_ref[pl.ds(PAD + base + sign * o, rows), :]

    out = None
    acc = None
    lo, hi = 0, 0
    for gi, w in enumerate(POOL_WINDOWS):
        for o in list(range(-(w // 2), lo)) + list(range(hi, w - w // 2)):
            acc = sh(o) if acc is None else acc + sh(o)
        lo, hi = -(w // 2), w - w // 2
        out = acc if out is None else jnp.where(lane_group >= gi, acc, out)
    return out


def _pool_counts(base, rows):
    pos = base + lax.broadcasted_iota(jnp.int32, (rows, PW), 0)
    lane_group = lax.broadcasted_iota(jnp.int32, (rows, PW), 1) // 64
    cnt = None
    for gi, w in enumerate(POOL_WINDOWS):
        lo = jnp.maximum(pos - w // 2, 0)
        hi = jnp.minimum(pos + w - 1 - w // 2, S - 1)
        c = (hi - lo + 1).astype(f32)
        cnt = c if cnt is None else jnp.where(lane_group >= gi, c, cnt)
    return cnt


def _pool_fwd(vp, wbd, scale, l):
    ch = 256

    def body(vp_ref, w_ref, sc_ref, y_ref, diff_ref, pad):
        pad[pl.ds(0, PAD), :] = jnp.zeros((PAD, PW), f32)
        pad[pl.ds(PAD + S, PAD), :] = jnp.zeros((PAD, PW), f32)
        pad[pl.ds(PAD, S), :] = vp_ref[...]
        for b in range(S // ch):
            base = b * ch
            pooled = _pool_sums(pad, base, ch, False) / _pool_counts(base, ch)
            diff = (pooled - vp_ref[pl.ds(base, ch), :]).astype(bf16)
            diff_ref[pl.ds(base, ch), :] = diff
            y_ref[pl.ds(base, ch), :] = _dot_nn(diff, w_ref[...]) * sc_ref[...]

    whole = lambda shape: pl.BlockSpec(shape, lambda i: (0,) * len(shape))
    return pl.pallas_call(
        body, grid=(1,),
        in_specs=[whole((S, PW)), pl.BlockSpec((None, PW, PW), lambda i: (l, 0, 0)),
                  pl.BlockSpec((None, 1, PW), lambda i: (l, 0, 0))],
        out_specs=[whole((S, PW)), whole((S, PW))],
        out_shape=[SDS((S, PW), f32), SDS((S, PW), bf16)],
        scratch_shapes=[pltpu.VMEM((S + 2 * PAD, PW), f32)],
        compiler_params=_CP, name="pool_fwd")(vp, wbd, scale)


def _pool_bwd(dy, diff, wbd, scale, l):
    ch = 256

    def body(dy_ref, diff_ref, w_ref, sc_ref, dvp_ref, dw_ref, dsc_ref, pad):
        pad[pl.ds(0, PAD), :] = jnp.zeros((PAD, PW), f32)
        pad[pl.ds(PAD + S, PAD), :] = jnp.zeros((PAD, PW), f32)
        dw = jnp.zeros((PW, PW), f32)
        dsc = jnp.zeros((1, PW), f32)
        for b in range(S // ch):
            base = b * ch
            dy = dy_ref[pl.ds(base, ch), :]
            diff = diff_ref[pl.ds(base, ch), :]
            dsc = dsc + jnp.sum(dy * _dot_nn(diff, w_ref[...]), axis=0, keepdims=True)
            dz = (dy * sc_ref[...]).astype(bf16)
            dw = dw + _dot_tn(diff, dz)
            ddiff = _dot_nt(dz, w_ref[...])
            dvp_ref[pl.ds(base, ch), :] = -ddiff
            pad[pl.ds(PAD + base, ch), :] = ddiff / _pool_counts(base, ch)
        dw_ref[...] = dw
        dsc_ref[...] = dsc
        for b in range(S // ch):
            base = b * ch
            dvp_ref[pl.ds(base, ch), :] += _pool_sums(pad, base, ch, True)

    whole = lambda shape: pl.BlockSpec(shape, lambda i: (0,) * len(shape))
    return pl.pallas_call(
        body, grid=(1,),
        in_specs=[whole((S, PW)), whole((S, PW)), pl.BlockSpec((None, PW, PW), lambda i: (l, 0, 0)),
                  pl.BlockSpec((None, 1, PW), lambda i: (l, 0, 0))],
        out_specs=[whole((S, PW)), whole((PW, PW)), whole((1, PW))],
        out_shape=[SDS((S, PW), f32), SDS((PW, PW), f32), SDS((1, PW), f32)],
        scratch_shapes=[pltpu.VMEM((S + 2 * PAD, PW), f32)],
        compiler_params=_CP, name="pool_bwd")(dy, diff, wbd, scale)


def _attn_block_index(b, lc):
    bpc = lc // QB
    kw = min(2 * QB, lc)
    row0 = pl.multiple_of(b * QB, QB)
    t0 = (b % bpc) * QB
    ks_in = jnp.clip(t0 - HALF, 0, lc - kw)
    kstart = pl.multiple_of((b // bpc) * lc + ks_in, HALF)
    qpos = t0 + lax.broadcasted_iota(jnp.int32, (QB, kw), 0)
    kpos = ks_in + lax.broadcasted_iota(jnp.int32, (QB, kw), 1)
    valid = jnp.abs(qpos - kpos) <= HALF
    return kw, row0, kstart, valid


def _attn_fwd(q, k, v, lc):
    def body(q_ref, k_ref, v_ref, o_ref, lse_ref):
        head0 = lax.broadcasted_iota(jnp.int32, (QB, 128), 1) < 64

        def blk(b, carry):
            kw, row0, kstart, valid = _attn_block_index(b, lc)
            qb = q_ref[pl.ds(row0, QB), :]
            kb = k_ref[pl.ds(kstart, kw), :]
            vb = v_ref[pl.ds(kstart, kw), :]
            outs, lses = [], []
            for hh in range(2):
                lm = head0 if hh == 0 else jnp.logical_not(head0)
                qh = jnp.where(lm, qb, jnp.zeros_like(qb))
                s = jnp.where(valid, _dot_nt(qh, kb), MASK_VALUE)
                m = jnp.max(s, axis=-1, keepdims=True)
                p = jnp.exp(s - m)
                den = jnp.sum(p, axis=-1, keepdims=True)
                outs.append(_dot_nn(p.astype(bf16), vb) / den)
                lses.append(jnp.broadcast_to(m + jnp.log(den), (QB, 128)))
            o_ref[pl.ds(row0, QB), :] = jnp.where(head0, outs[0], outs[1])
            lse_ref[pl.ds(row0, QB), :] = jnp.where(head0, lses[0], lses[1])
            return carry

        lax.fori_loop(0, S // QB, blk, 0)

    col = pl.BlockSpec((S, 128), lambda p: (0, p))
    return pl.pallas_call(
        body, grid=(NG,), in_specs=[col, col, col], out_specs=[col, col],
        out_shape=[SDS((S, AW), f32), SDS((S, AW), f32)],
        compiler_params=_CP, name=f"attn_fwd_{lc}")(q, k, v)


def _attn_bwd(q, k, v, do, lse, delta, lc):
    def body(q_ref, k_ref, v_ref, do_ref, lse_ref, dl_ref, dq_ref, dk_ref, dv_ref):
        head0 = lax.broadcasted_iota(jnp.int32, (QB, 128), 1) < 64
        dk_ref[...] = jnp.zeros_like(dk_ref)
        dv_ref[...] = jnp.zeros_like(dv_ref)

        def blk(b, carry):
            kw, row0, kstart, valid = _attn_block_index(b, lc)
            khead0 = lax.broadcasted_iota(jnp.int32, (kw, 128), 1) < 64
            qb = q_ref[pl.ds(row0, QB), :]
            dob = do_ref[pl.ds(row0, QB), :]
            lse = lse_ref[pl.ds(row0, QB), :]
            dl = dl_ref[pl.ds(row0, QB), :]
            kb = k_ref[pl.ds(kstart, kw), :]
            vb = v_ref[pl.ds(kstart, kw), :]
            dqs, dks, dvs = [], [], []
            for hh in range(2):
                lm = head0 if hh == 0 else jnp.logical_not(head0)
                c0 = 64 * hh
                qh = jnp.where(lm, qb, jnp.zeros_like(qb))
                doh = jnp.where(lm, dob, jnp.zeros_like(dob))
                s = jnp.where(valid, _dot_nt(qh, kb), MASK_VALUE)
                p = jnp.exp(s - lse[:, c0:c0 + 1])
                dp = _dot_nt(doh, vb)
                ds = (p * (dp - dl[:, c0:c0 + 1])).astype(bf16)
                dqs.append(_dot_nn(ds, kb))
                dks.append(_dot_tn(ds, qb))
                dvs.append(_dot_tn(p.astype(bf16), dob))
            dq_ref[pl.ds(row0, QB), :] = jnp.where(head0, dqs[0], dqs[1])
            dk_ref[pl.ds(kstart, kw), :] += jnp.where(khead0, dks[0], dks[1])
            dv_ref[pl.ds(kstart, kw), :] += jnp.where(khead0, dvs[0], dvs[1])
            return carry

        lax.fori_loop(0, S // QB, blk, 0)

    col = pl.BlockSpec((S, 128), lambda p: (0, p))
    return pl.pallas_call(
        body, grid=(NG,), in_specs=[col] * 6, out_specs=[col] * 3,
        out_shape=[SDS((S, AW), f32)] * 3,
        compiler_params=_CP, name=f"attn_bwd_{lc}")(q, k, v, do, lse, delta)


def _mix_out_fwd(x, ypool, o1, l1, o4, l4, o16, l16, wout, l):
    def body(x_ref, yp_ref, o1_ref, l1_ref, o4_ref, l4_ref, o16_ref, l16_ref, w_ref,
             xo_ref, mixed_ref, o_ref, lse1_ref, lse4_ref, lse16_ref, so4, sl4, so16, sl16, sl):
        for r in range(4):
            for j in range(NG):
                so4[j, pl.ds(r, TM // 4, stride=4), :] = o4_ref[r, :, _cols(j)]
                sl4[j, pl.ds(r, TM // 4, stride=4), :] = l4_ref[r, :, _cols(j)]
        for r in range(16):
            for j in range(NG):
                so16[j, pl.ds(r, TM // 16, stride=16), :] = o16_ref[r, :, _cols(j)]
                sl16[j, pl.ds(r, TM // 16, stride=16), :] = l16_ref[r, :, _cols(j)]
        mixed_ref[:, :PW] = yp_ref[...].astype(bf16)
        for j in range(NG):
            a, b, c = l1_ref[:, _cols(j)], sl4[j], sl16[j]
            m = jnp.maximum(jnp.maximum(a, b), c)
            wa, wb, wc = jnp.exp(a - m), jnp.exp(b - m), jnp.exp(c - m)
            den = wa + wb + wc
            y = (wa * o1_ref[:, _cols(j)] + wb * so4[j] + wc * so16[j]) / den
            lse = m + jnp.log(den)
            o_ref[:, _cols(j)] = y
            lse1_ref[:, _cols(j)] = lse
            sl[j] = lse
            mixed_ref[:, PW + 128 * j: PW + 128 * (j + 1)] = y.astype(bf16)
        for r in range(4):
            for j in range(NG):
                lse4_ref[r, :, _cols(j)] = sl[j, pl.ds(r, TM // 4, stride=4), :]
        for r in range(16):
            for j in range(NG):
                lse16_ref[r, :, _cols(j)] = sl[j, pl.ds(r, TM // 16, stride=16), :]
        xo_ref[...] = x_ref[...] + _dot_nn(mixed_ref[...], w_ref[...])

    scr = pltpu.VMEM((NG, TM, 128), f32)
    return pl.pallas_call(
        body, grid=(S // TM,),
        in_specs=[_tile(D), _tile(PW), _tile(AW), _tile(AW), _p4(), _p4(), _p16(), _p16(), _layer(D, D, l)],
        out_specs=[_tile(D), _tile(D), _tile(AW), _tile(AW), _p4(), _p16()],
        out_shape=[SDS((S, D), f32), SDS((S, D), bf16), SDS((S, AW), f32), SDS((S, AW), f32),
                   SDS((4, S // 4, AW), f32), SDS((16, S // 16, AW), f32)],
        scratch_shapes=[scr] * 5,
        compiler_params=_CP, name="mix_out_fwd")(x, ypool, o1, l1, o4, l4, o16, l16, wout)


def _segsum64(t):
    lane = lax.broadcasted_iota(jnp.int32, t.shape, 1)
    for s in (1, 2, 4, 8, 16, 32):
        t = t + jnp.where((lane & s) != 0, pltpu.roll(t, s, 1), pltpu.roll(t, 128 - s, 1))
    return t


def _mix_out_bwd(dxo, o, wout, l):
    def body(dxo_ref, o_ref, w_ref, dxb_ref, dyp_ref, do1, do4, do16, dl1, dl4, dl16, sdo, sdl):
        dxb = dxo_ref[...].astype(bf16)
        dxb_ref[...] = dxb
        dm = _dot_nt(dxb, w_ref[...])
        dyp_ref[...] = dm[:, :PW]
        for j in range(NG):
            d = dm[:, PW + 128 * j: PW + 128 * (j + 1)]
            dl = _segsum64(d * o_ref[:, _cols(j)])
            do1[:, _cols(j)] = d.astype(bf16)
            dl1[:, _cols(j)] = dl
            sdo[j] = d
            sdl[j] = dl
        for r in range(4):
            for j in range(NG):
                do4[r, :, _cols(j)] = sdo[j, pl.ds(r, TM // 4, stride=4), :].astype(bf16)
                dl4[r, :, _cols(j)] = sdl[j, pl.ds(r, TM // 4, stride=4), :]
        for r in range(16):
            for j in range(NG):
                do16[r, :, _cols(j)] = sdo[j, pl.ds(r, TM // 16, stride=16), :].astype(bf16)
                dl16[r, :, _cols(j)] = sdl[j, pl.ds(r, TM // 16, stride=16), :]

    scr = pltpu.VMEM((NG, TM, 128), f32)
    return pl.pallas_call(
        body, grid=(S // TM,),
        in_specs=[_tile(D), _tile(AW), _layer(D, D, l)],
        out_specs=[_tile(D), _tile(PW), _tile(AW), _p4(), _p16(), _tile(AW), _p4(), _p16()],
        out_shape=[SDS((S, D), bf16), SDS((S, PW), f32),
                   SDS((S, AW), bf16), SDS((4, S // 4, AW), bf16), SDS((16, S // 16, AW), bf16),
                   SDS((S, AW), f32), SDS((4, S // 4, AW), f32), SDS((16, S // 16, AW), f32)],
        scratch_shapes=[scr] * 2,
        compiler_params=_CP, name="mix_out_bwd")(dxo, o, wout)


def _loss_head(x, g, target):
    def body(x_ref, g_ref, t_ref, dx_ref, loss_ref, dg_ref):
        g = g_ref[...]
        r, xh, y = _rms(x_ref[...], g)
        err = y - t_ref[...]
        dy = err * (1.0 / D)

        @pl.when(pl.program_id(0) == 0)
        def _():
            loss_ref[...] = jnp.zeros_like(loss_ref)
            dg_ref[...] = jnp.zeros_like(dg_ref)

        loss_ref[...] += jnp.broadcast_to(0.5 * jnp.sum(jnp.mean(err * err, axis=-1, keepdims=True)), (1, D))
        dg_ref[...] += jnp.sum(dy * xh, axis=0, keepdims=True)
        dx_ref[...] = _rms_bwd(dy, r, xh, g)

    return pl.pallas_call(
        body, grid=(S // TM,),
        in_specs=[_tile(D), _const((1, D)), _tile(D)],
        out_specs=[_tile(D), _const((1, D)), _const((1, D))],
        out_shape=[SDS((S, D), f32), SDS((1, D), f32), SDS((1, D), f32)],
        compiler_params=_CP, name="loss_head")(x, g, target)


def _peer(k):
    x, y, c = lax.axis_index("x"), lax.axis_index("y"), lax.axis_index("c")
    px = 1 - x if k & 4 else x
    py = 1 - y if k & 2 else y
    pc = 1 - c if k & 1 else c
    return (px, py, pc), 4 * px + 2 * py + pc


def _all_gather(shards):
    n = len(shards)

    def body(*refs):
        ins, outs = refs[:n], refs[n:2 * n]
        send_sems, recv_sems, local_sems = refs[2 * n:]
        me, me_idx = _peer(0)
        sibling, sib_idx = _peer(1)
        far = [_peer(k) for k in (4, 2, 6)]
        far_sib = [_peer(k) for k in (5, 3, 7)]

        def rows(t, idx):
            r = ins[t].shape[1]
            return outs[t].at[:, pl.ds(idx * r, r), :]

        def copy(k, t, idx, to, src=None):
            return pltpu.make_async_remote_copy(
                src_ref=rows(t, idx) if src is None else src, dst_ref=rows(t, idx),
                send_sem=send_sems.at[k, t], recv_sem=recv_sems.at[k, t], device_id=to, device_id_type=_MESH)

        mine = [pltpu.make_async_copy(ins[t], rows(t, me_idx), local_sems.at[t]) for t in range(n)]
        for cp in mine:
            cp.start()
        first = [copy(0, t, me_idx, sibling, src=ins[t]) for t in range(n)]
        for j, (dev, _) in enumerate(far):
            first += [copy(1 + j, t, me_idx, dev, src=ins[t]) for t in range(n)]
        for cp in first:
            cp.start()
        passed = []
        for j, (_, idx) in enumerate(far):
            for t in range(n):
                copy(1 + j, t, idx, me).wait_recv()
                cp = copy(4 + j, t, idx, sibling)
                cp.start()
                passed.append(cp)
        for t in range(n):
            copy(0, t, sib_idx, me).wait_recv()
        for j, (_, idx) in enumerate(far_sib):
            for t in range(n):
                copy(4 + j, t, idx, me).wait_recv()
        for cp in first + passed:
            cp.wait_send()
        for cp in mine:
            cp.wait()

    return pl.pallas_call(
        body, in_specs=[_ANY] * n, out_specs=[_ANY] * n,
        out_shape=[SDS((a.shape[0], NDEV * a.shape[1], a.shape[2]), a.dtype) for a in shards],
        scratch_shapes=[pltpu.SemaphoreType.DMA((7, n)), pltpu.SemaphoreType.DMA((7, n)),
                        pltpu.SemaphoreType.DMA((n,))],
        name="all_gather_weights")(*shards)


def _all_to_all(fulls, small):
    n = len(fulls)

    def body(*refs):
        ins, small_in = refs[:n], refs[n]
        outs, small_out = refs[n + 1:2 * n + 1], refs[2 * n + 1]
        send_sems, recv_sems, local_sems = refs[2 * n + 2:]
        _, me_idx = _peer(0)

        def block(t, idx):
            r = outs[t].shape[2]
            return ins[t].at[:, pl.ds(idx * r, r), :]

        def copy(k, t, to_dev, to_idx, from_idx):
            if t < n:
                src, dst = block(t, to_idx), outs[t].at[from_idx]
            else:
                src, dst = small_in, small_out.at[from_idx]
            return pltpu.make_async_remote_copy(
                src_ref=src, dst_ref=dst, send_sem=send_sems.at[k - 1, t], recv_sem=recv_sems.at[k - 1, t],
                device_id=to_dev, device_id_type=_MESH)

        mine = [pltpu.make_async_copy(block(t, me_idx), outs[t].at[me_idx], local_sems.at[t]) for t in range(n)]
        mine.append(pltpu.make_async_copy(small_in, small_out.at[me_idx], local_sems.at[n]))
        for cp in mine:
            cp.start()
        sends = []
        for k in range(1, NDEV):
            dev, idx = _peer(k)
            sends += [copy(k, t, dev, idx, me_idx) for t in range(n + 1)]
        for cp in sends:
            cp.start()
        for k in range(1, NDEV):
            dev, idx = _peer(k)
            for t in range(n + 1):
                copy(k, t, dev, idx, idx).wait_recv()
        for cp in sends:
            cp.wait_send()
        for cp in mine:
            cp.wait()

    out_shape = [SDS((NDEV, a.shape[0], a.shape[1] // NDEV, a.shape[2]), a.dtype) for a in fulls]
    out_shape.append(SDS((NDEV,) + small.shape, small.dtype))
    return pl.pallas_call(
        body, in_specs=[_ANY] * (n + 1), out_specs=[_ANY] * (n + 1), out_shape=out_shape,
        scratch_shapes=[pltpu.SemaphoreType.DMA((NDEV - 1, n + 1)), pltpu.SemaphoreType.DMA((NDEV - 1, n + 1)),
                        pltpu.SemaphoreType.DMA((n + 1,))],
        name="all_to_all_grads")(*fulls, small)


def _sum_slots(slots, rb):
    r = slots.shape[1]

    def body(s_ref, o_ref):
        acc = s_ref[0].astype(f32)
        for s in range(1, NDEV):
            acc = acc + s_ref[s].astype(f32)
        o_ref[...] = acc

    return pl.pallas_call(
        body, grid=(r // rb,),
        in_specs=[pl.BlockSpec((NDEV, rb, D), lambda i: (0, i, 0))],
        out_specs=pl.BlockSpec((rb, D), lambda i: (i, 0)),
        out_shape=SDS((r, D), f32), compiler_params=_CP, name="sum_slots")(slots)


def _adamw(w, g, m, v):
    shape = w.shape
    cols = shape[-1]
    rows = w.size // cols
    rb = rows
    for cand in (512, 256, 128, 64, 32, 16, 8):
        if rows % cand == 0 and rows > cand:
            rb = cand
            break

    def body(w_ref, g_ref, m_ref, v_ref, d_ref, mo_ref, vo_ref):
        g = g_ref[...]
        m = ADAM_B1 * m_ref[...] + (1.0 - ADAM_B1) * g
        v = ADAM_B2 * v_ref[...] + (1.0 - ADAM_B2) * (g * g)
        m_hat = m / (1.0 - ADAM_B1 ** ADAM_STEP)
        v_hat = v / (1.0 - ADAM_B2 ** ADAM_STEP)
        d_ref[...] = -ADAM_LR * (m_hat / (jnp.sqrt(v_hat) + ADAM_EPS) + ADAM_WD * w_ref[...])
        mo_ref[...] = m
        vo_ref[...] = v

    spec = pl.BlockSpec((rb, cols), lambda i: (i, 0))
    outs = pl.pallas_call(
        body, grid=(rows // rb,), in_specs=[spec] * 4, out_specs=[spec] * 3,
        out_shape=[SDS((rows, cols), f32)] * 3, compiler_params=_CP, name="adamw")(
            *(a.reshape(rows, cols) for a in (w, g, m, v)))
    return tuple(o.reshape(shape) for o in outs)


_BIG = ("ffn1_w_gate", "ffn1_w_up", "ffn1_w_down", "w_in", "w_out", "ffn2_w_gate", "ffn2_w_up", "ffn2_w_down")
_TRANSPOSED = ("ffn1_w_gate", "ffn1_w_up", "w_in", "ffn2_w_gate", "ffn2_w_up")

def _block_diag(pool_w):
    out = jnp.zeros((L, PW, PW), pool_w.dtype)
    for gi in range(4):
        out = out.at[:, 64 * gi:64 * (gi + 1), 64 * gi:64 * (gi + 1)].set(pool_w[:, gi])
    return out


def kernel(x, positions, ffn1_norm, ffn1_w_gate, ffn1_w_up, ffn1_w_down, mix_norm, w_in, pool_w, pool_scale, w_out, ffn2_norm, ffn2_w_gate, ffn2_w_up, ffn2_w_down, final_norm, loss_target, m_ffn1_norm, m_ffn1_w_gate, m_ffn1_w_up, m_ffn1_w_down, m_mix_norm, m_w_in, m_pool_w, m_pool_scale, m_w_out, m_ffn2_norm, m_ffn2_w_gate, m_ffn2_w_up, m_ffn2_w_down, m_final_norm, v_ffn1_norm, v_ffn1_w_gate, v_ffn1_w_up, v_ffn1_w_down, v_mix_norm, v_w_in, v_pool_w, v_pool_scale, v_w_out, v_ffn2_norm, v_ffn2_w_gate, v_ffn2_w_up, v_ffn2_w_down, v_final_norm):
    weights = dict(ffn1_norm=ffn1_norm, ffn1_w_gate=ffn1_w_gate, ffn1_w_up=ffn1_w_up, ffn1_w_down=ffn1_w_down,
                   mix_norm=mix_norm, w_in=w_in, pool_w=pool_w, pool_scale=pool_scale, w_out=w_out,
                   ffn2_norm=ffn2_norm, ffn2_w_gate=ffn2_w_gate, ffn2_w_up=ffn2_w_up, ffn2_w_down=ffn2_w_down,
                   final_norm=final_norm)
    moms = dict(ffn1_norm=m_ffn1_norm, ffn1_w_gate=m_ffn1_w_gate, ffn1_w_up=m_ffn1_w_up, ffn1_w_down=m_ffn1_w_down,
                mix_norm=m_mix_norm, w_in=m_w_in, pool_w=m_pool_w, pool_scale=m_pool_scale, w_out=m_w_out,
                ffn2_norm=m_ffn2_norm, ffn2_w_gate=m_ffn2_w_gate, ffn2_w_up=m_ffn2_w_up, ffn2_w_down=m_ffn2_w_down,
                final_norm=m_final_norm)
    vels = dict(ffn1_norm=v_ffn1_norm, ffn1_w_gate=v_ffn1_w_gate, ffn1_w_up=v_ffn1_w_up, ffn1_w_down=v_ffn1_w_down,
                mix_norm=v_mix_norm, w_in=v_w_in, pool_w=v_pool_w, pool_scale=v_pool_scale, w_out=v_w_out,
                ffn2_norm=v_ffn2_norm, ffn2_w_gate=v_ffn2_w_gate, ffn2_w_up=v_ffn2_w_up, ffn2_w_down=v_ffn2_w_down,
                final_norm=v_final_norm)
    names = list(weights)

    tr = lambda w: jnp.swapaxes(w, 1, 2).astype(bf16)
    shards = [tr(weights[nm]) if nm in _TRANSPOSED else weights[nm].astype(bf16) for nm in _BIG]
    gathered = _all_gather(shards)

    dx, full, small = _forward_backward(x, positions, loss_target, gathered, ffn1_norm, mix_norm, ffn2_norm,
                                        final_norm, pool_w, pool_scale)

    fulls = [jnp.stack(full[nm]) for nm in _BIG]
    *slots, small_slots = _all_to_all(fulls, small)

    grads = {}
    for nm, sl in zip(_BIG, slots):
        rows = sl.shape[2]
        g = _sum_slots(sl.reshape(NDEV, L * rows, D), rows).reshape(L, rows, D)
        grads[nm] = jnp.swapaxes(g, 1, 2) if nm in _TRANSPOSED else g
    sm = _sum_slots(small_slots, SMALL_ROWS)
    grads["ffn1_norm"], grads["mix_norm"], grads["ffn2_norm"] = sm[0:L], sm[L:2 * L], sm[2 * L:3 * L]
    grads["final_norm"] = sm[3 * L]
    grads["pool_scale"] = sm[3 * L + 1].reshape(L, PW)
    grads["pool_w"] = sm[3 * L + 2:3 * L + 2 + L * 16].reshape(L, 4, 64, 64)
    loss = sm[3 * L + 2 + L * 16, 0]

    upd = {nm: _adamw(weights[nm], grads[nm], moms[nm], vels[nm]) for nm in names}
    return (loss, dx.reshape(1, S, D), *[grads[nm] for nm in names], *[upd[nm][0] for nm in names],
            *[upd[nm][1] for nm in names], *[upd[nm][2] for nm in names])


def _forward_backward(x, positions, loss_target, gathered, ffn1_norm, mix_norm, ffn2_norm, final_norm,
                      pool_w, pool_scale):
    gt1, ut1, dn1, wint, wout, gt2, ut2, dn2 = gathered
    big = _BIG
    g_ffn1 = ffn1_norm.reshape(L, 1, D)
    g_mix = mix_norm.reshape(L, 1, D)
    g_ffn2 = ffn2_norm.reshape(L, 1, D)
    wbd = _block_diag(pool_w).astype(bf16)
    pscale = pool_scale.reshape(L, 1, PW)
    tabs = _rope_tables(positions)

    xs = x.reshape(S, D)
    saved = []
    for l in range(L):
        x0 = xs
        x1, gate1, up1 = _ffn_fwd(x0, g_ffn1, gt1, ut1, dn1, l)
        hmix, vp, q1, k1, v1, q4, k4, v4, q16, k16, v16 = _mix_in_fwd(x1, g_mix, wint, tabs, l)
        flat = lambda a: a.reshape(S, AW)
        q4, k4, v4, q16, k16, v16 = map(flat, (q4, k4, v4, q16, k16, v16))
        ypool, diff = _pool_fwd(vp, wbd, pscale, l)
        o1, l1 = _attn_fwd(q1, k1, v1, S)
        o4, l4 = _attn_fwd(q4, k4, v4, S // 4)
        o16, l16 = _attn_fwd(q16, k16, v16, S // 16)
        r4 = lambda a: a.reshape(4, S // 4, AW)
        r16 = lambda a: a.reshape(16, S // 16, AW)
        x2, mixed, o, lse1, lse4, lse16 = _mix_out_fwd(x1, ypool, o1, l1, r4(o4), r4(l4), r16(o16), r16(l16), wout, l)
        x3, gate2, up2 = _ffn_fwd(x2, g_ffn2, gt2, ut2, dn2, l)
        saved.append(dict(x0=x0, x1=x1, x2=x2, gate1=gate1, up1=up1, gate2=gate2, up2=up2, hmix=hmix, diff=diff,
                          qkv=((q1, k1, v1), (q4, k4, v4), (q16, k16, v16)), mixed=mixed, o=o,
                          lse=(lse1, flat(lse4), flat(lse16))))
        xs = x3

    dx, loss_part, d_final = _loss_head(xs, final_norm.reshape(1, D), loss_target.reshape(S, D))

    full = {nm: [None] * L for nm in big}
    d_norm = {nm: [None] * L for nm in ("ffn1_norm", "mix_norm", "ffn2_norm")}
    d_poolw, d_pscale = [None] * L, [None] * L
    for l in reversed(range(L)):
        sv = saved[l]
        dx, dgate, dup, h, dy, d_norm["ffn2_norm"][l] = _ffn_bwd_d(sv["x2"], g_ffn2, sv["gate2"], sv["up2"], dx, gt2, ut2, dn2, l)
        full["ffn2_w_gate"][l], full["ffn2_w_up"][l], full["ffn2_w_down"][l] = _ffn_bwd_w(h, dy, sv["gate2"], sv["up2"], dgate, dup)

        dxb, dyp, do1, do4, do16, dl1, dl4, dl16 = _mix_out_bwd(dx, sv["o"], wout, l)
        full["w_out"][l] = _wgrad(sv["mixed"], dxb)
        dvp, dwbd, d_pscale[l] = _pool_bwd(dyp, sv["diff"], wbd, pscale, l)
        d_poolw[l] = jnp.stack([dwbd[64 * gi:64 * (gi + 1), 64 * gi:64 * (gi + 1)] for gi in range(4)])
        flat = lambda a: a.reshape(S, AW)
        dos, dls = (do1, flat(do4), flat(do16)), (dl1, flat(dl4), flat(dl16))
        grads = []
        for b, lc in enumerate((S, S // 4, S // 16)):
            qb, kb, vb = sv["qkv"][b]
            grads.append(_attn_bwd(qb, kb, vb, dos[b], sv["lse"][b], dls[b], lc))
        d4 = tuple(a.reshape(4, S // 4, AW) for a in grads[1])
        d16 = tuple(a.reshape(16, S // 16, AW) for a in grads[2])
        dx, dproj, d_norm["mix_norm"][l] = _mix_in_bwd(dx, sv["x1"], g_mix, wint, tabs, dvp, grads[0], d4, d16, l)
        full["w_in"][l] = _wgrad(dproj, sv["hmix"])

        dx, dgate, dup, h, dy, d_norm["ffn1_norm"][l] = _ffn_bwd_d(sv["x0"], g_ffn1, sv["gate1"], sv["up1"], dx, gt1, ut1, dn1, l)
        full["ffn1_w_gate"][l], full["ffn1_w_up"][l], full["ffn1_w_down"][l] = _ffn_bwd_w(h, dy, sv["gate1"], sv["up1"], dgate, dup)

    small = jnp.concatenate(
        [jnp.concatenate(d_norm[nm], axis=0) for nm in ("ffn1_norm", "mix_norm", "ffn2_norm")]
        + [d_final, jnp.concatenate(d_pscale, axis=1), jnp.stack(d_poolw).reshape(L * 16, D), loss_part,
           jnp.zeros((SMALL_ROWS - (3 * L + 2 + L * 16 + 1), D), f32)], axis=0)
    return dx, full, small
```

```python
import jax
import jax.numpy as jnp
from jax import lax
from jax.experimental import pallas as pl
from jax.experimental.pallas import tpu as pltpu

f32 = jnp.float32
bf16 = jnp.bfloat16
SDS = jax.ShapeDtypeStruct

D = 1024
S = 2048
F = 2816
L = 4
PW = 256
AW = 768
PROJ = PW + 3 * AW
NDEV = 8
SHARD_ROWS = (F // NDEV,) * 3 + (PROJ // NDEV, D // NDEV) + (F // NDEV,) * 3
SLOT_ROWS = sum(SHARD_ROWS)
TM = 256
QB = 128
HALF = 64
NG = AW // 128
NORM_EPS = 1e-6
MASK_VALUE = -1e30
ROPE_THETA = 500000.0
ADAM_LR, ADAM_B1, ADAM_B2, ADAM_EPS, ADAM_WD, ADAM_STEP = 0.001, 0.9, 0.999, 1e-08, 0.01, 10
POOL_WINDOWS = (2, 4, 8, 16)
PAD = 8
SMALL_ROWS = 96
VMEM_LIMIT = 56 * 1024 * 1024

_CP = pltpu.CompilerParams(vmem_limit_bytes=VMEM_LIMIT)
_ANY = pl.BlockSpec(memory_space=pl.ANY)
_HBM = pl.BlockSpec(memory_space=pltpu.HBM)
_SEM = pl.BlockSpec(memory_space=pltpu.SEMAPHORE)
_MESH = pl.DeviceIdType.MESH
_CP_SPLIT = pltpu.CompilerParams(has_side_effects=pltpu.SideEffectType.DATAFLOW_SIDE_EFFECTING)


def _dot_nn(a, b):
    return lax.dot_general(a, b, (((1,), (0,)), ((), ())), preferred_element_type=f32)


def _dot_nt(a, b):
    return lax.dot_general(a, b, (((1,), (1,)), ((), ())), preferred_element_type=f32)


def _dot_tn(a, b):
    return lax.dot_general(a, b, (((0,), (0,)), ((), ())), preferred_element_type=f32)


def _rms(x, g):
    r = lax.rsqrt(jnp.mean(x * x, axis=-1, keepdims=True) + NORM_EPS)
    xh = x * r
    return r, xh, xh * g


def _rms_bwd(dh, r, xh, g):
    dxh = dh * g
    return r * (dxh - xh * jnp.mean(dxh * xh, axis=-1, keepdims=True))


def _tile(cols):
    return pl.BlockSpec((TM, cols), lambda i: (i, 0))


def _const(shape):
    return pl.BlockSpec(shape, lambda i: (0,) * len(shape))


def _layer(rows, cols, l=None):
    return pl.BlockSpec((rows, cols), lambda i: (0, 0), pipeline_mode=pl.Buffered(1))


def _p4():
    return pl.BlockSpec((4, TM // 4, AW), lambda i: (0, i, 0))


def _p16():
    return pl.BlockSpec((16, TM // 16, AW), lambda i: (0, i, 0))


def _cols(j):
    return slice(128 * j, 128 * (j + 1))


def _ffn_fwd(x, g, gt, ut, dn, l=None):
    def body(x_ref, g_ref, gt_ref, ut_ref, dn_ref, xo_ref, gate_ref, up_ref):
        x = x_ref[...]
        _, _, hn = _rms(x, g_ref[...])
        h = hn.astype(bf16)
        gate = _dot_nt(h, gt_ref[...])
        up = _dot_nt(h, ut_ref[...])
        gate_ref[...] = gate.astype(bf16)
        up_ref[...] = up.astype(bf16)
        a = (gate * jax.nn.sigmoid(gate) * up).astype(bf16)
        xo_ref[...] = x + 0.5 * _dot_nn(a, dn_ref[...])

    return pl.pallas_call(
        body, grid=(S // TM,),
        in_specs=[_tile(D), _layer(1, D, l), _layer(F, D, l), _layer(F, D, l), _layer(F, D, l)],
        out_specs=[_tile(D), _tile(F), _tile(F)],
        out_shape=[SDS((S, D), f32), SDS((S, F), bf16), SDS((S, F), bf16)],
        compiler_params=_CP, name="ffn_fwd")(x, g, gt, ut, dn)


def _ffn_bwd_d(x, g, gate, up, dxo, gt, ut, dn, l=None):
    def body(x_ref, g_ref, gate_ref, up_ref, dxo_ref, gt_ref, ut_ref, dn_ref,
             dx_ref, dgate_ref, dup_ref, h_ref, dy_ref, dg_ref):
        x = x_ref[...]
        g = g_ref[...]
        r, xh, hn = _rms(x, g)
        h_ref[...] = hn.astype(bf16)
        dxo = dxo_ref[...]
        dy = (0.5 * dxo).astype(bf16)
        dy_ref[...] = dy
        da = _dot_nt(dy, dn_ref[...])
        gate = gate_ref[...].astype(f32)
        up = up_ref[...].astype(f32)
        sg = jax.nn.sigmoid(gate)
        dgate = (da * up * (sg * (1.0 + gate * (1.0 - sg)))).astype(bf16)
        dup = (da * (gate * sg)).astype(bf16)
        dgate_ref[...] = dgate
        dup_ref[...] = dup
        dh = _dot_nn(dgate, gt_ref[...]) + _dot_nn(dup, ut_ref[...])

        @pl.when(pl.program_id(0) == 0)
        def _():
            dg_ref[...] = jnp.zeros_like(dg_ref)

        dg_ref[...] += jnp.sum(dh * xh, axis=0, keepdims=True)
        dx_ref[...] = dxo + _rms_bwd(dh, r, xh, g)

    return pl.pallas_call(
        body, grid=(S // TM,),
        in_specs=[_tile(D), _layer(1, D, l), _tile(F), _tile(F), _tile(D),
                  _layer(F, D, l), _layer(F, D, l), _layer(F, D, l)],
        out_specs=[_tile(D), _tile(F), _tile(F), _tile(D), _tile(D), _const((1, D))],
        out_shape=[SDS((S, D), f32), SDS((S, F), bf16), SDS((S, F), bf16), SDS((S, D), bf16),
                   SDS((S, D), bf16), SDS((1, D), f32)],
        compiler_params=_CP, name="ffn_bwd_d")(x, g, gate, up, dxo, gt, ut, dn)


def _ffn_bwd_w(h, dy, gate, up, dgate, dup):
    fc = 256

    def body(h_ref, dy_ref, gate_ref, up_ref, dgate_ref, dup_ref, dgt_ref, dut_ref, ddn_ref):
        gate = gate_ref[...].astype(f32)
        a = (gate * jax.nn.sigmoid(gate) * up_ref[...].astype(f32)).astype(bf16)
        ddn_ref[...] = _dot_tn(a, dy_ref[...]).astype(bf16)
        h = h_ref[...]
        dgt_ref[...] = _dot_tn(dgate_ref[...], h).astype(bf16)
        dut_ref[...] = _dot_tn(dup_ref[...], h).astype(bf16)

    col = pl.BlockSpec((S, fc), lambda j: (0, j))
    row = pl.BlockSpec((fc, D), lambda j: (j, 0))
    full = pl.BlockSpec((S, D), lambda j: (0, 0))
    return pl.pallas_call(
        body, grid=(F // fc,),
        in_specs=[full, full, col, col, col, col],
        out_specs=[row, row, row],
        out_shape=[SDS((F, D), bf16)] * 3,
        compiler_params=_CP, name="ffn_bwd_w")(h, dy, gate, up, dgate, dup)


def _wgrad(a, b):
    m, n = a.shape[1], b.shape[1]
    mc = 256

    def body(a_ref, b_ref, o_ref):
        o_ref[...] = _dot_tn(a_ref[...], b_ref[...]).astype(bf16)

    return pl.pallas_call(
        body, grid=(m // mc,),
        in_specs=[pl.BlockSpec((S, mc), lambda j: (0, j)), pl.BlockSpec((S, n), lambda j: (0, 0))],
        out_specs=pl.BlockSpec((mc, n), lambda j: (j, 0)),
        out_shape=SDS((m, n), bf16),
        compiler_params=_CP, name="wgrad")(a, b)


def _rope(t, c, sn, sp):
    return t * c + pltpu.roll(t, 120, 1) * sn + pltpu.roll(t, 8, 1) * sp


def _rope_bwd(d, c, sn, sp):
    return d * c + pltpu.roll(d * sn, 8, 1) + pltpu.roll(d * sp, 120, 1)


def _rope_tables(positions):
    inv_freq = ROPE_THETA ** (-jnp.arange(0, 16, 2, dtype=f32) / 16)
    ang = positions.reshape(S, 1).astype(f32) * inv_freq
    cos, sin = jnp.cos(ang), jnp.sin(ang)
    one = jnp.ones((S, 48), f32)
    zero8 = jnp.zeros((S, 8), f32)
    zero48 = jnp.zeros((S, 48), f32)
    c = jnp.concatenate([cos, cos, one], axis=1)
    sn = jnp.concatenate([-sin, zero8, zero48], axis=1)
    sp = jnp.concatenate([zero8, sin, zero48], axis=1)
    return tuple(jnp.concatenate([t, t], axis=1) for t in (c, sn, sp))


def _mix_in_fwd(x, g, wint, tabs, l=None):
    def body(x_ref, g_ref, w_ref, c_ref, sn_ref, sp_ref,
             h_ref, vp_ref, q1, k1, v1, q4, k4, v4, q16, k16, v16, scr):
        _, _, hn = _rms(x_ref[...], g_ref[...])
        h = hn.astype(bf16)
        h_ref[...] = h
        proj = _dot_nt(h, w_ref[...])
        vp_ref[...] = proj[:, :PW]
        c, sn, sp = c_ref[...], sn_ref[...], sp_ref[...]
        for kind, (o1, o4, o16) in enumerate(((q1, q4, q16), (k1, k4, k16), (v1, v4, v16))):
            for j in range(NG):
                t = proj[:, PW + kind * AW + 128 * j: PW + kind * AW + 128 * (j + 1)]
                if kind == 0:
                    t = _rope(t, c, sn, sp) * 0.125
                elif kind == 1:
                    t = _rope(t, c, sn, sp)
                scr[j] = t
                o1[:, _cols(j)] = t.astype(bf16)
            for r in range(4):
                for j in range(NG):
                    o4[r, :, _cols(j)] = scr[j, pl.ds(r, TM // 4, stride=4), :].astype(bf16)
            for r in range(16):
                for j in range(NG):
                    o16[r, :, _cols(j)] = scr[j, pl.ds(r, TM // 16, stride=16), :].astype(bf16)

    nat, d4, d16 = SDS((S, AW), bf16), SDS((4, S // 4, AW), bf16), SDS((16, S // 16, AW), bf16)
    return pl.pallas_call(
        body, grid=(S // TM,),
        in_specs=[_tile(D), _layer(1, D, l), _layer(PROJ, D, l), _tile(128), _tile(128), _tile(128)],
        out_specs=[_tile(D), _tile(PW)] + [_tile(AW)] * 3 + [_p4()] * 3 + [_p16()] * 3,
        out_shape=[SDS((S, D), bf16), SDS((S, PW), f32)] + [nat] * 3 + [d4] * 3 + [d16] * 3,
        scratch_shapes=[pltpu.VMEM((NG, TM, 128), f32)],
        compiler_params=_CP, name="mix_in_fwd")(x, g, wint, *tabs)


def _mix_in_bwd(dxo, x, g, wint, tabs, dvp, d1, d4, d16, l=None):
    def body(dxo_ref, x_ref, g_ref, w_ref, c_ref, sn_ref, sp_ref, dvp_ref,
             dq1, dk1, dv1, dq4, dk4, dv4, dq16, dk16, dv16,
             dx_ref, dproj_ref, dg_ref, s4, s16):
        c, sn, sp = c_ref[...], sn_ref[...], sp_ref[...]
        dproj_ref[:, :PW] = dvp_ref[...].astype(bf16)
        for kind, (a1, a4, a16) in enumerate(((dq1, dq4, dq16), (dk1, dk4, dk16), (dv1, dv4, dv16))):
            for r in range(4):
                for j in range(NG):
                    s4[j, pl.ds(r, TM // 4, stride=4), :] = a4[r, :, _cols(j)]
            for r in range(16):
                for j in range(NG):
                    s16[j, pl.ds(r, TM // 16, stride=16), :] = a16[r, :, _cols(j)]
            for j in range(NG):
                t = a1[:, _cols(j)] + s4[j] + s16[j]
                if kind == 0:
                    t = _rope_bwd(t * 0.125, c, sn, sp)
                elif kind == 1:
                    t = _rope_bwd(t, c, sn, sp)
                dproj_ref[:, PW + kind * AW + 128 * j: PW + kind * AW + 128 * (j + 1)] = t.astype(bf16)
        g = g_ref[...]
        r_, xh, _ = _rms(x_ref[...], g)
        dh = _dot_nn(dproj_ref[...], w_ref[...])

        @pl.when(pl.program_id(0) == 0)
        def _():
            dg_ref[...] = jnp.zeros_like(dg_ref)

        dg_ref[...] += jnp.sum(dh * xh, axis=0, keepdims=True)
        dx_ref[...] = dxo_ref[...] + _rms_bwd(dh, r_, xh, g)

    return pl.pallas_call(
        body, grid=(S // TM,),
        in_specs=[_tile(D), _tile(D), _layer(1, D, l), _layer(PROJ, D, l), _tile(128), _tile(128), _tile(128),
                  _tile(PW)] + [_tile(AW)] * 3 + [_p4()] * 3 + [_p16()] * 3,
        out_specs=[_tile(D), _tile(PROJ), _const((1, D))],
        out_shape=[SDS((S, D), f32), SDS((S, PROJ), bf16), SDS((1, D), f32)],
        scratch_shapes=[pltpu.VMEM((NG, TM, 128), f32), pltpu.VMEM((NG, TM, 128), f32)],
        compiler_params=_CP, name="mix_in_bwd")(dxo, x, g, wint, *tabs, dvp, *d1, *d4, *d16)


def _pool_sums(pad_ref, base, rows, adjoint):
    lane_group = lax.broadcasted_iota(jnp.int32, (rows, PW), 1) // 64
    sign = -1 if adjoint else 1

    def sh(o):
        return pad_ref[pl.ds(PAD + base + sign * o, rows), :]

    out = None
    acc = None
    lo, hi = 0, 0
    for gi, w in enumerate(POOL_WINDOWS):
        for o in list(range(-(w // 2), lo)) + list(range(hi, w - w // 2)):
            acc = sh(o) if acc is None else acc + sh(o)
        lo, hi = -(w // 2), w - w // 2
        out = acc if out is None else jnp.where(lane_group >= gi, acc, out)
    return out


def _pool_counts(base, rows):
    pos = base + lax.broadcasted_iota(jnp.int32, (rows, PW), 0)
    lane_group = lax.broadcasted_iota(jnp.int32, (rows, PW), 1) // 64
    cnt = None
    for gi, w in enumerate(POOL_WINDOWS):
        lo = jnp.maximum(pos - w // 2, 0)
        hi = jnp.minimum(pos + w - 1 - w // 2, S - 1)
        c = (hi - lo + 1).astype(f32)
        cnt = c if cnt is None else jnp.where(lane_group >= gi, c, cnt)
    return cnt


def _pool_fwd(vp, wbd, scale, l=None):
    ch = 256

    def body(vp_ref, w_ref, sc_ref, y_ref, diff_ref, pad):
        pad[pl.ds(0, PAD), :] = jnp.zeros((PAD, PW), f32)
        pad[pl.ds(PAD + S, PAD), :] = jnp.zeros((PAD, PW), f32)
        pad[pl.ds(PAD, S), :] = vp_ref[...]
        for b in range(S // ch):
            base = b * ch
            pooled = _pool_sums(pad, base, ch, False) / _pool_counts(base, ch)
            diff = (pooled - vp_ref[pl.ds(base, ch), :]).astype(bf16)
            diff_ref[pl.ds(base, ch), :] = diff
            y_ref[pl.ds(base, ch), :] = _dot_nn(diff, w_ref[...]) * sc_ref[...]

    whole = lambda shape: pl.BlockSpec(shape, lambda i: (0,) * len(shape))
    return pl.pallas_call(
        body, grid=(1,),
        in_specs=[whole((S, PW)), whole((PW, PW)), whole((1, PW))],
        out_specs=[whole((S, PW)), whole((S, PW))],
        out_shape=[SDS((S, PW), f32), SDS((S, PW), bf16)],
        scratch_shapes=[pltpu.VMEM((S + 2 * PAD, PW), f32)],
        compiler_params=_CP, name="pool_fwd")(vp, wbd, scale)


def _pool_bwd(dy, diff, wbd, scale, l=None):
    ch = 256

    def body(dy_ref, diff_ref, w_ref, sc_ref, dvp_ref, dw_ref, dsc_ref, pad):
        pad[pl.ds(0, PAD), :] = jnp.zeros((PAD, PW), f32)
        pad[pl.ds(PAD + S, PAD), :] = jnp.zeros((PAD, PW), f32)
        dw = jnp.zeros((PW, PW), f32)
        dsc = jnp.zeros((1, PW), f32)
        for b in range(S // ch):
            base = b * ch
            dy = dy_ref[pl.ds(base, ch), :]
            diff = diff_ref[pl.ds(base, ch), :]
            dsc = dsc + jnp.sum(dy * _dot_nn(diff, w_ref[...]), axis=0, keepdims=True)
            dz = (dy * sc_ref[...]).astype(bf16)
            dw = dw + _dot_tn(diff, dz)
            ddiff = _dot_nt(dz, w_ref[...])
            dvp_ref[pl.ds(base, ch), :] = -ddiff
            pad[pl.ds(PAD + base, ch), :] = ddiff / _pool_counts(base, ch)
        dw_ref[...] = dw
        dsc_ref[...] = dsc
        for b in range(S // ch):
            base = b * ch
            dvp_ref[pl.ds(base, ch), :] += _pool_sums(pad, base, ch, True)

    whole = lambda shape: pl.BlockSpec(shape, lambda i: (0,) * len(shape))
    return pl.pallas_call(
        body, grid=(1,),
        in_specs=[whole((S, PW)), whole((S, PW)), whole((PW, PW)), whole((1, PW))],
        out_specs=[whole((S, PW)), whole((PW, PW)), whole((1, PW))],
        out_shape=[SDS((S, PW), f32), SDS((PW, PW), f32), SDS((1, PW), f32)],
        scratch_shapes=[pltpu.VMEM((S + 2 * PAD, PW), f32)],
        compiler_params=_CP, name="pool_bwd")(dy, diff, wbd, scale)


def _attn_block_index(b, lc):
    bpc = lc // QB
    kw = min(2 * QB, lc)
    row0 = pl.multiple_of(b * QB, QB)
    t0 = (b % bpc) * QB
    ks_in = jnp.clip(t0 - HALF, 0, lc - kw)
    kstart = pl.multiple_of((b // bpc) * lc + ks_in, HALF)
    qpos = t0 + lax.broadcasted_iota(jnp.int32, (QB, kw), 0)
    kpos = ks_in + lax.broadcasted_iota(jnp.int32, (QB, kw), 1)
    valid = jnp.abs(qpos - kpos) <= HALF
    return kw, row0, kstart, valid


def _attn_fwd(q, k, v, lc):
    def body(q_ref, k_ref, v_ref, o_ref, lse_ref):
        head0 = lax.broadcasted_iota(jnp.int32, (QB, 128), 1) < 64

        def blk(b, carry):
            kw, row0, kstart, valid = _attn_block_index(b, lc)
            qb = q_ref[pl.ds(row0, QB), :]
            kb = k_ref[pl.ds(kstart, kw), :]
            vb = v_ref[pl.ds(kstart, kw), :]
            outs, lses = [], []
            for hh in range(2):
                lm = head0 if hh == 0 else jnp.logical_not(head0)
                qh = jnp.where(lm, qb, jnp.zeros_like(qb))
                s = jnp.where(valid, _dot_nt(qh, kb), MASK_VALUE)
                m = jnp.max(s, axis=-1, keepdims=True)
                p = jnp.exp(s - m)
                den = jnp.sum(p, axis=-1, keepdims=True)
                outs.append(_dot_nn(p.astype(bf16), vb) / den)
                lses.append(jnp.broadcast_to(m + jnp.log(den), (QB, 128)))
            o_ref[pl.ds(row0, QB), :] = jnp.where(head0, outs[0], outs[1])
            lse_ref[pl.ds(row0, QB), :] = jnp.where(head0, lses[0], lses[1])
            return carry

        lax.fori_loop(0, S // QB, blk, 0)

    col = pl.BlockSpec((S, 128), lambda p: (0, p))
    return pl.pallas_call(
        body, grid=(NG,), in_specs=[col, col, col], out_specs=[col, col],
        out_shape=[SDS((S, AW), f32), SDS((S, AW), f32)],
        compiler_params=_CP, name=f"attn_fwd_{lc}")(q, k, v)


def _attn_bwd(q, k, v, do, lse, delta, lc):
    def body(q_ref, k_ref, v_ref, do_ref, lse_ref, dl_ref, dq_ref, dk_ref, dv_ref):
        head0 = lax.broadcasted_iota(jnp.int32, (QB, 128), 1) < 64
        dk_ref[...] = jnp.zeros_like(dk_ref)
        dv_ref[...] = jnp.zeros_like(dv_ref)

        def blk(b, carry):
            kw, row0, kstart, valid = _attn_block_index(b, lc)
            khead0 = lax.broadcasted_iota(jnp.int32, (kw, 128), 1) < 64
            qb = q_ref[pl.ds(row0, QB), :]
            dob = do_ref[pl.ds(row0, QB), :]
            lse = lse_ref[pl.ds(row0, QB), :]
            dl = dl_ref[pl.ds(row0, QB), :]
            kb = k_ref[pl.ds(kstart, kw), :]
            vb = v_ref[pl.ds(kstart, kw), :]
            dqs, dks, dvs = [], [], []
            for hh in range(2):
                lm = head0 if hh == 0 else jnp.logical_not(head0)
                c0 = 64 * hh
                qh = jnp.where(lm, qb, jnp.zeros_like(qb))
                doh = jnp.where(lm, dob, jnp.zeros_like(dob))
                s = jnp.where(valid, _dot_nt(qh, kb), MASK_VALUE)
                p = jnp.exp(s - lse[:, c0:c0 + 1])
                dp = _dot_nt(doh, vb)
                ds = (p * (dp - dl[:, c0:c0 + 1])).astype(bf16)
                dqs.append(_dot_nn(ds, kb))
                dks.append(_dot_tn(ds, qb))
                dvs.append(_dot_tn(p.astype(bf16), dob))
            dq_ref[pl.ds(row0, QB), :] = jnp.where(head0, dqs[0], dqs[1])
            dk_ref[pl.ds(kstart, kw), :] += jnp.where(khead0, dks[0], dks[1])
            dv_ref[pl.ds(kstart, kw), :] += jnp.where(khead0, dvs[0], dvs[1])
            return carry

        lax.fori_loop(0, S // QB, blk, 0)

    col = pl.BlockSpec((S, 128), lambda p: (0, p))
    return pl.pallas_call(
        body, grid=(NG,), in_specs=[col] * 6, out_specs=[col] * 3,
        out_shape=[SDS((S, AW), f32)] * 3,
        compiler_params=_CP, name=f"attn_bwd_{lc}")(q, k, v, do, lse, delta)


def _mix_out_fwd(x, ypool, o1, l1, o4, l4, o16, l16, wout, l=None):
    def body(x_ref, yp_ref, o1_ref, l1_ref, o4_ref, l4_ref, o16_ref, l16_ref, w_ref,
             xo_ref, mixed_ref, o_ref, lse1_ref, lse4_ref, lse16_ref, so4, sl4, so16, sl16, sl):
        for r in range(4):
            for j in range(NG):
                so4[j, pl.ds(r, TM // 4, stride=4), :] = o4_ref[r, :, _cols(j)]
                sl4[j, pl.ds(r, TM // 4, stride=4), :] = l4_ref[r, :, _cols(j)]
        for r in range(16):
            for j in range(NG):
                so16[j, pl.ds(r, TM // 16, stride=16), :] = o16_ref[r, :, _cols(j)]
                sl16[j, pl.ds(r, TM // 16, stride=16), :] = l16_ref[r, :, _cols(j)]
        mixed_ref[:, :PW] = yp_ref[...].astype(bf16)
        for j in range(NG):
            a, b, c = l1_ref[:, _cols(j)], sl4[j], sl16[j]
            m = jnp.maximum(jnp.maximum(a, b), c)
            wa, wb, wc = jnp.exp(a - m), jnp.exp(b - m), jnp.exp(c - m)
            den = wa + wb + wc
            y = (wa * o1_ref[:, _cols(j)] + wb * so4[j] + wc * so16[j]) / den
            lse = m + jnp.log(den)
            o_ref[:, _cols(j)] = y
            lse1_ref[:, _cols(j)] = lse
            sl[j] = lse
            mixed_ref[:, PW + 128 * j: PW + 128 * (j + 1)] = y.astype(bf16)
        for r in range(4):
            for j in range(NG):
                lse4_ref[r, :, _cols(j)] = sl[j, pl.ds(r, TM // 4, stride=4), :]
        for r in range(16):
            for j in range(NG):
                lse16_ref[r, :, _cols(j)] = sl[j, pl.ds(r, TM // 16, stride=16), :]
        xo_ref[...] = x_ref[...] + _dot_nn(mixed_ref[...], w_ref[...])

    scr = pltpu.VMEM((NG, TM, 128), f32)
    return pl.pallas_call(
        body, grid=(S // TM,),
        in_specs=[_tile(D), _tile(PW), _tile(AW), _tile(AW), _p4(), _p4(), _p16(), _p16(), _layer(D, D, l)],
        out_specs=[_tile(D), _tile(D), _tile(AW), _tile(AW), _p4(), _p16()],
        out_shape=[SDS((S, D), f32), SDS((S, D), bf16), SDS((S, AW), f32), SDS((S, AW), f32),
                   SDS((4, S // 4, AW), f32), SDS((16, S // 16, AW), f32)],
        scratch_shapes=[scr] * 5,
        compiler_params=_CP, name="mix_out_fwd")(x, ypool, o1, l1, o4, l4, o16, l16, wout)


def _segsum64(t):
    lane = lax.broadcasted_iota(jnp.int32, t.shape, 1)
    for s in (1, 2, 4, 8, 16, 32):
        t = t + jnp.where((lane & s) != 0, pltpu.roll(t, s, 1), pltpu.roll(t, 128 - s, 1))
    return t


def _mix_out_bwd(dxo, o, wout, l=None):
    def body(dxo_ref, o_ref, w_ref, dxb_ref, dyp_ref, do1, do4, do16, dl1, dl4, dl16, sdo, sdl):
        dxb = dxo_ref[...].astype(bf16)
        dxb_ref[...] = dxb
        dm = _dot_nt(dxb, w_ref[...])
        dyp_ref[...] = dm[:, :PW]
        for j in range(NG):
            d = dm[:, PW + 128 * j: PW + 128 * (j + 1)]
            dl = _segsum64(d * o_ref[:, _cols(j)])
            do1[:, _cols(j)] = d.astype(bf16)
            dl1[:, _cols(j)] = dl
            sdo[j] = d
            sdl[j] = dl
        for r in range(4):
            for j in range(NG):
                do4[r, :, _cols(j)] = sdo[j, pl.ds(r, TM // 4, stride=4), :].astype(bf16)
                dl4[r, :, _cols(j)] = sdl[j, pl.ds(r, TM // 4, stride=4), :]
        for r in range(16):
            for j in range(NG):
                do16[r, :, _cols(j)] = sdo[j, pl.ds(r, TM // 16, stride=16), :].astype(bf16)
                dl16[r, :, _cols(j)] = sdl[j, pl.ds(r, TM // 16, stride=16), :]

    scr = pltpu.VMEM((NG, TM, 128), f32)
    return pl.pallas_call(
        body, grid=(S // TM,),
        in_specs=[_tile(D), _tile(AW), _layer(D, D, l)],
        out_specs=[_tile(D), _tile(PW), _tile(AW), _p4(), _p16(), _tile(AW), _p4(), _p16()],
        out_shape=[SDS((S, D), bf16), SDS((S, PW), f32),
                   SDS((S, AW), bf16), SDS((4, S // 4, AW), bf16), SDS((16, S // 16, AW), bf16),
                   SDS((S, AW), f32), SDS((4, S // 4, AW), f32), SDS((16, S // 16, AW), f32)],
        scratch_shapes=[scr] * 2,
        compiler_params=_CP, name="mix_out_bwd")(dxo, o, wout)


def _loss_head(x, g, target):
    def body(x_ref, g_ref, t_ref, dx_ref, loss_ref, dg_ref):
        g = g_ref[...]
        r, xh, y = _rms(x_ref[...], g)
        err = y - t_ref[...]
        dy = err * (1.0 / D)

        @pl.when(pl.program_id(0) == 0)
        def _():
            loss_ref[...] = jnp.zeros_like(loss_ref)
            dg_ref[...] = jnp.zeros_like(dg_ref)

        loss_ref[...] += jnp.broadcast_to(0.5 * jnp.sum(jnp.mean(err * err, axis=-1, keepdims=True)), (1, D))
        dg_ref[...] += jnp.sum(dy * xh, axis=0, keepdims=True)
        dx_ref[...] = _rms_bwd(dy, r, xh, g)

    return pl.pallas_call(
        body, grid=(S // TM,),
        in_specs=[_tile(D), _const((1, D)), _tile(D)],
        out_specs=[_tile(D), _const((1, D)), _const((1, D))],
        out_shape=[SDS((S, D), f32), SDS((1, D), f32), SDS((1, D), f32)],
        compiler_params=_CP, name="loss_head")(x, g, target)


def _peer(k):
    x, y, c = lax.axis_index("x"), lax.axis_index("y"), lax.axis_index("c")
    px = 1 - x if k & 4 else x
    py = 1 - y if k & 2 else y
    pc = 1 - c if k & 1 else c
    return (px, py, pc), 4 * px + 2 * py + pc


def _all_gather(shards):
    n = len(shards)

    def body(*refs):
        ins, outs = refs[:n], refs[n:2 * n]
        send_sems, recv_sems, local_sems = refs[2 * n:]
        me, me_idx = _peer(0)
        sibling, sib_idx = _peer(1)
        far = [_peer(k) for k in (4, 2, 6)]
        far_sib = [_peer(k) for k in (5, 3, 7)]

        def rows(t, idx):
            r = ins[t].shape[1]
            return outs[t].at[:, pl.ds(idx * r, r), :]

        def copy(k, t, idx, to, src=None):
            return pltpu.make_async_remote_copy(
                src_ref=rows(t, idx) if src is None else src, dst_ref=rows(t, idx),
                send_sem=send_sems.at[k, t], recv_sem=recv_sems.at[k, t], device_id=to, device_id_type=_MESH)

        mine = [pltpu.make_async_copy(ins[t], rows(t, me_idx), local_sems.at[t]) for t in range(n)]
        for cp in mine:
            cp.start()
        first = [copy(0, t, me_idx, sibling, src=ins[t]) for t in range(n)]
        for j, (dev, _) in enumerate(far):
            first += [copy(1 + j, t, me_idx, dev, src=ins[t]) for t in range(n)]
        for cp in first:
            cp.start()
        passed = []
        for j, (_, idx) in enumerate(far):
            for t in range(n):
                copy(1 + j, t, idx, me).wait_recv()
                cp = copy(4 + j, t, idx, sibling)
                cp.start()
                passed.append(cp)
        for t in range(n):
            copy(0, t, sib_idx, me).wait_recv()
        for j, (_, idx) in enumerate(far_sib):
            for t in range(n):
                copy(4 + j, t, idx, me).wait_recv()
        for cp in first + passed:
            cp.wait_send()
        for cp in mine:
            cp.wait()

    return pl.pallas_call(
        body, in_specs=[_ANY] * n, out_specs=[_ANY] * n,
        out_shape=[SDS((a.shape[0], NDEV * a.shape[1], a.shape[2]), a.dtype) for a in shards],
        scratch_shapes=[pltpu.SemaphoreType.DMA((7, n)), pltpu.SemaphoreType.DMA((7, n)),
                        pltpu.SemaphoreType.DMA((n,))],
        name="all_gather_weights")(*shards)


def _gather_small(small):
    def body(small_in, small_out, send_sems, recv_sems, local_sem):
        _, me_idx = _peer(0)

        def copy(k, to_dev, from_idx):
            return pltpu.make_async_remote_copy(
                src_ref=small_in, dst_ref=small_out.at[from_idx], send_sem=send_sems.at[k - 1],
                recv_sem=recv_sems.at[k - 1], device_id=to_dev, device_id_type=_MESH)

        mine = pltpu.make_async_copy(small_in, small_out.at[me_idx], local_sem)
        mine.start()
        sends = [copy(k, _peer(k)[0], me_idx) for k in range(1, NDEV)]
        for cp in sends:
            cp.start()
        for k in range(1, NDEV):
            dev, idx = _peer(k)
            copy(k, dev, idx).wait_recv()
        for cp in sends:
            cp.wait_send()
        mine.wait()

    return pl.pallas_call(
        body, in_specs=[_ANY], out_specs=_ANY, out_shape=SDS((NDEV,) + small.shape, small.dtype),
        scratch_shapes=[pltpu.SemaphoreType.DMA((NDEV - 1,)), pltpu.SemaphoreType.DMA((NDEV - 1,)),
                        pltpu.SemaphoreType.DMA],
        name="gather_small")(small)


def _hbm(a):
    return pltpu.with_memory_space_constraint(a, pltpu.HBM)


def _rows(ref, idx):
    r = ref.shape[0] // NDEV
    return ref.at[pl.ds(idx * r, r), :]


def _row_copy(ref, idx, send_sem, recv_sem, to):
    return pltpu.make_async_remote_copy(src_ref=_rows(ref, idx), dst_ref=_rows(ref, idx), send_sem=send_sem,
                                        recv_sem=recv_sem, device_id=to, device_id_type=_MESH)


_TOKEN = SDS((8, 128), f32)
_FAR = (4, 2, 6)
_FAR_SIB = (5, 3, 7)


def _ag_start(lands, l):
    n = len(lands)

    def body(*refs):
        zones, send_sems, recv_sems, token = refs[:n], refs[n], refs[n + 1], refs[-1]
        _, me_idx = _peer(0)
        for k, mask in enumerate((1,) + _FAR):
            for t in range(n):
                _row_copy(zones[t], me_idx, send_sems.at[k * n + t], recv_sems.at[k * n + t], _peer(mask)[0]).start()
        token[...] = jnp.zeros_like(token)

    outs = pl.pallas_call(
        body, name=f"ag_start_{l}", in_specs=[_HBM] * n,
        out_specs=(_SEM, _SEM, *[_HBM] * n, pl.BlockSpec(memory_space=pltpu.VMEM)),
        out_shape=(pltpu.SemaphoreType.DMA((4 * n,)), pltpu.SemaphoreType.DMA((4 * n,)),
                   *[pltpu.HBM(a.shape, a.dtype) for a in lands], _TOKEN),
        input_output_aliases={t: 2 + t for t in range(n)}, compiler_params=_CP_SPLIT)(*[_hbm(a) for a in lands])
    return outs[0], outs[1], list(outs[2:2 + n]), outs[-1]


def _ag_pass(lands, recv_sems, after, l):
    n = len(lands)

    def body(*refs):
        zones, recv_sems = refs[:n], refs[n]
        psend, precv, token = refs[n + 2], refs[n + 3], refs[-1]
        me, _ = _peer(0)
        sibling, _ = _peer(1)
        for j, mask in enumerate(_FAR):
            idx = _peer(mask)[1]
            for t in range(n):
                _row_copy(zones[t], idx, psend.at[j * n + t], recv_sems.at[(1 + j) * n + t], me).wait_recv()
                _row_copy(zones[t], idx, psend.at[j * n + t], precv.at[j * n + t], sibling).start()
        token[...] = jnp.zeros_like(token)

    outs = pl.pallas_call(
        body, name=f"ag_pass_{l}", in_specs=[_HBM] * n + [_SEM, _ANY],
        out_specs=(_SEM, _SEM, *[_HBM] * n, pl.BlockSpec(memory_space=pltpu.VMEM)),
        out_shape=(pltpu.SemaphoreType.DMA((3 * n,)), pltpu.SemaphoreType.DMA((3 * n,)),
                   *[pltpu.HBM(a.shape, a.dtype) for a in lands], _TOKEN),
        input_output_aliases={t: 2 + t for t in range(n)}, compiler_params=_CP_SPLIT)(*lands, recv_sems, after)
    return outs[0], outs[1], list(outs[2:2 + n]), outs[-1]


def _ag_wait(lands, send_sems, recv_sems, psend, precv, after, l):
    n = len(lands)

    def body(*refs):
        zones = refs[:n]
        send_sems, recv_sems, psend, precv = refs[n:n + 4]
        me, me_idx = _peer(0)
        sib_idx = _peer(1)[1]
        for k in range(4):
            for t in range(n):
                _row_copy(zones[t], me_idx, send_sems.at[k * n + t], recv_sems.at[k * n + t], me).wait_send()
        for t in range(n):
            _row_copy(zones[t], sib_idx, send_sems.at[t], recv_sems.at[t], me).wait_recv()
        for j in range(3):
            mine, theirs = _peer(_FAR[j])[1], _peer(_FAR_SIB[j])[1]
            for t in range(n):
                _row_copy(zones[t], mine, psend.at[j * n + t], precv.at[j * n + t], me).wait_send()
                _row_copy(zones[t], theirs, psend.at[j * n + t], precv.at[j * n + t], me).wait_recv()

    outs = pl.pallas_call(
        body, name=f"ag_wait_{l}", in_specs=[_HBM] * n + [_SEM] * 4 + [_ANY], out_specs=tuple([_HBM] * n),
        out_shape=tuple(pltpu.HBM(a.shape, a.dtype) for a in lands),
        input_output_aliases={t: t for t in range(n)}, compiler_params=_CP_SPLIT)(
            *lands, send_sems, recv_sems, psend, precv, after)
    return list(outs)


def _slot_rows(ref, slot, t):
    off = sum(SHARD_ROWS[:t])
    return ref.at[slot, pl.ds(off, SHARD_ROWS[t]), :]


def _rs_start(grads, slots, l):
    n = len(grads)

    def body(*refs):
        full, slot_ref = refs[:n], refs[n]
        send_sems, recv_sems, token = refs[n + 1], refs[n + 2], refs[-1]
        _, me_idx = _peer(0)
        for k in range(1, NDEV):
            dev, idx = _peer(k)
            for t in range(n):
                pltpu.make_async_remote_copy(
                    src_ref=_rows(full[t], idx), dst_ref=_slot_rows(slot_ref, me_idx, t),
                    send_sem=send_sems.at[(k - 1) * n + t], recv_sem=recv_sems.at[(k - 1) * n + t],
                    device_id=dev, device_id_type=_MESH).start()
        token[...] = jnp.zeros_like(token)

    outs = pl.pallas_call(
        body, name=f"rs_start_{l}", in_specs=[_HBM] * (n + 1),
        out_specs=(_SEM, _SEM, *[_HBM] * (n + 1), pl.BlockSpec(memory_space=pltpu.VMEM)),
        out_shape=(pltpu.SemaphoreType.DMA(((NDEV - 1) * n,)), pltpu.SemaphoreType.DMA(((NDEV - 1) * n,)),
                   *[pltpu.HBM(a.shape, a.dtype) for a in grads], pltpu.HBM(slots.shape, slots.dtype), _TOKEN),
        input_output_aliases={t: 2 + t for t in range(n + 1)}, compiler_params=_CP_SPLIT)(
            *[_hbm(a) for a in grads], _hbm(slots))
    return outs[0], outs[1], list(outs[2:2 + n]), outs[2 + n], outs[-1]


def _rs_wait(grads, slots, send_sems, recv_sems, after, l):
    n = len(grads)

    def body(*refs):
        full, slot_ref, send_sems, recv_sems = refs[:n], refs[n], refs[n + 1], refs[n + 2]
        me, me_idx = _peer(0)
        for k in range(1, NDEV):
            idx = _peer(k)[1]
            for t in range(n):
                cp = pltpu.make_async_remote_copy(
                    src_ref=_rows(full[t], idx), dst_ref=_slot_rows(slot_ref, idx, t),
                    send_sem=send_sems.at[(k - 1) * n + t], recv_sem=recv_sems.at[(k - 1) * n + t],
                    device_id=me, device_id_type=_MESH)
                cp.wait_send()
                cp.wait_recv()

    outs = pl.pallas_call(
        body, name=f"rs_wait_{l}", in_specs=[_HBM] * (n + 1) + [_SEM, _SEM, _ANY],
        out_specs=tuple([_HBM] * (n + 1)),
        out_shape=(*[pltpu.HBM(a.shape, a.dtype) for a in grads], pltpu.HBM(slots.shape, slots.dtype)),
        input_output_aliases={t: t for t in range(n + 1)}, compiler_params=_CP_SPLIT)(
            *grads, slots, send_sems, recv_sems, after)
    return outs[n]


def _sum_slots(slots, rb):
    r = slots.shape[1]

    def body(s_ref, o_ref):
        acc = s_ref[0].astype(f32)
        for s in range(1, NDEV):
            acc = acc + s_ref[s].astype(f32)
        o_ref[...] = acc

    return pl.pallas_call(
        body, grid=(r // rb,),
        in_specs=[pl.BlockSpec((NDEV, rb, D), lambda i: (0, i, 0))],
        out_specs=pl.BlockSpec((rb, D), lambda i: (i, 0)),
        out_shape=SDS((r, D), f32), compiler_params=_CP, name="sum_slots")(slots)


def _adamw(w, g, m, v):
    shape = w.shape
    cols = shape[-1]
    rows = w.size // cols
    rb = rows
    for cand in (512, 256, 128, 64, 32, 16, 8):
        if rows % cand == 0 and rows > cand:
            rb = cand
            break

    def body(w_ref, g_ref, m_ref, v_ref, d_ref, mo_ref, vo_ref):
        g = g_ref[...]
        m = ADAM_B1 * m_ref[...] + (1.0 - ADAM_B1) * g
        v = ADAM_B2 * v_ref[...] + (1.0 - ADAM_B2) * (g * g)
        m_hat = m / (1.0 - ADAM_B1 ** ADAM_STEP)
        v_hat = v / (1.0 - ADAM_B2 ** ADAM_STEP)
        d_ref[...] = -ADAM_LR * (m_hat / (jnp.sqrt(v_hat) + ADAM_EPS) + ADAM_WD * w_ref[...])
        mo_ref[...] = m
        vo_ref[...] = v

    spec = pl.BlockSpec((rb, cols), lambda i: (i, 0))
    outs = pl.pallas_call(
        body, grid=(rows // rb,), in_specs=[spec] * 4, out_specs=[spec] * 3,
        out_shape=[SDS((rows, cols), f32)] * 3, compiler_params=_CP, name="adamw")(
            *(a.reshape(rows, cols) for a in (w, g, m, v)))
    return tuple(o.reshape(shape) for o in outs)


_BIG = ("ffn1_w_gate", "ffn1_w_up", "ffn1_w_down", "w_in", "w_out", "ffn2_w_gate", "ffn2_w_up", "ffn2_w_down")
_TRANSPOSED = ("ffn1_w_gate", "ffn1_w_up", "w_in", "ffn2_w_gate", "ffn2_w_up")

def _block_diag(pool_w):
    out = jnp.zeros((L, PW, PW), pool_w.dtype)
    for gi in range(4):
        out = out.at[:, 64 * gi:64 * (gi + 1), 64 * gi:64 * (gi + 1)].set(pool_w[:, gi])
    return out


def kernel(x, positions, ffn1_norm, ffn1_w_gate, ffn1_w_up, ffn1_w_down, mix_norm, w_in, pool_w, pool_scale, w_out, ffn2_norm, ffn2_w_gate, ffn2_w_up, ffn2_w_down, final_norm, loss_target, m_ffn1_norm, m_ffn1_w_gate, m_ffn1_w_up, m_ffn1_w_down, m_mix_norm, m_w_in, m_pool_w, m_pool_scale, m_w_out, m_ffn2_norm, m_ffn2_w_gate, m_ffn2_w_up, m_ffn2_w_down, m_final_norm, v_ffn1_norm, v_ffn1_w_gate, v_ffn1_w_up, v_ffn1_w_down, v_mix_norm, v_w_in, v_pool_w, v_pool_scale, v_w_out, v_ffn2_norm, v_ffn2_w_gate, v_ffn2_w_up, v_ffn2_w_down, v_final_norm):
    weights = dict(ffn1_norm=ffn1_norm, ffn1_w_gate=ffn1_w_gate, ffn1_w_up=ffn1_w_up, ffn1_w_down=ffn1_w_down,
                   mix_norm=mix_norm, w_in=w_in, pool_w=pool_w, pool_scale=pool_scale, w_out=w_out,
                   ffn2_norm=ffn2_norm, ffn2_w_gate=ffn2_w_gate, ffn2_w_up=ffn2_w_up, ffn2_w_down=ffn2_w_down,
                   final_norm=final_norm)
    moms = dict(ffn1_norm=m_ffn1_norm, ffn1_w_gate=m_ffn1_w_gate, ffn1_w_up=m_ffn1_w_up, ffn1_w_down=m_ffn1_w_down,
                mix_norm=m_mix_norm, w_in=m_w_in, pool_w=m_pool_w, pool_scale=m_pool_scale, w_out=m_w_out,
                ffn2_norm=m_ffn2_norm, ffn2_w_gate=m_ffn2_w_gate, ffn2_w_up=m_ffn2_w_up, ffn2_w_down=m_ffn2_w_down,
                final_norm=m_final_norm)
    vels = dict(ffn1_norm=v_ffn1_norm, ffn1_w_gate=v_ffn1_w_gate, ffn1_w_up=v_ffn1_w_up, ffn1_w_down=v_ffn1_w_down,
                mix_norm=v_mix_norm, w_in=v_w_in, pool_w=v_pool_w, pool_scale=v_pool_scale, w_out=v_w_out,
                ffn2_norm=v_ffn2_norm, ffn2_w_gate=v_ffn2_w_gate, ffn2_w_up=v_ffn2_w_up, ffn2_w_down=v_ffn2_w_down,
                final_norm=v_final_norm)
    names = list(weights)

    me_idx = 4 * lax.axis_index("x") + 2 * lax.axis_index("y") + lax.axis_index("c")

    tr = lambda w: jnp.swapaxes(w, 1, 2).astype(bf16)
    shards = [tr(weights[nm]) if nm in _TRANSPOSED else weights[nm].astype(bf16) for nm in _BIG]

    def landing_zones(l):
        return [lax.dynamic_update_slice(lax.empty((NDEV * s.shape[1], D), bf16), s[l], (me_idx * s.shape[1], 0))
                for s in shards]

    g_ffn1 = [ffn1_norm[l].reshape(1, D) for l in range(L)]
    g_mix = [mix_norm[l].reshape(1, D) for l in range(L)]
    g_ffn2 = [ffn2_norm[l].reshape(1, D) for l in range(L)]
    wbd_all = _block_diag(pool_w).astype(bf16)
    wbd = [wbd_all[l] for l in range(L)]
    pscale = [pool_scale[l].reshape(1, PW) for l in range(L)]
    tabs = _rope_tables(positions)
    flat = lambda a: a.reshape(S, AW)
    r4 = lambda a: a.reshape(4, S // 4, AW)
    r16 = lambda a: a.reshape(16, S // 16, AW)

    gathered = [None] * L
    gathered[0] = [a.reshape(a.shape[1], D) for a in _all_gather([s[0:1] for s in shards])]
    xs = x.reshape(S, D)
    saved = []
    for l in range(L):
        gt1, ut1, dn1, wint, wout, gt2, ut2, dn2 = gathered[l]
        ga, gb = g_ffn1[l], g_ffn2[l]
        if l + 1 < L:
            send_sems, recv_sems, zones, token = _ag_start(landing_zones(l + 1), l + 1)
            ga = ga + token[0, 0]
        x0 = xs
        x1, gate1, up1 = _ffn_fwd(x0, ga, gt1, ut1, dn1)
        hmix, vp, q1, k1, v1, q4, k4, v4, q16, k16, v16 = _mix_in_fwd(x1, g_mix[l], wint, tabs)
        q4, k4, v4, q16, k16, v16 = map(flat, (q4, k4, v4, q16, k16, v16))
        ypool, diff = _pool_fwd(vp, wbd[l], pscale[l])
        o1, l1 = _attn_fwd(q1, k1, v1, S)
        o4, l4 = _attn_fwd(q4, k4, v4, S // 4)
        o16, l16 = _attn_fwd(q16, k16, v16, S // 16)
        if l + 1 < L:
            psend, precv, zones, token = _ag_pass(zones, recv_sems, o16, l + 1)
            gb = gb + token[0, 0]
        x2, mixed, o, lse1, lse4, lse16 = _mix_out_fwd(x1, ypool, o1, l1, r4(o4), r4(l4), r16(o16), r16(l16), wout)
        x3, gate2, up2 = _ffn_fwd(x2, gb, gt2, ut2, dn2)
        if l + 1 < L:
            gathered[l + 1] = _ag_wait(zones, send_sems, recv_sems, psend, precv, x3, l + 1)
        saved.append(dict(x0=x0, x1=x1, x2=x2, gate1=gate1, up1=up1, gate2=gate2, up2=up2, hmix=hmix, diff=diff,
                          qkv=((q1, k1, v1), (q4, k4, v4), (q16, k16, v16)), mixed=mixed, o=o,
                          lse=(lse1, flat(lse4), flat(lse16))))
        xs = x3

    dx, loss_part, d_final = _loss_head(xs, final_norm.reshape(1, D), loss_target.reshape(S, D))

    d_norm = {nm: [None] * L for nm in ("ffn1_norm", "mix_norm", "ffn2_norm")}
    d_poolw, d_pscale = [None] * L, [None] * L
    slot_sums = [None] * L
    in_flight = None
    for l in reversed(range(L)):
        sv = saved[l]
        gt1, ut1, dn1, wint, wout, gt2, ut2, dn2 = gathered[l]
        gb = g_ffn2[l] if in_flight is None else g_ffn2[l] + in_flight[-1][0, 0]
        full = {}
        dx, dgate, dup, h, dy, d_norm["ffn2_norm"][l] = _ffn_bwd_d(sv["x2"], gb, sv["gate2"], sv["up2"], dx, gt2, ut2, dn2)
        full["ffn2_w_gate"], full["ffn2_w_up"], full["ffn2_w_down"] = _ffn_bwd_w(h, dy, sv["gate2"], sv["up2"], dgate, dup)

        dxb, dyp, do1, do4, do16, dl1, dl4, dl16 = _mix_out_bwd(dx, sv["o"], wout)
        full["w_out"] = _wgrad(sv["mixed"], dxb)
        dvp, dwbd, d_pscale[l] = _pool_bwd(dyp, sv["diff"], wbd[l], pscale[l])
        d_poolw[l] = jnp.stack([dwbd[64 * gi:64 * (gi + 1), 64 * gi:64 * (gi + 1)] for gi in range(4)])
        dos, dls = (do1, flat(do4), flat(do16)), (dl1, flat(dl4), flat(dl16))
        dqkv = []
        for b, lc in enumerate((S, S // 4, S // 16)):
            qb, kb, vb = sv["qkv"][b]
            dqkv.append(_attn_bwd(qb, kb, vb, dos[b], sv["lse"][b], dls[b], lc))
        d4 = tuple(r4(a) for a in dqkv[1])
        d16 = tuple(r16(a) for a in dqkv[2])
        dx, dproj, d_norm["mix_norm"][l] = _mix_in_bwd(dx, sv["x1"], g_mix[l], wint, tabs, dvp, dqkv[0], d4, d16)
        full["w_in"] = _wgrad(dproj, sv["hmix"])

        dx, dgate, dup, h, dy, d_norm["ffn1_norm"][l] = _ffn_bwd_d(sv["x0"], g_ffn1[l], sv["gate1"], sv["up1"], dx, gt1, ut1, dn1)
        full["ffn1_w_gate"], full["ffn1_w_up"], full["ffn1_w_down"] = _ffn_bwd_w(h, dy, sv["gate1"], sv["up1"], dgate, dup)

        if in_flight is not None:
            lp, grads_p, slots_p, ssem, rsem, _ = in_flight
            slot_sums[lp] = _sum_slots(_rs_wait(grads_p, slots_p, ssem, rsem, dx, lp), SLOT_ROWS // NDEV)
        grads_l = [full[nm] for nm in _BIG]
        own = jnp.concatenate([lax.dynamic_slice(g, (me_idx * r, 0), (r, D)) for g, r in zip(grads_l, SHARD_ROWS)], axis=0)
        slots = lax.dynamic_update_slice(lax.empty((NDEV, SLOT_ROWS, D), bf16), own[None], (me_idx, 0, 0))
        ssem, rsem, grads_l, slots, token = _rs_start(grads_l, slots, l)
        in_flight = (l, grads_l, slots, ssem, rsem, token)

    lp, grads_p, slots_p, ssem, rsem, _ = in_flight
    slot_sums[lp] = _sum_slots(_rs_wait(grads_p, slots_p, ssem, rsem, dx, lp), SLOT_ROWS // NDEV)

    pad8 = lambda a: jnp.pad(a, ((0, 8 - a.shape[0]), (0, 0)))
    misc = jnp.concatenate([d_final, jnp.concatenate(d_pscale, axis=1), loss_part], axis=0)
    small = jnp.concatenate(
        [pad8(jnp.concatenate(d_norm[nm], axis=0)) for nm in ("ffn1_norm", "mix_norm", "ffn2_norm")]
        + [pad8(misc), jnp.stack(d_poolw).reshape(L * 16, D)], axis=0)
    sm = _sum_slots(_gather_small(small), SMALL_ROWS)

    grads = {}
    off = 0
    for nm, r in zip(_BIG, SHARD_ROWS):
        g = jnp.stack([slot_sums[l][off:off + r] for l in range(L)])
        grads[nm] = jnp.swapaxes(g, 1, 2) if nm in _TRANSPOSED else g
        off += r
    grads["ffn1_norm"], grads["mix_norm"], grads["ffn2_norm"] = sm[0:L], sm[8:8 + L], sm[16:16 + L]
    grads["final_norm"] = sm[24]
    grads["pool_scale"] = sm[25].reshape(L, PW)
    grads["pool_w"] = sm[32:32 + L * 16].reshape(L, 4, 64, 64)
    loss = sm[26, 0]

    upd = {nm: _adamw(weights[nm], grads[nm], moms[nm], vels[nm]) for nm in names}
    return (loss, dx.reshape(1, S, D), *[grads[nm] for nm in names], *[upd[nm][0] for nm in names],
            *[upd[nm][1] for nm in names], *[upd[nm][2] for nm in names])
```

```python
import jax
import jax.numpy as jnp
from jax import lax
from jax.experimental import pallas as pl
from jax.experimental.pallas import tpu as pltpu

f32 = jnp.float32
bf16 = jnp.bfloat16
SDS = jax.ShapeDtypeStruct

D = 1024
S = 2048
F = 2816
L = 4
PW = 256
AW = 768
PROJ = PW + 3 * AW
NDEV = 8
SHARD_ROWS = (F // NDEV,) * 3 + (PROJ // NDEV, D // NDEV) + (F // NDEV,) * 3
SLOT_ROWS = sum(SHARD_ROWS)
TM = 256
QB = 128
HALF = 64
NG = AW // 128
NORM_EPS = 1e-6
MASK_VALUE = -1e30
ROPE_THETA = 500000.0
ADAM_LR, ADAM_B1, ADAM_B2, ADAM_EPS, ADAM_WD, ADAM_STEP = 0.001, 0.9, 0.999, 1e-08, 0.01, 10
POOL_WINDOWS = (2, 4, 8, 16)
PAD = 8
SMALL_ROWS = 96
VMEM_LIMIT = 56 * 1024 * 1024

_CP = pltpu.CompilerParams(vmem_limit_bytes=VMEM_LIMIT)
_ANY = pl.BlockSpec(memory_space=pl.ANY)
_HBM = pl.BlockSpec(memory_space=pltpu.HBM)
_SEM = pl.BlockSpec(memory_space=pltpu.SEMAPHORE)
_MESH = pl.DeviceIdType.MESH
_CP_SPLIT = pltpu.CompilerParams(has_side_effects=pltpu.SideEffectType.DATAFLOW_SIDE_EFFECTING)


def _dot_nn(a, b):
    return lax.dot_general(a, b, (((1,), (0,)), ((), ())), preferred_element_type=f32)


def _dot_nt(a, b):
    return lax.dot_general(a, b, (((1,), (1,)), ((), ())), preferred_element_type=f32)


def _dot_tn(a, b):
    return lax.dot_general(a, b, (((0,), (0,)), ((), ())), preferred_element_type=f32)


def _rms(x, g):
    r = lax.rsqrt(jnp.mean(x * x, axis=-1, keepdims=True) + NORM_EPS)
    xh = x * r
    return r, xh, xh * g


def _rms_bwd(dh, r, xh, g):
    dxh = dh * g
    return r * (dxh - xh * jnp.mean(dxh * xh, axis=-1, keepdims=True))


def _tile(cols):
    return pl.BlockSpec((TM, cols), lambda i: (i, 0))


def _const(shape):
    return pl.BlockSpec(shape, lambda i: (0,) * len(shape))


def _layer(rows, cols, l=None):
    return pl.BlockSpec((rows, cols), lambda i: (0, 0), pipeline_mode=pl.Buffered(1))


def _p4():
    return pl.BlockSpec((4, TM // 4, AW), lambda i: (0, i, 0))


def _p16():
    return pl.BlockSpec((16, TM // 16, AW), lambda i: (0, i, 0))


def _cols(j):
    return slice(128 * j, 128 * (j + 1))


def _ffn_fwd(x, g, gt, ut, dn, l=None):
    def body(x_ref, g_ref, gt_ref, ut_ref, dn_ref, xo_ref, gate_ref, up_ref):
        x = x_ref[...]
        _, _, hn = _rms(x, g_ref[...])
        h = hn.astype(bf16)
        gate = _dot_nt(h, gt_ref[...])
        up = _dot_nt(h, ut_ref[...])
        gate_ref[...] = gate.astype(bf16)
        up_ref[...] = up.astype(bf16)
        a = (gate * jax.nn.sigmoid(gate) * up).astype(bf16)
        xo_ref[...] = x + 0.5 * _dot_nn(a, dn_ref[...])

    return pl.pallas_call(
        body, grid=(S // TM,),
        in_specs=[_tile(D), _layer(1, D, l), _layer(F, D, l), _layer(F, D, l), _layer(F, D, l)],
        out_specs=[_tile(D), _tile(F), _tile(F)],
        out_shape=[SDS((S, D), f32), SDS((S, F), bf16), SDS((S, F), bf16)],
        compiler_params=_CP, name="ffn_fwd")(x, g, gt, ut, dn)


def _ffn_bwd_d(x, g, gate, up, dxo, gt, ut, dn, l=None):
    def body(x_ref, g_ref, gate_ref, up_ref, dxo_ref, gt_ref, ut_ref, dn_ref,
             dx_ref, dgate_ref, dup_ref, h_ref, dy_ref, dg_ref):
        x = x_ref[...]
        g = g_ref[...]
        r, xh, hn = _rms(x, g)
        h_ref[...] = hn.astype(bf16)
        dxo = dxo_ref[...]
        dy = (0.5 * dxo).astype(bf16)
        dy_ref[...] = dy
        da = _dot_nt(dy, dn_ref[...])
        gate = gate_ref[...].astype(f32)
        up = up_ref[...].astype(f32)
        sg = jax.nn.sigmoid(gate)
        dgate = (da * up * (sg * (1.0 + gate * (1.0 - sg)))).astype(bf16)
        dup = (da * (gate * sg)).astype(bf16)
        dgate_ref[...] = dgate
        dup_ref[...] = dup
        dh = _dot_nn(dgate, gt_ref[...]) + _dot_nn(dup, ut_ref[...])

        @pl.when(pl.program_id(0) == 0)
        def _():
            dg_ref[...] = jnp.zeros_like(dg_ref)

        dg_ref[...] += jnp.sum(dh * xh, axis=0, keepdims=True)
        dx_ref[...] = dxo + _rms_bwd(dh, r, xh, g)

    return pl.pallas_call(
        body, grid=(S // TM,),
        in_specs=[_tile(D), _layer(1, D, l), _tile(F), _tile(F), _tile(D),
                  _layer(F, D, l), _layer(F, D, l), _layer(F, D, l)],
        out_specs=[_tile(D), _tile(F), _tile(F), _tile(D), _tile(D), _const((1, D))],
        out_shape=[SDS((S, D), f32), SDS((S, F), bf16), SDS((S, F), bf16), SDS((S, D), bf16),
                   SDS((S, D), bf16), SDS((1, D), f32)],
        compiler_params=_CP, name="ffn_bwd_d")(x, g, gate, up, dxo, gt, ut, dn)


def _ffn_bwd_w(h, dy, gate, up, dgate, dup):
    fc = 256

    def body(h_ref, dy_ref, gate_ref, up_ref, dgate_ref, dup_ref, dgt_ref, dut_ref, ddn_ref):
        gate = gate_ref[...].astype(f32)
        a = (gate * jax.nn.sigmoid(gate) * up_ref[...].astype(f32)).astype(bf16)
        ddn_ref[...] = _dot_tn(a, dy_ref[...]).astype(bf16)
        h = h_ref[...]
        dgt_ref[...] = _dot_tn(dgate_ref[...], h).astype(bf16)
        dut_ref[...] = _dot_tn(dup_ref[...], h).astype(bf16)

    col = pl.BlockSpec((S, fc), lambda j: (0, j))
    row = pl.BlockSpec((fc, D), lambda j: (j, 0))
    full = pl.BlockSpec((S, D), lambda j: (0, 0))
    return pl.pallas_call(
        body, grid=(F // fc,),
        in_specs=[full, full, col, col, col, col],
        out_specs=[row, row, row],
        out_shape=[SDS((F, D), bf16)] * 3,
        compiler_params=_CP, name="ffn_bwd_w")(h, dy, gate, up, dgate, dup)


def _wgrad(a, b):
    m, n = a.shape[1], b.shape[1]
    mc = 256

    def body(a_ref, b_ref, o_ref):
        o_ref[...] = _dot_tn(a_ref[...], b_ref[...]).astype(bf16)

    return pl.pallas_call(
        body, grid=(m // mc,),
        in_specs=[pl.BlockSpec((S, mc), lambda j: (0, j)), pl.BlockSpec((S, n), lambda j: (0, 0))],
        out_specs=pl.BlockSpec((mc, n), lambda j: (j, 0)),
        out_shape=SDS((m, n), bf16),
        compiler_params=_CP, name="wgrad")(a, b)


def _rope(t, c, sn, sp):
    return t * c + pltpu.roll(t, 120, 1) * sn + pltpu.roll(t, 8, 1) * sp


def _rope_bwd(d, c, sn, sp):
    return d * c + pltpu.roll(d * sn, 8, 1) + pltpu.roll(d * sp, 120, 1)


def _rope_tables(positions):
    inv_freq = ROPE_THETA ** (-jnp.arange(0, 16, 2, dtype=f32) / 16)
    ang = positions.reshape(S, 1).astype(f32) * inv_freq
    cos, sin = jnp.cos(ang), jnp.sin(ang)
    one = jnp.ones((S, 48), f32)
    zero8 = jnp.zeros((S, 8), f32)
    zero48 = jnp.zeros((S, 48), f32)
    c = jnp.concatenate([cos, cos, one], axis=1)
    sn = jnp.concatenate([-sin, zero8, zero48], axis=1)
    sp = jnp.concatenate([zero8, sin, zero48], axis=1)
    return tuple(jnp.concatenate([t, t], axis=1) for t in (c, sn, sp))


def _mix_in_fwd(x, g, wint, tabs, l=None):
    def body(x_ref, g_ref, w_ref, c_ref, sn_ref, sp_ref,
             h_ref, vp_ref, q1, k1, v1, q4, k4, v4, q16, k16, v16, scr):
        _, _, hn = _rms(x_ref[...], g_ref[...])
        h = hn.astype(bf16)
        h_ref[...] = h
        proj = _dot_nt(h, w_ref[...])
        vp_ref[...] = proj[:, :PW]
        c, sn, sp = c_ref[...], sn_ref[...], sp_ref[...]
        for kind, (o1, o4, o16) in enumerate(((q1, q4, q16), (k1, k4, k16), (v1, v4, v16))):
            for j in range(NG):
                t = proj[:, PW + kind * AW + 128 * j: PW + kind * AW + 128 * (j + 1)]
                if kind == 0:
                    t = _rope(t, c, sn, sp) * 0.125
                elif kind == 1:
                    t = _rope(t, c, sn, sp)
                scr[j] = t
                o1[:, _cols(j)] = t.astype(bf16)
            for r in range(4):
                for j in range(NG):
                    o4[r, :, _cols(j)] = scr[j, pl.ds(r, TM // 4, stride=4), :].astype(bf16)
            for r in range(16):
                for j in range(NG):
                    o16[r, :, _cols(j)] = scr[j, pl.ds(r, TM // 16, stride=16), :].astype(bf16)

    nat, d4, d16 = SDS((S, AW), bf16), SDS((4, S // 4, AW), bf16), SDS((16, S // 16, AW), bf16)
    return pl.pallas_call(
        body, grid=(S // TM,),
        in_specs=[_tile(D), _layer(1, D, l), _layer(PROJ, D, l), _tile(128), _tile(128), _tile(128)],
        out_specs=[_tile(D), _tile(PW)] + [_tile(AW)] * 3 + [_p4()] * 3 + [_p16()] * 3,
        out_shape=[SDS((S, D), bf16), SDS((S, PW), f32)] + [nat] * 3 + [d4] * 3 + [d16] * 3,
        scratch_shapes=[pltpu.VMEM((NG, TM, 128), f32)],
        compiler_params=_CP, name="mix_in_fwd")(x, g, wint, *tabs)


def _mix_in_bwd(dxo, x, g, wint, tabs, dvp, d1, d4, d16, l=None):
    def body(dxo_ref, x_ref, g_ref, w_ref, c_ref, sn_ref, sp_ref, dvp_ref,
             dq1, dk1, dv1, dq4, dk4, dv4, dq16, dk16, dv16,
             dx_ref, dproj_ref, dg_ref, s4, s16):
        c, sn, sp = c_ref[...], sn_ref[...], sp_ref[...]
        dproj_ref[:, :PW] = dvp_ref[...].astype(bf16)
        for kind, (a1, a4, a16) in enumerate(((dq1, dq4, dq16), (dk1, dk4, dk16), (dv1, dv4, dv16))):
            for r in range(4):
                for j in range(NG):
                    s4[j, pl.ds(r, TM // 4, stride=4), :] = a4[r, :, _cols(j)]
            for r in range(16):
                for j in range(NG):
                    s16[j, pl.ds(r, TM // 16, stride=16), :] = a16[r, :, _cols(j)]
            for j in range(NG):
                t = a1[:, _cols(j)] + s4[j] + s16[j]
                if kind == 0:
                    t = _rope_bwd(t * 0.125, c, sn, sp)
                elif kind == 1:
                    t = _rope_bwd(t, c, sn, sp)
                dproj_ref[:, PW + kind * AW + 128 * j: PW + kind * AW + 128 * (j + 1)] = t.astype(bf16)
        g = g_ref[...]
        r_, xh, _ = _rms(x_ref[...], g)
        dh = _dot_nn(dproj_ref[...], w_ref[...])

        @pl.when(pl.program_id(0) == 0)
        def _():
            dg_ref[...] = jnp.zeros_like(dg_ref)

        dg_ref[...] += jnp.sum(dh * xh, axis=0, keepdims=True)
        dx_ref[...] = dxo_ref[...] + _rms_bwd(dh, r_, xh, g)

    return pl.pallas_call(
        body, grid=(S // TM,),
        in_specs=[_tile(D), _tile(D), _layer(1, D, l), _layer(PROJ, D, l), _tile(128), _tile(128), _tile(128),
                  _tile(PW)] + [_tile(AW)] * 3 + [_p4()] * 3 + [_p16()] * 3,
        out_specs=[_tile(D), _tile(PROJ), _const((1, D))],
        out_shape=[SDS((S, D), f32), SDS((S, PROJ), bf16), SDS((1, D), f32)],
        scratch_shapes=[pltpu.VMEM((NG, TM, 128), f32), pltpu.VMEM((NG, TM, 128), f32)],
        compiler_params=_CP, name="mix_in_bwd")(dxo, x, g, wint, *tabs, dvp, *d1, *d4, *d16)


def _pool_sums(pad_ref, base, rows, adjoint):
    lane_group = lax.broadcasted_iota(jnp.int32, (rows, PW), 1) // 64
    sign = -1 if adjoint else 1

    def sh(o):
        return pad_ref[pl.ds(PAD + base + sign * o, rows), :]

    out = None
    acc = None
    lo, hi = 0, 0
    for gi, w in enumerate(POOL_WINDOWS):
        for o in list(range(-(w // 2), lo)) + list(range(hi, w - w // 2)):
            acc = sh(o) if acc is None else acc + sh(o)
        lo, hi = -(w // 2), w - w // 2
        out = acc if out is None else jnp.where(lane_group >= gi, acc, out)
    return out


def _pool_counts(base, rows):
    pos = base + lax.broadcasted_iota(jnp.int32, (rows, PW), 0)
    lane_group = lax.broadcasted_iota(jnp.int32, (rows, PW), 1) // 64
    cnt = None
    for gi, w in enumerate(POOL_WINDOWS):
        lo = jnp.maximum(pos - w // 2, 0)
        hi = jnp.minimum(pos + w - 1 - w // 2, S - 1)
        c = (hi - lo + 1).astype(f32)
        cnt = c if cnt is None else jnp.where(lane_group >= gi, c, cnt)
    return cnt


def _pool_fwd(vp, wbd, scale, l=None):
    ch = 256

    def body(vp_ref, w_ref, sc_ref, y_ref, diff_ref, pad):
        pad[pl.ds(0, PAD), :] = jnp.zeros((PAD, PW), f32)
        pad[pl.ds(PAD + S, PAD), :] = jnp.zeros((PAD, PW), f32)
        pad[pl.ds(PAD, S), :] = vp_ref[...]
        for b in range(S // ch):
            base = b * ch
            pooled = _pool_sums(pad, base, ch, False) / _pool_counts(base, ch)
            diff = (pooled - vp_ref[pl.ds(base, ch), :]).astype(bf16)
            diff_ref[pl.ds(base, ch), :] = diff
            y_ref[pl.ds(base, ch), :] = _dot_nn(diff, w_ref[...]) * sc_ref[...]

    whole = lambda shape: pl.BlockSpec(shape, lambda i: (0,) * len(shape))
    return pl.pallas_call(
        body, grid=(1,),
        in_specs=[whole((S, PW)), whole((PW, PW)), whole((1, PW))],
        out_specs=[whole((S, PW)), whole((S, PW))],
        out_shape=[SDS((S, PW), f32), SDS((S, PW), bf16)],
        scratch_shapes=[pltpu.VMEM((S + 2 * PAD, PW), f32)],
        compiler_params=_CP, name="pool_fwd")(vp, wbd, scale)


def _pool_bwd(dy, diff, wbd, scale, l=None):
    ch = 256

    def body(dy_ref, diff_ref, w_ref, sc_ref, dvp_ref, dw_ref, dsc_ref, pad):
        pad[pl.ds(0, PAD), :] = jnp.zeros((PAD, PW), f32)
        pad[pl.ds(PAD + S, PAD), :] = jnp.zeros((PAD, PW), f32)
        dw = jnp.zeros((PW, PW), f32)
        dsc = jnp.zeros((1, PW), f32)
        for b in range(S // ch):
            base = b * ch
            dy = dy_ref[pl.ds(base, ch), :]
            diff = diff_ref[pl.ds(base, ch), :]
            dsc = dsc + jnp.sum(dy * _dot_nn(diff, w_ref[...]), axis=0, keepdims=True)
            dz = (dy * sc_ref[...]).astype(bf16)
            dw = dw + _dot_tn(diff, dz)
            ddiff = _dot_nt(dz, w_ref[...])
            dvp_ref[pl.ds(base, ch), :] = -ddiff
            pad[pl.ds(PAD + base, ch), :] = ddiff / _pool_counts(base, ch)
        dw_ref[...] = dw
        dsc_ref[...] = dsc
        for b in range(S // ch):
            base = b * ch
            dvp_ref[pl.ds(base, ch), :] += _pool_sums(pad, base, ch, True)

    whole = lambda shape: pl.BlockSpec(shape, lambda i: (0,) * len(shape))
    return pl.pallas_call(
        body, grid=(1,),
        in_specs=[whole((S, PW)), whole((S, PW)), whole((PW, PW)), whole((1, PW))],
        out_specs=[whole((S, PW)), whole((PW, PW)), whole((1, PW))],
        out_shape=[SDS((S, PW), f32), SDS((PW, PW), f32), SDS((1, PW), f32)],
        scratch_shapes=[pltpu.VMEM((S + 2 * PAD, PW), f32)],
        compiler_params=_CP, name="pool_bwd")(dy, diff, wbd, scale)


def _attn_block_index(b, lc):
    bpc = lc // QB
    kw = min(2 * QB, lc)
    row0 = pl.multiple_of(b * QB, QB)
    t0 = (b % bpc) * QB
    ks_in = jnp.clip(t0 - HALF, 0, lc - kw)
    kstart = pl.multiple_of((b // bpc) * lc + ks_in, HALF)
    qpos = t0 + lax.broadcasted_iota(jnp.int32, (QB, kw), 0)
    kpos = ks_in + lax.broadcasted_iota(jnp.int32, (QB, kw), 1)
    valid = jnp.abs(qpos - kpos) <= HALF
    return kw, row0, kstart, valid


def _attn_fwd(q, k, v, lc):
    def body(q_ref, k_ref, v_ref, o_ref, lse_ref):
        head0 = lax.broadcasted_iota(jnp.int32, (QB, 128), 1) < 64

        def blk(b, carry):
            kw, row0, kstart, valid = _attn_block_index(b, lc)
            qb = q_ref[pl.ds(row0, QB), :]
            kb = k_ref[pl.ds(kstart, kw), :]
            vb = v_ref[pl.ds(kstart, kw), :]
            outs, lses = [], []
            for hh in range(2):
                lm = head0 if hh == 0 else jnp.logical_not(head0)
                qh = jnp.where(lm, qb, jnp.zeros_like(qb))
                s = jnp.where(valid, _dot_nt(qh, kb), MASK_VALUE)
                m = jnp.max(s, axis=-1, keepdims=True)
                p = jnp.exp(s - m)
                den = jnp.sum(p, axis=-1, keepdims=True)
                outs.append(_dot_nn(p.astype(bf16), vb) / den)
                lses.append(jnp.broadcast_to(m + jnp.log(den), (QB, 128)))
            o_ref[pl.ds(row0, QB), :] = jnp.where(head0, outs[0], outs[1])
            lse_ref[pl.ds(row0, QB), :] = jnp.where(head0, lses[0], lses[1])
            return carry

        lax.fori_loop(0, S // QB, blk, 0)

    col = pl.BlockSpec((S, 128), lambda p: (0, p))
    return pl.pallas_call(
        body, grid=(NG,), in_specs=[col, col, col], out_specs=[col, col],
        out_shape=[SDS((S, AW), f32), SDS((S, AW), f32)],
        compiler_params=_CP, name=f"attn_fwd_{lc}")(q, k, v)


def _attn_bwd(q, k, v, do, lse, delta, lc):
    def body(q_ref, k_ref, v_ref, do_ref, lse_ref, dl_ref, dq_ref, dk_ref, dv_ref):
        head0 = lax.broadcasted_iota(jnp.int32, (QB, 128), 1) < 64
        dk_ref[...] = jnp.zeros_like(dk_ref)
        dv_ref[...] = jnp.zeros_like(dv_ref)

        def blk(b, carry):
            kw, row0, kstart, valid = _attn_block_index(b, lc)
            khead0 = lax.broadcasted_iota(jnp.int32, (kw, 128), 1) < 64
            qb = q_ref[pl.ds(row0, QB), :]
            dob = do_ref[pl.ds(row0, QB), :]
            lse = lse_ref[pl.ds(row0, QB), :]
            dl = dl_ref[pl.ds(row0, QB), :]
            kb = k_ref[pl.ds(kstart, kw), :]
            vb = v_ref[pl.ds(kstart, kw), :]
            dqs, dks, dvs = [], [], []
            for hh in range(2):
                lm = head0 if hh == 0 else jnp.logical_not(head0)
                c0 = 64 * hh
                qh = jnp.where(lm, qb, jnp.zeros_like(qb))
                doh = jnp.where(lm, dob, jnp.zeros_like(dob))
                s = jnp.where(valid, _dot_nt(qh, kb), MASK_VALUE)
                p = jnp.exp(s - lse[:, c0:c0 + 1])
                dp = _dot_nt(doh, vb)
                ds = (p * (dp - dl[:, c0:c0 + 1])).astype(bf16)
                dqs.append(_dot_nn(ds, kb))
                dks.append(_dot_tn(ds, qb))
                dvs.append(_dot_tn(p.astype(bf16), dob))
            dq_ref[pl.ds(row0, QB), :] = jnp.where(head0, dqs[0], dqs[1])
            dk_ref[pl.ds(kstart, kw), :] += jnp.where(khead0, dks[0], dks[1])
            dv_ref[pl.ds(kstart, kw), :] += jnp.where(khead0, dvs[0], dvs[1])
            return carry

        lax.fori_loop(0, S // QB, blk, 0)

    col = pl.BlockSpec((S, 128), lambda p: (0, p))
    return pl.pallas_call(
        body, grid=(NG,), in_specs=[col] * 6, out_specs=[col] * 3,
        out_shape=[SDS((S, AW), f32)] * 3,
        compiler_params=_CP, name=f"attn_bwd_{lc}")(q, k, v, do, lse, delta)


def _mix_out_fwd(x, ypool, o1, l1, o4, l4, o16, l16, wout, l=None):
    def body(x_ref, yp_ref, o1_ref, l1_ref, o4_ref, l4_ref, o16_ref, l16_ref, w_ref,
             xo_ref, mixed_ref, o_ref, lse1_ref, lse4_ref, lse16_ref, so4, sl4, so16, sl16, sl):
        for r in range(4):
            for j in range(NG):
                so4[j, pl.ds(r, TM // 4, stride=4), :] = o4_ref[r, :, _cols(j)]
                sl4[j, pl.ds(r, TM // 4, stride=4), :] = l4_ref[r, :, _cols(j)]
        for r in range(16):
            for j in range(NG):
                so16[j, pl.ds(r, TM // 16, stride=16), :] = o16_ref[r, :, _cols(j)]
                sl16[j, pl.ds(r, TM // 16, stride=16), :] = l16_ref[r, :, _cols(j)]
        mixed_ref[:, :PW] = yp_ref[...].astype(bf16)
        for j in range(NG):
            a, b, c = l1_ref[:, _cols(j)], sl4[j], sl16[j]
            m = jnp.maximum(jnp.maximum(a, b), c)
            wa, wb, wc = jnp.exp(a - m), jnp.exp(b - m), jnp.exp(c - m)
            den = wa + wb + wc
            y = (wa * o1_ref[:, _cols(j)] + wb * so4[j] + wc * so16[j]) / den
            lse = m + jnp.log(den)
            o_ref[:, _cols(j)] = y
            lse1_ref[:, _cols(j)] = lse
            sl[j] = lse
            mixed_ref[:, PW + 128 * j: PW + 128 * (j + 1)] = y.astype(bf16)
        for r in range(4):
            for j in range(NG):
                lse4_ref[r, :, _cols(j)] = sl[j, pl.ds(r, TM // 4, stride=4), :]
        for r in range(16):
            for j in range(NG):
                lse16_ref[r, :, _cols(j)] = sl[j, pl.ds(r, TM // 16, stride=16), :]
        xo_ref[...] = x_ref[...] + _dot_nn(mixed_ref[...], w_ref[...])

    scr = pltpu.VMEM((NG, TM, 128), f32)
    return pl.pallas_call(
        body, grid=(S // TM,),
        in_specs=[_tile(D), _tile(PW), _tile(AW), _tile(AW), _p4(), _p4(), _p16(), _p16(), _layer(D, D, l)],
        out_specs=[_tile(D), _tile(D), _tile(AW), _tile(AW), _p4(), _p16()],
        out_shape=[SDS((S, D), f32), SDS((S, D), bf16), SDS((S, AW), f32), SDS((S, AW), f32),
                   SDS((4, S // 4, AW), f32), SDS((16, S // 16, AW), f32)],
        scratch_shapes=[scr] * 5,
        compiler_params=_CP, name="mix_out_fwd")(x, ypool, o1, l1, o4, l4, o16, l16, wout)


def _segsum64(t):
    lane = lax.broadcasted_iota(jnp.int32, t.shape, 1)
    for s in (1, 2, 4, 8, 16, 32):
        t = t + jnp.where((lane & s) != 0, pltpu.roll(t, s, 1), pltpu.roll(t, 128 - s, 1))
    return t


def _mix_out_bwd(dxo, o, wout, l=None):
    def body(dxo_ref, o_ref, w_ref, dxb_ref, dyp_ref, do1, do4, do16, dl1, dl4, dl16, sdo, sdl):
        dxb = dxo_ref[...].astype(bf16)
        dxb_ref[...] = dxb
        dm = _dot_nt(dxb, w_ref[...])
        dyp_ref[...] = dm[:, :PW]
        for j in range(NG):
            d = dm[:, PW + 128 * j: PW + 128 * (j + 1)]
            dl = _segsum64(d * o_ref[:, _cols(j)])
            do1[:, _cols(j)] = d.astype(bf16)
            dl1[:, _cols(j)] = dl
            sdo[j] = d
            sdl[j] = dl
        for r in range(4):
            for j in range(NG):
                do4[r, :, _cols(j)] = sdo[j, pl.ds(r, TM // 4, stride=4), :].astype(bf16)
                dl4[r, :, _cols(j)] = sdl[j, pl.ds(r, TM // 4, stride=4), :]
        for r in range(16):
            for j in range(NG):
                do16[r, :, _cols(j)] = sdo[j, pl.ds(r, TM // 16, stride=16), :].astype(bf16)
                dl16[r, :, _cols(j)] = sdl[j, pl.ds(r, TM // 16, stride=16), :]

    scr = pltpu.VMEM((NG, TM, 128), f32)
    return pl.pallas_call(
        body, grid=(S // TM,),
        in_specs=[_tile(D), _tile(AW), _layer(D, D, l)],
        out_specs=[_tile(D), _tile(PW), _tile(AW), _p4(), _p16(), _tile(AW), _p4(), _p16()],
        out_shape=[SDS((S, D), bf16), SDS((S, PW), f32),
                   SDS((S, AW), bf16), SDS((4, S // 4, AW), bf16), SDS((16, S // 16, AW), bf16),
                   SDS((S, AW), f32), SDS((4, S // 4, AW), f32), SDS((16, S // 16, AW), f32)],
        scratch_shapes=[scr] * 2,
        compiler_params=_CP, name="mix_out_bwd")(dxo, o, wout)


def _loss_head(x, g, target):
    def body(x_ref, g_ref, t_ref, dx_ref, loss_ref, dg_ref):
        g = g_ref[...]
        r, xh, y = _rms(x_ref[...], g)
        err = y - t_ref[...]
        dy = err * (1.0 / D)

        @pl.when(pl.program_id(0) == 0)
        def _():
            loss_ref[...] = jnp.zeros_like(loss_ref)
            dg_ref[...] = jnp.zeros_like(dg_ref)

        loss_ref[...] += jnp.broadcast_to(0.5 * jnp.sum(jnp.mean(err * err, axis=-1, keepdims=True)), (1, D))
        dg_ref[...] += jnp.sum(dy * xh, axis=0, keepdims=True)
        dx_ref[...] = _rms_bwd(dy, r, xh, g)

    return pl.pallas_call(
        body, grid=(S // TM,),
        in_specs=[_tile(D), _const((1, D)), _tile(D)],
        out_specs=[_tile(D), _const((1, D)), _const((1, D))],
        out_shape=[SDS((S, D), f32), SDS((1, D), f32), SDS((1, D), f32)],
        compiler_params=_CP, name="loss_head")(x, g, target)


def _peer(k):
    x, y, c = lax.axis_index("x"), lax.axis_index("y"), lax.axis_index("c")
    px = 1 - x if k & 4 else x
    py = 1 - y if k & 2 else y
    pc = 1 - c if k & 1 else c
    return (px, py, pc), 4 * px + 2 * py + pc


def _all_gather(shards):
    n = len(shards)

    def body(*refs):
        ins, outs = refs[:n], refs[n:2 * n]
        send_sems, recv_sems, local_sems = refs[2 * n:]
        me, me_idx = _peer(0)
        sibling, sib_idx = _peer(1)
        far = [_peer(k) for k in (4, 2, 6)]
        far_sib = [_peer(k) for k in (5, 3, 7)]

        def rows(t, idx):
            r = ins[t].shape[1]
            return outs[t].at[:, pl.ds(idx * r, r), :]

        def copy(k, t, idx, to, src=None):
            return pltpu.make_async_remote_copy(
                src_ref=rows(t, idx) if src is None else src, dst_ref=rows(t, idx),
                send_sem=send_sems.at[k, t], recv_sem=recv_sems.at[k, t], device_id=to, device_id_type=_MESH)

        mine = [pltpu.make_async_copy(ins[t], rows(t, me_idx), local_sems.at[t]) for t in range(n)]
        for cp in mine:
            cp.start()
        first = [copy(0, t, me_idx, sibling, src=ins[t]) for t in range(n)]
        for j, (dev, _) in enumerate(far):
            first += [copy(1 + j, t, me_idx, dev, src=ins[t]) for t in range(n)]
        for cp in first:
            cp.start()
        passed = []
        for j, (_, idx) in enumerate(far):
            for t in range(n):
                copy(1 + j, t, idx, me).wait_recv()
                cp = copy(4 + j, t, idx, sibling)
                cp.start()
                passed.append(cp)
        for t in range(n):
            copy(0, t, sib_idx, me).wait_recv()
        for j, (_, idx) in enumerate(far_sib):
            for t in range(n):
                copy(4 + j, t, idx, me).wait_recv()
        for cp in first + passed:
            cp.wait_send()
        for cp in mine:
            cp.wait()

    return pl.pallas_call(
        body, in_specs=[_ANY] * n, out_specs=[_ANY] * n,
        out_shape=[SDS((a.shape[0], NDEV * a.shape[1], a.shape[2]), a.dtype) for a in shards],
        scratch_shapes=[pltpu.SemaphoreType.DMA((7, n)), pltpu.SemaphoreType.DMA((7, n)),
                        pltpu.SemaphoreType.DMA((n,))],
        name="all_gather_weights")(*shards)


def _gather_small(small):
    def body(small_in, small_out, send_sems, recv_sems, local_sem):
        _, me_idx = _peer(0)

        def copy(k, to_dev, from_idx):
            return pltpu.make_async_remote_copy(
                src_ref=small_in, dst_ref=small_out.at[from_idx], send_sem=send_sems.at[k - 1],
                recv_sem=recv_sems.at[k - 1], device_id=to_dev, device_id_type=_MESH)

        mine = pltpu.make_async_copy(small_in, small_out.at[me_idx], local_sem)
        mine.start()
        sends = [copy(k, _peer(k)[0], me_idx) for k in range(1, NDEV)]
        for cp in sends:
            cp.start()
        for k in range(1, NDEV):
            dev, idx = _peer(k)
            copy(k, dev, idx).wait_recv()
        for cp in sends:
            cp.wait_send()
        mine.wait()

    return pl.pallas_call(
        body, in_specs=[_ANY], out_specs=_ANY, out_shape=SDS((NDEV,) + small.shape, small.dtype),
        scratch_shapes=[pltpu.SemaphoreType.DMA((NDEV - 1,)), pltpu.SemaphoreType.DMA((NDEV - 1,)),
                        pltpu.SemaphoreType.DMA],
        name="gather_small")(small)


def _hbm(a):
    return pltpu.with_memory_space_constraint(a, pltpu.HBM)


def _rows(ref, idx):
    r = ref.shape[0] // NDEV
    return ref.at[pl.ds(idx * r, r), :]


def _row_copy(ref, idx, send_sem, recv_sem, to):
    return pltpu.make_async_remote_copy(src_ref=_rows(ref, idx), dst_ref=_rows(ref, idx), send_sem=send_sem,
                                        recv_sem=recv_sem, device_id=to, device_id_type=_MESH)


_TOKEN = SDS((8, 128), f32)
_FAR = (4, 2, 6)
_FAR_SIB = (5, 3, 7)


def _ag_start(lands, after, l):
    n = len(lands)

    def body(*refs):
        zones, send_sems, recv_sems, token = refs[:n], refs[n + 1], refs[n + 2], refs[-1]
        _, me_idx = _peer(0)
        for k, mask in enumerate((1,) + _FAR):
            for t in range(n):
                _row_copy(zones[t], me_idx, send_sems.at[k * n + t], recv_sems.at[k * n + t], _peer(mask)[0]).start()
        token[...] = jnp.zeros_like(token)

    outs = pl.pallas_call(
        body, name=f"ag_start_{l}", in_specs=[_HBM] * n + [_ANY],
        out_specs=(_SEM, _SEM, *[_HBM] * n, pl.BlockSpec(memory_space=pltpu.VMEM)),
        out_shape=(pltpu.SemaphoreType.DMA((4 * n,)), pltpu.SemaphoreType.DMA((4 * n,)),
                   *[pltpu.HBM(a.shape, a.dtype) for a in lands], _TOKEN),
        input_output_aliases={t: 2 + t for t in range(n)}, compiler_params=_CP_SPLIT)(
            *[_hbm(a) for a in lands], after)
    return outs[0], outs[1], list(outs[2:2 + n]), outs[-1]


def _ag_pass(lands, recv_sems, after, l):
    n = len(lands)

    def body(*refs):
        zones, recv_sems = refs[:n], refs[n]
        psend, precv, token = refs[n + 2], refs[n + 3], refs[-1]
        me, _ = _peer(0)
        sibling, _ = _peer(1)
        for j, mask in enumerate(_FAR):
            idx = _peer(mask)[1]
            for t in range(n):
                _row_copy(zones[t], idx, psend.at[j * n + t], recv_sems.at[(1 + j) * n + t], me).wait_recv()
                _row_copy(zones[t], idx, psend.at[j * n + t], precv.at[j * n + t], sibling).start()
        token[...] = jnp.zeros_like(token)

    outs = pl.pallas_call(
        body, name=f"ag_pass_{l}", in_specs=[_HBM] * n + [_SEM, _ANY],
        out_specs=(_SEM, _SEM, *[_HBM] * n, pl.BlockSpec(memory_space=pltpu.VMEM)),
        out_shape=(pltpu.SemaphoreType.DMA((3 * n,)), pltpu.SemaphoreType.DMA((3 * n,)),
                   *[pltpu.HBM(a.shape, a.dtype) for a in lands], _TOKEN),
        input_output_aliases={t: 2 + t for t in range(n)}, compiler_params=_CP_SPLIT)(*lands, recv_sems, after)
    return outs[0], outs[1], list(outs[2:2 + n]), outs[-1]


def _ag_wait(lands, send_sems, recv_sems, psend, precv, after, l):
    n = len(lands)

    def body(*refs):
        zones = refs[:n]
        send_sems, recv_sems, psend, precv = refs[n:n + 4]
        me, me_idx = _peer(0)
        sib_idx = _peer(1)[1]
        for k in range(4):
            for t in range(n):
                _row_copy(zones[t], me_idx, send_sems.at[k * n + t], recv_sems.at[k * n + t], me).wait_send()
        for t in range(n):
            _row_copy(zones[t], sib_idx, send_sems.at[t], recv_sems.at[t], me).wait_recv()
        for j in range(3):
            mine, theirs = _peer(_FAR[j])[1], _peer(_FAR_SIB[j])[1]
            for t in range(n):
                _row_copy(zones[t], mine, psend.at[j * n + t], precv.at[j * n + t], me).wait_send()
                _row_copy(zones[t], theirs, psend.at[j * n + t], precv.at[j * n + t], me).wait_recv()

    outs = pl.pallas_call(
        body, name=f"ag_wait_{l}", in_specs=[_HBM] * n + [_SEM] * 4 + [_ANY], out_specs=tuple([_HBM] * n),
        out_shape=tuple(pltpu.HBM(a.shape, a.dtype) for a in lands),
        input_output_aliases={t: t for t in range(n)}, compiler_params=_CP_SPLIT)(
            *lands, send_sems, recv_sems, psend, precv, after)
    return list(outs)


def _slot_rows(ref, slot, rows, t):
    return ref.at[slot, pl.ds(sum(rows[:t]), rows[t]), :]


def _rs_start(grads, slots, l):
    n = len(grads)
    rows = tuple(a.shape[0] // NDEV for a in grads)

    def body(*refs):
        full, slot_ref = refs[:n], refs[n]
        send_sems, recv_sems, token = refs[n + 1], refs[n + 2], refs[-1]
        _, me_idx = _peer(0)
        for k in range(1, NDEV):
            dev, idx = _peer(k)
            for t in range(n):
                pltpu.make_async_remote_copy(
                    src_ref=_rows(full[t], idx), dst_ref=_slot_rows(slot_ref, me_idx, rows, t),
                    send_sem=send_sems.at[(k - 1) * n + t], recv_sem=recv_sems.at[(k - 1) * n + t],
                    device_id=dev, device_id_type=_MESH).start()
        token[...] = jnp.zeros_like(token)

    outs = pl.pallas_call(
        body, name=f"rs_start_{l}", in_specs=[_HBM] * (n + 1),
        out_specs=(_SEM, _SEM, *[_HBM] * (n + 1), pl.BlockSpec(memory_space=pltpu.VMEM)),
        out_shape=(pltpu.SemaphoreType.DMA(((NDEV - 1) * n,)), pltpu.SemaphoreType.DMA(((NDEV - 1) * n,)),
                   *[pltpu.HBM(a.shape, a.dtype) for a in grads], pltpu.HBM(slots.shape, slots.dtype), _TOKEN),
        input_output_aliases={t: 2 + t for t in range(n + 1)}, compiler_params=_CP_SPLIT)(
            *[_hbm(a) for a in grads], _hbm(slots))
    return outs[0], outs[1], list(outs[2:2 + n]), outs[2 + n], outs[-1]


def _rs_wait(grads, slots, send_sems, recv_sems, after, l):
    n = len(grads)
    rows = tuple(a.shape[0] // NDEV for a in grads)

    def body(*refs):
        full, slot_ref, send_sems, recv_sems = refs[:n], refs[n], refs[n + 1], refs[n + 2]
        me, me_idx = _peer(0)
        for k in range(1, NDEV):
            idx = _peer(k)[1]
            for t in range(n):
                cp = pltpu.make_async_remote_copy(
                    src_ref=_rows(full[t], idx), dst_ref=_slot_rows(slot_ref, idx, rows, t),
                    send_sem=send_sems.at[(k - 1) * n + t], recv_sem=recv_sems.at[(k - 1) * n + t],
                    device_id=me, device_id_type=_MESH)
                cp.wait_send()
                cp.wait_recv()

    outs = pl.pallas_call(
        body, name=f"rs_wait_{l}", in_specs=[_HBM] * (n + 1) + [_SEM, _SEM, _ANY],
        out_specs=tuple([_HBM] * (n + 1)),
        out_shape=(*[pltpu.HBM(a.shape, a.dtype) for a in grads], pltpu.HBM(slots.shape, slots.dtype)),
        input_output_aliases={t: t for t in range(n + 1)}, compiler_params=_CP_SPLIT)(
            *grads, slots, send_sems, recv_sems, after)
    return outs[n]


def _sum_slots(slots, rb):
    r = slots.shape[1]

    def body(s_ref, o_ref):
        acc = s_ref[0].astype(f32)
        for s in range(1, NDEV):
            acc = acc + s_ref[s].astype(f32)
        o_ref[...] = acc

    return pl.pallas_call(
        body, grid=(r // rb,),
        in_specs=[pl.BlockSpec((NDEV, rb, D), lambda i: (0, i, 0))],
        out_specs=pl.BlockSpec((rb, D), lambda i: (i, 0)),
        out_shape=SDS((r, D), f32), compiler_params=_CP, name="sum_slots")(slots)


def _adamw(w, g, m, v):
    shape = w.shape
    cols = shape[-1]
    rows = w.size // cols
    rb = rows
    for cand in (512, 256, 128, 64, 32, 16, 8):
        if rows % cand == 0 and rows > cand:
            rb = cand
            break

    def body(w_ref, g_ref, m_ref, v_ref, d_ref, mo_ref, vo_ref):
        d_ref[...], mo_ref[...], vo_ref[...] = _adamw_math(w_ref[...], g_ref[...], m_ref[...], v_ref[...])

    spec = pl.BlockSpec((rb, cols), lambda i: (i, 0))
    outs = pl.pallas_call(
        body, grid=(rows // rb,), in_specs=[spec] * 4, out_specs=[spec] * 3,
        out_shape=[SDS((rows, cols), f32)] * 3, compiler_params=_CP, name="adamw")(
            *(a.reshape(rows, cols) for a in (w, g, m, v)))
    return tuple(o.reshape(shape) for o in outs)


def _adamw_math(w, g, m, v):
    m = ADAM_B1 * m + (1.0 - ADAM_B1) * g
    v = ADAM_B2 * v + (1.0 - ADAM_B2) * (g * g)
    m_hat = m / (1.0 - ADAM_B1 ** ADAM_STEP)
    v_hat = v / (1.0 - ADAM_B2 ** ADAM_STEP)
    return -ADAM_LR * (m_hat / (jnp.sqrt(v_hat) + ADAM_EPS) + ADAM_WD * w), m, v


def _adamw_layer(acc, w, g, m, v, l):
    _, r, c = w.shape
    rb = 256 if r % 256 == 0 and r > 256 else r

    def body(w_ref, g_ref, m_ref, v_ref, *refs):
        go_ref, d_ref, mo_ref, vo_ref = refs[-4:]
        g = g_ref[...]
        go_ref[...] = g
        d_ref[...], mo_ref[...], vo_ref[...] = _adamw_math(w_ref[...], g, m_ref[...], v_ref[...])

    lay = pl.BlockSpec((None, rb, c), lambda i: (l, i, 0))
    n_acc = 0 if acc is None else 4
    outs = pl.pallas_call(
        body, grid=(r // rb,),
        in_specs=[lay, pl.BlockSpec((rb, c), lambda i: (i, 0)), lay, lay] + [_ANY] * n_acc,
        out_specs=[lay] * 4, out_shape=[SDS(w.shape, f32)] * 4,
        input_output_aliases={4 + j: j for j in range(n_acc)},
        compiler_params=_CP, name="adamw_layer")(w, g, m, v, *(() if acc is None else acc))
    return tuple(outs)


_BIG = ("ffn1_w_gate", "ffn1_w_up", "ffn1_w_down", "w_in", "w_out", "ffn2_w_gate", "ffn2_w_up", "ffn2_w_down")
_TRANSPOSED = ("ffn1_w_gate", "ffn1_w_up", "w_in", "ffn2_w_gate", "ffn2_w_up")

def _block_diag(pool_w):
    out = jnp.zeros((L, PW, PW), pool_w.dtype)
    for gi in range(4):
        out = out.at[:, 64 * gi:64 * (gi + 1), 64 * gi:64 * (gi + 1)].set(pool_w[:, gi])
    return out


def kernel(x, positions, ffn1_norm, ffn1_w_gate, ffn1_w_up, ffn1_w_down, mix_norm, w_in, pool_w, pool_scale, w_out, ffn2_norm, ffn2_w_gate, ffn2_w_up, ffn2_w_down, final_norm, loss_target, m_ffn1_norm, m_ffn1_w_gate, m_ffn1_w_up, m_ffn1_w_down, m_mix_norm, m_w_in, m_pool_w, m_pool_scale, m_w_out, m_ffn2_norm, m_ffn2_w_gate, m_ffn2_w_up, m_ffn2_w_down, m_final_norm, v_ffn1_norm, v_ffn1_w_gate, v_ffn1_w_up, v_ffn1_w_down, v_mix_norm, v_w_in, v_pool_w, v_pool_scale, v_w_out, v_ffn2_norm, v_ffn2_w_gate, v_ffn2_w_up, v_ffn2_w_down, v_final_norm):
    weights = dict(ffn1_norm=ffn1_norm, ffn1_w_gate=ffn1_w_gate, ffn1_w_up=ffn1_w_up, ffn1_w_down=ffn1_w_down,
                   mix_norm=mix_norm, w_in=w_in, pool_w=pool_w, pool_scale=pool_scale, w_out=w_out,
                   ffn2_norm=ffn2_norm, ffn2_w_gate=ffn2_w_gate, ffn2_w_up=ffn2_w_up, ffn2_w_down=ffn2_w_down,
                   final_norm=final_norm)
    moms = dict(ffn1_norm=m_ffn1_norm, ffn1_w_gate=m_ffn1_w_gate, ffn1_w_up=m_ffn1_w_up, ffn1_w_down=m_ffn1_w_down,
                mix_norm=m_mix_norm, w_in=m_w_in, pool_w=m_pool_w, pool_scale=m_pool_scale, w_out=m_w_out,
                ffn2_norm=m_ffn2_norm, ffn2_w_gate=m_ffn2_w_gate, ffn2_w_up=m_ffn2_w_up, ffn2_w_down=m_ffn2_w_down,
                final_norm=m_final_norm)
    vels = dict(ffn1_norm=v_ffn1_norm, ffn1_w_gate=v_ffn1_w_gate, ffn1_w_up=v_ffn1_w_up, ffn1_w_down=v_ffn1_w_down,
                mix_norm=v_mix_norm, w_in=v_w_in, pool_w=v_pool_w, pool_scale=v_pool_scale, w_out=v_w_out,
                ffn2_norm=v_ffn2_norm, ffn2_w_gate=v_ffn2_w_gate, ffn2_w_up=v_ffn2_w_up, ffn2_w_down=v_ffn2_w_down,
                final_norm=v_final_norm)
    names = list(weights)

    me_idx = 4 * lax.axis_index("x") + 2 * lax.axis_index("y") + lax.axis_index("c")

    tr = lambda w: jnp.swapaxes(w, 1, 2).astype(bf16)
    shards = [tr(weights[nm]) if nm in _TRANSPOSED else weights[nm].astype(bf16) for nm in _BIG]

    def landing_zones(l, which):
        return [lax.dynamic_update_slice(lax.empty((NDEV * shards[t].shape[1], D), bf16), shards[t][l],
                                         (me_idx * shards[t].shape[1], 0)) for t in which]

    g_ffn1 = [ffn1_norm[l].reshape(1, D) for l in range(L)]
    g_mix = [mix_norm[l].reshape(1, D) for l in range(L)]
    g_ffn2 = [ffn2_norm[l].reshape(1, D) for l in range(L)]
    wbd_all = _block_diag(pool_w).astype(bf16)
    wbd = [wbd_all[l] for l in range(L)]
    pscale = [pool_scale[l].reshape(1, PW) for l in range(L)]
    tabs = _rope_tables(positions)
    flat = lambda a: a.reshape(S, AW)
    r4 = lambda a: a.reshape(4, S // 4, AW)
    r16 = lambda a: a.reshape(16, S // 16, AW)

    first, rest, whole = (0, 1, 2, 3), (4, 5, 6, 7), tuple(range(8))
    head = [a.reshape(a.shape[1], D) for a in _all_gather([shards[t][0:1] for t in first])]
    chain = {0: _ag_start(landing_zones(0, rest), head[0], "0"), 1: _ag_start(landing_zones(1, whole), head[0], "1")}
    gathered = [None] * L
    xs = x.reshape(S, D)
    saved = []
    for l in range(L):
        ga, gb = g_ffn1[l], g_ffn2[l]
        if l == 0:
            gt1, ut1, dn1, wint = head
            ga = ga + chain[0][3][0, 0] + chain[1][3][0, 0]
        else:
            gt1, ut1, dn1, wint, wout, gt2, ut2, dn2 = gathered[l]
            if l + 1 < L:
                chain[l + 1] = _ag_start(landing_zones(l + 1, whole), xs, str(l + 1))
                ga = ga + chain[l + 1][3][0, 0]
        x0 = xs
        x1, gate1, up1 = _ffn_fwd(x0, ga, gt1, ut1, dn1)
        hmix, vp, q1, k1, v1, q4, k4, v4, q16, k16, v16 = _mix_in_fwd(x1, g_mix[l], wint, tabs)
        q4, k4, v4, q16, k16, v16 = map(flat, (q4, k4, v4, q16, k16, v16))
        ypool, diff = _pool_fwd(vp, wbd[l], pscale[l])
        o1, l1 = _attn_fwd(q1, k1, v1, S)
        o4, l4 = _attn_fwd(q4, k4, v4, S // 4)
        if l == 0:
            send_sems, recv_sems, zones, _ = chain[0]
            psend, precv, zones, _ = _ag_pass(zones, recv_sems, o4, "0")
        o16, l16 = _attn_fwd(q16, k16, v16, S // 16)
        if l == 0:
            wout, gt2, ut2, dn2 = _ag_wait(zones, send_sems, recv_sems, psend, precv, o16, "0")
            gathered[0] = head + [wout, gt2, ut2, dn2]
        elif l + 1 < L:
            send_sems, recv_sems, zones, _ = chain[l + 1]
            psend, precv, zones, token = _ag_pass(zones, recv_sems, o16, str(l + 1))
            gb = gb + token[0, 0]
        x2, mixed, o, lse1, lse4, lse16 = _mix_out_fwd(x1, ypool, o1, l1, r4(o4), r4(l4), r16(o16), r16(l16), wout)
        if l == 0:
            send_sems, recv_sems, zones, _ = chain[1]
            psend, precv, zones, token = _ag_pass(zones, recv_sems, x2, "1")
            gb = gb + token[0, 0]
        x3, gate2, up2 = _ffn_fwd(x2, gb, gt2, ut2, dn2)
        if l + 1 < L:
            gathered[l + 1] = _ag_wait(zones, send_sems, recv_sems, psend, precv, x3, str(l + 1))
        saved.append(dict(x0=x0, x1=x1, x2=x2, gate1=gate1, up1=up1, gate2=gate2, up2=up2, hmix=hmix, diff=diff,
                          qkv=((q1, k1, v1), (q4, k4, v4), (q16, k16, v16)), mixed=mixed, o=o,
                          lse=(lse1, flat(lse4), flat(lse16))))
        xs = x3

    dx, loss_part, d_final = _loss_head(xs, final_norm.reshape(1, D), loss_target.reshape(S, D))

    d_norm = {nm: [None] * L for nm in ("ffn1_norm", "mix_norm", "ffn2_norm")}
    d_poolw, d_pscale = [None] * L, [None] * L
    group_a = ("ffn2_w_gate", "ffn2_w_up", "ffn2_w_down", "w_out")
    group_b = ("ffn1_w_gate", "ffn1_w_up", "ffn1_w_down", "w_in")
    acc = {}

    def exchange(full, group, tag):
        grads_g = [full[nm] for nm in group]
        rows = [g.shape[0] // NDEV for g in grads_g]
        own = jnp.concatenate([lax.dynamic_slice(g, (me_idx * r, 0), (r, D)) for g, r in zip(grads_g, rows)], axis=0)
        slots = lax.dynamic_update_slice(lax.empty((NDEV, sum(rows), D), bf16), own[None], (me_idx, 0, 0))
        ssem, rsem, grads_g, slots, token = _rs_start(grads_g, slots, tag)
        return (grads_g, slots, ssem, rsem, tag), token

    def update(l, group, flight, after):
        grads_g, slots, ssem, rsem, tag = flight
        total = _sum_slots(_rs_wait(grads_g, slots, ssem, rsem, after, tag), slots.shape[1] // 2)
        off = 0
        for nm, g in zip(group, grads_g):
            r = g.shape[0] // NDEV
            g = total[off:off + r]
            off += r
            acc[nm] = _adamw_layer(acc.get(nm), weights[nm], g.T if nm in _TRANSPOSED else g, moms[nm], vels[nm], l)
        return acc[group[-1]][0]

    flights = {}
    token_b = None
    for l in reversed(range(L)):
        sv = saved[l]
        gt1, ut1, dn1, wint, wout, gt2, ut2, dn2 = gathered[l]
        gb = g_ffn2[l] if token_b is None else g_ffn2[l] + token_b[0, 0]
        full = {}
        dx, dgate, dup, h, dy, d_norm["ffn2_norm"][l] = _ffn_bwd_d(sv["x2"], gb, sv["gate2"], sv["up2"], dx, gt2, ut2, dn2)
        full["ffn2_w_gate"], full["ffn2_w_up"], full["ffn2_w_down"] = _ffn_bwd_w(h, dy, sv["gate2"], sv["up2"], dgate, dup)

        dxb, dyp, do1, do4, do16, dl1, dl4, dl16 = _mix_out_bwd(dx, sv["o"], wout)
        full["w_out"] = _wgrad(sv["mixed"], dxb)
        flights[l, "a"], token_a = exchange(full, group_a, f"a{l}")
        dvp, dwbd, d_pscale[l] = _pool_bwd(dyp, sv["diff"], wbd[l], pscale[l] + token_a[0, 0])
        d_poolw[l] = jnp.stack([dwbd[64 * gi:64 * (gi + 1), 64 * gi:64 * (gi + 1)] for gi in range(4)])
        dos, dls = (do1, flat(do4), flat(do16)), (dl1, flat(dl4), flat(dl16))
        dqkv = []
        for b, lc in enumerate((S, S // 4, S // 16)):
            qb, kb, vb = sv["qkv"][b]
            dqkv.append(_attn_bwd(qb, kb, vb, dos[b], sv["lse"][b], dls[b], lc))
        d4 = tuple(r4(a) for a in dqkv[1])
        d16 = tuple(r16(a) for a in dqkv[2])
        dx, dproj, d_norm["mix_norm"][l] = _mix_in_bwd(dx, sv["x1"], g_mix[l], wint, tabs, dvp, dqkv[0], d4, d16)
        full["w_in"] = _wgrad(dproj, sv["hmix"])

        dx, dgate, dup, h, dy, d_norm["ffn1_norm"][l] = _ffn_bwd_d(sv["x0"], g_ffn1[l], sv["gate1"], sv["up1"], dx, gt1, ut1, dn1)
        full["ffn1_w_gate"], full["ffn1_w_up"], full["ffn1_w_down"] = _ffn_bwd_w(h, dy, sv["gate1"], sv["up1"], dgate, dup)

        if l + 1 < L and l + 1 >= 2:
            update(l + 1, group_a, flights.pop((l + 1, "a")), dx)
            update(l + 1, group_b, flights.pop((l + 1, "b")), dx)
        flights[l, "b"], token_b = exchange(full, group_b, f"b{l}")

    pad8 = lambda a: jnp.pad(a, ((0, 8 - a.shape[0]), (0, 0)))
    misc = jnp.concatenate([d_final, jnp.concatenate(d_pscale, axis=1), loss_part], axis=0)
    small = jnp.concatenate(
        [pad8(jnp.concatenate(d_norm[nm], axis=0)) for nm in ("ffn1_norm", "mix_norm", "ffn2_norm")]
        + [pad8(misc), jnp.stack(d_poolw).reshape(L * 16, D)], axis=0)
    sm = _sum_slots(_gather_small(small + token_b[0, 0]), SMALL_ROWS)
    grads = {}
    grads["ffn1_norm"], grads["mix_norm"], grads["ffn2_norm"] = sm[0:L], sm[8:8 + L], sm[16:16 + L]
    grads["final_norm"] = sm[24]
    grads["pool_scale"] = sm[25].reshape(L, PW)
    grads["pool_w"] = sm[32:32 + L * 16].reshape(L, 4, 64, 64)
    loss = sm[26, 0]

    after = sm
    for key in [(1, "a"), (1, "b"), (0, "a")]:
        after = update(key[0], group_a if key[1] == "a" else group_b, flights.pop(key), after)
    upd = {nm: _adamw(weights[nm], grads[nm], moms[nm], vels[nm]) for nm in names if nm not in _BIG}
    after = update(0, group_b, flights.pop((0, "b")), upd["final_norm"][0] + after[0, 0, 0])
    for nm in _BIG:
        grads[nm] = acc[nm][0]
        upd[nm] = acc[nm][1:]
    return (loss, dx.reshape(1, S, D), *[grads[nm] for nm in names], *[upd[nm][0] for nm in names],
            *[upd[nm][1] for nm in names], *[upd[nm][2] for nm in names])
```

```python
import jax
import jax.numpy as jnp
from jax import lax
from jax.experimental import pallas as pl
from jax.experimental.pallas import tpu as pltpu

f32 = jnp.float32
bf16 = jnp.bfloat16
SDS = jax.ShapeDtypeStruct

D = 1024
S = 2048
F = 2816
L = 4
PW = 256
AW = 768
PROJ = PW + 3 * AW
NDEV = 8
TM = 256
QB = 128
HALF = 64
NG = AW // 128
NORM_EPS = 1e-6
MASK_VALUE = -1e30
ROPE_THETA = 500000.0
ADAM_LR, ADAM_B1, ADAM_B2, ADAM_EPS, ADAM_WD, ADAM_STEP = 0.001, 0.9, 0.999, 1e-08, 0.01, 10
POOL_WINDOWS = (2, 4, 8, 16)
PAD = 8
SMALL_ROWS = 96
VMEM_LIMIT = 56 * 1024 * 1024

_CP = pltpu.CompilerParams(vmem_limit_bytes=VMEM_LIMIT)
_ANY = pl.BlockSpec(memory_space=pl.ANY)
_HBM = pl.BlockSpec(memory_space=pltpu.HBM)
_SEM = pl.BlockSpec(memory_space=pltpu.SEMAPHORE)
_MESH = pl.DeviceIdType.MESH
_CP_SPLIT = pltpu.CompilerParams(has_side_effects=pltpu.SideEffectType.DATAFLOW_SIDE_EFFECTING)


def _dot_nn(a, b):
    return lax.dot_general(a, b, (((1,), (0,)), ((), ())), preferred_element_type=f32)


def _dot_nt(a, b):
    return lax.dot_general(a, b, (((1,), (1,)), ((), ())), preferred_element_type=f32)


def _dot_tn(a, b):
    return lax.dot_general(a, b, (((0,), (0,)), ((), ())), preferred_element_type=f32)


def _rms(x, g):
    r = lax.rsqrt(jnp.mean(x * x, axis=-1, keepdims=True) + NORM_EPS)
    xh = x * r
    return r, xh, xh * g


def _rms_bwd(dh, r, xh, g):
    dxh = dh * g
    return r * (dxh - xh * jnp.mean(dxh * xh, axis=-1, keepdims=True))


def _tile(cols):
    return pl.BlockSpec((TM, cols), lambda i: (i, 0))


def _const(shape):
    return pl.BlockSpec(shape, lambda i: (0,) * len(shape))


def _layer(rows, cols, l=None):
    return pl.BlockSpec((rows, cols), lambda i: (0, 0), pipeline_mode=pl.Buffered(1))


def _p4():
    return pl.BlockSpec((4, TM // 4, AW), lambda i: (0, i, 0))


def _p16():
    return pl.BlockSpec((16, TM // 16, AW), lambda i: (0, i, 0))


def _cols(j):
    return slice(128 * j, 128 * (j + 1))


def _ffn_fwd(x, g, gt, ut, dn, l=None):
    def body(x_ref, g_ref, gt_ref, ut_ref, dn_ref, xo_ref, gate_ref, up_ref):
        x = x_ref[...]
        _, _, hn = _rms(x, g_ref[...])
        h = hn.astype(bf16)
        gate = _dot_nt(h, gt_ref[...])
        up = _dot_nt(h, ut_ref[...])
        gate_ref[...] = gate.astype(bf16)
        up_ref[...] = up.astype(bf16)
        a = (gate * jax.nn.sigmoid(gate) * up).astype(bf16)
        xo_ref[...] = x + 0.5 * _dot_nn(a, dn_ref[...])

    return pl.pallas_call(
        body, grid=(S // TM,),
        in_specs=[_tile(D), _layer(1, D, l), _layer(F, D, l), _layer(F, D, l), _layer(F, D, l)],
        out_specs=[_tile(D), _tile(F), _tile(F)],
        out_shape=[SDS((S, D), f32), SDS((S, F), bf16), SDS((S, F), bf16)],
        compiler_params=_CP, name="ffn_fwd")(x, g, gt, ut, dn)


def _ffn_bwd_d(x, g, gate, up, dxo, gt, ut, dn, l=None):
    def body(x_ref, g_ref, gate_ref, up_ref, dxo_ref, gt_ref, ut_ref, dn_ref,
             dx_ref, dgate_ref, dup_ref, h_ref, dy_ref, dg_ref):
        x = x_ref[...]
        g = g_ref[...]
        r, xh, hn = _rms(x, g)
        h_ref[...] = hn.astype(bf16)
        dxo = dxo_ref[...]
        dy = (0.5 * dxo).astype(bf16)
        dy_ref[...] = dy
        da = _dot_nt(dy, dn_ref[...])
        gate = gate_ref[...].astype(f32)
        up = up_ref[...].astype(f32)
        sg = jax.nn.sigmoid(gate)
        dgate = (da * up * (sg * (1.0 + gate * (1.0 - sg)))).astype(bf16)
        dup = (da * (gate * sg)).astype(bf16)
        dgate_ref[...] = dgate
        dup_ref[...] = dup
        dh = _dot_nn(dgate, gt_ref[...]) + _dot_nn(dup, ut_ref[...])

        @pl.when(pl.program_id(0) == 0)
        def _():
            dg_ref[...] = jnp.zeros_like(dg_ref)

        dg_ref[...] += jnp.sum(dh * xh, axis=0, keepdims=True)
        dx_ref[...] = dxo + _rms_bwd(dh, r, xh, g)

    return pl.pallas_call(
        body, grid=(S // TM,),
        in_specs=[_tile(D), _layer(1, D, l), _tile(F), _tile(F), _tile(D),
                  _layer(F, D, l), _layer(F, D, l), _layer(F, D, l)],
        out_specs=[_tile(D), _tile(F), _tile(F), _tile(D), _tile(D), _const((1, D))],
        out_shape=[SDS((S, D), f32), SDS((S, F), bf16), SDS((S, F), bf16), SDS((S, D), bf16),
                   SDS((S, D), bf16), SDS((1, D), f32)],
        compiler_params=_CP, name="ffn_bwd_d")(x, g, gate, up, dxo, gt, ut, dn)


def _ffn_bwd_w(h, dy, gate, up, dgate, dup):
    fc = 256

    def body(h_ref, dy_ref, gate_ref, up_ref, dgate_ref, dup_ref, dgt_ref, dut_ref, ddn_ref):
        gate = gate_ref[...].astype(f32)
        a = (gate * jax.nn.sigmoid(gate) * up_ref[...].astype(f32)).astype(bf16)
        ddn_ref[...] = _dot_tn(a, dy_ref[...]).astype(bf16)
        h = h_ref[...]
        dgt_ref[...] = _dot_tn(dgate_ref[...], h).astype(bf16)
        dut_ref[...] = _dot_tn(dup_ref[...], h).astype(bf16)

    col = pl.BlockSpec((S, fc), lambda j: (0, j))
    row = pl.BlockSpec((fc, D), lambda j: (j, 0))
    full = pl.BlockSpec((S, D), lambda j: (0, 0))
    return pl.pallas_call(
        body, grid=(F // fc,),
        in_specs=[full, full, col, col, col, col],
        out_specs=[row, row, row],
        out_shape=[SDS((F, D), bf16)] * 3,
        compiler_params=_CP, name="ffn_bwd_w")(h, dy, gate, up, dgate, dup)


def _wgrad(a, b):
    m, n = a.shape[1], b.shape[1]
    mc = 256

    def body(a_ref, b_ref, o_ref):
        o_ref[...] = _dot_tn(a_ref[...], b_ref[...]).astype(bf16)

    return pl.pallas_call(
        body, grid=(m // mc,),
        in_specs=[pl.BlockSpec((S, mc), lambda j: (0, j)), pl.BlockSpec((S, n), lambda j: (0, 0))],
        out_specs=pl.BlockSpec((mc, n), lambda j: (j, 0)),
        out_shape=SDS((m, n), bf16),
        compiler_params=_CP, name="wgrad")(a, b)


def _rope(t, c, sn, sp):
    return t * c + pltpu.roll(t, 120, 1) * sn + pltpu.roll(t, 8, 1) * sp


def _rope_bwd(d, c, sn, sp):
    return d * c + pltpu.roll(d * sn, 8, 1) + pltpu.roll(d * sp, 120, 1)


def _rope_tables(positions):
    inv_freq = ROPE_THETA ** (-jnp.arange(0, 16, 2, dtype=f32) / 16)
    ang = positions.reshape(S, 1).astype(f32) * inv_freq
    cos, sin = jnp.cos(ang), jnp.sin(ang)
    one = jnp.ones((S, 48), f32)
    zero8 = jnp.zeros((S, 8), f32)
    zero48 = jnp.zeros((S, 48), f32)
    c = jnp.concatenate([cos, cos, one], axis=1)
    sn = jnp.concatenate([-sin, zero8, zero48], axis=1)
    sp = jnp.concatenate([zero8, sin, zero48], axis=1)
    return tuple(jnp.concatenate([t, t], axis=1) for t in (c, sn, sp))


def _mix_in_fwd(x, g, wint, tabs, l=None):
    def body(x_ref, g_ref, w_ref, c_ref, sn_ref, sp_ref,
             h_ref, vp_ref, q1, k1, v1, q4, k4, v4, q16, k16, v16, scr):
        _, _, hn = _rms(x_ref[...], g_ref[...])
        h = hn.astype(bf16)
        h_ref[...] = h
        proj = _dot_nt(h, w_ref[...])
        vp_ref[...] = proj[:, :PW]
        c, sn, sp = c_ref[...], sn_ref[...], sp_ref[...]
        for kind, (o1, o4, o16) in enumerate(((q1, q4, q16), (k1, k4, k16), (v1, v4, v16))):
            for j in range(NG):
                t = proj[:, PW + kind * AW + 128 * j: PW + kind * AW + 128 * (j + 1)]
                if kind == 0:
                    t = _rope(t, c, sn, sp) * 0.125
                elif kind == 1:
                    t = _rope(t, c, sn, sp)
                scr[j] = t
                o1[:, _cols(j)] = t.astype(bf16)
            for r in range(4):
                for j in range(NG):
                    o4[r, :, _cols(j)] = scr[j, pl.ds(r, TM // 4, stride=4), :].astype(bf16)
            for r in range(16):
                for j in range(NG):
                    o16[r, :, _cols(j)] = scr[j, pl.ds(r, TM // 16, stride=16), :].astype(bf16)

    nat, d4, d16 = SDS((S, AW), bf16), SDS((4, S // 4, AW), bf16), SDS((16, S // 16, AW), bf16)
    return pl.pallas_call(
        body, grid=(S // TM,),
        in_specs=[_tile(D), _layer(1, D, l), _layer(PROJ, D, l), _tile(128), _tile(128), _tile(128)],
        out_specs=[_tile(D), _tile(PW)] + [_tile(AW)] * 3 + [_p4()] * 3 + [_p16()] * 3,
        out_shape=[SDS((S, D), bf16), SDS((S, PW), f32)] + [nat] * 3 + [d4] * 3 + [d16] * 3,
        scratch_shapes=[pltpu.VMEM((NG, TM, 128), f32)],
        compiler_params=_CP, name="mix_in_fwd")(x, g, wint, *tabs)


def _mix_in_bwd(dxo, x, g, wint, tabs, dvp, d1, d4, d16, l=None):
    def body(dxo_ref, x_ref, g_ref, w_ref, c_ref, sn_ref, sp_ref, dvp_ref,
             dq1, dk1, dv1, dq4, dk4, dv4, dq16, dk16, dv16,
             dx_ref, dproj_ref, dg_ref, s4, s16):
        c, sn, sp = c_ref[...], sn_ref[...], sp_ref[...]
        dproj_ref[:, :PW] = dvp_ref[...].astype(bf16)
        for kind, (a1, a4, a16) in enumerate(((dq1, dq4, dq16), (dk1, dk4, dk16), (dv1, dv4, dv16))):
            for r in range(4):
                for j in range(NG):
                    s4[j, pl.ds(r, TM // 4, stride=4), :] = a4[r, :, _cols(j)]
            for r in range(16):
                for j in range(NG):
                    s16[j, pl.ds(r, TM // 16, stride=16), :] = a16[r, :, _cols(j)]
            for j in range(NG):
                t = a1[:, _cols(j)] + s4[j] + s16[j]
                if kind == 0:
                    t = _rope_bwd(t * 0.125, c, sn, sp)
                elif kind == 1:
                    t = _rope_bwd(t, c, sn, sp)
                dproj_ref[:, PW + kind * AW + 128 * j: PW + kind * AW + 128 * (j + 1)] = t.astype(bf16)
        g = g_ref[...]
        r_, xh, _ = _rms(x_ref[...], g)
        dh = _dot_nn(dproj_ref[...], w_ref[...])

        @pl.when(pl.program_id(0) == 0)
        def _():
            dg_ref[...] = jnp.zeros_like(dg_ref)

        dg_ref[...] += jnp.sum(dh * xh, axis=0, keepdims=True)
        dx_ref[...] = dxo_ref[...] + _rms_bwd(dh, r_, xh, g)

    return pl.pallas_call(
        body, grid=(S // TM,),
        in_specs=[_tile(D), _tile(D), _layer(1, D, l), _layer(PROJ, D, l), _tile(128), _tile(128), _tile(128),
                  _tile(PW)] + [_tile(AW)] * 3 + [_p4()] * 3 + [_p16()] * 3,
        out_specs=[_tile(D), _tile(PROJ), _const((1, D))],
        out_shape=[SDS((S, D), f32), SDS((S, PROJ), bf16), SDS((1, D), f32)],
        scratch_shapes=[pltpu.VMEM((NG, TM, 128), f32), pltpu.VMEM((NG, TM, 128), f32)],
        compiler_params=_CP, name="mix_in_bwd")(dxo, x, g, wint, *tabs, dvp, *d1, *d4, *d16)


def _pool_sums(pad_ref, base, rows, adjoint):
    lane_group = lax.broadcasted_iota(jnp.int32, (rows, PW), 1) // 64
    sign = -1 if adjoint else 1

    def sh(o):
        return pad_ref[pl.ds(PAD + base + sign * o, rows), :]

    out = None
    acc = None
    lo, hi = 0, 0
    for gi, w in enumerate(POOL_WINDOWS):
        for o in list(range(-(w // 2), lo)) + list(range(hi, w - w // 2)):
            acc = sh(o) if acc is None else acc + sh(o)
        lo, hi = -(w // 2), w - w // 2
        out = acc if out is None else jnp.where(lane_group >= gi, acc, out)
    return out


def _pool_counts(base, rows):
    pos = base + lax.broadcasted_iota(jnp.int32, (rows, PW), 0)
    lane_group = lax.broadcasted_iota(jnp.int32, (rows, PW), 1) // 64
    cnt = None
    for gi, w in enumerate(POOL_WINDOWS):
        lo = jnp.maximum(pos - w // 2, 0)
        hi = jnp.minimum(pos + w - 1 - w // 2, S - 1)
        c = (hi - lo + 1).astype(f32)
        cnt = c if cnt is None else jnp.where(lane_group >= gi, c, cnt)
    return cnt


def _pool_fwd(vp, wbd, scale, l=None):
    ch = 256

    def body(vp_ref, w_ref, sc_ref, y_ref, diff_ref, pad):
        pad[pl.ds(0, PAD), :] = jnp.zeros((PAD, PW), f32)
        pad[pl.ds(PAD + S, PAD), :] = jnp.zeros((PAD, PW), f32)
        pad[pl.ds(PAD, S), :] = vp_ref[...]
        for b in range(S // ch):
            base = b * ch
            pooled = _pool_sums(pad, base, ch, False) / _pool_counts(base, ch)
            diff = (pooled - vp_ref[pl.ds(base, ch), :]).astype(bf16)
            diff_ref[pl.ds(base, ch), :] = diff
            y_ref[pl.ds(base, ch), :] = _dot_nn(diff, w_ref[...]) * sc_ref[...]

    whole = lambda shape: pl.BlockSpec(shape, lambda i: (0,) * len(shape))
    return pl.pallas_call(
        body, grid=(1,),
        in_specs=[whole((S, PW)), whole((PW, PW)), whole((1, PW))],
        out_specs=[whole((S, PW)), whole((S, PW))],
        out_shape=[SDS((S, PW), f32), SDS((S, PW), bf16)],
        scratch_shapes=[pltpu.VMEM((S + 2 * PAD, PW), f32)],
        compiler_params=_CP, name="pool_fwd")(vp, wbd, scale)


def _pool_bwd(dy, diff, wbd, scale, l=None):
    ch = 256

    def body(dy_ref, diff_ref, w_ref, sc_ref, dvp_ref, dw_ref, dsc_ref, pad):
        pad[pl.ds(0, PAD), :] = jnp.zeros((PAD, PW), f32)
        pad[pl.ds(PAD + S, PAD), :] = jnp.zeros((PAD, PW), f32)
        dw = jnp.zeros((PW, PW), f32)
        dsc = jnp.zeros((1, PW), f32)
        for b in range(S // ch):
            base = b * ch
            dy = dy_ref[pl.ds(base, ch), :]
            diff = diff_ref[pl.ds(base, ch), :]
            dsc = dsc + jnp.sum(dy * _dot_nn(diff, w_ref[...]), axis=0, keepdims=True)
            dz = (dy * sc_ref[...]).astype(bf16)
            dw = dw + _dot_tn(diff, dz)
            ddiff = _dot_nt(dz, w_ref[...])
            dvp_ref[pl.ds(base, ch), :] = -ddiff
            pad[pl.ds(PAD + base, ch), :] = ddiff / _pool_counts(base, ch)
        dw_ref[...] = dw
        dsc_ref[...] = dsc
        for b in range(S // ch):
            base = b * ch
            dvp_ref[pl.ds(base, ch), :] += _pool_sums(pad, base, ch, True)

    whole = lambda shape: pl.BlockSpec(shape, lambda i: (0,) * len(shape))
    return pl.pallas_call(
        body, grid=(1,),
        in_specs=[whole((S, PW)), whole((S, PW)), whole((PW, PW)), whole((1, PW))],
        out_specs=[whole((S, PW)), whole((PW, PW)), whole((1, PW))],
        out_shape=[SDS((S, PW), f32), SDS((PW, PW), f32), SDS((1, PW), f32)],
        scratch_shapes=[pltpu.VMEM((S + 2 * PAD, PW), f32)],
        compiler_params=_CP, name="pool_bwd")(dy, diff, wbd, scale)


def _attn_block_index(b, lc):
    bpc = lc // QB
    kw = min(2 * QB, lc)
    row0 = pl.multiple_of(b * QB, QB)
    t0 = (b % bpc) * QB
    ks_in = jnp.clip(t0 - HALF, 0, lc - kw)
    kstart = pl.multiple_of((b // bpc) * lc + ks_in, HALF)
    qpos = t0 + lax.broadcasted_iota(jnp.int32, (QB, kw), 0)
    kpos = ks_in + lax.broadcasted_iota(jnp.int32, (QB, kw), 1)
    valid = jnp.abs(qpos - kpos) <= HALF
    return kw, row0, kstart, valid


def _attn_fwd(q, k, v, lc):
    def body(q_ref, k_ref, v_ref, o_ref, lse_ref):
        head0 = lax.broadcasted_iota(jnp.int32, (QB, 128), 1) < 64

        def blk(b, carry):
            kw, row0, kstart, valid = _attn_block_index(b, lc)
            qb = q_ref[pl.ds(row0, QB), :]
            kb = k_ref[pl.ds(kstart, kw), :]
            vb = v_ref[pl.ds(kstart, kw), :]
            outs, lses = [], []
            for hh in range(2):
                lm = head0 if hh == 0 else jnp.logical_not(head0)
                qh = jnp.where(lm, qb, jnp.zeros_like(qb))
                s = jnp.where(valid, _dot_nt(qh, kb), MASK_VALUE)
                m = jnp.max(s, axis=-1, keepdims=True)
                p = jnp.exp(s - m)
                den = jnp.sum(p, axis=-1, keepdims=True)
                outs.append(_dot_nn(p.astype(bf16), vb) / den)
                lses.append(jnp.broadcast_to(m + jnp.log(den), (QB, 128)))
            o_ref[pl.ds(row0, QB), :] = jnp.where(head0, outs[0], outs[1])
            lse_ref[pl.ds(row0, QB), :] = jnp.where(head0, lses[0], lses[1])
            return carry

        lax.fori_loop(0, S // QB, blk, 0)

    col = pl.BlockSpec((S, 128), lambda p: (0, p))
    return pl.pallas_call(
        body, grid=(NG,), in_specs=[col, col, col], out_specs=[col, col],
        out_shape=[SDS((S, AW), f32), SDS((S, AW), f32)],
        compiler_params=_CP, name=f"attn_fwd_{lc}")(q, k, v)


def _attn_bwd(q, k, v, do, lse, delta, lc):
    def body(q_ref, k_ref, v_ref, do_ref, lse_ref, dl_ref, dq_ref, dk_ref, dv_ref):
        head0 = lax.broadcasted_iota(jnp.int32, (QB, 128), 1) < 64
        dk_ref[...] = jnp.zeros_like(dk_ref)
        dv_ref[...] = jnp.zeros_like(dv_ref)

        def blk(b, carry):
            kw, row0, kstart, valid = _attn_block_index(b, lc)
            khead0 = lax.broadcasted_iota(jnp.int32, (kw, 128), 1) < 64
            qb = q_ref[pl.ds(row0, QB), :]
            dob = do_ref[pl.ds(row0, QB), :]
            lse = lse_ref[pl.ds(row0, QB), :]
            dl = dl_ref[pl.ds(row0, QB), :]
            kb = k_ref[pl.ds(kstart, kw), :]
            vb = v_ref[pl.ds(kstart, kw), :]
            dqs, dks, dvs = [], [], []
            for hh in range(2):
                lm = head0 if hh == 0 else jnp.logical_not(head0)
                c0 = 64 * hh
                qh = jnp.where(lm, qb, jnp.zeros_like(qb))
                doh = jnp.where(lm, dob, jnp.zeros_like(dob))
                s = jnp.where(valid, _dot_nt(qh, kb), MASK_VALUE)
                p = jnp.exp(s - lse[:, c0:c0 + 1])
                dp = _dot_nt(doh, vb)
                ds = (p * (dp - dl[:, c0:c0 + 1])).astype(bf16)
                dqs.append(_dot_nn(ds, kb))
                dks.append(_dot_tn(ds, qb))
                dvs.append(_dot_tn(p.astype(bf16), dob))
            dq_ref[pl.ds(row0, QB), :] = jnp.where(head0, dqs[0], dqs[1])
            dk_ref[pl.ds(kstart, kw), :] += jnp.where(khead0, dks[0], dks[1])
            dv_ref[pl.ds(kstart, kw), :] += jnp.where(khead0, dvs[0], dvs[1])
            return carry

        lax.fori_loop(0, S // QB, blk, 0)

    col = pl.BlockSpec((S, 128), lambda p: (0, p))
    return pl.pallas_call(
        body, grid=(NG,), in_specs=[col] * 6, out_specs=[col] * 3,
        out_shape=[SDS((S, AW), f32)] * 3,
        compiler_params=_CP, name=f"attn_bwd_{lc}")(q, k, v, do, lse, delta)


def _mix_out_fwd(x, ypool, o1, l1, o4, l4, o16, l16, wout, l=None):
    def body(x_ref, yp_ref, o1_ref, l1_ref, o4_ref, l4_ref, o16_ref, l16_ref, w_ref,
             xo_ref, mixed_ref, o_ref, lse1_ref, lse4_ref, lse16_ref, so4, sl4, so16, sl16, sl):
        for r in range(4):
            for j in range(NG):
                so4[j, pl.ds(r, TM // 4, stride=4), :] = o4_ref[r, :, _cols(j)]
                sl4[j, pl.ds(r, TM // 4, stride=4), :] = l4_ref[r, :, _cols(j)]
        for r in range(16):
            for j in range(NG):
                so16[j, pl.ds(r, TM // 16, stride=16), :] = o16_ref[r, :, _cols(j)]
                sl16[j, pl.ds(r, TM // 16, stride=16), :] = l16_ref[r, :, _cols(j)]
        mixed_ref[:, :PW] = yp_ref[...].astype(bf16)
        for j in range(NG):
            a, b, c = l1_ref[:, _cols(j)], sl4[j], sl16[j]
            m = jnp.maximum(jnp.maximum(a, b), c)
            wa, wb, wc = jnp.exp(a - m), jnp.exp(b - m), jnp.exp(c - m)
            den = wa + wb + wc
            y = (wa * o1_ref[:, _cols(j)] + wb * so4[j] + wc * so16[j]) / den
            lse = m + jnp.log(den)
            o_ref[:, _cols(j)] = y
            lse1_ref[:, _cols(j)] = lse
            sl[j] = lse
            mixed_ref[:, PW + 128 * j: PW + 128 * (j + 1)] = y.astype(bf16)
        for r in range(4):
            for j in range(NG):
                lse4_ref[r, :, _cols(j)] = sl[j, pl.ds(r, TM // 4, stride=4), :]
        for r in range(16):
            for j in range(NG):
                lse16_ref[r, :, _cols(j)] = sl[j, pl.ds(r, TM // 16, stride=16), :]
        xo_ref[...] = x_ref[...] + _dot_nn(mixed_ref[...], w_ref[...])

    scr = pltpu.VMEM((NG, TM, 128), f32)
    return pl.pallas_call(
        body, grid=(S // TM,),
        in_specs=[_tile(D), _tile(PW), _tile(AW), _tile(AW), _p4(), _p4(), _p16(), _p16(), _layer(D, D, l)],
        out_specs=[_tile(D), _tile(D), _tile(AW), _tile(AW), _p4(), _p16()],
        out_shape=[SDS((S, D), f32), SDS((S, D), bf16), SDS((S, AW), f32), SDS((S, AW), f32),
                   SDS((4, S // 4, AW), f32), SDS((16, S // 16, AW), f32)],
        scratch_shapes=[scr] * 5,
        compiler_params=_CP, name="mix_out_fwd")(x, ypool, o1, l1, o4, l4, o16, l16, wout)


def _segsum64(t):
    lane = lax.broadcasted_iota(jnp.int32, t.shape, 1)
    for s in (1, 2, 4, 8, 16, 32):
        t = t + jnp.where((lane & s) != 0, pltpu.roll(t, s, 1), pltpu.roll(t, 128 - s, 1))
    return t


def _mix_out_bwd(dxo, o, wout, l=None):
    def body(dxo_ref, o_ref, w_ref, dxb_ref, dyp_ref, do1, do4, do16, dl1, dl4, dl16, sdo, sdl):
        dxb = dxo_ref[...].astype(bf16)
        dxb_ref[...] = dxb
        dm = _dot_nt(dxb, w_ref[...])
        dyp_ref[...] = dm[:, :PW]
        for j in range(NG):
            d = dm[:, PW + 128 * j: PW + 128 * (j + 1)]
            dl = _segsum64(d * o_ref[:, _cols(j)])
            do1[:, _cols(j)] = d.astype(bf16)
            dl1[:, _cols(j)] = dl
            sdo[j] = d
            sdl[j] = dl
        for r in range(4):
            for j in range(NG):
                do4[r, :, _cols(j)] = sdo[j, pl.ds(r, TM // 4, stride=4), :].astype(bf16)
                dl4[r, :, _cols(j)] = sdl[j, pl.ds(r, TM // 4, stride=4), :]
        for r in range(16):
            for j in range(NG):
                do16[r, :, _cols(j)] = sdo[j, pl.ds(r, TM // 16, stride=16), :].astype(bf16)
                dl16[r, :, _cols(j)] = sdl[j, pl.ds(r, TM // 16, stride=16), :]

    scr = pltpu.VMEM((NG, TM, 128), f32)
    return pl.pallas_call(
        body, grid=(S // TM,),
        in_specs=[_tile(D), _tile(AW), _layer(D, D, l)],
        out_specs=[_tile(D), _tile(PW), _tile(AW), _p4(), _p16(), _tile(AW), _p4(), _p16()],
        out_shape=[SDS((S, D), bf16), SDS((S, PW), f32),
                   SDS((S, AW), bf16), SDS((4, S // 4, AW), bf16), SDS((16, S // 16, AW), bf16),
                   SDS((S, AW), f32), SDS((4, S // 4, AW), f32), SDS((16, S // 16, AW), f32)],
        scratch_shapes=[scr] * 2,
        compiler_params=_CP, name="mix_out_bwd")(dxo, o, wout)


def _loss_head(x, g, target):
    def body(x_ref, g_ref, t_ref, dx_ref, loss_ref, dg_ref):
        g = g_ref[...]
        r, xh, y = _rms(x_ref[...], g)
        err = y - t_ref[...]
        dy = err * (1.0 / D)

        @pl.when(pl.program_id(0) == 0)
        def _():
            loss_ref[...] = jnp.zeros_like(loss_ref)
            dg_ref[...] = jnp.zeros_like(dg_ref)

        loss_ref[...] += jnp.broadcast_to(0.5 * jnp.sum(jnp.mean(err * err, axis=-1, keepdims=True)), (1, D))
        dg_ref[...] += jnp.sum(dy * xh, axis=0, keepdims=True)
        dx_ref[...] = _rms_bwd(dy, r, xh, g)

    return pl.pallas_call(
        body, grid=(S // TM,),
        in_specs=[_tile(D), _const((1, D)), _tile(D)],
        out_specs=[_tile(D), _const((1, D)), _const((1, D))],
        out_shape=[SDS((S, D), f32), SDS((1, D), f32), SDS((1, D), f32)],
        compiler_params=_CP, name="loss_head")(x, g, target)


def _peer(k):
    x, y, c = lax.axis_index("x"), lax.axis_index("y"), lax.axis_index("c")
    px = 1 - x if k & 4 else x
    py = 1 - y if k & 2 else y
    pc = 1 - c if k & 1 else c
    return (px, py, pc), 4 * px + 2 * py + pc


def _all_gather(shards):
    n = len(shards)

    def body(*refs):
        ins, outs = refs[:n], refs[n:2 * n]
        send_sems, recv_sems, local_sems = refs[2 * n:]
        me, me_idx = _peer(0)
        sibling, sib_idx = _peer(1)
        far = [_peer(k) for k in (4, 2, 6)]
        far_sib = [_peer(k) for k in (5, 3, 7)]

        def rows(t, idx):
            r = ins[t].shape[1]
            return outs[t].at[:, pl.ds(idx * r, r), :]

        def copy(k, t, idx, to, src=None):
            return pltpu.make_async_remote_copy(
                src_ref=rows(t, idx) if src is None else src, dst_ref=rows(t, idx),
                send_sem=send_sems.at[k, t], recv_sem=recv_sems.at[k, t], device_id=to, device_id_type=_MESH)

        mine = [pltpu.make_async_copy(ins[t], rows(t, me_idx), local_sems.at[t]) for t in range(n)]
        for cp in mine:
            cp.start()
        first = [copy(0, t, me_idx, sibling, src=ins[t]) for t in range(n)]
        for j, (dev, _) in enumerate(far):
            first += [copy(1 + j, t, me_idx, dev, src=ins[t]) for t in range(n)]
        for cp in first:
            cp.start()
        passed = []
        for j, (_, idx) in enumerate(far):
            for t in range(n):
                copy(1 + j, t, idx, me).wait_recv()
                cp = copy(4 + j, t, idx, sibling)
                cp.start()
                passed.append(cp)
        for t in range(n):
            copy(0, t, sib_idx, me).wait_recv()
        for j, (_, idx) in enumerate(far_sib):
            for t in range(n):
                copy(4 + j, t, idx, me).wait_recv()
        for cp in first + passed:
            cp.wait_send()
        for cp in mine:
            cp.wait()

    return pl.pallas_call(
        body, in_specs=[_ANY] * n, out_specs=[_ANY] * n,
        out_shape=[SDS((a.shape[0], NDEV * a.shape[1], a.shape[2]), a.dtype) for a in shards],
        scratch_shapes=[pltpu.SemaphoreType.DMA((7, n)), pltpu.SemaphoreType.DMA((7, n)),
                        pltpu.SemaphoreType.DMA((n,))],
        name="all_gather_weights")(*shards)


def _hbm(a):
    return pltpu.with_memory_space_constraint(a, pltpu.HBM)


def _rows(ref, idx):
    r = ref.shape[0] // NDEV
    return ref.at[pl.ds(idx * r, r), :]


def _row_copy(ref, idx, send_sem, recv_sem, to):
    return pltpu.make_async_remote_copy(src_ref=_rows(ref, idx), dst_ref=_rows(ref, idx), send_sem=send_sem,
                                        recv_sem=recv_sem, device_id=to, device_id_type=_MESH)


_TOKEN = SDS((8, 128), f32)
_FAR = (4, 2, 6)
_FAR_SIB = (5, 3, 7)


def _ag_start(lands, after, l):
    n = len(lands)

    def body(*refs):
        zones, send_sems, recv_sems, token = refs[:n], refs[n + 1], refs[n + 2], refs[-1]
        _, me_idx = _peer(0)
        for k, mask in enumerate((1,) + _FAR):
            for t in range(n):
                _row_copy(zones[t], me_idx, send_sems.at[k * n + t], recv_sems.at[k * n + t], _peer(mask)[0]).start()
        token[...] = jnp.zeros_like(token)

    outs = pl.pallas_call(
        body, name=f"ag_start_{l}", in_specs=[_HBM] * n + [_ANY],
        out_specs=(_SEM, _SEM, *[_HBM] * n, pl.BlockSpec(memory_space=pltpu.VMEM)),
        out_shape=(pltpu.SemaphoreType.DMA((4 * n,)), pltpu.SemaphoreType.DMA((4 * n,)),
                   *[pltpu.HBM(a.shape, a.dtype) for a in lands], _TOKEN),
        input_output_aliases={t: 2 + t for t in range(n)}, compiler_params=_CP_SPLIT)(
            *[_hbm(a) for a in lands], after)
    return outs[0], outs[1], list(outs[2:2 + n]), outs[-1]


def _ag_pass(lands, recv_sems, after, l):
    n = len(lands)

    def body(*refs):
        zones, recv_sems = refs[:n], refs[n]
        psend, precv, token = refs[n + 2], refs[n + 3], refs[-1]
        me, _ = _peer(0)
        sibling, _ = _peer(1)
        for j, mask in enumerate(_FAR):
            idx = _peer(mask)[1]
            for t in range(n):
                _row_copy(zones[t], idx, psend.at[j * n + t], recv_sems.at[(1 + j) * n + t], me).wait_recv()
                _row_copy(zones[t], idx, psend.at[j * n + t], precv.at[j * n + t], sibling).start()
        token[...] = jnp.zeros_like(token)

    outs = pl.pallas_call(
        body, name=f"ag_pass_{l}", in_specs=[_HBM] * n + [_SEM, _ANY],
        out_specs=(_SEM, _SEM, *[_HBM] * n, pl.BlockSpec(memory_space=pltpu.VMEM)),
        out_shape=(pltpu.SemaphoreType.DMA((3 * n,)), pltpu.SemaphoreType.DMA((3 * n,)),
                   *[pltpu.HBM(a.shape, a.dtype) for a in lands], _TOKEN),
        input_output_aliases={t: 2 + t for t in range(n)}, compiler_params=_CP_SPLIT)(*lands, recv_sems, after)
    return outs[0], outs[1], list(outs[2:2 + n]), outs[-1]


def _ag_wait(lands, send_sems, recv_sems, psend, precv, after, l):
    n = len(lands)

    def body(*refs):
        zones = refs[:n]
        send_sems, recv_sems, psend, precv = refs[n:n + 4]
        me, me_idx = _peer(0)
        sib_idx = _peer(1)[1]
        for k in range(4):
            for t in range(n):
                _row_copy(zones[t], me_idx, send_sems.at[k * n + t], recv_sems.at[k * n + t], me).wait_send()
        for t in range(n):
            _row_copy(zones[t], sib_idx, send_sems.at[t], recv_sems.at[t], me).wait_recv()
        for j in range(3):
            mine, theirs = _peer(_FAR[j])[1], _peer(_FAR_SIB[j])[1]
            for t in range(n):
                _row_copy(zones[t], mine, psend.at[j * n + t], precv.at[j * n + t], me).wait_send()
                _row_copy(zones[t], theirs, psend.at[j * n + t], precv.at[j * n + t], me).wait_recv()

    outs = pl.pallas_call(
        body, name=f"ag_wait_{l}", in_specs=[_HBM] * n + [_SEM] * 4 + [_ANY], out_specs=tuple([_HBM] * n),
        out_shape=tuple(pltpu.HBM(a.shape, a.dtype) for a in lands),
        input_output_aliases={t: t for t in range(n)}, compiler_params=_CP_SPLIT)(
            *lands, send_sems, recv_sems, psend, precv, after)
    return list(outs)


def _xchg_src(ref, slot_ref, idx):
    return _rows(ref, idx) if ref.shape[0] == NDEV * slot_ref.shape[1] else ref


def _rs_start(srcs, slots, after, tag):
    n = len(srcs)

    def body(*refs):
        src, slot = refs[:n], refs[n:2 * n]
        send_sems, recv_sems, token = refs[2 * n + 1], refs[2 * n + 2], refs[-1]
        _, me_idx = _peer(0)
        for k in range(1, NDEV):
            dev, idx = _peer(k)
            for t in range(n):
                pltpu.make_async_remote_copy(
                    src_ref=_xchg_src(src[t], slot[t], idx), dst_ref=slot[t].at[me_idx],
                    send_sem=send_sems.at[(k - 1) * n + t], recv_sem=recv_sems.at[(k - 1) * n + t],
                    device_id=dev, device_id_type=_MESH).start()
        token[...] = jnp.zeros_like(token)

    outs = pl.pallas_call(
        body, name=f"rs_start_{tag}", in_specs=[_HBM] * (2 * n) + [_ANY],
        out_specs=(_SEM, _SEM, *[_HBM] * (2 * n), pl.BlockSpec(memory_space=pltpu.VMEM)),
        out_shape=(pltpu.SemaphoreType.DMA(((NDEV - 1) * n,)), pltpu.SemaphoreType.DMA(((NDEV - 1) * n,)),
                   *[pltpu.HBM(a.shape, a.dtype) for a in list(srcs) + list(slots)], _TOKEN),
        input_output_aliases={t: 2 + t for t in range(2 * n)}, compiler_params=_CP_SPLIT)(
            *[_hbm(a) for a in list(srcs) + list(slots)], after)
    return outs[0], outs[1], list(outs[2:2 + n]), list(outs[2 + n:2 + 2 * n]), outs[-1]


def _rs_wait(srcs, slots, send_sems, recv_sems, after, tag):
    n = len(srcs)

    def body(*refs):
        src, slot, send_sems, recv_sems = refs[:n], refs[n:2 * n], refs[2 * n], refs[2 * n + 1]
        me, _ = _peer(0)
        for k in range(1, NDEV):
            idx = _peer(k)[1]
            for t in range(n):
                cp = pltpu.make_async_remote_copy(
                    src_ref=_xchg_src(src[t], slot[t], idx), dst_ref=slot[t].at[idx],
                    send_sem=send_sems.at[(k - 1) * n + t], recv_sem=recv_sems.at[(k - 1) * n + t],
                    device_id=me, device_id_type=_MESH)
                cp.wait_send()
                cp.wait_recv()

    outs = pl.pallas_call(
        body, name=f"rs_wait_{tag}", in_specs=[_HBM] * (2 * n) + [_SEM, _SEM, _ANY],
        out_specs=tuple([_HBM] * (2 * n)),
        out_shape=tuple(pltpu.HBM(a.shape, a.dtype) for a in list(srcs) + list(slots)),
        input_output_aliases={t: t for t in range(2 * n)}, compiler_params=_CP_SPLIT)(
            *srcs, *slots, send_sems, recv_sems, after)
    return list(outs[:n]), list(outs[n:])


def _sum_slots(slots, rb):
    r = slots.shape[1]

    def body(s_ref, o_ref):
        acc = s_ref[0].astype(f32)
        for s in range(1, NDEV):
            acc = acc + s_ref[s].astype(f32)
        o_ref[...] = acc

    return pl.pallas_call(
        body, grid=(r // rb,),
        in_specs=[pl.BlockSpec((NDEV, rb, D), lambda i: (0, i, 0))],
        out_specs=pl.BlockSpec((rb, D), lambda i: (i, 0)),
        out_shape=SDS((r, D), f32), compiler_params=_CP, name="sum_slots")(slots)


def _adamw(w, g, m, v):
    shape = w.shape
    cols = shape[-1]
    rows = w.size // cols
    rb = rows
    for cand in (512, 256, 128, 64, 32, 16, 8):
        if rows % cand == 0 and rows > cand:
            rb = cand
            break

    def body(w_ref, g_ref, m_ref, v_ref, d_ref, mo_ref, vo_ref):
        d_ref[...], mo_ref[...], vo_ref[...] = _adamw_math(w_ref[...], g_ref[...], m_ref[...], v_ref[...])

    spec = pl.BlockSpec((rb, cols), lambda i: (i, 0))
    outs = pl.pallas_call(
        body, grid=(rows // rb,), in_specs=[spec] * 4, out_specs=[spec] * 3,
        out_shape=[SDS((rows, cols), f32)] * 3, compiler_params=_CP, name="adamw")(
            *(a.reshape(rows, cols) for a in (w, g, m, v)))
    return tuple(o.reshape(shape) for o in outs)


def _adamw_math(w, g, m, v):
    m = ADAM_B1 * m + (1.0 - ADAM_B1) * g
    v = ADAM_B2 * v + (1.0 - ADAM_B2) * (g * g)
    m_hat = m / (1.0 - ADAM_B1 ** ADAM_STEP)
    v_hat = v / (1.0 - ADAM_B2 ** ADAM_STEP)
    return -ADAM_LR * (m_hat / (jnp.sqrt(v_hat) + ADAM_EPS) + ADAM_WD * w), m, v


def _reduce_adamw(acc, me, full, slots, w, m, v, l):
    _, r, _ = w.shape
    rb = r // 2 if r > 128 else r

    def body(me_ref, full_ref, slots_ref, w_ref, m_ref, v_ref, *refs):
        go_ref, d_ref, mo_ref, vo_ref = refs[-4:]
        own = full_ref[...].astype(f32)
        g = None
        for s in range(NDEV):
            part = jnp.where(me_ref[0] == s, own, slots_ref[s].astype(f32))
            g = part if g is None else g + part
        go_ref[...] = g
        d_ref[...], mo_ref[...], vo_ref[...] = _adamw_math(w_ref[...], g, m_ref[...], v_ref[...])

    steps = r // rb
    lay = pl.BlockSpec((None, rb, D), lambda i, me_ref: (l, i, 0))
    n_acc = 0 if acc is None else 4
    grid_spec = pltpu.PrefetchScalarGridSpec(
        num_scalar_prefetch=1, grid=(steps,),
        in_specs=[pl.BlockSpec((rb, D), lambda i, me_ref: (me_ref[0] * steps + i, 0)),
                  pl.BlockSpec((NDEV, rb, D), lambda i, me_ref: (0, i, 0)), lay, lay, lay] + [_ANY] * n_acc,
        out_specs=[lay] * 4)
    outs = pl.pallas_call(
        body, grid_spec=grid_spec, out_shape=[SDS(w.shape, f32)] * 4,
        input_output_aliases={6 + j: j for j in range(n_acc)},
        compiler_params=_CP, name="reduce_adamw")(me, full, slots, w, m, v, *(() if acc is None else acc))
    return tuple(outs)


_BIG = ("ffn1_w_gate", "ffn1_w_up", "ffn1_w_down", "w_in", "w_out", "ffn2_w_gate", "ffn2_w_up", "ffn2_w_down")
_TRANSPOSED = ("ffn1_w_gate", "ffn1_w_up", "w_in", "ffn2_w_gate", "ffn2_w_up")

def _block_diag(pool_w):
    out = jnp.zeros((L, PW, PW), pool_w.dtype)
    for gi in range(4):
        out = out.at[:, 64 * gi:64 * (gi + 1), 64 * gi:64 * (gi + 1)].set(pool_w[:, gi])
    return out


def kernel(x, positions, ffn1_norm, ffn1_w_gate, ffn1_w_up, ffn1_w_down, mix_norm, w_in, pool_w, pool_scale, w_out, ffn2_norm, ffn2_w_gate, ffn2_w_up, ffn2_w_down, final_norm, loss_target, m_ffn1_norm, m_ffn1_w_gate, m_ffn1_w_up, m_ffn1_w_down, m_mix_norm, m_w_in, m_pool_w, m_pool_scale, m_w_out, m_ffn2_norm, m_ffn2_w_gate, m_ffn2_w_up, m_ffn2_w_down, m_final_norm, v_ffn1_norm, v_ffn1_w_gate, v_ffn1_w_up, v_ffn1_w_down, v_mix_norm, v_w_in, v_pool_w, v_pool_scale, v_w_out, v_ffn2_norm, v_ffn2_w_gate, v_ffn2_w_up, v_ffn2_w_down, v_final_norm):
    weights = dict(ffn1_norm=ffn1_norm, ffn1_w_gate=ffn1_w_gate, ffn1_w_up=ffn1_w_up, ffn1_w_down=ffn1_w_down,
                   mix_norm=mix_norm, w_in=w_in, pool_w=pool_w, pool_scale=pool_scale, w_out=w_out,
                   ffn2_norm=ffn2_norm, ffn2_w_gate=ffn2_w_gate, ffn2_w_up=ffn2_w_up, ffn2_w_down=ffn2_w_down,
                   final_norm=final_norm)
    moms = dict(ffn1_norm=m_ffn1_norm, ffn1_w_gate=m_ffn1_w_gate, ffn1_w_up=m_ffn1_w_up, ffn1_w_down=m_ffn1_w_down,
                mix_norm=m_mix_norm, w_in=m_w_in, pool_w=m_pool_w, pool_scale=m_pool_scale, w_out=m_w_out,
                ffn2_norm=m_ffn2_norm, ffn2_w_gate=m_ffn2_w_gate, ffn2_w_up=m_ffn2_w_up, ffn2_w_down=m_ffn2_w_down,
                final_norm=m_final_norm)
    vels = dict(ffn1_norm=v_ffn1_norm, ffn1_w_gate=v_ffn1_w_gate, ffn1_w_up=v_ffn1_w_up, ffn1_w_down=v_ffn1_w_down,
                mix_norm=v_mix_norm, w_in=v_w_in, pool_w=v_pool_w, pool_scale=v_pool_scale, w_out=v_w_out,
                ffn2_norm=v_ffn2_norm, ffn2_w_gate=v_ffn2_w_gate, ffn2_w_up=v_ffn2_w_up, ffn2_w_down=v_ffn2_w_down,
                final_norm=v_final_norm)
    names = list(weights)

    me_idx = 4 * lax.axis_index("x") + 2 * lax.axis_index("y") + lax.axis_index("c")

    tr = lambda w: jnp.swapaxes(w, 1, 2).astype(bf16)
    shards = [tr(weights[nm]) if nm in _TRANSPOSED else weights[nm].astype(bf16) for nm in _BIG]

    def landing_zones(l, which):
        return [lax.dynamic_update_slice(lax.empty((NDEV * shards[t].shape[1], D), bf16), shards[t][l],
                                         (me_idx * shards[t].shape[1], 0)) for t in which]

    g_ffn1 = [ffn1_norm[l].reshape(1, D) for l in range(L)]
    g_mix = [mix_norm[l].reshape(1, D) for l in range(L)]
    g_ffn2 = [ffn2_norm[l].reshape(1, D) for l in range(L)]
    wbd_all = _block_diag(pool_w).astype(bf16)
    wbd = [wbd_all[l] for l in range(L)]
    pscale = [pool_scale[l].reshape(1, PW) for l in range(L)]
    tabs = _rope_tables(positions)
    flat = lambda a: a.reshape(S, AW)
    r4 = lambda a: a.reshape(4, S // 4, AW)
    r16 = lambda a: a.reshape(16, S // 16, AW)

    first, rest, whole = (0, 1, 2, 3), (4, 5, 6, 7), tuple(range(8))
    head = [a.reshape(a.shape[1], D) for a in _all_gather([shards[t][0:1] for t in first])]
    chain = {0: _ag_start(landing_zones(0, rest), head[0], "0"), 1: _ag_start(landing_zones(1, whole), head[0], "1")}
    gathered = [None] * L
    xs = x.reshape(S, D)
    saved = []
    for l in range(L):
        ga, gb = g_ffn1[l], g_ffn2[l]
        if l == 0:
            gt1, ut1, dn1, wint = head
            ga = ga + chain[0][3][0, 0] + chain[1][3][0, 0]
        else:
            gt1, ut1, dn1, wint, wout, gt2, ut2, dn2 = gathered[l]
            if l + 1 < L:
                chain[l + 1] = _ag_start(landing_zones(l + 1, whole), xs, str(l + 1))
                ga = ga + chain[l + 1][3][0, 0]
        x0 = xs
        x1, gate1, up1 = _ffn_fwd(x0, ga, gt1, ut1, dn1)
        hmix, vp, q1, k1, v1, q4, k4, v4, q16, k16, v16 = _mix_in_fwd(x1, g_mix[l], wint, tabs)
        q4, k4, v4, q16, k16, v16 = map(flat, (q4, k4, v4, q16, k16, v16))
        ypool, diff = _pool_fwd(vp, wbd[l], pscale[l])
        o1, l1 = _attn_fwd(q1, k1, v1, S)
        o4, l4 = _attn_fwd(q4, k4, v4, S // 4)
        if l == 0:
            send_sems, recv_sems, zones, _ = chain[0]
            psend, precv, zones, _ = _ag_pass(zones, recv_sems, o4, "0")
        o16, l16 = _attn_fwd(q16, k16, v16, S // 16)
        if l == 0:
            wout, gt2, ut2, dn2 = _ag_wait(zones, send_sems, recv_sems, psend, precv, o16, "0")
            gathered[0] = head + [wout, gt2, ut2, dn2]
        elif l + 1 < L:
            send_sems, recv_sems, zones, _ = chain[l + 1]
            psend, precv, zones, token = _ag_pass(zones, recv_sems, o16, str(l + 1))
            gb = gb + token[0, 0]
        x2, mixed, o, lse1, lse4, lse16 = _mix_out_fwd(x1, ypool, o1, l1, r4(o4), r4(l4), r16(o16), r16(l16), wout)
        if l == 0:
            send_sems, recv_sems, zones, _ = chain[1]
            psend, precv, zones, token = _ag_pass(zones, recv_sems, x2, "1")
            gb = gb + token[0, 0]
        x3, gate2, up2 = _ffn_fwd(x2, gb, gt2, ut2, dn2)
        if l + 1 < L:
            gathered[l + 1] = _ag_wait(zones, send_sems, recv_sems, psend, precv, x3, str(l + 1))
        saved.append(dict(x0=x0, x1=x1, x2=x2, gate1=gate1, up1=up1, gate2=gate2, up2=up2, hmix=hmix, diff=diff,
                          qkv=((q1, k1, v1), (q4, k4, v4), (q16, k16, v16)), mixed=mixed, o=o,
                          lse=(lse1, flat(lse4), flat(lse16))))
        xs = x3

    dx, loss_part, d_final = _loss_head(xs, final_norm.reshape(1, D), loss_target.reshape(S, D))

    d_norm = {nm: [None] * L for nm in ("ffn1_norm", "mix_norm", "ffn2_norm")}
    d_poolw, d_pscale = [None] * L, [None] * L
    group_a = ("ffn2_w_gate", "ffn2_w_up", "ffn2_w_down", "w_out")
    group_b = ("ffn1_w_gate", "ffn1_w_up", "ffn1_w_down", "w_in")
    acc = {}

    as_rows = lambda a, nm: jnp.swapaxes(a, 1, 2) if nm in _TRANSPOSED else a
    w_rows = {nm: as_rows(weights[nm], nm) for nm in _BIG}
    m_rows = {nm: as_rows(moms[nm], nm) for nm in _BIG}
    v_rows = {nm: as_rows(vels[nm], nm) for nm in _BIG}
    me_arr = me_idx.reshape(1).astype(jnp.int32)

    def exchange(full, group, after, tag, extra=None):
        srcs = [full[nm] for nm in group]
        slots = [lax.empty((NDEV, g.shape[0] // NDEV, D), bf16) for g in srcs]
        if extra is not None:
            srcs, slots = srcs + [extra[0]], slots + [extra[1]]
        ssem, rsem, srcs, slots, token = _rs_start(srcs, slots, after, tag)
        return (srcs, slots, ssem, rsem, tag), token

    def update(l, group, flight, after):
        srcs, slots, ssem, rsem, tag = flight
        srcs, slots = _rs_wait(srcs, slots, ssem, rsem, after, tag)
        for nm, full_g, slots_g in zip(group, srcs, slots):
            acc[nm] = _reduce_adamw(acc.get(nm), me_arr, full_g, slots_g, w_rows[nm], m_rows[nm], v_rows[nm], l)
        return acc[group[-1]][0], slots

    flights = {}
    token_b = None
    for l in reversed(range(L)):
        sv = saved[l]
        gt1, ut1, dn1, wint, wout, gt2, ut2, dn2 = gathered[l]
        gb = g_ffn2[l] if token_b is None else g_ffn2[l] + token_b[0, 0]
        full = {}
        dx, dgate, dup, h, dy, d_norm["ffn2_norm"][l] = _ffn_bwd_d(sv["x2"], gb, sv["gate2"], sv["up2"], dx, gt2, ut2, dn2)
        full["ffn2_w_gate"], full["ffn2_w_up"], full["ffn2_w_down"] = _ffn_bwd_w(h, dy, sv["gate2"], sv["up2"], dgate, dup)

        dxb, dyp, do1, do4, do16, dl1, dl4, dl16 = _mix_out_bwd(dx, sv["o"], wout)
        full["w_out"] = _wgrad(sv["mixed"], dxb)
        flights[l, "a"], token_a = exchange(full, group_a, dxb, f"a{l}")
        dvp, dwbd, d_pscale[l] = _pool_bwd(dyp, sv["diff"], wbd[l], pscale[l] + token_a[0, 0])
        d_poolw[l] = jnp.stack([dwbd[64 * gi:64 * (gi + 1), 64 * gi:64 * (gi + 1)] for gi in range(4)])
        dos, dls = (do1, flat(do4), flat(do16)), (dl1, flat(dl4), flat(dl16))
        dqkv = []
        for b, lc in enumerate((S, S // 4, S // 16)):
            qb, kb, vb = sv["qkv"][b]
            dqkv.append(_attn_bwd(qb, kb, vb, dos[b], sv["lse"][b], dls[b], lc))
        d4 = tuple(r4(a) for a in dqkv[1])
        d16 = tuple(r16(a) for a in dqkv[2])
        dx, dproj, d_norm["mix_norm"][l] = _mix_in_bwd(dx, sv["x1"], g_mix[l], wint, tabs, dvp, dqkv[0], d4, d16)
        full["w_in"] = _wgrad(dproj, sv["hmix"])

        dx, dgate, dup, h, dy, d_norm["ffn1_norm"][l] = _ffn_bwd_d(sv["x0"], g_ffn1[l], sv["gate1"], sv["up1"], dx, gt1, ut1, dn1)
        full["ffn1_w_gate"], full["ffn1_w_up"], full["ffn1_w_down"] = _ffn_bwd_w(h, dy, sv["gate1"], sv["up1"], dgate, dup)

        after = dx
        if l + 1 < L and l + 1 >= 2:
            after, _ = update(l + 1, group_a, flights.pop((l + 1, "a")), after)
            after, _ = update(l + 1, group_b, flights.pop((l + 1, "b")), after)
        if l > 0:
            flights[l, "b"], token_b = exchange(full, group_b, after, f"b{l}")

    pad8 = lambda a: jnp.pad(a, ((0, 8 - a.shape[0]), (0, 0)))
    misc = jnp.concatenate([d_final, jnp.concatenate(d_pscale, axis=1), loss_part], axis=0)
    small = jnp.concatenate(
        [pad8(jnp.concatenate(d_norm[nm], axis=0)) for nm in ("ffn1_norm", "mix_norm", "ffn2_norm")]
        + [pad8(misc), jnp.stack(d_poolw).reshape(L * 16, D)], axis=0)
    small_slots = lax.dynamic_update_slice(lax.empty((NDEV, SMALL_ROWS, D), f32), small[None], (me_idx, 0, 0))
    flights[0, "b"], token_b = exchange(full, group_b, dx, "b0", extra=(small, small_slots))

    after = token_b
    for key in [(1, "a"), (1, "b"), (0, "a")]:
        after, _ = update(key[0], group_a if key[1] == "a" else group_b, flights.pop(key), after)
    after, slots_b0 = update(0, group_b, flights.pop((0, "b")), after)

    sm = _sum_slots(slots_b0[-1], SMALL_ROWS)
    grads = {}
    grads["ffn1_norm"], grads["mix_norm"], grads["ffn2_norm"] = sm[0:L], sm[8:8 + L], sm[16:16 + L]
    grads["final_norm"] = sm[24]
    grads["pool_scale"] = sm[25].reshape(L, PW)
    grads["pool_w"] = sm[32:32 + L * 16].reshape(L, 4, 64, 64)
    loss = sm[26, 0]
    upd = {nm: _adamw(weights[nm], grads[nm], moms[nm], vels[nm]) for nm in names if nm not in _BIG}
    for nm in _BIG:
        grads[nm], upd[nm] = as_rows(acc[nm][0], nm), tuple(as_rows(a, nm) for a in acc[nm][1:])
    return (loss, dx.reshape(1, S, D), *[grads[nm] for nm in names], *[upd[nm][0] for nm in names],
            *[upd[nm][1] for nm in names], *[upd[nm][2] for nm in names])
```

```python
import jax
import jax.numpy as jnp
from jax import lax
from jax.experimental import pallas as pl
from jax.experimental.pallas import tpu as pltpu

f32 = jnp.float32
bf16 = jnp.bfloat16
SDS = jax.ShapeDtypeStruct

D = 1024
S = 2048
F = 2816
L = 4
PW = 256
AW = 768
PROJ = PW + 3 * AW
NDEV = 8
TM = 256
QB = 128
HALF = 64
NG = AW // 128
NORM_EPS = 1e-6
MASK_VALUE = -1e30
ROPE_THETA = 500000.0
ADAM_LR, ADAM_B1, ADAM_B2, ADAM_EPS, ADAM_WD, ADAM_STEP = 0.001, 0.9, 0.999, 1e-08, 0.01, 10
POOL_WINDOWS = (2, 4, 8, 16)
PAD = 8
SMALL_ROWS = 96
VMEM_LIMIT = 56 * 1024 * 1024

_CP = pltpu.CompilerParams(vmem_limit_bytes=VMEM_LIMIT)
_ANY = pl.BlockSpec(memory_space=pl.ANY)
_HBM = pl.BlockSpec(memory_space=pltpu.HBM)
_SEM = pl.BlockSpec(memory_space=pltpu.SEMAPHORE)
_MESH = pl.DeviceIdType.MESH
_CP_SPLIT = pltpu.CompilerParams(has_side_effects=pltpu.SideEffectType.DATAFLOW_SIDE_EFFECTING)


def _dot_nn(a, b):
    return lax.dot_general(a, b, (((1,), (0,)), ((), ())), preferred_element_type=f32)


def _dot_nt(a, b):
    return lax.dot_general(a, b, (((1,), (1,)), ((), ())), preferred_element_type=f32)


def _dot_tn(a, b):
    return lax.dot_general(a, b, (((0,), (0,)), ((), ())), preferred_element_type=f32)


def _rms(x, g):
    r = lax.rsqrt(jnp.mean(x * x, axis=-1, keepdims=True) + NORM_EPS)
    xh = x * r
    return r, xh, xh * g


def _rms_bwd(dh, r, xh, g):
    dxh = dh * g
    return r * (dxh - xh * jnp.mean(dxh * xh, axis=-1, keepdims=True))


def _tile(cols):
    return pl.BlockSpec((TM, cols), lambda i: (i, 0))


def _const(shape):
    return pl.BlockSpec(shape, lambda i: (0,) * len(shape))


def _layer(rows, cols, l=None):
    return pl.BlockSpec((rows, cols), lambda i: (0, 0), pipeline_mode=pl.Buffered(1))


def _p4():
    return pl.BlockSpec((4, TM // 4, AW), lambda i: (0, i, 0))


def _p16():
    return pl.BlockSpec((16, TM // 16, AW), lambda i: (0, i, 0))


def _cols(j):
    return slice(128 * j, 128 * (j + 1))


def _ffn_fwd(x, g, gt, ut, dn, l=None):
    def body(x_ref, g_ref, gt_ref, ut_ref, dn_ref, xo_ref, gate_ref, up_ref):
        x = x_ref[...]
        _, _, hn = _rms(x, g_ref[...])
        h = hn.astype(bf16)
        gate = _dot_nt(h, gt_ref[...])
        up = _dot_nt(h, ut_ref[...])
        gate_ref[...] = gate.astype(bf16)
        up_ref[...] = up.astype(bf16)
        a = (gate * jax.nn.sigmoid(gate) * up).astype(bf16)
        xo_ref[...] = x + 0.5 * _dot_nn(a, dn_ref[...])

    return pl.pallas_call(
        body, grid=(S // TM,),
        in_specs=[_tile(D), _layer(1, D, l), _layer(F, D, l), _layer(F, D, l), _layer(F, D, l)],
        out_specs=[_tile(D), _tile(F), _tile(F)],
        out_shape=[SDS((S, D), f32), SDS((S, F), bf16), SDS((S, F), bf16)],
        compiler_params=_CP, name="ffn_fwd")(x, g, gt, ut, dn)


def _ffn_bwd_d(x, g, gate, up, dxo, gt, ut, dn, l=None):
    def body(x_ref, g_ref, gate_ref, up_ref, dxo_ref, gt_ref, ut_ref, dn_ref,
             dx_ref, dgate_ref, dup_ref, h_ref, dy_ref, dg_ref):
        x = x_ref[...]
        g = g_ref[...]
        r, xh, hn = _rms(x, g)
        h_ref[...] = hn.astype(bf16)
        dxo = dxo_ref[...]
        dy = (0.5 * dxo).astype(bf16)
        dy_ref[...] = dy
        da = _dot_nt(dy, dn_ref[...])
        gate = gate_ref[...].astype(f32)
        up = up_ref[...].astype(f32)
        sg = jax.nn.sigmoid(gate)
        dgate = (da * up * (sg * (1.0 + gate * (1.0 - sg)))).astype(bf16)
        dup = (da * (gate * sg)).astype(bf16)
        dgate_ref[...] = dgate
        dup_ref[...] = dup
        dh = _dot_nn(dgate, gt_ref[...]) + _dot_nn(dup, ut_ref[...])

        @pl.when(pl.program_id(0) == 0)
        def _():
            dg_ref[...] = jnp.zeros_like(dg_ref)

        dg_ref[...] += jnp.sum(dh * xh, axis=0, keepdims=True)
        dx_ref[...] = dxo + _rms_bwd(dh, r, xh, g)

    return pl.pallas_call(
        body, grid=(S // TM,),
        in_specs=[_tile(D), _layer(1, D, l), _tile(F), _tile(F), _tile(D),
                  _layer(F, D, l), _layer(F, D, l), _layer(F, D, l)],
        out_specs=[_tile(D), _tile(F), _tile(F), _tile(D), _tile(D), _const((1, D))],
        out_shape=[SDS((S, D), f32), SDS((S, F), bf16), SDS((S, F), bf16), SDS((S, D), bf16),
                   SDS((S, D), bf16), SDS((1, D), f32)],
        compiler_params=_CP, name="ffn_bwd_d")(x, g, gate, up, dxo, gt, ut, dn)


def _ffn_bwd_w(h, dy, gate, up, dgate, dup):
    fc = 256

    def body(h_ref, dy_ref, gate_ref, up_ref, dgate_ref, dup_ref, dgt_ref, dut_ref, ddn_ref):
        gate = gate_ref[...].astype(f32)
        a = (gate * jax.nn.sigmoid(gate) * up_ref[...].astype(f32)).astype(bf16)
        ddn_ref[...] = _dot_tn(a, dy_ref[...]).astype(bf16)
        h = h_ref[...]
        dgt_ref[...] = _dot_tn(dgate_ref[...], h).astype(bf16)
        dut_ref[...] = _dot_tn(dup_ref[...], h).astype(bf16)

    col = pl.BlockSpec((S, fc), lambda j: (0, j))
    row = pl.BlockSpec((fc, D), lambda j: (j, 0))
    full = pl.BlockSpec((S, D), lambda j: (0, 0))
    return pl.pallas_call(
        body, grid=(F // fc,),
        in_specs=[full, full, col, col, col, col],
        out_specs=[row, row, row],
        out_shape=[SDS((F, D), bf16)] * 3,
        compiler_params=_CP, name="ffn_bwd_w")(h, dy, gate, up, dgate, dup)


def _wgrad(a, b):
    m, n = a.shape[1], b.shape[1]
    mc = 256

    def body(a_ref, b_ref, o_ref):
        o_ref[...] = _dot_tn(a_ref[...], b_ref[...]).astype(bf16)

    return pl.pallas_call(
        body, grid=(m // mc,),
        in_specs=[pl.BlockSpec((S, mc), lambda j: (0, j)), pl.BlockSpec((S, n), lambda j: (0, 0))],
        out_specs=pl.BlockSpec((mc, n), lambda j: (j, 0)),
        out_shape=SDS((m, n), bf16),
        compiler_params=_CP, name="wgrad")(a, b)


def _rope(t, c, sn, sp):
    return t * c + pltpu.roll(t, 120, 1) * sn + pltpu.roll(t, 8, 1) * sp


def _rope_bwd(d, c, sn, sp):
    return d * c + pltpu.roll(d * sn, 8, 1) + pltpu.roll(d * sp, 120, 1)


def _rope_tables(positions):
    inv_freq = ROPE_THETA ** (-jnp.arange(0, 16, 2, dtype=f32) / 16)
    ang = positions.reshape(S, 1).astype(f32) * inv_freq
    cos, sin = jnp.cos(ang), jnp.sin(ang)
    one = jnp.ones((S, 48), f32)
    zero8 = jnp.zeros((S, 8), f32)
    zero48 = jnp.zeros((S, 48), f32)
    c = jnp.concatenate([cos, cos, one], axis=1)
    sn = jnp.concatenate([-sin, zero8, zero48], axis=1)
    sp = jnp.concatenate([zero8, sin, zero48], axis=1)
    return tuple(jnp.concatenate([t, t], axis=1) for t in (c, sn, sp))


def _mix_in_fwd(x, g, wint, tabs, l=None):
    def body(x_ref, g_ref, w_ref, c_ref, sn_ref, sp_ref,
             h_ref, vp_ref, q1, k1, v1, q4, k4, v4, q16, k16, v16, scr):
        _, _, hn = _rms(x_ref[...], g_ref[...])
        h = hn.astype(bf16)
        h_ref[...] = h
        proj = _dot_nt(h, w_ref[...])
        vp_ref[...] = proj[:, :PW]
        c, sn, sp = c_ref[...], sn_ref[...], sp_ref[...]
        for kind, (o1, o4, o16) in enumerate(((q1, q4, q16), (k1, k4, k16), (v1, v4, v16))):
            for j in range(NG):
                t = proj[:, PW + kind * AW + 128 * j: PW + kind * AW + 128 * (j + 1)]
                if kind == 0:
                    t = _rope(t, c, sn, sp) * 0.125
                elif kind == 1:
                    t = _rope(t, c, sn, sp)
                scr[j] = t
                o1[:, _cols(j)] = t.astype(bf16)
            for r in range(4):
                for j in range(NG):
                    o4[r, :, _cols(j)] = scr[j, pl.ds(r, TM // 4, stride=4), :].astype(bf16)
            for r in range(16):
                for j in range(NG):
                    o16[r, :, _cols(j)] = scr[j, pl.ds(r, TM // 16, stride=16), :].astype(bf16)

    nat, d4, d16 = SDS((S, AW), bf16), SDS((4, S // 4, AW), bf16), SDS((16, S // 16, AW), bf16)
    return pl.pallas_call(
        body, grid=(S // TM,),
        in_specs=[_tile(D), _layer(1, D, l), _layer(PROJ, D, l), _tile(128), _tile(128), _tile(128)],
        out_specs=[_tile(D), _tile(PW)] + [_tile(AW)] * 3 + [_p4()] * 3 + [_p16()] * 3,
        out_shape=[SDS((S, D), bf16), SDS((S, PW), f32)] + [nat] * 3 + [d4] * 3 + [d16] * 3,
        scratch_shapes=[pltpu.VMEM((NG, TM, 128), f32)],
        compiler_params=_CP, name="mix_in_fwd")(x, g, wint, *tabs)


def _mix_in_bwd(dxo, x, g, wint, tabs, dvp, d1, d4, d16, l=None):
    def body(dxo_ref, x_ref, g_ref, w_ref, c_ref, sn_ref, sp_ref, dvp_ref,
             dq1, dk1, dv1, dq4, dk4, dv4, dq16, dk16, dv16,
             dx_ref, dproj_ref, dg_ref, s4, s16):
        c, sn, sp = c_ref[...], sn_ref[...], sp_ref[...]
        dproj_ref[:, :PW] = dvp_ref[...].astype(bf16)
        for kind, (a1, a4, a16) in enumerate(((dq1, dq4, dq16), (dk1, dk4, dk16), (dv1, dv4, dv16))):
            for r in range(4):
                for j in range(NG):
                    s4[j, pl.ds(r, TM // 4, stride=4), :] = a4[r, :, _cols(j)]
            for r in range(16):
                for j in range(NG):
                    s16[j, pl.ds(r, TM // 16, stride=16), :] = a16[r, :, _cols(j)]
            for j in range(NG):
                t = a1[:, _cols(j)] + s4[j] + s16[j]
                if kind == 0:
                    t = _rope_bwd(t * 0.125, c, sn, sp)
                elif kind == 1:
                    t = _rope_bwd(t, c, sn, sp)
                dproj_ref[:, PW + kind * AW + 128 * j: PW + kind * AW + 128 * (j + 1)] = t.astype(bf16)
        g = g_ref[...]
        r_, xh, _ = _rms(x_ref[...], g)
        dh = _dot_nn(dproj_ref[...], w_ref[...])

        @pl.when(pl.program_id(0) == 0)
        def _():
            dg_ref[...] = jnp.zeros_like(dg_ref)

        dg_ref[...] += jnp.sum(dh * xh, axis=0, keepdims=True)
        dx_ref[...] = dxo_ref[...] + _rms_bwd(dh, r_, xh, g)

    return pl.pallas_call(
        body, grid=(S // TM,),
        in_specs=[_tile(D), _tile(D), _layer(1, D, l), _layer(PROJ, D, l), _tile(128), _tile(128), _tile(128),
                  _tile(PW)] + [_tile(AW)] * 3 + [_p4()] * 3 + [_p16()] * 3,
        out_specs=[_tile(D), _tile(PROJ), _const((1, D))],
        out_shape=[SDS((S, D), f32), SDS((S, PROJ), bf16), SDS((1, D), f32)],
        scratch_shapes=[pltpu.VMEM((NG, TM, 128), f32), pltpu.VMEM((NG, TM, 128), f32)],
        compiler_params=_CP, name="mix_in_bwd")(dxo, x, g, wint, *tabs, dvp, *d1, *d4, *d16)


def _pool_sums(pad_ref, base, rows, adjoint):
    lane_group = lax.broadcasted_iota(jnp.int32, (rows, PW), 1) // 64
    sign = -1 if adjoint else 1

    def sh(o):
        return pad_ref[pl.ds(PAD + base + sign * o, rows), :]

    out = None
    acc = None
    lo, hi = 0, 0
    for gi, w in enumerate(POOL_WINDOWS):
        for o in list(range(-(w // 2), lo)) + list(range(hi, w - w // 2)):
            acc = sh(o) if acc is None else acc + sh(o)
        lo, hi = -(w // 2), w - w // 2
        out = acc if out is None else jnp.where(lane_group >= gi, acc, out)
    return out


def _pool_counts(base, rows):
    pos = base + lax.broadcasted_iota(jnp.int32, (rows, PW), 0)
    lane_group = lax.broadcasted_iota(jnp.int32, (rows, PW), 1) // 64
    cnt = None
    for gi, w in enumerate(POOL_WINDOWS):
        lo = jnp.maximum(pos - w // 2, 0)
        hi = jnp.minimum(pos + w - 1 - w // 2, S - 1)
        c = (hi - lo + 1).astype(f32)
        cnt = c if cnt is None else jnp.where(lane_group >= gi, c, cnt)
    return cnt


def _pool_fwd(vp, wbd, scale, l=None):
    ch = 256

    def body(vp_ref, w_ref, sc_ref, y_ref, diff_ref, pad):
        pad[pl.ds(0, PAD), :] = jnp.zeros((PAD, PW), f32)
        pad[pl.ds(PAD + S, PAD), :] = jnp.zeros((PAD, PW), f32)
        pad[pl.ds(PAD, S), :] = vp_ref[...]
        for b in range(S // ch):
            base = b * ch
            pooled = _pool_sums(pad, base, ch, False) / _pool_counts(base, ch)
            diff = (pooled - vp_ref[pl.ds(base, ch), :]).astype(bf16)
            diff_ref[pl.ds(base, ch), :] = diff
            y_ref[pl.ds(base, ch), :] = _dot_nn(diff, w_ref[...]) * sc_ref[...]

    whole = lambda shape: pl.BlockSpec(shape, lambda i: (0,) * len(shape))
    return pl.pallas_call(
        body, grid=(1,),
        in_specs=[whole((S, PW)), whole((PW, PW)), whole((1, PW))],
        out_specs=[whole((S, PW)), whole((S, PW))],
        out_shape=[SDS((S, PW), f32), SDS((S, PW), bf16)],
        scratch_shapes=[pltpu.VMEM((S + 2 * PAD, PW), f32)],
        compiler_params=_CP, name="pool_fwd")(vp, wbd, scale)


def _pool_bwd(dy, diff, wbd, scale, l=None):
    ch = 256

    def body(dy_ref, diff_ref, w_ref, sc_ref, dvp_ref, dw_ref, dsc_ref, pad):
        pad[pl.ds(0, PAD), :] = jnp.zeros((PAD, PW), f32)
        pad[pl.ds(PAD + S, PAD), :] = jnp.zeros((PAD, PW), f32)
        dw = jnp.zeros((PW, PW), f32)
        dsc = jnp.zeros((1, PW), f32)
        for b in range(S // ch):
            base = b * ch
            dy = dy_ref[pl.ds(base, ch), :]
            diff = diff_ref[pl.ds(base, ch), :]
            dsc = dsc + jnp.sum(dy * _dot_nn(diff, w_ref[...]), axis=0, keepdims=True)
            dz = (dy * sc_ref[...]).astype(bf16)
            dw = dw + _dot_tn(diff, dz)
            ddiff = _dot_nt(dz, w_ref[...])
            dvp_ref[pl.ds(base, ch), :] = -ddiff
            pad[pl.ds(PAD + base, ch), :] = ddiff / _pool_counts(base, ch)
        dw_ref[...] = dw
        dsc_ref[...] = dsc
        for b in range(S // ch):
            base = b * ch
            dvp_ref[pl.ds(base, ch), :] += _pool_sums(pad, base, ch, True)

    whole = lambda shape: pl.BlockSpec(shape, lambda i: (0,) * len(shape))
    return pl.pallas_call(
        body, grid=(1,),
        in_specs=[whole((S, PW)), whole((S, PW)), whole((PW, PW)), whole((1, PW))],
        out_specs=[whole((S, PW)), whole((PW, PW)), whole((1, PW))],
        out_shape=[SDS((S, PW), f32), SDS((PW, PW), f32), SDS((1, PW), f32)],
        scratch_shapes=[pltpu.VMEM((S + 2 * PAD, PW), f32)],
        compiler_params=_CP, name="pool_bwd")(dy, diff, wbd, scale)


def _attn_blocks(lc):
    bpc = lc // QB
    kw = min(2 * QB, lc)
    blocks = []
    for b in range(S // QB):
        t0 = (b % bpc) * QB
        ks_in = min(max(t0 - HALF, 0), lc - kw)
        blocks.append((b * QB, (b // bpc) * lc + ks_in, t0 - ks_in))
    return kw, blocks


def _attn_bias(bias_ref, kw, shifts):
    r = lax.broadcasted_iota(jnp.int32, (2 * QB, kw), 0) % QB
    c = lax.broadcasted_iota(jnp.int32, (2 * QB, kw), 1)
    for i, shift in enumerate(shifts):
        bias_ref[i] = jnp.where(jnp.abs(r + shift - c) <= HALF, 0.0, MASK_VALUE).astype(f32)


def _stack_heads(blk, head0):
    zero = jnp.zeros_like(blk)
    return jnp.concatenate([jnp.where(head0, blk, zero), jnp.where(head0, zero, blk)], axis=0)


def _attn_fwd(q, k, v, lc, after=None):
    kw, blocks = _attn_blocks(lc)
    shifts = sorted({b[2] for b in blocks})

    def body(q_ref, k_ref, v_ref, *refs):
        o_ref, lse_ref, bias_ref = refs[-3:]
        head0 = lax.broadcasted_iota(jnp.int32, (QB, 128), 1) < 64
        _attn_bias(bias_ref, kw, shifts)
        for row0, kstart, shift in blocks:
            q2 = _stack_heads(q_ref[pl.ds(row0, QB), :], head0)
            kb = k_ref[pl.ds(kstart, kw), :]
            vb = v_ref[pl.ds(kstart, kw), :]
            s = _dot_nt(q2, kb) + bias_ref[shifts.index(shift)]
            m = jnp.max(s, axis=-1, keepdims=True)
            p = jnp.exp(s - m)
            den = jnp.sum(p, axis=-1, keepdims=True)
            o2 = _dot_nn(p.astype(bf16), vb) / den
            lse2 = jnp.broadcast_to(m + jnp.log(den), (2 * QB, 128))
            o_ref[pl.ds(row0, QB), :] = jnp.where(head0, o2[:QB], o2[QB:])
            lse_ref[pl.ds(row0, QB), :] = jnp.where(head0, lse2[:QB], lse2[QB:])

    col = pl.BlockSpec((S, 128), lambda p: (0, p))
    extra = () if after is None else (after,)
    return pl.pallas_call(
        body, grid=(NG,), in_specs=[col, col, col] + [_ANY] * len(extra), out_specs=[col, col],
        out_shape=[SDS((S, AW), f32), SDS((S, AW), f32)],
        scratch_shapes=[pltpu.VMEM((len(shifts), 2 * QB, kw), f32)],
        compiler_params=_CP, name=f"attn_fwd_{lc}")(q, k, v, *extra)


def _attn_bwd(q, k, v, do, lse, delta, lc):
    kw, blocks = _attn_blocks(lc)
    shifts = sorted({b[2] for b in blocks})

    def body(q_ref, k_ref, v_ref, do_ref, lse_ref, dl_ref, dq_ref, dk_ref, dv_ref, bias_ref):
        head0 = lax.broadcasted_iota(jnp.int32, (QB, 128), 1) < 64
        _attn_bias(bias_ref, kw, shifts)
        dk_ref[...] = jnp.zeros_like(dk_ref)
        dv_ref[...] = jnp.zeros_like(dv_ref)
        for row0, kstart, shift in blocks:
            q2 = _stack_heads(q_ref[pl.ds(row0, QB), :], head0)
            do2 = _stack_heads(do_ref[pl.ds(row0, QB), :], head0)
            lse = lse_ref[pl.ds(row0, QB), :]
            dl = dl_ref[pl.ds(row0, QB), :]
            lse2 = jnp.concatenate([lse[:, 0:1], lse[:, 64:65]], axis=0)
            dl2 = jnp.concatenate([dl[:, 0:1], dl[:, 64:65]], axis=0)
            kb = k_ref[pl.ds(kstart, kw), :]
            vb = v_ref[pl.ds(kstart, kw), :]
            p = jnp.exp(_dot_nt(q2, kb) + bias_ref[shifts.index(shift)] - lse2)
            ds = (p * (_dot_nt(do2, vb) - dl2)).astype(bf16)
            dq2 = _dot_nn(ds, kb)
            dq_ref[pl.ds(row0, QB), :] = jnp.where(head0, dq2[:QB], dq2[QB:])
            dk_ref[pl.ds(kstart, kw), :] += _dot_tn(ds, q2)
            dv_ref[pl.ds(kstart, kw), :] += _dot_tn(p.astype(bf16), do2)

    col = pl.BlockSpec((S, 128), lambda p: (0, p))
    return pl.pallas_call(
        body, grid=(NG,), in_specs=[col] * 6, out_specs=[col] * 3,
        out_shape=[SDS((S, AW), f32)] * 3,
        scratch_shapes=[pltpu.VMEM((len(shifts), 2 * QB, kw), f32)],
        compiler_params=_CP, name=f"attn_bwd_{lc}")(q, k, v, do, lse, delta)


def _mix_out_fwd(x, ypool, o1, l1, o4, l4, o16, l16, wout, l=None):
    def body(x_ref, yp_ref, o1_ref, l1_ref, o4_ref, l4_ref, o16_ref, l16_ref, w_ref,
             xo_ref, mixed_ref, o_ref, lse1_ref, lse4_ref, lse16_ref, so4, sl4, so16, sl16, sl):
        for r in range(4):
            for j in range(NG):
                so4[j, pl.ds(r, TM // 4, stride=4), :] = o4_ref[r, :, _cols(j)]
                sl4[j, pl.ds(r, TM // 4, stride=4), :] = l4_ref[r, :, _cols(j)]
        for r in range(16):
            for j in range(NG):
                so16[j, pl.ds(r, TM // 16, stride=16), :] = o16_ref[r, :, _cols(j)]
                sl16[j, pl.ds(r, TM // 16, stride=16), :] = l16_ref[r, :, _cols(j)]
        mixed_ref[:, :PW] = yp_ref[...].astype(bf16)
        for j in range(NG):
            a, b, c = l1_ref[:, _cols(j)], sl4[j], sl16[j]
            m = jnp.maximum(jnp.maximum(a, b), c)
            wa, wb, wc = jnp.exp(a - m), jnp.exp(b - m), jnp.exp(c - m)
            den = wa + wb + wc
            y = (wa * o1_ref[:, _cols(j)] + wb * so4[j] + wc * so16[j]) / den
            lse = m + jnp.log(den)
            o_ref[:, _cols(j)] = y
            lse1_ref[:, _cols(j)] = lse
            sl[j] = lse
            mixed_ref[:, PW + 128 * j: PW + 128 * (j + 1)] = y.astype(bf16)
        for r in range(4):
            for j in range(NG):
                lse4_ref[r, :, _cols(j)] = sl[j, pl.ds(r, TM // 4, stride=4), :]
        for r in range(16):
            for j in range(NG):
                lse16_ref[r, :, _cols(j)] = sl[j, pl.ds(r, TM // 16, stride=16), :]
        xo_ref[...] = x_ref[...] + _dot_nn(mixed_ref[...], w_ref[...])

    scr = pltpu.VMEM((NG, TM, 128), f32)
    return pl.pallas_call(
        body, grid=(S // TM,),
        in_specs=[_tile(D), _tile(PW), _tile(AW), _tile(AW), _p4(), _p4(), _p16(), _p16(), _layer(D, D, l)],
        out_specs=[_tile(D), _tile(D), _tile(AW), _tile(AW), _p4(), _p16()],
        out_shape=[SDS((S, D), f32), SDS((S, D), bf16), SDS((S, AW), f32), SDS((S, AW), f32),
                   SDS((4, S // 4, AW), f32), SDS((16, S // 16, AW), f32)],
        scratch_shapes=[scr] * 5,
        compiler_params=_CP, name="mix_out_fwd")(x, ypool, o1, l1, o4, l4, o16, l16, wout)


def _segsum64(t):
    lane = lax.broadcasted_iota(jnp.int32, t.shape, 1)
    for s in (1, 2, 4, 8, 16, 32):
        t = t + jnp.where((lane & s) != 0, pltpu.roll(t, s, 1), pltpu.roll(t, 128 - s, 1))
    return t


def _mix_out_bwd(dxo, o, wout, l=None):
    def body(dxo_ref, o_ref, w_ref, dxb_ref, dyp_ref, do1, do4, do16, dl1, dl4, dl16, sdo, sdl):
        dxb = dxo_ref[...].astype(bf16)
        dxb_ref[...] = dxb
        dm = _dot_nt(dxb, w_ref[...])
        dyp_ref[...] = dm[:, :PW]
        for j in range(NG):
            d = dm[:, PW + 128 * j: PW + 128 * (j + 1)]
            dl = _segsum64(d * o_ref[:, _cols(j)])
            do1[:, _cols(j)] = d.astype(bf16)
            dl1[:, _cols(j)] = dl
            sdo[j] = d
            sdl[j] = dl
        for r in range(4):
            for j in range(NG):
                do4[r, :, _cols(j)] = sdo[j, pl.ds(r, TM // 4, stride=4), :].astype(bf16)
                dl4[r, :, _cols(j)] = sdl[j, pl.ds(r, TM // 4, stride=4), :]
        for r in range(16):
            for j in range(NG):
                do16[r, :, _cols(j)] = sdo[j, pl.ds(r, TM // 16, stride=16), :].astype(bf16)
                dl16[r, :, _cols(j)] = sdl[j, pl.ds(r, TM // 16, stride=16), :]

    scr = pltpu.VMEM((NG, TM, 128), f32)
    return pl.pallas_call(
        body, grid=(S // TM,),
        in_specs=[_tile(D), _tile(AW), _layer(D, D, l)],
        out_specs=[_tile(D), _tile(PW), _tile(AW), _p4(), _p16(), _tile(AW), _p4(), _p16()],
        out_shape=[SDS((S, D), bf16), SDS((S, PW), f32),
                   SDS((S, AW), bf16), SDS((4, S // 4, AW), bf16), SDS((16, S // 16, AW), bf16),
                   SDS((S, AW), f32), SDS((4, S // 4, AW), f32), SDS((16, S // 16, AW), f32)],
        scratch_shapes=[scr] * 2,
        compiler_params=_CP, name="mix_out_bwd")(dxo, o, wout)


def _loss_head(x, g, target):
    def body(x_ref, g_ref, t_ref, dx_ref, loss_ref, dg_ref):
        g = g_ref[...]
        r, xh, y = _rms(x_ref[...], g)
        err = y - t_ref[...]
        dy = err * (1.0 / D)

        @pl.when(pl.program_id(0) == 0)
        def _():
            loss_ref[...] = jnp.zeros_like(loss_ref)
            dg_ref[...] = jnp.zeros_like(dg_ref)

        loss_ref[...] += jnp.broadcast_to(0.5 * jnp.sum(jnp.mean(err * err, axis=-1, keepdims=True)), (1, D))
        dg_ref[...] += jnp.sum(dy * xh, axis=0, keepdims=True)
        dx_ref[...] = _rms_bwd(dy, r, xh, g)

    return pl.pallas_call(
        body, grid=(S // TM,),
        in_specs=[_tile(D), _const((1, D)), _tile(D)],
        out_specs=[_tile(D), _const((1, D)), _const((1, D))],
        out_shape=[SDS((S, D), f32), SDS((1, D), f32), SDS((1, D), f32)],
        compiler_params=_CP, name="loss_head")(x, g, target)


def _peer(k):
    x, y, c = lax.axis_index("x"), lax.axis_index("y"), lax.axis_index("c")
    px = 1 - x if k & 4 else x
    py = 1 - y if k & 2 else y
    pc = 1 - c if k & 1 else c
    return (px, py, pc), 4 * px + 2 * py + pc


def _all_gather(shards):
    n = len(shards)

    def body(*refs):
        ins, outs = refs[:n], refs[n:2 * n]
        send_sems, recv_sems, local_sems = refs[2 * n:]
        me, me_idx = _peer(0)
        sibling, sib_idx = _peer(1)
        far = [_peer(k) for k in (4, 2, 6)]
        far_sib = [_peer(k) for k in (5, 3, 7)]

        def rows(t, idx):
            r = ins[t].shape[1]
            return outs[t].at[:, pl.ds(idx * r, r), :]

        def copy(k, t, idx, to, src=None):
            return pltpu.make_async_remote_copy(
                src_ref=rows(t, idx) if src is None else src, dst_ref=rows(t, idx),
                send_sem=send_sems.at[k, t], recv_sem=recv_sems.at[k, t], device_id=to, device_id_type=_MESH)

        mine = [pltpu.make_async_copy(ins[t], rows(t, me_idx), local_sems.at[t]) for t in range(n)]
        for cp in mine:
            cp.start()
        first = [copy(0, t, me_idx, sibling, src=ins[t]) for t in range(n)]
        for j, (dev, _) in enumerate(far):
            first += [copy(1 + j, t, me_idx, dev, src=ins[t]) for t in range(n)]
        for cp in first:
            cp.start()
        passed = []
        for j, (_, idx) in enumerate(far):
            for t in range(n):
                copy(1 + j, t, idx, me).wait_recv()
                cp = copy(4 + j, t, idx, sibling)
                cp.start()
                passed.append(cp)
        for t in range(n):
            copy(0, t, sib_idx, me).wait_recv()
        for j, (_, idx) in enumerate(far_sib):
            for t in range(n):
                copy(4 + j, t, idx, me).wait_recv()
        for cp in first + passed:
            cp.wait_send()
        for cp in mine:
            cp.wait()

    return pl.pallas_call(
        body, in_specs=[_ANY] * n, out_specs=[_ANY] * n,
        out_shape=[SDS((a.shape[0], NDEV * a.shape[1], a.shape[2]), a.dtype) for a in shards],
        scratch_shapes=[pltpu.SemaphoreType.DMA((7, n)), pltpu.SemaphoreType.DMA((7, n)),
                        pltpu.SemaphoreType.DMA((n,))],
        name="all_gather_weights")(*shards)


def _hbm(a):
    return pltpu.with_memory_space_constraint(a, pltpu.HBM)


def _rows(ref, idx):
    r = ref.shape[0] // NDEV
    return ref.at[pl.ds(idx * r, r), :]


def _row_copy(ref, idx, send_sem, recv_sem, to):
    return pltpu.make_async_remote_copy(src_ref=_rows(ref, idx), dst_ref=_rows(ref, idx), send_sem=send_sem,
                                        recv_sem=recv_sem, device_id=to, device_id_type=_MESH)


_TOKEN = SDS((8, 128), f32)
_FAR = (4, 2, 6)
_FAR_SIB = (5, 3, 7)


def _ag_start(lands, after, l):
    n = len(lands)

    def body(*refs):
        zones, send_sems, recv_sems, token = refs[:n], refs[n + 1], refs[n + 2], refs[-1]
        _, me_idx = _peer(0)
        for k, mask in enumerate((1,) + _FAR):
            for t in range(n):
                _row_copy(zones[t], me_idx, send_sems.at[k * n + t], recv_sems.at[k * n + t], _peer(mask)[0]).start()
        token[...] = jnp.zeros_like(token)

    outs = pl.pallas_call(
        body, name=f"ag_start_{l}", in_specs=[_HBM] * n + [_ANY],
        out_specs=(_SEM, _SEM, *[_HBM] * n, pl.BlockSpec(memory_space=pltpu.VMEM)),
        out_shape=(pltpu.SemaphoreType.DMA((4 * n,)), pltpu.SemaphoreType.DMA((4 * n,)),
                   *[pltpu.HBM(a.shape, a.dtype) for a in lands], _TOKEN),
        input_output_aliases={t: 2 + t for t in range(n)}, compiler_params=_CP_SPLIT)(
            *[_hbm(a) for a in lands], after)
    return outs[0], outs[1], list(outs[2:2 + n]), outs[-1]


def _ag_pass(lands, recv_sems, after, l):
    n = len(lands)

    def body(*refs):
        zones, recv_sems = refs[:n], refs[n]
        psend, precv, token = refs[n + 2], refs[n + 3], refs[-1]
        me, _ = _peer(0)
        sibling, _ = _peer(1)
        for j, mask in enumerate(_FAR):
            idx = _peer(mask)[1]
            for t in range(n):
                _row_copy(zones[t], idx, psend.at[j * n + t], recv_sems.at[(1 + j) * n + t], me).wait_recv()
                _row_copy(zones[t], idx, psend.at[j * n + t], precv.at[j * n + t], sibling).start()
        token[...] = jnp.zeros_like(token)

    outs = pl.pallas_call(
        body, name=f"ag_pass_{l}", in_specs=[_HBM] * n + [_SEM, _ANY],
        out_specs=(_SEM, _SEM, *[_HBM] * n, pl.BlockSpec(memory_space=pltpu.VMEM)),
        out_shape=(pltpu.SemaphoreType.DMA((3 * n,)), pltpu.SemaphoreType.DMA((3 * n,)),
                   *[pltpu.HBM(a.shape, a.dtype) for a in lands], _TOKEN),
        input_output_aliases={t: 2 + t for t in range(n)}, compiler_params=_CP_SPLIT)(*lands, recv_sems, after)
    return outs[0], outs[1], list(outs[2:2 + n]), outs[-1]


def _ag_wait(lands, send_sems, recv_sems, psend, precv, after, l):
    n = len(lands)

    def body(*refs):
        zones = refs[:n]
        send_sems, recv_sems, psend, precv = refs[n:n + 4]
        me, me_idx = _peer(0)
        sib_idx = _peer(1)[1]
        for k in range(4):
            for t in range(n):
                _row_copy(zones[t], me_idx, send_sems.at[k * n + t], recv_sems.at[k * n + t], me).wait_send()
        for t in range(n):
            _row_copy(zones[t], sib_idx, send_sems.at[t], recv_sems.at[t], me).wait_recv()
        for j in range(3):
            mine, theirs = _peer(_FAR[j])[1], _peer(_FAR_SIB[j])[1]
            for t in range(n):
                _row_copy(zones[t], mine, psend.at[j * n + t], precv.at[j * n + t], me).wait_send()
                _row_copy(zones[t], theirs, psend.at[j * n + t], precv.at[j * n + t], me).wait_recv()

    outs = pl.pallas_call(
        body, name=f"ag_wait_{l}", in_specs=[_HBM] * n + [_SEM] * 4 + [_ANY], out_specs=tuple([_HBM] * n),
        out_shape=tuple(pltpu.HBM(a.shape, a.dtype) for a in lands),
        input_output_aliases={t: t for t in range(n)}, compiler_params=_CP_SPLIT)(
            *lands, send_sems, recv_sems, psend, precv, after)
    return list(outs)


def _xchg_src(ref, slot_ref, idx):
    return _rows(ref, idx) if ref.shape[0] == NDEV * slot_ref.shape[1] else ref


def _rs_start(srcs, slots, after, tag):
    n = len(srcs)
    after = list(after) if isinstance(after, (list, tuple)) else [after]

    def body(*refs):
        src, slot = refs[:n], refs[n:2 * n]
        send_sems, recv_sems, token = refs[2 * n + len(after)], refs[2 * n + len(after) + 1], refs[-1]
        _, me_idx = _peer(0)
        for k in range(1, NDEV):
            dev, idx = _peer(k)
            for t in range(n):
                pltpu.make_async_remote_copy(
                    src_ref=_xchg_src(src[t], slot[t], idx), dst_ref=slot[t].at[me_idx],
                    send_sem=send_sems.at[(k - 1) * n + t], recv_sem=recv_sems.at[(k - 1) * n + t],
                    device_id=dev, device_id_type=_MESH).start()
        token[...] = jnp.zeros_like(token)

    outs = pl.pallas_call(
        body, name=f"rs_start_{tag}", in_specs=[_HBM] * (2 * n) + [_ANY] * len(after),
        out_specs=(_SEM, _SEM, *[_HBM] * (2 * n), pl.BlockSpec(memory_space=pltpu.VMEM)),
        out_shape=(pltpu.SemaphoreType.DMA(((NDEV - 1) * n,)), pltpu.SemaphoreType.DMA(((NDEV - 1) * n,)),
                   *[pltpu.HBM(a.shape, a.dtype) for a in list(srcs) + list(slots)], _TOKEN),
        input_output_aliases={t: 2 + t for t in range(2 * n)}, compiler_params=_CP_SPLIT)(
            *[_hbm(a) for a in list(srcs) + list(slots)], *after)
    return outs[0], outs[1], list(outs[2:2 + n]), list(outs[2 + n:2 + 2 * n]), outs[-1]


def _rs_wait(srcs, slots, send_sems, recv_sems, after, tag):
    n = len(srcs)
    after = list(after) if isinstance(after, (list, tuple)) else [after]

    def body(*refs):
        src, slot, send_sems, recv_sems = refs[:n], refs[n:2 * n], refs[2 * n], refs[2 * n + 1]
        me, _ = _peer(0)
        for k in range(1, NDEV):
            idx = _peer(k)[1]
            for t in range(n):
                cp = pltpu.make_async_remote_copy(
                    src_ref=_xchg_src(src[t], slot[t], idx), dst_ref=slot[t].at[idx],
                    send_sem=send_sems.at[(k - 1) * n + t], recv_sem=recv_sems.at[(k - 1) * n + t],
                    device_id=me, device_id_type=_MESH)
                cp.wait_send()
                cp.wait_recv()

    outs = pl.pallas_call(
        body, name=f"rs_wait_{tag}", in_specs=[_HBM] * (2 * n) + [_SEM, _SEM] + [_ANY] * len(after),
        out_specs=tuple([_HBM] * (2 * n)),
        out_shape=tuple(pltpu.HBM(a.shape, a.dtype) for a in list(srcs) + list(slots)),
        input_output_aliases={t: t for t in range(2 * n)}, compiler_params=_CP_SPLIT)(
            *srcs, *slots, send_sems, recv_sems, *after)
    return list(outs[:n]), list(outs[n:])


def _sum_slots(slots, rb):
    r = slots.shape[1]

    def body(s_ref, o_ref):
        acc = s_ref[0].astype(f32)
        for s in range(1, NDEV):
            acc = acc + s_ref[s].astype(f32)
        o_ref[...] = acc

    return pl.pallas_call(
        body, grid=(r // rb,),
        in_specs=[pl.BlockSpec((NDEV, rb, D), lambda i: (0, i, 0))],
        out_specs=pl.BlockSpec((rb, D), lambda i: (i, 0)),
        out_shape=SDS((r, D), f32), compiler_params=_CP, name="sum_slots")(slots)


def _adamw(w, g, m, v):
    shape = w.shape
    cols = shape[-1]
    rows = w.size // cols
    rb = rows
    for cand in (512, 256, 128, 64, 32, 16, 8):
        if rows % cand == 0 and rows > cand:
            rb = cand
            break

    def body(w_ref, g_ref, m_ref, v_ref, d_ref, mo_ref, vo_ref):
        d_ref[...], mo_ref[...], vo_ref[...] = _adamw_math(w_ref[...], g_ref[...], m_ref[...], v_ref[...])

    spec = pl.BlockSpec((rb, cols), lambda i: (i, 0))
    outs = pl.pallas_call(
        body, grid=(rows // rb,), in_specs=[spec] * 4, out_specs=[spec] * 3,
        out_shape=[SDS((rows, cols), f32)] * 3, compiler_params=_CP, name="adamw")(
            *(a.reshape(rows, cols) for a in (w, g, m, v)))
    return tuple(o.reshape(shape) for o in outs)


def _adamw_math(w, g, m, v):
    m = ADAM_B1 * m + (1.0 - ADAM_B1) * g
    v = ADAM_B2 * v + (1.0 - ADAM_B2) * (g * g)
    m_hat = m / (1.0 - ADAM_B1 ** ADAM_STEP)
    v_hat = v / (1.0 - ADAM_B2 ** ADAM_STEP)
    return -ADAM_LR * (m_hat / (jnp.sqrt(v_hat) + ADAM_EPS) + ADAM_WD * w), m, v


def _reduce_adamw(acc, me, full, slots, w, m, v, l):
    _, r, _ = w.shape
    rb = r // 2 if r > 128 else r

    def body(me_ref, full_ref, slots_ref, w_ref, m_ref, v_ref, *refs):
        go_ref, d_ref, mo_ref, vo_ref = refs[-4:]
        own = full_ref[...].astype(f32)
        g = None
        for s in range(NDEV):
            part = jnp.where(me_ref[0] == s, own, slots_ref[s].astype(f32))
            g = part if g is None else g + part
        go_ref[...] = g
        d_ref[...], mo_ref[...], vo_ref[...] = _adamw_math(w_ref[...], g, m_ref[...], v_ref[...])

    steps = r // rb
    lay = pl.BlockSpec((None, rb, D), lambda i, me_ref: (l, i, 0))
    n_acc = 0 if acc is None else 4
    grid_spec = pltpu.PrefetchScalarGridSpec(
        num_scalar_prefetch=1, grid=(steps,),
        in_specs=[pl.BlockSpec((rb, D), lambda i, me_ref: (me_ref[0] * steps + i, 0)),
                  pl.BlockSpec((NDEV, rb, D), lambda i, me_ref: (0, i, 0)), lay, lay, lay] + [_ANY] * n_acc,
        out_specs=[lay] * 4)
    outs = pl.pallas_call(
        body, grid_spec=grid_spec, out_shape=[SDS(w.shape, f32)] * 4,
        input_output_aliases={6 + j: j for j in range(n_acc)},
        compiler_params=_CP, name="reduce_adamw")(me, full, slots, w, m, v, *(() if acc is None else acc))
    return tuple(outs)


_BIG = ("ffn1_w_gate", "ffn1_w_up", "ffn1_w_down", "w_in", "w_out", "ffn2_w_gate", "ffn2_w_up", "ffn2_w_down")
_TRANSPOSED = ("ffn1_w_gate", "ffn1_w_up", "w_in", "ffn2_w_gate", "ffn2_w_up")

def _block_diag(pool_w):
    out = jnp.zeros((L, PW, PW), pool_w.dtype)
    for gi in range(4):
        out = out.at[:, 64 * gi:64 * (gi + 1), 64 * gi:64 * (gi + 1)].set(pool_w[:, gi])
    return out


def kernel(x, positions, ffn1_norm, ffn1_w_gate, ffn1_w_up, ffn1_w_down, mix_norm, w_in, pool_w, pool_scale, w_out, ffn2_norm, ffn2_w_gate, ffn2_w_up, ffn2_w_down, final_norm, loss_target, m_ffn1_norm, m_ffn1_w_gate, m_ffn1_w_up, m_ffn1_w_down, m_mix_norm, m_w_in, m_pool_w, m_pool_scale, m_w_out, m_ffn2_norm, m_ffn2_w_gate, m_ffn2_w_up, m_ffn2_w_down, m_final_norm, v_ffn1_norm, v_ffn1_w_gate, v_ffn1_w_up, v_ffn1_w_down, v_mix_norm, v_w_in, v_pool_w, v_pool_scale, v_w_out, v_ffn2_norm, v_ffn2_w_gate, v_ffn2_w_up, v_ffn2_w_down, v_final_norm):
    weights = dict(ffn1_norm=ffn1_norm, ffn1_w_gate=ffn1_w_gate, ffn1_w_up=ffn1_w_up, ffn1_w_down=ffn1_w_down,
                   mix_norm=mix_norm, w_in=w_in, pool_w=pool_w, pool_scale=pool_scale, w_out=w_out,
                   ffn2_norm=ffn2_norm, ffn2_w_gate=ffn2_w_gate, ffn2_w_up=ffn2_w_up, ffn2_w_down=ffn2_w_down,
                   final_norm=final_norm)
    moms = dict(ffn1_norm=m_ffn1_norm, ffn1_w_gate=m_ffn1_w_gate, ffn1_w_up=m_ffn1_w_up, ffn1_w_down=m_ffn1_w_down,
                mix_norm=m_mix_norm, w_in=m_w_in, pool_w=m_pool_w, pool_scale=m_pool_scale, w_out=m_w_out,
                ffn2_norm=m_ffn2_norm, ffn2_w_gate=m_ffn2_w_gate, ffn2_w_up=m_ffn2_w_up, ffn2_w_down=m_ffn2_w_down,
                final_norm=m_final_norm)
    vels = dict(ffn1_norm=v_ffn1_norm, ffn1_w_gate=v_ffn1_w_gate, ffn1_w_up=v_ffn1_w_up, ffn1_w_down=v_ffn1_w_down,
                mix_norm=v_mix_norm, w_in=v_w_in, pool_w=v_pool_w, pool_scale=v_pool_scale, w_out=v_w_out,
                ffn2_norm=v_ffn2_norm, ffn2_w_gate=v_ffn2_w_gate, ffn2_w_up=v_ffn2_w_up, ffn2_w_down=v_ffn2_w_down,
                final_norm=v_final_norm)
    names = list(weights)

    me_idx = 4 * lax.axis_index("x") + 2 * lax.axis_index("y") + lax.axis_index("c")

    tr = lambda w: jnp.swapaxes(w, 1, 2).astype(bf16)
    shards = [tr(weights[nm]) if nm in _TRANSPOSED else weights[nm].astype(bf16) for nm in _BIG]

    def landing_zones(l, which):
        return [lax.dynamic_update_slice(lax.empty((NDEV * shards[t].shape[1], D), bf16), shards[t][l],
                                         (me_idx * shards[t].shape[1], 0)) for t in which]

    g_ffn1 = [ffn1_norm[l].reshape(1, D) for l in range(L)]
    g_mix = [mix_norm[l].reshape(1, D) for l in range(L)]
    g_ffn2 = [ffn2_norm[l].reshape(1, D) for l in range(L)]
    wbd_all = _block_diag(pool_w).astype(bf16)
    wbd = [wbd_all[l] for l in range(L)]
    pscale = [pool_scale[l].reshape(1, PW) for l in range(L)]
    tabs = _rope_tables(positions)
    flat = lambda a: a.reshape(S, AW)
    r4 = lambda a: a.reshape(4, S // 4, AW)
    r16 = lambda a: a.reshape(16, S // 16, AW)

    first, rest, whole = (0, 1, 2, 3), (4, 5, 6, 7), tuple(range(8))
    head = [a.reshape(a.shape[1], D) for a in _all_gather([shards[t][0:1] for t in first])]
    chain = {0: _ag_start(landing_zones(0, rest), head[0], "0")}
    gathered = [None] * L
    xs = x.reshape(S, D)
    saved = []
    for l in range(L):
        ga, gb = g_ffn1[l], g_ffn2[l]
        if l == 0:
            gt1, ut1, dn1, wint = head
            ga = ga + chain[0][3][0, 0]
        else:
            gt1, ut1, dn1, wint, wout, gt2, ut2, dn2 = gathered[l]
            if l + 1 < L:
                chain[l + 1] = _ag_start(landing_zones(l + 1, whole), xs, str(l + 1))
                ga = ga + chain[l + 1][3][0, 0]
        x0 = xs
        x1, gate1, up1 = _ffn_fwd(x0, ga, gt1, ut1, dn1)
        hmix, vp, q1, k1, v1, q4, k4, v4, q16, k16, v16 = _mix_in_fwd(x1, g_mix[l], wint, tabs)
        q4, k4, v4, q16, k16, v16 = map(flat, (q4, k4, v4, q16, k16, v16))
        ypool, diff = _pool_fwd(vp, wbd[l], pscale[l])
        o1, l1 = _attn_fwd(q1, k1, v1, S)
        if l == 0:
            send_sems, recv_sems, zones, _ = chain[0]
            psend, precv, zones, token = _ag_pass(zones, recv_sems, o1, "0")
            chain[1] = _ag_start(landing_zones(1, whole), token, "1")
            o4, l4 = _attn_fwd(q4, k4, v4, S // 4, after=chain[1][3])
        else:
            o4, l4 = _attn_fwd(q4, k4, v4, S // 4)
        o16, l16 = _attn_fwd(q16, k16, v16, S // 16)
        if l == 0:
            wout, gt2, ut2, dn2 = _ag_wait(zones, send_sems, recv_sems, psend, precv, o16, "0")
            gathered[0] = head + [wout, gt2, ut2, dn2]
        elif l + 1 < L:
            send_sems, recv_sems, zones, _ = chain[l + 1]
            psend, precv, zones, token = _ag_pass(zones, recv_sems, o16, str(l + 1))
            gb = gb + token[0, 0]
        x2, mixed, o, lse1, lse4, lse16 = _mix_out_fwd(x1, ypool, o1, l1, r4(o4), r4(l4), r16(o16), r16(l16), wout)
        if l == 0:
            send_sems, recv_sems, zones, _ = chain[1]
            psend, precv, zones, token = _ag_pass(zones, recv_sems, x2, "1")
            gb = gb + token[0, 0]
        x3, gate2, up2 = _ffn_fwd(x2, gb, gt2, ut2, dn2)
        if l + 1 < L:
            gathered[l + 1] = _ag_wait(zones, send_sems, recv_sems, psend, precv, x3, str(l + 1))
        saved.append(dict(x0=x0, x1=x1, x2=x2, gate1=gate1, up1=up1, gate2=gate2, up2=up2, hmix=hmix, diff=diff,
                          qkv=((q1, k1, v1), (q4, k4, v4), (q16, k16, v16)), mixed=mixed, o=o,
                          lse=(lse1, flat(lse4), flat(lse16))))
        xs = x3

    dx, loss_part, d_final = _loss_head(xs, final_norm.reshape(1, D), loss_target.reshape(S, D))

    d_norm = {nm: [None] * L for nm in ("ffn1_norm", "mix_norm", "ffn2_norm")}
    d_poolw, d_pscale = [None] * L, [None] * L
    group_a = ("ffn2_w_gate", "ffn2_w_up", "ffn2_w_down", "w_out")
    group_b = ("ffn1_w_gate", "ffn1_w_up", "ffn1_w_down", "w_in")
    acc = {}

    as_rows = lambda a, nm: jnp.swapaxes(a, 1, 2) if nm in _TRANSPOSED else a
    w_rows = {nm: as_rows(weights[nm], nm) for nm in _BIG}
    m_rows = {nm: as_rows(moms[nm], nm) for nm in _BIG}
    v_rows = {nm: as_rows(vels[nm], nm) for nm in _BIG}
    me_arr = me_idx.reshape(1).astype(jnp.int32)

    def exchange(full, group, after, tag, extra=None):
        srcs = [full[nm] for nm in group]
        slots = [lax.empty((NDEV, g.shape[0] // NDEV, D), bf16) for g in srcs]
        if extra is not None:
            srcs, slots = srcs + [extra[0]], slots + [extra[1]]
        ssem, rsem, srcs, slots, token = _rs_start(srcs, slots, after, tag)
        return (srcs, slots, ssem, rsem, tag), token

    def update(l, group, flight, after):
        srcs, slots, ssem, rsem, tag = flight
        srcs, slots = _rs_wait(srcs, slots, ssem, rsem, after, tag)
        for nm, full_g, slots_g in zip(group, srcs, slots):
            acc[nm] = _reduce_adamw(acc.get(nm), me_arr, full_g, slots_g, w_rows[nm], m_rows[nm], v_rows[nm], l)
        return [acc[nm][0] for nm in group], slots

    flights = {}
    token_b = None
    for l in reversed(range(L)):
        sv = saved[l]
        gt1, ut1, dn1, wint, wout, gt2, ut2, dn2 = gathered[l]
        gb = g_ffn2[l] if token_b is None else g_ffn2[l] + token_b[0, 0]
        full = {}
        dx, dgate, dup, h, dy, d_norm["ffn2_norm"][l] = _ffn_bwd_d(sv["x2"], gb, sv["gate2"], sv["up2"], dx, gt2, ut2, dn2)
        full["ffn2_w_gate"], full["ffn2_w_up"], full["ffn2_w_down"] = _ffn_bwd_w(h, dy, sv["gate2"], sv["up2"], dgate, dup)

        dxb, dyp, do1, do4, do16, dl1, dl4, dl16 = _mix_out_bwd(dx, sv["o"], wout)
        full["w_out"] = _wgrad(sv["mixed"], dxb)
        flights[l, "a"], token_a = exchange(full, group_a, dxb, f"a{l}")
        dvp, dwbd, d_pscale[l] = _pool_bwd(dyp, sv["diff"], wbd[l], pscale[l] + token_a[0, 0])
        d_poolw[l] = jnp.stack([dwbd[64 * gi:64 * (gi + 1), 64 * gi:64 * (gi + 1)] for gi in range(4)])
        dos, dls = (do1, flat(do4), flat(do16)), (dl1, flat(dl4), flat(dl16))
        dqkv = []
        for b, lc in enumerate((S, S // 4, S // 16)):
            qb, kb, vb = sv["qkv"][b]
            dqkv.append(_attn_bwd(qb, kb, vb, dos[b], sv["lse"][b], dls[b], lc))
        d4 = tuple(r4(a) for a in dqkv[1])
        d16 = tuple(r16(a) for a in dqkv[2])
        dx, dproj, d_norm["mix_norm"][l] = _mix_in_bwd(dx, sv["x1"], g_mix[l], wint, tabs, dvp, dqkv[0], d4, d16)
        full["w_in"] = _wgrad(dproj, sv["hmix"])

        dx, dgate, dup, h, dy, d_norm["ffn1_norm"][l] = _ffn_bwd_d(sv["x0"], g_ffn1[l], sv["gate1"], sv["up1"], dx, gt1, ut1, dn1)
        full["ffn1_w_gate"], full["ffn1_w_up"], full["ffn1_w_down"] = _ffn_bwd_w(h, dy, sv["gate1"], sv["up1"], dgate, dup)

        after = dx
        if l + 1 < L and l + 1 >= 2:
            after, _ = update(l + 1, group_a, flights.pop((l + 1, "a")), after)
            after, _ = update(l + 1, group_b, flights.pop((l + 1, "b")), after)
        if l > 0:
            flights[l, "b"], token_b = exchange(full, group_b, after, f"b{l}")

    pad8 = lambda a: jnp.pad(a, ((0, 8 - a.shape[0]), (0, 0)))
    misc = jnp.concatenate([d_final, jnp.concatenate(d_pscale, axis=1), loss_part], axis=0)
    small = jnp.concatenate(
        [pad8(jnp.concatenate(d_norm[nm], axis=0)) for nm in ("ffn1_norm", "mix_norm", "ffn2_norm")]
        + [pad8(misc), jnp.stack(d_poolw).reshape(L * 16, D)], axis=0)
    small_slots = lax.dynamic_update_slice(lax.empty((NDEV, SMALL_ROWS, D), f32), small[None], (me_idx, 0, 0))
    flights[0, "b"], token_b = exchange(full, group_b, dx, "b0", extra=(small, small_slots))

    after = token_b
    for key in [(1, "a"), (1, "b"), (0, "a")]:
        after, _ = update(key[0], group_a if key[1] == "a" else group_b, flights.pop(key), after)
    after, slots_b0 = update(0, group_b, flights.pop((0, "b")), after)

    sm = _sum_slots(slots_b0[-1], SMALL_ROWS)
    grads = {}
    grads["ffn1_norm"], grads["mix_norm"], grads["ffn2_norm"] = sm[0:L], sm[8:8 + L], sm[16:16 + L]
    grads["final_norm"] = sm[24]
    grads["pool_scale"] = sm[25].reshape(L, PW)
    grads["pool_w"] = sm[32:32 + L * 16].reshape(L, 4, 64, 64)
    loss = sm[26, 0]
    upd = {nm: _adamw(weights[nm], grads[nm], moms[nm], vels[nm]) for nm in names if nm not in _BIG}
    for nm in _BIG:
        grads[nm], upd[nm] = as_rows(acc[nm][0], nm), tuple(as_rows(a, nm) for a in acc[nm][1:])
    return (loss, dx.reshape(1, S, D), *[grads[nm] for nm in names], *[upd[nm][0] for nm in names],
            *[upd[nm][1] for nm in names], *[upd[nm][2] for nm in names])
```

```python
import jax
import jax.numpy as jnp
from jax import lax
from jax.experimental import pallas as pl
from jax.experimental.pallas import tpu as pltpu

f32 = jnp.float32
bf16 = jnp.bfloat16
SDS = jax.ShapeDtypeStruct

D = 1024
S = 2048
F = 2816
L = 4
PW = 256
AW = 768
PROJ = PW + 3 * AW
NDEV = 8
TM = 256
QB = 128
HALF = 64
NG = AW // 128
NORM_EPS = 1e-6
MASK_VALUE = -1e30
ROPE_THETA = 500000.0
ADAM_LR, ADAM_B1, ADAM_B2, ADAM_EPS, ADAM_WD, ADAM_STEP = 0.001, 0.9, 0.999, 1e-08, 0.01, 10
POOL_WINDOWS = (2, 4, 8, 16)
PAD = 8
SMALL_ROWS = 96
VMEM_LIMIT = 56 * 1024 * 1024

_CP = pltpu.CompilerParams(vmem_limit_bytes=VMEM_LIMIT)
_ANY = pl.BlockSpec(memory_space=pl.ANY)
_HBM = pl.BlockSpec(memory_space=pltpu.HBM)
_SEM = pl.BlockSpec(memory_space=pltpu.SEMAPHORE)
_MESH = pl.DeviceIdType.MESH
_CP_SPLIT = pltpu.CompilerParams(has_side_effects=pltpu.SideEffectType.DATAFLOW_SIDE_EFFECTING)


def _dot_nn(a, b):
    return lax.dot_general(a, b, (((1,), (0,)), ((), ())), preferred_element_type=f32)


def _dot_nt(a, b):
    return lax.dot_general(a, b, (((1,), (1,)), ((), ())), preferred_element_type=f32)


def _dot_tn(a, b):
    return lax.dot_general(a, b, (((0,), (0,)), ((), ())), preferred_element_type=f32)


def _rms(x, g):
    r = lax.rsqrt(jnp.mean(x * x, axis=-1, keepdims=True) + NORM_EPS)
    xh = x * r
    return r, xh, xh * g


def _rms_bwd(dh, r, xh, g):
    dxh = dh * g
    return r * (dxh - xh * jnp.mean(dxh * xh, axis=-1, keepdims=True))


def _tile(cols):
    return pl.BlockSpec((TM, cols), lambda i: (i, 0))


def _const(shape):
    return pl.BlockSpec(shape, lambda i: (0,) * len(shape))


def _layer(rows, cols, l=None):
    return pl.BlockSpec((rows, cols), lambda i: (0, 0), pipeline_mode=pl.Buffered(1))


def _p4():
    return pl.BlockSpec((4, TM // 4, AW), lambda i: (0, i, 0))


def _p16():
    return pl.BlockSpec((16, TM // 16, AW), lambda i: (0, i, 0))


def _cols(j):
    return slice(128 * j, 128 * (j + 1))


def _ffn_fwd(x, g, gt, ut, dn, l=None):
    def body(x_ref, g_ref, gt_ref, ut_ref, dn_ref, xo_ref, gate_ref, up_ref):
        x = x_ref[...]
        _, _, hn = _rms(x, g_ref[...])
        h = hn.astype(bf16)
        gate = _dot_nt(h, gt_ref[...])
        up = _dot_nt(h, ut_ref[...])
        gate_ref[...] = gate.astype(bf16)
        up_ref[...] = up.astype(bf16)
        a = (gate * jax.nn.sigmoid(gate) * up).astype(bf16)
        xo_ref[...] = x + 0.5 * _dot_nn(a, dn_ref[...])

    return pl.pallas_call(
        body, grid=(S // TM,),
        in_specs=[_tile(D), _layer(1, D, l), _layer(F, D, l), _layer(F, D, l), _layer(F, D, l)],
        out_specs=[_tile(D), _tile(F), _tile(F)],
        out_shape=[SDS((S, D), f32), SDS((S, F), bf16), SDS((S, F), bf16)],
        compiler_params=_CP, name="ffn_fwd")(x, g, gt, ut, dn)


def _ffn_bwd_d(x, g, gate, up, dxo, gt, ut, dn, l=None):
    def body(x_ref, g_ref, gate_ref, up_ref, dxo_ref, gt_ref, ut_ref, dn_ref,
             dx_ref, dgate_ref, dup_ref, h_ref, dy_ref, dg_ref):
        x = x_ref[...]
        g = g_ref[...]
        r, xh, hn = _rms(x, g)
        h_ref[...] = hn.astype(bf16)
        dxo = dxo_ref[...]
        dy = (0.5 * dxo).astype(bf16)
        dy_ref[...] = dy
        da = _dot_nt(dy, dn_ref[...])
        gate = gate_ref[...].astype(f32)
        up = up_ref[...].astype(f32)
        sg = jax.nn.sigmoid(gate)
        dgate = (da * up * (sg * (1.0 + gate * (1.0 - sg)))).astype(bf16)
        dup = (da * (gate * sg)).astype(bf16)
        dgate_ref[...] = dgate
        dup_ref[...] = dup
        dh = _dot_nn(dgate, gt_ref[...]) + _dot_nn(dup, ut_ref[...])

        @pl.when(pl.program_id(0) == 0)
        def _():
            dg_ref[...] = jnp.zeros_like(dg_ref)

        dg_ref[...] += jnp.sum(dh * xh, axis=0, keepdims=True)
        dx_ref[...] = dxo + _rms_bwd(dh, r, xh, g)

    return pl.pallas_call(
        body, grid=(S // TM,),
        in_specs=[_tile(D), _layer(1, D, l), _tile(F), _tile(F), _tile(D),
                  _layer(F, D, l), _layer(F, D, l), _layer(F, D, l)],
        out_specs=[_tile(D), _tile(F), _tile(F), _tile(D), _tile(D), _const((1, D))],
        out_shape=[SDS((S, D), f32), SDS((S, F), bf16), SDS((S, F), bf16), SDS((S, D), bf16),
                   SDS((S, D), bf16), SDS((1, D), f32)],
        compiler_params=_CP, name="ffn_bwd_d")(x, g, gate, up, dxo, gt, ut, dn)


def _ffn_bwd_w(h, dy, gate, up, dgate, dup):
    fc = 256

    def body(h_ref, dy_ref, gate_ref, up_ref, dgate_ref, dup_ref, dgt_ref, dut_ref, ddn_ref):
        gate = gate_ref[...].astype(f32)
        a = (gate * jax.nn.sigmoid(gate) * up_ref[...].astype(f32)).astype(bf16)
        ddn_ref[...] = _dot_tn(a, dy_ref[...]).astype(bf16)
        h = h_ref[...]
        dgt_ref[...] = _dot_tn(dgate_ref[...], h).astype(bf16)
        dut_ref[...] = _dot_tn(dup_ref[...], h).astype(bf16)

    col = pl.BlockSpec((S, fc), lambda j: (0, j))
    row = pl.BlockSpec((fc, D), lambda j: (j, 0))
    full = pl.BlockSpec((S, D), lambda j: (0, 0))
    return pl.pallas_call(
        body, grid=(F // fc,),
        in_specs=[full, full, col, col, col, col],
        out_specs=[row, row, row],
        out_shape=[SDS((F, D), bf16)] * 3,
        compiler_params=_CP, name="ffn_bwd_w")(h, dy, gate, up, dgate, dup)


def _wgrad(a, b):
    m, n = a.shape[1], b.shape[1]
    mc = 256

    def body(a_ref, b_ref, o_ref):
        o_ref[...] = _dot_tn(a_ref[...], b_ref[...]).astype(bf16)

    return pl.pallas_call(
        body, grid=(m // mc,),
        in_specs=[pl.BlockSpec((S, mc), lambda j: (0, j)), pl.BlockSpec((S, n), lambda j: (0, 0))],
        out_specs=pl.BlockSpec((mc, n), lambda j: (j, 0)),
        out_shape=SDS((m, n), bf16),
        compiler_params=_CP, name="wgrad")(a, b)


def _rope(t, c, sn, sp):
    return t * c + pltpu.roll(t, 120, 1) * sn + pltpu.roll(t, 8, 1) * sp


def _rope_bwd(d, c, sn, sp):
    return d * c + pltpu.roll(d * sn, 8, 1) + pltpu.roll(d * sp, 120, 1)


def _rope_tables(positions):
    inv_freq = ROPE_THETA ** (-jnp.arange(0, 16, 2, dtype=f32) / 16)
    ang = positions.reshape(S, 1).astype(f32) * inv_freq
    cos, sin = jnp.cos(ang), jnp.sin(ang)
    one = jnp.ones((S, 48), f32)
    zero8 = jnp.zeros((S, 8), f32)
    zero48 = jnp.zeros((S, 48), f32)
    c = jnp.concatenate([cos, cos, one], axis=1)
    sn = jnp.concatenate([-sin, zero8, zero48], axis=1)
    sp = jnp.concatenate([zero8, sin, zero48], axis=1)
    return tuple(jnp.concatenate([t, t], axis=1) for t in (c, sn, sp))


def _mix_in_fwd(x, g, wint, tabs, l=None):
    def body(x_ref, g_ref, w_ref, c_ref, sn_ref, sp_ref,
             h_ref, vp_ref, q1, k1, v1, q4, k4, v4, q16, k16, v16, scr):
        _, _, hn = _rms(x_ref[...], g_ref[...])
        h = hn.astype(bf16)
        h_ref[...] = h
        proj = _dot_nt(h, w_ref[...])
        vp_ref[...] = proj[:, :PW]
        c, sn, sp = c_ref[...], sn_ref[...], sp_ref[...]
        for kind, (o1, o4, o16) in enumerate(((q1, q4, q16), (k1, k4, k16), (v1, v4, v16))):
            for j in range(NG):
                t = proj[:, PW + kind * AW + 128 * j: PW + kind * AW + 128 * (j + 1)]
                if kind == 0:
                    t = _rope(t, c, sn, sp) * 0.125
                elif kind == 1:
                    t = _rope(t, c, sn, sp)
                scr[j] = t
                o1[:, _cols(j)] = t.astype(bf16)
            for r in range(4):
                for j in range(NG):
                    o4[r, :, _cols(j)] = scr[j, pl.ds(r, TM // 4, stride=4), :].astype(bf16)
            for r in range(16):
                for j in range(NG):
                    o16[r, :, _cols(j)] = scr[j, pl.ds(r, TM // 16, stride=16), :].astype(bf16)

    nat, d4, d16 = SDS((S, AW), bf16), SDS((4, S // 4, AW), bf16), SDS((16, S // 16, AW), bf16)
    return pl.pallas_call(
        body, grid=(S // TM,),
        in_specs=[_tile(D), _layer(1, D, l), _layer(PROJ, D, l), _tile(128), _tile(128), _tile(128)],
        out_specs=[_tile(D), _tile(PW)] + [_tile(AW)] * 3 + [_p4()] * 3 + [_p16()] * 3,
        out_shape=[SDS((S, D), bf16), SDS((S, PW), f32)] + [nat] * 3 + [d4] * 3 + [d16] * 3,
        scratch_shapes=[pltpu.VMEM((NG, TM, 128), f32)],
        compiler_params=_CP, name="mix_in_fwd")(x, g, wint, *tabs)


def _mix_in_bwd(dxo, x, g, wint, tabs, dvp, d1, d4, d16, l=None):
    def body(dxo_ref, x_ref, g_ref, w_ref, c_ref, sn_ref, sp_ref, dvp_ref,
             dq1, dk1, dv1, dq4, dk4, dv4, dq16, dk16, dv16,
             dx_ref, dproj_ref, dg_ref, s4, s16):
        c, sn, sp = c_ref[...], sn_ref[...], sp_ref[...]
        dproj_ref[:, :PW] = dvp_ref[...].astype(bf16)
        for kind, (a1, a4, a16) in enumerate(((dq1, dq4, dq16), (dk1, dk4, dk16), (dv1, dv4, dv16))):
            for r in range(4):
                for j in range(NG):
                    s4[j, pl.ds(r, TM // 4, stride=4), :] = a4[r, :, _cols(j)]
            for r in range(16):
                for j in range(NG):
                    s16[j, pl.ds(r, TM // 16, stride=16), :] = a16[r, :, _cols(j)]
            for j in range(NG):
                t = a1[:, _cols(j)] + s4[j] + s16[j]
                if kind == 0:
                    t = _rope_bwd(t * 0.125, c, sn, sp)
                elif kind == 1:
                    t = _rope_bwd(t, c, sn, sp)
                dproj_ref[:, PW + kind * AW + 128 * j: PW + kind * AW + 128 * (j + 1)] = t.astype(bf16)
        g = g_ref[...]
        r_, xh, _ = _rms(x_ref[...], g)
        dh = _dot_nn(dproj_ref[...], w_ref[...])

        @pl.when(pl.program_id(0) == 0)
        def _():
            dg_ref[...] = jnp.zeros_like(dg_ref)

        dg_ref[...] += jnp.sum(dh * xh, axis=0, keepdims=True)
        dx_ref[...] = dxo_ref[...] + _rms_bwd(dh, r_, xh, g)

    return pl.pallas_call(
        body, grid=(S // TM,),
        in_specs=[_tile(D), _tile(D), _layer(1, D, l), _layer(PROJ, D, l), _tile(128), _tile(128), _tile(128),
                  _tile(PW)] + [_tile(AW)] * 3 + [_p4()] * 3 + [_p16()] * 3,
        out_specs=[_tile(D), _tile(PROJ), _const((1, D))],
        out_shape=[SDS((S, D), f32), SDS((S, PROJ), bf16), SDS((1, D), f32)],
        scratch_shapes=[pltpu.VMEM((NG, TM, 128), f32), pltpu.VMEM((NG, TM, 128), f32)],
        compiler_params=_CP, name="mix_in_bwd")(dxo, x, g, wint, *tabs, dvp, *d1, *d4, *d16)


def _pool_sums(pad_ref, base, rows, adjoint):
    lane_group = lax.broadcasted_iota(jnp.int32, (rows, PW), 1) // 64
    sign = -1 if adjoint else 1

    def sh(o):
        return pad_ref[pl.ds(PAD + base + sign * o, rows), :]

    out = None
    acc = None
    lo, hi = 0, 0
    for gi, w in enumerate(POOL_WINDOWS):
        for o in list(range(-(w // 2), lo)) + list(range(hi, w - w // 2)):
            acc = sh(o) if acc is None else acc + sh(o)
        lo, hi = -(w // 2), w - w // 2
        out = acc if out is None else jnp.where(lane_group >= gi, acc, out)
    return out


def _pool_counts(base, rows):
    pos = base + lax.broadcasted_iota(jnp.int32, (rows, PW), 0)
    lane_group = lax.broadcasted_iota(jnp.int32, (rows, PW), 1) // 64
    cnt = None
    for gi, w in enumerate(POOL_WINDOWS):
        lo = jnp.maximum(pos - w // 2, 0)
        hi = jnp.minimum(pos + w - 1 - w // 2, S - 1)
        c = (hi - lo + 1).astype(f32)
        cnt = c if cnt is None else jnp.where(lane_group >= gi, c, cnt)
    return cnt


def _pool_fwd(vp, wbd, scale, l=None):
    ch = 256

    def body(vp_ref, w_ref, sc_ref, y_ref, diff_ref, pad):
        pad[pl.ds(0, PAD), :] = jnp.zeros((PAD, PW), f32)
        pad[pl.ds(PAD + S, PAD), :] = jnp.zeros((PAD, PW), f32)
        pad[pl.ds(PAD, S), :] = vp_ref[...]
        for b in range(S // ch):
            base = b * ch
            pooled = _pool_sums(pad, base, ch, False) / _pool_counts(base, ch)
            diff = (pooled - vp_ref[pl.ds(base, ch), :]).astype(bf16)
            diff_ref[pl.ds(base, ch), :] = diff
            y_ref[pl.ds(base, ch), :] = _dot_nn(diff, w_ref[...]) * sc_ref[...]

    whole = lambda shape: pl.BlockSpec(shape, lambda i: (0,) * len(shape))
    return pl.pallas_call(
        body, grid=(1,),
        in_specs=[whole((S, PW)), whole((PW, PW)), whole((1, PW))],
        out_specs=[whole((S, PW)), whole((S, PW))],
        out_shape=[SDS((S, PW), f32), SDS((S, PW), bf16)],
        scratch_shapes=[pltpu.VMEM((S + 2 * PAD, PW), f32)],
        compiler_params=_CP, name="pool_fwd")(vp, wbd, scale)


def _pool_bwd(dy, diff, wbd, scale, l=None):
    ch = 256

    def body(dy_ref, diff_ref, w_ref, sc_ref, dvp_ref, dw_ref, dsc_ref, pad):
        pad[pl.ds(0, PAD), :] = jnp.zeros((PAD, PW), f32)
        pad[pl.ds(PAD + S, PAD), :] = jnp.zeros((PAD, PW), f32)
        dw = jnp.zeros((PW, PW), f32)
        dsc = jnp.zeros((1, PW), f32)
        for b in range(S // ch):
            base = b * ch
            dy = dy_ref[pl.ds(base, ch), :]
            diff = diff_ref[pl.ds(base, ch), :]
            dsc = dsc + jnp.sum(dy * _dot_nn(diff, w_ref[...]), axis=0, keepdims=True)
            dz = (dy * sc_ref[...]).astype(bf16)
            dw = dw + _dot_tn(diff, dz)
            ddiff = _dot_nt(dz, w_ref[...])
            dvp_ref[pl.ds(base, ch), :] = -ddiff
            pad[pl.ds(PAD + base, ch), :] = ddiff / _pool_counts(base, ch)
        dw_ref[...] = dw
        dsc_ref[...] = dsc
        for b in range(S // ch):
            base = b * ch
            dvp_ref[pl.ds(base, ch), :] += _pool_sums(pad, base, ch, True)

    whole = lambda shape: pl.BlockSpec(shape, lambda i: (0,) * len(shape))
    return pl.pallas_call(
        body, grid=(1,),
        in_specs=[whole((S, PW)), whole((S, PW)), whole((PW, PW)), whole((1, PW))],
        out_specs=[whole((S, PW)), whole((PW, PW)), whole((1, PW))],
        out_shape=[SDS((S, PW), f32), SDS((PW, PW), f32), SDS((1, PW), f32)],
        scratch_shapes=[pltpu.VMEM((S + 2 * PAD, PW), f32)],
        compiler_params=_CP, name="pool_bwd")(dy, diff, wbd, scale)


def _attn_blocks(lc):
    bpc = lc // QB
    kw = min(2 * QB, lc)
    blocks = []
    for b in range(S // QB):
        t0 = (b % bpc) * QB
        ks_in = min(max(t0 - HALF, 0), lc - kw)
        blocks.append((b * QB, (b // bpc) * lc + ks_in, t0 - ks_in))
    return kw, blocks


def _attn_bias(bias_ref, kw, shifts):
    r = lax.broadcasted_iota(jnp.int32, (2 * QB, kw), 0) % QB
    c = lax.broadcasted_iota(jnp.int32, (2 * QB, kw), 1)
    for i, shift in enumerate(shifts):
        bias_ref[i] = jnp.where(jnp.abs(r + shift - c) <= HALF, 0.0, MASK_VALUE).astype(f32)


def _stack_heads(blk, head0):
    zero = jnp.zeros_like(blk)
    return jnp.concatenate([jnp.where(head0, blk, zero), jnp.where(head0, zero, blk)], axis=0)


def _attn_fwd(q, k, v, lc, after=None):
    kw, blocks = _attn_blocks(lc)
    shifts = sorted({b[2] for b in blocks})

    def body(q_ref, k_ref, v_ref, *refs):
        o_ref, lse_ref, bias_ref = refs[-3:]
        head0 = lax.broadcasted_iota(jnp.int32, (QB, 128), 1) < 64
        _attn_bias(bias_ref, kw, shifts)
        for row0, kstart, shift in blocks:
            q2 = _stack_heads(q_ref[pl.ds(row0, QB), :], head0)
            kb = k_ref[pl.ds(kstart, kw), :]
            vb = v_ref[pl.ds(kstart, kw), :]
            s = _dot_nt(q2, kb) + bias_ref[shifts.index(shift)]
            m = jnp.max(s, axis=-1, keepdims=True)
            p = jnp.exp(s - m)
            den = jnp.sum(p, axis=-1, keepdims=True)
            o2 = _dot_nn(p.astype(bf16), vb) / den
            lse2 = jnp.broadcast_to(m + jnp.log(den), (2 * QB, 128))
            o_ref[pl.ds(row0, QB), :] = jnp.where(head0, o2[:QB], o2[QB:])
            lse_ref[pl.ds(row0, QB), :] = jnp.where(head0, lse2[:QB], lse2[QB:])

    col = pl.BlockSpec((S, 128), lambda p: (0, p))
    extra = () if after is None else (after,)
    return pl.pallas_call(
        body, grid=(NG,), in_specs=[col, col, col] + [_ANY] * len(extra), out_specs=[col, col],
        out_shape=[SDS((S, AW), f32), SDS((S, AW), f32)],
        scratch_shapes=[pltpu.VMEM((len(shifts), 2 * QB, kw), f32)],
        compiler_params=_CP, name=f"attn_fwd_{lc}")(q, k, v, *extra)


def _attn_bwd(q, k, v, do, lse, delta, lc):
    kw, blocks = _attn_blocks(lc)
    shifts = sorted({b[2] for b in blocks})

    def body(q_ref, k_ref, v_ref, do_ref, lse_ref, dl_ref, dq_ref, dk_ref, dv_ref, bias_ref):
        head0 = lax.broadcasted_iota(jnp.int32, (QB, 128), 1) < 64
        _attn_bias(bias_ref, kw, shifts)
        dk_ref[...] = jnp.zeros_like(dk_ref)
        dv_ref[...] = jnp.zeros_like(dv_ref)
        for row0, kstart, shift in blocks:
            q2 = _stack_heads(q_ref[pl.ds(row0, QB), :], head0)
            do2 = _stack_heads(do_ref[pl.ds(row0, QB), :], head0)
            lse = lse_ref[pl.ds(row0, QB), :]
            dl = dl_ref[pl.ds(row0, QB), :]
            lse2 = jnp.concatenate([lse[:, 0:1], lse[:, 64:65]], axis=0)
            dl2 = jnp.concatenate([dl[:, 0:1], dl[:, 64:65]], axis=0)
            kb = k_ref[pl.ds(kstart, kw), :]
            vb = v_ref[pl.ds(kstart, kw), :]
            p = jnp.exp(_dot_nt(q2, kb) + bias_ref[shifts.index(shift)] - lse2)
            ds = (p * (_dot_nt(do2, vb) - dl2)).astype(bf16)
            dq2 = _dot_nn(ds, kb)
            dq_ref[pl.ds(row0, QB), :] = jnp.where(head0, dq2[:QB], dq2[QB:])
            dk_ref[pl.ds(kstart, kw), :] += _dot_tn(ds, q2)
            dv_ref[pl.ds(kstart, kw), :] += _dot_tn(p.astype(bf16), do2)

    col = pl.BlockSpec((S, 128), lambda p: (0, p))
    return pl.pallas_call(
        body, grid=(NG,), in_specs=[col] * 6, out_specs=[col] * 3,
        out_shape=[SDS((S, AW), f32)] * 3,
        scratch_shapes=[pltpu.VMEM((len(shifts), 2 * QB, kw), f32)],
        compiler_params=_CP, name=f"attn_bwd_{lc}")(q, k, v, do, lse, delta)


def _mix_out_fwd(x, ypool, o1, l1, o4, l4, o16, l16, wout, l=None):
    def body(x_ref, yp_ref, o1_ref, l1_ref, o4_ref, l4_ref, o16_ref, l16_ref, w_ref,
             xo_ref, mixed_ref, o_ref, lse1_ref, lse4_ref, lse16_ref, so4, sl4, so16, sl16, sl):
        for r in range(4):
            for j in range(NG):
                so4[j, pl.ds(r, TM // 4, stride=4), :] = o4_ref[r, :, _cols(j)]
                sl4[j, pl.ds(r, TM // 4, stride=4), :] = l4_ref[r, :, _cols(j)]
        for r in range(16):
            for j in range(NG):
                so16[j, pl.ds(r, TM // 16, stride=16), :] = o16_ref[r, :, _cols(j)]
                sl16[j, pl.ds(r, TM // 16, stride=16), :] = l16_ref[r, :, _cols(j)]
        mixed_ref[:, :PW] = yp_ref[...].astype(bf16)
        for j in range(NG):
            a, b, c = l1_ref[:, _cols(j)], sl4[j], sl16[j]
            m = jnp.maximum(jnp.maximum(a, b), c)
            wa, wb, wc = jnp.exp(a - m), jnp.exp(b - m), jnp.exp(c - m)
            den = wa + wb + wc
            y = (wa * o1_ref[:, _cols(j)] + wb * so4[j] + wc * so16[j]) / den
            lse = m + jnp.log(den)
            o_ref[:, _cols(j)] = y
            lse1_ref[:, _cols(j)] = lse
            sl[j] = lse
            mixed_ref[:, PW + 128 * j: PW + 128 * (j + 1)] = y.astype(bf16)
        for r in range(4):
            for j in range(NG):
                lse4_ref[r, :, _cols(j)] = sl[j, pl.ds(r, TM // 4, stride=4), :]
        for r in range(16):
            for j in range(NG):
                lse16_ref[r, :, _cols(j)] = sl[j, pl.ds(r, TM // 16, stride=16), :]
        xo_ref[...] = x_ref[...] + _dot_nn(mixed_ref[...], w_ref[...])

    scr = pltpu.VMEM((NG, TM, 128), f32)
    return pl.pallas_call(
        body, grid=(S // TM,),
        in_specs=[_tile(D), _tile(PW), _tile(AW), _tile(AW), _p4(), _p4(), _p16(), _p16(), _layer(D, D, l)],
        out_specs=[_tile(D), _tile(D), _tile(AW), _tile(AW), _p4(), _p16()],
        out_shape=[SDS((S, D), f32), SDS((S, D), bf16), SDS((S, AW), f32), SDS((S, AW), f32),
                   SDS((4, S // 4, AW), f32), SDS((16, S // 16, AW), f32)],
        scratch_shapes=[scr] * 5,
        compiler_params=_CP, name="mix_out_fwd")(x, ypool, o1, l1, o4, l4, o16, l16, wout)


def _segsum64(t):
    lane = lax.broadcasted_iota(jnp.int32, t.shape, 1)
    for s in (1, 2, 4, 8, 16, 32):
        t = t + jnp.where((lane & s) != 0, pltpu.roll(t, s, 1), pltpu.roll(t, 128 - s, 1))
    return t


def _mix_out_bwd(dxo, o, wout, l=None):
    def body(dxo_ref, o_ref, w_ref, dxb_ref, dyp_ref, do1, do4, do16, dl1, dl4, dl16, sdo, sdl):
        dxb = dxo_ref[...].astype(bf16)
        dxb_ref[...] = dxb
        dm = _dot_nt(dxb, w_ref[...])
        dyp_ref[...] = dm[:, :PW]
        for j in range(NG):
            d = dm[:, PW + 128 * j: PW + 128 * (j + 1)]
            dl = _segsum64(d * o_ref[:, _cols(j)])
            do1[:, _cols(j)] = d.astype(bf16)
            dl1[:, _cols(j)] = dl
            sdo[j] = d
            sdl[j] = dl
        for r in range(4):
            for j in range(NG):
                do4[r, :, _cols(j)] = sdo[j, pl.ds(r, TM // 4, stride=4), :].astype(bf16)
                dl4[r, :, _cols(j)] = sdl[j, pl.ds(r, TM // 4, stride=4), :]
        for r in range(16):
            for j in range(NG):
                do16[r, :, _cols(j)] = sdo[j, pl.ds(r, TM // 16, stride=16), :].astype(bf16)
                dl16[r, :, _cols(j)] = sdl[j, pl.ds(r, TM // 16, stride=16), :]

    scr = pltpu.VMEM((NG, TM, 128), f32)
    return pl.pallas_call(
        body, grid=(S // TM,),
        in_specs=[_tile(D), _tile(AW), _layer(D, D, l)],
        out_specs=[_tile(D), _tile(PW), _tile(AW), _p4(), _p16(), _tile(AW), _p4(), _p16()],
        out_shape=[SDS((S, D), bf16), SDS((S, PW), f32),
                   SDS((S, AW), bf16), SDS((4, S // 4, AW), bf16), SDS((16, S // 16, AW), bf16),
                   SDS((S, AW), f32), SDS((4, S // 4, AW), f32), SDS((16, S // 16, AW), f32)],
        scratch_shapes=[scr] * 2,
        compiler_params=_CP, name="mix_out_bwd")(dxo, o, wout)


def _loss_head(x, g, target):
    def body(x_ref, g_ref, t_ref, dx_ref, loss_ref, dg_ref):
        g = g_ref[...]
        r, xh, y = _rms(x_ref[...], g)
        err = y - t_ref[...]
        dy = err * (1.0 / D)

        @pl.when(pl.program_id(0) == 0)
        def _():
            loss_ref[...] = jnp.zeros_like(loss_ref)
            dg_ref[...] = jnp.zeros_like(dg_ref)

        loss_ref[...] += jnp.broadcast_to(0.5 * jnp.sum(jnp.mean(err * err, axis=-1, keepdims=True)), (1, D))
        dg_ref[...] += jnp.sum(dy * xh, axis=0, keepdims=True)
        dx_ref[...] = _rms_bwd(dy, r, xh, g)

    return pl.pallas_call(
        body, grid=(S // TM,),
        in_specs=[_tile(D), _const((1, D)), _tile(D)],
        out_specs=[_tile(D), _const((1, D)), _const((1, D))],
        out_shape=[SDS((S, D), f32), SDS((1, D), f32), SDS((1, D), f32)],
        compiler_params=_CP, name="loss_head")(x, g, target)


def _peer(k):
    x, y, c = lax.axis_index("x"), lax.axis_index("y"), lax.axis_index("c")
    px = 1 - x if k & 4 else x
    py = 1 - y if k & 2 else y
    pc = 1 - c if k & 1 else c
    return (px, py, pc), 4 * px + 2 * py + pc


def _all_gather(shards):
    n = len(shards)

    def body(*refs):
        ins, outs = refs[:n], refs[n:2 * n]
        send_sems, recv_sems, local_sems = refs[2 * n:]
        me, me_idx = _peer(0)
        sibling, sib_idx = _peer(1)
        far = [_peer(k) for k in (4, 2, 6)]
        far_sib = [_peer(k) for k in (5, 3, 7)]

        def rows(t, idx):
            r = ins[t].shape[1]
            return outs[t].at[:, pl.ds(idx * r, r), :]

        def copy(k, t, idx, to, src=None):
            return pltpu.make_async_remote_copy(
                src_ref=rows(t, idx) if src is None else src, dst_ref=rows(t, idx),
                send_sem=send_sems.at[k, t], recv_sem=recv_sems.at[k, t], device_id=to, device_id_type=_MESH)

        mine = [pltpu.make_async_copy(ins[t], rows(t, me_idx), local_sems.at[t]) for t in range(n)]
        for cp in mine:
            cp.start()
        first = [copy(0, t, me_idx, sibling, src=ins[t]) for t in range(n)]
        for j, (dev, _) in enumerate(far):
            first += [copy(1 + j, t, me_idx, dev, src=ins[t]) for t in range(n)]
        for cp in first:
            cp.start()
        passed = []
        for j, (_, idx) in enumerate(far):
            for t in range(n):
                copy(1 + j, t, idx, me).wait_recv()
                cp = copy(4 + j, t, idx, sibling)
                cp.start()
                passed.append(cp)
        for t in range(n):
            copy(0, t, sib_idx, me).wait_recv()
        for j, (_, idx) in enumerate(far_sib):
            for t in range(n):
                copy(4 + j, t, idx, me).wait_recv()
        for cp in first + passed:
            cp.wait_send()
        for cp in mine:
            cp.wait()

    return pl.pallas_call(
        body, in_specs=[_ANY] * n, out_specs=[_ANY] * n,
        out_shape=[SDS((a.shape[0], NDEV * a.shape[1], a.shape[2]), a.dtype) for a in shards],
        scratch_shapes=[pltpu.SemaphoreType.DMA((7, n)), pltpu.SemaphoreType.DMA((7, n)),
                        pltpu.SemaphoreType.DMA((n,))],
        name="all_gather_weights")(*shards)


def _hbm(a):
    return pltpu.with_memory_space_constraint(a, pltpu.HBM)


def _rows(ref, idx):
    r = ref.shape[0] // NDEV
    return ref.at[pl.ds(idx * r, r), :]


def _row_copy(ref, idx, send_sem, recv_sem, to):
    return pltpu.make_async_remote_copy(src_ref=_rows(ref, idx), dst_ref=_rows(ref, idx), send_sem=send_sem,
                                        recv_sem=recv_sem, device_id=to, device_id_type=_MESH)


_TOKEN = SDS((8, 128), f32)
_FAR = (4, 2, 6)
_FAR_SIB = (5, 3, 7)


def _ag_start(lands, after, l):
    n = len(lands)

    def body(*refs):
        zones, send_sems, recv_sems, token = refs[:n], refs[n + 1], refs[n + 2], refs[-1]
        _, me_idx = _peer(0)
        for k, mask in enumerate((1,) + _FAR):
            for t in range(n):
                _row_copy(zones[t], me_idx, send_sems.at[k * n + t], recv_sems.at[k * n + t], _peer(mask)[0]).start()
        token[...] = jnp.zeros_like(token)

    outs = pl.pallas_call(
        body, name=f"ag_start_{l}", in_specs=[_HBM] * n + [_ANY],
        out_specs=(_SEM, _SEM, *[_HBM] * n, pl.BlockSpec(memory_space=pltpu.VMEM)),
        out_shape=(pltpu.SemaphoreType.DMA((4 * n,)), pltpu.SemaphoreType.DMA((4 * n,)),
                   *[pltpu.HBM(a.shape, a.dtype) for a in lands], _TOKEN),
        input_output_aliases={t: 2 + t for t in range(n)}, compiler_params=_CP_SPLIT)(
            *[_hbm(a) for a in lands], after)
    return outs[0], outs[1], list(outs[2:2 + n]), outs[-1]


def _ag_pass(lands, recv_sems, after, l):
    n = len(lands)
    after = list(after) if isinstance(after, (list, tuple)) else [after]

    def body(*refs):
        zones, recv_sems = refs[:n], refs[n]
        psend, precv, token = refs[n + 1 + len(after)], refs[n + 2 + len(after)], refs[-1]
        me, _ = _peer(0)
        sibling, _ = _peer(1)
        for j, mask in enumerate(_FAR):
            idx = _peer(mask)[1]
            for t in range(n):
                _row_copy(zones[t], idx, psend.at[j * n + t], recv_sems.at[(1 + j) * n + t], me).wait_recv()
                _row_copy(zones[t], idx, psend.at[j * n + t], precv.at[j * n + t], sibling).start()
        token[...] = jnp.zeros_like(token)

    outs = pl.pallas_call(
        body, name=f"ag_pass_{l}", in_specs=[_HBM] * n + [_SEM] + [_ANY] * len(after),
        out_specs=(_SEM, _SEM, *[_HBM] * n, pl.BlockSpec(memory_space=pltpu.VMEM)),
        out_shape=(pltpu.SemaphoreType.DMA((3 * n,)), pltpu.SemaphoreType.DMA((3 * n,)),
                   *[pltpu.HBM(a.shape, a.dtype) for a in lands], _TOKEN),
        input_output_aliases={t: 2 + t for t in range(n)}, compiler_params=_CP_SPLIT)(*lands, recv_sems, *after)
    return outs[0], outs[1], list(outs[2:2 + n]), outs[-1]


def _ag_wait(lands, send_sems, recv_sems, psend, precv, after, l):
    n = len(lands)

    def body(*refs):
        zones = refs[:n]
        send_sems, recv_sems, psend, precv = refs[n:n + 4]
        me, me_idx = _peer(0)
        sib_idx = _peer(1)[1]
        for k in range(4):
            for t in range(n):
                _row_copy(zones[t], me_idx, send_sems.at[k * n + t], recv_sems.at[k * n + t], me).wait_send()
        for t in range(n):
            _row_copy(zones[t], sib_idx, send_sems.at[t], recv_sems.at[t], me).wait_recv()
        for j in range(3):
            mine, theirs = _peer(_FAR[j])[1], _peer(_FAR_SIB[j])[1]
            for t in range(n):
                _row_copy(zones[t], mine, psend.at[j * n + t], precv.at[j * n + t], me).wait_send()
                _row_copy(zones[t], theirs, psend.at[j * n + t], precv.at[j * n + t], me).wait_recv()

    outs = pl.pallas_call(
        body, name=f"ag_wait_{l}", in_specs=[_HBM] * n + [_SEM] * 4 + [_ANY], out_specs=tuple([_HBM] * n),
        out_shape=tuple(pltpu.HBM(a.shape, a.dtype) for a in lands),
        input_output_aliases={t: t for t in range(n)}, compiler_params=_CP_SPLIT)(
            *lands, send_sems, recv_sems, psend, precv, after)
    return list(outs)


def _xchg_src(ref, slot_ref, idx):
    return _rows(ref, idx) if ref.shape[0] == NDEV * slot_ref.shape[1] else ref


def _rs_start(srcs, slots, after, tag):
    n = len(srcs)
    after = list(after) if isinstance(after, (list, tuple)) else [after]

    def body(*refs):
        src, slot = refs[:n], refs[n:2 * n]
        send_sems, recv_sems, token = refs[2 * n + len(after)], refs[2 * n + len(after) + 1], refs[-1]
        _, me_idx = _peer(0)
        for k in range(1, NDEV):
            dev, idx = _peer(k)
            for t in range(n):
                pltpu.make_async_remote_copy(
                    src_ref=_xchg_src(src[t], slot[t], idx), dst_ref=slot[t].at[me_idx],
                    send_sem=send_sems.at[(k - 1) * n + t], recv_sem=recv_sems.at[(k - 1) * n + t],
                    device_id=dev, device_id_type=_MESH).start()
        token[...] = jnp.zeros_like(token)

    outs = pl.pallas_call(
        body, name=f"rs_start_{tag}", in_specs=[_HBM] * (2 * n) + [_ANY] * len(after),
        out_specs=(_SEM, _SEM, *[_HBM] * (2 * n), pl.BlockSpec(memory_space=pltpu.VMEM)),
        out_shape=(pltpu.SemaphoreType.DMA(((NDEV - 1) * n,)), pltpu.SemaphoreType.DMA(((NDEV - 1) * n,)),
                   *[pltpu.HBM(a.shape, a.dtype) for a in list(srcs) + list(slots)], _TOKEN),
        input_output_aliases={t: 2 + t for t in range(2 * n)}, compiler_params=_CP_SPLIT)(
            *[_hbm(a) for a in list(srcs) + list(slots)], *after)
    return outs[0], outs[1], list(outs[2:2 + n]), list(outs[2 + n:2 + 2 * n]), outs[-1]


def _rs_wait(srcs, slots, send_sems, recv_sems, after, tag):
    n = len(srcs)
    after = list(after) if isinstance(after, (list, tuple)) else [after]

    def body(*refs):
        src, slot, send_sems, recv_sems = refs[:n], refs[n:2 * n], refs[2 * n], refs[2 * n + 1]
        me, _ = _peer(0)
        for k in range(1, NDEV):
            idx = _peer(k)[1]
            for t in range(n):
                cp = pltpu.make_async_remote_copy(
                    src_ref=_xchg_src(src[t], slot[t], idx), dst_ref=slot[t].at[idx],
                    send_sem=send_sems.at[(k - 1) * n + t], recv_sem=recv_sems.at[(k - 1) * n + t],
                    device_id=me, device_id_type=_MESH)
                cp.wait_send()
                cp.wait_recv()

    outs = pl.pallas_call(
        body, name=f"rs_wait_{tag}", in_specs=[_HBM] * (2 * n) + [_SEM, _SEM] + [_ANY] * len(after),
        out_specs=tuple([_HBM] * (2 * n)),
        out_shape=tuple(pltpu.HBM(a.shape, a.dtype) for a in list(srcs) + list(slots)),
        input_output_aliases={t: t for t in range(2 * n)}, compiler_params=_CP_SPLIT)(
            *srcs, *slots, send_sems, recv_sems, *after)
    return list(outs[:n]), list(outs[n:])


def _sum_slots(slots, rb):
    r = slots.shape[1]

    def body(s_ref, o_ref):
        acc = s_ref[0].astype(f32)
        for s in range(1, NDEV):
            acc = acc + s_ref[s].astype(f32)
        o_ref[...] = acc

    return pl.pallas_call(
        body, grid=(r // rb,),
        in_specs=[pl.BlockSpec((NDEV, rb, D), lambda i: (0, i, 0))],
        out_specs=pl.BlockSpec((rb, D), lambda i: (i, 0)),
        out_shape=SDS((r, D), f32), compiler_params=_CP, name="sum_slots")(slots)


def _adamw(w, g, m, v):
    shape = w.shape
    cols = shape[-1]
    rows = w.size // cols
    rb = rows
    for cand in (512, 256, 128, 64, 32, 16, 8):
        if rows % cand == 0 and rows > cand:
            rb = cand
            break

    def body(w_ref, g_ref, m_ref, v_ref, d_ref, mo_ref, vo_ref):
        d_ref[...], mo_ref[...], vo_ref[...] = _adamw_math(w_ref[...], g_ref[...], m_ref[...], v_ref[...])

    spec = pl.BlockSpec((rb, cols), lambda i: (i, 0))
    outs = pl.pallas_call(
        body, grid=(rows // rb,), in_specs=[spec] * 4, out_specs=[spec] * 3,
        out_shape=[SDS((rows, cols), f32)] * 3, compiler_params=_CP, name="adamw")(
            *(a.reshape(rows, cols) for a in (w, g, m, v)))
    return tuple(o.reshape(shape) for o in outs)


def _adamw_math(w, g, m, v):
    m = ADAM_B1 * m + (1.0 - ADAM_B1) * g
    v = ADAM_B2 * v + (1.0 - ADAM_B2) * (g * g)
    m_hat = m / (1.0 - ADAM_B1 ** ADAM_STEP)
    v_hat = v / (1.0 - ADAM_B2 ** ADAM_STEP)
    return -ADAM_LR * (m_hat / (jnp.sqrt(v_hat) + ADAM_EPS) + ADAM_WD * w), m, v


def _reduce_adamw(acc, me, full, slots, w, m, v, l):
    _, r, _ = w.shape
    rb = r // 2 if r > 128 else r

    def body(me_ref, full_ref, slots_ref, w_ref, m_ref, v_ref, *refs):
        go_ref, d_ref, mo_ref, vo_ref = refs[-4:]
        own = full_ref[...].astype(f32)
        g = None
        for s in range(NDEV):
            part = jnp.where(me_ref[0] == s, own, slots_ref[s].astype(f32))
            g = part if g is None else g + part
        go_ref[...] = g
        d_ref[...], mo_ref[...], vo_ref[...] = _adamw_math(w_ref[...], g, m_ref[...], v_ref[...])

    steps = r // rb
    lay = pl.BlockSpec((None, rb, D), lambda i, me_ref: (l, i, 0))
    n_acc = 0 if acc is None else 4
    grid_spec = pltpu.PrefetchScalarGridSpec(
        num_scalar_prefetch=1, grid=(steps,),
        in_specs=[pl.BlockSpec((rb, D), lambda i, me_ref: (me_ref[0] * steps + i, 0)),
                  pl.BlockSpec((NDEV, rb, D), lambda i, me_ref: (0, i, 0)), lay, lay, lay] + [_ANY] * n_acc,
        out_specs=[lay] * 4)
    outs = pl.pallas_call(
        body, grid_spec=grid_spec, out_shape=[SDS(w.shape, f32)] * 4,
        input_output_aliases={6 + j: j for j in range(n_acc)},
        compiler_params=_CP, name="reduce_adamw")(me, full, slots, w, m, v, *(() if acc is None else acc))
    return tuple(outs)


_BIG = ("ffn1_w_gate", "ffn1_w_up", "ffn1_w_down", "w_in", "w_out", "ffn2_w_gate", "ffn2_w_up", "ffn2_w_down")
_TRANSPOSED = ("ffn1_w_gate", "ffn1_w_up", "w_in", "ffn2_w_gate", "ffn2_w_up")

def _block_diag(pool_w):
    out = jnp.zeros((L, PW, PW), pool_w.dtype)
    for gi in range(4):
        out = out.at[:, 64 * gi:64 * (gi + 1), 64 * gi:64 * (gi + 1)].set(pool_w[:, gi])
    return out


def kernel(x, positions, ffn1_norm, ffn1_w_gate, ffn1_w_up, ffn1_w_down, mix_norm, w_in, pool_w, pool_scale, w_out, ffn2_norm, ffn2_w_gate, ffn2_w_up, ffn2_w_down, final_norm, loss_target, m_ffn1_norm, m_ffn1_w_gate, m_ffn1_w_up, m_ffn1_w_down, m_mix_norm, m_w_in, m_pool_w, m_pool_scale, m_w_out, m_ffn2_norm, m_ffn2_w_gate, m_ffn2_w_up, m_ffn2_w_down, m_final_norm, v_ffn1_norm, v_ffn1_w_gate, v_ffn1_w_up, v_ffn1_w_down, v_mix_norm, v_w_in, v_pool_w, v_pool_scale, v_w_out, v_ffn2_norm, v_ffn2_w_gate, v_ffn2_w_up, v_ffn2_w_down, v_final_norm):
    weights = dict(ffn1_norm=ffn1_norm, ffn1_w_gate=ffn1_w_gate, ffn1_w_up=ffn1_w_up, ffn1_w_down=ffn1_w_down,
                   mix_norm=mix_norm, w_in=w_in, pool_w=pool_w, pool_scale=pool_scale, w_out=w_out,
                   ffn2_norm=ffn2_norm, ffn2_w_gate=ffn2_w_gate, ffn2_w_up=ffn2_w_up, ffn2_w_down=ffn2_w_down,
                   final_norm=final_norm)
    moms = dict(ffn1_norm=m_ffn1_norm, ffn1_w_gate=m_ffn1_w_gate, ffn1_w_up=m_ffn1_w_up, ffn1_w_down=m_ffn1_w_down,
                mix_norm=m_mix_norm, w_in=m_w_in, pool_w=m_pool_w, pool_scale=m_pool_scale, w_out=m_w_out,
                ffn2_norm=m_ffn2_norm, ffn2_w_gate=m_ffn2_w_gate, ffn2_w_up=m_ffn2_w_up, ffn2_w_down=m_ffn2_w_down,
                final_norm=m_final_norm)
    vels = dict(ffn1_norm=v_ffn1_norm, ffn1_w_gate=v_ffn1_w_gate, ffn1_w_up=v_ffn1_w_up, ffn1_w_down=v_ffn1_w_down,
                mix_norm=v_mix_norm, w_in=v_w_in, pool_w=v_pool_w, pool_scale=v_pool_scale, w_out=v_w_out,
                ffn2_norm=v_ffn2_norm, ffn2_w_gate=v_ffn2_w_gate, ffn2_w_up=v_ffn2_w_up, ffn2_w_down=v_ffn2_w_down,
                final_norm=v_final_norm)
    names = list(weights)

    me_idx = 4 * lax.axis_index("x") + 2 * lax.axis_index("y") + lax.axis_index("c")

    tr = lambda w: jnp.swapaxes(w, 1, 2).astype(bf16)
    shards = [tr(weights[nm]) if nm in _TRANSPOSED else weights[nm].astype(bf16) for nm in _BIG]

    def landing_zones(l, which):
        return [lax.dynamic_update_slice(lax.empty((NDEV * shards[t].shape[1], D), bf16), shards[t][l],
                                         (me_idx * shards[t].shape[1], 0)) for t in which]

    g_ffn1 = [ffn1_norm[l].reshape(1, D) for l in range(L)]
    g_mix = [mix_norm[l].reshape(1, D) for l in range(L)]
    g_ffn2 = [ffn2_norm[l].reshape(1, D) for l in range(L)]
    wbd_all = _block_diag(pool_w).astype(bf16)
    wbd = [wbd_all[l] for l in range(L)]
    pscale = [pool_scale[l].reshape(1, PW) for l in range(L)]
    tabs = _rope_tables(positions)
    flat = lambda a: a.reshape(S, AW)
    r4 = lambda a: a.reshape(4, S // 4, AW)
    r16 = lambda a: a.reshape(16, S // 16, AW)

    first, rest, whole = (0, 1, 2, 3), (4, 5, 6, 7), tuple(range(8))
    head = [a.reshape(a.shape[1], D) for a in _all_gather([shards[t][0:1] for t in first])]
    chain = {0: _ag_start(landing_zones(0, rest), head[0], "0"), 1: _ag_start(landing_zones(1, whole), head[0], "1")}
    gathered = [None] * L
    xs = x.reshape(S, D)
    saved = []
    for l in range(L):
        ga, gb = g_ffn1[l], g_ffn2[l]
        if l == 0:
            gt1, ut1, dn1, wint = head
            ga = ga + chain[0][3][0, 0] + chain[1][3][0, 0]
        else:
            gt1, ut1, dn1, wint, wout, gt2, ut2, dn2 = gathered[l]
        x0 = xs
        x1, gate1, up1 = _ffn_fwd(x0, ga, gt1, ut1, dn1)
        hmix, vp, q1, k1, v1, q4, k4, v4, q16, k16, v16 = _mix_in_fwd(x1, g_mix[l], wint, tabs)
        q4, k4, v4, q16, k16, v16 = map(flat, (q4, k4, v4, q16, k16, v16))
        ypool, diff = _pool_fwd(vp, wbd[l], pscale[l])
        o1, l1 = _attn_fwd(q1, k1, v1, S)
        o4, l4 = _attn_fwd(q4, k4, v4, S // 4)
        o16, l16 = _attn_fwd(q16, k16, v16, S // 16)
        if l == 0:
            send_sems, recv_sems, zones, _ = chain[0]
            psend, precv, zones, _ = _ag_pass(zones, recv_sems, [o1, o4, o16, ypool], "0")
            wout, gt2, ut2, dn2 = _ag_wait(zones, send_sems, recv_sems, psend, precv, o16, "0")
            gathered[0] = head + [wout, gt2, ut2, dn2]
        x2, mixed, o, lse1, lse4, lse16 = _mix_out_fwd(x1, ypool, o1, l1, r4(o4), r4(l4), r16(o16), r16(l16), wout)
        x3, gate2, up2 = _ffn_fwd(x2, gb, gt2, ut2, dn2)
        if l + 1 < L:
            send_sems, recv_sems, zones, _ = chain[l + 1]
            psend, precv, zones, token = _ag_pass(zones, recv_sems, x3, str(l + 1))
            if l + 2 < L:
                chain[l + 2] = _ag_start(landing_zones(l + 2, whole), token, str(l + 2))
                token = chain[l + 2][3]
            gathered[l + 1] = _ag_wait(zones, send_sems, recv_sems, psend, precv, token, str(l + 1))
        saved.append(dict(x0=x0, x1=x1, x2=x2, gate1=gate1, up1=up1, gate2=gate2, up2=up2, hmix=hmix, diff=diff,
                          qkv=((q1, k1, v1), (q4, k4, v4), (q16, k16, v16)), mixed=mixed, o=o,
                          lse=(lse1, flat(lse4), flat(lse16))))
        xs = x3

    dx, loss_part, d_final = _loss_head(xs, final_norm.reshape(1, D), loss_target.reshape(S, D))

    d_norm = {nm: [None] * L for nm in ("ffn1_norm", "mix_norm", "ffn2_norm")}
    d_poolw, d_pscale = [None] * L, [None] * L
    group_a = ("ffn2_w_gate", "ffn2_w_up", "ffn2_w_down", "w_out")
    group_b = ("ffn1_w_gate", "ffn1_w_up", "ffn1_w_down", "w_in")
    acc = {}

    as_rows = lambda a, nm: jnp.swapaxes(a, 1, 2) if nm in _TRANSPOSED else a
    w_rows = {nm: as_rows(weights[nm], nm) for nm in _BIG}
    m_rows = {nm: as_rows(moms[nm], nm) for nm in _BIG}
    v_rows = {nm: as_rows(vels[nm], nm) for nm in _BIG}
    me_arr = me_idx.reshape(1).astype(jnp.int32)

    def exchange(full, group, after, tag, extra=None):
        srcs = [full[nm] for nm in group]
        slots = [lax.empty((NDEV, g.shape[0] // NDEV, D), bf16) for g in srcs]
        if extra is not None:
            srcs, slots = srcs + [extra[0]], slots + [extra[1]]
        ssem, rsem, srcs, slots, token = _rs_start(srcs, slots, after, tag)
        return (srcs, slots, ssem, rsem, tag), token

    def update(l, group, flight, after):
        srcs, slots, ssem, rsem, tag = flight
        srcs, slots = _rs_wait(srcs, slots, ssem, rsem, after, tag)
        for nm, full_g, slots_g in zip(group, srcs, slots):
            acc[nm] = _reduce_adamw(acc.get(nm), me_arr, full_g, slots_g, w_rows[nm], m_rows[nm], v_rows[nm], l)
        return [acc[nm][0] for nm in group], slots

    flights = {}
    token_b = None
    for l in reversed(range(L)):
        sv = saved[l]
        gt1, ut1, dn1, wint, wout, gt2, ut2, dn2 = gathered[l]
        gb = g_ffn2[l] if token_b is None else g_ffn2[l] + token_b[0, 0]
        full = {}
        dx, dgate, dup, h, dy, d_norm["ffn2_norm"][l] = _ffn_bwd_d(sv["x2"], gb, sv["gate2"], sv["up2"], dx, gt2, ut2, dn2)
        full["ffn2_w_gate"], full["ffn2_w_up"], full["ffn2_w_down"] = _ffn_bwd_w(h, dy, sv["gate2"], sv["up2"], dgate, dup)

        dxb, dyp, do1, do4, do16, dl1, dl4, dl16 = _mix_out_bwd(dx, sv["o"], wout)
        full["w_out"] = _wgrad(sv["mixed"], dxb)
        flights[l, "a"], token_a = exchange(full, group_a, dxb, f"a{l}")
        dvp, dwbd, d_pscale[l] = _pool_bwd(dyp, sv["diff"], wbd[l], pscale[l] + token_a[0, 0])
        d_poolw[l] = jnp.stack([dwbd[64 * gi:64 * (gi + 1), 64 * gi:64 * (gi + 1)] for gi in range(4)])
        dos, dls = (do1, flat(do4), flat(do16)), (dl1, flat(dl4), flat(dl16))
        dqkv = []
        for b, lc in enumerate((S, S // 4, S // 16)):
            qb, kb, vb = sv["qkv"][b]
            dqkv.append(_attn_bwd(qb, kb, vb, dos[b], sv["lse"][b], dls[b], lc))
        d4 = tuple(r4(a) for a in dqkv[1])
        d16 = tuple(r16(a) for a in dqkv[2])
        dx, dproj, d_norm["mix_norm"][l] = _mix_in_bwd(dx, sv["x1"], g_mix[l], wint, tabs, dvp, dqkv[0], d4, d16)
        full["w_in"] = _wgrad(dproj, sv["hmix"])

        dx, dgate, dup, h, dy, d_norm["ffn1_norm"][l] = _ffn_bwd_d(sv["x0"], g_ffn1[l], sv["gate1"], sv["up1"], dx, gt1, ut1, dn1)
        full["ffn1_w_gate"], full["ffn1_w_up"], full["ffn1_w_down"] = _ffn_bwd_w(h, dy, sv["gate1"], sv["up1"], dgate, dup)

        after = dx
        if l + 1 < L and l + 1 >= 2:
            after, _ = update(l + 1, group_a, flights.pop((l + 1, "a")), after)
            after, _ = update(l + 1, group_b, flights.pop((l + 1, "b")), after)
        if l > 0:
            flights[l, "b"], token_b = exchange(full, group_b, after, f"b{l}")

    pad8 = lambda a: jnp.pad(a, ((0, 8 - a.shape[0]), (0, 0)))
    misc = jnp.concatenate([d_final, jnp.concatenate(d_pscale, axis=1), loss_part], axis=0)
    small = jnp.concatenate(
        [pad8(jnp.concatenate(d_norm[nm], axis=0)) for nm in ("ffn1_norm", "mix_norm", "ffn2_norm")]
        + [pad8(misc), jnp.stack(d_poolw).reshape(L * 16, D)], axis=0)
    small_slots = lax.dynamic_update_slice(lax.empty((NDEV, SMALL_ROWS, D), f32), small[None], (me_idx, 0, 0))
    flights[0, "b"], token_b = exchange(full, group_b, dx, "b0", extra=(small, small_slots))

    after = token_b
    for key in [(1, "a"), (1, "b"), (0, "a")]:
        after, _ = update(key[0], group_a if key[1] == "a" else group_b, flights.pop(key), after)
    after, slots_b0 = update(0, group_b, flights.pop((0, "b")), after)

    sm = _sum_slots(slots_b0[-1], SMALL_ROWS)
    grads = {}
    grads["ffn1_norm"], grads["mix_norm"], grads["ffn2_norm"] = sm[0:L], sm[8:8 + L], sm[16:16 + L]
    grads["final_norm"] = sm[24]
    grads["pool_scale"] = sm[25].reshape(L, PW)
    grads["pool_w"] = sm[32:32 + L * 16].reshape(L, 4, 64, 64)
    loss = sm[26, 0]
    upd = {nm: _adamw(weights[nm], grads[nm], moms[nm], vels[nm]) for nm in names if nm not in _BIG}
    for nm in _BIG:
        grads[nm], upd[nm] = as_rows(acc[nm][0], nm), tuple(as_rows(a, nm) for a in acc[nm][1:])
    return (loss, dx.reshape(1, S, D), *[grads[nm] for nm in names], *[upd[nm][0] for nm in names],
            *[upd[nm][1] for nm in names], *[upd[nm][2] for nm in names])
```

```python
import jax
import jax.numpy as jnp
from jax import lax
from jax.experimental import pallas as pl
from jax.experimental.pallas import tpu as pltpu

f32 = jnp.float32
bf16 = jnp.bfloat16
SDS = jax.ShapeDtypeStruct

D = 1024
S = 2048
F = 2816
L = 4
PW = 256
AW = 768
PROJ = PW + 3 * AW
NDEV = 8
TM = 256
QB = 128
HALF = 64
NG = AW // 128
NORM_EPS = 1e-6
MASK_VALUE = -1e30
ROPE_THETA = 500000.0
ADAM_LR, ADAM_B1, ADAM_B2, ADAM_EPS, ADAM_WD, ADAM_STEP = 0.001, 0.9, 0.999, 1e-08, 0.01, 10
POOL_WINDOWS = (2, 4, 8, 16)
PAD = 8
SMALL_ROWS = 96
VMEM_LIMIT = 56 * 1024 * 1024

_CP = pltpu.CompilerParams(vmem_limit_bytes=VMEM_LIMIT)
_ANY = pl.BlockSpec(memory_space=pl.ANY)
_HBM = pl.BlockSpec(memory_space=pltpu.HBM)
_SEM = pl.BlockSpec(memory_space=pltpu.SEMAPHORE)
_MESH = pl.DeviceIdType.MESH
_CP_SPLIT = pltpu.CompilerParams(has_side_effects=pltpu.SideEffectType.DATAFLOW_SIDE_EFFECTING)


def _dot_nn(a, b):
    return lax.dot_general(a, b, (((1,), (0,)), ((), ())), preferred_element_type=f32)


def _dot_nt(a, b):
    return lax.dot_general(a, b, (((1,), (1,)), ((), ())), preferred_element_type=f32)


def _dot_tn(a, b):
    return lax.dot_general(a, b, (((0,), (0,)), ((), ())), preferred_element_type=f32)


def _rms(x, g):
    r = lax.rsqrt(jnp.mean(x * x, axis=-1, keepdims=True) + NORM_EPS)
    xh = x * r
    return r, xh, xh * g


def _rms_bwd(dh, r, xh, g):
    dxh = dh * g
    return r * (dxh - xh * jnp.mean(dxh * xh, axis=-1, keepdims=True))


def _tile(cols):
    return pl.BlockSpec((TM, cols), lambda i: (i, 0))


def _const(shape):
    return pl.BlockSpec(shape, lambda i: (0,) * len(shape))


def _layer(rows, cols, l=None):
    return pl.BlockSpec((rows, cols), lambda i: (0, 0), pipeline_mode=pl.Buffered(1))


def _p4(cols=AW):
    return pl.BlockSpec((4, TM // 4, cols), lambda i: (0, i, 0))


def _p16(cols=AW):
    return pl.BlockSpec((16, TM // 16, cols), lambda i: (0, i, 0))


def _cols(j):
    return slice(128 * j, 128 * (j + 1))


def _ffn_fwd(x, g, gt, ut, dn, l=None):
    def body(x_ref, g_ref, gt_ref, ut_ref, dn_ref, xo_ref, gate_ref, up_ref):
        x = x_ref[...]
        _, _, hn = _rms(x, g_ref[...])
        h = hn.astype(bf16)
        gate = _dot_nt(h, gt_ref[...])
        up = _dot_nt(h, ut_ref[...])
        gate_ref[...] = gate.astype(bf16)
        up_ref[...] = up.astype(bf16)
        a = (gate * jax.nn.sigmoid(gate) * up).astype(bf16)
        xo_ref[...] = x + 0.5 * _dot_nn(a, dn_ref[...])

    return pl.pallas_call(
        body, grid=(S // TM,),
        in_specs=[_tile(D), _layer(1, D, l), _layer(F, D, l), _layer(F, D, l), _layer(F, D, l)],
        out_specs=[_tile(D), _tile(F), _tile(F)],
        out_shape=[SDS((S, D), f32), SDS((S, F), bf16), SDS((S, F), bf16)],
        compiler_params=_CP, name="ffn_fwd")(x, g, gt, ut, dn)


def _ffn_bwd_d(x, g, gate, up, dxo, gt, ut, dn, l=None):
    def body(x_ref, g_ref, gate_ref, up_ref, dxo_ref, gt_ref, ut_ref, dn_ref,
             dx_ref, dgate_ref, dup_ref, h_ref, dy_ref, dg_ref):
        x = x_ref[...]
        g = g_ref[...]
        r, xh, hn = _rms(x, g)
        h_ref[...] = hn.astype(bf16)
        dxo = dxo_ref[...]
        dy = (0.5 * dxo).astype(bf16)
        dy_ref[...] = dy
        da = _dot_nt(dy, dn_ref[...])
        gate = gate_ref[...].astype(f32)
        up = up_ref[...].astype(f32)
        sg = jax.nn.sigmoid(gate)
        dgate = (da * up * (sg * (1.0 + gate * (1.0 - sg)))).astype(bf16)
        dup = (da * (gate * sg)).astype(bf16)
        dgate_ref[...] = dgate
        dup_ref[...] = dup
        dh = _dot_nn(dgate, gt_ref[...]) + _dot_nn(dup, ut_ref[...])

        @pl.when(pl.program_id(0) == 0)
        def _():
            dg_ref[...] = jnp.zeros_like(dg_ref)

        dg_ref[...] += jnp.sum(dh * xh, axis=0, keepdims=True)
        dx_ref[...] = dxo + _rms_bwd(dh, r, xh, g)

    return pl.pallas_call(
        body, grid=(S // TM,),
        in_specs=[_tile(D), _layer(1, D, l), _tile(F), _tile(F), _tile(D),
                  _layer(F, D, l), _layer(F, D, l), _layer(F, D, l)],
        out_specs=[_tile(D), _tile(F), _tile(F), _tile(D), _tile(D), _const((1, D))],
        out_shape=[SDS((S, D), f32), SDS((S, F), bf16), SDS((S, F), bf16), SDS((S, D), bf16),
                   SDS((S, D), bf16), SDS((1, D), f32)],
        compiler_params=_CP, name="ffn_bwd_d")(x, g, gate, up, dxo, gt, ut, dn)


def _ffn_bwd_w(h, dy, gate, up, dgate, dup):
    fc = 256

    def body(h_ref, dy_ref, gate_ref, up_ref, dgate_ref, dup_ref, dgt_ref, dut_ref, ddn_ref):
        gate = gate_ref[...].astype(f32)
        a = (gate * jax.nn.sigmoid(gate) * up_ref[...].astype(f32)).astype(bf16)
        ddn_ref[...] = _dot_tn(a, dy_ref[...]).astype(bf16)
        h = h_ref[...]
        dgt_ref[...] = _dot_tn(dgate_ref[...], h).astype(bf16)
        dut_ref[...] = _dot_tn(dup_ref[...], h).astype(bf16)

    col = pl.BlockSpec((S, fc), lambda j: (0, j))
    row = pl.BlockSpec((fc, D), lambda j: (j, 0))
    full = pl.BlockSpec((S, D), lambda j: (0, 0))
    return pl.pallas_call(
        body, grid=(F // fc,),
        in_specs=[full, full, col, col, col, col],
        out_specs=[row, row, row],
        out_shape=[SDS((F, D), bf16)] * 3,
        compiler_params=_CP, name="ffn_bwd_w")(h, dy, gate, up, dgate, dup)


def _wgrad(a, b):
    m, n = a.shape[1], b.shape[1]
    mc = 256

    def body(a_ref, b_ref, o_ref):
        o_ref[...] = _dot_tn(a_ref[...], b_ref[...]).astype(bf16)

    return pl.pallas_call(
        body, grid=(m // mc,),
        in_specs=[pl.BlockSpec((S, mc), lambda j: (0, j)), pl.BlockSpec((S, n), lambda j: (0, 0))],
        out_specs=pl.BlockSpec((mc, n), lambda j: (j, 0)),
        out_shape=SDS((m, n), bf16),
        compiler_params=_CP, name="wgrad")(a, b)


def _rope(t, c, sn, sp):
    return t * c + pltpu.roll(t, 120, 1) * sn + pltpu.roll(t, 8, 1) * sp


def _rope_bwd(d, c, sn, sp):
    return d * c + pltpu.roll(d * sn, 8, 1) + pltpu.roll(d * sp, 120, 1)


def _rope_tables(positions):
    inv_freq = ROPE_THETA ** (-jnp.arange(0, 16, 2, dtype=f32) / 16)
    ang = positions.reshape(S, 1).astype(f32) * inv_freq
    cos, sin = jnp.cos(ang), jnp.sin(ang)
    one = jnp.ones((S, 48), f32)
    zero8 = jnp.zeros((S, 8), f32)
    zero48 = jnp.zeros((S, 48), f32)
    c = jnp.concatenate([cos, cos, one], axis=1)
    sn = jnp.concatenate([-sin, zero8, zero48], axis=1)
    sp = jnp.concatenate([zero8, sin, zero48], axis=1)
    return tuple(jnp.concatenate([t, t], axis=1) for t in (c, sn, sp))


def _mix_in_fwd(x, g, wint, tabs, l=None):
    def body(x_ref, g_ref, w_ref, c_ref, sn_ref, sp_ref,
             h_ref, vp_ref, q1, k1, v1, q4, k4, v4, q16, k16, v16, scr):
        _, _, hn = _rms(x_ref[...], g_ref[...])
        h = hn.astype(bf16)
        h_ref[...] = h
        proj = _dot_nt(h, w_ref[...])
        vp_ref[...] = proj[:, :PW]
        c, sn, sp = c_ref[...], sn_ref[...], sp_ref[...]
        for kind, (o1, o4, o16) in enumerate(((q1, q4, q16), (k1, k4, k16), (v1, v4, v16))):
            for j in range(NG):
                t = proj[:, PW + kind * AW + 128 * j: PW + kind * AW + 128 * (j + 1)]
                if kind == 0:
                    t = _rope(t, c, sn, sp) * 0.125
                elif kind == 1:
                    t = _rope(t, c, sn, sp)
                scr[j] = t
                o1[:, _cols(j)] = t.astype(bf16)
            for r in range(4):
                for j in range(NG):
                    o4[r, :, _cols(j)] = scr[j, pl.ds(r, TM // 4, stride=4), :].astype(bf16)
            for r in range(16):
                for j in range(NG):
                    o16[r, :, _cols(j)] = scr[j, pl.ds(r, TM // 16, stride=16), :].astype(bf16)

    nat, d4, d16 = SDS((S, AW), bf16), SDS((4, S // 4, AW), bf16), SDS((16, S // 16, AW), bf16)
    return pl.pallas_call(
        body, grid=(S // TM,),
        in_specs=[_tile(D), _layer(1, D, l), _layer(PROJ, D, l), _tile(128), _tile(128), _tile(128)],
        out_specs=[_tile(D), _tile(PW)] + [_tile(AW)] * 3 + [_p4()] * 3 + [_p16()] * 3,
        out_shape=[SDS((S, D), bf16), SDS((S, PW), f32)] + [nat] * 3 + [d4] * 3 + [d16] * 3,
        scratch_shapes=[pltpu.VMEM((NG, TM, 128), f32)],
        compiler_params=_CP, name="mix_in_fwd")(x, g, wint, *tabs)


def _mix_in_bwd(dxo, x, g, wint, tabs, dvp, d1, d4, d16, l=None):
    def body(dxo_ref, x_ref, g_ref, w_ref, c_ref, sn_ref, sp_ref, dvp_ref,
             dq1, dk1, dv1, dq4, dk4, dv4, dq16, dk16, dv16,
             dx_ref, dproj_ref, dg_ref, s4, s16):
        c, sn, sp = c_ref[...], sn_ref[...], sp_ref[...]
        dproj_ref[:, :PW] = dvp_ref[...].astype(bf16)
        for kind, (a1, a4, a16) in enumerate(((dq1, dq4, dq16), (dk1, dk4, dk16), (dv1, dv4, dv16))):
            for r in range(4):
                for j in range(NG):
                    s4[j, pl.ds(r, TM // 4, stride=4), :] = a4[r, :, _cols(j)]
            for r in range(16):
                for j in range(NG):
                    s16[j, pl.ds(r, TM // 16, stride=16), :] = a16[r, :, _cols(j)]
            for j in range(NG):
                t = a1[:, _cols(j)] + s4[j] + s16[j]
                if kind == 0:
                    t = _rope_bwd(t * 0.125, c, sn, sp)
                elif kind == 1:
                    t = _rope_bwd(t, c, sn, sp)
                dproj_ref[:, PW + kind * AW + 128 * j: PW + kind * AW + 128 * (j + 1)] = t.astype(bf16)
        g = g_ref[...]
        r_, xh, _ = _rms(x_ref[...], g)
        dh = _dot_nn(dproj_ref[...], w_ref[...])

        @pl.when(pl.program_id(0) == 0)
        def _():
            dg_ref[...] = jnp.zeros_like(dg_ref)

        dg_ref[...] += jnp.sum(dh * xh, axis=0, keepdims=True)
        dx_ref[...] = dxo_ref[...] + _rms_bwd(dh, r_, xh, g)

    return pl.pallas_call(
        body, grid=(S // TM,),
        in_specs=[_tile(D), _tile(D), _layer(1, D, l), _layer(PROJ, D, l), _tile(128), _tile(128), _tile(128),
                  _tile(PW)] + [_tile(AW)] * 3 + [_p4()] * 3 + [_p16()] * 3,
        out_specs=[_tile(D), _tile(PROJ), _const((1, D))],
        out_shape=[SDS((S, D), f32), SDS((S, PROJ), bf16), SDS((1, D), f32)],
        scratch_shapes=[pltpu.VMEM((NG, TM, 128), f32), pltpu.VMEM((NG, TM, 128), f32)],
        compiler_params=_CP, name="mix_in_bwd")(dxo, x, g, wint, *tabs, dvp, *d1, *d4, *d16)


def _pool_sums(pad_ref, base, rows, adjoint):
    lane_group = lax.broadcasted_iota(jnp.int32, (rows, PW), 1) // 64
    sign = -1 if adjoint else 1

    def sh(o):
        return pad_ref[pl.ds(PAD + base + sign * o, rows), :]

    out = None
    acc = None
    lo, hi = 0, 0
    for gi, w in enumerate(POOL_WINDOWS):
        for o in list(range(-(w // 2), lo)) + list(range(hi, w - w // 2)):
            acc = sh(o) if acc is None else acc + sh(o)
        lo, hi = -(w // 2), w - w // 2
        out = acc if out is None else jnp.where(lane_group >= gi, acc, out)
    return out


def _pool_counts(base, rows):
    pos = base + lax.broadcasted_iota(jnp.int32, (rows, PW), 0)
    lane_group = lax.broadcasted_iota(jnp.int32, (rows, PW), 1) // 64
    cnt = None
    for gi, w in enumerate(POOL_WINDOWS):
        lo = jnp.maximum(pos - w // 2, 0)
        hi = jnp.minimum(pos + w - 1 - w // 2, S - 1)
        c = (hi - lo + 1).astype(f32)
        cnt = c if cnt is None else jnp.where(lane_group >= gi, c, cnt)
    return cnt


def _pool_fwd(vp, wbd, scale, l=None):
    ch = 256

    def body(vp_ref, w_ref, sc_ref, y_ref, diff_ref, pad):
        pad[pl.ds(0, PAD), :] = jnp.zeros((PAD, PW), f32)
        pad[pl.ds(PAD + S, PAD), :] = jnp.zeros((PAD, PW), f32)
        pad[pl.ds(PAD, S), :] = vp_ref[...]
        for b in range(S // ch):
            base = b * ch
            pooled = _pool_sums(pad, base, ch, False) / _pool_counts(base, ch)
            diff = (pooled - vp_ref[pl.ds(base, ch), :]).astype(bf16)
            diff_ref[pl.ds(base, ch), :] = diff
            y_ref[pl.ds(base, ch), :] = _dot_nn(diff, w_ref[...]) * sc_ref[...]

    whole = lambda shape: pl.BlockSpec(shape, lambda i: (0,) * len(shape))
    return pl.pallas_call(
        body, grid=(1,),
        in_specs=[whole((S, PW)), whole((PW, PW)), whole((1, PW))],
        out_specs=[whole((S, PW)), whole((S, PW))],
        out_shape=[SDS((S, PW), f32), SDS((S, PW), bf16)],
        scratch_shapes=[pltpu.VMEM((S + 2 * PAD, PW), f32)],
        compiler_params=_CP, name="pool_fwd")(vp, wbd, scale)


def _pool_bwd(dy, diff, wbd, scale, l=None):
    ch = 256

    def body(dy_ref, diff_ref, w_ref, sc_ref, dvp_ref, dw_ref, dsc_ref, pad):
        pad[pl.ds(0, PAD), :] = jnp.zeros((PAD, PW), f32)
        pad[pl.ds(PAD + S, PAD), :] = jnp.zeros((PAD, PW), f32)
        dw = jnp.zeros((PW, PW), f32)
        dsc = jnp.zeros((1, PW), f32)
        for b in range(S // ch):
            base = b * ch
            dy = dy_ref[pl.ds(base, ch), :]
            diff = diff_ref[pl.ds(base, ch), :]
            dsc = dsc + jnp.sum(dy * _dot_nn(diff, w_ref[...]), axis=0, keepdims=True)
            dz = (dy * sc_ref[...]).astype(bf16)
            dw = dw + _dot_tn(diff, dz)
            ddiff = _dot_nt(dz, w_ref[...])
            dvp_ref[pl.ds(base, ch), :] = -ddiff
            pad[pl.ds(PAD + base, ch), :] = ddiff / _pool_counts(base, ch)
        dw_ref[...] = dw
        dsc_ref[...] = dsc
        for b in range(S // ch):
            base = b * ch
            dvp_ref[pl.ds(base, ch), :] += _pool_sums(pad, base, ch, True)

    whole = lambda shape: pl.BlockSpec(shape, lambda i: (0,) * len(shape))
    return pl.pallas_call(
        body, grid=(1,),
        in_specs=[whole((S, PW)), whole((S, PW)), whole((PW, PW)), whole((1, PW))],
        out_specs=[whole((S, PW)), whole((PW, PW)), whole((1, PW))],
        out_shape=[SDS((S, PW), f32), SDS((PW, PW), f32), SDS((1, PW), f32)],
        scratch_shapes=[pltpu.VMEM((S + 2 * PAD, PW), f32)],
        compiler_params=_CP, name="pool_bwd")(dy, diff, wbd, scale)


def _attn_blocks(lc):
    bpc = lc // QB
    kw = min(2 * QB, lc)
    blocks = []
    for b in range(S // QB):
        t0 = (b % bpc) * QB
        ks_in = min(max(t0 - HALF, 0), lc - kw)
        blocks.append((b * QB, (b // bpc) * lc + ks_in, t0 - ks_in))
    return kw, blocks


def _attn_bias(bias_ref, kw, shifts):
    r = lax.broadcasted_iota(jnp.int32, (2 * QB, kw), 0) % QB
    c = lax.broadcasted_iota(jnp.int32, (2 * QB, kw), 1)
    for i, shift in enumerate(shifts):
        bias_ref[i] = jnp.where(jnp.abs(r + shift - c) <= HALF, 0.0, MASK_VALUE).astype(f32)


def _head_put(stats, pair, v0, v1, lane):
    return jnp.where(lane == 2 * pair, v0, jnp.where(lane == 2 * pair + 1, v1, stats))


def _head_cols(stats, pair, lane):
    c0 = jnp.sum(jnp.where(lane == 2 * pair, stats, 0.0), axis=-1, keepdims=True)
    c1 = jnp.sum(jnp.where(lane == 2 * pair + 1, stats, 0.0), axis=-1, keepdims=True)
    return jnp.concatenate([c0, c1], axis=0)


def _head_spread(stats, pair, head0):
    return jnp.where(head0, stats[:, 2 * pair:2 * pair + 1], stats[:, 2 * pair + 1:2 * pair + 2])


def _stack_heads(blk, head0):
    zero = jnp.zeros_like(blk)
    return jnp.concatenate([jnp.where(head0, blk, zero), jnp.where(head0, zero, blk)], axis=0)


def _attn_fwd(q, k, v, lc, after=None):
    kw, blocks = _attn_blocks(lc)
    shifts = sorted({b[2] for b in blocks})

    def body(q_ref, k_ref, v_ref, *refs):
        o_ref, lse_ref, bias_ref = refs[-3:]
        lane = lax.broadcasted_iota(jnp.int32, (QB, 128), 1)
        head0 = lane < 64
        pair = pl.program_id(0)
        _attn_bias(bias_ref, kw, shifts)

        @pl.when(pair == 0)
        def _():
            lse_ref[...] = jnp.zeros_like(lse_ref)

        for row0, kstart, shift in blocks:
            q2 = _stack_heads(q_ref[pl.ds(row0, QB), :], head0)
            kb = k_ref[pl.ds(kstart, kw), :]
            vb = v_ref[pl.ds(kstart, kw), :]
            s = _dot_nt(q2, kb) + bias_ref[shifts.index(shift)]
            m = jnp.max(s, axis=-1, keepdims=True)
            p = jnp.exp(s - m)
            den = jnp.sum(p, axis=-1, keepdims=True)
            o2 = _dot_nn(p.astype(bf16), vb) / den
            lse2 = m + jnp.log(den)
            o_ref[pl.ds(row0, QB), :] = jnp.where(head0, o2[:QB], o2[QB:])
            lse_ref[pl.ds(row0, QB), :] = _head_put(lse_ref[pl.ds(row0, QB), :], pair, lse2[:QB], lse2[QB:], lane)

    col = pl.BlockSpec((S, 128), lambda p: (0, p))
    extra = () if after is None else (after,)
    return pl.pallas_call(
        body, grid=(NG,), in_specs=[col, col, col] + [_ANY] * len(extra),
        out_specs=[col, pl.BlockSpec((S, 128), lambda p: (0, 0))],
        out_shape=[SDS((S, AW), f32), SDS((S, 128), f32)],
        scratch_shapes=[pltpu.VMEM((len(shifts), 2 * QB, kw), f32)],
        compiler_params=_CP, name=f"attn_fwd_{lc}")(q, k, v, *extra)


def _attn_bwd(q, k, v, do, lse, delta, lc):
    kw, blocks = _attn_blocks(lc)
    shifts = sorted({b[2] for b in blocks})

    def body(q_ref, k_ref, v_ref, do_ref, lse_ref, dl_ref, dq_ref, dk_ref, dv_ref, bias_ref):
        lane = lax.broadcasted_iota(jnp.int32, (QB, 128), 1)
        head0 = lane < 64
        pair = pl.program_id(0)
        _attn_bias(bias_ref, kw, shifts)
        dk_ref[...] = jnp.zeros_like(dk_ref)
        dv_ref[...] = jnp.zeros_like(dv_ref)
        for row0, kstart, shift in blocks:
            q2 = _stack_heads(q_ref[pl.ds(row0, QB), :], head0)
            do2 = _stack_heads(do_ref[pl.ds(row0, QB), :], head0)
            lse2 = _head_cols(lse_ref[pl.ds(row0, QB), :], pair, lane)
            dl2 = _head_cols(dl_ref[pl.ds(row0, QB), :], pair, lane)
            kb = k_ref[pl.ds(kstart, kw), :]
            vb = v_ref[pl.ds(kstart, kw), :]
            p = jnp.exp(_dot_nt(q2, kb) + bias_ref[shifts.index(shift)] - lse2)
            ds = (p * (_dot_nt(do2, vb) - dl2)).astype(bf16)
            dq2 = _dot_nn(ds, kb)
            dq_ref[pl.ds(row0, QB), :] = jnp.where(head0, dq2[:QB], dq2[QB:])
            dk_ref[pl.ds(kstart, kw), :] += _dot_tn(ds, q2)
            dv_ref[pl.ds(kstart, kw), :] += _dot_tn(p.astype(bf16), do2)

    col = pl.BlockSpec((S, 128), lambda p: (0, p))
    stats = pl.BlockSpec((S, 128), lambda p: (0, 0))
    return pl.pallas_call(
        body, grid=(NG,), in_specs=[col] * 4 + [stats] * 2, out_specs=[col] * 3,
        out_shape=[SDS((S, AW), f32)] * 3,
        scratch_shapes=[pltpu.VMEM((len(shifts), 2 * QB, kw), f32)],
        compiler_params=_CP, name=f"attn_bwd_{lc}")(q, k, v, do, lse, delta)


def _mix_out_fwd(x, ypool, o1, l1, o4, l4, o16, l16, wout, l=None):
    def body(x_ref, yp_ref, o1_ref, l1_ref, o4_ref, l4_ref, o16_ref, l16_ref, w_ref,
             xo_ref, mixed_ref, o_ref, lse1_ref, lse4_ref, lse16_ref, so4, so16, sl4, sl16, sl):
        head0 = lax.broadcasted_iota(jnp.int32, (TM, 128), 1) < 64
        for r in range(4):
            sl4[pl.ds(r, TM // 4, stride=4), :] = l4_ref[r]
            for j in range(NG):
                so4[j, pl.ds(r, TM // 4, stride=4), :] = o4_ref[r, :, _cols(j)]
        for r in range(16):
            sl16[pl.ds(r, TM // 16, stride=16), :] = l16_ref[r]
            for j in range(NG):
                so16[j, pl.ds(r, TM // 16, stride=16), :] = o16_ref[r, :, _cols(j)]
        a, b, c = l1_ref[...], sl4[...], sl16[...]
        m = jnp.maximum(jnp.maximum(a, b), c)
        wa, wb, wc = jnp.exp(a - m), jnp.exp(b - m), jnp.exp(c - m)
        den = wa + wb + wc
        wa, wb, wc = wa / den, wb / den, wc / den
        lse = m + jnp.log(den)
        lse1_ref[...] = lse
        sl[...] = lse
        mixed_ref[:, :PW] = yp_ref[...].astype(bf16)
        for j in range(NG):
            y = (_head_spread(wa, j, head0) * o1_ref[:, _cols(j)] + _head_spread(wb, j, head0) * so4[j]
                 + _head_spread(wc, j, head0) * so16[j])
            o_ref[:, _cols(j)] = y
            mixed_ref[:, PW + 128 * j: PW + 128 * (j + 1)] = y.astype(bf16)
        for r in range(4):
            lse4_ref[r] = sl[pl.ds(r, TM // 4, stride=4), :]
        for r in range(16):
            lse16_ref[r] = sl[pl.ds(r, TM // 16, stride=16), :]
        xo_ref[...] = x_ref[...] + _dot_nn(mixed_ref[...], w_ref[...])

    wide, narrow = pltpu.VMEM((NG, TM, 128), f32), pltpu.VMEM((TM, 128), f32)
    return pl.pallas_call(
        body, grid=(S // TM,),
        in_specs=[_tile(D), _tile(PW), _tile(AW), _tile(128), _p4(), _p4(128), _p16(), _p16(128), _layer(D, D, l)],
        out_specs=[_tile(D), _tile(D), _tile(AW), _tile(128), _p4(128), _p16(128)],
        out_shape=[SDS((S, D), f32), SDS((S, D), bf16), SDS((S, AW), f32), SDS((S, 128), f32),
                   SDS((4, S // 4, 128), f32), SDS((16, S // 16, 128), f32)],
        scratch_shapes=[wide, wide, narrow, narrow, narrow],
        compiler_params=_CP, name="mix_out_fwd")(x, ypool, o1, l1, o4, l4, o16, l16, wout)


def _mix_out_bwd(dxo, o, wout, l=None):
    def body(dxo_ref, o_ref, w_ref, dxb_ref, dyp_ref, do1, do4, do16, dl1, dl4, dl16, sdo, sdl):
        dxb = dxo_ref[...].astype(bf16)
        dxb_ref[...] = dxb
        dm = _dot_nt(dxb, w_ref[...])
        dyp_ref[...] = dm[:, :PW]
        lane = lax.broadcasted_iota(jnp.int32, (TM, 128), 1)
        head0 = lane < 64
        dl = jnp.zeros((TM, 128), f32)
        for j in range(NG):
            d = dm[:, PW + 128 * j: PW + 128 * (j + 1)]
            prod = d * o_ref[:, _cols(j)]
            dl = _head_put(dl, j, jnp.sum(jnp.where(head0, prod, 0.0), axis=-1, keepdims=True),
                           jnp.sum(jnp.where(head0, 0.0, prod), axis=-1, keepdims=True), lane)
            do1[:, _cols(j)] = d.astype(bf16)
            sdo[j] = d
        dl1[...] = dl
        sdl[...] = dl
        for r in range(4):
            dl4[r] = sdl[pl.ds(r, TM // 4, stride=4), :]
            for j in range(NG):
                do4[r, :, _cols(j)] = sdo[j, pl.ds(r, TM // 4, stride=4), :].astype(bf16)
        for r in range(16):
            dl16[r] = sdl[pl.ds(r, TM // 16, stride=16), :]
            for j in range(NG):
                do16[r, :, _cols(j)] = sdo[j, pl.ds(r, TM // 16, stride=16), :].astype(bf16)

    return pl.pallas_call(
        body, grid=(S // TM,),
        in_specs=[_tile(D), _tile(AW), _layer(D, D, l)],
        out_specs=[_tile(D), _tile(PW), _tile(AW), _p4(), _p16(), _tile(128), _p4(128), _p16(128)],
        out_shape=[SDS((S, D), bf16), SDS((S, PW), f32),
                   SDS((S, AW), bf16), SDS((4, S // 4, AW), bf16), SDS((16, S // 16, AW), bf16),
                   SDS((S, 128), f32), SDS((4, S // 4, 128), f32), SDS((16, S // 16, 128), f32)],
        scratch_shapes=[pltpu.VMEM((NG, TM, 128), f32), pltpu.VMEM((TM, 128), f32)],
        compiler_params=_CP, name="mix_out_bwd")(dxo, o, wout)


def _loss_head(x, g, target):
    def body(x_ref, g_ref, t_ref, dx_ref, loss_ref, dg_ref):
        g = g_ref[...]
        r, xh, y = _rms(x_ref[...], g)
        err = y - t_ref[...]
        dy = err * (1.0 / D)

        @pl.when(pl.program_id(0) == 0)
        def _():
            loss_ref[...] = jnp.zeros_like(loss_ref)
            dg_ref[...] = jnp.zeros_like(dg_ref)

        loss_ref[...] += jnp.broadcast_to(0.5 * jnp.sum(jnp.mean(err * err, axis=-1, keepdims=True)), (1, D))
        dg_ref[...] += jnp.sum(dy * xh, axis=0, keepdims=True)
        dx_ref[...] = _rms_bwd(dy, r, xh, g)

    return pl.pallas_call(
        body, grid=(S // TM,),
        in_specs=[_tile(D), _const((1, D)), _tile(D)],
        out_specs=[_tile(D), _const((1, D)), _const((1, D))],
        out_shape=[SDS((S, D), f32), SDS((1, D), f32), SDS((1, D), f32)],
        compiler_params=_CP, name="loss_head")(x, g, target)


def _peer(k):
    x, y, c = lax.axis_index("x"), lax.axis_index("y"), lax.axis_index("c")
    px = 1 - x if k & 4 else x
    py = 1 - y if k & 2 else y
    pc = 1 - c if k & 1 else c
    return (px, py, pc), 4 * px + 2 * py + pc


def _all_gather(shards):
    n = len(shards)

    def body(*refs):
        ins, outs = refs[:n], refs[n:2 * n]
        send_sems, recv_sems, local_sems = refs[2 * n:]
        me, me_idx = _peer(0)
        sibling, sib_idx = _peer(1)
        far = [_peer(k) for k in (4, 2, 6)]
        far_sib = [_peer(k) for k in (5, 3, 7)]

        def rows(t, idx):
            r = ins[t].shape[1]
            return outs[t].at[:, pl.ds(idx * r, r), :]

        def copy(k, t, idx, to, src=None):
            return pltpu.make_async_remote_copy(
                src_ref=rows(t, idx) if src is None else src, dst_ref=rows(t, idx),
                send_sem=send_sems.at[k, t], recv_sem=recv_sems.at[k, t], device_id=to, device_id_type=_MESH)

        mine = [pltpu.make_async_copy(ins[t], rows(t, me_idx), local_sems.at[t]) for t in range(n)]
        for cp in mine:
            cp.start()
        first = [copy(0, t, me_idx, sibling, src=ins[t]) for t in range(n)]
        for j, (dev, _) in enumerate(far):
            first += [copy(1 + j, t, me_idx, dev, src=ins[t]) for t in range(n)]
        for cp in first:
            cp.start()
        passed = []
        for j, (_, idx) in enumerate(far):
            for t in range(n):
                copy(1 + j, t, idx, me).wait_recv()
                cp = copy(4 + j, t, idx, sibling)
                cp.start()
                passed.append(cp)
        for t in range(n):
            copy(0, t, sib_idx, me).wait_recv()
        for j, (_, idx) in enumerate(far_sib):
            for t in range(n):
                copy(4 + j, t, idx, me).wait_recv()
        for cp in first + passed:
            cp.wait_send()
        for cp in mine:
            cp.wait()

    return pl.pallas_call(
        body, in_specs=[_ANY] * n, out_specs=[_ANY] * n,
        out_shape=[SDS((a.shape[0], NDEV * a.shape[1], a.shape[2]), a.dtype) for a in shards],
        scratch_shapes=[pltpu.SemaphoreType.DMA((7, n)), pltpu.SemaphoreType.DMA((7, n)),
                        pltpu.SemaphoreType.DMA((n,))],
        name="all_gather_weights")(*shards)


def _hbm(a):
    return pltpu.with_memory_space_constraint(a, pltpu.HBM)


def _rows(ref, idx):
    r = ref.shape[0] // NDEV
    return ref.at[pl.ds(idx * r, r), :]


def _row_copy(ref, idx, send_sem, recv_sem, to):
    return pltpu.make_async_remote_copy(src_ref=_rows(ref, idx), dst_ref=_rows(ref, idx), send_sem=send_sem,
                                        recv_sem=recv_sem, device_id=to, device_id_type=_MESH)


def _place_own(shards, l):
    n = len(shards)

    def body(*refs):
        ins, outs, sems = refs[:n], refs[n:2 * n], refs[2 * n]
        _, me_idx = _peer(0)
        copies = [pltpu.make_async_copy(ins[t].at[l], _rows(outs[t], me_idx), sems.at[t]) for t in range(n)]
        for cp in copies:
            cp.start()
        for cp in copies:
            cp.wait()

    return pl.pallas_call(
        body, in_specs=[_ANY] * n, out_specs=[_ANY] * n,
        out_shape=[SDS((NDEV * s.shape[1], D), s.dtype) for s in shards],
        scratch_shapes=[pltpu.SemaphoreType.DMA((n,))], name="place_own")(*shards)


_TOKEN = SDS((8, 128), f32)
_FAR = (4, 2, 6)
_FAR_SIB = (5, 3, 7)


def _ag_start(lands, after, l):
    n = len(lands)

    def body(*refs):
        zones, send_sems, recv_sems, token = refs[:n], refs[n + 1], refs[n + 2], refs[-1]
        _, me_idx = _peer(0)
        for k, mask in enumerate((1,) + _FAR):
            for t in range(n):
                _row_copy(zones[t], me_idx, send_sems.at[k * n + t], recv_sems.at[k * n + t], _peer(mask)[0]).start()
        token[...] = jnp.zeros_like(token)

    outs = pl.pallas_call(
        body, name=f"ag_start_{l}", in_specs=[_HBM] * n + [_ANY],
        out_specs=(_SEM, _SEM, *[_HBM] * n, pl.BlockSpec(memory_space=pltpu.VMEM)),
        out_shape=(pltpu.SemaphoreType.DMA((4 * n,)), pltpu.SemaphoreType.DMA((4 * n,)),
                   *[pltpu.HBM(a.shape, a.dtype) for a in lands], _TOKEN),
        input_output_aliases={t: 2 + t for t in range(n)}, compiler_params=_CP_SPLIT)(
            *[_hbm(a) for a in lands], after)
    return outs[0], outs[1], list(outs[2:2 + n]), outs[-1]


def _ag_pass(lands, recv_sems, after, l):
    n = len(lands)
    after = list(after) if isinstance(after, (list, tuple)) else [after]

    def body(*refs):
        zones, recv_sems = refs[:n], refs[n]
        psend, precv, token = refs[n + 1 + len(after)], refs[n + 2 + len(after)], refs[-1]
        me, _ = _peer(0)
        sibling, _ = _peer(1)
        for j, mask in enumerate(_FAR):
            idx = _peer(mask)[1]
            for t in range(n):
                _row_copy(zones[t], idx, psend.at[j * n + t], recv_sems.at[(1 + j) * n + t], me).wait_recv()
                _row_copy(zones[t], idx, psend.at[j * n + t], precv.at[j * n + t], sibling).start()
        token[...] = jnp.zeros_like(token)

    outs = pl.pallas_call(
        body, name=f"ag_pass_{l}", in_specs=[_HBM] * n + [_SEM] + [_ANY] * len(after),
        out_specs=(_SEM, _SEM, *[_HBM] * n, pl.BlockSpec(memory_space=pltpu.VMEM)),
        out_shape=(pltpu.SemaphoreType.DMA((3 * n,)), pltpu.SemaphoreType.DMA((3 * n,)),
                   *[pltpu.HBM(a.shape, a.dtype) for a in lands], _TOKEN),
        input_output_aliases={t: 2 + t for t in range(n)}, compiler_params=_CP_SPLIT)(*lands, recv_sems, *after)
    return outs[0], outs[1], list(outs[2:2 + n]), outs[-1]


def _ag_wait(lands, send_sems, recv_sems, psend, precv, after, l):
    n = len(lands)

    def body(*refs):
        zones = refs[:n]
        send_sems, recv_sems, psend, precv = refs[n:n + 4]
        me, me_idx = _peer(0)
        sib_idx = _peer(1)[1]
        for k in range(4):
            for t in range(n):
                _row_copy(zones[t], me_idx, send_sems.at[k * n + t], recv_sems.at[k * n + t], me).wait_send()
        for t in range(n):
            _row_copy(zones[t], sib_idx, send_sems.at[t], recv_sems.at[t], me).wait_recv()
        for j in range(3):
            mine, theirs = _peer(_FAR[j])[1], _peer(_FAR_SIB[j])[1]
            for t in range(n):
                _row_copy(zones[t], mine, psend.at[j * n + t], precv.at[j * n + t], me).wait_send()
                _row_copy(zones[t], theirs, psend.at[j * n + t], precv.at[j * n + t], me).wait_recv()

    outs = pl.pallas_call(
        body, name=f"ag_wait_{l}", in_specs=[_HBM] * n + [_SEM] * 4 + [_ANY], out_specs=tuple([_HBM] * n),
        out_shape=tuple(pltpu.HBM(a.shape, a.dtype) for a in lands),
        input_output_aliases={t: t for t in range(n)}, compiler_params=_CP_SPLIT)(
            *lands, send_sems, recv_sems, psend, precv, after)
    return list(outs)


def _xchg_src(ref, slot_ref, idx):
    return _rows(ref, idx) if ref.shape[0] == NDEV * slot_ref.shape[1] else ref


def _rs_start(srcs, slots, after, tag):
    n = len(srcs)
    after = list(after) if isinstance(after, (list, tuple)) else [after]

    def body(*refs):
        src, slot = refs[:n], refs[n:2 * n]
        send_sems, recv_sems, token = refs[2 * n + len(after)], refs[2 * n + len(after) + 1], refs[-1]
        _, me_idx = _peer(0)
        for k in range(1, NDEV):
            dev, idx = _peer(k)
            for t in range(n):
                pltpu.make_async_remote_copy(
                    src_ref=_xchg_src(src[t], slot[t], idx), dst_ref=slot[t].at[me_idx],
                    send_sem=send_sems.at[(k - 1) * n + t], recv_sem=recv_sems.at[(k - 1) * n + t],
                    device_id=dev, device_id_type=_MESH).start()
        token[...] = jnp.zeros_like(token)

    outs = pl.pallas_call(
        body, name=f"rs_start_{tag}", in_specs=[_HBM] * (2 * n) + [_ANY] * len(after),
        out_specs=(_SEM, _SEM, *[_HBM] * (2 * n), pl.BlockSpec(memory_space=pltpu.VMEM)),
        out_shape=(pltpu.SemaphoreType.DMA(((NDEV - 1) * n,)), pltpu.SemaphoreType.DMA(((NDEV - 1) * n,)),
                   *[pltpu.HBM(a.shape, a.dtype) for a in list(srcs) + list(slots)], _TOKEN),
        input_output_aliases={t: 2 + t for t in range(2 * n)}, compiler_params=_CP_SPLIT)(
            *[_hbm(a) for a in list(srcs) + list(slots)], *after)
    return outs[0], outs[1], list(outs[2:2 + n]), list(outs[2 + n:2 + 2 * n]), outs[-1]


def _rs_wait(srcs, slots, send_sems, recv_sems, after, tag):
    n = len(srcs)
    after = list(after) if isinstance(after, (list, tuple)) else [after]

    def body(*refs):
        src, slot, send_sems, recv_sems = refs[:n], refs[n:2 * n], refs[2 * n], refs[2 * n + 1]
        me, _ = _peer(0)
        for k in range(1, NDEV):
            idx = _peer(k)[1]
            for t in range(n):
                cp = pltpu.make_async_remote_copy(
                    src_ref=_xchg_src(src[t], slot[t], idx), dst_ref=slot[t].at[idx],
                    send_sem=send_sems.at[(k - 1) * n + t], recv_sem=recv_sems.at[(k - 1) * n + t],
                    device_id=me, device_id_type=_MESH)
                cp.wait_send()
                cp.wait_recv()

    outs = pl.pallas_call(
        body, name=f"rs_wait_{tag}", in_specs=[_HBM] * (2 * n) + [_SEM, _SEM] + [_ANY] * len(after),
        out_specs=tuple([_HBM] * (2 * n)),
        out_shape=tuple(pltpu.HBM(a.shape, a.dtype) for a in list(srcs) + list(slots)),
        input_output_aliases={t: t for t in range(2 * n)}, compiler_params=_CP_SPLIT)(
            *srcs, *slots, send_sems, recv_sems, *after)
    return list(outs[:n]), list(outs[n:])


def _sum_slots(slots, rb):
    r = slots.shape[1]

    def body(s_ref, o_ref):
        acc = s_ref[0].astype(f32)
        for s in range(1, NDEV):
            acc = acc + s_ref[s].astype(f32)
        o_ref[...] = acc

    return pl.pallas_call(
        body, grid=(r // rb,),
        in_specs=[pl.BlockSpec((NDEV, rb, D), lambda i: (0, i, 0))],
        out_specs=pl.BlockSpec((rb, D), lambda i: (i, 0)),
        out_shape=SDS((r, D), f32), compiler_params=_CP, name="sum_slots")(slots)


def _adamw(w, g, m, v):
    shape = w.shape
    cols = shape[-1]
    rows = w.size // cols
    rb = rows
    for cand in (512, 256, 128, 64, 32, 16, 8):
        if rows % cand == 0 and rows > cand:
            rb = cand
            break

    def body(w_ref, g_ref, m_ref, v_ref, d_ref, mo_ref, vo_ref):
        d_ref[...], mo_ref[...], vo_ref[...] = _adamw_math(w_ref[...], g_ref[...], m_ref[...], v_ref[...])

    spec = pl.BlockSpec((rb, cols), lambda i: (i, 0))
    outs = pl.pallas_call(
        body, grid=(rows // rb,), in_specs=[spec] * 4, out_specs=[spec] * 3,
        out_shape=[SDS((rows, cols), f32)] * 3, compiler_params=_CP, name="adamw")(
            *(a.reshape(rows, cols) for a in (w, g, m, v)))
    return tuple(o.reshape(shape) for o in outs)


def _adamw_math(w, g, m, v):
    m = ADAM_B1 * m + (1.0 - ADAM_B1) * g
    v = ADAM_B2 * v + (1.0 - ADAM_B2) * (g * g)
    m_hat = m / (1.0 - ADAM_B1 ** ADAM_STEP)
    v_hat = v / (1.0 - ADAM_B2 ** ADAM_STEP)
    return -ADAM_LR * (m_hat / (jnp.sqrt(v_hat) + ADAM_EPS) + ADAM_WD * w), m, v


def _reduce_adamw(acc, me, full, slots, w, m, v, l):
    _, r, _ = w.shape
    rb = r // 2 if r > 128 else r

    def body(me_ref, full_ref, slots_ref, w_ref, m_ref, v_ref, *refs):
        go_ref, d_ref, mo_ref, vo_ref = refs[-4:]
        own = full_ref[...].astype(f32)
        g = None
        for s in range(NDEV):
            part = jnp.where(me_ref[0] == s, own, slots_ref[s].astype(f32))
            g = part if g is None else g + part
        go_ref[...] = g
        d_ref[...], mo_ref[...], vo_ref[...] = _adamw_math(w_ref[...], g, m_ref[...], v_ref[...])

    steps = r // rb
    lay = pl.BlockSpec((None, rb, D), lambda i, me_ref: (l, i, 0))
    n_acc = 0 if acc is None else 4
    grid_spec = pltpu.PrefetchScalarGridSpec(
        num_scalar_prefetch=1, grid=(steps,),
        in_specs=[pl.BlockSpec((rb, D), lambda i, me_ref: (me_ref[0] * steps + i, 0)),
                  pl.BlockSpec((NDEV, rb, D), lambda i, me_ref: (0, i, 0)), lay, lay, lay] + [_ANY] * n_acc,
        out_specs=[lay] * 4)
    outs = pl.pallas_call(
        body, grid_spec=grid_spec, out_shape=[SDS(w.shape, f32)] * 4,
        input_output_aliases={6 + j: j for j in range(n_acc)},
        compiler_params=_CP, name="reduce_adamw")(me, full, slots, w, m, v, *(() if acc is None else acc))
    return tuple(outs)


_BIG = ("ffn1_w_gate", "ffn1_w_up", "ffn1_w_down", "w_in", "w_out", "ffn2_w_gate", "ffn2_w_up", "ffn2_w_down")
_TRANSPOSED = ("ffn1_w_gate", "ffn1_w_up", "w_in", "ffn2_w_gate", "ffn2_w_up")

def _block_diag(pool_w):
    out = jnp.zeros((L, PW, PW), pool_w.dtype)
    for gi in range(4):
        out = out.at[:, 64 * gi:64 * (gi + 1), 64 * gi:64 * (gi + 1)].set(pool_w[:, gi])
    return out


def kernel(x, positions, ffn1_norm, ffn1_w_gate, ffn1_w_up, ffn1_w_down, mix_norm, w_in, pool_w, pool_scale, w_out, ffn2_norm, ffn2_w_gate, ffn2_w_up, ffn2_w_down, final_norm, loss_target, m_ffn1_norm, m_ffn1_w_gate, m_ffn1_w_up, m_ffn1_w_down, m_mix_norm, m_w_in, m_pool_w, m_pool_scale, m_w_out, m_ffn2_norm, m_ffn2_w_gate, m_ffn2_w_up, m_ffn2_w_down, m_final_norm, v_ffn1_norm, v_ffn1_w_gate, v_ffn1_w_up, v_ffn1_w_down, v_mix_norm, v_w_in, v_pool_w, v_pool_scale, v_w_out, v_ffn2_norm, v_ffn2_w_gate, v_ffn2_w_up, v_ffn2_w_down, v_final_norm):
    weights = dict(ffn1_norm=ffn1_norm, ffn1_w_gate=ffn1_w_gate, ffn1_w_up=ffn1_w_up, ffn1_w_down=ffn1_w_down,
                   mix_norm=mix_norm, w_in=w_in, pool_w=pool_w, pool_scale=pool_scale, w_out=w_out,
                   ffn2_norm=ffn2_norm, ffn2_w_gate=ffn2_w_gate, ffn2_w_up=ffn2_w_up, ffn2_w_down=ffn2_w_down,
                   final_norm=final_norm)
    moms = dict(ffn1_norm=m_ffn1_norm, ffn1_w_gate=m_ffn1_w_gate, ffn1_w_up=m_ffn1_w_up, ffn1_w_down=m_ffn1_w_down,
                mix_norm=m_mix_norm, w_in=m_w_in, pool_w=m_pool_w, pool_scale=m_pool_scale, w_out=m_w_out,
                ffn2_norm=m_ffn2_norm, ffn2_w_gate=m_ffn2_w_gate, ffn2_w_up=m_ffn2_w_up, ffn2_w_down=m_ffn2_w_down,
                final_norm=m_final_norm)
    vels = dict(ffn1_norm=v_ffn1_norm, ffn1_w_gate=v_ffn1_w_gate, ffn1_w_up=v_ffn1_w_up, ffn1_w_down=v_ffn1_w_down,
                mix_norm=v_mix_norm, w_in=v_w_in, pool_w=v_pool_w, pool_scale=v_pool_scale, w_out=v_w_out,
                ffn2_norm=v_ffn2_norm, ffn2_w_gate=v_ffn2_w_gate, ffn2_w_up=v_ffn2_w_up, ffn2_w_down=v_ffn2_w_down,
                final_norm=v_final_norm)
    names = list(weights)

    me_idx = 4 * lax.axis_index("x") + 2 * lax.axis_index("y") + lax.axis_index("c")

    tr = lambda w: jnp.swapaxes(w, 1, 2).astype(bf16)
    shards = [tr(weights[nm]) if nm in _TRANSPOSED else weights[nm].astype(bf16) for nm in _BIG]

    def landing_zones(l, which):
        return _place_own([shards[t] for t in which], l)

    g_ffn1 = [ffn1_norm[l].reshape(1, D) for l in range(L)]
    g_mix = [mix_norm[l].reshape(1, D) for l in range(L)]
    g_ffn2 = [ffn2_norm[l].reshape(1, D) for l in range(L)]
    wbd_all = _block_diag(pool_w).astype(bf16)
    wbd = [wbd_all[l] for l in range(L)]
    pscale = [pool_scale[l].reshape(1, PW) for l in range(L)]
    tabs = _rope_tables(positions)
    flat = lambda a: a.reshape(S, a.shape[-1])
    r4 = lambda a: a.reshape(4, S // 4, a.shape[-1])
    r16 = lambda a: a.reshape(16, S // 16, a.shape[-1])

    first, rest, whole = (0, 1, 2, 3), (4, 5, 6, 7), tuple(range(8))
    head = [a.reshape(a.shape[1], D) for a in _all_gather([shards[t][0:1] for t in first])]
    chain = {0: _ag_start(landing_zones(0, rest), head[0], "0"), 1: _ag_start(landing_zones(1, whole), head[0], "1")}
    gathered = [None] * L
    xs = x.reshape(S, D)
    saved = []
    for l in range(L):
        ga, gb = g_ffn1[l], g_ffn2[l]
        if l == 0:
            gt1, ut1, dn1, wint = head
            ga = ga + chain[0][3][0, 0] + chain[1][3][0, 0]
        else:
            gt1, ut1, dn1, wint, wout, gt2, ut2, dn2 = gathered[l]
        x0 = xs
        x1, gate1, up1 = _ffn_fwd(x0, ga, gt1, ut1, dn1)
        hmix, vp, q1, k1, v1, q4, k4, v4, q16, k16, v16 = _mix_in_fwd(x1, g_mix[l], wint, tabs)
        q4, k4, v4, q16, k16, v16 = map(flat, (q4, k4, v4, q16, k16, v16))
        ypool, diff = _pool_fwd(vp, wbd[l], pscale[l])
        o1, l1 = _attn_fwd(q1, k1, v1, S)
        o4, l4 = _attn_fwd(q4, k4, v4, S // 4)
        o16, l16 = _attn_fwd(q16, k16, v16, S // 16)
        if l == 0:
            send_sems, recv_sems, zones, _ = chain[0]
            psend, precv, zones, _ = _ag_pass(zones, recv_sems, [o1, o4, o16, ypool], "0")
            wout, gt2, ut2, dn2 = _ag_wait(zones, send_sems, recv_sems, psend, precv, o16, "0")
            gathered[0] = head + [wout, gt2, ut2, dn2]
        x2, mixed, o, lse1, lse4, lse16 = _mix_out_fwd(x1, ypool, o1, l1, r4(o4), r4(l4), r16(o16), r16(l16), wout)
        x3, gate2, up2 = _ffn_fwd(x2, gb, gt2, ut2, dn2)
        if l + 1 < L:
            send_sems, recv_sems, zones, _ = chain[l + 1]
            psend, precv, zones, token = _ag_pass(zones, recv_sems, x3, str(l + 1))
            if l + 2 < L:
                chain[l + 2] = _ag_start(landing_zones(l + 2, whole), token, str(l + 2))
                token = chain[l + 2][3]
            gathered[l + 1] = _ag_wait(zones, send_sems, recv_sems, psend, precv, token, str(l + 1))
        saved.append(dict(x0=x0, x1=x1, x2=x2, gate1=gate1, up1=up1, gate2=gate2, up2=up2, hmix=hmix, diff=diff,
                          qkv=((q1, k1, v1), (q4, k4, v4), (q16, k16, v16)), mixed=mixed, o=o,
                          lse=(lse1, flat(lse4), flat(lse16))))
        xs = x3

    dx, loss_part, d_final = _loss_head(xs, final_norm.reshape(1, D), loss_target.reshape(S, D))

    d_norm = {nm: [None] * L for nm in ("ffn1_norm", "mix_norm", "ffn2_norm")}
    d_poolw, d_pscale = [None] * L, [None] * L
    group_a = ("ffn2_w_gate", "ffn2_w_up", "ffn2_w_down", "w_out")
    group_b = ("ffn1_w_gate", "ffn1_w_up", "ffn1_w_down", "w_in")
    acc = {}

    as_rows = lambda a, nm: jnp.swapaxes(a, 1, 2) if nm in _TRANSPOSED else a
    w_rows = {nm: as_rows(weights[nm], nm) for nm in _BIG}
    m_rows = {nm: as_rows(moms[nm], nm) for nm in _BIG}
    v_rows = {nm: as_rows(vels[nm], nm) for nm in _BIG}
    me_arr = me_idx.reshape(1).astype(jnp.int32)

    def exchange(full, group, after, tag, extra=None):
        srcs = [full[nm] for nm in group]
        slots = [lax.empty((NDEV, g.shape[0] // NDEV, D), bf16) for g in srcs]
        if extra is not None:
            srcs, slots = srcs + [extra[0]], slots + [extra[1]]
        ssem, rsem, srcs, slots, token = _rs_start(srcs, slots, after, tag)
        return (srcs, slots, ssem, rsem, tag), token

    def update(l, group, flight, after):
        srcs, slots, ssem, rsem, tag = flight
        srcs, slots = _rs_wait(srcs, slots, ssem, rsem, after, tag)
        for nm, full_g, slots_g in zip(group, srcs, slots):
            acc[nm] = _reduce_adamw(acc.get(nm), me_arr, full_g, slots_g, w_rows[nm], m_rows[nm], v_rows[nm], l)
        return [acc[nm][0] for nm in group], slots

    flights = {}
    token_b = None
    for l in reversed(range(L)):
        sv = saved[l]
        gt1, ut1, dn1, wint, wout, gt2, ut2, dn2 = gathered[l]
        gb = g_ffn2[l] if token_b is None else g_ffn2[l] + token_b[0, 0]
        full = {}
        dx, dgate, dup, h, dy, d_norm["ffn2_norm"][l] = _ffn_bwd_d(sv["x2"], gb, sv["gate2"], sv["up2"], dx, gt2, ut2, dn2)
        full["ffn2_w_gate"], full["ffn2_w_up"], full["ffn2_w_down"] = _ffn_bwd_w(h, dy, sv["gate2"], sv["up2"], dgate, dup)

        dxb, dyp, do1, do4, do16, dl1, dl4, dl16 = _mix_out_bwd(dx, sv["o"], wout)
        full["w_out"] = _wgrad(sv["mixed"], dxb)
        flights[l, "a"], token_a = exchange(full, group_a, dxb, f"a{l}")
        dvp, dwbd, d_pscale[l] = _pool_bwd(dyp, sv["diff"], wbd[l], pscale[l] + token_a[0, 0])
        d_poolw[l] = jnp.stack([dwbd[64 * gi:64 * (gi + 1), 64 * gi:64 * (gi + 1)] for gi in range(4)])
        dos, dls = (do1, flat(do4), flat(do16)), (dl1, flat(dl4), flat(dl16))
        dqkv = []
        for b, lc in enumerate((S, S // 4, S // 16)):
            qb, kb, vb = sv["qkv"][b]
            dqkv.append(_attn_bwd(qb, kb, vb, dos[b], sv["lse"][b], dls[b], lc))
        d4 = tuple(r4(a) for a in dqkv[1])
        d16 = tuple(r16(a) for a in dqkv[2])
        dx, dproj, d_norm["mix_norm"][l] = _mix_in_bwd(dx, sv["x1"], g_mix[l], wint, tabs, dvp, dqkv[0], d4, d16)
        full["w_in"] = _wgrad(dproj, sv["hmix"])

        dx, dgate, dup, h, dy, d_norm["ffn1_norm"][l] = _ffn_bwd_d(sv["x0"], g_ffn1[l], sv["gate1"], sv["up1"], dx, gt1, ut1, dn1)
        full["ffn1_w_gate"], full["ffn1_w_up"], full["ffn1_w_down"] = _ffn_bwd_w(h, dy, sv["gate1"], sv["up1"], dgate, dup)

        after = dx
        if l + 1 < L and l + 1 >= 2:
            after, _ = update(l + 1, group_a, flights.pop((l + 1, "a")), after)
            after, _ = update(l + 1, group_b, flights.pop((l + 1, "b")), after)
        if l > 0:
            flights[l, "b"], token_b = exchange(full, group_b, after, f"b{l}")

    pad8 = lambda a: jnp.pad(a, ((0, 8 - a.shape[0]), (0, 0)))
    misc = jnp.concatenate([d_final, jnp.concatenate(d_pscale, axis=1), loss_part], axis=0)
    small = jnp.concatenate(
        [pad8(jnp.concatenate(d_norm[nm], axis=0)) for nm in ("ffn1_norm", "mix_norm", "ffn2_norm")]
        + [pad8(misc), jnp.stack(d_poolw).reshape(L * 16, D)], axis=0)
    small_slots = lax.dynamic_update_slice(lax.empty((NDEV, SMALL_ROWS, D), f32), small[None], (me_idx, 0, 0))
    flights[0, "b"], token_b = exchange(full, group_b, dx, "b0", extra=(small, small_slots))

    after = token_b
    for key in [(1, "a"), (1, "b"), (0, "a")]:
        after, _ = update(key[0], group_a if key[1] == "a" else group_b, flights.pop(key), after)
    after, slots_b0 = update(0, group_b, flights.pop((0, "b")), after)

    sm = _sum_slots(slots_b0[-1], SMALL_ROWS)
    grads = {}
    grads["ffn1_norm"], grads["mix_norm"], grads["ffn2_norm"] = sm[0:L], sm[8:8 + L], sm[16:16 + L]
    grads["final_norm"] = sm[24]
    grads["pool_scale"] = sm[25].reshape(L, PW)
    grads["pool_w"] = sm[32:32 + L * 16].reshape(L, 4, 64, 64)
    loss = sm[26, 0]
    upd = {nm: _adamw(weights[nm], grads[nm], moms[nm], vels[nm]) for nm in names if nm not in _BIG}
    for nm in _BIG:
        grads[nm], upd[nm] = as_rows(acc[nm][0], nm), tuple(as_rows(a, nm) for a in acc[nm][1:])
    return (loss, dx.reshape(1, S, D), *[grads[nm] for nm in names], *[upd[nm][0] for nm in names],
            *[upd[nm][1] for nm in names], *[upd[nm][2] for nm in names])
```

```python
import jax
import jax.numpy as jnp
from jax import lax
from jax.experimental import pallas as pl
from jax.experimental.pallas import tpu as pltpu

f32 = jnp.float32
bf16 = jnp.bfloat16
SDS = jax.ShapeDtypeStruct

D = 1024
S = 2048
F = 2816
L = 4
PW = 256
AW = 768
PROJ = PW + 3 * AW
NDEV = 8
TM = 256
QB = 128
HALF = 64
NG = AW // 128
NORM_EPS = 1e-6
MASK_VALUE = -1e30
ROPE_THETA = 500000.0
ADAM_LR, ADAM_B1, ADAM_B2, ADAM_EPS, ADAM_WD, ADAM_STEP = 0.001, 0.9, 0.999, 1e-08, 0.01, 10
POOL_WINDOWS = (2, 4, 8, 16)
PAD = 8
SMALL_ROWS = 96
VMEM_LIMIT = 56 * 1024 * 1024

_CP = pltpu.CompilerParams(vmem_limit_bytes=VMEM_LIMIT)
_ANY = pl.BlockSpec(memory_space=pl.ANY)
_HBM = pl.BlockSpec(memory_space=pltpu.HBM)
_SEM = pl.BlockSpec(memory_space=pltpu.SEMAPHORE)
_MESH = pl.DeviceIdType.MESH
_CP_SPLIT = pltpu.CompilerParams(has_side_effects=pltpu.SideEffectType.DATAFLOW_SIDE_EFFECTING)


def _dot_nn(a, b):
    return lax.dot_general(a, b, (((1,), (0,)), ((), ())), preferred_element_type=f32)


def _dot_nt(a, b):
    return lax.dot_general(a, b, (((1,), (1,)), ((), ())), preferred_element_type=f32)


def _dot_tn(a, b):
    return lax.dot_general(a, b, (((0,), (0,)), ((), ())), preferred_element_type=f32)


def _rms(x, g):
    r = lax.rsqrt(jnp.mean(x * x, axis=-1, keepdims=True) + NORM_EPS)
    xh = x * r
    return r, xh, xh * g


def _rms_bwd(dh, r, xh, g):
    dxh = dh * g
    return r * (dxh - xh * jnp.mean(dxh * xh, axis=-1, keepdims=True))


def _tile(cols):
    return pl.BlockSpec((TM, cols), lambda i: (i, 0))


def _const(shape):
    return pl.BlockSpec(shape, lambda i: (0,) * len(shape))


def _layer(rows, cols, l=None):
    return pl.BlockSpec((rows, cols), lambda i: (0, 0), pipeline_mode=pl.Buffered(1))


def _p4(cols=AW):
    return pl.BlockSpec((4, TM // 4, cols), lambda i: (0, i, 0))


def _p16(cols=AW):
    return pl.BlockSpec((16, TM // 16, cols), lambda i: (0, i, 0))


def _cols(j):
    return slice(128 * j, 128 * (j + 1))


def _ffn_fwd(x, g, gt, ut, dn, l=None):
    def body(x_ref, g_ref, gt_ref, ut_ref, dn_ref, xo_ref, gate_ref, up_ref):
        x = x_ref[...]
        _, _, hn = _rms(x, g_ref[...])
        h = hn.astype(bf16)
        gate = _dot_nt(h, gt_ref[...])
        up = _dot_nt(h, ut_ref[...])
        gate_ref[...] = gate.astype(bf16)
        up_ref[...] = up.astype(bf16)
        a = (gate * jax.nn.sigmoid(gate) * up).astype(bf16)
        xo_ref[...] = x + 0.5 * _dot_nn(a, dn_ref[...])

    return pl.pallas_call(
        body, grid=(S // TM,),
        in_specs=[_tile(D), _layer(1, D, l), _layer(F, D, l), _layer(F, D, l), _layer(F, D, l)],
        out_specs=[_tile(D), _tile(F), _tile(F)],
        out_shape=[SDS((S, D), f32), SDS((S, F), bf16), SDS((S, F), bf16)],
        compiler_params=_CP, name="ffn_fwd")(x, g, gt, ut, dn)


def _ffn_bwd_d(x, g, gate, up, dxo, gt, ut, dn, l=None):
    def body(x_ref, g_ref, gate_ref, up_ref, dxo_ref, gt_ref, ut_ref, dn_ref,
             dx_ref, dgate_ref, dup_ref, h_ref, dy_ref, dg_ref):
        x = x_ref[...]
        g = g_ref[...]
        r, xh, hn = _rms(x, g)
        h_ref[...] = hn.astype(bf16)
        dxo = dxo_ref[...]
        dy = (0.5 * dxo).astype(bf16)
        dy_ref[...] = dy
        da = _dot_nt(dy, dn_ref[...])
        gate = gate_ref[...].astype(f32)
        up = up_ref[...].astype(f32)
        sg = jax.nn.sigmoid(gate)
        dgate = (da * up * (sg * (1.0 + gate * (1.0 - sg)))).astype(bf16)
        dup = (da * (gate * sg)).astype(bf16)
        dgate_ref[...] = dgate
        dup_ref[...] = dup
        dh = _dot_nn(dgate, gt_ref[...]) + _dot_nn(dup, ut_ref[...])

        @pl.when(pl.program_id(0) == 0)
        def _():
            dg_ref[...] = jnp.zeros_like(dg_ref)

        dg_ref[...] += jnp.sum(dh * xh, axis=0, keepdims=True)
        dx_ref[...] = dxo + _rms_bwd(dh, r, xh, g)

    return pl.pallas_call(
        body, grid=(S // TM,),
        in_specs=[_tile(D), _layer(1, D, l), _tile(F), _tile(F), _tile(D),
                  _layer(F, D, l), _layer(F, D, l), _layer(F, D, l)],
        out_specs=[_tile(D), _tile(F), _tile(F), _tile(D), _tile(D), _const((1, D))],
        out_shape=[SDS((S, D), f32), SDS((S, F), bf16), SDS((S, F), bf16), SDS((S, D), bf16),
                   SDS((S, D), bf16), SDS((1, D), f32)],
        compiler_params=_CP, name="ffn_bwd_d")(x, g, gate, up, dxo, gt, ut, dn)


def _ffn_bwd_w(h, dy, gate, up, dgate, dup):
    fc = 256

    def body(h_ref, dy_ref, gate_ref, up_ref, dgate_ref, dup_ref, dgt_ref, dut_ref, ddn_ref):
        gate = gate_ref[...].astype(f32)
        a = (gate * jax.nn.sigmoid(gate) * up_ref[...].astype(f32)).astype(bf16)
        ddn_ref[...] = _dot_tn(a, dy_ref[...]).astype(bf16)
        h = h_ref[...]
        dgt_ref[...] = _dot_tn(dgate_ref[...], h).astype(bf16)
        dut_ref[...] = _dot_tn(dup_ref[...], h).astype(bf16)

    col = pl.BlockSpec((S, fc), lambda j: (0, j))
    row = pl.BlockSpec((fc, D), lambda j: (j, 0))
    full = pl.BlockSpec((S, D), lambda j: (0, 0))
    return pl.pallas_call(
        body, grid=(F // fc,),
        in_specs=[full, full, col, col, col, col],
        out_specs=[row, row, row],
        out_shape=[SDS((F, D), bf16)] * 3,
        compiler_params=_CP, name="ffn_bwd_w")(h, dy, gate, up, dgate, dup)


def _wgrad(a, b):
    m, n = a.shape[1], b.shape[1]
    mc = 256

    def body(a_ref, b_ref, o_ref):
        o_ref[...] = _dot_tn(a_ref[...], b_ref[...]).astype(bf16)

    return pl.pallas_call(
        body, grid=(m // mc,),
        in_specs=[pl.BlockSpec((S, mc), lambda j: (0, j)), pl.BlockSpec((S, n), lambda j: (0, 0))],
        out_specs=pl.BlockSpec((mc, n), lambda j: (j, 0)),
        out_shape=SDS((m, n), bf16),
        compiler_params=_CP, name="wgrad")(a, b)


def _rope(t, c, sn, sp):
    return t * c + pltpu.roll(t, 120, 1) * sn + pltpu.roll(t, 8, 1) * sp


def _rope_bwd(d, c, sn, sp):
    return d * c + pltpu.roll(d * sn, 8, 1) + pltpu.roll(d * sp, 120, 1)


def _rope_tables(positions):
    inv_freq = ROPE_THETA ** (-jnp.arange(0, 16, 2, dtype=f32) / 16)
    ang = positions.reshape(S, 1).astype(f32) * inv_freq
    cos, sin = jnp.cos(ang), jnp.sin(ang)
    one = jnp.ones((S, 48), f32)
    zero8 = jnp.zeros((S, 8), f32)
    zero48 = jnp.zeros((S, 48), f32)
    c = jnp.concatenate([cos, cos, one], axis=1)
    sn = jnp.concatenate([-sin, zero8, zero48], axis=1)
    sp = jnp.concatenate([zero8, sin, zero48], axis=1)
    return tuple(jnp.concatenate([t, t], axis=1) for t in (c, sn, sp))


def _mix_in_fwd(x, g, wint, tabs, l=None):
    def body(x_ref, g_ref, w_ref, c_ref, sn_ref, sp_ref,
             h_ref, vp_ref, q1, k1, v1, q4, k4, v4, q16, k16, v16, scr):
        _, _, hn = _rms(x_ref[...], g_ref[...])
        h = hn.astype(bf16)
        h_ref[...] = h
        proj = _dot_nt(h, w_ref[...])
        vp_ref[...] = proj[:, :PW]
        c, sn, sp = c_ref[...], sn_ref[...], sp_ref[...]
        for kind, (o1, o4, o16) in enumerate(((q1, q4, q16), (k1, k4, k16), (v1, v4, v16))):
            for j in range(NG):
                t = proj[:, PW + kind * AW + 128 * j: PW + kind * AW + 128 * (j + 1)]
                if kind == 0:
                    t = _rope(t, c, sn, sp) * 0.125
                elif kind == 1:
                    t = _rope(t, c, sn, sp)
                scr[j] = t
                o1[:, _cols(j)] = t.astype(bf16)
            for r in range(4):
                for j in range(NG):
                    o4[r, :, _cols(j)] = scr[j, pl.ds(r, TM // 4, stride=4), :].astype(bf16)
            for r in range(16):
                for j in range(NG):
                    o16[r, :, _cols(j)] = scr[j, pl.ds(r, TM // 16, stride=16), :].astype(bf16)

    nat, d4, d16 = SDS((S, AW), bf16), SDS((4, S // 4, AW), bf16), SDS((16, S // 16, AW), bf16)
    return pl.pallas_call(
        body, grid=(S // TM,),
        in_specs=[_tile(D), _layer(1, D, l), _layer(PROJ, D, l), _tile(128), _tile(128), _tile(128)],
        out_specs=[_tile(D), _tile(PW)] + [_tile(AW)] * 3 + [_p4()] * 3 + [_p16()] * 3,
        out_shape=[SDS((S, D), bf16), SDS((S, PW), f32)] + [nat] * 3 + [d4] * 3 + [d16] * 3,
        scratch_shapes=[pltpu.VMEM((NG, TM, 128), f32)],
        compiler_params=_CP, name="mix_in_fwd")(x, g, wint, *tabs)


def _mix_in_bwd(dxo, x, g, wint, tabs, dvp, d1, d4, d16, l=None):
    def body(dxo_ref, x_ref, g_ref, w_ref, c_ref, sn_ref, sp_ref, dvp_ref,
             dq1, dk1, dv1, dq4, dk4, dv4, dq16, dk16, dv16,
             dx_ref, dproj_ref, dg_ref, s4, s16):
        c, sn, sp = c_ref[...], sn_ref[...], sp_ref[...]
        dproj_ref[:, :PW] = dvp_ref[...].astype(bf16)
        for kind, (a1, a4, a16) in enumerate(((dq1, dq4, dq16), (dk1, dk4, dk16), (dv1, dv4, dv16))):
            for r in range(4):
                for j in range(NG):
                    s4[j, pl.ds(r, TM // 4, stride=4), :] = a4[r, :, _cols(j)]
            for r in range(16):
                for j in range(NG):
                    s16[j, pl.ds(r, TM // 16, stride=16), :] = a16[r, :, _cols(j)]
            for j in range(NG):
                t = a1[:, _cols(j)] + s4[j] + s16[j]
                if kind == 0:
                    t = _rope_bwd(t * 0.125, c, sn, sp)
                elif kind == 1:
                    t = _rope_bwd(t, c, sn, sp)
                dproj_ref[:, PW + kind * AW + 128 * j: PW + kind * AW + 128 * (j + 1)] = t.astype(bf16)
        g = g_ref[...]
        r_, xh, _ = _rms(x_ref[...], g)
        dh = _dot_nn(dproj_ref[...], w_ref[...])

        @pl.when(pl.program_id(0) == 0)
        def _():
            dg_ref[...] = jnp.zeros_like(dg_ref)

        dg_ref[...] += jnp.sum(dh * xh, axis=0, keepdims=True)
        dx_ref[...] = dxo_ref[...] + _rms_bwd(dh, r_, xh, g)

    return pl.pallas_call(
        body, grid=(S // TM,),
        in_specs=[_tile(D), _tile(D), _layer(1, D, l), _layer(PROJ, D, l), _tile(128), _tile(128), _tile(128),
                  _tile(PW)] + [_tile(AW)] * 3 + [_p4()] * 3 + [_p16()] * 3,
        out_specs=[_tile(D), _tile(PROJ), _const((1, D))],
        out_shape=[SDS((S, D), f32), SDS((S, PROJ), bf16), SDS((1, D), f32)],
        scratch_shapes=[pltpu.VMEM((NG, TM, 128), f32), pltpu.VMEM((NG, TM, 128), f32)],
        compiler_params=_CP, name="mix_in_bwd")(dxo, x, g, wint, *tabs, dvp, *d1, *d4, *d16)


def _pool_sums(pad_ref, base, rows, adjoint):
    lane_group = lax.broadcasted_iota(jnp.int32, (rows, PW), 1) // 64
    sign = -1 if adjoint else 1

    def sh(o):
        return pad_ref[pl.ds(PAD + base + sign * o, rows), :]

    out = None
    acc = None
    lo, hi = 0, 0
    for gi, w in enumerate(POOL_WINDOWS):
        for o in list(range(-(w // 2), lo)) + list(range(hi, w - w // 2)):
            acc = sh(o) if acc is None else acc + sh(o)
        lo, hi = -(w // 2), w - w // 2
        out = acc if out is None else jnp.where(lane_group >= gi, acc, out)
    return out


def _pool_counts(base, rows):
    pos = base + lax.broadcasted_iota(jnp.int32, (rows, PW), 0)
    lane_group = lax.broadcasted_iota(jnp.int32, (rows, PW), 1) // 64
    cnt = None
    for gi, w in enumerate(POOL_WINDOWS):
        lo = jnp.maximum(pos - w // 2, 0)
        hi = jnp.minimum(pos + w - 1 - w // 2, S - 1)
        c = (hi - lo + 1).astype(f32)
        cnt = c if cnt is None else jnp.where(lane_group >= gi, c, cnt)
    return cnt


def _pool_fwd(vp, wbd, scale, l=None):
    ch = 256

    def body(vp_ref, w_ref, sc_ref, y_ref, diff_ref, pad):
        pad[pl.ds(0, PAD), :] = jnp.zeros((PAD, PW), f32)
        pad[pl.ds(PAD + S, PAD), :] = jnp.zeros((PAD, PW), f32)
        pad[pl.ds(PAD, S), :] = vp_ref[...]
        for b in range(S // ch):
            base = b * ch
            pooled = _pool_sums(pad, base, ch, False) / _pool_counts(base, ch)
            diff = (pooled - vp_ref[pl.ds(base, ch), :]).astype(bf16)
            diff_ref[pl.ds(base, ch), :] = diff
            y_ref[pl.ds(base, ch), :] = _dot_nn(diff, w_ref[...]) * sc_ref[...]

    whole = lambda shape: pl.BlockSpec(shape, lambda i: (0,) * len(shape))
    return pl.pallas_call(
        body, grid=(1,),
        in_specs=[whole((S, PW)), whole((PW, PW)), whole((1, PW))],
        out_specs=[whole((S, PW)), whole((S, PW))],
        out_shape=[SDS((S, PW), f32), SDS((S, PW), bf16)],
        scratch_shapes=[pltpu.VMEM((S + 2 * PAD, PW), f32)],
        compiler_params=_CP, name="pool_fwd")(vp, wbd, scale)


def _pool_bwd(dy, diff, wbd, scale, l=None):
    ch = 256

    def body(dy_ref, diff_ref, w_ref, sc_ref, dvp_ref, dw_ref, dsc_ref, pad):
        pad[pl.ds(0, PAD), :] = jnp.zeros((PAD, PW), f32)
        pad[pl.ds(PAD + S, PAD), :] = jnp.zeros((PAD, PW), f32)
        dw = jnp.zeros((PW, PW), f32)
        dsc = jnp.zeros((1, PW), f32)
        for b in range(S // ch):
            base = b * ch
            dy = dy_ref[pl.ds(base, ch), :]
            diff = diff_ref[pl.ds(base, ch), :]
            dsc = dsc + jnp.sum(dy * _dot_nn(diff, w_ref[...]), axis=0, keepdims=True)
            dz = (dy * sc_ref[...]).astype(bf16)
            dw = dw + _dot_tn(diff, dz)
            ddiff = _dot_nt(dz, w_ref[...])
            dvp_ref[pl.ds(base, ch), :] = -ddiff
            pad[pl.ds(PAD + base, ch), :] = ddiff / _pool_counts(base, ch)
        dw_ref[...] = dw
        dsc_ref[...] = dsc
        for b in range(S // ch):
            base = b * ch
            dvp_ref[pl.ds(base, ch), :] += _pool_sums(pad, base, ch, True)

    whole = lambda shape: pl.BlockSpec(shape, lambda i: (0,) * len(shape))
    return pl.pallas_call(
        body, grid=(1,),
        in_specs=[whole((S, PW)), whole((S, PW)), whole((PW, PW)), whole((1, PW))],
        out_specs=[whole((S, PW)), whole((PW, PW)), whole((1, PW))],
        out_shape=[SDS((S, PW), f32), SDS((PW, PW), f32), SDS((1, PW), f32)],
        scratch_shapes=[pltpu.VMEM((S + 2 * PAD, PW), f32)],
        compiler_params=_CP, name="pool_bwd")(dy, diff, wbd, scale)


def _attn_blocks(lc):
    bpc = lc // QB
    kw = min(2 * QB, lc)
    blocks = []
    for b in range(S // QB):
        t0 = (b % bpc) * QB
        ks_in = min(max(t0 - HALF, 0), lc - kw)
        blocks.append((b * QB, (b // bpc) * lc + ks_in, t0 - ks_in))
    return kw, blocks


def _attn_bias(bias_ref, kw, shifts):
    r = lax.broadcasted_iota(jnp.int32, (2 * QB, kw), 0) % QB
    c = lax.broadcasted_iota(jnp.int32, (2 * QB, kw), 1)
    for i, shift in enumerate(shifts):
        bias_ref[i] = jnp.where(jnp.abs(r + shift - c) <= HALF, 0.0, MASK_VALUE).astype(f32)


def _head_put(stats, pair, v0, v1, lane):
    return jnp.where(lane == 2 * pair, v0, jnp.where(lane == 2 * pair + 1, v1, stats))


def _head_cols(stats, pair, lane):
    c0 = jnp.sum(jnp.where(lane == 2 * pair, stats, 0.0), axis=-1, keepdims=True)
    c1 = jnp.sum(jnp.where(lane == 2 * pair + 1, stats, 0.0), axis=-1, keepdims=True)
    return jnp.concatenate([c0, c1], axis=0)


def _head_spread(stats, pair, head0):
    return jnp.where(head0, stats[:, 2 * pair:2 * pair + 1], stats[:, 2 * pair + 1:2 * pair + 2])


def _stack_heads(blk, head0):
    zero = jnp.zeros_like(blk)
    return jnp.concatenate([jnp.where(head0, blk, zero), jnp.where(head0, zero, blk)], axis=0)


def _attn_fwd(q, k, v, lc, after=None):
    kw, blocks = _attn_blocks(lc)
    shifts = sorted({b[2] for b in blocks})

    def body(q_ref, k_ref, v_ref, *refs):
        o_ref, lse_ref, bias_ref = refs[-3:]
        lane = lax.broadcasted_iota(jnp.int32, (QB, 128), 1)
        head0 = lane < 64
        pair = pl.program_id(0)
        _attn_bias(bias_ref, kw, shifts)

        @pl.when(pair == 0)
        def _():
            lse_ref[...] = jnp.zeros_like(lse_ref)

        for row0, kstart, shift in blocks:
            q2 = _stack_heads(q_ref[pl.ds(row0, QB), :], head0)
            kb = k_ref[pl.ds(kstart, kw), :]
            vb = v_ref[pl.ds(kstart, kw), :]
            s = _dot_nt(q2, kb) + bias_ref[shifts.index(shift)]
            m = jnp.max(s, axis=-1, keepdims=True)
            p = jnp.exp(s - m)
            den = jnp.sum(p, axis=-1, keepdims=True)
            o2 = _dot_nn(p.astype(bf16), vb) / den
            lse2 = m + jnp.log(den)
            o_ref[pl.ds(row0, QB), :] = jnp.where(head0, o2[:QB], o2[QB:])
            lse_ref[pl.ds(row0, QB), :] = _head_put(lse_ref[pl.ds(row0, QB), :], pair, lse2[:QB], lse2[QB:], lane)

    col = pl.BlockSpec((S, 128), lambda p: (0, p))
    extra = () if after is None else (after,)
    return pl.pallas_call(
        body, grid=(NG,), in_specs=[col, col, col] + [_ANY] * len(extra),
        out_specs=[col, pl.BlockSpec((S, 128), lambda p: (0, 0))],
        out_shape=[SDS((S, AW), f32), SDS((S, 128), f32)],
        scratch_shapes=[pltpu.VMEM((len(shifts), 2 * QB, kw), f32)],
        compiler_params=_CP, name=f"attn_fwd_{lc}")(q, k, v, *extra)


def _attn_bwd(q, k, v, do, lse, delta, lc):
    kw, blocks = _attn_blocks(lc)
    shifts = sorted({b[2] for b in blocks})

    def body(q_ref, k_ref, v_ref, do_ref, lse_ref, dl_ref, dq_ref, dk_ref, dv_ref, bias_ref):
        lane = lax.broadcasted_iota(jnp.int32, (QB, 128), 1)
        head0 = lane < 64
        pair = pl.program_id(0)
        _attn_bias(bias_ref, kw, shifts)
        dk_ref[...] = jnp.zeros_like(dk_ref)
        dv_ref[...] = jnp.zeros_like(dv_ref)
        for row0, kstart, shift in blocks:
            q2 = _stack_heads(q_ref[pl.ds(row0, QB), :], head0)
            do2 = _stack_heads(do_ref[pl.ds(row0, QB), :], head0)
            lse2 = _head_cols(lse_ref[pl.ds(row0, QB), :], pair, lane)
            dl2 = _head_cols(dl_ref[pl.ds(row0, QB), :], pair, lane)
            kb = k_ref[pl.ds(kstart, kw), :]
            vb = v_ref[pl.ds(kstart, kw), :]
            p = jnp.exp(_dot_nt(q2, kb) + bias_ref[shifts.index(shift)] - lse2)
            ds = (p * (_dot_nt(do2, vb) - dl2)).astype(bf16)
            dq2 = _dot_nn(ds, kb)
            dq_ref[pl.ds(row0, QB), :] = jnp.where(head0, dq2[:QB], dq2[QB:])
            dk_ref[pl.ds(kstart, kw), :] += _dot_tn(ds, q2)
            dv_ref[pl.ds(kstart, kw), :] += _dot_tn(p.astype(bf16), do2)

    col = pl.BlockSpec((S, 128), lambda p: (0, p))
    stats = pl.BlockSpec((S, 128), lambda p: (0, 0))
    return pl.pallas_call(
        body, grid=(NG,), in_specs=[col] * 4 + [stats] * 2, out_specs=[col] * 3,
        out_shape=[SDS((S, AW), f32)] * 3,
        scratch_shapes=[pltpu.VMEM((len(shifts), 2 * QB, kw), f32)],
        compiler_params=_CP, name=f"attn_bwd_{lc}")(q, k, v, do, lse, delta)


def _mix_out_fwd(x, ypool, o1, l1, o4, l4, o16, l16, wout, l=None):
    def body(x_ref, yp_ref, o1_ref, l1_ref, o4_ref, l4_ref, o16_ref, l16_ref, w_ref,
             xo_ref, mixed_ref, o_ref, lse1_ref, lse4_ref, lse16_ref, so4, so16, sl4, sl16, sl):
        head0 = lax.broadcasted_iota(jnp.int32, (TM, 128), 1) < 64
        for r in range(4):
            sl4[pl.ds(r, TM // 4, stride=4), :] = l4_ref[r]
            for j in range(NG):
                so4[j, pl.ds(r, TM // 4, stride=4), :] = o4_ref[r, :, _cols(j)]
        for r in range(16):
            sl16[pl.ds(r, TM // 16, stride=16), :] = l16_ref[r]
            for j in range(NG):
                so16[j, pl.ds(r, TM // 16, stride=16), :] = o16_ref[r, :, _cols(j)]
        a, b, c = l1_ref[...], sl4[...], sl16[...]
        m = jnp.maximum(jnp.maximum(a, b), c)
        wa, wb, wc = jnp.exp(a - m), jnp.exp(b - m), jnp.exp(c - m)
        den = wa + wb + wc
        wa, wb, wc = wa / den, wb / den, wc / den
        lse = m + jnp.log(den)
        lse1_ref[...] = lse
        sl[...] = lse
        mixed_ref[:, :PW] = yp_ref[...].astype(bf16)
        for j in range(NG):
            y = (_head_spread(wa, j, head0) * o1_ref[:, _cols(j)] + _head_spread(wb, j, head0) * so4[j]
                 + _head_spread(wc, j, head0) * so16[j])
            o_ref[:, _cols(j)] = y
            mixed_ref[:, PW + 128 * j: PW + 128 * (j + 1)] = y.astype(bf16)
        for r in range(4):
            lse4_ref[r] = sl[pl.ds(r, TM // 4, stride=4), :]
        for r in range(16):
            lse16_ref[r] = sl[pl.ds(r, TM // 16, stride=16), :]
        xo_ref[...] = x_ref[...] + _dot_nn(mixed_ref[...], w_ref[...])

    wide, narrow = pltpu.VMEM((NG, TM, 128), f32), pltpu.VMEM((TM, 128), f32)
    return pl.pallas_call(
        body, grid=(S // TM,),
        in_specs=[_tile(D), _tile(PW), _tile(AW), _tile(128), _p4(), _p4(128), _p16(), _p16(128), _layer(D, D, l)],
        out_specs=[_tile(D), _tile(D), _tile(AW), _tile(128), _p4(128), _p16(128)],
        out_shape=[SDS((S, D), f32), SDS((S, D), bf16), SDS((S, AW), f32), SDS((S, 128), f32),
                   SDS((4, S // 4, 128), f32), SDS((16, S // 16, 128), f32)],
        scratch_shapes=[wide, wide, narrow, narrow, narrow],
        compiler_params=_CP, name="mix_out_fwd")(x, ypool, o1, l1, o4, l4, o16, l16, wout)


def _mix_out_bwd(dxo, o, wout, l=None):
    def body(dxo_ref, o_ref, w_ref, dxb_ref, dyp_ref, do1, do4, do16, dl1, dl4, dl16, sdo, sdl):
        dxb = dxo_ref[...].astype(bf16)
        dxb_ref[...] = dxb
        dm = _dot_nt(dxb, w_ref[...])
        dyp_ref[...] = dm[:, :PW]
        lane = lax.broadcasted_iota(jnp.int32, (TM, 128), 1)
        head0 = lane < 64
        dl = jnp.zeros((TM, 128), f32)
        for j in range(NG):
            d = dm[:, PW + 128 * j: PW + 128 * (j + 1)]
            prod = d * o_ref[:, _cols(j)]
            dl = _head_put(dl, j, jnp.sum(jnp.where(head0, prod, 0.0), axis=-1, keepdims=True),
                           jnp.sum(jnp.where(head0, 0.0, prod), axis=-1, keepdims=True), lane)
            do1[:, _cols(j)] = d.astype(bf16)
            sdo[j] = d
        dl1[...] = dl
        sdl[...] = dl
        for r in range(4):
            dl4[r] = sdl[pl.ds(r, TM // 4, stride=4), :]
            for j in range(NG):
                do4[r, :, _cols(j)] = sdo[j, pl.ds(r, TM // 4, stride=4), :].astype(bf16)
        for r in range(16):
            dl16[r] = sdl[pl.ds(r, TM // 16, stride=16), :]
            for j in range(NG):
                do16[r, :, _cols(j)] = sdo[j, pl.ds(r, TM // 16, stride=16), :].astype(bf16)

    return pl.pallas_call(
        body, grid=(S // TM,),
        in_specs=[_tile(D), _tile(AW), _layer(D, D, l)],
        out_specs=[_tile(D), _tile(PW), _tile(AW), _p4(), _p16(), _tile(128), _p4(128), _p16(128)],
        out_shape=[SDS((S, D), bf16), SDS((S, PW), f32),
                   SDS((S, AW), bf16), SDS((4, S // 4, AW), bf16), SDS((16, S // 16, AW), bf16),
                   SDS((S, 128), f32), SDS((4, S // 4, 128), f32), SDS((16, S // 16, 128), f32)],
        scratch_shapes=[pltpu.VMEM((NG, TM, 128), f32), pltpu.VMEM((TM, 128), f32)],
        compiler_params=_CP, name="mix_out_bwd")(dxo, o, wout)


def _loss_head(x, g, target):
    def body(x_ref, g_ref, t_ref, dx_ref, loss_ref, dg_ref):
        g = g_ref[...]
        r, xh, y = _rms(x_ref[...], g)
        err = y - t_ref[...]
        dy = err * (1.0 / D)

        @pl.when(pl.program_id(0) == 0)
        def _():
            loss_ref[...] = jnp.zeros_like(loss_ref)
            dg_ref[...] = jnp.zeros_like(dg_ref)

        loss_ref[...] += jnp.broadcast_to(0.5 * jnp.sum(jnp.mean(err * err, axis=-1, keepdims=True)), (1, D))
        dg_ref[...] += jnp.sum(dy * xh, axis=0, keepdims=True)
        dx_ref[...] = _rms_bwd(dy, r, xh, g)

    return pl.pallas_call(
        body, grid=(S // TM,),
        in_specs=[_tile(D), _const((1, D)), _tile(D)],
        out_specs=[_tile(D), _const((1, D)), _const((1, D))],
        out_shape=[SDS((S, D), f32), SDS((1, D), f32), SDS((1, D), f32)],
        compiler_params=_CP, name="loss_head")(x, g, target)


def _peer(k):
    x, y, c = lax.axis_index("x"), lax.axis_index("y"), lax.axis_index("c")
    px = 1 - x if k & 4 else x
    py = 1 - y if k & 2 else y
    pc = 1 - c if k & 1 else c
    return (px, py, pc), 4 * px + 2 * py + pc


def _all_gather(shards):
    n = len(shards)

    def body(*refs):
        ins, outs = refs[:n], refs[n:2 * n]
        send_sems, recv_sems, local_sems = refs[2 * n:]
        me, me_idx = _peer(0)
        sibling, sib_idx = _peer(1)
        far = [_peer(k) for k in (4, 2, 6)]
        far_sib = [_peer(k) for k in (5, 3, 7)]

        def rows(t, idx):
            r = ins[t].shape[1]
            return outs[t].at[:, pl.ds(idx * r, r), :]

        def copy(k, t, idx, to, src=None):
            return pltpu.make_async_remote_copy(
                src_ref=rows(t, idx) if src is None else src, dst_ref=rows(t, idx),
                send_sem=send_sems.at[k, t], recv_sem=recv_sems.at[k, t], device_id=to, device_id_type=_MESH)

        mine = [pltpu.make_async_copy(ins[t], rows(t, me_idx), local_sems.at[t]) for t in range(n)]
        for cp in mine:
            cp.start()
        first = [copy(0, t, me_idx, sibling, src=ins[t]) for t in range(n)]
        for j, (dev, _) in enumerate(far):
            first += [copy(1 + j, t, me_idx, dev, src=ins[t]) for t in range(n)]
        for cp in first:
            cp.start()
        passed = []
        for j, (_, idx) in enumerate(far):
            for t in range(n):
                copy(1 + j, t, idx, me).wait_recv()
                cp = copy(4 + j, t, idx, sibling)
                cp.start()
                passed.append(cp)
        for t in range(n):
            copy(0, t, sib_idx, me).wait_recv()
        for j, (_, idx) in enumerate(far_sib):
            for t in range(n):
                copy(4 + j, t, idx, me).wait_recv()
        for cp in first + passed:
            cp.wait_send()
        for cp in mine:
            cp.wait()

    return pl.pallas_call(
        body, in_specs=[_ANY] * n, out_specs=[_ANY] * n,
        out_shape=[SDS((a.shape[0], NDEV * a.shape[1], a.shape[2]), a.dtype) for a in shards],
        scratch_shapes=[pltpu.SemaphoreType.DMA((7, n)), pltpu.SemaphoreType.DMA((7, n)),
                        pltpu.SemaphoreType.DMA((n,))],
        name="all_gather_weights")(*shards)


def _hbm(a):
    return pltpu.with_memory_space_constraint(a, pltpu.HBM)


def _rows(ref, idx):
    r = ref.shape[0] // NDEV
    return ref.at[pl.ds(idx * r, r), :]


def _row_copy(ref, idx, send_sem, recv_sem, to):
    return pltpu.make_async_remote_copy(src_ref=_rows(ref, idx), dst_ref=_rows(ref, idx), send_sem=send_sem,
                                        recv_sem=recv_sem, device_id=to, device_id_type=_MESH)


def _place_own(me, shards, l):
    n = len(shards)

    def body(me_ref, *refs):
        for t in range(n):
            refs[n + t][...] = refs[t][...]

    grid_spec = pltpu.PrefetchScalarGridSpec(
        num_scalar_prefetch=1, grid=(1,),
        in_specs=[pl.BlockSpec((None, s.shape[1], D), lambda i, me_ref: (l, 0, 0)) for s in shards],
        out_specs=[pl.BlockSpec((s.shape[1], D), lambda i, me_ref: (me_ref[0], 0)) for s in shards])
    return pl.pallas_call(
        body, grid_spec=grid_spec, out_shape=[SDS((NDEV * s.shape[1], D), s.dtype) for s in shards],
        compiler_params=_CP, name="place_own")(me, *shards)


_TOKEN = SDS((8, 128), f32)
_FAR = (4, 2, 6)
_FAR_SIB = (5, 3, 7)


def _ag_start(lands, after, l):
    n = len(lands)

    def body(*refs):
        zones, send_sems, recv_sems, token = refs[:n], refs[n + 1], refs[n + 2], refs[-1]
        _, me_idx = _peer(0)
        for k, mask in enumerate((1,) + _FAR):
            for t in range(n):
                _row_copy(zones[t], me_idx, send_sems.at[k * n + t], recv_sems.at[k * n + t], _peer(mask)[0]).start()
        token[...] = jnp.zeros_like(token)

    outs = pl.pallas_call(
        body, name=f"ag_start_{l}", in_specs=[_HBM] * n + [_ANY],
        out_specs=(_SEM, _SEM, *[_HBM] * n, pl.BlockSpec(memory_space=pltpu.VMEM)),
        out_shape=(pltpu.SemaphoreType.DMA((4 * n,)), pltpu.SemaphoreType.DMA((4 * n,)),
                   *[pltpu.HBM(a.shape, a.dtype) for a in lands], _TOKEN),
        input_output_aliases={t: 2 + t for t in range(n)}, compiler_params=_CP_SPLIT)(
            *[_hbm(a) for a in lands], after)
    return outs[0], outs[1], list(outs[2:2 + n]), outs[-1]


def _ag_pass(lands, recv_sems, after, l):
    n = len(lands)
    after = list(after) if isinstance(after, (list, tuple)) else [after]

    def body(*refs):
        zones, recv_sems = refs[:n], refs[n]
        psend, precv, token = refs[n + 1 + len(after)], refs[n + 2 + len(after)], refs[-1]
        me, _ = _peer(0)
        sibling, _ = _peer(1)
        for j, mask in enumerate(_FAR):
            idx = _peer(mask)[1]
            for t in range(n):
                _row_copy(zones[t], idx, psend.at[j * n + t], recv_sems.at[(1 + j) * n + t], me).wait_recv()
                _row_copy(zones[t], idx, psend.at[j * n + t], precv.at[j * n + t], sibling).start()
        token[...] = jnp.zeros_like(token)

    outs = pl.pallas_call(
        body, name=f"ag_pass_{l}", in_specs=[_HBM] * n + [_SEM] + [_ANY] * len(after),
        out_specs=(_SEM, _SEM, *[_HBM] * n, pl.BlockSpec(memory_space=pltpu.VMEM)),
        out_shape=(pltpu.SemaphoreType.DMA((3 * n,)), pltpu.SemaphoreType.DMA((3 * n,)),
                   *[pltpu.HBM(a.shape, a.dtype) for a in lands], _TOKEN),
        input_output_aliases={t: 2 + t for t in range(n)}, compiler_params=_CP_SPLIT)(*lands, recv_sems, *after)
    return outs[0], outs[1], list(outs[2:2 + n]), outs[-1]


def _ag_wait(lands, send_sems, recv_sems, psend, precv, after, l):
    n = len(lands)

    def body(*refs):
        zones = refs[:n]
        send_sems, recv_sems, psend, precv = refs[n:n + 4]
        me, me_idx = _peer(0)
        sib_idx = _peer(1)[1]
        for k in range(4):
            for t in range(n):
                _row_copy(zones[t], me_idx, send_sems.at[k * n + t], recv_sems.at[k * n + t], me).wait_send()
        for t in range(n):
            _row_copy(zones[t], sib_idx, send_sems.at[t], recv_sems.at[t], me).wait_recv()
        for j in range(3):
            mine, theirs = _peer(_FAR[j])[1], _peer(_FAR_SIB[j])[1]
            for t in range(n):
                _row_copy(zones[t], mine, psend.at[j * n + t], precv.at[j * n + t], me).wait_send()
                _row_copy(zones[t], theirs, psend.at[j * n + t], precv.at[j * n + t], me).wait_recv()

    outs = pl.pallas_call(
        body, name=f"ag_wait_{l}", in_specs=[_HBM] * n + [_SEM] * 4 + [_ANY], out_specs=tuple([_HBM] * n),
        out_shape=tuple(pltpu.HBM(a.shape, a.dtype) for a in lands),
        input_output_aliases={t: t for t in range(n)}, compiler_params=_CP_SPLIT)(
            *lands, send_sems, recv_sems, psend, precv, after)
    return list(outs)


def _xchg_src(ref, slot_ref, idx):
    return _rows(ref, idx) if ref.shape[0] == NDEV * slot_ref.shape[1] else ref


def _rs_start(srcs, slots, after, tag):
    n = len(srcs)
    after = list(after) if isinstance(after, (list, tuple)) else [after]

    def body(*refs):
        src, slot = refs[:n], refs[n:2 * n]
        send_sems, recv_sems, token = refs[2 * n + len(after)], refs[2 * n + len(after) + 1], refs[-1]
        _, me_idx = _peer(0)
        for k in range(1, NDEV):
            dev, idx = _peer(k)
            for t in range(n):
                pltpu.make_async_remote_copy(
                    src_ref=_xchg_src(src[t], slot[t], idx), dst_ref=slot[t].at[me_idx],
                    send_sem=send_sems.at[(k - 1) * n + t], recv_sem=recv_sems.at[(k - 1) * n + t],
                    device_id=dev, device_id_type=_MESH).start()
        token[...] = jnp.zeros_like(token)

    outs = pl.pallas_call(
        body, name=f"rs_start_{tag}", in_specs=[_HBM] * (2 * n) + [_ANY] * len(after),
        out_specs=(_SEM, _SEM, *[_HBM] * (2 * n), pl.BlockSpec(memory_space=pltpu.VMEM)),
        out_shape=(pltpu.SemaphoreType.DMA(((NDEV - 1) * n,)), pltpu.SemaphoreType.DMA(((NDEV - 1) * n,)),
                   *[pltpu.HBM(a.shape, a.dtype) for a in list(srcs) + list(slots)], _TOKEN),
        input_output_aliases={t: 2 + t for t in range(2 * n)}, compiler_params=_CP_SPLIT)(
            *[_hbm(a) for a in list(srcs) + list(slots)], *after)
    return outs[0], outs[1], list(outs[2:2 + n]), list(outs[2 + n:2 + 2 * n]), outs[-1]


def _rs_wait(srcs, slots, send_sems, recv_sems, after, tag):
    n = len(srcs)
    after = list(after) if isinstance(after, (list, tuple)) else [after]

    def body(*refs):
        src, slot, send_sems, recv_sems = refs[:n], refs[n:2 * n], refs[2 * n], refs[2 * n + 1]
        me, _ = _peer(0)
        for k in range(1, NDEV):
            idx = _peer(k)[1]
            for t in range(n):
                cp = pltpu.make_async_remote_copy(
                    src_ref=_xchg_src(src[t], slot[t], idx), dst_ref=slot[t].at[idx],
                    send_sem=send_sems.at[(k - 1) * n + t], recv_sem=recv_sems.at[(k - 1) * n + t],
                    device_id=me, device_id_type=_MESH)
                cp.wait_send()
                cp.wait_recv()

    outs = pl.pallas_call(
        body, name=f"rs_wait_{tag}", in_specs=[_HBM] * (2 * n) + [_SEM, _SEM] + [_ANY] * len(after),
        out_specs=tuple([_HBM] * (2 * n)),
        out_shape=tuple(pltpu.HBM(a.shape, a.dtype) for a in list(srcs) + list(slots)),
        input_output_aliases={t: t for t in range(2 * n)}, compiler_params=_CP_SPLIT)(
            *srcs, *slots, send_sems, recv_sems, *after)
    return list(outs[:n]), list(outs[n:])


def _sum_slots(slots, rb):
    r = slots.shape[1]

    def body(s_ref, o_ref):
        acc = s_ref[0].astype(f32)
        for s in range(1, NDEV):
            acc = acc + s_ref[s].astype(f32)
        o_ref[...] = acc

    return pl.pallas_call(
        body, grid=(r // rb,),
        in_specs=[pl.BlockSpec((NDEV, rb, D), lambda i: (0, i, 0))],
        out_specs=pl.BlockSpec((rb, D), lambda i: (i, 0)),
        out_shape=SDS((r, D), f32), compiler_params=_CP, name="sum_slots")(slots)


def _adamw(w, g, m, v):
    shape = w.shape
    cols = shape[-1]
    rows = w.size // cols
    rb = rows
    for cand in (512, 256, 128, 64, 32, 16, 8):
        if rows % cand == 0 and rows > cand:
            rb = cand
            break

    def body(w_ref, g_ref, m_ref, v_ref, d_ref, mo_ref, vo_ref):
        d_ref[...], mo_ref[...], vo_ref[...] = _adamw_math(w_ref[...], g_ref[...], m_ref[...], v_ref[...])

    spec = pl.BlockSpec((rb, cols), lambda i: (i, 0))
    outs = pl.pallas_call(
        body, grid=(rows // rb,), in_specs=[spec] * 4, out_specs=[spec] * 3,
        out_shape=[SDS((rows, cols), f32)] * 3, compiler_params=_CP, name="adamw")(
            *(a.reshape(rows, cols) for a in (w, g, m, v)))
    return tuple(o.reshape(shape) for o in outs)


def _adamw_math(w, g, m, v):
    m = ADAM_B1 * m + (1.0 - ADAM_B1) * g
    v = ADAM_B2 * v + (1.0 - ADAM_B2) * (g * g)
    m_hat = m / (1.0 - ADAM_B1 ** ADAM_STEP)
    v_hat = v / (1.0 - ADAM_B2 ** ADAM_STEP)
    return -ADAM_LR * (m_hat / (jnp.sqrt(v_hat) + ADAM_EPS) + ADAM_WD * w), m, v


def _reduce_adamw(acc, me, full, slots, w, m, v, l):
    _, r, _ = w.shape
    rb = r // 2 if r > 128 else r

    def body(me_ref, full_ref, slots_ref, w_ref, m_ref, v_ref, *refs):
        go_ref, d_ref, mo_ref, vo_ref = refs[-4:]
        own = full_ref[...].astype(f32)
        g = None
        for s in range(NDEV):
            part = jnp.where(me_ref[0] == s, own, slots_ref[s].astype(f32))
            g = part if g is None else g + part
        go_ref[...] = g
        d_ref[...], mo_ref[...], vo_ref[...] = _adamw_math(w_ref[...], g, m_ref[...], v_ref[...])

    steps = r // rb
    lay = pl.BlockSpec((None, rb, D), lambda i, me_ref: (l, i, 0))
    n_acc = 0 if acc is None else 4
    grid_spec = pltpu.PrefetchScalarGridSpec(
        num_scalar_prefetch=1, grid=(steps,),
        in_specs=[pl.BlockSpec((rb, D), lambda i, me_ref: (me_ref[0] * steps + i, 0)),
                  pl.BlockSpec((NDEV, rb, D), lambda i, me_ref: (0, i, 0)), lay, lay, lay] + [_ANY] * n_acc,
        out_specs=[lay] * 4)
    outs = pl.pallas_call(
        body, grid_spec=grid_spec, out_shape=[SDS(w.shape, f32)] * 4,
        input_output_aliases={6 + j: j for j in range(n_acc)},
        compiler_params=_CP, name="reduce_adamw")(me, full, slots, w, m, v, *(() if acc is None else acc))
    return tuple(outs)


_BIG = ("ffn1_w_gate", "ffn1_w_up", "ffn1_w_down", "w_in", "w_out", "ffn2_w_gate", "ffn2_w_up", "ffn2_w_down")
_TRANSPOSED = ("ffn1_w_gate", "ffn1_w_up", "w_in", "ffn2_w_gate", "ffn2_w_up")

def _block_diag(pool_w):
    out = jnp.zeros((L, PW, PW), pool_w.dtype)
    for gi in range(4):
        out = out.at[:, 64 * gi:64 * (gi + 1), 64 * gi:64 * (gi + 1)].set(pool_w[:, gi])
    return out


def kernel(x, positions, ffn1_norm, ffn1_w_gate, ffn1_w_up, ffn1_w_down, mix_norm, w_in, pool_w, pool_scale, w_out, ffn2_norm, ffn2_w_gate, ffn2_w_up, ffn2_w_down, final_norm, loss_target, m_ffn1_norm, m_ffn1_w_gate, m_ffn1_w_up, m_ffn1_w_down, m_mix_norm, m_w_in, m_pool_w, m_pool_scale, m_w_out, m_ffn2_norm, m_ffn2_w_gate, m_ffn2_w_up, m_ffn2_w_down, m_final_norm, v_ffn1_norm, v_ffn1_w_gate, v_ffn1_w_up, v_ffn1_w_down, v_mix_norm, v_w_in, v_pool_w, v_pool_scale, v_w_out, v_ffn2_norm, v_ffn2_w_gate, v_ffn2_w_up, v_ffn2_w_down, v_final_norm):
    weights = dict(ffn1_norm=ffn1_norm, ffn1_w_gate=ffn1_w_gate, ffn1_w_up=ffn1_w_up, ffn1_w_down=ffn1_w_down,
                   mix_norm=mix_norm, w_in=w_in, pool_w=pool_w, pool_scale=pool_scale, w_out=w_out,
                   ffn2_norm=ffn2_norm, ffn2_w_gate=ffn2_w_gate, ffn2_w_up=ffn2_w_up, ffn2_w_down=ffn2_w_down,
                   final_norm=final_norm)
    moms = dict(ffn1_norm=m_ffn1_norm, ffn1_w_gate=m_ffn1_w_gate, ffn1_w_up=m_ffn1_w_up, ffn1_w_down=m_ffn1_w_down,
                mix_norm=m_mix_norm, w_in=m_w_in, pool_w=m_pool_w, pool_scale=m_pool_scale, w_out=m_w_out,
                ffn2_norm=m_ffn2_norm, ffn2_w_gate=m_ffn2_w_gate, ffn2_w_up=m_ffn2_w_up, ffn2_w_down=m_ffn2_w_down,
                final_norm=m_final_norm)
    vels = dict(ffn1_norm=v_ffn1_norm, ffn1_w_gate=v_ffn1_w_gate, ffn1_w_up=v_ffn1_w_up, ffn1_w_down=v_ffn1_w_down,
                mix_norm=v_mix_norm, w_in=v_w_in, pool_w=v_pool_w, pool_scale=v_pool_scale, w_out=v_w_out,
                ffn2_norm=v_ffn2_norm, ffn2_w_gate=v_ffn2_w_gate, ffn2_w_up=v_ffn2_w_up, ffn2_w_down=v_ffn2_w_down,
                final_norm=v_final_norm)
    names = list(weights)

    me_idx = 4 * lax.axis_index("x") + 2 * lax.axis_index("y") + lax.axis_index("c")
    me_arr = me_idx.reshape(1).astype(jnp.int32)

    tr = lambda w: jnp.swapaxes(w, 1, 2).astype(bf16)
    shards = [tr(weights[nm]) if nm in _TRANSPOSED else weights[nm].astype(bf16) for nm in _BIG]

    def landing_zones(l, which):
        return _place_own(me_arr, [shards[t] for t in which], l)

    g_ffn1 = [ffn1_norm[l].reshape(1, D) for l in range(L)]
    g_mix = [mix_norm[l].reshape(1, D) for l in range(L)]
    g_ffn2 = [ffn2_norm[l].reshape(1, D) for l in range(L)]
    wbd_all = _block_diag(pool_w).astype(bf16)
    wbd = [wbd_all[l] for l in range(L)]
    pscale = [pool_scale[l].reshape(1, PW) for l in range(L)]
    tabs = _rope_tables(positions)
    flat = lambda a: a.reshape(S, a.shape[-1])
    r4 = lambda a: a.reshape(4, S // 4, a.shape[-1])
    r16 = lambda a: a.reshape(16, S // 16, a.shape[-1])

    first, rest, whole = (0, 1, 2, 3), (4, 5, 6, 7), tuple(range(8))
    head = [a.reshape(a.shape[1], D) for a in _all_gather([shards[t][0:1] for t in first])]
    chain = {0: _ag_start(landing_zones(0, rest), head[0], "0"), 1: _ag_start(landing_zones(1, whole), head[0], "1")}
    gathered = [None] * L
    xs = x.reshape(S, D)
    saved = []
    for l in range(L):
        ga, gb = g_ffn1[l], g_ffn2[l]
        if l == 0:
            gt1, ut1, dn1, wint = head
            ga = ga + chain[0][3][0, 0] + chain[1][3][0, 0]
        else:
            gt1, ut1, dn1, wint, wout, gt2, ut2, dn2 = gathered[l]
        x0 = xs
        x1, gate1, up1 = _ffn_fwd(x0, ga, gt1, ut1, dn1)
        hmix, vp, q1, k1, v1, q4, k4, v4, q16, k16, v16 = _mix_in_fwd(x1, g_mix[l], wint, tabs)
        q4, k4, v4, q16, k16, v16 = map(flat, (q4, k4, v4, q16, k16, v16))
        ypool, diff = _pool_fwd(vp, wbd[l], pscale[l])
        o1, l1 = _attn_fwd(q1, k1, v1, S)
        o4, l4 = _attn_fwd(q4, k4, v4, S // 4)
        o16, l16 = _attn_fwd(q16, k16, v16, S // 16)
        if l == 0:
            send_sems, recv_sems, zones, _ = chain[0]
            psend, precv, zones, _ = _ag_pass(zones, recv_sems, [o1, o4, o16, ypool], "0")
            wout, gt2, ut2, dn2 = _ag_wait(zones, send_sems, recv_sems, psend, precv, o16, "0")
            gathered[0] = head + [wout, gt2, ut2, dn2]
        x2, mixed, o, lse1, lse4, lse16 = _mix_out_fwd(x1, ypool, o1, l1, r4(o4), r4(l4), r16(o16), r16(l16), wout)
        x3, gate2, up2 = _ffn_fwd(x2, gb, gt2, ut2, dn2)
        if l + 1 < L:
            send_sems, recv_sems, zones, _ = chain[l + 1]
            psend, precv, zones, token = _ag_pass(zones, recv_sems, x3, str(l + 1))
            if l + 2 < L:
                chain[l + 2] = _ag_start(landing_zones(l + 2, whole), token, str(l + 2))
                token = chain[l + 2][3]
            gathered[l + 1] = _ag_wait(zones, send_sems, recv_sems, psend, precv, token, str(l + 1))
        saved.append(dict(x0=x0, x1=x1, x2=x2, gate1=gate1, up1=up1, gate2=gate2, up2=up2, hmix=hmix, diff=diff,
                          qkv=((q1, k1, v1), (q4, k4, v4), (q16, k16, v16)), mixed=mixed, o=o,
                          lse=(lse1, flat(lse4), flat(lse16))))
        xs = x3

    dx, loss_part, d_final = _loss_head(xs, final_norm.reshape(1, D), loss_target.reshape(S, D))

    d_norm = {nm: [None] * L for nm in ("ffn1_norm", "mix_norm", "ffn2_norm")}
    d_poolw, d_pscale = [None] * L, [None] * L
    group_a = ("ffn2_w_gate", "ffn2_w_up", "ffn2_w_down", "w_out")
    group_b = ("ffn1_w_gate", "ffn1_w_up", "ffn1_w_down", "w_in")
    acc = {}

    as_rows = lambda a, nm: jnp.swapaxes(a, 1, 2) if nm in _TRANSPOSED else a
    w_rows = {nm: as_rows(weights[nm], nm) for nm in _BIG}
    m_rows = {nm: as_rows(moms[nm], nm) for nm in _BIG}
    v_rows = {nm: as_rows(vels[nm], nm) for nm in _BIG}

    def exchange(full, group, after, tag, extra=None):
        srcs = [full[nm] for nm in group]
        slots = [lax.empty((NDEV, g.shape[0] // NDEV, D), bf16) for g in srcs]
        if extra is not None:
            srcs, slots = srcs + [extra[0]], slots + [extra[1]]
        ssem, rsem, srcs, slots, token = _rs_start(srcs, slots, after, tag)
        return (srcs, slots, ssem, rsem, tag), token

    def update(l, group, flight, after):
        srcs, slots, ssem, rsem, tag = flight
        srcs, slots = _rs_wait(srcs, slots, ssem, rsem, after, tag)
        for nm, full_g, slots_g in zip(group, srcs, slots):
            acc[nm] = _reduce_adamw(acc.get(nm), me_arr, full_g, slots_g, w_rows[nm], m_rows[nm], v_rows[nm], l)
        return [acc[nm][0] for nm in group], slots

    flights = {}
    token_b = None
    for l in reversed(range(L)):
        sv = saved[l]
        gt1, ut1, dn1, wint, wout, gt2, ut2, dn2 = gathered[l]
        gb = g_ffn2[l] if token_b is None else g_ffn2[l] + token_b[0, 0]
        full = {}
        dx, dgate, dup, h, dy, d_norm["ffn2_norm"][l] = _ffn_bwd_d(sv["x2"], gb, sv["gate2"], sv["up2"], dx, gt2, ut2, dn2)
        full["ffn2_w_gate"], full["ffn2_w_up"], full["ffn2_w_down"] = _ffn_bwd_w(h, dy, sv["gate2"], sv["up2"], dgate, dup)

        dxb, dyp, do1, do4, do16, dl1, dl4, dl16 = _mix_out_bwd(dx, sv["o"], wout)
        full["w_out"] = _wgrad(sv["mixed"], dxb)
        flights[l, "a"], token_a = exchange(full, group_a, dxb, f"a{l}")
        dvp, dwbd, d_pscale[l] = _pool_bwd(dyp, sv["diff"], wbd[l], pscale[l] + token_a[0, 0])
        d_poolw[l] = jnp.stack([dwbd[64 * gi:64 * (gi + 1), 64 * gi:64 * (gi + 1)] for gi in range(4)])
        dos, dls = (do1, flat(do4), flat(do16)), (dl1, flat(dl4), flat(dl16))
        dqkv = []
        for b, lc in enumerate((S, S // 4, S // 16)):
            qb, kb, vb = sv["qkv"][b]
            dqkv.append(_attn_bwd(qb, kb, vb, dos[b], sv["lse"][b], dls[b], lc))
        d4 = tuple(r4(a) for a in dqkv[1])
        d16 = tuple(r16(a) for a in dqkv[2])
        dx, dproj, d_norm["mix_norm"][l] = _mix_in_bwd(dx, sv["x1"], g_mix[l], wint, tabs, dvp, dqkv[0], d4, d16)
        full["w_in"] = _wgrad(dproj, sv["hmix"])

        dx, dgate, dup, h, dy, d_norm["ffn1_norm"][l] = _ffn_bwd_d(sv["x0"], g_ffn1[l], sv["gate1"], sv["up1"], dx, gt1, ut1, dn1)
        full["ffn1_w_gate"], full["ffn1_w_up"], full["ffn1_w_down"] = _ffn_bwd_w(h, dy, sv["gate1"], sv["up1"], dgate, dup)

        after = dx
        if l + 1 < L and l + 1 >= 2:
            after, _ = update(l + 1, group_a, flights.pop((l + 1, "a")), after)
            after, _ = update(l + 1, group_b, flights.pop((l + 1, "b")), after)
        if l > 0:
            flights[l, "b"], token_b = exchange(full, group_b, after, f"b{l}")

    pad8 = lambda a: jnp.pad(a, ((0, 8 - a.shape[0]), (0, 0)))
    misc = jnp.concatenate([d_final, jnp.concatenate(d_pscale, axis=1), loss_part], axis=0)
    small = jnp.concatenate(
        [pad8(jnp.concatenate(d_norm[nm], axis=0)) for nm in ("ffn1_norm", "mix_norm", "ffn2_norm")]
        + [pad8(misc), jnp.stack(d_poolw).reshape(L * 16, D)], axis=0)
    small_slots = lax.dynamic_update_slice(lax.empty((NDEV, SMALL_ROWS, D), f32), small[None], (me_idx, 0, 0))
    flights[0, "b"], token_b = exchange(full, group_b, dx, "b0", extra=(small, small_slots))

    after = token_b
    for key in [(1, "a"), (1, "b"), (0, "a")]:
        after, _ = update(key[0], group_a if key[1] == "a" else group_b, flights.pop(key), after)
    after, slots_b0 = update(0, group_b, flights.pop((0, "b")), after)

    sm = _sum_slots(slots_b0[-1], SMALL_ROWS)
    grads = {}
    grads["ffn1_norm"], grads["mix_norm"], grads["ffn2_norm"] = sm[0:L], sm[8:8 + L], sm[16:16 + L]
    grads["final_norm"] = sm[24]
    grads["pool_scale"] = sm[25].reshape(L, PW)
    grads["pool_w"] = sm[32:32 + L * 16].reshape(L, 4, 64, 64)
    loss = sm[26, 0]
    upd = {nm: _adamw(weights[nm], grads[nm], moms[nm], vels[nm]) for nm in names if nm not in _BIG}
    for nm in _BIG:
        grads[nm], upd[nm] = as_rows(acc[nm][0], nm), tuple(as_rows(a, nm) for a in acc[nm][1:])
    return (loss, dx.reshape(1, S, D), *[grads[nm] for nm in names], *[upd[nm][0] for nm in names],
            *[upd[nm][1] for nm in names], *[upd[nm][2] for nm in names])
```

```python
import jax
import jax.numpy as jnp
from jax import lax
from jax.experimental import pallas as pl
from jax.experimental.pallas import tpu as pltpu

f32 = jnp.float32
bf16 = jnp.bfloat16
SDS = jax.ShapeDtypeStruct

D = 1024
S = 2048
F = 2816
L = 4
PW = 256
AW = 768
PROJ = PW + 3 * AW
NDEV = 8
TM = 256
QB = 128
HALF = 64
NG = AW // 128
NORM_EPS = 1e-6
MASK_VALUE = -1e30
ROPE_THETA = 500000.0
ADAM_LR, ADAM_B1, ADAM_B2, ADAM_EPS, ADAM_WD, ADAM_STEP = 0.001, 0.9, 0.999, 1e-08, 0.01, 10
POOL_WINDOWS = (2, 4, 8, 16)
PAD = 8
SMALL_ROWS = 96
VMEM_LIMIT = 56 * 1024 * 1024

_CP = pltpu.CompilerParams(vmem_limit_bytes=VMEM_LIMIT)
_ANY = pl.BlockSpec(memory_space=pl.ANY)
_HBM = pl.BlockSpec(memory_space=pltpu.HBM)
_SEM = pl.BlockSpec(memory_space=pltpu.SEMAPHORE)
_MESH = pl.DeviceIdType.MESH
_CP_SPLIT = pltpu.CompilerParams(has_side_effects=pltpu.SideEffectType.DATAFLOW_SIDE_EFFECTING)


def _dot_nn(a, b):
    return lax.dot_general(a, b, (((1,), (0,)), ((), ())), preferred_element_type=f32)


def _dot_nt(a, b):
    return lax.dot_general(a, b, (((1,), (1,)), ((), ())), preferred_element_type=f32)


def _dot_tn(a, b):
    return lax.dot_general(a, b, (((0,), (0,)), ((), ())), preferred_element_type=f32)


def _rms(x, g):
    r = lax.rsqrt(jnp.mean(x * x, axis=-1, keepdims=True) + NORM_EPS)
    xh = x * r
    return r, xh, xh * g


def _rms_bwd(dh, r, xh, g):
    dxh = dh * g
    return r * (dxh - xh * jnp.mean(dxh * xh, axis=-1, keepdims=True))


def _tile(cols):
    return pl.BlockSpec((TM, cols), lambda i: (i, 0))


def _const(shape):
    return pl.BlockSpec(shape, lambda i: (0,) * len(shape))


def _layer(rows, cols, l=None):
    return pl.BlockSpec((rows, cols), lambda i: (0, 0), pipeline_mode=pl.Buffered(1))


def _p4(cols=AW):
    return pl.BlockSpec((4, TM // 4, cols), lambda i: (0, i, 0))


def _p16(cols=AW):
    return pl.BlockSpec((16, TM // 16, cols), lambda i: (0, i, 0))


def _cols(j):
    return slice(128 * j, 128 * (j + 1))


def _ffn_fwd(x, g, gt, ut, dn, l=None):
    def body(x_ref, g_ref, gt_ref, ut_ref, dn_ref, xo_ref, gate_ref, up_ref):
        x = x_ref[...]
        _, _, hn = _rms(x, g_ref[...])
        h = hn.astype(bf16)
        gate = _dot_nt(h, gt_ref[...])
        up = _dot_nt(h, ut_ref[...])
        gate_ref[...] = gate.astype(bf16)
        up_ref[...] = up.astype(bf16)
        a = (gate * jax.nn.sigmoid(gate) * up).astype(bf16)
        xo_ref[...] = x + 0.5 * _dot_nn(a, dn_ref[...])

    return pl.pallas_call(
        body, grid=(S // TM,),
        in_specs=[_tile(D), _layer(1, D, l), _layer(F, D, l), _layer(F, D, l), _layer(F, D, l)],
        out_specs=[_tile(D), _tile(F), _tile(F)],
        out_shape=[SDS((S, D), f32), SDS((S, F), bf16), SDS((S, F), bf16)],
        compiler_params=_CP, name="ffn_fwd")(x, g, gt, ut, dn)


def _ffn_bwd_d(x, g, gate, up, dxo, gt, ut, dn, l=None):
    def body(x_ref, g_ref, gate_ref, up_ref, dxo_ref, gt_ref, ut_ref, dn_ref,
             dx_ref, dgate_ref, dup_ref, h_ref, dy_ref, dg_ref):
        x = x_ref[...]
        g = g_ref[...]
        r, xh, hn = _rms(x, g)
        h_ref[...] = hn.astype(bf16)
        dxo = dxo_ref[...]
        dy = (0.5 * dxo).astype(bf16)
        dy_ref[...] = dy
        da = _dot_nt(dy, dn_ref[...])
        gate = gate_ref[...].astype(f32)
        up = up_ref[...].astype(f32)
        sg = jax.nn.sigmoid(gate)
        dgate = (da * up * (sg * (1.0 + gate * (1.0 - sg)))).astype(bf16)
        dup = (da * (gate * sg)).astype(bf16)
        dgate_ref[...] = dgate
        dup_ref[...] = dup
        dh = _dot_nn(dgate, gt_ref[...]) + _dot_nn(dup, ut_ref[...])

        @pl.when(pl.program_id(0) == 0)
        def _():
            dg_ref[...] = jnp.zeros_like(dg_ref)

        dg_ref[...] += jnp.sum(dh * xh, axis=0, keepdims=True)
        dx_ref[...] = dxo + _rms_bwd(dh, r, xh, g)

    return pl.pallas_call(
        body, grid=(S // TM,),
        in_specs=[_tile(D), _layer(1, D, l), _tile(F), _tile(F), _tile(D),
                  _layer(F, D, l), _layer(F, D, l), _layer(F, D, l)],
        out_specs=[_tile(D), _tile(F), _tile(F), _tile(D), _tile(D), _const((1, D))],
        out_shape=[SDS((S, D), f32), SDS((S, F), bf16), SDS((S, F), bf16), SDS((S, D), bf16),
                   SDS((S, D), bf16), SDS((1, D), f32)],
        compiler_params=_CP, name="ffn_bwd_d")(x, g, gate, up, dxo, gt, ut, dn)


def _ffn_bwd_w(h, dy, gate, up, dgate, dup):
    fc = 256

    def body(h_ref, dy_ref, gate_ref, up_ref, dgate_ref, dup_ref, dgt_ref, dut_ref, ddn_ref):
        gate = gate_ref[...].astype(f32)
        a = (gate * jax.nn.sigmoid(gate) * up_ref[...].astype(f32)).astype(bf16)
        ddn_ref[...] = _dot_tn(a, dy_ref[...]).astype(bf16)
        h = h_ref[...]
        dgt_ref[...] = _dot_tn(dgate_ref[...], h).astype(bf16)
        dut_ref[...] = _dot_tn(dup_ref[...], h).astype(bf16)

    col = pl.BlockSpec((S, fc), lambda j: (0, j))
    row = pl.BlockSpec((fc, D), lambda j: (j, 0))
    full = pl.BlockSpec((S, D), lambda j: (0, 0))
    return pl.pallas_call(
        body, grid=(F // fc,),
        in_specs=[full, full, col, col, col, col],
        out_specs=[row, row, row],
        out_shape=[SDS((F, D), bf16)] * 3,
        compiler_params=_CP, name="ffn_bwd_w")(h, dy, gate, up, dgate, dup)


def _wgrad(a, b):
    m, n = a.shape[1], b.shape[1]
    mc = 256

    def body(a_ref, b_ref, o_ref):
        o_ref[...] = _dot_tn(a_ref[...], b_ref[...]).astype(bf16)

    return pl.pallas_call(
        body, grid=(m // mc,),
        in_specs=[pl.BlockSpec((S, mc), lambda j: (0, j)), pl.BlockSpec((S, n), lambda j: (0, 0))],
        out_specs=pl.BlockSpec((mc, n), lambda j: (j, 0)),
        out_shape=SDS((m, n), bf16),
        compiler_params=_CP, name="wgrad")(a, b)


def _rope(t, c, sn, sp):
    return t * c + pltpu.roll(t, 120, 1) * sn + pltpu.roll(t, 8, 1) * sp


def _rope_bwd(d, c, sn, sp):
    return d * c + pltpu.roll(d * sn, 8, 1) + pltpu.roll(d * sp, 120, 1)


def _rope_tables(positions):
    inv_freq = ROPE_THETA ** (-jnp.arange(0, 16, 2, dtype=f32) / 16)
    ang = positions.reshape(S, 1).astype(f32) * inv_freq
    cos, sin = jnp.cos(ang), jnp.sin(ang)
    one = jnp.ones((S, 48), f32)
    zero8 = jnp.zeros((S, 8), f32)
    zero48 = jnp.zeros((S, 48), f32)
    c = jnp.concatenate([cos, cos, one], axis=1)
    sn = jnp.concatenate([-sin, zero8, zero48], axis=1)
    sp = jnp.concatenate([zero8, sin, zero48], axis=1)
    return tuple(jnp.concatenate([t, t], axis=1) for t in (c, sn, sp))


def _mix_in_fwd(x, g, wint, tabs, l=None):
    def body(x_ref, g_ref, w_ref, c_ref, sn_ref, sp_ref,
             h_ref, vp_ref, q1, k1, v1, q4, k4, v4, q16, k16, v16, scr):
        _, _, hn = _rms(x_ref[...], g_ref[...])
        h = hn.astype(bf16)
        h_ref[...] = h
        proj = _dot_nt(h, w_ref[...])
        vp_ref[...] = proj[:, :PW]
        c, sn, sp = c_ref[...], sn_ref[...], sp_ref[...]
        for kind, (o1, o4, o16) in enumerate(((q1, q4, q16), (k1, k4, k16), (v1, v4, v16))):
            for j in range(NG):
                t = proj[:, PW + kind * AW + 128 * j: PW + kind * AW + 128 * (j + 1)]
                if kind == 0:
                    t = _rope(t, c, sn, sp) * 0.125
                elif kind == 1:
                    t = _rope(t, c, sn, sp)
                scr[j] = t
                o1[:, _cols(j)] = t.astype(bf16)
            for r in range(4):
                for j in range(NG):
                    o4[r, :, _cols(j)] = scr[j, pl.ds(r, TM // 4, stride=4), :].astype(bf16)
            for r in range(16):
                for j in range(NG):
                    o16[r, :, _cols(j)] = scr[j, pl.ds(r, TM // 16, stride=16), :].astype(bf16)

    nat, d4, d16 = SDS((S, AW), bf16), SDS((4, S // 4, AW), bf16), SDS((16, S // 16, AW), bf16)
    return pl.pallas_call(
        body, grid=(S // TM,),
        in_specs=[_tile(D), _layer(1, D, l), _layer(PROJ, D, l), _tile(128), _tile(128), _tile(128)],
        out_specs=[_tile(D), _tile(PW)] + [_tile(AW)] * 3 + [_p4()] * 3 + [_p16()] * 3,
        out_shape=[SDS((S, D), bf16), SDS((S, PW), f32)] + [nat] * 3 + [d4] * 3 + [d16] * 3,
        scratch_shapes=[pltpu.VMEM((NG, TM, 128), f32)],
        compiler_params=_CP, name="mix_in_fwd")(x, g, wint, *tabs)


def _mix_in_bwd(dxo, x, g, wint, tabs, dvp, d1, d4, d16, l=None):
    def body(dxo_ref, x_ref, g_ref, w_ref, c_ref, sn_ref, sp_ref, dvp_ref,
             dq1, dk1, dv1, dq4, dk4, dv4, dq16, dk16, dv16,
             dx_ref, dproj_ref, dg_ref, s4, s16):
        c, sn, sp = c_ref[...], sn_ref[...], sp_ref[...]
        dproj_ref[:, :PW] = dvp_ref[...].astype(bf16)
        for kind, (a1, a4, a16) in enumerate(((dq1, dq4, dq16), (dk1, dk4, dk16), (dv1, dv4, dv16))):
            for r in range(4):
                for j in range(NG):
                    s4[j, pl.ds(r, TM // 4, stride=4), :] = a4[r, :, _cols(j)]
            for r in range(16):
                for j in range(NG):
                    s16[j, pl.ds(r, TM // 16, stride=16), :] = a16[r, :, _cols(j)]
            for j in range(NG):
                t = a1[:, _cols(j)] + s4[j] + s16[j]
                if kind == 0:
                    t = _rope_bwd(t * 0.125, c, sn, sp)
                elif kind == 1:
                    t = _rope_bwd(t, c, sn, sp)
                dproj_ref[:, PW + kind * AW + 128 * j: PW + kind * AW + 128 * (j + 1)] = t.astype(bf16)
        g = g_ref[...]
        r_, xh, _ = _rms(x_ref[...], g)
        dh = _dot_nn(dproj_ref[...], w_ref[...])

        @pl.when(pl.program_id(0) == 0)
        def _():
            dg_ref[...] = jnp.zeros_like(dg_ref)

        dg_ref[...] += jnp.sum(dh * xh, axis=0, keepdims=True)
        dx_ref[...] = dxo_ref[...] + _rms_bwd(dh, r_, xh, g)

    return pl.pallas_call(
        body, grid=(S // TM,),
        in_specs=[_tile(D), _tile(D), _layer(1, D, l), _layer(PROJ, D, l), _tile(128), _tile(128), _tile(128),
                  _tile(PW)] + [_tile(AW)] * 3 + [_p4()] * 3 + [_p16()] * 3,
        out_specs=[_tile(D), _tile(PROJ), _const((1, D))],
        out_shape=[SDS((S, D), f32), SDS((S, PROJ), bf16), SDS((1, D), f32)],
        scratch_shapes=[pltpu.VMEM((NG, TM, 128), f32), pltpu.VMEM((NG, TM, 128), f32)],
        compiler_params=_CP, name="mix_in_bwd")(dxo, x, g, wint, *tabs, dvp, *d1, *d4, *d16)


def _pool_sums(pad_ref, base, rows, adjoint):
    lane_group = lax.broadcasted_iota(jnp.int32, (rows, PW), 1) // 64
    sign = -1 if adjoint else 1

    def sh(o):
        return pad_ref[pl.ds(PAD + base + sign * o, rows), :]

    out = None
    acc = None
    lo, hi = 0, 0
    for gi, w in enumerate(POOL_WINDOWS):
        for o in list(range(-(w // 2), lo)) + list(range(hi, w - w // 2)):
            acc = sh(o) if acc is None else acc + sh(o)
        lo, hi = -(w // 2), w - w // 2
        out = acc if out is None else jnp.where(lane_group >= gi, acc, out)
    return out


def _pool_counts(base, rows):
    pos = base + lax.broadcasted_iota(jnp.int32, (rows, PW), 0)
    lane_group = lax.broadcasted_iota(jnp.int32, (rows, PW), 1) // 64
    cnt = None
    for gi, w in enumerate(POOL_WINDOWS):
        lo = jnp.maximum(pos - w // 2, 0)
        hi = jnp.minimum(pos + w - 1 - w // 2, S - 1)
        c = (hi - lo + 1).astype(f32)
        cnt = c if cnt is None else jnp.where(lane_group >= gi, c, cnt)
    return cnt


def _pool_fwd(vp, wbd, scale, l=None):
    ch = 256

    def body(vp_ref, w_ref, sc_ref, y_ref, diff_ref, pad):
        pad[pl.ds(0, PAD), :] = jnp.zeros((PAD, PW), f32)
        pad[pl.ds(PAD + S, PAD), :] = jnp.zeros((PAD, PW), f32)
        pad[pl.ds(PAD, S), :] = vp_ref[...]
        for b in range(S // ch):
            base = b * ch
            pooled = _pool_sums(pad, base, ch, False) / _pool_counts(base, ch)
            diff = (pooled - vp_ref[pl.ds(base, ch), :]).astype(bf16)
            diff_ref[pl.ds(base, ch), :] = diff
            y_ref[pl.ds(base, ch), :] = _dot_nn(diff, w_ref[...]) * sc_ref[...]

    whole = lambda shape: pl.BlockSpec(shape, lambda i: (0,) * len(shape))
    return pl.pallas_call(
        body, grid=(1,),
        in_specs=[whole((S, PW)), whole((PW, PW)), whole((1, PW))],
        out_specs=[whole((S, PW)), whole((S, PW))],
        out_shape=[SDS((S, PW), f32), SDS((S, PW), bf16)],
        scratch_shapes=[pltpu.VMEM((S + 2 * PAD, PW), f32)],
        compiler_params=_CP, name="pool_fwd")(vp, wbd, scale)


def _pool_bwd(dy, diff, wbd, scale, l=None):
    ch = 256

    def body(dy_ref, diff_ref, w_ref, sc_ref, dvp_ref, dw_ref, dsc_ref, pad):
        pad[pl.ds(0, PAD), :] = jnp.zeros((PAD, PW), f32)
        pad[pl.ds(PAD + S, PAD), :] = jnp.zeros((PAD, PW), f32)
        dw = jnp.zeros((PW, PW), f32)
        dsc = jnp.zeros((1, PW), f32)
        for b in range(S // ch):
            base = b * ch
            dy = dy_ref[pl.ds(base, ch), :]
            diff = diff_ref[pl.ds(base, ch), :]
            dsc = dsc + jnp.sum(dy * _dot_nn(diff, w_ref[...]), axis=0, keepdims=True)
            dz = (dy * sc_ref[...]).astype(bf16)
            dw = dw + _dot_tn(diff, dz)
            ddiff = _dot_nt(dz, w_ref[...])
            dvp_ref[pl.ds(base, ch), :] = -ddiff
            pad[pl.ds(PAD + base, ch), :] = ddiff / _pool_counts(base, ch)
        dw_ref[...] = dw
        dsc_ref[...] = dsc
        for b in range(S // ch):
            base = b * ch
            dvp_ref[pl.ds(base, ch), :] += _pool_sums(pad, base, ch, True)

    whole = lambda shape: pl.BlockSpec(shape, lambda i: (0,) * len(shape))
    return pl.pallas_call(
        body, grid=(1,),
        in_specs=[whole((S, PW)), whole((S, PW)), whole((PW, PW)), whole((1, PW))],
        out_specs=[whole((S, PW)), whole((PW, PW)), whole((1, PW))],
        out_shape=[SDS((S, PW), f32), SDS((PW, PW), f32), SDS((1, PW), f32)],
        scratch_shapes=[pltpu.VMEM((S + 2 * PAD, PW), f32)],
        compiler_params=_CP, name="pool_bwd")(dy, diff, wbd, scale)


def _attn_blocks(lc):
    bpc = lc // QB
    kw = min(2 * QB, lc)
    blocks = []
    for b in range(S // QB):
        t0 = (b % bpc) * QB
        ks_in = min(max(t0 - HALF, 0), lc - kw)
        blocks.append((b * QB, (b // bpc) * lc + ks_in, t0 - ks_in))
    return kw, blocks


def _attn_bias(bias_ref, kw, shifts):
    r = lax.broadcasted_iota(jnp.int32, (2 * QB, kw), 0) % QB
    c = lax.broadcasted_iota(jnp.int32, (2 * QB, kw), 1)
    for i, shift in enumerate(shifts):
        bias_ref[i] = jnp.where(jnp.abs(r + shift - c) <= HALF, 0.0, MASK_VALUE).astype(f32)


def _head_put(stats, pair, v0, v1, lane):
    return jnp.where(lane == 2 * pair, v0, jnp.where(lane == 2 * pair + 1, v1, stats))


def _head_cols(stats, pair, lane):
    c0 = jnp.sum(jnp.where(lane == 2 * pair, stats, 0.0), axis=-1, keepdims=True)
    c1 = jnp.sum(jnp.where(lane == 2 * pair + 1, stats, 0.0), axis=-1, keepdims=True)
    return jnp.concatenate([c0, c1], axis=0)


def _head_spread(stats, pair, head0):
    return jnp.where(head0, stats[:, 2 * pair:2 * pair + 1], stats[:, 2 * pair + 1:2 * pair + 2])


def _stack_heads(blk, head0):
    zero = jnp.zeros_like(blk)
    return jnp.concatenate([jnp.where(head0, blk, zero), jnp.where(head0, zero, blk)], axis=0)


def _attn_fwd(q, k, v, lc, after=None):
    kw, blocks = _attn_blocks(lc)
    shifts = sorted({b[2] for b in blocks})

    def body(q_ref, k_ref, v_ref, *refs):
        o_ref, lse_ref, bias_ref = refs[-3:]
        lane = lax.broadcasted_iota(jnp.int32, (QB, 128), 1)
        head0 = lane < 64
        pair = pl.program_id(0)
        _attn_bias(bias_ref, kw, shifts)

        @pl.when(pair == 0)
        def _():
            lse_ref[...] = jnp.zeros_like(lse_ref)

        for row0, kstart, shift in blocks:
            q2 = _stack_heads(q_ref[pl.ds(row0, QB), :], head0)
            kb = k_ref[pl.ds(kstart, kw), :]
            vb = v_ref[pl.ds(kstart, kw), :]
            s = _dot_nt(q2, kb) + bias_ref[shifts.index(shift)]
            m = jnp.max(s, axis=-1, keepdims=True)
            p = jnp.exp(s - m)
            den = jnp.sum(p, axis=-1, keepdims=True)
            o2 = _dot_nn(p.astype(bf16), vb) / den
            lse2 = m + jnp.log(den)
            o_ref[pl.ds(row0, QB), :] = jnp.where(head0, o2[:QB], o2[QB:])
            lse_ref[pl.ds(row0, QB), :] = _head_put(lse_ref[pl.ds(row0, QB), :], pair, lse2[:QB], lse2[QB:], lane)

    col = pl.BlockSpec((S, 128), lambda p: (0, p))
    extra = () if after is None else (after,)
    return pl.pallas_call(
        body, grid=(NG,), in_specs=[col, col, col] + [_ANY] * len(extra),
        out_specs=[col, pl.BlockSpec((S, 128), lambda p: (0, 0))],
        out_shape=[SDS((S, AW), f32), SDS((S, 128), f32)],
        scratch_shapes=[pltpu.VMEM((len(shifts), 2 * QB, kw), f32)],
        compiler_params=_CP, name=f"attn_fwd_{lc}")(q, k, v, *extra)


def _attn_bwd(q, k, v, do, lse, delta, lc):
    kw, blocks = _attn_blocks(lc)
    shifts = sorted({b[2] for b in blocks})

    def body(q_ref, k_ref, v_ref, do_ref, lse_ref, dl_ref, dq_ref, dk_ref, dv_ref, bias_ref):
        lane = lax.broadcasted_iota(jnp.int32, (QB, 128), 1)
        head0 = lane < 64
        pair = pl.program_id(0)
        _attn_bias(bias_ref, kw, shifts)
        dk_ref[...] = jnp.zeros_like(dk_ref)
        dv_ref[...] = jnp.zeros_like(dv_ref)
        for row0, kstart, shift in blocks:
            q2 = _stack_heads(q_ref[pl.ds(row0, QB), :], head0)
            do2 = _stack_heads(do_ref[pl.ds(row0, QB), :], head0)
            lse2 = _head_cols(lse_ref[pl.ds(row0, QB), :], pair, lane)
            dl2 = _head_cols(dl_ref[pl.ds(row0, QB), :], pair, lane)
            kb = k_ref[pl.ds(kstart, kw), :]
            vb = v_ref[pl.ds(kstart, kw), :]
            p = jnp.exp(_dot_nt(q2, kb) + bias_ref[shifts.index(shift)] - lse2)
            ds = (p * (_dot_nt(do2, vb) - dl2)).astype(bf16)
            dq2 = _dot_nn(ds, kb)
            dq_ref[pl.ds(row0, QB), :] = jnp.where(head0, dq2[:QB], dq2[QB:])
            dk_ref[pl.ds(kstart, kw), :] += _dot_tn(ds, q2)
            dv_ref[pl.ds(kstart, kw), :] += _dot_tn(p.astype(bf16), do2)

    col = pl.BlockSpec((S, 128), lambda p: (0, p))
    stats = pl.BlockSpec((S, 128), lambda p: (0, 0))
    return pl.pallas_call(
        body, grid=(NG,), in_specs=[col] * 4 + [stats] * 2, out_specs=[col] * 3,
        out_shape=[SDS((S, AW), f32)] * 3,
        scratch_shapes=[pltpu.VMEM((len(shifts), 2 * QB, kw), f32)],
        compiler_params=_CP, name=f"attn_bwd_{lc}")(q, k, v, do, lse, delta)


def _mix_out_fwd(x, ypool, o1, l1, o4, l4, o16, l16, wout, l=None):
    def body(x_ref, yp_ref, o1_ref, l1_ref, o4_ref, l4_ref, o16_ref, l16_ref, w_ref,
             xo_ref, mixed_ref, o_ref, lse1_ref, lse4_ref, lse16_ref, so4, so16, sl4, sl16, sl):
        head0 = lax.broadcasted_iota(jnp.int32, (TM, 128), 1) < 64
        for r in range(4):
            sl4[pl.ds(r, TM // 4, stride=4), :] = l4_ref[r]
            for j in range(NG):
                so4[j, pl.ds(r, TM // 4, stride=4), :] = o4_ref[r, :, _cols(j)]
        for r in range(16):
            sl16[pl.ds(r, TM // 16, stride=16), :] = l16_ref[r]
            for j in range(NG):
                so16[j, pl.ds(r, TM // 16, stride=16), :] = o16_ref[r, :, _cols(j)]
        a, b, c = l1_ref[...], sl4[...], sl16[...]
        m = jnp.maximum(jnp.maximum(a, b), c)
        wa, wb, wc = jnp.exp(a - m), jnp.exp(b - m), jnp.exp(c - m)
        den = wa + wb + wc
        wa, wb, wc = wa / den, wb / den, wc / den
        lse = m + jnp.log(den)
        lse1_ref[...] = lse
        sl[...] = lse
        mixed_ref[:, :PW] = yp_ref[...].astype(bf16)
        for j in range(NG):
            y = (_head_spread(wa, j, head0) * o1_ref[:, _cols(j)] + _head_spread(wb, j, head0) * so4[j]
                 + _head_spread(wc, j, head0) * so16[j])
            o_ref[:, _cols(j)] = y
            mixed_ref[:, PW + 128 * j: PW + 128 * (j + 1)] = y.astype(bf16)
        for r in range(4):
            lse4_ref[r] = sl[pl.ds(r, TM // 4, stride=4), :]
        for r in range(16):
            lse16_ref[r] = sl[pl.ds(r, TM // 16, stride=16), :]
        xo_ref[...] = x_ref[...] + _dot_nn(mixed_ref[...], w_ref[...])

    wide, narrow = pltpu.VMEM((NG, TM, 128), f32), pltpu.VMEM((TM, 128), f32)
    return pl.pallas_call(
        body, grid=(S // TM,),
        in_specs=[_tile(D), _tile(PW), _tile(AW), _tile(128), _p4(), _p4(128), _p16(), _p16(128), _layer(D, D, l)],
        out_specs=[_tile(D), _tile(D), _tile(AW), _tile(128), _p4(128), _p16(128)],
        out_shape=[SDS((S, D), f32), SDS((S, D), bf16), SDS((S, AW), f32), SDS((S, 128), f32),
                   SDS((4, S // 4, 128), f32), SDS((16, S // 16, 128), f32)],
        scratch_shapes=[wide, wide, narrow, narrow, narrow],
        compiler_params=_CP, name="mix_out_fwd")(x, ypool, o1, l1, o4, l4, o16, l16, wout)


def _mix_out_bwd(dxo, o, wout, l=None):
    def body(dxo_ref, o_ref, w_ref, dxb_ref, dyp_ref, do1, do4, do16, dl1, dl4, dl16, sdo, sdl):
        dxb = dxo_ref[...].astype(bf16)
        dxb_ref[...] = dxb
        dm = _dot_nt(dxb, w_ref[...])
        dyp_ref[...] = dm[:, :PW]
        lane = lax.broadcasted_iota(jnp.int32, (TM, 128), 1)
        head0 = lane < 64
        dl = jnp.zeros((TM, 128), f32)
        for j in range(NG):
            d = dm[:, PW + 128 * j: PW + 128 * (j + 1)]
            prod = d * o_ref[:, _cols(j)]
            dl = _head_put(dl, j, jnp.sum(jnp.where(head0, prod, 0.0), axis=-1, keepdims=True),
                           jnp.sum(jnp.where(head0, 0.0, prod), axis=-1, keepdims=True), lane)
            do1[:, _cols(j)] = d.astype(bf16)
            sdo[j] = d
        dl1[...] = dl
        sdl[...] = dl
        for r in range(4):
            dl4[r] = sdl[pl.ds(r, TM // 4, stride=4), :]
            for j in range(NG):
                do4[r, :, _cols(j)] = sdo[j, pl.ds(r, TM // 4, stride=4), :].astype(bf16)
        for r in range(16):
            dl16[r] = sdl[pl.ds(r, TM // 16, stride=16), :]
            for j in range(NG):
                do16[r, :, _cols(j)] = sdo[j, pl.ds(r, TM // 16, stride=16), :].astype(bf16)

    return pl.pallas_call(
        body, grid=(S // TM,),
        in_specs=[_tile(D), _tile(AW), _layer(D, D, l)],
        out_specs=[_tile(D), _tile(PW), _tile(AW), _p4(), _p16(), _tile(128), _p4(128), _p16(128)],
        out_shape=[SDS((S, D), bf16), SDS((S, PW), f32),
                   SDS((S, AW), bf16), SDS((4, S // 4, AW), bf16), SDS((16, S // 16, AW), bf16),
                   SDS((S, 128), f32), SDS((4, S // 4, 128), f32), SDS((16, S // 16, 128), f32)],
        scratch_shapes=[pltpu.VMEM((NG, TM, 128), f32), pltpu.VMEM((TM, 128), f32)],
        compiler_params=_CP, name="mix_out_bwd")(dxo, o, wout)


def _loss_head(x, g, target):
    def body(x_ref, g_ref, t_ref, dx_ref, loss_ref, dg_ref):
        g = g_ref[...]
        r, xh, y = _rms(x_ref[...], g)
        err = y - t_ref[...]
        dy = err * (1.0 / D)

        @pl.when(pl.program_id(0) == 0)
        def _():
            loss_ref[...] = jnp.zeros_like(loss_ref)
            dg_ref[...] = jnp.zeros_like(dg_ref)

        loss_ref[...] += jnp.broadcast_to(0.5 * jnp.sum(jnp.mean(err * err, axis=-1, keepdims=True)), (1, D))
        dg_ref[...] += jnp.sum(dy * xh, axis=0, keepdims=True)
        dx_ref[...] = _rms_bwd(dy, r, xh, g)

    return pl.pallas_call(
        body, grid=(S // TM,),
        in_specs=[_tile(D), _const((1, D)), _tile(D)],
        out_specs=[_tile(D), _const((1, D)), _const((1, D))],
        out_shape=[SDS((S, D), f32), SDS((1, D), f32), SDS((1, D), f32)],
        compiler_params=_CP, name="loss_head")(x, g, target)


def _peer(k):
    x, y, c = lax.axis_index("x"), lax.axis_index("y"), lax.axis_index("c")
    px = 1 - x if k & 4 else x
    py = 1 - y if k & 2 else y
    pc = 1 - c if k & 1 else c
    return (px, py, pc), 4 * px + 2 * py + pc


def _diag_route():
    x, y, c = lax.axis_index("x"), lax.axis_index("y"), lax.axis_index("c")
    idx_x, idx_y = _peer(4)[1], _peer(2)[1]
    return idx_x + c * (idx_y - idx_x), (x + c * (1 - 2 * x), (1 - y) + c * (2 * y - 1), c)


def _all_gather(lands):
    n = len(lands)

    def body(*refs):
        zones, send_sems, recv_sems = refs[n:2 * n], refs[2 * n], refs[2 * n + 1]
        me, me_idx = _peer(0)
        sibling, sib_idx = _peer(1)
        (x_nbr, idx_x), (y_nbr, idx_y), idx_d = _peer(4), _peer(2), _peer(6)[1]
        fwd_idx, fwd_dev = _diag_route()

        def copy(k, t, idx, to):
            return _row_copy(zones[t], idx, send_sems.at[k, t], recv_sems.at[k, t], to)

        sent = []

        def send(k, t, idx, to):
            cp = copy(k, t, idx, to)
            cp.start()
            sent.append(cp)

        for t in range(n):
            send(0, t, me_idx, sibling)
            send(1, t, me_idx, x_nbr)
            send(2, t, me_idx, y_nbr)
        for t in range(n):
            copy(1, t, idx_x, me).wait_recv()
            send(3, t, idx_x, sibling)
        for t in range(n):
            copy(2, t, idx_y, me).wait_recv()
            send(4, t, idx_y, sibling)
        for t in range(n):
            send(5, t, fwd_idx, fwd_dev)
        for t in range(n):
            copy(5, t, idx_d, me).wait_recv()
            send(6, t, idx_d, sibling)
        for k, mask in ((0, 1), (3, 5), (4, 3), (6, 7)):
            for t in range(n):
                copy(k, t, _peer(mask)[1], me).wait_recv()
        for cp in sent:
            cp.wait_send()

    return pl.pallas_call(
        body, in_specs=[_ANY] * n, out_specs=[_ANY] * n,
        out_shape=[SDS(a.shape, a.dtype) for a in lands], input_output_aliases={t: t for t in range(n)},
        scratch_shapes=[pltpu.SemaphoreType.DMA((7, n)), pltpu.SemaphoreType.DMA((7, n))],
        name="all_gather_weights")(*lands)


def _hbm(a):
    return pltpu.with_memory_space_constraint(a, pltpu.HBM)


def _rows(ref, idx):
    r = ref.shape[0] // NDEV
    return ref.at[pl.ds(idx * r, r), :]


def _row_copy(ref, idx, send_sem, recv_sem, to):
    return pltpu.make_async_remote_copy(src_ref=_rows(ref, idx), dst_ref=_rows(ref, idx), send_sem=send_sem,
                                        recv_sem=recv_sem, device_id=to, device_id_type=_MESH)


def _place_own(me, shards, l):
    n = len(shards)

    def body(me_ref, *refs):
        for t in range(n):
            refs[n + t][...] = refs[t][...]

    grid_spec = pltpu.PrefetchScalarGridSpec(
        num_scalar_prefetch=1, grid=(1,),
        in_specs=[pl.BlockSpec((None, s.shape[1], D), lambda i, me_ref: (l, 0, 0)) for s in shards],
        out_specs=[pl.BlockSpec((s.shape[1], D), lambda i, me_ref: (me_ref[0], 0)) for s in shards])
    return pl.pallas_call(
        body, grid_spec=grid_spec, out_shape=[SDS((NDEV * s.shape[1], D), s.dtype) for s in shards],
        compiler_params=_CP, name="place_own")(me, *shards)


_TOKEN = SDS((8, 128), f32)
def _ag_start(lands, after, l):
    n = len(lands)
    after = list(after) if isinstance(after, (list, tuple)) else [after]

    def body(*refs):
        zones, send_sems, recv_sems, token = refs[:n], refs[n + len(after)], refs[n + len(after) + 1], refs[-1]
        _, me_idx = _peer(0)
        for k, mask in enumerate((1, 4, 2)):
            for t in range(n):
                _row_copy(zones[t], me_idx, send_sems.at[k * n + t], recv_sems.at[k * n + t], _peer(mask)[0]).start()
        token[...] = jnp.zeros_like(token)

    outs = pl.pallas_call(
        body, name=f"ag_start_{l}", in_specs=[_HBM] * n + [_ANY] * len(after),
        out_specs=(_SEM, _SEM, *[_HBM] * n, pl.BlockSpec(memory_space=pltpu.VMEM)),
        out_shape=(pltpu.SemaphoreType.DMA((3 * n,)), pltpu.SemaphoreType.DMA((3 * n,)),
                   *[pltpu.HBM(a.shape, a.dtype) for a in lands], _TOKEN),
        input_output_aliases={t: 2 + t for t in range(n)}, compiler_params=_CP_SPLIT)(
            *[_hbm(a) for a in lands], *after)
    return outs[0], outs[1], list(outs[2:2 + n]), outs[-1]


def _ag_pass(lands, recv_sems, after, l):
    n = len(lands)
    after = list(after) if isinstance(after, (list, tuple)) else [after]

    def body(*refs):
        zones, recv_sems = refs[:n], refs[n]
        psend, precv, token = refs[n + 1 + len(after)], refs[n + 2 + len(after)], refs[-1]
        me, _ = _peer(0)
        sibling, _ = _peer(1)
        for j, mask in enumerate((4, 2)):
            idx = _peer(mask)[1]
            for t in range(n):
                _row_copy(zones[t], idx, psend.at[j * n + t], recv_sems.at[(1 + j) * n + t], me).wait_recv()
                _row_copy(zones[t], idx, psend.at[j * n + t], precv.at[j * n + t], sibling).start()
        fwd_idx, fwd_dev = _diag_route()
        for t in range(n):
            _row_copy(zones[t], fwd_idx, psend.at[2 * n + t], precv.at[2 * n + t], fwd_dev).start()
        token[...] = jnp.zeros_like(token)

    outs = pl.pallas_call(
        body, name=f"ag_pass_{l}", in_specs=[_HBM] * n + [_SEM] + [_ANY] * len(after),
        out_specs=(_SEM, _SEM, *[_HBM] * n, pl.BlockSpec(memory_space=pltpu.VMEM)),
        out_shape=(pltpu.SemaphoreType.DMA((3 * n,)), pltpu.SemaphoreType.DMA((3 * n,)),
                   *[pltpu.HBM(a.shape, a.dtype) for a in lands], _TOKEN),
        input_output_aliases={t: 2 + t for t in range(n)}, compiler_params=_CP_SPLIT)(*lands, recv_sems, *after)
    return outs[0], outs[1], list(outs[2:2 + n]), outs[-1]


def _ag_last(lands, precv, after, l):
    n = len(lands)
    after = list(after) if isinstance(after, (list, tuple)) else [after]

    def body(*refs):
        zones, precv = refs[:n], refs[n]
        qsend, qrecv, token = refs[n + 1 + len(after)], refs[n + 2 + len(after)], refs[-1]
        me, _ = _peer(0)
        sibling, _ = _peer(1)
        idx = _peer(6)[1]
        for t in range(n):
            _row_copy(zones[t], idx, qsend.at[t], precv.at[2 * n + t], me).wait_recv()
            _row_copy(zones[t], idx, qsend.at[t], qrecv.at[t], sibling).start()
        token[...] = jnp.zeros_like(token)

    outs = pl.pallas_call(
        body, name=f"ag_last_{l}", in_specs=[_HBM] * n + [_SEM] + [_ANY] * len(after),
        out_specs=(_SEM, _SEM, *[_HBM] * n, pl.BlockSpec(memory_space=pltpu.VMEM)),
        out_shape=(pltpu.SemaphoreType.DMA((n,)), pltpu.SemaphoreType.DMA((n,)),
                   *[pltpu.HBM(a.shape, a.dtype) for a in lands], _TOKEN),
        input_output_aliases={t: 2 + t for t in range(n)}, compiler_params=_CP_SPLIT)(*lands, precv, *after)
    return outs[0], outs[1], list(outs[2:2 + n]), outs[-1]


def _ag_wait(lands, send_sems, recv_sems, psend, precv, qsend, qrecv, after, l):
    n = len(lands)
    after = list(after) if isinstance(after, (list, tuple)) else [after]

    def body(*refs):
        zones = refs[:n]
        send_sems, recv_sems, psend, precv, qsend, qrecv = refs[n:n + 6]
        me, me_idx = _peer(0)
        for k in range(3):
            for t in range(n):
                _row_copy(zones[t], me_idx, send_sems.at[k * n + t], recv_sems.at[k * n + t], me).wait_send()
        for t in range(n):
            _row_copy(zones[t], _peer(1)[1], send_sems.at[t], recv_sems.at[t], me).wait_recv()
        fwd_idx, _ = _diag_route()
        for j, (mine, theirs) in enumerate(((_peer(4)[1], _peer(5)[1]), (_peer(2)[1], _peer(3)[1]))):
            for t in range(n):
                _row_copy(zones[t], mine, psend.at[j * n + t], precv.at[j * n + t], me).wait_send()
                _row_copy(zones[t], theirs, psend.at[j * n + t], precv.at[j * n + t], me).wait_recv()
        for t in range(n):
            _row_copy(zones[t], fwd_idx, psend.at[2 * n + t], precv.at[2 * n + t], me).wait_send()
            _row_copy(zones[t], _peer(6)[1], qsend.at[t], qrecv.at[t], me).wait_send()
            _row_copy(zones[t], _peer(7)[1], qsend.at[t], qrecv.at[t], me).wait_recv()

    outs = pl.pallas_call(
        body, name=f"ag_wait_{l}", in_specs=[_HBM] * n + [_SEM] * 6 + [_ANY] * len(after),
        out_specs=tuple([_HBM] * n), out_shape=tuple(pltpu.HBM(a.shape, a.dtype) for a in lands),
        input_output_aliases={t: t for t in range(n)}, compiler_params=_CP_SPLIT)(
            *lands, send_sems, recv_sems, psend, precv, qsend, qrecv, *after)
    return list(outs)


def _xchg_src(ref, slot_ref, idx):
    return _rows(ref, idx) if ref.shape[0] == NDEV * slot_ref.shape[1] else ref


def _rs_start(srcs, slots, after, tag):
    n = len(srcs)
    after = list(after) if isinstance(after, (list, tuple)) else [after]

    def body(*refs):
        src, slot = refs[:n], refs[n:2 * n]
        send_sems, recv_sems, token = refs[2 * n + len(after)], refs[2 * n + len(after) + 1], refs[-1]
        _, me_idx = _peer(0)
        for k in range(1, NDEV):
            dev, idx = _peer(k)
            for t in range(n):
                pltpu.make_async_remote_copy(
                    src_ref=_xchg_src(src[t], slot[t], idx), dst_ref=slot[t].at[me_idx],
                    send_sem=send_sems.at[(k - 1) * n + t], recv_sem=recv_sems.at[(k - 1) * n + t],
                    device_id=dev, device_id_type=_MESH).start()
        token[...] = jnp.zeros_like(token)

    outs = pl.pallas_call(
        body, name=f"rs_start_{tag}", in_specs=[_HBM] * (2 * n) + [_ANY] * len(after),
        out_specs=(_SEM, _SEM, *[_HBM] * (2 * n), pl.BlockSpec(memory_space=pltpu.VMEM)),
        out_shape=(pltpu.SemaphoreType.DMA(((NDEV - 1) * n,)), pltpu.SemaphoreType.DMA(((NDEV - 1) * n,)),
                   *[pltpu.HBM(a.shape, a.dtype) for a in list(srcs) + list(slots)], _TOKEN),
        input_output_aliases={t: 2 + t for t in range(2 * n)}, compiler_params=_CP_SPLIT)(
            *[_hbm(a) for a in list(srcs) + list(slots)], *after)
    return outs[0], outs[1], list(outs[2:2 + n]), list(outs[2 + n:2 + 2 * n]), outs[-1]


def _rs_wait(srcs, slots, send_sems, recv_sems, after, tag):
    n = len(srcs)
    after = list(after) if isinstance(after, (list, tuple)) else [after]

    def body(*refs):
        src, slot, send_sems, recv_sems = refs[:n], refs[n:2 * n], refs[2 * n], refs[2 * n + 1]
        me, _ = _peer(0)
        for k in range(1, NDEV):
            idx = _peer(k)[1]
            for t in range(n):
                cp = pltpu.make_async_remote_copy(
                    src_ref=_xchg_src(src[t], slot[t], idx), dst_ref=slot[t].at[idx],
                    send_sem=send_sems.at[(k - 1) * n + t], recv_sem=recv_sems.at[(k - 1) * n + t],
                    device_id=me, device_id_type=_MESH)
                cp.wait_send()
                cp.wait_recv()

    outs = pl.pallas_call(
        body, name=f"rs_wait_{tag}", in_specs=[_HBM] * (2 * n) + [_SEM, _SEM] + [_ANY] * len(after),
        out_specs=tuple([_HBM] * (2 * n)),
        out_shape=tuple(pltpu.HBM(a.shape, a.dtype) for a in list(srcs) + list(slots)),
        input_output_aliases={t: t for t in range(2 * n)}, compiler_params=_CP_SPLIT)(
            *srcs, *slots, send_sems, recv_sems, *after)
    return list(outs[:n]), list(outs[n:])


def _sum_slots(slots, rb):
    r = slots.shape[1]

    def body(s_ref, o_ref):
        acc = s_ref[0].astype(f32)
        for s in range(1, NDEV):
            acc = acc + s_ref[s].astype(f32)
        o_ref[...] = acc

    return pl.pallas_call(
        body, grid=(r // rb,),
        in_specs=[pl.BlockSpec((NDEV, rb, D), lambda i: (0, i, 0))],
        out_specs=pl.BlockSpec((rb, D), lambda i: (i, 0)),
        out_shape=SDS((r, D), f32), compiler_params=_CP, name="sum_slots")(slots)


def _adamw(w, g, m, v):
    shape = w.shape
    cols = shape[-1]
    rows = w.size // cols
    rb = rows
    for cand in (512, 256, 128, 64, 32, 16, 8):
        if rows % cand == 0 and rows > cand:
            rb = cand
            break

    def body(w_ref, g_ref, m_ref, v_ref, d_ref, mo_ref, vo_ref):
        d_ref[...], mo_ref[...], vo_ref[...] = _adamw_math(w_ref[...], g_ref[...], m_ref[...], v_ref[...])

    spec = pl.BlockSpec((rb, cols), lambda i: (i, 0))
    outs = pl.pallas_call(
        body, grid=(rows // rb,), in_specs=[spec] * 4, out_specs=[spec] * 3,
        out_shape=[SDS((rows, cols), f32)] * 3, compiler_params=_CP, name="adamw")(
            *(a.reshape(rows, cols) for a in (w, g, m, v)))
    return tuple(o.reshape(shape) for o in outs)


def _adamw_math(w, g, m, v):
    m = ADAM_B1 * m + (1.0 - ADAM_B1) * g
    v = ADAM_B2 * v + (1.0 - ADAM_B2) * (g * g)
    m_hat = m / (1.0 - ADAM_B1 ** ADAM_STEP)
    v_hat = v / (1.0 - ADAM_B2 ** ADAM_STEP)
    return -ADAM_LR * (m_hat / (jnp.sqrt(v_hat) + ADAM_EPS) + ADAM_WD * w), m, v


def _reduce_adamw(acc, me, full, slots, w, m, v, l):
    _, r, _ = w.shape
    rb = r // 2 if r > 128 else r

    def body(me_ref, full_ref, slots_ref, w_ref, m_ref, v_ref, *refs):
        go_ref, d_ref, mo_ref, vo_ref = refs[-4:]
        own = full_ref[...].astype(f32)
        g = None
        for s in range(NDEV):
            part = jnp.where(me_ref[0] == s, own, slots_ref[s].astype(f32))
            g = part if g is None else g + part
        go_ref[...] = g
        d_ref[...], mo_ref[...], vo_ref[...] = _adamw_math(w_ref[...], g, m_ref[...], v_ref[...])

    steps = r // rb
    lay = pl.BlockSpec((None, rb, D), lambda i, me_ref: (l, i, 0))
    n_acc = 0 if acc is None else 4
    grid_spec = pltpu.PrefetchScalarGridSpec(
        num_scalar_prefetch=1, grid=(steps,),
        in_specs=[pl.BlockSpec((rb, D), lambda i, me_ref: (me_ref[0] * steps + i, 0)),
                  pl.BlockSpec((NDEV, rb, D), lambda i, me_ref: (0, i, 0)), lay, lay, lay] + [_ANY] * n_acc,
        out_specs=[lay] * 4)
    outs = pl.pallas_call(
        body, grid_spec=grid_spec, out_shape=[SDS(w.shape, f32)] * 4,
        input_output_aliases={6 + j: j for j in range(n_acc)},
        compiler_params=_CP, name="reduce_adamw")(me, full, slots, w, m, v, *(() if acc is None else acc))
    return tuple(outs)


_BIG = ("ffn1_w_gate", "ffn1_w_up", "ffn1_w_down", "w_in", "w_out", "ffn2_w_gate", "ffn2_w_up", "ffn2_w_down")
_TRANSPOSED = ("ffn1_w_gate", "ffn1_w_up", "w_in", "ffn2_w_gate", "ffn2_w_up")

def _block_diag(pool_w):
    out = jnp.zeros((L, PW, PW), pool_w.dtype)
    for gi in range(4):
        out = out.at[:, 64 * gi:64 * (gi + 1), 64 * gi:64 * (gi + 1)].set(pool_w[:, gi])
    return out


def kernel(x, positions, ffn1_norm, ffn1_w_gate, ffn1_w_up, ffn1_w_down, mix_norm, w_in, pool_w, pool_scale, w_out, ffn2_norm, ffn2_w_gate, ffn2_w_up, ffn2_w_down, final_norm, loss_target, m_ffn1_norm, m_ffn1_w_gate, m_ffn1_w_up, m_ffn1_w_down, m_mix_norm, m_w_in, m_pool_w, m_pool_scale, m_w_out, m_ffn2_norm, m_ffn2_w_gate, m_ffn2_w_up, m_ffn2_w_down, m_final_norm, v_ffn1_norm, v_ffn1_w_gate, v_ffn1_w_up, v_ffn1_w_down, v_mix_norm, v_w_in, v_pool_w, v_pool_scale, v_w_out, v_ffn2_norm, v_ffn2_w_gate, v_ffn2_w_up, v_ffn2_w_down, v_final_norm):
    weights = dict(ffn1_norm=ffn1_norm, ffn1_w_gate=ffn1_w_gate, ffn1_w_up=ffn1_w_up, ffn1_w_down=ffn1_w_down,
                   mix_norm=mix_norm, w_in=w_in, pool_w=pool_w, pool_scale=pool_scale, w_out=w_out,
                   ffn2_norm=ffn2_norm, ffn2_w_gate=ffn2_w_gate, ffn2_w_up=ffn2_w_up, ffn2_w_down=ffn2_w_down,
                   final_norm=final_norm)
    moms = dict(ffn1_norm=m_ffn1_norm, ffn1_w_gate=m_ffn1_w_gate, ffn1_w_up=m_ffn1_w_up, ffn1_w_down=m_ffn1_w_down,
                mix_norm=m_mix_norm, w_in=m_w_in, pool_w=m_pool_w, pool_scale=m_pool_scale, w_out=m_w_out,
                ffn2_norm=m_ffn2_norm, ffn2_w_gate=m_ffn2_w_gate, ffn2_w_up=m_ffn2_w_up, ffn2_w_down=m_ffn2_w_down,
                final_norm=m_final_norm)
    vels = dict(ffn1_norm=v_ffn1_norm, ffn1_w_gate=v_ffn1_w_gate, ffn1_w_up=v_ffn1_w_up, ffn1_w_down=v_ffn1_w_down,
                mix_norm=v_mix_norm, w_in=v_w_in, pool_w=v_pool_w, pool_scale=v_pool_scale, w_out=v_w_out,
                ffn2_norm=v_ffn2_norm, ffn2_w_gate=v_ffn2_w_gate, ffn2_w_up=v_ffn2_w_up, ffn2_w_down=v_ffn2_w_down,
                final_norm=v_final_norm)
    names = list(weights)

    me_idx = 4 * lax.axis_index("x") + 2 * lax.axis_index("y") + lax.axis_index("c")
    me_arr = me_idx.reshape(1).astype(jnp.int32)

    tr = lambda w: jnp.swapaxes(w, 1, 2).astype(bf16)
    shards = [tr(weights[nm]) if nm in _TRANSPOSED else weights[nm].astype(bf16) for nm in _BIG]

    def landing_zones(l, which):
        return _place_own(me_arr, [shards[t] for t in which], l)

    g_ffn1 = [ffn1_norm[l].reshape(1, D) for l in range(L)]
    g_mix = [mix_norm[l].reshape(1, D) for l in range(L)]
    g_ffn2 = [ffn2_norm[l].reshape(1, D) for l in range(L)]
    wbd_all = _block_diag(pool_w).astype(bf16)
    wbd = [wbd_all[l] for l in range(L)]
    pscale = [pool_scale[l].reshape(1, PW) for l in range(L)]
    tabs = _rope_tables(positions)
    flat = lambda a: a.reshape(S, a.shape[-1])
    r4 = lambda a: a.reshape(4, S // 4, a.shape[-1])
    r16 = lambda a: a.reshape(16, S // 16, a.shape[-1])

    first, rest, whole = (0, 1, 2, 3), (4, 5, 6, 7), tuple(range(8))

    def ag_begin(l, which, after):
        tag = f"{l}{'' if which == whole else 'r'}"
        send_sems, recv_sems, zones, token = _ag_start(landing_zones(l, which), after, tag)
        return dict(tag=tag, zones=zones, s=send_sems, r=recv_sems), token

    def ag_second(ch, after):
        ch["ps"], ch["pr"], ch["zones"], token = _ag_pass(ch["zones"], ch["r"], after, ch["tag"])
        return token

    def ag_third(ch, after):
        ch["qs"], ch["qr"], ch["zones"], token = _ag_last(ch["zones"], ch["pr"], after, ch["tag"])
        return token

    def ag_end(ch, after):
        return _ag_wait(ch["zones"], ch["s"], ch["r"], ch["ps"], ch["pr"], ch["qs"], ch["qr"], after, ch["tag"])

    head = _all_gather(landing_zones(0, first))
    ch_rest, tok_rest = ag_begin(0, rest, head[0])
    chains = {}
    chains[1], tok_next = ag_begin(1, whole, head[0])
    gathered = [None] * L
    xs = x.reshape(S, D)
    saved = []
    for l in range(L):
        ga, gb = g_ffn1[l], g_ffn2[l]
        if l == 0:
            gt1, ut1, dn1, wint = head
            ga = ga + tok_rest[0, 0] + tok_next[0, 0]
        else:
            gt1, ut1, dn1, wint, wout, gt2, ut2, dn2 = gathered[l]
        x0 = xs
        x1, gate1, up1 = _ffn_fwd(x0, ga, gt1, ut1, dn1)
        hmix, vp, q1, k1, v1, q4, k4, v4, q16, k16, v16 = _mix_in_fwd(x1, g_mix[l], wint, tabs)
        q4, k4, v4, q16, k16, v16 = map(flat, (q4, k4, v4, q16, k16, v16))
        ypool, diff = _pool_fwd(vp, wbd[l], pscale[l])
        after_attn = None
        if l == 0:
            after_attn = ag_second(ch_rest, [ypool, q16])
        o1, l1 = _attn_fwd(q1, k1, v1, S, after=after_attn)
        o4, l4 = _attn_fwd(q4, k4, v4, S // 4, after=after_attn)
        o16, l16 = _attn_fwd(q16, k16, v16, S // 16, after=after_attn)
        if l == 0:
            token = ag_third(ch_rest, [o1, o4, o16])
            wout, gt2, ut2, dn2 = ag_end(ch_rest, token)
            gathered[0] = list(head) + [wout, gt2, ut2, dn2]
        elif l + 1 < L:
            gb = gb + ag_second(chains[l + 1], [o1, o4, o16])[0, 0]
        x2, mixed, o, lse1, lse4, lse16 = _mix_out_fwd(x1, ypool, o1, l1, r4(o4), r4(l4), r16(o16), r16(l16), wout)
        if l == 0:
            gb = gb + ag_second(chains[1], x2)[0, 0]
        x3, gate2, up2 = _ffn_fwd(x2, gb, gt2, ut2, dn2)
        if l + 1 < L:
            token = ag_third(chains[l + 1], x3)
            if l + 2 < L:
                chains[l + 2], token = ag_begin(l + 2, whole, token)
            gathered[l + 1] = ag_end(chains[l + 1], token)
        saved.append(dict(x0=x0, x1=x1, x2=x2, gate1=gate1, up1=up1, gate2=gate2, up2=up2, hmix=hmix, diff=diff,
                          qkv=((q1, k1, v1), (q4, k4, v4), (q16, k16, v16)), mixed=mixed, o=o,
                          lse=(lse1, flat(lse4), flat(lse16))))
        xs = x3

    dx, loss_part, d_final = _loss_head(xs, final_norm.reshape(1, D), loss_target.reshape(S, D))

    d_norm = {nm: [None] * L for nm in ("ffn1_norm", "mix_norm", "ffn2_norm")}
    d_poolw, d_pscale = [None] * L, [None] * L
    group_a = ("ffn2_w_gate", "ffn2_w_up", "ffn2_w_down", "w_out")
    group_b = ("ffn1_w_gate", "ffn1_w_up", "ffn1_w_down", "w_in")
    acc = {}

    as_rows = lambda a, nm: jnp.swapaxes(a, 1, 2) if nm in _TRANSPOSED else a
    w_rows = {nm: as_rows(weights[nm], nm) for nm in _BIG}
    m_rows = {nm: as_rows(moms[nm], nm) for nm in _BIG}
    v_rows = {nm: as_rows(vels[nm], nm) for nm in _BIG}

    def exchange(full, group, after, tag, extra=None):
        srcs = [full[nm] for nm in group]
        slots = [lax.empty((NDEV, g.shape[0] // NDEV, D), bf16) for g in srcs]
        if extra is not None:
            srcs, slots = srcs + [extra[0]], slots + [extra[1]]
        ssem, rsem, srcs, slots, token = _rs_start(srcs, slots, after, tag)
        return (srcs, slots, ssem, rsem, tag), token

    def update(l, group, flight, after):
        srcs, slots, ssem, rsem, tag = flight
        srcs, slots = _rs_wait(srcs, slots, ssem, rsem, after, tag)
        for nm, full_g, slots_g in zip(group, srcs, slots):
            acc[nm] = _reduce_adamw(acc.get(nm), me_arr, full_g, slots_g, w_rows[nm], m_rows[nm], v_rows[nm], l)
        return [acc[nm][0] for nm in group], slots

    flights = {}
    token_b = None
    for l in reversed(range(L)):
        sv = saved[l]
        gt1, ut1, dn1, wint, wout, gt2, ut2, dn2 = gathered[l]
        gb = g_ffn2[l] if token_b is None else g_ffn2[l] + token_b[0, 0]
        full = {}
        dx, dgate, dup, h, dy, d_norm["ffn2_norm"][l] = _ffn_bwd_d(sv["x2"], gb, sv["gate2"], sv["up2"], dx, gt2, ut2, dn2)
        full["ffn2_w_gate"], full["ffn2_w_up"], full["ffn2_w_down"] = _ffn_bwd_w(h, dy, sv["gate2"], sv["up2"], dgate, dup)

        dxb, dyp, do1, do4, do16, dl1, dl4, dl16 = _mix_out_bwd(dx, sv["o"], wout)
        full["w_out"] = _wgrad(sv["mixed"], dxb)
        flights[l, "a"], token_a = exchange(full, group_a, dxb, f"a{l}")
        dvp, dwbd, d_pscale[l] = _pool_bwd(dyp, sv["diff"], wbd[l], pscale[l] + token_a[0, 0])
        d_poolw[l] = jnp.stack([dwbd[64 * gi:64 * (gi + 1), 64 * gi:64 * (gi + 1)] for gi in range(4)])
        dos, dls = (do1, flat(do4), flat(do16)), (dl1, flat(dl4), flat(dl16))
        dqkv = []
        for b, lc in enumerate((S, S // 4, S // 16)):
            qb, kb, vb = sv["qkv"][b]
            dqkv.append(_attn_bwd(qb, kb, vb, dos[b], sv["lse"][b], dls[b], lc))
        d4 = tuple(r4(a) for a in dqkv[1])
        d16 = tuple(r16(a) for a in dqkv[2])
        dx, dproj, d_norm["mix_norm"][l] = _mix_in_bwd(dx, sv["x1"], g_mix[l], wint, tabs, dvp, dqkv[0], d4, d16)
        full["w_in"] = _wgrad(dproj, sv["hmix"])

        dx, dgate, dup, h, dy, d_norm["ffn1_norm"][l] = _ffn_bwd_d(sv["x0"], g_ffn1[l], sv["gate1"], sv["up1"], dx, gt1, ut1, dn1)
        full["ffn1_w_gate"], full["ffn1_w_up"], full["ffn1_w_down"] = _ffn_bwd_w(h, dy, sv["gate1"], sv["up1"], dgate, dup)

        after = dx
        if l + 1 < L and l + 1 >= 2:
            after, _ = update(l + 1, group_a, flights.pop((l + 1, "a")), after)
            after, _ = update(l + 1, group_b, flights.pop((l + 1, "b")), after)
        if l > 0:
            flights[l, "b"], token_b = exchange(full, group_b, after, f"b{l}")

    pad8 = lambda a: jnp.pad(a, ((0, 8 - a.shape[0]), (0, 0)))
    misc = jnp.concatenate([d_final, jnp.concatenate(d_pscale, axis=1), loss_part], axis=0)
    small = jnp.concatenate(
        [pad8(jnp.concatenate(d_norm[nm], axis=0)) for nm in ("ffn1_norm", "mix_norm", "ffn2_norm")]
        + [pad8(misc), jnp.stack(d_poolw).reshape(L * 16, D)], axis=0)
    small_slots = lax.dynamic_update_slice(lax.empty((NDEV, SMALL_ROWS, D), f32), small[None], (me_idx, 0, 0))
    flights[0, "b"], token_b = exchange(full, group_b, dx, "b0", extra=(small, small_slots))

    after = token_b
    for key in [(1, "a"), (1, "b"), (0, "a")]:
        after, _ = update(key[0], group_a if key[1] == "a" else group_b, flights.pop(key), after)
    after, slots_b0 = update(0, group_b, flights.pop((0, "b")), after)

    sm = _sum_slots(slots_b0[-1], SMALL_ROWS)
    grads = {}
    grads["ffn1_norm"], grads["mix_norm"], grads["ffn2_norm"] = sm[0:L], sm[8:8 + L], sm[16:16 + L]
    grads["final_norm"] = sm[24]
    grads["pool_scale"] = sm[25].reshape(L, PW)
    grads["pool_w"] = sm[32:32 + L * 16].reshape(L, 4, 64, 64)
    loss = sm[26, 0]
    upd = {nm: _adamw(weights[nm], grads[nm], moms[nm], vels[nm]) for nm in names if nm not in _BIG}
    for nm in _BIG:
        grads[nm], upd[nm] = as_rows(acc[nm][0], nm), tuple(as_rows(a, nm) for a in acc[nm][1:])
    return (loss, dx.reshape(1, S, D), *[grads[nm] for nm in names], *[upd[nm][0] for nm in names],
            *[upd[nm][1] for nm in names], *[upd[nm][2] for nm in names])
```

```python
import jax
import jax.numpy as jnp
from jax import lax
from jax.experimental import pallas as pl
from jax.experimental.pallas import tpu as pltpu

f32 = jnp.float32
bf16 = jnp.bfloat16
SDS = jax.ShapeDtypeStruct

D = 1024
S = 2048
F = 2816
L = 4
PW = 256
AW = 768
PROJ = PW + 3 * AW
NDEV = 8
TM = 256
QB = 128
HALF = 64
NG = AW // 128
NORM_EPS = 1e-6
MASK_VALUE = -1e30
ROPE_THETA = 500000.0
ADAM_LR, ADAM_B1, ADAM_B2, ADAM_EPS, ADAM_WD, ADAM_STEP = 0.001, 0.9, 0.999, 1e-08, 0.01, 10
POOL_WINDOWS = (2, 4, 8, 16)
PAD = 8
SMALL_ROWS = 96
VMEM_LIMIT = 56 * 1024 * 1024

_CP = pltpu.CompilerParams(vmem_limit_bytes=VMEM_LIMIT)
_ANY = pl.BlockSpec(memory_space=pl.ANY)
_HBM = pl.BlockSpec(memory_space=pltpu.HBM)
_SEM = pl.BlockSpec(memory_space=pltpu.SEMAPHORE)
_MESH = pl.DeviceIdType.MESH
_CP_SPLIT = pltpu.CompilerParams(has_side_effects=pltpu.SideEffectType.DATAFLOW_SIDE_EFFECTING)


def _dot_nn(a, b):
    return lax.dot_general(a, b, (((1,), (0,)), ((), ())), preferred_element_type=f32)


def _dot_nt(a, b):
    return lax.dot_general(a, b, (((1,), (1,)), ((), ())), preferred_element_type=f32)


def _dot_tn(a, b):
    return lax.dot_general(a, b, (((0,), (0,)), ((), ())), preferred_element_type=f32)


def _rms(x, g):
    r = lax.rsqrt(jnp.mean(x * x, axis=-1, keepdims=True) + NORM_EPS)
    xh = x * r
    return r, xh, xh * g


def _rms_bwd(dh, r, xh, g):
    dxh = dh * g
    return r * (dxh - xh * jnp.mean(dxh * xh, axis=-1, keepdims=True))


def _tile(cols):
    return pl.BlockSpec((TM, cols), lambda i: (i, 0))


def _const(shape):
    return pl.BlockSpec(shape, lambda i: (0,) * len(shape))


def _layer(rows, cols, l=None):
    return pl.BlockSpec((rows, cols), lambda i: (0, 0), pipeline_mode=pl.Buffered(1))


def _p4(cols=AW):
    return pl.BlockSpec((4, TM // 4, cols), lambda i: (0, i, 0))


def _p16(cols=AW):
    return pl.BlockSpec((16, TM // 16, cols), lambda i: (0, i, 0))


def _cols(j):
    return slice(128 * j, 128 * (j + 1))


def _ffn_fwd(x, g, gt, ut, dn, l=None):
    def body(x_ref, g_ref, gt_ref, ut_ref, dn_ref, xo_ref, gate_ref, up_ref):
        x = x_ref[...]
        _, _, hn = _rms(x, g_ref[...])
        h = hn.astype(bf16)
        gate = _dot_nt(h, gt_ref[...])
        up = _dot_nt(h, ut_ref[...])
        gate_ref[...] = gate.astype(bf16)
        up_ref[...] = up.astype(bf16)
        a = (gate * jax.nn.sigmoid(gate) * up).astype(bf16)
        xo_ref[...] = x + 0.5 * _dot_nn(a, dn_ref[...])

    return pl.pallas_call(
        body, grid=(S // TM,),
        in_specs=[_tile(D), _layer(1, D, l), _layer(F, D, l), _layer(F, D, l), _layer(F, D, l)],
        out_specs=[_tile(D), _tile(F), _tile(F)],
        out_shape=[SDS((S, D), f32), SDS((S, F), bf16), SDS((S, F), bf16)],
        compiler_params=_CP, name="ffn_fwd")(x, g, gt, ut, dn)


def _ffn_bwd_d(x, g, gate, up, dxo, gt, ut, dn, l=None):
    def body(x_ref, g_ref, gate_ref, up_ref, dxo_ref, gt_ref, ut_ref, dn_ref,
             dx_ref, dgate_ref, dup_ref, h_ref, dy_ref, dg_ref):
        x = x_ref[...]
        g = g_ref[...]
        r, xh, hn = _rms(x, g)
        h_ref[...] = hn.astype(bf16)
        dxo = dxo_ref[...]
        dy = (0.5 * dxo).astype(bf16)
        dy_ref[...] = dy
        da = _dot_nt(dy, dn_ref[...])
        gate = gate_ref[...].astype(f32)
        up = up_ref[...].astype(f32)
        sg = jax.nn.sigmoid(gate)
        dgate = (da * up * (sg * (1.0 + gate * (1.0 - sg)))).astype(bf16)
        dup = (da * (gate * sg)).astype(bf16)
        dgate_ref[...] = dgate
        dup_ref[...] = dup
        dh = _dot_nn(dgate, gt_ref[...]) + _dot_nn(dup, ut_ref[...])

        @pl.when(pl.program_id(0) == 0)
        def _():
            dg_ref[...] = jnp.zeros_like(dg_ref)

        dg_ref[...] += jnp.sum(dh * xh, axis=0, keepdims=True)
        dx_ref[...] = dxo + _rms_bwd(dh, r, xh, g)

    return pl.pallas_call(
        body, grid=(S // TM,),
        in_specs=[_tile(D), _layer(1, D, l), _tile(F), _tile(F), _tile(D),
                  _layer(F, D, l), _layer(F, D, l), _layer(F, D, l)],
        out_specs=[_tile(D), _tile(F), _tile(F), _tile(D), _tile(D), _const((1, D))],
        out_shape=[SDS((S, D), f32), SDS((S, F), bf16), SDS((S, F), bf16), SDS((S, D), bf16),
                   SDS((S, D), bf16), SDS((1, D), f32)],
        compiler_params=_CP, name="ffn_bwd_d")(x, g, gate, up, dxo, gt, ut, dn)


def _ffn_bwd_w(h, dy, gate, up, dgate, dup):
    fc = 256

    def body(h_ref, dy_ref, gate_ref, up_ref, dgate_ref, dup_ref, dgt_ref, dut_ref, ddn_ref):
        gate = gate_ref[...].astype(f32)
        a = (gate * jax.nn.sigmoid(gate) * up_ref[...].astype(f32)).astype(bf16)
        ddn_ref[...] = _dot_tn(a, dy_ref[...]).astype(bf16)
        h = h_ref[...]
        dgt_ref[...] = _dot_tn(dgate_ref[...], h).astype(bf16)
        dut_ref[...] = _dot_tn(dup_ref[...], h).astype(bf16)

    col = pl.BlockSpec((S, fc), lambda j: (0, j))
    row = pl.BlockSpec((fc, D), lambda j: (j, 0))
    full = pl.BlockSpec((S, D), lambda j: (0, 0))
    return pl.pallas_call(
        body, grid=(F // fc,),
        in_specs=[full, full, col, col, col, col],
        out_specs=[row, row, row],
        out_shape=[SDS((F, D), bf16)] * 3,
        compiler_params=_CP, name="ffn_bwd_w")(h, dy, gate, up, dgate, dup)


def _wgrad(a, b):
    m, n = a.shape[1], b.shape[1]
    mc = 256

    def body(a_ref, b_ref, o_ref):
        o_ref[...] = _dot_tn(a_ref[...], b_ref[...]).astype(bf16)

    return pl.pallas_call(
        body, grid=(m // mc,),
        in_specs=[pl.BlockSpec((S, mc), lambda j: (0, j)), pl.BlockSpec((S, n), lambda j: (0, 0))],
        out_specs=pl.BlockSpec((mc, n), lambda j: (j, 0)),
        out_shape=SDS((m, n), bf16),
        compiler_params=_CP, name="wgrad")(a, b)


def _rope(t, c, sn, sp):
    return t * c + pltpu.roll(t, 120, 1) * sn + pltpu.roll(t, 8, 1) * sp


def _rope_bwd(d, c, sn, sp):
    return d * c + pltpu.roll(d * sn, 8, 1) + pltpu.roll(d * sp, 120, 1)


def _rope_tables(positions):
    inv_freq = ROPE_THETA ** (-jnp.arange(0, 16, 2, dtype=f32) / 16)
    ang = positions.reshape(S, 1).astype(f32) * inv_freq
    cos, sin = jnp.cos(ang), jnp.sin(ang)
    one = jnp.ones((S, 48), f32)
    zero8 = jnp.zeros((S, 8), f32)
    zero48 = jnp.zeros((S, 48), f32)
    c = jnp.concatenate([cos, cos, one], axis=1)
    sn = jnp.concatenate([-sin, zero8, zero48], axis=1)
    sp = jnp.concatenate([zero8, sin, zero48], axis=1)
    return tuple(jnp.concatenate([t, t], axis=1) for t in (c, sn, sp))


def _mix_in_fwd(x, g, wint, tabs, l=None):
    def body(x_ref, g_ref, w_ref, c_ref, sn_ref, sp_ref,
             h_ref, vp_ref, q1, k1, v1, q4, k4, v4, q16, k16, v16, scr):
        _, _, hn = _rms(x_ref[...], g_ref[...])
        h = hn.astype(bf16)
        h_ref[...] = h
        proj = _dot_nt(h, w_ref[...])
        vp_ref[...] = proj[:, :PW]
        c, sn, sp = c_ref[...], sn_ref[...], sp_ref[...]
        for kind, (o1, o4, o16) in enumerate(((q1, q4, q16), (k1, k4, k16), (v1, v4, v16))):
            for j in range(NG):
                t = proj[:, PW + kind * AW + 128 * j: PW + kind * AW + 128 * (j + 1)]
                if kind == 0:
                    t = _rope(t, c, sn, sp) * 0.125
                elif kind == 1:
                    t = _rope(t, c, sn, sp)
                scr[j] = t
                o1[:, _cols(j)] = t.astype(bf16)
            for r in range(4):
                for j in range(NG):
                    o4[r, :, _cols(j)] = scr[j, pl.ds(r, TM // 4, stride=4), :].astype(bf16)
            for r in range(16):
                for j in range(NG):
                    o16[r, :, _cols(j)] = scr[j, pl.ds(r, TM // 16, stride=16), :].astype(bf16)

    nat, d4, d16 = SDS((S, AW), bf16), SDS((4, S // 4, AW), bf16), SDS((16, S // 16, AW), bf16)
    return pl.pallas_call(
        body, grid=(S // TM,),
        in_specs=[_tile(D), _layer(1, D, l), _layer(PROJ, D, l), _tile(128), _tile(128), _tile(128)],
        out_specs=[_tile(D), _tile(PW)] + [_tile(AW)] * 3 + [_p4()] * 3 + [_p16()] * 3,
        out_shape=[SDS((S, D), bf16), SDS((S, PW), f32)] + [nat] * 3 + [d4] * 3 + [d16] * 3,
        scratch_shapes=[pltpu.VMEM((NG, TM, 128), f32)],
        compiler_params=_CP, name="mix_in_fwd")(x, g, wint, *tabs)


def _mix_in_bwd(dxo, x, g, wint, tabs, dvp, d1, d4, d16, l=None):
    def body(dxo_ref, x_ref, g_ref, w_ref, c_ref, sn_ref, sp_ref, dvp_ref,
             dq1, dk1, dv1, dq4, dk4, dv4, dq16, dk16, dv16,
             dx_ref, dproj_ref, dg_ref, s4, s16):
        c, sn, sp = c_ref[...], sn_ref[...], sp_ref[...]
        dproj_ref[:, :PW] = dvp_ref[...].astype(bf16)
        for kind, (a1, a4, a16) in enumerate(((dq1, dq4, dq16), (dk1, dk4, dk16), (dv1, dv4, dv16))):
            for r in range(4):
                for j in range(NG):
                    s4[j, pl.ds(r, TM // 4, stride=4), :] = a4[r, :, _cols(j)]
            for r in range(16):
                for j in range(NG):
                    s16[j, pl.ds(r, TM // 16, stride=16), :] = a16[r, :, _cols(j)]
            for j in range(NG):
                t = a1[:, _cols(j)] + s4[j] + s16[j]
                if kind == 0:
                    t = _rope_bwd(t * 0.125, c, sn, sp)
                elif kind == 1:
                    t = _rope_bwd(t, c, sn, sp)
                dproj_ref[:, PW + kind * AW + 128 * j: PW + kind * AW + 128 * (j + 1)] = t.astype(bf16)
        g = g_ref[...]
        r_, xh, _ = _rms(x_ref[...], g)
        dh = _dot_nn(dproj_ref[...], w_ref[...])

        @pl.when(pl.program_id(0) == 0)
        def _():
            dg_ref[...] = jnp.zeros_like(dg_ref)

        dg_ref[...] += jnp.sum(dh * xh, axis=0, keepdims=True)
        dx_ref[...] = dxo_ref[...] + _rms_bwd(dh, r_, xh, g)

    return pl.pallas_call(
        body, grid=(S // TM,),
        in_specs=[_tile(D), _tile(D), _layer(1, D, l), _layer(PROJ, D, l), _tile(128), _tile(128), _tile(128),
                  _tile(PW)] + [_tile(AW)] * 3 + [_p4()] * 3 + [_p16()] * 3,
        out_specs=[_tile(D), _tile(PROJ), _const((1, D))],
        out_shape=[SDS((S, D), f32), SDS((S, PROJ), bf16), SDS((1, D), f32)],
        scratch_shapes=[pltpu.VMEM((NG, TM, 128), f32), pltpu.VMEM((NG, TM, 128), f32)],
        compiler_params=_CP, name="mix_in_bwd")(dxo, x, g, wint, *tabs, dvp, *d1, *d4, *d16)


def _pool_sums(pad_ref, base, rows, adjoint):
    lane_group = lax.broadcasted_iota(jnp.int32, (rows, PW), 1) // 64
    sign = -1 if adjoint else 1

    def sh(o):
        return pad_ref[pl.ds(PAD + base + sign * o, rows), :]

    out = None
    acc = None
    lo, hi = 0, 0
    for gi, w in enumerate(POOL_WINDOWS):
        for o in list(range(-(w // 2), lo)) + list(range(hi, w - w // 2)):
            acc = sh(o) if acc is None else acc + sh(o)
        lo, hi = -(w // 2), w - w // 2
        out = acc if out is None else jnp.where(lane_group >= gi, acc, out)
    return out


def _pool_counts(base, rows):
    pos = base + lax.broadcasted_iota(jnp.int32, (rows, PW), 0)
    lane_group = lax.broadcasted_iota(jnp.int32, (rows, PW), 1) // 64
    cnt = None
    for gi, w in enumerate(POOL_WINDOWS):
        lo = jnp.maximum(pos - w // 2, 0)
        hi = jnp.minimum(pos + w - 1 - w // 2, S - 1)
        c = (hi - lo + 1).astype(f32)
        cnt = c if cnt is None else jnp.where(lane_group >= gi, c, cnt)
    return cnt


def _pool_fwd(vp, wbd, scale, l=None):
    ch = 256

    def body(vp_ref, w_ref, sc_ref, y_ref, diff_ref, pad):
        pad[pl.ds(0, PAD), :] = jnp.zeros((PAD, PW), f32)
        pad[pl.ds(PAD + S, PAD), :] = jnp.zeros((PAD, PW), f32)
        pad[pl.ds(PAD, S), :] = vp_ref[...]
        for b in range(S // ch):
            base = b * ch
            pooled = _pool_sums(pad, base, ch, False) / _pool_counts(base, ch)
            diff = (pooled - vp_ref[pl.ds(base, ch), :]).astype(bf16)
            diff_ref[pl.ds(base, ch), :] = diff
            y_ref[pl.ds(base, ch), :] = _dot_nn(diff, w_ref[...]) * sc_ref[...]

    whole = lambda shape: pl.BlockSpec(shape, lambda i: (0,) * len(shape))
    return pl.pallas_call(
        body, grid=(1,),
        in_specs=[whole((S, PW)), whole((PW, PW)), whole((1, PW))],
        out_specs=[whole((S, PW)), whole((S, PW))],
        out_shape=[SDS((S, PW), f32), SDS((S, PW), bf16)],
        scratch_shapes=[pltpu.VMEM((S + 2 * PAD, PW), f32)],
        compiler_params=_CP, name="pool_fwd")(vp, wbd, scale)


def _pool_bwd(dy, diff, wbd, scale, l=None):
    ch = 256

    def body(dy_ref, diff_ref, w_ref, sc_ref, dvp_ref, dw_ref, dsc_ref, pad):
        pad[pl.ds(0, PAD), :] = jnp.zeros((PAD, PW), f32)
        pad[pl.ds(PAD + S, PAD), :] = jnp.zeros((PAD, PW), f32)
        dw = jnp.zeros((PW, PW), f32)
        dsc = jnp.zeros((1, PW), f32)
        for b in range(S // ch):
            base = b * ch
            dy = dy_ref[pl.ds(base, ch), :]
            diff = diff_ref[pl.ds(base, ch), :]
            dsc = dsc + jnp.sum(dy * _dot_nn(diff, w_ref[...]), axis=0, keepdims=True)
            dz = (dy * sc_ref[...]).astype(bf16)
            dw = dw + _dot_tn(diff, dz)
            ddiff = _dot_nt(dz, w_ref[...])
            dvp_ref[pl.ds(base, ch), :] = -ddiff
            pad[pl.ds(PAD + base, ch), :] = ddiff / _pool_counts(base, ch)
        dw_ref[...] = dw
        dsc_ref[...] = dsc
        for b in range(S // ch):
            base = b * ch
            dvp_ref[pl.ds(base, ch), :] += _pool_sums(pad, base, ch, True)

    whole = lambda shape: pl.BlockSpec(shape, lambda i: (0,) * len(shape))
    return pl.pallas_call(
        body, grid=(1,),
        in_specs=[whole((S, PW)), whole((S, PW)), whole((PW, PW)), whole((1, PW))],
        out_specs=[whole((S, PW)), whole((PW, PW)), whole((1, PW))],
        out_shape=[SDS((S, PW), f32), SDS((PW, PW), f32), SDS((1, PW), f32)],
        scratch_shapes=[pltpu.VMEM((S + 2 * PAD, PW), f32)],
        compiler_params=_CP, name="pool_bwd")(dy, diff, wbd, scale)


def _attn_blocks(lc):
    bpc = lc // QB
    kw = min(2 * QB, lc)
    blocks = []
    for b in range(S // QB):
        t0 = (b % bpc) * QB
        ks_in = min(max(t0 - HALF, 0), lc - kw)
        blocks.append((b * QB, (b // bpc) * lc + ks_in, t0 - ks_in))
    return kw, blocks


def _attn_bias(bias_ref, kw, shifts):
    r = lax.broadcasted_iota(jnp.int32, (2 * QB, kw), 0) % QB
    c = lax.broadcasted_iota(jnp.int32, (2 * QB, kw), 1)
    for i, shift in enumerate(shifts):
        bias_ref[i] = jnp.where(jnp.abs(r + shift - c) <= HALF, 0.0, MASK_VALUE).astype(f32)


def _head_put(stats, pair, v0, v1, lane):
    return jnp.where(lane == 2 * pair, v0, jnp.where(lane == 2 * pair + 1, v1, stats))


def _head_cols(stats, pair, lane):
    c0 = jnp.sum(jnp.where(lane == 2 * pair, stats, 0.0), axis=-1, keepdims=True)
    c1 = jnp.sum(jnp.where(lane == 2 * pair + 1, stats, 0.0), axis=-1, keepdims=True)
    return jnp.concatenate([c0, c1], axis=0)


def _head_spread(stats, pair, head0):
    return jnp.where(head0, stats[:, 2 * pair:2 * pair + 1], stats[:, 2 * pair + 1:2 * pair + 2])


def _stack_heads(blk, head0):
    zero = jnp.zeros_like(blk)
    return jnp.concatenate([jnp.where(head0, blk, zero), jnp.where(head0, zero, blk)], axis=0)


def _attn_fwd(q, k, v, lc, after=None):
    kw, blocks = _attn_blocks(lc)
    shifts = sorted({b[2] for b in blocks})

    def body(q_ref, k_ref, v_ref, *refs):
        o_ref, lse_ref, bias_ref = refs[-3:]
        lane = lax.broadcasted_iota(jnp.int32, (QB, 128), 1)
        head0 = lane < 64
        pair = pl.program_id(0)
        _attn_bias(bias_ref, kw, shifts)

        @pl.when(pair == 0)
        def _():
            lse_ref[...] = jnp.zeros_like(lse_ref)

        for row0, kstart, shift in blocks:
            q2 = _stack_heads(q_ref[pl.ds(row0, QB), :], head0)
            kb = k_ref[pl.ds(kstart, kw), :]
            vb = v_ref[pl.ds(kstart, kw), :]
            s = _dot_nt(q2, kb) + bias_ref[shifts.index(shift)]
            m = jnp.max(s, axis=-1, keepdims=True)
            p = jnp.exp(s - m)
            den = jnp.sum(p, axis=-1, keepdims=True)
            o2 = _dot_nn(p.astype(bf16), vb) / den
            lse2 = m + jnp.log(den)
            o_ref[pl.ds(row0, QB), :] = jnp.where(head0, o2[:QB], o2[QB:])
            lse_ref[pl.ds(row0, QB), :] = _head_put(lse_ref[pl.ds(row0, QB), :], pair, lse2[:QB], lse2[QB:], lane)

    col = pl.BlockSpec((S, 128), lambda p: (0, p))
    extra = () if after is None else (after,)
    return pl.pallas_call(
        body, grid=(NG,), in_specs=[col, col, col] + [_ANY] * len(extra),
        out_specs=[col, pl.BlockSpec((S, 128), lambda p: (0, 0))],
        out_shape=[SDS((S, AW), f32), SDS((S, 128), f32)],
        scratch_shapes=[pltpu.VMEM((len(shifts), 2 * QB, kw), f32)],
        compiler_params=_CP, name=f"attn_fwd_{lc}")(q, k, v, *extra)


def _attn_bwd(q, k, v, do, lse, delta, lc):
    kw, blocks = _attn_blocks(lc)
    shifts = sorted({b[2] for b in blocks})

    def body(q_ref, k_ref, v_ref, do_ref, lse_ref, dl_ref, dq_ref, dk_ref, dv_ref, bias_ref):
        lane = lax.broadcasted_iota(jnp.int32, (QB, 128), 1)
        head0 = lane < 64
        pair = pl.program_id(0)
        _attn_bias(bias_ref, kw, shifts)
        dk_ref[...] = jnp.zeros_like(dk_ref)
        dv_ref[...] = jnp.zeros_like(dv_ref)
        for row0, kstart, shift in blocks:
            q2 = _stack_heads(q_ref[pl.ds(row0, QB), :], head0)
            do2 = _stack_heads(do_ref[pl.ds(row0, QB), :], head0)
            lse2 = _head_cols(lse_ref[pl.ds(row0, QB), :], pair, lane)
            dl2 = _head_cols(dl_ref[pl.ds(row0, QB), :], pair, lane)
            kb = k_ref[pl.ds(kstart, kw), :]
            vb = v_ref[pl.ds(kstart, kw), :]
            p = jnp.exp(_dot_nt(q2, kb) + bias_ref[shifts.index(shift)] - lse2)
            ds = (p * (_dot_nt(do2, vb) - dl2)).astype(bf16)
            dq2 = _dot_nn(ds, kb)
            dq_ref[pl.ds(row0, QB), :] = jnp.where(head0, dq2[:QB], dq2[QB:])
            dk_ref[pl.ds(kstart, kw), :] += _dot_tn(ds, q2)
            dv_ref[pl.ds(kstart, kw), :] += _dot_tn(p.astype(bf16), do2)

    col = pl.BlockSpec((S, 128), lambda p: (0, p))
    stats = pl.BlockSpec((S, 128), lambda p: (0, 0))
    return pl.pallas_call(
        body, grid=(NG,), in_specs=[col] * 4 + [stats] * 2, out_specs=[col] * 3,
        out_shape=[SDS((S, AW), f32)] * 3,
        scratch_shapes=[pltpu.VMEM((len(shifts), 2 * QB, kw), f32)],
        compiler_params=_CP, name=f"attn_bwd_{lc}")(q, k, v, do, lse, delta)


def _mix_out_fwd(x, ypool, o1, l1, o4, l4, o16, l16, wout, l=None):
    def body(x_ref, yp_ref, o1_ref, l1_ref, o4_ref, l4_ref, o16_ref, l16_ref, w_ref,
             xo_ref, mixed_ref, o_ref, lse1_ref, lse4_ref, lse16_ref, so4, so16, sl4, sl16, sl):
        head0 = lax.broadcasted_iota(jnp.int32, (TM, 128), 1) < 64
        for r in range(4):
            sl4[pl.ds(r, TM // 4, stride=4), :] = l4_ref[r]
            for j in range(NG):
                so4[j, pl.ds(r, TM // 4, stride=4), :] = o4_ref[r, :, _cols(j)]
        for r in range(16):
            sl16[pl.ds(r, TM // 16, stride=16), :] = l16_ref[r]
            for j in range(NG):
                so16[j, pl.ds(r, TM // 16, stride=16), :] = o16_ref[r, :, _cols(j)]
        a, b, c = l1_ref[...], sl4[...], sl16[...]
        m = jnp.maximum(jnp.maximum(a, b), c)
        wa, wb, wc = jnp.exp(a - m), jnp.exp(b - m), jnp.exp(c - m)
        den = wa + wb + wc
        wa, wb, wc = wa / den, wb / den, wc / den
        lse = m + jnp.log(den)
        lse1_ref[...] = lse
        sl[...] = lse
        mixed_ref[:, :PW] = yp_ref[...].astype(bf16)
        for j in range(NG):
            y = (_head_spread(wa, j, head0) * o1_ref[:, _cols(j)] + _head_spread(wb, j, head0) * so4[j]
                 + _head_spread(wc, j, head0) * so16[j])
            o_ref[:, _cols(j)] = y
            mixed_ref[:, PW + 128 * j: PW + 128 * (j + 1)] = y.astype(bf16)
        for r in range(4):
            lse4_ref[r] = sl[pl.ds(r, TM // 4, stride=4), :]
        for r in range(16):
            lse16_ref[r] = sl[pl.ds(r, TM // 16, stride=16), :]
        xo_ref[...] = x_ref[...] + _dot_nn(mixed_ref[...], w_ref[...])

    wide, narrow = pltpu.VMEM((NG, TM, 128), f32), pltpu.VMEM((TM, 128), f32)
    return pl.pallas_call(
        body, grid=(S // TM,),
        in_specs=[_tile(D), _tile(PW), _tile(AW), _tile(128), _p4(), _p4(128), _p16(), _p16(128), _layer(D, D, l)],
        out_specs=[_tile(D), _tile(D), _tile(AW), _tile(128), _p4(128), _p16(128)],
        out_shape=[SDS((S, D), f32), SDS((S, D), bf16), SDS((S, AW), f32), SDS((S, 128), f32),
                   SDS((4, S // 4, 128), f32), SDS((16, S // 16, 128), f32)],
        scratch_shapes=[wide, wide, narrow, narrow, narrow],
        compiler_params=_CP, name="mix_out_fwd")(x, ypool, o1, l1, o4, l4, o16, l16, wout)


def _mix_out_bwd(dxo, o, wout, l=None):
    def body(dxo_ref, o_ref, w_ref, dxb_ref, dyp_ref, do1, do4, do16, dl1, dl4, dl16, sdo, sdl):
        dxb = dxo_ref[...].astype(bf16)
        dxb_ref[...] = dxb
        dm = _dot_nt(dxb, w_ref[...])
        dyp_ref[...] = dm[:, :PW]
        lane = lax.broadcasted_iota(jnp.int32, (TM, 128), 1)
        head0 = lane < 64
        dl = jnp.zeros((TM, 128), f32)
        for j in range(NG):
            d = dm[:, PW + 128 * j: PW + 128 * (j + 1)]
            prod = d * o_ref[:, _cols(j)]
            dl = _head_put(dl, j, jnp.sum(jnp.where(head0, prod, 0.0), axis=-1, keepdims=True),
                           jnp.sum(jnp.where(head0, 0.0, prod), axis=-1, keepdims=True), lane)
            do1[:, _cols(j)] = d.astype(bf16)
            sdo[j] = d
        dl1[...] = dl
        sdl[...] = dl
        for r in range(4):
            dl4[r] = sdl[pl.ds(r, TM // 4, stride=4), :]
            for j in range(NG):
                do4[r, :, _cols(j)] = sdo[j, pl.ds(r, TM // 4, stride=4), :].astype(bf16)
        for r in range(16):
            dl16[r] = sdl[pl.ds(r, TM // 16, stride=16), :]
            for j in range(NG):
                do16[r, :, _cols(j)] = sdo[j, pl.ds(r, TM // 16, stride=16), :].astype(bf16)

    return pl.pallas_call(
        body, grid=(S // TM,),
        in_specs=[_tile(D), _tile(AW), _layer(D, D, l)],
        out_specs=[_tile(D), _tile(PW), _tile(AW), _p4(), _p16(), _tile(128), _p4(128), _p16(128)],
        out_shape=[SDS((S, D), bf16), SDS((S, PW), f32),
                   SDS((S, AW), bf16), SDS((4, S // 4, AW), bf16), SDS((16, S // 16, AW), bf16),
                   SDS((S, 128), f32), SDS((4, S // 4, 128), f32), SDS((16, S // 16, 128), f32)],
        scratch_shapes=[pltpu.VMEM((NG, TM, 128), f32), pltpu.VMEM((TM, 128), f32)],
        compiler_params=_CP, name="mix_out_bwd")(dxo, o, wout)


def _loss_head(x, g, target):
    def body(x_ref, g_ref, t_ref, dx_ref, loss_ref, dg_ref):
        g = g_ref[...]
        r, xh, y = _rms(x_ref[...], g)
        err = y - t_ref[...]
        dy = err * (1.0 / D)

        @pl.when(pl.program_id(0) == 0)
        def _():
            loss_ref[...] = jnp.zeros_like(loss_ref)
            dg_ref[...] = jnp.zeros_like(dg_ref)

        loss_ref[...] += jnp.broadcast_to(0.5 * jnp.sum(jnp.mean(err * err, axis=-1, keepdims=True)), (1, D))
        dg_ref[...] += jnp.sum(dy * xh, axis=0, keepdims=True)
        dx_ref[...] = _rms_bwd(dy, r, xh, g)

    return pl.pallas_call(
        body, grid=(S // TM,),
        in_specs=[_tile(D), _const((1, D)), _tile(D)],
        out_specs=[_tile(D), _const((1, D)), _const((1, D))],
        out_shape=[SDS((S, D), f32), SDS((1, D), f32), SDS((1, D), f32)],
        compiler_params=_CP, name="loss_head")(x, g, target)


def _peer(k):
    x, y, c = lax.axis_index("x"), lax.axis_index("y"), lax.axis_index("c")
    px = 1 - x if k & 4 else x
    py = 1 - y if k & 2 else y
    pc = 1 - c if k & 1 else c
    return (px, py, pc), 4 * px + 2 * py + pc


def _diag_route():
    x, y, c = lax.axis_index("x"), lax.axis_index("y"), lax.axis_index("c")
    idx_x, idx_y = _peer(4)[1], _peer(2)[1]
    return idx_x + c * (idx_y - idx_x), (x + c * (1 - 2 * x), (1 - y) + c * (2 * y - 1), c)


def _all_gather(lands):
    n = len(lands)

    def body(*refs):
        zones, send_sems, recv_sems = refs[n:2 * n], refs[2 * n], refs[2 * n + 1]
        me, me_idx = _peer(0)
        sibling, sib_idx = _peer(1)
        (x_nbr, idx_x), (y_nbr, idx_y), idx_d = _peer(4), _peer(2), _peer(6)[1]
        fwd_idx, fwd_dev = _diag_route()

        def copy(k, t, idx, to):
            return _row_copy(zones[t], idx, send_sems.at[k, t], recv_sems.at[k, t], to)

        sent = []

        def send(k, t, idx, to):
            cp = copy(k, t, idx, to)
            cp.start()
            sent.append(cp)

        for t in range(n):
            send(0, t, me_idx, sibling)
            send(1, t, me_idx, x_nbr)
            send(2, t, me_idx, y_nbr)
        for t in range(n):
            copy(1, t, idx_x, me).wait_recv()
            send(3, t, idx_x, sibling)
        for t in range(n):
            copy(2, t, idx_y, me).wait_recv()
            send(4, t, idx_y, sibling)
        for t in range(n):
            send(5, t, fwd_idx, fwd_dev)
        for t in range(n):
            copy(5, t, idx_d, me).wait_recv()
            send(6, t, idx_d, sibling)
        for k, mask in ((0, 1), (3, 5), (4, 3), (6, 7)):
            for t in range(n):
                copy(k, t, _peer(mask)[1], me).wait_recv()
        for cp in sent:
            cp.wait_send()

    return pl.pallas_call(
        body, in_specs=[_ANY] * n, out_specs=[_ANY] * n,
        out_shape=[SDS(a.shape, a.dtype) for a in lands], input_output_aliases={t: t for t in range(n)},
        scratch_shapes=[pltpu.SemaphoreType.DMA((7, n)), pltpu.SemaphoreType.DMA((7, n))],
        name="all_gather_weights")(*lands)


def _hbm(a):
    return pltpu.with_memory_space_constraint(a, pltpu.HBM)


def _rows(ref, idx):
    r = ref.shape[0] // NDEV
    return ref.at[pl.ds(idx * r, r), :]


def _row_copy(ref, idx, send_sem, recv_sem, to):
    return pltpu.make_async_remote_copy(src_ref=_rows(ref, idx), dst_ref=_rows(ref, idx), send_sem=send_sem,
                                        recv_sem=recv_sem, device_id=to, device_id_type=_MESH)


def _place_own(me, shards, l):
    n = len(shards)

    def body(me_ref, *refs):
        for t in range(n):
            refs[n + t][...] = refs[t][...]

    grid_spec = pltpu.PrefetchScalarGridSpec(
        num_scalar_prefetch=1, grid=(1,),
        in_specs=[pl.BlockSpec((None, s.shape[1], D), lambda i, me_ref: (l, 0, 0)) for s in shards],
        out_specs=[pl.BlockSpec((s.shape[1], D), lambda i, me_ref: (me_ref[0], 0)) for s in shards])
    return pl.pallas_call(
        body, grid_spec=grid_spec, out_shape=[SDS((NDEV * s.shape[1], D), s.dtype) for s in shards],
        compiler_params=_CP, name="place_own")(me, *shards)


_TOKEN = SDS((8, 128), f32)
def _ag_start(lands, after, l):
    n = len(lands)
    after = list(after) if isinstance(after, (list, tuple)) else [after]

    def body(*refs):
        zones, send_sems, recv_sems, token = refs[:n], refs[n + len(after)], refs[n + len(after) + 1], refs[-1]
        _, me_idx = _peer(0)
        for k, mask in enumerate((1, 4, 2)):
            for t in range(n):
                _row_copy(zones[t], me_idx, send_sems.at[k * n + t], recv_sems.at[k * n + t], _peer(mask)[0]).start()
        token[...] = jnp.zeros_like(token)

    outs = pl.pallas_call(
        body, name=f"ag_start_{l}", in_specs=[_HBM] * n + [_ANY] * len(after),
        out_specs=(_SEM, _SEM, *[_HBM] * n, pl.BlockSpec(memory_space=pltpu.VMEM)),
        out_shape=(pltpu.SemaphoreType.DMA((3 * n,)), pltpu.SemaphoreType.DMA((3 * n,)),
                   *[pltpu.HBM(a.shape, a.dtype) for a in lands], _TOKEN),
        input_output_aliases={t: 2 + t for t in range(n)}, compiler_params=_CP_SPLIT)(
            *[_hbm(a) for a in lands], *after)
    return outs[0], outs[1], list(outs[2:2 + n]), outs[-1]


def _ag_pass(lands, recv_sems, after, l):
    n = len(lands)
    after = list(after) if isinstance(after, (list, tuple)) else [after]

    def body(*refs):
        zones, recv_sems = refs[:n], refs[n]
        psend, precv, token = refs[n + 1 + len(after)], refs[n + 2 + len(after)], refs[-1]
        me, _ = _peer(0)
        sibling, _ = _peer(1)
        for j, mask in enumerate((4, 2)):
            idx = _peer(mask)[1]
            for t in range(n):
                _row_copy(zones[t], idx, psend.at[j * n + t], recv_sems.at[(1 + j) * n + t], me).wait_recv()
                _row_copy(zones[t], idx, psend.at[j * n + t], precv.at[j * n + t], sibling).start()
        fwd_idx, fwd_dev = _diag_route()
        for t in range(n):
            _row_copy(zones[t], fwd_idx, psend.at[2 * n + t], precv.at[2 * n + t], fwd_dev).start()
        token[...] = jnp.zeros_like(token)

    outs = pl.pallas_call(
        body, name=f"ag_pass_{l}", in_specs=[_HBM] * n + [_SEM] + [_ANY] * len(after),
        out_specs=(_SEM, _SEM, *[_HBM] * n, pl.BlockSpec(memory_space=pltpu.VMEM)),
        out_shape=(pltpu.SemaphoreType.DMA((3 * n,)), pltpu.SemaphoreType.DMA((3 * n,)),
                   *[pltpu.HBM(a.shape, a.dtype) for a in lands], _TOKEN),
        input_output_aliases={t: 2 + t for t in range(n)}, compiler_params=_CP_SPLIT)(*lands, recv_sems, *after)
    return outs[0], outs[1], list(outs[2:2 + n]), outs[-1]


def _ag_last(lands, precv, after, l):
    n = len(lands)
    after = list(after) if isinstance(after, (list, tuple)) else [after]

    def body(*refs):
        zones, precv = refs[:n], refs[n]
        qsend, qrecv, token = refs[n + 1 + len(after)], refs[n + 2 + len(after)], refs[-1]
        me, _ = _peer(0)
        sibling, _ = _peer(1)
        idx = _peer(6)[1]
        for t in range(n):
            _row_copy(zones[t], idx, qsend.at[t], precv.at[2 * n + t], me).wait_recv()
            _row_copy(zones[t], idx, qsend.at[t], qrecv.at[t], sibling).start()
        token[...] = jnp.zeros_like(token)

    outs = pl.pallas_call(
        body, name=f"ag_last_{l}", in_specs=[_HBM] * n + [_SEM] + [_ANY] * len(after),
        out_specs=(_SEM, _SEM, *[_HBM] * n, pl.BlockSpec(memory_space=pltpu.VMEM)),
        out_shape=(pltpu.SemaphoreType.DMA((n,)), pltpu.SemaphoreType.DMA((n,)),
                   *[pltpu.HBM(a.shape, a.dtype) for a in lands], _TOKEN),
        input_output_aliases={t: 2 + t for t in range(n)}, compiler_params=_CP_SPLIT)(*lands, precv, *after)
    return outs[0], outs[1], list(outs[2:2 + n]), outs[-1]


def _ag_wait(lands, send_sems, recv_sems, psend, precv, qsend, qrecv, after, l):
    n = len(lands)
    after = list(after) if isinstance(after, (list, tuple)) else [after]

    def body(*refs):
        zones = refs[:n]
        send_sems, recv_sems, psend, precv, qsend, qrecv = refs[n:n + 6]
        me, me_idx = _peer(0)
        for k in range(3):
            for t in range(n):
                _row_copy(zones[t], me_idx, send_sems.at[k * n + t], recv_sems.at[k * n + t], me).wait_send()
        for t in range(n):
            _row_copy(zones[t], _peer(1)[1], send_sems.at[t], recv_sems.at[t], me).wait_recv()
        fwd_idx, _ = _diag_route()
        for j, (mine, theirs) in enumerate(((_peer(4)[1], _peer(5)[1]), (_peer(2)[1], _peer(3)[1]))):
            for t in range(n):
                _row_copy(zones[t], mine, psend.at[j * n + t], precv.at[j * n + t], me).wait_send()
                _row_copy(zones[t], theirs, psend.at[j * n + t], precv.at[j * n + t], me).wait_recv()
        for t in range(n):
            _row_copy(zones[t], fwd_idx, psend.at[2 * n + t], precv.at[2 * n + t], me).wait_send()
            _row_copy(zones[t], _peer(6)[1], qsend.at[t], qrecv.at[t], me).wait_send()
            _row_copy(zones[t], _peer(7)[1], qsend.at[t], qrecv.at[t], me).wait_recv()

    outs = pl.pallas_call(
        body, name=f"ag_wait_{l}", in_specs=[_HBM] * n + [_SEM] * 6 + [_ANY] * len(after),
        out_specs=tuple([_HBM] * n), out_shape=tuple(pltpu.HBM(a.shape, a.dtype) for a in lands),
        input_output_aliases={t: t for t in range(n)}, compiler_params=_CP_SPLIT)(
            *lands, send_sems, recv_sems, psend, precv, qsend, qrecv, *after)
    return list(outs)


def _xchg_src(ref, slot_ref, idx):
    return _rows(ref, idx) if ref.shape[0] == NDEV * slot_ref.shape[1] else ref


def _rs_start(srcs, slots, after, tag):
    n = len(srcs)
    after = list(after) if isinstance(after, (list, tuple)) else [after]

    def body(*refs):
        src, slot = refs[:n], refs[n:2 * n]
        send_sems, recv_sems, token = refs[2 * n + len(after)], refs[2 * n + len(after) + 1], refs[-1]
        _, me_idx = _peer(0)
        for k in range(1, NDEV):
            dev, idx = _peer(k)
            for t in range(n):
                pltpu.make_async_remote_copy(
                    src_ref=_xchg_src(src[t], slot[t], idx), dst_ref=slot[t].at[me_idx],
                    send_sem=send_sems.at[(k - 1) * n + t], recv_sem=recv_sems.at[(k - 1) * n + t],
                    device_id=dev, device_id_type=_MESH).start()
        token[...] = jnp.zeros_like(token)

    outs = pl.pallas_call(
        body, name=f"rs_start_{tag}", in_specs=[_HBM] * (2 * n) + [_ANY] * len(after),
        out_specs=(_SEM, _SEM, *[_HBM] * (2 * n), pl.BlockSpec(memory_space=pltpu.VMEM)),
        out_shape=(pltpu.SemaphoreType.DMA(((NDEV - 1) * n,)), pltpu.SemaphoreType.DMA(((NDEV - 1) * n,)),
                   *[pltpu.HBM(a.shape, a.dtype) for a in list(srcs) + list(slots)], _TOKEN),
        input_output_aliases={t: 2 + t for t in range(2 * n)}, compiler_params=_CP_SPLIT)(
            *[_hbm(a) for a in list(srcs) + list(slots)], *after)
    return outs[0], outs[1], list(outs[2:2 + n]), list(outs[2 + n:2 + 2 * n]), outs[-1]


def _rs_wait(srcs, slots, send_sems, recv_sems, after, tag):
    n = len(srcs)
    after = list(after) if isinstance(after, (list, tuple)) else [after]

    def body(*refs):
        src, slot, send_sems, recv_sems = refs[:n], refs[n:2 * n], refs[2 * n], refs[2 * n + 1]
        me, _ = _peer(0)
        for k in range(1, NDEV):
            idx = _peer(k)[1]
            for t in range(n):
                cp = pltpu.make_async_remote_copy(
                    src_ref=_xchg_src(src[t], slot[t], idx), dst_ref=slot[t].at[idx],
                    send_sem=send_sems.at[(k - 1) * n + t], recv_sem=recv_sems.at[(k - 1) * n + t],
                    device_id=me, device_id_type=_MESH)
                cp.wait_send()
                cp.wait_recv()

    outs = pl.pallas_call(
        body, name=f"rs_wait_{tag}", in_specs=[_HBM] * (2 * n) + [_SEM, _SEM] + [_ANY] * len(after),
        out_specs=tuple([_HBM] * (2 * n)),
        out_shape=tuple(pltpu.HBM(a.shape, a.dtype) for a in list(srcs) + list(slots)),
        input_output_aliases={t: t for t in range(2 * n)}, compiler_params=_CP_SPLIT)(
            *srcs, *slots, send_sems, recv_sems, *after)
    return list(outs[:n]), list(outs[n:])


def _pair_start(full4s, bufs, after, tag):
    n = len(full4s)
    after = list(after) if isinstance(after, (list, tuple)) else [after]

    def body(*refs):
        full, buf = refs[:n], refs[n:2 * n]
        send_sems, recv_sems, token = refs[2 * n + len(after)], refs[2 * n + len(after) + 1], refs[-1]
        c = lax.axis_index("c")
        for t in range(n):
            pltpu.make_async_remote_copy(src_ref=full[t].at[:, 1 - c], dst_ref=buf[t], send_sem=send_sems.at[t],
                                         recv_sem=recv_sems.at[t], device_id=_peer(1)[0], device_id_type=_MESH).start()
        token[...] = jnp.zeros_like(token)

    outs = pl.pallas_call(
        body, name=f"pair_start_{tag}", in_specs=[_HBM] * (2 * n) + [_ANY] * len(after),
        out_specs=(_SEM, _SEM, *[_HBM] * (2 * n), pl.BlockSpec(memory_space=pltpu.VMEM)),
        out_shape=(pltpu.SemaphoreType.DMA((n,)), pltpu.SemaphoreType.DMA((n,)),
                   *[pltpu.HBM(a.shape, a.dtype) for a in list(full4s) + list(bufs)], _TOKEN),
        input_output_aliases={t: 2 + t for t in range(2 * n)}, compiler_params=_CP_SPLIT)(
            *[_hbm(a) for a in list(full4s) + list(bufs)], *after)
    return outs[0], outs[1], list(outs[2:2 + n]), list(outs[2 + n:2 + 2 * n]), outs[-1]


def _pair_wait(full4s, bufs, send_sems, recv_sems, after, tag):
    n = len(full4s)
    after = list(after) if isinstance(after, (list, tuple)) else [after]

    def body(*refs):
        full, buf, send_sems, recv_sems = refs[:n], refs[n:2 * n], refs[2 * n], refs[2 * n + 1]
        c = lax.axis_index("c")
        for t in range(n):
            cp = pltpu.make_async_remote_copy(src_ref=full[t].at[:, 1 - c], dst_ref=buf[t], send_sem=send_sems.at[t],
                                              recv_sem=recv_sems.at[t], device_id=_peer(0)[0], device_id_type=_MESH)
            cp.wait_send()
            cp.wait_recv()

    outs = pl.pallas_call(
        body, name=f"pair_wait_{tag}", in_specs=[_HBM] * (2 * n) + [_SEM, _SEM] + [_ANY] * len(after),
        out_specs=tuple([_HBM] * (2 * n)),
        out_shape=tuple(pltpu.HBM(a.shape, a.dtype) for a in list(full4s) + list(bufs)),
        input_output_aliases={t: t for t in range(2 * n)}, compiler_params=_CP_SPLIT)(
            *full4s, *bufs, send_sems, recv_sems, *after)
    return list(outs[:n]), list(outs[n:])


def _pair_sum(core, full4s, bufs):
    n = len(full4s)

    def body(core_ref, *refs):
        for t in range(n):
            refs[2 * n + t][...] = (refs[t][...].astype(f32) + refs[n + t][...].astype(f32)).astype(bf16)

    grid_spec = pltpu.PrefetchScalarGridSpec(
        num_scalar_prefetch=1, grid=(4,),
        in_specs=[pl.BlockSpec((None, None) + a.shape[2:], lambda j, core_ref: (j, core_ref[0], 0, 0)) for a in full4s]
        + [pl.BlockSpec((None,) + b.shape[1:], lambda j, core_ref: (j, 0, 0)) for b in bufs],
        out_specs=[pl.BlockSpec((None,) + b.shape[1:], lambda j, core_ref: (j, 0, 0)) for b in bufs])
    return pl.pallas_call(
        body, grid_spec=grid_spec, out_shape=[SDS(b.shape, bf16) for b in bufs],
        compiler_params=_CP, name="pair_sum")(core, *full4s, *bufs)


def _chip_start(sums, slots, after, tag):
    n = len(sums)
    after = list(after) if isinstance(after, (list, tuple)) else [after]

    def body(*refs):
        src, slot = refs[:n], refs[n:2 * n]
        send_sems, recv_sems, token = refs[2 * n + len(after)], refs[2 * n + len(after) + 1], refs[-1]
        my_chip = 2 * lax.axis_index("x") + lax.axis_index("y")
        for k, mask in enumerate((4, 2, 6)):
            dev, _ = _peer(mask)
            for t in range(n):
                pltpu.make_async_remote_copy(
                    src_ref=src[t].at[2 * dev[0] + dev[1]], dst_ref=slot[t].at[my_chip],
                    send_sem=send_sems.at[k * n + t], recv_sem=recv_sems.at[k * n + t],
                    device_id=dev, device_id_type=_MESH).start()
        token[...] = jnp.zeros_like(token)

    outs = pl.pallas_call(
        body, name=f"chip_start_{tag}", in_specs=[_HBM] * (2 * n) + [_ANY] * len(after),
        out_specs=(_SEM, _SEM, *[_HBM] * (2 * n), pl.BlockSpec(memory_space=pltpu.VMEM)),
        out_shape=(pltpu.SemaphoreType.DMA((3 * n,)), pltpu.SemaphoreType.DMA((3 * n,)),
                   *[pltpu.HBM(a.shape, a.dtype) for a in list(sums) + list(slots)], _TOKEN),
        input_output_aliases={t: 2 + t for t in range(2 * n)}, compiler_params=_CP_SPLIT)(
            *[_hbm(a) for a in list(sums) + list(slots)], *after)
    return outs[0], outs[1], list(outs[2:2 + n]), list(outs[2 + n:2 + 2 * n]), outs[-1]


def _chip_wait(sums, slots, send_sems, recv_sems, after, tag):
    n = len(sums)
    after = list(after) if isinstance(after, (list, tuple)) else [after]

    def body(*refs):
        src, slot, send_sems, recv_sems = refs[:n], refs[n:2 * n], refs[2 * n], refs[2 * n + 1]
        for k, mask in enumerate((4, 2, 6)):
            dev, _ = _peer(mask)
            chip = 2 * dev[0] + dev[1]
            for t in range(n):
                cp = pltpu.make_async_remote_copy(
                    src_ref=src[t].at[chip], dst_ref=slot[t].at[chip],
                    send_sem=send_sems.at[k * n + t], recv_sem=recv_sems.at[k * n + t],
                    device_id=_peer(0)[0], device_id_type=_MESH)
                cp.wait_send()
                cp.wait_recv()

    outs = pl.pallas_call(
        body, name=f"chip_wait_{tag}", in_specs=[_HBM] * (2 * n) + [_SEM, _SEM] + [_ANY] * len(after),
        out_specs=tuple([_HBM] * (2 * n)),
        out_shape=tuple(pltpu.HBM(a.shape, a.dtype) for a in list(sums) + list(slots)),
        input_output_aliases={t: t for t in range(2 * n)}, compiler_params=_CP_SPLIT)(
            *sums, *slots, send_sems, recv_sems, *after)
    return list(outs[:n]), list(outs[n:])


def _sum_slots(slots, rb):
    r = slots.shape[1]

    def body(s_ref, o_ref):
        acc = s_ref[0].astype(f32)
        for s in range(1, NDEV):
            acc = acc + s_ref[s].astype(f32)
        o_ref[...] = acc

    return pl.pallas_call(
        body, grid=(r // rb,),
        in_specs=[pl.BlockSpec((NDEV, rb, D), lambda i: (0, i, 0))],
        out_specs=pl.BlockSpec((rb, D), lambda i: (i, 0)),
        out_shape=SDS((r, D), f32), compiler_params=_CP, name="sum_slots")(slots)


def _adamw(w, g, m, v):
    shape = w.shape
    cols = shape[-1]
    rows = w.size // cols
    rb = rows
    for cand in (512, 256, 128, 64, 32, 16, 8):
        if rows % cand == 0 and rows > cand:
            rb = cand
            break

    def body(w_ref, g_ref, m_ref, v_ref, d_ref, mo_ref, vo_ref):
        d_ref[...], mo_ref[...], vo_ref[...] = _adamw_math(w_ref[...], g_ref[...], m_ref[...], v_ref[...])

    spec = pl.BlockSpec((rb, cols), lambda i: (i, 0))
    outs = pl.pallas_call(
        body, grid=(rows // rb,), in_specs=[spec] * 4, out_specs=[spec] * 3,
        out_shape=[SDS((rows, cols), f32)] * 3, compiler_params=_CP, name="adamw")(
            *(a.reshape(rows, cols) for a in (w, g, m, v)))
    return tuple(o.reshape(shape) for o in outs)


def _adamw_math(w, g, m, v):
    m = ADAM_B1 * m + (1.0 - ADAM_B1) * g
    v = ADAM_B2 * v + (1.0 - ADAM_B2) * (g * g)
    m_hat = m / (1.0 - ADAM_B1 ** ADAM_STEP)
    v_hat = v / (1.0 - ADAM_B2 ** ADAM_STEP)
    return -ADAM_LR * (m_hat / (jnp.sqrt(v_hat) + ADAM_EPS) + ADAM_WD * w), m, v


def _reduce_adamw(acc, me, full, slots, w, m, v, l):
    _, r, _ = w.shape
    ns = slots.shape[0]
    rb = r // 2 if r > 128 else r

    def body(me_ref, full_ref, slots_ref, w_ref, m_ref, v_ref, *refs):
        go_ref, d_ref, mo_ref, vo_ref = refs[-4:]
        own = full_ref[...].astype(f32)
        g = None
        for s in range(ns):
            part = jnp.where(me_ref[0] == s, own, slots_ref[s].astype(f32))
            g = part if g is None else g + part
        go_ref[...] = g
        d_ref[...], mo_ref[...], vo_ref[...] = _adamw_math(w_ref[...], g, m_ref[...], v_ref[...])

    steps = r // rb
    lay = pl.BlockSpec((None, rb, D), lambda i, me_ref: (l, i, 0))
    n_acc = 0 if acc is None else 4
    grid_spec = pltpu.PrefetchScalarGridSpec(
        num_scalar_prefetch=1, grid=(steps,),
        in_specs=[pl.BlockSpec((rb, D), lambda i, me_ref: (me_ref[0] * steps + i, 0)),
                  pl.BlockSpec((ns, rb, D), lambda i, me_ref: (0, i, 0)), lay, lay, lay] + [_ANY] * n_acc,
        out_specs=[lay] * 4)
    outs = pl.pallas_call(
        body, grid_spec=grid_spec, out_shape=[SDS(w.shape, f32)] * 4,
        input_output_aliases={6 + j: j for j in range(n_acc)},
        compiler_params=_CP, name="reduce_adamw")(me, full, slots, w, m, v, *(() if acc is None else acc))
    return tuple(outs)


_BIG = ("ffn1_w_gate", "ffn1_w_up", "ffn1_w_down", "w_in", "w_out", "ffn2_w_gate", "ffn2_w_up", "ffn2_w_down")
_TRANSPOSED = ("ffn1_w_gate", "ffn1_w_up", "w_in", "ffn2_w_gate", "ffn2_w_up")

def _block_diag(pool_w):
    out = jnp.zeros((L, PW, PW), pool_w.dtype)
    for gi in range(4):
        out = out.at[:, 64 * gi:64 * (gi + 1), 64 * gi:64 * (gi + 1)].set(pool_w[:, gi])
    return out


def kernel(x, positions, ffn1_norm, ffn1_w_gate, ffn1_w_up, ffn1_w_down, mix_norm, w_in, pool_w, pool_scale, w_out, ffn2_norm, ffn2_w_gate, ffn2_w_up, ffn2_w_down, final_norm, loss_target, m_ffn1_norm, m_ffn1_w_gate, m_ffn1_w_up, m_ffn1_w_down, m_mix_norm, m_w_in, m_pool_w, m_pool_scale, m_w_out, m_ffn2_norm, m_ffn2_w_gate, m_ffn2_w_up, m_ffn2_w_down, m_final_norm, v_ffn1_norm, v_ffn1_w_gate, v_ffn1_w_up, v_ffn1_w_down, v_mix_norm, v_w_in, v_pool_w, v_pool_scale, v_w_out, v_ffn2_norm, v_ffn2_w_gate, v_ffn2_w_up, v_ffn2_w_down, v_final_norm):
    weights = dict(ffn1_norm=ffn1_norm, ffn1_w_gate=ffn1_w_gate, ffn1_w_up=ffn1_w_up, ffn1_w_down=ffn1_w_down,
                   mix_norm=mix_norm, w_in=w_in, pool_w=pool_w, pool_scale=pool_scale, w_out=w_out,
                   ffn2_norm=ffn2_norm, ffn2_w_gate=ffn2_w_gate, ffn2_w_up=ffn2_w_up, ffn2_w_down=ffn2_w_down,
                   final_norm=final_norm)
    moms = dict(ffn1_norm=m_ffn1_norm, ffn1_w_gate=m_ffn1_w_gate, ffn1_w_up=m_ffn1_w_up, ffn1_w_down=m_ffn1_w_down,
                mix_norm=m_mix_norm, w_in=m_w_in, pool_w=m_pool_w, pool_scale=m_pool_scale, w_out=m_w_out,
                ffn2_norm=m_ffn2_norm, ffn2_w_gate=m_ffn2_w_gate, ffn2_w_up=m_ffn2_w_up, ffn2_w_down=m_ffn2_w_down,
                final_norm=m_final_norm)
    vels = dict(ffn1_norm=v_ffn1_norm, ffn1_w_gate=v_ffn1_w_gate, ffn1_w_up=v_ffn1_w_up, ffn1_w_down=v_ffn1_w_down,
                mix_norm=v_mix_norm, w_in=v_w_in, pool_w=v_pool_w, pool_scale=v_pool_scale, w_out=v_w_out,
                ffn2_norm=v_ffn2_norm, ffn2_w_gate=v_ffn2_w_gate, ffn2_w_up=v_ffn2_w_up, ffn2_w_down=v_ffn2_w_down,
                final_norm=v_final_norm)
    names = list(weights)

    me_idx = 4 * lax.axis_index("x") + 2 * lax.axis_index("y") + lax.axis_index("c")
    me_arr = me_idx.reshape(1).astype(jnp.int32)

    tr = lambda w: jnp.swapaxes(w, 1, 2).astype(bf16)
    shards = [tr(weights[nm]) if nm in _TRANSPOSED else weights[nm].astype(bf16) for nm in _BIG]

    def landing_zones(l, which):
        return _place_own(me_arr, [shards[t] for t in which], l)

    g_ffn1 = [ffn1_norm[l].reshape(1, D) for l in range(L)]
    g_mix = [mix_norm[l].reshape(1, D) for l in range(L)]
    g_ffn2 = [ffn2_norm[l].reshape(1, D) for l in range(L)]
    wbd_all = _block_diag(pool_w).astype(bf16)
    wbd = [wbd_all[l] for l in range(L)]
    pscale = [pool_scale[l].reshape(1, PW) for l in range(L)]
    tabs = _rope_tables(positions)
    flat = lambda a: a.reshape(S, a.shape[-1])
    r4 = lambda a: a.reshape(4, S // 4, a.shape[-1])
    r16 = lambda a: a.reshape(16, S // 16, a.shape[-1])

    first, rest, whole = (0, 1, 2, 3), (4, 5, 6, 7), tuple(range(8))

    def ag_begin(l, which, after):
        tag = f"{l}{'' if which == whole else 'r'}"
        send_sems, recv_sems, zones, token = _ag_start(landing_zones(l, which), after, tag)
        return dict(tag=tag, zones=zones, s=send_sems, r=recv_sems), token

    def ag_second(ch, after):
        ch["ps"], ch["pr"], ch["zones"], token = _ag_pass(ch["zones"], ch["r"], after, ch["tag"])
        return token

    def ag_third(ch, after):
        ch["qs"], ch["qr"], ch["zones"], token = _ag_last(ch["zones"], ch["pr"], after, ch["tag"])
        return token

    def ag_end(ch, after):
        return _ag_wait(ch["zones"], ch["s"], ch["r"], ch["ps"], ch["pr"], ch["qs"], ch["qr"], after, ch["tag"])

    head = _all_gather(landing_zones(0, first))
    ch_rest, tok_rest = ag_begin(0, rest, head[0])
    chains = {}
    chains[1], tok_next = ag_begin(1, whole, head[0])
    gathered = [None] * L
    xs = x.reshape(S, D)
    saved = []
    for l in range(L):
        ga, gb = g_ffn1[l], g_ffn2[l]
        if l == 0:
            gt1, ut1, dn1, wint = head
            ga = ga + tok_rest[0, 0] + tok_next[0, 0]
        else:
            gt1, ut1, dn1, wint, wout, gt2, ut2, dn2 = gathered[l]
        x0 = xs
        x1, gate1, up1 = _ffn_fwd(x0, ga, gt1, ut1, dn1)
        hmix, vp, q1, k1, v1, q4, k4, v4, q16, k16, v16 = _mix_in_fwd(x1, g_mix[l], wint, tabs)
        q4, k4, v4, q16, k16, v16 = map(flat, (q4, k4, v4, q16, k16, v16))
        ypool, diff = _pool_fwd(vp, wbd[l], pscale[l])
        after_attn = None
        if l == 0:
            after_attn = ag_second(ch_rest, [ypool, q16])
        o1, l1 = _attn_fwd(q1, k1, v1, S, after=after_attn)
        o4, l4 = _attn_fwd(q4, k4, v4, S // 4, after=after_attn)
        o16, l16 = _attn_fwd(q16, k16, v16, S // 16, after=after_attn)
        if l == 0:
            token = ag_third(ch_rest, [o1, o4, o16])
            wout, gt2, ut2, dn2 = ag_end(ch_rest, token)
            gathered[0] = list(head) + [wout, gt2, ut2, dn2]
        elif l + 1 < L:
            gb = gb + ag_second(chains[l + 1], [o1, o4, o16])[0, 0]
        x2, mixed, o, lse1, lse4, lse16 = _mix_out_fwd(x1, ypool, o1, l1, r4(o4), r4(l4), r16(o16), r16(l16), wout)
        if l == 0:
            gb = gb + ag_second(chains[1], x2)[0, 0]
        x3, gate2, up2 = _ffn_fwd(x2, gb, gt2, ut2, dn2)
        if l + 1 < L:
            token = ag_third(chains[l + 1], x3)
            if l + 2 < L:
                chains[l + 2], token = ag_begin(l + 2, whole, token)
            gathered[l + 1] = ag_end(chains[l + 1], token)
        saved.append(dict(x0=x0, x1=x1, x2=x2, gate1=gate1, up1=up1, gate2=gate2, up2=up2, hmix=hmix, diff=diff,
                          qkv=((q1, k1, v1), (q4, k4, v4), (q16, k16, v16)), mixed=mixed, o=o,
                          lse=(lse1, flat(lse4), flat(lse16))))
        xs = x3

    dx, loss_part, d_final = _loss_head(xs, final_norm.reshape(1, D), loss_target.reshape(S, D))

    d_norm = {nm: [None] * L for nm in ("ffn1_norm", "mix_norm", "ffn2_norm")}
    d_poolw, d_pscale = [None] * L, [None] * L
    group_a = ("ffn2_w_gate", "ffn2_w_up", "ffn2_w_down", "w_out")
    group_b = ("ffn1_w_gate", "ffn1_w_up", "ffn1_w_down", "w_in")
    acc = {}

    as_rows = lambda a, nm: jnp.swapaxes(a, 1, 2) if nm in _TRANSPOSED else a
    w_rows = {nm: as_rows(weights[nm], nm) for nm in _BIG}
    m_rows = {nm: as_rows(moms[nm], nm) for nm in _BIG}
    v_rows = {nm: as_rows(vels[nm], nm) for nm in _BIG}

    def exchange(full, group, after, tag):
        srcs = [full[nm] for nm in group]
        slots = [lax.empty((NDEV, g.shape[0] // NDEV, D), bf16) for g in srcs]
        ssem, rsem, srcs, slots, token = _rs_start(srcs, slots, after, tag)
        return (srcs, slots, ssem, rsem, tag), token

    def update(l, group, flight, after):
        srcs, slots, ssem, rsem, tag = flight
        srcs, slots = _rs_wait(srcs, slots, ssem, rsem, after, tag)
        for nm, full_g, slots_g in zip(group, srcs, slots):
            acc[nm] = _reduce_adamw(acc.get(nm), me_arr, full_g, slots_g, w_rows[nm], m_rows[nm], v_rows[nm], l)
        return [acc[nm][0] for nm in group], slots

    core_arr = lax.axis_index("c").reshape(1).astype(jnp.int32)
    chip_arr = (2 * lax.axis_index("x") + lax.axis_index("y")).reshape(1).astype(jnp.int32)

    def exchange_cores(full, group, after, tag):
        full4s = [full[nm].reshape(4, 2, full[nm].shape[0] // NDEV, D) for nm in group]
        bufs = [lax.empty((4,) + a.shape[2:], bf16) for a in full4s]
        ssem, rsem, full4s, bufs, token = _pair_start(full4s, bufs, after, tag)
        return (full4s, bufs, ssem, rsem, tag), token

    def exchange_chips(flight, after):
        full4s, bufs, ssem, rsem, tag = flight
        full4s, bufs = _pair_wait(full4s, bufs, ssem, rsem, after, tag)
        sums = _pair_sum(core_arr, full4s, bufs)
        slots = [lax.empty(a.shape, bf16) for a in sums]
        ssem, rsem, sums, slots, token = _chip_start(sums, slots, sums[0], tag)
        return (sums, slots, ssem, rsem, tag), token

    def update_chips(l, group, flight, after):
        sums, slots, ssem, rsem, tag = flight
        sums, slots = _chip_wait(sums, slots, ssem, rsem, after, tag)
        for nm, sums_g, slots_g in zip(group, sums, slots):
            own = sums_g.reshape(4 * sums_g.shape[1], D)
            acc[nm] = _reduce_adamw(acc.get(nm), chip_arr, own, slots_g, w_rows[nm], m_rows[nm], v_rows[nm], l)
        return [acc[nm][0] for nm in group]

    flights = {}
    token_b = None
    for l in reversed(range(L)):
        sv = saved[l]
        gt1, ut1, dn1, wint, wout, gt2, ut2, dn2 = gathered[l]
        gb = g_ffn2[l] if token_b is None else g_ffn2[l] + token_b[0, 0]
        full = {}
        dx, dgate, dup, h, dy, d_norm["ffn2_norm"][l] = _ffn_bwd_d(sv["x2"], gb, sv["gate2"], sv["up2"], dx, gt2, ut2, dn2)
        full["ffn2_w_gate"], full["ffn2_w_up"], full["ffn2_w_down"] = _ffn_bwd_w(h, dy, sv["gate2"], sv["up2"], dgate, dup)

        dxb, dyp, do1, do4, do16, dl1, dl4, dl16 = _mix_out_bwd(dx, sv["o"], wout)
        full["w_out"] = _wgrad(sv["mixed"], dxb)
        flights[l, "a"], token_a = (exchange_cores if l == 0 else exchange)(full, group_a, dxb, f"a{l}")
        dvp, dwbd, d_pscale[l] = _pool_bwd(dyp, sv["diff"], wbd[l], pscale[l] + token_a[0, 0])
        d_poolw[l] = jnp.stack([dwbd[64 * gi:64 * (gi + 1), 64 * gi:64 * (gi + 1)] for gi in range(4)])
        dos, dls = (do1, flat(do4), flat(do16)), (dl1, flat(dl4), flat(dl16))
        dqkv = []
        for b, lc in enumerate((S, S // 4, S // 16)):
            qb, kb, vb = sv["qkv"][b]
            dqkv.append(_attn_bwd(qb, kb, vb, dos[b], sv["lse"][b], dls[b], lc))
        d4 = tuple(r4(a) for a in dqkv[1])
        d16 = tuple(r16(a) for a in dqkv[2])
        gm = g_mix[l]
        if l == 0:
            flights[0, "a"], token_a = exchange_chips(flights[0, "a"], [dqkv[0][0], dqkv[1][0], dqkv[2][0]])
            gm = gm + token_a[0, 0]
        dx, dproj, d_norm["mix_norm"][l] = _mix_in_bwd(dx, sv["x1"], gm, wint, tabs, dvp, dqkv[0], d4, d16)
        full["w_in"] = _wgrad(dproj, sv["hmix"])

        dx, dgate, dup, h, dy, d_norm["ffn1_norm"][l] = _ffn_bwd_d(sv["x0"], g_ffn1[l], sv["gate1"], sv["up1"], dx, gt1, ut1, dn1)
        full["ffn1_w_gate"], full["ffn1_w_up"], full["ffn1_w_down"] = _ffn_bwd_w(h, dy, sv["gate1"], sv["up1"], dgate, dup)

        after = dx
        if l + 1 < L and l + 1 >= 2:
            after, _ = update(l + 1, group_a, flights.pop((l + 1, "a")), after)
            after, _ = update(l + 1, group_b, flights.pop((l + 1, "b")), after)
        if l > 0:
            flights[l, "b"], token_b = exchange(full, group_b, after, f"b{l}")

    flights[0, "b"], token_b = exchange_cores(full, group_b, dx, "b0")
    flights[0, "b"], token_b = exchange_chips(flights[0, "b"], token_b)
    pad8 = lambda a: jnp.pad(a, ((0, 8 - a.shape[0]), (0, 0)))
    misc = jnp.concatenate([d_final, jnp.concatenate(d_pscale, axis=1), loss_part], axis=0)
    small = jnp.concatenate(
        [pad8(jnp.concatenate(d_norm[nm], axis=0)) for nm in ("ffn1_norm", "mix_norm", "ffn2_norm")]
        + [pad8(misc), jnp.stack(d_poolw).reshape(L * 16, D)], axis=0)
    small_slots = lax.dynamic_update_slice(lax.empty((NDEV, SMALL_ROWS, D), f32), small[None], (me_idx, 0, 0))
    pack_sems = _rs_start([small], [small_slots], token_b, "pack")

    after = pack_sems[-1]
    for key in [(1, "a"), (1, "b")]:
        after, _ = update(key[0], group_a if key[1] == "a" else group_b, flights.pop(key), after)
    after = update_chips(0, group_a, flights.pop((0, "a")), after)
    after = update_chips(0, group_b, flights.pop((0, "b")), after)
    _, pack_slots = _rs_wait(pack_sems[2], pack_sems[3], pack_sems[0], pack_sems[1], after, "pack")

    sm = _sum_slots(pack_slots[0], SMALL_ROWS)
    grads = {}
    grads["ffn1_norm"], grads["mix_norm"], grads["ffn2_norm"] = sm[0:L], sm[8:8 + L], sm[16:16 + L]
    grads["final_norm"] = sm[24]
    grads["pool_scale"] = sm[25].reshape(L, PW)
    grads["pool_w"] = sm[32:32 + L * 16].reshape(L, 4, 64, 64)
    loss = sm[26, 0]
    upd = {nm: _adamw(weights[nm], grads[nm], moms[nm], vels[nm]) for nm in names if nm not in _BIG}
    for nm in _BIG:
        grads[nm], upd[nm] = as_rows(acc[nm][0], nm), tuple(as_rows(a, nm) for a in acc[nm][1:])
    return (loss, dx.reshape(1, S, D), *[grads[nm] for nm in names], *[upd[nm][0] for nm in names],
            *[upd[nm][1] for nm in names], *[upd[nm][2] for nm in names])
```

```python
import jax
import jax.numpy as jnp
from jax import lax
from jax.experimental import pallas as pl
from jax.experimental.pallas import tpu as pltpu

f32 = jnp.float32
bf16 = jnp.bfloat16
SDS = jax.ShapeDtypeStruct

D = 1024
S = 2048
F = 2816
L = 4
PW = 256
AW = 768
PROJ = PW + 3 * AW
NDEV = 8
TM = 256
QB = 128
HALF = 64
NG = AW // 128
NORM_EPS = 1e-6
MASK_VALUE = -1e30
ROPE_THETA = 500000.0
ADAM_LR, ADAM_B1, ADAM_B2, ADAM_EPS, ADAM_WD, ADAM_STEP = 0.001, 0.9, 0.999, 1e-08, 0.01, 10
POOL_WINDOWS = (2, 4, 8, 16)
PAD = 8
SMALL_ROWS = 96
VMEM_LIMIT = 56 * 1024 * 1024

_CP = pltpu.CompilerParams(vmem_limit_bytes=VMEM_LIMIT)
_ANY = pl.BlockSpec(memory_space=pl.ANY)
_HBM = pl.BlockSpec(memory_space=pltpu.HBM)
_SEM = pl.BlockSpec(memory_space=pltpu.SEMAPHORE)
_MESH = pl.DeviceIdType.MESH
_CP_SPLIT = pltpu.CompilerParams(has_side_effects=pltpu.SideEffectType.DATAFLOW_SIDE_EFFECTING)


def _dot_nn(a, b):
    return lax.dot_general(a, b, (((1,), (0,)), ((), ())), preferred_element_type=f32)


def _dot_nt(a, b):
    return lax.dot_general(a, b, (((1,), (1,)), ((), ())), preferred_element_type=f32)


def _dot_tn(a, b):
    return lax.dot_general(a, b, (((0,), (0,)), ((), ())), preferred_element_type=f32)


def _rms(x, g):
    r = lax.rsqrt(jnp.mean(x * x, axis=-1, keepdims=True) + NORM_EPS)
    xh = x * r
    return r, xh, xh * g


def _rms_bwd(dh, r, xh, g):
    dxh = dh * g
    return r * (dxh - xh * jnp.mean(dxh * xh, axis=-1, keepdims=True))


def _tile(cols):
    return pl.BlockSpec((TM, cols), lambda i: (i, 0))


def _const(shape):
    return pl.BlockSpec(shape, lambda i: (0,) * len(shape))


def _layer(rows, cols, l=None):
    return pl.BlockSpec((rows, cols), lambda i: (0, 0), pipeline_mode=pl.Buffered(1))


def _p4(cols=AW):
    return pl.BlockSpec((4, TM // 4, cols), lambda i: (0, i, 0))


def _p16(cols=AW):
    return pl.BlockSpec((16, TM // 16, cols), lambda i: (0, i, 0))


def _cols(j):
    return slice(128 * j, 128 * (j + 1))


def _ffn_fwd(x, g, gt, ut, dn, l=None):
    def body(x_ref, g_ref, gt_ref, ut_ref, dn_ref, xo_ref, gate_ref, up_ref):
        x = x_ref[...]
        _, _, hn = _rms(x, g_ref[...])
        h = hn.astype(bf16)
        gate = _dot_nt(h, gt_ref[...])
        up = _dot_nt(h, ut_ref[...])
        gate_ref[...] = gate.astype(bf16)
        up_ref[...] = up.astype(bf16)
        a = (gate * jax.nn.sigmoid(gate) * up).astype(bf16)
        xo_ref[...] = x + 0.5 * _dot_nn(a, dn_ref[...])

    return pl.pallas_call(
        body, grid=(S // TM,),
        in_specs=[_tile(D), _layer(1, D, l), _layer(F, D, l), _layer(F, D, l), _layer(F, D, l)],
        out_specs=[_tile(D), _tile(F), _tile(F)],
        out_shape=[SDS((S, D), f32), SDS((S, F), bf16), SDS((S, F), bf16)],
        compiler_params=_CP, name="ffn_fwd")(x, g, gt, ut, dn)


def _ffn_bwd_d(x, g, gate, up, dxo, gt, ut, dn, l=None):
    def body(x_ref, g_ref, gate_ref, up_ref, dxo_ref, gt_ref, ut_ref, dn_ref,
             dx_ref, dgate_ref, dup_ref, h_ref, dy_ref, dg_ref):
        x = x_ref[...]
        g = g_ref[...]
        r, xh, hn = _rms(x, g)
        h_ref[...] = hn.astype(bf16)
        dxo = dxo_ref[...]
        dy = (0.5 * dxo).astype(bf16)
        dy_ref[...] = dy
        da = _dot_nt(dy, dn_ref[...])
        gate = gate_ref[...].astype(f32)
        up = up_ref[...].astype(f32)
        sg = jax.nn.sigmoid(gate)
        dgate = (da * up * (sg * (1.0 + gate * (1.0 - sg)))).astype(bf16)
        dup = (da * (gate * sg)).astype(bf16)
        dgate_ref[...] = dgate
        dup_ref[...] = dup
        dh = _dot_nn(dgate, gt_ref[...]) + _dot_nn(dup, ut_ref[...])

        @pl.when(pl.program_id(0) == 0)
        def _():
            dg_ref[...] = jnp.zeros_like(dg_ref)

        dg_ref[...] += jnp.sum(dh * xh, axis=0, keepdims=True)
        dx_ref[...] = dxo + _rms_bwd(dh, r, xh, g)

    return pl.pallas_call(
        body, grid=(S // TM,),
        in_specs=[_tile(D), _layer(1, D, l), _tile(F), _tile(F), _tile(D),
                  _layer(F, D, l), _layer(F, D, l), _layer(F, D, l)],
        out_specs=[_tile(D), _tile(F), _tile(F), _tile(D), _tile(D), _const((1, D))],
        out_shape=[SDS((S, D), f32), SDS((S, F), bf16), SDS((S, F), bf16), SDS((S, D), bf16),
                   SDS((S, D), bf16), SDS((1, D), f32)],
        compiler_params=_CP, name="ffn_bwd_d")(x, g, gate, up, dxo, gt, ut, dn)


def _ffn_bwd_w(h, dy, gate, up, dgate, dup):
    fc = 256

    def body(h_ref, dy_ref, gate_ref, up_ref, dgate_ref, dup_ref, dgt_ref, dut_ref, ddn_ref):
        gate = gate_ref[...].astype(f32)
        a = (gate * jax.nn.sigmoid(gate) * up_ref[...].astype(f32)).astype(bf16)
        ddn_ref[...] = _dot_tn(a, dy_ref[...]).astype(bf16)
        h = h_ref[...]
        dgt_ref[...] = _dot_tn(dgate_ref[...], h).astype(bf16)
        dut_ref[...] = _dot_tn(dup_ref[...], h).astype(bf16)

    col = pl.BlockSpec((S, fc), lambda j: (0, j))
    row = pl.BlockSpec((fc, D), lambda j: (j, 0))
    full = pl.BlockSpec((S, D), lambda j: (0, 0))
    return pl.pallas_call(
        body, grid=(F // fc,),
        in_specs=[full, full, col, col, col, col],
        out_specs=[row, row, row],
        out_shape=[SDS((F, D), bf16)] * 3,
        compiler_params=_CP, name="ffn_bwd_w")(h, dy, gate, up, dgate, dup)


def _wgrad(a, b):
    m, n = a.shape[1], b.shape[1]
    mc = 256

    def body(a_ref, b_ref, o_ref):
        o_ref[...] = _dot_tn(a_ref[...], b_ref[...]).astype(bf16)

    return pl.pallas_call(
        body, grid=(m // mc,),
        in_specs=[pl.BlockSpec((S, mc), lambda j: (0, j)), pl.BlockSpec((S, n), lambda j: (0, 0))],
        out_specs=pl.BlockSpec((mc, n), lambda j: (j, 0)),
        out_shape=SDS((m, n), bf16),
        compiler_params=_CP, name="wgrad")(a, b)


def _rope(t, c, sn, sp):
    return t * c + pltpu.roll(t, 120, 1) * sn + pltpu.roll(t, 8, 1) * sp


def _rope_bwd(d, c, sn, sp):
    return d * c + pltpu.roll(d * sn, 8, 1) + pltpu.roll(d * sp, 120, 1)


def _rope_tables(positions):
    inv_freq = ROPE_THETA ** (-jnp.arange(0, 16, 2, dtype=f32) / 16)
    ang = positions.reshape(S, 1).astype(f32) * inv_freq
    cos, sin = jnp.cos(ang), jnp.sin(ang)
    one = jnp.ones((S, 48), f32)
    zero8 = jnp.zeros((S, 8), f32)
    zero48 = jnp.zeros((S, 48), f32)
    c = jnp.concatenate([cos, cos, one], axis=1)
    sn = jnp.concatenate([-sin, zero8, zero48], axis=1)
    sp = jnp.concatenate([zero8, sin, zero48], axis=1)
    return tuple(jnp.concatenate([t, t], axis=1) for t in (c, sn, sp))


def _mix_in_fwd(x, g, wint, tabs, l=None):
    def body(x_ref, g_ref, w_ref, c_ref, sn_ref, sp_ref,
             h_ref, vp_ref, q1, k1, v1, q4, k4, v4, q16, k16, v16, scr):
        _, _, hn = _rms(x_ref[...], g_ref[...])
        h = hn.astype(bf16)
        h_ref[...] = h
        proj = _dot_nt(h, w_ref[...])
        vp_ref[...] = proj[:, :PW]
        c, sn, sp = c_ref[...], sn_ref[...], sp_ref[...]
        for kind, (o1, o4, o16) in enumerate(((q1, q4, q16), (k1, k4, k16), (v1, v4, v16))):
            for j in range(NG):
                t = proj[:, PW + kind * AW + 128 * j: PW + kind * AW + 128 * (j + 1)]
                if kind == 0:
                    t = _rope(t, c, sn, sp) * 0.125
                elif kind == 1:
                    t = _rope(t, c, sn, sp)
                scr[j] = t
                o1[:, _cols(j)] = t.astype(bf16)
            for r in range(4):
                for j in range(NG):
                    o4[r, :, _cols(j)] = scr[j, pl.ds(r, TM // 4, stride=4), :].astype(bf16)
            for r in range(16):
                for j in range(NG):
                    o16[r, :, _cols(j)] = scr[j, pl.ds(r, TM // 16, stride=16), :].astype(bf16)

    nat, d4, d16 = SDS((S, AW), bf16), SDS((4, S // 4, AW), bf16), SDS((16, S // 16, AW), bf16)
    return pl.pallas_call(
        body, grid=(S // TM,),
        in_specs=[_tile(D), _layer(1, D, l), _layer(PROJ, D, l), _tile(128), _tile(128), _tile(128)],
        out_specs=[_tile(D), _tile(PW)] + [_tile(AW)] * 3 + [_p4()] * 3 + [_p16()] * 3,
        out_shape=[SDS((S, D), bf16), SDS((S, PW), f32)] + [nat] * 3 + [d4] * 3 + [d16] * 3,
        scratch_shapes=[pltpu.VMEM((NG, TM, 128), f32)],
        compiler_params=_CP, name="mix_in_fwd")(x, g, wint, *tabs)


def _mix_in_bwd(dxo, x, g, wint, tabs, dvp, d1, d4, d16, l=None):
    def body(dxo_ref, x_ref, g_ref, w_ref, c_ref, sn_ref, sp_ref, dvp_ref,
             dq1, dk1, dv1, dq4, dk4, dv4, dq16, dk16, dv16,
             dx_ref, dproj_ref, dg_ref, s4, s16):
        c, sn, sp = c_ref[...], sn_ref[...], sp_ref[...]
        dproj_ref[:, :PW] = dvp_ref[...].astype(bf16)
        for kind, (a1, a4, a16) in enumerate(((dq1, dq4, dq16), (dk1, dk4, dk16), (dv1, dv4, dv16))):
            for r in range(4):
                for j in range(NG):
                    s4[j, pl.ds(r, TM // 4, stride=4), :] = a4[r, :, _cols(j)].astype(f32)
            for r in range(16):
                for j in range(NG):
                    s16[j, pl.ds(r, TM // 16, stride=16), :] = a16[r, :, _cols(j)].astype(f32)
            for j in range(NG):
                t = a1[:, _cols(j)].astype(f32) + s4[j] + s16[j]
                if kind == 0:
                    t = _rope_bwd(t * 0.125, c, sn, sp)
                elif kind == 1:
                    t = _rope_bwd(t, c, sn, sp)
                dproj_ref[:, PW + kind * AW + 128 * j: PW + kind * AW + 128 * (j + 1)] = t.astype(bf16)
        g = g_ref[...]
        r_, xh, _ = _rms(x_ref[...], g)
        dh = _dot_nn(dproj_ref[...], w_ref[...])

        @pl.when(pl.program_id(0) == 0)
        def _():
            dg_ref[...] = jnp.zeros_like(dg_ref)

        dg_ref[...] += jnp.sum(dh * xh, axis=0, keepdims=True)
        dx_ref[...] = dxo_ref[...] + _rms_bwd(dh, r_, xh, g)

    return pl.pallas_call(
        body, grid=(S // TM,),
        in_specs=[_tile(D), _tile(D), _layer(1, D, l), _layer(PROJ, D, l), _tile(128), _tile(128), _tile(128),
                  _tile(PW)] + [_tile(AW)] * 3 + [_p4()] * 3 + [_p16()] * 3,
        out_specs=[_tile(D), _tile(PROJ), _const((1, D))],
        out_shape=[SDS((S, D), f32), SDS((S, PROJ), bf16), SDS((1, D), f32)],
        scratch_shapes=[pltpu.VMEM((NG, TM, 128), f32), pltpu.VMEM((NG, TM, 128), f32)],
        compiler_params=_CP, name="mix_in_bwd")(dxo, x, g, wint, *tabs, dvp, *d1, *d4, *d16)


def _pool_sums(pad_ref, base, rows, adjoint):
    lane_group = lax.broadcasted_iota(jnp.int32, (rows, PW), 1) // 64
    sign = -1 if adjoint else 1

    def sh(o):
        return pad_ref[pl.ds(PAD + base + sign * o, rows), :]

    out = None
    acc = None
    lo, hi = 0, 0
    for gi, w in enumerate(POOL_WINDOWS):
        for o in list(range(-(w // 2), lo)) + list(range(hi, w - w // 2)):
            acc = sh(o) if acc is None else acc + sh(o)
        lo, hi = -(w // 2), w - w // 2
        out = acc if out is None else jnp.where(lane_group >= gi, acc, out)
    return out


def _pool_counts(base, rows):
    pos = base + lax.broadcasted_iota(jnp.int32, (rows, PW), 0)
    lane_group = lax.broadcasted_iota(jnp.int32, (rows, PW), 1) // 64
    cnt = None
    for gi, w in enumerate(POOL_WINDOWS):
        lo = jnp.maximum(pos - w // 2, 0)
        hi = jnp.minimum(pos + w - 1 - w // 2, S - 1)
        c = (hi - lo + 1).astype(f32)
        cnt = c if cnt is None else jnp.where(lane_group >= gi, c, cnt)
    return cnt


def _pool_fwd(vp, wbd, scale, l=None):
    ch = 256

    def body(vp_ref, w_ref, sc_ref, y_ref, diff_ref, pad):
        pad[pl.ds(0, PAD), :] = jnp.zeros((PAD, PW), f32)
        pad[pl.ds(PAD + S, PAD), :] = jnp.zeros((PAD, PW), f32)
        pad[pl.ds(PAD, S), :] = vp_ref[...]
        for b in range(S // ch):
            base = b * ch
            pooled = _pool_sums(pad, base, ch, False) / _pool_counts(base, ch)
            diff = (pooled - vp_ref[pl.ds(base, ch), :]).astype(bf16)
            diff_ref[pl.ds(base, ch), :] = diff
            y_ref[pl.ds(base, ch), :] = _dot_nn(diff, w_ref[...]) * sc_ref[...]

    whole = lambda shape: pl.BlockSpec(shape, lambda i: (0,) * len(shape))
    return pl.pallas_call(
        body, grid=(1,),
        in_specs=[whole((S, PW)), whole((PW, PW)), whole((1, PW))],
        out_specs=[whole((S, PW)), whole((S, PW))],
        out_shape=[SDS((S, PW), f32), SDS((S, PW), bf16)],
        scratch_shapes=[pltpu.VMEM((S + 2 * PAD, PW), f32)],
        compiler_params=_CP, name="pool_fwd")(vp, wbd, scale)


def _pool_bwd(dy, diff, wbd, scale, l=None):
    ch = 256

    def body(dy_ref, diff_ref, w_ref, sc_ref, dvp_ref, dw_ref, dsc_ref, pad):
        pad[pl.ds(0, PAD), :] = jnp.zeros((PAD, PW), f32)
        pad[pl.ds(PAD + S, PAD), :] = jnp.zeros((PAD, PW), f32)
        dw = jnp.zeros((PW, PW), f32)
        dsc = jnp.zeros((1, PW), f32)
        for b in range(S // ch):
            base = b * ch
            dy = dy_ref[pl.ds(base, ch), :]
            diff = diff_ref[pl.ds(base, ch), :]
            dsc = dsc + jnp.sum(dy * _dot_nn(diff, w_ref[...]), axis=0, keepdims=True)
            dz = (dy * sc_ref[...]).astype(bf16)
            dw = dw + _dot_tn(diff, dz)
            ddiff = _dot_nt(dz, w_ref[...])
            dvp_ref[pl.ds(base, ch), :] = -ddiff
            pad[pl.ds(PAD + base, ch), :] = ddiff / _pool_counts(base, ch)
        dw_ref[...] = dw
        dsc_ref[...] = dsc
        for b in range(S // ch):
            base = b * ch
            dvp_ref[pl.ds(base, ch), :] += _pool_sums(pad, base, ch, True)

    whole = lambda shape: pl.BlockSpec(shape, lambda i: (0,) * len(shape))
    return pl.pallas_call(
        body, grid=(1,),
        in_specs=[whole((S, PW)), whole((S, PW)), whole((PW, PW)), whole((1, PW))],
        out_specs=[whole((S, PW)), whole((PW, PW)), whole((1, PW))],
        out_shape=[SDS((S, PW), f32), SDS((PW, PW), f32), SDS((1, PW), f32)],
        scratch_shapes=[pltpu.VMEM((S + 2 * PAD, PW), f32)],
        compiler_params=_CP, name="pool_bwd")(dy, diff, wbd, scale)


def _attn_blocks(lc):
    bpc = lc // QB
    kw = min(2 * QB, lc)
    blocks = []
    for b in range(S // QB):
        t0 = (b % bpc) * QB
        ks_in = min(max(t0 - HALF, 0), lc - kw)
        blocks.append((b * QB, (b // bpc) * lc + ks_in, t0 - ks_in))
    return kw, blocks


def _attn_bias(bias_ref, kw, shifts):
    r = lax.broadcasted_iota(jnp.int32, (2 * QB, kw), 0) % QB
    c = lax.broadcasted_iota(jnp.int32, (2 * QB, kw), 1)
    for i, shift in enumerate(shifts):
        bias_ref[i] = jnp.where(jnp.abs(r + shift - c) <= HALF, 0.0, MASK_VALUE).astype(f32)


def _head_put(stats, pair, v0, v1, lane):
    return jnp.where(lane == 2 * pair, v0, jnp.where(lane == 2 * pair + 1, v1, stats))


def _head_cols(stats, pair, lane):
    c0 = jnp.sum(jnp.where(lane == 2 * pair, stats, 0.0), axis=-1, keepdims=True)
    c1 = jnp.sum(jnp.where(lane == 2 * pair + 1, stats, 0.0), axis=-1, keepdims=True)
    return jnp.concatenate([c0, c1], axis=0)


def _head_spread(stats, pair, head0):
    return jnp.where(head0, stats[:, 2 * pair:2 * pair + 1], stats[:, 2 * pair + 1:2 * pair + 2])


def _stack_heads(blk, head0):
    zero = jnp.zeros_like(blk)
    return jnp.concatenate([jnp.where(head0, blk, zero), jnp.where(head0, zero, blk)], axis=0)


def _attn_fwd(q, k, v, lc, after=None):
    kw, blocks = _attn_blocks(lc)
    shifts = sorted({b[2] for b in blocks})

    def body(q_ref, k_ref, v_ref, *refs):
        o_ref, lse_ref, bias_ref = refs[-3:]
        lane = lax.broadcasted_iota(jnp.int32, (QB, 128), 1)
        head0 = lane < 64
        pair = pl.program_id(0)
        _attn_bias(bias_ref, kw, shifts)

        @pl.when(pair == 0)
        def _():
            lse_ref[...] = jnp.zeros_like(lse_ref)

        for row0, kstart, shift in blocks:
            q2 = _stack_heads(q_ref[pl.ds(row0, QB), :], head0)
            kb = k_ref[pl.ds(kstart, kw), :]
            vb = v_ref[pl.ds(kstart, kw), :]
            s = _dot_nt(q2, kb) + bias_ref[shifts.index(shift)]
            m = jnp.max(s, axis=-1, keepdims=True)
            p = jnp.exp(s - m)
            den = jnp.sum(p, axis=-1, keepdims=True)
            o2 = _dot_nn(p.astype(bf16), vb) / den
            lse2 = m + jnp.log(den)
            o_ref[pl.ds(row0, QB), :] = jnp.where(head0, o2[:QB], o2[QB:]).astype(bf16)
            lse_ref[pl.ds(row0, QB), :] = _head_put(lse_ref[pl.ds(row0, QB), :], pair, lse2[:QB], lse2[QB:], lane)

    col = pl.BlockSpec((S, 128), lambda p: (0, p))
    extra = () if after is None else (after,)
    return pl.pallas_call(
        body, grid=(NG,), in_specs=[col, col, col] + [_ANY] * len(extra),
        out_specs=[col, pl.BlockSpec((S, 128), lambda p: (0, 0))],
        out_shape=[SDS((S, AW), bf16), SDS((S, 128), f32)],
        scratch_shapes=[pltpu.VMEM((len(shifts), 2 * QB, kw), f32)],
        compiler_params=_CP, name=f"attn_fwd_{lc}")(q, k, v, *extra)


def _attn_bwd(q, k, v, do, lse, delta, lc):
    kw, blocks = _attn_blocks(lc)
    shifts = sorted({b[2] for b in blocks})

    def body(q_ref, k_ref, v_ref, do_ref, lse_ref, dl_ref, dq_ref, dk_out, dv_out, bias_ref, dk_ref, dv_ref):
        lane = lax.broadcasted_iota(jnp.int32, (QB, 128), 1)
        head0 = lane < 64
        pair = pl.program_id(0)
        _attn_bias(bias_ref, kw, shifts)
        dk_ref[...] = jnp.zeros_like(dk_ref)
        dv_ref[...] = jnp.zeros_like(dv_ref)
        for row0, kstart, shift in blocks:
            q2 = _stack_heads(q_ref[pl.ds(row0, QB), :], head0)
            do2 = _stack_heads(do_ref[pl.ds(row0, QB), :], head0)
            lse2 = _head_cols(lse_ref[pl.ds(row0, QB), :], pair, lane)
            dl2 = _head_cols(dl_ref[pl.ds(row0, QB), :], pair, lane)
            kb = k_ref[pl.ds(kstart, kw), :]
            vb = v_ref[pl.ds(kstart, kw), :]
            p = jnp.exp(_dot_nt(q2, kb) + bias_ref[shifts.index(shift)] - lse2)
            ds = (p * (_dot_nt(do2, vb) - dl2)).astype(bf16)
            dq2 = _dot_nn(ds, kb)
            dq_ref[pl.ds(row0, QB), :] = jnp.where(head0, dq2[:QB], dq2[QB:]).astype(bf16)
            dk_ref[pl.ds(kstart, kw), :] += _dot_tn(ds, q2)
            dv_ref[pl.ds(kstart, kw), :] += _dot_tn(p.astype(bf16), do2)
        dk_out[...] = dk_ref[...].astype(bf16)
        dv_out[...] = dv_ref[...].astype(bf16)

    col = pl.BlockSpec((S, 128), lambda p: (0, p))
    stats = pl.BlockSpec((S, 128), lambda p: (0, 0))
    return pl.pallas_call(
        body, grid=(NG,), in_specs=[col] * 4 + [stats] * 2, out_specs=[col] * 3,
        out_shape=[SDS((S, AW), bf16)] * 3,
        scratch_shapes=[pltpu.VMEM((len(shifts), 2 * QB, kw), f32), pltpu.VMEM((S, 128), f32),
                        pltpu.VMEM((S, 128), f32)],
        compiler_params=_CP, name=f"attn_bwd_{lc}")(q, k, v, do, lse, delta)


def _mix_out_fwd(x, ypool, o1, l1, o4, l4, o16, l16, wout, l=None):
    def body(x_ref, yp_ref, o1_ref, l1_ref, o4_ref, l4_ref, o16_ref, l16_ref, w_ref,
             xo_ref, mixed_ref, o_ref, lse1_ref, lse4_ref, lse16_ref, so4, so16, sl4, sl16, sl):
        head0 = lax.broadcasted_iota(jnp.int32, (TM, 128), 1) < 64
        for r in range(4):
            sl4[pl.ds(r, TM // 4, stride=4), :] = l4_ref[r]
            for j in range(NG):
                so4[j, pl.ds(r, TM // 4, stride=4), :] = o4_ref[r, :, _cols(j)].astype(f32)
        for r in range(16):
            sl16[pl.ds(r, TM // 16, stride=16), :] = l16_ref[r]
            for j in range(NG):
                so16[j, pl.ds(r, TM // 16, stride=16), :] = o16_ref[r, :, _cols(j)].astype(f32)
        a, b, c = l1_ref[...], sl4[...], sl16[...]
        m = jnp.maximum(jnp.maximum(a, b), c)
        wa, wb, wc = jnp.exp(a - m), jnp.exp(b - m), jnp.exp(c - m)
        den = wa + wb + wc
        wa, wb, wc = wa / den, wb / den, wc / den
        lse = m + jnp.log(den)
        lse1_ref[...] = lse
        sl[...] = lse
        mixed_ref[:, :PW] = yp_ref[...].astype(bf16)
        for j in range(NG):
            y = (_head_spread(wa, j, head0) * o1_ref[:, _cols(j)].astype(f32) + _head_spread(wb, j, head0) * so4[j]
                 + _head_spread(wc, j, head0) * so16[j])
            o_ref[:, _cols(j)] = y
            mixed_ref[:, PW + 128 * j: PW + 128 * (j + 1)] = y.astype(bf16)
        for r in range(4):
            lse4_ref[r] = sl[pl.ds(r, TM // 4, stride=4), :]
        for r in range(16):
            lse16_ref[r] = sl[pl.ds(r, TM // 16, stride=16), :]
        xo_ref[...] = x_ref[...] + _dot_nn(mixed_ref[...], w_ref[...])

    wide, narrow = pltpu.VMEM((NG, TM, 128), f32), pltpu.VMEM((TM, 128), f32)
    return pl.pallas_call(
        body, grid=(S // TM,),
        in_specs=[_tile(D), _tile(PW), _tile(AW), _tile(128), _p4(), _p4(128), _p16(), _p16(128), _layer(D, D, l)],
        out_specs=[_tile(D), _tile(D), _tile(AW), _tile(128), _p4(128), _p16(128)],
        out_shape=[SDS((S, D), f32), SDS((S, D), bf16), SDS((S, AW), f32), SDS((S, 128), f32),
                   SDS((4, S // 4, 128), f32), SDS((16, S // 16, 128), f32)],
        scratch_shapes=[wide, wide, narrow, narrow, narrow],
        compiler_params=_CP, name="mix_out_fwd")(x, ypool, o1, l1, o4, l4, o16, l16, wout)


def _mix_out_bwd(dxo, o, wout, l=None):
    def body(dxo_ref, o_ref, w_ref, dxb_ref, dyp_ref, do1, do4, do16, dl1, dl4, dl16, sdo, sdl):
        dxb = dxo_ref[...].astype(bf16)
        dxb_ref[...] = dxb
        dm = _dot_nt(dxb, w_ref[...])
        dyp_ref[...] = dm[:, :PW]
        lane = lax.broadcasted_iota(jnp.int32, (TM, 128), 1)
        head0 = lane < 64
        dl = jnp.zeros((TM, 128), f32)
        for j in range(NG):
            d = dm[:, PW + 128 * j: PW + 128 * (j + 1)]
            prod = d * o_ref[:, _cols(j)]
            dl = _head_put(dl, j, jnp.sum(jnp.where(head0, prod, 0.0), axis=-1, keepdims=True),
                           jnp.sum(jnp.where(head0, 0.0, prod), axis=-1, keepdims=True), lane)
            do1[:, _cols(j)] = d.astype(bf16)
            sdo[j] = d
        dl1[...] = dl
        sdl[...] = dl
        for r in range(4):
            dl4[r] = sdl[pl.ds(r, TM // 4, stride=4), :]
            for j in range(NG):
                do4[r, :, _cols(j)] = sdo[j, pl.ds(r, TM // 4, stride=4), :].astype(bf16)
        for r in range(16):
            dl16[r] = sdl[pl.ds(r, TM // 16, stride=16), :]
            for j in range(NG):
                do16[r, :, _cols(j)] = sdo[j, pl.ds(r, TM // 16, stride=16), :].astype(bf16)

    return pl.pallas_call(
        body, grid=(S // TM,),
        in_specs=[_tile(D), _tile(AW), _layer(D, D, l)],
        out_specs=[_tile(D), _tile(PW), _tile(AW), _p4(), _p16(), _tile(128), _p4(128), _p16(128)],
        out_shape=[SDS((S, D), bf16), SDS((S, PW), f32),
                   SDS((S, AW), bf16), SDS((4, S // 4, AW), bf16), SDS((16, S // 16, AW), bf16),
                   SDS((S, 128), f32), SDS((4, S // 4, 128), f32), SDS((16, S // 16, 128), f32)],
        scratch_shapes=[pltpu.VMEM((NG, TM, 128), f32), pltpu.VMEM((TM, 128), f32)],
        compiler_params=_CP, name="mix_out_bwd")(dxo, o, wout)


def _loss_head(x, g, target):
    def body(x_ref, g_ref, t_ref, dx_ref, loss_ref, dg_ref):
        g = g_ref[...]
        r, xh, y = _rms(x_ref[...], g)
        err = y - t_ref[...]
        dy = err * (1.0 / D)

        @pl.when(pl.program_id(0) == 0)
        def _():
            loss_ref[...] = jnp.zeros_like(loss_ref)
            dg_ref[...] = jnp.zeros_like(dg_ref)

        loss_ref[...] += jnp.broadcast_to(0.5 * jnp.sum(jnp.mean(err * err, axis=-1, keepdims=True)), (1, D))
        dg_ref[...] += jnp.sum(dy * xh, axis=0, keepdims=True)
        dx_ref[...] = _rms_bwd(dy, r, xh, g)

    return pl.pallas_call(
        body, grid=(S // TM,),
        in_specs=[_tile(D), _const((1, D)), _tile(D)],
        out_specs=[_tile(D), _const((1, D)), _const((1, D))],
        out_shape=[SDS((S, D), f32), SDS((1, D), f32), SDS((1, D), f32)],
        compiler_params=_CP, name="loss_head")(x, g, target)


def _peer(k):
    x, y, c = lax.axis_index("x"), lax.axis_index("y"), lax.axis_index("c")
    px = 1 - x if k & 4 else x
    py = 1 - y if k & 2 else y
    pc = 1 - c if k & 1 else c
    return (px, py, pc), 4 * px + 2 * py + pc


def _diag_route():
    x, y, c = lax.axis_index("x"), lax.axis_index("y"), lax.axis_index("c")
    idx_x, idx_y = _peer(4)[1], _peer(2)[1]
    return idx_x + c * (idx_y - idx_x), (x + c * (1 - 2 * x), (1 - y) + c * (2 * y - 1), c)


def _all_gather(lands):
    n = len(lands)

    def body(*refs):
        zones, send_sems, recv_sems = refs[n:2 * n], refs[2 * n], refs[2 * n + 1]
        me, me_idx = _peer(0)
        sibling, sib_idx = _peer(1)
        (x_nbr, idx_x), (y_nbr, idx_y), idx_d = _peer(4), _peer(2), _peer(6)[1]
        fwd_idx, fwd_dev = _diag_route()

        def copy(k, t, idx, to):
            return _row_copy(zones[t], idx, send_sems.at[k, t], recv_sems.at[k, t], to)

        sent = []

        def send(k, t, idx, to):
            cp = copy(k, t, idx, to)
            cp.start()
            sent.append(cp)

        for t in range(n):
            send(0, t, me_idx, sibling)
            send(1, t, me_idx, x_nbr)
            send(2, t, me_idx, y_nbr)
        for t in range(n):
            copy(1, t, idx_x, me).wait_recv()
            send(3, t, idx_x, sibling)
        for t in range(n):
            copy(2, t, idx_y, me).wait_recv()
            send(4, t, idx_y, sibling)
        for t in range(n):
            send(5, t, fwd_idx, fwd_dev)
        for t in range(n):
            copy(5, t, idx_d, me).wait_recv()
            send(6, t, idx_d, sibling)
        for k, mask in ((0, 1), (3, 5), (4, 3), (6, 7)):
            for t in range(n):
                copy(k, t, _peer(mask)[1], me).wait_recv()
        for cp in sent:
            cp.wait_send()

    return pl.pallas_call(
        body, in_specs=[_ANY] * n, out_specs=[_ANY] * n,
        out_shape=[SDS(a.shape, a.dtype) for a in lands], input_output_aliases={t: t for t in range(n)},
        scratch_shapes=[pltpu.SemaphoreType.DMA((7, n)), pltpu.SemaphoreType.DMA((7, n))],
        name="all_gather_weights")(*lands)


def _hbm(a):
    return pltpu.with_memory_space_constraint(a, pltpu.HBM)


def _rows(ref, idx):
    r = ref.shape[0] // NDEV
    return ref.at[pl.ds(idx * r, r), :]


def _row_copy(ref, idx, send_sem, recv_sem, to):
    return pltpu.make_async_remote_copy(src_ref=_rows(ref, idx), dst_ref=_rows(ref, idx), send_sem=send_sem,
                                        recv_sem=recv_sem, device_id=to, device_id_type=_MESH)


def _place_own(me, shards, l):
    n = len(shards)

    def body(me_ref, *refs):
        for t in range(n):
            refs[n + t][...] = refs[t][...]

    grid_spec = pltpu.PrefetchScalarGridSpec(
        num_scalar_prefetch=1, grid=(1,),
        in_specs=[pl.BlockSpec((None, s.shape[1], D), lambda i, me_ref: (l, 0, 0)) for s in shards],
        out_specs=[pl.BlockSpec((s.shape[1], D), lambda i, me_ref: (me_ref[0], 0)) for s in shards])
    return pl.pallas_call(
        body, grid_spec=grid_spec, out_shape=[SDS((NDEV * s.shape[1], D), s.dtype) for s in shards],
        compiler_params=_CP, name="place_own")(me, *shards)


_TOKEN = SDS((8, 128), f32)
def _ag_start(lands, after, l):
    n = len(lands)
    after = list(after) if isinstance(after, (list, tuple)) else [after]

    def body(*refs):
        zones, send_sems, recv_sems, token = refs[:n], refs[n + len(after)], refs[n + len(after) + 1], refs[-1]
        _, me_idx = _peer(0)
        for k, mask in enumerate((1, 4, 2)):
            for t in range(n):
                _row_copy(zones[t], me_idx, send_sems.at[k * n + t], recv_sems.at[k * n + t], _peer(mask)[0]).start()
        token[...] = jnp.zeros_like(token)

    outs = pl.pallas_call(
        body, name=f"ag_start_{l}", in_specs=[_HBM] * n + [_ANY] * len(after),
        out_specs=(_SEM, _SEM, *[_HBM] * n, pl.BlockSpec(memory_space=pltpu.VMEM)),
        out_shape=(pltpu.SemaphoreType.DMA((3 * n,)), pltpu.SemaphoreType.DMA((3 * n,)),
                   *[pltpu.HBM(a.shape, a.dtype) for a in lands], _TOKEN),
        input_output_aliases={t: 2 + t for t in range(n)}, compiler_params=_CP_SPLIT)(
            *[_hbm(a) for a in lands], *after)
    return outs[0], outs[1], list(outs[2:2 + n]), outs[-1]


def _ag_pass(lands, recv_sems, after, l):
    n = len(lands)
    after = list(after) if isinstance(after, (list, tuple)) else [after]

    def body(*refs):
        zones, recv_sems = refs[:n], refs[n]
        psend, precv, token = refs[n + 1 + len(after)], refs[n + 2 + len(after)], refs[-1]
        me, _ = _peer(0)
        sibling, _ = _peer(1)
        for j, mask in enumerate((4, 2)):
            idx = _peer(mask)[1]
            for t in range(n):
                _row_copy(zones[t], idx, psend.at[j * n + t], recv_sems.at[(1 + j) * n + t], me).wait_recv()
                _row_copy(zones[t], idx, psend.at[j * n + t], precv.at[j * n + t], sibling).start()
        fwd_idx, fwd_dev = _diag_route()
        for t in range(n):
            _row_copy(zones[t], fwd_idx, psend.at[2 * n + t], precv.at[2 * n + t], fwd_dev).start()
        token[...] = jnp.zeros_like(token)

    outs = pl.pallas_call(
        body, name=f"ag_pass_{l}", in_specs=[_HBM] * n + [_SEM] + [_ANY] * len(after),
        out_specs=(_SEM, _SEM, *[_HBM] * n, pl.BlockSpec(memory_space=pltpu.VMEM)),
        out_shape=(pltpu.SemaphoreType.DMA((3 * n,)), pltpu.SemaphoreType.DMA((3 * n,)),
                   *[pltpu.HBM(a.shape, a.dtype) for a in lands], _TOKEN),
        input_output_aliases={t: 2 + t for t in range(n)}, compiler_params=_CP_SPLIT)(*lands, recv_sems, *after)
    return outs[0], outs[1], list(outs[2:2 + n]), outs[-1]


def _ag_last(lands, precv, after, l):
    n = len(lands)
    after = list(after) if isinstance(after, (list, tuple)) else [after]

    def body(*refs):
        zones, precv = refs[:n], refs[n]
        qsend, qrecv, token = refs[n + 1 + len(after)], refs[n + 2 + len(after)], refs[-1]
        me, _ = _peer(0)
        sibling, _ = _peer(1)
        idx = _peer(6)[1]
        for t in range(n):
            _row_copy(zones[t], idx, qsend.at[t], precv.at[2 * n + t], me).wait_recv()
            _row_copy(zones[t], idx, qsend.at[t], qrecv.at[t], sibling).start()
        token[...] = jnp.zeros_like(token)

    outs = pl.pallas_call(
        body, name=f"ag_last_{l}", in_specs=[_HBM] * n + [_SEM] + [_ANY] * len(after),
        out_specs=(_SEM, _SEM, *[_HBM] * n, pl.BlockSpec(memory_space=pltpu.VMEM)),
        out_shape=(pltpu.SemaphoreType.DMA((n,)), pltpu.SemaphoreType.DMA((n,)),
                   *[pltpu.HBM(a.shape, a.dtype) for a in lands], _TOKEN),
        input_output_aliases={t: 2 + t for t in range(n)}, compiler_params=_CP_SPLIT)(*lands, precv, *after)
    return outs[0], outs[1], list(outs[2:2 + n]), outs[-1]


def _ag_wait(lands, send_sems, recv_sems, psend, precv, qsend, qrecv, after, l):
    n = len(lands)
    after = list(after) if isinstance(after, (list, tuple)) else [after]

    def body(*refs):
        zones = refs[:n]
        send_sems, recv_sems, psend, precv, qsend, qrecv = refs[n:n + 6]
        me, me_idx = _peer(0)
        for k in range(3):
            for t in range(n):
                _row_copy(zones[t], me_idx, send_sems.at[k * n + t], recv_sems.at[k * n + t], me).wait_send()
        for t in range(n):
            _row_copy(zones[t], _peer(1)[1], send_sems.at[t], recv_sems.at[t], me).wait_recv()
        fwd_idx, _ = _diag_route()
        for j, (mine, theirs) in enumerate(((_peer(4)[1], _peer(5)[1]), (_peer(2)[1], _peer(3)[1]))):
            for t in range(n):
                _row_copy(zones[t], mine, psend.at[j * n + t], precv.at[j * n + t], me).wait_send()
                _row_copy(zones[t], theirs, psend.at[j * n + t], precv.at[j * n + t], me).wait_recv()
        for t in range(n):
            _row_copy(zones[t], fwd_idx, psend.at[2 * n + t], precv.at[2 * n + t], me).wait_send()
            _row_copy(zones[t], _peer(6)[1], qsend.at[t], qrecv.at[t], me).wait_send()
            _row_copy(zones[t], _peer(7)[1], qsend.at[t], qrecv.at[t], me).wait_recv()

    outs = pl.pallas_call(
        body, name=f"ag_wait_{l}", in_specs=[_HBM] * n + [_SEM] * 6 + [_ANY] * len(after),
        out_specs=tuple([_HBM] * n), out_shape=tuple(pltpu.HBM(a.shape, a.dtype) for a in lands),
        input_output_aliases={t: t for t in range(n)}, compiler_params=_CP_SPLIT)(
            *lands, send_sems, recv_sems, psend, precv, qsend, qrecv, *after)
    return list(outs)


def _xchg_src(ref, slot_ref, idx):
    return _rows(ref, idx) if ref.shape[0] == NDEV * slot_ref.shape[1] else ref


def _rs_start(srcs, slots, after, tag):
    n = len(srcs)
    after = list(after) if isinstance(after, (list, tuple)) else [after]

    def body(*refs):
        src, slot = refs[:n], refs[n:2 * n]
        send_sems, recv_sems, token = refs[2 * n + len(after)], refs[2 * n + len(after) + 1], refs[-1]
        _, me_idx = _peer(0)
        for k in range(1, NDEV):
            dev, idx = _peer(k)
            for t in range(n):
                pltpu.make_async_remote_copy(
                    src_ref=_xchg_src(src[t], slot[t], idx), dst_ref=slot[t].at[me_idx],
                    send_sem=send_sems.at[(k - 1) * n + t], recv_sem=recv_sems.at[(k - 1) * n + t],
                    device_id=dev, device_id_type=_MESH).start()
        token[...] = jnp.zeros_like(token)

    outs = pl.pallas_call(
        body, name=f"rs_start_{tag}", in_specs=[_HBM] * (2 * n) + [_ANY] * len(after),
        out_specs=(_SEM, _SEM, *[_HBM] * (2 * n), pl.BlockSpec(memory_space=pltpu.VMEM)),
        out_shape=(pltpu.SemaphoreType.DMA(((NDEV - 1) * n,)), pltpu.SemaphoreType.DMA(((NDEV - 1) * n,)),
                   *[pltpu.HBM(a.shape, a.dtype) for a in list(srcs) + list(slots)], _TOKEN),
        input_output_aliases={t: 2 + t for t in range(2 * n)}, compiler_params=_CP_SPLIT)(
            *[_hbm(a) for a in list(srcs) + list(slots)], *after)
    return outs[0], outs[1], list(outs[2:2 + n]), list(outs[2 + n:2 + 2 * n]), outs[-1]


def _rs_wait(srcs, slots, send_sems, recv_sems, after, tag):
    n = len(srcs)
    after = list(after) if isinstance(after, (list, tuple)) else [after]

    def body(*refs):
        src, slot, send_sems, recv_sems = refs[:n], refs[n:2 * n], refs[2 * n], refs[2 * n + 1]
        me, _ = _peer(0)
        for k in range(1, NDEV):
            idx = _peer(k)[1]
            for t in range(n):
                cp = pltpu.make_async_remote_copy(
                    src_ref=_xchg_src(src[t], slot[t], idx), dst_ref=slot[t].at[idx],
                    send_sem=send_sems.at[(k - 1) * n + t], recv_sem=recv_sems.at[(k - 1) * n + t],
                    device_id=me, device_id_type=_MESH)
                cp.wait_send()
                cp.wait_recv()

    outs = pl.pallas_call(
        body, name=f"rs_wait_{tag}", in_specs=[_HBM] * (2 * n) + [_SEM, _SEM] + [_ANY] * len(after),
        out_specs=tuple([_HBM] * (2 * n)),
        out_shape=tuple(pltpu.HBM(a.shape, a.dtype) for a in list(srcs) + list(slots)),
        input_output_aliases={t: t for t in range(2 * n)}, compiler_params=_CP_SPLIT)(
            *srcs, *slots, send_sems, recv_sems, *after)
    return list(outs[:n]), list(outs[n:])


def _pair_start(full4s, bufs, after, tag):
    n = len(full4s)
    after = list(after) if isinstance(after, (list, tuple)) else [after]

    def body(*refs):
        full, buf = refs[:n], refs[n:2 * n]
        send_sems, recv_sems, token = refs[2 * n + len(after)], refs[2 * n + len(after) + 1], refs[-1]
        c = lax.axis_index("c")
        for t in range(n):
            pltpu.make_async_remote_copy(src_ref=full[t].at[:, 1 - c], dst_ref=buf[t], send_sem=send_sems.at[t],
                                         recv_sem=recv_sems.at[t], device_id=_peer(1)[0], device_id_type=_MESH).start()
        token[...] = jnp.zeros_like(token)

    outs = pl.pallas_call(
        body, name=f"pair_start_{tag}", in_specs=[_HBM] * (2 * n) + [_ANY] * len(after),
        out_specs=(_SEM, _SEM, *[_HBM] * (2 * n), pl.BlockSpec(memory_space=pltpu.VMEM)),
        out_shape=(pltpu.SemaphoreType.DMA((n,)), pltpu.SemaphoreType.DMA((n,)),
                   *[pltpu.HBM(a.shape, a.dtype) for a in list(full4s) + list(bufs)], _TOKEN),
        input_output_aliases={t: 2 + t for t in range(2 * n)}, compiler_params=_CP_SPLIT)(
            *[_hbm(a) for a in list(full4s) + list(bufs)], *after)
    return outs[0], outs[1], list(outs[2:2 + n]), list(outs[2 + n:2 + 2 * n]), outs[-1]


def _pair_wait(full4s, bufs, send_sems, recv_sems, after, tag):
    n = len(full4s)
    after = list(after) if isinstance(after, (list, tuple)) else [after]

    def body(*refs):
        full, buf, send_sems, recv_sems = refs[:n], refs[n:2 * n], refs[2 * n], refs[2 * n + 1]
        c = lax.axis_index("c")
        for t in range(n):
            cp = pltpu.make_async_remote_copy(src_ref=full[t].at[:, 1 - c], dst_ref=buf[t], send_sem=send_sems.at[t],
                                              recv_sem=recv_sems.at[t], device_id=_peer(0)[0], device_id_type=_MESH)
            cp.wait_send()
            cp.wait_recv()

    outs = pl.pallas_call(
        body, name=f"pair_wait_{tag}", in_specs=[_HBM] * (2 * n) + [_SEM, _SEM] + [_ANY] * len(after),
        out_specs=tuple([_HBM] * (2 * n)),
        out_shape=tuple(pltpu.HBM(a.shape, a.dtype) for a in list(full4s) + list(bufs)),
        input_output_aliases={t: t for t in range(2 * n)}, compiler_params=_CP_SPLIT)(
            *full4s, *bufs, send_sems, recv_sems, *after)
    return list(outs[:n]), list(outs[n:])


def _pair_sum(core, full4s, bufs):
    n = len(full4s)

    def body(core_ref, *refs):
        for t in range(n):
            refs[2 * n + t][...] = (refs[t][...].astype(f32) + refs[n + t][...].astype(f32)).astype(bf16)

    grid_spec = pltpu.PrefetchScalarGridSpec(
        num_scalar_prefetch=1, grid=(4,),
        in_specs=[pl.BlockSpec((None, None) + a.shape[2:], lambda j, core_ref: (j, core_ref[0], 0, 0)) for a in full4s]
        + [pl.BlockSpec((None,) + b.shape[1:], lambda j, core_ref: (j, 0, 0)) for b in bufs],
        out_specs=[pl.BlockSpec((None,) + b.shape[1:], lambda j, core_ref: (j, 0, 0)) for b in bufs])
    return pl.pallas_call(
        body, grid_spec=grid_spec, out_shape=[SDS(b.shape, bf16) for b in bufs],
        compiler_params=_CP, name="pair_sum")(core, *full4s, *bufs)


def _chip_start(sums, slots, after, tag):
    n = len(sums)
    after = list(after) if isinstance(after, (list, tuple)) else [after]

    def body(*refs):
        src, slot = refs[:n], refs[n:2 * n]
        send_sems, recv_sems, token = refs[2 * n + len(after)], refs[2 * n + len(after) + 1], refs[-1]
        my_chip = 2 * lax.axis_index("x") + lax.axis_index("y")
        for k, mask in enumerate((4, 2, 6)):
            dev, _ = _peer(mask)
            for t in range(n):
                pltpu.make_async_remote_copy(
                    src_ref=src[t].at[2 * dev[0] + dev[1]], dst_ref=slot[t].at[my_chip],
                    send_sem=send_sems.at[k * n + t], recv_sem=recv_sems.at[k * n + t],
                    device_id=dev, device_id_type=_MESH).start()
        token[...] = jnp.zeros_like(token)

    outs = pl.pallas_call(
        body, name=f"chip_start_{tag}", in_specs=[_HBM] * (2 * n) + [_ANY] * len(after),
        out_specs=(_SEM, _SEM, *[_HBM] * (2 * n), pl.BlockSpec(memory_space=pltpu.VMEM)),
        out_shape=(pltpu.SemaphoreType.DMA((3 * n,)), pltpu.SemaphoreType.DMA((3 * n,)),
                   *[pltpu.HBM(a.shape, a.dtype) for a in list(sums) + list(slots)], _TOKEN),
        input_output_aliases={t: 2 + t for t in range(2 * n)}, compiler_params=_CP_SPLIT)(
            *[_hbm(a) for a in list(sums) + list(slots)], *after)
    return outs[0], outs[1], list(outs[2:2 + n]), list(outs[2 + n:2 + 2 * n]), outs[-1]


def _chip_wait(sums, slots, send_sems, recv_sems, after, tag):
    n = len(sums)
    after = list(after) if isinstance(after, (list, tuple)) else [after]

    def body(*refs):
        src, slot, send_sems, recv_sems = refs[:n], refs[n:2 * n], refs[2 * n], refs[2 * n + 1]
        for k, mask in enumerate((4, 2, 6)):
            dev, _ = _peer(mask)
            chip = 2 * dev[0] + dev[1]
            for t in range(n):
                cp = pltpu.make_async_remote_copy(
                    src_ref=src[t].at[chip], dst_ref=slot[t].at[chip],
                    send_sem=send_sems.at[k * n + t], recv_sem=recv_sems.at[k * n + t],
                    device_id=_peer(0)[0], device_id_type=_MESH)
                cp.wait_send()
                cp.wait_recv()

    outs = pl.pallas_call(
        body, name=f"chip_wait_{tag}", in_specs=[_HBM] * (2 * n) + [_SEM, _SEM] + [_ANY] * len(after),
        out_specs=tuple([_HBM] * (2 * n)),
        out_shape=tuple(pltpu.HBM(a.shape, a.dtype) for a in list(sums) + list(slots)),
        input_output_aliases={t: t for t in range(2 * n)}, compiler_params=_CP_SPLIT)(
            *sums, *slots, send_sems, recv_sems, *after)
    return list(outs[:n]), list(outs[n:])


def _sum_slots(slots, rb):
    r = slots.shape[1]

    def body(s_ref, o_ref):
        acc = s_ref[0].astype(f32)
        for s in range(1, NDEV):
            acc = acc + s_ref[s].astype(f32)
        o_ref[...] = acc

    return pl.pallas_call(
        body, grid=(r // rb,),
        in_specs=[pl.BlockSpec((NDEV, rb, D), lambda i: (0, i, 0))],
        out_specs=pl.BlockSpec((rb, D), lambda i: (i, 0)),
        out_shape=SDS((r, D), f32), compiler_params=_CP, name="sum_slots")(slots)


def _adamw(w, g, m, v):
    shape = w.shape
    cols = shape[-1]
    rows = w.size // cols
    rb = rows
    for cand in (512, 256, 128, 64, 32, 16, 8):
        if rows % cand == 0 and rows > cand:
            rb = cand
            break

    def body(w_ref, g_ref, m_ref, v_ref, d_ref, mo_ref, vo_ref):
        d_ref[...], mo_ref[...], vo_ref[...] = _adamw_math(w_ref[...], g_ref[...], m_ref[...], v_ref[...])

    spec = pl.BlockSpec((rb, cols), lambda i: (i, 0))
    outs = pl.pallas_call(
        body, grid=(rows // rb,), in_specs=[spec] * 4, out_specs=[spec] * 3,
        out_shape=[SDS((rows, cols), f32)] * 3, compiler_params=_CP, name="adamw")(
            *(a.reshape(rows, cols) for a in (w, g, m, v)))
    return tuple(o.reshape(shape) for o in outs)


def _adamw_math(w, g, m, v):
    m = ADAM_B1 * m + (1.0 - ADAM_B1) * g
    v = ADAM_B2 * v + (1.0 - ADAM_B2) * (g * g)
    m_hat = m / (1.0 - ADAM_B1 ** ADAM_STEP)
    v_hat = v / (1.0 - ADAM_B2 ** ADAM_STEP)
    return -ADAM_LR * (m_hat / (jnp.sqrt(v_hat) + ADAM_EPS) + ADAM_WD * w), m, v


def _reduce_adamw(acc, me, full, slots, w, m, v, l):
    _, r, _ = w.shape
    ns = slots.shape[0]
    rb = r // 2 if r > 128 else r

    def body(me_ref, full_ref, slots_ref, w_ref, m_ref, v_ref, *refs):
        go_ref, d_ref, mo_ref, vo_ref = refs[-4:]
        own = full_ref[...].astype(f32)
        g = None
        for s in range(ns):
            part = jnp.where(me_ref[0] == s, own, slots_ref[s].astype(f32))
            g = part if g is None else g + part
        go_ref[...] = g
        d_ref[...], mo_ref[...], vo_ref[...] = _adamw_math(w_ref[...], g, m_ref[...], v_ref[...])

    steps = r // rb
    lay = pl.BlockSpec((None, rb, D), lambda i, me_ref: (l, i, 0))
    n_acc = 0 if acc is None else 4
    grid_spec = pltpu.PrefetchScalarGridSpec(
        num_scalar_prefetch=1, grid=(steps,),
        in_specs=[pl.BlockSpec((rb, D), lambda i, me_ref: (me_ref[0] * steps + i, 0)),
                  pl.BlockSpec((ns, rb, D), lambda i, me_ref: (0, i, 0)), lay, lay, lay] + [_ANY] * n_acc,
        out_specs=[lay] * 4)
    outs = pl.pallas_call(
        body, grid_spec=grid_spec, out_shape=[SDS(w.shape, f32)] * 4,
        input_output_aliases={6 + j: j for j in range(n_acc)},
        compiler_params=_CP, name="reduce_adamw")(me, full, slots, w, m, v, *(() if acc is None else acc))
    return tuple(outs)


_BIG = ("ffn1_w_gate", "ffn1_w_up", "ffn1_w_down", "w_in", "w_out", "ffn2_w_gate", "ffn2_w_up", "ffn2_w_down")
_TRANSPOSED = ("ffn1_w_gate", "ffn1_w_up", "w_in", "ffn2_w_gate", "ffn2_w_up")

def _block_diag(pool_w):
    out = jnp.zeros((L, PW, PW), pool_w.dtype)
    for gi in range(4):
        out = out.at[:, 64 * gi:64 * (gi + 1), 64 * gi:64 * (gi + 1)].set(pool_w[:, gi])
    return out


def kernel(x, positions, ffn1_norm, ffn1_w_gate, ffn1_w_up, ffn1_w_down, mix_norm, w_in, pool_w, pool_scale, w_out, ffn2_norm, ffn2_w_gate, ffn2_w_up, ffn2_w_down, final_norm, loss_target, m_ffn1_norm, m_ffn1_w_gate, m_ffn1_w_up, m_ffn1_w_down, m_mix_norm, m_w_in, m_pool_w, m_pool_scale, m_w_out, m_ffn2_norm, m_ffn2_w_gate, m_ffn2_w_up, m_ffn2_w_down, m_final_norm, v_ffn1_norm, v_ffn1_w_gate, v_ffn1_w_up, v_ffn1_w_down, v_mix_norm, v_w_in, v_pool_w, v_pool_scale, v_w_out, v_ffn2_norm, v_ffn2_w_gate, v_ffn2_w_up, v_ffn2_w_down, v_final_norm):
    weights = dict(ffn1_norm=ffn1_norm, ffn1_w_gate=ffn1_w_gate, ffn1_w_up=ffn1_w_up, ffn1_w_down=ffn1_w_down,
                   mix_norm=mix_norm, w_in=w_in, pool_w=pool_w, pool_scale=pool_scale, w_out=w_out,
                   ffn2_norm=ffn2_norm, ffn2_w_gate=ffn2_w_gate, ffn2_w_up=ffn2_w_up, ffn2_w_down=ffn2_w_down,
                   final_norm=final_norm)
    moms = dict(ffn1_norm=m_ffn1_norm, ffn1_w_gate=m_ffn1_w_gate, ffn1_w_up=m_ffn1_w_up, ffn1_w_down=m_ffn1_w_down,
                mix_norm=m_mix_norm, w_in=m_w_in, pool_w=m_pool_w, pool_scale=m_pool_scale, w_out=m_w_out,
                ffn2_norm=m_ffn2_norm, ffn2_w_gate=m_ffn2_w_gate, ffn2_w_up=m_ffn2_w_up, ffn2_w_down=m_ffn2_w_down,
                final_norm=m_final_norm)
    vels = dict(ffn1_norm=v_ffn1_norm, ffn1_w_gate=v_ffn1_w_gate, ffn1_w_up=v_ffn1_w_up, ffn1_w_down=v_ffn1_w_down,
                mix_norm=v_mix_norm, w_in=v_w_in, pool_w=v_pool_w, pool_scale=v_pool_scale, w_out=v_w_out,
                ffn2_norm=v_ffn2_norm, ffn2_w_gate=v_ffn2_w_gate, ffn2_w_up=v_ffn2_w_up, ffn2_w_down=v_ffn2_w_down,
                final_norm=v_final_norm)
    names = list(weights)

    me_idx = 4 * lax.axis_index("x") + 2 * lax.axis_index("y") + lax.axis_index("c")
    me_arr = me_idx.reshape(1).astype(jnp.int32)

    tr = lambda w: jnp.swapaxes(w, 1, 2).astype(bf16)
    shards = [tr(weights[nm]) if nm in _TRANSPOSED else weights[nm].astype(bf16) for nm in _BIG]

    def landing_zones(l, which):
        return _place_own(me_arr, [shards[t] for t in which], l)

    g_ffn1 = [ffn1_norm[l].reshape(1, D) for l in range(L)]
    g_mix = [mix_norm[l].reshape(1, D) for l in range(L)]
    g_ffn2 = [ffn2_norm[l].reshape(1, D) for l in range(L)]
    wbd_all = _block_diag(pool_w).astype(bf16)
    wbd = [wbd_all[l] for l in range(L)]
    pscale = [pool_scale[l].reshape(1, PW) for l in range(L)]
    tabs = _rope_tables(positions)
    flat = lambda a: a.reshape(S, a.shape[-1])
    r4 = lambda a: a.reshape(4, S // 4, a.shape[-1])
    r16 = lambda a: a.reshape(16, S // 16, a.shape[-1])

    first, rest, whole = (0, 1, 2, 3), (4, 5, 6, 7), tuple(range(8))

    def ag_begin(l, which, after):
        tag = f"{l}{'' if which == whole else 'r'}"
        send_sems, recv_sems, zones, token = _ag_start(landing_zones(l, which), after, tag)
        return dict(tag=tag, zones=zones, s=send_sems, r=recv_sems), token

    def ag_second(ch, after):
        ch["ps"], ch["pr"], ch["zones"], token = _ag_pass(ch["zones"], ch["r"], after, ch["tag"])
        return token

    def ag_third(ch, after):
        ch["qs"], ch["qr"], ch["zones"], token = _ag_last(ch["zones"], ch["pr"], after, ch["tag"])
        return token

    def ag_end(ch, after):
        return _ag_wait(ch["zones"], ch["s"], ch["r"], ch["ps"], ch["pr"], ch["qs"], ch["qr"], after, ch["tag"])

    head = _all_gather(landing_zones(0, first))
    ch_rest, tok_rest = ag_begin(0, rest, head[0])
    chains = {}
    chains[1], tok_next = ag_begin(1, whole, head[0])
    gathered = [None] * L
    xs = x.reshape(S, D)
    saved = []
    for l in range(L):
        ga, gb = g_ffn1[l], g_ffn2[l]
        if l == 0:
            gt1, ut1, dn1, wint = head
            ga = ga + tok_rest[0, 0] + tok_next[0, 0]
        else:
            gt1, ut1, dn1, wint, wout, gt2, ut2, dn2 = gathered[l]
        x0 = xs
        x1, gate1, up1 = _ffn_fwd(x0, ga, gt1, ut1, dn1)
        hmix, vp, q1, k1, v1, q4, k4, v4, q16, k16, v16 = _mix_in_fwd(x1, g_mix[l], wint, tabs)
        q4, k4, v4, q16, k16, v16 = map(flat, (q4, k4, v4, q16, k16, v16))
        ypool, diff = _pool_fwd(vp, wbd[l], pscale[l])
        after_attn = None
        if l == 0:
            after_attn = ag_second(ch_rest, [ypool, q16])
        o1, l1 = _attn_fwd(q1, k1, v1, S, after=after_attn)
        o4, l4 = _attn_fwd(q4, k4, v4, S // 4, after=after_attn)
        o16, l16 = _attn_fwd(q16, k16, v16, S // 16, after=after_attn)
        if l == 0:
            token = ag_third(ch_rest, [o1, o4, o16])
            wout, gt2, ut2, dn2 = ag_end(ch_rest, token)
            gathered[0] = list(head) + [wout, gt2, ut2, dn2]
        elif l + 1 < L:
            gb = gb + ag_second(chains[l + 1], [o1, o4, o16])[0, 0]
        x2, mixed, o, lse1, lse4, lse16 = _mix_out_fwd(x1, ypool, o1, l1, r4(o4), r4(l4), r16(o16), r16(l16), wout)
        if l == 0:
            gb = gb + ag_second(chains[1], x2)[0, 0]
        x3, gate2, up2 = _ffn_fwd(x2, gb, gt2, ut2, dn2)
        if l + 1 < L:
            token = ag_third(chains[l + 1], x3)
            if l + 2 < L:
                chains[l + 2], token = ag_begin(l + 2, whole, token)
            gathered[l + 1] = ag_end(chains[l + 1], token)
        saved.append(dict(x0=x0, x1=x1, x2=x2, gate1=gate1, up1=up1, gate2=gate2, up2=up2, hmix=hmix, diff=diff,
                          qkv=((q1, k1, v1), (q4, k4, v4), (q16, k16, v16)), mixed=mixed, o=o,
                          lse=(lse1, flat(lse4), flat(lse16))))
        xs = x3

    dx, loss_part, d_final = _loss_head(xs, final_norm.reshape(1, D), loss_target.reshape(S, D))

    d_norm = {nm: [None] * L for nm in ("ffn1_norm", "mix_norm", "ffn2_norm")}
    d_poolw, d_pscale = [None] * L, [None] * L
    group_a = ("ffn2_w_gate", "ffn2_w_up", "ffn2_w_down", "w_out")
    group_b = ("ffn1_w_gate", "ffn1_w_up", "ffn1_w_down", "w_in")
    acc = {}

    as_rows = lambda a, nm: jnp.swapaxes(a, 1, 2) if nm in _TRANSPOSED else a
    w_rows = {nm: as_rows(weights[nm], nm) for nm in _BIG}
    m_rows = {nm: as_rows(moms[nm], nm) for nm in _BIG}
    v_rows = {nm: as_rows(vels[nm], nm) for nm in _BIG}

    def exchange(full, group, after, tag):
        srcs = [full[nm] for nm in group]
        slots = [lax.empty((NDEV, g.shape[0] // NDEV, D), bf16) for g in srcs]
        ssem, rsem, srcs, slots, token = _rs_start(srcs, slots, after, tag)
        return (srcs, slots, ssem, rsem, tag), token

    def update(l, group, flight, after):
        srcs, slots, ssem, rsem, tag = flight
        srcs, slots = _rs_wait(srcs, slots, ssem, rsem, after, tag)
        for nm, full_g, slots_g in zip(group, srcs, slots):
            acc[nm] = _reduce_adamw(acc.get(nm), me_arr, full_g, slots_g, w_rows[nm], m_rows[nm], v_rows[nm], l)
        return [acc[nm][0] for nm in group], slots

    core_arr = lax.axis_index("c").reshape(1).astype(jnp.int32)
    chip_arr = (2 * lax.axis_index("x") + lax.axis_index("y")).reshape(1).astype(jnp.int32)

    def exchange_cores(full, group, after, tag):
        full4s = [full[nm].reshape(4, 2, full[nm].shape[0] // NDEV, D) for nm in group]
        bufs = [lax.empty((4,) + a.shape[2:], bf16) for a in full4s]
        ssem, rsem, full4s, bufs, token = _pair_start(full4s, bufs, after, tag)
        return (full4s, bufs, ssem, rsem, tag), token

    def exchange_chips(flight, after):
        full4s, bufs, ssem, rsem, tag = flight
        full4s, bufs = _pair_wait(full4s, bufs, ssem, rsem, after, tag)
        sums = _pair_sum(core_arr, full4s, bufs)
        slots = [lax.empty(a.shape, bf16) for a in sums]
        ssem, rsem, sums, slots, token = _chip_start(sums, slots, bufs[0], tag)
        return (sums, slots, ssem, rsem, tag), token

    def update_chips(l, group, flight, after):
        sums, slots, ssem, rsem, tag = flight
        sums, slots = _chip_wait(sums, slots, ssem, rsem, after, tag)
        for nm, sums_g, slots_g in zip(group, sums, slots):
            own = sums_g.reshape(4 * sums_g.shape[1], D)
            acc[nm] = _reduce_adamw(acc.get(nm), chip_arr, own, slots_g, w_rows[nm], m_rows[nm], v_rows[nm], l)
        return [acc[nm][0] for nm in group]

    flights = {}
    token_b = None
    for l in reversed(range(L)):
        sv = saved[l]
        gt1, ut1, dn1, wint, wout, gt2, ut2, dn2 = gathered[l]
        gb = g_ffn2[l] if token_b is None else g_ffn2[l] + token_b[0, 0]
        full = {}
        dx, dgate, dup, h, dy, d_norm["ffn2_norm"][l] = _ffn_bwd_d(sv["x2"], gb, sv["gate2"], sv["up2"], dx, gt2, ut2, dn2)
        full["ffn2_w_gate"], full["ffn2_w_up"], full["ffn2_w_down"] = _ffn_bwd_w(h, dy, sv["gate2"], sv["up2"], dgate, dup)

        dxb, dyp, do1, do4, do16, dl1, dl4, dl16 = _mix_out_bwd(dx, sv["o"], wout)
        full["w_out"] = _wgrad(sv["mixed"], dxb)
        flights[l, "a"], token_a = (exchange_cores if l == 0 else exchange)(full, group_a, dxb, f"a{l}")
        dvp, dwbd, d_pscale[l] = _pool_bwd(dyp, sv["diff"], wbd[l], pscale[l] + token_a[0, 0])
        d_poolw[l] = jnp.stack([dwbd[64 * gi:64 * (gi + 1), 64 * gi:64 * (gi + 1)] for gi in range(4)])
        dos, dls = (do1, flat(do4), flat(do16)), (dl1, flat(dl4), flat(dl16))
        dqkv = []
        for b, lc in enumerate((S, S // 4, S // 16)):
            qb, kb, vb = sv["qkv"][b]
            dqkv.append(_attn_bwd(qb, kb, vb, dos[b], sv["lse"][b], dls[b], lc))
        d4 = tuple(r4(a) for a in dqkv[1])
        d16 = tuple(r16(a) for a in dqkv[2])
        gm = g_mix[l]
        if l == 0:
            flights[0, "a"], token_a = exchange_chips(flights[0, "a"], [dqkv[0][0], dqkv[1][0], dqkv[2][0]])
            gm = gm + token_a[0, 0]
        dx, dproj, d_norm["mix_norm"][l] = _mix_in_bwd(dx, sv["x1"], gm, wint, tabs, dvp, dqkv[0], d4, d16)
        full["w_in"] = _wgrad(dproj, sv["hmix"])

        dx, dgate, dup, h, dy, d_norm["ffn1_norm"][l] = _ffn_bwd_d(sv["x0"], g_ffn1[l], sv["gate1"], sv["up1"], dx, gt1, ut1, dn1)
        full["ffn1_w_gate"], full["ffn1_w_up"], full["ffn1_w_down"] = _ffn_bwd_w(h, dy, sv["gate1"], sv["up1"], dgate, dup)

        after = dx
        if l + 1 < L and l + 1 >= 2:
            after, _ = update(l + 1, group_a, flights.pop((l + 1, "a")), after)
            after, _ = update(l + 1, group_b, flights.pop((l + 1, "b")), after)
        if l > 0:
            flights[l, "b"], token_b = exchange(full, group_b, after, f"b{l}")

    flights[0, "b"], token_b = exchange_cores(full, group_b, dx, "b0")
    flights[0, "b"], token_b = exchange_chips(flights[0, "b"], token_b)
    pad8 = lambda a: jnp.pad(a, ((0, 8 - a.shape[0]), (0, 0)))
    misc = jnp.concatenate([d_final, jnp.concatenate(d_pscale, axis=1), loss_part], axis=0)
    small = jnp.concatenate(
        [pad8(jnp.concatenate(d_norm[nm], axis=0)) for nm in ("ffn1_norm", "mix_norm", "ffn2_norm")]
        + [pad8(misc), jnp.stack(d_poolw).reshape(L * 16, D)], axis=0)
    small_slots = lax.dynamic_update_slice(lax.empty((NDEV, SMALL_ROWS, D), f32), small[None], (me_idx, 0, 0))
    pack_sems = _rs_start([small], [small_slots], token_b, "pack")

    after = pack_sems[-1]
    for key in [(1, "a"), (1, "b")]:
        after, _ = update(key[0], group_a if key[1] == "a" else group_b, flights.pop(key), after)
    after = update_chips(0, group_a, flights.pop((0, "a")), after)
    after = update_chips(0, group_b, flights.pop((0, "b")), after)
    _, pack_slots = _rs_wait(pack_sems[2], pack_sems[3], pack_sems[0], pack_sems[1], after, "pack")

    sm = _sum_slots(pack_slots[0], SMALL_ROWS)
    grads = {}
    grads["ffn1_norm"], grads["mix_norm"], grads["ffn2_norm"] = sm[0:L], sm[8:8 + L], sm[16:16 + L]
    grads["final_norm"] = sm[24]
    grads["pool_scale"] = sm[25].reshape(L, PW)
    grads["pool_w"] = sm[32:32 + L * 16].reshape(L, 4, 64, 64)
    loss = sm[26, 0]
    upd = {nm: _adamw(weights[nm], grads[nm], moms[nm], vels[nm]) for nm in names if nm not in _BIG}
    for nm in _BIG:
        grads[nm], upd[nm] = as_rows(acc[nm][0], nm), tuple(as_rows(a, nm) for a in acc[nm][1:])
    return (loss, dx.reshape(1, S, D), *[grads[nm] for nm in names], *[upd[nm][0] for nm in names],
            *[upd[nm][1] for nm in names], *[upd[nm][2] for nm in names])
```

```python
import jax
import jax.numpy as jnp
from jax import lax
from jax.experimental import pallas as pl
from jax.experimental.pallas import tpu as pltpu

f32 = jnp.float32
bf16 = jnp.bfloat16
SDS = jax.ShapeDtypeStruct

D = 1024
S = 2048
F = 2816
L = 4
PW = 256
AW = 768
PROJ = PW + 3 * AW
NDEV = 8
TM = 256
QB = 128
HALF = 64
NG = AW // 128
NORM_EPS = 1e-6
MASK_VALUE = -1e30
ROPE_THETA = 500000.0
ADAM_LR, ADAM_B1, ADAM_B2, ADAM_EPS, ADAM_WD, ADAM_STEP = 0.001, 0.9, 0.999, 1e-08, 0.01, 10
POOL_WINDOWS = (2, 4, 8, 16)
PAD = 8
SMALL_ROWS = 96
VMEM_LIMIT = 56 * 1024 * 1024

_CP = pltpu.CompilerParams(vmem_limit_bytes=VMEM_LIMIT)
_ANY = pl.BlockSpec(memory_space=pl.ANY)
_HBM = pl.BlockSpec(memory_space=pltpu.HBM)
_SEM = pl.BlockSpec(memory_space=pltpu.SEMAPHORE)
_MESH = pl.DeviceIdType.MESH
_CP_SPLIT = pltpu.CompilerParams(has_side_effects=pltpu.SideEffectType.DATAFLOW_SIDE_EFFECTING)


def _dot_nn(a, b):
    return lax.dot_general(a, b, (((1,), (0,)), ((), ())), preferred_element_type=f32)


def _dot_nt(a, b):
    return lax.dot_general(a, b, (((1,), (1,)), ((), ())), preferred_element_type=f32)


def _dot_tn(a, b):
    return lax.dot_general(a, b, (((0,), (0,)), ((), ())), preferred_element_type=f32)


def _rms(x, g):
    r = lax.rsqrt(jnp.mean(x * x, axis=-1, keepdims=True) + NORM_EPS)
    xh = x * r
    return r, xh, xh * g


def _rms_bwd(dh, r, xh, g):
    dxh = dh * g
    return r * (dxh - xh * jnp.mean(dxh * xh, axis=-1, keepdims=True))


def _tile(cols):
    return pl.BlockSpec((TM, cols), lambda i: (i, 0))


def _const(shape):
    return pl.BlockSpec(shape, lambda i: (0,) * len(shape))


def _layer(rows, cols, l=None):
    return pl.BlockSpec((rows, cols), lambda i: (0, 0), pipeline_mode=pl.Buffered(1))


def _p4(cols=AW):
    return pl.BlockSpec((4, TM // 4, cols), lambda i: (0, i, 0))


def _p16(cols=AW):
    return pl.BlockSpec((16, TM // 16, cols), lambda i: (0, i, 0))


def _cols(j):
    return slice(128 * j, 128 * (j + 1))


def _follow(body, n_in, after):
    k = len(after)
    return body if k == 0 else (lambda *refs: body(*refs[:n_in], *refs[n_in + k:]))


def _ffn_fwd(x, g, gt, ut, dn, after=()):
    def body(x_ref, g_ref, gt_ref, ut_ref, dn_ref, xo_ref, gate_ref, up_ref):
        x = x_ref[...]
        _, _, hn = _rms(x, g_ref[...])
        h = hn.astype(bf16)
        gate = _dot_nt(h, gt_ref[...])
        up = _dot_nt(h, ut_ref[...])
        gate_ref[...] = gate.astype(bf16)
        up_ref[...] = up.astype(bf16)
        a = (gate * jax.nn.sigmoid(gate) * up).astype(bf16)
        xo_ref[...] = x + 0.5 * _dot_nn(a, dn_ref[...])

    return pl.pallas_call(
        _follow(body, 5, after), grid=(S // TM,),
        in_specs=[_tile(D), _layer(1, D), _layer(F, D), _layer(F, D), _layer(F, D)] + [_ANY] * len(after),
        out_specs=[_tile(D), _tile(F), _tile(F)],
        out_shape=[SDS((S, D), f32), SDS((S, F), bf16), SDS((S, F), bf16)],
        compiler_params=_CP, name="ffn_fwd")(x, g, gt, ut, dn, *after)


def _ffn_bwd_d(x, g, gate, up, dxo, gt, ut, dn, after=()):
    def body(x_ref, g_ref, gate_ref, up_ref, dxo_ref, gt_ref, ut_ref, dn_ref,
             dx_ref, dgate_ref, dup_ref, h_ref, dy_ref, dg_ref):
        x = x_ref[...]
        g = g_ref[...]
        r, xh, hn = _rms(x, g)
        h_ref[...] = hn.astype(bf16)
        dxo = dxo_ref[...]
        dy = (0.5 * dxo).astype(bf16)
        dy_ref[...] = dy
        da = _dot_nt(dy, dn_ref[...])
        gate = gate_ref[...].astype(f32)
        up = up_ref[...].astype(f32)
        sg = jax.nn.sigmoid(gate)
        dgate = (da * up * (sg * (1.0 + gate * (1.0 - sg)))).astype(bf16)
        dup = (da * (gate * sg)).astype(bf16)
        dgate_ref[...] = dgate
        dup_ref[...] = dup
        dh = _dot_nn(dgate, gt_ref[...]) + _dot_nn(dup, ut_ref[...])

        @pl.when(pl.program_id(0) == 0)
        def _():
            dg_ref[...] = jnp.zeros_like(dg_ref)

        dg_ref[...] += jnp.sum(dh * xh, axis=0, keepdims=True)
        dx_ref[...] = dxo + _rms_bwd(dh, r, xh, g)

    return pl.pallas_call(
        _follow(body, 8, after), grid=(S // TM,),
        in_specs=[_tile(D), _layer(1, D), _tile(F), _tile(F), _tile(D),
                  _layer(F, D), _layer(F, D), _layer(F, D)] + [_ANY] * len(after),
        out_specs=[_tile(D), _tile(F), _tile(F), _tile(D), _tile(D), _const((1, D))],
        out_shape=[SDS((S, D), f32), SDS((S, F), bf16), SDS((S, F), bf16), SDS((S, D), bf16),
                   SDS((S, D), bf16), SDS((1, D), f32)],
        compiler_params=_CP, name="ffn_bwd_d")(x, g, gate, up, dxo, gt, ut, dn, *after)


def _ffn_bwd_w(h, dy, gate, up, dgate, dup):
    fc = 256

    def body(h_ref, dy_ref, gate_ref, up_ref, dgate_ref, dup_ref, dgt_ref, dut_ref, ddn_ref):
        gate = gate_ref[...].astype(f32)
        a = (gate * jax.nn.sigmoid(gate) * up_ref[...].astype(f32)).astype(bf16)
        ddn_ref[...] = _dot_tn(a, dy_ref[...]).astype(bf16)
        h = h_ref[...]
        dgt_ref[...] = _dot_tn(dgate_ref[...], h).astype(bf16)
        dut_ref[...] = _dot_tn(dup_ref[...], h).astype(bf16)

    col = pl.BlockSpec((S, fc), lambda j: (0, j))
    row = pl.BlockSpec((fc, D), lambda j: (j, 0))
    full = pl.BlockSpec((S, D), lambda j: (0, 0))
    return pl.pallas_call(
        body, grid=(F // fc,),
        in_specs=[full, full, col, col, col, col],
        out_specs=[row, row, row],
        out_shape=[SDS((F, D), bf16)] * 3,
        compiler_params=_CP, name="ffn_bwd_w")(h, dy, gate, up, dgate, dup)


def _wgrad(a, b):
    m, n = a.shape[1], b.shape[1]
    mc = 256

    def body(a_ref, b_ref, o_ref):
        o_ref[...] = _dot_tn(a_ref[...], b_ref[...]).astype(bf16)

    return pl.pallas_call(
        body, grid=(m // mc,),
        in_specs=[pl.BlockSpec((S, mc), lambda j: (0, j)), pl.BlockSpec((S, n), lambda j: (0, 0))],
        out_specs=pl.BlockSpec((mc, n), lambda j: (j, 0)),
        out_shape=SDS((m, n), bf16),
        compiler_params=_CP, name="wgrad")(a, b)


def _rope(t, c, sn, sp):
    return t * c + pltpu.roll(t, 120, 1) * sn + pltpu.roll(t, 8, 1) * sp


def _rope_bwd(d, c, sn, sp):
    return d * c + pltpu.roll(d * sn, 8, 1) + pltpu.roll(d * sp, 120, 1)


def _rope_tables(positions):
    inv_freq = ROPE_THETA ** (-jnp.arange(0, 16, 2, dtype=f32) / 16)
    ang = positions.reshape(S, 1).astype(f32) * inv_freq
    cos, sin = jnp.cos(ang), jnp.sin(ang)
    one = jnp.ones((S, 48), f32)
    zero8 = jnp.zeros((S, 8), f32)
    zero48 = jnp.zeros((S, 48), f32)
    c = jnp.concatenate([cos, cos, one], axis=1)
    sn = jnp.concatenate([-sin, zero8, zero48], axis=1)
    sp = jnp.concatenate([zero8, sin, zero48], axis=1)
    return tuple(jnp.concatenate([t, t], axis=1) for t in (c, sn, sp))


def _dilation_perm(n, back=False):
    per = TM // n
    i = lax.broadcasted_iota(jnp.int32, (TM, TM), 1 if back else 0)
    j = lax.broadcasted_iota(jnp.int32, (TM, TM), 0 if back else 1)
    return jnp.where(j == n * (i % per) + i // per, 1.0, 0.0).astype(bf16)


def _mix_in_fwd(x, g, wint, tabs):
    def body(x_ref, g_ref, w_ref, c_ref, sn_ref, sp_ref,
             h_ref, vp_ref, q1, k1, v1, q4, k4, v4, q16, k16, v16):
        _, _, hn = _rms(x_ref[...], g_ref[...])
        h = hn.astype(bf16)
        h_ref[...] = h
        proj = _dot_nt(h, w_ref[...])
        vp_ref[...] = proj[:, :PW]
        c, sn, sp = c_ref[...], sn_ref[...], sp_ref[...]
        perm4, perm16 = _dilation_perm(4), _dilation_perm(16)
        for kind, (o1, o4, o16) in enumerate(((q1, q4, q16), (k1, k4, k16), (v1, v4, v16))):
            for j in range(NG):
                t = proj[:, PW + kind * AW + 128 * j: PW + kind * AW + 128 * (j + 1)]
                if kind == 0:
                    t = _rope(t, c, sn, sp) * 0.125
                elif kind == 1:
                    t = _rope(t, c, sn, sp)
                o1[:, _cols(j)] = t.astype(bf16)
            nat = o1[...]
            o4[...] = _dot_nn(perm4, nat).astype(bf16).reshape(4, TM // 4, AW)
            o16[...] = _dot_nn(perm16, nat).astype(bf16).reshape(16, TM // 16, AW)

    nat, d4, d16 = SDS((S, AW), bf16), SDS((4, S // 4, AW), bf16), SDS((16, S // 16, AW), bf16)
    return pl.pallas_call(
        body, grid=(S // TM,),
        in_specs=[_tile(D), _layer(1, D), _layer(PROJ, D), _tile(128), _tile(128), _tile(128)],
        out_specs=[_tile(D), _tile(PW)] + [_tile(AW)] * 3 + [_p4()] * 3 + [_p16()] * 3,
        out_shape=[SDS((S, D), bf16), SDS((S, PW), f32)] + [nat] * 3 + [d4] * 3 + [d16] * 3,
        compiler_params=_CP, name="mix_in_fwd")(x, g, wint, *tabs)


def _mix_in_bwd(dxo, x, g, wint, tabs, dvp, d1, d4, d16, after=()):
    def body(dxo_ref, x_ref, g_ref, w_ref, c_ref, sn_ref, sp_ref, dvp_ref,
             dq1, dk1, dv1, dq4, dk4, dv4, dq16, dk16, dv16,
             dx_ref, dproj_ref, dg_ref):
        c, sn, sp = c_ref[...], sn_ref[...], sp_ref[...]
        dproj_ref[:, :PW] = dvp_ref[...].astype(bf16)
        back4, back16 = _dilation_perm(4, True), _dilation_perm(16, True)
        for kind, (a1, a4, a16) in enumerate(((dq1, dq4, dq16), (dk1, dk4, dk16), (dv1, dv4, dv16))):
            n4 = _dot_nn(back4, a4[...].reshape(TM, AW))
            n16 = _dot_nn(back16, a16[...].reshape(TM, AW))
            for j in range(NG):
                t = a1[:, _cols(j)].astype(f32) + n4[:, _cols(j)] + n16[:, _cols(j)]
                if kind == 0:
                    t = _rope_bwd(t * 0.125, c, sn, sp)
                elif kind == 1:
                    t = _rope_bwd(t, c, sn, sp)
                dproj_ref[:, PW + kind * AW + 128 * j: PW + kind * AW + 128 * (j + 1)] = t.astype(bf16)
        g = g_ref[...]
        r_, xh, _ = _rms(x_ref[...], g)
        dh = _dot_nn(dproj_ref[...], w_ref[...])

        @pl.when(pl.program_id(0) == 0)
        def _():
            dg_ref[...] = jnp.zeros_like(dg_ref)

        dg_ref[...] += jnp.sum(dh * xh, axis=0, keepdims=True)
        dx_ref[...] = dxo_ref[...] + _rms_bwd(dh, r_, xh, g)

    return pl.pallas_call(
        _follow(body, 17, after), grid=(S // TM,),
        in_specs=[_tile(D), _tile(D), _layer(1, D), _layer(PROJ, D), _tile(128), _tile(128), _tile(128),
                  _tile(PW)] + [_tile(AW)] * 3 + [_p4()] * 3 + [_p16()] * 3 + [_ANY] * len(after),
        out_specs=[_tile(D), _tile(PROJ), _const((1, D))],
        out_shape=[SDS((S, D), f32), SDS((S, PROJ), bf16), SDS((1, D), f32)],
        compiler_params=_CP, name="mix_in_bwd")(dxo, x, g, wint, *tabs, dvp, *d1, *d4, *d16, *after)


def _pool_sums(pad_ref, base, rows, adjoint):
    lane_group = lax.broadcasted_iota(jnp.int32, (rows, PW), 1) // 64
    sign = -1 if adjoint else 1

    def sh(o):
        return pad_ref[pl.ds(PAD + base + sign * o, rows), :]

    out = None
    acc = None
    lo, hi = 0, 0
    for gi, w in enumerate(POOL_WINDOWS):
        for o in list(range(-(w // 2), lo)) + list(range(hi, w - w // 2)):
            acc = sh(o) if acc is None else acc + sh(o)
        lo, hi = -(w // 2), w - w // 2
        out = acc if out is None else jnp.where(lane_group >= gi, acc, out)
    return out


def _pool_counts(base, rows):
    pos = base + lax.broadcasted_iota(jnp.int32, (rows, PW), 0)
    lane_group = lax.broadcasted_iota(jnp.int32, (rows, PW), 1) // 64
    cnt = None
    for gi, w in enumerate(POOL_WINDOWS):
        lo = jnp.maximum(pos - w // 2, 0)
        hi = jnp.minimum(pos + w - 1 - w // 2, S - 1)
        c = (hi - lo + 1).astype(f32)
        cnt = c if cnt is None else jnp.where(lane_group >= gi, c, cnt)
    return cnt


def _pool_fwd(vp, wbd, scale, l=None):
    ch = 256

    def body(vp_ref, w_ref, sc_ref, y_ref, diff_ref, pad):
        pad[pl.ds(0, PAD), :] = jnp.zeros((PAD, PW), f32)
        pad[pl.ds(PAD + S, PAD), :] = jnp.zeros((PAD, PW), f32)
        pad[pl.ds(PAD, S), :] = vp_ref[...]
        for b in range(S // ch):
            base = b * ch
            pooled = _pool_sums(pad, base, ch, False) / _pool_counts(base, ch)
            diff = (pooled - vp_ref[pl.ds(base, ch), :]).astype(bf16)
            diff_ref[pl.ds(base, ch), :] = diff
            y_ref[pl.ds(base, ch), :] = _dot_nn(diff, w_ref[...]) * sc_ref[...]

    whole = lambda shape: pl.BlockSpec(shape, lambda i: (0,) * len(shape))
    return pl.pallas_call(
        body, grid=(1,),
        in_specs=[whole((S, PW)), whole((PW, PW)), whole((1, PW))],
        out_specs=[whole((S, PW)), whole((S, PW))],
        out_shape=[SDS((S, PW), f32), SDS((S, PW), bf16)],
        scratch_shapes=[pltpu.VMEM((S + 2 * PAD, PW), f32)],
        compiler_params=_CP, name="pool_fwd")(vp, wbd, scale)


def _pool_bwd(dy, diff, wbd, scale, after=()):
    ch = 256

    def body(dy_ref, diff_ref, w_ref, sc_ref, dvp_ref, dw_ref, dsc_ref, pad):
        pad[pl.ds(0, PAD), :] = jnp.zeros((PAD, PW), f32)
        pad[pl.ds(PAD + S, PAD), :] = jnp.zeros((PAD, PW), f32)
        dw = jnp.zeros((PW, PW), f32)
        dsc = jnp.zeros((1, PW), f32)
        for b in range(S // ch):
            base = b * ch
            dy = dy_ref[pl.ds(base, ch), :]
            diff = diff_ref[pl.ds(base, ch), :]
            dsc = dsc + jnp.sum(dy * _dot_nn(diff, w_ref[...]), axis=0, keepdims=True)
            dz = (dy * sc_ref[...]).astype(bf16)
            dw = dw + _dot_tn(diff, dz)
            ddiff = _dot_nt(dz, w_ref[...])
            dvp_ref[pl.ds(base, ch), :] = -ddiff
            pad[pl.ds(PAD + base, ch), :] = ddiff / _pool_counts(base, ch)
        for gi in range(4):
            dw_ref[gi] = dw[64 * gi:64 * (gi + 1), 64 * gi:64 * (gi + 1)]
        dsc_ref[...] = dsc
        for b in range(S // ch):
            base = b * ch
            dvp_ref[pl.ds(base, ch), :] += _pool_sums(pad, base, ch, True)

    whole = lambda shape: pl.BlockSpec(shape, lambda i: (0,) * len(shape))
    return pl.pallas_call(
        _follow(body, 4, after), grid=(1,),
        in_specs=[whole((S, PW)), whole((S, PW)), whole((PW, PW)), whole((1, PW))] + [_ANY] * len(after),
        out_specs=[whole((S, PW)), whole((4, 64, 64)), whole((1, PW))],
        out_shape=[SDS((S, PW), f32), SDS((4, 64, 64), f32), SDS((1, PW), f32)],
        scratch_shapes=[pltpu.VMEM((S + 2 * PAD, PW), f32)],
        compiler_params=_CP, name="pool_bwd")(dy, diff, wbd, scale, *after)


def _attn_blocks(lc):
    bpc = lc // QB
    kw = min(2 * QB, lc)
    blocks = []
    for b in range(S // QB):
        t0 = (b % bpc) * QB
        ks_in = min(max(t0 - HALF, 0), lc - kw)
        blocks.append((b * QB, (b // bpc) * lc + ks_in, t0 - ks_in))
    return kw, blocks


def _attn_bias(bias_ref, kw, shifts):
    r = lax.broadcasted_iota(jnp.int32, (2 * QB, kw), 0) % QB
    c = lax.broadcasted_iota(jnp.int32, (2 * QB, kw), 1)
    for i, shift in enumerate(shifts):
        bias_ref[i] = jnp.where(jnp.abs(r + shift - c) <= HALF, 0.0, MASK_VALUE).astype(f32)


def _head_put(stats, pair, v0, v1, lane):
    return jnp.where(lane == 2 * pair, v0, jnp.where(lane == 2 * pair + 1, v1, stats))


def _head_cols(stats, pair, lane):
    c0 = jnp.sum(jnp.where(lane == 2 * pair, stats, 0.0), axis=-1, keepdims=True)
    c1 = jnp.sum(jnp.where(lane == 2 * pair + 1, stats, 0.0), axis=-1, keepdims=True)
    return jnp.concatenate([c0, c1], axis=0)


def _head_spread(stats, pair, head0):
    return jnp.where(head0, stats[:, 2 * pair:2 * pair + 1], stats[:, 2 * pair + 1:2 * pair + 2])


def _stack_heads(blk, head0):
    zero = jnp.zeros_like(blk)
    return jnp.concatenate([jnp.where(head0, blk, zero), jnp.where(head0, zero, blk)], axis=0)


def _attn_fwd(q, k, v, lc, after=None):
    kw, blocks = _attn_blocks(lc)
    shifts = sorted({b[2] for b in blocks})

    def body(q_ref, k_ref, v_ref, *refs):
        o_ref, lse_ref, bias_ref = refs[-3:]
        lane = lax.broadcasted_iota(jnp.int32, (QB, 128), 1)
        head0 = lane < 64
        pair = pl.program_id(0)
        _attn_bias(bias_ref, kw, shifts)

        @pl.when(pair == 0)
        def _():
            lse_ref[...] = jnp.zeros_like(lse_ref)

        for row0, kstart, shift in blocks:
            q2 = _stack_heads(q_ref[pl.ds(row0, QB), :], head0)
            kb = k_ref[pl.ds(kstart, kw), :]
            vb = v_ref[pl.ds(kstart, kw), :]
            s = _dot_nt(q2, kb) + bias_ref[shifts.index(shift)]
            m = jnp.max(s, axis=-1, keepdims=True)
            p = jnp.exp(s - m)
            den = jnp.sum(p, axis=-1, keepdims=True)
            o2 = _dot_nn(p.astype(bf16), vb) / den
            lse2 = m + jnp.log(den)
            o_ref[pl.ds(row0, QB), :] = jnp.where(head0, o2[:QB], o2[QB:]).astype(bf16)
            lse_ref[pl.ds(row0, QB), :] = _head_put(lse_ref[pl.ds(row0, QB), :], pair, lse2[:QB], lse2[QB:], lane)

    col = pl.BlockSpec((S, 128), lambda p: (0, p))
    extra = () if after is None else (after,)
    return pl.pallas_call(
        body, grid=(NG,), in_specs=[col, col, col] + [_ANY] * len(extra),
        out_specs=[col, pl.BlockSpec((S, 128), lambda p: (0, 0))],
        out_shape=[SDS((S, AW), bf16), SDS((S, 128), f32)],
        scratch_shapes=[pltpu.VMEM((len(shifts), 2 * QB, kw), f32)],
        compiler_params=_CP, name=f"attn_fwd_{lc}")(q, k, v, *extra)


def _attn_bwd(q, k, v, do, lse, delta, lc):
    kw, blocks = _attn_blocks(lc)
    shifts = sorted({b[2] for b in blocks})

    def body(q_ref, k_ref, v_ref, do_ref, lse_ref, dl_ref, dq_ref, dk_out, dv_out, bias_ref, dk_ref, dv_ref):
        lane = lax.broadcasted_iota(jnp.int32, (QB, 128), 1)
        head0 = lane < 64
        pair = pl.program_id(0)
        _attn_bias(bias_ref, kw, shifts)
        dk_ref[...] = jnp.zeros_like(dk_ref)
        dv_ref[...] = jnp.zeros_like(dv_ref)
        for row0, kstart, shift in blocks:
            q2 = _stack_heads(q_ref[pl.ds(row0, QB), :], head0)
            do2 = _stack_heads(do_ref[pl.ds(row0, QB), :], head0)
            lse2 = _head_cols(lse_ref[pl.ds(row0, QB), :], pair, lane)
            dl2 = _head_cols(dl_ref[pl.ds(row0, QB), :], pair, lane)
            kb = k_ref[pl.ds(kstart, kw), :]
            vb = v_ref[pl.ds(kstart, kw), :]
            p = jnp.exp(_dot_nt(q2, kb) + bias_ref[shifts.index(shift)] - lse2)
            ds = (p * (_dot_nt(do2, vb) - dl2)).astype(bf16)
            dq2 = _dot_nn(ds, kb)
            dq_ref[pl.ds(row0, QB), :] = jnp.where(head0, dq2[:QB], dq2[QB:]).astype(bf16)
            dk_ref[pl.ds(kstart, kw), :] += _dot_tn(ds, q2)
            dv_ref[pl.ds(kstart, kw), :] += _dot_tn(p.astype(bf16), do2)
        dk_out[...] = dk_ref[...].astype(bf16)
        dv_out[...] = dv_ref[...].astype(bf16)

    col = pl.BlockSpec((S, 128), lambda p: (0, p))
    stats = pl.BlockSpec((S, 128), lambda p: (0, 0))
    return pl.pallas_call(
        body, grid=(NG,), in_specs=[col] * 4 + [stats] * 2, out_specs=[col] * 3,
        out_shape=[SDS((S, AW), bf16)] * 3,
        scratch_shapes=[pltpu.VMEM((len(shifts), 2 * QB, kw), f32), pltpu.VMEM((S, 128), f32),
                        pltpu.VMEM((S, 128), f32)],
        compiler_params=_CP, name=f"attn_bwd_{lc}")(q, k, v, do, lse, delta)


def _mix_out_fwd(x, ypool, o1, l1, o4, l4, o16, l16, wout, l=None):
    def body(x_ref, yp_ref, o1_ref, l1_ref, o4_ref, l4_ref, o16_ref, l16_ref, w_ref,
             xo_ref, mixed_ref, o_ref, lse1_ref, lse4_ref, lse16_ref, so4, so16, sl4, sl16, sl):
        head0 = lax.broadcasted_iota(jnp.int32, (TM, 128), 1) < 64
        for r in range(4):
            sl4[pl.ds(r, TM // 4, stride=4), :] = l4_ref[r]
            for j in range(NG):
                so4[j, pl.ds(r, TM // 4, stride=4), :] = o4_ref[r, :, _cols(j)].astype(f32)
        for r in range(16):
            sl16[pl.ds(r, TM // 16, stride=16), :] = l16_ref[r]
            for j in range(NG):
                so16[j, pl.ds(r, TM // 16, stride=16), :] = o16_ref[r, :, _cols(j)].astype(f32)
        a, b, c = l1_ref[...], sl4[...], sl16[...]
        m = jnp.maximum(jnp.maximum(a, b), c)
        wa, wb, wc = jnp.exp(a - m), jnp.exp(b - m), jnp.exp(c - m)
        den = wa + wb + wc
        wa, wb, wc = wa / den, wb / den, wc / den
        lse = m + jnp.log(den)
        lse1_ref[...] = lse
        sl[...] = lse
        mixed_ref[:, :PW] = yp_ref[...].astype(bf16)
        for j in range(NG):
            y = (_head_spread(wa, j, head0) * o1_ref[:, _cols(j)].astype(f32) + _head_spread(wb, j, head0) * so4[j]
                 + _head_spread(wc, j, head0) * so16[j])
            o_ref[:, _cols(j)] = y
            mixed_ref[:, PW + 128 * j: PW + 128 * (j + 1)] = y.astype(bf16)
        for r in range(4):
            lse4_ref[r] = sl[pl.ds(r, TM // 4, stride=4), :]
        for r in range(16):
            lse16_ref[r] = sl[pl.ds(r, TM // 16, stride=16), :]
        xo_ref[...] = x_ref[...] + _dot_nn(mixed_ref[...], w_ref[...])

    wide, narrow = pltpu.VMEM((NG, TM, 128), f32), pltpu.VMEM((TM, 128), f32)
    return pl.pallas_call(
        body, grid=(S // TM,),
        in_specs=[_tile(D), _tile(PW), _tile(AW), _tile(128), _p4(), _p4(128), _p16(), _p16(128), _layer(D, D, l)],
        out_specs=[_tile(D), _tile(D), _tile(AW), _tile(128), _p4(128), _p16(128)],
        out_shape=[SDS((S, D), f32), SDS((S, D), bf16), SDS((S, AW), f32), SDS((S, 128), f32),
                   SDS((4, S // 4, 128), f32), SDS((16, S // 16, 128), f32)],
        scratch_shapes=[wide, wide, narrow, narrow, narrow],
        compiler_params=_CP, name="mix_out_fwd")(x, ypool, o1, l1, o4, l4, o16, l16, wout)


def _mix_out_bwd(dxo, o, wout, l=None):
    def body(dxo_ref, o_ref, w_ref, dxb_ref, dyp_ref, do1, do4, do16, dl1, dl4, dl16, sdo, sdl):
        dxb = dxo_ref[...].astype(bf16)
        dxb_ref[...] = dxb
        dm = _dot_nt(dxb, w_ref[...])
        dyp_ref[...] = dm[:, :PW]
        lane = lax.broadcasted_iota(jnp.int32, (TM, 128), 1)
        head0 = lane < 64
        dl = jnp.zeros((TM, 128), f32)
        for j in range(NG):
            d = dm[:, PW + 128 * j: PW + 128 * (j + 1)]
            prod = d * o_ref[:, _cols(j)]
            dl = _head_put(dl, j, jnp.sum(jnp.where(head0, prod, 0.0), axis=-1, keepdims=True),
                           jnp.sum(jnp.where(head0, 0.0, prod), axis=-1, keepdims=True), lane)
            do1[:, _cols(j)] = d.astype(bf16)
            sdo[j] = d
        dl1[...] = dl
        sdl[...] = dl
        for r in range(4):
            dl4[r] = sdl[pl.ds(r, TM // 4, stride=4), :]
            for j in range(NG):
                do4[r, :, _cols(j)] = sdo[j, pl.ds(r, TM // 4, stride=4), :].astype(bf16)
        for r in range(16):
            dl16[r] = sdl[pl.ds(r, TM // 16, stride=16), :]
            for j in range(NG):
                do16[r, :, _cols(j)] = sdo[j, pl.ds(r, TM // 16, stride=16), :].astype(bf16)

    return pl.pallas_call(
        body, grid=(S // TM,),
        in_specs=[_tile(D), _tile(AW), _layer(D, D, l)],
        out_specs=[_tile(D), _tile(PW), _tile(AW), _p4(), _p16(), _tile(128), _p4(128), _p16(128)],
        out_shape=[SDS((S, D), bf16), SDS((S, PW), f32),
                   SDS((S, AW), bf16), SDS((4, S // 4, AW), bf16), SDS((16, S // 16, AW), bf16),
                   SDS((S, 128), f32), SDS((4, S // 4, 128), f32), SDS((16, S // 16, 128), f32)],
        scratch_shapes=[pltpu.VMEM((NG, TM, 128), f32), pltpu.VMEM((TM, 128), f32)],
        compiler_params=_CP, name="mix_out_bwd")(dxo, o, wout)


def _loss_head(x, g, target):
    def body(x_ref, g_ref, t_ref, dx_ref, loss_ref, dg_ref):
        g = g_ref[...]
        r, xh, y = _rms(x_ref[...], g)
        err = y - t_ref[...]
        dy = err * (1.0 / D)

        @pl.when(pl.program_id(0) == 0)
        def _():
            loss_ref[...] = jnp.zeros_like(loss_ref)
            dg_ref[...] = jnp.zeros_like(dg_ref)

        loss_ref[...] += jnp.broadcast_to(0.5 * jnp.sum(jnp.mean(err * err, axis=-1, keepdims=True)), (1, D))
        dg_ref[...] += jnp.sum(dy * xh, axis=0, keepdims=True)
        dx_ref[...] = _rms_bwd(dy, r, xh, g)

    return pl.pallas_call(
        body, grid=(S // TM,),
        in_specs=[_tile(D), _const((1, D)), _tile(D)],
        out_specs=[_tile(D), _const((1, D)), _const((1, D))],
        out_shape=[SDS((S, D), f32), SDS((1, D), f32), SDS((1, D), f32)],
        compiler_params=_CP, name="loss_head")(x, g, target)


def _peer(k):
    x, y, c = lax.axis_index("x"), lax.axis_index("y"), lax.axis_index("c")
    px = 1 - x if k & 4 else x
    py = 1 - y if k & 2 else y
    pc = 1 - c if k & 1 else c
    return (px, py, pc), 4 * px + 2 * py + pc


def _diag_route():
    x, y, c = lax.axis_index("x"), lax.axis_index("y"), lax.axis_index("c")
    idx_x, idx_y = _peer(4)[1], _peer(2)[1]
    return idx_x + c * (idx_y - idx_x), (x + c * (1 - 2 * x), (1 - y) + c * (2 * y - 1), c)


def _all_gather(lands):
    n = len(lands)

    def body(*refs):
        zones, send_sems, recv_sems = refs[n:2 * n], refs[2 * n], refs[2 * n + 1]
        me, me_idx = _peer(0)
        sibling, sib_idx = _peer(1)
        (x_nbr, idx_x), (y_nbr, idx_y), idx_d = _peer(4), _peer(2), _peer(6)[1]
        fwd_idx, fwd_dev = _diag_route()

        def copy(k, t, idx, to):
            return _row_copy(zones[t], idx, send_sems.at[k, t], recv_sems.at[k, t], to)

        sent = []

        def send(k, t, idx, to):
            cp = copy(k, t, idx, to)
            cp.start()
            sent.append(cp)

        for t in range(n):
            send(0, t, me_idx, sibling)
            send(1, t, me_idx, x_nbr)
            send(2, t, me_idx, y_nbr)
        for t in range(n):
            copy(1, t, idx_x, me).wait_recv()
            send(3, t, idx_x, sibling)
        for t in range(n):
            copy(2, t, idx_y, me).wait_recv()
            send(4, t, idx_y, sibling)
        for t in range(n):
            send(5, t, fwd_idx, fwd_dev)
        for t in range(n):
            copy(5, t, idx_d, me).wait_recv()
            send(6, t, idx_d, sibling)
        for k, mask in ((0, 1), (3, 5), (4, 3), (6, 7)):
            for t in range(n):
                copy(k, t, _peer(mask)[1], me).wait_recv()
        for cp in sent:
            cp.wait_send()

    return pl.pallas_call(
        body, in_specs=[_ANY] * n, out_specs=[_ANY] * n,
        out_shape=[SDS(a.shape, a.dtype) for a in lands], input_output_aliases={t: t for t in range(n)},
        scratch_shapes=[pltpu.SemaphoreType.DMA((7, n)), pltpu.SemaphoreType.DMA((7, n))],
        name="all_gather_weights")(*lands)


def _hbm(a):
    return pltpu.with_memory_space_constraint(a, pltpu.HBM)


def _rows(ref, idx):
    r = ref.shape[0] // NDEV
    return ref.at[pl.ds(idx * r, r), :]


def _row_copy(ref, idx, send_sem, recv_sem, to):
    return pltpu.make_async_remote_copy(src_ref=_rows(ref, idx), dst_ref=_rows(ref, idx), send_sem=send_sem,
                                        recv_sem=recv_sem, device_id=to, device_id_type=_MESH)


def _place_own(me, shards, l):
    n = len(shards)

    def body(me_ref, *refs):
        for t in range(n):
            refs[n + t][...] = refs[t][...]

    grid_spec = pltpu.PrefetchScalarGridSpec(
        num_scalar_prefetch=1, grid=(1,),
        in_specs=[pl.BlockSpec((None, s.shape[1], D), lambda i, me_ref: (l, 0, 0)) for s in shards],
        out_specs=[pl.BlockSpec((s.shape[1], D), lambda i, me_ref: (me_ref[0], 0)) for s in shards])
    return pl.pallas_call(
        body, grid_spec=grid_spec, out_shape=[SDS((NDEV * s.shape[1], D), s.dtype) for s in shards],
        compiler_params=_CP, name="place_own")(me, *shards)


_TOKEN = SDS((8, 128), f32)
def _ag_start(lands, after, l):
    n = len(lands)
    after = list(after) if isinstance(after, (list, tuple)) else [after]

    def body(*refs):
        zones, send_sems, recv_sems, token = refs[:n], refs[n + len(after)], refs[n + len(after) + 1], refs[-1]
        _, me_idx = _peer(0)
        for k, mask in enumerate((1, 4, 2)):
            for t in range(n):
                _row_copy(zones[t], me_idx, send_sems.at[k * n + t], recv_sems.at[k * n + t], _peer(mask)[0]).start()
        token[...] = jnp.zeros_like(token)

    outs = pl.pallas_call(
        body, name=f"ag_start_{l}", in_specs=[_HBM] * n + [_ANY] * len(after),
        out_specs=(_SEM, _SEM, *[_HBM] * n, pl.BlockSpec(memory_space=pltpu.VMEM)),
        out_shape=(pltpu.SemaphoreType.DMA((3 * n,)), pltpu.SemaphoreType.DMA((3 * n,)),
                   *[pltpu.HBM(a.shape, a.dtype) for a in lands], _TOKEN),
        input_output_aliases={t: 2 + t for t in range(n)}, compiler_params=_CP_SPLIT)(
            *[_hbm(a) for a in lands], *after)
    return outs[0], outs[1], list(outs[2:2 + n]), outs[-1]


def _ag_pass(lands, recv_sems, after, l):
    n = len(lands)
    after = list(after) if isinstance(after, (list, tuple)) else [after]

    def body(*refs):
        zones, recv_sems = refs[:n], refs[n]
        psend, precv, token = refs[n + 1 + len(after)], refs[n + 2 + len(after)], refs[-1]
        me, _ = _peer(0)
        sibling, _ = _peer(1)
        for j, mask in enumerate((4, 2)):
            idx = _peer(mask)[1]
            for t in range(n):
                _row_copy(zones[t], idx, psend.at[j * n + t], recv_sems.at[(1 + j) * n + t], me).wait_recv()
                _row_copy(zones[t], idx, psend.at[j * n + t], precv.at[j * n + t], sibling).start()
        fwd_idx, fwd_dev = _diag_route()
        for t in range(n):
            _row_copy(zones[t], fwd_idx, psend.at[2 * n + t], precv.at[2 * n + t], fwd_dev).start()
        token[...] = jnp.zeros_like(token)

    outs = pl.pallas_call(
        body, name=f"ag_pass_{l}", in_specs=[_HBM] * n + [_SEM] + [_ANY] * len(after),
        out_specs=(_SEM, _SEM, *[_HBM] * n, pl.BlockSpec(memory_space=pltpu.VMEM)),
        out_shape=(pltpu.SemaphoreType.DMA((3 * n,)), pltpu.SemaphoreType.DMA((3 * n,)),
                   *[pltpu.HBM(a.shape, a.dtype) for a in lands], _TOKEN),
        input_output_aliases={t: 2 + t for t in range(n)}, compiler_params=_CP_SPLIT)(*lands, recv_sems, *after)
    return outs[0], outs[1], list(outs[2:2 + n]), outs[-1]


def _ag_last(lands, precv, after, l):
    n = len(lands)
    after = list(after) if isinstance(after, (list, tuple)) else [after]

    def body(*refs):
        zones, precv = refs[:n], refs[n]
        qsend, qrecv, token = refs[n + 1 + len(after)], refs[n + 2 + len(after)], refs[-1]
        me, _ = _peer(0)
        sibling, _ = _peer(1)
        idx = _peer(6)[1]
        for t in range(n):
            _row_copy(zones[t], idx, qsend.at[t], precv.at[2 * n + t], me).wait_recv()
            _row_copy(zones[t], idx, qsend.at[t], qrecv.at[t], sibling).start()
        token[...] = jnp.zeros_like(token)

    outs = pl.pallas_call(
        body, name=f"ag_last_{l}", in_specs=[_HBM] * n + [_SEM] + [_ANY] * len(after),
        out_specs=(_SEM, _SEM, *[_HBM] * n, pl.BlockSpec(memory_space=pltpu.VMEM)),
        out_shape=(pltpu.SemaphoreType.DMA((n,)), pltpu.SemaphoreType.DMA((n,)),
                   *[pltpu.HBM(a.shape, a.dtype) for a in lands], _TOKEN),
        input_output_aliases={t: 2 + t for t in range(n)}, compiler_params=_CP_SPLIT)(*lands, precv, *after)
    return outs[0], outs[1], list(outs[2:2 + n]), outs[-1]


def _ag_wait(lands, send_sems, recv_sems, psend, precv, qsend, qrecv, after, l):
    n = len(lands)
    after = list(after) if isinstance(after, (list, tuple)) else [after]

    def body(*refs):
        zones = refs[:n]
        send_sems, recv_sems, psend, precv, qsend, qrecv = refs[n:n + 6]
        me, me_idx = _peer(0)
        for k in range(3):
            for t in range(n):
                _row_copy(zones[t], me_idx, send_sems.at[k * n + t], recv_sems.at[k * n + t], me).wait_send()
        for t in range(n):
            _row_copy(zones[t], _peer(1)[1], send_sems.at[t], recv_sems.at[t], me).wait_recv()
        fwd_idx, _ = _diag_route()
        for j, (mine, theirs) in enumerate(((_peer(4)[1], _peer(5)[1]), (_peer(2)[1], _peer(3)[1]))):
            for t in range(n):
                _row_copy(zones[t], mine, psend.at[j * n + t], precv.at[j * n + t], me).wait_send()
                _row_copy(zones[t], theirs, psend.at[j * n + t], precv.at[j * n + t], me).wait_recv()
        for t in range(n):
            _row_copy(zones[t], fwd_idx, psend.at[2 * n + t], precv.at[2 * n + t], me).wait_send()
            _row_copy(zones[t], _peer(6)[1], qsend.at[t], qrecv.at[t], me).wait_send()
            _row_copy(zones[t], _peer(7)[1], qsend.at[t], qrecv.at[t], me).wait_recv()

    outs = pl.pallas_call(
        body, name=f"ag_wait_{l}", in_specs=[_HBM] * n + [_SEM] * 6 + [_ANY] * len(after),
        out_specs=tuple([_HBM] * n), out_shape=tuple(pltpu.HBM(a.shape, a.dtype) for a in lands),
        input_output_aliases={t: t for t in range(n)}, compiler_params=_CP_SPLIT)(
            *lands, send_sems, recv_sems, psend, precv, qsend, qrecv, *after)
    return list(outs)


def _xchg_src(ref, slot_ref, idx):
    return _rows(ref, idx) if ref.shape[0] == NDEV * slot_ref.shape[1] else ref


def _rs_start(srcs, slots, after, tag):
    n = len(srcs)
    after = list(after) if isinstance(after, (list, tuple)) else [after]

    def body(*refs):
        src, slot = refs[:n], refs[n:2 * n]
        send_sems, recv_sems, token = refs[2 * n + len(after)], refs[2 * n + len(after) + 1], refs[-1]
        _, me_idx = _peer(0)
        for k in range(1, NDEV):
            dev, idx = _peer(k)
            for t in range(n):
                pltpu.make_async_remote_copy(
                    src_ref=_xchg_src(src[t], slot[t], idx), dst_ref=slot[t].at[me_idx],
                    send_sem=send_sems.at[(k - 1) * n + t], recv_sem=recv_sems.at[(k - 1) * n + t],
                    device_id=dev, device_id_type=_MESH).start()
        token[...] = jnp.zeros_like(token)

    outs = pl.pallas_call(
        body, name=f"rs_start_{tag}", in_specs=[_HBM] * (2 * n) + [_ANY] * len(after),
        out_specs=(_SEM, _SEM, *[_HBM] * (2 * n), pl.BlockSpec(memory_space=pltpu.VMEM)),
        out_shape=(pltpu.SemaphoreType.DMA(((NDEV - 1) * n,)), pltpu.SemaphoreType.DMA(((NDEV - 1) * n,)),
                   *[pltpu.HBM(a.shape, a.dtype) for a in list(srcs) + list(slots)], _TOKEN),
        input_output_aliases={t: 2 + t for t in range(2 * n)}, compiler_params=_CP_SPLIT)(
            *[_hbm(a) for a in list(srcs) + list(slots)], *after)
    return outs[0], outs[1], list(outs[2:2 + n]), list(outs[2 + n:2 + 2 * n]), outs[-1]


def _rs_wait(srcs, slots, send_sems, recv_sems, after, tag):
    n = len(srcs)
    after = list(after) if isinstance(after, (list, tuple)) else [after]

    def body(*refs):
        src, slot, send_sems, recv_sems = refs[:n], refs[n:2 * n], refs[2 * n], refs[2 * n + 1]
        me, _ = _peer(0)
        for k in range(1, NDEV):
            idx = _peer(k)[1]
            for t in range(n):
                cp = pltpu.make_async_remote_copy(
                    src_ref=_xchg_src(src[t], slot[t], idx), dst_ref=slot[t].at[idx],
                    send_sem=send_sems.at[(k - 1) * n + t], recv_sem=recv_sems.at[(k - 1) * n + t],
                    device_id=me, device_id_type=_MESH)
                cp.wait_send()
                cp.wait_recv()

    outs = pl.pallas_call(
        body, name=f"rs_wait_{tag}", in_specs=[_HBM] * (2 * n) + [_SEM, _SEM] + [_ANY] * len(after),
        out_specs=tuple([_HBM] * (2 * n)),
        out_shape=tuple(pltpu.HBM(a.shape, a.dtype) for a in list(srcs) + list(slots)),
        input_output_aliases={t: t for t in range(2 * n)}, compiler_params=_CP_SPLIT)(
            *srcs, *slots, send_sems, recv_sems, *after)
    return list(outs[:n]), list(outs[n:])


def _pair_start(full4s, bufs, after, tag):
    n = len(full4s)
    after = list(after) if isinstance(after, (list, tuple)) else [after]

    def body(*refs):
        full, buf = refs[:n], refs[n:2 * n]
        send_sems, recv_sems, token = refs[2 * n + len(after)], refs[2 * n + len(after) + 1], refs[-1]
        c = lax.axis_index("c")
        for t in range(n):
            pltpu.make_async_remote_copy(src_ref=full[t].at[:, 1 - c], dst_ref=buf[t], send_sem=send_sems.at[t],
                                         recv_sem=recv_sems.at[t], device_id=_peer(1)[0], device_id_type=_MESH).start()
        token[...] = jnp.zeros_like(token)

    outs = pl.pallas_call(
        body, name=f"pair_start_{tag}", in_specs=[_HBM] * (2 * n) + [_ANY] * len(after),
        out_specs=(_SEM, _SEM, *[_HBM] * (2 * n), pl.BlockSpec(memory_space=pltpu.VMEM)),
        out_shape=(pltpu.SemaphoreType.DMA((n,)), pltpu.SemaphoreType.DMA((n,)),
                   *[pltpu.HBM(a.shape, a.dtype) for a in list(full4s) + list(bufs)], _TOKEN),
        input_output_aliases={t: 2 + t for t in range(2 * n)}, compiler_params=_CP_SPLIT)(
            *[_hbm(a) for a in list(full4s) + list(bufs)], *after)
    return outs[0], outs[1], list(outs[2:2 + n]), list(outs[2 + n:2 + 2 * n]), outs[-1]


def _pair_wait(full4s, bufs, send_sems, recv_sems, after, tag):
    n = len(full4s)
    after = list(after) if isinstance(after, (list, tuple)) else [after]

    def body(*refs):
        full, buf, send_sems, recv_sems = refs[:n], refs[n:2 * n], refs[2 * n], refs[2 * n + 1]
        c = lax.axis_index("c")
        for t in range(n):
            cp = pltpu.make_async_remote_copy(src_ref=full[t].at[:, 1 - c], dst_ref=buf[t], send_sem=send_sems.at[t],
                                              recv_sem=recv_sems.at[t], device_id=_peer(0)[0], device_id_type=_MESH)
            cp.wait_send()
            cp.wait_recv()

    outs = pl.pallas_call(
        body, name=f"pair_wait_{tag}", in_specs=[_HBM] * (2 * n) + [_SEM, _SEM] + [_ANY] * len(after),
        out_specs=tuple([_HBM] * (2 * n)),
        out_shape=tuple(pltpu.HBM(a.shape, a.dtype) for a in list(full4s) + list(bufs)),
        input_output_aliases={t: t for t in range(2 * n)}, compiler_params=_CP_SPLIT)(
            *full4s, *bufs, send_sems, recv_sems, *after)
    return list(outs[:n]), list(outs[n:])


def _pair_sum(core, full4s, bufs):
    n = len(full4s)

    def body(core_ref, *refs):
        for t in range(n):
            refs[2 * n + t][...] = (refs[t][...].astype(f32) + refs[n + t][...].astype(f32)).astype(bf16)

    grid_spec = pltpu.PrefetchScalarGridSpec(
        num_scalar_prefetch=1, grid=(4,),
        in_specs=[pl.BlockSpec((None, None) + a.shape[2:], lambda j, core_ref: (j, core_ref[0], 0, 0)) for a in full4s]
        + [pl.BlockSpec((None,) + b.shape[1:], lambda j, core_ref: (j, 0, 0)) for b in bufs],
        out_specs=[pl.BlockSpec((None,) + b.shape[1:], lambda j, core_ref: (j, 0, 0)) for b in bufs])
    return pl.pallas_call(
        body, grid_spec=grid_spec, out_shape=[SDS(b.shape, bf16) for b in bufs],
        compiler_params=_CP, name="pair_sum")(core, *full4s, *bufs)


def _chip_start(sums, slots, after, tag):
    n = len(sums)
    after = list(after) if isinstance(after, (list, tuple)) else [after]

    def body(*refs):
        src, slot = refs[:n], refs[n:2 * n]
        send_sems, recv_sems, token = refs[2 * n + len(after)], refs[2 * n + len(after) + 1], refs[-1]
        my_chip = 2 * lax.axis_index("x") + lax.axis_index("y")
        for k, mask in enumerate((4, 2, 6)):
            dev, _ = _peer(mask)
            for t in range(n):
                pltpu.make_async_remote_copy(
                    src_ref=src[t].at[2 * dev[0] + dev[1]], dst_ref=slot[t].at[my_chip],
                    send_sem=send_sems.at[k * n + t], recv_sem=recv_sems.at[k * n + t],
                    device_id=dev, device_id_type=_MESH).start()
        token[...] = jnp.zeros_like(token)

    outs = pl.pallas_call(
        body, name=f"chip_start_{tag}", in_specs=[_HBM] * (2 * n) + [_ANY] * len(after),
        out_specs=(_SEM, _SEM, *[_HBM] * (2 * n), pl.BlockSpec(memory_space=pltpu.VMEM)),
        out_shape=(pltpu.SemaphoreType.DMA((3 * n,)), pltpu.SemaphoreType.DMA((3 * n,)),
                   *[pltpu.HBM(a.shape, a.dtype) for a in list(sums) + list(slots)], _TOKEN),
        input_output_aliases={t: 2 + t for t in range(2 * n)}, compiler_params=_CP_SPLIT)(
            *[_hbm(a) for a in list(sums) + list(slots)], *after)
    return outs[0], outs[1], list(outs[2:2 + n]), list(outs[2 + n:2 + 2 * n]), outs[-1]


def _chip_wait(sums, slots, send_sems, recv_sems, after, tag):
    n = len(sums)
    after = list(after) if isinstance(after, (list, tuple)) else [after]

    def body(*refs):
        src, slot, send_sems, recv_sems = refs[:n], refs[n:2 * n], refs[2 * n], refs[2 * n + 1]
        for k, mask in enumerate((4, 2, 6)):
            dev, _ = _peer(mask)
            chip = 2 * dev[0] + dev[1]
            for t in range(n):
                cp = pltpu.make_async_remote_copy(
                    src_ref=src[t].at[chip], dst_ref=slot[t].at[chip],
                    send_sem=send_sems.at[k * n + t], recv_sem=recv_sems.at[k * n + t],
                    device_id=_peer(0)[0], device_id_type=_MESH)
                cp.wait_send()
                cp.wait_recv()

    outs = pl.pallas_call(
        body, name=f"chip_wait_{tag}", in_specs=[_HBM] * (2 * n) + [_SEM, _SEM] + [_ANY] * len(after),
        out_specs=tuple([_HBM] * (2 * n)),
        out_shape=tuple(pltpu.HBM(a.shape, a.dtype) for a in list(sums) + list(slots)),
        input_output_aliases={t: t for t in range(2 * n)}, compiler_params=_CP_SPLIT)(
            *sums, *slots, send_sems, recv_sems, *after)
    return list(outs[:n]), list(outs[n:])


def _sum_slots(slots, rb):
    r = slots.shape[1]

    def body(s_ref, o_ref):
        acc = s_ref[0].astype(f32)
        for s in range(1, NDEV):
            acc = acc + s_ref[s].astype(f32)
        o_ref[...] = acc

    return pl.pallas_call(
        body, grid=(r // rb,),
        in_specs=[pl.BlockSpec((NDEV, rb, D), lambda i: (0, i, 0))],
        out_specs=pl.BlockSpec((rb, D), lambda i: (i, 0)),
        out_shape=SDS((r, D), f32), compiler_params=_CP, name="sum_slots")(slots)


def _adamw(w, g, m, v):
    shape = w.shape
    cols = shape[-1]
    rows = w.size // cols
    rb = rows
    for cand in (512, 256, 128, 64, 32, 16, 8):
        if rows % cand == 0 and rows > cand:
            rb = cand
            break

    def body(w_ref, g_ref, m_ref, v_ref, d_ref, mo_ref, vo_ref):
        d_ref[...], mo_ref[...], vo_ref[...] = _adamw_math(w_ref[...], g_ref[...], m_ref[...], v_ref[...])

    spec = pl.BlockSpec((rb, cols), lambda i: (i, 0))
    outs = pl.pallas_call(
        body, grid=(rows // rb,), in_specs=[spec] * 4, out_specs=[spec] * 3,
        out_shape=[SDS((rows, cols), f32)] * 3, compiler_params=_CP, name="adamw")(
            *(a.reshape(rows, cols) for a in (w, g, m, v)))
    return tuple(o.reshape(shape) for o in outs)


def _adamw_math(w, g, m, v):
    m = ADAM_B1 * m + (1.0 - ADAM_B1) * g
    v = ADAM_B2 * v + (1.0 - ADAM_B2) * (g * g)
    m_hat = m / (1.0 - ADAM_B1 ** ADAM_STEP)
    v_hat = v / (1.0 - ADAM_B2 ** ADAM_STEP)
    return -ADAM_LR * (m_hat / (jnp.sqrt(v_hat) + ADAM_EPS) + ADAM_WD * w), m, v


def _reduce_adamw(acc, me, full, slots, w, m, v, l):
    _, r, _ = w.shape
    ns = slots.shape[0]
    rb = r // 2 if r > 128 else r

    def body(me_ref, full_ref, slots_ref, w_ref, m_ref, v_ref, *refs):
        go_ref, d_ref, mo_ref, vo_ref = refs[-4:]
        own = full_ref[...].astype(f32)
        g = None
        for s in range(ns):
            part = jnp.where(me_ref[0] == s, own, slots_ref[s].astype(f32))
            g = part if g is None else g + part
        go_ref[...] = g
        d_ref[...], mo_ref[...], vo_ref[...] = _adamw_math(w_ref[...], g, m_ref[...], v_ref[...])

    steps = r // rb
    lay = pl.BlockSpec((None, rb, D), lambda i, me_ref: (l, i, 0))
    n_acc = 0 if acc is None else 4
    grid_spec = pltpu.PrefetchScalarGridSpec(
        num_scalar_prefetch=1, grid=(steps,),
        in_specs=[pl.BlockSpec((rb, D), lambda i, me_ref: (me_ref[0] * steps + i, 0)),
                  pl.BlockSpec((ns, rb, D), lambda i, me_ref: (0, i, 0)), lay, lay, lay] + [_ANY] * n_acc,
        out_specs=[lay] * 4)
    outs = pl.pallas_call(
        body, grid_spec=grid_spec, out_shape=[SDS(w.shape, f32)] * 4,
        input_output_aliases={6 + j: j for j in range(n_acc)},
        compiler_params=_CP, name="reduce_adamw")(me, full, slots, w, m, v, *(() if acc is None else acc))
    return tuple(outs)


_BIG = ("ffn1_w_gate", "ffn1_w_up", "ffn1_w_down", "w_in", "w_out", "ffn2_w_gate", "ffn2_w_up", "ffn2_w_down")
_TRANSPOSED = ("ffn1_w_gate", "ffn1_w_up", "w_in", "ffn2_w_gate", "ffn2_w_up")

def _block_diag(pool_w):
    out = jnp.zeros((L, PW, PW), pool_w.dtype)
    for gi in range(4):
        out = out.at[:, 64 * gi:64 * (gi + 1), 64 * gi:64 * (gi + 1)].set(pool_w[:, gi])
    return out


def kernel(x, positions, ffn1_norm, ffn1_w_gate, ffn1_w_up, ffn1_w_down, mix_norm, w_in, pool_w, pool_scale, w_out, ffn2_norm, ffn2_w_gate, ffn2_w_up, ffn2_w_down, final_norm, loss_target, m_ffn1_norm, m_ffn1_w_gate, m_ffn1_w_up, m_ffn1_w_down, m_mix_norm, m_w_in, m_pool_w, m_pool_scale, m_w_out, m_ffn2_norm, m_ffn2_w_gate, m_ffn2_w_up, m_ffn2_w_down, m_final_norm, v_ffn1_norm, v_ffn1_w_gate, v_ffn1_w_up, v_ffn1_w_down, v_mix_norm, v_w_in, v_pool_w, v_pool_scale, v_w_out, v_ffn2_norm, v_ffn2_w_gate, v_ffn2_w_up, v_ffn2_w_down, v_final_norm):
    weights = dict(ffn1_norm=ffn1_norm, ffn1_w_gate=ffn1_w_gate, ffn1_w_up=ffn1_w_up, ffn1_w_down=ffn1_w_down,
                   mix_norm=mix_norm, w_in=w_in, pool_w=pool_w, pool_scale=pool_scale, w_out=w_out,
                   ffn2_norm=ffn2_norm, ffn2_w_gate=ffn2_w_gate, ffn2_w_up=ffn2_w_up, ffn2_w_down=ffn2_w_down,
                   final_norm=final_norm)
    moms = dict(ffn1_norm=m_ffn1_norm, ffn1_w_gate=m_ffn1_w_gate, ffn1_w_up=m_ffn1_w_up, ffn1_w_down=m_ffn1_w_down,
                mix_norm=m_mix_norm, w_in=m_w_in, pool_w=m_pool_w, pool_scale=m_pool_scale, w_out=m_w_out,
                ffn2_norm=m_ffn2_norm, ffn2_w_gate=m_ffn2_w_gate, ffn2_w_up=m_ffn2_w_up, ffn2_w_down=m_ffn2_w_down,
                final_norm=m_final_norm)
    vels = dict(ffn1_norm=v_ffn1_norm, ffn1_w_gate=v_ffn1_w_gate, ffn1_w_up=v_ffn1_w_up, ffn1_w_down=v_ffn1_w_down,
                mix_norm=v_mix_norm, w_in=v_w_in, pool_w=v_pool_w, pool_scale=v_pool_scale, w_out=v_w_out,
                ffn2_norm=v_ffn2_norm, ffn2_w_gate=v_ffn2_w_gate, ffn2_w_up=v_ffn2_w_up, ffn2_w_down=v_ffn2_w_down,
                final_norm=v_final_norm)
    names = list(weights)

    me_idx = 4 * lax.axis_index("x") + 2 * lax.axis_index("y") + lax.axis_index("c")
    me_arr = me_idx.reshape(1).astype(jnp.int32)

    tr = lambda w: jnp.swapaxes(w, 1, 2).astype(bf16)
    shards = [tr(weights[nm]) if nm in _TRANSPOSED else weights[nm].astype(bf16) for nm in _BIG]

    def landing_zones(l, which):
        return _place_own(me_arr, [shards[t] for t in which], l)

    g_ffn1 = [ffn1_norm[l].reshape(1, D) for l in range(L)]
    g_mix = [mix_norm[l].reshape(1, D) for l in range(L)]
    g_ffn2 = [ffn2_norm[l].reshape(1, D) for l in range(L)]
    wbd_all = _block_diag(pool_w).astype(bf16)
    wbd = [wbd_all[l] for l in range(L)]
    pscale = [pool_scale[l].reshape(1, PW) for l in range(L)]
    tabs = _rope_tables(positions)
    flat = lambda a: a.reshape(S, a.shape[-1])
    r4 = lambda a: a.reshape(4, S // 4, a.shape[-1])
    r16 = lambda a: a.reshape(16, S // 16, a.shape[-1])

    first, rest, whole = (0, 1, 2, 3), (4, 5, 6, 7), tuple(range(8))

    def ag_begin(l, which, after):
        tag = f"{l}{'' if which == whole else 'r'}"
        send_sems, recv_sems, zones, token = _ag_start(landing_zones(l, which), after, tag)
        return dict(tag=tag, zones=zones, s=send_sems, r=recv_sems), token

    def ag_second(ch, after):
        ch["ps"], ch["pr"], ch["zones"], token = _ag_pass(ch["zones"], ch["r"], after, ch["tag"])
        return token

    def ag_third(ch, after):
        ch["qs"], ch["qr"], ch["zones"], token = _ag_last(ch["zones"], ch["pr"], after, ch["tag"])
        return token

    def ag_end(ch, after):
        return _ag_wait(ch["zones"], ch["s"], ch["r"], ch["ps"], ch["pr"], ch["qs"], ch["qr"], after, ch["tag"])

    head = _all_gather(landing_zones(0, first))
    ch_rest, tok_rest = ag_begin(0, rest, head[0])
    chains = {}
    chains[1], tok_next = ag_begin(1, whole, head[0])
    gathered = [None] * L
    xs = x.reshape(S, D)
    saved = []
    for l in range(L):
        first_after, second_after = (), ()
        if l == 0:
            gt1, ut1, dn1, wint = head
            first_after = (tok_rest, tok_next)
        else:
            gt1, ut1, dn1, wint, wout, gt2, ut2, dn2 = gathered[l]
        x0 = xs
        x1, gate1, up1 = _ffn_fwd(x0, g_ffn1[l], gt1, ut1, dn1, after=first_after)
        hmix, vp, q1, k1, v1, q4, k4, v4, q16, k16, v16 = _mix_in_fwd(x1, g_mix[l], wint, tabs)
        q4, k4, v4, q16, k16, v16 = map(flat, (q4, k4, v4, q16, k16, v16))
        ypool, diff = _pool_fwd(vp, wbd[l], pscale[l])
        after_attn = None
        if l == 0:
            after_attn = ag_second(ch_rest, [ypool, q16])
        o1, l1 = _attn_fwd(q1, k1, v1, S, after=after_attn)
        o4, l4 = _attn_fwd(q4, k4, v4, S // 4, after=after_attn)
        o16, l16 = _attn_fwd(q16, k16, v16, S // 16, after=after_attn)
        if l == 0:
            token = ag_third(ch_rest, [o1, o4, o16])
            wout, gt2, ut2, dn2 = ag_end(ch_rest, token)
            gathered[0] = list(head) + [wout, gt2, ut2, dn2]
        elif l + 1 < L:
            second_after = (ag_second(chains[l + 1], [o1, o4, o16]),)
        x2, mixed, o, lse1, lse4, lse16 = _mix_out_fwd(x1, ypool, o1, l1, r4(o4), r4(l4), r16(o16), r16(l16), wout)
        if l == 0:
            second_after = (ag_second(chains[1], x2),)
        x3, gate2, up2 = _ffn_fwd(x2, g_ffn2[l], gt2, ut2, dn2, after=second_after)
        if l + 1 < L:
            token = ag_third(chains[l + 1], x3)
            if l + 2 < L:
                chains[l + 2], token = ag_begin(l + 2, whole, token)
            gathered[l + 1] = ag_end(chains[l + 1], token)
        saved.append(dict(x0=x0, x1=x1, x2=x2, gate1=gate1, up1=up1, gate2=gate2, up2=up2, hmix=hmix, diff=diff,
                          qkv=((q1, k1, v1), (q4, k4, v4), (q16, k16, v16)), mixed=mixed, o=o,
                          lse=(lse1, flat(lse4), flat(lse16))))
        xs = x3

    dx, loss_part, d_final = _loss_head(xs, final_norm.reshape(1, D), loss_target.reshape(S, D))

    d_norm = {nm: [None] * L for nm in ("ffn1_norm", "mix_norm", "ffn2_norm")}
    d_poolw, d_pscale = [None] * L, [None] * L
    group_a = ("ffn2_w_gate", "ffn2_w_up", "ffn2_w_down", "w_out")
    group_b = ("ffn1_w_gate", "ffn1_w_up", "ffn1_w_down", "w_in")
    acc = {}

    as_rows = lambda a, nm: jnp.swapaxes(a, 1, 2) if nm in _TRANSPOSED else a
    w_rows = {nm: as_rows(weights[nm], nm) for nm in _BIG}
    m_rows = {nm: as_rows(moms[nm], nm) for nm in _BIG}
    v_rows = {nm: as_rows(vels[nm], nm) for nm in _BIG}

    def exchange(full, group, after, tag):
        srcs = [full[nm] for nm in group]
        slots = [lax.empty((NDEV, g.shape[0] // NDEV, D), bf16) for g in srcs]
        ssem, rsem, srcs, slots, token = _rs_start(srcs, slots, after, tag)
        return (srcs, slots, ssem, rsem, tag), token

    def update(l, group, flight, after):
        srcs, slots, ssem, rsem, tag = flight
        srcs, slots = _rs_wait(srcs, slots, ssem, rsem, after, tag)
        for nm, full_g, slots_g in zip(group, srcs, slots):
            acc[nm] = _reduce_adamw(acc.get(nm), me_arr, full_g, slots_g, w_rows[nm], m_rows[nm], v_rows[nm], l)
        return [acc[nm][0] for nm in group], slots

    core_arr = lax.axis_index("c").reshape(1).astype(jnp.int32)
    chip_arr = (2 * lax.axis_index("x") + lax.axis_index("y")).reshape(1).astype(jnp.int32)

    def exchange_cores(full, group, after, tag):
        full4s = [full[nm].reshape(4, 2, full[nm].shape[0] // NDEV, D) for nm in group]
        bufs = [lax.empty((4,) + a.shape[2:], bf16) for a in full4s]
        ssem, rsem, full4s, bufs, token = _pair_start(full4s, bufs, after, tag)
        return (full4s, bufs, ssem, rsem, tag), token

    def exchange_chips(flight, after):
        full4s, bufs, ssem, rsem, tag = flight
        full4s, bufs = _pair_wait(full4s, bufs, ssem, rsem, after, tag)
        sums = _pair_sum(core_arr, full4s, bufs)
        slots = [lax.empty(a.shape, bf16) for a in sums]
        ssem, rsem, sums, slots, token = _chip_start(sums, slots, bufs[0], tag)
        return (sums, slots, ssem, rsem, tag), token

    def update_chips(l, group, flight, after):
        sums, slots, ssem, rsem, tag = flight
        sums, slots = _chip_wait(sums, slots, ssem, rsem, after, tag)
        for nm, sums_g, slots_g in zip(group, sums, slots):
            own = sums_g.reshape(4 * sums_g.shape[1], D)
            acc[nm] = _reduce_adamw(acc.get(nm), chip_arr, own, slots_g, w_rows[nm], m_rows[nm], v_rows[nm], l)
        return [acc[nm][0] for nm in group]

    flights = {}
    token_b = None
    for l in reversed(range(L)):
        sv = saved[l]
        gt1, ut1, dn1, wint, wout, gt2, ut2, dn2 = gathered[l]
        full = {}
        dx, dgate, dup, h, dy, d_norm["ffn2_norm"][l] = _ffn_bwd_d(
            sv["x2"], g_ffn2[l], sv["gate2"], sv["up2"], dx, gt2, ut2, dn2, after=() if token_b is None else (token_b,))
        full["ffn2_w_gate"], full["ffn2_w_up"], full["ffn2_w_down"] = _ffn_bwd_w(h, dy, sv["gate2"], sv["up2"], dgate, dup)

        dxb, dyp, do1, do4, do16, dl1, dl4, dl16 = _mix_out_bwd(dx, sv["o"], wout)
        full["w_out"] = _wgrad(sv["mixed"], dxb)
        flights[l, "a"], token_a = (exchange_cores if l == 0 else exchange)(full, group_a, dxb, f"a{l}")
        dvp, d_poolw[l], d_pscale[l] = _pool_bwd(dyp, sv["diff"], wbd[l], pscale[l], after=(token_a,))
        dos, dls = (do1, flat(do4), flat(do16)), (dl1, flat(dl4), flat(dl16))
        dqkv = []
        for b, lc in enumerate((S, S // 4, S // 16)):
            qb, kb, vb = sv["qkv"][b]
            dqkv.append(_attn_bwd(qb, kb, vb, dos[b], sv["lse"][b], dls[b], lc))
        d4 = tuple(r4(a) for a in dqkv[1])
        d16 = tuple(r16(a) for a in dqkv[2])
        mix_after = ()
        if l == 0:
            flights[0, "a"], token_a = exchange_chips(flights[0, "a"], [dqkv[0][0], dqkv[1][0], dqkv[2][0]])
            mix_after = (token_a,)
        dx, dproj, d_norm["mix_norm"][l] = _mix_in_bwd(dx, sv["x1"], g_mix[l], wint, tabs, dvp, dqkv[0], d4, d16,
                                                       after=mix_after)
        full["w_in"] = _wgrad(dproj, sv["hmix"])

        dx, dgate, dup, h, dy, d_norm["ffn1_norm"][l] = _ffn_bwd_d(sv["x0"], g_ffn1[l], sv["gate1"], sv["up1"], dx, gt1, ut1, dn1)
        full["ffn1_w_gate"], full["ffn1_w_up"], full["ffn1_w_down"] = _ffn_bwd_w(h, dy, sv["gate1"], sv["up1"], dgate, dup)

        after = dx
        if l + 1 < L and l + 1 >= 2:
            after, _ = update(l + 1, group_a, flights.pop((l + 1, "a")), after)
            after, _ = update(l + 1, group_b, flights.pop((l + 1, "b")), after)
        if l > 0:
            flights[l, "b"], token_b = exchange(full, group_b, after, f"b{l}")

    flights[0, "b"], token_b = exchange_cores(full, group_b, dx, "b0")
    flights[0, "b"], token_b = exchange_chips(flights[0, "b"], token_b)
    pad8 = lambda a: jnp.pad(a, ((0, 8 - a.shape[0]), (0, 0)))
    misc = jnp.concatenate([d_final, jnp.concatenate(d_pscale, axis=1), loss_part], axis=0)
    small = jnp.concatenate(
        [pad8(jnp.concatenate(d_norm[nm], axis=0)) for nm in ("ffn1_norm", "mix_norm", "ffn2_norm")]
        + [pad8(misc), jnp.stack(d_poolw).reshape(L * 16, D)], axis=0)
    small_slots = lax.dynamic_update_slice(lax.empty((NDEV, SMALL_ROWS, D), f32), small[None], (me_idx, 0, 0))
    pack_sems = _rs_start([small], [small_slots], token_b, "pack")

    after = pack_sems[-1]
    for key in [(1, "a"), (1, "b")]:
        after, _ = update(key[0], group_a if key[1] == "a" else group_b, flights.pop(key), after)
    after = update_chips(0, group_a, flights.pop((0, "a")), after)
    after = update_chips(0, group_b, flights.pop((0, "b")), after)
    _, pack_slots = _rs_wait(pack_sems[2], pack_sems[3], pack_sems[0], pack_sems[1], after, "pack")

    sm = _sum_slots(pack_slots[0], SMALL_ROWS)
    grads = {}
    grads["ffn1_norm"], grads["mix_norm"], grads["ffn2_norm"] = sm[0:L], sm[8:8 + L], sm[16:16 + L]
    grads["final_norm"] = sm[24]
    grads["pool_scale"] = sm[25].reshape(L, PW)
    grads["pool_w"] = sm[32:32 + L * 16].reshape(L, 4, 64, 64)
    loss = sm[26, 0]
    upd = {nm: _adamw(weights[nm], grads[nm], moms[nm], vels[nm]) for nm in names if nm not in _BIG}
    for nm in _BIG:
        grads[nm], upd[nm] = as_rows(acc[nm][0], nm), tuple(as_rows(a, nm) for a in acc[nm][1:])
    return (loss, dx.reshape(1, S, D), *[grads[nm] for nm in names], *[upd[nm][0] for nm in names],
            *[upd[nm][1] for nm in names], *[upd[nm][2] for nm in names])
```

```python
import jax
import jax.numpy as jnp
from jax import lax
from jax.experimental import pallas as pl
from jax.experimental.pallas import tpu as pltpu

f32 = jnp.float32
bf16 = jnp.bfloat16
SDS = jax.ShapeDtypeStruct

D = 1024
S = 2048
F = 2816
L = 4
PW = 256
AW = 768
PROJ = PW + 3 * AW
NDEV = 8
TM = 256
QB = 128
HALF = 64
NG = AW // 128
NORM_EPS = 1e-6
MASK_VALUE = -1e30
ROPE_THETA = 500000.0
ADAM_LR, ADAM_B1, ADAM_B2, ADAM_EPS, ADAM_WD, ADAM_STEP = 0.001, 0.9, 0.999, 1e-08, 0.01, 10
POOL_WINDOWS = (2, 4, 8, 16)
PAD = 8
SMALL_ROWS = 96
VMEM_LIMIT = 56 * 1024 * 1024

_CP = pltpu.CompilerParams(vmem_limit_bytes=VMEM_LIMIT)
_ANY = pl.BlockSpec(memory_space=pl.ANY)
_HBM = pl.BlockSpec(memory_space=pltpu.HBM)
_SEM = pl.BlockSpec(memory_space=pltpu.SEMAPHORE)
_MESH = pl.DeviceIdType.MESH
_CP_SPLIT = pltpu.CompilerParams(has_side_effects=pltpu.SideEffectType.DATAFLOW_SIDE_EFFECTING)


def _dot_nn(a, b):
    return lax.dot_general(a, b, (((1,), (0,)), ((), ())), preferred_element_type=f32)


def _dot_nt(a, b):
    return lax.dot_general(a, b, (((1,), (1,)), ((), ())), preferred_element_type=f32)


def _dot_tn(a, b):
    return lax.dot_general(a, b, (((0,), (0,)), ((), ())), preferred_element_type=f32)


def _rms(x, g):
    r = lax.rsqrt(jnp.mean(x * x, axis=-1, keepdims=True) + NORM_EPS)
    xh = x * r
    return r, xh, xh * g


def _rms_bwd(dh, r, xh, g):
    dxh = dh * g
    return r * (dxh - xh * jnp.mean(dxh * xh, axis=-1, keepdims=True))


def _tile(cols):
    return pl.BlockSpec((TM, cols), lambda i: (i, 0))


def _const(shape):
    return pl.BlockSpec(shape, lambda i: (0,) * len(shape))


def _layer(rows, cols):
    return pl.BlockSpec((rows, cols), lambda i: (0, 0), pipeline_mode=pl.Buffered(1))


def _p4(cols=AW):
    return pl.BlockSpec((4, TM // 4, cols), lambda i: (0, i, 0))


def _p16(cols=AW):
    return pl.BlockSpec((16, TM // 16, cols), lambda i: (0, i, 0))


def _cols(j):
    return slice(128 * j, 128 * (j + 1))


def _follow(body, n_in, after):
    k = len(after)
    return body if k == 0 else (lambda *refs: body(*refs[:n_in], *refs[n_in + k:]))


def _ffn_fwd(x, g, gt, ut, dn, after=()):
    def body(x_ref, g_ref, gt_ref, ut_ref, dn_ref, xo_ref, gate_ref, up_ref):
        x = x_ref[...]
        _, _, hn = _rms(x, g_ref[...])
        h = hn.astype(bf16)
        gate = _dot_nt(h, gt_ref[...])
        up = _dot_nt(h, ut_ref[...])
        gate_ref[...] = gate.astype(bf16)
        up_ref[...] = up.astype(bf16)
        a = (gate * jax.nn.sigmoid(gate) * up).astype(bf16)
        xo_ref[...] = x + 0.5 * _dot_nn(a, dn_ref[...])

    return pl.pallas_call(
        _follow(body, 5, after), grid=(S // TM,),
        in_specs=[_tile(D), _layer(1, D), _layer(F, D), _layer(F, D), _layer(F, D)] + [_ANY] * len(after),
        out_specs=[_tile(D), _tile(F), _tile(F)],
        out_shape=[SDS((S, D), f32), SDS((S, F), bf16), SDS((S, F), bf16)],
        compiler_params=_CP, name="ffn_fwd")(x, g, gt, ut, dn, *after)


def _ffn_bwd_d(x, g, gate, up, dxo, gt, ut, dn, after=()):
    def body(x_ref, g_ref, gate_ref, up_ref, dxo_ref, gt_ref, ut_ref, dn_ref,
             dx_ref, dgate_ref, dup_ref, h_ref, dy_ref, dg_ref):
        x = x_ref[...]
        g = g_ref[...]
        r, xh, hn = _rms(x, g)
        h_ref[...] = hn.astype(bf16)
        dxo = dxo_ref[...]
        dy = (0.5 * dxo).astype(bf16)
        dy_ref[...] = dy
        da = _dot_nt(dy, dn_ref[...])
        gate = gate_ref[...].astype(f32)
        up = up_ref[...].astype(f32)
        sg = jax.nn.sigmoid(gate)
        dgate = (da * up * (sg * (1.0 + gate * (1.0 - sg)))).astype(bf16)
        dup = (da * (gate * sg)).astype(bf16)
        dgate_ref[...] = dgate
        dup_ref[...] = dup
        dh = _dot_nn(dgate, gt_ref[...]) + _dot_nn(dup, ut_ref[...])

        @pl.when(pl.program_id(0) == 0)
        def _():
            dg_ref[...] = jnp.zeros_like(dg_ref)

        dg_ref[...] += jnp.sum(dh * xh, axis=0, keepdims=True)
        dx_ref[...] = dxo + _rms_bwd(dh, r, xh, g)

    return pl.pallas_call(
        _follow(body, 8, after), grid=(S // TM,),
        in_specs=[_tile(D), _layer(1, D), _tile(F), _tile(F), _tile(D),
                  _layer(F, D), _layer(F, D), _layer(F, D)] + [_ANY] * len(after),
        out_specs=[_tile(D), _tile(F), _tile(F), _tile(D), _tile(D), _const((1, D))],
        out_shape=[SDS((S, D), f32), SDS((S, F), bf16), SDS((S, F), bf16), SDS((S, D), bf16),
                   SDS((S, D), bf16), SDS((1, D), f32)],
        compiler_params=_CP, name="ffn_bwd_d")(x, g, gate, up, dxo, gt, ut, dn, *after)


def _ffn_bwd_w(h, dy, gate, up, dgate, dup):
    fc = 256

    def body(h_ref, dy_ref, gate_ref, up_ref, dgate_ref, dup_ref, dgt_ref, dut_ref, ddn_ref):
        gate = gate_ref[...].astype(f32)
        a = (gate * jax.nn.sigmoid(gate) * up_ref[...].astype(f32)).astype(bf16)
        ddn_ref[...] = _dot_tn(a, dy_ref[...]).astype(bf16)
        h = h_ref[...]
        dgt_ref[...] = _dot_tn(dgate_ref[...], h).astype(bf16)
        dut_ref[...] = _dot_tn(dup_ref[...], h).astype(bf16)

    col = pl.BlockSpec((S, fc), lambda j: (0, j))
    row = pl.BlockSpec((fc, D), lambda j: (j, 0))
    full = pl.BlockSpec((S, D), lambda j: (0, 0))
    return pl.pallas_call(
        body, grid=(F // fc,),
        in_specs=[full, full, col, col, col, col],
        out_specs=[row, row, row],
        out_shape=[SDS((F, D), bf16)] * 3,
        compiler_params=_CP, name="ffn_bwd_w")(h, dy, gate, up, dgate, dup)


def _wgrad(a, b):
    m, n = a.shape[1], b.shape[1]
    mc = 256

    def body(a_ref, b_ref, o_ref):
        o_ref[...] = _dot_tn(a_ref[...], b_ref[...]).astype(bf16)

    return pl.pallas_call(
        body, grid=(m // mc,),
        in_specs=[pl.BlockSpec((S, mc), lambda j: (0, j)), pl.BlockSpec((S, n), lambda j: (0, 0))],
        out_specs=pl.BlockSpec((mc, n), lambda j: (j, 0)),
        out_shape=SDS((m, n), bf16),
        compiler_params=_CP, name="wgrad")(a, b)


def _rope(t, c, sn, sp):
    return t * c + pltpu.roll(t, 120, 1) * sn + pltpu.roll(t, 8, 1) * sp


def _rope_bwd(d, c, sn, sp):
    return d * c + pltpu.roll(d * sn, 8, 1) + pltpu.roll(d * sp, 120, 1)


def _rope_tables(positions):
    inv_freq = ROPE_THETA ** (-jnp.arange(0, 16, 2, dtype=f32) / 16)
    ang = positions.reshape(S, 1).astype(f32) * inv_freq
    cos, sin = jnp.cos(ang), jnp.sin(ang)
    one = jnp.ones((S, 48), f32)
    zero8 = jnp.zeros((S, 8), f32)
    zero48 = jnp.zeros((S, 48), f32)
    c = jnp.concatenate([cos, cos, one], axis=1)
    sn = jnp.concatenate([-sin, zero8, zero48], axis=1)
    sp = jnp.concatenate([zero8, sin, zero48], axis=1)
    return tuple(jnp.concatenate([t, t], axis=1) for t in (c, sn, sp))


def _dilation_perm(n, back=False):
    per = TM // n
    i = lax.broadcasted_iota(jnp.int32, (TM, TM), 1 if back else 0)
    j = lax.broadcasted_iota(jnp.int32, (TM, TM), 0 if back else 1)
    return jnp.where(j == n * (i % per) + i // per, 1.0, 0.0).astype(bf16)


def _mix_in_fwd(x, g, wint, tabs):
    def body(x_ref, g_ref, w_ref, c_ref, sn_ref, sp_ref,
             h_ref, vp_ref, q1, k1, v1, q4, k4, v4, q16, k16, v16):
        _, _, hn = _rms(x_ref[...], g_ref[...])
        h = hn.astype(bf16)
        h_ref[...] = h
        proj = _dot_nt(h, w_ref[...])
        vp_ref[...] = proj[:, :PW]
        c, sn, sp = c_ref[...], sn_ref[...], sp_ref[...]
        perm4, perm16 = _dilation_perm(4), _dilation_perm(16)
        for kind, (o1, o4, o16) in enumerate(((q1, q4, q16), (k1, k4, k16), (v1, v4, v16))):
            for j in range(NG):
                t = proj[:, PW + kind * AW + 128 * j: PW + kind * AW + 128 * (j + 1)]
                if kind == 0:
                    t = _rope(t, c, sn, sp) * 0.125
                elif kind == 1:
                    t = _rope(t, c, sn, sp)
                o1[:, _cols(j)] = t.astype(bf16)
            nat = o1[...]
            o4[...] = _dot_nn(perm4, nat).astype(bf16).reshape(4, TM // 4, AW)
            o16[...] = _dot_nn(perm16, nat).astype(bf16).reshape(16, TM // 16, AW)

    nat, d4, d16 = SDS((S, AW), bf16), SDS((4, S // 4, AW), bf16), SDS((16, S // 16, AW), bf16)
    return pl.pallas_call(
        body, grid=(S // TM,),
        in_specs=[_tile(D), _layer(1, D), _layer(PROJ, D), _tile(128), _tile(128), _tile(128)],
        out_specs=[_tile(D), _tile(PW)] + [_tile(AW)] * 3 + [_p4()] * 3 + [_p16()] * 3,
        out_shape=[SDS((S, D), bf16), SDS((S, PW), f32)] + [nat] * 3 + [d4] * 3 + [d16] * 3,
        compiler_params=_CP, name="mix_in_fwd")(x, g, wint, *tabs)


def _mix_in_bwd(dxo, x, g, wint, tabs, dvp, d1, d4, d16, after=()):
    def body(dxo_ref, x_ref, g_ref, w_ref, c_ref, sn_ref, sp_ref, dvp_ref,
             dq1, dk1, dv1, dq4, dk4, dv4, dq16, dk16, dv16,
             dx_ref, dproj_ref, dg_ref):
        c, sn, sp = c_ref[...], sn_ref[...], sp_ref[...]
        dproj_ref[:, :PW] = dvp_ref[...].astype(bf16)
        back4, back16 = _dilation_perm(4, True), _dilation_perm(16, True)
        for kind, (a1, a4, a16) in enumerate(((dq1, dq4, dq16), (dk1, dk4, dk16), (dv1, dv4, dv16))):
            n4 = _dot_nn(back4, a4[...].reshape(TM, AW))
            n16 = _dot_nn(back16, a16[...].reshape(TM, AW))
            for j in range(NG):
                t = a1[:, _cols(j)].astype(f32) + n4[:, _cols(j)] + n16[:, _cols(j)]
                if kind == 0:
                    t = _rope_bwd(t * 0.125, c, sn, sp)
                elif kind == 1:
                    t = _rope_bwd(t, c, sn, sp)
                dproj_ref[:, PW + kind * AW + 128 * j: PW + kind * AW + 128 * (j + 1)] = t.astype(bf16)
        g = g_ref[...]
        r_, xh, _ = _rms(x_ref[...], g)
        dh = _dot_nn(dproj_ref[...], w_ref[...])

        @pl.when(pl.program_id(0) == 0)
        def _():
            dg_ref[...] = jnp.zeros_like(dg_ref)

        dg_ref[...] += jnp.sum(dh * xh, axis=0, keepdims=True)
        dx_ref[...] = dxo_ref[...] + _rms_bwd(dh, r_, xh, g)

    return pl.pallas_call(
        _follow(body, 17, after), grid=(S // TM,),
        in_specs=[_tile(D), _tile(D), _layer(1, D), _layer(PROJ, D), _tile(128), _tile(128), _tile(128),
                  _tile(PW)] + [_tile(AW)] * 3 + [_p4()] * 3 + [_p16()] * 3 + [_ANY] * len(after),
        out_specs=[_tile(D), _tile(PROJ), _const((1, D))],
        out_shape=[SDS((S, D), f32), SDS((S, PROJ), bf16), SDS((1, D), f32)],
        compiler_params=_CP, name="mix_in_bwd")(dxo, x, g, wint, *tabs, dvp, *d1, *d4, *d16, *after)


def _pool_sums(pad_ref, base, rows, adjoint):
    lane_group = lax.broadcasted_iota(jnp.int32, (rows, PW), 1) // 64
    sign = -1 if adjoint else 1

    def sh(o):
        return pad_ref[pl.ds(PAD + base + sign * o, rows), :]

    out = None
    acc = None
    lo, hi = 0, 0
    for gi, w in enumerate(POOL_WINDOWS):
        for o in list(range(-(w // 2), lo)) + list(range(hi, w - w // 2)):
            acc = sh(o) if acc is None else acc + sh(o)
        lo, hi = -(w // 2), w - w // 2
        out = acc if out is None else jnp.where(lane_group >= gi, acc, out)
    return out


def _pool_counts(base, rows):
    pos = base + lax.broadcasted_iota(jnp.int32, (rows, PW), 0)
    lane_group = lax.broadcasted_iota(jnp.int32, (rows, PW), 1) // 64
    cnt = None
    for gi, w in enumerate(POOL_WINDOWS):
        lo = jnp.maximum(pos - w // 2, 0)
        hi = jnp.minimum(pos + w - 1 - w // 2, S - 1)
        c = (hi - lo + 1).astype(f32)
        cnt = c if cnt is None else jnp.where(lane_group >= gi, c, cnt)
    return cnt


def _pool_fwd(vp, wbd, scale):
    ch = 256

    def body(vp_ref, w_ref, sc_ref, y_ref, diff_ref, pad):
        pad[pl.ds(0, PAD), :] = jnp.zeros((PAD, PW), f32)
        pad[pl.ds(PAD + S, PAD), :] = jnp.zeros((PAD, PW), f32)
        pad[pl.ds(PAD, S), :] = vp_ref[...]
        for b in range(S // ch):
            base = b * ch
            pooled = _pool_sums(pad, base, ch, False) / _pool_counts(base, ch)
            diff = (pooled - vp_ref[pl.ds(base, ch), :]).astype(bf16)
            diff_ref[pl.ds(base, ch), :] = diff
            y_ref[pl.ds(base, ch), :] = _dot_nn(diff, w_ref[...]) * sc_ref[...]

    whole = lambda shape: pl.BlockSpec(shape, lambda i: (0,) * len(shape))
    return pl.pallas_call(
        body, grid=(1,),
        in_specs=[whole((S, PW)), whole((PW, PW)), whole((1, PW))],
        out_specs=[whole((S, PW)), whole((S, PW))],
        out_shape=[SDS((S, PW), f32), SDS((S, PW), bf16)],
        scratch_shapes=[pltpu.VMEM((S + 2 * PAD, PW), f32)],
        compiler_params=_CP, name="pool_fwd")(vp, wbd, scale)


def _pool_bwd(dy, diff, wbd, scale, after=()):
    ch = 256

    def body(dy_ref, diff_ref, w_ref, sc_ref, dvp_ref, dw_ref, dsc_ref, pad):
        pad[pl.ds(0, PAD), :] = jnp.zeros((PAD, PW), f32)
        pad[pl.ds(PAD + S, PAD), :] = jnp.zeros((PAD, PW), f32)
        dw = jnp.zeros((PW, PW), f32)
        dsc = jnp.zeros((1, PW), f32)
        for b in range(S // ch):
            base = b * ch
            dy = dy_ref[pl.ds(base, ch), :]
            diff = diff_ref[pl.ds(base, ch), :]
            dsc = dsc + jnp.sum(dy * _dot_nn(diff, w_ref[...]), axis=0, keepdims=True)
            dz = (dy * sc_ref[...]).astype(bf16)
            dw = dw + _dot_tn(diff, dz)
            ddiff = _dot_nt(dz, w_ref[...])
            dvp_ref[pl.ds(base, ch), :] = -ddiff
            pad[pl.ds(PAD + base, ch), :] = ddiff / _pool_counts(base, ch)
        for gi in range(4):
            dw_ref[gi] = dw[64 * gi:64 * (gi + 1), 64 * gi:64 * (gi + 1)]
        dsc_ref[...] = dsc
        for b in range(S // ch):
            base = b * ch
            dvp_ref[pl.ds(base, ch), :] += _pool_sums(pad, base, ch, True)

    whole = lambda shape: pl.BlockSpec(shape, lambda i: (0,) * len(shape))
    return pl.pallas_call(
        _follow(body, 4, after), grid=(1,),
        in_specs=[whole((S, PW)), whole((S, PW)), whole((PW, PW)), whole((1, PW))] + [_ANY] * len(after),
        out_specs=[whole((S, PW)), whole((4, 64, 64)), whole((1, PW))],
        out_shape=[SDS((S, PW), f32), SDS((4, 64, 64), f32), SDS((1, PW), f32)],
        scratch_shapes=[pltpu.VMEM((S + 2 * PAD, PW), f32)],
        compiler_params=_CP, name="pool_bwd")(dy, diff, wbd, scale, *after)


def _attn_blocks(lc):
    bpc = lc // QB
    kw = min(2 * QB, lc)
    blocks = []
    for b in range(S // QB):
        t0 = (b % bpc) * QB
        ks_in = min(max(t0 - HALF, 0), lc - kw)
        blocks.append((b * QB, (b // bpc) * lc + ks_in, t0 - ks_in))
    return kw, blocks


def _attn_bias(bias_ref, kw, shifts):
    r = lax.broadcasted_iota(jnp.int32, (2 * QB, kw), 0) % QB
    c = lax.broadcasted_iota(jnp.int32, (2 * QB, kw), 1)
    for i, shift in enumerate(shifts):
        bias_ref[i] = jnp.where(jnp.abs(r + shift - c) <= HALF, 0.0, MASK_VALUE).astype(f32)


def _head_put(stats, pair, v0, v1, lane):
    return jnp.where(lane == 2 * pair, v0, jnp.where(lane == 2 * pair + 1, v1, stats))


def _head_cols(stats, pair, lane):
    c0 = jnp.sum(jnp.where(lane == 2 * pair, stats, 0.0), axis=-1, keepdims=True)
    c1 = jnp.sum(jnp.where(lane == 2 * pair + 1, stats, 0.0), axis=-1, keepdims=True)
    return jnp.concatenate([c0, c1], axis=0)


def _head_spread(stats, pair, head0):
    return jnp.where(head0, stats[:, 2 * pair:2 * pair + 1], stats[:, 2 * pair + 1:2 * pair + 2])


def _stack_heads(blk, head0):
    zero = jnp.zeros_like(blk)
    return jnp.concatenate([jnp.where(head0, blk, zero), jnp.where(head0, zero, blk)], axis=0)


def _attn_fwd(q, k, v, lc, after=None):
    kw, blocks = _attn_blocks(lc)
    shifts = sorted({b[2] for b in blocks})

    def body(q_ref, k_ref, v_ref, *refs):
        o_ref, lse_ref, bias_ref = refs[-3:]
        lane = lax.broadcasted_iota(jnp.int32, (QB, 128), 1)
        head0 = lane < 64
        pair = pl.program_id(0)
        _attn_bias(bias_ref, kw, shifts)

        @pl.when(pair == 0)
        def _():
            lse_ref[...] = jnp.zeros_like(lse_ref)

        for row0, kstart, shift in blocks:
            q2 = _stack_heads(q_ref[pl.ds(row0, QB), :], head0)
            kb = k_ref[pl.ds(kstart, kw), :]
            vb = v_ref[pl.ds(kstart, kw), :]
            s = _dot_nt(q2, kb) + bias_ref[shifts.index(shift)]
            m = jnp.max(s, axis=-1, keepdims=True)
            p = jnp.exp(s - m)
            den = jnp.sum(p, axis=-1, keepdims=True)
            o2 = _dot_nn(p.astype(bf16), vb) / den
            lse2 = m + jnp.log(den)
            o_ref[pl.ds(row0, QB), :] = jnp.where(head0, o2[:QB], o2[QB:]).astype(bf16)
            lse_ref[pl.ds(row0, QB), :] = _head_put(lse_ref[pl.ds(row0, QB), :], pair, lse2[:QB], lse2[QB:], lane)

    col = pl.BlockSpec((S, 128), lambda p: (0, p))
    extra = () if after is None else (after,)
    return pl.pallas_call(
        body, grid=(NG,), in_specs=[col, col, col] + [_ANY] * len(extra),
        out_specs=[col, pl.BlockSpec((S, 128), lambda p: (0, 0))],
        out_shape=[SDS((S, AW), bf16), SDS((S, 128), f32)],
        scratch_shapes=[pltpu.VMEM((len(shifts), 2 * QB, kw), f32)],
        compiler_params=_CP, name=f"attn_fwd_{lc}")(q, k, v, *extra)


def _attn_bwd(q, k, v, do, lse, delta, lc):
    kw, blocks = _attn_blocks(lc)
    shifts = sorted({b[2] for b in blocks})

    def body(q_ref, k_ref, v_ref, do_ref, lse_ref, dl_ref, dq_ref, dk_out, dv_out, bias_ref, dk_ref, dv_ref):
        lane = lax.broadcasted_iota(jnp.int32, (QB, 128), 1)
        head0 = lane < 64
        pair = pl.program_id(0)
        _attn_bias(bias_ref, kw, shifts)
        dk_ref[...] = jnp.zeros_like(dk_ref)
        dv_ref[...] = jnp.zeros_like(dv_ref)
        for row0, kstart, shift in blocks:
            q2 = _stack_heads(q_ref[pl.ds(row0, QB), :], head0)
            do2 = _stack_heads(do_ref[pl.ds(row0, QB), :], head0)
            lse2 = _head_cols(lse_ref[pl.ds(row0, QB), :], pair, lane)
            dl2 = _head_cols(dl_ref[pl.ds(row0, QB), :], pair, lane)
            kb = k_ref[pl.ds(kstart, kw), :]
            vb = v_ref[pl.ds(kstart, kw), :]
            p = jnp.exp(_dot_nt(q2, kb) + bias_ref[shifts.index(shift)] - lse2)
            ds = (p * (_dot_nt(do2, vb) - dl2)).astype(bf16)
            dq2 = _dot_nn(ds, kb)
            dq_ref[pl.ds(row0, QB), :] = jnp.where(head0, dq2[:QB], dq2[QB:]).astype(bf16)
            dk_ref[pl.ds(kstart, kw), :] += _dot_tn(ds, q2)
            dv_ref[pl.ds(kstart, kw), :] += _dot_tn(p.astype(bf16), do2)
        dk_out[...] = dk_ref[...].astype(bf16)
        dv_out[...] = dv_ref[...].astype(bf16)

    col = pl.BlockSpec((S, 128), lambda p: (0, p))
    stats = pl.BlockSpec((S, 128), lambda p: (0, 0))
    return pl.pallas_call(
        body, grid=(NG,), in_specs=[col] * 4 + [stats] * 2, out_specs=[col] * 3,
        out_shape=[SDS((S, AW), bf16)] * 3,
        scratch_shapes=[pltpu.VMEM((len(shifts), 2 * QB, kw), f32), pltpu.VMEM((S, 128), f32),
                        pltpu.VMEM((S, 128), f32)],
        compiler_params=_CP, name=f"attn_bwd_{lc}")(q, k, v, do, lse, delta)


def _mix_out_fwd(x, ypool, o1, l1, o4, l4, o16, l16, wout):
    def body(x_ref, yp_ref, o1_ref, l1_ref, o4_ref, l4_ref, o16_ref, l16_ref, w_ref,
             xo_ref, mixed_ref, o_ref, lse1_ref, lse4_ref, lse16_ref, sl4, sl16, sl):
        head0 = lax.broadcasted_iota(jnp.int32, (TM, 128), 1) < 64
        for r in range(4):
            sl4[pl.ds(r, TM // 4, stride=4), :] = l4_ref[r]
        for r in range(16):
            sl16[pl.ds(r, TM // 16, stride=16), :] = l16_ref[r]
        n4 = _dot_nn(_dilation_perm(4, True), o4_ref[...].reshape(TM, AW))
        n16 = _dot_nn(_dilation_perm(16, True), o16_ref[...].reshape(TM, AW))
        a, b, c = l1_ref[...], sl4[...], sl16[...]
        m = jnp.maximum(jnp.maximum(a, b), c)
        wa, wb, wc = jnp.exp(a - m), jnp.exp(b - m), jnp.exp(c - m)
        den = wa + wb + wc
        wa, wb, wc = wa / den, wb / den, wc / den
        lse = m + jnp.log(den)
        lse1_ref[...] = lse
        sl[...] = lse
        mixed_ref[:, :PW] = yp_ref[...].astype(bf16)
        for j in range(NG):
            y = (_head_spread(wa, j, head0) * o1_ref[:, _cols(j)].astype(f32)
                 + _head_spread(wb, j, head0) * n4[:, _cols(j)] + _head_spread(wc, j, head0) * n16[:, _cols(j)])
            o_ref[:, _cols(j)] = y
            mixed_ref[:, PW + 128 * j: PW + 128 * (j + 1)] = y.astype(bf16)
        for r in range(4):
            lse4_ref[r] = sl[pl.ds(r, TM // 4, stride=4), :]
        for r in range(16):
            lse16_ref[r] = sl[pl.ds(r, TM // 16, stride=16), :]
        xo_ref[...] = x_ref[...] + _dot_nn(mixed_ref[...], w_ref[...])

    return pl.pallas_call(
        body, grid=(S // TM,),
        in_specs=[_tile(D), _tile(PW), _tile(AW), _tile(128), _p4(), _p4(128), _p16(), _p16(128), _layer(D, D)],
        out_specs=[_tile(D), _tile(D), _tile(AW), _tile(128), _p4(128), _p16(128)],
        out_shape=[SDS((S, D), f32), SDS((S, D), bf16), SDS((S, AW), f32), SDS((S, 128), f32),
                   SDS((4, S // 4, 128), f32), SDS((16, S // 16, 128), f32)],
        scratch_shapes=[pltpu.VMEM((TM, 128), f32)] * 3,
        compiler_params=_CP, name="mix_out_fwd")(x, ypool, o1, l1, o4, l4, o16, l16, wout)


def _mix_out_bwd(dxo, o, wout):
    def body(dxo_ref, o_ref, w_ref, dxb_ref, dyp_ref, do1, do4, do16, dl1, dl4, dl16, sdl):
        dxb = dxo_ref[...].astype(bf16)
        dxb_ref[...] = dxb
        dm = _dot_nt(dxb, w_ref[...])
        dyp_ref[...] = dm[:, :PW]
        lane = lax.broadcasted_iota(jnp.int32, (TM, 128), 1)
        head0 = lane < 64
        dl = jnp.zeros((TM, 128), f32)
        for j in range(NG):
            d = dm[:, PW + 128 * j: PW + 128 * (j + 1)]
            prod = d * o_ref[:, _cols(j)]
            dl = _head_put(dl, j, jnp.sum(jnp.where(head0, prod, 0.0), axis=-1, keepdims=True),
                           jnp.sum(jnp.where(head0, 0.0, prod), axis=-1, keepdims=True), lane)
            do1[:, _cols(j)] = d.astype(bf16)
        dl1[...] = dl
        sdl[...] = dl
        for r in range(4):
            dl4[r] = sdl[pl.ds(r, TM // 4, stride=4), :]
        for r in range(16):
            dl16[r] = sdl[pl.ds(r, TM // 16, stride=16), :]
        nat = do1[...]
        do4[...] = _dot_nn(_dilation_perm(4), nat).astype(bf16).reshape(4, TM // 4, AW)
        do16[...] = _dot_nn(_dilation_perm(16), nat).astype(bf16).reshape(16, TM // 16, AW)

    return pl.pallas_call(
        body, grid=(S // TM,),
        in_specs=[_tile(D), _tile(AW), _layer(D, D)],
        out_specs=[_tile(D), _tile(PW), _tile(AW), _p4(), _p16(), _tile(128), _p4(128), _p16(128)],
        out_shape=[SDS((S, D), bf16), SDS((S, PW), f32),
                   SDS((S, AW), bf16), SDS((4, S // 4, AW), bf16), SDS((16, S // 16, AW), bf16),
                   SDS((S, 128), f32), SDS((4, S // 4, 128), f32), SDS((16, S // 16, 128), f32)],
        scratch_shapes=[pltpu.VMEM((TM, 128), f32)],
        compiler_params=_CP, name="mix_out_bwd")(dxo, o, wout)


def _loss_head(x, g, target):
    def body(x_ref, g_ref, t_ref, dx_ref, loss_ref, dg_ref):
        g = g_ref[...]
        r, xh, y = _rms(x_ref[...], g)
        err = y - t_ref[...]
        dy = err * (1.0 / D)

        @pl.when(pl.program_id(0) == 0)
        def _():
            loss_ref[...] = jnp.zeros_like(loss_ref)
            dg_ref[...] = jnp.zeros_like(dg_ref)

        loss_ref[...] += jnp.broadcast_to(0.5 * jnp.sum(jnp.mean(err * err, axis=-1, keepdims=True)), (1, D))
        dg_ref[...] += jnp.sum(dy * xh, axis=0, keepdims=True)
        dx_ref[...] = _rms_bwd(dy, r, xh, g)

    return pl.pallas_call(
        body, grid=(S // TM,),
        in_specs=[_tile(D), _const((1, D)), _tile(D)],
        out_specs=[_tile(D), _const((1, D)), _const((1, D))],
        out_shape=[SDS((S, D), f32), SDS((1, D), f32), SDS((1, D), f32)],
        compiler_params=_CP, name="loss_head")(x, g, target)


def _peer(k):
    x, y, c = lax.axis_index("x"), lax.axis_index("y"), lax.axis_index("c")
    px = 1 - x if k & 4 else x
    py = 1 - y if k & 2 else y
    pc = 1 - c if k & 1 else c
    return (px, py, pc), 4 * px + 2 * py + pc


def _diag_route():
    x, y, c = lax.axis_index("x"), lax.axis_index("y"), lax.axis_index("c")
    idx_x, idx_y = _peer(4)[1], _peer(2)[1]
    return idx_x + c * (idx_y - idx_x), (x + c * (1 - 2 * x), (1 - y) + c * (2 * y - 1), c)


def _all_gather(lands):
    n = len(lands)

    def body(*refs):
        zones, send_sems, recv_sems = refs[n:2 * n], refs[2 * n], refs[2 * n + 1]
        me, me_idx = _peer(0)
        sibling, sib_idx = _peer(1)
        (x_nbr, idx_x), (y_nbr, idx_y), idx_d = _peer(4), _peer(2), _peer(6)[1]
        fwd_idx, fwd_dev = _diag_route()

        def copy(k, t, idx, to):
            return _row_copy(zones[t], idx, send_sems.at[k, t], recv_sems.at[k, t], to)

        sent = []

        def send(k, t, idx, to):
            cp = copy(k, t, idx, to)
            cp.start()
            sent.append(cp)

        for t in range(n):
            send(0, t, me_idx, sibling)
            send(1, t, me_idx, x_nbr)
            send(2, t, me_idx, y_nbr)
        for t in range(n):
            copy(1, t, idx_x, me).wait_recv()
            send(3, t, idx_x, sibling)
        for t in range(n):
            copy(2, t, idx_y, me).wait_recv()
            send(4, t, idx_y, sibling)
        for t in range(n):
            send(5, t, fwd_idx, fwd_dev)
        for t in range(n):
            copy(5, t, idx_d, me).wait_recv()
            send(6, t, idx_d, sibling)
        for k, mask in ((0, 1), (3, 5), (4, 3), (6, 7)):
            for t in range(n):
                copy(k, t, _peer(mask)[1], me).wait_recv()
        for cp in sent:
            cp.wait_send()

    return pl.pallas_call(
        body, in_specs=[_ANY] * n, out_specs=[_ANY] * n,
        out_shape=[SDS(a.shape, a.dtype) for a in lands], input_output_aliases={t: t for t in range(n)},
        scratch_shapes=[pltpu.SemaphoreType.DMA((7, n)), pltpu.SemaphoreType.DMA((7, n))],
        name="all_gather_weights")(*lands)


def _hbm(a):
    return pltpu.with_memory_space_constraint(a, pltpu.HBM)


def _rows(ref, idx):
    r = ref.shape[0] // NDEV
    return ref.at[pl.ds(idx * r, r), :]


def _row_copy(ref, idx, send_sem, recv_sem, to):
    return pltpu.make_async_remote_copy(src_ref=_rows(ref, idx), dst_ref=_rows(ref, idx), send_sem=send_sem,
                                        recv_sem=recv_sem, device_id=to, device_id_type=_MESH)


def _place_own(me, shards, l):
    n = len(shards)

    def body(me_ref, *refs):
        for t in range(n):
            refs[n + t][...] = refs[t][...]

    grid_spec = pltpu.PrefetchScalarGridSpec(
        num_scalar_prefetch=1, grid=(1,),
        in_specs=[pl.BlockSpec((None, s.shape[1], D), lambda i, me_ref: (l, 0, 0)) for s in shards],
        out_specs=[pl.BlockSpec((s.shape[1], D), lambda i, me_ref: (me_ref[0], 0)) for s in shards])
    return pl.pallas_call(
        body, grid_spec=grid_spec, out_shape=[SDS((NDEV * s.shape[1], D), s.dtype) for s in shards],
        compiler_params=_CP, name="place_own")(me, *shards)


_TOKEN = SDS((8, 128), f32)
def _ag_start(lands, after, l):
    n = len(lands)
    after = list(after) if isinstance(after, (list, tuple)) else [after]

    def body(*refs):
        zones, send_sems, recv_sems, token = refs[:n], refs[n + len(after)], refs[n + len(after) + 1], refs[-1]
        _, me_idx = _peer(0)
        for k, mask in enumerate((1, 4, 2)):
            for t in range(n):
                _row_copy(zones[t], me_idx, send_sems.at[k * n + t], recv_sems.at[k * n + t], _peer(mask)[0]).start()
        token[...] = jnp.zeros_like(token)

    outs = pl.pallas_call(
        body, name=f"ag_start_{l}", in_specs=[_HBM] * n + [_ANY] * len(after),
        out_specs=(_SEM, _SEM, *[_HBM] * n, pl.BlockSpec(memory_space=pltpu.VMEM)),
        out_shape=(pltpu.SemaphoreType.DMA((3 * n,)), pltpu.SemaphoreType.DMA((3 * n,)),
                   *[pltpu.HBM(a.shape, a.dtype) for a in lands], _TOKEN),
        input_output_aliases={t: 2 + t for t in range(n)}, compiler_params=_CP_SPLIT)(
            *[_hbm(a) for a in lands], *after)
    return outs[0], outs[1], list(outs[2:2 + n]), outs[-1]


def _ag_pass(lands, recv_sems, after, l):
    n = len(lands)
    after = list(after) if isinstance(after, (list, tuple)) else [after]

    def body(*refs):
        zones, recv_sems = refs[:n], refs[n]
        psend, precv, token = refs[n + 1 + len(after)], refs[n + 2 + len(after)], refs[-1]
        me, _ = _peer(0)
        sibling, _ = _peer(1)
        for j, mask in enumerate((4, 2)):
            idx = _peer(mask)[1]
            for t in range(n):
                _row_copy(zones[t], idx, psend.at[j * n + t], recv_sems.at[(1 + j) * n + t], me).wait_recv()
                _row_copy(zones[t], idx, psend.at[j * n + t], precv.at[j * n + t], sibling).start()
        fwd_idx, fwd_dev = _diag_route()
        for t in range(n):
            _row_copy(zones[t], fwd_idx, psend.at[2 * n + t], precv.at[2 * n + t], fwd_dev).start()
        token[...] = jnp.zeros_like(token)

    outs = pl.pallas_call(
        body, name=f"ag_pass_{l}", in_specs=[_HBM] * n + [_SEM] + [_ANY] * len(after),
        out_specs=(_SEM, _SEM, *[_HBM] * n, pl.BlockSpec(memory_space=pltpu.VMEM)),
        out_shape=(pltpu.SemaphoreType.DMA((3 * n,)), pltpu.SemaphoreType.DMA((3 * n,)),
                   *[pltpu.HBM(a.shape, a.dtype) for a in lands], _TOKEN),
        input_output_aliases={t: 2 + t for t in range(n)}, compiler_params=_CP_SPLIT)(*lands, recv_sems, *after)
    return outs[0], outs[1], list(outs[2:2 + n]), outs[-1]


def _ag_last(lands, precv, after, l):
    n = len(lands)
    after = list(after) if isinstance(after, (list, tuple)) else [after]

    def body(*refs):
        zones, precv = refs[:n], refs[n]
        qsend, qrecv, token = refs[n + 1 + len(after)], refs[n + 2 + len(after)], refs[-1]
        me, _ = _peer(0)
        sibling, _ = _peer(1)
        idx = _peer(6)[1]
        for t in range(n):
            _row_copy(zones[t], idx, qsend.at[t], precv.at[2 * n + t], me).wait_recv()
            _row_copy(zones[t], idx, qsend.at[t], qrecv.at[t], sibling).start()
        token[...] = jnp.zeros_like(token)

    outs = pl.pallas_call(
        body, name=f"ag_last_{l}", in_specs=[_HBM] * n + [_SEM] + [_ANY] * len(after),
        out_specs=(_SEM, _SEM, *[_HBM] * n, pl.BlockSpec(memory_space=pltpu.VMEM)),
        out_shape=(pltpu.SemaphoreType.DMA((n,)), pltpu.SemaphoreType.DMA((n,)),
                   *[pltpu.HBM(a.shape, a.dtype) for a in lands], _TOKEN),
        input_output_aliases={t: 2 + t for t in range(n)}, compiler_params=_CP_SPLIT)(*lands, precv, *after)
    return outs[0], outs[1], list(outs[2:2 + n]), outs[-1]


def _ag_wait(lands, send_sems, recv_sems, psend, precv, qsend, qrecv, after, l):
    n = len(lands)
    after = list(after) if isinstance(after, (list, tuple)) else [after]

    def body(*refs):
        zones = refs[:n]
        send_sems, recv_sems, psend, precv, qsend, qrecv = refs[n:n + 6]
        me, me_idx = _peer(0)
        for k in range(3):
            for t in range(n):
                _row_copy(zones[t], me_idx, send_sems.at[k * n + t], recv_sems.at[k * n + t], me).wait_send()
        for t in range(n):
            _row_copy(zones[t], _peer(1)[1], send_sems.at[t], recv_sems.at[t], me).wait_recv()
        fwd_idx, _ = _diag_route()
        for j, (mine, theirs) in enumerate(((_peer(4)[1], _peer(5)[1]), (_peer(2)[1], _peer(3)[1]))):
            for t in range(n):
                _row_copy(zones[t], mine, psend.at[j * n + t], precv.at[j * n + t], me).wait_send()
                _row_copy(zones[t], theirs, psend.at[j * n + t], precv.at[j * n + t], me).wait_recv()
        for t in range(n):
            _row_copy(zones[t], fwd_idx, psend.at[2 * n + t], precv.at[2 * n + t], me).wait_send()
            _row_copy(zones[t], _peer(6)[1], qsend.at[t], qrecv.at[t], me).wait_send()
            _row_copy(zones[t], _peer(7)[1], qsend.at[t], qrecv.at[t], me).wait_recv()

    outs = pl.pallas_call(
        body, name=f"ag_wait_{l}", in_specs=[_HBM] * n + [_SEM] * 6 + [_ANY] * len(after),
        out_specs=tuple([_HBM] * n), out_shape=tuple(pltpu.HBM(a.shape, a.dtype) for a in lands),
        input_output_aliases={t: t for t in range(n)}, compiler_params=_CP_SPLIT)(
            *lands, send_sems, recv_sems, psend, precv, qsend, qrecv, *after)
    return list(outs)


def _xchg_src(ref, slot_ref, idx):
    return _rows(ref, idx) if ref.shape[0] == NDEV * slot_ref.shape[1] else ref


def _rs_start(srcs, slots, after, tag):
    n = len(srcs)
    after = list(after) if isinstance(after, (list, tuple)) else [after]

    def body(*refs):
        src, slot = refs[:n], refs[n:2 * n]
        send_sems, recv_sems, token = refs[2 * n + len(after)], refs[2 * n + len(after) + 1], refs[-1]
        _, me_idx = _peer(0)
        for k in range(1, NDEV):
            dev, idx = _peer(k)
            for t in range(n):
                pltpu.make_async_remote_copy(
                    src_ref=_xchg_src(src[t], slot[t], idx), dst_ref=slot[t].at[me_idx],
                    send_sem=send_sems.at[(k - 1) * n + t], recv_sem=recv_sems.at[(k - 1) * n + t],
                    device_id=dev, device_id_type=_MESH).start()
        token[...] = jnp.zeros_like(token)

    outs = pl.pallas_call(
        body, name=f"rs_start_{tag}", in_specs=[_HBM] * (2 * n) + [_ANY] * len(after),
        out_specs=(_SEM, _SEM, *[_HBM] * (2 * n), pl.BlockSpec(memory_space=pltpu.VMEM)),
        out_shape=(pltpu.SemaphoreType.DMA(((NDEV - 1) * n,)), pltpu.SemaphoreType.DMA(((NDEV - 1) * n,)),
                   *[pltpu.HBM(a.shape, a.dtype) for a in list(srcs) + list(slots)], _TOKEN),
        input_output_aliases={t: 2 + t for t in range(2 * n)}, compiler_params=_CP_SPLIT)(
            *[_hbm(a) for a in list(srcs) + list(slots)], *after)
    return outs[0], outs[1], list(outs[2:2 + n]), list(outs[2 + n:2 + 2 * n]), outs[-1]


def _rs_wait(srcs, slots, send_sems, recv_sems, after, tag):
    n = len(srcs)
    after = list(after) if isinstance(after, (list, tuple)) else [after]

    def body(*refs):
        src, slot, send_sems, recv_sems = refs[:n], refs[n:2 * n], refs[2 * n], refs[2 * n + 1]
        me, _ = _peer(0)
        for k in range(1, NDEV):
            idx = _peer(k)[1]
            for t in range(n):
                cp = pltpu.make_async_remote_copy(
                    src_ref=_xchg_src(src[t], slot[t], idx), dst_ref=slot[t].at[idx],
                    send_sem=send_sems.at[(k - 1) * n + t], recv_sem=recv_sems.at[(k - 1) * n + t],
                    device_id=me, device_id_type=_MESH)
                cp.wait_send()
                cp.wait_recv()

    outs = pl.pallas_call(
        body, name=f"rs_wait_{tag}", in_specs=[_HBM] * (2 * n) + [_SEM, _SEM] + [_ANY] * len(after),
        out_specs=tuple([_HBM] * (2 * n)),
        out_shape=tuple(pltpu.HBM(a.shape, a.dtype) for a in list(srcs) + list(slots)),
        input_output_aliases={t: t for t in range(2 * n)}, compiler_params=_CP_SPLIT)(
            *srcs, *slots, send_sems, recv_sems, *after)
    return list(outs[:n]), list(outs[n:])


def _pair_start(full4s, bufs, after, tag):
    n = len(full4s)
    after = list(after) if isinstance(after, (list, tuple)) else [after]

    def body(*refs):
        full, buf = refs[:n], refs[n:2 * n]
        send_sems, recv_sems, token = refs[2 * n + len(after)], refs[2 * n + len(after) + 1], refs[-1]
        c = lax.axis_index("c")
        for t in range(n):
            pltpu.make_async_remote_copy(src_ref=full[t].at[:, 1 - c], dst_ref=buf[t], send_sem=send_sems.at[t],
                                         recv_sem=recv_sems.at[t], device_id=_peer(1)[0], device_id_type=_MESH).start()
        token[...] = jnp.zeros_like(token)

    outs = pl.pallas_call(
        body, name=f"pair_start_{tag}", in_specs=[_HBM] * (2 * n) + [_ANY] * len(after),
        out_specs=(_SEM, _SEM, *[_HBM] * (2 * n), pl.BlockSpec(memory_space=pltpu.VMEM)),
        out_shape=(pltpu.SemaphoreType.DMA((n,)), pltpu.SemaphoreType.DMA((n,)),
                   *[pltpu.HBM(a.shape, a.dtype) for a in list(full4s) + list(bufs)], _TOKEN),
        input_output_aliases={t: 2 + t for t in range(2 * n)}, compiler_params=_CP_SPLIT)(
            *[_hbm(a) for a in list(full4s) + list(bufs)], *after)
    return outs[0], outs[1], list(outs[2:2 + n]), list(outs[2 + n:2 + 2 * n]), outs[-1]


def _pair_wait(full4s, bufs, send_sems, recv_sems, after, tag):
    n = len(full4s)
    after = list(after) if isinstance(after, (list, tuple)) else [after]

    def body(*refs):
        full, buf, send_sems, recv_sems = refs[:n], refs[n:2 * n], refs[2 * n], refs[2 * n + 1]
        c = lax.axis_index("c")
        for t in range(n):
            cp = pltpu.make_async_remote_copy(src_ref=full[t].at[:, 1 - c], dst_ref=buf[t], send_sem=send_sems.at[t],
                                              recv_sem=recv_sems.at[t], device_id=_peer(0)[0], device_id_type=_MESH)
            cp.wait_send()
            cp.wait_recv()

    outs = pl.pallas_call(
        body, name=f"pair_wait_{tag}", in_specs=[_HBM] * (2 * n) + [_SEM, _SEM] + [_ANY] * len(after),
        out_specs=tuple([_HBM] * (2 * n)),
        out_shape=tuple(pltpu.HBM(a.shape, a.dtype) for a in list(full4s) + list(bufs)),
        input_output_aliases={t: t for t in range(2 * n)}, compiler_params=_CP_SPLIT)(
            *full4s, *bufs, send_sems, recv_sems, *after)
    return list(outs[:n]), list(outs[n:])


def _pair_sum(core, full4s, bufs):
    n = len(full4s)

    def body(core_ref, *refs):
        for t in range(n):
            refs[2 * n + t][...] = (refs[t][...].astype(f32) + refs[n + t][...].astype(f32)).astype(bf16)

    grid_spec = pltpu.PrefetchScalarGridSpec(
        num_scalar_prefetch=1, grid=(4,),
        in_specs=[pl.BlockSpec((None, None) + a.shape[2:], lambda j, core_ref: (j, core_ref[0], 0, 0)) for a in full4s]
        + [pl.BlockSpec((None,) + b.shape[1:], lambda j, core_ref: (j, 0, 0)) for b in bufs],
        out_specs=[pl.BlockSpec((None,) + b.shape[1:], lambda j, core_ref: (j, 0, 0)) for b in bufs])
    return pl.pallas_call(
        body, grid_spec=grid_spec, out_shape=[SDS(b.shape, bf16) for b in bufs],
        compiler_params=_CP, name="pair_sum")(core, *full4s, *bufs)


def _chip_start(sums, slots, after, tag):
    n = len(sums)
    after = list(after) if isinstance(after, (list, tuple)) else [after]

    def body(*refs):
        src, slot = refs[:n], refs[n:2 * n]
        send_sems, recv_sems, token = refs[2 * n + len(after)], refs[2 * n + len(after) + 1], refs[-1]
        my_chip = 2 * lax.axis_index("x") + lax.axis_index("y")
        for k, mask in enumerate((4, 2, 6)):
            dev, _ = _peer(mask)
            for t in range(n):
                pltpu.make_async_remote_copy(
                    src_ref=src[t].at[2 * dev[0] + dev[1]], dst_ref=slot[t].at[my_chip],
                    send_sem=send_sems.at[k * n + t], recv_sem=recv_sems.at[k * n + t],
                    device_id=dev, device_id_type=_MESH).start()
        token[...] = jnp.zeros_like(token)

    outs = pl.pallas_call(
        body, name=f"chip_start_{tag}", in_specs=[_HBM] * (2 * n) + [_ANY] * len(after),
        out_specs=(_SEM, _SEM, *[_HBM] * (2 * n), pl.BlockSpec(memory_space=pltpu.VMEM)),
        out_shape=(pltpu.SemaphoreType.DMA((3 * n,)), pltpu.SemaphoreType.DMA((3 * n,)),
                   *[pltpu.HBM(a.shape, a.dtype) for a in list(sums) + list(slots)], _TOKEN),
        input_output_aliases={t: 2 + t for t in range(2 * n)}, compiler_params=_CP_SPLIT)(
            *[_hbm(a) for a in list(sums) + list(slots)], *after)
    return outs[0], outs[1], list(outs[2:2 + n]), list(outs[2 + n:2 + 2 * n]), outs[-1]


def _chip_wait(sums, slots, send_sems, recv_sems, after, tag):
    n = len(sums)
    after = list(after) if isinstance(after, (list, tuple)) else [after]

    def body(*refs):
        src, slot, send_sems, recv_sems = refs[:n], refs[n:2 * n], refs[2 * n], refs[2 * n + 1]
        for k, mask in enumerate((4, 2, 6)):
            dev, _ = _peer(mask)
            chip = 2 * dev[0] + dev[1]
            for t in range(n):
                cp = pltpu.make_async_remote_copy(
                    src_ref=src[t].at[chip], dst_ref=slot[t].at[chip],
                    send_sem=send_sems.at[k * n + t], recv_sem=recv_sems.at[k * n + t],
                    device_id=_peer(0)[0], device_id_type=_MESH)
                cp.wait_send()
                cp.wait_recv()

    outs = pl.pallas_call(
        body, name=f"chip_wait_{tag}", in_specs=[_HBM] * (2 * n) + [_SEM, _SEM] + [_ANY] * len(after),
        out_specs=tuple([_HBM] * (2 * n)),
        out_shape=tuple(pltpu.HBM(a.shape, a.dtype) for a in list(sums) + list(slots)),
        input_output_aliases={t: t for t in range(2 * n)}, compiler_params=_CP_SPLIT)(
            *sums, *slots, send_sems, recv_sems, *after)
    return list(outs[:n]), list(outs[n:])


def _sum_slots(slots, rb):
    r = slots.shape[1]

    def body(s_ref, o_ref):
        acc = s_ref[0].astype(f32)
        for s in range(1, NDEV):
            acc = acc + s_ref[s].astype(f32)
        o_ref[...] = acc

    return pl.pallas_call(
        body, grid=(r // rb,),
        in_specs=[pl.BlockSpec((NDEV, rb, D), lambda i: (0, i, 0))],
        out_specs=pl.BlockSpec((rb, D), lambda i: (i, 0)),
        out_shape=SDS((r, D), f32), compiler_params=_CP, name="sum_slots")(slots)


def _adamw(w, g, m, v):
    shape = w.shape
    cols = shape[-1]
    rows = w.size // cols
    rb = rows
    for cand in (512, 256, 128, 64, 32, 16, 8):
        if rows % cand == 0 and rows > cand:
            rb = cand
            break

    def body(w_ref, g_ref, m_ref, v_ref, d_ref, mo_ref, vo_ref):
        d_ref[...], mo_ref[...], vo_ref[...] = _adamw_math(w_ref[...], g_ref[...], m_ref[...], v_ref[...])

    spec = pl.BlockSpec((rb, cols), lambda i: (i, 0))
    outs = pl.pallas_call(
        body, grid=(rows // rb,), in_specs=[spec] * 4, out_specs=[spec] * 3,
        out_shape=[SDS((rows, cols), f32)] * 3, compiler_params=_CP, name="adamw")(
            *(a.reshape(rows, cols) for a in (w, g, m, v)))
    return tuple(o.reshape(shape) for o in outs)


def _adamw_math(w, g, m, v):
    m = ADAM_B1 * m + (1.0 - ADAM_B1) * g
    v = ADAM_B2 * v + (1.0 - ADAM_B2) * (g * g)
    m_hat = m / (1.0 - ADAM_B1 ** ADAM_STEP)
    v_hat = v / (1.0 - ADAM_B2 ** ADAM_STEP)
    return -ADAM_LR * (m_hat / (jnp.sqrt(v_hat) + ADAM_EPS) + ADAM_WD * w), m, v


def _reduce_adamw(acc, me, full, slots, w, m, v, l):
    _, r, _ = w.shape
    ns = slots.shape[0]
    rb = r // 2 if r > 128 else r

    def body(me_ref, full_ref, slots_ref, w_ref, m_ref, v_ref, *refs):
        go_ref, d_ref, mo_ref, vo_ref = refs[-4:]
        own = full_ref[...].astype(f32)
        g = None
        for s in range(ns):
            part = jnp.where(me_ref[0] == s, own, slots_ref[s].astype(f32))
            g = part if g is None else g + part
        go_ref[...] = g
        d_ref[...], mo_ref[...], vo_ref[...] = _adamw_math(w_ref[...], g, m_ref[...], v_ref[...])

    steps = r // rb
    lay = pl.BlockSpec((None, rb, D), lambda i, me_ref: (l, i, 0))
    n_acc = 0 if acc is None else 4
    grid_spec = pltpu.PrefetchScalarGridSpec(
        num_scalar_prefetch=1, grid=(steps,),
        in_specs=[pl.BlockSpec((rb, D), lambda i, me_ref: (me_ref[0] * steps + i, 0)),
                  pl.BlockSpec((ns, rb, D), lambda i, me_ref: (0, i, 0)), lay, lay, lay] + [_ANY] * n_acc,
        out_specs=[lay] * 4)
    outs = pl.pallas_call(
        body, grid_spec=grid_spec, out_shape=[SDS(w.shape, f32)] * 4,
        input_output_aliases={6 + j: j for j in range(n_acc)},
        compiler_params=_CP, name="reduce_adamw")(me, full, slots, w, m, v, *(() if acc is None else acc))
    return tuple(outs)


_BIG = ("ffn1_w_gate", "ffn1_w_up", "ffn1_w_down", "w_in", "w_out", "ffn2_w_gate", "ffn2_w_up", "ffn2_w_down")
_TRANSPOSED = ("ffn1_w_gate", "ffn1_w_up", "w_in", "ffn2_w_gate", "ffn2_w_up")

def _block_diag(pool_w):
    out = jnp.zeros((L, PW, PW), pool_w.dtype)
    for gi in range(4):
        out = out.at[:, 64 * gi:64 * (gi + 1), 64 * gi:64 * (gi + 1)].set(pool_w[:, gi])
    return out


def kernel(x, positions, ffn1_norm, ffn1_w_gate, ffn1_w_up, ffn1_w_down, mix_norm, w_in, pool_w, pool_scale, w_out, ffn2_norm, ffn2_w_gate, ffn2_w_up, ffn2_w_down, final_norm, loss_target, m_ffn1_norm, m_ffn1_w_gate, m_ffn1_w_up, m_ffn1_w_down, m_mix_norm, m_w_in, m_pool_w, m_pool_scale, m_w_out, m_ffn2_norm, m_ffn2_w_gate, m_ffn2_w_up, m_ffn2_w_down, m_final_norm, v_ffn1_norm, v_ffn1_w_gate, v_ffn1_w_up, v_ffn1_w_down, v_mix_norm, v_w_in, v_pool_w, v_pool_scale, v_w_out, v_ffn2_norm, v_ffn2_w_gate, v_ffn2_w_up, v_ffn2_w_down, v_final_norm):
    weights = dict(ffn1_norm=ffn1_norm, ffn1_w_gate=ffn1_w_gate, ffn1_w_up=ffn1_w_up, ffn1_w_down=ffn1_w_down,
                   mix_norm=mix_norm, w_in=w_in, pool_w=pool_w, pool_scale=pool_scale, w_out=w_out,
                   ffn2_norm=ffn2_norm, ffn2_w_gate=ffn2_w_gate, ffn2_w_up=ffn2_w_up, ffn2_w_down=ffn2_w_down,
                   final_norm=final_norm)
    moms = dict(ffn1_norm=m_ffn1_norm, ffn1_w_gate=m_ffn1_w_gate, ffn1_w_up=m_ffn1_w_up, ffn1_w_down=m_ffn1_w_down,
                mix_norm=m_mix_norm, w_in=m_w_in, pool_w=m_pool_w, pool_scale=m_pool_scale, w_out=m_w_out,
                ffn2_norm=m_ffn2_norm, ffn2_w_gate=m_ffn2_w_gate, ffn2_w_up=m_ffn2_w_up, ffn2_w_down=m_ffn2_w_down,
                final_norm=m_final_norm)
    vels = dict(ffn1_norm=v_ffn1_norm, ffn1_w_gate=v_ffn1_w_gate, ffn1_w_up=v_ffn1_w_up, ffn1_w_down=v_ffn1_w_down,
                mix_norm=v_mix_norm, w_in=v_w_in, pool_w=v_pool_w, pool_scale=v_pool_scale, w_out=v_w_out,
                ffn2_norm=v_ffn2_norm, ffn2_w_gate=v_ffn2_w_gate, ffn2_w_up=v_ffn2_w_up, ffn2_w_down=v_ffn2_w_down,
                final_norm=v_final_norm)
    names = list(weights)

    me_idx = 4 * lax.axis_index("x") + 2 * lax.axis_index("y") + lax.axis_index("c")
    me_arr = me_idx.reshape(1).astype(jnp.int32)

    tr = lambda w: jnp.swapaxes(w, 1, 2).astype(bf16)
    shards = [tr(weights[nm]) if nm in _TRANSPOSED else weights[nm].astype(bf16) for nm in _BIG]

    def landing_zones(l, which):
        return _place_own(me_arr, [shards[t] for t in which], l)

    g_ffn1 = [ffn1_norm[l].reshape(1, D) for l in range(L)]
    g_mix = [mix_norm[l].reshape(1, D) for l in range(L)]
    g_ffn2 = [ffn2_norm[l].reshape(1, D) for l in range(L)]
    wbd_all = _block_diag(pool_w).astype(bf16)
    wbd = [wbd_all[l] for l in range(L)]
    pscale = [pool_scale[l].reshape(1, PW) for l in range(L)]
    tabs = _rope_tables(positions)
    flat = lambda a: a.reshape(S, a.shape[-1])
    r4 = lambda a: a.reshape(4, S // 4, a.shape[-1])
    r16 = lambda a: a.reshape(16, S // 16, a.shape[-1])

    first, rest, whole = (0, 1, 2, 3), (4, 5, 6, 7), tuple(range(8))

    def ag_begin(l, which, after):
        tag = f"{l}{'' if which == whole else 'r'}"
        send_sems, recv_sems, zones, token = _ag_start(landing_zones(l, which), after, tag)
        return dict(tag=tag, zones=zones, s=send_sems, r=recv_sems), token

    def ag_second(ch, after):
        ch["ps"], ch["pr"], ch["zones"], token = _ag_pass(ch["zones"], ch["r"], after, ch["tag"])
        return token

    def ag_third(ch, after):
        ch["qs"], ch["qr"], ch["zones"], token = _ag_last(ch["zones"], ch["pr"], after, ch["tag"])
        return token

    def ag_end(ch, after):
        return _ag_wait(ch["zones"], ch["s"], ch["r"], ch["ps"], ch["pr"], ch["qs"], ch["qr"], after, ch["tag"])

    head = _all_gather(landing_zones(0, first))
    ch_rest, tok_rest = ag_begin(0, rest, head[0])
    chains = {}
    chains[1], tok_next = ag_begin(1, whole, head[0])
    gathered = [None] * L
    xs = x.reshape(S, D)
    saved = []
    for l in range(L):
        first_after, second_after = (), ()
        if l == 0:
            gt1, ut1, dn1, wint = head
            first_after = (tok_rest, tok_next)
        else:
            gt1, ut1, dn1, wint, wout, gt2, ut2, dn2 = gathered[l]
        x0 = xs
        x1, gate1, up1 = _ffn_fwd(x0, g_ffn1[l], gt1, ut1, dn1, after=first_after)
        hmix, vp, q1, k1, v1, q4, k4, v4, q16, k16, v16 = _mix_in_fwd(x1, g_mix[l], wint, tabs)
        q4, k4, v4, q16, k16, v16 = map(flat, (q4, k4, v4, q16, k16, v16))
        ypool, diff = _pool_fwd(vp, wbd[l], pscale[l])
        after_attn = None
        if l == 0:
            after_attn = ag_second(ch_rest, [ypool, q16])
        o1, l1 = _attn_fwd(q1, k1, v1, S, after=after_attn)
        o4, l4 = _attn_fwd(q4, k4, v4, S // 4, after=after_attn)
        o16, l16 = _attn_fwd(q16, k16, v16, S // 16, after=after_attn)
        if l == 0:
            token = ag_third(ch_rest, [o1, o4, o16])
            wout, gt2, ut2, dn2 = ag_end(ch_rest, token)
            gathered[0] = list(head) + [wout, gt2, ut2, dn2]
        elif l + 1 < L:
            second_after = (ag_second(chains[l + 1], [o1, o4, o16]),)
        x2, mixed, o, lse1, lse4, lse16 = _mix_out_fwd(x1, ypool, o1, l1, r4(o4), r4(l4), r16(o16), r16(l16), wout)
        if l == 0:
            second_after = (ag_second(chains[1], x2),)
        x3, gate2, up2 = _ffn_fwd(x2, g_ffn2[l], gt2, ut2, dn2, after=second_after)
        if l + 1 < L:
            token = ag_third(chains[l + 1], x3)
            if l + 2 < L:
                chains[l + 2], token = ag_begin(l + 2, whole, token)
            gathered[l + 1] = ag_end(chains[l + 1], token)
        saved.append(dict(x0=x0, x1=x1, x2=x2, gate1=gate1, up1=up1, gate2=gate2, up2=up2, hmix=hmix, diff=diff,
                          qkv=((q1, k1, v1), (q4, k4, v4), (q16, k16, v16)), mixed=mixed, o=o,
                          lse=(lse1, flat(lse4), flat(lse16))))
        xs = x3

    dx, loss_part, d_final = _loss_head(xs, final_norm.reshape(1, D), loss_target.reshape(S, D))

    d_norm = {nm: [None] * L for nm in ("ffn1_norm", "mix_norm", "ffn2_norm")}
    d_poolw, d_pscale = [None] * L, [None] * L
    group_a = ("ffn2_w_gate", "ffn2_w_up", "ffn2_w_down", "w_out")
    group_b = ("ffn1_w_gate", "ffn1_w_up", "ffn1_w_down", "w_in")
    acc = {}

    as_rows = lambda a, nm: jnp.swapaxes(a, 1, 2) if nm in _TRANSPOSED else a
    w_rows = {nm: as_rows(weights[nm], nm) for nm in _BIG}
    m_rows = {nm: as_rows(moms[nm], nm) for nm in _BIG}
    v_rows = {nm: as_rows(vels[nm], nm) for nm in _BIG}

    def exchange(full, group, after, tag):
        srcs = [full[nm] for nm in group]
        slots = [lax.empty((NDEV, g.shape[0] // NDEV, D), bf16) for g in srcs]
        ssem, rsem, srcs, slots, token = _rs_start(srcs, slots, after, tag)
        return (srcs, slots, ssem, rsem, tag), token

    def update(l, group, flight, after):
        srcs, slots, ssem, rsem, tag = flight
        srcs, slots = _rs_wait(srcs, slots, ssem, rsem, after, tag)
        for nm, full_g, slots_g in zip(group, srcs, slots):
            acc[nm] = _reduce_adamw(acc.get(nm), me_arr, full_g, slots_g, w_rows[nm], m_rows[nm], v_rows[nm], l)
        return [acc[nm][0] for nm in group], slots

    core_arr = lax.axis_index("c").reshape(1).astype(jnp.int32)
    chip_arr = (2 * lax.axis_index("x") + lax.axis_index("y")).reshape(1).astype(jnp.int32)

    def exchange_cores(full, group, after, tag):
        full4s = [full[nm].reshape(4, 2, full[nm].shape[0] // NDEV, D) for nm in group]
        bufs = [lax.empty((4,) + a.shape[2:], bf16) for a in full4s]
        ssem, rsem, full4s, bufs, token = _pair_start(full4s, bufs, after, tag)
        return (full4s, bufs, ssem, rsem, tag), token

    def exchange_chips(flight, after):
        full4s, bufs, ssem, rsem, tag = flight
        full4s, bufs = _pair_wait(full4s, bufs, ssem, rsem, after, tag)
        sums = _pair_sum(core_arr, full4s, bufs)
        slots = [lax.empty(a.shape, bf16) for a in sums]
        ssem, rsem, sums, slots, token = _chip_start(sums, slots, bufs[0], tag)
        return (sums, slots, ssem, rsem, tag), token

    def update_chips(l, group, flight, after):
        sums, slots, ssem, rsem, tag = flight
        sums, slots = _chip_wait(sums, slots, ssem, rsem, after, tag)
        for nm, sums_g, slots_g in zip(group, sums, slots):
            own = sums_g.reshape(4 * sums_g.shape[1], D)
            acc[nm] = _reduce_adamw(acc.get(nm), chip_arr, own, slots_g, w_rows[nm], m_rows[nm], v_rows[nm], l)
        return [acc[nm][0] for nm in group]

    flights = {}
    token_b = None
    for l in reversed(range(L)):
        sv = saved[l]
        gt1, ut1, dn1, wint, wout, gt2, ut2, dn2 = gathered[l]
        full = {}
        dx, dgate, dup, h, dy, d_norm["ffn2_norm"][l] = _ffn_bwd_d(
            sv["x2"], g_ffn2[l], sv["gate2"], sv["up2"], dx, gt2, ut2, dn2, after=() if token_b is None else (token_b,))
        full["ffn2_w_gate"], full["ffn2_w_up"], full["ffn2_w_down"] = _ffn_bwd_w(h, dy, sv["gate2"], sv["up2"], dgate, dup)

        dxb, dyp, do1, do4, do16, dl1, dl4, dl16 = _mix_out_bwd(dx, sv["o"], wout)
        full["w_out"] = _wgrad(sv["mixed"], dxb)
        flights[l, "a"], token_a = (exchange_cores if l == 0 else exchange)(full, group_a, dxb, f"a{l}")
        dvp, d_poolw[l], d_pscale[l] = _pool_bwd(dyp, sv["diff"], wbd[l], pscale[l], after=(token_a,))
        dos, dls = (do1, flat(do4), flat(do16)), (dl1, flat(dl4), flat(dl16))
        dqkv = []
        for b, lc in enumerate((S, S // 4, S // 16)):
            qb, kb, vb = sv["qkv"][b]
            dqkv.append(_attn_bwd(qb, kb, vb, dos[b], sv["lse"][b], dls[b], lc))
        d4 = tuple(r4(a) for a in dqkv[1])
        d16 = tuple(r16(a) for a in dqkv[2])
        mix_after = ()
        if l == 0:
            flights[0, "a"], token_a = exchange_chips(flights[0, "a"], [dqkv[0][0], dqkv[1][0], dqkv[2][0]])
            mix_after = (token_a,)
        dx, dproj, d_norm["mix_norm"][l] = _mix_in_bwd(dx, sv["x1"], g_mix[l], wint, tabs, dvp, dqkv[0], d4, d16,
                                                       after=mix_after)
        full["w_in"] = _wgrad(dproj, sv["hmix"])

        dx, dgate, dup, h, dy, d_norm["ffn1_norm"][l] = _ffn_bwd_d(sv["x0"], g_ffn1[l], sv["gate1"], sv["up1"], dx, gt1, ut1, dn1)
        full["ffn1_w_gate"], full["ffn1_w_up"], full["ffn1_w_down"] = _ffn_bwd_w(h, dy, sv["gate1"], sv["up1"], dgate, dup)

        after = dx
        if l + 1 < L and l + 1 >= 2:
            after, _ = update(l + 1, group_a, flights.pop((l + 1, "a")), after)
            after, _ = update(l + 1, group_b, flights.pop((l + 1, "b")), after)
        if l > 0:
            flights[l, "b"], token_b = exchange(full, group_b, after, f"b{l}")

    flights[0, "b"], token_b = exchange_cores(full, group_b, dx, "b0")
    flights[0, "b"], token_b = exchange_chips(flights[0, "b"], token_b)
    pad8 = lambda a: jnp.pad(a, ((0, 8 - a.shape[0]), (0, 0)))
    misc = jnp.concatenate([d_final, jnp.concatenate(d_pscale, axis=1), loss_part], axis=0)
    small = jnp.concatenate(
        [pad8(jnp.concatenate(d_norm[nm], axis=0)) for nm in ("ffn1_norm", "mix_norm", "ffn2_norm")]
        + [pad8(misc), jnp.stack(d_poolw).reshape(L * 16, D)], axis=0)
    small_slots = lax.dynamic_update_slice(lax.empty((NDEV, SMALL_ROWS, D), f32), small[None], (me_idx, 0, 0))
    pack_sems = _rs_start([small], [small_slots], token_b, "pack")

    after = pack_sems[-1]
    for key in [(1, "a"), (1, "b")]:
        after, _ = update(key[0], group_a if key[1] == "a" else group_b, flights.pop(key), after)
    after = update_chips(0, group_a, flights.pop((0, "a")), after)
    after = update_chips(0, group_b, flights.pop((0, "b")), after)
    _, pack_slots = _rs_wait(pack_sems[2], pack_sems[3], pack_sems[0], pack_sems[1], after, "pack")

    sm = _sum_slots(pack_slots[0], SMALL_ROWS)
    grads = {}
    grads["ffn1_norm"], grads["mix_norm"], grads["ffn2_norm"] = sm[0:L], sm[8:8 + L], sm[16:16 + L]
    grads["final_norm"] = sm[24]
    grads["pool_scale"] = sm[25].reshape(L, PW)
    grads["pool_w"] = sm[32:32 + L * 16].reshape(L, 4, 64, 64)
    loss = sm[26, 0]
    upd = {nm: _adamw(weights[nm], grads[nm], moms[nm], vels[nm]) for nm in names if nm not in _BIG}
    for nm in _BIG:
        grads[nm], upd[nm] = as_rows(acc[nm][0], nm), tuple(as_rows(a, nm) for a in acc[nm][1:])
    return (loss, dx.reshape(1, S, D), *[grads[nm] for nm in names], *[upd[nm][0] for nm in names],
            *[upd[nm][1] for nm in names], *[upd[nm][2] for nm in names])
```

```python
import jax
import jax.numpy as jnp
from jax import lax
from jax.experimental import pallas as pl
from jax.experimental.pallas import tpu as pltpu

f32 = jnp.float32
bf16 = jnp.bfloat16
SDS = jax.ShapeDtypeStruct

D = 1024
S = 2048
F = 2816
L = 4
PW = 256
AW = 768
PROJ = PW + 3 * AW
NDEV = 8
TM = 256
QB = 128
HALF = 64
NG = AW // 128
NORM_EPS = 1e-6
MASK_VALUE = -1e30
ROPE_THETA = 500000.0
ADAM_LR, ADAM_B1, ADAM_B2, ADAM_EPS, ADAM_WD, ADAM_STEP = 0.001, 0.9, 0.999, 1e-08, 0.01, 10
POOL_WINDOWS = (2, 4, 8, 16)
PAD = 8
SMALL_ROWS = 96
VMEM_LIMIT = 56 * 1024 * 1024

_CP = pltpu.CompilerParams(vmem_limit_bytes=VMEM_LIMIT)
_ANY = pl.BlockSpec(memory_space=pl.ANY)
_HBM = pl.BlockSpec(memory_space=pltpu.HBM)
_SEM = pl.BlockSpec(memory_space=pltpu.SEMAPHORE)
_MESH = pl.DeviceIdType.MESH
_CP_SPLIT = pltpu.CompilerParams(has_side_effects=pltpu.SideEffectType.DATAFLOW_SIDE_EFFECTING)


def _dot_nn(a, b):
    return lax.dot_general(a, b, (((1,), (0,)), ((), ())), preferred_element_type=f32)


def _dot_nt(a, b):
    return lax.dot_general(a, b, (((1,), (1,)), ((), ())), preferred_element_type=f32)


def _dot_tn(a, b):
    return lax.dot_general(a, b, (((0,), (0,)), ((), ())), preferred_element_type=f32)


def _rms(x, g):
    r = lax.rsqrt(jnp.mean(x * x, axis=-1, keepdims=True) + NORM_EPS)
    xh = x * r
    return r, xh, xh * g


def _rms_bwd(dh, r, xh, g):
    dxh = dh * g
    return r * (dxh - xh * jnp.mean(dxh * xh, axis=-1, keepdims=True))


def _tile(cols, rows=TM):
    return pl.BlockSpec((rows, cols), lambda i: (i, 0))


def _const(shape):
    return pl.BlockSpec(shape, lambda i: (0,) * len(shape))


def _layer(rows, cols):
    return pl.BlockSpec((rows, cols), lambda i: (0, 0), pipeline_mode=pl.Buffered(1))


def _p4(cols=AW):
    return pl.BlockSpec((4, TM // 4, cols), lambda i: (0, i, 0))


def _p16(cols=AW):
    return pl.BlockSpec((16, TM // 16, cols), lambda i: (0, i, 0))


def _cols(j):
    return slice(128 * j, 128 * (j + 1))


def _follow(body, n_in, after):
    k = len(after)
    return body if k == 0 else (lambda *refs: body(*refs[:n_in], *refs[n_in + k:]))


def _ffn_fwd(x, g, gt, ut, dn, after=()):
    def body(x_ref, g_ref, gt_ref, ut_ref, dn_ref, xo_ref, gate_ref, up_ref):
        x = x_ref[...]
        _, _, hn = _rms(x, g_ref[...])
        h = hn.astype(bf16)
        gate = _dot_nt(h, gt_ref[...])
        up = _dot_nt(h, ut_ref[...])
        gate_ref[...] = gate.astype(bf16)
        up_ref[...] = up.astype(bf16)
        a = (gate * jax.nn.sigmoid(gate) * up).astype(bf16)
        xo_ref[...] = x + 0.5 * _dot_nn(a, dn_ref[...])

    rows = 2 * TM
    return pl.pallas_call(
        _follow(body, 5, after), grid=(S // rows,),
        in_specs=[_tile(D, rows), _layer(1, D), _layer(F, D), _layer(F, D), _layer(F, D)] + [_ANY] * len(after),
        out_specs=[_tile(D, rows), _tile(F, rows), _tile(F, rows)],
        out_shape=[SDS((S, D), f32), SDS((S, F), bf16), SDS((S, F), bf16)],
        compiler_params=_CP, name="ffn_fwd")(x, g, gt, ut, dn, *after)


def _ffn_bwd_d(x, g, gate, up, dxo, gt, ut, dn, after=()):
    def body(x_ref, g_ref, gate_ref, up_ref, dxo_ref, gt_ref, ut_ref, dn_ref,
             dx_ref, dgate_ref, dup_ref, h_ref, dy_ref, dg_ref):
        x = x_ref[...]
        g = g_ref[...]
        r, xh, hn = _rms(x, g)
        h_ref[...] = hn.astype(bf16)
        dxo = dxo_ref[...]
        dy = (0.5 * dxo).astype(bf16)
        dy_ref[...] = dy
        da = _dot_nt(dy, dn_ref[...])
        gate = gate_ref[...].astype(f32)
        up = up_ref[...].astype(f32)
        sg = jax.nn.sigmoid(gate)
        dgate = (da * up * (sg * (1.0 + gate * (1.0 - sg)))).astype(bf16)
        dup = (da * (gate * sg)).astype(bf16)
        dgate_ref[...] = dgate
        dup_ref[...] = dup
        dh = _dot_nn(dgate, gt_ref[...]) + _dot_nn(dup, ut_ref[...])

        @pl.when(pl.program_id(0) == 0)
        def _():
            dg_ref[...] = jnp.zeros_like(dg_ref)

        dg_ref[...] += jnp.sum(dh * xh, axis=0, keepdims=True)
        dx_ref[...] = dxo + _rms_bwd(dh, r, xh, g)

    return pl.pallas_call(
        _follow(body, 8, after), grid=(S // TM,),
        in_specs=[_tile(D), _layer(1, D), _tile(F), _tile(F), _tile(D),
                  _layer(F, D), _layer(F, D), _layer(F, D)] + [_ANY] * len(after),
        out_specs=[_tile(D), _tile(F), _tile(F), _tile(D), _tile(D), _const((1, D))],
        out_shape=[SDS((S, D), f32), SDS((S, F), bf16), SDS((S, F), bf16), SDS((S, D), bf16),
                   SDS((S, D), bf16), SDS((1, D), f32)],
        compiler_params=_CP, name="ffn_bwd_d")(x, g, gate, up, dxo, gt, ut, dn, *after)


def _ffn_bwd_w(h, dy, gate, up, dgate, dup):
    fc = 256

    def body(h_ref, dy_ref, gate_ref, up_ref, dgate_ref, dup_ref, dgt_ref, dut_ref, ddn_ref):
        gate = gate_ref[...].astype(f32)
        a = (gate * jax.nn.sigmoid(gate) * up_ref[...].astype(f32)).astype(bf16)
        ddn_ref[...] = _dot_tn(a, dy_ref[...]).astype(bf16)
        h = h_ref[...]
        dgt_ref[...] = _dot_tn(dgate_ref[...], h).astype(bf16)
        dut_ref[...] = _dot_tn(dup_ref[...], h).astype(bf16)

    col = pl.BlockSpec((S, fc), lambda j: (0, j))
    row = pl.BlockSpec((fc, D), lambda j: (j, 0))
    full = pl.BlockSpec((S, D), lambda j: (0, 0))
    return pl.pallas_call(
        body, grid=(F // fc,),
        in_specs=[full, full, col, col, col, col],
        out_specs=[row, row, row],
        out_shape=[SDS((F, D), bf16)] * 3,
        compiler_params=_CP, name="ffn_bwd_w")(h, dy, gate, up, dgate, dup)


def _wgrad(a, b):
    m, n = a.shape[1], b.shape[1]
    mc = 256

    def body(a_ref, b_ref, o_ref):
        o_ref[...] = _dot_tn(a_ref[...], b_ref[...]).astype(bf16)

    return pl.pallas_call(
        body, grid=(m // mc,),
        in_specs=[pl.BlockSpec((S, mc), lambda j: (0, j)), pl.BlockSpec((S, n), lambda j: (0, 0))],
        out_specs=pl.BlockSpec((mc, n), lambda j: (j, 0)),
        out_shape=SDS((m, n), bf16),
        compiler_params=_CP, name="wgrad")(a, b)


def _rope(t, c, sn, sp):
    return t * c + pltpu.roll(t, 120, 1) * sn + pltpu.roll(t, 8, 1) * sp


def _rope_bwd(d, c, sn, sp):
    return d * c + pltpu.roll(d * sn, 8, 1) + pltpu.roll(d * sp, 120, 1)


def _rope_tables(positions):
    inv_freq = ROPE_THETA ** (-jnp.arange(0, 16, 2, dtype=f32) / 16)
    ang = positions.reshape(S, 1).astype(f32) * inv_freq
    cos, sin = jnp.cos(ang), jnp.sin(ang)
    one = jnp.ones((S, 48), f32)
    zero8 = jnp.zeros((S, 8), f32)
    zero48 = jnp.zeros((S, 48), f32)
    c = jnp.concatenate([cos, cos, one], axis=1)
    sn = jnp.concatenate([-sin, zero8, zero48], axis=1)
    sp = jnp.concatenate([zero8, sin, zero48], axis=1)
    return tuple(jnp.concatenate([t, t], axis=1) for t in (c, sn, sp))


def _dilation_perm(n, back=False):
    per = TM // n
    i = lax.broadcasted_iota(jnp.int32, (TM, TM), 1 if back else 0)
    j = lax.broadcasted_iota(jnp.int32, (TM, TM), 0 if back else 1)
    return jnp.where(j == n * (i % per) + i // per, 1.0, 0.0).astype(bf16)


def _mix_in_fwd(x, g, wint, tabs):
    def body(x_ref, g_ref, w_ref, c_ref, sn_ref, sp_ref,
             h_ref, vp_ref, q1, k1, v1, q4, k4, v4, q16, k16, v16):
        _, _, hn = _rms(x_ref[...], g_ref[...])
        h = hn.astype(bf16)
        h_ref[...] = h
        proj = _dot_nt(h, w_ref[...])
        vp_ref[...] = proj[:, :PW]
        c, sn, sp = c_ref[...], sn_ref[...], sp_ref[...]
        perm4, perm16 = _dilation_perm(4), _dilation_perm(16)
        for kind, (o1, o4, o16) in enumerate(((q1, q4, q16), (k1, k4, k16), (v1, v4, v16))):
            for j in range(NG):
                t = proj[:, PW + kind * AW + 128 * j: PW + kind * AW + 128 * (j + 1)]
                if kind == 0:
                    t = _rope(t, c, sn, sp) * 0.125
                elif kind == 1:
                    t = _rope(t, c, sn, sp)
                o1[:, _cols(j)] = t.astype(bf16)
            nat = o1[...]
            o4[...] = _dot_nn(perm4, nat).astype(bf16).reshape(4, TM // 4, AW)
            o16[...] = _dot_nn(perm16, nat).astype(bf16).reshape(16, TM // 16, AW)

    nat, d4, d16 = SDS((S, AW), bf16), SDS((4, S // 4, AW), bf16), SDS((16, S // 16, AW), bf16)
    return pl.pallas_call(
        body, grid=(S // TM,),
        in_specs=[_tile(D), _layer(1, D), _layer(PROJ, D), _tile(128), _tile(128), _tile(128)],
        out_specs=[_tile(D), _tile(PW)] + [_tile(AW)] * 3 + [_p4()] * 3 + [_p16()] * 3,
        out_shape=[SDS((S, D), bf16), SDS((S, PW), f32)] + [nat] * 3 + [d4] * 3 + [d16] * 3,
        compiler_params=_CP, name="mix_in_fwd")(x, g, wint, *tabs)


def _mix_in_bwd(dxo, x, g, wint, tabs, dvp, d1, d4, d16, after=()):
    def body(dxo_ref, x_ref, g_ref, w_ref, c_ref, sn_ref, sp_ref, dvp_ref,
             dq1, dk1, dv1, dq4, dk4, dv4, dq16, dk16, dv16,
             dx_ref, dproj_ref, dg_ref):
        c, sn, sp = c_ref[...], sn_ref[...], sp_ref[...]
        dproj_ref[:, :PW] = dvp_ref[...].astype(bf16)
        back4, back16 = _dilation_perm(4, True), _dilation_perm(16, True)
        for kind, (a1, a4, a16) in enumerate(((dq1, dq4, dq16), (dk1, dk4, dk16), (dv1, dv4, dv16))):
            n4 = _dot_nn(back4, a4[...].reshape(TM, AW))
            n16 = _dot_nn(back16, a16[...].reshape(TM, AW))
            for j in range(NG):
                t = a1[:, _cols(j)].astype(f32) + n4[:, _cols(j)] + n16[:, _cols(j)]
                if kind == 0:
                    t = _rope_bwd(t * 0.125, c, sn, sp)
                elif kind == 1:
                    t = _rope_bwd(t, c, sn, sp)
                dproj_ref[:, PW + kind * AW + 128 * j: PW + kind * AW + 128 * (j + 1)] = t.astype(bf16)
        g = g_ref[...]
        r_, xh, _ = _rms(x_ref[...], g)
        dh = _dot_nn(dproj_ref[...], w_ref[...])

        @pl.when(pl.program_id(0) == 0)
        def _():
            dg_ref[...] = jnp.zeros_like(dg_ref)

        dg_ref[...] += jnp.sum(dh * xh, axis=0, keepdims=True)
        dx_ref[...] = dxo_ref[...] + _rms_bwd(dh, r_, xh, g)

    return pl.pallas_call(
        _follow(body, 17, after), grid=(S // TM,),
        in_specs=[_tile(D), _tile(D), _layer(1, D), _layer(PROJ, D), _tile(128), _tile(128), _tile(128),
                  _tile(PW)] + [_tile(AW)] * 3 + [_p4()] * 3 + [_p16()] * 3 + [_ANY] * len(after),
        out_specs=[_tile(D), _tile(PROJ), _const((1, D))],
        out_shape=[SDS((S, D), f32), SDS((S, PROJ), bf16), SDS((1, D), f32)],
        compiler_params=_CP, name="mix_in_bwd")(dxo, x, g, wint, *tabs, dvp, *d1, *d4, *d16, *after)


def _pool_sums(pad_ref, base, rows, adjoint):
    lane_group = lax.broadcasted_iota(jnp.int32, (rows, PW), 1) // 64
    sign = -1 if adjoint else 1

    def sh(o):
        return pad_ref[pl.ds(PAD + base + sign * o, rows), :]

    out = None
    acc = None
    lo, hi = 0, 0
    for gi, w in enumerate(POOL_WINDOWS):
        for o in list(range(-(w // 2), lo)) + list(range(hi, w - w // 2)):
            acc = sh(o) if acc is None else acc + sh(o)
        lo, hi = -(w // 2), w - w // 2
        out = acc if out is None else jnp.where(lane_group >= gi, acc, out)
    return out


def _pool_counts(base, rows):
    pos = base + lax.broadcasted_iota(jnp.int32, (rows, PW), 0)
    lane_group = lax.broadcasted_iota(jnp.int32, (rows, PW), 1) // 64
    cnt = None
    for gi, w in enumerate(POOL_WINDOWS):
        lo = jnp.maximum(pos - w // 2, 0)
        hi = jnp.minimum(pos + w - 1 - w // 2, S - 1)
        c = (hi - lo + 1).astype(f32)
        cnt = c if cnt is None else jnp.where(lane_group >= gi, c, cnt)
    return cnt


def _pool_fwd(vp, wbd, scale):
    ch = 256

    def body(vp_ref, w_ref, sc_ref, y_ref, diff_ref, pad):
        pad[pl.ds(0, PAD), :] = jnp.zeros((PAD, PW), f32)
        pad[pl.ds(PAD + S, PAD), :] = jnp.zeros((PAD, PW), f32)
        pad[pl.ds(PAD, S), :] = vp_ref[...]
        for b in range(S // ch):
            base = b * ch
            pooled = _pool_sums(pad, base, ch, False) / _pool_counts(base, ch)
            diff = (pooled - vp_ref[pl.ds(base, ch), :]).astype(bf16)
            diff_ref[pl.ds(base, ch), :] = diff
            y_ref[pl.ds(base, ch), :] = _dot_nn(diff, w_ref[...]) * sc_ref[...]

    whole = lambda shape: pl.BlockSpec(shape, lambda i: (0,) * len(shape))
    return pl.pallas_call(
        body, grid=(1,),
        in_specs=[whole((S, PW)), whole((PW, PW)), whole((1, PW))],
        out_specs=[whole((S, PW)), whole((S, PW))],
        out_shape=[SDS((S, PW), f32), SDS((S, PW), bf16)],
        scratch_shapes=[pltpu.VMEM((S + 2 * PAD, PW), f32)],
        compiler_params=_CP, name="pool_fwd")(vp, wbd, scale)


def _pool_bwd(dy, diff, wbd, scale, after=()):
    ch = 256

    def body(dy_ref, diff_ref, w_ref, sc_ref, dvp_ref, dw_ref, dsc_ref, pad):
        pad[pl.ds(0, PAD), :] = jnp.zeros((PAD, PW), f32)
        pad[pl.ds(PAD + S, PAD), :] = jnp.zeros((PAD, PW), f32)
        dw = jnp.zeros((PW, PW), f32)
        dsc = jnp.zeros((1, PW), f32)
        for b in range(S // ch):
            base = b * ch
            dy = dy_ref[pl.ds(base, ch), :]
            diff = diff_ref[pl.ds(base, ch), :]
            dsc = dsc + jnp.sum(dy * _dot_nn(diff, w_ref[...]), axis=0, keepdims=True)
            dz = (dy * sc_ref[...]).astype(bf16)
            dw = dw + _dot_tn(diff, dz)
            ddiff = _dot_nt(dz, w_ref[...])
            dvp_ref[pl.ds(base, ch), :] = -ddiff
            pad[pl.ds(PAD + base, ch), :] = ddiff / _pool_counts(base, ch)
        for gi in range(4):
            dw_ref[gi] = dw[64 * gi:64 * (gi + 1), 64 * gi:64 * (gi + 1)]
        dsc_ref[...] = dsc
        for b in range(S // ch):
            base = b * ch
            dvp_ref[pl.ds(base, ch), :] += _pool_sums(pad, base, ch, True)

    whole = lambda shape: pl.BlockSpec(shape, lambda i: (0,) * len(shape))
    return pl.pallas_call(
        _follow(body, 4, after), grid=(1,),
        in_specs=[whole((S, PW)), whole((S, PW)), whole((PW, PW)), whole((1, PW))] + [_ANY] * len(after),
        out_specs=[whole((S, PW)), whole((4, 64, 64)), whole((1, PW))],
        out_shape=[SDS((S, PW), f32), SDS((4, 64, 64), f32), SDS((1, PW), f32)],
        scratch_shapes=[pltpu.VMEM((S + 2 * PAD, PW), f32)],
        compiler_params=_CP, name="pool_bwd")(dy, diff, wbd, scale, *after)


def _attn_blocks(lc):
    bpc = lc // QB
    kw = min(2 * QB, lc)
    blocks = []
    for b in range(S // QB):
        t0 = (b % bpc) * QB
        ks_in = min(max(t0 - HALF, 0), lc - kw)
        blocks.append((b * QB, (b // bpc) * lc + ks_in, t0 - ks_in))
    return kw, blocks


def _attn_bias(bias_ref, kw, shifts):
    r = lax.broadcasted_iota(jnp.int32, (2 * QB, kw), 0) % QB
    c = lax.broadcasted_iota(jnp.int32, (2 * QB, kw), 1)
    for i, shift in enumerate(shifts):
        bias_ref[i] = jnp.where(jnp.abs(r + shift - c) <= HALF, 0.0, MASK_VALUE).astype(f32)


def _head_put(stats, pair, v0, v1, lane):
    return jnp.where(lane == 2 * pair, v0, jnp.where(lane == 2 * pair + 1, v1, stats))


def _head_cols(stats, pair, lane):
    c0 = jnp.sum(jnp.where(lane == 2 * pair, stats, 0.0), axis=-1, keepdims=True)
    c1 = jnp.sum(jnp.where(lane == 2 * pair + 1, stats, 0.0), axis=-1, keepdims=True)
    return jnp.concatenate([c0, c1], axis=0)


def _head_spread(stats, pair, head0):
    return jnp.where(head0, stats[:, 2 * pair:2 * pair + 1], stats[:, 2 * pair + 1:2 * pair + 2])


def _stack_heads(blk, head0):
    zero = jnp.zeros_like(blk)
    return jnp.concatenate([jnp.where(head0, blk, zero), jnp.where(head0, zero, blk)], axis=0)


def _attn_fwd(q, k, v, lc, after=None):
    kw, blocks = _attn_blocks(lc)
    shifts = sorted({b[2] for b in blocks})

    def body(q_ref, k_ref, v_ref, *refs):
        o_ref, lse_ref, bias_ref = refs[-3:]
        lane = lax.broadcasted_iota(jnp.int32, (QB, 128), 1)
        head0 = lane < 64
        pair = pl.program_id(0)
        _attn_bias(bias_ref, kw, shifts)

        @pl.when(pair == 0)
        def _():
            lse_ref[...] = jnp.zeros_like(lse_ref)

        for row0, kstart, shift in blocks:
            q2 = _stack_heads(q_ref[pl.ds(row0, QB), :], head0)
            kb = k_ref[pl.ds(kstart, kw), :]
            vb = v_ref[pl.ds(kstart, kw), :]
            s = _dot_nt(q2, kb) + bias_ref[shifts.index(shift)]
            m = jnp.max(s, axis=-1, keepdims=True)
            p = jnp.exp(s - m)
            den = jnp.sum(p, axis=-1, keepdims=True)
            o2 = _dot_nn(p.astype(bf16), vb) / den
            lse2 = m + jnp.log(den)
            o_ref[pl.ds(row0, QB), :] = jnp.where(head0, o2[:QB], o2[QB:]).astype(bf16)
            lse_ref[pl.ds(row0, QB), :] = _head_put(lse_ref[pl.ds(row0, QB), :], pair, lse2[:QB], lse2[QB:], lane)

    col = pl.BlockSpec((S, 128), lambda p: (0, p))
    extra = () if after is None else (after,)
    return pl.pallas_call(
        body, grid=(NG,), in_specs=[col, col, col] + [_ANY] * len(extra),
        out_specs=[col, pl.BlockSpec((S, 128), lambda p: (0, 0))],
        out_shape=[SDS((S, AW), bf16), SDS((S, 128), f32)],
        scratch_shapes=[pltpu.VMEM((len(shifts), 2 * QB, kw), f32)],
        compiler_params=_CP, name=f"attn_fwd_{lc}")(q, k, v, *extra)


def _attn_bwd(q, k, v, do, lse, delta, lc):
    kw, blocks = _attn_blocks(lc)
    shifts = sorted({b[2] for b in blocks})

    def body(q_ref, k_ref, v_ref, do_ref, lse_ref, dl_ref, dq_ref, dk_out, dv_out, bias_ref, dk_ref, dv_ref):
        lane = lax.broadcasted_iota(jnp.int32, (QB, 128), 1)
        head0 = lane < 64
        pair = pl.program_id(0)
        _attn_bias(bias_ref, kw, shifts)
        dk_ref[...] = jnp.zeros_like(dk_ref)
        dv_ref[...] = jnp.zeros_like(dv_ref)
        for row0, kstart, shift in blocks:
            q2 = _stack_heads(q_ref[pl.ds(row0, QB), :], head0)
            do2 = _stack_heads(do_ref[pl.ds(row0, QB), :], head0)
            lse2 = _head_cols(lse_ref[pl.ds(row0, QB), :], pair, lane)
            dl2 = _head_cols(dl_ref[pl.ds(row0, QB), :], pair, lane)
            kb = k_ref[pl.ds(kstart, kw), :]
            vb = v_ref[pl.ds(kstart, kw), :]
            p = jnp.exp(_dot_nt(q2, kb) + bias_ref[shifts.index(shift)] - lse2)
            ds = (p * (_dot_nt(do2, vb) - dl2)).astype(bf16)
            dq2 = _dot_nn(ds, kb)
            dq_ref[pl.ds(row0, QB), :] = jnp.where(head0, dq2[:QB], dq2[QB:]).astype(bf16)
            dk_ref[pl.ds(kstart, kw), :] += _dot_tn(ds, q2)
            dv_ref[pl.ds(kstart, kw), :] += _dot_tn(p.astype(bf16), do2)
        dk_out[...] = dk_ref[...].astype(bf16)
        dv_out[...] = dv_ref[...].astype(bf16)

    col = pl.BlockSpec((S, 128), lambda p: (0, p))
    stats = pl.BlockSpec((S, 128), lambda p: (0, 0))
    return pl.pallas_call(
        body, grid=(NG,), in_specs=[col] * 4 + [stats] * 2, out_specs=[col] * 3,
        out_shape=[SDS((S, AW), bf16)] * 3,
        scratch_shapes=[pltpu.VMEM((len(shifts), 2 * QB, kw), f32), pltpu.VMEM((S, 128), f32),
                        pltpu.VMEM((S, 128), f32)],
        compiler_params=_CP, name=f"attn_bwd_{lc}")(q, k, v, do, lse, delta)


def _mix_out_fwd(x, ypool, o1, l1, o4, l4, o16, l16, wout):
    def body(x_ref, yp_ref, o1_ref, l1_ref, o4_ref, l4_ref, o16_ref, l16_ref, w_ref,
             xo_ref, mixed_ref, o_ref, lse1_ref, lse4_ref, lse16_ref, sl4, sl16, sl):
        head0 = lax.broadcasted_iota(jnp.int32, (TM, 128), 1) < 64
        for r in range(4):
            sl4[pl.ds(r, TM // 4, stride=4), :] = l4_ref[r]
        for r in range(16):
            sl16[pl.ds(r, TM // 16, stride=16), :] = l16_ref[r]
        n4 = _dot_nn(_dilation_perm(4, True), o4_ref[...].reshape(TM, AW))
        n16 = _dot_nn(_dilation_perm(16, True), o16_ref[...].reshape(TM, AW))
        a, b, c = l1_ref[...], sl4[...], sl16[...]
        m = jnp.maximum(jnp.maximum(a, b), c)
        wa, wb, wc = jnp.exp(a - m), jnp.exp(b - m), jnp.exp(c - m)
        den = wa + wb + wc
        wa, wb, wc = wa / den, wb / den, wc / den
        lse = m + jnp.log(den)
        lse1_ref[...] = lse
        sl[...] = lse
        mixed_ref[:, :PW] = yp_ref[...].astype(bf16)
        for j in range(NG):
            y = (_head_spread(wa, j, head0) * o1_ref[:, _cols(j)].astype(f32)
                 + _head_spread(wb, j, head0) * n4[:, _cols(j)] + _head_spread(wc, j, head0) * n16[:, _cols(j)])
            o_ref[:, _cols(j)] = y
            mixed_ref[:, PW + 128 * j: PW + 128 * (j + 1)] = y.astype(bf16)
        for r in range(4):
            lse4_ref[r] = sl[pl.ds(r, TM // 4, stride=4), :]
        for r in range(16):
            lse16_ref[r] = sl[pl.ds(r, TM // 16, stride=16), :]
        xo_ref[...] = x_ref[...] + _dot_nn(mixed_ref[...], w_ref[...])

    return pl.pallas_call(
        body, grid=(S // TM,),
        in_specs=[_tile(D), _tile(PW), _tile(AW), _tile(128), _p4(), _p4(128), _p16(), _p16(128), _layer(D, D)],
        out_specs=[_tile(D), _tile(D), _tile(AW), _tile(128), _p4(128), _p16(128)],
        out_shape=[SDS((S, D), f32), SDS((S, D), bf16), SDS((S, AW), f32), SDS((S, 128), f32),
                   SDS((4, S // 4, 128), f32), SDS((16, S // 16, 128), f32)],
        scratch_shapes=[pltpu.VMEM((TM, 128), f32)] * 3,
        compiler_params=_CP, name="mix_out_fwd")(x, ypool, o1, l1, o4, l4, o16, l16, wout)


def _mix_out_bwd(dxo, o, wout):
    def body(dxo_ref, o_ref, w_ref, dxb_ref, dyp_ref, do1, do4, do16, dl1, dl4, dl16, sdl):
        dxb = dxo_ref[...].astype(bf16)
        dxb_ref[...] = dxb
        dm = _dot_nt(dxb, w_ref[...])
        dyp_ref[...] = dm[:, :PW]
        lane = lax.broadcasted_iota(jnp.int32, (TM, 128), 1)
        head0 = lane < 64
        dl = jnp.zeros((TM, 128), f32)
        for j in range(NG):
            d = dm[:, PW + 128 * j: PW + 128 * (j + 1)]
            prod = d * o_ref[:, _cols(j)]
            dl = _head_put(dl, j, jnp.sum(jnp.where(head0, prod, 0.0), axis=-1, keepdims=True),
                           jnp.sum(jnp.where(head0, 0.0, prod), axis=-1, keepdims=True), lane)
            do1[:, _cols(j)] = d.astype(bf16)
        dl1[...] = dl
        sdl[...] = dl
        for r in range(4):
            dl4[r] = sdl[pl.ds(r, TM // 4, stride=4), :]
        for r in range(16):
            dl16[r] = sdl[pl.ds(r, TM // 16, stride=16), :]
        nat = do1[...]
        do4[...] = _dot_nn(_dilation_perm(4), nat).astype(bf16).reshape(4, TM // 4, AW)
        do16[...] = _dot_nn(_dilation_perm(16), nat).astype(bf16).reshape(16, TM // 16, AW)

    return pl.pallas_call(
        body, grid=(S // TM,),
        in_specs=[_tile(D), _tile(AW), _layer(D, D)],
        out_specs=[_tile(D), _tile(PW), _tile(AW), _p4(), _p16(), _tile(128), _p4(128), _p16(128)],
        out_shape=[SDS((S, D), bf16), SDS((S, PW), f32),
                   SDS((S, AW), bf16), SDS((4, S // 4, AW), bf16), SDS((16, S // 16, AW), bf16),
                   SDS((S, 128), f32), SDS((4, S // 4, 128), f32), SDS((16, S // 16, 128), f32)],
        scratch_shapes=[pltpu.VMEM((TM, 128), f32)],
        compiler_params=_CP, name="mix_out_bwd")(dxo, o, wout)


def _loss_head(x, g, target):
    def body(x_ref, g_ref, t_ref, dx_ref, loss_ref, dg_ref):
        g = g_ref[...]
        r, xh, y = _rms(x_ref[...], g)
        err = y - t_ref[...]
        dy = err * (1.0 / D)

        @pl.when(pl.program_id(0) == 0)
        def _():
            loss_ref[...] = jnp.zeros_like(loss_ref)
            dg_ref[...] = jnp.zeros_like(dg_ref)

        loss_ref[...] += jnp.broadcast_to(0.5 * jnp.sum(jnp.mean(err * err, axis=-1, keepdims=True)), (1, D))
        dg_ref[...] += jnp.sum(dy * xh, axis=0, keepdims=True)
        dx_ref[...] = _rms_bwd(dy, r, xh, g)

    return pl.pallas_call(
        body, grid=(S // TM,),
        in_specs=[_tile(D), _const((1, D)), _tile(D)],
        out_specs=[_tile(D), _const((1, D)), _const((1, D))],
        out_shape=[SDS((S, D), f32), SDS((1, D), f32), SDS((1, D), f32)],
        compiler_params=_CP, name="loss_head")(x, g, target)


def _peer(k):
    x, y, c = lax.axis_index("x"), lax.axis_index("y"), lax.axis_index("c")
    px = 1 - x if k & 4 else x
    py = 1 - y if k & 2 else y
    pc = 1 - c if k & 1 else c
    return (px, py, pc), 4 * px + 2 * py + pc


def _diag_route():
    x, y, c = lax.axis_index("x"), lax.axis_index("y"), lax.axis_index("c")
    idx_x, idx_y = _peer(4)[1], _peer(2)[1]
    return idx_x + c * (idx_y - idx_x), (x + c * (1 - 2 * x), (1 - y) + c * (2 * y - 1), c)


def _all_gather(lands):
    n = len(lands)

    def body(*refs):
        zones, send_sems, recv_sems = refs[n:2 * n], refs[2 * n], refs[2 * n + 1]
        me, me_idx = _peer(0)
        sibling, sib_idx = _peer(1)
        (x_nbr, idx_x), (y_nbr, idx_y), idx_d = _peer(4), _peer(2), _peer(6)[1]
        fwd_idx, fwd_dev = _diag_route()

        def copy(k, t, idx, to):
            return _row_copy(zones[t], idx, send_sems.at[k, t], recv_sems.at[k, t], to)

        sent = []

        def send(k, t, idx, to):
            cp = copy(k, t, idx, to)
            cp.start()
            sent.append(cp)

        for t in range(n):
            send(0, t, me_idx, sibling)
            send(1, t, me_idx, x_nbr)
            send(2, t, me_idx, y_nbr)
        for t in range(n):
            copy(1, t, idx_x, me).wait_recv()
            send(3, t, idx_x, sibling)
        for t in range(n):
            copy(2, t, idx_y, me).wait_recv()
            send(4, t, idx_y, sibling)
        for t in range(n):
            send(5, t, fwd_idx, fwd_dev)
        for t in range(n):
            copy(5, t, idx_d, me).wait_recv()
            send(6, t, idx_d, sibling)
        for k, mask in ((0, 1), (3, 5), (4, 3), (6, 7)):
            for t in range(n):
                copy(k, t, _peer(mask)[1], me).wait_recv()
        for cp in sent:
            cp.wait_send()

    return pl.pallas_call(
        body, in_specs=[_ANY] * n, out_specs=[_ANY] * n,
        out_shape=[SDS(a.shape, a.dtype) for a in lands], input_output_aliases={t: t for t in range(n)},
        scratch_shapes=[pltpu.SemaphoreType.DMA((7, n)), pltpu.SemaphoreType.DMA((7, n))],
        name="all_gather_weights")(*lands)


def _hbm(a):
    return pltpu.with_memory_space_constraint(a, pltpu.HBM)


def _rows(ref, idx):
    r = ref.shape[0] // NDEV
    return ref.at[pl.ds(idx * r, r), :]


def _row_copy(ref, idx, send_sem, recv_sem, to):
    return pltpu.make_async_remote_copy(src_ref=_rows(ref, idx), dst_ref=_rows(ref, idx), send_sem=send_sem,
                                        recv_sem=recv_sem, device_id=to, device_id_type=_MESH)


def _place_own(me, shards, l):
    n = len(shards)

    def body(me_ref, *refs):
        for t in range(n):
            refs[n + t][...] = refs[t][...]

    grid_spec = pltpu.PrefetchScalarGridSpec(
        num_scalar_prefetch=1, grid=(1,),
        in_specs=[pl.BlockSpec((None, s.shape[1], D), lambda i, me_ref: (l, 0, 0)) for s in shards],
        out_specs=[pl.BlockSpec((s.shape[1], D), lambda i, me_ref: (me_ref[0], 0)) for s in shards])
    return pl.pallas_call(
        body, grid_spec=grid_spec, out_shape=[SDS((NDEV * s.shape[1], D), s.dtype) for s in shards],
        compiler_params=_CP, name="place_own")(me, *shards)


_TOKEN = SDS((8, 128), f32)
def _ag_start(lands, after, l):
    n = len(lands)
    after = list(after) if isinstance(after, (list, tuple)) else [after]

    def body(*refs):
        zones, send_sems, recv_sems, token = refs[:n], refs[n + len(after)], refs[n + len(after) + 1], refs[-1]
        _, me_idx = _peer(0)
        for k, mask in enumerate((1, 4, 2)):
            for t in range(n):
                _row_copy(zones[t], me_idx, send_sems.at[k * n + t], recv_sems.at[k * n + t], _peer(mask)[0]).start()
        token[...] = jnp.zeros_like(token)

    outs = pl.pallas_call(
        body, name=f"ag_start_{l}", in_specs=[_HBM] * n + [_ANY] * len(after),
        out_specs=(_SEM, _SEM, *[_HBM] * n, pl.BlockSpec(memory_space=pltpu.VMEM)),
        out_shape=(pltpu.SemaphoreType.DMA((3 * n,)), pltpu.SemaphoreType.DMA((3 * n,)),
                   *[pltpu.HBM(a.shape, a.dtype) for a in lands], _TOKEN),
        input_output_aliases={t: 2 + t for t in range(n)}, compiler_params=_CP_SPLIT)(
            *[_hbm(a) for a in lands], *after)
    return outs[0], outs[1], list(outs[2:2 + n]), outs[-1]


def _ag_pass(lands, recv_sems, after, l):
    n = len(lands)
    after = list(after) if isinstance(after, (list, tuple)) else [after]

    def body(*refs):
        zones, recv_sems = refs[:n], refs[n]
        psend, precv, token = refs[n + 1 + len(after)], refs[n + 2 + len(after)], refs[-1]
        me, _ = _peer(0)
        sibling, _ = _peer(1)
        for j, mask in enumerate((4, 2)):
            idx = _peer(mask)[1]
            for t in range(n):
                _row_copy(zones[t], idx, psend.at[j * n + t], recv_sems.at[(1 + j) * n + t], me).wait_recv()
                _row_copy(zones[t], idx, psend.at[j * n + t], precv.at[j * n + t], sibling).start()
        fwd_idx, fwd_dev = _diag_route()
        for t in range(n):
            _row_copy(zones[t], fwd_idx, psend.at[2 * n + t], precv.at[2 * n + t], fwd_dev).start()
        token[...] = jnp.zeros_like(token)

    outs = pl.pallas_call(
        body, name=f"ag_pass_{l}", in_specs=[_HBM] * n + [_SEM] + [_ANY] * len(after),
        out_specs=(_SEM, _SEM, *[_HBM] * n, pl.BlockSpec(memory_space=pltpu.VMEM)),
        out_shape=(pltpu.SemaphoreType.DMA((3 * n,)), pltpu.SemaphoreType.DMA((3 * n,)),
                   *[pltpu.HBM(a.shape, a.dtype) for a in lands], _TOKEN),
        input_output_aliases={t: 2 + t for t in range(n)}, compiler_params=_CP_SPLIT)(*lands, recv_sems, *after)
    return outs[0], outs[1], list(outs[2:2 + n]), outs[-1]


def _ag_last(lands, precv, after, l):
    n = len(lands)
    after = list(after) if isinstance(after, (list, tuple)) else [after]

    def body(*refs):
        zones, precv = refs[:n], refs[n]
        qsend, qrecv, token = refs[n + 1 + len(after)], refs[n + 2 + len(after)], refs[-1]
        me, _ = _peer(0)
        sibling, _ = _peer(1)
        idx = _peer(6)[1]
        for t in range(n):
            _row_copy(zones[t], idx, qsend.at[t], precv.at[2 * n + t], me).wait_recv()
            _row_copy(zones[t], idx, qsend.at[t], qrecv.at[t], sibling).start()
        token[...] = jnp.zeros_like(token)

    outs = pl.pallas_call(
        body, name=f"ag_last_{l}", in_specs=[_HBM] * n + [_SEM] + [_ANY] * len(after),
        out_specs=(_SEM, _SEM, *[_HBM] * n, pl.BlockSpec(memory_space=pltpu.VMEM)),
        out_shape=(pltpu.SemaphoreType.DMA((n,)), pltpu.SemaphoreType.DMA((n,)),
                   *[pltpu.HBM(a.shape, a.dtype) for a in lands], _TOKEN),
        input_output_aliases={t: 2 + t for t in range(n)}, compiler_params=_CP_SPLIT)(*lands, precv, *after)
    return outs[0], outs[1], list(outs[2:2 + n]), outs[-1]


def _ag_wait(lands, send_sems, recv_sems, psend, precv, qsend, qrecv, after, l):
    n = len(lands)
    after = list(after) if isinstance(after, (list, tuple)) else [after]

    def body(*refs):
        zones = refs[:n]
        send_sems, recv_sems, psend, precv, qsend, qrecv = refs[n:n + 6]
        me, me_idx = _peer(0)
        for k in range(3):
            for t in range(n):
                _row_copy(zones[t], me_idx, send_sems.at[k * n + t], recv_sems.at[k * n + t], me).wait_send()
        for t in range(n):
            _row_copy(zones[t], _peer(1)[1], send_sems.at[t], recv_sems.at[t], me).wait_recv()
        fwd_idx, _ = _diag_route()
        for j, (mine, theirs) in enumerate(((_peer(4)[1], _peer(5)[1]), (_peer(2)[1], _peer(3)[1]))):
            for t in range(n):
                _row_copy(zones[t], mine, psend.at[j * n + t], precv.at[j * n + t], me).wait_send()
                _row_copy(zones[t], theirs, psend.at[j * n + t], precv.at[j * n + t], me).wait_recv()
        for t in range(n):
            _row_copy(zones[t], fwd_idx, psend.at[2 * n + t], precv.at[2 * n + t], me).wait_send()
            _row_copy(zones[t], _peer(6)[1], qsend.at[t], qrecv.at[t], me).wait_send()
            _row_copy(zones[t], _peer(7)[1], qsend.at[t], qrecv.at[t], me).wait_recv()

    outs = pl.pallas_call(
        body, name=f"ag_wait_{l}", in_specs=[_HBM] * n + [_SEM] * 6 + [_ANY] * len(after),
        out_specs=tuple([_HBM] * n), out_shape=tuple(pltpu.HBM(a.shape, a.dtype) for a in lands),
        input_output_aliases={t: t for t in range(n)}, compiler_params=_CP_SPLIT)(
            *lands, send_sems, recv_sems, psend, precv, qsend, qrecv, *after)
    return list(outs)


def _xchg_src(ref, slot_ref, idx):
    return _rows(ref, idx) if ref.shape[0] == NDEV * slot_ref.shape[1] else ref


def _rs_start(srcs, slots, after, tag):
    n = len(srcs)
    after = list(after) if isinstance(after, (list, tuple)) else [after]

    def body(*refs):
        src, slot = refs[:n], refs[n:2 * n]
        send_sems, recv_sems, token = refs[2 * n + len(after)], refs[2 * n + len(after) + 1], refs[-1]
        _, me_idx = _peer(0)
        for k in range(1, NDEV):
            dev, idx = _peer(k)
            for t in range(n):
                pltpu.make_async_remote_copy(
                    src_ref=_xchg_src(src[t], slot[t], idx), dst_ref=slot[t].at[me_idx],
                    send_sem=send_sems.at[(k - 1) * n + t], recv_sem=recv_sems.at[(k - 1) * n + t],
                    device_id=dev, device_id_type=_MESH).start()
        token[...] = jnp.zeros_like(token)

    outs = pl.pallas_call(
        body, name=f"rs_start_{tag}", in_specs=[_HBM] * (2 * n) + [_ANY] * len(after),
        out_specs=(_SEM, _SEM, *[_HBM] * (2 * n), pl.BlockSpec(memory_space=pltpu.VMEM)),
        out_shape=(pltpu.SemaphoreType.DMA(((NDEV - 1) * n,)), pltpu.SemaphoreType.DMA(((NDEV - 1) * n,)),
                   *[pltpu.HBM(a.shape, a.dtype) for a in list(srcs) + list(slots)], _TOKEN),
        input_output_aliases={t: 2 + t for t in range(2 * n)}, compiler_params=_CP_SPLIT)(
            *[_hbm(a) for a in list(srcs) + list(slots)], *after)
    return outs[0], outs[1], list(outs[2:2 + n]), list(outs[2 + n:2 + 2 * n]), outs[-1]


def _rs_wait(srcs, slots, send_sems, recv_sems, after, tag):
    n = len(srcs)
    after = list(after) if isinstance(after, (list, tuple)) else [after]

    def body(*refs):
        src, slot, send_sems, recv_sems = refs[:n], refs[n:2 * n], refs[2 * n], refs[2 * n + 1]
        me, _ = _peer(0)
        for k in range(1, NDEV):
            idx = _peer(k)[1]
            for t in range(n):
                cp = pltpu.make_async_remote_copy(
                    src_ref=_xchg_src(src[t], slot[t], idx), dst_ref=slot[t].at[idx],
                    send_sem=send_sems.at[(k - 1) * n + t], recv_sem=recv_sems.at[(k - 1) * n + t],
                    device_id=me, device_id_type=_MESH)
                cp.wait_send()
                cp.wait_recv()

    outs = pl.pallas_call(
        body, name=f"rs_wait_{tag}", in_specs=[_HBM] * (2 * n) + [_SEM, _SEM] + [_ANY] * len(after),
        out_specs=tuple([_HBM] * (2 * n)),
        out_shape=tuple(pltpu.HBM(a.shape, a.dtype) for a in list(srcs) + list(slots)),
        input_output_aliases={t: t for t in range(2 * n)}, compiler_params=_CP_SPLIT)(
            *srcs, *slots, send_sems, recv_sems, *after)
    return list(outs[:n]), list(outs[n:])


def _pair_start(full4s, bufs, after, tag):
    n = len(full4s)
    after = list(after) if isinstance(after, (list, tuple)) else [after]

    def body(*refs):
        full, buf = refs[:n], refs[n:2 * n]
        send_sems, recv_sems, token = refs[2 * n + len(after)], refs[2 * n + len(after) + 1], refs[-1]
        c = lax.axis_index("c")
        for t in range(n):
            pltpu.make_async_remote_copy(src_ref=full[t].at[:, 1 - c], dst_ref=buf[t], send_sem=send_sems.at[t],
                                         recv_sem=recv_sems.at[t], device_id=_peer(1)[0], device_id_type=_MESH).start()
        token[...] = jnp.zeros_like(token)

    outs = pl.pallas_call(
        body, name=f"pair_start_{tag}", in_specs=[_HBM] * (2 * n) + [_ANY] * len(after),
        out_specs=(_SEM, _SEM, *[_HBM] * (2 * n), pl.BlockSpec(memory_space=pltpu.VMEM)),
        out_shape=(pltpu.SemaphoreType.DMA((n,)), pltpu.SemaphoreType.DMA((n,)),
                   *[pltpu.HBM(a.shape, a.dtype) for a in list(full4s) + list(bufs)], _TOKEN),
        input_output_aliases={t: 2 + t for t in range(2 * n)}, compiler_params=_CP_SPLIT)(
            *[_hbm(a) for a in list(full4s) + list(bufs)], *after)
    return outs[0], outs[1], list(outs[2:2 + n]), list(outs[2 + n:2 + 2 * n]), outs[-1]


def _pair_wait(full4s, bufs, send_sems, recv_sems, after, tag):
    n = len(full4s)
    after = list(after) if isinstance(after, (list, tuple)) else [after]

    def body(*refs):
        full, buf, send_sems, recv_sems = refs[:n], refs[n:2 * n], refs[2 * n], refs[2 * n + 1]
        c = lax.axis_index("c")
        for t in range(n):
            cp = pltpu.make_async_remote_copy(src_ref=full[t].at[:, 1 - c], dst_ref=buf[t], send_sem=send_sems.at[t],
                                              recv_sem=recv_sems.at[t], device_id=_peer(0)[0], device_id_type=_MESH)
            cp.wait_send()
            cp.wait_recv()

    outs = pl.pallas_call(
        body, name=f"pair_wait_{tag}", in_specs=[_HBM] * (2 * n) + [_SEM, _SEM] + [_ANY] * len(after),
        out_specs=tuple([_HBM] * (2 * n)),
        out_shape=tuple(pltpu.HBM(a.shape, a.dtype) for a in list(full4s) + list(bufs)),
        input_output_aliases={t: t for t in range(2 * n)}, compiler_params=_CP_SPLIT)(
            *full4s, *bufs, send_sems, recv_sems, *after)
    return list(outs[:n]), list(outs[n:])


def _pair_sum(core, full4s, bufs):
    n = len(full4s)

    def body(core_ref, *refs):
        for t in range(n):
            refs[2 * n + t][...] = (refs[t][...].astype(f32) + refs[n + t][...].astype(f32)).astype(bf16)

    grid_spec = pltpu.PrefetchScalarGridSpec(
        num_scalar_prefetch=1, grid=(4,),
        in_specs=[pl.BlockSpec((None, None) + a.shape[2:], lambda j, core_ref: (j, core_ref[0], 0, 0)) for a in full4s]
        + [pl.BlockSpec((None,) + b.shape[1:], lambda j, core_ref: (j, 0, 0)) for b in bufs],
        out_specs=[pl.BlockSpec((None,) + b.shape[1:], lambda j, core_ref: (j, 0, 0)) for b in bufs])
    return pl.pallas_call(
        body, grid_spec=grid_spec, out_shape=[SDS(b.shape, bf16) for b in bufs],
        compiler_params=_CP, name="pair_sum")(core, *full4s, *bufs)


def _chip_start(sums, slots, after, tag):
    n = len(sums)
    after = list(after) if isinstance(after, (list, tuple)) else [after]

    def body(*refs):
        src, slot = refs[:n], refs[n:2 * n]
        send_sems, recv_sems, token = refs[2 * n + len(after)], refs[2 * n + len(after) + 1], refs[-1]
        my_chip = 2 * lax.axis_index("x") + lax.axis_index("y")
        for k, mask in enumerate((4, 2, 6)):
            dev, _ = _peer(mask)
            for t in range(n):
                pltpu.make_async_remote_copy(
                    src_ref=src[t].at[2 * dev[0] + dev[1]], dst_ref=slot[t].at[my_chip],
                    send_sem=send_sems.at[k * n + t], recv_sem=recv_sems.at[k * n + t],
                    device_id=dev, device_id_type=_MESH).start()
        token[...] = jnp.zeros_like(token)

    outs = pl.pallas_call(
        body, name=f"chip_start_{tag}", in_specs=[_HBM] * (2 * n) + [_ANY] * len(after),
        out_specs=(_SEM, _SEM, *[_HBM] * (2 * n), pl.BlockSpec(memory_space=pltpu.VMEM)),
        out_shape=(pltpu.SemaphoreType.DMA((3 * n,)), pltpu.SemaphoreType.DMA((3 * n,)),
                   *[pltpu.HBM(a.shape, a.dtype) for a in list(sums) + list(slots)], _TOKEN),
        input_output_aliases={t: 2 + t for t in range(2 * n)}, compiler_params=_CP_SPLIT)(
            *[_hbm(a) for a in list(sums) + list(slots)], *after)
    return outs[0], outs[1], list(outs[2:2 + n]), list(outs[2 + n:2 + 2 * n]), outs[-1]


def _chip_wait(sums, slots, send_sems, recv_sems, after, tag):
    n = len(sums)
    after = list(after) if isinstance(after, (list, tuple)) else [after]

    def body(*refs):
        src, slot, send_sems, recv_sems = refs[:n], refs[n:2 * n], refs[2 * n], refs[2 * n + 1]
        for k, mask in enumerate((4, 2, 6)):
            dev, _ = _peer(mask)
            chip = 2 * dev[0] + dev[1]
            for t in range(n):
                cp = pltpu.make_async_remote_copy(
                    src_ref=src[t].at[chip], dst_ref=slot[t].at[chip],
                    send_sem=send_sems.at[k * n + t], recv_sem=recv_sems.at[k * n + t],
                    device_id=_peer(0)[0], device_id_type=_MESH)
                cp.wait_send()
                cp.wait_recv()

    outs = pl.pallas_call(
        body, name=f"chip_wait_{tag}", in_specs=[_HBM] * (2 * n) + [_SEM, _SEM] + [_ANY] * len(after),
        out_specs=tuple([_HBM] * (2 * n)),
        out_shape=tuple(pltpu.HBM(a.shape, a.dtype) for a in list(sums) + list(slots)),
        input_output_aliases={t: t for t in range(2 * n)}, compiler_params=_CP_SPLIT)(
            *sums, *slots, send_sems, recv_sems, *after)
    return list(outs[:n]), list(outs[n:])


def _sum_slots(slots, rb):
    r = slots.shape[1]

    def body(s_ref, o_ref):
        acc = s_ref[0].astype(f32)
        for s in range(1, NDEV):
            acc = acc + s_ref[s].astype(f32)
        o_ref[...] = acc

    return pl.pallas_call(
        body, grid=(r // rb,),
        in_specs=[pl.BlockSpec((NDEV, rb, D), lambda i: (0, i, 0))],
        out_specs=pl.BlockSpec((rb, D), lambda i: (i, 0)),
        out_shape=SDS((r, D), f32), compiler_params=_CP, name="sum_slots")(slots)


def _adamw(w, g, m, v):
    shape = w.shape
    cols = shape[-1]
    rows = w.size // cols
    rb = rows
    for cand in (512, 256, 128, 64, 32, 16, 8):
        if rows % cand == 0 and rows > cand:
            rb = cand
            break

    def body(w_ref, g_ref, m_ref, v_ref, d_ref, mo_ref, vo_ref):
        d_ref[...], mo_ref[...], vo_ref[...] = _adamw_math(w_ref[...], g_ref[...], m_ref[...], v_ref[...])

    spec = pl.BlockSpec((rb, cols), lambda i: (i, 0))
    outs = pl.pallas_call(
        body, grid=(rows // rb,), in_specs=[spec] * 4, out_specs=[spec] * 3,
        out_shape=[SDS((rows, cols), f32)] * 3, compiler_params=_CP, name="adamw")(
            *(a.reshape(rows, cols) for a in (w, g, m, v)))
    return tuple(o.reshape(shape) for o in outs)


def _adamw_math(w, g, m, v):
    m = ADAM_B1 * m + (1.0 - ADAM_B1) * g
    v = ADAM_B2 * v + (1.0 - ADAM_B2) * (g * g)
    m_hat = m / (1.0 - ADAM_B1 ** ADAM_STEP)
    v_hat = v / (1.0 - ADAM_B2 ** ADAM_STEP)
    return -ADAM_LR * (m_hat / (jnp.sqrt(v_hat) + ADAM_EPS) + ADAM_WD * w), m, v


def _reduce_adamw(acc, me, full, slots, w, m, v, l):
    _, r, _ = w.shape
    ns = slots.shape[0]
    rb = r // 2 if r > 128 else r

    def body(me_ref, full_ref, slots_ref, w_ref, m_ref, v_ref, *refs):
        go_ref, d_ref, mo_ref, vo_ref = refs[-4:]
        own = full_ref[...].astype(f32)
        g = None
        for s in range(ns):
            part = jnp.where(me_ref[0] == s, own, slots_ref[s].astype(f32))
            g = part if g is None else g + part
        go_ref[...] = g
        d_ref[...], mo_ref[...], vo_ref[...] = _adamw_math(w_ref[...], g, m_ref[...], v_ref[...])

    steps = r // rb
    lay = pl.BlockSpec((None, rb, D), lambda i, me_ref: (l, i, 0))
    n_acc = 0 if acc is None else 4
    grid_spec = pltpu.PrefetchScalarGridSpec(
        num_scalar_prefetch=1, grid=(steps,),
        in_specs=[pl.BlockSpec((rb, D), lambda i, me_ref: (me_ref[0] * steps + i, 0)),
                  pl.BlockSpec((ns, rb, D), lambda i, me_ref: (0, i, 0)), lay, lay, lay] + [_ANY] * n_acc,
        out_specs=[lay] * 4)
    outs = pl.pallas_call(
        body, grid_spec=grid_spec, out_shape=[SDS(w.shape, f32)] * 4,
        input_output_aliases={6 + j: j for j in range(n_acc)},
        compiler_params=_CP, name="reduce_adamw")(me, full, slots, w, m, v, *(() if acc is None else acc))
    return tuple(outs)


_BIG = ("ffn1_w_gate", "ffn1_w_up", "ffn1_w_down", "w_in", "w_out", "ffn2_w_gate", "ffn2_w_up", "ffn2_w_down")
_TRANSPOSED = ("ffn1_w_gate", "ffn1_w_up", "w_in", "ffn2_w_gate", "ffn2_w_up")

def _block_diag(pool_w):
    out = jnp.zeros((L, PW, PW), pool_w.dtype)
    for gi in range(4):
        out = out.at[:, 64 * gi:64 * (gi + 1), 64 * gi:64 * (gi + 1)].set(pool_w[:, gi])
    return out


def kernel(x, positions, ffn1_norm, ffn1_w_gate, ffn1_w_up, ffn1_w_down, mix_norm, w_in, pool_w, pool_scale, w_out, ffn2_norm, ffn2_w_gate, ffn2_w_up, ffn2_w_down, final_norm, loss_target, m_ffn1_norm, m_ffn1_w_gate, m_ffn1_w_up, m_ffn1_w_down, m_mix_norm, m_w_in, m_pool_w, m_pool_scale, m_w_out, m_ffn2_norm, m_ffn2_w_gate, m_ffn2_w_up, m_ffn2_w_down, m_final_norm, v_ffn1_norm, v_ffn1_w_gate, v_ffn1_w_up, v_ffn1_w_down, v_mix_norm, v_w_in, v_pool_w, v_pool_scale, v_w_out, v_ffn2_norm, v_ffn2_w_gate, v_ffn2_w_up, v_ffn2_w_down, v_final_norm):
    weights = dict(ffn1_norm=ffn1_norm, ffn1_w_gate=ffn1_w_gate, ffn1_w_up=ffn1_w_up, ffn1_w_down=ffn1_w_down,
                   mix_norm=mix_norm, w_in=w_in, pool_w=pool_w, pool_scale=pool_scale, w_out=w_out,
                   ffn2_norm=ffn2_norm, ffn2_w_gate=ffn2_w_gate, ffn2_w_up=ffn2_w_up, ffn2_w_down=ffn2_w_down,
                   final_norm=final_norm)
    moms = dict(ffn1_norm=m_ffn1_norm, ffn1_w_gate=m_ffn1_w_gate, ffn1_w_up=m_ffn1_w_up, ffn1_w_down=m_ffn1_w_down,
                mix_norm=m_mix_norm, w_in=m_w_in, pool_w=m_pool_w, pool_scale=m_pool_scale, w_out=m_w_out,
                ffn2_norm=m_ffn2_norm, ffn2_w_gate=m_ffn2_w_gate, ffn2_w_up=m_ffn2_w_up, ffn2_w_down=m_ffn2_w_down,
                final_norm=m_final_norm)
    vels = dict(ffn1_norm=v_ffn1_norm, ffn1_w_gate=v_ffn1_w_gate, ffn1_w_up=v_ffn1_w_up, ffn1_w_down=v_ffn1_w_down,
                mix_norm=v_mix_norm, w_in=v_w_in, pool_w=v_pool_w, pool_scale=v_pool_scale, w_out=v_w_out,
                ffn2_norm=v_ffn2_norm, ffn2_w_gate=v_ffn2_w_gate, ffn2_w_up=v_ffn2_w_up, ffn2_w_down=v_ffn2_w_down,
                final_norm=v_final_norm)
    names = list(weights)

    me_idx = 4 * lax.axis_index("x") + 2 * lax.axis_index("y") + lax.axis_index("c")
    me_arr = me_idx.reshape(1).astype(jnp.int32)

    tr = lambda w: jnp.swapaxes(w, 1, 2).astype(bf16)
    shards = [tr(weights[nm]) if nm in _TRANSPOSED else weights[nm].astype(bf16) for nm in _BIG]

    def landing_zones(l, which):
        return _place_own(me_arr, [shards[t] for t in which], l)

    g_ffn1 = [ffn1_norm[l].reshape(1, D) for l in range(L)]
    g_mix = [mix_norm[l].reshape(1, D) for l in range(L)]
    g_ffn2 = [ffn2_norm[l].reshape(1, D) for l in range(L)]
    wbd_all = _block_diag(pool_w).astype(bf16)
    wbd = [wbd_all[l] for l in range(L)]
    pscale = [pool_scale[l].reshape(1, PW) for l in range(L)]
    tabs = _rope_tables(positions)
    flat = lambda a: a.reshape(S, a.shape[-1])
    r4 = lambda a: a.reshape(4, S // 4, a.shape[-1])
    r16 = lambda a: a.reshape(16, S // 16, a.shape[-1])

    first, rest, whole = (0, 1, 2, 3), (4, 5, 6, 7), tuple(range(8))

    def ag_begin(l, which, after):
        tag = f"{l}{'' if which == whole else 'r'}"
        send_sems, recv_sems, zones, token = _ag_start(landing_zones(l, which), after, tag)
        return dict(tag=tag, zones=zones, s=send_sems, r=recv_sems), token

    def ag_second(ch, after):
        ch["ps"], ch["pr"], ch["zones"], token = _ag_pass(ch["zones"], ch["r"], after, ch["tag"])
        return token

    def ag_third(ch, after):
        ch["qs"], ch["qr"], ch["zones"], token = _ag_last(ch["zones"], ch["pr"], after, ch["tag"])
        return token

    def ag_end(ch, after):
        return _ag_wait(ch["zones"], ch["s"], ch["r"], ch["ps"], ch["pr"], ch["qs"], ch["qr"], after, ch["tag"])

    head = _all_gather(landing_zones(0, first))
    ch_rest, tok_rest = ag_begin(0, rest, head[0])
    chains = {}
    chains[1], tok_next = ag_begin(1, whole, head[0])
    gathered = [None] * L
    xs = x.reshape(S, D)
    saved = []
    for l in range(L):
        first_after, second_after = (), ()
        if l == 0:
            gt1, ut1, dn1, wint = head
            first_after = (tok_rest, tok_next)
        else:
            gt1, ut1, dn1, wint, wout, gt2, ut2, dn2 = gathered[l]
        x0 = xs
        x1, gate1, up1 = _ffn_fwd(x0, g_ffn1[l], gt1, ut1, dn1, after=first_after)
        hmix, vp, q1, k1, v1, q4, k4, v4, q16, k16, v16 = _mix_in_fwd(x1, g_mix[l], wint, tabs)
        q4, k4, v4, q16, k16, v16 = map(flat, (q4, k4, v4, q16, k16, v16))
        ypool, diff = _pool_fwd(vp, wbd[l], pscale[l])
        after_attn = None
        if l == 0:
            after_attn = ag_second(ch_rest, [ypool, q16])
        o1, l1 = _attn_fwd(q1, k1, v1, S, after=after_attn)
        o4, l4 = _attn_fwd(q4, k4, v4, S // 4, after=after_attn)
        o16, l16 = _attn_fwd(q16, k16, v16, S // 16, after=after_attn)
        if l == 0:
            token = ag_third(ch_rest, [o1, o4, o16])
            wout, gt2, ut2, dn2 = ag_end(ch_rest, token)
            gathered[0] = list(head) + [wout, gt2, ut2, dn2]
        elif l + 1 < L:
            second_after = (ag_second(chains[l + 1], [o1, o4, o16]),)
        x2, mixed, o, lse1, lse4, lse16 = _mix_out_fwd(x1, ypool, o1, l1, r4(o4), r4(l4), r16(o16), r16(l16), wout)
        if l == 0:
            second_after = (ag_second(chains[1], x2),)
        x3, gate2, up2 = _ffn_fwd(x2, g_ffn2[l], gt2, ut2, dn2, after=second_after)
        if l + 1 < L:
            token = ag_third(chains[l + 1], x3)
            if l + 2 < L:
                chains[l + 2], token = ag_begin(l + 2, whole, token)
            gathered[l + 1] = ag_end(chains[l + 1], token)
        saved.append(dict(x0=x0, x1=x1, x2=x2, gate1=gate1, up1=up1, gate2=gate2, up2=up2, hmix=hmix, diff=diff,
                          qkv=((q1, k1, v1), (q4, k4, v4), (q16, k16, v16)), mixed=mixed, o=o,
                          lse=(lse1, flat(lse4), flat(lse16))))
        xs = x3

    dx, loss_part, d_final = _loss_head(xs, final_norm.reshape(1, D), loss_target.reshape(S, D))

    d_norm = {nm: [None] * L for nm in ("ffn1_norm", "mix_norm", "ffn2_norm")}
    d_poolw, d_pscale = [None] * L, [None] * L
    group_a = ("ffn2_w_gate", "ffn2_w_up", "ffn2_w_down", "w_out")
    group_b = ("ffn1_w_gate", "ffn1_w_up", "ffn1_w_down", "w_in")
    acc = {}

    as_rows = lambda a, nm: jnp.swapaxes(a, 1, 2) if nm in _TRANSPOSED else a
    w_rows = {nm: as_rows(weights[nm], nm) for nm in _BIG}
    m_rows = {nm: as_rows(moms[nm], nm) for nm in _BIG}
    v_rows = {nm: as_rows(vels[nm], nm) for nm in _BIG}

    def exchange(full, group, after, tag):
        srcs = [full[nm] for nm in group]
        slots = [lax.empty((NDEV, g.shape[0] // NDEV, D), bf16) for g in srcs]
        ssem, rsem, srcs, slots, token = _rs_start(srcs, slots, after, tag)
        return (srcs, slots, ssem, rsem, tag), token

    def update(l, group, flight, after):
        srcs, slots, ssem, rsem, tag = flight
        srcs, slots = _rs_wait(srcs, slots, ssem, rsem, after, tag)
        for nm, full_g, slots_g in zip(group, srcs, slots):
            acc[nm] = _reduce_adamw(acc.get(nm), me_arr, full_g, slots_g, w_rows[nm], m_rows[nm], v_rows[nm], l)
        return [acc[nm][0] for nm in group], slots

    core_arr = lax.axis_index("c").reshape(1).astype(jnp.int32)
    chip_arr = (2 * lax.axis_index("x") + lax.axis_index("y")).reshape(1).astype(jnp.int32)

    def exchange_cores(full, group, after, tag):
        full4s = [full[nm].reshape(4, 2, full[nm].shape[0] // NDEV, D) for nm in group]
        bufs = [lax.empty((4,) + a.shape[2:], bf16) for a in full4s]
        ssem, rsem, full4s, bufs, token = _pair_start(full4s, bufs, after, tag)
        return (full4s, bufs, ssem, rsem, tag), token

    def exchange_chips(flight, after):
        full4s, bufs, ssem, rsem, tag = flight
        full4s, bufs = _pair_wait(full4s, bufs, ssem, rsem, after, tag)
        sums = _pair_sum(core_arr, full4s, bufs)
        slots = [lax.empty(a.shape, bf16) for a in sums]
        ssem, rsem, sums, slots, token = _chip_start(sums, slots, bufs[0], tag)
        return (sums, slots, ssem, rsem, tag), token

    def update_chips(l, group, flight, after):
        sums, slots, ssem, rsem, tag = flight
        sums, slots = _chip_wait(sums, slots, ssem, rsem, after, tag)
        for nm, sums_g, slots_g in zip(group, sums, slots):
            own = sums_g.reshape(4 * sums_g.shape[1], D)
            acc[nm] = _reduce_adamw(acc.get(nm), chip_arr, own, slots_g, w_rows[nm], m_rows[nm], v_rows[nm], l)
        return [acc[nm][0] for nm in group]

    flights = {}
    token_b = None
    for l in reversed(range(L)):
        sv = saved[l]
        gt1, ut1, dn1, wint, wout, gt2, ut2, dn2 = gathered[l]
        full = {}
        dx, dgate, dup, h, dy, d_norm["ffn2_norm"][l] = _ffn_bwd_d(
            sv["x2"], g_ffn2[l], sv["gate2"], sv["up2"], dx, gt2, ut2, dn2, after=() if token_b is None else (token_b,))
        full["ffn2_w_gate"], full["ffn2_w_up"], full["ffn2_w_down"] = _ffn_bwd_w(h, dy, sv["gate2"], sv["up2"], dgate, dup)

        dxb, dyp, do1, do4, do16, dl1, dl4, dl16 = _mix_out_bwd(dx, sv["o"], wout)
        full["w_out"] = _wgrad(sv["mixed"], dxb)
        flights[l, "a"], token_a = (exchange_cores if l == 0 else exchange)(full, group_a, dxb, f"a{l}")
        dvp, d_poolw[l], d_pscale[l] = _pool_bwd(dyp, sv["diff"], wbd[l], pscale[l], after=(token_a,))
        dos, dls = (do1, flat(do4), flat(do16)), (dl1, flat(dl4), flat(dl16))
        dqkv = []
        for b, lc in enumerate((S, S // 4, S // 16)):
            qb, kb, vb = sv["qkv"][b]
            dqkv.append(_attn_bwd(qb, kb, vb, dos[b], sv["lse"][b], dls[b], lc))
        d4 = tuple(r4(a) for a in dqkv[1])
        d16 = tuple(r16(a) for a in dqkv[2])
        mix_after = ()
        if l == 0:
            flights[0, "a"], token_a = exchange_chips(flights[0, "a"], [dqkv[0][0], dqkv[1][0], dqkv[2][0]])
            mix_after = (token_a,)
        dx, dproj, d_norm["mix_norm"][l] = _mix_in_bwd(dx, sv["x1"], g_mix[l], wint, tabs, dvp, dqkv[0], d4, d16,
                                                       after=mix_after)
        full["w_in"] = _wgrad(dproj, sv["hmix"])

        dx, dgate, dup, h, dy, d_norm["ffn1_norm"][l] = _ffn_bwd_d(sv["x0"], g_ffn1[l], sv["gate1"], sv["up1"], dx, gt1, ut1, dn1)
        full["ffn1_w_gate"], full["ffn1_w_up"], full["ffn1_w_down"] = _ffn_bwd_w(h, dy, sv["gate1"], sv["up1"], dgate, dup)

        after = dx
        if l + 1 < L and l + 1 >= 2:
            after, _ = update(l + 1, group_a, flights.pop((l + 1, "a")), after)
            after, _ = update(l + 1, group_b, flights.pop((l + 1, "b")), after)
        if l > 0:
            flights[l, "b"], token_b = exchange(full, group_b, after, f"b{l}")

    flights[0, "b"], token_b = exchange_cores(full, group_b, dx, "b0")
    pad8 = lambda a: jnp.pad(a, ((0, 8 - a.shape[0]), (0, 0)))
    misc = jnp.concatenate([d_final, jnp.concatenate(d_pscale, axis=1), loss_part], axis=0)
    small = jnp.concatenate(
        [pad8(jnp.concatenate(d_norm[nm], axis=0)) for nm in ("ffn1_norm", "mix_norm", "ffn2_norm")]
        + [pad8(misc), jnp.stack(d_poolw).reshape(L * 16, D)], axis=0)
    small_slots = lax.dynamic_update_slice(lax.empty((NDEV, SMALL_ROWS, D), f32), small[None], (me_idx, 0, 0))
    pack_sems = _rs_start([small], [small_slots], token_b, "pack")
    flights[0, "b"], token_b = exchange_chips(flights[0, "b"], pack_sems[-1])

    after = token_b
    for key in [(1, "a"), (1, "b")]:
        after, _ = update(key[0], group_a if key[1] == "a" else group_b, flights.pop(key), after)
    _, pack_slots = _rs_wait(pack_sems[2], pack_sems[3], pack_sems[0], pack_sems[1], after, "pack")
    sm = _sum_slots(pack_slots[0], SMALL_ROWS)
    grads = {}
    grads["ffn1_norm"], grads["mix_norm"], grads["ffn2_norm"] = sm[0:L], sm[8:8 + L], sm[16:16 + L]
    grads["final_norm"] = sm[24]
    grads["pool_scale"] = sm[25].reshape(L, PW)
    grads["pool_w"] = sm[32:32 + L * 16].reshape(L, 4, 64, 64)
    loss = sm[26, 0]
    upd = {nm: _adamw(weights[nm], grads[nm], moms[nm], vels[nm]) for nm in names if nm not in _BIG}
    after = update_chips(0, group_a, flights.pop((0, "a")), [upd[nm][0] for nm in upd])
    update_chips(0, group_b, flights.pop((0, "b")), after)
    for nm in _BIG:
        grads[nm], upd[nm] = as_rows(acc[nm][0], nm), tuple(as_rows(a, nm) for a in acc[nm][1:])
    return (loss, dx.reshape(1, S, D), *[grads[nm] for nm in names], *[upd[nm][0] for nm in names],
            *[upd[nm][1] for nm in names], *[upd[nm][2] for nm in names])
```

```python
import jax
import jax.numpy as jnp
from jax import lax
from jax.experimental import pallas as pl
from jax.experimental.pallas import tpu as pltpu

f32 = jnp.float32
bf16 = jnp.bfloat16
SDS = jax.ShapeDtypeStruct

D = 1024
S = 2048
F = 2816
L = 4
PW = 256
AW = 768
PROJ = PW + 3 * AW
NDEV = 8
TM = 256
QB = 128
HALF = 64
NG = AW // 128
NORM_EPS = 1e-6
MASK_VALUE = -1e30
ROPE_THETA = 500000.0
ADAM_LR, ADAM_B1, ADAM_B2, ADAM_EPS, ADAM_WD, ADAM_STEP = 0.001, 0.9, 0.999, 1e-08, 0.01, 10
POOL_WINDOWS = (2, 4, 8, 16)
PAD = 8
SMALL_ROWS = 96
VMEM_LIMIT = 56 * 1024 * 1024

_CP = pltpu.CompilerParams(vmem_limit_bytes=VMEM_LIMIT)
_ANY = pl.BlockSpec(memory_space=pl.ANY)
_HBM = pl.BlockSpec(memory_space=pltpu.HBM)
_SEM = pl.BlockSpec(memory_space=pltpu.SEMAPHORE)
_MESH = pl.DeviceIdType.MESH
_CP_SPLIT = pltpu.CompilerParams(has_side_effects=pltpu.SideEffectType.DATAFLOW_SIDE_EFFECTING)


def _dot_nn(a, b):
    return lax.dot_general(a, b, (((1,), (0,)), ((), ())), preferred_element_type=f32)


def _dot_nt(a, b):
    return lax.dot_general(a, b, (((1,), (1,)), ((), ())), preferred_element_type=f32)


def _dot_tn(a, b):
    return lax.dot_general(a, b, (((0,), (0,)), ((), ())), preferred_element_type=f32)


def _rms(x, g):
    r = lax.rsqrt(jnp.mean(x * x, axis=-1, keepdims=True) + NORM_EPS)
    xh = x * r
    return r, xh, xh * g


def _rms_bwd(dh, r, xh, g):
    dxh = dh * g
    return r * (dxh - xh * jnp.mean(dxh * xh, axis=-1, keepdims=True))


def _tile(cols, rows=TM):
    return pl.BlockSpec((rows, cols), lambda i: (i, 0))


def _const(shape):
    return pl.BlockSpec(shape, lambda i: (0,) * len(shape))


def _layer(rows, cols):
    return pl.BlockSpec((rows, cols), lambda i: (0, 0), pipeline_mode=pl.Buffered(1))


def _p4(cols=AW):
    return pl.BlockSpec((4, TM // 4, cols), lambda i: (0, i, 0))


def _p16(cols=AW):
    return pl.BlockSpec((16, TM // 16, cols), lambda i: (0, i, 0))


def _cols(j):
    return slice(128 * j, 128 * (j + 1))


def _follow(body, n_in, after):
    k = len(after)
    return body if k == 0 else (lambda *refs: body(*refs[:n_in], *refs[n_in + k:]))


def _ffn_fwd(x, g, gt, ut, dn, after=()):
    def body(x_ref, g_ref, gt_ref, ut_ref, dn_ref, xo_ref, gate_ref, up_ref):
        x = x_ref[...]
        _, _, hn = _rms(x, g_ref[...])
        h = hn.astype(bf16)
        gate = _dot_nt(h, gt_ref[...])
        up = _dot_nt(h, ut_ref[...])
        gate_ref[...] = gate.astype(bf16)
        up_ref[...] = up.astype(bf16)
        a = (gate * jax.nn.sigmoid(gate) * up).astype(bf16)
        xo_ref[...] = x + 0.5 * _dot_nn(a, dn_ref[...])

    rows = 2 * TM
    return pl.pallas_call(
        _follow(body, 5, after), grid=(S // rows,),
        in_specs=[_tile(D, rows), _layer(1, D), _layer(F, D), _layer(F, D), _layer(F, D)] + [_ANY] * len(after),
        out_specs=[_tile(D, rows), _tile(F, rows), _tile(F, rows)],
        out_shape=[SDS((S, D), f32), SDS((S, F), bf16), SDS((S, F), bf16)],
        compiler_params=_CP, name="ffn_fwd")(x, g, gt, ut, dn, *after)


def _ffn_bwd_d(x, g, gate, up, dxo, gt, ut, dn, after=()):
    def body(x_ref, g_ref, gate_ref, up_ref, dxo_ref, gt_ref, ut_ref, dn_ref,
             dx_ref, dgate_ref, dup_ref, h_ref, dy_ref, dg_ref):
        x = x_ref[...]
        g = g_ref[...]
        r, xh, hn = _rms(x, g)
        h_ref[...] = hn.astype(bf16)
        dxo = dxo_ref[...]
        dy = (0.5 * dxo).astype(bf16)
        dy_ref[...] = dy
        da = _dot_nt(dy, dn_ref[...])
        gate = gate_ref[...].astype(f32)
        up = up_ref[...].astype(f32)
        sg = jax.nn.sigmoid(gate)
        dgate = (da * up * (sg * (1.0 + gate * (1.0 - sg)))).astype(bf16)
        dup = (da * (gate * sg)).astype(bf16)
        dgate_ref[...] = dgate
        dup_ref[...] = dup
        dh = _dot_nn(dgate, gt_ref[...]) + _dot_nn(dup, ut_ref[...])

        @pl.when(pl.program_id(0) == 0)
        def _():
            dg_ref[...] = jnp.zeros_like(dg_ref)

        dg_ref[...] += jnp.sum(dh * xh, axis=0, keepdims=True)
        dx_ref[...] = dxo + _rms_bwd(dh, r, xh, g)

    return pl.pallas_call(
        _follow(body, 8, after), grid=(S // TM,),
        in_specs=[_tile(D), _layer(1, D), _tile(F), _tile(F), _tile(D),
                  _layer(F, D), _layer(F, D), _layer(F, D)] + [_ANY] * len(after),
        out_specs=[_tile(D), _tile(F), _tile(F), _tile(D), _tile(D), _const((1, D))],
        out_shape=[SDS((S, D), f32), SDS((S, F), bf16), SDS((S, F), bf16), SDS((S, D), bf16),
                   SDS((S, D), bf16), SDS((1, D), f32)],
        compiler_params=_CP, name="ffn_bwd_d")(x, g, gate, up, dxo, gt, ut, dn, *after)


def _ffn_bwd_w(h, dy, gate, up, dgate, dup):
    fc = 256

    def body(h_ref, dy_ref, gate_ref, up_ref, dgate_ref, dup_ref, dgt_ref, dut_ref, ddn_ref):
        gate = gate_ref[...].astype(f32)
        a = (gate * jax.nn.sigmoid(gate) * up_ref[...].astype(f32)).astype(bf16)
        ddn_ref[...] = _dot_tn(a, dy_ref[...]).astype(bf16)
        h = h_ref[...]
        dgt_ref[...] = _dot_tn(dgate_ref[...], h).astype(bf16)
        dut_ref[...] = _dot_tn(dup_ref[...], h).astype(bf16)

    col = pl.BlockSpec((S, fc), lambda j: (0, j))
    row = pl.BlockSpec((fc, D), lambda j: (j, 0))
    full = pl.BlockSpec((S, D), lambda j: (0, 0))
    return pl.pallas_call(
        body, grid=(F // fc,),
        in_specs=[full, full, col, col, col, col],
        out_specs=[row, row, row],
        out_shape=[SDS((F, D), bf16)] * 3,
        compiler_params=_CP, name="ffn_bwd_w")(h, dy, gate, up, dgate, dup)


def _wgrad(a, b):
    m, n = a.shape[1], b.shape[1]
    mc = 2 * TM

    def body(a_ref, b_ref, o_ref):
        o_ref[...] = _dot_tn(a_ref[...], b_ref[...]).astype(bf16)

    return pl.pallas_call(
        body, grid=(m // mc,),
        in_specs=[pl.BlockSpec((S, mc), lambda j: (0, j)), pl.BlockSpec((S, n), lambda j: (0, 0))],
        out_specs=pl.BlockSpec((mc, n), lambda j: (j, 0)),
        out_shape=SDS((m, n), bf16),
        compiler_params=_CP, name="wgrad")(a, b)


def _rope(t, c, sn, sp):
    return t * c + pltpu.roll(t, 120, 1) * sn + pltpu.roll(t, 8, 1) * sp


def _rope_bwd(d, c, sn, sp):
    return d * c + pltpu.roll(d * sn, 8, 1) + pltpu.roll(d * sp, 120, 1)


def _rope_tables(positions):
    inv_freq = ROPE_THETA ** (-jnp.arange(0, 16, 2, dtype=f32) / 16)
    ang = positions.reshape(S, 1).astype(f32) * inv_freq
    cos, sin = jnp.cos(ang), jnp.sin(ang)
    one = jnp.ones((S, 48), f32)
    zero8 = jnp.zeros((S, 8), f32)
    zero48 = jnp.zeros((S, 48), f32)
    c = jnp.concatenate([cos, cos, one], axis=1)
    sn = jnp.concatenate([-sin, zero8, zero48], axis=1)
    sp = jnp.concatenate([zero8, sin, zero48], axis=1)
    return tuple(jnp.concatenate([t, t], axis=1) for t in (c, sn, sp))


def _dilation_perm(n, back=False):
    per = TM // n
    i = lax.broadcasted_iota(jnp.int32, (TM, TM), 1 if back else 0)
    j = lax.broadcasted_iota(jnp.int32, (TM, TM), 0 if back else 1)
    return jnp.where(j == n * (i % per) + i // per, 1.0, 0.0).astype(bf16)


def _mix_in_fwd(x, g, wint, tabs):
    def body(x_ref, g_ref, w_ref, c_ref, sn_ref, sp_ref,
             h_ref, vp_ref, q1, k1, v1, q4, k4, v4, q16, k16, v16):
        _, _, hn = _rms(x_ref[...], g_ref[...])
        h = hn.astype(bf16)
        h_ref[...] = h
        proj = _dot_nt(h, w_ref[...])
        vp_ref[...] = proj[:, :PW]
        c, sn, sp = c_ref[...], sn_ref[...], sp_ref[...]
        perm4, perm16 = _dilation_perm(4), _dilation_perm(16)
        for kind, (o1, o4, o16) in enumerate(((q1, q4, q16), (k1, k4, k16), (v1, v4, v16))):
            for j in range(NG):
                t = proj[:, PW + kind * AW + 128 * j: PW + kind * AW + 128 * (j + 1)]
                if kind == 0:
                    t = _rope(t, c, sn, sp) * 0.125
                elif kind == 1:
                    t = _rope(t, c, sn, sp)
                o1[:, _cols(j)] = t.astype(bf16)
            nat = o1[...]
            o4[...] = _dot_nn(perm4, nat).astype(bf16).reshape(4, TM // 4, AW)
            o16[...] = _dot_nn(perm16, nat).astype(bf16).reshape(16, TM // 16, AW)

    nat, d4, d16 = SDS((S, AW), bf16), SDS((4, S // 4, AW), bf16), SDS((16, S // 16, AW), bf16)
    return pl.pallas_call(
        body, grid=(S // TM,),
        in_specs=[_tile(D), _layer(1, D), _layer(PROJ, D), _tile(128), _tile(128), _tile(128)],
        out_specs=[_tile(D), _tile(PW)] + [_tile(AW)] * 3 + [_p4()] * 3 + [_p16()] * 3,
        out_shape=[SDS((S, D), bf16), SDS((S, PW), f32)] + [nat] * 3 + [d4] * 3 + [d16] * 3,
        compiler_params=_CP, name="mix_in_fwd")(x, g, wint, *tabs)


def _mix_in_bwd(dxo, x, g, wint, tabs, dvp, d1, d4, d16, after=()):
    def body(dxo_ref, x_ref, g_ref, w_ref, c_ref, sn_ref, sp_ref, dvp_ref,
             dq1, dk1, dv1, dq4, dk4, dv4, dq16, dk16, dv16,
             dx_ref, dproj_ref, dg_ref):
        c, sn, sp = c_ref[...], sn_ref[...], sp_ref[...]
        dproj_ref[:, :PW] = dvp_ref[...].astype(bf16)
        back4, back16 = _dilation_perm(4, True), _dilation_perm(16, True)
        for kind, (a1, a4, a16) in enumerate(((dq1, dq4, dq16), (dk1, dk4, dk16), (dv1, dv4, dv16))):
            n4 = _dot_nn(back4, a4[...].reshape(TM, AW))
            n16 = _dot_nn(back16, a16[...].reshape(TM, AW))
            for j in range(NG):
                t = a1[:, _cols(j)].astype(f32) + n4[:, _cols(j)] + n16[:, _cols(j)]
                if kind == 0:
                    t = _rope_bwd(t * 0.125, c, sn, sp)
                elif kind == 1:
                    t = _rope_bwd(t, c, sn, sp)
                dproj_ref[:, PW + kind * AW + 128 * j: PW + kind * AW + 128 * (j + 1)] = t.astype(bf16)
        g = g_ref[...]
        r_, xh, _ = _rms(x_ref[...], g)
        dh = _dot_nn(dproj_ref[...], w_ref[...])

        @pl.when(pl.program_id(0) == 0)
        def _():
            dg_ref[...] = jnp.zeros_like(dg_ref)

        dg_ref[...] += jnp.sum(dh * xh, axis=0, keepdims=True)
        dx_ref[...] = dxo_ref[...] + _rms_bwd(dh, r_, xh, g)

    return pl.pallas_call(
        _follow(body, 17, after), grid=(S // TM,),
        in_specs=[_tile(D), _tile(D), _layer(1, D), _layer(PROJ, D), _tile(128), _tile(128), _tile(128),
                  _tile(PW)] + [_tile(AW)] * 3 + [_p4()] * 3 + [_p16()] * 3 + [_ANY] * len(after),
        out_specs=[_tile(D), _tile(PROJ), _const((1, D))],
        out_shape=[SDS((S, D), f32), SDS((S, PROJ), bf16), SDS((1, D), f32)],
        compiler_params=_CP, name="mix_in_bwd")(dxo, x, g, wint, *tabs, dvp, *d1, *d4, *d16, *after)


def _pool_sums(pad_ref, base, rows, adjoint):
    lane_group = lax.broadcasted_iota(jnp.int32, (rows, PW), 1) // 64
    sign = -1 if adjoint else 1

    def sh(o):
        return pad_ref[pl.ds(PAD + base + sign * o, rows), :]

    out = None
    acc = None
    lo, hi = 0, 0
    for gi, w in enumerate(POOL_WINDOWS):
        for o in list(range(-(w // 2), lo)) + list(range(hi, w - w // 2)):
            acc = sh(o) if acc is None else acc + sh(o)
        lo, hi = -(w // 2), w - w // 2
        out = acc if out is None else jnp.where(lane_group >= gi, acc, out)
    return out


def _pool_counts(base, rows):
    pos = base + lax.broadcasted_iota(jnp.int32, (rows, PW), 0)
    lane_group = lax.broadcasted_iota(jnp.int32, (rows, PW), 1) // 64
    cnt = None
    for gi, w in enumerate(POOL_WINDOWS):
        lo = jnp.maximum(pos - w // 2, 0)
        hi = jnp.minimum(pos + w - 1 - w // 2, S - 1)
        c = (hi - lo + 1).astype(f32)
        cnt = c if cnt is None else jnp.where(lane_group >= gi, c, cnt)
    return cnt


def _pool_fwd(vp, wbd, scale):
    ch = 256

    def body(vp_ref, w_ref, sc_ref, y_ref, diff_ref, pad):
        pad[pl.ds(0, PAD), :] = jnp.zeros((PAD, PW), f32)
        pad[pl.ds(PAD + S, PAD), :] = jnp.zeros((PAD, PW), f32)
        pad[pl.ds(PAD, S), :] = vp_ref[...]
        for b in range(S // ch):
            base = b * ch
            pooled = _pool_sums(pad, base, ch, False) / _pool_counts(base, ch)
            diff = (pooled - vp_ref[pl.ds(base, ch), :]).astype(bf16)
            diff_ref[pl.ds(base, ch), :] = diff
            y_ref[pl.ds(base, ch), :] = _dot_nn(diff, w_ref[...]) * sc_ref[...]

    whole = lambda shape: pl.BlockSpec(shape, lambda i: (0,) * len(shape))
    return pl.pallas_call(
        body, grid=(1,),
        in_specs=[whole((S, PW)), whole((PW, PW)), whole((1, PW))],
        out_specs=[whole((S, PW)), whole((S, PW))],
        out_shape=[SDS((S, PW), f32), SDS((S, PW), bf16)],
        scratch_shapes=[pltpu.VMEM((S + 2 * PAD, PW), f32)],
        compiler_params=_CP, name="pool_fwd")(vp, wbd, scale)


def _pool_bwd(dy, diff, wbd, scale, after=()):
    ch = 256

    def body(dy_ref, diff_ref, w_ref, sc_ref, dvp_ref, dw_ref, dsc_ref, pad):
        pad[pl.ds(0, PAD), :] = jnp.zeros((PAD, PW), f32)
        pad[pl.ds(PAD + S, PAD), :] = jnp.zeros((PAD, PW), f32)
        dw = jnp.zeros((PW, PW), f32)
        dsc = jnp.zeros((1, PW), f32)
        for b in range(S // ch):
            base = b * ch
            dy = dy_ref[pl.ds(base, ch), :]
            diff = diff_ref[pl.ds(base, ch), :]
            dsc = dsc + jnp.sum(dy * _dot_nn(diff, w_ref[...]), axis=0, keepdims=True)
            dz = (dy * sc_ref[...]).astype(bf16)
            dw = dw + _dot_tn(diff, dz)
            ddiff = _dot_nt(dz, w_ref[...])
            dvp_ref[pl.ds(base, ch), :] = -ddiff
            pad[pl.ds(PAD + base, ch), :] = ddiff / _pool_counts(base, ch)
        for gi in range(4):
            dw_ref[gi] = dw[64 * gi:64 * (gi + 1), 64 * gi:64 * (gi + 1)]
        dsc_ref[...] = dsc
        for b in range(S // ch):
            base = b * ch
            dvp_ref[pl.ds(base, ch), :] += _pool_sums(pad, base, ch, True)

    whole = lambda shape: pl.BlockSpec(shape, lambda i: (0,) * len(shape))
    return pl.pallas_call(
        _follow(body, 4, after), grid=(1,),
        in_specs=[whole((S, PW)), whole((S, PW)), whole((PW, PW)), whole((1, PW))] + [_ANY] * len(after),
        out_specs=[whole((S, PW)), whole((4, 64, 64)), whole((1, PW))],
        out_shape=[SDS((S, PW), f32), SDS((4, 64, 64), f32), SDS((1, PW), f32)],
        scratch_shapes=[pltpu.VMEM((S + 2 * PAD, PW), f32)],
        compiler_params=_CP, name="pool_bwd")(dy, diff, wbd, scale, *after)


def _attn_blocks(lc):
    bpc = lc // QB
    kw = min(2 * QB, lc)
    blocks = []
    for b in range(S // QB):
        t0 = (b % bpc) * QB
        ks_in = min(max(t0 - HALF, 0), lc - kw)
        blocks.append((b * QB, (b // bpc) * lc + ks_in, t0 - ks_in))
    return kw, blocks


def _attn_bias(bias_ref, kw, shifts):
    r = lax.broadcasted_iota(jnp.int32, (2 * QB, kw), 0) % QB
    c = lax.broadcasted_iota(jnp.int32, (2 * QB, kw), 1)
    for i, shift in enumerate(shifts):
        bias_ref[i] = jnp.where(jnp.abs(r + shift - c) <= HALF, 0.0, MASK_VALUE).astype(f32)


def _head_put(stats, pair, v0, v1, lane):
    return jnp.where(lane == 2 * pair, v0, jnp.where(lane == 2 * pair + 1, v1, stats))


def _head_cols(stats, pair, lane):
    c0 = jnp.sum(jnp.where(lane == 2 * pair, stats, 0.0), axis=-1, keepdims=True)
    c1 = jnp.sum(jnp.where(lane == 2 * pair + 1, stats, 0.0), axis=-1, keepdims=True)
    return jnp.concatenate([c0, c1], axis=0)


def _head_spread(stats, pair, head0):
    return jnp.where(head0, stats[:, 2 * pair:2 * pair + 1], stats[:, 2 * pair + 1:2 * pair + 2])


def _stack_heads(blk, head0):
    zero = jnp.zeros_like(blk)
    return jnp.concatenate([jnp.where(head0, blk, zero), jnp.where(head0, zero, blk)], axis=0)


def _attn_fwd(q, k, v, lc, after=None):
    kw, blocks = _attn_blocks(lc)
    shifts = sorted({b[2] for b in blocks})

    def body(q_ref, k_ref, v_ref, *refs):
        o_ref, lse_ref, bias_ref = refs[-3:]
        lane = lax.broadcasted_iota(jnp.int32, (QB, 128), 1)
        head0 = lane < 64
        pair = pl.program_id(0)
        _attn_bias(bias_ref, kw, shifts)

        @pl.when(pair == 0)
        def _():
            lse_ref[...] = jnp.zeros_like(lse_ref)

        for row0, kstart, shift in blocks:
            q2 = _stack_heads(q_ref[pl.ds(row0, QB), :], head0)
            kb = k_ref[pl.ds(kstart, kw), :]
            vb = v_ref[pl.ds(kstart, kw), :]
            s = _dot_nt(q2, kb) + bias_ref[shifts.index(shift)]
            m = jnp.max(s, axis=-1, keepdims=True)
            p = jnp.exp(s - m)
            den = jnp.sum(p, axis=-1, keepdims=True)
            o2 = _dot_nn(p.astype(bf16), vb) / den
            lse2 = m + jnp.log(den)
            o_ref[pl.ds(row0, QB), :] = jnp.where(head0, o2[:QB], o2[QB:]).astype(bf16)
            lse_ref[pl.ds(row0, QB), :] = _head_put(lse_ref[pl.ds(row0, QB), :], pair, lse2[:QB], lse2[QB:], lane)

    col = pl.BlockSpec((S, 128), lambda p: (0, p))
    extra = () if after is None else (after,)
    return pl.pallas_call(
        body, grid=(NG,), in_specs=[col, col, col] + [_ANY] * len(extra),
        out_specs=[col, pl.BlockSpec((S, 128), lambda p: (0, 0))],
        out_shape=[SDS((S, AW), bf16), SDS((S, 128), f32)],
        scratch_shapes=[pltpu.VMEM((len(shifts), 2 * QB, kw), f32)],
        compiler_params=_CP, name=f"attn_fwd_{lc}")(q, k, v, *extra)


def _attn_bwd(q, k, v, do, lse, delta, lc):
    kw, blocks = _attn_blocks(lc)
    shifts = sorted({b[2] for b in blocks})

    def body(q_ref, k_ref, v_ref, do_ref, lse_ref, dl_ref, dq_ref, dk_out, dv_out, bias_ref, dk_ref, dv_ref):
        lane = lax.broadcasted_iota(jnp.int32, (QB, 128), 1)
        head0 = lane < 64
        pair = pl.program_id(0)
        _attn_bias(bias_ref, kw, shifts)
        dk_ref[...] = jnp.zeros_like(dk_ref)
        dv_ref[...] = jnp.zeros_like(dv_ref)
        for row0, kstart, shift in blocks:
            q2 = _stack_heads(q_ref[pl.ds(row0, QB), :], head0)
            do2 = _stack_heads(do_ref[pl.ds(row0, QB), :], head0)
            lse2 = _head_cols(lse_ref[pl.ds(row0, QB), :], pair, lane)
            dl2 = _head_cols(dl_ref[pl.ds(row0, QB), :], pair, lane)
            kb = k_ref[pl.ds(kstart, kw), :]
            vb = v_ref[pl.ds(kstart, kw), :]
            p = jnp.exp(_dot_nt(q2, kb) + bias_ref[shifts.index(shift)] - lse2)
            ds = (p * (_dot_nt(do2, vb) - dl2)).astype(bf16)
            dq2 = _dot_nn(ds, kb)
            dq_ref[pl.ds(row0, QB), :] = jnp.where(head0, dq2[:QB], dq2[QB:]).astype(bf16)
            dk_ref[pl.ds(kstart, kw), :] += _dot_tn(ds, q2)
            dv_ref[pl.ds(kstart, kw), :] += _dot_tn(p.astype(bf16), do2)
        dk_out[...] = dk_ref[...].astype(bf16)
        dv_out[...] = dv_ref[...].astype(bf16)

    col = pl.BlockSpec((S, 128), lambda p: (0, p))
    stats = pl.BlockSpec((S, 128), lambda p: (0, 0))
    return pl.pallas_call(
        body, grid=(NG,), in_specs=[col] * 4 + [stats] * 2, out_specs=[col] * 3,
        out_shape=[SDS((S, AW), bf16)] * 3,
        scratch_shapes=[pltpu.VMEM((len(shifts), 2 * QB, kw), f32), pltpu.VMEM((S, 128), f32),
                        pltpu.VMEM((S, 128), f32)],
        compiler_params=_CP, name=f"attn_bwd_{lc}")(q, k, v, do, lse, delta)


def _mix_out_fwd(x, ypool, o1, l1, o4, l4, o16, l16, wout):
    def body(x_ref, yp_ref, o1_ref, l1_ref, o4_ref, l4_ref, o16_ref, l16_ref, w_ref,
             xo_ref, mixed_ref, o_ref, lse1_ref, lse4_ref, lse16_ref, sl4, sl16, sl):
        head0 = lax.broadcasted_iota(jnp.int32, (TM, 128), 1) < 64
        for r in range(4):
            sl4[pl.ds(r, TM // 4, stride=4), :] = l4_ref[r]
        for r in range(16):
            sl16[pl.ds(r, TM // 16, stride=16), :] = l16_ref[r]
        n4 = _dot_nn(_dilation_perm(4, True), o4_ref[...].reshape(TM, AW))
        n16 = _dot_nn(_dilation_perm(16, True), o16_ref[...].reshape(TM, AW))
        a, b, c = l1_ref[...], sl4[...], sl16[...]
        m = jnp.maximum(jnp.maximum(a, b), c)
        wa, wb, wc = jnp.exp(a - m), jnp.exp(b - m), jnp.exp(c - m)
        den = wa + wb + wc
        wa, wb, wc = wa / den, wb / den, wc / den
        lse = m + jnp.log(den)
        lse1_ref[...] = lse
        sl[...] = lse
        mixed_ref[:, :PW] = yp_ref[...].astype(bf16)
        for j in range(NG):
            y = (_head_spread(wa, j, head0) * o1_ref[:, _cols(j)].astype(f32)
                 + _head_spread(wb, j, head0) * n4[:, _cols(j)] + _head_spread(wc, j, head0) * n16[:, _cols(j)])
            o_ref[:, _cols(j)] = y
            mixed_ref[:, PW + 128 * j: PW + 128 * (j + 1)] = y.astype(bf16)
        for r in range(4):
            lse4_ref[r] = sl[pl.ds(r, TM // 4, stride=4), :]
        for r in range(16):
            lse16_ref[r] = sl[pl.ds(r, TM // 16, stride=16), :]
        xo_ref[...] = x_ref[...] + _dot_nn(mixed_ref[...], w_ref[...])

    return pl.pallas_call(
        body, grid=(S // TM,),
        in_specs=[_tile(D), _tile(PW), _tile(AW), _tile(128), _p4(), _p4(128), _p16(), _p16(128), _layer(D, D)],
        out_specs=[_tile(D), _tile(D), _tile(AW), _tile(128), _p4(128), _p16(128)],
        out_shape=[SDS((S, D), f32), SDS((S, D), bf16), SDS((S, AW), f32), SDS((S, 128), f32),
                   SDS((4, S // 4, 128), f32), SDS((16, S // 16, 128), f32)],
        scratch_shapes=[pltpu.VMEM((TM, 128), f32)] * 3,
        compiler_params=_CP, name="mix_out_fwd")(x, ypool, o1, l1, o4, l4, o16, l16, wout)


def _mix_out_bwd(dxo, o, wout):
    def body(dxo_ref, o_ref, w_ref, dxb_ref, dyp_ref, do1, do4, do16, dl1, dl4, dl16, sdl):
        dxb = dxo_ref[...].astype(bf16)
        dxb_ref[...] = dxb
        dm = _dot_nt(dxb, w_ref[...])
        dyp_ref[...] = dm[:, :PW]
        lane = lax.broadcasted_iota(jnp.int32, (TM, 128), 1)
        head0 = lane < 64
        dl = jnp.zeros((TM, 128), f32)
        for j in range(NG):
            d = dm[:, PW + 128 * j: PW + 128 * (j + 1)]
            prod = d * o_ref[:, _cols(j)]
            dl = _head_put(dl, j, jnp.sum(jnp.where(head0, prod, 0.0), axis=-1, keepdims=True),
                           jnp.sum(jnp.where(head0, 0.0, prod), axis=-1, keepdims=True), lane)
            do1[:, _cols(j)] = d.astype(bf16)
        dl1[...] = dl
        sdl[...] = dl
        for r in range(4):
            dl4[r] = sdl[pl.ds(r, TM // 4, stride=4), :]
        for r in range(16):
            dl16[r] = sdl[pl.ds(r, TM // 16, stride=16), :]
        nat = do1[...]
        do4[...] = _dot_nn(_dilation_perm(4), nat).astype(bf16).reshape(4, TM // 4, AW)
        do16[...] = _dot_nn(_dilation_perm(16), nat).astype(bf16).reshape(16, TM // 16, AW)

    return pl.pallas_call(
        body, grid=(S // TM,),
        in_specs=[_tile(D), _tile(AW), _layer(D, D)],
        out_specs=[_tile(D), _tile(PW), _tile(AW), _p4(), _p16(), _tile(128), _p4(128), _p16(128)],
        out_shape=[SDS((S, D), bf16), SDS((S, PW), f32),
                   SDS((S, AW), bf16), SDS((4, S // 4, AW), bf16), SDS((16, S // 16, AW), bf16),
                   SDS((S, 128), f32), SDS((4, S // 4, 128), f32), SDS((16, S // 16, 128), f32)],
        scratch_shapes=[pltpu.VMEM((TM, 128), f32)],
        compiler_params=_CP, name="mix_out_bwd")(dxo, o, wout)


def _loss_head(x, g, target):
    def body(x_ref, g_ref, t_ref, dx_ref, loss_ref, dg_ref):
        g = g_ref[...]
        r, xh, y = _rms(x_ref[...], g)
        err = y - t_ref[...]
        dy = err * (1.0 / D)

        @pl.when(pl.program_id(0) == 0)
        def _():
            loss_ref[...] = jnp.zeros_like(loss_ref)
            dg_ref[...] = jnp.zeros_like(dg_ref)

        loss_ref[...] += jnp.broadcast_to(0.5 * jnp.sum(jnp.mean(err * err, axis=-1, keepdims=True)), (1, D))
        dg_ref[...] += jnp.sum(dy * xh, axis=0, keepdims=True)
        dx_ref[...] = _rms_bwd(dy, r, xh, g)

    return pl.pallas_call(
        body, grid=(S // TM,),
        in_specs=[_tile(D), _const((1, D)), _tile(D)],
        out_specs=[_tile(D), _const((1, D)), _const((1, D))],
        out_shape=[SDS((S, D), f32), SDS((1, D), f32), SDS((1, D), f32)],
        compiler_params=_CP, name="loss_head")(x, g, target)


def _peer(k):
    x, y, c = lax.axis_index("x"), lax.axis_index("y"), lax.axis_index("c")
    px = 1 - x if k & 4 else x
    py = 1 - y if k & 2 else y
    pc = 1 - c if k & 1 else c
    return (px, py, pc), 4 * px + 2 * py + pc


def _diag_route():
    x, y, c = lax.axis_index("x"), lax.axis_index("y"), lax.axis_index("c")
    idx_x, idx_y = _peer(4)[1], _peer(2)[1]
    return idx_x + c * (idx_y - idx_x), (x + c * (1 - 2 * x), (1 - y) + c * (2 * y - 1), c)


def _all_gather(lands):
    n = len(lands)

    def body(*refs):
        zones, send_sems, recv_sems = refs[n:2 * n], refs[2 * n], refs[2 * n + 1]
        me, me_idx = _peer(0)
        sibling, sib_idx = _peer(1)
        (x_nbr, idx_x), (y_nbr, idx_y), idx_d = _peer(4), _peer(2), _peer(6)[1]
        fwd_idx, fwd_dev = _diag_route()

        def copy(k, t, idx, to):
            return _row_copy(zones[t], idx, send_sems.at[k, t], recv_sems.at[k, t], to)

        sent = []

        def send(k, t, idx, to):
            cp = copy(k, t, idx, to)
            cp.start()
            sent.append(cp)

        for t in range(n):
            send(0, t, me_idx, sibling)
            send(1, t, me_idx, x_nbr)
            send(2, t, me_idx, y_nbr)
        for t in range(n):
            copy(1, t, idx_x, me).wait_recv()
            send(3, t, idx_x, sibling)
        for t in range(n):
            copy(2, t, idx_y, me).wait_recv()
            send(4, t, idx_y, sibling)
        for t in range(n):
            send(5, t, fwd_idx, fwd_dev)
        for t in range(n):
            copy(5, t, idx_d, me).wait_recv()
            send(6, t, idx_d, sibling)
        for k, mask in ((0, 1), (3, 5), (4, 3), (6, 7)):
            for t in range(n):
                copy(k, t, _peer(mask)[1], me).wait_recv()
        for cp in sent:
            cp.wait_send()

    return pl.pallas_call(
        body, in_specs=[_ANY] * n, out_specs=[_ANY] * n,
        out_shape=[SDS(a.shape, a.dtype) for a in lands], input_output_aliases={t: t for t in range(n)},
        scratch_shapes=[pltpu.SemaphoreType.DMA((7, n)), pltpu.SemaphoreType.DMA((7, n))],
        name="all_gather_weights")(*lands)


def _hbm(a):
    return pltpu.with_memory_space_constraint(a, pltpu.HBM)


def _rows(ref, idx):
    r = ref.shape[0] // NDEV
    return ref.at[pl.ds(idx * r, r), :]


def _row_copy(ref, idx, send_sem, recv_sem, to):
    return pltpu.make_async_remote_copy(src_ref=_rows(ref, idx), dst_ref=_rows(ref, idx), send_sem=send_sem,
                                        recv_sem=recv_sem, device_id=to, device_id_type=_MESH)


def _place_own(me, shards, l):
    n = len(shards)

    def body(me_ref, *refs):
        for t in range(n):
            refs[n + t][...] = refs[t][...]

    grid_spec = pltpu.PrefetchScalarGridSpec(
        num_scalar_prefetch=1, grid=(1,),
        in_specs=[pl.BlockSpec((None, s.shape[1], D), lambda i, me_ref: (l, 0, 0)) for s in shards],
        out_specs=[pl.BlockSpec((s.shape[1], D), lambda i, me_ref: (me_ref[0], 0)) for s in shards])
    return pl.pallas_call(
        body, grid_spec=grid_spec, out_shape=[SDS((NDEV * s.shape[1], D), s.dtype) for s in shards],
        compiler_params=_CP, name="place_own")(me, *shards)


_TOKEN = SDS((8, 128), f32)
def _ag_start(lands, after, l):
    n = len(lands)
    after = list(after) if isinstance(after, (list, tuple)) else [after]

    def body(*refs):
        zones, send_sems, recv_sems, token = refs[:n], refs[n + len(after)], refs[n + len(after) + 1], refs[-1]
        _, me_idx = _peer(0)
        for k, mask in enumerate((1, 4, 2)):
            for t in range(n):
                _row_copy(zones[t], me_idx, send_sems.at[k * n + t], recv_sems.at[k * n + t], _peer(mask)[0]).start()
        token[...] = jnp.zeros_like(token)

    outs = pl.pallas_call(
        body, name=f"ag_start_{l}", in_specs=[_HBM] * n + [_ANY] * len(after),
        out_specs=(_SEM, _SEM, *[_HBM] * n, pl.BlockSpec(memory_space=pltpu.VMEM)),
        out_shape=(pltpu.SemaphoreType.DMA((3 * n,)), pltpu.SemaphoreType.DMA((3 * n,)),
                   *[pltpu.HBM(a.shape, a.dtype) for a in lands], _TOKEN),
        input_output_aliases={t: 2 + t for t in range(n)}, compiler_params=_CP_SPLIT)(
            *[_hbm(a) for a in lands], *after)
    return outs[0], outs[1], list(outs[2:2 + n]), outs[-1]


def _ag_pass(lands, recv_sems, after, l):
    n = len(lands)
    after = list(after) if isinstance(after, (list, tuple)) else [after]

    def body(*refs):
        zones, recv_sems = refs[:n], refs[n]
        psend, precv, token = refs[n + 1 + len(after)], refs[n + 2 + len(after)], refs[-1]
        me, _ = _peer(0)
        sibling, _ = _peer(1)
        for j, mask in enumerate((4, 2)):
            idx = _peer(mask)[1]
            for t in range(n):
                _row_copy(zones[t], idx, psend.at[j * n + t], recv_sems.at[(1 + j) * n + t], me).wait_recv()
                _row_copy(zones[t], idx, psend.at[j * n + t], precv.at[j * n + t], sibling).start()
        fwd_idx, fwd_dev = _diag_route()
        for t in range(n):
            _row_copy(zones[t], fwd_idx, psend.at[2 * n + t], precv.at[2 * n + t], fwd_dev).start()
        token[...] = jnp.zeros_like(token)

    outs = pl.pallas_call(
        body, name=f"ag_pass_{l}", in_specs=[_HBM] * n + [_SEM] + [_ANY] * len(after),
        out_specs=(_SEM, _SEM, *[_HBM] * n, pl.BlockSpec(memory_space=pltpu.VMEM)),
        out_shape=(pltpu.SemaphoreType.DMA((3 * n,)), pltpu.SemaphoreType.DMA((3 * n,)),
                   *[pltpu.HBM(a.shape, a.dtype) for a in lands], _TOKEN),
        input_output_aliases={t: 2 + t for t in range(n)}, compiler_params=_CP_SPLIT)(*lands, recv_sems, *after)
    return outs[0], outs[1], list(outs[2:2 + n]), outs[-1]


def _ag_last(lands, precv, after, l):
    n = len(lands)
    after = list(after) if isinstance(after, (list, tuple)) else [after]

    def body(*refs):
        zones, precv = refs[:n], refs[n]
        qsend, qrecv, token = refs[n + 1 + len(after)], refs[n + 2 + len(after)], refs[-1]
        me, _ = _peer(0)
        sibling, _ = _peer(1)
        idx = _peer(6)[1]
        for t in range(n):
            _row_copy(zones[t], idx, qsend.at[t], precv.at[2 * n + t], me).wait_recv()
            _row_copy(zones[t], idx, qsend.at[t], qrecv.at[t], sibling).start()
        token[...] = jnp.zeros_like(token)

    outs = pl.pallas_call(
        body, name=f"ag_last_{l}", in_specs=[_HBM] * n + [_SEM] + [_ANY] * len(after),
        out_specs=(_SEM, _SEM, *[_HBM] * n, pl.BlockSpec(memory_space=pltpu.VMEM)),
        out_shape=(pltpu.SemaphoreType.DMA((n,)), pltpu.SemaphoreType.DMA((n,)),
                   *[pltpu.HBM(a.shape, a.dtype) for a in lands], _TOKEN),
        input_output_aliases={t: 2 + t for t in range(n)}, compiler_params=_CP_SPLIT)(*lands, precv, *after)
    return outs[0], outs[1], list(outs[2:2 + n]), outs[-1]


def _ag_wait(lands, send_sems, recv_sems, psend, precv, qsend, qrecv, after, l):
    n = len(lands)
    after = list(after) if isinstance(after, (list, tuple)) else [after]

    def body(*refs):
        zones = refs[:n]
        send_sems, recv_sems, psend, precv, qsend, qrecv = refs[n:n + 6]
        me, me_idx = _peer(0)
        for k in range(3):
            for t in range(n):
                _row_copy(zones[t], me_idx, send_sems.at[k * n + t], recv_sems.at[k * n + t], me).wait_send()
        for t in range(n):
            _row_copy(zones[t], _peer(1)[1], send_sems.at[t], recv_sems.at[t], me).wait_recv()
        fwd_idx, _ = _diag_route()
        for j, (mine, theirs) in enumerate(((_peer(4)[1], _peer(5)[1]), (_peer(2)[1], _peer(3)[1]))):
            for t in range(n):
                _row_copy(zones[t], mine, psend.at[j * n + t], precv.at[j * n + t], me).wait_send()
                _row_copy(zones[t], theirs, psend.at[j * n + t], precv.at[j * n + t], me).wait_recv()
        for t in range(n):
            _row_copy(zones[t], fwd_idx, psend.at[2 * n + t], precv.at[2 * n + t], me).wait_send()
            _row_copy(zones[t], _peer(6)[1], qsend.at[t], qrecv.at[t], me).wait_send()
            _row_copy(zones[t], _peer(7)[1], qsend.at[t], qrecv.at[t], me).wait_recv()

    outs = pl.pallas_call(
        body, name=f"ag_wait_{l}", in_specs=[_HBM] * n + [_SEM] * 6 + [_ANY] * len(after),
        out_specs=tuple([_HBM] * n), out_shape=tuple(pltpu.HBM(a.shape, a.dtype) for a in lands),
        input_output_aliases={t: t for t in range(n)}, compiler_params=_CP_SPLIT)(
            *lands, send_sems, recv_sems, psend, precv, qsend, qrecv, *after)
    return list(outs)


def _xchg_src(ref, slot_ref, idx):
    return _rows(ref, idx) if ref.shape[0] == NDEV * slot_ref.shape[1] else ref


def _rs_start(srcs, slots, after, tag):
    n = len(srcs)
    after = list(after) if isinstance(after, (list, tuple)) else [after]

    def body(*refs):
        src, slot = refs[:n], refs[n:2 * n]
        send_sems, recv_sems, token = refs[2 * n + len(after)], refs[2 * n + len(after) + 1], refs[-1]
        _, me_idx = _peer(0)
        for k in range(1, NDEV):
            dev, idx = _peer(k)
            for t in range(n):
                pltpu.make_async_remote_copy(
                    src_ref=_xchg_src(src[t], slot[t], idx), dst_ref=slot[t].at[me_idx],
                    send_sem=send_sems.at[(k - 1) * n + t], recv_sem=recv_sems.at[(k - 1) * n + t],
                    device_id=dev, device_id_type=_MESH).start()
        token[...] = jnp.zeros_like(token)

    outs = pl.pallas_call(
        body, name=f"rs_start_{tag}", in_specs=[_HBM] * (2 * n) + [_ANY] * len(after),
        out_specs=(_SEM, _SEM, *[_HBM] * (2 * n), pl.BlockSpec(memory_space=pltpu.VMEM)),
        out_shape=(pltpu.SemaphoreType.DMA(((NDEV - 1) * n,)), pltpu.SemaphoreType.DMA(((NDEV - 1) * n,)),
                   *[pltpu.HBM(a.shape, a.dtype) for a in list(srcs) + list(slots)], _TOKEN),
        input_output_aliases={t: 2 + t for t in range(2 * n)}, compiler_params=_CP_SPLIT)(
            *[_hbm(a) for a in list(srcs) + list(slots)], *after)
    return outs[0], outs[1], list(outs[2:2 + n]), list(outs[2 + n:2 + 2 * n]), outs[-1]


def _rs_wait(srcs, slots, send_sems, recv_sems, after, tag):
    n = len(srcs)
    after = list(after) if isinstance(after, (list, tuple)) else [after]

    def body(*refs):
        src, slot, send_sems, recv_sems = refs[:n], refs[n:2 * n], refs[2 * n], refs[2 * n + 1]
        me, _ = _peer(0)
        for k in range(1, NDEV):
            idx = _peer(k)[1]
            for t in range(n):
                cp = pltpu.make_async_remote_copy(
                    src_ref=_xchg_src(src[t], slot[t], idx), dst_ref=slot[t].at[idx],
                    send_sem=send_sems.at[(k - 1) * n + t], recv_sem=recv_sems.at[(k - 1) * n + t],
                    device_id=me, device_id_type=_MESH)
                cp.wait_send()
                cp.wait_recv()

    outs = pl.pallas_call(
        body, name=f"rs_wait_{tag}", in_specs=[_HBM] * (2 * n) + [_SEM, _SEM] + [_ANY] * len(after),
        out_specs=tuple([_HBM] * (2 * n)),
        out_shape=tuple(pltpu.HBM(a.shape, a.dtype) for a in list(srcs) + list(slots)),
        input_output_aliases={t: t for t in range(2 * n)}, compiler_params=_CP_SPLIT)(
            *srcs, *slots, send_sems, recv_sems, *after)
    return list(outs[:n]), list(outs[n:])


def _pair_start(full4s, bufs, after, tag):
    n = len(full4s)
    after = list(after) if isinstance(after, (list, tuple)) else [after]

    def body(*refs):
        full, buf = refs[:n], refs[n:2 * n]
        send_sems, recv_sems, token = refs[2 * n + len(after)], refs[2 * n + len(after) + 1], refs[-1]
        c = lax.axis_index("c")
        for t in range(n):
            pltpu.make_async_remote_copy(src_ref=full[t].at[:, 1 - c], dst_ref=buf[t], send_sem=send_sems.at[t],
                                         recv_sem=recv_sems.at[t], device_id=_peer(1)[0], device_id_type=_MESH).start()
        token[...] = jnp.zeros_like(token)

    outs = pl.pallas_call(
        body, name=f"pair_start_{tag}", in_specs=[_HBM] * (2 * n) + [_ANY] * len(after),
        out_specs=(_SEM, _SEM, *[_HBM] * (2 * n), pl.BlockSpec(memory_space=pltpu.VMEM)),
        out_shape=(pltpu.SemaphoreType.DMA((n,)), pltpu.SemaphoreType.DMA((n,)),
                   *[pltpu.HBM(a.shape, a.dtype) for a in list(full4s) + list(bufs)], _TOKEN),
        input_output_aliases={t: 2 + t for t in range(2 * n)}, compiler_params=_CP_SPLIT)(
            *[_hbm(a) for a in list(full4s) + list(bufs)], *after)
    return outs[0], outs[1], list(outs[2:2 + n]), list(outs[2 + n:2 + 2 * n]), outs[-1]


def _pair_wait(full4s, bufs, send_sems, recv_sems, after, tag):
    n = len(full4s)
    after = list(after) if isinstance(after, (list, tuple)) else [after]

    def body(*refs):
        full, buf, send_sems, recv_sems = refs[:n], refs[n:2 * n], refs[2 * n], refs[2 * n + 1]
        c = lax.axis_index("c")
        for t in range(n):
            cp = pltpu.make_async_remote_copy(src_ref=full[t].at[:, 1 - c], dst_ref=buf[t], send_sem=send_sems.at[t],
                                              recv_sem=recv_sems.at[t], device_id=_peer(0)[0], device_id_type=_MESH)
            cp.wait_send()
            cp.wait_recv()

    outs = pl.pallas_call(
        body, name=f"pair_wait_{tag}", in_specs=[_HBM] * (2 * n) + [_SEM, _SEM] + [_ANY] * len(after),
        out_specs=tuple([_HBM] * (2 * n)),
        out_shape=tuple(pltpu.HBM(a.shape, a.dtype) for a in list(full4s) + list(bufs)),
        input_output_aliases={t: t for t in range(2 * n)}, compiler_params=_CP_SPLIT)(
            *full4s, *bufs, send_sems, recv_sems, *after)
    return list(outs[:n]), list(outs[n:])


def _pair_sum(core, full4s, bufs):
    n = len(full4s)

    def body(core_ref, *refs):
        for t in range(n):
            refs[2 * n + t][...] = (refs[t][...].astype(f32) + refs[n + t][...].astype(f32)).astype(bf16)

    grid_spec = pltpu.PrefetchScalarGridSpec(
        num_scalar_prefetch=1, grid=(4,),
        in_specs=[pl.BlockSpec((None, None) + a.shape[2:], lambda j, core_ref: (j, core_ref[0], 0, 0)) for a in full4s]
        + [pl.BlockSpec((None,) + b.shape[1:], lambda j, core_ref: (j, 0, 0)) for b in bufs],
        out_specs=[pl.BlockSpec((None,) + b.shape[1:], lambda j, core_ref: (j, 0, 0)) for b in bufs])
    return pl.pallas_call(
        body, grid_spec=grid_spec, out_shape=[SDS(b.shape, bf16) for b in bufs],
        compiler_params=_CP, name="pair_sum")(core, *full4s, *bufs)


def _chip_start(sums, slots, after, tag):
    n = len(sums)
    after = list(after) if isinstance(after, (list, tuple)) else [after]

    def body(*refs):
        src, slot = refs[:n], refs[n:2 * n]
        send_sems, recv_sems, token = refs[2 * n + len(after)], refs[2 * n + len(after) + 1], refs[-1]
        my_chip = 2 * lax.axis_index("x") + lax.axis_index("y")
        for k, mask in enumerate((4, 2, 6)):
            dev, _ = _peer(mask)
            for t in range(n):
                pltpu.make_async_remote_copy(
                    src_ref=src[t].at[2 * dev[0] + dev[1]], dst_ref=slot[t].at[my_chip],
                    send_sem=send_sems.at[k * n + t], recv_sem=recv_sems.at[k * n + t],
                    device_id=dev, device_id_type=_MESH).start()
        token[...] = jnp.zeros_like(token)

    outs = pl.pallas_call(
        body, name=f"chip_start_{tag}", in_specs=[_HBM] * (2 * n) + [_ANY] * len(after),
        out_specs=(_SEM, _SEM, *[_HBM] * (2 * n), pl.BlockSpec(memory_space=pltpu.VMEM)),
        out_shape=(pltpu.SemaphoreType.DMA((3 * n,)), pltpu.SemaphoreType.DMA((3 * n,)),
                   *[pltpu.HBM(a.shape, a.dtype) for a in list(sums) + list(slots)], _TOKEN),
        input_output_aliases={t: 2 + t for t in range(2 * n)}, compiler_params=_CP_SPLIT)(
            *[_hbm(a) for a in list(sums) + list(slots)], *after)
    return outs[0], outs[1], list(outs[2:2 + n]), list(outs[2 + n:2 + 2 * n]), outs[-1]


def _chip_wait(sums, slots, send_sems, recv_sems, after, tag):
    n = len(sums)
    after = list(after) if isinstance(after, (list, tuple)) else [after]

    def body(*refs):
        src, slot, send_sems, recv_sems = refs[:n], refs[n:2 * n], refs[2 * n], refs[2 * n + 1]
        for k, mask in enumerate((4, 2, 6)):
            dev, _ = _peer(mask)
            chip = 2 * dev[0] + dev[1]
            for t in range(n):
                cp = pltpu.make_async_remote_copy(
                    src_ref=src[t].at[chip], dst_ref=slot[t].at[chip],
                    send_sem=send_sems.at[k * n + t], recv_sem=recv_sems.at[k * n + t],
                    device_id=_peer(0)[0], device_id_type=_MESH)
                cp.wait_send()
                cp.wait_recv()

    outs = pl.pallas_call(
        body, name=f"chip_wait_{tag}", in_specs=[_HBM] * (2 * n) + [_SEM, _SEM] + [_ANY] * len(after),
        out_specs=tuple([_HBM] * (2 * n)),
        out_shape=tuple(pltpu.HBM(a.shape, a.dtype) for a in list(sums) + list(slots)),
        input_output_aliases={t: t for t in range(2 * n)}, compiler_params=_CP_SPLIT)(
            *sums, *slots, send_sems, recv_sems, *after)
    return list(outs[:n]), list(outs[n:])


def _sum_slots(slots, rb):
    r = slots.shape[1]

    def body(s_ref, o_ref):
        acc = s_ref[0].astype(f32)
        for s in range(1, NDEV):
            acc = acc + s_ref[s].astype(f32)
        o_ref[...] = acc

    return pl.pallas_call(
        body, grid=(r // rb,),
        in_specs=[pl.BlockSpec((NDEV, rb, D), lambda i: (0, i, 0))],
        out_specs=pl.BlockSpec((rb, D), lambda i: (i, 0)),
        out_shape=SDS((r, D), f32), compiler_params=_CP, name="sum_slots")(slots)


def _adamw(w, g, m, v):
    shape = w.shape
    cols = shape[-1]
    rows = w.size // cols
    rb = rows
    for cand in (512, 256, 128, 64, 32, 16, 8):
        if rows % cand == 0 and rows > cand:
            rb = cand
            break

    def body(w_ref, g_ref, m_ref, v_ref, d_ref, mo_ref, vo_ref):
        d_ref[...], mo_ref[...], vo_ref[...] = _adamw_math(w_ref[...], g_ref[...], m_ref[...], v_ref[...])

    spec = pl.BlockSpec((rb, cols), lambda i: (i, 0))
    outs = pl.pallas_call(
        body, grid=(rows // rb,), in_specs=[spec] * 4, out_specs=[spec] * 3,
        out_shape=[SDS((rows, cols), f32)] * 3, compiler_params=_CP, name="adamw")(
            *(a.reshape(rows, cols) for a in (w, g, m, v)))
    return tuple(o.reshape(shape) for o in outs)


def _adamw_math(w, g, m, v):
    m = ADAM_B1 * m + (1.0 - ADAM_B1) * g
    v = ADAM_B2 * v + (1.0 - ADAM_B2) * (g * g)
    m_hat = m / (1.0 - ADAM_B1 ** ADAM_STEP)
    v_hat = v / (1.0 - ADAM_B2 ** ADAM_STEP)
    return -ADAM_LR * (m_hat / (jnp.sqrt(v_hat) + ADAM_EPS) + ADAM_WD * w), m, v


def _reduce_adamw(acc, me, full, slots, w, m, v, l):
    _, r, _ = w.shape
    ns = slots.shape[0]
    rb = r // 2 if r > 128 else r

    def body(me_ref, full_ref, slots_ref, w_ref, m_ref, v_ref, *refs):
        go_ref, d_ref, mo_ref, vo_ref = refs[-4:]
        own = full_ref[...].astype(f32)
        g = None
        for s in range(ns):
            part = jnp.where(me_ref[0] == s, own, slots_ref[s].astype(f32))
            g = part if g is None else g + part
        go_ref[...] = g
        d_ref[...], mo_ref[...], vo_ref[...] = _adamw_math(w_ref[...], g, m_ref[...], v_ref[...])

    steps = r // rb
    lay = pl.BlockSpec((None, rb, D), lambda i, me_ref: (l, i, 0))
    n_acc = 0 if acc is None else 4
    grid_spec = pltpu.PrefetchScalarGridSpec(
        num_scalar_prefetch=1, grid=(steps,),
        in_specs=[pl.BlockSpec((rb, D), lambda i, me_ref: (me_ref[0] * steps + i, 0)),
                  pl.BlockSpec((ns, rb, D), lambda i, me_ref: (0, i, 0)), lay, lay, lay] + [_ANY] * n_acc,
        out_specs=[lay] * 4)
    outs = pl.pallas_call(
        body, grid_spec=grid_spec, out_shape=[SDS(w.shape, f32)] * 4,
        input_output_aliases={6 + j: j for j in range(n_acc)},
        compiler_params=_CP, name="reduce_adamw")(me, full, slots, w, m, v, *(() if acc is None else acc))
    return tuple(outs)


_BIG = ("ffn1_w_gate", "ffn1_w_up", "ffn1_w_down", "w_in", "w_out", "ffn2_w_gate", "ffn2_w_up", "ffn2_w_down")
_TRANSPOSED = ("ffn1_w_gate", "ffn1_w_up", "w_in", "ffn2_w_gate", "ffn2_w_up")

def _block_diag(pool_w):
    out = jnp.zeros((L, PW, PW), pool_w.dtype)
    for gi in range(4):
        out = out.at[:, 64 * gi:64 * (gi + 1), 64 * gi:64 * (gi + 1)].set(pool_w[:, gi])
    return out


def kernel(x, positions, ffn1_norm, ffn1_w_gate, ffn1_w_up, ffn1_w_down, mix_norm, w_in, pool_w, pool_scale, w_out, ffn2_norm, ffn2_w_gate, ffn2_w_up, ffn2_w_down, final_norm, loss_target, m_ffn1_norm, m_ffn1_w_gate, m_ffn1_w_up, m_ffn1_w_down, m_mix_norm, m_w_in, m_pool_w, m_pool_scale, m_w_out, m_ffn2_norm, m_ffn2_w_gate, m_ffn2_w_up, m_ffn2_w_down, m_final_norm, v_ffn1_norm, v_ffn1_w_gate, v_ffn1_w_up, v_ffn1_w_down, v_mix_norm, v_w_in, v_pool_w, v_pool_scale, v_w_out, v_ffn2_norm, v_ffn2_w_gate, v_ffn2_w_up, v_ffn2_w_down, v_final_norm):
    weights = dict(ffn1_norm=ffn1_norm, ffn1_w_gate=ffn1_w_gate, ffn1_w_up=ffn1_w_up, ffn1_w_down=ffn1_w_down,
                   mix_norm=mix_norm, w_in=w_in, pool_w=pool_w, pool_scale=pool_scale, w_out=w_out,
                   ffn2_norm=ffn2_norm, ffn2_w_gate=ffn2_w_gate, ffn2_w_up=ffn2_w_up, ffn2_w_down=ffn2_w_down,
                   final_norm=final_norm)
    moms = dict(ffn1_norm=m_ffn1_norm, ffn1_w_gate=m_ffn1_w_gate, ffn1_w_up=m_ffn1_w_up, ffn1_w_down=m_ffn1_w_down,
                mix_norm=m_mix_norm, w_in=m_w_in, pool_w=m_pool_w, pool_scale=m_pool_scale, w_out=m_w_out,
                ffn2_norm=m_ffn2_norm, ffn2_w_gate=m_ffn2_w_gate, ffn2_w_up=m_ffn2_w_up, ffn2_w_down=m_ffn2_w_down,
                final_norm=m_final_norm)
    vels = dict(ffn1_norm=v_ffn1_norm, ffn1_w_gate=v_ffn1_w_gate, ffn1_w_up=v_ffn1_w_up, ffn1_w_down=v_ffn1_w_down,
                mix_norm=v_mix_norm, w_in=v_w_in, pool_w=v_pool_w, pool_scale=v_pool_scale, w_out=v_w_out,
                ffn2_norm=v_ffn2_norm, ffn2_w_gate=v_ffn2_w_gate, ffn2_w_up=v_ffn2_w_up, ffn2_w_down=v_ffn2_w_down,
                final_norm=v_final_norm)
    names = list(weights)

    me_idx = 4 * lax.axis_index("x") + 2 * lax.axis_index("y") + lax.axis_index("c")
    me_arr = me_idx.reshape(1).astype(jnp.int32)

    tr = lambda w: jnp.swapaxes(w, 1, 2).astype(bf16)
    shards = [tr(weights[nm]) if nm in _TRANSPOSED else weights[nm].astype(bf16) for nm in _BIG]

    def landing_zones(l, which):
        return _place_own(me_arr, [shards[t] for t in which], l)

    g_ffn1 = [ffn1_norm[l].reshape(1, D) for l in range(L)]
    g_mix = [mix_norm[l].reshape(1, D) for l in range(L)]
    g_ffn2 = [ffn2_norm[l].reshape(1, D) for l in range(L)]
    wbd_all = _block_diag(pool_w).astype(bf16)
    wbd = [wbd_all[l] for l in range(L)]
    pscale = [pool_scale[l].reshape(1, PW) for l in range(L)]
    tabs = _rope_tables(positions)
    flat = lambda a: a.reshape(S, a.shape[-1])
    r4 = lambda a: a.reshape(4, S // 4, a.shape[-1])
    r16 = lambda a: a.reshape(16, S // 16, a.shape[-1])

    first, rest, whole = (0, 1, 2, 3), (4, 5, 6, 7), tuple(range(8))

    def ag_begin(l, which, after):
        tag = f"{l}{'' if which == whole else 'r'}"
        send_sems, recv_sems, zones, token = _ag_start(landing_zones(l, which), after, tag)
        return dict(tag=tag, zones=zones, s=send_sems, r=recv_sems), token

    def ag_second(ch, after):
        ch["ps"], ch["pr"], ch["zones"], token = _ag_pass(ch["zones"], ch["r"], after, ch["tag"])
        return token

    def ag_third(ch, after):
        ch["qs"], ch["qr"], ch["zones"], token = _ag_last(ch["zones"], ch["pr"], after, ch["tag"])
        return token

    def ag_end(ch, after):
        return _ag_wait(ch["zones"], ch["s"], ch["r"], ch["ps"], ch["pr"], ch["qs"], ch["qr"], after, ch["tag"])

    head = _all_gather(landing_zones(0, first))
    ch_rest, tok_rest = ag_begin(0, rest, head[0])
    chains = {}
    chains[1], tok_next = ag_begin(1, whole, head[0])
    gathered = [None] * L
    xs = x.reshape(S, D)
    saved = []
    for l in range(L):
        first_after, second_after = (), ()
        if l == 0:
            gt1, ut1, dn1, wint = head
            first_after = (tok_rest, tok_next)
        else:
            gt1, ut1, dn1, wint, wout, gt2, ut2, dn2 = gathered[l]
        x0 = xs
        x1, gate1, up1 = _ffn_fwd(x0, g_ffn1[l], gt1, ut1, dn1, after=first_after)
        hmix, vp, q1, k1, v1, q4, k4, v4, q16, k16, v16 = _mix_in_fwd(x1, g_mix[l], wint, tabs)
        q4, k4, v4, q16, k16, v16 = map(flat, (q4, k4, v4, q16, k16, v16))
        ypool, diff = _pool_fwd(vp, wbd[l], pscale[l])
        after_attn = None
        if l == 0:
            after_attn = ag_second(ch_rest, [ypool, q16])
        o1, l1 = _attn_fwd(q1, k1, v1, S, after=after_attn)
        o4, l4 = _attn_fwd(q4, k4, v4, S // 4, after=after_attn)
        o16, l16 = _attn_fwd(q16, k16, v16, S // 16, after=after_attn)
        if l == 0:
            token = ag_third(ch_rest, [o1, o4, o16])
            wout, gt2, ut2, dn2 = ag_end(ch_rest, token)
            gathered[0] = list(head) + [wout, gt2, ut2, dn2]
        elif l + 1 < L:
            second_after = (ag_second(chains[l + 1], [o1, o4, o16]),)
        x2, mixed, o, lse1, lse4, lse16 = _mix_out_fwd(x1, ypool, o1, l1, r4(o4), r4(l4), r16(o16), r16(l16), wout)
        if l == 0:
            second_after = (ag_second(chains[1], x2),)
        x3, gate2, up2 = _ffn_fwd(x2, g_ffn2[l], gt2, ut2, dn2, after=second_after)
        if l + 1 < L:
            token = ag_third(chains[l + 1], x3)
            if l + 2 < L:
                chains[l + 2], token = ag_begin(l + 2, whole, token)
            gathered[l + 1] = ag_end(chains[l + 1], token)
        saved.append(dict(x0=x0, x1=x1, x2=x2, gate1=gate1, up1=up1, gate2=gate2, up2=up2, hmix=hmix, diff=diff,
                          qkv=((q1, k1, v1), (q4, k4, v4), (q16, k16, v16)), mixed=mixed, o=o,
                          lse=(lse1, flat(lse4), flat(lse16))))
        xs = x3

    dx, loss_part, d_final = _loss_head(xs, final_norm.reshape(1, D), loss_target.reshape(S, D))

    d_norm = {nm: [None] * L for nm in ("ffn1_norm", "mix_norm", "ffn2_norm")}
    d_poolw, d_pscale = [None] * L, [None] * L
    group_a = ("ffn2_w_gate", "ffn2_w_up", "ffn2_w_down", "w_out")
    group_b = ("ffn1_w_gate", "ffn1_w_up", "ffn1_w_down", "w_in")
    acc = {}

    as_rows = lambda a, nm: jnp.swapaxes(a, 1, 2) if nm in _TRANSPOSED else a
    w_rows = {nm: as_rows(weights[nm], nm) for nm in _BIG}
    m_rows = {nm: as_rows(moms[nm], nm) for nm in _BIG}
    v_rows = {nm: as_rows(vels[nm], nm) for nm in _BIG}

    def exchange(full, group, after, tag):
        srcs = [full[nm] for nm in group]
        slots = [lax.empty((NDEV, g.shape[0] // NDEV, D), bf16) for g in srcs]
        ssem, rsem, srcs, slots, token = _rs_start(srcs, slots, after, tag)
        return (srcs, slots, ssem, rsem, tag), token

    def update(l, group, flight, after):
        srcs, slots, ssem, rsem, tag = flight
        srcs, slots = _rs_wait(srcs, slots, ssem, rsem, after, tag)
        for nm, full_g, slots_g in zip(group, srcs, slots):
            acc[nm] = _reduce_adamw(acc.get(nm), me_arr, full_g, slots_g, w_rows[nm], m_rows[nm], v_rows[nm], l)
        return [acc[nm][0] for nm in group], slots

    core_arr = lax.axis_index("c").reshape(1).astype(jnp.int32)
    chip_arr = (2 * lax.axis_index("x") + lax.axis_index("y")).reshape(1).astype(jnp.int32)

    def exchange_cores(full, group, after, tag):
        full4s = [full[nm].reshape(4, 2, full[nm].shape[0] // NDEV, D) for nm in group]
        bufs = [lax.empty((4,) + a.shape[2:], bf16) for a in full4s]
        ssem, rsem, full4s, bufs, token = _pair_start(full4s, bufs, after, tag)
        return (full4s, bufs, ssem, rsem, tag), token

    def exchange_chips(flight, after):
        full4s, bufs, ssem, rsem, tag = flight
        full4s, bufs = _pair_wait(full4s, bufs, ssem, rsem, after, tag)
        sums = _pair_sum(core_arr, full4s, bufs)
        slots = [lax.empty(a.shape, bf16) for a in sums]
        ssem, rsem, sums, slots, token = _chip_start(sums, slots, bufs[0], tag)
        return (sums, slots, ssem, rsem, tag), token

    def update_chips(l, group, flight, after):
        sums, slots, ssem, rsem, tag = flight
        sums, slots = _chip_wait(sums, slots, ssem, rsem, after, tag)
        for nm, sums_g, slots_g in zip(group, sums, slots):
            own = sums_g.reshape(4 * sums_g.shape[1], D)
            acc[nm] = _reduce_adamw(acc.get(nm), chip_arr, own, slots_g, w_rows[nm], m_rows[nm], v_rows[nm], l)
        return [acc[nm][0] for nm in group]

    flights = {}
    token_b = None
    for l in reversed(range(L)):
        sv = saved[l]
        gt1, ut1, dn1, wint, wout, gt2, ut2, dn2 = gathered[l]
        full = {}
        dx, dgate, dup, h, dy, d_norm["ffn2_norm"][l] = _ffn_bwd_d(
            sv["x2"], g_ffn2[l], sv["gate2"], sv["up2"], dx, gt2, ut2, dn2, after=() if token_b is None else (token_b,))
        full["ffn2_w_gate"], full["ffn2_w_up"], full["ffn2_w_down"] = _ffn_bwd_w(h, dy, sv["gate2"], sv["up2"], dgate, dup)

        dxb, dyp, do1, do4, do16, dl1, dl4, dl16 = _mix_out_bwd(dx, sv["o"], wout)
        full["w_out"] = _wgrad(sv["mixed"], dxb)
        flights[l, "a"], token_a = (exchange_cores if l == 0 else exchange)(full, group_a, dxb, f"a{l}")
        dvp, d_poolw[l], d_pscale[l] = _pool_bwd(dyp, sv["diff"], wbd[l], pscale[l], after=(token_a,))
        dos, dls = (do1, flat(do4), flat(do16)), (dl1, flat(dl4), flat(dl16))
        dqkv = []
        for b, lc in enumerate((S, S // 4, S // 16)):
            qb, kb, vb = sv["qkv"][b]
            dqkv.append(_attn_bwd(qb, kb, vb, dos[b], sv["lse"][b], dls[b], lc))
        d4 = tuple(r4(a) for a in dqkv[1])
        d16 = tuple(r16(a) for a in dqkv[2])
        mix_after = ()
        if l == 0:
            flights[0, "a"], token_a = exchange_chips(flights[0, "a"], [dqkv[0][0], dqkv[1][0], dqkv[2][0]])
            mix_after = (token_a,)
        dx, dproj, d_norm["mix_norm"][l] = _mix_in_bwd(dx, sv["x1"], g_mix[l], wint, tabs, dvp, dqkv[0], d4, d16,
                                                       after=mix_after)
        full["w_in"] = _wgrad(dproj, sv["hmix"])

        dx, dgate, dup, h, dy, d_norm["ffn1_norm"][l] = _ffn_bwd_d(sv["x0"], g_ffn1[l], sv["gate1"], sv["up1"], dx, gt1, ut1, dn1)
        full["ffn1_w_gate"], full["ffn1_w_up"], full["ffn1_w_down"] = _ffn_bwd_w(h, dy, sv["gate1"], sv["up1"], dgate, dup)

        after = dx
        if l + 1 < L and l + 1 >= 2:
            after, _ = update(l + 1, group_a, flights.pop((l + 1, "a")), after)
            after, _ = update(l + 1, group_b, flights.pop((l + 1, "b")), after)
        if l > 0:
            flights[l, "b"], token_b = exchange(full, group_b, after, f"b{l}")

    flights[0, "b"], token_b = exchange_cores(full, group_b, dx, "b0")
    pad8 = lambda a: jnp.pad(a, ((0, 8 - a.shape[0]), (0, 0)))
    misc = jnp.concatenate([d_final, jnp.concatenate(d_pscale, axis=1), loss_part], axis=0)
    small = jnp.concatenate(
        [pad8(jnp.concatenate(d_norm[nm], axis=0)) for nm in ("ffn1_norm", "mix_norm", "ffn2_norm")]
        + [pad8(misc), jnp.stack(d_poolw).reshape(L * 16, D)], axis=0)
    small_slots = lax.dynamic_update_slice(lax.empty((NDEV, SMALL_ROWS, D), f32), small[None], (me_idx, 0, 0))
    pack_sems = _rs_start([small], [small_slots], token_b, "pack")
    flights[0, "b"], token_b = exchange_chips(flights[0, "b"], pack_sems[-1])

    after = token_b
    for key in [(1, "a"), (1, "b")]:
        after, _ = update(key[0], group_a if key[1] == "a" else group_b, flights.pop(key), after)
    _, pack_slots = _rs_wait(pack_sems[2], pack_sems[3], pack_sems[0], pack_sems[1], after, "pack")
    sm = _sum_slots(pack_slots[0], SMALL_ROWS)
    grads = {}
    grads["ffn1_norm"], grads["mix_norm"], grads["ffn2_norm"] = sm[0:L], sm[8:8 + L], sm[16:16 + L]
    grads["final_norm"] = sm[24]
    grads["pool_scale"] = sm[25].reshape(L, PW)
    grads["pool_w"] = sm[32:32 + L * 16].reshape(L, 4, 64, 64)
    loss = sm[26, 0]
    upd = {nm: _adamw(weights[nm], grads[nm], moms[nm], vels[nm]) for nm in names if nm not in _BIG}
    after = update_chips(0, group_a, flights.pop((0, "a")), [upd[nm][0] for nm in upd])
    update_chips(0, group_b, flights.pop((0, "b")), after)
    for nm in _BIG:
        grads[nm], upd[nm] = as_rows(acc[nm][0], nm), tuple(as_rows(a, nm) for a in acc[nm][1:])
    return (loss, dx.reshape(1, S, D), *[grads[nm] for nm in names], *[upd[nm][0] for nm in names],
            *[upd[nm][1] for nm in names], *[upd[nm][2] for nm in names])
```

```python
import jax
import jax.numpy as jnp
from jax import lax
from jax.experimental import pallas as pl
from jax.experimental.pallas import tpu as pltpu

f32 = jnp.float32
bf16 = jnp.bfloat16
SDS = jax.ShapeDtypeStruct

D = 1024
S = 2048
F = 2816
L = 4
PW = 256
AW = 768
PROJ = PW + 3 * AW
NDEV = 8
TM = 256
QB = 128
HALF = 64
NG = AW // 128
NORM_EPS = 1e-6
MASK_VALUE = -1e30
ROPE_THETA = 500000.0
ADAM_LR, ADAM_B1, ADAM_B2, ADAM_EPS, ADAM_WD, ADAM_STEP = 0.001, 0.9, 0.999, 1e-08, 0.01, 10
POOL_WINDOWS = (2, 4, 8, 16)
PAD = 8
SMALL_ROWS = 96
VMEM_LIMIT = 56 * 1024 * 1024

_CP = pltpu.CompilerParams(vmem_limit_bytes=VMEM_LIMIT)
_ANY = pl.BlockSpec(memory_space=pl.ANY)
_HBM = pl.BlockSpec(memory_space=pltpu.HBM)
_SEM = pl.BlockSpec(memory_space=pltpu.SEMAPHORE)
_MESH = pl.DeviceIdType.MESH
_CP_SPLIT = pltpu.CompilerParams(has_side_effects=pltpu.SideEffectType.DATAFLOW_SIDE_EFFECTING)


def _dot_nn(a, b):
    return lax.dot_general(a, b, (((1,), (0,)), ((), ())), preferred_element_type=f32)


def _dot_nt(a, b):
    return lax.dot_general(a, b, (((1,), (1,)), ((), ())), preferred_element_type=f32)


def _dot_tn(a, b):
    return lax.dot_general(a, b, (((0,), (0,)), ((), ())), preferred_element_type=f32)


def _rms(x, g):
    r = lax.rsqrt(jnp.mean(x * x, axis=-1, keepdims=True) + NORM_EPS)
    xh = x * r
    return r, xh, xh * g


def _rms_bwd(dh, r, xh, g):
    dxh = dh * g
    return r * (dxh - xh * jnp.mean(dxh * xh, axis=-1, keepdims=True))


def _tile(cols, rows=TM):
    return pl.BlockSpec((rows, cols), lambda i: (i, 0))


def _const(shape):
    return pl.BlockSpec(shape, lambda i: (0,) * len(shape))


def _layer(rows, cols):
    return pl.BlockSpec((rows, cols), lambda i: (0, 0), pipeline_mode=pl.Buffered(1))


def _p4(cols=AW):
    return pl.BlockSpec((4, TM // 4, cols), lambda i: (0, i, 0))


def _p16(cols=AW):
    return pl.BlockSpec((16, TM // 16, cols), lambda i: (0, i, 0))


def _cols(j):
    return slice(128 * j, 128 * (j + 1))


def _follow(body, n_in, after):
    k = len(after)
    return body if k == 0 else (lambda *refs: body(*refs[:n_in], *refs[n_in + k:]))


def _ffn_fwd(x, g, gt, ut, dn, after=()):
    def body(x_ref, g_ref, gt_ref, ut_ref, dn_ref, xo_ref, gate_ref, up_ref):
        x = x_ref[...]
        _, _, hn = _rms(x, g_ref[...])
        h = hn.astype(bf16)
        gate = _dot_nt(h, gt_ref[...])
        up = _dot_nt(h, ut_ref[...])
        gate_ref[...] = gate.astype(bf16)
        up_ref[...] = up.astype(bf16)
        a = (gate * jax.nn.sigmoid(gate) * up).astype(bf16)
        xo_ref[...] = x + 0.5 * _dot_nn(a, dn_ref[...])

    rows = 2 * TM
    return pl.pallas_call(
        _follow(body, 5, after), grid=(S // rows,),
        in_specs=[_tile(D, rows), _layer(1, D), _layer(F, D), _layer(F, D), _layer(F, D)] + [_ANY] * len(after),
        out_specs=[_tile(D, rows), _tile(F, rows), _tile(F, rows)],
        out_shape=[SDS((S, D), f32), SDS((S, F), bf16), SDS((S, F), bf16)],
        compiler_params=_CP, name="ffn_fwd")(x, g, gt, ut, dn, *after)


def _ffn_bwd_d(x, g, gate, up, dxo, gt, ut, dn, after=()):
    def body(x_ref, g_ref, gate_ref, up_ref, dxo_ref, gt_ref, ut_ref, dn_ref,
             dx_ref, dgate_ref, dup_ref, h_ref, dy_ref, dg_ref):
        x = x_ref[...]
        g = g_ref[...]
        r, xh, hn = _rms(x, g)
        h_ref[...] = hn.astype(bf16)
        dxo = dxo_ref[...]
        dy = (0.5 * dxo).astype(bf16)
        dy_ref[...] = dy
        da = _dot_nt(dy, dn_ref[...])
        gate = gate_ref[...].astype(f32)
        up = up_ref[...].astype(f32)
        sg = jax.nn.sigmoid(gate)
        dgate = (da * up * (sg * (1.0 + gate * (1.0 - sg)))).astype(bf16)
        dup = (da * (gate * sg)).astype(bf16)
        dgate_ref[...] = dgate
        dup_ref[...] = dup
        dh = _dot_nn(dgate, gt_ref[...]) + _dot_nn(dup, ut_ref[...])

        @pl.when(pl.program_id(0) == 0)
        def _():
            dg_ref[...] = jnp.zeros_like(dg_ref)

        dg_ref[...] += jnp.sum(dh * xh, axis=0, keepdims=True)
        dx_ref[...] = dxo + _rms_bwd(dh, r, xh, g)

    return pl.pallas_call(
        _follow(body, 8, after), grid=(S // TM,),
        in_specs=[_tile(D), _layer(1, D), _tile(F), _tile(F), _tile(D),
                  _layer(F, D), _layer(F, D), _layer(F, D)] + [_ANY] * len(after),
        out_specs=[_tile(D), _tile(F), _tile(F), _tile(D), _tile(D), _const((1, D))],
        out_shape=[SDS((S, D), f32), SDS((S, F), bf16), SDS((S, F), bf16), SDS((S, D), bf16),
                   SDS((S, D), bf16), SDS((1, D), f32)],
        compiler_params=_CP, name="ffn_bwd_d")(x, g, gate, up, dxo, gt, ut, dn, *after)


def _ffn_bwd_w(h, dy, gate, up, dgate, dup):
    fc = 256

    def body(h_ref, dy_ref, gate_ref, up_ref, dgate_ref, dup_ref, dgt_ref, dut_ref, ddn_ref):
        gate = gate_ref[...].astype(f32)
        a = (gate * jax.nn.sigmoid(gate) * up_ref[...].astype(f32)).astype(bf16)
        ddn_ref[...] = _dot_tn(a, dy_ref[...]).astype(bf16)
        h = h_ref[...]
        dgt_ref[...] = _dot_tn(dgate_ref[...], h).astype(bf16)
        dut_ref[...] = _dot_tn(dup_ref[...], h).astype(bf16)

    col = pl.BlockSpec((S, fc), lambda j: (0, j))
    row = pl.BlockSpec((fc, D), lambda j: (j, 0))
    full = pl.BlockSpec((S, D), lambda j: (0, 0))
    return pl.pallas_call(
        body, grid=(F // fc,),
        in_specs=[full, full, col, col, col, col],
        out_specs=[row, row, row],
        out_shape=[SDS((F, D), bf16)] * 3,
        compiler_params=_CP, name="ffn_bwd_w")(h, dy, gate, up, dgate, dup)


def _wgrad(a, b):
    m, n = a.shape[1], b.shape[1]
    mc = 2 * TM

    def body(a_ref, b_ref, o_ref):
        o_ref[...] = _dot_tn(a_ref[...], b_ref[...]).astype(bf16)

    return pl.pallas_call(
        body, grid=(m // mc,),
        in_specs=[pl.BlockSpec((S, mc), lambda j: (0, j)), pl.BlockSpec((S, n), lambda j: (0, 0))],
        out_specs=pl.BlockSpec((mc, n), lambda j: (j, 0)),
        out_shape=SDS((m, n), bf16),
        compiler_params=_CP, name="wgrad")(a, b)


def _rope(t, c, sn, sp):
    return t * c + pltpu.roll(t, 120, 1) * sn + pltpu.roll(t, 8, 1) * sp


def _rope_bwd(d, c, sn, sp):
    return d * c + pltpu.roll(d * sn, 8, 1) + pltpu.roll(d * sp, 120, 1)


def _rope_tables(positions):
    inv_freq = ROPE_THETA ** (-jnp.arange(0, 16, 2, dtype=f32) / 16)
    ang = positions.reshape(S, 1).astype(f32) * inv_freq
    cos, sin = jnp.cos(ang), jnp.sin(ang)
    one = jnp.ones((S, 48), f32)
    zero8 = jnp.zeros((S, 8), f32)
    zero48 = jnp.zeros((S, 48), f32)
    c = jnp.concatenate([cos, cos, one], axis=1)
    sn = jnp.concatenate([-sin, zero8, zero48], axis=1)
    sp = jnp.concatenate([zero8, sin, zero48], axis=1)
    return tuple(jnp.concatenate([t, t], axis=1) for t in (c, sn, sp))


def _dilation_perm(n, back=False):
    per = TM // n
    i = lax.broadcasted_iota(jnp.int32, (TM, TM), 1 if back else 0)
    j = lax.broadcasted_iota(jnp.int32, (TM, TM), 0 if back else 1)
    return jnp.where(j == n * (i % per) + i // per, 1.0, 0.0).astype(bf16)


def _mix_in_fwd(x, g, wint, tabs):
    def body(x_ref, g_ref, w_ref, c_ref, sn_ref, sp_ref,
             h_ref, vp_ref, q1, k1, v1, q4, k4, v4, q16, k16, v16):
        _, _, hn = _rms(x_ref[...], g_ref[...])
        h = hn.astype(bf16)
        h_ref[...] = h
        proj = _dot_nt(h, w_ref[...])
        vp_ref[...] = proj[:, :PW]
        c, sn, sp = c_ref[...], sn_ref[...], sp_ref[...]
        perm4, perm16 = _dilation_perm(4), _dilation_perm(16)
        for kind, (o1, o4, o16) in enumerate(((q1, q4, q16), (k1, k4, k16), (v1, v4, v16))):
            for j in range(NG):
                t = proj[:, PW + kind * AW + 128 * j: PW + kind * AW + 128 * (j + 1)]
                if kind == 0:
                    t = _rope(t, c, sn, sp) * 0.125
                elif kind == 1:
                    t = _rope(t, c, sn, sp)
                o1[:, _cols(j)] = t.astype(bf16)
            nat = o1[...]
            o4[...] = _dot_nn(perm4, nat).astype(bf16).reshape(4, TM // 4, AW)
            o16[...] = _dot_nn(perm16, nat).astype(bf16).reshape(16, TM // 16, AW)

    nat, d4, d16 = SDS((S, AW), bf16), SDS((4, S // 4, AW), bf16), SDS((16, S // 16, AW), bf16)
    return pl.pallas_call(
        body, grid=(S // TM,),
        in_specs=[_tile(D), _layer(1, D), _layer(PROJ, D), _tile(128), _tile(128), _tile(128)],
        out_specs=[_tile(D), _tile(PW)] + [_tile(AW)] * 3 + [_p4()] * 3 + [_p16()] * 3,
        out_shape=[SDS((S, D), bf16), SDS((S, PW), f32)] + [nat] * 3 + [d4] * 3 + [d16] * 3,
        compiler_params=_CP, name="mix_in_fwd")(x, g, wint, *tabs)


def _mix_in_bwd(dxo, x, g, wint, tabs, dvp, d1, d4, d16, after=()):
    def body(dxo_ref, x_ref, g_ref, w_ref, c_ref, sn_ref, sp_ref, dvp_ref,
             dq1, dk1, dv1, dq4, dk4, dv4, dq16, dk16, dv16,
             dx_ref, dproj_ref, dg_ref):
        c, sn, sp = c_ref[...], sn_ref[...], sp_ref[...]
        dproj_ref[:, :PW] = dvp_ref[...].astype(bf16)
        back4, back16 = _dilation_perm(4, True), _dilation_perm(16, True)
        for kind, (a1, a4, a16) in enumerate(((dq1, dq4, dq16), (dk1, dk4, dk16), (dv1, dv4, dv16))):
            n4 = _dot_nn(back4, a4[...].reshape(TM, AW))
            n16 = _dot_nn(back16, a16[...].reshape(TM, AW))
            for j in range(NG):
                t = a1[:, _cols(j)].astype(f32) + n4[:, _cols(j)] + n16[:, _cols(j)]
                if kind == 0:
                    t = _rope_bwd(t * 0.125, c, sn, sp)
                elif kind == 1:
                    t = _rope_bwd(t, c, sn, sp)
                dproj_ref[:, PW + kind * AW + 128 * j: PW + kind * AW + 128 * (j + 1)] = t.astype(bf16)
        g = g_ref[...]
        r_, xh, _ = _rms(x_ref[...], g)
        dh = _dot_nn(dproj_ref[...], w_ref[...])

        @pl.when(pl.program_id(0) == 0)
        def _():
            dg_ref[...] = jnp.zeros_like(dg_ref)

        dg_ref[...] += jnp.sum(dh * xh, axis=0, keepdims=True)
        dx_ref[...] = dxo_ref[...] + _rms_bwd(dh, r_, xh, g)

    return pl.pallas_call(
        _follow(body, 17, after), grid=(S // TM,),
        in_specs=[_tile(D), _tile(D), _layer(1, D), _layer(PROJ, D), _tile(128), _tile(128), _tile(128),
                  _tile(PW)] + [_tile(AW)] * 3 + [_p4()] * 3 + [_p16()] * 3 + [_ANY] * len(after),
        out_specs=[_tile(D), _tile(PROJ), _const((1, D))],
        out_shape=[SDS((S, D), f32), SDS((S, PROJ), bf16), SDS((1, D), f32)],
        compiler_params=_CP, name="mix_in_bwd")(dxo, x, g, wint, *tabs, dvp, *d1, *d4, *d16, *after)


def _pool_sums(pad_ref, base, rows, adjoint):
    lane_group = lax.broadcasted_iota(jnp.int32, (rows, PW), 1) // 64
    sign = -1 if adjoint else 1

    def sh(o):
        return pad_ref[pl.ds(PAD + base + sign * o, rows), :]

    out = None
    acc = None
    lo, hi = 0, 0
    for gi, w in enumerate(POOL_WINDOWS):
        for o in list(range(-(w // 2), lo)) + list(range(hi, w - w // 2)):
            acc = sh(o) if acc is None else acc + sh(o)
        lo, hi = -(w // 2), w - w // 2
        out = acc if out is None else jnp.where(lane_group >= gi, acc, out)
    return out


def _pool_counts(base, rows):
    pos = base + lax.broadcasted_iota(jnp.int32, (rows, PW), 0)
    lane_group = lax.broadcasted_iota(jnp.int32, (rows, PW), 1) // 64
    cnt = None
    for gi, w in enumerate(POOL_WINDOWS):
        lo = jnp.maximum(pos - w // 2, 0)
        hi = jnp.minimum(pos + w - 1 - w // 2, S - 1)
        c = (hi - lo + 1).astype(f32)
        cnt = c if cnt is None else jnp.where(lane_group >= gi, c, cnt)
    return cnt


def _pool_fwd(vp, wbd, scale):
    ch = 256

    def body(vp_ref, w_ref, sc_ref, y_ref, diff_ref, pad):
        pad[pl.ds(0, PAD), :] = jnp.zeros((PAD, PW), f32)
        pad[pl.ds(PAD + S, PAD), :] = jnp.zeros((PAD, PW), f32)
        pad[pl.ds(PAD, S), :] = vp_ref[...]
        for b in range(S // ch):
            base = b * ch
            pooled = _pool_sums(pad, base, ch, False) / _pool_counts(base, ch)
            diff = (pooled - vp_ref[pl.ds(base, ch), :]).astype(bf16)
            diff_ref[pl.ds(base, ch), :] = diff
            y_ref[pl.ds(base, ch), :] = _dot_nn(diff, w_ref[...]) * sc_ref[...]

    whole = lambda shape: pl.BlockSpec(shape, lambda i: (0,) * len(shape))
    return pl.pallas_call(
        body, grid=(1,),
        in_specs=[whole((S, PW)), whole((PW, PW)), whole((1, PW))],
        out_specs=[whole((S, PW)), whole((S, PW))],
        out_shape=[SDS((S, PW), f32), SDS((S, PW), bf16)],
        scratch_shapes=[pltpu.VMEM((S + 2 * PAD, PW), f32)],
        compiler_params=_CP, name="pool_fwd")(vp, wbd, scale)


def _pool_bwd(dy, diff, wbd, scale, after=()):
    ch = 256

    def body(dy_ref, diff_ref, w_ref, sc_ref, dvp_ref, dw_ref, dsc_ref, pad):
        pad[pl.ds(0, PAD), :] = jnp.zeros((PAD, PW), f32)
        pad[pl.ds(PAD + S, PAD), :] = jnp.zeros((PAD, PW), f32)
        dw = jnp.zeros((PW, PW), f32)
        dsc = jnp.zeros((1, PW), f32)
        for b in range(S // ch):
            base = b * ch
            dy = dy_ref[pl.ds(base, ch), :]
            diff = diff_ref[pl.ds(base, ch), :]
            dsc = dsc + jnp.sum(dy * _dot_nn(diff, w_ref[...]), axis=0, keepdims=True)
            dz = (dy * sc_ref[...]).astype(bf16)
            dw = dw + _dot_tn(diff, dz)
            ddiff = _dot_nt(dz, w_ref[...])
            dvp_ref[pl.ds(base, ch), :] = -ddiff
            pad[pl.ds(PAD + base, ch), :] = ddiff / _pool_counts(base, ch)
        for gi in range(4):
            dw_ref[gi] = dw[64 * gi:64 * (gi + 1), 64 * gi:64 * (gi + 1)]
        dsc_ref[...] = dsc
        for b in range(S // ch):
            base = b * ch
            dvp_ref[pl.ds(base, ch), :] += _pool_sums(pad, base, ch, True)

    whole = lambda shape: pl.BlockSpec(shape, lambda i: (0,) * len(shape))
    return pl.pallas_call(
        _follow(body, 4, after), grid=(1,),
        in_specs=[whole((S, PW)), whole((S, PW)), whole((PW, PW)), whole((1, PW))] + [_ANY] * len(after),
        out_specs=[whole((S, PW)), whole((4, 64, 64)), whole((1, PW))],
        out_shape=[SDS((S, PW), f32), SDS((4, 64, 64), f32), SDS((1, PW), f32)],
        scratch_shapes=[pltpu.VMEM((S + 2 * PAD, PW), f32)],
        compiler_params=_CP, name="pool_bwd")(dy, diff, wbd, scale, *after)


def _attn_blocks(lc):
    bpc = lc // QB
    kw = min(2 * QB, lc)
    blocks = []
    for b in range(S // QB):
        t0 = (b % bpc) * QB
        ks_in = min(max(t0 - HALF, 0), lc - kw)
        blocks.append((b * QB, (b // bpc) * lc + ks_in, t0 - ks_in))
    return kw, blocks


def _attn_bias(bias_ref, kw, shifts):
    r = lax.broadcasted_iota(jnp.int32, (2 * QB, kw), 0) % QB
    c = lax.broadcasted_iota(jnp.int32, (2 * QB, kw), 1)
    for i, shift in enumerate(shifts):
        bias_ref[i] = jnp.where(jnp.abs(r + shift - c) <= HALF, 0.0, MASK_VALUE).astype(f32)


def _head_put(stats, pair, v0, v1, lane):
    return jnp.where(lane == 2 * pair, v0, jnp.where(lane == 2 * pair + 1, v1, stats))


def _head_cols(stats, pair, lane):
    c0 = jnp.sum(jnp.where(lane == 2 * pair, stats, 0.0), axis=-1, keepdims=True)
    c1 = jnp.sum(jnp.where(lane == 2 * pair + 1, stats, 0.0), axis=-1, keepdims=True)
    return jnp.concatenate([c0, c1], axis=0)


def _head_spread(stats, pair, head0):
    return jnp.where(head0, stats[:, 2 * pair:2 * pair + 1], stats[:, 2 * pair + 1:2 * pair + 2])


def _stack_heads(blk, head0):
    zero = jnp.zeros_like(blk)
    return jnp.concatenate([jnp.where(head0, blk, zero), jnp.where(head0, zero, blk)], axis=0)


def _attn_fwd(q, k, v, lc, after=None):
    kw, blocks = _attn_blocks(lc)
    shifts = sorted({b[2] for b in blocks})

    def body(q_ref, k_ref, v_ref, *refs):
        o_ref, lse_ref, bias_ref = refs[-3:]
        lane = lax.broadcasted_iota(jnp.int32, (QB, 128), 1)
        head0 = lane < 64
        pair = pl.program_id(0)
        _attn_bias(bias_ref, kw, shifts)

        @pl.when(pair == 0)
        def _():
            lse_ref[...] = jnp.zeros_like(lse_ref)

        for row0, kstart, shift in blocks:
            q2 = _stack_heads(q_ref[pl.ds(row0, QB), :], head0)
            kb = k_ref[pl.ds(kstart, kw), :]
            vb = v_ref[pl.ds(kstart, kw), :]
            s = _dot_nt(q2, kb) + bias_ref[shifts.index(shift)]
            m = jnp.max(s, axis=-1, keepdims=True)
            p = jnp.exp(s - m)
            den = jnp.sum(p, axis=-1, keepdims=True)
            o2 = _dot_nn(p.astype(bf16), vb) / den
            lse2 = m + jnp.log(den)
            o_ref[pl.ds(row0, QB), :] = jnp.where(head0, o2[:QB], o2[QB:]).astype(bf16)
            lse_ref[pl.ds(row0, QB), :] = _head_put(lse_ref[pl.ds(row0, QB), :], pair, lse2[:QB], lse2[QB:], lane)

    col = pl.BlockSpec((S, 128), lambda p: (0, p))
    extra = () if after is None else (after,)
    return pl.pallas_call(
        body, grid=(NG,), in_specs=[col, col, col] + [_ANY] * len(extra),
        out_specs=[col, pl.BlockSpec((S, 128), lambda p: (0, 0))],
        out_shape=[SDS((S, AW), bf16), SDS((S, 128), f32)],
        scratch_shapes=[pltpu.VMEM((len(shifts), 2 * QB, kw), f32)],
        compiler_params=_CP, name=f"attn_fwd_{lc}")(q, k, v, *extra)


def _attn_bwd(q, k, v, do, lse, delta, lc):
    kw, blocks = _attn_blocks(lc)
    shifts = sorted({b[2] for b in blocks})

    def body(q_ref, k_ref, v_ref, do_ref, lse_ref, dl_ref, dq_ref, dk_out, dv_out, bias_ref, dk_ref, dv_ref):
        lane = lax.broadcasted_iota(jnp.int32, (QB, 128), 1)
        head0 = lane < 64
        pair = pl.program_id(0)
        _attn_bias(bias_ref, kw, shifts)
        dk_ref[...] = jnp.zeros_like(dk_ref)
        dv_ref[...] = jnp.zeros_like(dv_ref)
        for row0, kstart, shift in blocks:
            q2 = _stack_heads(q_ref[pl.ds(row0, QB), :], head0)
            do2 = _stack_heads(do_ref[pl.ds(row0, QB), :], head0)
            lse2 = _head_cols(lse_ref[pl.ds(row0, QB), :], pair, lane)
            dl2 = _head_cols(dl_ref[pl.ds(row0, QB), :], pair, lane)
            kb = k_ref[pl.ds(kstart, kw), :]
            vb = v_ref[pl.ds(kstart, kw), :]
            p = jnp.exp(_dot_nt(q2, kb) + bias_ref[shifts.index(shift)] - lse2)
            ds = (p * (_dot_nt(do2, vb) - dl2)).astype(bf16)
            dq2 = _dot_nn(ds, kb)
            dq_ref[pl.ds(row0, QB), :] = jnp.where(head0, dq2[:QB], dq2[QB:]).astype(bf16)
            dk_ref[pl.ds(kstart, kw), :] += _dot_tn(ds, q2)
            dv_ref[pl.ds(kstart, kw), :] += _dot_tn(p.astype(bf16), do2)
        dk_out[...] = dk_ref[...].astype(bf16)
        dv_out[...] = dv_ref[...].astype(bf16)

    col = pl.BlockSpec((S, 128), lambda p: (0, p))
    stats = pl.BlockSpec((S, 128), lambda p: (0, 0))
    return pl.pallas_call(
        body, grid=(NG,), in_specs=[col] * 4 + [stats] * 2, out_specs=[col] * 3,
        out_shape=[SDS((S, AW), bf16)] * 3,
        scratch_shapes=[pltpu.VMEM((len(shifts), 2 * QB, kw), f32), pltpu.VMEM((S, 128), f32),
                        pltpu.VMEM((S, 128), f32)],
        compiler_params=_CP, name=f"attn_bwd_{lc}")(q, k, v, do, lse, delta)


def _mix_out_fwd(x, ypool, o1, l1, o4, l4, o16, l16, wout):
    def body(x_ref, yp_ref, o1_ref, l1_ref, o4_ref, l4_ref, o16_ref, l16_ref, w_ref,
             xo_ref, mixed_ref, o_ref, lse1_ref, lse4_ref, lse16_ref, sl4, sl16, sl):
        head0 = lax.broadcasted_iota(jnp.int32, (TM, 128), 1) < 64
        for r in range(4):
            sl4[pl.ds(r, TM // 4, stride=4), :] = l4_ref[r]
        for r in range(16):
            sl16[pl.ds(r, TM // 16, stride=16), :] = l16_ref[r]
        n4 = _dot_nn(_dilation_perm(4, True), o4_ref[...].reshape(TM, AW))
        n16 = _dot_nn(_dilation_perm(16, True), o16_ref[...].reshape(TM, AW))
        a, b, c = l1_ref[...], sl4[...], sl16[...]
        m = jnp.maximum(jnp.maximum(a, b), c)
        wa, wb, wc = jnp.exp(a - m), jnp.exp(b - m), jnp.exp(c - m)
        den = wa + wb + wc
        wa, wb, wc = wa / den, wb / den, wc / den
        lse = m + jnp.log(den)
        lse1_ref[...] = lse
        sl[...] = lse
        mixed_ref[:, :PW] = yp_ref[...].astype(bf16)
        for j in range(NG):
            y = (_head_spread(wa, j, head0) * o1_ref[:, _cols(j)].astype(f32)
                 + _head_spread(wb, j, head0) * n4[:, _cols(j)] + _head_spread(wc, j, head0) * n16[:, _cols(j)])
            o_ref[:, _cols(j)] = y
            mixed_ref[:, PW + 128 * j: PW + 128 * (j + 1)] = y.astype(bf16)
        for r in range(4):
            lse4_ref[r] = sl[pl.ds(r, TM // 4, stride=4), :]
        for r in range(16):
            lse16_ref[r] = sl[pl.ds(r, TM // 16, stride=16), :]
        xo_ref[...] = x_ref[...] + _dot_nn(mixed_ref[...], w_ref[...])

    return pl.pallas_call(
        body, grid=(S // TM,),
        in_specs=[_tile(D), _tile(PW), _tile(AW), _tile(128), _p4(), _p4(128), _p16(), _p16(128), _layer(D, D)],
        out_specs=[_tile(D), _tile(D), _tile(AW), _tile(128), _p4(128), _p16(128)],
        out_shape=[SDS((S, D), f32), SDS((S, D), bf16), SDS((S, AW), f32), SDS((S, 128), f32),
                   SDS((4, S // 4, 128), f32), SDS((16, S // 16, 128), f32)],
        scratch_shapes=[pltpu.VMEM((TM, 128), f32)] * 3,
        compiler_params=_CP, name="mix_out_fwd")(x, ypool, o1, l1, o4, l4, o16, l16, wout)


def _mix_out_bwd(dxo, o, wout):
    def body(dxo_ref, o_ref, w_ref, dxb_ref, dyp_ref, do1, do4, do16, dl1, dl4, dl16, sdl):
        dxb = dxo_ref[...].astype(bf16)
        dxb_ref[...] = dxb
        dm = _dot_nt(dxb, w_ref[...])
        dyp_ref[...] = dm[:, :PW]
        lane = lax.broadcasted_iota(jnp.int32, (TM, 128), 1)
        head0 = lane < 64
        dl = jnp.zeros((TM, 128), f32)
        for j in range(NG):
            d = dm[:, PW + 128 * j: PW + 128 * (j + 1)]
            prod = d * o_ref[:, _cols(j)]
            dl = _head_put(dl, j, jnp.sum(jnp.where(head0, prod, 0.0), axis=-1, keepdims=True),
                           jnp.sum(jnp.where(head0, 0.0, prod), axis=-1, keepdims=True), lane)
            do1[:, _cols(j)] = d.astype(bf16)
        dl1[...] = dl
        sdl[...] = dl
        for r in range(4):
            dl4[r] = sdl[pl.ds(r, TM // 4, stride=4), :]
        for r in range(16):
            dl16[r] = sdl[pl.ds(r, TM // 16, stride=16), :]
        nat = do1[...]
        do4[...] = _dot_nn(_dilation_perm(4), nat).astype(bf16).reshape(4, TM // 4, AW)
        do16[...] = _dot_nn(_dilation_perm(16), nat).astype(bf16).reshape(16, TM // 16, AW)

    return pl.pallas_call(
        body, grid=(S // TM,),
        in_specs=[_tile(D), _tile(AW), _layer(D, D)],
        out_specs=[_tile(D), _tile(PW), _tile(AW), _p4(), _p16(), _tile(128), _p4(128), _p16(128)],
        out_shape=[SDS((S, D), bf16), SDS((S, PW), f32),
                   SDS((S, AW), bf16), SDS((4, S // 4, AW), bf16), SDS((16, S // 16, AW), bf16),
                   SDS((S, 128), f32), SDS((4, S // 4, 128), f32), SDS((16, S // 16, 128), f32)],
        scratch_shapes=[pltpu.VMEM((TM, 128), f32)],
        compiler_params=_CP, name="mix_out_bwd")(dxo, o, wout)


def _loss_head(x, g, target):
    def body(x_ref, g_ref, t_ref, dx_ref, loss_ref, dg_ref):
        g = g_ref[...]
        r, xh, y = _rms(x_ref[...], g)
        err = y - t_ref[...]
        dy = err * (1.0 / D)

        @pl.when(pl.program_id(0) == 0)
        def _():
            loss_ref[...] = jnp.zeros_like(loss_ref)
            dg_ref[...] = jnp.zeros_like(dg_ref)

        loss_ref[...] += jnp.broadcast_to(0.5 * jnp.sum(jnp.mean(err * err, axis=-1, keepdims=True)), (1, D))
        dg_ref[...] += jnp.sum(dy * xh, axis=0, keepdims=True)
        dx_ref[...] = _rms_bwd(dy, r, xh, g)

    return pl.pallas_call(
        body, grid=(S // TM,),
        in_specs=[_tile(D), _const((1, D)), _tile(D)],
        out_specs=[_tile(D), _const((1, D)), _const((1, D))],
        out_shape=[SDS((S, D), f32), SDS((1, D), f32), SDS((1, D), f32)],
        compiler_params=_CP, name="loss_head")(x, g, target)


def _peer(k):
    x, y, c = lax.axis_index("x"), lax.axis_index("y"), lax.axis_index("c")
    px = 1 - x if k & 4 else x
    py = 1 - y if k & 2 else y
    pc = 1 - c if k & 1 else c
    return (px, py, pc), 4 * px + 2 * py + pc


def _diag_route():
    x, y, c = lax.axis_index("x"), lax.axis_index("y"), lax.axis_index("c")
    idx_x, idx_y = _peer(4)[1], _peer(2)[1]
    return idx_x + c * (idx_y - idx_x), (x + c * (1 - 2 * x), (1 - y) + c * (2 * y - 1), c)


def _all_gather(lands):
    n = len(lands)

    def body(*refs):
        zones, send_sems, recv_sems = refs[n:2 * n], refs[2 * n], refs[2 * n + 1]
        me, me_idx = _peer(0)
        sibling, sib_idx = _peer(1)
        (x_nbr, idx_x), (y_nbr, idx_y), idx_d = _peer(4), _peer(2), _peer(6)[1]
        fwd_idx, fwd_dev = _diag_route()

        def copy(k, t, idx, to):
            return _row_copy(zones[t], idx, send_sems.at[k, t], recv_sems.at[k, t], to)

        sent = []

        def send(k, t, idx, to):
            cp = copy(k, t, idx, to)
            cp.start()
            sent.append(cp)

        for t in range(n):
            send(0, t, me_idx, sibling)
            send(1, t, me_idx, x_nbr)
            send(2, t, me_idx, y_nbr)
        for t in range(n):
            copy(1, t, idx_x, me).wait_recv()
            send(3, t, idx_x, sibling)
        for t in range(n):
            copy(2, t, idx_y, me).wait_recv()
            send(4, t, idx_y, sibling)
        for t in range(n):
            send(5, t, fwd_idx, fwd_dev)
        for t in range(n):
            copy(5, t, idx_d, me).wait_recv()
            send(6, t, idx_d, sibling)
        for k, mask in ((0, 1), (3, 5), (4, 3), (6, 7)):
            for t in range(n):
                copy(k, t, _peer(mask)[1], me).wait_recv()
        for cp in sent:
            cp.wait_send()

    return pl.pallas_call(
        body, in_specs=[_ANY] * n, out_specs=[_ANY] * n,
        out_shape=[SDS(a.shape, a.dtype) for a in lands], input_output_aliases={t: t for t in range(n)},
        scratch_shapes=[pltpu.SemaphoreType.DMA((7, n)), pltpu.SemaphoreType.DMA((7, n))],
        name="all_gather_weights")(*lands)


def _hbm(a):
    return pltpu.with_memory_space_constraint(a, pltpu.HBM)


def _rows(ref, idx):
    r = ref.shape[0] // NDEV
    return ref.at[pl.ds(idx * r, r), :]


def _row_copy(ref, idx, send_sem, recv_sem, to):
    return pltpu.make_async_remote_copy(src_ref=_rows(ref, idx), dst_ref=_rows(ref, idx), send_sem=send_sem,
                                        recv_sem=recv_sem, device_id=to, device_id_type=_MESH)


def _place_own(me, shards, l):
    n = len(shards)

    def body(me_ref, *refs):
        for t in range(n):
            refs[n + t][...] = refs[t][...].astype(bf16)

    grid_spec = pltpu.PrefetchScalarGridSpec(
        num_scalar_prefetch=1, grid=(1,),
        in_specs=[pl.BlockSpec((None, s.shape[1], D), lambda i, me_ref: (l, 0, 0)) for s in shards],
        out_specs=[pl.BlockSpec((s.shape[1], D), lambda i, me_ref: (me_ref[0], 0)) for s in shards])
    return pl.pallas_call(
        body, grid_spec=grid_spec, out_shape=[SDS((NDEV * s.shape[1], D), bf16) for s in shards],
        compiler_params=_CP, name="place_own")(me, *shards)


_TOKEN = SDS((8, 128), f32)
def _ag_start(lands, after, l):
    n = len(lands)
    after = list(after) if isinstance(after, (list, tuple)) else [after]

    def body(*refs):
        zones, send_sems, recv_sems, token = refs[:n], refs[n + len(after)], refs[n + len(after) + 1], refs[-1]
        _, me_idx = _peer(0)
        for k, mask in enumerate((1, 4, 2)):
            for t in range(n):
                _row_copy(zones[t], me_idx, send_sems.at[k * n + t], recv_sems.at[k * n + t], _peer(mask)[0]).start()
        token[...] = jnp.zeros_like(token)

    outs = pl.pallas_call(
        body, name=f"ag_start_{l}", in_specs=[_HBM] * n + [_ANY] * len(after),
        out_specs=(_SEM, _SEM, *[_HBM] * n, pl.BlockSpec(memory_space=pltpu.VMEM)),
        out_shape=(pltpu.SemaphoreType.DMA((3 * n,)), pltpu.SemaphoreType.DMA((3 * n,)),
                   *[pltpu.HBM(a.shape, a.dtype) for a in lands], _TOKEN),
        input_output_aliases={t: 2 + t for t in range(n)}, compiler_params=_CP_SPLIT)(
            *[_hbm(a) for a in lands], *after)
    return outs[0], outs[1], list(outs[2:2 + n]), outs[-1]


def _ag_pass(lands, recv_sems, after, l):
    n = len(lands)
    after = list(after) if isinstance(after, (list, tuple)) else [after]

    def body(*refs):
        zones, recv_sems = refs[:n], refs[n]
        psend, precv, token = refs[n + 1 + len(after)], refs[n + 2 + len(after)], refs[-1]
        me, _ = _peer(0)
        sibling, _ = _peer(1)
        for j, mask in enumerate((4, 2)):
            idx = _peer(mask)[1]
            for t in range(n):
                _row_copy(zones[t], idx, psend.at[j * n + t], recv_sems.at[(1 + j) * n + t], me).wait_recv()
                _row_copy(zones[t], idx, psend.at[j * n + t], precv.at[j * n + t], sibling).start()
        fwd_idx, fwd_dev = _diag_route()
        for t in range(n):
            _row_copy(zones[t], fwd_idx, psend.at[2 * n + t], precv.at[2 * n + t], fwd_dev).start()
        token[...] = jnp.zeros_like(token)

    outs = pl.pallas_call(
        body, name=f"ag_pass_{l}", in_specs=[_HBM] * n + [_SEM] + [_ANY] * len(after),
        out_specs=(_SEM, _SEM, *[_HBM] * n, pl.BlockSpec(memory_space=pltpu.VMEM)),
        out_shape=(pltpu.SemaphoreType.DMA((3 * n,)), pltpu.SemaphoreType.DMA((3 * n,)),
                   *[pltpu.HBM(a.shape, a.dtype) for a in lands], _TOKEN),
        input_output_aliases={t: 2 + t for t in range(n)}, compiler_params=_CP_SPLIT)(*lands, recv_sems, *after)
    return outs[0], outs[1], list(outs[2:2 + n]), outs[-1]


def _ag_last(lands, precv, after, l):
    n = len(lands)
    after = list(after) if isinstance(after, (list, tuple)) else [after]

    def body(*refs):
        zones, precv = refs[:n], refs[n]
        qsend, qrecv, token = refs[n + 1 + len(after)], refs[n + 2 + len(after)], refs[-1]
        me, _ = _peer(0)
        sibling, _ = _peer(1)
        idx = _peer(6)[1]
        for t in range(n):
            _row_copy(zones[t], idx, qsend.at[t], precv.at[2 * n + t], me).wait_recv()
            _row_copy(zones[t], idx, qsend.at[t], qrecv.at[t], sibling).start()
        token[...] = jnp.zeros_like(token)

    outs = pl.pallas_call(
        body, name=f"ag_last_{l}", in_specs=[_HBM] * n + [_SEM] + [_ANY] * len(after),
        out_specs=(_SEM, _SEM, *[_HBM] * n, pl.BlockSpec(memory_space=pltpu.VMEM)),
        out_shape=(pltpu.SemaphoreType.DMA((n,)), pltpu.SemaphoreType.DMA((n,)),
                   *[pltpu.HBM(a.shape, a.dtype) for a in lands], _TOKEN),
        input_output_aliases={t: 2 + t for t in range(n)}, compiler_params=_CP_SPLIT)(*lands, precv, *after)
    return outs[0], outs[1], list(outs[2:2 + n]), outs[-1]


def _ag_wait(lands, send_sems, recv_sems, psend, precv, qsend, qrecv, after, l):
    n = len(lands)
    after = list(after) if isinstance(after, (list, tuple)) else [after]

    def body(*refs):
        zones = refs[:n]
        send_sems, recv_sems, psend, precv, qsend, qrecv = refs[n:n + 6]
        me, me_idx = _peer(0)
        for k in range(3):
            for t in range(n):
                _row_copy(zones[t], me_idx, send_sems.at[k * n + t], recv_sems.at[k * n + t], me).wait_send()
        for t in range(n):
            _row_copy(zones[t], _peer(1)[1], send_sems.at[t], recv_sems.at[t], me).wait_recv()
        fwd_idx, _ = _diag_route()
        for j, (mine, theirs) in enumerate(((_peer(4)[1], _peer(5)[1]), (_peer(2)[1], _peer(3)[1]))):
            for t in range(n):
                _row_copy(zones[t], mine, psend.at[j * n + t], precv.at[j * n + t], me).wait_send()
                _row_copy(zones[t], theirs, psend.at[j * n + t], precv.at[j * n + t], me).wait_recv()
        for t in range(n):
            _row_copy(zones[t], fwd_idx, psend.at[2 * n + t], precv.at[2 * n + t], me).wait_send()
            _row_copy(zones[t], _peer(6)[1], qsend.at[t], qrecv.at[t], me).wait_send()
            _row_copy(zones[t], _peer(7)[1], qsend.at[t], qrecv.at[t], me).wait_recv()

    outs = pl.pallas_call(
        body, name=f"ag_wait_{l}", in_specs=[_HBM] * n + [_SEM] * 6 + [_ANY] * len(after),
        out_specs=tuple([_HBM] * n), out_shape=tuple(pltpu.HBM(a.shape, a.dtype) for a in lands),
        input_output_aliases={t: t for t in range(n)}, compiler_params=_CP_SPLIT)(
            *lands, send_sems, recv_sems, psend, precv, qsend, qrecv, *after)
    return list(outs)


def _xchg_src(ref, slot_ref, idx):
    return _rows(ref, idx) if ref.shape[0] == NDEV * slot_ref.shape[1] else ref


def _rs_start(srcs, slots, after, tag):
    n = len(srcs)
    after = list(after) if isinstance(after, (list, tuple)) else [after]

    def body(*refs):
        src, slot = refs[:n], refs[n:2 * n]
        send_sems, recv_sems, token = refs[2 * n + len(after)], refs[2 * n + len(after) + 1], refs[-1]
        _, me_idx = _peer(0)
        for k in range(1, NDEV):
            dev, idx = _peer(k)
            for t in range(n):
                pltpu.make_async_remote_copy(
                    src_ref=_xchg_src(src[t], slot[t], idx), dst_ref=slot[t].at[me_idx],
                    send_sem=send_sems.at[(k - 1) * n + t], recv_sem=recv_sems.at[(k - 1) * n + t],
                    device_id=dev, device_id_type=_MESH).start()
        token[...] = jnp.zeros_like(token)

    outs = pl.pallas_call(
        body, name=f"rs_start_{tag}", in_specs=[_HBM] * (2 * n) + [_ANY] * len(after),
        out_specs=(_SEM, _SEM, *[_HBM] * (2 * n), pl.BlockSpec(memory_space=pltpu.VMEM)),
        out_shape=(pltpu.SemaphoreType.DMA(((NDEV - 1) * n,)), pltpu.SemaphoreType.DMA(((NDEV - 1) * n,)),
                   *[pltpu.HBM(a.shape, a.dtype) for a in list(srcs) + list(slots)], _TOKEN),
        input_output_aliases={t: 2 + t for t in range(2 * n)}, compiler_params=_CP_SPLIT)(
            *[_hbm(a) for a in list(srcs) + list(slots)], *after)
    return outs[0], outs[1], list(outs[2:2 + n]), list(outs[2 + n:2 + 2 * n]), outs[-1]


def _rs_wait(srcs, slots, send_sems, recv_sems, after, tag):
    n = len(srcs)
    after = list(after) if isinstance(after, (list, tuple)) else [after]

    def body(*refs):
        src, slot, send_sems, recv_sems = refs[:n], refs[n:2 * n], refs[2 * n], refs[2 * n + 1]
        me, _ = _peer(0)
        for k in range(1, NDEV):
            idx = _peer(k)[1]
            for t in range(n):
                cp = pltpu.make_async_remote_copy(
                    src_ref=_xchg_src(src[t], slot[t], idx), dst_ref=slot[t].at[idx],
                    send_sem=send_sems.at[(k - 1) * n + t], recv_sem=recv_sems.at[(k - 1) * n + t],
                    device_id=me, device_id_type=_MESH)
                cp.wait_send()
                cp.wait_recv()

    outs = pl.pallas_call(
        body, name=f"rs_wait_{tag}", in_specs=[_HBM] * (2 * n) + [_SEM, _SEM] + [_ANY] * len(after),
        out_specs=tuple([_HBM] * (2 * n)),
        out_shape=tuple(pltpu.HBM(a.shape, a.dtype) for a in list(srcs) + list(slots)),
        input_output_aliases={t: t for t in range(2 * n)}, compiler_params=_CP_SPLIT)(
            *srcs, *slots, send_sems, recv_sems, *after)
    return list(outs[:n]), list(outs[n:])


def _pair_start(full4s, bufs, after, tag):
    n = len(full4s)
    after = list(after) if isinstance(after, (list, tuple)) else [after]

    def body(*refs):
        full, buf = refs[:n], refs[n:2 * n]
        send_sems, recv_sems, token = refs[2 * n + len(after)], refs[2 * n + len(after) + 1], refs[-1]
        c = lax.axis_index("c")
        for t in range(n):
            pltpu.make_async_remote_copy(src_ref=full[t].at[:, 1 - c], dst_ref=buf[t], send_sem=send_sems.at[t],
                                         recv_sem=recv_sems.at[t], device_id=_peer(1)[0], device_id_type=_MESH).start()
        token[...] = jnp.zeros_like(token)

    outs = pl.pallas_call(
        body, name=f"pair_start_{tag}", in_specs=[_HBM] * (2 * n) + [_ANY] * len(after),
        out_specs=(_SEM, _SEM, *[_HBM] * (2 * n), pl.BlockSpec(memory_space=pltpu.VMEM)),
        out_shape=(pltpu.SemaphoreType.DMA((n,)), pltpu.SemaphoreType.DMA((n,)),
                   *[pltpu.HBM(a.shape, a.dtype) for a in list(full4s) + list(bufs)], _TOKEN),
        input_output_aliases={t: 2 + t for t in range(2 * n)}, compiler_params=_CP_SPLIT)(
            *[_hbm(a) for a in list(full4s) + list(bufs)], *after)
    return outs[0], outs[1], list(outs[2:2 + n]), list(outs[2 + n:2 + 2 * n]), outs[-1]


def _pair_wait(full4s, bufs, send_sems, recv_sems, after, tag):
    n = len(full4s)
    after = list(after) if isinstance(after, (list, tuple)) else [after]

    def body(*refs):
        full, buf, send_sems, recv_sems = refs[:n], refs[n:2 * n], refs[2 * n], refs[2 * n + 1]
        c = lax.axis_index("c")
        for t in range(n):
            cp = pltpu.make_async_remote_copy(src_ref=full[t].at[:, 1 - c], dst_ref=buf[t], send_sem=send_sems.at[t],
                                              recv_sem=recv_sems.at[t], device_id=_peer(0)[0], device_id_type=_MESH)
            cp.wait_send()
            cp.wait_recv()

    outs = pl.pallas_call(
        body, name=f"pair_wait_{tag}", in_specs=[_HBM] * (2 * n) + [_SEM, _SEM] + [_ANY] * len(after),
        out_specs=tuple([_HBM] * (2 * n)),
        out_shape=tuple(pltpu.HBM(a.shape, a.dtype) for a in list(full4s) + list(bufs)),
        input_output_aliases={t: t for t in range(2 * n)}, compiler_params=_CP_SPLIT)(
            *full4s, *bufs, send_sems, recv_sems, *after)
    return list(outs[:n]), list(outs[n:])


def _pair_sum(core, full4s, bufs):
    n = len(full4s)

    def body(core_ref, *refs):
        for t in range(n):
            refs[2 * n + t][...] = (refs[t][...].astype(f32) + refs[n + t][...].astype(f32)).astype(bf16)

    grid_spec = pltpu.PrefetchScalarGridSpec(
        num_scalar_prefetch=1, grid=(4,),
        in_specs=[pl.BlockSpec((None, None) + a.shape[2:], lambda j, core_ref: (j, core_ref[0], 0, 0)) for a in full4s]
        + [pl.BlockSpec((None,) + b.shape[1:], lambda j, core_ref: (j, 0, 0)) for b in bufs],
        out_specs=[pl.BlockSpec((None,) + b.shape[1:], lambda j, core_ref: (j, 0, 0)) for b in bufs])
    return pl.pallas_call(
        body, grid_spec=grid_spec, out_shape=[SDS(b.shape, bf16) for b in bufs],
        compiler_params=_CP, name="pair_sum")(core, *full4s, *bufs)


def _chip_start(sums, slots, after, tag):
    n = len(sums)
    after = list(after) if isinstance(after, (list, tuple)) else [after]

    def body(*refs):
        src, slot = refs[:n], refs[n:2 * n]
        send_sems, recv_sems, token = refs[2 * n + len(after)], refs[2 * n + len(after) + 1], refs[-1]
        my_chip = 2 * lax.axis_index("x") + lax.axis_index("y")
        for k, mask in enumerate((4, 2, 6)):
            dev, _ = _peer(mask)
            for t in range(n):
                pltpu.make_async_remote_copy(
                    src_ref=src[t].at[2 * dev[0] + dev[1]], dst_ref=slot[t].at[my_chip],
                    send_sem=send_sems.at[k * n + t], recv_sem=recv_sems.at[k * n + t],
                    device_id=dev, device_id_type=_MESH).start()
        token[...] = jnp.zeros_like(token)

    outs = pl.pallas_call(
        body, name=f"chip_start_{tag}", in_specs=[_HBM] * (2 * n) + [_ANY] * len(after),
        out_specs=(_SEM, _SEM, *[_HBM] * (2 * n), pl.BlockSpec(memory_space=pltpu.VMEM)),
        out_shape=(pltpu.SemaphoreType.DMA((3 * n,)), pltpu.SemaphoreType.DMA((3 * n,)),
                   *[pltpu.HBM(a.shape, a.dtype) for a in list(sums) + list(slots)], _TOKEN),
        input_output_aliases={t: 2 + t for t in range(2 * n)}, compiler_params=_CP_SPLIT)(
            *[_hbm(a) for a in list(sums) + list(slots)], *after)
    return outs[0], outs[1], list(outs[2:2 + n]), list(outs[2 + n:2 + 2 * n]), outs[-1]


def _chip_wait(sums, slots, send_sems, recv_sems, after, tag):
    n = len(sums)
    after = list(after) if isinstance(after, (list, tuple)) else [after]

    def body(*refs):
        src, slot, send_sems, recv_sems = refs[:n], refs[n:2 * n], refs[2 * n], refs[2 * n + 1]
        for k, mask in enumerate((4, 2, 6)):
            dev, _ = _peer(mask)
            chip = 2 * dev[0] + dev[1]
            for t in range(n):
                cp = pltpu.make_async_remote_copy(
                    src_ref=src[t].at[chip], dst_ref=slot[t].at[chip],
                    send_sem=send_sems.at[k * n + t], recv_sem=recv_sems.at[k * n + t],
                    device_id=_peer(0)[0], device_id_type=_MESH)
                cp.wait_send()
                cp.wait_recv()

    outs = pl.pallas_call(
        body, name=f"chip_wait_{tag}", in_specs=[_HBM] * (2 * n) + [_SEM, _SEM] + [_ANY] * len(after),
        out_specs=tuple([_HBM] * (2 * n)),
        out_shape=tuple(pltpu.HBM(a.shape, a.dtype) for a in list(sums) + list(slots)),
        input_output_aliases={t: t for t in range(2 * n)}, compiler_params=_CP_SPLIT)(
            *sums, *slots, send_sems, recv_sems, *after)
    return list(outs[:n]), list(outs[n:])


def _sum_slots(slots, rb):
    r = slots.shape[1]

    def body(s_ref, o_ref):
        acc = s_ref[0].astype(f32)
        for s in range(1, NDEV):
            acc = acc + s_ref[s].astype(f32)
        o_ref[...] = acc

    return pl.pallas_call(
        body, grid=(r // rb,),
        in_specs=[pl.BlockSpec((NDEV, rb, D), lambda i: (0, i, 0))],
        out_specs=pl.BlockSpec((rb, D), lambda i: (i, 0)),
        out_shape=SDS((r, D), f32), compiler_params=_CP, name="sum_slots")(slots)


def _adamw(w, g, m, v):
    shape = w.shape
    cols = shape[-1]
    rows = w.size // cols
    rb = rows
    for cand in (512, 256, 128, 64, 32, 16, 8):
        if rows % cand == 0 and rows > cand:
            rb = cand
            break

    def body(w_ref, g_ref, m_ref, v_ref, d_ref, mo_ref, vo_ref):
        d_ref[...], mo_ref[...], vo_ref[...] = _adamw_math(w_ref[...], g_ref[...], m_ref[...], v_ref[...])

    spec = pl.BlockSpec((rb, cols), lambda i: (i, 0))
    outs = pl.pallas_call(
        body, grid=(rows // rb,), in_specs=[spec] * 4, out_specs=[spec] * 3,
        out_shape=[SDS((rows, cols), f32)] * 3, compiler_params=_CP, name="adamw")(
            *(a.reshape(rows, cols) for a in (w, g, m, v)))
    return tuple(o.reshape(shape) for o in outs)


def _adamw_math(w, g, m, v):
    m = ADAM_B1 * m + (1.0 - ADAM_B1) * g
    v = ADAM_B2 * v + (1.0 - ADAM_B2) * (g * g)
    m_hat = m / (1.0 - ADAM_B1 ** ADAM_STEP)
    v_hat = v / (1.0 - ADAM_B2 ** ADAM_STEP)
    return -ADAM_LR * (m_hat / (jnp.sqrt(v_hat) + ADAM_EPS) + ADAM_WD * w), m, v


def _reduce_adamw(acc, me, full, slots, w, m, v, l):
    _, r, _ = w.shape
    ns = slots.shape[0]
    rb = r // 2 if r > 128 else r

    def body(me_ref, full_ref, slots_ref, w_ref, m_ref, v_ref, *refs):
        go_ref, d_ref, mo_ref, vo_ref = refs[-4:]
        own = full_ref[...].astype(f32)
        g = None
        for s in range(ns):
            part = jnp.where(me_ref[0] == s, own, slots_ref[s].astype(f32))
            g = part if g is None else g + part
        go_ref[...] = g
        d_ref[...], mo_ref[...], vo_ref[...] = _adamw_math(w_ref[...], g, m_ref[...], v_ref[...])

    steps = r // rb
    lay = pl.BlockSpec((None, rb, D), lambda i, me_ref: (l, i, 0))
    n_acc = 0 if acc is None else 4
    grid_spec = pltpu.PrefetchScalarGridSpec(
        num_scalar_prefetch=1, grid=(steps,),
        in_specs=[pl.BlockSpec((rb, D), lambda i, me_ref: (me_ref[0] * steps + i, 0)),
                  pl.BlockSpec((ns, rb, D), lambda i, me_ref: (0, i, 0)), lay, lay, lay] + [_ANY] * n_acc,
        out_specs=[lay] * 4)
    outs = pl.pallas_call(
        body, grid_spec=grid_spec, out_shape=[SDS(w.shape, f32)] * 4,
        input_output_aliases={6 + j: j for j in range(n_acc)},
        compiler_params=_CP, name="reduce_adamw")(me, full, slots, w, m, v, *(() if acc is None else acc))
    return tuple(outs)


_BIG = ("ffn1_w_gate", "ffn1_w_up", "ffn1_w_down", "w_in", "w_out", "ffn2_w_gate", "ffn2_w_up", "ffn2_w_down")
_TRANSPOSED = ("ffn1_w_gate", "ffn1_w_up", "w_in", "ffn2_w_gate", "ffn2_w_up")

def _block_diag(pool_w):
    out = jnp.zeros((L, PW, PW), pool_w.dtype)
    for gi in range(4):
        out = out.at[:, 64 * gi:64 * (gi + 1), 64 * gi:64 * (gi + 1)].set(pool_w[:, gi])
    return out


def kernel(x, positions, ffn1_norm, ffn1_w_gate, ffn1_w_up, ffn1_w_down, mix_norm, w_in, pool_w, pool_scale, w_out, ffn2_norm, ffn2_w_gate, ffn2_w_up, ffn2_w_down, final_norm, loss_target, m_ffn1_norm, m_ffn1_w_gate, m_ffn1_w_up, m_ffn1_w_down, m_mix_norm, m_w_in, m_pool_w, m_pool_scale, m_w_out, m_ffn2_norm, m_ffn2_w_gate, m_ffn2_w_up, m_ffn2_w_down, m_final_norm, v_ffn1_norm, v_ffn1_w_gate, v_ffn1_w_up, v_ffn1_w_down, v_mix_norm, v_w_in, v_pool_w, v_pool_scale, v_w_out, v_ffn2_norm, v_ffn2_w_gate, v_ffn2_w_up, v_ffn2_w_down, v_final_norm):
    weights = dict(ffn1_norm=ffn1_norm, ffn1_w_gate=ffn1_w_gate, ffn1_w_up=ffn1_w_up, ffn1_w_down=ffn1_w_down,
                   mix_norm=mix_norm, w_in=w_in, pool_w=pool_w, pool_scale=pool_scale, w_out=w_out,
                   ffn2_norm=ffn2_norm, ffn2_w_gate=ffn2_w_gate, ffn2_w_up=ffn2_w_up, ffn2_w_down=ffn2_w_down,
                   final_norm=final_norm)
    moms = dict(ffn1_norm=m_ffn1_norm, ffn1_w_gate=m_ffn1_w_gate, ffn1_w_up=m_ffn1_w_up, ffn1_w_down=m_ffn1_w_down,
                mix_norm=m_mix_norm, w_in=m_w_in, pool_w=m_pool_w, pool_scale=m_pool_scale, w_out=m_w_out,
                ffn2_norm=m_ffn2_norm, ffn2_w_gate=m_ffn2_w_gate, ffn2_w_up=m_ffn2_w_up, ffn2_w_down=m_ffn2_w_down,
                final_norm=m_final_norm)
    vels = dict(ffn1_norm=v_ffn1_norm, ffn1_w_gate=v_ffn1_w_gate, ffn1_w_up=v_ffn1_w_up, ffn1_w_down=v_ffn1_w_down,
                mix_norm=v_mix_norm, w_in=v_w_in, pool_w=v_pool_w, pool_scale=v_pool_scale, w_out=v_w_out,
                ffn2_norm=v_ffn2_norm, ffn2_w_gate=v_ffn2_w_gate, ffn2_w_up=v_ffn2_w_up, ffn2_w_down=v_ffn2_w_down,
                final_norm=v_final_norm)
    names = list(weights)

    me_idx = 4 * lax.axis_index("x") + 2 * lax.axis_index("y") + lax.axis_index("c")
    me_arr = me_idx.reshape(1).astype(jnp.int32)

    as_rows = lambda a, nm: jnp.swapaxes(a, 1, 2) if nm in _TRANSPOSED else a
    w_rows = {nm: as_rows(weights[nm], nm) for nm in _BIG}
    m_rows = {nm: as_rows(moms[nm], nm) for nm in _BIG}
    v_rows = {nm: as_rows(vels[nm], nm) for nm in _BIG}

    def landing_zones(l, which):
        return _place_own(me_arr, [w_rows[_BIG[t]] for t in which], l)

    g_ffn1 = [ffn1_norm[l].reshape(1, D) for l in range(L)]
    g_mix = [mix_norm[l].reshape(1, D) for l in range(L)]
    g_ffn2 = [ffn2_norm[l].reshape(1, D) for l in range(L)]
    wbd_all = _block_diag(pool_w).astype(bf16)
    wbd = [wbd_all[l] for l in range(L)]
    pscale = [pool_scale[l].reshape(1, PW) for l in range(L)]
    tabs = _rope_tables(positions)
    flat = lambda a: a.reshape(S, a.shape[-1])
    r4 = lambda a: a.reshape(4, S // 4, a.shape[-1])
    r16 = lambda a: a.reshape(16, S // 16, a.shape[-1])

    first, rest, whole = (0, 1, 2, 3), (4, 5, 6, 7), tuple(range(8))

    def ag_begin(l, which, after):
        tag = f"{l}{'' if which == whole else 'r'}"
        send_sems, recv_sems, zones, token = _ag_start(landing_zones(l, which), after, tag)
        return dict(tag=tag, zones=zones, s=send_sems, r=recv_sems), token

    def ag_second(ch, after):
        ch["ps"], ch["pr"], ch["zones"], token = _ag_pass(ch["zones"], ch["r"], after, ch["tag"])
        return token

    def ag_third(ch, after):
        ch["qs"], ch["qr"], ch["zones"], token = _ag_last(ch["zones"], ch["pr"], after, ch["tag"])
        return token

    def ag_end(ch, after):
        return _ag_wait(ch["zones"], ch["s"], ch["r"], ch["ps"], ch["pr"], ch["qs"], ch["qr"], after, ch["tag"])

    head = _all_gather(landing_zones(0, first))
    ch_rest, tok_rest = ag_begin(0, rest, head[0])
    chains = {}
    chains[1], tok_next = ag_begin(1, whole, head[0])
    gathered = [None] * L
    xs = x.reshape(S, D)
    saved = []
    for l in range(L):
        first_after, second_after = (), ()
        if l == 0:
            gt1, ut1, dn1, wint = head
            first_after = (tok_rest, tok_next)
        else:
            gt1, ut1, dn1, wint, wout, gt2, ut2, dn2 = gathered[l]
        x0 = xs
        x1, gate1, up1 = _ffn_fwd(x0, g_ffn1[l], gt1, ut1, dn1, after=first_after)
        hmix, vp, q1, k1, v1, q4, k4, v4, q16, k16, v16 = _mix_in_fwd(x1, g_mix[l], wint, tabs)
        q4, k4, v4, q16, k16, v16 = map(flat, (q4, k4, v4, q16, k16, v16))
        ypool, diff = _pool_fwd(vp, wbd[l], pscale[l])
        after_attn = None
        if l == 0:
            after_attn = ag_second(ch_rest, [ypool, q16])
        o1, l1 = _attn_fwd(q1, k1, v1, S, after=after_attn)
        o4, l4 = _attn_fwd(q4, k4, v4, S // 4, after=after_attn)
        o16, l16 = _attn_fwd(q16, k16, v16, S // 16, after=after_attn)
        if l == 0:
            token = ag_third(ch_rest, [o1, o4, o16])
            wout, gt2, ut2, dn2 = ag_end(ch_rest, token)
            gathered[0] = list(head) + [wout, gt2, ut2, dn2]
        elif l + 1 < L:
            second_after = (ag_second(chains[l + 1], [o1, o4, o16]),)
        x2, mixed, o, lse1, lse4, lse16 = _mix_out_fwd(x1, ypool, o1, l1, r4(o4), r4(l4), r16(o16), r16(l16), wout)
        if l == 0:
            second_after = (ag_second(chains[1], x2),)
        x3, gate2, up2 = _ffn_fwd(x2, g_ffn2[l], gt2, ut2, dn2, after=second_after)
        if l + 1 < L:
            token = ag_third(chains[l + 1], x3)
            if l + 2 < L:
                chains[l + 2], token = ag_begin(l + 2, whole, token)
            gathered[l + 1] = ag_end(chains[l + 1], token)
        saved.append(dict(x0=x0, x1=x1, x2=x2, gate1=gate1, up1=up1, gate2=gate2, up2=up2, hmix=hmix, diff=diff,
                          qkv=((q1, k1, v1), (q4, k4, v4), (q16, k16, v16)), mixed=mixed, o=o,
                          lse=(lse1, flat(lse4), flat(lse16))))
        xs = x3

    dx, loss_part, d_final = _loss_head(xs, final_norm.reshape(1, D), loss_target.reshape(S, D))

    d_norm = {nm: [None] * L for nm in ("ffn1_norm", "mix_norm", "ffn2_norm")}
    d_poolw, d_pscale = [None] * L, [None] * L
    group_a = ("ffn2_w_gate", "ffn2_w_up", "ffn2_w_down", "w_out")
    group_b = ("ffn1_w_gate", "ffn1_w_up", "ffn1_w_down", "w_in")
    acc = {}

    def exchange(full, group, after, tag):
        srcs = [full[nm] for nm in group]
        slots = [lax.empty((NDEV, g.shape[0] // NDEV, D), bf16) for g in srcs]
        ssem, rsem, srcs, slots, token = _rs_start(srcs, slots, after, tag)
        return (srcs, slots, ssem, rsem, tag), token

    def update(l, group, flight, after):
        srcs, slots, ssem, rsem, tag = flight
        srcs, slots = _rs_wait(srcs, slots, ssem, rsem, after, tag)
        for nm, full_g, slots_g in zip(group, srcs, slots):
            acc[nm] = _reduce_adamw(acc.get(nm), me_arr, full_g, slots_g, w_rows[nm], m_rows[nm], v_rows[nm], l)
        return [acc[nm][0] for nm in group], slots

    core_arr = lax.axis_index("c").reshape(1).astype(jnp.int32)
    chip_arr = (2 * lax.axis_index("x") + lax.axis_index("y")).reshape(1).astype(jnp.int32)

    def exchange_cores(full, group, after, tag):
        full4s = [full[nm].reshape(4, 2, full[nm].shape[0] // NDEV, D) for nm in group]
        bufs = [lax.empty((4,) + a.shape[2:], bf16) for a in full4s]
        ssem, rsem, full4s, bufs, token = _pair_start(full4s, bufs, after, tag)
        return (full4s, bufs, ssem, rsem, tag), token

    def exchange_chips(flight, after):
        full4s, bufs, ssem, rsem, tag = flight
        full4s, bufs = _pair_wait(full4s, bufs, ssem, rsem, after, tag)
        sums = _pair_sum(core_arr, full4s, bufs)
        slots = [lax.empty(a.shape, bf16) for a in sums]
        ssem, rsem, sums, slots, token = _chip_start(sums, slots, bufs[0], tag)
        return (sums, slots, ssem, rsem, tag), token

    def update_chips(l, group, flight, after):
        sums, slots, ssem, rsem, tag = flight
        sums, slots = _chip_wait(sums, slots, ssem, rsem, after, tag)
        for nm, sums_g, slots_g in zip(group, sums, slots):
            own = sums_g.reshape(4 * sums_g.shape[1], D)
            acc[nm] = _reduce_adamw(acc.get(nm), chip_arr, own, slots_g, w_rows[nm], m_rows[nm], v_rows[nm], l)
        return [acc[nm][0] for nm in group]

    flights = {}
    token_b = None
    for l in reversed(range(L)):
        sv = saved[l]
        gt1, ut1, dn1, wint, wout, gt2, ut2, dn2 = gathered[l]
        full = {}
        dx, dgate, dup, h, dy, d_norm["ffn2_norm"][l] = _ffn_bwd_d(
            sv["x2"], g_ffn2[l], sv["gate2"], sv["up2"], dx, gt2, ut2, dn2, after=() if token_b is None else (token_b,))
        full["ffn2_w_gate"], full["ffn2_w_up"], full["ffn2_w_down"] = _ffn_bwd_w(h, dy, sv["gate2"], sv["up2"], dgate, dup)

        dxb, dyp, do1, do4, do16, dl1, dl4, dl16 = _mix_out_bwd(dx, sv["o"], wout)
        full["w_out"] = _wgrad(sv["mixed"], dxb)
        flights[l, "a"], token_a = (exchange_cores if l == 0 else exchange)(full, group_a, dxb, f"a{l}")
        dvp, d_poolw[l], d_pscale[l] = _pool_bwd(dyp, sv["diff"], wbd[l], pscale[l], after=(token_a,))
        dos, dls = (do1, flat(do4), flat(do16)), (dl1, flat(dl4), flat(dl16))
        dqkv = []
        for b, lc in enumerate((S, S // 4, S // 16)):
            qb, kb, vb = sv["qkv"][b]
            dqkv.append(_attn_bwd(qb, kb, vb, dos[b], sv["lse"][b], dls[b], lc))
        d4 = tuple(r4(a) for a in dqkv[1])
        d16 = tuple(r16(a) for a in dqkv[2])
        mix_after = ()
        if l == 0:
            flights[0, "a"], token_a = exchange_chips(flights[0, "a"], [dqkv[0][0], dqkv[1][0], dqkv[2][0]])
            mix_after = (token_a,)
        dx, dproj, d_norm["mix_norm"][l] = _mix_in_bwd(dx, sv["x1"], g_mix[l], wint, tabs, dvp, dqkv[0], d4, d16,
                                                       after=mix_after)
        full["w_in"] = _wgrad(dproj, sv["hmix"])

        dx, dgate, dup, h, dy, d_norm["ffn1_norm"][l] = _ffn_bwd_d(sv["x0"], g_ffn1[l], sv["gate1"], sv["up1"], dx, gt1, ut1, dn1)
        full["ffn1_w_gate"], full["ffn1_w_up"], full["ffn1_w_down"] = _ffn_bwd_w(h, dy, sv["gate1"], sv["up1"], dgate, dup)

        after = dx
        if l + 1 < L and l + 1 >= 2:
            after, _ = update(l + 1, group_a, flights.pop((l + 1, "a")), after)
            after, _ = update(l + 1, group_b, flights.pop((l + 1, "b")), after)
        if l > 0:
            flights[l, "b"], token_b = exchange(full, group_b, after, f"b{l}")

    flights[0, "b"], token_b = exchange_cores(full, group_b, dx, "b0")
    pad8 = lambda a: jnp.pad(a, ((0, 8 - a.shape[0]), (0, 0)))
    misc = jnp.concatenate([d_final, jnp.concatenate(d_pscale, axis=1), loss_part], axis=0)
    small = jnp.concatenate(
        [pad8(jnp.concatenate(d_norm[nm], axis=0)) for nm in ("ffn1_norm", "mix_norm", "ffn2_norm")]
        + [pad8(misc), jnp.stack(d_poolw).reshape(L * 16, D)], axis=0)
    small_slots = lax.dynamic_update_slice(lax.empty((NDEV, SMALL_ROWS, D), f32), small[None], (me_idx, 0, 0))
    pack_sems = _rs_start([small], [small_slots], token_b, "pack")
    flights[0, "b"], token_b = exchange_chips(flights[0, "b"], pack_sems[-1])

    after = token_b
    for key in [(1, "a"), (1, "b")]:
        after, _ = update(key[0], group_a if key[1] == "a" else group_b, flights.pop(key), after)
    _, pack_slots = _rs_wait(pack_sems[2], pack_sems[3], pack_sems[0], pack_sems[1], after, "pack")
    sm = _sum_slots(pack_slots[0], SMALL_ROWS)
    grads = {}
    grads["ffn1_norm"], grads["mix_norm"], grads["ffn2_norm"] = sm[0:L], sm[8:8 + L], sm[16:16 + L]
    grads["final_norm"] = sm[24]
    grads["pool_scale"] = sm[25].reshape(L, PW)
    grads["pool_w"] = sm[32:32 + L * 16].reshape(L, 4, 64, 64)
    loss = sm[26, 0]
    upd = {nm: _adamw(weights[nm], grads[nm], moms[nm], vels[nm]) for nm in names if nm not in _BIG}
    after = update_chips(0, group_a, flights.pop((0, "a")), [upd[nm][0] for nm in upd])
    update_chips(0, group_b, flights.pop((0, "b")), after)
    for nm in _BIG:
        grads[nm], upd[nm] = as_rows(acc[nm][0], nm), tuple(as_rows(a, nm) for a in acc[nm][1:])
    return (loss, dx.reshape(1, S, D), *[grads[nm] for nm in names], *[upd[nm][0] for nm in names],
            *[upd[nm][1] for nm in names], *[upd[nm][2] for nm in names])
```

```python
import jax
import jax.numpy as jnp
from jax import lax
from jax.experimental import pallas as pl
from jax.experimental.pallas import tpu as pltpu

f32 = jnp.float32
bf16 = jnp.bfloat16
SDS = jax.ShapeDtypeStruct

D = 1024
S = 2048
F = 2816
L = 4
PW = 256
AW = 768
PROJ = PW + 3 * AW
NDEV = 8
TM = 256
QB = 128
HALF = 64
NG = AW // 128
NORM_EPS = 1e-6
MASK_VALUE = -1e30
ROPE_THETA = 500000.0
ADAM_LR, ADAM_B1, ADAM_B2, ADAM_EPS, ADAM_WD, ADAM_STEP = 0.001, 0.9, 0.999, 1e-08, 0.01, 10
POOL_WINDOWS = (2, 4, 8, 16)
PAD = 8
SMALL_ROWS = 96
VMEM_LIMIT = 56 * 1024 * 1024

_CP = pltpu.CompilerParams(vmem_limit_bytes=VMEM_LIMIT)
_ANY = pl.BlockSpec(memory_space=pl.ANY)
_HBM = pl.BlockSpec(memory_space=pltpu.HBM)
_SEM = pl.BlockSpec(memory_space=pltpu.SEMAPHORE)
_MESH = pl.DeviceIdType.MESH
_CP_SPLIT = pltpu.CompilerParams(has_side_effects=pltpu.SideEffectType.DATAFLOW_SIDE_EFFECTING)


def _dot_nn(a, b):
    return lax.dot_general(a, b, (((1,), (0,)), ((), ())), preferred_element_type=f32)


def _dot_nt(a, b):
    return lax.dot_general(a, b, (((1,), (1,)), ((), ())), preferred_element_type=f32)


def _dot_tn(a, b):
    return lax.dot_general(a, b, (((0,), (0,)), ((), ())), preferred_element_type=f32)


def _rms(x, g):
    r = lax.rsqrt(jnp.mean(x * x, axis=-1, keepdims=True) + NORM_EPS)
    xh = x * r
    return r, xh, xh * g


def _rms_bwd(dh, r, xh, g):
    dxh = dh * g
    return r * (dxh - xh * jnp.mean(dxh * xh, axis=-1, keepdims=True))


def _tile(cols, rows=TM):
    return pl.BlockSpec((rows, cols), lambda i: (i, 0))


def _const(shape):
    return pl.BlockSpec(shape, lambda i: (0,) * len(shape))


def _layer(rows, cols):
    return pl.BlockSpec((rows, cols), lambda i: (0, 0), pipeline_mode=pl.Buffered(1))


def _p4(cols=AW):
    return pl.BlockSpec((4, TM // 4, cols), lambda i: (0, i, 0))


def _p16(cols=AW):
    return pl.BlockSpec((16, TM // 16, cols), lambda i: (0, i, 0))


def _cols(j):
    return slice(128 * j, 128 * (j + 1))


def _follow(body, n_in, after):
    k = len(after)
    return body if k == 0 else (lambda *refs: body(*refs[:n_in], *refs[n_in + k:]))


def _ffn_fwd(x, g, gt, ut, dn, after=()):
    def body(x_ref, g_ref, gt_ref, ut_ref, dn_ref, xo_ref, gate_ref, up_ref):
        x = x_ref[...]
        _, _, hn = _rms(x, g_ref[...])
        h = hn.astype(bf16)
        gate = _dot_nt(h, gt_ref[...])
        up = _dot_nt(h, ut_ref[...])
        gate_ref[...] = gate.astype(bf16)
        up_ref[...] = up.astype(bf16)
        a = (gate * jax.nn.sigmoid(gate) * up).astype(bf16)
        xo_ref[...] = x + 0.5 * _dot_nn(a, dn_ref[...])

    rows = 2 * TM
    return pl.pallas_call(
        _follow(body, 5, after), grid=(S // rows,),
        in_specs=[_tile(D, rows), _layer(1, D), _layer(F, D), _layer(F, D), _layer(F, D)] + [_ANY] * len(after),
        out_specs=[_tile(D, rows), _tile(F, rows), _tile(F, rows)],
        out_shape=[SDS((S, D), f32), SDS((S, F), bf16), SDS((S, F), bf16)],
        compiler_params=_CP, name="ffn_fwd")(x, g, gt, ut, dn, *after)


def _ffn_bwd_d(x, g, gate, up, dxo, gt, ut, dn, after=()):
    def body(x_ref, g_ref, gate_ref, up_ref, dxo_ref, gt_ref, ut_ref, dn_ref,
             dx_ref, dgate_ref, dup_ref, h_ref, dy_ref, dg_ref):
        x = x_ref[...]
        g = g_ref[...]
        r, xh, hn = _rms(x, g)
        h_ref[...] = hn.astype(bf16)
        dxo = dxo_ref[...]
        dy = (0.5 * dxo).astype(bf16)
        dy_ref[...] = dy
        da = _dot_nt(dy, dn_ref[...])
        gate = gate_ref[...].astype(f32)
        up = up_ref[...].astype(f32)
        sg = jax.nn.sigmoid(gate)
        dgate = (da * up * (sg * (1.0 + gate * (1.0 - sg)))).astype(bf16)
        dup = (da * (gate * sg)).astype(bf16)
        dgate_ref[...] = dgate
        dup_ref[...] = dup
        dh = _dot_nn(dgate, gt_ref[...]) + _dot_nn(dup, ut_ref[...])

        @pl.when(pl.program_id(0) == 0)
        def _():
            dg_ref[...] = jnp.zeros_like(dg_ref)

        dg_ref[...] += jnp.sum(dh * xh, axis=0, keepdims=True)
        dx_ref[...] = dxo + _rms_bwd(dh, r, xh, g)

    return pl.pallas_call(
        _follow(body, 8, after), grid=(S // TM,),
        in_specs=[_tile(D), _layer(1, D), _tile(F), _tile(F), _tile(D),
                  _layer(F, D), _layer(F, D), _layer(F, D)] + [_ANY] * len(after),
        out_specs=[_tile(D), _tile(F), _tile(F), _tile(D), _tile(D), _const((1, D))],
        out_shape=[SDS((S, D), f32), SDS((S, F), bf16), SDS((S, F), bf16), SDS((S, D), bf16),
                   SDS((S, D), bf16), SDS((1, D), f32)],
        compiler_params=_CP, name="ffn_bwd_d")(x, g, gate, up, dxo, gt, ut, dn, *after)


def _ffn_bwd_w(h, dy, gate, up, dgate, dup):
    fc = 256

    def body(h_ref, dy_ref, gate_ref, up_ref, dgate_ref, dup_ref, dgt_ref, dut_ref, ddn_ref):
        gate = gate_ref[...].astype(f32)
        a = (gate * jax.nn.sigmoid(gate) * up_ref[...].astype(f32)).astype(bf16)
        ddn_ref[...] = _dot_tn(a, dy_ref[...]).astype(bf16)
        h = h_ref[...]
        dgt_ref[...] = _dot_tn(dgate_ref[...], h).astype(bf16)
        dut_ref[...] = _dot_tn(dup_ref[...], h).astype(bf16)

    col = pl.BlockSpec((S, fc), lambda j: (0, j))
    row = pl.BlockSpec((fc, D), lambda j: (j, 0))
    full = pl.BlockSpec((S, D), lambda j: (0, 0))
    return pl.pallas_call(
        body, grid=(F // fc,),
        in_specs=[full, full, col, col, col, col],
        out_specs=[row, row, row],
        out_shape=[SDS((F, D), bf16)] * 3,
        compiler_params=_CP, name="ffn_bwd_w")(h, dy, gate, up, dgate, dup)


def _wgrad(a, b):
    m, n = a.shape[1], b.shape[1]
    mc = 2 * TM

    def body(a_ref, b_ref, o_ref):
        o_ref[...] = _dot_tn(a_ref[...], b_ref[...]).astype(bf16)

    return pl.pallas_call(
        body, grid=(m // mc,),
        in_specs=[pl.BlockSpec((S, mc), lambda j: (0, j)), pl.BlockSpec((S, n), lambda j: (0, 0))],
        out_specs=pl.BlockSpec((mc, n), lambda j: (j, 0)),
        out_shape=SDS((m, n), bf16),
        compiler_params=_CP, name="wgrad")(a, b)


def _rope(t, c, sn, sp):
    return t * c + pltpu.roll(t, 120, 1) * sn + pltpu.roll(t, 8, 1) * sp


def _rope_bwd(d, c, sn, sp):
    return d * c + pltpu.roll(d * sn, 8, 1) + pltpu.roll(d * sp, 120, 1)


def _rope_tables(positions):
    inv_freq = ROPE_THETA ** (-jnp.arange(0, 16, 2, dtype=f32) / 16)
    ang = positions.reshape(S, 1).astype(f32) * inv_freq
    cos, sin = jnp.cos(ang), jnp.sin(ang)
    one = jnp.ones((S, 48), f32)
    zero8 = jnp.zeros((S, 8), f32)
    zero48 = jnp.zeros((S, 48), f32)
    c = jnp.concatenate([cos, cos, one], axis=1)
    sn = jnp.concatenate([-sin, zero8, zero48], axis=1)
    sp = jnp.concatenate([zero8, sin, zero48], axis=1)
    return tuple(jnp.concatenate([t, t], axis=1) for t in (c, sn, sp))


def _dilation_perm(n, back=False):
    per = TM // n
    i = lax.broadcasted_iota(jnp.int32, (TM, TM), 1 if back else 0)
    j = lax.broadcasted_iota(jnp.int32, (TM, TM), 0 if back else 1)
    return jnp.where(j == n * (i % per) + i // per, 1.0, 0.0).astype(bf16)


def _mix_in_fwd(x, g, wint, tabs):
    def body(x_ref, g_ref, w_ref, c_ref, sn_ref, sp_ref,
             h_ref, vp_ref, q1, k1, v1, q4, k4, v4, q16, k16, v16):
        _, _, hn = _rms(x_ref[...], g_ref[...])
        h = hn.astype(bf16)
        h_ref[...] = h
        proj = _dot_nt(h, w_ref[...])
        vp_ref[...] = proj[:, :PW]
        c, sn, sp = c_ref[...], sn_ref[...], sp_ref[...]
        perm4, perm16 = _dilation_perm(4), _dilation_perm(16)
        for kind, (o1, o4, o16) in enumerate(((q1, q4, q16), (k1, k4, k16), (v1, v4, v16))):
            for j in range(NG):
                t = proj[:, PW + kind * AW + 128 * j: PW + kind * AW + 128 * (j + 1)]
                if kind == 0:
                    t = _rope(t, c, sn, sp) * 0.125
                elif kind == 1:
                    t = _rope(t, c, sn, sp)
                o1[:, _cols(j)] = t.astype(bf16)
            nat = o1[...]
            o4[...] = _dot_nn(perm4, nat).astype(bf16).reshape(4, TM // 4, AW)
            o16[...] = _dot_nn(perm16, nat).astype(bf16).reshape(16, TM // 16, AW)

    nat, d4, d16 = SDS((S, AW), bf16), SDS((4, S // 4, AW), bf16), SDS((16, S // 16, AW), bf16)
    return pl.pallas_call(
        body, grid=(S // TM,),
        in_specs=[_tile(D), _layer(1, D), _layer(PROJ, D), _tile(128), _tile(128), _tile(128)],
        out_specs=[_tile(D), _tile(PW)] + [_tile(AW)] * 3 + [_p4()] * 3 + [_p16()] * 3,
        out_shape=[SDS((S, D), bf16), SDS((S, PW), f32)] + [nat] * 3 + [d4] * 3 + [d16] * 3,
        compiler_params=_CP, name="mix_in_fwd")(x, g, wint, *tabs)


def _mix_in_bwd(dxo, x, g, wint, tabs, dvp, d1, d4, d16, after=()):
    def body(dxo_ref, x_ref, g_ref, w_ref, c_ref, sn_ref, sp_ref, dvp_ref,
             dq1, dk1, dv1, dq4, dk4, dv4, dq16, dk16, dv16,
             dx_ref, dproj_ref, dg_ref):
        c, sn, sp = c_ref[...], sn_ref[...], sp_ref[...]
        dproj_ref[:, :PW] = dvp_ref[...].astype(bf16)
        back4, back16 = _dilation_perm(4, True), _dilation_perm(16, True)
        for kind, (a1, a4, a16) in enumerate(((dq1, dq4, dq16), (dk1, dk4, dk16), (dv1, dv4, dv16))):
            n4 = _dot_nn(back4, a4[...].reshape(TM, AW))
            n16 = _dot_nn(back16, a16[...].reshape(TM, AW))
            for j in range(NG):
                t = a1[:, _cols(j)].astype(f32) + n4[:, _cols(j)] + n16[:, _cols(j)]
                if kind == 0:
                    t = _rope_bwd(t * 0.125, c, sn, sp)
                elif kind == 1:
                    t = _rope_bwd(t, c, sn, sp)
                dproj_ref[:, PW + kind * AW + 128 * j: PW + kind * AW + 128 * (j + 1)] = t.astype(bf16)
        g = g_ref[...]
        r_, xh, _ = _rms(x_ref[...], g)
        dh = _dot_nn(dproj_ref[...], w_ref[...])

        @pl.when(pl.program_id(0) == 0)
        def _():
            dg_ref[...] = jnp.zeros_like(dg_ref)

        dg_ref[...] += jnp.sum(dh * xh, axis=0, keepdims=True)
        dx_ref[...] = dxo_ref[...] + _rms_bwd(dh, r_, xh, g)

    return pl.pallas_call(
        _follow(body, 17, after), grid=(S // TM,),
        in_specs=[_tile(D), _tile(D), _layer(1, D), _layer(PROJ, D), _tile(128), _tile(128), _tile(128),
                  _tile(PW)] + [_tile(AW)] * 3 + [_p4()] * 3 + [_p16()] * 3 + [_ANY] * len(after),
        out_specs=[_tile(D), _tile(PROJ), _const((1, D))],
        out_shape=[SDS((S, D), f32), SDS((S, PROJ), bf16), SDS((1, D), f32)],
        compiler_params=_CP, name="mix_in_bwd")(dxo, x, g, wint, *tabs, dvp, *d1, *d4, *d16, *after)


def _pool_sums(pad_ref, base, rows, adjoint):
    lane_group = lax.broadcasted_iota(jnp.int32, (rows, PW), 1) // 64
    sign = -1 if adjoint else 1

    def sh(o):
        return pad_ref[pl.ds(PAD + base + sign * o, rows), :]

    out = None
    acc = None
    lo, hi = 0, 0
    for gi, w in enumerate(POOL_WINDOWS):
        for o in list(range(-(w // 2), lo)) + list(range(hi, w - w // 2)):
            acc = sh(o) if acc is None else acc + sh(o)
        lo, hi = -(w // 2), w - w // 2
        out = acc if out is None else jnp.where(lane_group >= gi, acc, out)
    return out


def _pool_counts(base, rows):
    pos = base + lax.broadcasted_iota(jnp.int32, (rows, PW), 0)
    lane_group = lax.broadcasted_iota(jnp.int32, (rows, PW), 1) // 64
    cnt = None
    for gi, w in enumerate(POOL_WINDOWS):
        lo = jnp.maximum(pos - w // 2, 0)
        hi = jnp.minimum(pos + w - 1 - w // 2, S - 1)
        c = (hi - lo + 1).astype(f32)
        cnt = c if cnt is None else jnp.where(lane_group >= gi, c, cnt)
    return cnt


def _pool_fwd(vp, wbd, scale):
    ch = 256

    def body(vp_ref, w_ref, sc_ref, y_ref, diff_ref, pad):
        pad[pl.ds(0, PAD), :] = jnp.zeros((PAD, PW), f32)
        pad[pl.ds(PAD + S, PAD), :] = jnp.zeros((PAD, PW), f32)
        pad[pl.ds(PAD, S), :] = vp_ref[...]
        for b in range(S // ch):
            base = b * ch
            pooled = _pool_sums(pad, base, ch, False) / _pool_counts(base, ch)
            diff = (pooled - vp_ref[pl.ds(base, ch), :]).astype(bf16)
            diff_ref[pl.ds(base, ch), :] = diff
            y_ref[pl.ds(base, ch), :] = _dot_nn(diff, w_ref[...]) * sc_ref[...]

    whole = lambda shape: pl.BlockSpec(shape, lambda i: (0,) * len(shape))
    return pl.pallas_call(
        body, grid=(1,),
        in_specs=[whole((S, PW)), whole((PW, PW)), whole((1, PW))],
        out_specs=[whole((S, PW)), whole((S, PW))],
        out_shape=[SDS((S, PW), f32), SDS((S, PW), bf16)],
        scratch_shapes=[pltpu.VMEM((S + 2 * PAD, PW), f32)],
        compiler_params=_CP, name="pool_fwd")(vp, wbd, scale)


def _pool_bwd(dy, diff, wbd, scale, after=()):
    ch = 256

    def body(dy_ref, diff_ref, w_ref, sc_ref, dvp_ref, dw_ref, dsc_ref, pad):
        pad[pl.ds(0, PAD), :] = jnp.zeros((PAD, PW), f32)
        pad[pl.ds(PAD + S, PAD), :] = jnp.zeros((PAD, PW), f32)
        dw = jnp.zeros((PW, PW), f32)
        dsc = jnp.zeros((1, PW), f32)
        for b in range(S // ch):
            base = b * ch
            dy = dy_ref[pl.ds(base, ch), :]
            diff = diff_ref[pl.ds(base, ch), :]
            dsc = dsc + jnp.sum(dy * _dot_nn(diff, w_ref[...]), axis=0, keepdims=True)
            dz = (dy * sc_ref[...]).astype(bf16)
            dw = dw + _dot_tn(diff, dz)
            ddiff = _dot_nt(dz, w_ref[...])
            dvp_ref[pl.ds(base, ch), :] = -ddiff
            pad[pl.ds(PAD + base, ch), :] = ddiff / _pool_counts(base, ch)
        for gi in range(4):
            dw_ref[gi] = dw[64 * gi:64 * (gi + 1), 64 * gi:64 * (gi + 1)]
        dsc_ref[...] = dsc
        for b in range(S // ch):
            base = b * ch
            dvp_ref[pl.ds(base, ch), :] += _pool_sums(pad, base, ch, True)

    whole = lambda shape: pl.BlockSpec(shape, lambda i: (0,) * len(shape))
    return pl.pallas_call(
        _follow(body, 4, after), grid=(1,),
        in_specs=[whole((S, PW)), whole((S, PW)), whole((PW, PW)), whole((1, PW))] + [_ANY] * len(after),
        out_specs=[whole((S, PW)), whole((4, 64, 64)), whole((1, PW))],
        out_shape=[SDS((S, PW), f32), SDS((4, 64, 64), f32), SDS((1, PW), f32)],
        scratch_shapes=[pltpu.VMEM((S + 2 * PAD, PW), f32)],
        compiler_params=_CP, name="pool_bwd")(dy, diff, wbd, scale, *after)


def _attn_blocks(lc):
    bpc = lc // QB
    kw = min(2 * QB, lc)
    blocks = []
    for b in range(S // QB):
        t0 = (b % bpc) * QB
        ks_in = min(max(t0 - HALF, 0), lc - kw)
        blocks.append((b * QB, (b // bpc) * lc + ks_in, t0 - ks_in))
    return kw, blocks


def _attn_bias(bias_ref, kw, shifts):
    r = lax.broadcasted_iota(jnp.int32, (2 * QB, kw), 0) % QB
    c = lax.broadcasted_iota(jnp.int32, (2 * QB, kw), 1)
    for i, shift in enumerate(shifts):
        bias_ref[i] = jnp.where(jnp.abs(r + shift - c) <= HALF, 0.0, MASK_VALUE).astype(f32)


def _head_put(stats, pair, v0, v1, lane):
    return jnp.where(lane == 2 * pair, v0, jnp.where(lane == 2 * pair + 1, v1, stats))


def _head_cols(stats, pair, lane):
    c0 = jnp.sum(jnp.where(lane == 2 * pair, stats, 0.0), axis=-1, keepdims=True)
    c1 = jnp.sum(jnp.where(lane == 2 * pair + 1, stats, 0.0), axis=-1, keepdims=True)
    return jnp.concatenate([c0, c1], axis=0)


def _head_spread(stats, pair, head0):
    return jnp.where(head0, stats[:, 2 * pair:2 * pair + 1], stats[:, 2 * pair + 1:2 * pair + 2])


def _stack_heads(blk, head0):
    zero = jnp.zeros_like(blk)
    return jnp.concatenate([jnp.where(head0, blk, zero), jnp.where(head0, zero, blk)], axis=0)


def _attn_fwd(q, k, v, lc, after=None):
    kw, blocks = _attn_blocks(lc)
    shifts = sorted({b[2] for b in blocks})

    def body(q_ref, k_ref, v_ref, *refs):
        o_ref, lse_ref, bias_ref = refs[-3:]
        lane = lax.broadcasted_iota(jnp.int32, (QB, 128), 1)
        head0 = lane < 64
        pair = pl.program_id(0)
        _attn_bias(bias_ref, kw, shifts)

        @pl.when(pair == 0)
        def _():
            lse_ref[...] = jnp.zeros_like(lse_ref)

        for row0, kstart, shift in blocks:
            q2 = _stack_heads(q_ref[pl.ds(row0, QB), :], head0)
            kb = k_ref[pl.ds(kstart, kw), :]
            vb = v_ref[pl.ds(kstart, kw), :]
            s = _dot_nt(q2, kb) + bias_ref[shifts.index(shift)]
            m = jnp.max(s, axis=-1, keepdims=True)
            p = jnp.exp(s - m)
            den = jnp.sum(p, axis=-1, keepdims=True)
            o2 = _dot_nn(p.astype(bf16), vb) / den
            lse2 = m + jnp.log(den)
            o_ref[pl.ds(row0, QB), :] = jnp.where(head0, o2[:QB], o2[QB:]).astype(bf16)
            lse_ref[pl.ds(row0, QB), :] = _head_put(lse_ref[pl.ds(row0, QB), :], pair, lse2[:QB], lse2[QB:], lane)

    col = pl.BlockSpec((S, 128), lambda p: (0, p))
    extra = () if after is None else (after,)
    return pl.pallas_call(
        body, grid=(NG,), in_specs=[col, col, col] + [_ANY] * len(extra),
        out_specs=[col, pl.BlockSpec((S, 128), lambda p: (0, 0))],
        out_shape=[SDS((S, AW), bf16), SDS((S, 128), f32)],
        scratch_shapes=[pltpu.VMEM((len(shifts), 2 * QB, kw), f32)],
        compiler_params=_CP, name=f"attn_fwd_{lc}")(q, k, v, *extra)


def _attn_bwd(q, k, v, do, lse, delta, lc):
    kw, blocks = _attn_blocks(lc)
    shifts = sorted({b[2] for b in blocks})

    def body(q_ref, k_ref, v_ref, do_ref, lse_ref, dl_ref, dq_ref, dk_out, dv_out, bias_ref, dk_ref, dv_ref):
        lane = lax.broadcasted_iota(jnp.int32, (QB, 128), 1)
        head0 = lane < 64
        pair = pl.program_id(0)
        _attn_bias(bias_ref, kw, shifts)
        dk_ref[...] = jnp.zeros_like(dk_ref)
        dv_ref[...] = jnp.zeros_like(dv_ref)
        for row0, kstart, shift in blocks:
            q2 = _stack_heads(q_ref[pl.ds(row0, QB), :], head0)
            do2 = _stack_heads(do_ref[pl.ds(row0, QB), :], head0)
            lse2 = _head_cols(lse_ref[pl.ds(row0, QB), :], pair, lane)
            dl2 = _head_cols(dl_ref[pl.ds(row0, QB), :], pair, lane)
            kb = k_ref[pl.ds(kstart, kw), :]
            vb = v_ref[pl.ds(kstart, kw), :]
            p = jnp.exp(_dot_nt(q2, kb) + bias_ref[shifts.index(shift)] - lse2)
            ds = (p * (_dot_nt(do2, vb) - dl2)).astype(bf16)
            dq2 = _dot_nn(ds, kb)
            dq_ref[pl.ds(row0, QB), :] = jnp.where(head0, dq2[:QB], dq2[QB:]).astype(bf16)
            dk_ref[pl.ds(kstart, kw), :] += _dot_tn(ds, q2)
            dv_ref[pl.ds(kstart, kw), :] += _dot_tn(p.astype(bf16), do2)
        dk_out[...] = dk_ref[...].astype(bf16)
        dv_out[...] = dv_ref[...].astype(bf16)

    col = pl.BlockSpec((S, 128), lambda p: (0, p))
    stats = pl.BlockSpec((S, 128), lambda p: (0, 0))
    return pl.pallas_call(
        body, grid=(NG,), in_specs=[col] * 4 + [stats] * 2, out_specs=[col] * 3,
        out_shape=[SDS((S, AW), bf16)] * 3,
        scratch_shapes=[pltpu.VMEM((len(shifts), 2 * QB, kw), f32), pltpu.VMEM((S, 128), f32),
                        pltpu.VMEM((S, 128), f32)],
        compiler_params=_CP, name=f"attn_bwd_{lc}")(q, k, v, do, lse, delta)


def _mix_out_fwd(x, ypool, o1, l1, o4, l4, o16, l16, wout):
    def body(x_ref, yp_ref, o1_ref, l1_ref, o4_ref, l4_ref, o16_ref, l16_ref, w_ref,
             xo_ref, mixed_ref, o_ref, lse1_ref, lse4_ref, lse16_ref, sl4, sl16, sl):
        head0 = lax.broadcasted_iota(jnp.int32, (TM, 128), 1) < 64
        for r in range(4):
            sl4[pl.ds(r, TM // 4, stride=4), :] = l4_ref[r]
        for r in range(16):
            sl16[pl.ds(r, TM // 16, stride=16), :] = l16_ref[r]
        n4 = _dot_nn(_dilation_perm(4, True), o4_ref[...].reshape(TM, AW))
        n16 = _dot_nn(_dilation_perm(16, True), o16_ref[...].reshape(TM, AW))
        a, b, c = l1_ref[...], sl4[...], sl16[...]
        m = jnp.maximum(jnp.maximum(a, b), c)
        wa, wb, wc = jnp.exp(a - m), jnp.exp(b - m), jnp.exp(c - m)
        den = wa + wb + wc
        wa, wb, wc = wa / den, wb / den, wc / den
        lse = m + jnp.log(den)
        lse1_ref[...] = lse
        sl[...] = lse
        mixed_ref[:, :PW] = yp_ref[...].astype(bf16)
        for j in range(NG):
            y = (_head_spread(wa, j, head0) * o1_ref[:, _cols(j)].astype(f32)
                 + _head_spread(wb, j, head0) * n4[:, _cols(j)] + _head_spread(wc, j, head0) * n16[:, _cols(j)])
            o_ref[:, _cols(j)] = y
            mixed_ref[:, PW + 128 * j: PW + 128 * (j + 1)] = y.astype(bf16)
        for r in range(4):
            lse4_ref[r] = sl[pl.ds(r, TM // 4, stride=4), :]
        for r in range(16):
            lse16_ref[r] = sl[pl.ds(r, TM // 16, stride=16), :]
        xo_ref[...] = x_ref[...] + _dot_nn(mixed_ref[...], w_ref[...])

    return pl.pallas_call(
        body, grid=(S // TM,),
        in_specs=[_tile(D), _tile(PW), _tile(AW), _tile(128), _p4(), _p4(128), _p16(), _p16(128), _layer(D, D)],
        out_specs=[_tile(D), _tile(D), _tile(AW), _tile(128), _p4(128), _p16(128)],
        out_shape=[SDS((S, D), f32), SDS((S, D), bf16), SDS((S, AW), f32), SDS((S, 128), f32),
                   SDS((4, S // 4, 128), f32), SDS((16, S // 16, 128), f32)],
        scratch_shapes=[pltpu.VMEM((TM, 128), f32)] * 3,
        compiler_params=_CP, name="mix_out_fwd")(x, ypool, o1, l1, o4, l4, o16, l16, wout)


def _mix_out_bwd(dxo, o, wout):
    def body(dxo_ref, o_ref, w_ref, dxb_ref, dyp_ref, do1, do4, do16, dl1, dl4, dl16, sdl):
        dxb = dxo_ref[...].astype(bf16)
        dxb_ref[...] = dxb
        dm = _dot_nt(dxb, w_ref[...])
        dyp_ref[...] = dm[:, :PW]
        lane = lax.broadcasted_iota(jnp.int32, (TM, 128), 1)
        head0 = lane < 64
        dl = jnp.zeros((TM, 128), f32)
        for j in range(NG):
            d = dm[:, PW + 128 * j: PW + 128 * (j + 1)]
            prod = d * o_ref[:, _cols(j)]
            dl = _head_put(dl, j, jnp.sum(jnp.where(head0, prod, 0.0), axis=-1, keepdims=True),
                           jnp.sum(jnp.where(head0, 0.0, prod), axis=-1, keepdims=True), lane)
            do1[:, _cols(j)] = d.astype(bf16)
        dl1[...] = dl
        sdl[...] = dl
        for r in range(4):
            dl4[r] = sdl[pl.ds(r, TM // 4, stride=4), :]
        for r in range(16):
            dl16[r] = sdl[pl.ds(r, TM // 16, stride=16), :]
        nat = do1[...]
        do4[...] = _dot_nn(_dilation_perm(4), nat).astype(bf16).reshape(4, TM // 4, AW)
        do16[...] = _dot_nn(_dilation_perm(16), nat).astype(bf16).reshape(16, TM // 16, AW)

    return pl.pallas_call(
        body, grid=(S // TM,),
        in_specs=[_tile(D), _tile(AW), _layer(D, D)],
        out_specs=[_tile(D), _tile(PW), _tile(AW), _p4(), _p16(), _tile(128), _p4(128), _p16(128)],
        out_shape=[SDS((S, D), bf16), SDS((S, PW), f32),
                   SDS((S, AW), bf16), SDS((4, S // 4, AW), bf16), SDS((16, S // 16, AW), bf16),
                   SDS((S, 128), f32), SDS((4, S // 4, 128), f32), SDS((16, S // 16, 128), f32)],
        scratch_shapes=[pltpu.VMEM((TM, 128), f32)],
        compiler_params=_CP, name="mix_out_bwd")(dxo, o, wout)


def _loss_head(x, g, target):
    def body(x_ref, g_ref, t_ref, dx_ref, loss_ref, dg_ref):
        g = g_ref[...]
        r, xh, y = _rms(x_ref[...], g)
        err = y - t_ref[...]
        dy = err * (1.0 / D)

        @pl.when(pl.program_id(0) == 0)
        def _():
            loss_ref[...] = jnp.zeros_like(loss_ref)
            dg_ref[...] = jnp.zeros_like(dg_ref)

        loss_ref[...] += jnp.broadcast_to(0.5 * jnp.sum(jnp.mean(err * err, axis=-1, keepdims=True)), (1, D))
        dg_ref[...] += jnp.sum(dy * xh, axis=0, keepdims=True)
        dx_ref[...] = _rms_bwd(dy, r, xh, g)

    return pl.pallas_call(
        body, grid=(S // TM,),
        in_specs=[_tile(D), _const((1, D)), _tile(D)],
        out_specs=[_tile(D), _const((1, D)), _const((1, D))],
        out_shape=[SDS((S, D), f32), SDS((1, D), f32), SDS((1, D), f32)],
        compiler_params=_CP, name="loss_head")(x, g, target)


def _peer(k):
    x, y, c = lax.axis_index("x"), lax.axis_index("y"), lax.axis_index("c")
    px = 1 - x if k & 4 else x
    py = 1 - y if k & 2 else y
    pc = 1 - c if k & 1 else c
    return (px, py, pc), 4 * px + 2 * py + pc


def _diag_route():
    x, y, c = lax.axis_index("x"), lax.axis_index("y"), lax.axis_index("c")
    idx_x, idx_y = _peer(4)[1], _peer(2)[1]
    return idx_x + c * (idx_y - idx_x), (x + c * (1 - 2 * x), (1 - y) + c * (2 * y - 1), c)


def _all_gather(lands):
    n = len(lands)

    def body(*refs):
        zones, send_sems, recv_sems = refs[n:2 * n], refs[2 * n], refs[2 * n + 1]
        me, me_idx = _peer(0)
        sibling, sib_idx = _peer(1)
        (x_nbr, idx_x), (y_nbr, idx_y), idx_d = _peer(4), _peer(2), _peer(6)[1]
        fwd_idx, fwd_dev = _diag_route()

        def copy(k, t, idx, to):
            return _row_copy(zones[t], idx, send_sems.at[k, t], recv_sems.at[k, t], to)

        sent = []

        def send(k, t, idx, to):
            cp = copy(k, t, idx, to)
            cp.start()
            sent.append(cp)

        for t in range(n):
            send(0, t, me_idx, sibling)
            send(1, t, me_idx, x_nbr)
            send(2, t, me_idx, y_nbr)
        for t in range(n):
            copy(1, t, idx_x, me).wait_recv()
            send(3, t, idx_x, sibling)
        for t in range(n):
            copy(2, t, idx_y, me).wait_recv()
            send(4, t, idx_y, sibling)
        for t in range(n):
            send(5, t, fwd_idx, fwd_dev)
        for t in range(n):
            copy(5, t, idx_d, me).wait_recv()
            send(6, t, idx_d, sibling)
        for k, mask in ((0, 1), (3, 5), (4, 3), (6, 7)):
            for t in range(n):
                copy(k, t, _peer(mask)[1], me).wait_recv()
        for cp in sent:
            cp.wait_send()

    return pl.pallas_call(
        body, in_specs=[_ANY] * n, out_specs=[_ANY] * n,
        out_shape=[SDS(a.shape, a.dtype) for a in lands], input_output_aliases={t: t for t in range(n)},
        scratch_shapes=[pltpu.SemaphoreType.DMA((7, n)), pltpu.SemaphoreType.DMA((7, n))],
        name="all_gather_weights")(*lands)


def _hbm(a):
    return pltpu.with_memory_space_constraint(a, pltpu.HBM)


def _rows(ref, idx):
    r = ref.shape[0] // NDEV
    return ref.at[pl.ds(idx * r, r), :]


def _row_copy(ref, idx, send_sem, recv_sem, to):
    return pltpu.make_async_remote_copy(src_ref=_rows(ref, idx), dst_ref=_rows(ref, idx), send_sem=send_sem,
                                        recv_sem=recv_sem, device_id=to, device_id_type=_MESH)


def _place_own(me, shards, l):
    n = len(shards)

    def body(me_ref, *refs):
        for t in range(n):
            refs[n + t][...] = refs[t][...].astype(bf16)

    grid_spec = pltpu.PrefetchScalarGridSpec(
        num_scalar_prefetch=1, grid=(1,),
        in_specs=[pl.BlockSpec((None, s.shape[1], D), lambda i, me_ref: (l, 0, 0)) for s in shards],
        out_specs=[pl.BlockSpec((s.shape[1], D), lambda i, me_ref: (me_ref[0], 0)) for s in shards])
    return pl.pallas_call(
        body, grid_spec=grid_spec, out_shape=[SDS((NDEV * s.shape[1], D), bf16) for s in shards],
        compiler_params=_CP, name="place_own")(me, *shards)


_TOKEN = SDS((8, 128), f32)
def _ag_start(lands, after, l):
    n = len(lands)
    after = list(after) if isinstance(after, (list, tuple)) else [after]

    def body(*refs):
        zones, send_sems, recv_sems, token = refs[:n], refs[n + len(after)], refs[n + len(after) + 1], refs[-1]
        _, me_idx = _peer(0)
        for k, mask in enumerate((1, 4, 2)):
            for t in range(n):
                _row_copy(zones[t], me_idx, send_sems.at[k * n + t], recv_sems.at[k * n + t], _peer(mask)[0]).start()
        token[...] = jnp.zeros_like(token)

    outs = pl.pallas_call(
        body, name=f"ag_start_{l}", in_specs=[_HBM] * n + [_ANY] * len(after),
        out_specs=(_SEM, _SEM, *[_HBM] * n, pl.BlockSpec(memory_space=pltpu.VMEM)),
        out_shape=(pltpu.SemaphoreType.DMA((3 * n,)), pltpu.SemaphoreType.DMA((3 * n,)),
                   *[pltpu.HBM(a.shape, a.dtype) for a in lands], _TOKEN),
        input_output_aliases={t: 2 + t for t in range(n)}, compiler_params=_CP_SPLIT)(
            *[_hbm(a) for a in lands], *after)
    return outs[0], outs[1], list(outs[2:2 + n]), outs[-1]


def _ag_pass(lands, recv_sems, after, l):
    n = len(lands)
    after = list(after) if isinstance(after, (list, tuple)) else [after]

    def body(*refs):
        zones, recv_sems = refs[:n], refs[n]
        psend, precv, token = refs[n + 1 + len(after)], refs[n + 2 + len(after)], refs[-1]
        me, _ = _peer(0)
        sibling, _ = _peer(1)
        for j, mask in enumerate((4, 2)):
            idx = _peer(mask)[1]
            for t in range(n):
                _row_copy(zones[t], idx, psend.at[j * n + t], recv_sems.at[(1 + j) * n + t], me).wait_recv()
                _row_copy(zones[t], idx, psend.at[j * n + t], precv.at[j * n + t], sibling).start()
        fwd_idx, fwd_dev = _diag_route()
        for t in range(n):
            _row_copy(zones[t], fwd_idx, psend.at[2 * n + t], precv.at[2 * n + t], fwd_dev).start()
        token[...] = jnp.zeros_like(token)

    outs = pl.pallas_call(
        body, name=f"ag_pass_{l}", in_specs=[_HBM] * n + [_SEM] + [_ANY] * len(after),
        out_specs=(_SEM, _SEM, *[_HBM] * n, pl.BlockSpec(memory_space=pltpu.VMEM)),
        out_shape=(pltpu.SemaphoreType.DMA((3 * n,)), pltpu.SemaphoreType.DMA((3 * n,)),
                   *[pltpu.HBM(a.shape, a.dtype) for a in lands], _TOKEN),
        input_output_aliases={t: 2 + t for t in range(n)}, compiler_params=_CP_SPLIT)(*lands, recv_sems, *after)
    return outs[0], outs[1], list(outs[2:2 + n]), outs[-1]


def _ag_last(lands, precv, after, l):
    n = len(lands)
    after = list(after) if isinstance(after, (list, tuple)) else [after]

    def body(*refs):
        zones, precv = refs[:n], refs[n]
        qsend, qrecv, token = refs[n + 1 + len(after)], refs[n + 2 + len(after)], refs[-1]
        me, _ = _peer(0)
        sibling, _ = _peer(1)
        idx = _peer(6)[1]
        for t in range(n):
            _row_copy(zones[t], idx, qsend.at[t], precv.at[2 * n + t], me).wait_recv()
            _row_copy(zones[t], idx, qsend.at[t], qrecv.at[t], sibling).start()
        token[...] = jnp.zeros_like(token)

    outs = pl.pallas_call(
        body, name=f"ag_last_{l}", in_specs=[_HBM] * n + [_SEM] + [_ANY] * len(after),
        out_specs=(_SEM, _SEM, *[_HBM] * n, pl.BlockSpec(memory_space=pltpu.VMEM)),
        out_shape=(pltpu.SemaphoreType.DMA((n,)), pltpu.SemaphoreType.DMA((n,)),
                   *[pltpu.HBM(a.shape, a.dtype) for a in lands], _TOKEN),
        input_output_aliases={t: 2 + t for t in range(n)}, compiler_params=_CP_SPLIT)(*lands, precv, *after)
    return outs[0], outs[1], list(outs[2:2 + n]), outs[-1]


def _ag_wait(lands, send_sems, recv_sems, psend, precv, qsend, qrecv, after, l):
    n = len(lands)
    after = list(after) if isinstance(after, (list, tuple)) else [after]

    def body(*refs):
        zones = refs[:n]
        send_sems, recv_sems, psend, precv, qsend, qrecv = refs[n:n + 6]
        me, me_idx = _peer(0)
        for k in range(3):
            for t in range(n):
                _row_copy(zones[t], me_idx, send_sems.at[k * n + t], recv_sems.at[k * n + t], me).wait_send()
        for t in range(n):
            _row_copy(zones[t], _peer(1)[1], send_sems.at[t], recv_sems.at[t], me).wait_recv()
        fwd_idx, _ = _diag_route()
        for j, (mine, theirs) in enumerate(((_peer(4)[1], _peer(5)[1]), (_peer(2)[1], _peer(3)[1]))):
            for t in range(n):
                _row_copy(zones[t], mine, psend.at[j * n + t], precv.at[j * n + t], me).wait_send()
                _row_copy(zones[t], theirs, psend.at[j * n + t], precv.at[j * n + t], me).wait_recv()
        for t in range(n):
            _row_copy(zones[t], fwd_idx, psend.at[2 * n + t], precv.at[2 * n + t], me).wait_send()
            _row_copy(zones[t], _peer(6)[1], qsend.at[t], qrecv.at[t], me).wait_send()
            _row_copy(zones[t], _peer(7)[1], qsend.at[t], qrecv.at[t], me).wait_recv()

    outs = pl.pallas_call(
        body, name=f"ag_wait_{l}", in_specs=[_HBM] * n + [_SEM] * 6 + [_ANY] * len(after),
        out_specs=tuple([_HBM] * n), out_shape=tuple(pltpu.HBM(a.shape, a.dtype) for a in lands),
        input_output_aliases={t: t for t in range(n)}, compiler_params=_CP_SPLIT)(
            *lands, send_sems, recv_sems, psend, precv, qsend, qrecv, *after)
    return list(outs)


def _xchg_src(ref, slot_ref, idx):
    return _rows(ref, idx) if ref.shape[0] == NDEV * slot_ref.shape[1] else ref


def _rs_start(srcs, slots, after, tag):
    n = len(srcs)
    after = list(after) if isinstance(after, (list, tuple)) else [after]

    def body(*refs):
        src, slot = refs[:n], refs[n:2 * n]
        send_sems, recv_sems, token = refs[2 * n + len(after)], refs[2 * n + len(after) + 1], refs[-1]
        _, me_idx = _peer(0)
        for k in range(1, NDEV):
            dev, idx = _peer(k)
            for t in range(n):
                pltpu.make_async_remote_copy(
                    src_ref=_xchg_src(src[t], slot[t], idx), dst_ref=slot[t].at[me_idx],
                    send_sem=send_sems.at[(k - 1) * n + t], recv_sem=recv_sems.at[(k - 1) * n + t],
                    device_id=dev, device_id_type=_MESH).start()
        token[...] = jnp.zeros_like(token)

    outs = pl.pallas_call(
        body, name=f"rs_start_{tag}", in_specs=[_HBM] * (2 * n) + [_ANY] * len(after),
        out_specs=(_SEM, _SEM, *[_HBM] * (2 * n), pl.BlockSpec(memory_space=pltpu.VMEM)),
        out_shape=(pltpu.SemaphoreType.DMA(((NDEV - 1) * n,)), pltpu.SemaphoreType.DMA(((NDEV - 1) * n,)),
                   *[pltpu.HBM(a.shape, a.dtype) for a in list(srcs) + list(slots)], _TOKEN),
        input_output_aliases={t: 2 + t for t in range(2 * n)}, compiler_params=_CP_SPLIT)(
            *[_hbm(a) for a in list(srcs) + list(slots)], *after)
    return outs[0], outs[1], list(outs[2:2 + n]), list(outs[2 + n:2 + 2 * n]), outs[-1]


def _rs_wait(srcs, slots, send_sems, recv_sems, after, tag):
    n = len(srcs)
    after = list(after) if isinstance(after, (list, tuple)) else [after]

    def body(*refs):
        src, slot, send_sems, recv_sems = refs[:n], refs[n:2 * n], refs[2 * n], refs[2 * n + 1]
        me, _ = _peer(0)
        for k in range(1, NDEV):
            idx = _peer(k)[1]
            for t in range(n):
                cp = pltpu.make_async_remote_copy(
                    src_ref=_xchg_src(src[t], slot[t], idx), dst_ref=slot[t].at[idx],
                    send_sem=send_sems.at[(k - 1) * n + t], recv_sem=recv_sems.at[(k - 1) * n + t],
                    device_id=me, device_id_type=_MESH)
                cp.wait_send()
                cp.wait_recv()

    outs = pl.pallas_call(
        body, name=f"rs_wait_{tag}", in_specs=[_HBM] * (2 * n) + [_SEM, _SEM] + [_ANY] * len(after),
        out_specs=tuple([_HBM] * (2 * n)),
        out_shape=tuple(pltpu.HBM(a.shape, a.dtype) for a in list(srcs) + list(slots)),
        input_output_aliases={t: t for t in range(2 * n)}, compiler_params=_CP_SPLIT)(
            *srcs, *slots, send_sems, recv_sems, *after)
    return list(outs[:n]), list(outs[n:])


def _pair_start(full4s, bufs, after, tag):
    n = len(full4s)
    after = list(after) if isinstance(after, (list, tuple)) else [after]

    def body(*refs):
        full, buf = refs[:n], refs[n:2 * n]
        send_sems, recv_sems, token = refs[2 * n + len(after)], refs[2 * n + len(after) + 1], refs[-1]
        c = lax.axis_index("c")
        for t in range(n):
            pltpu.make_async_remote_copy(src_ref=full[t].at[:, 1 - c], dst_ref=buf[t], send_sem=send_sems.at[t],
                                         recv_sem=recv_sems.at[t], device_id=_peer(1)[0], device_id_type=_MESH).start()
        token[...] = jnp.zeros_like(token)

    outs = pl.pallas_call(
        body, name=f"pair_start_{tag}", in_specs=[_HBM] * (2 * n) + [_ANY] * len(after),
        out_specs=(_SEM, _SEM, *[_HBM] * (2 * n), pl.BlockSpec(memory_space=pltpu.VMEM)),
        out_shape=(pltpu.SemaphoreType.DMA((n,)), pltpu.SemaphoreType.DMA((n,)),
                   *[pltpu.HBM(a.shape, a.dtype) for a in list(full4s) + list(bufs)], _TOKEN),
        input_output_aliases={t: 2 + t for t in range(2 * n)}, compiler_params=_CP_SPLIT)(
            *[_hbm(a) for a in list(full4s) + list(bufs)], *after)
    return outs[0], outs[1], list(outs[2:2 + n]), list(outs[2 + n:2 + 2 * n]), outs[-1]


def _pair_wait(full4s, bufs, send_sems, recv_sems, after, tag):
    n = len(full4s)
    after = list(after) if isinstance(after, (list, tuple)) else [after]

    def body(*refs):
        full, buf, send_sems, recv_sems = refs[:n], refs[n:2 * n], refs[2 * n], refs[2 * n + 1]
        c = lax.axis_index("c")
        for t in range(n):
            cp = pltpu.make_async_remote_copy(src_ref=full[t].at[:, 1 - c], dst_ref=buf[t], send_sem=send_sems.at[t],
                                              recv_sem=recv_sems.at[t], device_id=_peer(0)[0], device_id_type=_MESH)
            cp.wait_send()
            cp.wait_recv()

    outs = pl.pallas_call(
        body, name=f"pair_wait_{tag}", in_specs=[_HBM] * (2 * n) + [_SEM, _SEM] + [_ANY] * len(after),
        out_specs=tuple([_HBM] * (2 * n)),
        out_shape=tuple(pltpu.HBM(a.shape, a.dtype) for a in list(full4s) + list(bufs)),
        input_output_aliases={t: t for t in range(2 * n)}, compiler_params=_CP_SPLIT)(
            *full4s, *bufs, send_sems, recv_sems, *after)
    return list(outs[:n]), list(outs[n:])


def _pair_sum(core, full4s, bufs):
    n = len(full4s)

    def body(core_ref, *refs):
        for t in range(n):
            refs[2 * n + t][...] = (refs[t][...].astype(f32) + refs[n + t][...].astype(f32)).astype(bf16)

    grid_spec = pltpu.PrefetchScalarGridSpec(
        num_scalar_prefetch=1, grid=(4,),
        in_specs=[pl.BlockSpec((None, None) + a.shape[2:], lambda j, core_ref: (j, core_ref[0], 0, 0)) for a in full4s]
        + [pl.BlockSpec((None,) + b.shape[1:], lambda j, core_ref: (j, 0, 0)) for b in bufs],
        out_specs=[pl.BlockSpec((None,) + b.shape[1:], lambda j, core_ref: (j, 0, 0)) for b in bufs])
    return pl.pallas_call(
        body, grid_spec=grid_spec, out_shape=[SDS(b.shape, bf16) for b in bufs],
        compiler_params=_CP, name="pair_sum")(core, *full4s, *bufs)


def _chip_start(sums, slots, after, tag):
    n = len(sums)
    after = list(after) if isinstance(after, (list, tuple)) else [after]

    def body(*refs):
        src, slot = refs[:n], refs[n:2 * n]
        send_sems, recv_sems, token = refs[2 * n + len(after)], refs[2 * n + len(after) + 1], refs[-1]
        my_chip = 2 * lax.axis_index("x") + lax.axis_index("y")
        for k, mask in enumerate((4, 2, 6)):
            dev, _ = _peer(mask)
            for t in range(n):
                pltpu.make_async_remote_copy(
                    src_ref=src[t].at[2 * dev[0] + dev[1]], dst_ref=slot[t].at[my_chip],
                    send_sem=send_sems.at[k * n + t], recv_sem=recv_sems.at[k * n + t],
                    device_id=dev, device_id_type=_MESH).start()
        token[...] = jnp.zeros_like(token)

    outs = pl.pallas_call(
        body, name=f"chip_start_{tag}", in_specs=[_HBM] * (2 * n) + [_ANY] * len(after),
        out_specs=(_SEM, _SEM, *[_HBM] * (2 * n), pl.BlockSpec(memory_space=pltpu.VMEM)),
        out_shape=(pltpu.SemaphoreType.DMA((3 * n,)), pltpu.SemaphoreType.DMA((3 * n,)),
                   *[pltpu.HBM(a.shape, a.dtype) for a in list(sums) + list(slots)], _TOKEN),
        input_output_aliases={t: 2 + t for t in range(2 * n)}, compiler_params=_CP_SPLIT)(
            *[_hbm(a) for a in list(sums) + list(slots)], *after)
    return outs[0], outs[1], list(outs[2:2 + n]), list(outs[2 + n:2 + 2 * n]), outs[-1]


def _chip_wait(sums, slots, send_sems, recv_sems, after, tag):
    n = len(sums)
    after = list(after) if isinstance(after, (list, tuple)) else [after]

    def body(*refs):
        src, slot, send_sems, recv_sems = refs[:n], refs[n:2 * n], refs[2 * n], refs[2 * n + 1]
        for k, mask in enumerate((4, 2, 6)):
            dev, _ = _peer(mask)
            chip = 2 * dev[0] + dev[1]
            for t in range(n):
                cp = pltpu.make_async_remote_copy(
                    src_ref=src[t].at[chip], dst_ref=slot[t].at[chip],
                    send_sem=send_sems.at[k * n + t], recv_sem=recv_sems.at[k * n + t],
                    device_id=_peer(0)[0], device_id_type=_MESH)
                cp.wait_send()
                cp.wait_recv()

    outs = pl.pallas_call(
        body, name=f"chip_wait_{tag}", in_specs=[_HBM] * (2 * n) + [_SEM, _SEM] + [_ANY] * len(after),
        out_specs=tuple([_HBM] * (2 * n)),
        out_shape=tuple(pltpu.HBM(a.shape, a.dtype) for a in list(sums) + list(slots)),
        input_output_aliases={t: t for t in range(2 * n)}, compiler_params=_CP_SPLIT)(
            *sums, *slots, send_sems, recv_sems, *after)
    return list(outs[:n]), list(outs[n:])


def _sum_slots(slots, rb):
    r = slots.shape[1]

    def body(s_ref, o_ref):
        acc = s_ref[0].astype(f32)
        for s in range(1, NDEV):
            acc = acc + s_ref[s].astype(f32)
        o_ref[...] = acc

    return pl.pallas_call(
        body, grid=(r // rb,),
        in_specs=[pl.BlockSpec((NDEV, rb, D), lambda i: (0, i, 0))],
        out_specs=pl.BlockSpec((rb, D), lambda i: (i, 0)),
        out_shape=SDS((r, D), f32), compiler_params=_CP, name="sum_slots")(slots)


def _adamw(w, g, m, v):
    shape = w.shape
    cols = shape[-1]
    rows = w.size // cols
    rb = rows
    for cand in (512, 256, 128, 64, 32, 16, 8):
        if rows % cand == 0 and rows > cand:
            rb = cand
            break

    def body(w_ref, g_ref, m_ref, v_ref, d_ref, mo_ref, vo_ref):
        d_ref[...], mo_ref[...], vo_ref[...] = _adamw_math(w_ref[...], g_ref[...], m_ref[...], v_ref[...])

    spec = pl.BlockSpec((rb, cols), lambda i: (i, 0))
    outs = pl.pallas_call(
        body, grid=(rows // rb,), in_specs=[spec] * 4, out_specs=[spec] * 3,
        out_shape=[SDS((rows, cols), f32)] * 3, compiler_params=_CP, name="adamw")(
            *(a.reshape(rows, cols) for a in (w, g, m, v)))
    return tuple(o.reshape(shape) for o in outs)


def _adamw_math(w, g, m, v):
    m = ADAM_B1 * m + (1.0 - ADAM_B1) * g
    v = ADAM_B2 * v + (1.0 - ADAM_B2) * (g * g)
    m_hat = m / (1.0 - ADAM_B1 ** ADAM_STEP)
    v_hat = v / (1.0 - ADAM_B2 ** ADAM_STEP)
    return -ADAM_LR * (m_hat / (jnp.sqrt(v_hat) + ADAM_EPS) + ADAM_WD * w), m, v


def _reduce_adamw(acc, me, full, slots, w, m, v, l):
    _, r, _ = w.shape
    ns = slots.shape[0]
    rb = r // 2 if r > 128 else r

    def body(me_ref, full_ref, slots_ref, w_ref, m_ref, v_ref, *refs):
        go_ref, d_ref, mo_ref, vo_ref = refs[-4:]
        own = full_ref[...].astype(f32)
        g = None
        for s in range(ns):
            part = jnp.where(me_ref[0] == s, own, slots_ref[s].astype(f32))
            g = part if g is None else g + part
        go_ref[...] = g
        d_ref[...], mo_ref[...], vo_ref[...] = _adamw_math(w_ref[...], g, m_ref[...], v_ref[...])

    steps = r // rb
    lay = pl.BlockSpec((None, rb, D), lambda i, me_ref: (l, i, 0))
    n_acc = 0 if acc is None else 4
    grid_spec = pltpu.PrefetchScalarGridSpec(
        num_scalar_prefetch=1, grid=(steps,),
        in_specs=[pl.BlockSpec((rb, D), lambda i, me_ref: (me_ref[0] * steps + i, 0)),
                  pl.BlockSpec((ns, rb, D), lambda i, me_ref: (0, i, 0)), lay, lay, lay] + [_ANY] * n_acc,
        out_specs=[lay] * 4)
    outs = pl.pallas_call(
        body, grid_spec=grid_spec, out_shape=[SDS(w.shape, f32)] * 4,
        input_output_aliases={6 + j: j for j in range(n_acc)},
        compiler_params=_CP, name="reduce_adamw")(me, full, slots, w, m, v, *(() if acc is None else acc))
    return tuple(outs)


_BIG = ("ffn1_w_gate", "ffn1_w_up", "ffn1_w_down", "w_in", "w_out", "ffn2_w_gate", "ffn2_w_up", "ffn2_w_down")
_TRANSPOSED = ("ffn1_w_gate", "ffn1_w_up", "w_in", "ffn2_w_gate", "ffn2_w_up")

def _block_diag(pool_w):
    out = jnp.zeros((L, PW, PW), pool_w.dtype)
    for gi in range(4):
        out = out.at[:, 64 * gi:64 * (gi + 1), 64 * gi:64 * (gi + 1)].set(pool_w[:, gi])
    return out


def kernel(x, positions, ffn1_norm, ffn1_w_gate, ffn1_w_up, ffn1_w_down, mix_norm, w_in, pool_w, pool_scale, w_out, ffn2_norm, ffn2_w_gate, ffn2_w_up, ffn2_w_down, final_norm, loss_target, m_ffn1_norm, m_ffn1_w_gate, m_ffn1_w_up, m_ffn1_w_down, m_mix_norm, m_w_in, m_pool_w, m_pool_scale, m_w_out, m_ffn2_norm, m_ffn2_w_gate, m_ffn2_w_up, m_ffn2_w_down, m_final_norm, v_ffn1_norm, v_ffn1_w_gate, v_ffn1_w_up, v_ffn1_w_down, v_mix_norm, v_w_in, v_pool_w, v_pool_scale, v_w_out, v_ffn2_norm, v_ffn2_w_gate, v_ffn2_w_up, v_ffn2_w_down, v_final_norm):
    weights = dict(ffn1_norm=ffn1_norm, ffn1_w_gate=ffn1_w_gate, ffn1_w_up=ffn1_w_up, ffn1_w_down=ffn1_w_down,
                   mix_norm=mix_norm, w_in=w_in, pool_w=pool_w, pool_scale=pool_scale, w_out=w_out,
                   ffn2_norm=ffn2_norm, ffn2_w_gate=ffn2_w_gate, ffn2_w_up=ffn2_w_up, ffn2_w_down=ffn2_w_down,
                   final_norm=final_norm)
    moms = dict(ffn1_norm=m_ffn1_norm, ffn1_w_gate=m_ffn1_w_gate, ffn1_w_up=m_ffn1_w_up, ffn1_w_down=m_ffn1_w_down,
                mix_norm=m_mix_norm, w_in=m_w_in, pool_w=m_pool_w, pool_scale=m_pool_scale, w_out=m_w_out,
                ffn2_norm=m_ffn2_norm, ffn2_w_gate=m_ffn2_w_gate, ffn2_w_up=m_ffn2_w_up, ffn2_w_down=m_ffn2_w_down,
                final_norm=m_final_norm)
    vels = dict(ffn1_norm=v_ffn1_norm, ffn1_w_gate=v_ffn1_w_gate, ffn1_w_up=v_ffn1_w_up, ffn1_w_down=v_ffn1_w_down,
                mix_norm=v_mix_norm, w_in=v_w_in, pool_w=v_pool_w, pool_scale=v_pool_scale, w_out=v_w_out,
                ffn2_norm=v_ffn2_norm, ffn2_w_gate=v_ffn2_w_gate, ffn2_w_up=v_ffn2_w_up, ffn2_w_down=v_ffn2_w_down,
                final_norm=v_final_norm)
    names = list(weights)

    me_idx = 4 * lax.axis_index("x") + 2 * lax.axis_index("y") + lax.axis_index("c")
    me_arr = me_idx.reshape(1).astype(jnp.int32)

    as_rows = lambda a, nm: jnp.swapaxes(a, 1, 2) if nm in _TRANSPOSED else a
    w_rows = {nm: as_rows(weights[nm], nm) for nm in _BIG}
    m_rows = {nm: as_rows(moms[nm], nm) for nm in _BIG}
    v_rows = {nm: as_rows(vels[nm], nm) for nm in _BIG}

    def landing_zones(l, which):
        return _place_own(me_arr, [w_rows[_BIG[t]] for t in which], l)

    g_ffn1 = [ffn1_norm[l].reshape(1, D) for l in range(L)]
    g_mix = [mix_norm[l].reshape(1, D) for l in range(L)]
    g_ffn2 = [ffn2_norm[l].reshape(1, D) for l in range(L)]
    wbd_all = _block_diag(pool_w).astype(bf16)
    wbd = [wbd_all[l] for l in range(L)]
    pscale = [pool_scale[l].reshape(1, PW) for l in range(L)]
    tabs = _rope_tables(positions)
    flat = lambda a: a.reshape(S, a.shape[-1])
    r4 = lambda a: a.reshape(4, S // 4, a.shape[-1])
    r16 = lambda a: a.reshape(16, S // 16, a.shape[-1])

    first, rest, whole = (0, 1, 2, 3), (4, 5, 6, 7), tuple(range(8))

    def ag_begin(l, which, after):
        tag = f"{l}{'' if which == whole else 'r'}"
        send_sems, recv_sems, zones, token = _ag_start(landing_zones(l, which), after, tag)
        return dict(tag=tag, zones=zones, s=send_sems, r=recv_sems), token

    def ag_second(ch, after):
        ch["ps"], ch["pr"], ch["zones"], token = _ag_pass(ch["zones"], ch["r"], after, ch["tag"])
        return token

    def ag_third(ch, after):
        ch["qs"], ch["qr"], ch["zones"], token = _ag_last(ch["zones"], ch["pr"], after, ch["tag"])
        return token

    def ag_end(ch, after):
        return _ag_wait(ch["zones"], ch["s"], ch["r"], ch["ps"], ch["pr"], ch["qs"], ch["qr"], after, ch["tag"])

    head = _all_gather(landing_zones(0, first))
    ch_rest, tok_rest = ag_begin(0, rest, head[0])
    chains = {}
    chains[1], tok_next = ag_begin(1, whole, head[0])
    gathered = [None] * L
    xs = x.reshape(S, D)
    saved = []
    for l in range(L):
        first_after, second_after = (), ()
        if l == 0:
            gt1, ut1, dn1, wint = head
            first_after = (tok_rest, tok_next)
        else:
            gt1, ut1, dn1, wint, wout, gt2, ut2, dn2 = gathered[l]
        x0 = xs
        x1, gate1, up1 = _ffn_fwd(x0, g_ffn1[l], gt1, ut1, dn1, after=first_after)
        hmix, vp, q1, k1, v1, q4, k4, v4, q16, k16, v16 = _mix_in_fwd(x1, g_mix[l], wint, tabs)
        q4, k4, v4, q16, k16, v16 = map(flat, (q4, k4, v4, q16, k16, v16))
        ypool, diff = _pool_fwd(vp, wbd[l], pscale[l])
        after_attn = None
        if l == 0:
            after_attn = ag_second(ch_rest, [ypool, q16])
        o1, l1 = _attn_fwd(q1, k1, v1, S, after=after_attn)
        o4, l4 = _attn_fwd(q4, k4, v4, S // 4, after=after_attn)
        o16, l16 = _attn_fwd(q16, k16, v16, S // 16, after=after_attn)
        if l == 0:
            token = ag_third(ch_rest, [o1, o4, o16])
            wout, gt2, ut2, dn2 = ag_end(ch_rest, token)
            gathered[0] = list(head) + [wout, gt2, ut2, dn2]
        elif l + 1 < L:
            second_after = (ag_second(chains[l + 1], [o1, o4, o16]),)
        x2, mixed, o, lse1, lse4, lse16 = _mix_out_fwd(x1, ypool, o1, l1, r4(o4), r4(l4), r16(o16), r16(l16), wout)
        if l == 0:
            second_after = (ag_second(chains[1], x2),)
        x3, gate2, up2 = _ffn_fwd(x2, g_ffn2[l], gt2, ut2, dn2, after=second_after)
        if l + 1 < L:
            token = ag_third(chains[l + 1], x3)
            if l + 2 < L:
                chains[l + 2], token = ag_begin(l + 2, whole, token)
            gathered[l + 1] = ag_end(chains[l + 1], token)
        saved.append(dict(x0=x0, x1=x1, x2=x2, gate1=gate1, up1=up1, gate2=gate2, up2=up2, hmix=hmix, diff=diff,
                          qkv=((q1, k1, v1), (q4, k4, v4), (q16, k16, v16)), mixed=mixed, o=o,
                          lse=(lse1, flat(lse4), flat(lse16))))
        xs = x3

    dx, loss_part, d_final = _loss_head(xs, final_norm.reshape(1, D), loss_target.reshape(S, D))

    d_norm = {nm: [None] * L for nm in ("ffn1_norm", "mix_norm", "ffn2_norm")}
    d_poolw, d_pscale = [None] * L, [None] * L
    group_a = ("ffn2_w_gate", "ffn2_w_up", "ffn2_w_down", "w_out")
    group_b = ("ffn1_w_gate", "ffn1_w_up", "ffn1_w_down", "w_in")
    acc = {}

    def exchange(full, group, after, tag):
        srcs = [full[nm] for nm in group]
        slots = [lax.empty((NDEV, g.shape[0] // NDEV, D), bf16) for g in srcs]
        ssem, rsem, srcs, slots, token = _rs_start(srcs, slots, after, tag)
        return (srcs, slots, ssem, rsem, tag), token

    def update(l, group, flight, after):
        srcs, slots, ssem, rsem, tag = flight
        srcs, slots = _rs_wait(srcs, slots, ssem, rsem, after, tag)
        for nm, full_g, slots_g in zip(group, srcs, slots):
            acc[nm] = _reduce_adamw(acc.get(nm), me_arr, full_g, slots_g, w_rows[nm], m_rows[nm], v_rows[nm], l)
        return [acc[nm][0] for nm in group], slots

    core_arr = lax.axis_index("c").reshape(1).astype(jnp.int32)
    chip_arr = (2 * lax.axis_index("x") + lax.axis_index("y")).reshape(1).astype(jnp.int32)

    def exchange_cores(full, group, after, tag):
        full4s = [full[nm].reshape(4, 2, full[nm].shape[0] // NDEV, D) for nm in group]
        bufs = [lax.empty((4,) + a.shape[2:], bf16) for a in full4s]
        ssem, rsem, full4s, bufs, token = _pair_start(full4s, bufs, after, tag)
        return (full4s, bufs, ssem, rsem, tag), token

    def exchange_chips(flight, after):
        full4s, bufs, ssem, rsem, tag = flight
        full4s, bufs = _pair_wait(full4s, bufs, ssem, rsem, after, tag)
        sums = _pair_sum(core_arr, full4s, bufs)
        slots = [lax.empty(a.shape, bf16) for a in sums]
        ssem, rsem, sums, slots, token = _chip_start(sums, slots, bufs[0], tag)
        return (sums, slots, ssem, rsem, tag), token

    def update_chips(l, group, flight, after):
        sums, slots, ssem, rsem, tag = flight
        sums, slots = _chip_wait(sums, slots, ssem, rsem, after, tag)
        for nm, sums_g, slots_g in zip(group, sums, slots):
            own = sums_g.reshape(4 * sums_g.shape[1], D)
            acc[nm] = _reduce_adamw(acc.get(nm), chip_arr, own, slots_g, w_rows[nm], m_rows[nm], v_rows[nm], l)
        return [acc[nm][0] for nm in group]

    flights = {}
    token_b = None
    for l in reversed(range(L)):
        sv = saved[l]
        gt1, ut1, dn1, wint, wout, gt2, ut2, dn2 = gathered[l]
        full = {}
        dx, dgate, dup, h, dy, d_norm["ffn2_norm"][l] = _ffn_bwd_d(
            sv["x2"], g_ffn2[l], sv["gate2"], sv["up2"], dx, gt2, ut2, dn2, after=() if token_b is None else (token_b,))
        full["ffn2_w_gate"], full["ffn2_w_up"], full["ffn2_w_down"] = _ffn_bwd_w(h, dy, sv["gate2"], sv["up2"], dgate, dup)

        dxb, dyp, do1, do4, do16, dl1, dl4, dl16 = _mix_out_bwd(dx, sv["o"], wout)
        full["w_out"] = _wgrad(sv["mixed"], dxb)
        flights[l, "a"], token_a = (exchange_cores if l == 0 else exchange)(full, group_a, dxb, f"a{l}")
        dvp, d_poolw[l], d_pscale[l] = _pool_bwd(dyp, sv["diff"], wbd[l], pscale[l], after=(token_a,))
        dos, dls = (do1, flat(do4), flat(do16)), (dl1, flat(dl4), flat(dl16))
        dqkv = []
        for b, lc in enumerate((S, S // 4, S // 16)):
            qb, kb, vb = sv["qkv"][b]
            dqkv.append(_attn_bwd(qb, kb, vb, dos[b], sv["lse"][b], dls[b], lc))
        d4 = tuple(r4(a) for a in dqkv[1])
        d16 = tuple(r16(a) for a in dqkv[2])
        mix_after = ()
        if l == 0:
            flights[0, "a"], token_a = exchange_chips(flights[0, "a"], [dqkv[0][0], dqkv[1][0], dqkv[2][0]])
            mix_after = (token_a,)
        dx, dproj, d_norm["mix_norm"][l] = _mix_in_bwd(dx, sv["x1"], g_mix[l], wint, tabs, dvp, dqkv[0], d4, d16,
                                                       after=mix_after)
        full["w_in"] = _wgrad(dproj, sv["hmix"])

        dx, dgate, dup, h, dy, d_norm["ffn1_norm"][l] = _ffn_bwd_d(sv["x0"], g_ffn1[l], sv["gate1"], sv["up1"], dx, gt1, ut1, dn1)
        full["ffn1_w_gate"], full["ffn1_w_up"], full["ffn1_w_down"] = _ffn_bwd_w(h, dy, sv["gate1"], sv["up1"], dgate, dup)

        after = dx
        if l + 1 < L and l + 1 >= 2:
            after, _ = update(l + 1, group_a, flights.pop((l + 1, "a")), after)
        if l + 1 < L and l + 1 >= 3:
            after, _ = update(l + 1, group_b, flights.pop((l + 1, "b")), after)
        if l > 0:
            flights[l, "b"], token_b = exchange(full, group_b, after, f"b{l}")

    flights[0, "b"], token_b = exchange_cores(full, group_b, dx, "b0")
    pad8 = lambda a: jnp.pad(a, ((0, 8 - a.shape[0]), (0, 0)))
    misc = jnp.concatenate([d_final, jnp.concatenate(d_pscale, axis=1), loss_part], axis=0)
    small = jnp.concatenate(
        [pad8(jnp.concatenate(d_norm[nm], axis=0)) for nm in ("ffn1_norm", "mix_norm", "ffn2_norm")]
        + [pad8(misc), jnp.stack(d_poolw).reshape(L * 16, D)], axis=0)
    small_slots = lax.dynamic_update_slice(lax.empty((NDEV, SMALL_ROWS, D), f32), small[None], (me_idx, 0, 0))
    pack_sems = _rs_start([small], [small_slots], token_b, "pack")
    flights[0, "b"], token_b = exchange_chips(flights[0, "b"], pack_sems[-1])

    after = token_b
    for key in [(2, "b"), (1, "a"), (1, "b")]:
        after, _ = update(key[0], group_a if key[1] == "a" else group_b, flights.pop(key), after)
    _, pack_slots = _rs_wait(pack_sems[2], pack_sems[3], pack_sems[0], pack_sems[1], after, "pack")
    sm = _sum_slots(pack_slots[0], SMALL_ROWS)
    grads = {}
    grads["ffn1_norm"], grads["mix_norm"], grads["ffn2_norm"] = sm[0:L], sm[8:8 + L], sm[16:16 + L]
    grads["final_norm"] = sm[24]
    grads["pool_scale"] = sm[25].reshape(L, PW)
    grads["pool_w"] = sm[32:32 + L * 16].reshape(L, 4, 64, 64)
    loss = sm[26, 0]
    upd = {nm: _adamw(weights[nm], grads[nm], moms[nm], vels[nm]) for nm in names if nm not in _BIG}
    after = update_chips(0, group_a, flights.pop((0, "a")), [upd[nm][0] for nm in upd])
    update_chips(0, group_b, flights.pop((0, "b")), after)
    for nm in _BIG:
        grads[nm], upd[nm] = as_rows(acc[nm][0], nm), tuple(as_rows(a, nm) for a in acc[nm][1:])
    return (loss, dx.reshape(1, S, D), *[grads[nm] for nm in names], *[upd[nm][0] for nm in names],
            *[upd[nm][1] for nm in names], *[upd[nm][2] for nm in names])
```

```python
import jax
import jax.numpy as jnp
from jax import lax
from jax.experimental import pallas as pl
from jax.experimental.pallas import tpu as pltpu

f32 = jnp.float32
bf16 = jnp.bfloat16
SDS = jax.ShapeDtypeStruct

D = 1024
S = 2048
F = 2816
L = 4
PW = 256
AW = 768
PROJ = PW + 3 * AW
NDEV = 8
TM = 256
QB = 128
HALF = 64
NG = AW // 128
NORM_EPS = 1e-6
MASK_VALUE = -1e30
ROPE_THETA = 500000.0
ADAM_LR, ADAM_B1, ADAM_B2, ADAM_EPS, ADAM_WD, ADAM_STEP = 0.001, 0.9, 0.999, 1e-08, 0.01, 10
POOL_WINDOWS = (2, 4, 8, 16)
PAD = 8
SMALL_ROWS = 96
VMEM_LIMIT = 56 * 1024 * 1024

_CP = pltpu.CompilerParams(vmem_limit_bytes=VMEM_LIMIT)
_ANY = pl.BlockSpec(memory_space=pl.ANY)
_HBM = pl.BlockSpec(memory_space=pltpu.HBM)
_SEM = pl.BlockSpec(memory_space=pltpu.SEMAPHORE)
_MESH = pl.DeviceIdType.MESH
_CP_SPLIT = pltpu.CompilerParams(has_side_effects=pltpu.SideEffectType.DATAFLOW_SIDE_EFFECTING)


def _dot_nn(a, b):
    return lax.dot_general(a, b, (((1,), (0,)), ((), ())), preferred_element_type=f32)


def _dot_nt(a, b):
    return lax.dot_general(a, b, (((1,), (1,)), ((), ())), preferred_element_type=f32)


def _dot_tn(a, b):
    return lax.dot_general(a, b, (((0,), (0,)), ((), ())), preferred_element_type=f32)


def _rms(x, g):
    r = lax.rsqrt(jnp.mean(x * x, axis=-1, keepdims=True) + NORM_EPS)
    xh = x * r
    return r, xh, xh * g


def _rms_bwd(dh, r, xh, g):
    dxh = dh * g
    return r * (dxh - xh * jnp.mean(dxh * xh, axis=-1, keepdims=True))


def _tile(cols, rows=TM):
    return pl.BlockSpec((rows, cols), lambda i: (i, 0))


def _const(shape):
    return pl.BlockSpec(shape, lambda i: (0,) * len(shape))


def _layer(rows, cols):
    return pl.BlockSpec((rows, cols), lambda i: (0, 0), pipeline_mode=pl.Buffered(1))


def _p4(cols=AW):
    return pl.BlockSpec((4, TM // 4, cols), lambda i: (0, i, 0))


def _p16(cols=AW):
    return pl.BlockSpec((16, TM // 16, cols), lambda i: (0, i, 0))


def _cols(j):
    return slice(128 * j, 128 * (j + 1))


def _follow(body, n_in, after):
    k = len(after)
    return body if k == 0 else (lambda *refs: body(*refs[:n_in], *refs[n_in + k:]))


def _ffn_fwd(x, g, gt, ut, dn, after=()):
    def body(x_ref, g_ref, gt_ref, ut_ref, dn_ref, xo_ref, gate_ref, up_ref):
        x = x_ref[...]
        _, _, hn = _rms(x, g_ref[...])
        h = hn.astype(bf16)
        gate = _dot_nt(h, gt_ref[...])
        up = _dot_nt(h, ut_ref[...])
        gate_ref[...] = gate.astype(bf16)
        up_ref[...] = up.astype(bf16)
        a = (gate * jax.nn.sigmoid(gate) * up).astype(bf16)
        xo_ref[...] = x + 0.5 * _dot_nn(a, dn_ref[...])

    rows = 2 * TM
    return pl.pallas_call(
        _follow(body, 5, after), grid=(S // rows,),
        in_specs=[_tile(D, rows), _layer(1, D), _layer(F, D), _layer(F, D), _layer(F, D)] + [_ANY] * len(after),
        out_specs=[_tile(D, rows), _tile(F, rows), _tile(F, rows)],
        out_shape=[SDS((S, D), f32), SDS((S, F), bf16), SDS((S, F), bf16)],
        compiler_params=_CP, name="ffn_fwd")(x, g, gt, ut, dn, *after)


def _ffn_bwd_d(x, g, gate, up, dxo, gt, ut, dn, after=()):
    def body(x_ref, g_ref, gate_ref, up_ref, dxo_ref, gt_ref, ut_ref, dn_ref,
             dx_ref, dgate_ref, dup_ref, h_ref, dy_ref, dg_ref):
        x = x_ref[...]
        g = g_ref[...]
        r, xh, hn = _rms(x, g)
        h_ref[...] = hn.astype(bf16)
        dxo = dxo_ref[...]
        dy = (0.5 * dxo).astype(bf16)
        dy_ref[...] = dy
        da = _dot_nt(dy, dn_ref[...])
        gate = gate_ref[...].astype(f32)
        up = up_ref[...].astype(f32)
        sg = jax.nn.sigmoid(gate)
        dgate = (da * up * (sg * (1.0 + gate * (1.0 - sg)))).astype(bf16)
        dup = (da * (gate * sg)).astype(bf16)
        dgate_ref[...] = dgate
        dup_ref[...] = dup
        dh = _dot_nn(dgate, gt_ref[...]) + _dot_nn(dup, ut_ref[...])

        @pl.when(pl.program_id(0) == 0)
        def _():
            dg_ref[...] = jnp.zeros_like(dg_ref)

        dg_ref[...] += jnp.sum(dh * xh, axis=0, keepdims=True)
        dx_ref[...] = dxo + _rms_bwd(dh, r, xh, g)

    return pl.pallas_call(
        _follow(body, 8, after), grid=(S // TM,),
        in_specs=[_tile(D), _layer(1, D), _tile(F), _tile(F), _tile(D),
                  _layer(F, D), _layer(F, D), _layer(F, D)] + [_ANY] * len(after),
        out_specs=[_tile(D), _tile(F), _tile(F), _tile(D), _tile(D), _const((1, D))],
        out_shape=[SDS((S, D), f32), SDS((S, F), bf16), SDS((S, F), bf16), SDS((S, D), bf16),
                   SDS((S, D), bf16), SDS((1, D), f32)],
        compiler_params=_CP, name="ffn_bwd_d")(x, g, gate, up, dxo, gt, ut, dn, *after)


def _ffn_bwd_w(h, dy, gate, up, dgate, dup):
    fc = 256

    def body(h_ref, dy_ref, gate_ref, up_ref, dgate_ref, dup_ref, dgt_ref, dut_ref, ddn_ref):
        gate = gate_ref[...].astype(f32)
        a = (gate * jax.nn.sigmoid(gate) * up_ref[...].astype(f32)).astype(bf16)
        ddn_ref[...] = _dot_tn(a, dy_ref[...]).astype(bf16)
        h = h_ref[...]
        dgt_ref[...] = _dot_tn(dgate_ref[...], h).astype(bf16)
        dut_ref[...] = _dot_tn(dup_ref[...], h).astype(bf16)

    col = pl.BlockSpec((S, fc), lambda j: (0, j))
    row = pl.BlockSpec((fc, D), lambda j: (j, 0))
    full = pl.BlockSpec((S, D), lambda j: (0, 0))
    return pl.pallas_call(
        body, grid=(F // fc,),
        in_specs=[full, full, col, col, col, col],
        out_specs=[row, row, row],
        out_shape=[SDS((F, D), bf16)] * 3,
        compiler_params=_CP, name="ffn_bwd_w")(h, dy, gate, up, dgate, dup)


def _wgrad(a, b):
    m, n = a.shape[1], b.shape[1]
    mc = 2 * TM

    def body(a_ref, b_ref, o_ref):
        o_ref[...] = _dot_tn(a_ref[...], b_ref[...]).astype(bf16)

    return pl.pallas_call(
        body, grid=(m // mc,),
        in_specs=[pl.BlockSpec((S, mc), lambda j: (0, j)), pl.BlockSpec((S, n), lambda j: (0, 0))],
        out_specs=pl.BlockSpec((mc, n), lambda j: (j, 0)),
        out_shape=SDS((m, n), bf16),
        compiler_params=_CP, name="wgrad")(a, b)


def _rope(t, c, sn, sp):
    return t * c + pltpu.roll(t, 120, 1) * sn + pltpu.roll(t, 8, 1) * sp


def _rope_bwd(d, c, sn, sp):
    return d * c + pltpu.roll(d * sn, 8, 1) + pltpu.roll(d * sp, 120, 1)


def _rope_tables(positions):
    inv_freq = ROPE_THETA ** (-jnp.arange(0, 16, 2, dtype=f32) / 16)
    ang = positions.reshape(S, 1).astype(f32) * inv_freq
    cos, sin = jnp.cos(ang), jnp.sin(ang)
    one = jnp.ones((S, 48), f32)
    zero8 = jnp.zeros((S, 8), f32)
    zero48 = jnp.zeros((S, 48), f32)
    c = jnp.concatenate([cos, cos, one], axis=1)
    sn = jnp.concatenate([-sin, zero8, zero48], axis=1)
    sp = jnp.concatenate([zero8, sin, zero48], axis=1)
    return tuple(jnp.concatenate([t, t], axis=1) for t in (c, sn, sp))


def _dilation_perm(n, back=False):
    per = TM // n
    i = lax.broadcasted_iota(jnp.int32, (TM, TM), 1 if back else 0)
    j = lax.broadcasted_iota(jnp.int32, (TM, TM), 0 if back else 1)
    return jnp.where(j == n * (i % per) + i // per, 1.0, 0.0).astype(bf16)


def _mix_in_fwd(x, g, wint, tabs):
    def body(x_ref, g_ref, w_ref, c_ref, sn_ref, sp_ref,
             h_ref, vp_ref, q1, k1, v1, q4, k4, v4, q16, k16, v16):
        _, _, hn = _rms(x_ref[...], g_ref[...])
        h = hn.astype(bf16)
        h_ref[...] = h
        proj = _dot_nt(h, w_ref[...])
        vp_ref[...] = proj[:, :PW]
        c, sn, sp = c_ref[...], sn_ref[...], sp_ref[...]
        perm4, perm16 = _dilation_perm(4), _dilation_perm(16)
        for kind, (o1, o4, o16) in enumerate(((q1, q4, q16), (k1, k4, k16), (v1, v4, v16))):
            for j in range(NG):
                t = proj[:, PW + kind * AW + 128 * j: PW + kind * AW + 128 * (j + 1)]
                if kind == 0:
                    t = _rope(t, c, sn, sp) * 0.125
                elif kind == 1:
                    t = _rope(t, c, sn, sp)
                o1[:, _cols(j)] = t.astype(bf16)
            nat = o1[...]
            o4[...] = _dot_nn(perm4, nat).astype(bf16).reshape(4, TM // 4, AW)
            o16[...] = _dot_nn(perm16, nat).astype(bf16).reshape(16, TM // 16, AW)

    nat, d4, d16 = SDS((S, AW), bf16), SDS((4, S // 4, AW), bf16), SDS((16, S // 16, AW), bf16)
    return pl.pallas_call(
        body, grid=(S // TM,),
        in_specs=[_tile(D), _layer(1, D), _layer(PROJ, D), _tile(128), _tile(128), _tile(128)],
        out_specs=[_tile(D), _tile(PW)] + [_tile(AW)] * 3 + [_p4()] * 3 + [_p16()] * 3,
        out_shape=[SDS((S, D), bf16), SDS((S, PW), f32)] + [nat] * 3 + [d4] * 3 + [d16] * 3,
        compiler_params=_CP, name="mix_in_fwd")(x, g, wint, *tabs)


def _mix_in_bwd(dxo, x, g, wint, tabs, dvp, d1, d4, d16, after=()):
    def body(dxo_ref, x_ref, g_ref, w_ref, c_ref, sn_ref, sp_ref, dvp_ref,
             dq1, dk1, dv1, dq4, dk4, dv4, dq16, dk16, dv16,
             dx_ref, dproj_ref, dg_ref):
        c, sn, sp = c_ref[...], sn_ref[...], sp_ref[...]
        dproj_ref[:, :PW] = dvp_ref[...].astype(bf16)
        back4, back16 = _dilation_perm(4, True), _dilation_perm(16, True)
        for kind, (a1, a4, a16) in enumerate(((dq1, dq4, dq16), (dk1, dk4, dk16), (dv1, dv4, dv16))):
            n4 = _dot_nn(back4, a4[...].reshape(TM, AW))
            n16 = _dot_nn(back16, a16[...].reshape(TM, AW))
            for j in range(NG):
                t = a1[:, _cols(j)].astype(f32) + n4[:, _cols(j)] + n16[:, _cols(j)]
                if kind == 0:
                    t = _rope_bwd(t * 0.125, c, sn, sp)
                elif kind == 1:
                    t = _rope_bwd(t, c, sn, sp)
                dproj_ref[:, PW + kind * AW + 128 * j: PW + kind * AW + 128 * (j + 1)] = t.astype(bf16)
        g = g_ref[...]
        r_, xh, _ = _rms(x_ref[...], g)
        dh = _dot_nn(dproj_ref[...], w_ref[...])

        @pl.when(pl.program_id(0) == 0)
        def _():
            dg_ref[...] = jnp.zeros_like(dg_ref)

        dg_ref[...] += jnp.sum(dh * xh, axis=0, keepdims=True)
        dx_ref[...] = dxo_ref[...] + _rms_bwd(dh, r_, xh, g)

    return pl.pallas_call(
        _follow(body, 17, after), grid=(S // TM,),
        in_specs=[_tile(D), _tile(D), _layer(1, D), _layer(PROJ, D), _tile(128), _tile(128), _tile(128),
                  _tile(PW)] + [_tile(AW)] * 3 + [_p4()] * 3 + [_p16()] * 3 + [_ANY] * len(after),
        out_specs=[_tile(D), _tile(PROJ), _const((1, D))],
        out_shape=[SDS((S, D), f32), SDS((S, PROJ), bf16), SDS((1, D), f32)],
        compiler_params=_CP, name="mix_in_bwd")(dxo, x, g, wint, *tabs, dvp, *d1, *d4, *d16, *after)


def _pool_sums(pad_ref, base, rows, adjoint):
    lane_group = lax.broadcasted_iota(jnp.int32, (rows, PW), 1) // 64
    sign = -1 if adjoint else 1

    def sh(o):
        return pad_ref[pl.ds(PAD + base + sign * o, rows), :]

    out = None
    acc = None
    lo, hi = 0, 0
    for gi, w in enumerate(POOL_WINDOWS):
        for o in list(range(-(w // 2), lo)) + list(range(hi, w - w // 2)):
            acc = sh(o) if acc is None else acc + sh(o)
        lo, hi = -(w // 2), w - w // 2
        out = acc if out is None else jnp.where(lane_group >= gi, acc, out)
    return out


def _pool_counts(base, rows):
    pos = base + lax.broadcasted_iota(jnp.int32, (rows, PW), 0)
    lane_group = lax.broadcasted_iota(jnp.int32, (rows, PW), 1) // 64
    cnt = None
    for gi, w in enumerate(POOL_WINDOWS):
        lo = jnp.maximum(pos - w // 2, 0)
        hi = jnp.minimum(pos + w - 1 - w // 2, S - 1)
        c = (hi - lo + 1).astype(f32)
        cnt = c if cnt is None else jnp.where(lane_group >= gi, c, cnt)
    return cnt


def _pool_fwd(vp, wbd, scale):
    ch = 256

    def body(vp_ref, w_ref, sc_ref, y_ref, diff_ref, pad):
        pad[pl.ds(0, PAD), :] = jnp.zeros((PAD, PW), f32)
        pad[pl.ds(PAD + S, PAD), :] = jnp.zeros((PAD, PW), f32)
        pad[pl.ds(PAD, S), :] = vp_ref[...]
        for b in range(S // ch):
            base = b * ch
            pooled = _pool_sums(pad, base, ch, False) / _pool_counts(base, ch)
            diff = (pooled - vp_ref[pl.ds(base, ch), :]).astype(bf16)
            diff_ref[pl.ds(base, ch), :] = diff
            y_ref[pl.ds(base, ch), :] = _dot_nn(diff, w_ref[...]) * sc_ref[...]

    whole = lambda shape: pl.BlockSpec(shape, lambda i: (0,) * len(shape))
    return pl.pallas_call(
        body, grid=(1,),
        in_specs=[whole((S, PW)), whole((PW, PW)), whole((1, PW))],
        out_specs=[whole((S, PW)), whole((S, PW))],
        out_shape=[SDS((S, PW), f32), SDS((S, PW), bf16)],
        scratch_shapes=[pltpu.VMEM((S + 2 * PAD, PW), f32)],
        compiler_params=_CP, name="pool_fwd")(vp, wbd, scale)


def _pool_bwd(dy, diff, wbd, scale, after=()):
    ch = 256

    def body(dy_ref, diff_ref, w_ref, sc_ref, dvp_ref, dw_ref, dsc_ref, pad):
        pad[pl.ds(0, PAD), :] = jnp.zeros((PAD, PW), f32)
        pad[pl.ds(PAD + S, PAD), :] = jnp.zeros((PAD, PW), f32)
        dw = jnp.zeros((PW, PW), f32)
        dsc = jnp.zeros((1, PW), f32)
        for b in range(S // ch):
            base = b * ch
            dy = dy_ref[pl.ds(base, ch), :]
            diff = diff_ref[pl.ds(base, ch), :]
            dsc = dsc + jnp.sum(dy * _dot_nn(diff, w_ref[...]), axis=0, keepdims=True)
            dz = (dy * sc_ref[...]).astype(bf16)
            dw = dw + _dot_tn(diff, dz)
            ddiff = _dot_nt(dz, w_ref[...])
            dvp_ref[pl.ds(base, ch), :] = -ddiff
            pad[pl.ds(PAD + base, ch), :] = ddiff / _pool_counts(base, ch)
        for gi in range(4):
            dw_ref[gi] = dw[64 * gi:64 * (gi + 1), 64 * gi:64 * (gi + 1)]
        dsc_ref[...] = dsc
        for b in range(S // ch):
            base = b * ch
            dvp_ref[pl.ds(base, ch), :] += _pool_sums(pad, base, ch, True)

    whole = lambda shape: pl.BlockSpec(shape, lambda i: (0,) * len(shape))
    return pl.pallas_call(
        _follow(body, 4, after), grid=(1,),
        in_specs=[whole((S, PW)), whole((S, PW)), whole((PW, PW)), whole((1, PW))] + [_ANY] * len(after),
        out_specs=[whole((S, PW)), whole((4, 64, 64)), whole((1, PW))],
        out_shape=[SDS((S, PW), f32), SDS((4, 64, 64), f32), SDS((1, PW), f32)],
        scratch_shapes=[pltpu.VMEM((S + 2 * PAD, PW), f32)],
        compiler_params=_CP, name="pool_bwd")(dy, diff, wbd, scale, *after)


def _attn_blocks(lc):
    bpc = lc // QB
    kw = min(2 * QB, lc)
    blocks = []
    for b in range(S // QB):
        t0 = (b % bpc) * QB
        ks_in = min(max(t0 - HALF, 0), lc - kw)
        blocks.append((b * QB, (b // bpc) * lc + ks_in, t0 - ks_in))
    return kw, blocks


def _attn_bias(bias_ref, kw, shifts):
    r = lax.broadcasted_iota(jnp.int32, (2 * QB, kw), 0) % QB
    c = lax.broadcasted_iota(jnp.int32, (2 * QB, kw), 1)
    for i, shift in enumerate(shifts):
        bias_ref[i] = jnp.where(jnp.abs(r + shift - c) <= HALF, 0.0, MASK_VALUE).astype(f32)


def _head_put(stats, pair, v0, v1, lane):
    return jnp.where(lane == 2 * pair, v0, jnp.where(lane == 2 * pair + 1, v1, stats))


def _head_cols(stats, pair, lane):
    c0 = jnp.sum(jnp.where(lane == 2 * pair, stats, 0.0), axis=-1, keepdims=True)
    c1 = jnp.sum(jnp.where(lane == 2 * pair + 1, stats, 0.0), axis=-1, keepdims=True)
    return jnp.concatenate([c0, c1], axis=0)


def _head_spread(stats, pair, head0):
    return jnp.where(head0, stats[:, 2 * pair:2 * pair + 1], stats[:, 2 * pair + 1:2 * pair + 2])


def _stack_heads(blk, head0):
    zero = jnp.zeros_like(blk)
    return jnp.concatenate([jnp.where(head0, blk, zero), jnp.where(head0, zero, blk)], axis=0)


def _attn_fwd(q, k, v, lc, after=None):
    kw, blocks = _attn_blocks(lc)
    shifts = sorted({b[2] for b in blocks})

    def body(q_ref, k_ref, v_ref, *refs):
        o_ref, lse_ref, bias_ref = refs[-3:]
        lane = lax.broadcasted_iota(jnp.int32, (QB, 128), 1)
        head0 = lane < 64
        pair = pl.program_id(0)
        _attn_bias(bias_ref, kw, shifts)

        @pl.when(pair == 0)
        def _():
            lse_ref[...] = jnp.zeros_like(lse_ref)

        for row0, kstart, shift in blocks:
            q2 = _stack_heads(q_ref[pl.ds(row0, QB), :], head0)
            kb = k_ref[pl.ds(kstart, kw), :]
            vb = v_ref[pl.ds(kstart, kw), :]
            s = _dot_nt(q2, kb) + bias_ref[shifts.index(shift)]
            m = jnp.max(s, axis=-1, keepdims=True)
            p = jnp.exp(s - m)
            den = jnp.sum(p, axis=-1, keepdims=True)
            o2 = _dot_nn(p.astype(bf16), vb) / den
            lse2 = m + jnp.log(den)
            o_ref[pl.ds(row0, QB), :] = jnp.where(head0, o2[:QB], o2[QB:]).astype(bf16)
            lse_ref[pl.ds(row0, QB), :] = _head_put(lse_ref[pl.ds(row0, QB), :], pair, lse2[:QB], lse2[QB:], lane)

    col = pl.BlockSpec((S, 128), lambda p: (0, p))
    extra = () if after is None else (after,)
    return pl.pallas_call(
        body, grid=(NG,), in_specs=[col, col, col] + [_ANY] * len(extra),
        out_specs=[col, pl.BlockSpec((S, 128), lambda p: (0, 0))],
        out_shape=[SDS((S, AW), bf16), SDS((S, 128), f32)],
        scratch_shapes=[pltpu.VMEM((len(shifts), 2 * QB, kw), f32)],
        compiler_params=_CP, name=f"attn_fwd_{lc}")(q, k, v, *extra)


def _attn_bwd(q, k, v, do, lse, delta, lc):
    kw, blocks = _attn_blocks(lc)
    shifts = sorted({b[2] for b in blocks})

    def body(q_ref, k_ref, v_ref, do_ref, lse_ref, dl_ref, dq_ref, dk_out, dv_out, bias_ref, dk_ref, dv_ref):
        lane = lax.broadcasted_iota(jnp.int32, (QB, 128), 1)
        head0 = lane < 64
        pair = pl.program_id(0)
        _attn_bias(bias_ref, kw, shifts)
        dk_ref[...] = jnp.zeros_like(dk_ref)
        dv_ref[...] = jnp.zeros_like(dv_ref)
        for row0, kstart, shift in blocks:
            q2 = _stack_heads(q_ref[pl.ds(row0, QB), :], head0)
            do2 = _stack_heads(do_ref[pl.ds(row0, QB), :], head0)
            lse2 = _head_cols(lse_ref[pl.ds(row0, QB), :], pair, lane)
            dl2 = _head_cols(dl_ref[pl.ds(row0, QB), :], pair, lane)
            kb = k_ref[pl.ds(kstart, kw), :]
            vb = v_ref[pl.ds(kstart, kw), :]
            p = jnp.exp(_dot_nt(q2, kb) + bias_ref[shifts.index(shift)] - lse2)
            ds = (p * (_dot_nt(do2, vb) - dl2)).astype(bf16)
            dq2 = _dot_nn(ds, kb)
            dq_ref[pl.ds(row0, QB), :] = jnp.where(head0, dq2[:QB], dq2[QB:]).astype(bf16)
            dk_ref[pl.ds(kstart, kw), :] += _dot_tn(ds, q2)
            dv_ref[pl.ds(kstart, kw), :] += _dot_tn(p.astype(bf16), do2)
        dk_out[...] = dk_ref[...].astype(bf16)
        dv_out[...] = dv_ref[...].astype(bf16)

    col = pl.BlockSpec((S, 128), lambda p: (0, p))
    stats = pl.BlockSpec((S, 128), lambda p: (0, 0))
    return pl.pallas_call(
        body, grid=(NG,), in_specs=[col] * 4 + [stats] * 2, out_specs=[col] * 3,
        out_shape=[SDS((S, AW), bf16)] * 3,
        scratch_shapes=[pltpu.VMEM((len(shifts), 2 * QB, kw), f32), pltpu.VMEM((S, 128), f32),
                        pltpu.VMEM((S, 128), f32)],
        compiler_params=_CP, name=f"attn_bwd_{lc}")(q, k, v, do, lse, delta)


def _mix_out_fwd(x, ypool, o1, l1, o4, l4, o16, l16, wout):
    def body(x_ref, yp_ref, o1_ref, l1_ref, o4_ref, l4_ref, o16_ref, l16_ref, w_ref,
             xo_ref, mixed_ref, o_ref, lse1_ref, lse4_ref, lse16_ref, sl4, sl16, sl):
        head0 = lax.broadcasted_iota(jnp.int32, (TM, 128), 1) < 64
        for r in range(4):
            sl4[pl.ds(r, TM // 4, stride=4), :] = l4_ref[r]
        for r in range(16):
            sl16[pl.ds(r, TM // 16, stride=16), :] = l16_ref[r]
        n4 = _dot_nn(_dilation_perm(4, True), o4_ref[...].reshape(TM, AW))
        n16 = _dot_nn(_dilation_perm(16, True), o16_ref[...].reshape(TM, AW))
        a, b, c = l1_ref[...], sl4[...], sl16[...]
        m = jnp.maximum(jnp.maximum(a, b), c)
        wa, wb, wc = jnp.exp(a - m), jnp.exp(b - m), jnp.exp(c - m)
        den = wa + wb + wc
        wa, wb, wc = wa / den, wb / den, wc / den
        lse = m + jnp.log(den)
        lse1_ref[...] = lse
        sl[...] = lse
        mixed_ref[:, :PW] = yp_ref[...].astype(bf16)
        for j in range(NG):
            y = (_head_spread(wa, j, head0) * o1_ref[:, _cols(j)].astype(f32)
                 + _head_spread(wb, j, head0) * n4[:, _cols(j)] + _head_spread(wc, j, head0) * n16[:, _cols(j)])
            o_ref[:, _cols(j)] = y
            mixed_ref[:, PW + 128 * j: PW + 128 * (j + 1)] = y.astype(bf16)
        for r in range(4):
            lse4_ref[r] = sl[pl.ds(r, TM // 4, stride=4), :]
        for r in range(16):
            lse16_ref[r] = sl[pl.ds(r, TM // 16, stride=16), :]
        xo_ref[...] = x_ref[...] + _dot_nn(mixed_ref[...], w_ref[...])

    return pl.pallas_call(
        body, grid=(S // TM,),
        in_specs=[_tile(D), _tile(PW), _tile(AW), _tile(128), _p4(), _p4(128), _p16(), _p16(128), _layer(D, D)],
        out_specs=[_tile(D), _tile(D), _tile(AW), _tile(128), _p4(128), _p16(128)],
        out_shape=[SDS((S, D), f32), SDS((S, D), bf16), SDS((S, AW), f32), SDS((S, 128), f32),
                   SDS((4, S // 4, 128), f32), SDS((16, S // 16, 128), f32)],
        scratch_shapes=[pltpu.VMEM((TM, 128), f32)] * 3,
        compiler_params=_CP, name="mix_out_fwd")(x, ypool, o1, l1, o4, l4, o16, l16, wout)


def _mix_out_bwd(dxo, o, wout):
    def body(dxo_ref, o_ref, w_ref, dxb_ref, dyp_ref, do1, do4, do16, dl1, dl4, dl16, sdl):
        dxb = dxo_ref[...].astype(bf16)
        dxb_ref[...] = dxb
        dm = _dot_nt(dxb, w_ref[...])
        dyp_ref[...] = dm[:, :PW]
        lane = lax.broadcasted_iota(jnp.int32, (TM, 128), 1)
        head0 = lane < 64
        dl = jnp.zeros((TM, 128), f32)
        for j in range(NG):
            d = dm[:, PW + 128 * j: PW + 128 * (j + 1)]
            prod = d * o_ref[:, _cols(j)]
            dl = _head_put(dl, j, jnp.sum(jnp.where(head0, prod, 0.0), axis=-1, keepdims=True),
                           jnp.sum(jnp.where(head0, 0.0, prod), axis=-1, keepdims=True), lane)
            do1[:, _cols(j)] = d.astype(bf16)
        dl1[...] = dl
        sdl[...] = dl
        for r in range(4):
            dl4[r] = sdl[pl.ds(r, TM // 4, stride=4), :]
        for r in range(16):
            dl16[r] = sdl[pl.ds(r, TM // 16, stride=16), :]
        nat = do1[...]
        do4[...] = _dot_nn(_dilation_perm(4), nat).astype(bf16).reshape(4, TM // 4, AW)
        do16[...] = _dot_nn(_dilation_perm(16), nat).astype(bf16).reshape(16, TM // 16, AW)

    return pl.pallas_call(
        body, grid=(S // TM,),
        in_specs=[_tile(D), _tile(AW), _layer(D, D)],
        out_specs=[_tile(D), _tile(PW), _tile(AW), _p4(), _p16(), _tile(128), _p4(128), _p16(128)],
        out_shape=[SDS((S, D), bf16), SDS((S, PW), f32),
                   SDS((S, AW), bf16), SDS((4, S // 4, AW), bf16), SDS((16, S // 16, AW), bf16),
                   SDS((S, 128), f32), SDS((4, S // 4, 128), f32), SDS((16, S // 16, 128), f32)],
        scratch_shapes=[pltpu.VMEM((TM, 128), f32)],
        compiler_params=_CP, name="mix_out_bwd")(dxo, o, wout)


def _loss_head(x, g, target):
    def body(x_ref, g_ref, t_ref, dx_ref, loss_ref, dg_ref):
        g = g_ref[...]
        r, xh, y = _rms(x_ref[...], g)
        err = y - t_ref[...]
        dy = err * (1.0 / D)

        @pl.when(pl.program_id(0) == 0)
        def _():
            loss_ref[...] = jnp.zeros_like(loss_ref)
            dg_ref[...] = jnp.zeros_like(dg_ref)

        loss_ref[...] += jnp.broadcast_to(0.5 * jnp.sum(jnp.mean(err * err, axis=-1, keepdims=True)), (1, D))
        dg_ref[...] += jnp.sum(dy * xh, axis=0, keepdims=True)
        dx_ref[...] = _rms_bwd(dy, r, xh, g)

    return pl.pallas_call(
        body, grid=(S // TM,),
        in_specs=[_tile(D), _const((1, D)), _tile(D)],
        out_specs=[_tile(D), _const((1, D)), _const((1, D))],
        out_shape=[SDS((S, D), f32), SDS((1, D), f32), SDS((1, D), f32)],
        compiler_params=_CP, name="loss_head")(x, g, target)


def _peer(k):
    x, y, c = lax.axis_index("x"), lax.axis_index("y"), lax.axis_index("c")
    px = 1 - x if k & 4 else x
    py = 1 - y if k & 2 else y
    pc = 1 - c if k & 1 else c
    return (px, py, pc), 4 * px + 2 * py + pc


def _diag_route():
    x, y, c = lax.axis_index("x"), lax.axis_index("y"), lax.axis_index("c")
    idx_x, idx_y = _peer(4)[1], _peer(2)[1]
    return idx_x + c * (idx_y - idx_x), (x + c * (1 - 2 * x), (1 - y) + c * (2 * y - 1), c)


def _all_gather(lands):
    n = len(lands)

    def body(*refs):
        zones, send_sems, recv_sems = refs[n:2 * n], refs[2 * n], refs[2 * n + 1]
        me, me_idx = _peer(0)
        sibling, sib_idx = _peer(1)
        (x_nbr, idx_x), (y_nbr, idx_y), idx_d = _peer(4), _peer(2), _peer(6)[1]
        fwd_idx, fwd_dev = _diag_route()

        def copy(k, t, idx, to):
            return _row_copy(zones[t], idx, send_sems.at[k, t], recv_sems.at[k, t], to)

        sent = []

        def send(k, t, idx, to):
            cp = copy(k, t, idx, to)
            cp.start()
            sent.append(cp)

        for t in range(n):
            send(0, t, me_idx, sibling)
            send(1, t, me_idx, x_nbr)
            send(2, t, me_idx, y_nbr)
        for t in range(n):
            copy(1, t, idx_x, me).wait_recv()
            send(3, t, idx_x, sibling)
        for t in range(n):
            copy(2, t, idx_y, me).wait_recv()
            send(4, t, idx_y, sibling)
        for t in range(n):
            send(5, t, fwd_idx, fwd_dev)
        for t in range(n):
            copy(5, t, idx_d, me).wait_recv()
            send(6, t, idx_d, sibling)
        for k, mask in ((0, 1), (3, 5), (4, 3), (6, 7)):
            for t in range(n):
                copy(k, t, _peer(mask)[1], me).wait_recv()
        for cp in sent:
            cp.wait_send()

    return pl.pallas_call(
        body, in_specs=[_ANY] * n, out_specs=[_ANY] * n,
        out_shape=[SDS(a.shape, a.dtype) for a in lands], input_output_aliases={t: t for t in range(n)},
        scratch_shapes=[pltpu.SemaphoreType.DMA((7, n)), pltpu.SemaphoreType.DMA((7, n))],
        name="all_gather_weights")(*lands)


def _hbm(a):
    return pltpu.with_memory_space_constraint(a, pltpu.HBM)


def _rows(ref, idx):
    r = ref.shape[0] // NDEV
    return ref.at[pl.ds(idx * r, r), :]


def _row_copy(ref, idx, send_sem, recv_sem, to):
    return pltpu.make_async_remote_copy(src_ref=_rows(ref, idx), dst_ref=_rows(ref, idx), send_sem=send_sem,
                                        recv_sem=recv_sem, device_id=to, device_id_type=_MESH)


def _place_own(me, shards, l):
    n = len(shards)

    def body(me_ref, *refs):
        for t in range(n):
            refs[n + t][...] = refs[t][...].astype(bf16)

    grid_spec = pltpu.PrefetchScalarGridSpec(
        num_scalar_prefetch=1, grid=(1,),
        in_specs=[pl.BlockSpec((None, s.shape[1], D), lambda i, me_ref: (l, 0, 0)) for s in shards],
        out_specs=[pl.BlockSpec((s.shape[1], D), lambda i, me_ref: (me_ref[0], 0)) for s in shards])
    return pl.pallas_call(
        body, grid_spec=grid_spec, out_shape=[SDS((NDEV * s.shape[1], D), bf16) for s in shards],
        compiler_params=_CP, name="place_own")(me, *shards)


_TOKEN = SDS((8, 128), f32)
def _ag_start(lands, after, l):
    n = len(lands)
    after = list(after) if isinstance(after, (list, tuple)) else [after]

    def body(*refs):
        zones, send_sems, recv_sems, token = refs[:n], refs[n + len(after)], refs[n + len(after) + 1], refs[-1]
        _, me_idx = _peer(0)
        for k, mask in enumerate((1, 4, 2)):
            for t in range(n):
                _row_copy(zones[t], me_idx, send_sems.at[k * n + t], recv_sems.at[k * n + t], _peer(mask)[0]).start()
        token[...] = jnp.zeros_like(token)

    outs = pl.pallas_call(
        body, name=f"ag_start_{l}", in_specs=[_HBM] * n + [_ANY] * len(after),
        out_specs=(_SEM, _SEM, *[_HBM] * n, pl.BlockSpec(memory_space=pltpu.VMEM)),
        out_shape=(pltpu.SemaphoreType.DMA((3 * n,)), pltpu.SemaphoreType.DMA((3 * n,)),
                   *[pltpu.HBM(a.shape, a.dtype) for a in lands], _TOKEN),
        input_output_aliases={t: 2 + t for t in range(n)}, compiler_params=_CP_SPLIT)(
            *[_hbm(a) for a in lands], *after)
    return outs[0], outs[1], list(outs[2:2 + n]), outs[-1]


def _ag_pass(lands, recv_sems, after, l):
    n = len(lands)
    after = list(after) if isinstance(after, (list, tuple)) else [after]

    def body(*refs):
        zones, recv_sems = refs[:n], refs[n]
        psend, precv, token = refs[n + 1 + len(after)], refs[n + 2 + len(after)], refs[-1]
        me, _ = _peer(0)
        sibling, _ = _peer(1)
        for j, mask in enumerate((4, 2)):
            idx = _peer(mask)[1]
            for t in range(n):
                _row_copy(zones[t], idx, psend.at[j * n + t], recv_sems.at[(1 + j) * n + t], me).wait_recv()
                _row_copy(zones[t], idx, psend.at[j * n + t], precv.at[j * n + t], sibling).start()
        fwd_idx, fwd_dev = _diag_route()
        for t in range(n):
            _row_copy(zones[t], fwd_idx, psend.at[2 * n + t], precv.at[2 * n + t], fwd_dev).start()
        token[...] = jnp.zeros_like(token)

    outs = pl.pallas_call(
        body, name=f"ag_pass_{l}", in_specs=[_HBM] * n + [_SEM] + [_ANY] * len(after),
        out_specs=(_SEM, _SEM, *[_HBM] * n, pl.BlockSpec(memory_space=pltpu.VMEM)),
        out_shape=(pltpu.SemaphoreType.DMA((3 * n,)), pltpu.SemaphoreType.DMA((3 * n,)),
                   *[pltpu.HBM(a.shape, a.dtype) for a in lands], _TOKEN),
        input_output_aliases={t: 2 + t for t in range(n)}, compiler_params=_CP_SPLIT)(*lands, recv_sems, *after)
    return outs[0], outs[1], list(outs[2:2 + n]), outs[-1]


def _ag_last(lands, precv, after, l):
    n = len(lands)
    after = list(after) if isinstance(after, (list, tuple)) else [after]

    def body(*refs):
        zones, precv = refs[:n], refs[n]
        qsend, qrecv, token = refs[n + 1 + len(after)], refs[n + 2 + len(after)], refs[-1]
        me, _ = _peer(0)
        sibling, _ = _peer(1)
        idx = _peer(6)[1]
        for t in range(n):
            _row_copy(zones[t], idx, qsend.at[t], precv.at[2 * n + t], me).wait_recv()
            _row_copy(zones[t], idx, qsend.at[t], qrecv.at[t], sibling).start()
        token[...] = jnp.zeros_like(token)

    outs = pl.pallas_call(
        body, name=f"ag_last_{l}", in_specs=[_HBM] * n + [_SEM] + [_ANY] * len(after),
        out_specs=(_SEM, _SEM, *[_HBM] * n, pl.BlockSpec(memory_space=pltpu.VMEM)),
        out_shape=(pltpu.SemaphoreType.DMA((n,)), pltpu.SemaphoreType.DMA((n,)),
                   *[pltpu.HBM(a.shape, a.dtype) for a in lands], _TOKEN),
        input_output_aliases={t: 2 + t for t in range(n)}, compiler_params=_CP_SPLIT)(*lands, precv, *after)
    return outs[0], outs[1], list(outs[2:2 + n]), outs[-1]


def _ag_wait(lands, send_sems, recv_sems, psend, precv, qsend, qrecv, after, l):
    n = len(lands)
    after = list(after) if isinstance(after, (list, tuple)) else [after]

    def body(*refs):
        zones = refs[:n]
        send_sems, recv_sems, psend, precv, qsend, qrecv = refs[n:n + 6]
        me, me_idx = _peer(0)
        for k in range(3):
            for t in range(n):
                _row_copy(zones[t], me_idx, send_sems.at[k * n + t], recv_sems.at[k * n + t], me).wait_send()
        for t in range(n):
            _row_copy(zones[t], _peer(1)[1], send_sems.at[t], recv_sems.at[t], me).wait_recv()
        fwd_idx, _ = _diag_route()
        for j, (mine, theirs) in enumerate(((_peer(4)[1], _peer(5)[1]), (_peer(2)[1], _peer(3)[1]))):
            for t in range(n):
                _row_copy(zones[t], mine, psend.at[j * n + t], precv.at[j * n + t], me).wait_send()
                _row_copy(zones[t], theirs, psend.at[j * n + t], precv.at[j * n + t], me).wait_recv()
        for t in range(n):
            _row_copy(zones[t], fwd_idx, psend.at[2 * n + t], precv.at[2 * n + t], me).wait_send()
            _row_copy(zones[t], _peer(6)[1], qsend.at[t], qrecv.at[t], me).wait_send()
            _row_copy(zones[t], _peer(7)[1], qsend.at[t], qrecv.at[t], me).wait_recv()

    outs = pl.pallas_call(
        body, name=f"ag_wait_{l}", in_specs=[_HBM] * n + [_SEM] * 6 + [_ANY] * len(after),
        out_specs=tuple([_HBM] * n), out_shape=tuple(pltpu.HBM(a.shape, a.dtype) for a in lands),
        input_output_aliases={t: t for t in range(n)}, compiler_params=_CP_SPLIT)(
            *lands, send_sems, recv_sems, psend, precv, qsend, qrecv, *after)
    return list(outs)


def _xchg_src(ref, slot_ref, idx):
    return _rows(ref, idx) if ref.shape[0] == NDEV * slot_ref.shape[1] else ref


def _rs_start(srcs, slots, after, tag):
    n = len(srcs)
    after = list(after) if isinstance(after, (list, tuple)) else [after]

    def body(*refs):
        src, slot = refs[:n], refs[n:2 * n]
        send_sems, recv_sems, token = refs[2 * n + len(after)], refs[2 * n + len(after) + 1], refs[-1]
        _, me_idx = _peer(0)
        for k in range(1, NDEV):
            dev, idx = _peer(k)
            for t in range(n):
                pltpu.make_async_remote_copy(
                    src_ref=_xchg_src(src[t], slot[t], idx), dst_ref=slot[t].at[me_idx],
                    send_sem=send_sems.at[(k - 1) * n + t], recv_sem=recv_sems.at[(k - 1) * n + t],
                    device_id=dev, device_id_type=_MESH).start()
        token[...] = jnp.zeros_like(token)

    outs = pl.pallas_call(
        body, name=f"rs_start_{tag}", in_specs=[_HBM] * (2 * n) + [_ANY] * len(after),
        out_specs=(_SEM, _SEM, *[_HBM] * (2 * n), pl.BlockSpec(memory_space=pltpu.VMEM)),
        out_shape=(pltpu.SemaphoreType.DMA(((NDEV - 1) * n,)), pltpu.SemaphoreType.DMA(((NDEV - 1) * n,)),
                   *[pltpu.HBM(a.shape, a.dtype) for a in list(srcs) + list(slots)], _TOKEN),
        input_output_aliases={t: 2 + t for t in range(2 * n)}, compiler_params=_CP_SPLIT)(
            *[_hbm(a) for a in list(srcs) + list(slots)], *after)
    return outs[0], outs[1], list(outs[2:2 + n]), list(outs[2 + n:2 + 2 * n]), outs[-1]


def _rs_wait(srcs, slots, send_sems, recv_sems, after, tag):
    n = len(srcs)
    after = list(after) if isinstance(after, (list, tuple)) else [after]

    def body(*refs):
        src, slot, send_sems, recv_sems = refs[:n], refs[n:2 * n], refs[2 * n], refs[2 * n + 1]
        me, _ = _peer(0)
        for k in range(1, NDEV):
            idx = _peer(k)[1]
            for t in range(n):
                cp = pltpu.make_async_remote_copy(
                    src_ref=_xchg_src(src[t], slot[t], idx), dst_ref=slot[t].at[idx],
                    send_sem=send_sems.at[(k - 1) * n + t], recv_sem=recv_sems.at[(k - 1) * n + t],
                    device_id=me, device_id_type=_MESH)
                cp.wait_send()
                cp.wait_recv()

    outs = pl.pallas_call(
        body, name=f"rs_wait_{tag}", in_specs=[_HBM] * (2 * n) + [_SEM, _SEM] + [_ANY] * len(after),
        out_specs=tuple([_HBM] * (2 * n)),
        out_shape=tuple(pltpu.HBM(a.shape, a.dtype) for a in list(srcs) + list(slots)),
        input_output_aliases={t: t for t in range(2 * n)}, compiler_params=_CP_SPLIT)(
            *srcs, *slots, send_sems, recv_sems, *after)
    return list(outs[:n]), list(outs[n:])


def _pair_start(full4s, bufs, after, tag):
    n = len(full4s)
    after = list(after) if isinstance(after, (list, tuple)) else [after]

    def body(*refs):
        full, buf = refs[:n], refs[n:2 * n]
        send_sems, recv_sems, token = refs[2 * n + len(after)], refs[2 * n + len(after) + 1], refs[-1]
        c = lax.axis_index("c")
        for t in range(n):
            pltpu.make_async_remote_copy(src_ref=full[t].at[:, 1 - c], dst_ref=buf[t], send_sem=send_sems.at[t],
                                         recv_sem=recv_sems.at[t], device_id=_peer(1)[0], device_id_type=_MESH).start()
        token[...] = jnp.zeros_like(token)

    outs = pl.pallas_call(
        body, name=f"pair_start_{tag}", in_specs=[_HBM] * (2 * n) + [_ANY] * len(after),
        out_specs=(_SEM, _SEM, *[_HBM] * (2 * n), pl.BlockSpec(memory_space=pltpu.VMEM)),
        out_shape=(pltpu.SemaphoreType.DMA((n,)), pltpu.SemaphoreType.DMA((n,)),
                   *[pltpu.HBM(a.shape, a.dtype) for a in list(full4s) + list(bufs)], _TOKEN),
        input_output_aliases={t: 2 + t for t in range(2 * n)}, compiler_params=_CP_SPLIT)(
            *[_hbm(a) for a in list(full4s) + list(bufs)], *after)
    return outs[0], outs[1], list(outs[2:2 + n]), list(outs[2 + n:2 + 2 * n]), outs[-1]


def _pair_wait(full4s, bufs, send_sems, recv_sems, after, tag):
    n = len(full4s)
    after = list(after) if isinstance(after, (list, tuple)) else [after]

    def body(*refs):
        full, buf, send_sems, recv_sems = refs[:n], refs[n:2 * n], refs[2 * n], refs[2 * n + 1]
        c = lax.axis_index("c")
        for t in range(n):
            cp = pltpu.make_async_remote_copy(src_ref=full[t].at[:, 1 - c], dst_ref=buf[t], send_sem=send_sems.at[t],
                                              recv_sem=recv_sems.at[t], device_id=_peer(0)[0], device_id_type=_MESH)
            cp.wait_send()
            cp.wait_recv()

    outs = pl.pallas_call(
        body, name=f"pair_wait_{tag}", in_specs=[_HBM] * (2 * n) + [_SEM, _SEM] + [_ANY] * len(after),
        out_specs=tuple([_HBM] * (2 * n)),
        out_shape=tuple(pltpu.HBM(a.shape, a.dtype) for a in list(full4s) + list(bufs)),
        input_output_aliases={t: t for t in range(2 * n)}, compiler_params=_CP_SPLIT)(
            *full4s, *bufs, send_sems, recv_sems, *after)
    return list(outs[:n]), list(outs[n:])


def _pair_sum(core, full4s, bufs):
    n = len(full4s)

    def body(core_ref, *refs):
        for t in range(n):
            refs[2 * n + t][...] = (refs[t][...].astype(f32) + refs[n + t][...].astype(f32)).astype(bf16)

    grid_spec = pltpu.PrefetchScalarGridSpec(
        num_scalar_prefetch=1, grid=(4,),
        in_specs=[pl.BlockSpec((None, None) + a.shape[2:], lambda j, core_ref: (j, core_ref[0], 0, 0)) for a in full4s]
        + [pl.BlockSpec((None,) + b.shape[1:], lambda j, core_ref: (j, 0, 0)) for b in bufs],
        out_specs=[pl.BlockSpec((None,) + b.shape[1:], lambda j, core_ref: (j, 0, 0)) for b in bufs])
    return pl.pallas_call(
        body, grid_spec=grid_spec, out_shape=[SDS(b.shape, bf16) for b in bufs],
        compiler_params=_CP, name="pair_sum")(core, *full4s, *bufs)


def _chip_start(sums, slots, after, tag):
    n = len(sums)
    after = list(after) if isinstance(after, (list, tuple)) else [after]

    def body(*refs):
        src, slot = refs[:n], refs[n:2 * n]
        send_sems, recv_sems, token = refs[2 * n + len(after)], refs[2 * n + len(after) + 1], refs[-1]
        my_chip = 2 * lax.axis_index("x") + lax.axis_index("y")
        for k, mask in enumerate((4, 2, 6)):
            dev, _ = _peer(mask)
            for t in range(n):
                pltpu.make_async_remote_copy(
                    src_ref=src[t].at[2 * dev[0] + dev[1]], dst_ref=slot[t].at[my_chip],
                    send_sem=send_sems.at[k * n + t], recv_sem=recv_sems.at[k * n + t],
                    device_id=dev, device_id_type=_MESH).start()
        token[...] = jnp.zeros_like(token)

    outs = pl.pallas_call(
        body, name=f"chip_start_{tag}", in_specs=[_HBM] * (2 * n) + [_ANY] * len(after),
        out_specs=(_SEM, _SEM, *[_HBM] * (2 * n), pl.BlockSpec(memory_space=pltpu.VMEM)),
        out_shape=(pltpu.SemaphoreType.DMA((3 * n,)), pltpu.SemaphoreType.DMA((3 * n,)),
                   *[pltpu.HBM(a.shape, a.dtype) for a in list(sums) + list(slots)], _TOKEN),
        input_output_aliases={t: 2 + t for t in range(2 * n)}, compiler_params=_CP_SPLIT)(
            *[_hbm(a) for a in list(sums) + list(slots)], *after)
    return outs[0], outs[1], list(outs[2:2 + n]), list(outs[2 + n:2 + 2 * n]), outs[-1]


def _chip_wait(sums, slots, send_sems, recv_sems, after, tag):
    n = len(sums)
    after = list(after) if isinstance(after, (list, tuple)) else [after]

    def body(*refs):
        src, slot, send_sems, recv_sems = refs[:n], refs[n:2 * n], refs[2 * n], refs[2 * n + 1]
        for k, mask in enumerate((4, 2, 6)):
            dev, _ = _peer(mask)
            chip = 2 * dev[0] + dev[1]
            for t in range(n):
                cp = pltpu.make_async_remote_copy(
                    src_ref=src[t].at[chip], dst_ref=slot[t].at[chip],
                    send_sem=send_sems.at[k * n + t], recv_sem=recv_sems.at[k * n + t],
                    device_id=_peer(0)[0], device_id_type=_MESH)
                cp.wait_send()
                cp.wait_recv()

    outs = pl.pallas_call(
        body, name=f"chip_wait_{tag}", in_specs=[_HBM] * (2 * n) + [_SEM, _SEM] + [_ANY] * len(after),
        out_specs=tuple([_HBM] * (2 * n)),
        out_shape=tuple(pltpu.HBM(a.shape, a.dtype) for a in list(sums) + list(slots)),
        input_output_aliases={t: t for t in range(2 * n)}, compiler_params=_CP_SPLIT)(
            *sums, *slots, send_sems, recv_sems, *after)
    return list(outs[:n]), list(outs[n:])


def _sum_slots(slots, rb):
    r = slots.shape[1]

    def body(s_ref, o_ref):
        acc = s_ref[0].astype(f32)
        for s in range(1, NDEV):
            acc = acc + s_ref[s].astype(f32)
        o_ref[...] = acc

    return pl.pallas_call(
        body, grid=(r // rb,),
        in_specs=[pl.BlockSpec((NDEV, rb, D), lambda i: (0, i, 0))],
        out_specs=pl.BlockSpec((rb, D), lambda i: (i, 0)),
        out_shape=SDS((r, D), f32), compiler_params=_CP, name="sum_slots")(slots)


def _adamw(w, g, m, v):
    shape = w.shape
    cols = shape[-1]
    rows = w.size // cols
    rb = rows
    for cand in (512, 256, 128, 64, 32, 16, 8):
        if rows % cand == 0 and rows > cand:
            rb = cand
            break

    def body(w_ref, g_ref, m_ref, v_ref, d_ref, mo_ref, vo_ref):
        d_ref[...], mo_ref[...], vo_ref[...] = _adamw_math(w_ref[...], g_ref[...], m_ref[...], v_ref[...])

    spec = pl.BlockSpec((rb, cols), lambda i: (i, 0))
    outs = pl.pallas_call(
        body, grid=(rows // rb,), in_specs=[spec] * 4, out_specs=[spec] * 3,
        out_shape=[SDS((rows, cols), f32)] * 3, compiler_params=_CP, name="adamw")(
            *(a.reshape(rows, cols) for a in (w, g, m, v)))
    return tuple(o.reshape(shape) for o in outs)


def _adamw_math(w, g, m, v):
    m = ADAM_B1 * m + (1.0 - ADAM_B1) * g
    v = ADAM_B2 * v + (1.0 - ADAM_B2) * (g * g)
    m_hat = m / (1.0 - ADAM_B1 ** ADAM_STEP)
    v_hat = v / (1.0 - ADAM_B2 ** ADAM_STEP)
    return -ADAM_LR * (m_hat / (jnp.sqrt(v_hat) + ADAM_EPS) + ADAM_WD * w), m, v


def _reduce_adamw(acc, me, full, slots, w, m, v, l):
    _, r, _ = w.shape
    ns = slots.shape[0]
    rb = r // 2 if r > 128 else r

    def body(me_ref, full_ref, slots_ref, w_ref, m_ref, v_ref, *refs):
        go_ref, d_ref, mo_ref, vo_ref = refs[-4:]
        own = full_ref[...].astype(f32)
        g = None
        for s in range(ns):
            part = jnp.where(me_ref[0] == s, own, slots_ref[s].astype(f32))
            g = part if g is None else g + part
        go_ref[...] = g
        d_ref[...], mo_ref[...], vo_ref[...] = _adamw_math(w_ref[...], g, m_ref[...], v_ref[...])

    steps = r // rb
    lay = pl.BlockSpec((None, rb, D), lambda i, me_ref: (l, i, 0))
    n_acc = 0 if acc is None else 4
    grid_spec = pltpu.PrefetchScalarGridSpec(
        num_scalar_prefetch=1, grid=(steps,),
        in_specs=[pl.BlockSpec((rb, D), lambda i, me_ref: (me_ref[0] * steps + i, 0)),
                  pl.BlockSpec((ns, rb, D), lambda i, me_ref: (0, i, 0)), lay, lay, lay] + [_ANY] * n_acc,
        out_specs=[lay] * 4)
    outs = pl.pallas_call(
        body, grid_spec=grid_spec, out_shape=[SDS(w.shape, f32)] * 4,
        input_output_aliases={6 + j: j for j in range(n_acc)},
        compiler_params=_CP, name="reduce_adamw")(me, full, slots, w, m, v, *(() if acc is None else acc))
    return tuple(outs)


_BIG = ("ffn1_w_gate", "ffn1_w_up", "ffn1_w_down", "w_in", "w_out", "ffn2_w_gate", "ffn2_w_up", "ffn2_w_down")
_TRANSPOSED = ("ffn1_w_gate", "ffn1_w_up", "w_in", "ffn2_w_gate", "ffn2_w_up")

def _block_diag(pool_w):
    out = jnp.zeros((L, PW, PW), pool_w.dtype)
    for gi in range(4):
        out = out.at[:, 64 * gi:64 * (gi + 1), 64 * gi:64 * (gi + 1)].set(pool_w[:, gi])
    return out


def kernel(x, positions, ffn1_norm, ffn1_w_gate, ffn1_w_up, ffn1_w_down, mix_norm, w_in, pool_w, pool_scale, w_out, ffn2_norm, ffn2_w_gate, ffn2_w_up, ffn2_w_down, final_norm, loss_target, m_ffn1_norm, m_ffn1_w_gate, m_ffn1_w_up, m_ffn1_w_down, m_mix_norm, m_w_in, m_pool_w, m_pool_scale, m_w_out, m_ffn2_norm, m_ffn2_w_gate, m_ffn2_w_up, m_ffn2_w_down, m_final_norm, v_ffn1_norm, v_ffn1_w_gate, v_ffn1_w_up, v_ffn1_w_down, v_mix_norm, v_w_in, v_pool_w, v_pool_scale, v_w_out, v_ffn2_norm, v_ffn2_w_gate, v_ffn2_w_up, v_ffn2_w_down, v_final_norm):
    weights = dict(ffn1_norm=ffn1_norm, ffn1_w_gate=ffn1_w_gate, ffn1_w_up=ffn1_w_up, ffn1_w_down=ffn1_w_down,
                   mix_norm=mix_norm, w_in=w_in, pool_w=pool_w, pool_scale=pool_scale, w_out=w_out,
                   ffn2_norm=ffn2_norm, ffn2_w_gate=ffn2_w_gate, ffn2_w_up=ffn2_w_up, ffn2_w_down=ffn2_w_down,
                   final_norm=final_norm)
    moms = dict(ffn1_norm=m_ffn1_norm, ffn1_w_gate=m_ffn1_w_gate, ffn1_w_up=m_ffn1_w_up, ffn1_w_down=m_ffn1_w_down,
                mix_norm=m_mix_norm, w_in=m_w_in, pool_w=m_pool_w, pool_scale=m_pool_scale, w_out=m_w_out,
                ffn2_norm=m_ffn2_norm, ffn2_w_gate=m_ffn2_w_gate, ffn2_w_up=m_ffn2_w_up, ffn2_w_down=m_ffn2_w_down,
                final_norm=m_final_norm)
    vels = dict(ffn1_norm=v_ffn1_norm, ffn1_w_gate=v_ffn1_w_gate, ffn1_w_up=v_ffn1_w_up, ffn1_w_down=v_ffn1_w_down,
                mix_norm=v_mix_norm, w_in=v_w_in, pool_w=v_pool_w, pool_scale=v_pool_scale, w_out=v_w_out,
                ffn2_norm=v_ffn2_norm, ffn2_w_gate=v_ffn2_w_gate, ffn2_w_up=v_ffn2_w_up, ffn2_w_down=v_ffn2_w_down,
                final_norm=v_final_norm)
    names = list(weights)

    me_idx = 4 * lax.axis_index("x") + 2 * lax.axis_index("y") + lax.axis_index("c")
    me_arr = me_idx.reshape(1).astype(jnp.int32)

    as_rows = lambda a, nm: jnp.swapaxes(a, 1, 2) if nm in _TRANSPOSED else a
    w_rows = {nm: as_rows(weights[nm], nm) for nm in _BIG}
    m_rows = {nm: as_rows(moms[nm], nm) for nm in _BIG}
    v_rows = {nm: as_rows(vels[nm], nm) for nm in _BIG}

    def landing_zones(l, which):
        return _place_own(me_arr, [w_rows[_BIG[t]] for t in which], l)

    g_ffn1 = [ffn1_norm[l].reshape(1, D) for l in range(L)]
    g_mix = [mix_norm[l].reshape(1, D) for l in range(L)]
    g_ffn2 = [ffn2_norm[l].reshape(1, D) for l in range(L)]
    wbd_all = _block_diag(pool_w).astype(bf16)
    wbd = [wbd_all[l] for l in range(L)]
    pscale = [pool_scale[l].reshape(1, PW) for l in range(L)]
    tabs = _rope_tables(positions)
    flat = lambda a: a.reshape(S, a.shape[-1])
    r4 = lambda a: a.reshape(4, S // 4, a.shape[-1])
    r16 = lambda a: a.reshape(16, S // 16, a.shape[-1])

    first, rest, whole = (0, 1, 2, 3), (4, 5, 6, 7), tuple(range(8))

    def ag_begin(l, which, after, zones=None):
        tag = f"{l}{'' if which == whole else 'r'}"
        zones = landing_zones(l, which) if zones is None else zones
        send_sems, recv_sems, zones, token = _ag_start(zones, after, tag)
        return dict(tag=tag, zones=zones, s=send_sems, r=recv_sems), token

    def ag_second(ch, after):
        ch["ps"], ch["pr"], ch["zones"], token = _ag_pass(ch["zones"], ch["r"], after, ch["tag"])
        return token

    def ag_third(ch, after):
        ch["qs"], ch["qr"], ch["zones"], token = _ag_last(ch["zones"], ch["pr"], after, ch["tag"])
        return token

    def ag_end(ch, after):
        return _ag_wait(ch["zones"], ch["s"], ch["r"], ch["ps"], ch["pr"], ch["qs"], ch["qr"], after, ch["tag"])

    head = _all_gather(landing_zones(0, first))
    ch_rest, tok_rest = ag_begin(0, rest, head[0])
    chains = {}
    chains[1], tok_next = ag_begin(1, whole, head[0])
    gathered = [None] * L
    xs = x.reshape(S, D)
    saved = []
    for l in range(L):
        first_after, second_after = (), ()
        if l == 0:
            gt1, ut1, dn1, wint = head
            first_after = (tok_rest, tok_next)
        else:
            gt1, ut1, dn1, wint, wout, gt2, ut2, dn2 = gathered[l]
        x0 = xs
        x1, gate1, up1 = _ffn_fwd(x0, g_ffn1[l], gt1, ut1, dn1, after=first_after)
        hmix, vp, q1, k1, v1, q4, k4, v4, q16, k16, v16 = _mix_in_fwd(x1, g_mix[l], wint, tabs)
        q4, k4, v4, q16, k16, v16 = map(flat, (q4, k4, v4, q16, k16, v16))
        ypool, diff = _pool_fwd(vp, wbd[l], pscale[l])
        after_attn = None
        if l == 0:
            after_attn = ag_second(ch_rest, [ypool, q16])
        o1, l1 = _attn_fwd(q1, k1, v1, S, after=after_attn)
        o4, l4 = _attn_fwd(q4, k4, v4, S // 4, after=after_attn)
        o16, l16 = _attn_fwd(q16, k16, v16, S // 16, after=after_attn)
        if l == 0:
            early_zones = {ll: landing_zones(ll, whole) for ll in range(2, L)}
            token = ag_third(ch_rest, [o1, o4, o16] + [z for zs in early_zones.values() for z in zs])
            wout, gt2, ut2, dn2 = ag_end(ch_rest, token)
            gathered[0] = list(head) + [wout, gt2, ut2, dn2]
        elif l + 1 < L:
            second_after = (ag_second(chains[l + 1], [o1, o4, o16]),)
        x2, mixed, o, lse1, lse4, lse16 = _mix_out_fwd(x1, ypool, o1, l1, r4(o4), r4(l4), r16(o16), r16(l16), wout)
        if l == 0:
            second_after = (ag_second(chains[1], x2),)
        x3, gate2, up2 = _ffn_fwd(x2, g_ffn2[l], gt2, ut2, dn2, after=second_after)
        if l + 1 < L:
            token = ag_third(chains[l + 1], x3)
            if l + 2 < L:
                chains[l + 2], token = ag_begin(l + 2, whole, token, early_zones[l + 2])
            gathered[l + 1] = ag_end(chains[l + 1], token)
        saved.append(dict(x0=x0, x1=x1, x2=x2, gate1=gate1, up1=up1, gate2=gate2, up2=up2, hmix=hmix, diff=diff,
                          qkv=((q1, k1, v1), (q4, k4, v4), (q16, k16, v16)), mixed=mixed, o=o,
                          lse=(lse1, flat(lse4), flat(lse16))))
        xs = x3

    dx, loss_part, d_final = _loss_head(xs, final_norm.reshape(1, D), loss_target.reshape(S, D))

    d_norm = {nm: [None] * L for nm in ("ffn1_norm", "mix_norm", "ffn2_norm")}
    d_poolw, d_pscale = [None] * L, [None] * L
    group_a = ("ffn2_w_gate", "ffn2_w_up", "ffn2_w_down", "w_out")
    group_b = ("ffn1_w_gate", "ffn1_w_up", "ffn1_w_down", "w_in")
    acc = {}

    def exchange(full, group, after, tag):
        srcs = [full[nm] for nm in group]
        slots = [lax.empty((NDEV, g.shape[0] // NDEV, D), bf16) for g in srcs]
        ssem, rsem, srcs, slots, token = _rs_start(srcs, slots, after, tag)
        return (srcs, slots, ssem, rsem, tag), token

    def update(l, group, flight, after):
        srcs, slots, ssem, rsem, tag = flight
        srcs, slots = _rs_wait(srcs, slots, ssem, rsem, after, tag)
        for nm, full_g, slots_g in zip(group, srcs, slots):
            acc[nm] = _reduce_adamw(acc.get(nm), me_arr, full_g, slots_g, w_rows[nm], m_rows[nm], v_rows[nm], l)
        return [acc[nm][0] for nm in group], slots

    core_arr = lax.axis_index("c").reshape(1).astype(jnp.int32)
    chip_arr = (2 * lax.axis_index("x") + lax.axis_index("y")).reshape(1).astype(jnp.int32)

    def exchange_cores(full, group, after, tag):
        full4s = [full[nm].reshape(4, 2, full[nm].shape[0] // NDEV, D) for nm in group]
        bufs = [lax.empty((4,) + a.shape[2:], bf16) for a in full4s]
        ssem, rsem, full4s, bufs, token = _pair_start(full4s, bufs, after, tag)
        return (full4s, bufs, ssem, rsem, tag), token

    def exchange_chips(flight, after):
        full4s, bufs, ssem, rsem, tag = flight
        full4s, bufs = _pair_wait(full4s, bufs, ssem, rsem, after, tag)
        sums = _pair_sum(core_arr, full4s, bufs)
        slots = [lax.empty(a.shape, bf16) for a in sums]
        ssem, rsem, sums, slots, token = _chip_start(sums, slots, bufs[0], tag)
        return (sums, slots, ssem, rsem, tag), token

    def update_chips(l, group, flight, after):
        sums, slots, ssem, rsem, tag = flight
        sums, slots = _chip_wait(sums, slots, ssem, rsem, after, tag)
        for nm, sums_g, slots_g in zip(group, sums, slots):
            own = sums_g.reshape(4 * sums_g.shape[1], D)
            acc[nm] = _reduce_adamw(acc.get(nm), chip_arr, own, slots_g, w_rows[nm], m_rows[nm], v_rows[nm], l)
        return [acc[nm][0] for nm in group]

    flights = {}
    token_b = None
    for l in reversed(range(L)):
        sv = saved[l]
        gt1, ut1, dn1, wint, wout, gt2, ut2, dn2 = gathered[l]
        full = {}
        dx, dgate, dup, h, dy, d_norm["ffn2_norm"][l] = _ffn_bwd_d(
            sv["x2"], g_ffn2[l], sv["gate2"], sv["up2"], dx, gt2, ut2, dn2, after=() if token_b is None else (token_b,))
        full["ffn2_w_gate"], full["ffn2_w_up"], full["ffn2_w_down"] = _ffn_bwd_w(h, dy, sv["gate2"], sv["up2"], dgate, dup)

        dxb, dyp, do1, do4, do16, dl1, dl4, dl16 = _mix_out_bwd(dx, sv["o"], wout)
        full["w_out"] = _wgrad(sv["mixed"], dxb)
        flights[l, "a"], token_a = (exchange_cores if l == 0 else exchange)(full, group_a, dxb, f"a{l}")
        dvp, d_poolw[l], d_pscale[l] = _pool_bwd(dyp, sv["diff"], wbd[l], pscale[l], after=(token_a,))
        dos, dls = (do1, flat(do4), flat(do16)), (dl1, flat(dl4), flat(dl16))
        dqkv = []
        for b, lc in enumerate((S, S // 4, S // 16)):
            qb, kb, vb = sv["qkv"][b]
            dqkv.append(_attn_bwd(qb, kb, vb, dos[b], sv["lse"][b], dls[b], lc))
        d4 = tuple(r4(a) for a in dqkv[1])
        d16 = tuple(r16(a) for a in dqkv[2])
        mix_after = ()
        if l == 0:
            flights[0, "a"], token_a = exchange_chips(flights[0, "a"], [dqkv[0][0], dqkv[1][0], dqkv[2][0]])
            mix_after = (token_a,)
        dx, dproj, d_norm["mix_norm"][l] = _mix_in_bwd(dx, sv["x1"], g_mix[l], wint, tabs, dvp, dqkv[0], d4, d16,
                                                       after=mix_after)
        full["w_in"] = _wgrad(dproj, sv["hmix"])

        dx, dgate, dup, h, dy, d_norm["ffn1_norm"][l] = _ffn_bwd_d(sv["x0"], g_ffn1[l], sv["gate1"], sv["up1"], dx, gt1, ut1, dn1)
        full["ffn1_w_gate"], full["ffn1_w_up"], full["ffn1_w_down"] = _ffn_bwd_w(h, dy, sv["gate1"], sv["up1"], dgate, dup)

        after = dx
        if l + 1 < L and l + 1 >= 2:
            after, _ = update(l + 1, group_a, flights.pop((l + 1, "a")), after)
        if l + 1 < L and l + 1 >= 3:
            after, _ = update(l + 1, group_b, flights.pop((l + 1, "b")), after)
        if l > 0:
            flights[l, "b"], token_b = exchange(full, group_b, after, f"b{l}")

    flights[0, "b"], token_b = exchange_cores(full, group_b, dx, "b0")
    pad8 = lambda a: jnp.pad(a, ((0, 8 - a.shape[0]), (0, 0)))
    misc = jnp.concatenate([d_final, jnp.concatenate(d_pscale, axis=1), loss_part], axis=0)
    small = jnp.concatenate(
        [pad8(jnp.concatenate(d_norm[nm], axis=0)) for nm in ("ffn1_norm", "mix_norm", "ffn2_norm")]
        + [pad8(misc), jnp.stack(d_poolw).reshape(L * 16, D)], axis=0)
    small_slots = lax.dynamic_update_slice(lax.empty((NDEV, SMALL_ROWS, D), f32), small[None], (me_idx, 0, 0))
    pack_sems = _rs_start([small], [small_slots], token_b, "pack")
    flights[0, "b"], token_b = exchange_chips(flights[0, "b"], pack_sems[-1])

    after = token_b
    for key in [(2, "b"), (1, "a"), (1, "b")]:
        after, _ = update(key[0], group_a if key[1] == "a" else group_b, flights.pop(key), after)
    _, pack_slots = _rs_wait(pack_sems[2], pack_sems[3], pack_sems[0], pack_sems[1], after, "pack")
    sm = _sum_slots(pack_slots[0], SMALL_ROWS)
    grads = {}
    grads["ffn1_norm"], grads["mix_norm"], grads["ffn2_norm"] = sm[0:L], sm[8:8 + L], sm[16:16 + L]
    grads["final_norm"] = sm[24]
    grads["pool_scale"] = sm[25].reshape(L, PW)
    grads["pool_w"] = sm[32:32 + L * 16].reshape(L, 4, 64, 64)
    loss = sm[26, 0]
    upd = {nm: _adamw(weights[nm], grads[nm], moms[nm], vels[nm]) for nm in names if nm not in _BIG}
    after = update_chips(0, group_a, flights.pop((0, "a")), [upd[nm][0] for nm in upd])
    update_chips(0, group_b, flights.pop((0, "b")), after)
    for nm in _BIG:
        grads[nm], upd[nm] = as_rows(acc[nm][0], nm), tuple(as_rows(a, nm) for a in acc[nm][1:])
    return (loss, dx.reshape(1, S, D), *[grads[nm] for nm in names], *[upd[nm][0] for nm in names],
            *[upd[nm][1] for nm in names], *[upd[nm][2] for nm in names])
```

```python
import jax
import jax.numpy as jnp
from jax import lax
from jax.experimental import pallas as pl
from jax.experimental.pallas import tpu as pltpu

f32 = jnp.float32
bf16 = jnp.bfloat16
SDS = jax.ShapeDtypeStruct

D = 1024
S = 2048
F = 2816
L = 4
PW = 256
AW = 768
PROJ = PW + 3 * AW
NDEV = 8
TM = 256
QB = 128
HALF = 64
NG = AW // 128
NORM_EPS = 1e-6
MASK_VALUE = -1e30
ROPE_THETA = 500000.0
ADAM_LR, ADAM_B1, ADAM_B2, ADAM_EPS, ADAM_WD, ADAM_STEP = 0.001, 0.9, 0.999, 1e-08, 0.01, 10
POOL_WINDOWS = (2, 4, 8, 16)
PAD = 8
SMALL_ROWS = 96
VMEM_LIMIT = 56 * 1024 * 1024

_CP = pltpu.CompilerParams(vmem_limit_bytes=VMEM_LIMIT)
_ANY = pl.BlockSpec(memory_space=pl.ANY)
_HBM = pl.BlockSpec(memory_space=pltpu.HBM)
_SEM = pl.BlockSpec(memory_space=pltpu.SEMAPHORE)
_MESH = pl.DeviceIdType.MESH
_CP_SPLIT = pltpu.CompilerParams(has_side_effects=pltpu.SideEffectType.DATAFLOW_SIDE_EFFECTING)


def _dot_nn(a, b):
    return lax.dot_general(a, b, (((1,), (0,)), ((), ())), preferred_element_type=f32)


def _dot_nt(a, b):
    return lax.dot_general(a, b, (((1,), (1,)), ((), ())), preferred_element_type=f32)


def _dot_tn(a, b):
    return lax.dot_general(a, b, (((0,), (0,)), ((), ())), preferred_element_type=f32)


def _rms(x, g):
    r = lax.rsqrt(jnp.mean(x * x, axis=-1, keepdims=True) + NORM_EPS)
    xh = x * r
    return r, xh, xh * g


def _rms_bwd(dh, r, xh, g):
    dxh = dh * g
    return r * (dxh - xh * jnp.mean(dxh * xh, axis=-1, keepdims=True))


def _tile(cols, rows=TM):
    return pl.BlockSpec((rows, cols), lambda i: (i, 0))


def _const(shape):
    return pl.BlockSpec(shape, lambda i: (0,) * len(shape))


def _layer(rows, cols):
    return pl.BlockSpec((rows, cols), lambda i: (0, 0), pipeline_mode=pl.Buffered(1))


def _p4(cols=AW):
    return pl.BlockSpec((4, TM // 4, cols), lambda i: (0, i, 0))


def _p16(cols=AW):
    return pl.BlockSpec((16, TM // 16, cols), lambda i: (0, i, 0))


def _cols(j):
    return slice(128 * j, 128 * (j + 1))


def _follow(body, n_in, after):
    k = len(after)
    return body if k == 0 else (lambda *refs: body(*refs[:n_in], *refs[n_in + k:]))


def _ffn_fwd(x, g, gt, ut, dn, after=()):
    def body(x_ref, g_ref, gt_ref, ut_ref, dn_ref, xo_ref, gate_ref, up_ref):
        x = x_ref[...]
        _, _, hn = _rms(x, g_ref[...])
        h = hn.astype(bf16)
        gate = _dot_nt(h, gt_ref[...])
        up = _dot_nt(h, ut_ref[...])
        gate_ref[...] = gate.astype(bf16)
        up_ref[...] = up.astype(bf16)
        a = (gate * jax.nn.sigmoid(gate) * up).astype(bf16)
        xo_ref[...] = x + 0.5 * _dot_nn(a, dn_ref[...])

    rows = 2 * TM
    return pl.pallas_call(
        _follow(body, 5, after), grid=(S // rows,),
        in_specs=[_tile(D, rows), _layer(1, D), _layer(F, D), _layer(F, D), _layer(F, D)] + [_ANY] * len(after),
        out_specs=[_tile(D, rows), _tile(F, rows), _tile(F, rows)],
        out_shape=[SDS((S, D), f32), SDS((S, F), bf16), SDS((S, F), bf16)],
        compiler_params=_CP, name="ffn_fwd")(x, g, gt, ut, dn, *after)


def _ffn_bwd_d(x, g, gate, up, dxo, gt, ut, dn, after=()):
    def body(x_ref, g_ref, gate_ref, up_ref, dxo_ref, gt_ref, ut_ref, dn_ref,
             dx_ref, dgate_ref, dup_ref, h_ref, dy_ref, dg_ref):
        x = x_ref[...]
        g = g_ref[...]
        r, xh, hn = _rms(x, g)
        h_ref[...] = hn.astype(bf16)
        dxo = dxo_ref[...]
        dy = (0.5 * dxo).astype(bf16)
        dy_ref[...] = dy
        da = _dot_nt(dy, dn_ref[...])
        gate = gate_ref[...].astype(f32)
        up = up_ref[...].astype(f32)
        sg = jax.nn.sigmoid(gate)
        dgate = (da * up * (sg * (1.0 + gate * (1.0 - sg)))).astype(bf16)
        dup = (da * (gate * sg)).astype(bf16)
        dgate_ref[...] = dgate
        dup_ref[...] = dup
        dh = _dot_nn(dgate, gt_ref[...]) + _dot_nn(dup, ut_ref[...])

        @pl.when(pl.program_id(0) == 0)
        def _():
            dg_ref[...] = jnp.zeros_like(dg_ref)

        dg_ref[...] += jnp.sum(dh * xh, axis=0, keepdims=True)
        dx_ref[...] = dxo + _rms_bwd(dh, r, xh, g)

    return pl.pallas_call(
        _follow(body, 8, after), grid=(S // TM,),
        in_specs=[_tile(D), _layer(1, D), _tile(F), _tile(F), _tile(D),
                  _layer(F, D), _layer(F, D), _layer(F, D)] + [_ANY] * len(after),
        out_specs=[_tile(D), _tile(F), _tile(F), _tile(D), _tile(D), _const((1, D))],
        out_shape=[SDS((S, D), f32), SDS((S, F), bf16), SDS((S, F), bf16), SDS((S, D), bf16),
                   SDS((S, D), bf16), SDS((1, D), f32)],
        compiler_params=_CP, name="ffn_bwd_d")(x, g, gate, up, dxo, gt, ut, dn, *after)


def _ffn_bwd_w(h, dy, gate, up, dgate, dup):
    fc = 256

    def body(h_ref, dy_ref, gate_ref, up_ref, dgate_ref, dup_ref, dgt_ref, dut_ref, ddn_ref):
        gate = gate_ref[...].astype(f32)
        a = (gate * jax.nn.sigmoid(gate) * up_ref[...].astype(f32)).astype(bf16)
        ddn_ref[...] = _dot_tn(a, dy_ref[...]).astype(bf16)
        h = h_ref[...]
        dgt_ref[...] = _dot_tn(dgate_ref[...], h).astype(bf16)
        dut_ref[...] = _dot_tn(dup_ref[...], h).astype(bf16)

    col = pl.BlockSpec((S, fc), lambda j: (0, j))
    row = pl.BlockSpec((fc, D), lambda j: (j, 0))
    full = pl.BlockSpec((S, D), lambda j: (0, 0))
    return pl.pallas_call(
        body, grid=(F // fc,),
        in_specs=[full, full, col, col, col, col],
        out_specs=[row, row, row],
        out_shape=[SDS((F, D), bf16)] * 3,
        compiler_params=_CP, name="ffn_bwd_w")(h, dy, gate, up, dgate, dup)


def _wgrad(a, b):
    m, n = a.shape[1], b.shape[1]
    mc = 2 * TM

    def body(a_ref, b_ref, o_ref):
        o_ref[...] = _dot_tn(a_ref[...], b_ref[...]).astype(bf16)

    return pl.pallas_call(
        body, grid=(m // mc,),
        in_specs=[pl.BlockSpec((S, mc), lambda j: (0, j)), pl.BlockSpec((S, n), lambda j: (0, 0))],
        out_specs=pl.BlockSpec((mc, n), lambda j: (j, 0)),
        out_shape=SDS((m, n), bf16),
        compiler_params=_CP, name="wgrad")(a, b)


def _rope(t, c, sn, sp):
    return t * c + pltpu.roll(t, 120, 1) * sn + pltpu.roll(t, 8, 1) * sp


def _rope_bwd(d, c, sn, sp):
    return d * c + pltpu.roll(d * sn, 8, 1) + pltpu.roll(d * sp, 120, 1)


def _rope_tables(positions):
    inv_freq = ROPE_THETA ** (-jnp.arange(0, 16, 2, dtype=f32) / 16)
    ang = positions.reshape(S, 1).astype(f32) * inv_freq
    cos, sin = jnp.cos(ang), jnp.sin(ang)
    one = jnp.ones((S, 48), f32)
    zero8 = jnp.zeros((S, 8), f32)
    zero48 = jnp.zeros((S, 48), f32)
    c = jnp.concatenate([cos, cos, one], axis=1)
    sn = jnp.concatenate([-sin, zero8, zero48], axis=1)
    sp = jnp.concatenate([zero8, sin, zero48], axis=1)
    return tuple(jnp.concatenate([t, t], axis=1) for t in (c, sn, sp))


def _dilation_perm(n, back=False):
    per = TM // n
    i = lax.broadcasted_iota(jnp.int32, (TM, TM), 1 if back else 0)
    j = lax.broadcasted_iota(jnp.int32, (TM, TM), 0 if back else 1)
    return jnp.where(j == n * (i % per) + i // per, 1.0, 0.0).astype(bf16)


def _mix_in_fwd(x, g, wint, tabs):
    def body(x_ref, g_ref, w_ref, c_ref, sn_ref, sp_ref,
             h_ref, vp_ref, q1, k1, v1, q4, k4, v4, q16, k16, v16):
        _, _, hn = _rms(x_ref[...], g_ref[...])
        h = hn.astype(bf16)
        h_ref[...] = h
        proj = _dot_nt(h, w_ref[...])
        vp_ref[...] = proj[:, :PW]
        c, sn, sp = c_ref[...], sn_ref[...], sp_ref[...]
        perm4, perm16 = _dilation_perm(4), _dilation_perm(16)
        for kind, (o1, o4, o16) in enumerate(((q1, q4, q16), (k1, k4, k16), (v1, v4, v16))):
            for j in range(NG):
                t = proj[:, PW + kind * AW + 128 * j: PW + kind * AW + 128 * (j + 1)]
                if kind == 0:
                    t = _rope(t, c, sn, sp) * 0.125
                elif kind == 1:
                    t = _rope(t, c, sn, sp)
                o1[:, _cols(j)] = t.astype(bf16)
            nat = o1[...]
            o4[...] = _dot_nn(perm4, nat).astype(bf16).reshape(4, TM // 4, AW)
            o16[...] = _dot_nn(perm16, nat).astype(bf16).reshape(16, TM // 16, AW)

    nat, d4, d16 = SDS((S, AW), bf16), SDS((4, S // 4, AW), bf16), SDS((16, S // 16, AW), bf16)
    return pl.pallas_call(
        body, grid=(S // TM,),
        in_specs=[_tile(D), _layer(1, D), _layer(PROJ, D), _tile(128), _tile(128), _tile(128)],
        out_specs=[_tile(D), _tile(PW)] + [_tile(AW)] * 3 + [_p4()] * 3 + [_p16()] * 3,
        out_shape=[SDS((S, D), bf16), SDS((S, PW), f32)] + [nat] * 3 + [d4] * 3 + [d16] * 3,
        compiler_params=_CP, name="mix_in_fwd")(x, g, wint, *tabs)


def _mix_in_bwd(dxo, x, g, wint, tabs, dvp, d1, d4, d16, after=()):
    def body(dxo_ref, x_ref, g_ref, w_ref, c_ref, sn_ref, sp_ref, dvp_ref,
             dq1, dk1, dv1, dq4, dk4, dv4, dq16, dk16, dv16,
             dx_ref, dproj_ref, dg_ref):
        c, sn, sp = c_ref[...], sn_ref[...], sp_ref[...]
        dproj_ref[:, :PW] = dvp_ref[...].astype(bf16)
        back4, back16 = _dilation_perm(4, True), _dilation_perm(16, True)
        for kind, (a1, a4, a16) in enumerate(((dq1, dq4, dq16), (dk1, dk4, dk16), (dv1, dv4, dv16))):
            n4 = _dot_nn(back4, a4[...].reshape(TM, AW))
            n16 = _dot_nn(back16, a16[...].reshape(TM, AW))
            for j in range(NG):
                t = a1[:, _cols(j)].astype(f32) + n4[:, _cols(j)] + n16[:, _cols(j)]
                if kind == 0:
                    t = _rope_bwd(t * 0.125, c, sn, sp)
                elif kind == 1:
                    t = _rope_bwd(t, c, sn, sp)
                dproj_ref[:, PW + kind * AW + 128 * j: PW + kind * AW + 128 * (j + 1)] = t.astype(bf16)
        g = g_ref[...]
        r_, xh, _ = _rms(x_ref[...], g)
        dh = _dot_nn(dproj_ref[...], w_ref[...])

        @pl.when(pl.program_id(0) == 0)
        def _():
            dg_ref[...] = jnp.zeros_like(dg_ref)

        dg_ref[...] += jnp.sum(dh * xh, axis=0, keepdims=True)
        dx_ref[...] = dxo_ref[...] + _rms_bwd(dh, r_, xh, g)

    return pl.pallas_call(
        _follow(body, 17, after), grid=(S // TM,),
        in_specs=[_tile(D), _tile(D), _layer(1, D), _layer(PROJ, D), _tile(128), _tile(128), _tile(128),
                  _tile(PW)] + [_tile(AW)] * 3 + [_p4()] * 3 + [_p16()] * 3 + [_ANY] * len(after),
        out_specs=[_tile(D), _tile(PROJ), _const((1, D))],
        out_shape=[SDS((S, D), f32), SDS((S, PROJ), bf16), SDS((1, D), f32)],
        compiler_params=_CP, name="mix_in_bwd")(dxo, x, g, wint, *tabs, dvp, *d1, *d4, *d16, *after)


def _pool_sums(pad_ref, base, rows, adjoint):
    lane_group = lax.broadcasted_iota(jnp.int32, (rows, PW), 1) // 64
    sign = -1 if adjoint else 1

    def sh(o):
        return pad_ref[pl.ds(PAD + base + sign * o, rows), :]

    out = None
    acc = None
    lo, hi = 0, 0
    for gi, w in enumerate(POOL_WINDOWS):
        for o in list(range(-(w // 2), lo)) + list(range(hi, w - w // 2)):
            acc = sh(o) if acc is None else acc + sh(o)
        lo, hi = -(w // 2), w - w // 2
        out = acc if out is None else jnp.where(lane_group >= gi, acc, out)
    return out


def _pool_counts(base, rows):
    pos = base + lax.broadcasted_iota(jnp.int32, (rows, PW), 0)
    lane_group = lax.broadcasted_iota(jnp.int32, (rows, PW), 1) // 64
    cnt = None
    for gi, w in enumerate(POOL_WINDOWS):
        lo = jnp.maximum(pos - w // 2, 0)
        hi = jnp.minimum(pos + w - 1 - w // 2, S - 1)
        c = (hi - lo + 1).astype(f32)
        cnt = c if cnt is None else jnp.where(lane_group >= gi, c, cnt)
    return cnt


def _pool_fwd(vp, wbd, scale):
    ch = 256

    def body(vp_ref, w_ref, sc_ref, y_ref, diff_ref, pad):
        pad[pl.ds(0, PAD), :] = jnp.zeros((PAD, PW), f32)
        pad[pl.ds(PAD + S, PAD), :] = jnp.zeros((PAD, PW), f32)
        pad[pl.ds(PAD, S), :] = vp_ref[...]
        for b in range(S // ch):
            base = b * ch
            pooled = _pool_sums(pad, base, ch, False) / _pool_counts(base, ch)
            diff = (pooled - vp_ref[pl.ds(base, ch), :]).astype(bf16)
            diff_ref[pl.ds(base, ch), :] = diff
            y_ref[pl.ds(base, ch), :] = _dot_nn(diff, w_ref[...]) * sc_ref[...]

    whole = lambda shape: pl.BlockSpec(shape, lambda i: (0,) * len(shape))
    return pl.pallas_call(
        body, grid=(1,),
        in_specs=[whole((S, PW)), whole((PW, PW)), whole((1, PW))],
        out_specs=[whole((S, PW)), whole((S, PW))],
        out_shape=[SDS((S, PW), f32), SDS((S, PW), bf16)],
        scratch_shapes=[pltpu.VMEM((S + 2 * PAD, PW), f32)],
        compiler_params=_CP, name="pool_fwd")(vp, wbd, scale)


def _pool_bwd(dy, diff, wbd, scale, after=()):
    ch = 256

    def body(dy_ref, diff_ref, w_ref, sc_ref, dvp_ref, dw_ref, dsc_ref, pad):
        pad[pl.ds(0, PAD), :] = jnp.zeros((PAD, PW), f32)
        pad[pl.ds(PAD + S, PAD), :] = jnp.zeros((PAD, PW), f32)
        dw = jnp.zeros((PW, PW), f32)
        dsc = jnp.zeros((1, PW), f32)
        for b in range(S // ch):
            base = b * ch
            dy = dy_ref[pl.ds(base, ch), :]
            diff = diff_ref[pl.ds(base, ch), :]
            dsc = dsc + jnp.sum(dy * _dot_nn(diff, w_ref[...]), axis=0, keepdims=True)
            dz = (dy * sc_ref[...]).astype(bf16)
            dw = dw + _dot_tn(diff, dz)
            ddiff = _dot_nt(dz, w_ref[...])
            dvp_ref[pl.ds(base, ch), :] = -ddiff
            pad[pl.ds(PAD + base, ch), :] = ddiff / _pool_counts(base, ch)
        for gi in range(4):
            dw_ref[gi] = dw[64 * gi:64 * (gi + 1), 64 * gi:64 * (gi + 1)]
        dsc_ref[...] = dsc
        for b in range(S // ch):
            base = b * ch
            dvp_ref[pl.ds(base, ch), :] += _pool_sums(pad, base, ch, True)

    whole = lambda shape: pl.BlockSpec(shape, lambda i: (0,) * len(shape))
    return pl.pallas_call(
        _follow(body, 4, after), grid=(1,),
        in_specs=[whole((S, PW)), whole((S, PW)), whole((PW, PW)), whole((1, PW))] + [_ANY] * len(after),
        out_specs=[whole((S, PW)), whole((4, 64, 64)), whole((1, PW))],
        out_shape=[SDS((S, PW), f32), SDS((4, 64, 64), f32), SDS((1, PW), f32)],
        scratch_shapes=[pltpu.VMEM((S + 2 * PAD, PW), f32)],
        compiler_params=_CP, name="pool_bwd")(dy, diff, wbd, scale, *after)


def _attn_blocks(lc):
    bpc = lc // QB
    kw = min(2 * QB, lc)
    blocks = []
    for b in range(S // QB):
        t0 = (b % bpc) * QB
        ks_in = min(max(t0 - HALF, 0), lc - kw)
        blocks.append((b * QB, (b // bpc) * lc + ks_in, t0 - ks_in))
    return kw, blocks


def _attn_bias(bias_ref, kw, shifts):
    r = lax.broadcasted_iota(jnp.int32, (2 * QB, kw), 0) % QB
    c = lax.broadcasted_iota(jnp.int32, (2 * QB, kw), 1)
    for i, shift in enumerate(shifts):
        bias_ref[i] = jnp.where(jnp.abs(r + shift - c) <= HALF, 0.0, MASK_VALUE).astype(f32)


def _head_put(stats, pair, v0, v1, lane):
    return jnp.where(lane == 2 * pair, v0, jnp.where(lane == 2 * pair + 1, v1, stats))


def _head_cols(stats, pair, lane):
    c0 = jnp.sum(jnp.where(lane == 2 * pair, stats, 0.0), axis=-1, keepdims=True)
    c1 = jnp.sum(jnp.where(lane == 2 * pair + 1, stats, 0.0), axis=-1, keepdims=True)
    return jnp.concatenate([c0, c1], axis=0)


def _head_spread(stats, pair, head0):
    return jnp.where(head0, stats[:, 2 * pair:2 * pair + 1], stats[:, 2 * pair + 1:2 * pair + 2])


def _stack_heads(blk, head0):
    zero = jnp.zeros_like(blk)
    return jnp.concatenate([jnp.where(head0, blk, zero), jnp.where(head0, zero, blk)], axis=0)


def _attn_fwd(q, k, v, lc, after=None):
    kw, blocks = _attn_blocks(lc)
    shifts = sorted({b[2] for b in blocks})

    def body(q_ref, k_ref, v_ref, *refs):
        o_ref, lse_ref, bias_ref = refs[-3:]
        lane = lax.broadcasted_iota(jnp.int32, (QB, 128), 1)
        head0 = lane < 64
        pair = pl.program_id(0)
        _attn_bias(bias_ref, kw, shifts)

        @pl.when(pair == 0)
        def _():
            lse_ref[...] = jnp.zeros_like(lse_ref)

        for row0, kstart, shift in blocks:
            q2 = _stack_heads(q_ref[pl.ds(row0, QB), :], head0)
            kb = k_ref[pl.ds(kstart, kw), :]
            vb = v_ref[pl.ds(kstart, kw), :]
            s = _dot_nt(q2, kb) + bias_ref[shifts.index(shift)]
            m = jnp.max(s, axis=-1, keepdims=True)
            p = jnp.exp(s - m)
            den = jnp.sum(p, axis=-1, keepdims=True)
            o2 = _dot_nn(p.astype(bf16), vb) / den
            lse2 = m + jnp.log(den)
            o_ref[pl.ds(row0, QB), :] = jnp.where(head0, o2[:QB], o2[QB:]).astype(bf16)
            lse_ref[pl.ds(row0, QB), :] = _head_put(lse_ref[pl.ds(row0, QB), :], pair, lse2[:QB], lse2[QB:], lane)

    col = pl.BlockSpec((S, 128), lambda p: (0, p))
    extra = () if after is None else (after,)
    return pl.pallas_call(
        body, grid=(NG,), in_specs=[col, col, col] + [_ANY] * len(extra),
        out_specs=[col, pl.BlockSpec((S, 128), lambda p: (0, 0))],
        out_shape=[SDS((S, AW), bf16), SDS((S, 128), f32)],
        scratch_shapes=[pltpu.VMEM((len(shifts), 2 * QB, kw), f32)],
        compiler_params=_CP, name=f"attn_fwd_{lc}")(q, k, v, *extra)


def _attn_bwd(q, k, v, do, lse, delta, lc):
    kw, blocks = _attn_blocks(lc)
    shifts = sorted({b[2] for b in blocks})

    def body(q_ref, k_ref, v_ref, do_ref, lse_ref, dl_ref, dq_ref, dk_out, dv_out, bias_ref, dk_ref, dv_ref):
        lane = lax.broadcasted_iota(jnp.int32, (QB, 128), 1)
        head0 = lane < 64
        pair = pl.program_id(0)
        _attn_bias(bias_ref, kw, shifts)
        dk_ref[...] = jnp.zeros_like(dk_ref)
        dv_ref[...] = jnp.zeros_like(dv_ref)
        for row0, kstart, shift in blocks:
            q2 = _stack_heads(q_ref[pl.ds(row0, QB), :], head0)
            do2 = _stack_heads(do_ref[pl.ds(row0, QB), :], head0)
            lse2 = _head_cols(lse_ref[pl.ds(row0, QB), :], pair, lane)
            dl2 = _head_cols(dl_ref[pl.ds(row0, QB), :], pair, lane)
            kb = k_ref[pl.ds(kstart, kw), :]
            vb = v_ref[pl.ds(kstart, kw), :]
            p = jnp.exp(_dot_nt(q2, kb) + bias_ref[shifts.index(shift)] - lse2)
            ds = (p * (_dot_nt(do2, vb) - dl2)).astype(bf16)
            dq2 = _dot_nn(ds, kb)
            dq_ref[pl.ds(row0, QB), :] = jnp.where(head0, dq2[:QB], dq2[QB:]).astype(bf16)
            dk_ref[pl.ds(kstart, kw), :] += _dot_tn(ds, q2)
            dv_ref[pl.ds(kstart, kw), :] += _dot_tn(p.astype(bf16), do2)
        dk_out[...] = dk_ref[...].astype(bf16)
        dv_out[...] = dv_ref[...].astype(bf16)

    col = pl.BlockSpec((S, 128), lambda p: (0, p))
    stats = pl.BlockSpec((S, 128), lambda p: (0, 0))
    return pl.pallas_call(
        body, grid=(NG,), in_specs=[col] * 4 + [stats] * 2, out_specs=[col] * 3,
        out_shape=[SDS((S, AW), bf16)] * 3,
        scratch_shapes=[pltpu.VMEM((len(shifts), 2 * QB, kw), f32), pltpu.VMEM((S, 128), f32),
                        pltpu.VMEM((S, 128), f32)],
        compiler_params=_CP, name=f"attn_bwd_{lc}")(q, k, v, do, lse, delta)


def _mix_out_fwd(x, ypool, o1, l1, o4, l4, o16, l16, wout):
    def body(x_ref, yp_ref, o1_ref, l1_ref, o4_ref, l4_ref, o16_ref, l16_ref, w_ref,
             xo_ref, mixed_ref, o_ref, lse1_ref, lse4_ref, lse16_ref, sl4, sl16, sl):
        head0 = lax.broadcasted_iota(jnp.int32, (TM, 128), 1) < 64
        for r in range(4):
            sl4[pl.ds(r, TM // 4, stride=4), :] = l4_ref[r]
        for r in range(16):
            sl16[pl.ds(r, TM // 16, stride=16), :] = l16_ref[r]
        n4 = _dot_nn(_dilation_perm(4, True), o4_ref[...].reshape(TM, AW))
        n16 = _dot_nn(_dilation_perm(16, True), o16_ref[...].reshape(TM, AW))
        a, b, c = l1_ref[...], sl4[...], sl16[...]
        m = jnp.maximum(jnp.maximum(a, b), c)
        wa, wb, wc = jnp.exp(a - m), jnp.exp(b - m), jnp.exp(c - m)
        den = wa + wb + wc
        wa, wb, wc = wa / den, wb / den, wc / den
        lse = m + jnp.log(den)
        lse1_ref[...] = lse
        sl[...] = lse
        mixed_ref[:, :PW] = yp_ref[...].astype(bf16)
        for j in range(NG):
            y = (_head_spread(wa, j, head0) * o1_ref[:, _cols(j)].astype(f32)
                 + _head_spread(wb, j, head0) * n4[:, _cols(j)] + _head_spread(wc, j, head0) * n16[:, _cols(j)])
            o_ref[:, _cols(j)] = y
            mixed_ref[:, PW + 128 * j: PW + 128 * (j + 1)] = y.astype(bf16)
        for r in range(4):
            lse4_ref[r] = sl[pl.ds(r, TM // 4, stride=4), :]
        for r in range(16):
            lse16_ref[r] = sl[pl.ds(r, TM // 16, stride=16), :]
        xo_ref[...] = x_ref[...] + _dot_nn(mixed_ref[...], w_ref[...])

    return pl.pallas_call(
        body, grid=(S // TM,),
        in_specs=[_tile(D), _tile(PW), _tile(AW), _tile(128), _p4(), _p4(128), _p16(), _p16(128), _layer(D, D)],
        out_specs=[_tile(D), _tile(D), _tile(AW), _tile(128), _p4(128), _p16(128)],
        out_shape=[SDS((S, D), f32), SDS((S, D), bf16), SDS((S, AW), f32), SDS((S, 128), f32),
                   SDS((4, S // 4, 128), f32), SDS((16, S // 16, 128), f32)],
        scratch_shapes=[pltpu.VMEM((TM, 128), f32)] * 3,
        compiler_params=_CP, name="mix_out_fwd")(x, ypool, o1, l1, o4, l4, o16, l16, wout)


def _mix_out_bwd(dxo, o, wout):
    def body(dxo_ref, o_ref, w_ref, dxb_ref, dyp_ref, do1, do4, do16, dl1, dl4, dl16, sdl):
        dxb = dxo_ref[...].astype(bf16)
        dxb_ref[...] = dxb
        dm = _dot_nt(dxb, w_ref[...])
        dyp_ref[...] = dm[:, :PW]
        lane = lax.broadcasted_iota(jnp.int32, (TM, 128), 1)
        head0 = lane < 64
        dl = jnp.zeros((TM, 128), f32)
        for j in range(NG):
            d = dm[:, PW + 128 * j: PW + 128 * (j + 1)]
            prod = d * o_ref[:, _cols(j)]
            dl = _head_put(dl, j, jnp.sum(jnp.where(head0, prod, 0.0), axis=-1, keepdims=True),
                           jnp.sum(jnp.where(head0, 0.0, prod), axis=-1, keepdims=True), lane)
            do1[:, _cols(j)] = d.astype(bf16)
        dl1[...] = dl
        sdl[...] = dl
        for r in range(4):
            dl4[r] = sdl[pl.ds(r, TM // 4, stride=4), :]
        for r in range(16):
            dl16[r] = sdl[pl.ds(r, TM // 16, stride=16), :]
        nat = do1[...]
        do4[...] = _dot_nn(_dilation_perm(4), nat).astype(bf16).reshape(4, TM // 4, AW)
        do16[...] = _dot_nn(_dilation_perm(16), nat).astype(bf16).reshape(16, TM // 16, AW)

    return pl.pallas_call(
        body, grid=(S // TM,),
        in_specs=[_tile(D), _tile(AW), _layer(D, D)],
        out_specs=[_tile(D), _tile(PW), _tile(AW), _p4(), _p16(), _tile(128), _p4(128), _p16(128)],
        out_shape=[SDS((S, D), bf16), SDS((S, PW), f32),
                   SDS((S, AW), bf16), SDS((4, S // 4, AW), bf16), SDS((16, S // 16, AW), bf16),
                   SDS((S, 128), f32), SDS((4, S // 4, 128), f32), SDS((16, S // 16, 128), f32)],
        scratch_shapes=[pltpu.VMEM((TM, 128), f32)],
        compiler_params=_CP, name="mix_out_bwd")(dxo, o, wout)


def _loss_head(x, g, target):
    def body(x_ref, g_ref, t_ref, dx_ref, loss_ref, dg_ref):
        g = g_ref[...]
        r, xh, y = _rms(x_ref[...], g)
        err = y - t_ref[...]
        dy = err * (1.0 / D)

        @pl.when(pl.program_id(0) == 0)
        def _():
            loss_ref[...] = jnp.zeros_like(loss_ref)
            dg_ref[...] = jnp.zeros_like(dg_ref)

        loss_ref[...] += jnp.broadcast_to(0.5 * jnp.sum(jnp.mean(err * err, axis=-1, keepdims=True)), (1, D))
        dg_ref[...] += jnp.sum(dy * xh, axis=0, keepdims=True)
        dx_ref[...] = _rms_bwd(dy, r, xh, g)

    return pl.pallas_call(
        body, grid=(S // TM,),
        in_specs=[_tile(D), _const((1, D)), _tile(D)],
        out_specs=[_tile(D), _const((1, D)), _const((1, D))],
        out_shape=[SDS((S, D), f32), SDS((1, D), f32), SDS((1, D), f32)],
        compiler_params=_CP, name="loss_head")(x, g, target)


def _peer(k):
    x, y, c = lax.axis_index("x"), lax.axis_index("y"), lax.axis_index("c")
    px = 1 - x if k & 4 else x
    py = 1 - y if k & 2 else y
    pc = 1 - c if k & 1 else c
    return (px, py, pc), 4 * px + 2 * py + pc


def _diag_route():
    x, y, c = lax.axis_index("x"), lax.axis_index("y"), lax.axis_index("c")
    idx_x, idx_y = _peer(4)[1], _peer(2)[1]
    return idx_x + c * (idx_y - idx_x), (x + c * (1 - 2 * x), (1 - y) + c * (2 * y - 1), c)


def _all_gather(lands):
    n = len(lands)

    def body(*refs):
        zones, send_sems, recv_sems = refs[n:2 * n], refs[2 * n], refs[2 * n + 1]
        me, me_idx = _peer(0)
        sibling, sib_idx = _peer(1)
        (x_nbr, idx_x), (y_nbr, idx_y), idx_d = _peer(4), _peer(2), _peer(6)[1]
        fwd_idx, fwd_dev = _diag_route()

        def copy(k, t, idx, to):
            return _row_copy(zones[t], idx, send_sems.at[k, t], recv_sems.at[k, t], to)

        sent = []

        def send(k, t, idx, to):
            cp = copy(k, t, idx, to)
            cp.start()
            sent.append(cp)

        for t in range(n):
            send(0, t, me_idx, sibling)
            send(1, t, me_idx, x_nbr)
            send(2, t, me_idx, y_nbr)
        for t in range(n):
            copy(1, t, idx_x, me).wait_recv()
            send(3, t, idx_x, sibling)
        for t in range(n):
            copy(2, t, idx_y, me).wait_recv()
            send(4, t, idx_y, sibling)
        for t in range(n):
            send(5, t, fwd_idx, fwd_dev)
        for t in range(n):
            copy(5, t, idx_d, me).wait_recv()
            send(6, t, idx_d, sibling)
        for k, mask in ((0, 1), (3, 5), (4, 3), (6, 7)):
            for t in range(n):
                copy(k, t, _peer(mask)[1], me).wait_recv()
        for cp in sent:
            cp.wait_send()

    return pl.pallas_call(
        body, in_specs=[_ANY] * n, out_specs=[_ANY] * n,
        out_shape=[SDS(a.shape, a.dtype) for a in lands], input_output_aliases={t: t for t in range(n)},
        scratch_shapes=[pltpu.SemaphoreType.DMA((7, n)), pltpu.SemaphoreType.DMA((7, n))],
        name="all_gather_weights")(*lands)


def _hbm(a):
    return pltpu.with_memory_space_constraint(a, pltpu.HBM)


def _rows(ref, idx):
    r = ref.shape[0] // NDEV
    return ref.at[pl.ds(idx * r, r), :]


def _row_copy(ref, idx, send_sem, recv_sem, to):
    return pltpu.make_async_remote_copy(src_ref=_rows(ref, idx), dst_ref=_rows(ref, idx), send_sem=send_sem,
                                        recv_sem=recv_sem, device_id=to, device_id_type=_MESH)


def _place_own(me, shards, l):
    n = len(shards)

    def body(me_ref, *refs):
        for t in range(n):
            refs[n + t][...] = refs[t][...].astype(bf16)

    grid_spec = pltpu.PrefetchScalarGridSpec(
        num_scalar_prefetch=1, grid=(1,),
        in_specs=[pl.BlockSpec((None, s.shape[1], D), lambda i, me_ref: (l, 0, 0)) for s in shards],
        out_specs=[pl.BlockSpec((s.shape[1], D), lambda i, me_ref: (me_ref[0], 0)) for s in shards])
    return pl.pallas_call(
        body, grid_spec=grid_spec, out_shape=[SDS((NDEV * s.shape[1], D), bf16) for s in shards],
        compiler_params=_CP, name="place_own")(me, *shards)


_TOKEN = SDS((8, 128), f32)
def _ag_start(lands, after, l):
    n = len(lands)
    after = list(after) if isinstance(after, (list, tuple)) else [after]

    def body(*refs):
        zones, send_sems, recv_sems, token = refs[:n], refs[n + len(after)], refs[n + len(after) + 1], refs[-1]
        _, me_idx = _peer(0)
        for k, mask in enumerate((1, 4, 2)):
            for t in range(n):
                _row_copy(zones[t], me_idx, send_sems.at[k * n + t], recv_sems.at[k * n + t], _peer(mask)[0]).start()
        token[...] = jnp.zeros_like(token)

    outs = pl.pallas_call(
        body, name=f"ag_start_{l}", in_specs=[_HBM] * n + [_ANY] * len(after),
        out_specs=(_SEM, _SEM, *[_HBM] * n, pl.BlockSpec(memory_space=pltpu.VMEM)),
        out_shape=(pltpu.SemaphoreType.DMA((3 * n,)), pltpu.SemaphoreType.DMA((3 * n,)),
                   *[pltpu.HBM(a.shape, a.dtype) for a in lands], _TOKEN),
        input_output_aliases={t: 2 + t for t in range(n)}, compiler_params=_CP_SPLIT)(
            *[_hbm(a) for a in lands], *after)
    return outs[0], outs[1], list(outs[2:2 + n]), outs[-1]


def _ag_pass(lands, recv_sems, after, l):
    n = len(lands)
    after = list(after) if isinstance(after, (list, tuple)) else [after]

    def body(*refs):
        zones, recv_sems = refs[:n], refs[n]
        psend, precv, token = refs[n + 1 + len(after)], refs[n + 2 + len(after)], refs[-1]
        me, _ = _peer(0)
        sibling, _ = _peer(1)
        for j, mask in enumerate((4, 2)):
            idx = _peer(mask)[1]
            for t in range(n):
                _row_copy(zones[t], idx, psend.at[j * n + t], recv_sems.at[(1 + j) * n + t], me).wait_recv()
                _row_copy(zones[t], idx, psend.at[j * n + t], precv.at[j * n + t], sibling).start()
        fwd_idx, fwd_dev = _diag_route()
        for t in range(n):
            _row_copy(zones[t], fwd_idx, psend.at[2 * n + t], precv.at[2 * n + t], fwd_dev).start()
        token[...] = jnp.zeros_like(token)

    outs = pl.pallas_call(
        body, name=f"ag_pass_{l}", in_specs=[_HBM] * n + [_SEM] + [_ANY] * len(after),
        out_specs=(_SEM, _SEM, *[_HBM] * n, pl.BlockSpec(memory_space=pltpu.VMEM)),
        out_shape=(pltpu.SemaphoreType.DMA((3 * n,)), pltpu.SemaphoreType.DMA((3 * n,)),
                   *[pltpu.HBM(a.shape, a.dtype) for a in lands], _TOKEN),
        input_output_aliases={t: 2 + t for t in range(n)}, compiler_params=_CP_SPLIT)(*lands, recv_sems, *after)
    return outs[0], outs[1], list(outs[2:2 + n]), outs[-1]


def _ag_last(lands, precv, after, l):
    n = len(lands)
    after = list(after) if isinstance(after, (list, tuple)) else [after]

    def body(*refs):
        zones, precv = refs[:n], refs[n]
        qsend, qrecv, token = refs[n + 1 + len(after)], refs[n + 2 + len(after)], refs[-1]
        me, _ = _peer(0)
        sibling, _ = _peer(1)
        idx = _peer(6)[1]
        for t in range(n):
            _row_copy(zones[t], idx, qsend.at[t], precv.at[2 * n + t], me).wait_recv()
            _row_copy(zones[t], idx, qsend.at[t], qrecv.at[t], sibling).start()
        token[...] = jnp.zeros_like(token)

    outs = pl.pallas_call(
        body, name=f"ag_last_{l}", in_specs=[_HBM] * n + [_SEM] + [_ANY] * len(after),
        out_specs=(_SEM, _SEM, *[_HBM] * n, pl.BlockSpec(memory_space=pltpu.VMEM)),
        out_shape=(pltpu.SemaphoreType.DMA((n,)), pltpu.SemaphoreType.DMA((n,)),
                   *[pltpu.HBM(a.shape, a.dtype) for a in lands], _TOKEN),
        input_output_aliases={t: 2 + t for t in range(n)}, compiler_params=_CP_SPLIT)(*lands, precv, *after)
    return outs[0], outs[1], list(outs[2:2 + n]), outs[-1]


def _ag_wait(lands, send_sems, recv_sems, psend, precv, qsend, qrecv, after, l):
    n = len(lands)
    after = list(after) if isinstance(after, (list, tuple)) else [after]

    def body(*refs):
        zones = refs[:n]
        send_sems, recv_sems, psend, precv, qsend, qrecv = refs[n:n + 6]
        me, me_idx = _peer(0)
        for k in range(3):
            for t in range(n):
                _row_copy(zones[t], me_idx, send_sems.at[k * n + t], recv_sems.at[k * n + t], me).wait_send()
        for t in range(n):
            _row_copy(zones[t], _peer(1)[1], send_sems.at[t], recv_sems.at[t], me).wait_recv()
        fwd_idx, _ = _diag_route()
        for j, (mine, theirs) in enumerate(((_peer(4)[1], _peer(5)[1]), (_peer(2)[1], _peer(3)[1]))):
            for t in range(n):
                _row_copy(zones[t], mine, psend.at[j * n + t], precv.at[j * n + t], me).wait_send()
                _row_copy(zones[t], theirs, psend.at[j * n + t], precv.at[j * n + t], me).wait_recv()
        for t in range(n):
            _row_copy(zones[t], fwd_idx, psend.at[2 * n + t], precv.at[2 * n + t], me).wait_send()
            _row_copy(zones[t], _peer(6)[1], qsend.at[t], qrecv.at[t], me).wait_send()
            _row_copy(zones[t], _peer(7)[1], qsend.at[t], qrecv.at[t], me).wait_recv()

    outs = pl.pallas_call(
        body, name=f"ag_wait_{l}", in_specs=[_HBM] * n + [_SEM] * 6 + [_ANY] * len(after),
        out_specs=tuple([_HBM] * n), out_shape=tuple(pltpu.HBM(a.shape, a.dtype) for a in lands),
        input_output_aliases={t: t for t in range(n)}, compiler_params=_CP_SPLIT)(
            *lands, send_sems, recv_sems, psend, precv, qsend, qrecv, *after)
    return list(outs)


def _xchg_src(ref, slot_ref, idx):
    return _rows(ref, idx) if ref.shape[0] == NDEV * slot_ref.shape[1] else ref


def _rs_start(srcs, slots, after, tag):
    n = len(srcs)
    after = list(after) if isinstance(after, (list, tuple)) else [after]

    def body(*refs):
        src, slot = refs[:n], refs[n:2 * n]
        send_sems, recv_sems, token = refs[2 * n + len(after)], refs[2 * n + len(after) + 1], refs[-1]
        _, me_idx = _peer(0)
        for k in range(1, NDEV):
            dev, idx = _peer(k)
            for t in range(n):
                pltpu.make_async_remote_copy(
                    src_ref=_xchg_src(src[t], slot[t], idx), dst_ref=slot[t].at[me_idx],
                    send_sem=send_sems.at[(k - 1) * n + t], recv_sem=recv_sems.at[(k - 1) * n + t],
                    device_id=dev, device_id_type=_MESH).start()
        token[...] = jnp.zeros_like(token)

    outs = pl.pallas_call(
        body, name=f"rs_start_{tag}", in_specs=[_HBM] * (2 * n) + [_ANY] * len(after),
        out_specs=(_SEM, _SEM, *[_HBM] * (2 * n), pl.BlockSpec(memory_space=pltpu.VMEM)),
        out_shape=(pltpu.SemaphoreType.DMA(((NDEV - 1) * n,)), pltpu.SemaphoreType.DMA(((NDEV - 1) * n,)),
                   *[pltpu.HBM(a.shape, a.dtype) for a in list(srcs) + list(slots)], _TOKEN),
        input_output_aliases={t: 2 + t for t in range(2 * n)}, compiler_params=_CP_SPLIT)(
            *[_hbm(a) for a in list(srcs) + list(slots)], *after)
    return outs[0], outs[1], list(outs[2:2 + n]), list(outs[2 + n:2 + 2 * n]), outs[-1]


def _rs_wait(srcs, slots, send_sems, recv_sems, after, tag):
    n = len(srcs)
    after = list(after) if isinstance(after, (list, tuple)) else [after]

    def body(*refs):
        src, slot, send_sems, recv_sems = refs[:n], refs[n:2 * n], refs[2 * n], refs[2 * n + 1]
        me, _ = _peer(0)
        for k in range(1, NDEV):
            idx = _peer(k)[1]
            for t in range(n):
                cp = pltpu.make_async_remote_copy(
                    src_ref=_xchg_src(src[t], slot[t], idx), dst_ref=slot[t].at[idx],
                    send_sem=send_sems.at[(k - 1) * n + t], recv_sem=recv_sems.at[(k - 1) * n + t],
                    device_id=me, device_id_type=_MESH)
                cp.wait_send()
                cp.wait_recv()

    outs = pl.pallas_call(
        body, name=f"rs_wait_{tag}", in_specs=[_HBM] * (2 * n) + [_SEM, _SEM] + [_ANY] * len(after),
        out_specs=tuple([_HBM] * (2 * n)),
        out_shape=tuple(pltpu.HBM(a.shape, a.dtype) for a in list(srcs) + list(slots)),
        input_output_aliases={t: t for t in range(2 * n)}, compiler_params=_CP_SPLIT)(
            *srcs, *slots, send_sems, recv_sems, *after)
    return list(outs[:n]), list(outs[n:])


def _pair_start(full4s, bufs, after, tag):
    n = len(full4s)
    after = list(after) if isinstance(after, (list, tuple)) else [after]

    def body(*refs):
        full, buf = refs[:n], refs[n:2 * n]
        send_sems, recv_sems, token = refs[2 * n + len(after)], refs[2 * n + len(after) + 1], refs[-1]
        c = lax.axis_index("c")
        for t in range(n):
            pltpu.make_async_remote_copy(src_ref=full[t].at[:, 1 - c], dst_ref=buf[t], send_sem=send_sems.at[t],
                                         recv_sem=recv_sems.at[t], device_id=_peer(1)[0], device_id_type=_MESH).start()
        token[...] = jnp.zeros_like(token)

    outs = pl.pallas_call(
        body, name=f"pair_start_{tag}", in_specs=[_HBM] * (2 * n) + [_ANY] * len(after),
        out_specs=(_SEM, _SEM, *[_HBM] * (2 * n), pl.BlockSpec(memory_space=pltpu.VMEM)),
        out_shape=(pltpu.SemaphoreType.DMA((n,)), pltpu.SemaphoreType.DMA((n,)),
                   *[pltpu.HBM(a.shape, a.dtype) for a in list(full4s) + list(bufs)], _TOKEN),
        input_output_aliases={t: 2 + t for t in range(2 * n)}, compiler_params=_CP_SPLIT)(
            *[_hbm(a) for a in list(full4s) + list(bufs)], *after)
    return outs[0], outs[1], list(outs[2:2 + n]), list(outs[2 + n:2 + 2 * n]), outs[-1]


def _pair_wait(full4s, bufs, send_sems, recv_sems, after, tag):
    n = len(full4s)
    after = list(after) if isinstance(after, (list, tuple)) else [after]

    def body(*refs):
        full, buf, send_sems, recv_sems = refs[:n], refs[n:2 * n], refs[2 * n], refs[2 * n + 1]
        c = lax.axis_index("c")
        for t in range(n):
            cp = pltpu.make_async_remote_copy(src_ref=full[t].at[:, 1 - c], dst_ref=buf[t], send_sem=send_sems.at[t],
                                              recv_sem=recv_sems.at[t], device_id=_peer(0)[0], device_id_type=_MESH)
            cp.wait_send()
            cp.wait_recv()

    outs = pl.pallas_call(
        body, name=f"pair_wait_{tag}", in_specs=[_HBM] * (2 * n) + [_SEM, _SEM] + [_ANY] * len(after),
        out_specs=tuple([_HBM] * (2 * n)),
        out_shape=tuple(pltpu.HBM(a.shape, a.dtype) for a in list(full4s) + list(bufs)),
        input_output_aliases={t: t for t in range(2 * n)}, compiler_params=_CP_SPLIT)(
            *full4s, *bufs, send_sems, recv_sems, *after)
    return list(outs[:n]), list(outs[n:])


def _pair_sum(core, full4s, bufs):
    n = len(full4s)

    def body(core_ref, *refs):
        for t in range(n):
            refs[2 * n + t][...] = (refs[t][...].astype(f32) + refs[n + t][...].astype(f32)).astype(bf16)

    grid_spec = pltpu.PrefetchScalarGridSpec(
        num_scalar_prefetch=1, grid=(4,),
        in_specs=[pl.BlockSpec((None, None) + a.shape[2:], lambda j, core_ref: (j, core_ref[0], 0, 0)) for a in full4s]
        + [pl.BlockSpec((None,) + b.shape[1:], lambda j, core_ref: (j, 0, 0)) for b in bufs],
        out_specs=[pl.BlockSpec((None,) + b.shape[1:], lambda j, core_ref: (j, 0, 0)) for b in bufs])
    return pl.pallas_call(
        body, grid_spec=grid_spec, out_shape=[SDS(b.shape, bf16) for b in bufs],
        compiler_params=_CP, name="pair_sum")(core, *full4s, *bufs)


def _chip_start(sums, slots, after, tag):
    n = len(sums)
    after = list(after) if isinstance(after, (list, tuple)) else [after]

    def body(*refs):
        src, slot = refs[:n], refs[n:2 * n]
        send_sems, recv_sems, token = refs[2 * n + len(after)], refs[2 * n + len(after) + 1], refs[-1]
        my_chip = 2 * lax.axis_index("x") + lax.axis_index("y")
        for k, mask in enumerate((4, 2, 6)):
            dev, _ = _peer(mask)
            for t in range(n):
                pltpu.make_async_remote_copy(
                    src_ref=src[t].at[2 * dev[0] + dev[1]], dst_ref=slot[t].at[my_chip],
                    send_sem=send_sems.at[k * n + t], recv_sem=recv_sems.at[k * n + t],
                    device_id=dev, device_id_type=_MESH).start()
        token[...] = jnp.zeros_like(token)

    outs = pl.pallas_call(
        body, name=f"chip_start_{tag}", in_specs=[_HBM] * (2 * n) + [_ANY] * len(after),
        out_specs=(_SEM, _SEM, *[_HBM] * (2 * n), pl.BlockSpec(memory_space=pltpu.VMEM)),
        out_shape=(pltpu.SemaphoreType.DMA((3 * n,)), pltpu.SemaphoreType.DMA((3 * n,)),
                   *[pltpu.HBM(a.shape, a.dtype) for a in list(sums) + list(slots)], _TOKEN),
        input_output_aliases={t: 2 + t for t in range(2 * n)}, compiler_params=_CP_SPLIT)(
            *[_hbm(a) for a in list(sums) + list(slots)], *after)
    return outs[0], outs[1], list(outs[2:2 + n]), list(outs[2 + n:2 + 2 * n]), outs[-1]


def _chip_wait(sums, slots, send_sems, recv_sems, after, tag):
    n = len(sums)
    after = list(after) if isinstance(after, (list, tuple)) else [after]

    def body(*refs):
        src, slot, send_sems, recv_sems = refs[:n], refs[n:2 * n], refs[2 * n], refs[2 * n + 1]
        for k, mask in enumerate((4, 2, 6)):
            dev, _ = _peer(mask)
            chip = 2 * dev[0] + dev[1]
            for t in range(n):
                cp = pltpu.make_async_remote_copy(
                    src_ref=src[t].at[chip], dst_ref=slot[t].at[chip],
                    send_sem=send_sems.at[k * n + t], recv_sem=recv_sems.at[k * n + t],
                    device_id=_peer(0)[0], device_id_type=_MESH)
                cp.wait_send()
                cp.wait_recv()

    outs = pl.pallas_call(
        body, name=f"chip_wait_{tag}", in_specs=[_HBM] * (2 * n) + [_SEM, _SEM] + [_ANY] * len(after),
        out_specs=tuple([_HBM] * (2 * n)),
        out_shape=tuple(pltpu.HBM(a.shape, a.dtype) for a in list(sums) + list(slots)),
        input_output_aliases={t: t for t in range(2 * n)}, compiler_params=_CP_SPLIT)(
            *sums, *slots, send_sems, recv_sems, *after)
    return list(outs[:n]), list(outs[n:])


def _sum_slots(slots, rb):
    r = slots.shape[1]

    def body(s_ref, o_ref):
        acc = s_ref[0].astype(f32)
        for s in range(1, NDEV):
            acc = acc + s_ref[s].astype(f32)
        o_ref[...] = acc

    return pl.pallas_call(
        body, grid=(r // rb,),
        in_specs=[pl.BlockSpec((NDEV, rb, D), lambda i: (0, i, 0))],
        out_specs=pl.BlockSpec((rb, D), lambda i: (i, 0)),
        out_shape=SDS((r, D), f32), compiler_params=_CP, name="sum_slots")(slots)


def _adamw(w, g, m, v):
    shape = w.shape
    cols = shape[-1]
    rows = w.size // cols
    rb = rows
    for cand in (512, 256, 128, 64, 32, 16, 8):
        if rows % cand == 0 and rows > cand:
            rb = cand
            break

    def body(w_ref, g_ref, m_ref, v_ref, d_ref, mo_ref, vo_ref):
        d_ref[...], mo_ref[...], vo_ref[...] = _adamw_math(w_ref[...], g_ref[...], m_ref[...], v_ref[...])

    spec = pl.BlockSpec((rb, cols), lambda i: (i, 0))
    outs = pl.pallas_call(
        body, grid=(rows // rb,), in_specs=[spec] * 4, out_specs=[spec] * 3,
        out_shape=[SDS((rows, cols), f32)] * 3, compiler_params=_CP, name="adamw")(
            *(a.reshape(rows, cols) for a in (w, g, m, v)))
    return tuple(o.reshape(shape) for o in outs)


def _adamw_math(w, g, m, v):
    m = ADAM_B1 * m + (1.0 - ADAM_B1) * g
    v = ADAM_B2 * v + (1.0 - ADAM_B2) * (g * g)
    m_hat = m / (1.0 - ADAM_B1 ** ADAM_STEP)
    v_hat = v / (1.0 - ADAM_B2 ** ADAM_STEP)
    return -ADAM_LR * (m_hat / (jnp.sqrt(v_hat) + ADAM_EPS) + ADAM_WD * w), m, v


def _reduce_adamw(acc, me, full, slots, w, m, v, l):
    _, r, _ = w.shape
    ns = slots.shape[0]
    rb = r // 2 if r > 128 else r

    def body(me_ref, full_ref, slots_ref, w_ref, m_ref, v_ref, *refs):
        go_ref, d_ref, mo_ref, vo_ref = refs[-4:]
        own = full_ref[...].astype(f32)
        g = None
        for s in range(ns):
            part = jnp.where(me_ref[0] == s, own, slots_ref[s].astype(f32))
            g = part if g is None else g + part
        go_ref[...] = g
        d_ref[...], mo_ref[...], vo_ref[...] = _adamw_math(w_ref[...], g, m_ref[...], v_ref[...])

    steps = r // rb
    lay = pl.BlockSpec((None, rb, D), lambda i, me_ref: (l, i, 0))
    n_acc = 0 if acc is None else 4
    grid_spec = pltpu.PrefetchScalarGridSpec(
        num_scalar_prefetch=1, grid=(steps,),
        in_specs=[pl.BlockSpec((rb, D), lambda i, me_ref: (me_ref[0] * steps + i, 0)),
                  pl.BlockSpec((ns, rb, D), lambda i, me_ref: (0, i, 0)), lay, lay, lay] + [_ANY] * n_acc,
        out_specs=[lay] * 4)
    outs = pl.pallas_call(
        body, grid_spec=grid_spec, out_shape=[SDS(w.shape, f32)] * 4,
        input_output_aliases={6 + j: j for j in range(n_acc)},
        compiler_params=_CP, name="reduce_adamw")(me, full, slots, w, m, v, *(() if acc is None else acc))
    return tuple(outs)


_BIG = ("ffn1_w_gate", "ffn1_w_up", "ffn1_w_down", "w_in", "w_out", "ffn2_w_gate", "ffn2_w_up", "ffn2_w_down")
_TRANSPOSED = ("ffn1_w_gate", "ffn1_w_up", "w_in", "ffn2_w_gate", "ffn2_w_up")

def _block_diag(pool_w):
    out = jnp.zeros((L, PW, PW), pool_w.dtype)
    for gi in range(4):
        out = out.at[:, 64 * gi:64 * (gi + 1), 64 * gi:64 * (gi + 1)].set(pool_w[:, gi])
    return out


def kernel(x, positions, ffn1_norm, ffn1_w_gate, ffn1_w_up, ffn1_w_down, mix_norm, w_in, pool_w, pool_scale, w_out, ffn2_norm, ffn2_w_gate, ffn2_w_up, ffn2_w_down, final_norm, loss_target, m_ffn1_norm, m_ffn1_w_gate, m_ffn1_w_up, m_ffn1_w_down, m_mix_norm, m_w_in, m_pool_w, m_pool_scale, m_w_out, m_ffn2_norm, m_ffn2_w_gate, m_ffn2_w_up, m_ffn2_w_down, m_final_norm, v_ffn1_norm, v_ffn1_w_gate, v_ffn1_w_up, v_ffn1_w_down, v_mix_norm, v_w_in, v_pool_w, v_pool_scale, v_w_out, v_ffn2_norm, v_ffn2_w_gate, v_ffn2_w_up, v_ffn2_w_down, v_final_norm):
    weights = dict(ffn1_norm=ffn1_norm, ffn1_w_gate=ffn1_w_gate, ffn1_w_up=ffn1_w_up, ffn1_w_down=ffn1_w_down,
                   mix_norm=mix_norm, w_in=w_in, pool_w=pool_w, pool_scale=pool_scale, w_out=w_out,
                   ffn2_norm=ffn2_norm, ffn2_w_gate=ffn2_w_gate, ffn2_w_up=ffn2_w_up, ffn2_w_down=ffn2_w_down,
                   final_norm=final_norm)
    moms = dict(ffn1_norm=m_ffn1_norm, ffn1_w_gate=m_ffn1_w_gate, ffn1_w_up=m_ffn1_w_up, ffn1_w_down=m_ffn1_w_down,
                mix_norm=m_mix_norm, w_in=m_w_in, pool_w=m_pool_w, pool_scale=m_pool_scale, w_out=m_w_out,
                ffn2_norm=m_ffn2_norm, ffn2_w_gate=m_ffn2_w_gate, ffn2_w_up=m_ffn2_w_up, ffn2_w_down=m_ffn2_w_down,
                final_norm=m_final_norm)
    vels = dict(ffn1_norm=v_ffn1_norm, ffn1_w_gate=v_ffn1_w_gate, ffn1_w_up=v_ffn1_w_up, ffn1_w_down=v_ffn1_w_down,
                mix_norm=v_mix_norm, w_in=v_w_in, pool_w=v_pool_w, pool_scale=v_pool_scale, w_out=v_w_out,
                ffn2_norm=v_ffn2_norm, ffn2_w_gate=v_ffn2_w_gate, ffn2_w_up=v_ffn2_w_up, ffn2_w_down=v_ffn2_w_down,
                final_norm=v_final_norm)
    names = list(weights)

    me_idx = 4 * lax.axis_index("x") + 2 * lax.axis_index("y") + lax.axis_index("c")
    me_arr = me_idx.reshape(1).astype(jnp.int32)

    as_rows = lambda a, nm: jnp.swapaxes(a, 1, 2) if nm in _TRANSPOSED else a
    w_rows = {nm: as_rows(weights[nm], nm) for nm in _BIG}
    m_rows = {nm: as_rows(moms[nm], nm) for nm in _BIG}
    v_rows = {nm: as_rows(vels[nm], nm) for nm in _BIG}

    def landing_zones(l, which):
        return _place_own(me_arr, [w_rows[_BIG[t]] for t in which], l)

    g_ffn1 = [ffn1_norm[l].reshape(1, D) for l in range(L)]
    g_mix = [mix_norm[l].reshape(1, D) for l in range(L)]
    g_ffn2 = [ffn2_norm[l].reshape(1, D) for l in range(L)]
    wbd_all = _block_diag(pool_w).astype(bf16)
    wbd = [wbd_all[l] for l in range(L)]
    pscale = [pool_scale[l].reshape(1, PW) for l in range(L)]
    tabs = _rope_tables(positions)
    flat = lambda a: a.reshape(S, a.shape[-1])
    r4 = lambda a: a.reshape(4, S // 4, a.shape[-1])
    r16 = lambda a: a.reshape(16, S // 16, a.shape[-1])

    first, rest, whole = (0, 1, 2, 3), (4, 5, 6, 7), tuple(range(8))

    def ag_begin(l, which, after, zones=None):
        tag = f"{l}{'' if which == whole else 'r'}"
        zones = landing_zones(l, which) if zones is None else zones
        send_sems, recv_sems, zones, token = _ag_start(zones, after, tag)
        return dict(tag=tag, zones=zones, s=send_sems, r=recv_sems), token

    def ag_second(ch, after):
        ch["ps"], ch["pr"], ch["zones"], token = _ag_pass(ch["zones"], ch["r"], after, ch["tag"])
        return token

    def ag_third(ch, after):
        ch["qs"], ch["qr"], ch["zones"], token = _ag_last(ch["zones"], ch["pr"], after, ch["tag"])
        return token

    def ag_end(ch, after):
        return _ag_wait(ch["zones"], ch["s"], ch["r"], ch["ps"], ch["pr"], ch["qs"], ch["qr"], after, ch["tag"])

    head = _all_gather(landing_zones(0, first))
    ch_rest, tok_rest = ag_begin(0, rest, head[0])
    chains = {}
    gathered = [None] * L
    xs = x.reshape(S, D)
    saved = []
    for l in range(L):
        first_after, second_after = (), ()
        if l == 0:
            gt1, ut1, dn1, wint = head
            first_after = (tok_rest,)
        else:
            gt1, ut1, dn1, wint, wout, gt2, ut2, dn2 = gathered[l]
        x0 = xs
        x1, gate1, up1 = _ffn_fwd(x0, g_ffn1[l], gt1, ut1, dn1, after=first_after)
        hmix, vp, q1, k1, v1, q4, k4, v4, q16, k16, v16 = _mix_in_fwd(x1, g_mix[l], wint, tabs)
        q4, k4, v4, q16, k16, v16 = map(flat, (q4, k4, v4, q16, k16, v16))
        ypool, diff = _pool_fwd(vp, wbd[l], pscale[l])
        after_attn = None
        if l == 0:
            chains[1], tok_next = ag_begin(1, whole, x1)
            after_attn = ag_second(ch_rest, [ypool, q16, tok_next])
        o1, l1 = _attn_fwd(q1, k1, v1, S, after=after_attn)
        o4, l4 = _attn_fwd(q4, k4, v4, S // 4, after=after_attn)
        o16, l16 = _attn_fwd(q16, k16, v16, S // 16, after=after_attn)
        if l == 0:
            early_zones = {ll: landing_zones(ll, whole) for ll in range(2, L)}
            token = ag_third(ch_rest, [o1, o4, o16] + [z for zs in early_zones.values() for z in zs])
            wout, gt2, ut2, dn2 = ag_end(ch_rest, token)
            gathered[0] = list(head) + [wout, gt2, ut2, dn2]
        elif l + 1 < L:
            second_after = (ag_second(chains[l + 1], [o1, o4, o16]),)
        x2, mixed, o, lse1, lse4, lse16 = _mix_out_fwd(x1, ypool, o1, l1, r4(o4), r4(l4), r16(o16), r16(l16), wout)
        if l == 0:
            second_after = (ag_second(chains[1], x2),)
        x3, gate2, up2 = _ffn_fwd(x2, g_ffn2[l], gt2, ut2, dn2, after=second_after)
        if l + 1 < L:
            token = ag_third(chains[l + 1], x3)
            if l + 2 < L:
                chains[l + 2], token = ag_begin(l + 2, whole, token, early_zones[l + 2])
            gathered[l + 1] = ag_end(chains[l + 1], token)
        saved.append(dict(x0=x0, x1=x1, x2=x2, gate1=gate1, up1=up1, gate2=gate2, up2=up2, hmix=hmix, diff=diff,
                          qkv=((q1, k1, v1), (q4, k4, v4), (q16, k16, v16)), mixed=mixed, o=o,
                          lse=(lse1, flat(lse4), flat(lse16))))
        xs = x3

    dx, loss_part, d_final = _loss_head(xs, final_norm.reshape(1, D), loss_target.reshape(S, D))

    d_norm = {nm: [None] * L for nm in ("ffn1_norm", "mix_norm", "ffn2_norm")}
    d_poolw, d_pscale = [None] * L, [None] * L
    group_a = ("ffn2_w_gate", "ffn2_w_up", "ffn2_w_down", "w_out")
    group_b = ("ffn1_w_gate", "ffn1_w_up", "ffn1_w_down", "w_in")
    acc = {}

    def exchange(full, group, after, tag):
        srcs = [full[nm] for nm in group]
        slots = [lax.empty((NDEV, g.shape[0] // NDEV, D), bf16) for g in srcs]
        ssem, rsem, srcs, slots, token = _rs_start(srcs, slots, after, tag)
        return (srcs, slots, ssem, rsem, tag), token

    def update(l, group, flight, after):
        srcs, slots, ssem, rsem, tag = flight
        srcs, slots = _rs_wait(srcs, slots, ssem, rsem, after, tag)
        for nm, full_g, slots_g in zip(group, srcs, slots):
            acc[nm] = _reduce_adamw(acc.get(nm), me_arr, full_g, slots_g, w_rows[nm], m_rows[nm], v_rows[nm], l)
        return [acc[nm][0] for nm in group], slots

    core_arr = lax.axis_index("c").reshape(1).astype(jnp.int32)
    chip_arr = (2 * lax.axis_index("x") + lax.axis_index("y")).reshape(1).astype(jnp.int32)

    def exchange_cores(full, group, after, tag):
        full4s = [full[nm].reshape(4, 2, full[nm].shape[0] // NDEV, D) for nm in group]
        bufs = [lax.empty((4,) + a.shape[2:], bf16) for a in full4s]
        ssem, rsem, full4s, bufs, token = _pair_start(full4s, bufs, after, tag)
        return (full4s, bufs, ssem, rsem, tag), token

    def exchange_chips(flight, after):
        full4s, bufs, ssem, rsem, tag = flight
        full4s, bufs = _pair_wait(full4s, bufs, ssem, rsem, after, tag)
        sums = _pair_sum(core_arr, full4s, bufs)
        slots = [lax.empty(a.shape, bf16) for a in sums]
        ssem, rsem, sums, slots, token = _chip_start(sums, slots, bufs[0], tag)
        return (sums, slots, ssem, rsem, tag), token

    def update_chips(l, group, flight, after):
        sums, slots, ssem, rsem, tag = flight
        sums, slots = _chip_wait(sums, slots, ssem, rsem, after, tag)
        for nm, sums_g, slots_g in zip(group, sums, slots):
            own = sums_g.reshape(4 * sums_g.shape[1], D)
            acc[nm] = _reduce_adamw(acc.get(nm), chip_arr, own, slots_g, w_rows[nm], m_rows[nm], v_rows[nm], l)
        return [acc[nm][0] for nm in group]

    flights = {}
    token_b = None
    for l in reversed(range(L)):
        sv = saved[l]
        gt1, ut1, dn1, wint, wout, gt2, ut2, dn2 = gathered[l]
        full = {}
        dx, dgate, dup, h, dy, d_norm["ffn2_norm"][l] = _ffn_bwd_d(
            sv["x2"], g_ffn2[l], sv["gate2"], sv["up2"], dx, gt2, ut2, dn2, after=() if token_b is None else (token_b,))
        full["ffn2_w_gate"], full["ffn2_w_up"], full["ffn2_w_down"] = _ffn_bwd_w(h, dy, sv["gate2"], sv["up2"], dgate, dup)

        dxb, dyp, do1, do4, do16, dl1, dl4, dl16 = _mix_out_bwd(dx, sv["o"], wout)
        full["w_out"] = _wgrad(sv["mixed"], dxb)
        flights[l, "a"], token_a = (exchange_cores if l == 0 else exchange)(full, group_a, dxb, f"a{l}")
        dvp, d_poolw[l], d_pscale[l] = _pool_bwd(dyp, sv["diff"], wbd[l], pscale[l], after=(token_a,))
        dos, dls = (do1, flat(do4), flat(do16)), (dl1, flat(dl4), flat(dl16))
        dqkv = []
        for b, lc in enumerate((S, S // 4, S // 16)):
            qb, kb, vb = sv["qkv"][b]
            dqkv.append(_attn_bwd(qb, kb, vb, dos[b], sv["lse"][b], dls[b], lc))
        d4 = tuple(r4(a) for a in dqkv[1])
        d16 = tuple(r16(a) for a in dqkv[2])
        mix_after = ()
        if l == 0:
            flights[0, "a"], token_a = exchange_chips(flights[0, "a"], [dqkv[0][0], dqkv[1][0], dqkv[2][0]])
            mix_after = (token_a,)
        dx, dproj, d_norm["mix_norm"][l] = _mix_in_bwd(dx, sv["x1"], g_mix[l], wint, tabs, dvp, dqkv[0], d4, d16,
                                                       after=mix_after)
        full["w_in"] = _wgrad(dproj, sv["hmix"])

        dx, dgate, dup, h, dy, d_norm["ffn1_norm"][l] = _ffn_bwd_d(sv["x0"], g_ffn1[l], sv["gate1"], sv["up1"], dx, gt1, ut1, dn1)
        full["ffn1_w_gate"], full["ffn1_w_up"], full["ffn1_w_down"] = _ffn_bwd_w(h, dy, sv["gate1"], sv["up1"], dgate, dup)

        after = dx
        if l + 1 < L and l + 1 >= 2:
            after, _ = update(l + 1, group_a, flights.pop((l + 1, "a")), after)
        if l + 1 < L and l + 1 >= 3:
            after, _ = update(l + 1, group_b, flights.pop((l + 1, "b")), after)
        if l > 0:
            flights[l, "b"], token_b = exchange(full, group_b, after, f"b{l}")

    flights[0, "b"], token_b = exchange_cores(full, group_b, dx, "b0")
    pad8 = lambda a: jnp.pad(a, ((0, 8 - a.shape[0]), (0, 0)))
    misc = jnp.concatenate([d_final, jnp.concatenate(d_pscale, axis=1), loss_part], axis=0)
    small = jnp.concatenate(
        [pad8(jnp.concatenate(d_norm[nm], axis=0)) for nm in ("ffn1_norm", "mix_norm", "ffn2_norm")]
        + [pad8(misc), jnp.stack(d_poolw).reshape(L * 16, D)], axis=0)
    small_slots = lax.dynamic_update_slice(lax.empty((NDEV, SMALL_ROWS, D), f32), small[None], (me_idx, 0, 0))
    pack_sems = _rs_start([small], [small_slots], token_b, "pack")
    flights[0, "b"], token_b = exchange_chips(flights[0, "b"], pack_sems[-1])

    after = token_b
    for key in [(2, "b"), (1, "a"), (1, "b")]:
        after, _ = update(key[0], group_a if key[1] == "a" else group_b, flights.pop(key), after)
    _, pack_slots = _rs_wait(pack_sems[2], pack_sems[3], pack_sems[0], pack_sems[1], after, "pack")
    sm = _sum_slots(pack_slots[0], SMALL_ROWS)
    grads = {}
    grads["ffn1_norm"], grads["mix_norm"], grads["ffn2_norm"] = sm[0:L], sm[8:8 + L], sm[16:16 + L]
    grads["final_norm"] = sm[24]
    grads["pool_scale"] = sm[25].reshape(L, PW)
    grads["pool_w"] = sm[32:32 + L * 16].reshape(L, 4, 64, 64)
    loss = sm[26, 0]
    upd = {nm: _adamw(weights[nm], grads[nm], moms[nm], vels[nm]) for nm in names if nm not in _BIG}
    after = update_chips(0, group_a, flights.pop((0, "a")), [upd[nm][0] for nm in upd])
    update_chips(0, group_b, flights.pop((0, "b")), after)
    for nm in _BIG:
        grads[nm], upd[nm] = as_rows(acc[nm][0], nm), tuple(as_rows(a, nm) for a in acc[nm][1:])
    return (loss, dx.reshape(1, S, D), *[grads[nm] for nm in names], *[upd[nm][0] for nm in names],
            *[upd[nm][1] for nm in names], *[upd[nm][2] for nm in names])
```

```python
import jax
import jax.numpy as jnp
from jax import lax
from jax.experimental import pallas as pl
from jax.experimental.pallas import tpu as pltpu

f32 = jnp.float32
bf16 = jnp.bfloat16
SDS = jax.ShapeDtypeStruct

D = 1024
S = 2048
F = 2816
L = 4
PW = 256
AW = 768
PROJ = PW + 3 * AW
NDEV = 8
TM = 256
QB = 128
HALF = 64
NG = AW // 128
NORM_EPS = 1e-6
MASK_VALUE = -1e30
ROPE_THETA = 500000.0
ADAM_LR, ADAM_B1, ADAM_B2, ADAM_EPS, ADAM_WD, ADAM_STEP = 0.001, 0.9, 0.999, 1e-08, 0.01, 10
POOL_WINDOWS = (2, 4, 8, 16)
PAD = 8
SMALL_ROWS = 96
VMEM_LIMIT = 56 * 1024 * 1024

_CP = pltpu.CompilerParams(vmem_limit_bytes=VMEM_LIMIT)
_ANY = pl.BlockSpec(memory_space=pl.ANY)
_HBM = pl.BlockSpec(memory_space=pltpu.HBM)
_SEM = pl.BlockSpec(memory_space=pltpu.SEMAPHORE)
_MESH = pl.DeviceIdType.MESH
_CP_SPLIT = pltpu.CompilerParams(has_side_effects=pltpu.SideEffectType.DATAFLOW_SIDE_EFFECTING)


def _dot_nn(a, b):
    return lax.dot_general(a, b, (((1,), (0,)), ((), ())), preferred_element_type=f32)


def _dot_nt(a, b):
    return lax.dot_general(a, b, (((1,), (1,)), ((), ())), preferred_element_type=f32)


def _dot_tn(a, b):
    return lax.dot_general(a, b, (((0,), (0,)), ((), ())), preferred_element_type=f32)


def _rms(x, g):
    r = lax.rsqrt(jnp.mean(x * x, axis=-1, keepdims=True) + NORM_EPS)
    xh = x * r
    return r, xh, xh * g


def _rms_bwd(dh, r, xh, g):
    dxh = dh * g
    return r * (dxh - xh * jnp.mean(dxh * xh, axis=-1, keepdims=True))


def _tile(cols, rows=TM):
    return pl.BlockSpec((rows, cols), lambda i: (i, 0))


def _const(shape):
    return pl.BlockSpec(shape, lambda i: (0,) * len(shape))


def _layer(rows, cols):
    return pl.BlockSpec((rows, cols), lambda i: (0, 0), pipeline_mode=pl.Buffered(1))


def _p4(cols=AW):
    return pl.BlockSpec((4, TM // 4, cols), lambda i: (0, i, 0))


def _p16(cols=AW):
    return pl.BlockSpec((16, TM // 16, cols), lambda i: (0, i, 0))


def _cols(j):
    return slice(128 * j, 128 * (j + 1))


def _follow(body, n_in, after):
    k = len(after)
    return body if k == 0 else (lambda *refs: body(*refs[:n_in], *refs[n_in + k:]))


def _ffn_fwd(x, g, gt, ut, dn, after=()):
    def body(x_ref, g_ref, gt_ref, ut_ref, dn_ref, xo_ref, gate_ref, up_ref):
        x = x_ref[...]
        _, _, hn = _rms(x, g_ref[...])
        h = hn.astype(bf16)
        gate = _dot_nt(h, gt_ref[...])
        up = _dot_nt(h, ut_ref[...])
        gate_ref[...] = gate.astype(bf16)
        up_ref[...] = up.astype(bf16)
        a = (gate * jax.nn.sigmoid(gate) * up).astype(bf16)
        xo_ref[...] = x + 0.5 * _dot_nn(a, dn_ref[...])

    rows = 2 * TM
    return pl.pallas_call(
        _follow(body, 5, after), grid=(S // rows,),
        in_specs=[_tile(D, rows), _layer(1, D), _layer(F, D), _layer(F, D), _layer(F, D)] + [_ANY] * len(after),
        out_specs=[_tile(D, rows), _tile(F, rows), _tile(F, rows)],
        out_shape=[SDS((S, D), f32), SDS((S, F), bf16), SDS((S, F), bf16)],
        compiler_params=_CP, name="ffn_fwd")(x, g, gt, ut, dn, *after)


def _ffn_bwd_d(x, g, gate, up, dxo, gt, ut, dn, after=()):
    def body(x_ref, g_ref, gate_ref, up_ref, dxo_ref, gt_ref, ut_ref, dn_ref,
             dx_ref, dgate_ref, dup_ref, h_ref, dy_ref, dg_ref):
        x = x_ref[...]
        g = g_ref[...]
        r, xh, hn = _rms(x, g)
        h_ref[...] = hn.astype(bf16)
        dxo = dxo_ref[...]
        dy = (0.5 * dxo).astype(bf16)
        dy_ref[...] = dy
        da = _dot_nt(dy, dn_ref[...])
        gate = gate_ref[...].astype(f32)
        up = up_ref[...].astype(f32)
        sg = jax.nn.sigmoid(gate)
        dgate = (da * up * (sg * (1.0 + gate * (1.0 - sg)))).astype(bf16)
        dup = (da * (gate * sg)).astype(bf16)
        dgate_ref[...] = dgate
        dup_ref[...] = dup
        dh = _dot_nn(dgate, gt_ref[...]) + _dot_nn(dup, ut_ref[...])

        @pl.when(pl.program_id(0) == 0)
        def _():
            dg_ref[...] = jnp.zeros_like(dg_ref)

        dg_ref[...] += jnp.sum(dh * xh, axis=0, keepdims=True)
        dx_ref[...] = dxo + _rms_bwd(dh, r, xh, g)

    return pl.pallas_call(
        _follow(body, 8, after), grid=(S // TM,),
        in_specs=[_tile(D), _layer(1, D), _tile(F), _tile(F), _tile(D),
                  _layer(F, D), _layer(F, D), _layer(F, D)] + [_ANY] * len(after),
        out_specs=[_tile(D), _tile(F), _tile(F), _tile(D), _tile(D), _const((1, D))],
        out_shape=[SDS((S, D), f32), SDS((S, F), bf16), SDS((S, F), bf16), SDS((S, D), bf16),
                   SDS((S, D), bf16), SDS((1, D), f32)],
        compiler_params=_CP, name="ffn_bwd_d")(x, g, gate, up, dxo, gt, ut, dn, *after)


def _ffn_bwd_w(h, dy, gate, up, dgate, dup):
    fc = 256

    def body(h_ref, dy_ref, gate_ref, up_ref, dgate_ref, dup_ref, dgt_ref, dut_ref, ddn_ref):
        gate = gate_ref[...].astype(f32)
        a = (gate * jax.nn.sigmoid(gate) * up_ref[...].astype(f32)).astype(bf16)
        ddn_ref[...] = _dot_tn(a, dy_ref[...]).astype(bf16)
        h = h_ref[...]
        dgt_ref[...] = _dot_tn(dgate_ref[...], h).astype(bf16)
        dut_ref[...] = _dot_tn(dup_ref[...], h).astype(bf16)

    col = pl.BlockSpec((S, fc), lambda j: (0, j))
    row = pl.BlockSpec((fc, D), lambda j: (j, 0))
    full = pl.BlockSpec((S, D), lambda j: (0, 0))
    return pl.pallas_call(
        body, grid=(F // fc,),
        in_specs=[full, full, col, col, col, col],
        out_specs=[row, row, row],
        out_shape=[SDS((F, D), bf16)] * 3,
        compiler_params=_CP, name="ffn_bwd_w")(h, dy, gate, up, dgate, dup)


def _wgrad(a, b):
    m, n = a.shape[1], b.shape[1]
    mc = 2 * TM

    def body(a_ref, b_ref, o_ref):
        o_ref[...] = _dot_tn(a_ref[...], b_ref[...]).astype(bf16)

    return pl.pallas_call(
        body, grid=(m // mc,),
        in_specs=[pl.BlockSpec((S, mc), lambda j: (0, j)), pl.BlockSpec((S, n), lambda j: (0, 0))],
        out_specs=pl.BlockSpec((mc, n), lambda j: (j, 0)),
        out_shape=SDS((m, n), bf16),
        compiler_params=_CP, name="wgrad")(a, b)


def _rope(t, c, sn, sp):
    return t * c + pltpu.roll(t, 120, 1) * sn + pltpu.roll(t, 8, 1) * sp


def _rope_bwd(d, c, sn, sp):
    return d * c + pltpu.roll(d * sn, 8, 1) + pltpu.roll(d * sp, 120, 1)


def _rope_tables(positions):
    inv_freq = ROPE_THETA ** (-jnp.arange(0, 16, 2, dtype=f32) / 16)
    ang = positions.reshape(S, 1).astype(f32) * inv_freq
    cos, sin = jnp.cos(ang), jnp.sin(ang)
    one = jnp.ones((S, 48), f32)
    zero8 = jnp.zeros((S, 8), f32)
    zero48 = jnp.zeros((S, 48), f32)
    c = jnp.concatenate([cos, cos, one], axis=1)
    sn = jnp.concatenate([-sin, zero8, zero48], axis=1)
    sp = jnp.concatenate([zero8, sin, zero48], axis=1)
    return tuple(jnp.concatenate([t, t], axis=1) for t in (c, sn, sp))


def _dilation_perm(n, back=False):
    per = TM // n
    i = lax.broadcasted_iota(jnp.int32, (TM, TM), 1 if back else 0)
    j = lax.broadcasted_iota(jnp.int32, (TM, TM), 0 if back else 1)
    return jnp.where(j == n * (i % per) + i // per, 1.0, 0.0).astype(bf16)


def _mix_in_fwd(x, g, wint, tabs):
    def body(x_ref, g_ref, w_ref, c_ref, sn_ref, sp_ref,
             h_ref, vp_ref, q1, k1, v1, q4, k4, v4, q16, k16, v16):
        _, _, hn = _rms(x_ref[...], g_ref[...])
        h = hn.astype(bf16)
        h_ref[...] = h
        proj = _dot_nt(h, w_ref[...])
        vp_ref[...] = proj[:, :PW]
        c, sn, sp = c_ref[...], sn_ref[...], sp_ref[...]
        perm4, perm16 = _dilation_perm(4), _dilation_perm(16)
        for kind, (o1, o4, o16) in enumerate(((q1, q4, q16), (k1, k4, k16), (v1, v4, v16))):
            for j in range(NG):
                t = proj[:, PW + kind * AW + 128 * j: PW + kind * AW + 128 * (j + 1)]
                if kind == 0:
                    t = _rope(t, c, sn, sp) * 0.125
                elif kind == 1:
                    t = _rope(t, c, sn, sp)
                o1[:, _cols(j)] = t.astype(bf16)
            nat = o1[...]
            o4[...] = _dot_nn(perm4, nat).astype(bf16).reshape(4, TM // 4, AW)
            o16[...] = _dot_nn(perm16, nat).astype(bf16).reshape(16, TM // 16, AW)

    nat, d4, d16 = SDS((S, AW), bf16), SDS((4, S // 4, AW), bf16), SDS((16, S // 16, AW), bf16)
    return pl.pallas_call(
        body, grid=(S // TM,),
        in_specs=[_tile(D), _layer(1, D), _layer(PROJ, D), _tile(128), _tile(128), _tile(128)],
        out_specs=[_tile(D), _tile(PW)] + [_tile(AW)] * 3 + [_p4()] * 3 + [_p16()] * 3,
        out_shape=[SDS((S, D), bf16), SDS((S, PW), f32)] + [nat] * 3 + [d4] * 3 + [d16] * 3,
        compiler_params=_CP, name="mix_in_fwd")(x, g, wint, *tabs)


def _mix_in_bwd(dxo, x, g, wint, tabs, dvp, d1, d4, d16, after=()):
    def body(dxo_ref, x_ref, g_ref, w_ref, c_ref, sn_ref, sp_ref, dvp_ref,
             dq1, dk1, dv1, dq4, dk4, dv4, dq16, dk16, dv16,
             dx_ref, dproj_ref, dg_ref):
        c, sn, sp = c_ref[...], sn_ref[...], sp_ref[...]
        dproj_ref[:, :PW] = dvp_ref[...].astype(bf16)
        back4, back16 = _dilation_perm(4, True), _dilation_perm(16, True)
        for kind, (a1, a4, a16) in enumerate(((dq1, dq4, dq16), (dk1, dk4, dk16), (dv1, dv4, dv16))):
            n4 = _dot_nn(back4, a4[...].reshape(TM, AW))
            n16 = _dot_nn(back16, a16[...].reshape(TM, AW))
            for j in range(NG):
                t = a1[:, _cols(j)].astype(f32) + n4[:, _cols(j)] + n16[:, _cols(j)]
                if kind == 0:
                    t = _rope_bwd(t * 0.125, c, sn, sp)
                elif kind == 1:
                    t = _rope_bwd(t, c, sn, sp)
                dproj_ref[:, PW + kind * AW + 128 * j: PW + kind * AW + 128 * (j + 1)] = t.astype(bf16)
        g = g_ref[...]
        r_, xh, _ = _rms(x_ref[...], g)
        dh = _dot_nn(dproj_ref[...], w_ref[...])

        @pl.when(pl.program_id(0) == 0)
        def _():
            dg_ref[...] = jnp.zeros_like(dg_ref)

        dg_ref[...] += jnp.sum(dh * xh, axis=0, keepdims=True)
        dx_ref[...] = dxo_ref[...] + _rms_bwd(dh, r_, xh, g)

    return pl.pallas_call(
        _follow(body, 17, after), grid=(S // TM,),
        in_specs=[_tile(D), _tile(D), _layer(1, D), _layer(PROJ, D), _tile(128), _tile(128), _tile(128),
                  _tile(PW)] + [_tile(AW)] * 3 + [_p4()] * 3 + [_p16()] * 3 + [_ANY] * len(after),
        out_specs=[_tile(D), _tile(PROJ), _const((1, D))],
        out_shape=[SDS((S, D), f32), SDS((S, PROJ), bf16), SDS((1, D), f32)],
        compiler_params=_CP, name="mix_in_bwd")(dxo, x, g, wint, *tabs, dvp, *d1, *d4, *d16, *after)


def _pool_sums(pad_ref, base, rows, adjoint):
    lane_group = lax.broadcasted_iota(jnp.int32, (rows, PW), 1) // 64
    sign = -1 if adjoint else 1

    def sh(o):
        return pad_ref[pl.ds(PAD + base + sign * o, rows), :]

    out = None
    acc = None
    lo, hi = 0, 0
    for gi, w in enumerate(POOL_WINDOWS):
        for o in list(range(-(w // 2), lo)) + list(range(hi, w - w // 2)):
            acc = sh(o) if acc is None else acc + sh(o)
        lo, hi = -(w // 2), w - w // 2
        out = acc if out is None else jnp.where(lane_group >= gi, acc, out)
    return out


def _pool_counts(base, rows):
    pos = base + lax.broadcasted_iota(jnp.int32, (rows, PW), 0)
    lane_group = lax.broadcasted_iota(jnp.int32, (rows, PW), 1) // 64
    cnt = None
    for gi, w in enumerate(POOL_WINDOWS):
        lo = jnp.maximum(pos - w // 2, 0)
        hi = jnp.minimum(pos + w - 1 - w // 2, S - 1)
        c = (hi - lo + 1).astype(f32)
        cnt = c if cnt is None else jnp.where(lane_group >= gi, c, cnt)
    return cnt


def _pool_fwd(vp, wbd, scale):
    ch = 256

    def body(vp_ref, w_ref, sc_ref, y_ref, diff_ref, pad):
        pad[pl.ds(0, PAD), :] = jnp.zeros((PAD, PW), f32)
        pad[pl.ds(PAD + S, PAD), :] = jnp.zeros((PAD, PW), f32)
        pad[pl.ds(PAD, S), :] = vp_ref[...]
        for b in range(S // ch):
            base = b * ch
            pooled = _pool_sums(pad, base, ch, False) / _pool_counts(base, ch)
            diff = (pooled - vp_ref[pl.ds(base, ch), :]).astype(bf16)
            diff_ref[pl.ds(base, ch), :] = diff
            y_ref[pl.ds(base, ch), :] = _dot_nn(diff, w_ref[...]) * sc_ref[...]

    whole = lambda shape: pl.BlockSpec(shape, lambda i: (0,) * len(shape))
    return pl.pallas_call(
        body, grid=(1,),
        in_specs=[whole((S, PW)), whole((PW, PW)), whole((1, PW))],
        out_specs=[whole((S, PW)), whole((S, PW))],
        out_shape=[SDS((S, PW), f32), SDS((S, PW), bf16)],
        scratch_shapes=[pltpu.VMEM((S + 2 * PAD, PW), f32)],
        compiler_params=_CP, name="pool_fwd")(vp, wbd, scale)


def _pool_bwd(dy, diff, wbd, scale, after=()):
    ch = 256

    def body(dy_ref, diff_ref, w_ref, sc_ref, dvp_ref, dw_ref, dsc_ref, pad):
        pad[pl.ds(0, PAD), :] = jnp.zeros((PAD, PW), f32)
        pad[pl.ds(PAD + S, PAD), :] = jnp.zeros((PAD, PW), f32)
        dw = jnp.zeros((PW, PW), f32)
        dsc = jnp.zeros((1, PW), f32)
        for b in range(S // ch):
            base = b * ch
            dy = dy_ref[pl.ds(base, ch), :]
            diff = diff_ref[pl.ds(base, ch), :]
            dsc = dsc + jnp.sum(dy * _dot_nn(diff, w_ref[...]), axis=0, keepdims=True)
            dz = (dy * sc_ref[...]).astype(bf16)
            dw = dw + _dot_tn(diff, dz)
            ddiff = _dot_nt(dz, w_ref[...])
            dvp_ref[pl.ds(base, ch), :] = -ddiff
            pad[pl.ds(PAD + base, ch), :] = ddiff / _pool_counts(base, ch)
        for gi in range(4):
            dw_ref[gi] = dw[64 * gi:64 * (gi + 1), 64 * gi:64 * (gi + 1)]
        dsc_ref[...] = dsc
        for b in range(S // ch):
            base = b * ch
            dvp_ref[pl.ds(base, ch), :] += _pool_sums(pad, base, ch, True)

    whole = lambda shape: pl.BlockSpec(shape, lambda i: (0,) * len(shape))
    return pl.pallas_call(
        _follow(body, 4, after), grid=(1,),
        in_specs=[whole((S, PW)), whole((S, PW)), whole((PW, PW)), whole((1, PW))] + [_ANY] * len(after),
        out_specs=[whole((S, PW)), whole((4, 64, 64)), whole((1, PW))],
        out_shape=[SDS((S, PW), f32), SDS((4, 64, 64), f32), SDS((1, PW), f32)],
        scratch_shapes=[pltpu.VMEM((S + 2 * PAD, PW), f32)],
        compiler_params=_CP, name="pool_bwd")(dy, diff, wbd, scale, *after)


def _attn_blocks(lc):
    bpc = lc // QB
    kw = min(2 * QB, lc)
    blocks = []
    for b in range(S // QB):
        t0 = (b % bpc) * QB
        ks_in = min(max(t0 - HALF, 0), lc - kw)
        blocks.append((b * QB, (b // bpc) * lc + ks_in, t0 - ks_in))
    return kw, blocks


def _attn_bias(bias_ref, kw, shifts):
    r = lax.broadcasted_iota(jnp.int32, (2 * QB, kw), 0) % QB
    c = lax.broadcasted_iota(jnp.int32, (2 * QB, kw), 1)
    for i, shift in enumerate(shifts):
        bias_ref[i] = jnp.where(jnp.abs(r + shift - c) <= HALF, 0.0, MASK_VALUE).astype(f32)


def _head_put(stats, pair, v0, v1, lane):
    return jnp.where(lane == 2 * pair, v0, jnp.where(lane == 2 * pair + 1, v1, stats))


def _head_cols(stats, pair, lane):
    c0 = jnp.sum(jnp.where(lane == 2 * pair, stats, 0.0), axis=-1, keepdims=True)
    c1 = jnp.sum(jnp.where(lane == 2 * pair + 1, stats, 0.0), axis=-1, keepdims=True)
    return jnp.concatenate([c0, c1], axis=0)


def _head_spread(stats, pair, head0):
    return jnp.where(head0, stats[:, 2 * pair:2 * pair + 1], stats[:, 2 * pair + 1:2 * pair + 2])


def _stack_heads(blk, head0):
    zero = jnp.zeros_like(blk)
    return jnp.concatenate([jnp.where(head0, blk, zero), jnp.where(head0, zero, blk)], axis=0)


def _attn_fwd(q, k, v, lc, after=None):
    kw, blocks = _attn_blocks(lc)
    shifts = sorted({b[2] for b in blocks})

    def body(q_ref, k_ref, v_ref, *refs):
        o_ref, lse_ref, bias_ref = refs[-3:]
        lane = lax.broadcasted_iota(jnp.int32, (QB, 128), 1)
        head0 = lane < 64
        pair = pl.program_id(0)
        _attn_bias(bias_ref, kw, shifts)

        @pl.when(pair == 0)
        def _():
            lse_ref[...] = jnp.zeros_like(lse_ref)

        for row0, kstart, shift in blocks:
            q2 = _stack_heads(q_ref[pl.ds(row0, QB), :], head0)
            kb = k_ref[pl.ds(kstart, kw), :]
            vb = v_ref[pl.ds(kstart, kw), :]
            s = _dot_nt(q2, kb) + bias_ref[shifts.index(shift)]
            m = jnp.max(s, axis=-1, keepdims=True)
            p = jnp.exp(s - m)
            den = jnp.sum(p, axis=-1, keepdims=True)
            o2 = _dot_nn(p.astype(bf16), vb) / den
            lse2 = m + jnp.log(den)
            o_ref[pl.ds(row0, QB), :] = jnp.where(head0, o2[:QB], o2[QB:]).astype(bf16)
            lse_ref[pl.ds(row0, QB), :] = _head_put(lse_ref[pl.ds(row0, QB), :], pair, lse2[:QB], lse2[QB:], lane)

    col = pl.BlockSpec((S, 128), lambda p: (0, p))
    extra = () if after is None else (after,)
    return pl.pallas_call(
        body, grid=(NG,), in_specs=[col, col, col] + [_ANY] * len(extra),
        out_specs=[col, pl.BlockSpec((S, 128), lambda p: (0, 0))],
        out_shape=[SDS((S, AW), bf16), SDS((S, 128), f32)],
        scratch_shapes=[pltpu.VMEM((len(shifts), 2 * QB, kw), f32)],
        compiler_params=_CP, name=f"attn_fwd_{lc}")(q, k, v, *extra)


def _attn_bwd(q, k, v, do, lse, delta, lc):
    kw, blocks = _attn_blocks(lc)
    shifts = sorted({b[2] for b in blocks})

    def body(q_ref, k_ref, v_ref, do_ref, lse_ref, dl_ref, dq_ref, dk_out, dv_out, bias_ref, dk_ref, dv_ref):
        lane = lax.broadcasted_iota(jnp.int32, (QB, 128), 1)
        head0 = lane < 64
        pair = pl.program_id(0)
        _attn_bias(bias_ref, kw, shifts)
        dk_ref[...] = jnp.zeros_like(dk_ref)
        dv_ref[...] = jnp.zeros_like(dv_ref)
        for row0, kstart, shift in blocks:
            q2 = _stack_heads(q_ref[pl.ds(row0, QB), :], head0)
            do2 = _stack_heads(do_ref[pl.ds(row0, QB), :], head0)
            lse2 = _head_cols(lse_ref[pl.ds(row0, QB), :], pair, lane)
            dl2 = _head_cols(dl_ref[pl.ds(row0, QB), :], pair, lane)
            kb = k_ref[pl.ds(kstart, kw), :]
            vb = v_ref[pl.ds(kstart, kw), :]
            p = jnp.exp(_dot_nt(q2, kb) + bias_ref[shifts.index(shift)] - lse2)
            ds = (p * (_dot_nt(do2, vb) - dl2)).astype(bf16)
            dq2 = _dot_nn(ds, kb)
            dq_ref[pl.ds(row0, QB), :] = jnp.where(head0, dq2[:QB], dq2[QB:]).astype(bf16)
            dk_ref[pl.ds(kstart, kw), :] += _dot_tn(ds, q2)
            dv_ref[pl.ds(kstart, kw), :] += _dot_tn(p.astype(bf16), do2)
        dk_out[...] = dk_ref[...].astype(bf16)
        dv_out[...] = dv_ref[...].astype(bf16)

    col = pl.BlockSpec((S, 128), lambda p: (0, p))
    stats = pl.BlockSpec((S, 128), lambda p: (0, 0))
    return pl.pallas_call(
        body, grid=(NG,), in_specs=[col] * 4 + [stats] * 2, out_specs=[col] * 3,
        out_shape=[SDS((S, AW), bf16)] * 3,
        scratch_shapes=[pltpu.VMEM((len(shifts), 2 * QB, kw), f32), pltpu.VMEM((S, 128), f32),
                        pltpu.VMEM((S, 128), f32)],
        compiler_params=_CP, name=f"attn_bwd_{lc}")(q, k, v, do, lse, delta)


def _mix_out_fwd(x, ypool, o1, l1, o4, l4, o16, l16, wout):
    def body(x_ref, yp_ref, o1_ref, l1_ref, o4_ref, l4_ref, o16_ref, l16_ref, w_ref,
             xo_ref, mixed_ref, o_ref, lse1_ref, lse4_ref, lse16_ref, sl4, sl16, sl):
        head0 = lax.broadcasted_iota(jnp.int32, (TM, 128), 1) < 64
        for r in range(4):
            sl4[pl.ds(r, TM // 4, stride=4), :] = l4_ref[r]
        for r in range(16):
            sl16[pl.ds(r, TM // 16, stride=16), :] = l16_ref[r]
        n4 = _dot_nn(_dilation_perm(4, True), o4_ref[...].reshape(TM, AW))
        n16 = _dot_nn(_dilation_perm(16, True), o16_ref[...].reshape(TM, AW))
        a, b, c = l1_ref[...], sl4[...], sl16[...]
        m = jnp.maximum(jnp.maximum(a, b), c)
        wa, wb, wc = jnp.exp(a - m), jnp.exp(b - m), jnp.exp(c - m)
        den = wa + wb + wc
        wa, wb, wc = wa / den, wb / den, wc / den
        lse = m + jnp.log(den)
        lse1_ref[...] = lse
        sl[...] = lse
        mixed_ref[:, :PW] = yp_ref[...].astype(bf16)
        for j in range(NG):
            y = (_head_spread(wa, j, head0) * o1_ref[:, _cols(j)].astype(f32)
                 + _head_spread(wb, j, head0) * n4[:, _cols(j)] + _head_spread(wc, j, head0) * n16[:, _cols(j)])
            o_ref[:, _cols(j)] = y
            mixed_ref[:, PW + 128 * j: PW + 128 * (j + 1)] = y.astype(bf16)
        for r in range(4):
            lse4_ref[r] = sl[pl.ds(r, TM // 4, stride=4), :]
        for r in range(16):
            lse16_ref[r] = sl[pl.ds(r, TM // 16, stride=16), :]
        xo_ref[...] = x_ref[...] + _dot_nn(mixed_ref[...], w_ref[...])

    return pl.pallas_call(
        body, grid=(S // TM,),
        in_specs=[_tile(D), _tile(PW), _tile(AW), _tile(128), _p4(), _p4(128), _p16(), _p16(128), _layer(D, D)],
        out_specs=[_tile(D), _tile(D), _tile(AW), _tile(128), _p4(128), _p16(128)],
        out_shape=[SDS((S, D), f32), SDS((S, D), bf16), SDS((S, AW), f32), SDS((S, 128), f32),
                   SDS((4, S // 4, 128), f32), SDS((16, S // 16, 128), f32)],
        scratch_shapes=[pltpu.VMEM((TM, 128), f32)] * 3,
        compiler_params=_CP, name="mix_out_fwd")(x, ypool, o1, l1, o4, l4, o16, l16, wout)


def _mix_out_bwd(dxo, o, wout):
    def body(dxo_ref, o_ref, w_ref, dxb_ref, dyp_ref, do1, do4, do16, dl1, dl4, dl16, sdl):
        dxb = dxo_ref[...].astype(bf16)
        dxb_ref[...] = dxb
        dm = _dot_nt(dxb, w_ref[...])
        dyp_ref[...] = dm[:, :PW]
        lane = lax.broadcasted_iota(jnp.int32, (TM, 128), 1)
        head0 = lane < 64
        dl = jnp.zeros((TM, 128), f32)
        for j in range(NG):
            d = dm[:, PW + 128 * j: PW + 128 * (j + 1)]
            prod = d * o_ref[:, _cols(j)]
            dl = _head_put(dl, j, jnp.sum(jnp.where(head0, prod, 0.0), axis=-1, keepdims=True),
                           jnp.sum(jnp.where(head0, 0.0, prod), axis=-1, keepdims=True), lane)
            do1[:, _cols(j)] = d.astype(bf16)
        dl1[...] = dl
        sdl[...] = dl
        for r in range(4):
            dl4[r] = sdl[pl.ds(r, TM // 4, stride=4), :]
        for r in range(16):
            dl16[r] = sdl[pl.ds(r, TM // 16, stride=16), :]
        nat = do1[...]
        do4[...] = _dot_nn(_dilation_perm(4), nat).astype(bf16).reshape(4, TM // 4, AW)
        do16[...] = _dot_nn(_dilation_perm(16), nat).astype(bf16).reshape(16, TM // 16, AW)

    return pl.pallas_call(
        body, grid=(S // TM,),
        in_specs=[_tile(D), _tile(AW), _layer(D, D)],
        out_specs=[_tile(D), _tile(PW), _tile(AW), _p4(), _p16(), _tile(128), _p4(128), _p16(128)],
        out_shape=[SDS((S, D), bf16), SDS((S, PW), f32),
                   SDS((S, AW), bf16), SDS((4, S // 4, AW), bf16), SDS((16, S // 16, AW), bf16),
                   SDS((S, 128), f32), SDS((4, S // 4, 128), f32), SDS((16, S // 16, 128), f32)],
        scratch_shapes=[pltpu.VMEM((TM, 128), f32)],
        compiler_params=_CP, name="mix_out_bwd")(dxo, o, wout)


def _loss_head(x, g, target):
    def body(x_ref, g_ref, t_ref, dx_ref, loss_ref, dg_ref):
        g = g_ref[...]
        r, xh, y = _rms(x_ref[...], g)
        err = y - t_ref[...]
        dy = err * (1.0 / D)

        @pl.when(pl.program_id(0) == 0)
        def _():
            loss_ref[...] = jnp.zeros_like(loss_ref)
            dg_ref[...] = jnp.zeros_like(dg_ref)

        loss_ref[...] += jnp.broadcast_to(0.5 * jnp.sum(jnp.mean(err * err, axis=-1, keepdims=True)), (1, D))
        dg_ref[...] += jnp.sum(dy * xh, axis=0, keepdims=True)
        dx_ref[...] = _rms_bwd(dy, r, xh, g)

    return pl.pallas_call(
        body, grid=(S // TM,),
        in_specs=[_tile(D), _const((1, D)), _tile(D)],
        out_specs=[_tile(D), _const((1, D)), _const((1, D))],
        out_shape=[SDS((S, D), f32), SDS((1, D), f32), SDS((1, D), f32)],
        compiler_params=_CP, name="loss_head")(x, g, target)


def _peer(k):
    x, y, c = lax.axis_index("x"), lax.axis_index("y"), lax.axis_index("c")
    px = 1 - x if k & 4 else x
    py = 1 - y if k & 2 else y
    pc = 1 - c if k & 1 else c
    return (px, py, pc), 4 * px + 2 * py + pc


def _diag_route():
    x, y, c = lax.axis_index("x"), lax.axis_index("y"), lax.axis_index("c")
    idx_x, idx_y = _peer(4)[1], _peer(2)[1]
    return idx_x + c * (idx_y - idx_x), (x + c * (1 - 2 * x), (1 - y) + c * (2 * y - 1), c)


def _all_gather(lands):
    n = len(lands)

    def body(*refs):
        zones, send_sems, recv_sems = refs[n:2 * n], refs[2 * n], refs[2 * n + 1]
        me, me_idx = _peer(0)
        sibling, sib_idx = _peer(1)
        (x_nbr, idx_x), (y_nbr, idx_y), idx_d = _peer(4), _peer(2), _peer(6)[1]
        fwd_idx, fwd_dev = _diag_route()

        def copy(k, t, idx, to):
            return _row_copy(zones[t], idx, send_sems.at[k, t], recv_sems.at[k, t], to)

        sent = []

        def send(k, t, idx, to):
            cp = copy(k, t, idx, to)
            cp.start()
            sent.append(cp)

        for t in range(n):
            send(0, t, me_idx, sibling)
            send(1, t, me_idx, x_nbr)
            send(2, t, me_idx, y_nbr)
        for t in range(n):
            copy(1, t, idx_x, me).wait_recv()
            send(3, t, idx_x, sibling)
        for t in range(n):
            copy(2, t, idx_y, me).wait_recv()
            send(4, t, idx_y, sibling)
        for t in range(n):
            send(5, t, fwd_idx, fwd_dev)
        for t in range(n):
            copy(5, t, idx_d, me).wait_recv()
            send(6, t, idx_d, sibling)
        for k, mask in ((0, 1), (3, 5), (4, 3), (6, 7)):
            for t in range(n):
                copy(k, t, _peer(mask)[1], me).wait_recv()
        for cp in sent:
            cp.wait_send()

    return pl.pallas_call(
        body, in_specs=[_ANY] * n, out_specs=[_ANY] * n,
        out_shape=[SDS(a.shape, a.dtype) for a in lands], input_output_aliases={t: t for t in range(n)},
        scratch_shapes=[pltpu.SemaphoreType.DMA((7, n)), pltpu.SemaphoreType.DMA((7, n))],
        name="all_gather_weights")(*lands)


def _hbm(a):
    return pltpu.with_memory_space_constraint(a, pltpu.HBM)


def _rows(ref, idx):
    r = ref.shape[0] // NDEV
    return ref.at[pl.ds(idx * r, r), :]


def _row_copy(ref, idx, send_sem, recv_sem, to):
    return pltpu.make_async_remote_copy(src_ref=_rows(ref, idx), dst_ref=_rows(ref, idx), send_sem=send_sem,
                                        recv_sem=recv_sem, device_id=to, device_id_type=_MESH)


def _place_own(me, shards, l):
    n = len(shards)

    def body(me_ref, *refs):
        for t in range(n):
            refs[n + t][...] = refs[t][...].astype(bf16)

    grid_spec = pltpu.PrefetchScalarGridSpec(
        num_scalar_prefetch=1, grid=(1,),
        in_specs=[pl.BlockSpec((None, s.shape[1], D), lambda i, me_ref: (l, 0, 0)) for s in shards],
        out_specs=[pl.BlockSpec((s.shape[1], D), lambda i, me_ref: (me_ref[0], 0)) for s in shards])
    return pl.pallas_call(
        body, grid_spec=grid_spec, out_shape=[SDS((NDEV * s.shape[1], D), bf16) for s in shards],
        compiler_params=_CP, name="place_own")(me, *shards)


_TOKEN = SDS((8, 128), f32)
def _ag_start(lands, after, l):
    n = len(lands)
    after = list(after) if isinstance(after, (list, tuple)) else [after]

    def body(*refs):
        zones, send_sems, recv_sems, token = refs[:n], refs[n + len(after)], refs[n + len(after) + 1], refs[-1]
        _, me_idx = _peer(0)
        for k, mask in enumerate((1, 4, 2)):
            for t in range(n):
                _row_copy(zones[t], me_idx, send_sems.at[k * n + t], recv_sems.at[k * n + t], _peer(mask)[0]).start()
        token[...] = jnp.zeros_like(token)

    outs = pl.pallas_call(
        body, name=f"ag_start_{l}", in_specs=[_HBM] * n + [_ANY] * len(after),
        out_specs=(_SEM, _SEM, *[_HBM] * n, pl.BlockSpec(memory_space=pltpu.VMEM)),
        out_shape=(pltpu.SemaphoreType.DMA((3 * n,)), pltpu.SemaphoreType.DMA((3 * n,)),
                   *[pltpu.HBM(a.shape, a.dtype) for a in lands], _TOKEN),
        input_output_aliases={t: 2 + t for t in range(n)}, compiler_params=_CP_SPLIT)(
            *[_hbm(a) for a in lands], *after)
    return outs[0], outs[1], list(outs[2:2 + n]), outs[-1]


def _ag_pass(lands, recv_sems, after, l):
    n = len(lands)
    after = list(after) if isinstance(after, (list, tuple)) else [after]

    def body(*refs):
        zones, recv_sems = refs[:n], refs[n]
        psend, precv, token = refs[n + 1 + len(after)], refs[n + 2 + len(after)], refs[-1]
        me, _ = _peer(0)
        sibling, _ = _peer(1)
        for j, mask in enumerate((4, 2)):
            idx = _peer(mask)[1]
            for t in range(n):
                _row_copy(zones[t], idx, psend.at[j * n + t], recv_sems.at[(1 + j) * n + t], me).wait_recv()
                _row_copy(zones[t], idx, psend.at[j * n + t], precv.at[j * n + t], sibling).start()
        fwd_idx, fwd_dev = _diag_route()
        for t in range(n):
            _row_copy(zones[t], fwd_idx, psend.at[2 * n + t], precv.at[2 * n + t], fwd_dev).start()
        token[...] = jnp.zeros_like(token)

    outs = pl.pallas_call(
        body, name=f"ag_pass_{l}", in_specs=[_HBM] * n + [_SEM] + [_ANY] * len(after),
        out_specs=(_SEM, _SEM, *[_HBM] * n, pl.BlockSpec(memory_space=pltpu.VMEM)),
        out_shape=(pltpu.SemaphoreType.DMA((3 * n,)), pltpu.SemaphoreType.DMA((3 * n,)),
                   *[pltpu.HBM(a.shape, a.dtype) for a in lands], _TOKEN),
        input_output_aliases={t: 2 + t for t in range(n)}, compiler_params=_CP_SPLIT)(*lands, recv_sems, *after)
    return outs[0], outs[1], list(outs[2:2 + n]), outs[-1]


def _ag_last(lands, precv, after, l):
    n = len(lands)
    after = list(after) if isinstance(after, (list, tuple)) else [after]

    def body(*refs):
        zones, precv = refs[:n], refs[n]
        qsend, qrecv, token = refs[n + 1 + len(after)], refs[n + 2 + len(after)], refs[-1]
        me, _ = _peer(0)
        sibling, _ = _peer(1)
        idx = _peer(6)[1]
        for t in range(n):
            _row_copy(zones[t], idx, qsend.at[t], precv.at[2 * n + t], me).wait_recv()
            _row_copy(zones[t], idx, qsend.at[t], qrecv.at[t], sibling).start()
        token[...] = jnp.zeros_like(token)

    outs = pl.pallas_call(
        body, name=f"ag_last_{l}", in_specs=[_HBM] * n + [_SEM] + [_ANY] * len(after),
        out_specs=(_SEM, _SEM, *[_HBM] * n, pl.BlockSpec(memory_space=pltpu.VMEM)),
        out_shape=(pltpu.SemaphoreType.DMA((n,)), pltpu.SemaphoreType.DMA((n,)),
                   *[pltpu.HBM(a.shape, a.dtype) for a in lands], _TOKEN),
        input_output_aliases={t: 2 + t for t in range(n)}, compiler_params=_CP_SPLIT)(*lands, precv, *after)
    return outs[0], outs[1], list(outs[2:2 + n]), outs[-1]


def _ag_wait(lands, send_sems, recv_sems, psend, precv, qsend, qrecv, after, l):
    n = len(lands)
    after = list(after) if isinstance(after, (list, tuple)) else [after]

    def body(*refs):
        zones = refs[:n]
        send_sems, recv_sems, psend, precv, qsend, qrecv = refs[n:n + 6]
        me, me_idx = _peer(0)
        for k in range(3):
            for t in range(n):
                _row_copy(zones[t], me_idx, send_sems.at[k * n + t], recv_sems.at[k * n + t], me).wait_send()
        for t in range(n):
            _row_copy(zones[t], _peer(1)[1], send_sems.at[t], recv_sems.at[t], me).wait_recv()
        fwd_idx, _ = _diag_route()
        for j, (mine, theirs) in enumerate(((_peer(4)[1], _peer(5)[1]), (_peer(2)[1], _peer(3)[1]))):
            for t in range(n):
                _row_copy(zones[t], mine, psend.at[j * n + t], precv.at[j * n + t], me).wait_send()
                _row_copy(zones[t], theirs, psend.at[j * n + t], precv.at[j * n + t], me).wait_recv()
        for t in range(n):
            _row_copy(zones[t], fwd_idx, psend.at[2 * n + t], precv.at[2 * n + t], me).wait_send()
            _row_copy(zones[t], _peer(6)[1], qsend.at[t], qrecv.at[t], me).wait_send()
            _row_copy(zones[t], _peer(7)[1], qsend.at[t], qrecv.at[t], me).wait_recv()

    outs = pl.pallas_call(
        body, name=f"ag_wait_{l}", in_specs=[_HBM] * n + [_SEM] * 6 + [_ANY] * len(after),
        out_specs=tuple([_HBM] * n), out_shape=tuple(pltpu.HBM(a.shape, a.dtype) for a in lands),
        input_output_aliases={t: t for t in range(n)}, compiler_params=_CP_SPLIT)(
            *lands, send_sems, recv_sems, psend, precv, qsend, qrecv, *after)
    return list(outs)


def _xchg_src(ref, slot_ref, idx):
    return _rows(ref, idx) if ref.shape[0] == NDEV * slot_ref.shape[1] else ref


def _rs_start(srcs, slots, after, tag):
    n = len(srcs)
    after = list(after) if isinstance(after, (list, tuple)) else [after]

    def body(*refs):
        src, slot = refs[:n], refs[n:2 * n]
        send_sems, recv_sems, token = refs[2 * n + len(after)], refs[2 * n + len(after) + 1], refs[-1]
        _, me_idx = _peer(0)
        for k in range(1, NDEV):
            dev, idx = _peer(k)
            for t in range(n):
                pltpu.make_async_remote_copy(
                    src_ref=_xchg_src(src[t], slot[t], idx), dst_ref=slot[t].at[me_idx],
                    send_sem=send_sems.at[(k - 1) * n + t], recv_sem=recv_sems.at[(k - 1) * n + t],
                    device_id=dev, device_id_type=_MESH).start()
        token[...] = jnp.zeros_like(token)

    outs = pl.pallas_call(
        body, name=f"rs_start_{tag}", in_specs=[_HBM] * (2 * n) + [_ANY] * len(after),
        out_specs=(_SEM, _SEM, *[_HBM] * (2 * n), pl.BlockSpec(memory_space=pltpu.VMEM)),
        out_shape=(pltpu.SemaphoreType.DMA(((NDEV - 1) * n,)), pltpu.SemaphoreType.DMA(((NDEV - 1) * n,)),
                   *[pltpu.HBM(a.shape, a.dtype) for a in list(srcs) + list(slots)], _TOKEN),
        input_output_aliases={t: 2 + t for t in range(2 * n)}, compiler_params=_CP_SPLIT)(
            *[_hbm(a) for a in list(srcs) + list(slots)], *after)
    return outs[0], outs[1], list(outs[2:2 + n]), list(outs[2 + n:2 + 2 * n]), outs[-1]


def _rs_wait(srcs, slots, send_sems, recv_sems, after, tag):
    n = len(srcs)
    after = list(after) if isinstance(after, (list, tuple)) else [after]

    def body(*refs):
        src, slot, send_sems, recv_sems = refs[:n], refs[n:2 * n], refs[2 * n], refs[2 * n + 1]
        me, _ = _peer(0)
        for k in range(1, NDEV):
            idx = _peer(k)[1]
            for t in range(n):
                cp = pltpu.make_async_remote_copy(
                    src_ref=_xchg_src(src[t], slot[t], idx), dst_ref=slot[t].at[idx],
                    send_sem=send_sems.at[(k - 1) * n + t], recv_sem=recv_sems.at[(k - 1) * n + t],
                    device_id=me, device_id_type=_MESH)
                cp.wait_send()
                cp.wait_recv()

    outs = pl.pallas_call(
        body, name=f"rs_wait_{tag}", in_specs=[_HBM] * (2 * n) + [_SEM, _SEM] + [_ANY] * len(after),
        out_specs=tuple([_HBM] * (2 * n)),
        out_shape=tuple(pltpu.HBM(a.shape, a.dtype) for a in list(srcs) + list(slots)),
        input_output_aliases={t: t for t in range(2 * n)}, compiler_params=_CP_SPLIT)(
            *srcs, *slots, send_sems, recv_sems, *after)
    return list(outs[:n]), list(outs[n:])


def _pair_start(full4s, bufs, after, tag):
    n = len(full4s)
    after = list(after) if isinstance(after, (list, tuple)) else [after]

    def body(*refs):
        full, buf = refs[:n], refs[n:2 * n]
        send_sems, recv_sems, token = refs[2 * n + len(after)], refs[2 * n + len(after) + 1], refs[-1]
        c = lax.axis_index("c")
        for t in range(n):
            pltpu.make_async_remote_copy(src_ref=full[t].at[:, 1 - c], dst_ref=buf[t], send_sem=send_sems.at[t],
                                         recv_sem=recv_sems.at[t], device_id=_peer(1)[0], device_id_type=_MESH).start()
        token[...] = jnp.zeros_like(token)

    outs = pl.pallas_call(
        body, name=f"pair_start_{tag}", in_specs=[_HBM] * (2 * n) + [_ANY] * len(after),
        out_specs=(_SEM, _SEM, *[_HBM] * (2 * n), pl.BlockSpec(memory_space=pltpu.VMEM)),
        out_shape=(pltpu.SemaphoreType.DMA((n,)), pltpu.SemaphoreType.DMA((n,)),
                   *[pltpu.HBM(a.shape, a.dtype) for a in list(full4s) + list(bufs)], _TOKEN),
        input_output_aliases={t: 2 + t for t in range(2 * n)}, compiler_params=_CP_SPLIT)(
            *[_hbm(a) for a in list(full4s) + list(bufs)], *after)
    return outs[0], outs[1], list(outs[2:2 + n]), list(outs[2 + n:2 + 2 * n]), outs[-1]


def _pair_wait(full4s, bufs, send_sems, recv_sems, after, tag):
    n = len(full4s)
    after = list(after) if isinstance(after, (list, tuple)) else [after]

    def body(*refs):
        full, buf, send_sems, recv_sems = refs[:n], refs[n:2 * n], refs[2 * n], refs[2 * n + 1]
        c = lax.axis_index("c")
        for t in range(n):
            cp = pltpu.make_async_remote_copy(src_ref=full[t].at[:, 1 - c], dst_ref=buf[t], send_sem=send_sems.at[t],
                                              recv_sem=recv_sems.at[t], device_id=_peer(0)[0], device_id_type=_MESH)
            cp.wait_send()
            cp.wait_recv()

    outs = pl.pallas_call(
        body, name=f"pair_wait_{tag}", in_specs=[_HBM] * (2 * n) + [_SEM, _SEM] + [_ANY] * len(after),
        out_specs=tuple([_HBM] * (2 * n)),
        out_shape=tuple(pltpu.HBM(a.shape, a.dtype) for a in list(full4s) + list(bufs)),
        input_output_aliases={t: t for t in range(2 * n)}, compiler_params=_CP_SPLIT)(
            *full4s, *bufs, send_sems, recv_sems, *after)
    return list(outs[:n]), list(outs[n:])


def _pair_sum(core, full4s, bufs):
    n = len(full4s)

    def body(core_ref, *refs):
        for t in range(n):
            refs[2 * n + t][...] = (refs[t][...].astype(f32) + refs[n + t][...].astype(f32)).astype(bf16)

    grid_spec = pltpu.PrefetchScalarGridSpec(
        num_scalar_prefetch=1, grid=(4,),
        in_specs=[pl.BlockSpec((None, None) + a.shape[2:], lambda j, core_ref: (j, core_ref[0], 0, 0)) for a in full4s]
        + [pl.BlockSpec((None,) + b.shape[1:], lambda j, core_ref: (j, 0, 0)) for b in bufs],
        out_specs=[pl.BlockSpec((None,) + b.shape[1:], lambda j, core_ref: (j, 0, 0)) for b in bufs])
    return pl.pallas_call(
        body, grid_spec=grid_spec, out_shape=[SDS(b.shape, bf16) for b in bufs],
        compiler_params=_CP, name="pair_sum")(core, *full4s, *bufs)


def _chip_start(sums, slots, after, tag):
    n = len(sums)
    after = list(after) if isinstance(after, (list, tuple)) else [after]

    def body(*refs):
        src, slot = refs[:n], refs[n:2 * n]
        send_sems, recv_sems, token = refs[2 * n + len(after)], refs[2 * n + len(after) + 1], refs[-1]
        my_chip = 2 * lax.axis_index("x") + lax.axis_index("y")
        for k, mask in enumerate((4, 2, 6)):
            dev, _ = _peer(mask)
            for t in range(n):
                pltpu.make_async_remote_copy(
                    src_ref=src[t].at[2 * dev[0] + dev[1]], dst_ref=slot[t].at[my_chip],
                    send_sem=send_sems.at[k * n + t], recv_sem=recv_sems.at[k * n + t],
                    device_id=dev, device_id_type=_MESH).start()
        token[...] = jnp.zeros_like(token)

    outs = pl.pallas_call(
        body, name=f"chip_start_{tag}", in_specs=[_HBM] * (2 * n) + [_ANY] * len(after),
        out_specs=(_SEM, _SEM, *[_HBM] * (2 * n), pl.BlockSpec(memory_space=pltpu.VMEM)),
        out_shape=(pltpu.SemaphoreType.DMA((3 * n,)), pltpu.SemaphoreType.DMA((3 * n,)),
                   *[pltpu.HBM(a.shape, a.dtype) for a in list(sums) + list(slots)], _TOKEN),
        input_output_aliases={t: 2 + t for t in range(2 * n)}, compiler_params=_CP_SPLIT)(
            *[_hbm(a) for a in list(sums) + list(slots)], *after)
    return outs[0], outs[1], list(outs[2:2 + n]), list(outs[2 + n:2 + 2 * n]), outs[-1]


def _chip_wait(sums, slots, send_sems, recv_sems, after, tag):
    n = len(sums)
    after = list(after) if isinstance(after, (list, tuple)) else [after]

    def body(*refs):
        src, slot, send_sems, recv_sems = refs[:n], refs[n:2 * n], refs[2 * n], refs[2 * n + 1]
        for k, mask in enumerate((4, 2, 6)):
            dev, _ = _peer(mask)
            chip = 2 * dev[0] + dev[1]
            for t in range(n):
                cp = pltpu.make_async_remote_copy(
                    src_ref=src[t].at[chip], dst_ref=slot[t].at[chip],
                    send_sem=send_sems.at[k * n + t], recv_sem=recv_sems.at[k * n + t],
                    device_id=_peer(0)[0], device_id_type=_MESH)
                cp.wait_send()
                cp.wait_recv()

    outs = pl.pallas_call(
        body, name=f"chip_wait_{tag}", in_specs=[_HBM] * (2 * n) + [_SEM, _SEM] + [_ANY] * len(after),
        out_specs=tuple([_HBM] * (2 * n)),
        out_shape=tuple(pltpu.HBM(a.shape, a.dtype) for a in list(sums) + list(slots)),
        input_output_aliases={t: t for t in range(2 * n)}, compiler_params=_CP_SPLIT)(
            *sums, *slots, send_sems, recv_sems, *after)
    return list(outs[:n]), list(outs[n:])


def _sum_slots(slots, rb):
    r = slots.shape[1]

    def body(s_ref, o_ref):
        acc = s_ref[0].astype(f32)
        for s in range(1, NDEV):
            acc = acc + s_ref[s].astype(f32)
        o_ref[...] = acc

    return pl.pallas_call(
        body, grid=(r // rb,),
        in_specs=[pl.BlockSpec((NDEV, rb, D), lambda i: (0, i, 0))],
        out_specs=pl.BlockSpec((rb, D), lambda i: (i, 0)),
        out_shape=SDS((r, D), f32), compiler_params=_CP, name="sum_slots")(slots)


def _adamw(w, g, m, v):
    shape = w.shape
    cols = shape[-1]
    rows = w.size // cols
    rb = rows
    for cand in (512, 256, 128, 64, 32, 16, 8):
        if rows % cand == 0 and rows > cand:
            rb = cand
            break

    def body(w_ref, g_ref, m_ref, v_ref, d_ref, mo_ref, vo_ref):
        d_ref[...], mo_ref[...], vo_ref[...] = _adamw_math(w_ref[...], g_ref[...], m_ref[...], v_ref[...])

    spec = pl.BlockSpec((rb, cols), lambda i: (i, 0))
    outs = pl.pallas_call(
        body, grid=(rows // rb,), in_specs=[spec] * 4, out_specs=[spec] * 3,
        out_shape=[SDS((rows, cols), f32)] * 3, compiler_params=_CP, name="adamw")(
            *(a.reshape(rows, cols) for a in (w, g, m, v)))
    return tuple(o.reshape(shape) for o in outs)


def _adamw_math(w, g, m, v):
    m = ADAM_B1 * m + (1.0 - ADAM_B1) * g
    v = ADAM_B2 * v + (1.0 - ADAM_B2) * (g * g)
    m_hat = m / (1.0 - ADAM_B1 ** ADAM_STEP)
    v_hat = v / (1.0 - ADAM_B2 ** ADAM_STEP)
    return -ADAM_LR * (m_hat / (jnp.sqrt(v_hat) + ADAM_EPS) + ADAM_WD * w), m, v


def _reduce_adamw(acc, me, full, slots, w, m, v, l):
    _, r, _ = w.shape
    ns = slots.shape[0]
    rb = r // 2 if r > 128 else r

    def body(me_ref, full_ref, slots_ref, w_ref, m_ref, v_ref, *refs):
        go_ref, d_ref, mo_ref, vo_ref = refs[-4:]
        own = full_ref[...].astype(f32)
        g = None
        for s in range(ns):
            part = jnp.where(me_ref[0] == s, own, slots_ref[s].astype(f32))
            g = part if g is None else g + part
        go_ref[...] = g
        d_ref[...], mo_ref[...], vo_ref[...] = _adamw_math(w_ref[...], g, m_ref[...], v_ref[...])

    steps = r // rb
    lay = pl.BlockSpec((None, rb, D), lambda i, me_ref: (l, i, 0))
    n_acc = 0 if acc is None else 4
    grid_spec = pltpu.PrefetchScalarGridSpec(
        num_scalar_prefetch=1, grid=(steps,),
        in_specs=[pl.BlockSpec((rb, D), lambda i, me_ref: (me_ref[0] * steps + i, 0)),
                  pl.BlockSpec((ns, rb, D), lambda i, me_ref: (0, i, 0)), lay, lay, lay] + [_ANY] * n_acc,
        out_specs=[lay] * 4)
    outs = pl.pallas_call(
        body, grid_spec=grid_spec, out_shape=[SDS(w.shape, f32)] * 4,
        input_output_aliases={6 + j: j for j in range(n_acc)},
        compiler_params=_CP, name="reduce_adamw")(me, full, slots, w, m, v, *(() if acc is None else acc))
    return tuple(outs)


_BIG = ("ffn1_w_gate", "ffn1_w_up", "ffn1_w_down", "w_in", "w_out", "ffn2_w_gate", "ffn2_w_up", "ffn2_w_down")
_TRANSPOSED = ("ffn1_w_gate", "ffn1_w_up", "w_in", "ffn2_w_gate", "ffn2_w_up")

def _block_diag(pool_w):
    out = jnp.zeros((L, PW, PW), pool_w.dtype)
    for gi in range(4):
        out = out.at[:, 64 * gi:64 * (gi + 1), 64 * gi:64 * (gi + 1)].set(pool_w[:, gi])
    return out


def kernel(x, positions, ffn1_norm, ffn1_w_gate, ffn1_w_up, ffn1_w_down, mix_norm, w_in, pool_w, pool_scale, w_out, ffn2_norm, ffn2_w_gate, ffn2_w_up, ffn2_w_down, final_norm, loss_target, m_ffn1_norm, m_ffn1_w_gate, m_ffn1_w_up, m_ffn1_w_down, m_mix_norm, m_w_in, m_pool_w, m_pool_scale, m_w_out, m_ffn2_norm, m_ffn2_w_gate, m_ffn2_w_up, m_ffn2_w_down, m_final_norm, v_ffn1_norm, v_ffn1_w_gate, v_ffn1_w_up, v_ffn1_w_down, v_mix_norm, v_w_in, v_pool_w, v_pool_scale, v_w_out, v_ffn2_norm, v_ffn2_w_gate, v_ffn2_w_up, v_ffn2_w_down, v_final_norm):
    weights = dict(ffn1_norm=ffn1_norm, ffn1_w_gate=ffn1_w_gate, ffn1_w_up=ffn1_w_up, ffn1_w_down=ffn1_w_down,
                   mix_norm=mix_norm, w_in=w_in, pool_w=pool_w, pool_scale=pool_scale, w_out=w_out,
                   ffn2_norm=ffn2_norm, ffn2_w_gate=ffn2_w_gate, ffn2_w_up=ffn2_w_up, ffn2_w_down=ffn2_w_down,
                   final_norm=final_norm)
    moms = dict(ffn1_norm=m_ffn1_norm, ffn1_w_gate=m_ffn1_w_gate, ffn1_w_up=m_ffn1_w_up, ffn1_w_down=m_ffn1_w_down,
                mix_norm=m_mix_norm, w_in=m_w_in, pool_w=m_pool_w, pool_scale=m_pool_scale, w_out=m_w_out,
                ffn2_norm=m_ffn2_norm, ffn2_w_gate=m_ffn2_w_gate, ffn2_w_up=m_ffn2_w_up, ffn2_w_down=m_ffn2_w_down,
                final_norm=m_final_norm)
    vels = dict(ffn1_norm=v_ffn1_norm, ffn1_w_gate=v_ffn1_w_gate, ffn1_w_up=v_ffn1_w_up, ffn1_w_down=v_ffn1_w_down,
                mix_norm=v_mix_norm, w_in=v_w_in, pool_w=v_pool_w, pool_scale=v_pool_scale, w_out=v_w_out,
                ffn2_norm=v_ffn2_norm, ffn2_w_gate=v_ffn2_w_gate, ffn2_w_up=v_ffn2_w_up, ffn2_w_down=v_ffn2_w_down,
                final_norm=v_final_norm)
    names = list(weights)

    me_idx = 4 * lax.axis_index("x") + 2 * lax.axis_index("y") + lax.axis_index("c")
    me_arr = me_idx.reshape(1).astype(jnp.int32)

    as_rows = lambda a, nm: jnp.swapaxes(a, 1, 2) if nm in _TRANSPOSED else a
    w_rows = {nm: as_rows(weights[nm], nm) for nm in _BIG}
    m_rows = {nm: as_rows(moms[nm], nm) for nm in _BIG}
    v_rows = {nm: as_rows(vels[nm], nm) for nm in _BIG}

    def landing_zones(l, which):
        return _place_own(me_arr, [w_rows[_BIG[t]] for t in which], l)

    g_ffn1 = [ffn1_norm[l].reshape(1, D) for l in range(L)]
    g_mix = [mix_norm[l].reshape(1, D) for l in range(L)]
    g_ffn2 = [ffn2_norm[l].reshape(1, D) for l in range(L)]
    wbd_all = _block_diag(pool_w).astype(bf16)
    wbd = [wbd_all[l] for l in range(L)]
    pscale = [pool_scale[l].reshape(1, PW) for l in range(L)]
    tabs = _rope_tables(positions)
    flat = lambda a: a.reshape(S, a.shape[-1])
    r4 = lambda a: a.reshape(4, S // 4, a.shape[-1])
    r16 = lambda a: a.reshape(16, S // 16, a.shape[-1])

    first, rest, whole = (0, 1, 2, 3), (4, 5, 6, 7), tuple(range(8))

    def ag_begin(l, which, after, zones=None):
        tag = f"{l}{'' if which == whole else 'h' if which == first else 'r'}"
        zones = landing_zones(l, which) if zones is None else zones
        send_sems, recv_sems, zones, token = _ag_start(zones, after, tag)
        return dict(tag=tag, zones=zones, s=send_sems, r=recv_sems), token

    def ag_second(ch, after):
        ch["ps"], ch["pr"], ch["zones"], token = _ag_pass(ch["zones"], ch["r"], after, ch["tag"])
        return token

    def ag_third(ch, after):
        ch["qs"], ch["qr"], ch["zones"], token = _ag_last(ch["zones"], ch["pr"], after, ch["tag"])
        return token

    def ag_end(ch, after):
        return _ag_wait(ch["zones"], ch["s"], ch["r"], ch["ps"], ch["pr"], ch["qs"], ch["qr"], after, ch["tag"])

    ch_head, _ = ag_begin(0, first, [])
    zones_rest, zones_next = landing_zones(0, rest), landing_zones(1, whole)
    early_zones = {ll: landing_zones(ll, whole) for ll in range(2, L)}
    fill = [z for zs in (zones_rest, zones_next, *early_zones.values(), tabs, wbd) for z in zs]
    head = ag_end(ch_head, ag_third(ch_head, ag_second(ch_head, fill)))
    ch_rest, tok_rest = ag_begin(0, rest, head[0], zones_rest)
    chains = {}
    chains[1], tok_next = ag_begin(1, whole, head[0], zones_next)
    gathered = [None] * L
    xs = x.reshape(S, D)
    saved = []
    for l in range(L):
        first_after, second_after = (), ()
        if l == 0:
            gt1, ut1, dn1, wint = head
            first_after = (tok_rest, tok_next)
        else:
            gt1, ut1, dn1, wint, wout, gt2, ut2, dn2 = gathered[l]
        x0 = xs
        x1, gate1, up1 = _ffn_fwd(x0, g_ffn1[l], gt1, ut1, dn1, after=first_after)
        hmix, vp, q1, k1, v1, q4, k4, v4, q16, k16, v16 = _mix_in_fwd(x1, g_mix[l], wint, tabs)
        q4, k4, v4, q16, k16, v16 = map(flat, (q4, k4, v4, q16, k16, v16))
        ypool, diff = _pool_fwd(vp, wbd[l], pscale[l])
        after_attn = None
        if l == 0:
            after_attn = ag_second(ch_rest, [ypool, q16])
        o1, l1 = _attn_fwd(q1, k1, v1, S, after=after_attn)
        o4, l4 = _attn_fwd(q4, k4, v4, S // 4, after=after_attn)
        o16, l16 = _attn_fwd(q16, k16, v16, S // 16, after=after_attn)
        if l == 0:
            token = ag_third(ch_rest, [o1, o4, o16])
            wout, gt2, ut2, dn2 = ag_end(ch_rest, token)
            gathered[0] = list(head) + [wout, gt2, ut2, dn2]
        elif l + 1 < L:
            second_after = (ag_second(chains[l + 1], [o1, o4, o16]),)
        x2, mixed, o, lse1, lse4, lse16 = _mix_out_fwd(x1, ypool, o1, l1, r4(o4), r4(l4), r16(o16), r16(l16), wout)
        if l == 0:
            second_after = (ag_second(chains[1], x2),)
        x3, gate2, up2 = _ffn_fwd(x2, g_ffn2[l], gt2, ut2, dn2, after=second_after)
        if l + 1 < L:
            token = ag_third(chains[l + 1], x3)
            if l + 2 < L:
                chains[l + 2], token = ag_begin(l + 2, whole, token, early_zones[l + 2])
            gathered[l + 1] = ag_end(chains[l + 1], token)
        saved.append(dict(x0=x0, x1=x1, x2=x2, gate1=gate1, up1=up1, gate2=gate2, up2=up2, hmix=hmix, diff=diff,
                          qkv=((q1, k1, v1), (q4, k4, v4), (q16, k16, v16)), mixed=mixed, o=o,
                          lse=(lse1, flat(lse4), flat(lse16))))
        xs = x3

    dx, loss_part, d_final = _loss_head(xs, final_norm.reshape(1, D), loss_target.reshape(S, D))

    d_norm = {nm: [None] * L for nm in ("ffn1_norm", "mix_norm", "ffn2_norm")}
    d_poolw, d_pscale = [None] * L, [None] * L
    group_a = ("ffn2_w_gate", "ffn2_w_up", "ffn2_w_down", "w_out")
    group_b = ("ffn1_w_gate", "ffn1_w_up", "ffn1_w_down", "w_in")
    acc = {}

    def exchange(full, group, after, tag):
        srcs = [full[nm] for nm in group]
        slots = [lax.empty((NDEV, g.shape[0] // NDEV, D), bf16) for g in srcs]
        ssem, rsem, srcs, slots, token = _rs_start(srcs, slots, after, tag)
        return (srcs, slots, ssem, rsem, tag), token

    def update(l, group, flight, after):
        srcs, slots, ssem, rsem, tag = flight
        srcs, slots = _rs_wait(srcs, slots, ssem, rsem, after, tag)
        for nm, full_g, slots_g in zip(group, srcs, slots):
            acc[nm] = _reduce_adamw(acc.get(nm), me_arr, full_g, slots_g, w_rows[nm], m_rows[nm], v_rows[nm], l)
        return [acc[nm][0] for nm in group], slots

    core_arr = lax.axis_index("c").reshape(1).astype(jnp.int32)
    chip_arr = (2 * lax.axis_index("x") + lax.axis_index("y")).reshape(1).astype(jnp.int32)

    def exchange_cores(full, group, after, tag):
        full4s = [full[nm].reshape(4, 2, full[nm].shape[0] // NDEV, D) for nm in group]
        bufs = [lax.empty((4,) + a.shape[2:], bf16) for a in full4s]
        ssem, rsem, full4s, bufs, token = _pair_start(full4s, bufs, after, tag)
        return (full4s, bufs, ssem, rsem, tag), token

    def exchange_chips(flight, after):
        full4s, bufs, ssem, rsem, tag = flight
        full4s, bufs = _pair_wait(full4s, bufs, ssem, rsem, after, tag)
        sums = _pair_sum(core_arr, full4s, bufs)
        slots = [lax.empty(a.shape, bf16) for a in sums]
        ssem, rsem, sums, slots, token = _chip_start(sums, slots, bufs[0], tag)
        return (sums, slots, ssem, rsem, tag), token

    def update_chips(l, group, flight, after):
        sums, slots, ssem, rsem, tag = flight
        sums, slots = _chip_wait(sums, slots, ssem, rsem, after, tag)
        for nm, sums_g, slots_g in zip(group, sums, slots):
            own = sums_g.reshape(4 * sums_g.shape[1], D)
            acc[nm] = _reduce_adamw(acc.get(nm), chip_arr, own, slots_g, w_rows[nm], m_rows[nm], v_rows[nm], l)
        return [acc[nm][0] for nm in group]

    flights = {}
    token_b = None
    for l in reversed(range(L)):
        sv = saved[l]
        gt1, ut1, dn1, wint, wout, gt2, ut2, dn2 = gathered[l]
        full = {}
        dx, dgate, dup, h, dy, d_norm["ffn2_norm"][l] = _ffn_bwd_d(
            sv["x2"], g_ffn2[l], sv["gate2"], sv["up2"], dx, gt2, ut2, dn2, after=() if token_b is None else (token_b,))
        full["ffn2_w_gate"], full["ffn2_w_up"], full["ffn2_w_down"] = _ffn_bwd_w(h, dy, sv["gate2"], sv["up2"], dgate, dup)

        dxb, dyp, do1, do4, do16, dl1, dl4, dl16 = _mix_out_bwd(dx, sv["o"], wout)
        full["w_out"] = _wgrad(sv["mixed"], dxb)
        flights[l, "a"], token_a = (exchange_cores if l == 0 else exchange)(full, group_a, dxb, f"a{l}")
        dvp, d_poolw[l], d_pscale[l] = _pool_bwd(dyp, sv["diff"], wbd[l], pscale[l], after=(token_a,))
        dos, dls = (do1, flat(do4), flat(do16)), (dl1, flat(dl4), flat(dl16))
        dqkv = []
        for b, lc in enumerate((S, S // 4, S // 16)):
            qb, kb, vb = sv["qkv"][b]
            dqkv.append(_attn_bwd(qb, kb, vb, dos[b], sv["lse"][b], dls[b], lc))
        d4 = tuple(r4(a) for a in dqkv[1])
        d16 = tuple(r16(a) for a in dqkv[2])
        mix_after = ()
        if l == 0:
            flights[0, "a"], token_a = exchange_chips(flights[0, "a"], [dqkv[0][0], dqkv[1][0], dqkv[2][0]])
            mix_after = (token_a,)
        dx, dproj, d_norm["mix_norm"][l] = _mix_in_bwd(dx, sv["x1"], g_mix[l], wint, tabs, dvp, dqkv[0], d4, d16,
                                                       after=mix_after)
        full["w_in"] = _wgrad(dproj, sv["hmix"])

        dx, dgate, dup, h, dy, d_norm["ffn1_norm"][l] = _ffn_bwd_d(sv["x0"], g_ffn1[l], sv["gate1"], sv["up1"], dx, gt1, ut1, dn1)
        full["ffn1_w_gate"], full["ffn1_w_up"], full["ffn1_w_down"] = _ffn_bwd_w(h, dy, sv["gate1"], sv["up1"], dgate, dup)

        after = dx
        if l + 1 < L and l + 1 >= 2:
            after, _ = update(l + 1, group_a, flights.pop((l + 1, "a")), after)
        if l + 1 < L and l + 1 >= 3:
            after, _ = update(l + 1, group_b, flights.pop((l + 1, "b")), after)
        if l > 0:
            flights[l, "b"], token_b = exchange(full, group_b, after, f"b{l}")

    flights[0, "b"], token_b = exchange_cores(full, group_b, dx, "b0")
    pad8 = lambda a: jnp.pad(a, ((0, 8 - a.shape[0]), (0, 0)))
    misc = jnp.concatenate([d_final, jnp.concatenate(d_pscale, axis=1), loss_part], axis=0)
    small = jnp.concatenate(
        [pad8(jnp.concatenate(d_norm[nm], axis=0)) for nm in ("ffn1_norm", "mix_norm", "ffn2_norm")]
        + [pad8(misc), jnp.stack(d_poolw).reshape(L * 16, D)], axis=0)
    small_slots = lax.dynamic_update_slice(lax.empty((NDEV, SMALL_ROWS, D), f32), small[None], (me_idx, 0, 0))
    pack_sems = _rs_start([small], [small_slots], token_b, "pack")
    flights[0, "b"], token_b = exchange_chips(flights[0, "b"], pack_sems[-1])

    after = token_b
    for key in [(2, "b"), (1, "a"), (1, "b")]:
        after, _ = update(key[0], group_a if key[1] == "a" else group_b, flights.pop(key), after)
    _, pack_slots = _rs_wait(pack_sems[2], pack_sems[3], pack_sems[0], pack_sems[1], after, "pack")
    sm = _sum_slots(pack_slots[0], SMALL_ROWS)
    grads = {}
    grads["ffn1_norm"], grads["mix_norm"], grads["ffn2_norm"] = sm[0:L], sm[8:8 + L], sm[16:16 + L]
    grads["final_norm"] = sm[24]
    grads["pool_scale"] = sm[25].reshape(L, PW)
    grads["pool_w"] = sm[32:32 + L * 16].reshape(L, 4, 64, 64)
    loss = sm[26, 0]
    upd = {nm: _adamw(weights[nm], grads[nm], moms[nm], vels[nm]) for nm in names if nm not in _BIG}
    after = update_chips(0, group_a, flights.pop((0, "a")), [upd[nm][0] for nm in upd])
    update_chips(0, group_b, flights.pop((0, "b")), after)
    for nm in _BIG:
        grads[nm], upd[nm] = as_rows(acc[nm][0], nm), tuple(as_rows(a, nm) for a in acc[nm][1:])
    return (loss, dx.reshape(1, S, D), *[grads[nm] for nm in names], *[upd[nm][0] for nm in names],
            *[upd[nm][1] for nm in names], *[upd[nm][2] for nm in names])
```

```python
import jax
import jax.numpy as jnp
from jax import lax
from jax.experimental import pallas as pl
from jax.experimental.pallas import tpu as pltpu

f32 = jnp.float32
bf16 = jnp.bfloat16
SDS = jax.ShapeDtypeStruct

D = 1024
S = 2048
F = 2816
L = 4
PW = 256
AW = 768
PROJ = PW + 3 * AW
NDEV = 8
TM = 256
QB = 128
HALF = 64
NG = AW // 128
NORM_EPS = 1e-6
MASK_VALUE = -1e30
ROPE_THETA = 500000.0
ADAM_LR, ADAM_B1, ADAM_B2, ADAM_EPS, ADAM_WD, ADAM_STEP = 0.001, 0.9, 0.999, 1e-08, 0.01, 10
POOL_WINDOWS = (2, 4, 8, 16)
PAD = 8
SMALL_ROWS = 96
VMEM_LIMIT = 56 * 1024 * 1024

_CP = pltpu.CompilerParams(vmem_limit_bytes=VMEM_LIMIT)
_ANY = pl.BlockSpec(memory_space=pl.ANY)
_HBM = pl.BlockSpec(memory_space=pltpu.HBM)
_SEM = pl.BlockSpec(memory_space=pltpu.SEMAPHORE)
_MESH = pl.DeviceIdType.MESH
_CP_SPLIT = pltpu.CompilerParams(has_side_effects=pltpu.SideEffectType.DATAFLOW_SIDE_EFFECTING)


def _dot_nn(a, b):
    return lax.dot_general(a, b, (((1,), (0,)), ((), ())), preferred_element_type=f32)


def _dot_nt(a, b):
    return lax.dot_general(a, b, (((1,), (1,)), ((), ())), preferred_element_type=f32)


def _dot_tn(a, b):
    return lax.dot_general(a, b, (((0,), (0,)), ((), ())), preferred_element_type=f32)


def _rms(x, g):
    r = lax.rsqrt(jnp.mean(x * x, axis=-1, keepdims=True) + NORM_EPS)
    xh = x * r
    return r, xh, xh * g


def _rms_bwd(dh, r, xh, g):
    dxh = dh * g
    return r * (dxh - xh * jnp.mean(dxh * xh, axis=-1, keepdims=True))


def _tile(cols, rows=TM):
    return pl.BlockSpec((rows, cols), lambda i: (i, 0))


def _const(shape):
    return pl.BlockSpec(shape, lambda i: (0,) * len(shape))


def _layer(rows, cols):
    return pl.BlockSpec((rows, cols), lambda i: (0, 0), pipeline_mode=pl.Buffered(1))


def _p4(cols=AW):
    return pl.BlockSpec((4, TM // 4, cols), lambda i: (0, i, 0))


def _p16(cols=AW):
    return pl.BlockSpec((16, TM // 16, cols), lambda i: (0, i, 0))


def _cols(j):
    return slice(128 * j, 128 * (j + 1))


def _follow(body, n_in, after):
    k = len(after)
    return body if k == 0 else (lambda *refs: body(*refs[:n_in], *refs[n_in + k:]))


def _ffn_fwd(x, g, gt, ut, dn, after=()):
    def body(x_ref, g_ref, gt_ref, ut_ref, dn_ref, xo_ref, gate_ref, up_ref):
        x = x_ref[...]
        _, _, hn = _rms(x, g_ref[...])
        h = hn.astype(bf16)
        gate = _dot_nt(h, gt_ref[...])
        up = _dot_nt(h, ut_ref[...])
        gate_ref[...] = gate.astype(bf16)
        up_ref[...] = up.astype(bf16)
        a = (gate * jax.nn.sigmoid(gate) * up).astype(bf16)
        xo_ref[...] = x + 0.5 * _dot_nn(a, dn_ref[...])

    rows = 2 * TM
    return pl.pallas_call(
        _follow(body, 5, after), grid=(S // rows,),
        in_specs=[_tile(D, rows), _layer(1, D), _layer(F, D), _layer(F, D), _layer(F, D)] + [_ANY] * len(after),
        out_specs=[_tile(D, rows), _tile(F, rows), _tile(F, rows)],
        out_shape=[SDS((S, D), f32), SDS((S, F), bf16), SDS((S, F), bf16)],
        compiler_params=_CP, name="ffn_fwd")(x, g, gt, ut, dn, *after)


def _ffn_bwd_d(x, g, gate, up, dxo, gt, ut, dn, after=()):
    def body(x_ref, g_ref, gate_ref, up_ref, dxo_ref, gt_ref, ut_ref, dn_ref,
             dx_ref, dgate_ref, dup_ref, h_ref, dy_ref, dg_ref):
        x = x_ref[...]
        g = g_ref[...]
        r, xh, hn = _rms(x, g)
        h_ref[...] = hn.astype(bf16)
        dxo = dxo_ref[...]
        dy = (0.5 * dxo).astype(bf16)
        dy_ref[...] = dy
        da = _dot_nt(dy, dn_ref[...])
        gate = gate_ref[...].astype(f32)
        up = up_ref[...].astype(f32)
        sg = jax.nn.sigmoid(gate)
        dgate = (da * up * (sg * (1.0 + gate * (1.0 - sg)))).astype(bf16)
        dup = (da * (gate * sg)).astype(bf16)
        dgate_ref[...] = dgate
        dup_ref[...] = dup
        dh = _dot_nn(dgate, gt_ref[...]) + _dot_nn(dup, ut_ref[...])

        @pl.when(pl.program_id(0) == 0)
        def _():
            dg_ref[...] = jnp.zeros_like(dg_ref)

        dg_ref[...] += jnp.sum(dh * xh, axis=0, keepdims=True)
        dx_ref[...] = dxo + _rms_bwd(dh, r, xh, g)

    return pl.pallas_call(
        _follow(body, 8, after), grid=(S // TM,),
        in_specs=[_tile(D), _layer(1, D), _tile(F), _tile(F), _tile(D),
                  _layer(F, D), _layer(F, D), _layer(F, D)] + [_ANY] * len(after),
        out_specs=[_tile(D), _tile(F), _tile(F), _tile(D), _tile(D), _const((1, D))],
        out_shape=[SDS((S, D), f32), SDS((S, F), bf16), SDS((S, F), bf16), SDS((S, D), bf16),
                   SDS((S, D), bf16), SDS((1, D), f32)],
        compiler_params=_CP, name="ffn_bwd_d")(x, g, gate, up, dxo, gt, ut, dn, *after)


def _ffn_bwd_w(h, dy, gate, up, dgate, dup):
    fc = 256

    def body(h_ref, dy_ref, gate_ref, up_ref, dgate_ref, dup_ref, dgt_ref, dut_ref, ddn_ref):
        gate = gate_ref[...].astype(f32)
        a = (gate * jax.nn.sigmoid(gate) * up_ref[...].astype(f32)).astype(bf16)
        ddn_ref[...] = _dot_tn(a, dy_ref[...]).astype(bf16)
        h = h_ref[...]
        dgt_ref[...] = _dot_tn(dgate_ref[...], h).astype(bf16)
        dut_ref[...] = _dot_tn(dup_ref[...], h).astype(bf16)

    col = pl.BlockSpec((S, fc), lambda j: (0, j))
    row = pl.BlockSpec((fc, D), lambda j: (j, 0))
    full = pl.BlockSpec((S, D), lambda j: (0, 0))
    return pl.pallas_call(
        body, grid=(F // fc,),
        in_specs=[full, full, col, col, col, col],
        out_specs=[row, row, row],
        out_shape=[SDS((F, D), bf16)] * 3,
        compiler_params=_CP, name="ffn_bwd_w")(h, dy, gate, up, dgate, dup)


def _wgrad(a, b):
    m, n = a.shape[1], b.shape[1]
    mc = 2 * TM

    def body(a_ref, b_ref, o_ref):
        o_ref[...] = _dot_tn(a_ref[...], b_ref[...]).astype(bf16)

    return pl.pallas_call(
        body, grid=(m // mc,),
        in_specs=[pl.BlockSpec((S, mc), lambda j: (0, j)), pl.BlockSpec((S, n), lambda j: (0, 0))],
        out_specs=pl.BlockSpec((mc, n), lambda j: (j, 0)),
        out_shape=SDS((m, n), bf16),
        compiler_params=_CP, name="wgrad")(a, b)


def _rope(t, c, sn, sp):
    return t * c + pltpu.roll(t, 120, 1) * sn + pltpu.roll(t, 8, 1) * sp


def _rope_bwd(d, c, sn, sp):
    return d * c + pltpu.roll(d * sn, 8, 1) + pltpu.roll(d * sp, 120, 1)


def _rope_tables(positions):
    inv_freq = ROPE_THETA ** (-jnp.arange(0, 16, 2, dtype=f32) / 16)
    ang = positions.reshape(S, 1).astype(f32) * inv_freq
    cos, sin = jnp.cos(ang), jnp.sin(ang)
    one = jnp.ones((S, 48), f32)
    zero8 = jnp.zeros((S, 8), f32)
    zero48 = jnp.zeros((S, 48), f32)
    c = jnp.concatenate([cos, cos, one], axis=1)
    sn = jnp.concatenate([-sin, zero8, zero48], axis=1)
    sp = jnp.concatenate([zero8, sin, zero48], axis=1)
    return tuple(jnp.concatenate([t, t], axis=1) for t in (c, sn, sp))


def _dilation_perm(n, back=False):
    per = TM // n
    i = lax.broadcasted_iota(jnp.int32, (TM, TM), 1 if back else 0)
    j = lax.broadcasted_iota(jnp.int32, (TM, TM), 0 if back else 1)
    return jnp.where(j == n * (i % per) + i // per, 1.0, 0.0).astype(bf16)


def _mix_in_fwd(x, g, wint, tabs):
    def body(x_ref, g_ref, w_ref, c_ref, sn_ref, sp_ref,
             h_ref, vp_ref, q1, k1, v1, q4, k4, v4, q16, k16, v16):
        _, _, hn = _rms(x_ref[...], g_ref[...])
        h = hn.astype(bf16)
        h_ref[...] = h
        proj = _dot_nt(h, w_ref[...])
        vp_ref[...] = proj[:, :PW]
        c, sn, sp = c_ref[...], sn_ref[...], sp_ref[...]
        perm4, perm16 = _dilation_perm(4), _dilation_perm(16)
        for kind, (o1, o4, o16) in enumerate(((q1, q4, q16), (k1, k4, k16), (v1, v4, v16))):
            for j in range(NG):
                t = proj[:, PW + kind * AW + 128 * j: PW + kind * AW + 128 * (j + 1)]
                if kind == 0:
                    t = _rope(t, c, sn, sp) * 0.125
                elif kind == 1:
                    t = _rope(t, c, sn, sp)
                o1[:, _cols(j)] = t.astype(bf16)
            nat = o1[...]
            o4[...] = _dot_nn(perm4, nat).astype(bf16).reshape(4, TM // 4, AW)
            o16[...] = _dot_nn(perm16, nat).astype(bf16).reshape(16, TM // 16, AW)

    nat, d4, d16 = SDS((S, AW), bf16), SDS((4, S // 4, AW), bf16), SDS((16, S // 16, AW), bf16)
    return pl.pallas_call(
        body, grid=(S // TM,),
        in_specs=[_tile(D), _layer(1, D), _layer(PROJ, D), _tile(128), _tile(128), _tile(128)],
        out_specs=[_tile(D), _tile(PW)] + [_tile(AW)] * 3 + [_p4()] * 3 + [_p16()] * 3,
        out_shape=[SDS((S, D), bf16), SDS((S, PW), f32)] + [nat] * 3 + [d4] * 3 + [d16] * 3,
        compiler_params=_CP, name="mix_in_fwd")(x, g, wint, *tabs)


def _mix_in_bwd(dxo, x, g, wint, tabs, dvp, d1, d4, d16, after=()):
    def body(dxo_ref, x_ref, g_ref, w_ref, c_ref, sn_ref, sp_ref, dvp_ref,
             dq1, dk1, dv1, dq4, dk4, dv4, dq16, dk16, dv16,
             dx_ref, dproj_ref, dg_ref):
        c, sn, sp = c_ref[...], sn_ref[...], sp_ref[...]
        dproj_ref[:, :PW] = dvp_ref[...].astype(bf16)
        back4, back16 = _dilation_perm(4, True), _dilation_perm(16, True)
        for kind, (a1, a4, a16) in enumerate(((dq1, dq4, dq16), (dk1, dk4, dk16), (dv1, dv4, dv16))):
            n4 = _dot_nn(back4, a4[...].reshape(TM, AW))
            n16 = _dot_nn(back16, a16[...].reshape(TM, AW))
            for j in range(NG):
                t = a1[:, _cols(j)].astype(f32) + n4[:, _cols(j)] + n16[:, _cols(j)]
                if kind == 0:
                    t = _rope_bwd(t * 0.125, c, sn, sp)
                elif kind == 1:
                    t = _rope_bwd(t, c, sn, sp)
                dproj_ref[:, PW + kind * AW + 128 * j: PW + kind * AW + 128 * (j + 1)] = t.astype(bf16)
        g = g_ref[...]
        r_, xh, _ = _rms(x_ref[...], g)
        dh = _dot_nn(dproj_ref[...], w_ref[...])

        @pl.when(pl.program_id(0) == 0)
        def _():
            dg_ref[...] = jnp.zeros_like(dg_ref)

        dg_ref[...] += jnp.sum(dh * xh, axis=0, keepdims=True)
        dx_ref[...] = dxo_ref[...] + _rms_bwd(dh, r_, xh, g)

    return pl.pallas_call(
        _follow(body, 17, after), grid=(S // TM,),
        in_specs=[_tile(D), _tile(D), _layer(1, D), _layer(PROJ, D), _tile(128), _tile(128), _tile(128),
                  _tile(PW)] + [_tile(AW)] * 3 + [_p4()] * 3 + [_p16()] * 3 + [_ANY] * len(after),
        out_specs=[_tile(D), _tile(PROJ), _const((1, D))],
        out_shape=[SDS((S, D), f32), SDS((S, PROJ), bf16), SDS((1, D), f32)],
        compiler_params=_CP, name="mix_in_bwd")(dxo, x, g, wint, *tabs, dvp, *d1, *d4, *d16, *after)


def _pool_sums(pad_ref, base, rows, adjoint):
    lane_group = lax.broadcasted_iota(jnp.int32, (rows, PW), 1) // 64
    sign = -1 if adjoint else 1

    def sh(o):
        return pad_ref[pl.ds(PAD + base + sign * o, rows), :]

    out = None
    acc = None
    lo, hi = 0, 0
    for gi, w in enumerate(POOL_WINDOWS):
        for o in list(range(-(w // 2), lo)) + list(range(hi, w - w // 2)):
            acc = sh(o) if acc is None else acc + sh(o)
        lo, hi = -(w // 2), w - w // 2
        out = acc if out is None else jnp.where(lane_group >= gi, acc, out)
    return out


def _pool_counts(base, rows):
    pos = base + lax.broadcasted_iota(jnp.int32, (rows, PW), 0)
    lane_group = lax.broadcasted_iota(jnp.int32, (rows, PW), 1) // 64
    cnt = None
    for gi, w in enumerate(POOL_WINDOWS):
        lo = jnp.maximum(pos - w // 2, 0)
        hi = jnp.minimum(pos + w - 1 - w // 2, S - 1)
        c = (hi - lo + 1).astype(f32)
        cnt = c if cnt is None else jnp.where(lane_group >= gi, c, cnt)
    return cnt


def _pool_fwd(vp, wbd, scale):
    ch = 256

    def body(vp_ref, w_ref, sc_ref, y_ref, diff_ref, pad):
        pad[pl.ds(0, PAD), :] = jnp.zeros((PAD, PW), f32)
        pad[pl.ds(PAD + S, PAD), :] = jnp.zeros((PAD, PW), f32)
        pad[pl.ds(PAD, S), :] = vp_ref[...]
        for b in range(S // ch):
            base = b * ch
            pooled = _pool_sums(pad, base, ch, False) / _pool_counts(base, ch)
            diff = (pooled - vp_ref[pl.ds(base, ch), :]).astype(bf16)
            diff_ref[pl.ds(base, ch), :] = diff
            y_ref[pl.ds(base, ch), :] = _dot_nn(diff, w_ref[...]) * sc_ref[...]

    whole = lambda shape: pl.BlockSpec(shape, lambda i: (0,) * len(shape))
    return pl.pallas_call(
        body, grid=(1,),
        in_specs=[whole((S, PW)), whole((PW, PW)), whole((1, PW))],
        out_specs=[whole((S, PW)), whole((S, PW))],
        out_shape=[SDS((S, PW), f32), SDS((S, PW), bf16)],
        scratch_shapes=[pltpu.VMEM((S + 2 * PAD, PW), f32)],
        compiler_params=_CP, name="pool_fwd")(vp, wbd, scale)


def _pool_bwd(dy, diff, wbd, scale, after=()):
    ch = 256

    def body(dy_ref, diff_ref, w_ref, sc_ref, dvp_ref, dw_ref, dsc_ref, pad):
        pad[pl.ds(0, PAD), :] = jnp.zeros((PAD, PW), f32)
        pad[pl.ds(PAD + S, PAD), :] = jnp.zeros((PAD, PW), f32)
        dw = jnp.zeros((PW, PW), f32)
        dsc = jnp.zeros((1, PW), f32)
        for b in range(S // ch):
            base = b * ch
            dy = dy_ref[pl.ds(base, ch), :]
            diff = diff_ref[pl.ds(base, ch), :]
            dsc = dsc + jnp.sum(dy * _dot_nn(diff, w_ref[...]), axis=0, keepdims=True)
            dz = (dy * sc_ref[...]).astype(bf16)
            dw = dw + _dot_tn(diff, dz)
            ddiff = _dot_nt(dz, w_ref[...])
            dvp_ref[pl.ds(base, ch), :] = -ddiff
            pad[pl.ds(PAD + base, ch), :] = ddiff / _pool_counts(base, ch)
        for gi in range(4):
            dw_ref[gi] = dw[64 * gi:64 * (gi + 1), 64 * gi:64 * (gi + 1)]
        dsc_ref[...] = dsc
        for b in range(S // ch):
            base = b * ch
            dvp_ref[pl.ds(base, ch), :] += _pool_sums(pad, base, ch, True)

    whole = lambda shape: pl.BlockSpec(shape, lambda i: (0,) * len(shape))
    return pl.pallas_call(
        _follow(body, 4, after), grid=(1,),
        in_specs=[whole((S, PW)), whole((S, PW)), whole((PW, PW)), whole((1, PW))] + [_ANY] * len(after),
        out_specs=[whole((S, PW)), whole((4, 64, 64)), whole((1, PW))],
        out_shape=[SDS((S, PW), f32), SDS((4, 64, 64), f32), SDS((1, PW), f32)],
        scratch_shapes=[pltpu.VMEM((S + 2 * PAD, PW), f32)],
        compiler_params=_CP, name="pool_bwd")(dy, diff, wbd, scale, *after)


def _attn_blocks(lc):
    bpc = lc // QB
    kw = min(2 * QB, lc)
    blocks = []
    for b in range(S // QB):
        t0 = (b % bpc) * QB
        ks_in = min(max(t0 - HALF, 0), lc - kw)
        blocks.append((b * QB, (b // bpc) * lc + ks_in, t0 - ks_in))
    return kw, blocks


def _attn_bias(bias_ref, kw, shifts):
    r = lax.broadcasted_iota(jnp.int32, (2 * QB, kw), 0) % QB
    c = lax.broadcasted_iota(jnp.int32, (2 * QB, kw), 1)
    for i, shift in enumerate(shifts):
        bias_ref[i] = jnp.where(jnp.abs(r + shift - c) <= HALF, 0.0, MASK_VALUE).astype(f32)


def _head_put(stats, pair, v0, v1, lane):
    return jnp.where(lane == 2 * pair, v0, jnp.where(lane == 2 * pair + 1, v1, stats))


def _head_cols(stats, pair, lane):
    c0 = jnp.sum(jnp.where(lane == 2 * pair, stats, 0.0), axis=-1, keepdims=True)
    c1 = jnp.sum(jnp.where(lane == 2 * pair + 1, stats, 0.0), axis=-1, keepdims=True)
    return jnp.concatenate([c0, c1], axis=0)


def _head_spread(stats, pair, head0):
    return jnp.where(head0, stats[:, 2 * pair:2 * pair + 1], stats[:, 2 * pair + 1:2 * pair + 2])


def _stack_heads(blk, head0):
    zero = jnp.zeros_like(blk)
    return jnp.concatenate([jnp.where(head0, blk, zero), jnp.where(head0, zero, blk)], axis=0)


def _attn_fwd(q, k, v, lc, after=None):
    kw, blocks = _attn_blocks(lc)
    shifts = sorted({b[2] for b in blocks})

    def body(q_ref, k_ref, v_ref, *refs):
        o_ref, lse_ref, bias_ref = refs[-3:]
        lane = lax.broadcasted_iota(jnp.int32, (QB, 128), 1)
        head0 = lane < 64
        pair = pl.program_id(0)
        _attn_bias(bias_ref, kw, shifts)

        @pl.when(pair == 0)
        def _():
            lse_ref[...] = jnp.zeros_like(lse_ref)

        for row0, kstart, shift in blocks:
            q2 = _stack_heads(q_ref[pl.ds(row0, QB), :], head0)
            kb = k_ref[pl.ds(kstart, kw), :]
            vb = v_ref[pl.ds(kstart, kw), :]
            s = _dot_nt(q2, kb) + bias_ref[shifts.index(shift)]
            m = jnp.max(s, axis=-1, keepdims=True)
            p = jnp.exp(s - m)
            den = jnp.sum(p, axis=-1, keepdims=True)
            o2 = _dot_nn(p.astype(bf16), vb) / den
            lse2 = m + jnp.log(den)
            o_ref[pl.ds(row0, QB), :] = jnp.where(head0, o2[:QB], o2[QB:]).astype(bf16)
            lse_ref[pl.ds(row0, QB), :] = _head_put(lse_ref[pl.ds(row0, QB), :], pair, lse2[:QB], lse2[QB:], lane)

    col = pl.BlockSpec((S, 128), lambda p: (0, p))
    extra = () if after is None else (after,)
    return pl.pallas_call(
        body, grid=(NG,), in_specs=[col, col, col] + [_ANY] * len(extra),
        out_specs=[col, pl.BlockSpec((S, 128), lambda p: (0, 0))],
        out_shape=[SDS((S, AW), bf16), SDS((S, 128), f32)],
        scratch_shapes=[pltpu.VMEM((len(shifts), 2 * QB, kw), f32)],
        compiler_params=_CP, name=f"attn_fwd_{lc}")(q, k, v, *extra)


def _attn_bwd(q, k, v, do, lse, delta, lc):
    kw, blocks = _attn_blocks(lc)
    shifts = sorted({b[2] for b in blocks})

    def body(q_ref, k_ref, v_ref, do_ref, lse_ref, dl_ref, dq_ref, dk_out, dv_out, bias_ref, dk_ref, dv_ref):
        lane = lax.broadcasted_iota(jnp.int32, (QB, 128), 1)
        head0 = lane < 64
        pair = pl.program_id(0)
        _attn_bias(bias_ref, kw, shifts)
        dk_ref[...] = jnp.zeros_like(dk_ref)
        dv_ref[...] = jnp.zeros_like(dv_ref)
        for row0, kstart, shift in blocks:
            q2 = _stack_heads(q_ref[pl.ds(row0, QB), :], head0)
            do2 = _stack_heads(do_ref[pl.ds(row0, QB), :], head0)
            lse2 = _head_cols(lse_ref[pl.ds(row0, QB), :], pair, lane)
            dl2 = _head_cols(dl_ref[pl.ds(row0, QB), :], pair, lane)
            kb = k_ref[pl.ds(kstart, kw), :]
            vb = v_ref[pl.ds(kstart, kw), :]
            p = jnp.exp(_dot_nt(q2, kb) + bias_ref[shifts.index(shift)] - lse2)
            ds = (p * (_dot_nt(do2, vb) - dl2)).astype(bf16)
            dq2 = _dot_nn(ds, kb)
            dq_ref[pl.ds(row0, QB), :] = jnp.where(head0, dq2[:QB], dq2[QB:]).astype(bf16)
            dk_ref[pl.ds(kstart, kw), :] += _dot_tn(ds, q2)
            dv_ref[pl.ds(kstart, kw), :] += _dot_tn(p.astype(bf16), do2)
        dk_out[...] = dk_ref[...].astype(bf16)
        dv_out[...] = dv_ref[...].astype(bf16)

    col = pl.BlockSpec((S, 128), lambda p: (0, p))
    stats = pl.BlockSpec((S, 128), lambda p: (0, 0))
    return pl.pallas_call(
        body, grid=(NG,), in_specs=[col] * 4 + [stats] * 2, out_specs=[col] * 3,
        out_shape=[SDS((S, AW), bf16)] * 3,
        scratch_shapes=[pltpu.VMEM((len(shifts), 2 * QB, kw), f32), pltpu.VMEM((S, 128), f32),
                        pltpu.VMEM((S, 128), f32)],
        compiler_params=_CP, name=f"attn_bwd_{lc}")(q, k, v, do, lse, delta)


def _mix_out_fwd(x, ypool, o1, l1, o4, l4, o16, l16, wout):
    def body(x_ref, yp_ref, o1_ref, l1_ref, o4_ref, l4_ref, o16_ref, l16_ref, w_ref,
             xo_ref, mixed_ref, o_ref, lse1_ref, lse4_ref, lse16_ref, sl4, sl16, sl):
        head0 = lax.broadcasted_iota(jnp.int32, (TM, 128), 1) < 64
        for r in range(4):
            sl4[pl.ds(r, TM // 4, stride=4), :] = l4_ref[r]
        for r in range(16):
            sl16[pl.ds(r, TM // 16, stride=16), :] = l16_ref[r]
        n4 = _dot_nn(_dilation_perm(4, True), o4_ref[...].reshape(TM, AW))
        n16 = _dot_nn(_dilation_perm(16, True), o16_ref[...].reshape(TM, AW))
        a, b, c = l1_ref[...], sl4[...], sl16[...]
        m = jnp.maximum(jnp.maximum(a, b), c)
        wa, wb, wc = jnp.exp(a - m), jnp.exp(b - m), jnp.exp(c - m)
        den = wa + wb + wc
        wa, wb, wc = wa / den, wb / den, wc / den
        lse = m + jnp.log(den)
        lse1_ref[...] = lse
        sl[...] = lse
        mixed_ref[:, :PW] = yp_ref[...].astype(bf16)
        for j in range(NG):
            y = (_head_spread(wa, j, head0) * o1_ref[:, _cols(j)].astype(f32)
                 + _head_spread(wb, j, head0) * n4[:, _cols(j)] + _head_spread(wc, j, head0) * n16[:, _cols(j)])
            o_ref[:, _cols(j)] = y
            mixed_ref[:, PW + 128 * j: PW + 128 * (j + 1)] = y.astype(bf16)
        for r in range(4):
            lse4_ref[r] = sl[pl.ds(r, TM // 4, stride=4), :]
        for r in range(16):
            lse16_ref[r] = sl[pl.ds(r, TM // 16, stride=16), :]
        xo_ref[...] = x_ref[...] + _dot_nn(mixed_ref[...], w_ref[...])

    return pl.pallas_call(
        body, grid=(S // TM,),
        in_specs=[_tile(D), _tile(PW), _tile(AW), _tile(128), _p4(), _p4(128), _p16(), _p16(128), _layer(D, D)],
        out_specs=[_tile(D), _tile(D), _tile(AW), _tile(128), _p4(128), _p16(128)],
        out_shape=[SDS((S, D), f32), SDS((S, D), bf16), SDS((S, AW), f32), SDS((S, 128), f32),
                   SDS((4, S // 4, 128), f32), SDS((16, S // 16, 128), f32)],
        scratch_shapes=[pltpu.VMEM((TM, 128), f32)] * 3,
        compiler_params=_CP, name="mix_out_fwd")(x, ypool, o1, l1, o4, l4, o16, l16, wout)


def _mix_out_bwd(dxo, o, wout):
    def body(dxo_ref, o_ref, w_ref, dxb_ref, dyp_ref, do1, do4, do16, dl1, dl4, dl16, sdl):
        dxb = dxo_ref[...].astype(bf16)
        dxb_ref[...] = dxb
        dm = _dot_nt(dxb, w_ref[...])
        dyp_ref[...] = dm[:, :PW]
        lane = lax.broadcasted_iota(jnp.int32, (TM, 128), 1)
        head0 = lane < 64
        dl = jnp.zeros((TM, 128), f32)
        for j in range(NG):
            d = dm[:, PW + 128 * j: PW + 128 * (j + 1)]
            prod = d * o_ref[:, _cols(j)]
            dl = _head_put(dl, j, jnp.sum(jnp.where(head0, prod, 0.0), axis=-1, keepdims=True),
                           jnp.sum(jnp.where(head0, 0.0, prod), axis=-1, keepdims=True), lane)
            do1[:, _cols(j)] = d.astype(bf16)
        dl1[...] = dl
        sdl[...] = dl
        for r in range(4):
            dl4[r] = sdl[pl.ds(r, TM // 4, stride=4), :]
        for r in range(16):
            dl16[r] = sdl[pl.ds(r, TM // 16, stride=16), :]
        nat = do1[...]
        do4[...] = _dot_nn(_dilation_perm(4), nat).astype(bf16).reshape(4, TM // 4, AW)
        do16[...] = _dot_nn(_dilation_perm(16), nat).astype(bf16).reshape(16, TM // 16, AW)

    return pl.pallas_call(
        body, grid=(S // TM,),
        in_specs=[_tile(D), _tile(AW), _layer(D, D)],
        out_specs=[_tile(D), _tile(PW), _tile(AW), _p4(), _p16(), _tile(128), _p4(128), _p16(128)],
        out_shape=[SDS((S, D), bf16), SDS((S, PW), f32),
                   SDS((S, AW), bf16), SDS((4, S // 4, AW), bf16), SDS((16, S // 16, AW), bf16),
                   SDS((S, 128), f32), SDS((4, S // 4, 128), f32), SDS((16, S // 16, 128), f32)],
        scratch_shapes=[pltpu.VMEM((TM, 128), f32)],
        compiler_params=_CP, name="mix_out_bwd")(dxo, o, wout)


def _loss_head(x, g, target):
    def body(x_ref, g_ref, t_ref, dx_ref, loss_ref, dg_ref):
        g = g_ref[...]
        r, xh, y = _rms(x_ref[...], g)
        err = y - t_ref[...]
        dy = err * (1.0 / D)

        @pl.when(pl.program_id(0) == 0)
        def _():
            loss_ref[...] = jnp.zeros_like(loss_ref)
            dg_ref[...] = jnp.zeros_like(dg_ref)

        loss_ref[...] += jnp.broadcast_to(0.5 * jnp.sum(jnp.mean(err * err, axis=-1, keepdims=True)), (1, D))
        dg_ref[...] += jnp.sum(dy * xh, axis=0, keepdims=True)
        dx_ref[...] = _rms_bwd(dy, r, xh, g)

    return pl.pallas_call(
        body, grid=(S // TM,),
        in_specs=[_tile(D), _const((1, D)), _tile(D)],
        out_specs=[_tile(D), _const((1, D)), _const((1, D))],
        out_shape=[SDS((S, D), f32), SDS((1, D), f32), SDS((1, D), f32)],
        compiler_params=_CP, name="loss_head")(x, g, target)


def _peer(k):
    x, y, c = lax.axis_index("x"), lax.axis_index("y"), lax.axis_index("c")
    px = 1 - x if k & 4 else x
    py = 1 - y if k & 2 else y
    pc = 1 - c if k & 1 else c
    return (px, py, pc), 4 * px + 2 * py + pc


def _diag_route():
    x, y, c = lax.axis_index("x"), lax.axis_index("y"), lax.axis_index("c")
    idx_x, idx_y = _peer(4)[1], _peer(2)[1]
    return idx_x + c * (idx_y - idx_x), (x + c * (1 - 2 * x), (1 - y) + c * (2 * y - 1), c)


def _hbm(a):
    return pltpu.with_memory_space_constraint(a, pltpu.HBM)


def _rows(ref, idx):
    r = ref.shape[0] // NDEV
    return ref.at[pl.ds(idx * r, r), :]


def _row_copy(ref, idx, send_sem, recv_sem, to):
    return pltpu.make_async_remote_copy(src_ref=_rows(ref, idx), dst_ref=_rows(ref, idx), send_sem=send_sem,
                                        recv_sem=recv_sem, device_id=to, device_id_type=_MESH)


def _place_own(me, shards, l):
    n = len(shards)

    def body(me_ref, *refs):
        for t in range(n):
            refs[n + t][...] = refs[t][...].astype(bf16)

    grid_spec = pltpu.PrefetchScalarGridSpec(
        num_scalar_prefetch=1, grid=(1,),
        in_specs=[pl.BlockSpec((None, s.shape[1], D), lambda i, me_ref: (l, 0, 0)) for s in shards],
        out_specs=[pl.BlockSpec((s.shape[1], D), lambda i, me_ref: (me_ref[0], 0)) for s in shards])
    return pl.pallas_call(
        body, grid_spec=grid_spec, out_shape=[SDS((NDEV * s.shape[1], D), bf16) for s in shards],
        compiler_params=_CP, name="place_own")(me, *shards)


_TOKEN = SDS((8, 128), f32)
def _ag_start(lands, after, l):
    n = len(lands)
    after = list(after) if isinstance(after, (list, tuple)) else [after]

    def body(*refs):
        zones, send_sems, recv_sems, token = refs[:n], refs[n + len(after)], refs[n + len(after) + 1], refs[-1]
        _, me_idx = _peer(0)
        for k, mask in enumerate((1, 4, 2)):
            for t in range(n):
                _row_copy(zones[t], me_idx, send_sems.at[k * n + t], recv_sems.at[k * n + t], _peer(mask)[0]).start()
        token[...] = jnp.zeros_like(token)

    outs = pl.pallas_call(
        body, name=f"ag_start_{l}", in_specs=[_HBM] * n + [_ANY] * len(after),
        out_specs=(_SEM, _SEM, *[_HBM] * n, pl.BlockSpec(memory_space=pltpu.VMEM)),
        out_shape=(pltpu.SemaphoreType.DMA((3 * n,)), pltpu.SemaphoreType.DMA((3 * n,)),
                   *[pltpu.HBM(a.shape, a.dtype) for a in lands], _TOKEN),
        input_output_aliases={t: 2 + t for t in range(n)}, compiler_params=_CP_SPLIT)(
            *[_hbm(a) for a in lands], *after)
    return outs[0], outs[1], list(outs[2:2 + n]), outs[-1]


def _ag_pass(lands, recv_sems, after, l):
    n = len(lands)
    after = list(after) if isinstance(after, (list, tuple)) else [after]

    def body(*refs):
        zones, recv_sems = refs[:n], refs[n]
        psend, precv, token = refs[n + 1 + len(after)], refs[n + 2 + len(after)], refs[-1]
        me, _ = _peer(0)
        sibling, _ = _peer(1)
        for j, mask in enumerate((4, 2)):
            idx = _peer(mask)[1]
            for t in range(n):
                _row_copy(zones[t], idx, psend.at[j * n + t], recv_sems.at[(1 + j) * n + t], me).wait_recv()
                _row_copy(zones[t], idx, psend.at[j * n + t], precv.at[j * n + t], sibling).start()
        fwd_idx, fwd_dev = _diag_route()
        for t in range(n):
            _row_copy(zones[t], fwd_idx, psend.at[2 * n + t], precv.at[2 * n + t], fwd_dev).start()
        token[...] = jnp.zeros_like(token)

    outs = pl.pallas_call(
        body, name=f"ag_pass_{l}", in_specs=[_HBM] * n + [_SEM] + [_ANY] * len(after),
        out_specs=(_SEM, _SEM, *[_HBM] * n, pl.BlockSpec(memory_space=pltpu.VMEM)),
        out_shape=(pltpu.SemaphoreType.DMA((3 * n,)), pltpu.SemaphoreType.DMA((3 * n,)),
                   *[pltpu.HBM(a.shape, a.dtype) for a in lands], _TOKEN),
        input_output_aliases={t: 2 + t for t in range(n)}, compiler_params=_CP_SPLIT)(*lands, recv_sems, *after)
    return outs[0], outs[1], list(outs[2:2 + n]), outs[-1]


def _ag_last(lands, precv, after, l):
    n = len(lands)
    after = list(after) if isinstance(after, (list, tuple)) else [after]

    def body(*refs):
        zones, precv = refs[:n], refs[n]
        qsend, qrecv, token = refs[n + 1 + len(after)], refs[n + 2 + len(after)], refs[-1]
        me, _ = _peer(0)
        sibling, _ = _peer(1)
        idx = _peer(6)[1]
        for t in range(n):
            _row_copy(zones[t], idx, qsend.at[t], precv.at[2 * n + t], me).wait_recv()
            _row_copy(zones[t], idx, qsend.at[t], qrecv.at[t], sibling).start()
        token[...] = jnp.zeros_like(token)

    outs = pl.pallas_call(
        body, name=f"ag_last_{l}", in_specs=[_HBM] * n + [_SEM] + [_ANY] * len(after),
        out_specs=(_SEM, _SEM, *[_HBM] * n, pl.BlockSpec(memory_space=pltpu.VMEM)),
        out_shape=(pltpu.SemaphoreType.DMA((n,)), pltpu.SemaphoreType.DMA((n,)),
                   *[pltpu.HBM(a.shape, a.dtype) for a in lands], _TOKEN),
        input_output_aliases={t: 2 + t for t in range(n)}, compiler_params=_CP_SPLIT)(*lands, precv, *after)
    return outs[0], outs[1], list(outs[2:2 + n]), outs[-1]


def _ag_wait(lands, send_sems, recv_sems, psend, precv, qsend, qrecv, after, l):
    n = len(lands)
    after = list(after) if isinstance(after, (list, tuple)) else [after]

    def body(*refs):
        zones = refs[:n]
        send_sems, recv_sems, psend, precv, qsend, qrecv = refs[n:n + 6]
        me, me_idx = _peer(0)
        for k in range(3):
            for t in range(n):
                _row_copy(zones[t], me_idx, send_sems.at[k * n + t], recv_sems.at[k * n + t], me).wait_send()
        for t in range(n):
            _row_copy(zones[t], _peer(1)[1], send_sems.at[t], recv_sems.at[t], me).wait_recv()
        fwd_idx, _ = _diag_route()
        for j, (mine, theirs) in enumerate(((_peer(4)[1], _peer(5)[1]), (_peer(2)[1], _peer(3)[1]))):
            for t in range(n):
                _row_copy(zones[t], mine, psend.at[j * n + t], precv.at[j * n + t], me).wait_send()
                _row_copy(zones[t], theirs, psend.at[j * n + t], precv.at[j * n + t], me).wait_recv()
        for t in range(n):
            _row_copy(zones[t], fwd_idx, psend.at[2 * n + t], precv.at[2 * n + t], me).wait_send()
            _row_copy(zones[t], _peer(6)[1], qsend.at[t], qrecv.at[t], me).wait_send()
            _row_copy(zones[t], _peer(7)[1], qsend.at[t], qrecv.at[t], me).wait_recv()

    outs = pl.pallas_call(
        body, name=f"ag_wait_{l}", in_specs=[_HBM] * n + [_SEM] * 6 + [_ANY] * len(after),
        out_specs=tuple([_HBM] * n), out_shape=tuple(pltpu.HBM(a.shape, a.dtype) for a in lands),
        input_output_aliases={t: t for t in range(n)}, compiler_params=_CP_SPLIT)(
            *lands, send_sems, recv_sems, psend, precv, qsend, qrecv, *after)
    return list(outs)


def _xchg_src(ref, slot_ref, idx):
    return _rows(ref, idx) if ref.shape[0] == NDEV * slot_ref.shape[1] else ref


def _rs_start(srcs, slots, after, tag):
    n = len(srcs)
    after = list(after) if isinstance(after, (list, tuple)) else [after]

    def body(*refs):
        src, slot = refs[:n], refs[n:2 * n]
        send_sems, recv_sems, token = refs[2 * n + len(after)], refs[2 * n + len(after) + 1], refs[-1]
        _, me_idx = _peer(0)
        for k in range(1, NDEV):
            dev, idx = _peer(k)
            for t in range(n):
                pltpu.make_async_remote_copy(
                    src_ref=_xchg_src(src[t], slot[t], idx), dst_ref=slot[t].at[me_idx],
                    send_sem=send_sems.at[(k - 1) * n + t], recv_sem=recv_sems.at[(k - 1) * n + t],
                    device_id=dev, device_id_type=_MESH).start()
        token[...] = jnp.zeros_like(token)

    outs = pl.pallas_call(
        body, name=f"rs_start_{tag}", in_specs=[_HBM] * (2 * n) + [_ANY] * len(after),
        out_specs=(_SEM, _SEM, *[_HBM] * (2 * n), pl.BlockSpec(memory_space=pltpu.VMEM)),
        out_shape=(pltpu.SemaphoreType.DMA(((NDEV - 1) * n,)), pltpu.SemaphoreType.DMA(((NDEV - 1) * n,)),
                   *[pltpu.HBM(a.shape, a.dtype) for a in list(srcs) + list(slots)], _TOKEN),
        input_output_aliases={t: 2 + t for t in range(2 * n)}, compiler_params=_CP_SPLIT)(
            *[_hbm(a) for a in list(srcs) + list(slots)], *after)
    return outs[0], outs[1], list(outs[2:2 + n]), list(outs[2 + n:2 + 2 * n]), outs[-1]


def _rs_wait(srcs, slots, send_sems, recv_sems, after, tag):
    n = len(srcs)
    after = list(after) if isinstance(after, (list, tuple)) else [after]

    def body(*refs):
        src, slot, send_sems, recv_sems = refs[:n], refs[n:2 * n], refs[2 * n], refs[2 * n + 1]
        me, _ = _peer(0)
        for k in range(1, NDEV):
            idx = _peer(k)[1]
            for t in range(n):
                cp = pltpu.make_async_remote_copy(
                    src_ref=_xchg_src(src[t], slot[t], idx), dst_ref=slot[t].at[idx],
                    send_sem=send_sems.at[(k - 1) * n + t], recv_sem=recv_sems.at[(k - 1) * n + t],
                    device_id=me, device_id_type=_MESH)
                cp.wait_send()
                cp.wait_recv()

    outs = pl.pallas_call(
        body, name=f"rs_wait_{tag}", in_specs=[_HBM] * (2 * n) + [_SEM, _SEM] + [_ANY] * len(after),
        out_specs=tuple([_HBM] * (2 * n)),
        out_shape=tuple(pltpu.HBM(a.shape, a.dtype) for a in list(srcs) + list(slots)),
        input_output_aliases={t: t for t in range(2 * n)}, compiler_params=_CP_SPLIT)(
            *srcs, *slots, send_sems, recv_sems, *after)
    return list(outs[:n]), list(outs[n:])


def _pair_start(full4s, bufs, after, tag):
    n = len(full4s)
    after = list(after) if isinstance(after, (list, tuple)) else [after]

    def body(*refs):
        full, buf = refs[:n], refs[n:2 * n]
        send_sems, recv_sems, token = refs[2 * n + len(after)], refs[2 * n + len(after) + 1], refs[-1]
        c = lax.axis_index("c")
        for t in range(n):
            pltpu.make_async_remote_copy(src_ref=full[t].at[:, 1 - c], dst_ref=buf[t], send_sem=send_sems.at[t],
                                         recv_sem=recv_sems.at[t], device_id=_peer(1)[0], device_id_type=_MESH).start()
        token[...] = jnp.zeros_like(token)

    outs = pl.pallas_call(
        body, name=f"pair_start_{tag}", in_specs=[_HBM] * (2 * n) + [_ANY] * len(after),
        out_specs=(_SEM, _SEM, *[_HBM] * (2 * n), pl.BlockSpec(memory_space=pltpu.VMEM)),
        out_shape=(pltpu.SemaphoreType.DMA((n,)), pltpu.SemaphoreType.DMA((n,)),
                   *[pltpu.HBM(a.shape, a.dtype) for a in list(full4s) + list(bufs)], _TOKEN),
        input_output_aliases={t: 2 + t for t in range(2 * n)}, compiler_params=_CP_SPLIT)(
            *[_hbm(a) for a in list(full4s) + list(bufs)], *after)
    return outs[0], outs[1], list(outs[2:2 + n]), list(outs[2 + n:2 + 2 * n]), outs[-1]


def _pair_wait(full4s, bufs, send_sems, recv_sems, after, tag):
    n = len(full4s)
    after = list(after) if isinstance(after, (list, tuple)) else [after]

    def body(*refs):
        full, buf, send_sems, recv_sems = refs[:n], refs[n:2 * n], refs[2 * n], refs[2 * n + 1]
        c = lax.axis_index("c")
        for t in range(n):
            cp = pltpu.make_async_remote_copy(src_ref=full[t].at[:, 1 - c], dst_ref=buf[t], send_sem=send_sems.at[t],
                                              recv_sem=recv_sems.at[t], device_id=_peer(0)[0], device_id_type=_MESH)
            cp.wait_send()
            cp.wait_recv()

    outs = pl.pallas_call(
        body, name=f"pair_wait_{tag}", in_specs=[_HBM] * (2 * n) + [_SEM, _SEM] + [_ANY] * len(after),
        out_specs=tuple([_HBM] * (2 * n)),
        out_shape=tuple(pltpu.HBM(a.shape, a.dtype) for a in list(full4s) + list(bufs)),
        input_output_aliases={t: t for t in range(2 * n)}, compiler_params=_CP_SPLIT)(
            *full4s, *bufs, send_sems, recv_sems, *after)
    return list(outs[:n]), list(outs[n:])


def _pair_sum(core, full4s, bufs):
    n = len(full4s)

    def body(core_ref, *refs):
        for t in range(n):
            refs[2 * n + t][...] = (refs[t][...].astype(f32) + refs[n + t][...].astype(f32)).astype(bf16)

    grid_spec = pltpu.PrefetchScalarGridSpec(
        num_scalar_prefetch=1, grid=(4,),
        in_specs=[pl.BlockSpec((None, None) + a.shape[2:], lambda j, core_ref: (j, core_ref[0], 0, 0)) for a in full4s]
        + [pl.BlockSpec((None,) + b.shape[1:], lambda j, core_ref: (j, 0, 0)) for b in bufs],
        out_specs=[pl.BlockSpec((None,) + b.shape[1:], lambda j, core_ref: (j, 0, 0)) for b in bufs])
    return pl.pallas_call(
        body, grid_spec=grid_spec, out_shape=[SDS(b.shape, bf16) for b in bufs],
        compiler_params=_CP, name="pair_sum")(core, *full4s, *bufs)


def _chip_start(sums, slots, after, tag):
    n = len(sums)
    after = list(after) if isinstance(after, (list, tuple)) else [after]

    def body(*refs):
        src, slot = refs[:n], refs[n:2 * n]
        send_sems, recv_sems, token = refs[2 * n + len(after)], refs[2 * n + len(after) + 1], refs[-1]
        my_chip = 2 * lax.axis_index("x") + lax.axis_index("y")
        for k, mask in enumerate((4, 2, 6)):
            dev, _ = _peer(mask)
            for t in range(n):
                pltpu.make_async_remote_copy(
                    src_ref=src[t].at[2 * dev[0] + dev[1]], dst_ref=slot[t].at[my_chip],
                    send_sem=send_sems.at[k * n + t], recv_sem=recv_sems.at[k * n + t],
                    device_id=dev, device_id_type=_MESH).start()
        token[...] = jnp.zeros_like(token)

    outs = pl.pallas_call(
        body, name=f"chip_start_{tag}", in_specs=[_HBM] * (2 * n) + [_ANY] * len(after),
        out_specs=(_SEM, _SEM, *[_HBM] * (2 * n), pl.BlockSpec(memory_space=pltpu.VMEM)),
        out_shape=(pltpu.SemaphoreType.DMA((3 * n,)), pltpu.SemaphoreType.DMA((3 * n,)),
                   *[pltpu.HBM(a.shape, a.dtype) for a in list(sums) + list(slots)], _TOKEN),
        input_output_aliases={t: 2 + t for t in range(2 * n)}, compiler_params=_CP_SPLIT)(
            *[_hbm(a) for a in list(sums) + list(slots)], *after)
    return outs[0], outs[1], list(outs[2:2 + n]), list(outs[2 + n:2 + 2 * n]), outs[-1]


def _chip_wait(sums, slots, send_sems, recv_sems, after, tag):
    n = len(sums)
    after = list(after) if isinstance(after, (list, tuple)) else [after]

    def body(*refs):
        src, slot, send_sems, recv_sems = refs[:n], refs[n:2 * n], refs[2 * n], refs[2 * n + 1]
        for k, mask in enumerate((4, 2, 6)):
            dev, _ = _peer(mask)
            chip = 2 * dev[0] + dev[1]
            for t in range(n):
                cp = pltpu.make_async_remote_copy(
                    src_ref=src[t].at[chip], dst_ref=slot[t].at[chip],
                    send_sem=send_sems.at[k * n + t], recv_sem=recv_sems.at[k * n + t],
                    device_id=_peer(0)[0], device_id_type=_MESH)
                cp.wait_send()
                cp.wait_recv()

    outs = pl.pallas_call(
        body, name=f"chip_wait_{tag}", in_specs=[_HBM] * (2 * n) + [_SEM, _SEM] + [_ANY] * len(after),
        out_specs=tuple([_HBM] * (2 * n)),
        out_shape=tuple(pltpu.HBM(a.shape, a.dtype) for a in list(sums) + list(slots)),
        input_output_aliases={t: t for t in range(2 * n)}, compiler_params=_CP_SPLIT)(
            *sums, *slots, send_sems, recv_sems, *after)
    return list(outs[:n]), list(outs[n:])


def _sum_slots(slots, rb):
    r = slots.shape[1]

    def body(s_ref, o_ref):
        acc = s_ref[0].astype(f32)
        for s in range(1, NDEV):
            acc = acc + s_ref[s].astype(f32)
        o_ref[...] = acc

    return pl.pallas_call(
        body, grid=(r // rb,),
        in_specs=[pl.BlockSpec((NDEV, rb, D), lambda i: (0, i, 0))],
        out_specs=pl.BlockSpec((rb, D), lambda i: (i, 0)),
        out_shape=SDS((r, D), f32), compiler_params=_CP, name="sum_slots")(slots)


def _adamw(w, g, m, v):
    shape = w.shape
    cols = shape[-1]
    rows = w.size // cols
    rb = rows
    for cand in (512, 256, 128, 64, 32, 16, 8):
        if rows % cand == 0 and rows > cand:
            rb = cand
            break

    def body(w_ref, g_ref, m_ref, v_ref, d_ref, mo_ref, vo_ref):
        d_ref[...], mo_ref[...], vo_ref[...] = _adamw_math(w_ref[...], g_ref[...], m_ref[...], v_ref[...])

    spec = pl.BlockSpec((rb, cols), lambda i: (i, 0))
    outs = pl.pallas_call(
        body, grid=(rows // rb,), in_specs=[spec] * 4, out_specs=[spec] * 3,
        out_shape=[SDS((rows, cols), f32)] * 3, compiler_params=_CP, name="adamw")(
            *(a.reshape(rows, cols) for a in (w, g, m, v)))
    return tuple(o.reshape(shape) for o in outs)


def _adamw_math(w, g, m, v):
    m = ADAM_B1 * m + (1.0 - ADAM_B1) * g
    v = ADAM_B2 * v + (1.0 - ADAM_B2) * (g * g)
    m_hat = m / (1.0 - ADAM_B1 ** ADAM_STEP)
    v_hat = v / (1.0 - ADAM_B2 ** ADAM_STEP)
    return -ADAM_LR * (m_hat / (jnp.sqrt(v_hat) + ADAM_EPS) + ADAM_WD * w), m, v


def _reduce_adamw(acc, me, full, slots, w, m, v, l):
    _, r, _ = w.shape
    ns = slots.shape[0]
    rb = r // 2 if r > 128 else r

    def body(me_ref, full_ref, slots_ref, w_ref, m_ref, v_ref, *refs):
        go_ref, d_ref, mo_ref, vo_ref = refs[-4:]
        own = full_ref[...].astype(f32)
        g = None
        for s in range(ns):
            part = jnp.where(me_ref[0] == s, own, slots_ref[s].astype(f32))
            g = part if g is None else g + part
        go_ref[...] = g
        d_ref[...], mo_ref[...], vo_ref[...] = _adamw_math(w_ref[...], g, m_ref[...], v_ref[...])

    steps = r // rb
    lay = pl.BlockSpec((None, rb, D), lambda i, me_ref: (l, i, 0))
    n_acc = 0 if acc is None else 4
    grid_spec = pltpu.PrefetchScalarGridSpec(
        num_scalar_prefetch=1, grid=(steps,),
        in_specs=[pl.BlockSpec((rb, D), lambda i, me_ref: (me_ref[0] * steps + i, 0)),
                  pl.BlockSpec((ns, rb, D), lambda i, me_ref: (0, i, 0)), lay, lay, lay] + [_ANY] * n_acc,
        out_specs=[lay] * 4)
    outs = pl.pallas_call(
        body, grid_spec=grid_spec, out_shape=[SDS(w.shape, f32)] * 4,
        input_output_aliases={6 + j: j for j in range(n_acc)},
        compiler_params=_CP, name="reduce_adamw")(me, full, slots, w, m, v, *(() if acc is None else acc))
    return tuple(outs)


_BIG = ("ffn1_w_gate", "ffn1_w_up", "ffn1_w_down", "w_in", "w_out", "ffn2_w_gate", "ffn2_w_up", "ffn2_w_down")
_TRANSPOSED = ("ffn1_w_gate", "ffn1_w_up", "w_in", "ffn2_w_gate", "ffn2_w_up")

def _block_diag(pool_w):
    out = jnp.zeros((L, PW, PW), pool_w.dtype)
    for gi in range(4):
        out = out.at[:, 64 * gi:64 * (gi + 1), 64 * gi:64 * (gi + 1)].set(pool_w[:, gi])
    return out


def kernel(x, positions, ffn1_norm, ffn1_w_gate, ffn1_w_up, ffn1_w_down, mix_norm, w_in, pool_w, pool_scale, w_out, ffn2_norm, ffn2_w_gate, ffn2_w_up, ffn2_w_down, final_norm, loss_target, m_ffn1_norm, m_ffn1_w_gate, m_ffn1_w_up, m_ffn1_w_down, m_mix_norm, m_w_in, m_pool_w, m_pool_scale, m_w_out, m_ffn2_norm, m_ffn2_w_gate, m_ffn2_w_up, m_ffn2_w_down, m_final_norm, v_ffn1_norm, v_ffn1_w_gate, v_ffn1_w_up, v_ffn1_w_down, v_mix_norm, v_w_in, v_pool_w, v_pool_scale, v_w_out, v_ffn2_norm, v_ffn2_w_gate, v_ffn2_w_up, v_ffn2_w_down, v_final_norm):
    weights = dict(ffn1_norm=ffn1_norm, ffn1_w_gate=ffn1_w_gate, ffn1_w_up=ffn1_w_up, ffn1_w_down=ffn1_w_down,
                   mix_norm=mix_norm, w_in=w_in, pool_w=pool_w, pool_scale=pool_scale, w_out=w_out,
                   ffn2_norm=ffn2_norm, ffn2_w_gate=ffn2_w_gate, ffn2_w_up=ffn2_w_up, ffn2_w_down=ffn2_w_down,
                   final_norm=final_norm)
    moms = dict(ffn1_norm=m_ffn1_norm, ffn1_w_gate=m_ffn1_w_gate, ffn1_w_up=m_ffn1_w_up, ffn1_w_down=m_ffn1_w_down,
                mix_norm=m_mix_norm, w_in=m_w_in, pool_w=m_pool_w, pool_scale=m_pool_scale, w_out=m_w_out,
                ffn2_norm=m_ffn2_norm, ffn2_w_gate=m_ffn2_w_gate, ffn2_w_up=m_ffn2_w_up, ffn2_w_down=m_ffn2_w_down,
                final_norm=m_final_norm)
    vels = dict(ffn1_norm=v_ffn1_norm, ffn1_w_gate=v_ffn1_w_gate, ffn1_w_up=v_ffn1_w_up, ffn1_w_down=v_ffn1_w_down,
                mix_norm=v_mix_norm, w_in=v_w_in, pool_w=v_pool_w, pool_scale=v_pool_scale, w_out=v_w_out,
                ffn2_norm=v_ffn2_norm, ffn2_w_gate=v_ffn2_w_gate, ffn2_w_up=v_ffn2_w_up, ffn2_w_down=v_ffn2_w_down,
                final_norm=v_final_norm)
    names = list(weights)

    me_idx = 4 * lax.axis_index("x") + 2 * lax.axis_index("y") + lax.axis_index("c")
    me_arr = me_idx.reshape(1).astype(jnp.int32)

    as_rows = lambda a, nm: jnp.swapaxes(a, 1, 2) if nm in _TRANSPOSED else a
    w_rows = {nm: as_rows(weights[nm], nm) for nm in _BIG}
    m_rows = {nm: as_rows(moms[nm], nm) for nm in _BIG}
    v_rows = {nm: as_rows(vels[nm], nm) for nm in _BIG}

    def landing_zones(l, which):
        return _place_own(me_arr, [w_rows[_BIG[t]] for t in which], l)

    g_ffn1 = [ffn1_norm[l].reshape(1, D) for l in range(L)]
    g_mix = [mix_norm[l].reshape(1, D) for l in range(L)]
    g_ffn2 = [ffn2_norm[l].reshape(1, D) for l in range(L)]
    wbd_all = _block_diag(pool_w).astype(bf16)
    wbd = [wbd_all[l] for l in range(L)]
    pscale = [pool_scale[l].reshape(1, PW) for l in range(L)]
    tabs = _rope_tables(positions)
    flat = lambda a: a.reshape(S, a.shape[-1])
    r4 = lambda a: a.reshape(4, S // 4, a.shape[-1])
    r16 = lambda a: a.reshape(16, S // 16, a.shape[-1])

    first, rest, whole = (0, 1, 2, 3), (4, 5, 6, 7), tuple(range(8))

    def ag_begin(l, which, after, zones=None):
        tag = f"{l}{'' if which == whole else 'h' if which == first else 'r'}"
        zones = landing_zones(l, which) if zones is None else zones
        send_sems, recv_sems, zones, token = _ag_start(zones, after, tag)
        return dict(tag=tag, zones=zones, s=send_sems, r=recv_sems), token

    def ag_second(ch, after):
        ch["ps"], ch["pr"], ch["zones"], token = _ag_pass(ch["zones"], ch["r"], after, ch["tag"])
        return token

    def ag_third(ch, after):
        ch["qs"], ch["qr"], ch["zones"], token = _ag_last(ch["zones"], ch["pr"], after, ch["tag"])
        return token

    def ag_end(ch, after):
        return _ag_wait(ch["zones"], ch["s"], ch["r"], ch["ps"], ch["pr"], ch["qs"], ch["qr"], after, ch["tag"])

    ch_head, _ = ag_begin(0, first, [])
    zones_rest, zones_next = landing_zones(0, rest), landing_zones(1, whole)
    fill = [z for zs in (zones_rest, zones_next, tabs, wbd) for z in zs]
    head = ag_end(ch_head, ag_third(ch_head, ag_second(ch_head, fill)))
    ch_rest, tok_rest = ag_begin(0, rest, head[0], zones_rest)
    chains = {}
    chains[1], tok_next = ag_begin(1, whole, head[0], zones_next)
    gathered = [None] * L
    xs = x.reshape(S, D)
    saved = []
    for l in range(L):
        first_after, second_after = (), ()
        if l == 0:
            gt1, ut1, dn1, wint = head
            first_after = (tok_rest, tok_next)
        else:
            gt1, ut1, dn1, wint, wout, gt2, ut2, dn2 = gathered[l]
        x0 = xs
        x1, gate1, up1 = _ffn_fwd(x0, g_ffn1[l], gt1, ut1, dn1, after=first_after)
        hmix, vp, q1, k1, v1, q4, k4, v4, q16, k16, v16 = _mix_in_fwd(x1, g_mix[l], wint, tabs)
        q4, k4, v4, q16, k16, v16 = map(flat, (q4, k4, v4, q16, k16, v16))
        ypool, diff = _pool_fwd(vp, wbd[l], pscale[l])
        after_attn = None
        if l == 0:
            after_attn = ag_second(ch_rest, [ypool, q16])
        o1, l1 = _attn_fwd(q1, k1, v1, S, after=after_attn)
        o4, l4 = _attn_fwd(q4, k4, v4, S // 4, after=after_attn)
        o16, l16 = _attn_fwd(q16, k16, v16, S // 16, after=after_attn)
        if l == 0:
            early_zones = {ll: landing_zones(ll, whole) for ll in range(2, L)}
            token = ag_third(ch_rest, [o1, o4, o16] + [z for zs in early_zones.values() for z in zs])
            wout, gt2, ut2, dn2 = ag_end(ch_rest, token)
            gathered[0] = list(head) + [wout, gt2, ut2, dn2]
        elif l + 1 < L:
            second_after = (ag_second(chains[l + 1], [o1, o4, o16]),)
        x2, mixed, o, lse1, lse4, lse16 = _mix_out_fwd(x1, ypool, o1, l1, r4(o4), r4(l4), r16(o16), r16(l16), wout)
        if l == 0:
            second_after = (ag_second(chains[1], x2),)
        x3, gate2, up2 = _ffn_fwd(x2, g_ffn2[l], gt2, ut2, dn2, after=second_after)
        if l + 1 < L:
            token = ag_third(chains[l + 1], x3)
            if l + 2 < L:
                chains[l + 2], token = ag_begin(l + 2, whole, token, early_zones[l + 2])
            gathered[l + 1] = ag_end(chains[l + 1], token)
        saved.append(dict(x0=x0, x1=x1, x2=x2, gate1=gate1, up1=up1, gate2=gate2, up2=up2, hmix=hmix, diff=diff,
                          qkv=((q1, k1, v1), (q4, k4, v4), (q16, k16, v16)), mixed=mixed, o=o,
                          lse=(lse1, flat(lse4), flat(lse16))))
        xs = x3

    dx, loss_part, d_final = _loss_head(xs, final_norm.reshape(1, D), loss_target.reshape(S, D))

    d_norm = {nm: [None] * L for nm in ("ffn1_norm", "mix_norm", "ffn2_norm")}
    d_poolw, d_pscale = [None] * L, [None] * L
    group_a = ("ffn2_w_gate", "ffn2_w_up", "ffn2_w_down", "w_out")
    group_b = ("ffn1_w_gate", "ffn1_w_up", "ffn1_w_down", "w_in")
    acc = {}

    def exchange(full, group, after, tag):
        srcs = [full[nm] for nm in group]
        slots = [lax.empty((NDEV, g.shape[0] // NDEV, D), bf16) for g in srcs]
        ssem, rsem, srcs, slots, token = _rs_start(srcs, slots, after, tag)
        return (srcs, slots, ssem, rsem, tag), token

    def update(l, group, flight, after):
        srcs, slots, ssem, rsem, tag = flight
        srcs, slots = _rs_wait(srcs, slots, ssem, rsem, after, tag)
        for nm, full_g, slots_g in zip(group, srcs, slots):
            acc[nm] = _reduce_adamw(acc.get(nm), me_arr, full_g, slots_g, w_rows[nm], m_rows[nm], v_rows[nm], l)
        return [acc[nm][0] for nm in group], slots

    core_arr = lax.axis_index("c").reshape(1).astype(jnp.int32)
    chip_arr = (2 * lax.axis_index("x") + lax.axis_index("y")).reshape(1).astype(jnp.int32)

    def exchange_cores(full, group, after, tag):
        full4s = [full[nm].reshape(4, 2, full[nm].shape[0] // NDEV, D) for nm in group]
        bufs = [lax.empty((4,) + a.shape[2:], bf16) for a in full4s]
        ssem, rsem, full4s, bufs, token = _pair_start(full4s, bufs, after, tag)
        return (full4s, bufs, ssem, rsem, tag), token

    def exchange_chips(flight, after):
        full4s, bufs, ssem, rsem, tag = flight
        full4s, bufs = _pair_wait(full4s, bufs, ssem, rsem, after, tag)
        sums = _pair_sum(core_arr, full4s, bufs)
        slots = [lax.empty(a.shape, bf16) for a in sums]
        ssem, rsem, sums, slots, token = _chip_start(sums, slots, bufs[0], tag)
        return (sums, slots, ssem, rsem, tag), token

    def update_chips(l, group, flight, after):
        sums, slots, ssem, rsem, tag = flight
        sums, slots = _chip_wait(sums, slots, ssem, rsem, after, tag)
        for nm, sums_g, slots_g in zip(group, sums, slots):
            own = sums_g.reshape(4 * sums_g.shape[1], D)
            acc[nm] = _reduce_adamw(acc.get(nm), chip_arr, own, slots_g, w_rows[nm], m_rows[nm], v_rows[nm], l)
        return [acc[nm][0] for nm in group]

    flights = {}
    token_b = None
    for l in reversed(range(L)):
        sv = saved[l]
        gt1, ut1, dn1, wint, wout, gt2, ut2, dn2 = gathered[l]
        full = {}
        dx, dgate, dup, h, dy, d_norm["ffn2_norm"][l] = _ffn_bwd_d(
            sv["x2"], g_ffn2[l], sv["gate2"], sv["up2"], dx, gt2, ut2, dn2, after=() if token_b is None else (token_b,))
        full["ffn2_w_gate"], full["ffn2_w_up"], full["ffn2_w_down"] = _ffn_bwd_w(h, dy, sv["gate2"], sv["up2"], dgate, dup)

        dxb, dyp, do1, do4, do16, dl1, dl4, dl16 = _mix_out_bwd(dx, sv["o"], wout)
        full["w_out"] = _wgrad(sv["mixed"], dxb)
        flights[l, "a"], token_a = (exchange_cores if l == 0 else exchange)(full, group_a, dxb, f"a{l}")
        dvp, d_poolw[l], d_pscale[l] = _pool_bwd(dyp, sv["diff"], wbd[l], pscale[l], after=(token_a,))
        dos, dls = (do1, flat(do4), flat(do16)), (dl1, flat(dl4), flat(dl16))
        dqkv = []
        for b, lc in enumerate((S, S // 4, S // 16)):
            qb, kb, vb = sv["qkv"][b]
            dqkv.append(_attn_bwd(qb, kb, vb, dos[b], sv["lse"][b], dls[b], lc))
        d4 = tuple(r4(a) for a in dqkv[1])
        d16 = tuple(r16(a) for a in dqkv[2])
        mix_after = ()
        if l == 0:
            flights[0, "a"], token_a = exchange_chips(flights[0, "a"], [dqkv[0][0], dqkv[1][0], dqkv[2][0]])
            mix_after = (token_a,)
        dx, dproj, d_norm["mix_norm"][l] = _mix_in_bwd(dx, sv["x1"], g_mix[l], wint, tabs, dvp, dqkv[0], d4, d16,
                                                       after=mix_after)
        full["w_in"] = _wgrad(dproj, sv["hmix"])

        dx, dgate, dup, h, dy, d_norm["ffn1_norm"][l] = _ffn_bwd_d(sv["x0"], g_ffn1[l], sv["gate1"], sv["up1"], dx, gt1, ut1, dn1)
        full["ffn1_w_gate"], full["ffn1_w_up"], full["ffn1_w_down"] = _ffn_bwd_w(h, dy, sv["gate1"], sv["up1"], dgate, dup)

        after = dx
        if l + 1 < L and l + 1 >= 2:
            after, _ = update(l + 1, group_a, flights.pop((l + 1, "a")), after)
        if l + 1 < L and l + 1 >= 3:
            after, _ = update(l + 1, group_b, flights.pop((l + 1, "b")), after)
        if l > 0:
            flights[l, "b"], token_b = exchange(full, group_b, after, f"b{l}")

    flights[0, "b"], token_b = exchange_cores(full, group_b, dx, "b0")
    pad8 = lambda a: jnp.pad(a, ((0, 8 - a.shape[0]), (0, 0)))
    misc = jnp.concatenate([d_final, jnp.concatenate(d_pscale, axis=1), loss_part], axis=0)
    small = jnp.concatenate(
        [pad8(jnp.concatenate(d_norm[nm], axis=0)) for nm in ("ffn1_norm", "mix_norm", "ffn2_norm")]
        + [pad8(misc), jnp.stack(d_poolw).reshape(L * 16, D)], axis=0)
    small_slots = lax.dynamic_update_slice(lax.empty((NDEV, SMALL_ROWS, D), f32), small[None], (me_idx, 0, 0))
    pack_sems = _rs_start([small], [small_slots], token_b, "pack")
    flights[0, "b"], token_b = exchange_chips(flights[0, "b"], pack_sems[-1])

    after = token_b
    for key in [(2, "b"), (1, "a"), (1, "b")]:
        after, _ = update(key[0], group_a if key[1] == "a" else group_b, flights.pop(key), after)
    _, pack_slots = _rs_wait(pack_sems[2], pack_sems[3], pack_sems[0], pack_sems[1], after, "pack")
    sm = _sum_slots(pack_slots[0], SMALL_ROWS)
    grads = {}
    grads["ffn1_norm"], grads["mix_norm"], grads["ffn2_norm"] = sm[0:L], sm[8:8 + L], sm[16:16 + L]
    grads["final_norm"] = sm[24]
    grads["pool_scale"] = sm[25].reshape(L, PW)
    grads["pool_w"] = sm[32:32 + L * 16].reshape(L, 4, 64, 64)
    loss = sm[26, 0]
    upd = {nm: _adamw(weights[nm], grads[nm], moms[nm], vels[nm]) for nm in names if nm not in _BIG}
    after = update_chips(0, group_a, flights.pop((0, "a")), [upd[nm][0] for nm in upd])
    update_chips(0, group_b, flights.pop((0, "b")), after)
    for nm in _BIG:
        grads[nm], upd[nm] = as_rows(acc[nm][0], nm), tuple(as_rows(a, nm) for a in acc[nm][1:])
    return (loss, dx.reshape(1, S, D), *[grads[nm] for nm in names], *[upd[nm][0] for nm in names],
            *[upd[nm][1] for nm in names], *[upd[nm][2] for nm in names])
```

```python
import jax
import jax.numpy as jnp
from jax import lax
from jax.experimental import pallas as pl
from jax.experimental.pallas import tpu as pltpu

f32 = jnp.float32
bf16 = jnp.bfloat16
SDS = jax.ShapeDtypeStruct

D = 1024
S = 2048
F = 2816
L = 4
PW = 256
AW = 768
PROJ = PW + 3 * AW
NDEV = 8
TM = 256
QB = 128
HALF = 64
NG = AW // 128
NORM_EPS = 1e-6
MASK_VALUE = -1e30
ROPE_THETA = 500000.0
ADAM_LR, ADAM_B1, ADAM_B2, ADAM_EPS, ADAM_WD, ADAM_STEP = 0.001, 0.9, 0.999, 1e-08, 0.01, 10
POOL_WINDOWS = (2, 4, 8, 16)
PAD = 8
SMALL_ROWS = 96
VMEM_LIMIT = 56 * 1024 * 1024

_CP = pltpu.CompilerParams(vmem_limit_bytes=VMEM_LIMIT)
_ANY = pl.BlockSpec(memory_space=pl.ANY)
_HBM = pl.BlockSpec(memory_space=pltpu.HBM)
_SEM = pl.BlockSpec(memory_space=pltpu.SEMAPHORE)
_MESH = pl.DeviceIdType.MESH
_CP_SPLIT = pltpu.CompilerParams(has_side_effects=pltpu.SideEffectType.DATAFLOW_SIDE_EFFECTING)


def _dot_nn(a, b):
    return lax.dot_general(a, b, (((1,), (0,)), ((), ())), preferred_element_type=f32)


def _dot_nt(a, b):
    return lax.dot_general(a, b, (((1,), (1,)), ((), ())), preferred_element_type=f32)


def _dot_tn(a, b):
    return lax.dot_general(a, b, (((0,), (0,)), ((), ())), preferred_element_type=f32)


def _rms(x, g):
    r = lax.rsqrt(jnp.mean(x * x, axis=-1, keepdims=True) + NORM_EPS)
    xh = x * r
    return r, xh, xh * g


def _rms_bwd(dh, r, xh, g):
    dxh = dh * g
    return r * (dxh - xh * jnp.mean(dxh * xh, axis=-1, keepdims=True))


def _tile(cols, rows=TM):
    return pl.BlockSpec((rows, cols), lambda i: (i, 0))


def _const(shape):
    return pl.BlockSpec(shape, lambda i: (0,) * len(shape))


def _layer(rows, cols):
    return pl.BlockSpec((rows, cols), lambda i: (0, 0), pipeline_mode=pl.Buffered(1))


def _p4(cols=AW):
    return pl.BlockSpec((4, TM // 4, cols), lambda i: (0, i, 0))


def _p16(cols=AW):
    return pl.BlockSpec((16, TM // 16, cols), lambda i: (0, i, 0))


def _cols(j):
    return slice(128 * j, 128 * (j + 1))


def _follow(body, n_in, after):
    k = len(after)
    return body if k == 0 else (lambda *refs: body(*refs[:n_in], *refs[n_in + k:]))


def _ffn_fwd(x, g, gt, ut, dn, after=()):
    def body(x_ref, g_ref, gt_ref, ut_ref, dn_ref, xo_ref, gate_ref, up_ref):
        x = x_ref[...]
        _, _, hn = _rms(x, g_ref[...])
        h = hn.astype(bf16)
        gate = _dot_nt(h, gt_ref[...])
        up = _dot_nt(h, ut_ref[...])
        gate_ref[...] = gate.astype(bf16)
        up_ref[...] = up.astype(bf16)
        a = (gate * jax.nn.sigmoid(gate) * up).astype(bf16)
        xo_ref[...] = x + 0.5 * _dot_nn(a, dn_ref[...])

    rows = 2 * TM
    return pl.pallas_call(
        _follow(body, 5, after), grid=(S // rows,),
        in_specs=[_tile(D, rows), _layer(1, D), _layer(F, D), _layer(F, D), _layer(F, D)] + [_ANY] * len(after),
        out_specs=[_tile(D, rows), _tile(F, rows), _tile(F, rows)],
        out_shape=[SDS((S, D), f32), SDS((S, F), bf16), SDS((S, F), bf16)],
        compiler_params=_CP, name="ffn_fwd")(x, g, gt, ut, dn, *after)


def _ffn_bwd_d(x, g, gate, up, dxo, gt, ut, dn, after=()):
    def body(x_ref, g_ref, gate_ref, up_ref, dxo_ref, gt_ref, ut_ref, dn_ref,
             dx_ref, dgate_ref, dup_ref, h_ref, dy_ref, dg_ref):
        x = x_ref[...]
        g = g_ref[...]
        r, xh, hn = _rms(x, g)
        h_ref[...] = hn.astype(bf16)
        dxo = dxo_ref[...]
        dy = (0.5 * dxo).astype(bf16)
        dy_ref[...] = dy
        da = _dot_nt(dy, dn_ref[...])
        gate = gate_ref[...].astype(f32)
        up = up_ref[...].astype(f32)
        sg = jax.nn.sigmoid(gate)
        dgate = (da * up * (sg * (1.0 + gate * (1.0 - sg)))).astype(bf16)
        dup = (da * (gate * sg)).astype(bf16)
        dgate_ref[...] = dgate
        dup_ref[...] = dup
        dh = _dot_nn(dgate, gt_ref[...]) + _dot_nn(dup, ut_ref[...])

        @pl.when(pl.program_id(0) == 0)
        def _():
            dg_ref[...] = jnp.zeros_like(dg_ref)

        dg_ref[...] += jnp.sum(dh * xh, axis=0, keepdims=True)
        dx_ref[...] = dxo + _rms_bwd(dh, r, xh, g)

    return pl.pallas_call(
        _follow(body, 8, after), grid=(S // TM,),
        in_specs=[_tile(D), _layer(1, D), _tile(F), _tile(F), _tile(D),
                  _layer(F, D), _layer(F, D), _layer(F, D)] + [_ANY] * len(after),
        out_specs=[_tile(D), _tile(F), _tile(F), _tile(D), _tile(D), _const((1, D))],
        out_shape=[SDS((S, D), f32), SDS((S, F), bf16), SDS((S, F), bf16), SDS((S, D), bf16),
                   SDS((S, D), bf16), SDS((1, D), f32)],
        compiler_params=_CP, name="ffn_bwd_d")(x, g, gate, up, dxo, gt, ut, dn, *after)


def _ffn_bwd_w(h, dy, gate, up, dgate, dup):
    fc = 256

    def body(h_ref, dy_ref, gate_ref, up_ref, dgate_ref, dup_ref, dgt_ref, dut_ref, ddn_ref):
        gate = gate_ref[...].astype(f32)
        a = (gate * jax.nn.sigmoid(gate) * up_ref[...].astype(f32)).astype(bf16)
        ddn_ref[...] = _dot_tn(a, dy_ref[...]).astype(bf16)
        h = h_ref[...]
        dgt_ref[...] = _dot_tn(dgate_ref[...], h).astype(bf16)
        dut_ref[...] = _dot_tn(dup_ref[...], h).astype(bf16)

    col = pl.BlockSpec((S, fc), lambda j: (0, j))
    row = pl.BlockSpec((fc, D), lambda j: (j, 0))
    full = pl.BlockSpec((S, D), lambda j: (0, 0))
    return pl.pallas_call(
        body, grid=(F // fc,),
        in_specs=[full, full, col, col, col, col],
        out_specs=[row, row, row],
        out_shape=[SDS((F, D), bf16)] * 3,
        compiler_params=_CP, name="ffn_bwd_w")(h, dy, gate, up, dgate, dup)


def _wgrad(a, b):
    m, n = a.shape[1], b.shape[1]
    mc = 2 * TM

    def body(a_ref, b_ref, o_ref):
        o_ref[...] = _dot_tn(a_ref[...], b_ref[...]).astype(bf16)

    return pl.pallas_call(
        body, grid=(m // mc,),
        in_specs=[pl.BlockSpec((S, mc), lambda j: (0, j)), pl.BlockSpec((S, n), lambda j: (0, 0))],
        out_specs=pl.BlockSpec((mc, n), lambda j: (j, 0)),
        out_shape=SDS((m, n), bf16),
        compiler_params=_CP, name="wgrad")(a, b)


def _rope(t, c, sn, sp):
    return t * c + pltpu.roll(t, 120, 1) * sn + pltpu.roll(t, 8, 1) * sp


def _rope_bwd(d, c, sn, sp):
    return d * c + pltpu.roll(d * sn, 8, 1) + pltpu.roll(d * sp, 120, 1)


def _rope_tables(positions):
    inv_freq = ROPE_THETA ** (-jnp.arange(0, 16, 2, dtype=f32) / 16)
    ang = positions.reshape(S, 1).astype(f32) * inv_freq
    cos, sin = jnp.cos(ang), jnp.sin(ang)
    one = jnp.ones((S, 48), f32)
    zero8 = jnp.zeros((S, 8), f32)
    zero48 = jnp.zeros((S, 48), f32)
    c = jnp.concatenate([cos, cos, one], axis=1)
    sn = jnp.concatenate([-sin, zero8, zero48], axis=1)
    sp = jnp.concatenate([zero8, sin, zero48], axis=1)
    return tuple(jnp.concatenate([t, t], axis=1) for t in (c, sn, sp))


def _dilation_perm(n, back=False):
    per = TM // n
    i = lax.broadcasted_iota(jnp.int32, (TM, TM), 1 if back else 0)
    j = lax.broadcasted_iota(jnp.int32, (TM, TM), 0 if back else 1)
    return jnp.where(j == n * (i % per) + i // per, 1.0, 0.0).astype(bf16)


def _mix_in_fwd(x, g, wint, tabs):
    def body(x_ref, g_ref, w_ref, c_ref, sn_ref, sp_ref,
             h_ref, vp_ref, q1, k1, v1, q4, k4, v4, q16, k16, v16):
        _, _, hn = _rms(x_ref[...], g_ref[...])
        h = hn.astype(bf16)
        h_ref[...] = h
        proj = _dot_nt(h, w_ref[...])
        vp_ref[...] = proj[:, :PW]
        c, sn, sp = c_ref[...], sn_ref[...], sp_ref[...]
        perm4, perm16 = _dilation_perm(4), _dilation_perm(16)
        for kind, (o1, o4, o16) in enumerate(((q1, q4, q16), (k1, k4, k16), (v1, v4, v16))):
            for j in range(NG):
                t = proj[:, PW + kind * AW + 128 * j: PW + kind * AW + 128 * (j + 1)]
                if kind == 0:
                    t = _rope(t, c, sn, sp) * 0.125
                elif kind == 1:
                    t = _rope(t, c, sn, sp)
                o1[:, _cols(j)] = t.astype(bf16)
            nat = o1[...]
            o4[...] = _dot_nn(perm4, nat).astype(bf16).reshape(4, TM // 4, AW)
            o16[...] = _dot_nn(perm16, nat).astype(bf16).reshape(16, TM // 16, AW)

    nat, d4, d16 = SDS((S, AW), bf16), SDS((4, S // 4, AW), bf16), SDS((16, S // 16, AW), bf16)
    return pl.pallas_call(
        body, grid=(S // TM,),
        in_specs=[_tile(D), _layer(1, D), _layer(PROJ, D), _tile(128), _tile(128), _tile(128)],
        out_specs=[_tile(D), _tile(PW)] + [_tile(AW)] * 3 + [_p4()] * 3 + [_p16()] * 3,
        out_shape=[SDS((S, D), bf16), SDS((S, PW), f32)] + [nat] * 3 + [d4] * 3 + [d16] * 3,
        compiler_params=_CP, name="mix_in_fwd")(x, g, wint, *tabs)


def _mix_in_bwd(dxo, x, g, wint, tabs, dvp, d1, d4, d16, after=()):
    def body(dxo_ref, x_ref, g_ref, w_ref, c_ref, sn_ref, sp_ref, dvp_ref,
             dq1, dk1, dv1, dq4, dk4, dv4, dq16, dk16, dv16,
             dx_ref, dproj_ref, dg_ref):
        c, sn, sp = c_ref[...], sn_ref[...], sp_ref[...]
        dproj_ref[:, :PW] = dvp_ref[...].astype(bf16)
        back4, back16 = _dilation_perm(4, True), _dilation_perm(16, True)
        for kind, (a1, a4, a16) in enumerate(((dq1, dq4, dq16), (dk1, dk4, dk16), (dv1, dv4, dv16))):
            n4 = _dot_nn(back4, a4[...].reshape(TM, AW))
            n16 = _dot_nn(back16, a16[...].reshape(TM, AW))
            for j in range(NG):
                t = a1[:, _cols(j)].astype(f32) + n4[:, _cols(j)] + n16[:, _cols(j)]
                if kind == 0:
                    t = _rope_bwd(t * 0.125, c, sn, sp)
                elif kind == 1:
                    t = _rope_bwd(t, c, sn, sp)
                dproj_ref[:, PW + kind * AW + 128 * j: PW + kind * AW + 128 * (j + 1)] = t.astype(bf16)
        g = g_ref[...]
        r_, xh, _ = _rms(x_ref[...], g)
        dh = _dot_nn(dproj_ref[...], w_ref[...])

        @pl.when(pl.program_id(0) == 0)
        def _():
            dg_ref[...] = jnp.zeros_like(dg_ref)

        dg_ref[...] += jnp.sum(dh * xh, axis=0, keepdims=True)
        dx_ref[...] = dxo_ref[...] + _rms_bwd(dh, r_, xh, g)

    return pl.pallas_call(
        _follow(body, 17, after), grid=(S // TM,),
        in_specs=[_tile(D), _tile(D), _layer(1, D), _layer(PROJ, D), _tile(128), _tile(128), _tile(128),
                  _tile(PW)] + [_tile(AW)] * 3 + [_p4()] * 3 + [_p16()] * 3 + [_ANY] * len(after),
        out_specs=[_tile(D), _tile(PROJ), _const((1, D))],
        out_shape=[SDS((S, D), f32), SDS((S, PROJ), bf16), SDS((1, D), f32)],
        compiler_params=_CP, name="mix_in_bwd")(dxo, x, g, wint, *tabs, dvp, *d1, *d4, *d16, *after)


def _pool_sums(pad_ref, base, rows, adjoint):
    lane_group = lax.broadcasted_iota(jnp.int32, (rows, PW), 1) // 64
    sign = -1 if adjoint else 1

    def sh(o):
        return pad_ref[pl.ds(PAD + base + sign * o, rows), :]

    out = None
    acc = None
    lo, hi = 0, 0
    for gi, w in enumerate(POOL_WINDOWS):
        for o in list(range(-(w // 2), lo)) + list(range(hi, w - w // 2)):
            acc = sh(o) if acc is None else acc + sh(o)
        lo, hi = -(w // 2), w - w // 2
        out = acc if out is None else jnp.where(lane_group >= gi, acc, out)
    return out


def _pool_counts(base, rows):
    pos = base + lax.broadcasted_iota(jnp.int32, (rows, PW), 0)
    lane_group = lax.broadcasted_iota(jnp.int32, (rows, PW), 1) // 64
    cnt = None
    for gi, w in enumerate(POOL_WINDOWS):
        lo = jnp.maximum(pos - w // 2, 0)
        hi = jnp.minimum(pos + w - 1 - w // 2, S - 1)
        c = (hi - lo + 1).astype(f32)
        cnt = c if cnt is None else jnp.where(lane_group >= gi, c, cnt)
    return cnt


def _pool_fwd(vp, wbd, scale):
    ch = 256

    def body(vp_ref, w_ref, sc_ref, y_ref, diff_ref, pad):
        pad[pl.ds(0, PAD), :] = jnp.zeros((PAD, PW), f32)
        pad[pl.ds(PAD + S, PAD), :] = jnp.zeros((PAD, PW), f32)
        pad[pl.ds(PAD, S), :] = vp_ref[...]
        for b in range(S // ch):
            base = b * ch
            pooled = _pool_sums(pad, base, ch, False) / _pool_counts(base, ch)
            diff = (pooled - vp_ref[pl.ds(base, ch), :]).astype(bf16)
            diff_ref[pl.ds(base, ch), :] = diff
            y_ref[pl.ds(base, ch), :] = _dot_nn(diff, w_ref[...]) * sc_ref[...]

    whole = lambda shape: pl.BlockSpec(shape, lambda i: (0,) * len(shape))
    return pl.pallas_call(
        body, grid=(1,),
        in_specs=[whole((S, PW)), whole((PW, PW)), whole((1, PW))],
        out_specs=[whole((S, PW)), whole((S, PW))],
        out_shape=[SDS((S, PW), f32), SDS((S, PW), bf16)],
        scratch_shapes=[pltpu.VMEM((S + 2 * PAD, PW), f32)],
        compiler_params=_CP, name="pool_fwd")(vp, wbd, scale)


def _pool_bwd(dy, diff, wbd, scale, after=()):
    ch = 256

    def body(dy_ref, diff_ref, w_ref, sc_ref, dvp_ref, dw_ref, dsc_ref, pad):
        pad[pl.ds(0, PAD), :] = jnp.zeros((PAD, PW), f32)
        pad[pl.ds(PAD + S, PAD), :] = jnp.zeros((PAD, PW), f32)
        dw = jnp.zeros((PW, PW), f32)
        dsc = jnp.zeros((1, PW), f32)
        for b in range(S // ch):
            base = b * ch
            dy = dy_ref[pl.ds(base, ch), :]
            diff = diff_ref[pl.ds(base, ch), :]
            dsc = dsc + jnp.sum(dy * _dot_nn(diff, w_ref[...]), axis=0, keepdims=True)
            dz = (dy * sc_ref[...]).astype(bf16)
            dw = dw + _dot_tn(diff, dz)
            ddiff = _dot_nt(dz, w_ref[...])
            dvp_ref[pl.ds(base, ch), :] = -ddiff
            pad[pl.ds(PAD + base, ch), :] = ddiff / _pool_counts(base, ch)
        for gi in range(4):
            dw_ref[gi] = dw[64 * gi:64 * (gi + 1), 64 * gi:64 * (gi + 1)]
        dsc_ref[...] = dsc
        for b in range(S // ch):
            base = b * ch
            dvp_ref[pl.ds(base, ch), :] += _pool_sums(pad, base, ch, True)

    whole = lambda shape: pl.BlockSpec(shape, lambda i: (0,) * len(shape))
    return pl.pallas_call(
        _follow(body, 4, after), grid=(1,),
        in_specs=[whole((S, PW)), whole((S, PW)), whole((PW, PW)), whole((1, PW))] + [_ANY] * len(after),
        out_specs=[whole((S, PW)), whole((4, 64, 64)), whole((1, PW))],
        out_shape=[SDS((S, PW), f32), SDS((4, 64, 64), f32), SDS((1, PW), f32)],
        scratch_shapes=[pltpu.VMEM((S + 2 * PAD, PW), f32)],
        compiler_params=_CP, name="pool_bwd")(dy, diff, wbd, scale, *after)


def _attn_blocks(lc):
    bpc = lc // QB
    kw = min(2 * QB, lc)
    blocks = []
    for b in range(S // QB):
        t0 = (b % bpc) * QB
        ks_in = min(max(t0 - HALF, 0), lc - kw)
        blocks.append((b * QB, (b // bpc) * lc + ks_in, t0 - ks_in))
    return kw, blocks


def _attn_bias(bias_ref, kw, shifts):
    r = lax.broadcasted_iota(jnp.int32, (2 * QB, kw), 0) % QB
    c = lax.broadcasted_iota(jnp.int32, (2 * QB, kw), 1)
    for i, shift in enumerate(shifts):
        bias_ref[i] = jnp.where(jnp.abs(r + shift - c) <= HALF, 0.0, MASK_VALUE).astype(f32)


def _head_put(stats, pair, v0, v1, lane):
    return jnp.where(lane == 2 * pair, v0, jnp.where(lane == 2 * pair + 1, v1, stats))


def _head_cols(stats, pair, lane):
    c0 = jnp.sum(jnp.where(lane == 2 * pair, stats, 0.0), axis=-1, keepdims=True)
    c1 = jnp.sum(jnp.where(lane == 2 * pair + 1, stats, 0.0), axis=-1, keepdims=True)
    return jnp.concatenate([c0, c1], axis=0)


def _head_spread(stats, pair, head0):
    return jnp.where(head0, stats[:, 2 * pair:2 * pair + 1], stats[:, 2 * pair + 1:2 * pair + 2])


def _stack_heads(blk, head0):
    zero = jnp.zeros_like(blk)
    return jnp.concatenate([jnp.where(head0, blk, zero), jnp.where(head0, zero, blk)], axis=0)


def _attn_fwd(q, k, v, lc, after=None):
    kw, blocks = _attn_blocks(lc)
    shifts = sorted({b[2] for b in blocks})

    def body(q_ref, k_ref, v_ref, *refs):
        o_ref, lse_ref, bias_ref = refs[-3:]
        lane = lax.broadcasted_iota(jnp.int32, (QB, 128), 1)
        head0 = lane < 64
        pair = pl.program_id(0)
        _attn_bias(bias_ref, kw, shifts)

        @pl.when(pair == 0)
        def _():
            lse_ref[...] = jnp.zeros_like(lse_ref)

        for row0, kstart, shift in blocks:
            q2 = _stack_heads(q_ref[pl.ds(row0, QB), :], head0)
            kb = k_ref[pl.ds(kstart, kw), :]
            vb = v_ref[pl.ds(kstart, kw), :]
            s = _dot_nt(q2, kb) + bias_ref[shifts.index(shift)]
            m = jnp.max(s, axis=-1, keepdims=True)
            p = jnp.exp(s - m)
            den = jnp.sum(p, axis=-1, keepdims=True)
            o2 = _dot_nn(p.astype(bf16), vb) / den
            lse2 = m + jnp.log(den)
            o_ref[pl.ds(row0, QB), :] = jnp.where(head0, o2[:QB], o2[QB:]).astype(bf16)
            lse_ref[pl.ds(row0, QB), :] = _head_put(lse_ref[pl.ds(row0, QB), :], pair, lse2[:QB], lse2[QB:], lane)

    col = pl.BlockSpec((S, 128), lambda p: (0, p))
    extra = () if after is None else (after,)
    return pl.pallas_call(
        body, grid=(NG,), in_specs=[col, col, col] + [_ANY] * len(extra),
        out_specs=[col, pl.BlockSpec((S, 128), lambda p: (0, 0))],
        out_shape=[SDS((S, AW), bf16), SDS((S, 128), f32)],
        scratch_shapes=[pltpu.VMEM((len(shifts), 2 * QB, kw), f32)],
        compiler_params=_CP, name=f"attn_fwd_{lc}")(q, k, v, *extra)


def _attn_bwd(q, k, v, do, lse, delta, lc):
    kw, blocks = _attn_blocks(lc)
    shifts = sorted({b[2] for b in blocks})

    def body(q_ref, k_ref, v_ref, do_ref, lse_ref, dl_ref, dq_ref, dk_out, dv_out, bias_ref, dk_ref, dv_ref):
        lane = lax.broadcasted_iota(jnp.int32, (QB, 128), 1)
        head0 = lane < 64
        pair = pl.program_id(0)
        _attn_bias(bias_ref, kw, shifts)
        dk_ref[...] = jnp.zeros_like(dk_ref)
        dv_ref[...] = jnp.zeros_like(dv_ref)
        for row0, kstart, shift in blocks:
            q2 = _stack_heads(q_ref[pl.ds(row0, QB), :], head0)
            do2 = _stack_heads(do_ref[pl.ds(row0, QB), :], head0)
            lse2 = _head_cols(lse_ref[pl.ds(row0, QB), :], pair, lane)
            dl2 = _head_cols(dl_ref[pl.ds(row0, QB), :], pair, lane)
            kb = k_ref[pl.ds(kstart, kw), :]
            vb = v_ref[pl.ds(kstart, kw), :]
            p = jnp.exp(_dot_nt(q2, kb) + bias_ref[shifts.index(shift)] - lse2)
            ds = (p * (_dot_nt(do2, vb) - dl2)).astype(bf16)
            dq2 = _dot_nn(ds, kb)
            dq_ref[pl.ds(row0, QB), :] = jnp.where(head0, dq2[:QB], dq2[QB:]).astype(bf16)
            dk_ref[pl.ds(kstart, kw), :] += _dot_tn(ds, q2)
            dv_ref[pl.ds(kstart, kw), :] += _dot_tn(p.astype(bf16), do2)
        dk_out[...] = dk_ref[...].astype(bf16)
        dv_out[...] = dv_ref[...].astype(bf16)

    col = pl.BlockSpec((S, 128), lambda p: (0, p))
    stats = pl.BlockSpec((S, 128), lambda p: (0, 0))
    return pl.pallas_call(
        body, grid=(NG,), in_specs=[col] * 4 + [stats] * 2, out_specs=[col] * 3,
        out_shape=[SDS((S, AW), bf16)] * 3,
        scratch_shapes=[pltpu.VMEM((len(shifts), 2 * QB, kw), f32), pltpu.VMEM((S, 128), f32),
                        pltpu.VMEM((S, 128), f32)],
        compiler_params=_CP, name=f"attn_bwd_{lc}")(q, k, v, do, lse, delta)


def _mix_out_fwd(x, ypool, o1, l1, o4, l4, o16, l16, wout):
    def body(x_ref, yp_ref, o1_ref, l1_ref, o4_ref, l4_ref, o16_ref, l16_ref, w_ref,
             xo_ref, mixed_ref, o_ref, lse1_ref, lse4_ref, lse16_ref, sl4, sl16, sl):
        head0 = lax.broadcasted_iota(jnp.int32, (TM, 128), 1) < 64
        for r in range(4):
            sl4[pl.ds(r, TM // 4, stride=4), :] = l4_ref[r]
        for r in range(16):
            sl16[pl.ds(r, TM // 16, stride=16), :] = l16_ref[r]
        n4 = _dot_nn(_dilation_perm(4, True), o4_ref[...].reshape(TM, AW))
        n16 = _dot_nn(_dilation_perm(16, True), o16_ref[...].reshape(TM, AW))
        a, b, c = l1_ref[...], sl4[...], sl16[...]
        m = jnp.maximum(jnp.maximum(a, b), c)
        wa, wb, wc = jnp.exp(a - m), jnp.exp(b - m), jnp.exp(c - m)
        den = wa + wb + wc
        wa, wb, wc = wa / den, wb / den, wc / den
        lse = m + jnp.log(den)
        lse1_ref[...] = lse
        sl[...] = lse
        mixed_ref[:, :PW] = yp_ref[...].astype(bf16)
        for j in range(NG):
            y = (_head_spread(wa, j, head0) * o1_ref[:, _cols(j)].astype(f32)
                 + _head_spread(wb, j, head0) * n4[:, _cols(j)] + _head_spread(wc, j, head0) * n16[:, _cols(j)])
            o_ref[:, _cols(j)] = y
            mixed_ref[:, PW + 128 * j: PW + 128 * (j + 1)] = y.astype(bf16)
        for r in range(4):
            lse4_ref[r] = sl[pl.ds(r, TM // 4, stride=4), :]
        for r in range(16):
            lse16_ref[r] = sl[pl.ds(r, TM // 16, stride=16), :]
        xo_ref[...] = x_ref[...] + _dot_nn(mixed_ref[...], w_ref[...])

    return pl.pallas_call(
        body, grid=(S // TM,),
        in_specs=[_tile(D), _tile(PW), _tile(AW), _tile(128), _p4(), _p4(128), _p16(), _p16(128), _layer(D, D)],
        out_specs=[_tile(D), _tile(D), _tile(AW), _tile(128), _p4(128), _p16(128)],
        out_shape=[SDS((S, D), f32), SDS((S, D), bf16), SDS((S, AW), f32), SDS((S, 128), f32),
                   SDS((4, S // 4, 128), f32), SDS((16, S // 16, 128), f32)],
        scratch_shapes=[pltpu.VMEM((TM, 128), f32)] * 3,
        compiler_params=_CP, name="mix_out_fwd")(x, ypool, o1, l1, o4, l4, o16, l16, wout)


def _mix_out_bwd(dxo, o, wout):
    def body(dxo_ref, o_ref, w_ref, dxb_ref, dyp_ref, do1, do4, do16, dl1, dl4, dl16, sdl):
        dxb = dxo_ref[...].astype(bf16)
        dxb_ref[...] = dxb
        dm = _dot_nt(dxb, w_ref[...])
        dyp_ref[...] = dm[:, :PW]
        lane = lax.broadcasted_iota(jnp.int32, (TM, 128), 1)
        head0 = lane < 64
        dl = jnp.zeros((TM, 128), f32)
        for j in range(NG):
            d = dm[:, PW + 128 * j: PW + 128 * (j + 1)]
            prod = d * o_ref[:, _cols(j)]
            dl = _head_put(dl, j, jnp.sum(jnp.where(head0, prod, 0.0), axis=-1, keepdims=True),
                           jnp.sum(jnp.where(head0, 0.0, prod), axis=-1, keepdims=True), lane)
            do1[:, _cols(j)] = d.astype(bf16)
        dl1[...] = dl
        sdl[...] = dl
        for r in range(4):
            dl4[r] = sdl[pl.ds(r, TM // 4, stride=4), :]
        for r in range(16):
            dl16[r] = sdl[pl.ds(r, TM // 16, stride=16), :]
        nat = do1[...]
        do4[...] = _dot_nn(_dilation_perm(4), nat).astype(bf16).reshape(4, TM // 4, AW)
        do16[...] = _dot_nn(_dilation_perm(16), nat).astype(bf16).reshape(16, TM // 16, AW)

    return pl.pallas_call(
        body, grid=(S // TM,),
        in_specs=[_tile(D), _tile(AW), _layer(D, D)],
        out_specs=[_tile(D), _tile(PW), _tile(AW), _p4(), _p16(), _tile(128), _p4(128), _p16(128)],
        out_shape=[SDS((S, D), bf16), SDS((S, PW), f32),
                   SDS((S, AW), bf16), SDS((4, S // 4, AW), bf16), SDS((16, S // 16, AW), bf16),
                   SDS((S, 128), f32), SDS((4, S // 4, 128), f32), SDS((16, S // 16, 128), f32)],
        scratch_shapes=[pltpu.VMEM((TM, 128), f32)],
        compiler_params=_CP, name="mix_out_bwd")(dxo, o, wout)


def _loss_head(x, g, target):
    def body(x_ref, g_ref, t_ref, dx_ref, loss_ref, dg_ref):
        g = g_ref[...]
        r, xh, y = _rms(x_ref[...], g)
        err = y - t_ref[...]
        dy = err * (1.0 / D)

        @pl.when(pl.program_id(0) == 0)
        def _():
            loss_ref[...] = jnp.zeros_like(loss_ref)
            dg_ref[...] = jnp.zeros_like(dg_ref)

        loss_ref[...] += jnp.broadcast_to(0.5 * jnp.sum(jnp.mean(err * err, axis=-1, keepdims=True)), (1, D))
        dg_ref[...] += jnp.sum(dy * xh, axis=0, keepdims=True)
        dx_ref[...] = _rms_bwd(dy, r, xh, g)

    return pl.pallas_call(
        body, grid=(S // TM,),
        in_specs=[_tile(D), _const((1, D)), _tile(D)],
        out_specs=[_tile(D), _const((1, D)), _const((1, D))],
        out_shape=[SDS((S, D), f32), SDS((1, D), f32), SDS((1, D), f32)],
        compiler_params=_CP, name="loss_head")(x, g, target)


def _peer(k):
    x, y, c = lax.axis_index("x"), lax.axis_index("y"), lax.axis_index("c")
    px = 1 - x if k & 4 else x
    py = 1 - y if k & 2 else y
    pc = 1 - c if k & 1 else c
    return (px, py, pc), 4 * px + 2 * py + pc


def _diag_route():
    x, y, c = lax.axis_index("x"), lax.axis_index("y"), lax.axis_index("c")
    idx_x, idx_y = _peer(4)[1], _peer(2)[1]
    return idx_x + c * (idx_y - idx_x), (x + c * (1 - 2 * x), (1 - y) + c * (2 * y - 1), c)


def _hbm(a):
    return pltpu.with_memory_space_constraint(a, pltpu.HBM)


def _rows(ref, idx):
    r = ref.shape[0] // NDEV
    return ref.at[pl.ds(idx * r, r), :]


def _row_copy(ref, idx, send_sem, recv_sem, to):
    return pltpu.make_async_remote_copy(src_ref=_rows(ref, idx), dst_ref=_rows(ref, idx), send_sem=send_sem,
                                        recv_sem=recv_sem, device_id=to, device_id_type=_MESH)


def _place_own(me, shards, l):
    n = len(shards)

    def body(me_ref, *refs):
        for t in range(n):
            refs[n + t][...] = refs[t][...].astype(bf16)

    grid_spec = pltpu.PrefetchScalarGridSpec(
        num_scalar_prefetch=1, grid=(1,),
        in_specs=[pl.BlockSpec((None, s.shape[1], D), lambda i, me_ref: (l, 0, 0)) for s in shards],
        out_specs=[pl.BlockSpec((s.shape[1], D), lambda i, me_ref: (me_ref[0], 0)) for s in shards])
    return pl.pallas_call(
        body, grid_spec=grid_spec, out_shape=[SDS((NDEV * s.shape[1], D), bf16) for s in shards],
        compiler_params=_CP, name="place_own")(me, *shards)


_TOKEN = SDS((8, 128), f32)
def _ag_start(lands, after, l):
    n = len(lands)
    after = list(after) if isinstance(after, (list, tuple)) else [after]

    def body(*refs):
        zones, send_sems, recv_sems, token = refs[:n], refs[n + len(after)], refs[n + len(after) + 1], refs[-1]
        _, me_idx = _peer(0)
        for k, mask in enumerate((1, 4, 2)):
            for t in range(n):
                _row_copy(zones[t], me_idx, send_sems.at[k * n + t], recv_sems.at[k * n + t], _peer(mask)[0]).start()
        token[...] = jnp.zeros_like(token)

    outs = pl.pallas_call(
        body, name=f"ag_start_{l}", in_specs=[_HBM] * n + [_ANY] * len(after),
        out_specs=(_SEM, _SEM, *[_HBM] * n, pl.BlockSpec(memory_space=pltpu.VMEM)),
        out_shape=(pltpu.SemaphoreType.DMA((3 * n,)), pltpu.SemaphoreType.DMA((3 * n,)),
                   *[pltpu.HBM(a.shape, a.dtype) for a in lands], _TOKEN),
        input_output_aliases={t: 2 + t for t in range(n)}, compiler_params=_CP_SPLIT)(
            *[_hbm(a) for a in lands], *after)
    return outs[0], outs[1], list(outs[2:2 + n]), outs[-1]


def _ag_pass(lands, recv_sems, after, l):
    n = len(lands)
    after = list(after) if isinstance(after, (list, tuple)) else [after]

    def body(*refs):
        zones, recv_sems = refs[:n], refs[n]
        psend, precv, token = refs[n + 1 + len(after)], refs[n + 2 + len(after)], refs[-1]
        me, _ = _peer(0)
        sibling, _ = _peer(1)
        for j, mask in enumerate((4, 2)):
            idx = _peer(mask)[1]
            for t in range(n):
                _row_copy(zones[t], idx, psend.at[j * n + t], recv_sems.at[(1 + j) * n + t], me).wait_recv()
                _row_copy(zones[t], idx, psend.at[j * n + t], precv.at[j * n + t], sibling).start()
        fwd_idx, fwd_dev = _diag_route()
        for t in range(n):
            _row_copy(zones[t], fwd_idx, psend.at[2 * n + t], precv.at[2 * n + t], fwd_dev).start()
        token[...] = jnp.zeros_like(token)

    outs = pl.pallas_call(
        body, name=f"ag_pass_{l}", in_specs=[_HBM] * n + [_SEM] + [_ANY] * len(after),
        out_specs=(_SEM, _SEM, *[_HBM] * n, pl.BlockSpec(memory_space=pltpu.VMEM)),
        out_shape=(pltpu.SemaphoreType.DMA((3 * n,)), pltpu.SemaphoreType.DMA((3 * n,)),
                   *[pltpu.HBM(a.shape, a.dtype) for a in lands], _TOKEN),
        input_output_aliases={t: 2 + t for t in range(n)}, compiler_params=_CP_SPLIT)(*lands, recv_sems, *after)
    return outs[0], outs[1], list(outs[2:2 + n]), outs[-1]


def _ag_last(lands, precv, after, l):
    n = len(lands)
    after = list(after) if isinstance(after, (list, tuple)) else [after]

    def body(*refs):
        zones, precv = refs[:n], refs[n]
        qsend, qrecv, token = refs[n + 1 + len(after)], refs[n + 2 + len(after)], refs[-1]
        me, _ = _peer(0)
        sibling, _ = _peer(1)
        idx = _peer(6)[1]
        for t in range(n):
            _row_copy(zones[t], idx, qsend.at[t], precv.at[2 * n + t], me).wait_recv()
            _row_copy(zones[t], idx, qsend.at[t], qrecv.at[t], sibling).start()
        token[...] = jnp.zeros_like(token)

    outs = pl.pallas_call(
        body, name=f"ag_last_{l}", in_specs=[_HBM] * n + [_SEM] + [_ANY] * len(after),
        out_specs=(_SEM, _SEM, *[_HBM] * n, pl.BlockSpec(memory_space=pltpu.VMEM)),
        out_shape=(pltpu.SemaphoreType.DMA((n,)), pltpu.SemaphoreType.DMA((n,)),
                   *[pltpu.HBM(a.shape, a.dtype) for a in lands], _TOKEN),
        input_output_aliases={t: 2 + t for t in range(n)}, compiler_params=_CP_SPLIT)(*lands, precv, *after)
    return outs[0], outs[1], list(outs[2:2 + n]), outs[-1]


def _ag_wait(lands, send_sems, recv_sems, psend, precv, qsend, qrecv, after, l):
    n = len(lands)
    after = list(after) if isinstance(after, (list, tuple)) else [after]

    def body(*refs):
        zones = refs[:n]
        send_sems, recv_sems, psend, precv, qsend, qrecv = refs[n:n + 6]
        me, me_idx = _peer(0)
        for k in range(3):
            for t in range(n):
                _row_copy(zones[t], me_idx, send_sems.at[k * n + t], recv_sems.at[k * n + t], me).wait_send()
        for t in range(n):
            _row_copy(zones[t], _peer(1)[1], send_sems.at[t], recv_sems.at[t], me).wait_recv()
        fwd_idx, _ = _diag_route()
        for j, (mine, theirs) in enumerate(((_peer(4)[1], _peer(5)[1]), (_peer(2)[1], _peer(3)[1]))):
            for t in range(n):
                _row_copy(zones[t], mine, psend.at[j * n + t], precv.at[j * n + t], me).wait_send()
                _row_copy(zones[t], theirs, psend.at[j * n + t], precv.at[j * n + t], me).wait_recv()
        for t in range(n):
            _row_copy(zones[t], fwd_idx, psend.at[2 * n + t], precv.at[2 * n + t], me).wait_send()
            _row_copy(zones[t], _peer(6)[1], qsend.at[t], qrecv.at[t], me).wait_send()
            _row_copy(zones[t], _peer(7)[1], qsend.at[t], qrecv.at[t], me).wait_recv()

    outs = pl.pallas_call(
        body, name=f"ag_wait_{l}", in_specs=[_HBM] * n + [_SEM] * 6 + [_ANY] * len(after),
        out_specs=tuple([_HBM] * n), out_shape=tuple(pltpu.HBM(a.shape, a.dtype) for a in lands),
        input_output_aliases={t: t for t in range(n)}, compiler_params=_CP_SPLIT)(
            *lands, send_sems, recv_sems, psend, precv, qsend, qrecv, *after)
    return list(outs)


def _xchg_src(ref, slot_ref, idx):
    return _rows(ref, idx) if ref.shape[0] == NDEV * slot_ref.shape[1] else ref


def _rs_start(srcs, slots, after, tag):
    n = len(srcs)
    after = list(after) if isinstance(after, (list, tuple)) else [after]

    def body(*refs):
        src, slot = refs[:n], refs[n:2 * n]
        send_sems, recv_sems, token = refs[2 * n + len(after)], refs[2 * n + len(after) + 1], refs[-1]
        _, me_idx = _peer(0)
        for k in range(1, NDEV):
            dev, idx = _peer(k)
            for t in range(n):
                pltpu.make_async_remote_copy(
                    src_ref=_xchg_src(src[t], slot[t], idx), dst_ref=slot[t].at[me_idx],
                    send_sem=send_sems.at[(k - 1) * n + t], recv_sem=recv_sems.at[(k - 1) * n + t],
                    device_id=dev, device_id_type=_MESH).start()
        token[...] = jnp.zeros_like(token)

    outs = pl.pallas_call(
        body, name=f"rs_start_{tag}", in_specs=[_HBM] * (2 * n) + [_ANY] * len(after),
        out_specs=(_SEM, _SEM, *[_HBM] * (2 * n), pl.BlockSpec(memory_space=pltpu.VMEM)),
        out_shape=(pltpu.SemaphoreType.DMA(((NDEV - 1) * n,)), pltpu.SemaphoreType.DMA(((NDEV - 1) * n,)),
                   *[pltpu.HBM(a.shape, a.dtype) for a in list(srcs) + list(slots)], _TOKEN),
        input_output_aliases={t: 2 + t for t in range(2 * n)}, compiler_params=_CP_SPLIT)(
            *[_hbm(a) for a in list(srcs) + list(slots)], *after)
    return outs[0], outs[1], list(outs[2:2 + n]), list(outs[2 + n:2 + 2 * n]), outs[-1]


def _rs_wait(srcs, slots, send_sems, recv_sems, after, tag):
    n = len(srcs)
    after = list(after) if isinstance(after, (list, tuple)) else [after]

    def body(*refs):
        src, slot, send_sems, recv_sems = refs[:n], refs[n:2 * n], refs[2 * n], refs[2 * n + 1]
        me, _ = _peer(0)
        for k in range(1, NDEV):
            idx = _peer(k)[1]
            for t in range(n):
                cp = pltpu.make_async_remote_copy(
                    src_ref=_xchg_src(src[t], slot[t], idx), dst_ref=slot[t].at[idx],
                    send_sem=send_sems.at[(k - 1) * n + t], recv_sem=recv_sems.at[(k - 1) * n + t],
                    device_id=me, device_id_type=_MESH)
                cp.wait_send()
                cp.wait_recv()

    outs = pl.pallas_call(
        body, name=f"rs_wait_{tag}", in_specs=[_HBM] * (2 * n) + [_SEM, _SEM] + [_ANY] * len(after),
        out_specs=tuple([_HBM] * (2 * n)),
        out_shape=tuple(pltpu.HBM(a.shape, a.dtype) for a in list(srcs) + list(slots)),
        input_output_aliases={t: t for t in range(2 * n)}, compiler_params=_CP_SPLIT)(
            *srcs, *slots, send_sems, recv_sems, *after)
    return list(outs[:n]), list(outs[n:])


def _pair_start(full4s, bufs, after, tag):
    n = len(full4s)
    after = list(after) if isinstance(after, (list, tuple)) else [after]

    def body(*refs):
        full, buf = refs[:n], refs[n:2 * n]
        send_sems, recv_sems, token = refs[2 * n + len(after)], refs[2 * n + len(after) + 1], refs[-1]
        c = lax.axis_index("c")
        for t in range(n):
            pltpu.make_async_remote_copy(src_ref=full[t].at[:, 1 - c], dst_ref=buf[t], send_sem=send_sems.at[t],
                                         recv_sem=recv_sems.at[t], device_id=_peer(1)[0], device_id_type=_MESH).start()
        token[...] = jnp.zeros_like(token)

    outs = pl.pallas_call(
        body, name=f"pair_start_{tag}", in_specs=[_HBM] * (2 * n) + [_ANY] * len(after),
        out_specs=(_SEM, _SEM, *[_HBM] * (2 * n), pl.BlockSpec(memory_space=pltpu.VMEM)),
        out_shape=(pltpu.SemaphoreType.DMA((n,)), pltpu.SemaphoreType.DMA((n,)),
                   *[pltpu.HBM(a.shape, a.dtype) for a in list(full4s) + list(bufs)], _TOKEN),
        input_output_aliases={t: 2 + t for t in range(2 * n)}, compiler_params=_CP_SPLIT)(
            *[_hbm(a) for a in list(full4s) + list(bufs)], *after)
    return outs[0], outs[1], list(outs[2:2 + n]), list(outs[2 + n:2 + 2 * n]), outs[-1]


def _pair_wait(full4s, bufs, send_sems, recv_sems, after, tag):
    n = len(full4s)
    after = list(after) if isinstance(after, (list, tuple)) else [after]

    def body(*refs):
        full, buf, send_sems, recv_sems = refs[:n], refs[n:2 * n], refs[2 * n], refs[2 * n + 1]
        c = lax.axis_index("c")
        for t in range(n):
            cp = pltpu.make_async_remote_copy(src_ref=full[t].at[:, 1 - c], dst_ref=buf[t], send_sem=send_sems.at[t],
                                              recv_sem=recv_sems.at[t], device_id=_peer(0)[0], device_id_type=_MESH)
            cp.wait_send()
            cp.wait_recv()

    outs = pl.pallas_call(
        body, name=f"pair_wait_{tag}", in_specs=[_HBM] * (2 * n) + [_SEM, _SEM] + [_ANY] * len(after),
        out_specs=tuple([_HBM] * (2 * n)),
        out_shape=tuple(pltpu.HBM(a.shape, a.dtype) for a in list(full4s) + list(bufs)),
        input_output_aliases={t: t for t in range(2 * n)}, compiler_params=_CP_SPLIT)(
            *full4s, *bufs, send_sems, recv_sems, *after)
    return list(outs[:n]), list(outs[n:])


def _pair_sum(core, full4s, bufs):
    n = len(full4s)

    def body(core_ref, *refs):
        for t in range(n):
            refs[2 * n + t][...] = (refs[t][...].astype(f32) + refs[n + t][...].astype(f32)).astype(bf16)

    grid_spec = pltpu.PrefetchScalarGridSpec(
        num_scalar_prefetch=1, grid=(4,),
        in_specs=[pl.BlockSpec((None, None) + a.shape[2:], lambda j, core_ref: (j, core_ref[0], 0, 0)) for a in full4s]
        + [pl.BlockSpec((None,) + b.shape[1:], lambda j, core_ref: (j, 0, 0)) for b in bufs],
        out_specs=[pl.BlockSpec((None,) + b.shape[1:], lambda j, core_ref: (j, 0, 0)) for b in bufs])
    return pl.pallas_call(
        body, grid_spec=grid_spec, out_shape=[SDS(b.shape, bf16) for b in bufs],
        compiler_params=_CP, name="pair_sum")(core, *full4s, *bufs)


def _chip_start(sums, slots, after, tag):
    n = len(sums)
    after = list(after) if isinstance(after, (list, tuple)) else [after]

    def body(*refs):
        src, slot = refs[:n], refs[n:2 * n]
        send_sems, recv_sems, token = refs[2 * n + len(after)], refs[2 * n + len(after) + 1], refs[-1]
        my_chip = 2 * lax.axis_index("x") + lax.axis_index("y")
        for k, mask in enumerate((4, 2, 6)):
            dev, _ = _peer(mask)
            for t in range(n):
                pltpu.make_async_remote_copy(
                    src_ref=src[t].at[2 * dev[0] + dev[1]], dst_ref=slot[t].at[my_chip],
                    send_sem=send_sems.at[k * n + t], recv_sem=recv_sems.at[k * n + t],
                    device_id=dev, device_id_type=_MESH).start()
        token[...] = jnp.zeros_like(token)

    outs = pl.pallas_call(
        body, name=f"chip_start_{tag}", in_specs=[_HBM] * (2 * n) + [_ANY] * len(after),
        out_specs=(_SEM, _SEM, *[_HBM] * (2 * n), pl.BlockSpec(memory_space=pltpu.VMEM)),
        out_shape=(pltpu.SemaphoreType.DMA((3 * n,)), pltpu.SemaphoreType.DMA((3 * n,)),
                   *[pltpu.HBM(a.shape, a.dtype) for a in list(sums) + list(slots)], _TOKEN),
        input_output_aliases={t: 2 + t for t in range(2 * n)}, compiler_params=_CP_SPLIT)(
            *[_hbm(a) for a in list(sums) + list(slots)], *after)
    return outs[0], outs[1], list(outs[2:2 + n]), list(outs[2 + n:2 + 2 * n]), outs[-1]


def _chip_wait(sums, slots, send_sems, recv_sems, after, tag):
    n = len(sums)
    after = list(after) if isinstance(after, (list, tuple)) else [after]

    def body(*refs):
        src, slot, send_sems, recv_sems = refs[:n], refs[n:2 * n], refs[2 * n], refs[2 * n + 1]
        for k, mask in enumerate((4, 2, 6)):
            dev, _ = _peer(mask)
            chip = 2 * dev[0] + dev[1]
            for t in range(n):
                cp = pltpu.make_async_remote_copy(
                    src_ref=src[t].at[chip], dst_ref=slot[t].at[chip],
                    send_sem=send_sems.at[k * n + t], recv_sem=recv_sems.at[k * n + t],
                    device_id=_peer(0)[0], device_id_type=_MESH)
                cp.wait_send()
                cp.wait_recv()

    outs = pl.pallas_call(
        body, name=f"chip_wait_{tag}", in_specs=[_HBM] * (2 * n) + [_SEM, _SEM] + [_ANY] * len(after),
        out_specs=tuple([_HBM] * (2 * n)),
        out_shape=tuple(pltpu.HBM(a.shape, a.dtype) for a in list(sums) + list(slots)),
        input_output_aliases={t: t for t in range(2 * n)}, compiler_params=_CP_SPLIT)(
            *sums, *slots, send_sems, recv_sems, *after)
    return list(outs[:n]), list(outs[n:])


def _sum_slots(slots, rb):
    r = slots.shape[1]

    def body(s_ref, o_ref):
        acc = s_ref[0].astype(f32)
        for s in range(1, NDEV):
            acc = acc + s_ref[s].astype(f32)
        o_ref[...] = acc

    return pl.pallas_call(
        body, grid=(r // rb,),
        in_specs=[pl.BlockSpec((NDEV, rb, D), lambda i: (0, i, 0))],
        out_specs=pl.BlockSpec((rb, D), lambda i: (i, 0)),
        out_shape=SDS((r, D), f32), compiler_params=_CP, name="sum_slots")(slots)


def _adamw(w, g, m, v):
    shape = w.shape
    cols = shape[-1]
    rows = w.size // cols
    rb = rows
    for cand in (512, 256, 128, 64, 32, 16, 8):
        if rows % cand == 0 and rows > cand:
            rb = cand
            break

    def body(w_ref, g_ref, m_ref, v_ref, d_ref, mo_ref, vo_ref):
        d_ref[...], mo_ref[...], vo_ref[...] = _adamw_math(w_ref[...], g_ref[...], m_ref[...], v_ref[...])

    spec = pl.BlockSpec((rb, cols), lambda i: (i, 0))
    outs = pl.pallas_call(
        body, grid=(rows // rb,), in_specs=[spec] * 4, out_specs=[spec] * 3,
        out_shape=[SDS((rows, cols), f32)] * 3, compiler_params=_CP, name="adamw")(
            *(a.reshape(rows, cols) for a in (w, g, m, v)))
    return tuple(o.reshape(shape) for o in outs)


def _adamw_math(w, g, m, v):
    m = ADAM_B1 * m + (1.0 - ADAM_B1) * g
    v = ADAM_B2 * v + (1.0 - ADAM_B2) * (g * g)
    m_hat = m / (1.0 - ADAM_B1 ** ADAM_STEP)
    v_hat = v / (1.0 - ADAM_B2 ** ADAM_STEP)
    return -ADAM_LR * (m_hat / (jnp.sqrt(v_hat) + ADAM_EPS) + ADAM_WD * w), m, v


def _reduce_adamw(acc, me, full, slots, w, m, v, l):
    _, r, _ = w.shape
    ns = slots.shape[0]
    rb = r // 2 if r > 128 else r

    def body(me_ref, full_ref, slots_ref, w_ref, m_ref, v_ref, *refs):
        go_ref, d_ref, mo_ref, vo_ref = refs[-4:]
        own = full_ref[...].astype(f32)
        g = None
        for s in range(ns):
            part = jnp.where(me_ref[0] == s, own, slots_ref[s].astype(f32))
            g = part if g is None else g + part
        go_ref[...] = g
        d_ref[...], mo_ref[...], vo_ref[...] = _adamw_math(w_ref[...], g, m_ref[...], v_ref[...])

    steps = r // rb
    lay = pl.BlockSpec((None, rb, D), lambda i, me_ref: (l, i, 0))
    n_acc = 0 if acc is None else 4
    grid_spec = pltpu.PrefetchScalarGridSpec(
        num_scalar_prefetch=1, grid=(steps,),
        in_specs=[pl.BlockSpec((rb, D), lambda i, me_ref: (me_ref[0] * steps + i, 0)),
                  pl.BlockSpec((ns, rb, D), lambda i, me_ref: (0, i, 0)), lay, lay, lay] + [_ANY] * n_acc,
        out_specs=[lay] * 4)
    outs = pl.pallas_call(
        body, grid_spec=grid_spec, out_shape=[SDS(w.shape, f32)] * 4,
        input_output_aliases={6 + j: j for j in range(n_acc)},
        compiler_params=_CP, name="reduce_adamw")(me, full, slots, w, m, v, *(() if acc is None else acc))
    return tuple(outs)


_BIG = ("ffn1_w_gate", "ffn1_w_up", "ffn1_w_down", "w_in", "w_out", "ffn2_w_gate", "ffn2_w_up", "ffn2_w_down")
_TRANSPOSED = ("ffn1_w_gate", "ffn1_w_up", "w_in", "ffn2_w_gate", "ffn2_w_up")

def _block_diag(pool_w):
    out = jnp.zeros((L, PW, PW), pool_w.dtype)
    for gi in range(4):
        out = out.at[:, 64 * gi:64 * (gi + 1), 64 * gi:64 * (gi + 1)].set(pool_w[:, gi])
    return out


def kernel(x, positions, ffn1_norm, ffn1_w_gate, ffn1_w_up, ffn1_w_down, mix_norm, w_in, pool_w, pool_scale, w_out, ffn2_norm, ffn2_w_gate, ffn2_w_up, ffn2_w_down, final_norm, loss_target, m_ffn1_norm, m_ffn1_w_gate, m_ffn1_w_up, m_ffn1_w_down, m_mix_norm, m_w_in, m_pool_w, m_pool_scale, m_w_out, m_ffn2_norm, m_ffn2_w_gate, m_ffn2_w_up, m_ffn2_w_down, m_final_norm, v_ffn1_norm, v_ffn1_w_gate, v_ffn1_w_up, v_ffn1_w_down, v_mix_norm, v_w_in, v_pool_w, v_pool_scale, v_w_out, v_ffn2_norm, v_ffn2_w_gate, v_ffn2_w_up, v_ffn2_w_down, v_final_norm):
    weights = dict(ffn1_norm=ffn1_norm, ffn1_w_gate=ffn1_w_gate, ffn1_w_up=ffn1_w_up, ffn1_w_down=ffn1_w_down,
                   mix_norm=mix_norm, w_in=w_in, pool_w=pool_w, pool_scale=pool_scale, w_out=w_out,
                   ffn2_norm=ffn2_norm, ffn2_w_gate=ffn2_w_gate, ffn2_w_up=ffn2_w_up, ffn2_w_down=ffn2_w_down,
                   final_norm=final_norm)
    moms = dict(ffn1_norm=m_ffn1_norm, ffn1_w_gate=m_ffn1_w_gate, ffn1_w_up=m_ffn1_w_up, ffn1_w_down=m_ffn1_w_down,
                mix_norm=m_mix_norm, w_in=m_w_in, pool_w=m_pool_w, pool_scale=m_pool_scale, w_out=m_w_out,
                ffn2_norm=m_ffn2_norm, ffn2_w_gate=m_ffn2_w_gate, ffn2_w_up=m_ffn2_w_up, ffn2_w_down=m_ffn2_w_down,
                final_norm=m_final_norm)
    vels = dict(ffn1_norm=v_ffn1_norm, ffn1_w_gate=v_ffn1_w_gate, ffn1_w_up=v_ffn1_w_up, ffn1_w_down=v_ffn1_w_down,
                mix_norm=v_mix_norm, w_in=v_w_in, pool_w=v_pool_w, pool_scale=v_pool_scale, w_out=v_w_out,
                ffn2_norm=v_ffn2_norm, ffn2_w_gate=v_ffn2_w_gate, ffn2_w_up=v_ffn2_w_up, ffn2_w_down=v_ffn2_w_down,
                final_norm=v_final_norm)
    names = list(weights)

    me_idx = 4 * lax.axis_index("x") + 2 * lax.axis_index("y") + lax.axis_index("c")
    me_arr = me_idx.reshape(1).astype(jnp.int32)

    as_rows = lambda a, nm: jnp.swapaxes(a, 1, 2) if nm in _TRANSPOSED else a
    w_rows = {nm: as_rows(weights[nm], nm) for nm in _BIG}
    m_rows = {nm: as_rows(moms[nm], nm) for nm in _BIG}
    v_rows = {nm: as_rows(vels[nm], nm) for nm in _BIG}

    def landing_zones(l, which):
        return _place_own(me_arr, [w_rows[_BIG[t]] for t in which], l)

    g_ffn1 = [ffn1_norm[l].reshape(1, D) for l in range(L)]
    g_mix = [mix_norm[l].reshape(1, D) for l in range(L)]
    g_ffn2 = [ffn2_norm[l].reshape(1, D) for l in range(L)]
    wbd_all = _block_diag(pool_w).astype(bf16)
    wbd = [wbd_all[l] for l in range(L)]
    pscale = [pool_scale[l].reshape(1, PW) for l in range(L)]
    tabs = _rope_tables(positions)
    flat = lambda a: a.reshape(S, a.shape[-1])
    r4 = lambda a: a.reshape(4, S // 4, a.shape[-1])
    r16 = lambda a: a.reshape(16, S // 16, a.shape[-1])

    first, rest, whole = (0, 1, 2, 3, 4), (5, 6, 7), tuple(range(8))

    def ag_begin(l, which, after, zones=None):
        tag = f"{l}{'' if which == whole else 'h' if which == first else 'r'}"
        zones = landing_zones(l, which) if zones is None else zones
        send_sems, recv_sems, zones, token = _ag_start(zones, after, tag)
        return dict(tag=tag, zones=zones, s=send_sems, r=recv_sems), token

    def ag_second(ch, after):
        ch["ps"], ch["pr"], ch["zones"], token = _ag_pass(ch["zones"], ch["r"], after, ch["tag"])
        return token

    def ag_third(ch, after):
        ch["qs"], ch["qr"], ch["zones"], token = _ag_last(ch["zones"], ch["pr"], after, ch["tag"])
        return token

    def ag_end(ch, after):
        return _ag_wait(ch["zones"], ch["s"], ch["r"], ch["ps"], ch["pr"], ch["qs"], ch["qr"], after, ch["tag"])

    ch_head, _ = ag_begin(0, first, [])
    zones_rest, zones_next = landing_zones(0, rest), landing_zones(1, whole)
    fill = [z for zs in (zones_rest, zones_next, tabs, wbd) for z in zs]
    head = ag_end(ch_head, ag_third(ch_head, ag_second(ch_head, fill)))
    ch_rest, tok_rest = ag_begin(0, rest, head[0], zones_rest)
    chains = {}
    chains[1], tok_next = ag_begin(1, whole, head[0], zones_next)
    gathered = [None] * L
    xs = x.reshape(S, D)
    saved = []
    for l in range(L):
        first_after, second_after = (), ()
        if l == 0:
            gt1, ut1, dn1, wint, wout = head
            first_after = (tok_rest, tok_next)
        else:
            gt1, ut1, dn1, wint, wout, gt2, ut2, dn2 = gathered[l]
        x0 = xs
        x1, gate1, up1 = _ffn_fwd(x0, g_ffn1[l], gt1, ut1, dn1, after=first_after)
        hmix, vp, q1, k1, v1, q4, k4, v4, q16, k16, v16 = _mix_in_fwd(x1, g_mix[l], wint, tabs)
        q4, k4, v4, q16, k16, v16 = map(flat, (q4, k4, v4, q16, k16, v16))
        ypool, diff = _pool_fwd(vp, wbd[l], pscale[l])
        after_attn = None
        if l == 0:
            after_attn = ag_second(ch_rest, [ypool, q16])
        o1, l1 = _attn_fwd(q1, k1, v1, S, after=after_attn)
        o4, l4 = _attn_fwd(q4, k4, v4, S // 4, after=after_attn)
        o16, l16 = _attn_fwd(q16, k16, v16, S // 16, after=after_attn)
        if 0 < l < L - 1:
            second_after = (ag_second(chains[l + 1], [o1, o4, o16]),)
        x2, mixed, o, lse1, lse4, lse16 = _mix_out_fwd(x1, ypool, o1, l1, r4(o4), r4(l4), r16(o16), r16(l16), wout)
        if l == 0:
            early_zones = {ll: landing_zones(ll, whole) for ll in range(2, L)}
            token = ag_third(ch_rest, [x2] + [z for zs in early_zones.values() for z in zs])
            gt2, ut2, dn2 = ag_end(ch_rest, token)
            gathered[0] = list(head) + [gt2, ut2, dn2]
            second_after = (ag_second(chains[1], gt2),)
        x3, gate2, up2 = _ffn_fwd(x2, g_ffn2[l], gt2, ut2, dn2, after=second_after)
        if l + 1 < L:
            token = ag_third(chains[l + 1], x3)
            if l + 2 < L:
                chains[l + 2], token = ag_begin(l + 2, whole, token, early_zones[l + 2])
            gathered[l + 1] = ag_end(chains[l + 1], token)
        saved.append(dict(x0=x0, x1=x1, x2=x2, gate1=gate1, up1=up1, gate2=gate2, up2=up2, hmix=hmix, diff=diff,
                          qkv=((q1, k1, v1), (q4, k4, v4), (q16, k16, v16)), mixed=mixed, o=o,
                          lse=(lse1, flat(lse4), flat(lse16))))
        xs = x3

    dx, loss_part, d_final = _loss_head(xs, final_norm.reshape(1, D), loss_target.reshape(S, D))

    d_norm = {nm: [None] * L for nm in ("ffn1_norm", "mix_norm", "ffn2_norm")}
    d_poolw, d_pscale = [None] * L, [None] * L
    group_a = ("ffn2_w_gate", "ffn2_w_up", "ffn2_w_down", "w_out")
    group_b = ("ffn1_w_gate", "ffn1_w_up", "ffn1_w_down", "w_in")
    acc = {}

    def exchange(full, group, after, tag):
        srcs = [full[nm] for nm in group]
        slots = [lax.empty((NDEV, g.shape[0] // NDEV, D), bf16) for g in srcs]
        ssem, rsem, srcs, slots, token = _rs_start(srcs, slots, after, tag)
        return (srcs, slots, ssem, rsem, tag), token

    def update(l, group, flight, after):
        srcs, slots, ssem, rsem, tag = flight
        srcs, slots = _rs_wait(srcs, slots, ssem, rsem, after, tag)
        for nm, full_g, slots_g in zip(group, srcs, slots):
            acc[nm] = _reduce_adamw(acc.get(nm), me_arr, full_g, slots_g, w_rows[nm], m_rows[nm], v_rows[nm], l)
        return [acc[nm][0] for nm in group], slots

    core_arr = lax.axis_index("c").reshape(1).astype(jnp.int32)
    chip_arr = (2 * lax.axis_index("x") + lax.axis_index("y")).reshape(1).astype(jnp.int32)

    def exchange_cores(full, group, after, tag):
        full4s = [full[nm].reshape(4, 2, full[nm].shape[0] // NDEV, D) for nm in group]
        bufs = [lax.empty((4,) + a.shape[2:], bf16) for a in full4s]
        ssem, rsem, full4s, bufs, token = _pair_start(full4s, bufs, after, tag)
        return (full4s, bufs, ssem, rsem, tag), token

    def exchange_chips(flight, after):
        full4s, bufs, ssem, rsem, tag = flight
        full4s, bufs = _pair_wait(full4s, bufs, ssem, rsem, after, tag)
        sums = _pair_sum(core_arr, full4s, bufs)
        slots = [lax.empty(a.shape, bf16) for a in sums]
        ssem, rsem, sums, slots, token = _chip_start(sums, slots, bufs[0], tag)
        return (sums, slots, ssem, rsem, tag), token

    def update_chips(l, group, flight, after):
        sums, slots, ssem, rsem, tag = flight
        sums, slots = _chip_wait(sums, slots, ssem, rsem, after, tag)
        for nm, sums_g, slots_g in zip(group, sums, slots):
            own = sums_g.reshape(4 * sums_g.shape[1], D)
            acc[nm] = _reduce_adamw(acc.get(nm), chip_arr, own, slots_g, w_rows[nm], m_rows[nm], v_rows[nm], l)
        return [acc[nm][0] for nm in group]

    flights = {}
    token_b = None
    for l in reversed(range(L)):
        sv = saved[l]
        gt1, ut1, dn1, wint, wout, gt2, ut2, dn2 = gathered[l]
        full = {}
        dx, dgate, dup, h, dy, d_norm["ffn2_norm"][l] = _ffn_bwd_d(
            sv["x2"], g_ffn2[l], sv["gate2"], sv["up2"], dx, gt2, ut2, dn2, after=() if token_b is None else (token_b,))
        full["ffn2_w_gate"], full["ffn2_w_up"], full["ffn2_w_down"] = _ffn_bwd_w(h, dy, sv["gate2"], sv["up2"], dgate, dup)

        dxb, dyp, do1, do4, do16, dl1, dl4, dl16 = _mix_out_bwd(dx, sv["o"], wout)
        full["w_out"] = _wgrad(sv["mixed"], dxb)
        flights[l, "a"], token_a = (exchange_cores if l == 0 else exchange)(full, group_a, dxb, f"a{l}")
        dvp, d_poolw[l], d_pscale[l] = _pool_bwd(dyp, sv["diff"], wbd[l], pscale[l], after=(token_a,))
        dos, dls = (do1, flat(do4), flat(do16)), (dl1, flat(dl4), flat(dl16))
        dqkv = []
        for b, lc in enumerate((S, S // 4, S // 16)):
            qb, kb, vb = sv["qkv"][b]
            dqkv.append(_attn_bwd(qb, kb, vb, dos[b], sv["lse"][b], dls[b], lc))
        d4 = tuple(r4(a) for a in dqkv[1])
        d16 = tuple(r16(a) for a in dqkv[2])
        mix_after = ()
        if l == 0:
            flights[0, "a"], token_a = exchange_chips(flights[0, "a"], [dqkv[0][0], dqkv[1][0], dqkv[2][0]])
            mix_after = (token_a,)
        dx, dproj, d_norm["mix_norm"][l] = _mix_in_bwd(dx, sv["x1"], g_mix[l], wint, tabs, dvp, dqkv[0], d4, d16,
                                                       after=mix_after)
        full["w_in"] = _wgrad(dproj, sv["hmix"])

        dx, dgate, dup, h, dy, d_norm["ffn1_norm"][l] = _ffn_bwd_d(sv["x0"], g_ffn1[l], sv["gate1"], sv["up1"], dx, gt1, ut1, dn1)
        full["ffn1_w_gate"], full["ffn1_w_up"], full["ffn1_w_down"] = _ffn_bwd_w(h, dy, sv["gate1"], sv["up1"], dgate, dup)

        after = dx
        if l + 1 < L and l + 1 >= 2:
            after, _ = update(l + 1, group_a, flights.pop((l + 1, "a")), after)
        if l + 1 < L and l + 1 >= 3:
            after, _ = update(l + 1, group_b, flights.pop((l + 1, "b")), after)
        if l > 0:
            flights[l, "b"], token_b = exchange(full, group_b, after, f"b{l}")

    flights[0, "b"], token_b = exchange_cores(full, group_b, dx, "b0")
    pad8 = lambda a: jnp.pad(a, ((0, 8 - a.shape[0]), (0, 0)))
    misc = jnp.concatenate([d_final, jnp.concatenate(d_pscale, axis=1), loss_part], axis=0)
    small = jnp.concatenate(
        [pad8(jnp.concatenate(d_norm[nm], axis=0)) for nm in ("ffn1_norm", "mix_norm", "ffn2_norm")]
        + [pad8(misc), jnp.stack(d_poolw).reshape(L * 16, D)], axis=0)
    small_slots = lax.dynamic_update_slice(lax.empty((NDEV, SMALL_ROWS, D), f32), small[None], (me_idx, 0, 0))
    pack_sems = _rs_start([small], [small_slots], token_b, "pack")
    flights[0, "b"], token_b = exchange_chips(flights[0, "b"], pack_sems[-1])

    after = token_b
    for key in [(2, "b"), (1, "a"), (1, "b")]:
        after, _ = update(key[0], group_a if key[1] == "a" else group_b, flights.pop(key), after)
    _, pack_slots = _rs_wait(pack_sems[2], pack_sems[3], pack_sems[0], pack_sems[1], after, "pack")
    sm = _sum_slots(pack_slots[0], SMALL_ROWS)
    grads = {}
    grads["ffn1_norm"], grads["mix_norm"], grads["ffn2_norm"] = sm[0:L], sm[8:8 + L], sm[16:16 + L]
    grads["final_norm"] = sm[24]
    grads["pool_scale"] = sm[25].reshape(L, PW)
    grads["pool_w"] = sm[32:32 + L * 16].reshape(L, 4, 64, 64)
    loss = sm[26, 0]
    upd = {nm: _adamw(weights[nm], grads[nm], moms[nm], vels[nm]) for nm in names if nm not in _BIG}
    after = update_chips(0, group_a, flights.pop((0, "a")), [upd[nm][0] for nm in upd])
    update_chips(0, group_b, flights.pop((0, "b")), after)
    for nm in _BIG:
        grads[nm], upd[nm] = as_rows(acc[nm][0], nm), tuple(as_rows(a, nm) for a in acc[nm][1:])
    return (loss, dx.reshape(1, S, D), *[grads[nm] for nm in names], *[upd[nm][0] for nm in names],
            *[upd[nm][1] for nm in names], *[upd[nm][2] for nm in names])
```

```python
import jax
import jax.numpy as jnp
from jax import lax
from jax.experimental import pallas as pl
from jax.experimental.pallas import tpu as pltpu

f32 = jnp.float32
bf16 = jnp.bfloat16
SDS = jax.ShapeDtypeStruct

D = 1024
S = 2048
F = 2816
L = 4
PW = 256
AW = 768
PROJ = PW + 3 * AW
NDEV = 8
TM = 256
QB = 128
HALF = 64
NG = AW // 128
NORM_EPS = 1e-6
MASK_VALUE = -1e30
ROPE_THETA = 500000.0
ADAM_LR, ADAM_B1, ADAM_B2, ADAM_EPS, ADAM_WD, ADAM_STEP = 0.001, 0.9, 0.999, 1e-08, 0.01, 10
POOL_WINDOWS = (2, 4, 8, 16)
PAD = 8
SMALL_ROWS = 96
VMEM_LIMIT = 56 * 1024 * 1024

_CP = pltpu.CompilerParams(vmem_limit_bytes=VMEM_LIMIT)
_ANY = pl.BlockSpec(memory_space=pl.ANY)
_HBM = pl.BlockSpec(memory_space=pltpu.HBM)
_SEM = pl.BlockSpec(memory_space=pltpu.SEMAPHORE)
_MESH = pl.DeviceIdType.MESH
_CP_SPLIT = pltpu.CompilerParams(has_side_effects=pltpu.SideEffectType.DATAFLOW_SIDE_EFFECTING)


def _dot_nn(a, b):
    return lax.dot_general(a, b, (((1,), (0,)), ((), ())), preferred_element_type=f32)


def _dot_nt(a, b):
    return lax.dot_general(a, b, (((1,), (1,)), ((), ())), preferred_element_type=f32)


def _dot_tn(a, b):
    return lax.dot_general(a, b, (((0,), (0,)), ((), ())), preferred_element_type=f32)


def _rms(x, g):
    r = lax.rsqrt(jnp.mean(x * x, axis=-1, keepdims=True) + NORM_EPS)
    xh = x * r
    return r, xh, xh * g


def _rms_bwd(dh, r, xh, g):
    dxh = dh * g
    return r * (dxh - xh * jnp.mean(dxh * xh, axis=-1, keepdims=True))


def _tile(cols, rows=TM):
    return pl.BlockSpec((rows, cols), lambda i: (i, 0))


def _const(shape):
    return pl.BlockSpec(shape, lambda i: (0,) * len(shape))


def _layer(rows, cols):
    return pl.BlockSpec((rows, cols), lambda i: (0, 0), pipeline_mode=pl.Buffered(1))


def _p4(cols=AW):
    return pl.BlockSpec((4, TM // 4, cols), lambda i: (0, i, 0))


def _p16(cols=AW):
    return pl.BlockSpec((16, TM // 16, cols), lambda i: (0, i, 0))


def _cols(j):
    return slice(128 * j, 128 * (j + 1))


def _follow(body, n_in, after):
    k = len(after)
    return body if k == 0 else (lambda *refs: body(*refs[:n_in], *refs[n_in + k:]))


def _ffn_fwd(x, g, gt, ut, dn, after=()):
    def body(x_ref, g_ref, gt_ref, ut_ref, dn_ref, xo_ref, gate_ref, up_ref):
        x = x_ref[...]
        _, _, hn = _rms(x, g_ref[...])
        h = hn.astype(bf16)
        gate = _dot_nt(h, gt_ref[...])
        up = _dot_nt(h, ut_ref[...])
        gate_ref[...] = gate.astype(bf16)
        up_ref[...] = up.astype(bf16)
        a = (gate * jax.nn.sigmoid(gate) * up).astype(bf16)
        xo_ref[...] = x + 0.5 * _dot_nn(a, dn_ref[...])

    rows = 2 * TM
    return pl.pallas_call(
        _follow(body, 5, after), grid=(S // rows,),
        in_specs=[_tile(D, rows), _layer(1, D), _layer(F, D), _layer(F, D), _layer(F, D)] + [_ANY] * len(after),
        out_specs=[_tile(D, rows), _tile(F, rows), _tile(F, rows)],
        out_shape=[SDS((S, D), f32), SDS((S, F), bf16), SDS((S, F), bf16)],
        compiler_params=_CP, name="ffn_fwd")(x, g, gt, ut, dn, *after)


def _ffn_bwd_d(x, g, gate, up, dxo, gt, ut, dn, after=()):
    def body(x_ref, g_ref, gate_ref, up_ref, dxo_ref, gt_ref, ut_ref, dn_ref,
             dx_ref, dgate_ref, dup_ref, h_ref, dy_ref, dg_ref):
        x = x_ref[...]
        g = g_ref[...]
        r, xh, hn = _rms(x, g)
        h_ref[...] = hn.astype(bf16)
        dxo = dxo_ref[...]
        dy = (0.5 * dxo).astype(bf16)
        dy_ref[...] = dy
        da = _dot_nt(dy, dn_ref[...])
        gate = gate_ref[...].astype(f32)
        up = up_ref[...].astype(f32)
        sg = jax.nn.sigmoid(gate)
        dgate = (da * up * (sg * (1.0 + gate * (1.0 - sg)))).astype(bf16)
        dup = (da * (gate * sg)).astype(bf16)
        dgate_ref[...] = dgate
        dup_ref[...] = dup
        dh = _dot_nn(dgate, gt_ref[...]) + _dot_nn(dup, ut_ref[...])

        @pl.when(pl.program_id(0) == 0)
        def _():
            dg_ref[...] = jnp.zeros_like(dg_ref)

        dg_ref[...] += jnp.sum(dh * xh, axis=0, keepdims=True)
        dx_ref[...] = dxo + _rms_bwd(dh, r, xh, g)

    return pl.pallas_call(
        _follow(body, 8, after), grid=(S // TM,),
        in_specs=[_tile(D), _layer(1, D), _tile(F), _tile(F), _tile(D),
                  _layer(F, D), _layer(F, D), _layer(F, D)] + [_ANY] * len(after),
        out_specs=[_tile(D), _tile(F), _tile(F), _tile(D), _tile(D), _const((1, D))],
        out_shape=[SDS((S, D), f32), SDS((S, F), bf16), SDS((S, F), bf16), SDS((S, D), bf16),
                   SDS((S, D), bf16), SDS((1, D), f32)],
        compiler_params=_CP, name="ffn_bwd_d")(x, g, gate, up, dxo, gt, ut, dn, *after)


def _ffn_bwd_w(h, dy, gate, up, dgate, dup):
    fc = 256

    def body(h_ref, dy_ref, gate_ref, up_ref, dgate_ref, dup_ref, dgt_ref, dut_ref, ddn_ref):
        gate = gate_ref[...].astype(f32)
        a = (gate * jax.nn.sigmoid(gate) * up_ref[...].astype(f32)).astype(bf16)
        ddn_ref[...] = _dot_tn(a, dy_ref[...]).astype(bf16)
        h = h_ref[...]
        dgt_ref[...] = _dot_tn(dgate_ref[...], h).astype(bf16)
        dut_ref[...] = _dot_tn(dup_ref[...], h).astype(bf16)

    col = pl.BlockSpec((S, fc), lambda j: (0, j))
    row = pl.BlockSpec((fc, D), lambda j: (j, 0))
    full = pl.BlockSpec((S, D), lambda j: (0, 0))
    return pl.pallas_call(
        body, grid=(F // fc,),
        in_specs=[full, full, col, col, col, col],
        out_specs=[row, row, row],
        out_shape=[SDS((F, D), bf16)] * 3,
        compiler_params=_CP, name="ffn_bwd_w")(h, dy, gate, up, dgate, dup)


def _wgrad(a, b):
    m, n = a.shape[1], b.shape[1]
    mc = 2 * TM

    def body(a_ref, b_ref, o_ref):
        o_ref[...] = _dot_tn(a_ref[...], b_ref[...]).astype(bf16)

    return pl.pallas_call(
        body, grid=(m // mc,),
        in_specs=[pl.BlockSpec((S, mc), lambda j: (0, j)), pl.BlockSpec((S, n), lambda j: (0, 0))],
        out_specs=pl.BlockSpec((mc, n), lambda j: (j, 0)),
        out_shape=SDS((m, n), bf16),
        compiler_params=_CP, name="wgrad")(a, b)


def _rope(t, c, sn, sp):
    return t * c + pltpu.roll(t, 120, 1) * sn + pltpu.roll(t, 8, 1) * sp


def _rope_bwd(d, c, sn, sp):
    return d * c + pltpu.roll(d * sn, 8, 1) + pltpu.roll(d * sp, 120, 1)


def _rope_tables(positions):
    inv_freq = ROPE_THETA ** (-jnp.arange(0, 16, 2, dtype=f32) / 16)
    ang = positions.reshape(S, 1).astype(f32) * inv_freq
    cos, sin = jnp.cos(ang), jnp.sin(ang)
    one = jnp.ones((S, 48), f32)
    zero8 = jnp.zeros((S, 8), f32)
    zero48 = jnp.zeros((S, 48), f32)
    c = jnp.concatenate([cos, cos, one], axis=1)
    sn = jnp.concatenate([-sin, zero8, zero48], axis=1)
    sp = jnp.concatenate([zero8, sin, zero48], axis=1)
    return tuple(jnp.concatenate([t, t], axis=1) for t in (c, sn, sp))


def _dilation_perm(n, back=False):
    per = TM // n
    i = lax.broadcasted_iota(jnp.int32, (TM, TM), 1 if back else 0)
    j = lax.broadcasted_iota(jnp.int32, (TM, TM), 0 if back else 1)
    return jnp.where(j == n * (i % per) + i // per, 1.0, 0.0).astype(bf16)


def _mix_in_fwd(x, g, wint, tabs):
    def body(x_ref, g_ref, w_ref, c_ref, sn_ref, sp_ref,
             h_ref, vp_ref, q1, k1, v1, q4, k4, v4, q16, k16, v16):
        _, _, hn = _rms(x_ref[...], g_ref[...])
        h = hn.astype(bf16)
        h_ref[...] = h
        proj = _dot_nt(h, w_ref[...])
        vp_ref[...] = proj[:, :PW]
        c, sn, sp = c_ref[...], sn_ref[...], sp_ref[...]
        perm4, perm16 = _dilation_perm(4), _dilation_perm(16)
        for kind, (o1, o4, o16) in enumerate(((q1, q4, q16), (k1, k4, k16), (v1, v4, v16))):
            for j in range(NG):
                t = proj[:, PW + kind * AW + 128 * j: PW + kind * AW + 128 * (j + 1)]
                if kind == 0:
                    t = _rope(t, c, sn, sp) * 0.125
                elif kind == 1:
                    t = _rope(t, c, sn, sp)
                o1[:, _cols(j)] = t.astype(bf16)
            nat = o1[...]
            o4[...] = _dot_nn(perm4, nat).astype(bf16).reshape(4, TM // 4, AW)
            o16[...] = _dot_nn(perm16, nat).astype(bf16).reshape(16, TM // 16, AW)

    nat, d4, d16 = SDS((S, AW), bf16), SDS((4, S // 4, AW), bf16), SDS((16, S // 16, AW), bf16)
    return pl.pallas_call(
        body, grid=(S // TM,),
        in_specs=[_tile(D), _layer(1, D), _layer(PROJ, D), _tile(128), _tile(128), _tile(128)],
        out_specs=[_tile(D), _tile(PW)] + [_tile(AW)] * 3 + [_p4()] * 3 + [_p16()] * 3,
        out_shape=[SDS((S, D), bf16), SDS((S, PW), f32)] + [nat] * 3 + [d4] * 3 + [d16] * 3,
        compiler_params=_CP, name="mix_in_fwd")(x, g, wint, *tabs)


def _mix_in_bwd(dxo, x, g, wint, tabs, dvp, d1, d4, d16, after=()):
    def body(dxo_ref, x_ref, g_ref, w_ref, c_ref, sn_ref, sp_ref, dvp_ref,
             dq1, dk1, dv1, dq4, dk4, dv4, dq16, dk16, dv16,
             dx_ref, dproj_ref, dg_ref):
        c, sn, sp = c_ref[...], sn_ref[...], sp_ref[...]
        dproj_ref[:, :PW] = dvp_ref[...].astype(bf16)
        back4, back16 = _dilation_perm(4, True), _dilation_perm(16, True)
        for kind, (a1, a4, a16) in enumerate(((dq1, dq4, dq16), (dk1, dk4, dk16), (dv1, dv4, dv16))):
            n4 = _dot_nn(back4, a4[...].reshape(TM, AW))
            n16 = _dot_nn(back16, a16[...].reshape(TM, AW))
            for j in range(NG):
                t = a1[:, _cols(j)].astype(f32) + n4[:, _cols(j)] + n16[:, _cols(j)]
                if kind == 0:
                    t = _rope_bwd(t * 0.125, c, sn, sp)
                elif kind == 1:
                    t = _rope_bwd(t, c, sn, sp)
                dproj_ref[:, PW + kind * AW + 128 * j: PW + kind * AW + 128 * (j + 1)] = t.astype(bf16)
        g = g_ref[...]
        r_, xh, _ = _rms(x_ref[...], g)
        dh = _dot_nn(dproj_ref[...], w_ref[...])

        @pl.when(pl.program_id(0) == 0)
        def _():
            dg_ref[...] = jnp.zeros_like(dg_ref)

        dg_ref[...] += jnp.sum(dh * xh, axis=0, keepdims=True)
        dx_ref[...] = dxo_ref[...] + _rms_bwd(dh, r_, xh, g)

    return pl.pallas_call(
        _follow(body, 17, after), grid=(S // TM,),
        in_specs=[_tile(D), _tile(D), _layer(1, D), _layer(PROJ, D), _tile(128), _tile(128), _tile(128),
                  _tile(PW)] + [_tile(AW)] * 3 + [_p4()] * 3 + [_p16()] * 3 + [_ANY] * len(after),
        out_specs=[_tile(D), _tile(PROJ), _const((1, D))],
        out_shape=[SDS((S, D), f32), SDS((S, PROJ), bf16), SDS((1, D), f32)],
        compiler_params=_CP, name="mix_in_bwd")(dxo, x, g, wint, *tabs, dvp, *d1, *d4, *d16, *after)


def _pool_sums(pad_ref, base, rows, adjoint):
    lane_group = lax.broadcasted_iota(jnp.int32, (rows, PW), 1) // 64
    sign = -1 if adjoint else 1

    def sh(o):
        return pad_ref[pl.ds(PAD + base + sign * o, rows), :]

    out = None
    acc = None
    lo, hi = 0, 0
    for gi, w in enumerate(POOL_WINDOWS):
        for o in list(range(-(w // 2), lo)) + list(range(hi, w - w // 2)):
            acc = sh(o) if acc is None else acc + sh(o)
        lo, hi = -(w // 2), w - w // 2
        out = acc if out is None else jnp.where(lane_group >= gi, acc, out)
    return out


def _pool_counts(base, rows):
    pos = base + lax.broadcasted_iota(jnp.int32, (rows, PW), 0)
    lane_group = lax.broadcasted_iota(jnp.int32, (rows, PW), 1) // 64
    cnt = None
    for gi, w in enumerate(POOL_WINDOWS):
        lo = jnp.maximum(pos - w // 2, 0)
        hi = jnp.minimum(pos + w - 1 - w // 2, S - 1)
        c = (hi - lo + 1).astype(f32)
        cnt = c if cnt is None else jnp.where(lane_group >= gi, c, cnt)
    return cnt


def _pool_fwd(vp, wbd, scale):
    ch = 256

    def body(vp_ref, w_ref, sc_ref, y_ref, diff_ref, pad):
        pad[pl.ds(0, PAD), :] = jnp.zeros((PAD, PW), f32)
        pad[pl.ds(PAD + S, PAD), :] = jnp.zeros((PAD, PW), f32)
        pad[pl.ds(PAD, S), :] = vp_ref[...]
        for b in range(S // ch):
            base = b * ch
            pooled = _pool_sums(pad, base, ch, False) / _pool_counts(base, ch)
            diff = (pooled - vp_ref[pl.ds(base, ch), :]).astype(bf16)
            diff_ref[pl.ds(base, ch), :] = diff
            y_ref[pl.ds(base, ch), :] = _dot_nn(diff, w_ref[...]) * sc_ref[...]

    whole = lambda shape: pl.BlockSpec(shape, lambda i: (0,) * len(shape))
    return pl.pallas_call(
        body, grid=(1,),
        in_specs=[whole((S, PW)), whole((PW, PW)), whole((1, PW))],
        out_specs=[whole((S, PW)), whole((S, PW))],
        out_shape=[SDS((S, PW), f32), SDS((S, PW), bf16)],
        scratch_shapes=[pltpu.VMEM((S + 2 * PAD, PW), f32)],
        compiler_params=_CP, name="pool_fwd")(vp, wbd, scale)


def _pool_bwd(dy, diff, wbd, scale, after=()):
    ch = 256

    def body(dy_ref, diff_ref, w_ref, sc_ref, dvp_ref, dw_ref, dsc_ref, pad):
        pad[pl.ds(0, PAD), :] = jnp.zeros((PAD, PW), f32)
        pad[pl.ds(PAD + S, PAD), :] = jnp.zeros((PAD, PW), f32)
        dw = jnp.zeros((PW, PW), f32)
        dsc = jnp.zeros((1, PW), f32)
        for b in range(S // ch):
            base = b * ch
            dy = dy_ref[pl.ds(base, ch), :]
            diff = diff_ref[pl.ds(base, ch), :]
            dsc = dsc + jnp.sum(dy * _dot_nn(diff, w_ref[...]), axis=0, keepdims=True)
            dz = (dy * sc_ref[...]).astype(bf16)
            dw = dw + _dot_tn(diff, dz)
            ddiff = _dot_nt(dz, w_ref[...])
            dvp_ref[pl.ds(base, ch), :] = -ddiff
            pad[pl.ds(PAD + base, ch), :] = ddiff / _pool_counts(base, ch)
        for gi in range(4):
            dw_ref[gi] = dw[64 * gi:64 * (gi + 1), 64 * gi:64 * (gi + 1)]
        dsc_ref[...] = dsc
        for b in range(S // ch):
            base = b * ch
            dvp_ref[pl.ds(base, ch), :] += _pool_sums(pad, base, ch, True)

    whole = lambda shape: pl.BlockSpec(shape, lambda i: (0,) * len(shape))
    return pl.pallas_call(
        _follow(body, 4, after), grid=(1,),
        in_specs=[whole((S, PW)), whole((S, PW)), whole((PW, PW)), whole((1, PW))] + [_ANY] * len(after),
        out_specs=[whole((S, PW)), whole((4, 64, 64)), whole((1, PW))],
        out_shape=[SDS((S, PW), f32), SDS((4, 64, 64), f32), SDS((1, PW), f32)],
        scratch_shapes=[pltpu.VMEM((S + 2 * PAD, PW), f32)],
        compiler_params=_CP, name="pool_bwd")(dy, diff, wbd, scale, *after)


def _attn_blocks(lc):
    bpc = lc // QB
    kw = min(2 * QB, lc)
    blocks = []
    for b in range(S // QB):
        t0 = (b % bpc) * QB
        ks_in = min(max(t0 - HALF, 0), lc - kw)
        blocks.append((b * QB, (b // bpc) * lc + ks_in, t0 - ks_in))
    return kw, blocks


def _attn_bias(bias_ref, kw, shifts):
    r = lax.broadcasted_iota(jnp.int32, (2 * QB, kw), 0) % QB
    c = lax.broadcasted_iota(jnp.int32, (2 * QB, kw), 1)
    for i, shift in enumerate(shifts):
        bias_ref[i] = jnp.where(jnp.abs(r + shift - c) <= HALF, 0.0, MASK_VALUE).astype(f32)


def _head_put(stats, pair, v0, v1, lane):
    return jnp.where(lane == 2 * pair, v0, jnp.where(lane == 2 * pair + 1, v1, stats))


def _head_cols(stats, pair, lane):
    c0 = jnp.sum(jnp.where(lane == 2 * pair, stats, 0.0), axis=-1, keepdims=True)
    c1 = jnp.sum(jnp.where(lane == 2 * pair + 1, stats, 0.0), axis=-1, keepdims=True)
    return jnp.concatenate([c0, c1], axis=0)


def _head_spread(stats, pair, head0):
    return jnp.where(head0, stats[:, 2 * pair:2 * pair + 1], stats[:, 2 * pair + 1:2 * pair + 2])


def _stack_heads(blk, head0):
    zero = jnp.zeros_like(blk)
    return jnp.concatenate([jnp.where(head0, blk, zero), jnp.where(head0, zero, blk)], axis=0)


def _attn_fwd(q, k, v, lc, after=None):
    kw, blocks = _attn_blocks(lc)
    shifts = sorted({b[2] for b in blocks})

    def body(q_ref, k_ref, v_ref, *refs):
        o_ref, lse_ref, bias_ref = refs[-3:]
        lane = lax.broadcasted_iota(jnp.int32, (QB, 128), 1)
        head0 = lane < 64
        pair = pl.program_id(0)
        _attn_bias(bias_ref, kw, shifts)

        @pl.when(pair == 0)
        def _():
            lse_ref[...] = jnp.zeros_like(lse_ref)

        for row0, kstart, shift in blocks:
            q2 = _stack_heads(q_ref[pl.ds(row0, QB), :], head0)
            kb = k_ref[pl.ds(kstart, kw), :]
            vb = v_ref[pl.ds(kstart, kw), :]
            s = _dot_nt(q2, kb) + bias_ref[shifts.index(shift)]
            m = jnp.max(s, axis=-1, keepdims=True)
            p = jnp.exp(s - m)
            den = jnp.sum(p, axis=-1, keepdims=True)
            o2 = _dot_nn(p.astype(bf16), vb) / den
            lse2 = m + jnp.log(den)
            o_ref[pl.ds(row0, QB), :] = jnp.where(head0, o2[:QB], o2[QB:]).astype(bf16)
            lse_ref[pl.ds(row0, QB), :] = _head_put(lse_ref[pl.ds(row0, QB), :], pair, lse2[:QB], lse2[QB:], lane)

    col = pl.BlockSpec((S, 128), lambda p: (0, p))
    extra = () if after is None else (after,)
    return pl.pallas_call(
        body, grid=(NG,), in_specs=[col, col, col] + [_ANY] * len(extra),
        out_specs=[col, pl.BlockSpec((S, 128), lambda p: (0, 0))],
        out_shape=[SDS((S, AW), bf16), SDS((S, 128), f32)],
        scratch_shapes=[pltpu.VMEM((len(shifts), 2 * QB, kw), f32)],
        compiler_params=_CP, name=f"attn_fwd_{lc}")(q, k, v, *extra)


def _attn_bwd(q, k, v, do, lse, delta, lc):
    kw, blocks = _attn_blocks(lc)
    shifts = sorted({b[2] for b in blocks})

    def body(q_ref, k_ref, v_ref, do_ref, lse_ref, dl_ref, dq_ref, dk_out, dv_out, bias_ref, dk_ref, dv_ref):
        lane = lax.broadcasted_iota(jnp.int32, (QB, 128), 1)
        head0 = lane < 64
        pair = pl.program_id(0)
        _attn_bias(bias_ref, kw, shifts)
        dk_ref[...] = jnp.zeros_like(dk_ref)
        dv_ref[...] = jnp.zeros_like(dv_ref)
        for row0, kstart, shift in blocks:
            q2 = _stack_heads(q_ref[pl.ds(row0, QB), :], head0)
            do2 = _stack_heads(do_ref[pl.ds(row0, QB), :], head0)
            lse2 = _head_cols(lse_ref[pl.ds(row0, QB), :], pair, lane)
            dl2 = _head_cols(dl_ref[pl.ds(row0, QB), :], pair, lane)
            kb = k_ref[pl.ds(kstart, kw), :]
            vb = v_ref[pl.ds(kstart, kw), :]
            p = jnp.exp(_dot_nt(q2, kb) + bias_ref[shifts.index(shift)] - lse2)
            ds = (p * (_dot_nt(do2, vb) - dl2)).astype(bf16)
            dq2 = _dot_nn(ds, kb)
            dq_ref[pl.ds(row0, QB), :] = jnp.where(head0, dq2[:QB], dq2[QB:]).astype(bf16)
            dk_ref[pl.ds(kstart, kw), :] += _dot_tn(ds, q2)
            dv_ref[pl.ds(kstart, kw), :] += _dot_tn(p.astype(bf16), do2)
        dk_out[...] = dk_ref[...].astype(bf16)
        dv_out[...] = dv_ref[...].astype(bf16)

    col = pl.BlockSpec((S, 128), lambda p: (0, p))
    stats = pl.BlockSpec((S, 128), lambda p: (0, 0))
    return pl.pallas_call(
        body, grid=(NG,), in_specs=[col] * 4 + [stats] * 2, out_specs=[col] * 3,
        out_shape=[SDS((S, AW), bf16)] * 3,
        scratch_shapes=[pltpu.VMEM((len(shifts), 2 * QB, kw), f32), pltpu.VMEM((S, 128), f32),
                        pltpu.VMEM((S, 128), f32)],
        compiler_params=_CP, name=f"attn_bwd_{lc}")(q, k, v, do, lse, delta)


def _mix_out_fwd(x, ypool, o1, l1, o4, l4, o16, l16, wout):
    def body(x_ref, yp_ref, o1_ref, l1_ref, o4_ref, l4_ref, o16_ref, l16_ref, w_ref,
             xo_ref, mixed_ref, o_ref, lse1_ref, lse4_ref, lse16_ref, sl4, sl16, sl):
        head0 = lax.broadcasted_iota(jnp.int32, (TM, 128), 1) < 64
        for r in range(4):
            sl4[pl.ds(r, TM // 4, stride=4), :] = l4_ref[r]
        for r in range(16):
            sl16[pl.ds(r, TM // 16, stride=16), :] = l16_ref[r]
        n4 = _dot_nn(_dilation_perm(4, True), o4_ref[...].reshape(TM, AW))
        n16 = _dot_nn(_dilation_perm(16, True), o16_ref[...].reshape(TM, AW))
        a, b, c = l1_ref[...], sl4[...], sl16[...]
        m = jnp.maximum(jnp.maximum(a, b), c)
        wa, wb, wc = jnp.exp(a - m), jnp.exp(b - m), jnp.exp(c - m)
        den = wa + wb + wc
        wa, wb, wc = wa / den, wb / den, wc / den
        lse = m + jnp.log(den)
        lse1_ref[...] = lse
        sl[...] = lse
        mixed_ref[:, :PW] = yp_ref[...].astype(bf16)
        for j in range(NG):
            y = (_head_spread(wa, j, head0) * o1_ref[:, _cols(j)].astype(f32)
                 + _head_spread(wb, j, head0) * n4[:, _cols(j)] + _head_spread(wc, j, head0) * n16[:, _cols(j)])
            o_ref[:, _cols(j)] = y
            mixed_ref[:, PW + 128 * j: PW + 128 * (j + 1)] = y.astype(bf16)
        for r in range(4):
            lse4_ref[r] = sl[pl.ds(r, TM // 4, stride=4), :]
        for r in range(16):
            lse16_ref[r] = sl[pl.ds(r, TM // 16, stride=16), :]
        xo_ref[...] = x_ref[...] + _dot_nn(mixed_ref[...], w_ref[...])

    return pl.pallas_call(
        body, grid=(S // TM,),
        in_specs=[_tile(D), _tile(PW), _tile(AW), _tile(128), _p4(), _p4(128), _p16(), _p16(128), _layer(D, D)],
        out_specs=[_tile(D), _tile(D), _tile(AW), _tile(128), _p4(128), _p16(128)],
        out_shape=[SDS((S, D), f32), SDS((S, D), bf16), SDS((S, AW), f32), SDS((S, 128), f32),
                   SDS((4, S // 4, 128), f32), SDS((16, S // 16, 128), f32)],
        scratch_shapes=[pltpu.VMEM((TM, 128), f32)] * 3,
        compiler_params=_CP, name="mix_out_fwd")(x, ypool, o1, l1, o4, l4, o16, l16, wout)


def _mix_out_bwd(dxo, o, wout):
    def body(dxo_ref, o_ref, w_ref, dxb_ref, dyp_ref, do1, do4, do16, dl1, dl4, dl16, sdl):
        dxb = dxo_ref[...].astype(bf16)
        dxb_ref[...] = dxb
        dm = _dot_nt(dxb, w_ref[...])
        dyp_ref[...] = dm[:, :PW]
        lane = lax.broadcasted_iota(jnp.int32, (TM, 128), 1)
        head0 = lane < 64
        dl = jnp.zeros((TM, 128), f32)
        for j in range(NG):
            d = dm[:, PW + 128 * j: PW + 128 * (j + 1)]
            prod = d * o_ref[:, _cols(j)]
            dl = _head_put(dl, j, jnp.sum(jnp.where(head0, prod, 0.0), axis=-1, keepdims=True),
                           jnp.sum(jnp.where(head0, 0.0, prod), axis=-1, keepdims=True), lane)
            do1[:, _cols(j)] = d.astype(bf16)
        dl1[...] = dl
        sdl[...] = dl
        for r in range(4):
            dl4[r] = sdl[pl.ds(r, TM // 4, stride=4), :]
        for r in range(16):
            dl16[r] = sdl[pl.ds(r, TM // 16, stride=16), :]
        nat = do1[...]
        do4[...] = _dot_nn(_dilation_perm(4), nat).astype(bf16).reshape(4, TM // 4, AW)
        do16[...] = _dot_nn(_dilation_perm(16), nat).astype(bf16).reshape(16, TM // 16, AW)

    return pl.pallas_call(
        body, grid=(S // TM,),
        in_specs=[_tile(D), _tile(AW), _layer(D, D)],
        out_specs=[_tile(D), _tile(PW), _tile(AW), _p4(), _p16(), _tile(128), _p4(128), _p16(128)],
        out_shape=[SDS((S, D), bf16), SDS((S, PW), f32),
                   SDS((S, AW), bf16), SDS((4, S // 4, AW), bf16), SDS((16, S // 16, AW), bf16),
                   SDS((S, 128), f32), SDS((4, S // 4, 128), f32), SDS((16, S // 16, 128), f32)],
        scratch_shapes=[pltpu.VMEM((TM, 128), f32)],
        compiler_params=_CP, name="mix_out_bwd")(dxo, o, wout)


def _loss_head(x, g, target):
    def body(x_ref, g_ref, t_ref, dx_ref, loss_ref, dg_ref):
        g = g_ref[...]
        r, xh, y = _rms(x_ref[...], g)
        err = y - t_ref[...]
        dy = err * (1.0 / D)

        @pl.when(pl.program_id(0) == 0)
        def _():
            loss_ref[...] = jnp.zeros_like(loss_ref)
            dg_ref[...] = jnp.zeros_like(dg_ref)

        loss_ref[...] += jnp.broadcast_to(0.5 * jnp.sum(jnp.mean(err * err, axis=-1, keepdims=True)), (1, D))
        dg_ref[...] += jnp.sum(dy * xh, axis=0, keepdims=True)
        dx_ref[...] = _rms_bwd(dy, r, xh, g)

    return pl.pallas_call(
        body, grid=(S // TM,),
        in_specs=[_tile(D), _const((1, D)), _tile(D)],
        out_specs=[_tile(D), _const((1, D)), _const((1, D))],
        out_shape=[SDS((S, D), f32), SDS((1, D), f32), SDS((1, D), f32)],
        compiler_params=_CP, name="loss_head")(x, g, target)


def _peer(k):
    x, y, c = lax.axis_index("x"), lax.axis_index("y"), lax.axis_index("c")
    px = 1 - x if k & 4 else x
    py = 1 - y if k & 2 else y
    pc = 1 - c if k & 1 else c
    return (px, py, pc), 4 * px + 2 * py + pc


def _diag_route():
    x, y, c = lax.axis_index("x"), lax.axis_index("y"), lax.axis_index("c")
    idx_x, idx_y = _peer(4)[1], _peer(2)[1]
    return idx_x + c * (idx_y - idx_x), (x + c * (1 - 2 * x), (1 - y) + c * (2 * y - 1), c)


def _hbm(a):
    return pltpu.with_memory_space_constraint(a, pltpu.HBM)


def _rows(ref, idx):
    r = ref.shape[0] // NDEV
    return ref.at[pl.ds(idx * r, r), :]


def _row_copy(ref, idx, send_sem, recv_sem, to):
    return pltpu.make_async_remote_copy(src_ref=_rows(ref, idx), dst_ref=_rows(ref, idx), send_sem=send_sem,
                                        recv_sem=recv_sem, device_id=to, device_id_type=_MESH)


def _place_own(me, shards, l):
    n = len(shards)

    def body(me_ref, *refs):
        for t in range(n):
            refs[n + t][...] = refs[t][...].astype(bf16)

    grid_spec = pltpu.PrefetchScalarGridSpec(
        num_scalar_prefetch=1, grid=(1,),
        in_specs=[pl.BlockSpec((None, s.shape[1], D), lambda i, me_ref: (l, 0, 0)) for s in shards],
        out_specs=[pl.BlockSpec((s.shape[1], D), lambda i, me_ref: (me_ref[0], 0)) for s in shards])
    return pl.pallas_call(
        body, grid_spec=grid_spec, out_shape=[SDS((NDEV * s.shape[1], D), bf16) for s in shards],
        compiler_params=_CP, name="place_own")(me, *shards)


_TOKEN = SDS((8, 128), f32)
def _ag_start(lands, after, l):
    n = len(lands)
    after = list(after) if isinstance(after, (list, tuple)) else [after]

    def body(*refs):
        zones, send_sems, recv_sems, token = refs[:n], refs[n + len(after)], refs[n + len(after) + 1], refs[-1]
        _, me_idx = _peer(0)
        for k, mask in enumerate((1, 4, 2)):
            for t in range(n):
                _row_copy(zones[t], me_idx, send_sems.at[k * n + t], recv_sems.at[k * n + t], _peer(mask)[0]).start()
        token[...] = jnp.zeros_like(token)

    outs = pl.pallas_call(
        body, name=f"ag_start_{l}", in_specs=[_HBM] * n + [_ANY] * len(after),
        out_specs=(_SEM, _SEM, *[_HBM] * n, pl.BlockSpec(memory_space=pltpu.VMEM)),
        out_shape=(pltpu.SemaphoreType.DMA((3 * n,)), pltpu.SemaphoreType.DMA((3 * n,)),
                   *[pltpu.HBM(a.shape, a.dtype) for a in lands], _TOKEN),
        input_output_aliases={t: 2 + t for t in range(n)}, compiler_params=_CP_SPLIT)(
            *[_hbm(a) for a in lands], *after)
    return outs[0], outs[1], list(outs[2:2 + n]), outs[-1]


def _ag_pass(lands, recv_sems, after, l):
    n = len(lands)
    after = list(after) if isinstance(after, (list, tuple)) else [after]

    def body(*refs):
        zones, recv_sems = refs[:n], refs[n]
        psend, precv, token = refs[n + 1 + len(after)], refs[n + 2 + len(after)], refs[-1]
        me, _ = _peer(0)
        sibling, _ = _peer(1)
        for j, mask in enumerate((4, 2)):
            idx = _peer(mask)[1]
            for t in range(n):
                _row_copy(zones[t], idx, psend.at[j * n + t], recv_sems.at[(1 + j) * n + t], me).wait_recv()
                _row_copy(zones[t], idx, psend.at[j * n + t], precv.at[j * n + t], sibling).start()
        fwd_idx, fwd_dev = _diag_route()
        for t in range(n):
            _row_copy(zones[t], fwd_idx, psend.at[2 * n + t], precv.at[2 * n + t], fwd_dev).start()
        token[...] = jnp.zeros_like(token)

    outs = pl.pallas_call(
        body, name=f"ag_pass_{l}", in_specs=[_HBM] * n + [_SEM] + [_ANY] * len(after),
        out_specs=(_SEM, _SEM, *[_HBM] * n, pl.BlockSpec(memory_space=pltpu.VMEM)),
        out_shape=(pltpu.SemaphoreType.DMA((3 * n,)), pltpu.SemaphoreType.DMA((3 * n,)),
                   *[pltpu.HBM(a.shape, a.dtype) for a in lands], _TOKEN),
        input_output_aliases={t: 2 + t for t in range(n)}, compiler_params=_CP_SPLIT)(*lands, recv_sems, *after)
    return outs[0], outs[1], list(outs[2:2 + n]), outs[-1]


def _ag_last(lands, precv, after, l):
    n = len(lands)
    after = list(after) if isinstance(after, (list, tuple)) else [after]

    def body(*refs):
        zones, precv = refs[:n], refs[n]
        qsend, qrecv, token = refs[n + 1 + len(after)], refs[n + 2 + len(after)], refs[-1]
        me, _ = _peer(0)
        sibling, _ = _peer(1)
        idx = _peer(6)[1]
        for t in range(n):
            _row_copy(zones[t], idx, qsend.at[t], precv.at[2 * n + t], me).wait_recv()
            _row_copy(zones[t], idx, qsend.at[t], qrecv.at[t], sibling).start()
        token[...] = jnp.zeros_like(token)

    outs = pl.pallas_call(
        body, name=f"ag_last_{l}", in_specs=[_HBM] * n + [_SEM] + [_ANY] * len(after),
        out_specs=(_SEM, _SEM, *[_HBM] * n, pl.BlockSpec(memory_space=pltpu.VMEM)),
        out_shape=(pltpu.SemaphoreType.DMA((n,)), pltpu.SemaphoreType.DMA((n,)),
                   *[pltpu.HBM(a.shape, a.dtype) for a in lands], _TOKEN),
        input_output_aliases={t: 2 + t for t in range(n)}, compiler_params=_CP_SPLIT)(*lands, precv, *after)
    return outs[0], outs[1], list(outs[2:2 + n]), outs[-1]


def _ag_wait(lands, send_sems, recv_sems, psend, precv, qsend, qrecv, after, l):
    n = len(lands)
    after = list(after) if isinstance(after, (list, tuple)) else [after]

    def body(*refs):
        zones = refs[:n]
        send_sems, recv_sems, psend, precv, qsend, qrecv = refs[n:n + 6]
        me, me_idx = _peer(0)
        for k in range(3):
            for t in range(n):
                _row_copy(zones[t], me_idx, send_sems.at[k * n + t], recv_sems.at[k * n + t], me).wait_send()
        for t in range(n):
            _row_copy(zones[t], _peer(1)[1], send_sems.at[t], recv_sems.at[t], me).wait_recv()
        fwd_idx, _ = _diag_route()
        for j, (mine, theirs) in enumerate(((_peer(4)[1], _peer(5)[1]), (_peer(2)[1], _peer(3)[1]))):
            for t in range(n):
                _row_copy(zones[t], mine, psend.at[j * n + t], precv.at[j * n + t], me).wait_send()
                _row_copy(zones[t], theirs, psend.at[j * n + t], precv.at[j * n + t], me).wait_recv()
        for t in range(n):
            _row_copy(zones[t], fwd_idx, psend.at[2 * n + t], precv.at[2 * n + t], me).wait_send()
            _row_copy(zones[t], _peer(6)[1], qsend.at[t], qrecv.at[t], me).wait_send()
            _row_copy(zones[t], _peer(7)[1], qsend.at[t], qrecv.at[t], me).wait_recv()

    outs = pl.pallas_call(
        body, name=f"ag_wait_{l}", in_specs=[_HBM] * n + [_SEM] * 6 + [_ANY] * len(after),
        out_specs=tuple([_HBM] * n), out_shape=tuple(pltpu.HBM(a.shape, a.dtype) for a in lands),
        input_output_aliases={t: t for t in range(n)}, compiler_params=_CP_SPLIT)(
            *lands, send_sems, recv_sems, psend, precv, qsend, qrecv, *after)
    return list(outs)


def _xchg_src(ref, slot_ref, idx):
    return _rows(ref, idx) if ref.shape[0] == NDEV * slot_ref.shape[1] else ref


def _rs_start(srcs, slots, after, tag):
    n = len(srcs)
    after = list(after) if isinstance(after, (list, tuple)) else [after]

    def body(*refs):
        src, slot = refs[:n], refs[n:2 * n]
        send_sems, recv_sems, token = refs[2 * n + len(after)], refs[2 * n + len(after) + 1], refs[-1]
        _, me_idx = _peer(0)
        for k in range(1, NDEV):
            dev, idx = _peer(k)
            for t in range(n):
                pltpu.make_async_remote_copy(
                    src_ref=_xchg_src(src[t], slot[t], idx), dst_ref=slot[t].at[me_idx],
                    send_sem=send_sems.at[(k - 1) * n + t], recv_sem=recv_sems.at[(k - 1) * n + t],
                    device_id=dev, device_id_type=_MESH).start()
        token[...] = jnp.zeros_like(token)

    outs = pl.pallas_call(
        body, name=f"rs_start_{tag}", in_specs=[_HBM] * (2 * n) + [_ANY] * len(after),
        out_specs=(_SEM, _SEM, *[_HBM] * (2 * n), pl.BlockSpec(memory_space=pltpu.VMEM)),
        out_shape=(pltpu.SemaphoreType.DMA(((NDEV - 1) * n,)), pltpu.SemaphoreType.DMA(((NDEV - 1) * n,)),
                   *[pltpu.HBM(a.shape, a.dtype) for a in list(srcs) + list(slots)], _TOKEN),
        input_output_aliases={t: 2 + t for t in range(2 * n)}, compiler_params=_CP_SPLIT)(
            *[_hbm(a) for a in list(srcs) + list(slots)], *after)
    return outs[0], outs[1], list(outs[2:2 + n]), list(outs[2 + n:2 + 2 * n]), outs[-1]


def _rs_wait(srcs, slots, send_sems, recv_sems, after, tag):
    n = len(srcs)
    after = list(after) if isinstance(after, (list, tuple)) else [after]

    def body(*refs):
        src, slot, send_sems, recv_sems = refs[:n], refs[n:2 * n], refs[2 * n], refs[2 * n + 1]
        me, _ = _peer(0)
        for k in range(1, NDEV):
            idx = _peer(k)[1]
            for t in range(n):
                cp = pltpu.make_async_remote_copy(
                    src_ref=_xchg_src(src[t], slot[t], idx), dst_ref=slot[t].at[idx],
                    send_sem=send_sems.at[(k - 1) * n + t], recv_sem=recv_sems.at[(k - 1) * n + t],
                    device_id=me, device_id_type=_MESH)
                cp.wait_send()
                cp.wait_recv()

    outs = pl.pallas_call(
        body, name=f"rs_wait_{tag}", in_specs=[_HBM] * (2 * n) + [_SEM, _SEM] + [_ANY] * len(after),
        out_specs=tuple([_HBM] * (2 * n)),
        out_shape=tuple(pltpu.HBM(a.shape, a.dtype) for a in list(srcs) + list(slots)),
        input_output_aliases={t: t for t in range(2 * n)}, compiler_params=_CP_SPLIT)(
            *srcs, *slots, send_sems, recv_sems, *after)
    return list(outs[:n]), list(outs[n:])


def _pair_start(full4s, bufs, after, tag):
    n = len(full4s)
    after = list(after) if isinstance(after, (list, tuple)) else [after]

    def body(*refs):
        full, buf = refs[:n], refs[n:2 * n]
        send_sems, recv_sems, token = refs[2 * n + len(after)], refs[2 * n + len(after) + 1], refs[-1]
        c = lax.axis_index("c")
        for t in range(n):
            pltpu.make_async_remote_copy(src_ref=full[t].at[:, 1 - c], dst_ref=buf[t], send_sem=send_sems.at[t],
                                         recv_sem=recv_sems.at[t], device_id=_peer(1)[0], device_id_type=_MESH).start()
        token[...] = jnp.zeros_like(token)

    outs = pl.pallas_call(
        body, name=f"pair_start_{tag}", in_specs=[_HBM] * (2 * n) + [_ANY] * len(after),
        out_specs=(_SEM, _SEM, *[_HBM] * (2 * n), pl.BlockSpec(memory_space=pltpu.VMEM)),
        out_shape=(pltpu.SemaphoreType.DMA((n,)), pltpu.SemaphoreType.DMA((n,)),
                   *[pltpu.HBM(a.shape, a.dtype) for a in list(full4s) + list(bufs)], _TOKEN),
        input_output_aliases={t: 2 + t for t in range(2 * n)}, compiler_params=_CP_SPLIT)(
            *[_hbm(a) for a in list(full4s) + list(bufs)], *after)
    return outs[0], outs[1], list(outs[2:2 + n]), list(outs[2 + n:2 + 2 * n]), outs[-1]


def _pair_wait(full4s, bufs, send_sems, recv_sems, after, tag):
    n = len(full4s)
    after = list(after) if isinstance(after, (list, tuple)) else [after]

    def body(*refs):
        full, buf, send_sems, recv_sems = refs[:n], refs[n:2 * n], refs[2 * n], refs[2 * n + 1]
        c = lax.axis_index("c")
        for t in range(n):
            cp = pltpu.make_async_remote_copy(src_ref=full[t].at[:, 1 - c], dst_ref=buf[t], send_sem=send_sems.at[t],
                                              recv_sem=recv_sems.at[t], device_id=_peer(0)[0], device_id_type=_MESH)
            cp.wait_send()
            cp.wait_recv()

    outs = pl.pallas_call(
        body, name=f"pair_wait_{tag}", in_specs=[_HBM] * (2 * n) + [_SEM, _SEM] + [_ANY] * len(after),
        out_specs=tuple([_HBM] * (2 * n)),
        out_shape=tuple(pltpu.HBM(a.shape, a.dtype) for a in list(full4s) + list(bufs)),
        input_output_aliases={t: t for t in range(2 * n)}, compiler_params=_CP_SPLIT)(
            *full4s, *bufs, send_sems, recv_sems, *after)
    return list(outs[:n]), list(outs[n:])


def _pair_sum(core, full4s, bufs):
    n = len(full4s)

    def body(core_ref, *refs):
        for t in range(n):
            refs[2 * n + t][...] = (refs[t][...].astype(f32) + refs[n + t][...].astype(f32)).astype(bf16)

    grid_spec = pltpu.PrefetchScalarGridSpec(
        num_scalar_prefetch=1, grid=(4,),
        in_specs=[pl.BlockSpec((None, None) + a.shape[2:], lambda j, core_ref: (j, core_ref[0], 0, 0)) for a in full4s]
        + [pl.BlockSpec((None,) + b.shape[1:], lambda j, core_ref: (j, 0, 0)) for b in bufs],
        out_specs=[pl.BlockSpec((None,) + b.shape[1:], lambda j, core_ref: (j, 0, 0)) for b in bufs])
    return pl.pallas_call(
        body, grid_spec=grid_spec, out_shape=[SDS(b.shape, bf16) for b in bufs],
        compiler_params=_CP, name="pair_sum")(core, *full4s, *bufs)


def _chip_start(sums, slots, after, tag):
    n = len(sums)
    after = list(after) if isinstance(after, (list, tuple)) else [after]

    def body(*refs):
        src, slot = refs[:n], refs[n:2 * n]
        send_sems, recv_sems, token = refs[2 * n + len(after)], refs[2 * n + len(after) + 1], refs[-1]
        my_chip = 2 * lax.axis_index("x") + lax.axis_index("y")
        for k, mask in enumerate((4, 2, 6)):
            dev, _ = _peer(mask)
            for t in range(n):
                pltpu.make_async_remote_copy(
                    src_ref=src[t].at[2 * dev[0] + dev[1]], dst_ref=slot[t].at[my_chip],
                    send_sem=send_sems.at[k * n + t], recv_sem=recv_sems.at[k * n + t],
                    device_id=dev, device_id_type=_MESH).start()
        token[...] = jnp.zeros_like(token)

    outs = pl.pallas_call(
        body, name=f"chip_start_{tag}", in_specs=[_HBM] * (2 * n) + [_ANY] * len(after),
        out_specs=(_SEM, _SEM, *[_HBM] * (2 * n), pl.BlockSpec(memory_space=pltpu.VMEM)),
        out_shape=(pltpu.SemaphoreType.DMA((3 * n,)), pltpu.SemaphoreType.DMA((3 * n,)),
                   *[pltpu.HBM(a.shape, a.dtype) for a in list(sums) + list(slots)], _TOKEN),
        input_output_aliases={t: 2 + t for t in range(2 * n)}, compiler_params=_CP_SPLIT)(
            *[_hbm(a) for a in list(sums) + list(slots)], *after)
    return outs[0], outs[1], list(outs[2:2 + n]), list(outs[2 + n:2 + 2 * n]), outs[-1]


def _chip_wait(sums, slots, send_sems, recv_sems, after, tag):
    n = len(sums)
    after = list(after) if isinstance(after, (list, tuple)) else [after]

    def body(*refs):
        src, slot, send_sems, recv_sems = refs[:n], refs[n:2 * n], refs[2 * n], refs[2 * n + 1]
        for k, mask in enumerate((4, 2, 6)):
            dev, _ = _peer(mask)
            chip = 2 * dev[0] + dev[1]
            for t in range(n):
                cp = pltpu.make_async_remote_copy(
                    src_ref=src[t].at[chip], dst_ref=slot[t].at[chip],
                    send_sem=send_sems.at[k * n + t], recv_sem=recv_sems.at[k * n + t],
                    device_id=_peer(0)[0], device_id_type=_MESH)
                cp.wait_send()
                cp.wait_recv()

    outs = pl.pallas_call(
        body, name=f"chip_wait_{tag}", in_specs=[_HBM] * (2 * n) + [_SEM, _SEM] + [_ANY] * len(after),
        out_specs=tuple([_HBM] * (2 * n)),
        out_shape=tuple(pltpu.HBM(a.shape, a.dtype) for a in list(sums) + list(slots)),
        input_output_aliases={t: t for t in range(2 * n)}, compiler_params=_CP_SPLIT)(
            *sums, *slots, send_sems, recv_sems, *after)
    return list(outs[:n]), list(outs[n:])


def _sum_slots(slots, rb):
    r = slots.shape[1]

    def body(s_ref, o_ref):
        acc = s_ref[0].astype(f32)
        for s in range(1, NDEV):
            acc = acc + s_ref[s].astype(f32)
        o_ref[...] = acc

    return pl.pallas_call(
        body, grid=(r // rb,),
        in_specs=[pl.BlockSpec((NDEV, rb, D), lambda i: (0, i, 0))],
        out_specs=pl.BlockSpec((rb, D), lambda i: (i, 0)),
        out_shape=SDS((r, D), f32), compiler_params=_CP, name="sum_slots")(slots)


def _adamw(w, g, m, v):
    shape = w.shape
    cols = shape[-1]
    rows = w.size // cols
    rb = rows
    for cand in (512, 256, 128, 64, 32, 16, 8):
        if rows % cand == 0 and rows > cand:
            rb = cand
            break

    def body(w_ref, g_ref, m_ref, v_ref, d_ref, mo_ref, vo_ref):
        d_ref[...], mo_ref[...], vo_ref[...] = _adamw_math(w_ref[...], g_ref[...], m_ref[...], v_ref[...])

    spec = pl.BlockSpec((rb, cols), lambda i: (i, 0))
    outs = pl.pallas_call(
        body, grid=(rows // rb,), in_specs=[spec] * 4, out_specs=[spec] * 3,
        out_shape=[SDS((rows, cols), f32)] * 3, compiler_params=_CP, name="adamw")(
            *(a.reshape(rows, cols) for a in (w, g, m, v)))
    return tuple(o.reshape(shape) for o in outs)


def _adamw_math(w, g, m, v):
    m = ADAM_B1 * m + (1.0 - ADAM_B1) * g
    v = ADAM_B2 * v + (1.0 - ADAM_B2) * (g * g)
    m_hat = m / (1.0 - ADAM_B1 ** ADAM_STEP)
    v_hat = v / (1.0 - ADAM_B2 ** ADAM_STEP)
    return -ADAM_LR * (m_hat / (jnp.sqrt(v_hat) + ADAM_EPS) + ADAM_WD * w), m, v


def _reduce_adamw(acc, me, full, slots, w, m, v, l):
    _, r, _ = w.shape
    ns = slots.shape[0]
    rb = r // 2 if r > 128 else r

    def body(me_ref, full_ref, slots_ref, w_ref, m_ref, v_ref, *refs):
        go_ref, d_ref, mo_ref, vo_ref = refs[-4:]
        own = full_ref[...].astype(f32)
        g = None
        for s in range(ns):
            part = jnp.where(me_ref[0] == s, own, slots_ref[s].astype(f32))
            g = part if g is None else g + part
        go_ref[...] = g
        d_ref[...], mo_ref[...], vo_ref[...] = _adamw_math(w_ref[...], g, m_ref[...], v_ref[...])

    steps = r // rb
    lay = pl.BlockSpec((None, rb, D), lambda i, me_ref: (l, i, 0))
    n_acc = 0 if acc is None else 4
    grid_spec = pltpu.PrefetchScalarGridSpec(
        num_scalar_prefetch=1, grid=(steps,),
        in_specs=[pl.BlockSpec((rb, D), lambda i, me_ref: (me_ref[0] * steps + i, 0)),
                  pl.BlockSpec((ns, rb, D), lambda i, me_ref: (0, i, 0)), lay, lay, lay] + [_ANY] * n_acc,
        out_specs=[lay] * 4)
    outs = pl.pallas_call(
        body, grid_spec=grid_spec, out_shape=[SDS(w.shape, f32)] * 4,
        input_output_aliases={6 + j: j for j in range(n_acc)},
        compiler_params=_CP, name="reduce_adamw")(me, full, slots, w, m, v, *(() if acc is None else acc))
    return tuple(outs)


_BIG = ("ffn1_w_gate", "ffn1_w_up", "ffn1_w_down", "w_in", "w_out", "ffn2_w_gate", "ffn2_w_up", "ffn2_w_down")
_TRANSPOSED = ("ffn1_w_gate", "ffn1_w_up", "w_in", "ffn2_w_gate", "ffn2_w_up")

def _block_diag(pool_w):
    out = jnp.zeros((L, PW, PW), pool_w.dtype)
    for gi in range(4):
        out = out.at[:, 64 * gi:64 * (gi + 1), 64 * gi:64 * (gi + 1)].set(pool_w[:, gi])
    return out


def kernel(x, positions, ffn1_norm, ffn1_w_gate, ffn1_w_up, ffn1_w_down, mix_norm, w_in, pool_w, pool_scale, w_out, ffn2_norm, ffn2_w_gate, ffn2_w_up, ffn2_w_down, final_norm, loss_target, m_ffn1_norm, m_ffn1_w_gate, m_ffn1_w_up, m_ffn1_w_down, m_mix_norm, m_w_in, m_pool_w, m_pool_scale, m_w_out, m_ffn2_norm, m_ffn2_w_gate, m_ffn2_w_up, m_ffn2_w_down, m_final_norm, v_ffn1_norm, v_ffn1_w_gate, v_ffn1_w_up, v_ffn1_w_down, v_mix_norm, v_w_in, v_pool_w, v_pool_scale, v_w_out, v_ffn2_norm, v_ffn2_w_gate, v_ffn2_w_up, v_ffn2_w_down, v_final_norm):
    weights = dict(ffn1_norm=ffn1_norm, ffn1_w_gate=ffn1_w_gate, ffn1_w_up=ffn1_w_up, ffn1_w_down=ffn1_w_down,
                   mix_norm=mix_norm, w_in=w_in, pool_w=pool_w, pool_scale=pool_scale, w_out=w_out,
                   ffn2_norm=ffn2_norm, ffn2_w_gate=ffn2_w_gate, ffn2_w_up=ffn2_w_up, ffn2_w_down=ffn2_w_down,
                   final_norm=final_norm)
    moms = dict(ffn1_norm=m_ffn1_norm, ffn1_w_gate=m_ffn1_w_gate, ffn1_w_up=m_ffn1_w_up, ffn1_w_down=m_ffn1_w_down,
                mix_norm=m_mix_norm, w_in=m_w_in, pool_w=m_pool_w, pool_scale=m_pool_scale, w_out=m_w_out,
                ffn2_norm=m_ffn2_norm, ffn2_w_gate=m_ffn2_w_gate, ffn2_w_up=m_ffn2_w_up, ffn2_w_down=m_ffn2_w_down,
                final_norm=m_final_norm)
    vels = dict(ffn1_norm=v_ffn1_norm, ffn1_w_gate=v_ffn1_w_gate, ffn1_w_up=v_ffn1_w_up, ffn1_w_down=v_ffn1_w_down,
                mix_norm=v_mix_norm, w_in=v_w_in, pool_w=v_pool_w, pool_scale=v_pool_scale, w_out=v_w_out,
                ffn2_norm=v_ffn2_norm, ffn2_w_gate=v_ffn2_w_gate, ffn2_w_up=v_ffn2_w_up, ffn2_w_down=v_ffn2_w_down,
                final_norm=v_final_norm)
    names = list(weights)

    me_idx = 4 * lax.axis_index("x") + 2 * lax.axis_index("y") + lax.axis_index("c")
    me_arr = me_idx.reshape(1).astype(jnp.int32)

    as_rows = lambda a, nm: jnp.swapaxes(a, 1, 2) if nm in _TRANSPOSED else a
    w_rows = {nm: as_rows(weights[nm], nm) for nm in _BIG}
    m_rows = {nm: as_rows(moms[nm], nm) for nm in _BIG}
    v_rows = {nm: as_rows(vels[nm], nm) for nm in _BIG}

    def landing_zones(l, which):
        return _place_own(me_arr, [w_rows[_BIG[t]] for t in which], l)

    g_ffn1 = [ffn1_norm[l].reshape(1, D) for l in range(L)]
    g_mix = [mix_norm[l].reshape(1, D) for l in range(L)]
    g_ffn2 = [ffn2_norm[l].reshape(1, D) for l in range(L)]
    wbd_all = _block_diag(pool_w).astype(bf16)
    wbd = [wbd_all[l] for l in range(L)]
    pscale = [pool_scale[l].reshape(1, PW) for l in range(L)]
    tabs = _rope_tables(positions)
    flat = lambda a: a.reshape(S, a.shape[-1])
    r4 = lambda a: a.reshape(4, S // 4, a.shape[-1])
    r16 = lambda a: a.reshape(16, S // 16, a.shape[-1])

    first, rest, whole = (0, 1, 2, 3, 4), (5, 6, 7), tuple(range(8))

    def ag_begin(l, which, after, zones=None):
        tag = f"{l}{'' if which == whole else 'h' if which == first else 'r'}"
        zones = landing_zones(l, which) if zones is None else zones
        send_sems, recv_sems, zones, token = _ag_start(zones, after, tag)
        return dict(tag=tag, zones=zones, s=send_sems, r=recv_sems), token

    def ag_second(ch, after):
        ch["ps"], ch["pr"], ch["zones"], token = _ag_pass(ch["zones"], ch["r"], after, ch["tag"])
        return token

    def ag_third(ch, after):
        ch["qs"], ch["qr"], ch["zones"], token = _ag_last(ch["zones"], ch["pr"], after, ch["tag"])
        return token

    def ag_end(ch, after):
        return _ag_wait(ch["zones"], ch["s"], ch["r"], ch["ps"], ch["pr"], ch["qs"], ch["qr"], after, ch["tag"])

    ch_head, _ = ag_begin(0, first, [])
    zones_rest, zones_next = landing_zones(0, rest), landing_zones(1, whole)
    early_zones = {ll: landing_zones(ll, whole) for ll in range(2, L)}
    fill = [z for zs in (zones_rest, zones_next, *early_zones.values(), tabs, wbd) for z in zs]
    head = ag_end(ch_head, ag_third(ch_head, ag_second(ch_head, fill)))
    ch_rest, tok_rest = ag_begin(0, rest, head[0], zones_rest)
    chains = {}
    chains[1], tok_next = ag_begin(1, whole, head[0], zones_next)
    gathered = [None] * L
    xs = x.reshape(S, D)
    saved = []
    for l in range(L):
        first_after, second_after = (), ()
        if l == 0:
            gt1, ut1, dn1, wint, wout = head
            first_after = (tok_rest, tok_next)
        else:
            gt1, ut1, dn1, wint, wout, gt2, ut2, dn2 = gathered[l]
        x0 = xs
        x1, gate1, up1 = _ffn_fwd(x0, g_ffn1[l], gt1, ut1, dn1, after=first_after)
        hmix, vp, q1, k1, v1, q4, k4, v4, q16, k16, v16 = _mix_in_fwd(x1, g_mix[l], wint, tabs)
        q4, k4, v4, q16, k16, v16 = map(flat, (q4, k4, v4, q16, k16, v16))
        ypool, diff = _pool_fwd(vp, wbd[l], pscale[l])
        after_attn = None
        if l == 0:
            after_attn = ag_second(ch_rest, [ypool, q16])
        o1, l1 = _attn_fwd(q1, k1, v1, S, after=after_attn)
        o4, l4 = _attn_fwd(q4, k4, v4, S // 4, after=after_attn)
        o16, l16 = _attn_fwd(q16, k16, v16, S // 16, after=after_attn)
        if 0 < l < L - 1:
            second_after = (ag_second(chains[l + 1], [o1, o4, o16]),)
        x2, mixed, o, lse1, lse4, lse16 = _mix_out_fwd(x1, ypool, o1, l1, r4(o4), r4(l4), r16(o16), r16(l16), wout)
        if l == 0:
            token = ag_third(ch_rest, x2)
            gt2, ut2, dn2 = ag_end(ch_rest, token)
            gathered[0] = list(head) + [gt2, ut2, dn2]
            second_after = (ag_second(chains[1], gt2),)
        x3, gate2, up2 = _ffn_fwd(x2, g_ffn2[l], gt2, ut2, dn2, after=second_after)
        if l + 1 < L:
            token = ag_third(chains[l + 1], x3)
            if l + 2 < L:
                chains[l + 2], token = ag_begin(l + 2, whole, token, early_zones[l + 2])
            gathered[l + 1] = ag_end(chains[l + 1], token)
        saved.append(dict(x0=x0, x1=x1, x2=x2, gate1=gate1, up1=up1, gate2=gate2, up2=up2, hmix=hmix, diff=diff,
                          qkv=((q1, k1, v1), (q4, k4, v4), (q16, k16, v16)), mixed=mixed, o=o,
                          lse=(lse1, flat(lse4), flat(lse16))))
        xs = x3

    dx, loss_part, d_final = _loss_head(xs, final_norm.reshape(1, D), loss_target.reshape(S, D))

    d_norm = {nm: [None] * L for nm in ("ffn1_norm", "mix_norm", "ffn2_norm")}
    d_poolw, d_pscale = [None] * L, [None] * L
    group_a = ("ffn2_w_gate", "ffn2_w_up", "ffn2_w_down", "w_out")
    group_b = ("ffn1_w_gate", "ffn1_w_up", "ffn1_w_down", "w_in")
    acc = {}

    def exchange(full, group, after, tag):
        srcs = [full[nm] for nm in group]
        slots = [lax.empty((NDEV, g.shape[0] // NDEV, D), bf16) for g in srcs]
        ssem, rsem, srcs, slots, token = _rs_start(srcs, slots, after, tag)
        return (srcs, slots, ssem, rsem, tag), token

    def update(l, group, flight, after):
        srcs, slots, ssem, rsem, tag = flight
        srcs, slots = _rs_wait(srcs, slots, ssem, rsem, after, tag)
        for nm, full_g, slots_g in zip(group, srcs, slots):
            acc[nm] = _reduce_adamw(acc.get(nm), me_arr, full_g, slots_g, w_rows[nm], m_rows[nm], v_rows[nm], l)
        return [acc[nm][0] for nm in group], slots

    core_arr = lax.axis_index("c").reshape(1).astype(jnp.int32)
    chip_arr = (2 * lax.axis_index("x") + lax.axis_index("y")).reshape(1).astype(jnp.int32)

    def exchange_cores(full, group, after, tag):
        full4s = [full[nm].reshape(4, 2, full[nm].shape[0] // NDEV, D) for nm in group]
        bufs = [lax.empty((4,) + a.shape[2:], bf16) for a in full4s]
        ssem, rsem, full4s, bufs, token = _pair_start(full4s, bufs, after, tag)
        return (full4s, bufs, ssem, rsem, tag), token

    def exchange_chips(flight, after):
        full4s, bufs, ssem, rsem, tag = flight
        full4s, bufs = _pair_wait(full4s, bufs, ssem, rsem, after, tag)
        sums = _pair_sum(core_arr, full4s, bufs)
        slots = [lax.empty(a.shape, bf16) for a in sums]
        ssem, rsem, sums, slots, token = _chip_start(sums, slots, bufs[0], tag)
        return (sums, slots, ssem, rsem, tag), token

    def update_chips(l, group, flight, after):
        sums, slots, ssem, rsem, tag = flight
        sums, slots = _chip_wait(sums, slots, ssem, rsem, after, tag)
        for nm, sums_g, slots_g in zip(group, sums, slots):
            own = sums_g.reshape(4 * sums_g.shape[1], D)
            acc[nm] = _reduce_adamw(acc.get(nm), chip_arr, own, slots_g, w_rows[nm], m_rows[nm], v_rows[nm], l)
        return [acc[nm][0] for nm in group]

    flights = {}
    token_b = None
    for l in reversed(range(L)):
        sv = saved[l]
        gt1, ut1, dn1, wint, wout, gt2, ut2, dn2 = gathered[l]
        full = {}
        dx, dgate, dup, h, dy, d_norm["ffn2_norm"][l] = _ffn_bwd_d(
            sv["x2"], g_ffn2[l], sv["gate2"], sv["up2"], dx, gt2, ut2, dn2, after=() if token_b is None else (token_b,))
        full["ffn2_w_gate"], full["ffn2_w_up"], full["ffn2_w_down"] = _ffn_bwd_w(h, dy, sv["gate2"], sv["up2"], dgate, dup)

        dxb, dyp, do1, do4, do16, dl1, dl4, dl16 = _mix_out_bwd(dx, sv["o"], wout)
        full["w_out"] = _wgrad(sv["mixed"], dxb)
        flights[l, "a"], token_a = (exchange_cores if l == 0 else exchange)(full, group_a, dxb, f"a{l}")
        dvp, d_poolw[l], d_pscale[l] = _pool_bwd(dyp, sv["diff"], wbd[l], pscale[l], after=(token_a,))
        dos, dls = (do1, flat(do4), flat(do16)), (dl1, flat(dl4), flat(dl16))
        dqkv = []
        for b, lc in enumerate((S, S // 4, S // 16)):
            qb, kb, vb = sv["qkv"][b]
            dqkv.append(_attn_bwd(qb, kb, vb, dos[b], sv["lse"][b], dls[b], lc))
        d4 = tuple(r4(a) for a in dqkv[1])
        d16 = tuple(r16(a) for a in dqkv[2])
        mix_after = ()
        if l == 0:
            flights[0, "a"], token_a = exchange_chips(flights[0, "a"], [dqkv[0][0], dqkv[1][0], dqkv[2][0]])
            mix_after = (token_a,)
        dx, dproj, d_norm["mix_norm"][l] = _mix_in_bwd(dx, sv["x1"], g_mix[l], wint, tabs, dvp, dqkv[0], d4, d16,
                                                       after=mix_after)
        full["w_in"] = _wgrad(dproj, sv["hmix"])

        dx, dgate, dup, h, dy, d_norm["ffn1_norm"][l] = _ffn_bwd_d(sv["x0"], g_ffn1[l], sv["gate1"], sv["up1"], dx, gt1, ut1, dn1)
        full["ffn1_w_gate"], full["ffn1_w_up"], full["ffn1_w_down"] = _ffn_bwd_w(h, dy, sv["gate1"], sv["up1"], dgate, dup)

        after = dx
        if l + 1 < L and l + 1 >= 2:
            after, _ = update(l + 1, group_a, flights.pop((l + 1, "a")), after)
        if l + 1 < L and l + 1 >= 3:
            after, _ = update(l + 1, group_b, flights.pop((l + 1, "b")), after)
        if l > 0:
            flights[l, "b"], token_b = exchange(full, group_b, after, f"b{l}")

    flights[0, "b"], token_b = exchange_cores(full, group_b, dx, "b0")
    pad8 = lambda a: jnp.pad(a, ((0, 8 - a.shape[0]), (0, 0)))
    misc = jnp.concatenate([d_final, jnp.concatenate(d_pscale, axis=1), loss_part], axis=0)
    small = jnp.concatenate(
        [pad8(jnp.concatenate(d_norm[nm], axis=0)) for nm in ("ffn1_norm", "mix_norm", "ffn2_norm")]
        + [pad8(misc), jnp.stack(d_poolw).reshape(L * 16, D)], axis=0)
    small_slots = lax.dynamic_update_slice(lax.empty((NDEV, SMALL_ROWS, D), f32), small[None], (me_idx, 0, 0))
    pack_sems = _rs_start([small], [small_slots], token_b, "pack")
    flights[0, "b"], token_b = exchange_chips(flights[0, "b"], pack_sems[-1])

    after = token_b
    for key in [(2, "b"), (1, "a"), (1, "b")]:
        after, _ = update(key[0], group_a if key[1] == "a" else group_b, flights.pop(key), after)
    _, pack_slots = _rs_wait(pack_sems[2], pack_sems[3], pack_sems[0], pack_sems[1], after, "pack")
    sm = _sum_slots(pack_slots[0], SMALL_ROWS)
    grads = {}
    grads["ffn1_norm"], grads["mix_norm"], grads["ffn2_norm"] = sm[0:L], sm[8:8 + L], sm[16:16 + L]
    grads["final_norm"] = sm[24]
    grads["pool_scale"] = sm[25].reshape(L, PW)
    grads["pool_w"] = sm[32:32 + L * 16].reshape(L, 4, 64, 64)
    loss = sm[26, 0]
    upd = {nm: _adamw(weights[nm], grads[nm], moms[nm], vels[nm]) for nm in names if nm not in _BIG}
    after = update_chips(0, group_a, flights.pop((0, "a")), [upd[nm][0] for nm in upd])
    update_chips(0, group_b, flights.pop((0, "b")), after)
    for nm in _BIG:
        grads[nm], upd[nm] = as_rows(acc[nm][0], nm), tuple(as_rows(a, nm) for a in acc[nm][1:])
    return (loss, dx.reshape(1, S, D), *[grads[nm] for nm in names], *[upd[nm][0] for nm in names],
            *[upd[nm][1] for nm in names], *[upd[nm][2] for nm in names])
```

```python
import jax
import jax.numpy as jnp
from jax import lax
from jax.experimental import pallas as pl
from jax.experimental.pallas import tpu as pltpu

f32 = jnp.float32
bf16 = jnp.bfloat16
SDS = jax.ShapeDtypeStruct

D = 1024
S = 2048
F = 2816
L = 4
PW = 256
AW = 768
PROJ = PW + 3 * AW
NDEV = 8
TM = 256
QB = 128
HALF = 64
NG = AW // 128
NORM_EPS = 1e-6
MASK_VALUE = -1e30
ROPE_THETA = 500000.0
ADAM_LR, ADAM_B1, ADAM_B2, ADAM_EPS, ADAM_WD, ADAM_STEP = 0.001, 0.9, 0.999, 1e-08, 0.01, 10
POOL_WINDOWS = (2, 4, 8, 16)
PAD = 8
SMALL_ROWS = 96
VMEM_LIMIT = 56 * 1024 * 1024

_CP = pltpu.CompilerParams(vmem_limit_bytes=VMEM_LIMIT)
_ANY = pl.BlockSpec(memory_space=pl.ANY)
_HBM = pl.BlockSpec(memory_space=pltpu.HBM)
_SEM = pl.BlockSpec(memory_space=pltpu.SEMAPHORE)
_MESH = pl.DeviceIdType.MESH
_CP_SPLIT = pltpu.CompilerParams(has_side_effects=pltpu.SideEffectType.DATAFLOW_SIDE_EFFECTING)


def _dot_nn(a, b):
    return lax.dot_general(a, b, (((1,), (0,)), ((), ())), preferred_element_type=f32)


def _dot_nt(a, b):
    return lax.dot_general(a, b, (((1,), (1,)), ((), ())), preferred_element_type=f32)


def _dot_tn(a, b):
    return lax.dot_general(a, b, (((0,), (0,)), ((), ())), preferred_element_type=f32)


def _rms(x, g):
    r = lax.rsqrt(jnp.mean(x * x, axis=-1, keepdims=True) + NORM_EPS)
    xh = x * r
    return r, xh, xh * g


def _rms_bwd(dh, r, xh, g):
    dxh = dh * g
    return r * (dxh - xh * jnp.mean(dxh * xh, axis=-1, keepdims=True))


def _tile(cols, rows=TM):
    return pl.BlockSpec((rows, cols), lambda i: (i, 0))


def _const(shape):
    return pl.BlockSpec(shape, lambda i: (0,) * len(shape))


def _layer(rows, cols):
    return pl.BlockSpec((rows, cols), lambda i: (0, 0), pipeline_mode=pl.Buffered(1))


def _p4(cols=AW):
    return pl.BlockSpec((4, TM // 4, cols), lambda i: (0, i, 0))


def _p16(cols=AW):
    return pl.BlockSpec((16, TM // 16, cols), lambda i: (0, i, 0))


def _cols(j):
    return slice(128 * j, 128 * (j + 1))


def _follow(body, n_in, after):
    k = len(after)
    return body if k == 0 else (lambda *refs: body(*refs[:n_in], *refs[n_in + k:]))


def _ffn_fwd(x, g, gt, ut, dn, after=()):
    def body(x_ref, g_ref, gt_ref, ut_ref, dn_ref, xo_ref, gate_ref, up_ref):
        x = x_ref[...]
        _, _, hn = _rms(x, g_ref[...])
        h = hn.astype(bf16)
        gate = _dot_nt(h, gt_ref[...])
        up = _dot_nt(h, ut_ref[...])
        gate_ref[...] = gate.astype(bf16)
        up_ref[...] = up.astype(bf16)
        a = (gate * jax.nn.sigmoid(gate) * up).astype(bf16)
        xo_ref[...] = x + 0.5 * _dot_nn(a, dn_ref[...])

    rows = 2 * TM
    return pl.pallas_call(
        _follow(body, 5, after), grid=(S // rows,),
        in_specs=[_tile(D, rows), _layer(1, D), _layer(F, D), _layer(F, D), _layer(F, D)] + [_ANY] * len(after),
        out_specs=[_tile(D, rows), _tile(F, rows), _tile(F, rows)],
        out_shape=[SDS((S, D), f32), SDS((S, F), bf16), SDS((S, F), bf16)],
        compiler_params=_CP, name="ffn_fwd")(x, g, gt, ut, dn, *after)


def _ffn_bwd_d(x, g, gate, up, dxo, gt, ut, dn, after=()):
    def body(x_ref, g_ref, gate_ref, up_ref, dxo_ref, gt_ref, ut_ref, dn_ref,
             dx_ref, dgate_ref, dup_ref, h_ref, dy_ref, dg_ref):
        x = x_ref[...]
        g = g_ref[...]
        r, xh, hn = _rms(x, g)
        h_ref[...] = hn.astype(bf16)
        dxo = dxo_ref[...]
        dy = (0.5 * dxo).astype(bf16)
        dy_ref[...] = dy
        da = _dot_nt(dy, dn_ref[...])
        gate = gate_ref[...].astype(f32)
        up = up_ref[...].astype(f32)
        sg = jax.nn.sigmoid(gate)
        dgate = (da * up * (sg * (1.0 + gate * (1.0 - sg)))).astype(bf16)
        dup = (da * (gate * sg)).astype(bf16)
        dgate_ref[...] = dgate
        dup_ref[...] = dup
        dh = _dot_nn(dgate, gt_ref[...]) + _dot_nn(dup, ut_ref[...])

        @pl.when(pl.program_id(0) == 0)
        def _():
            dg_ref[...] = jnp.zeros_like(dg_ref)

        dg_ref[...] += jnp.sum(dh * xh, axis=0, keepdims=True)
        dx_ref[...] = dxo + _rms_bwd(dh, r, xh, g)

    return pl.pallas_call(
        _follow(body, 8, after), grid=(S // TM,),
        in_specs=[_tile(D), _layer(1, D), _tile(F), _tile(F), _tile(D),
                  _layer(F, D), _layer(F, D), _layer(F, D)] + [_ANY] * len(after),
        out_specs=[_tile(D), _tile(F), _tile(F), _tile(D), _tile(D), _const((1, D))],
        out_shape=[SDS((S, D), f32), SDS((S, F), bf16), SDS((S, F), bf16), SDS((S, D), bf16),
                   SDS((S, D), bf16), SDS((1, D), f32)],
        compiler_params=_CP, name="ffn_bwd_d")(x, g, gate, up, dxo, gt, ut, dn, *after)


def _ffn_bwd_w(h, dy, gate, up, dgate, dup):
    fc = 256

    def body(h_ref, dy_ref, gate_ref, up_ref, dgate_ref, dup_ref, dgt_ref, dut_ref, ddn_ref):
        gate = gate_ref[...].astype(f32)
        a = (gate * jax.nn.sigmoid(gate) * up_ref[...].astype(f32)).astype(bf16)
        ddn_ref[...] = _dot_tn(a, dy_ref[...]).astype(bf16)
        h = h_ref[...]
        dgt_ref[...] = _dot_tn(dgate_ref[...], h).astype(bf16)
        dut_ref[...] = _dot_tn(dup_ref[...], h).astype(bf16)

    col = pl.BlockSpec((S, fc), lambda j: (0, j))
    row = pl.BlockSpec((fc, D), lambda j: (j, 0))
    full = pl.BlockSpec((S, D), lambda j: (0, 0))
    return pl.pallas_call(
        body, grid=(F // fc,),
        in_specs=[full, full, col, col, col, col],
        out_specs=[row, row, row],
        out_shape=[SDS((F, D), bf16)] * 3,
        compiler_params=_CP, name="ffn_bwd_w")(h, dy, gate, up, dgate, dup)


def _wgrad(a, b):
    m, n = a.shape[1], b.shape[1]
    mc = 2 * TM

    def body(a_ref, b_ref, o_ref):
        o_ref[...] = _dot_tn(a_ref[...], b_ref[...]).astype(bf16)

    return pl.pallas_call(
        body, grid=(m // mc,),
        in_specs=[pl.BlockSpec((S, mc), lambda j: (0, j)), pl.BlockSpec((S, n), lambda j: (0, 0))],
        out_specs=pl.BlockSpec((mc, n), lambda j: (j, 0)),
        out_shape=SDS((m, n), bf16),
        compiler_params=_CP, name="wgrad")(a, b)


def _rope(t, c, sn, sp):
    return t * c + pltpu.roll(t, 120, 1) * sn + pltpu.roll(t, 8, 1) * sp


def _rope_bwd(d, c, sn, sp):
    return d * c + pltpu.roll(d * sn, 8, 1) + pltpu.roll(d * sp, 120, 1)


def _rope_tables(positions):
    inv_freq = ROPE_THETA ** (-jnp.arange(0, 16, 2, dtype=f32) / 16)
    ang = positions.reshape(S, 1).astype(f32) * inv_freq
    cos, sin = jnp.cos(ang), jnp.sin(ang)
    one = jnp.ones((S, 48), f32)
    zero8 = jnp.zeros((S, 8), f32)
    zero48 = jnp.zeros((S, 48), f32)
    c = jnp.concatenate([cos, cos, one], axis=1)
    sn = jnp.concatenate([-sin, zero8, zero48], axis=1)
    sp = jnp.concatenate([zero8, sin, zero48], axis=1)
    return tuple(jnp.concatenate([t, t], axis=1) for t in (c, sn, sp))


def _dilation_perm(n, back=False):
    per = TM // n
    i = lax.broadcasted_iota(jnp.int32, (TM, TM), 1 if back else 0)
    j = lax.broadcasted_iota(jnp.int32, (TM, TM), 0 if back else 1)
    return jnp.where(j == n * (i % per) + i // per, 1.0, 0.0).astype(bf16)


def _mix_in_fwd(x, g, wint, tabs):
    def body(x_ref, g_ref, w_ref, c_ref, sn_ref, sp_ref,
             h_ref, vp_ref, q1, k1, v1, q4, k4, v4, q16, k16, v16):
        _, _, hn = _rms(x_ref[...], g_ref[...])
        h = hn.astype(bf16)
        h_ref[...] = h
        proj = _dot_nt(h, w_ref[...])
        vp_ref[...] = proj[:, :PW]
        c, sn, sp = c_ref[...], sn_ref[...], sp_ref[...]
        perm4, perm16 = _dilation_perm(4), _dilation_perm(16)
        for kind, (o1, o4, o16) in enumerate(((q1, q4, q16), (k1, k4, k16), (v1, v4, v16))):
            for j in range(NG):
                t = proj[:, PW + kind * AW + 128 * j: PW + kind * AW + 128 * (j + 1)]
                if kind == 0:
                    t = _rope(t, c, sn, sp) * 0.125
                elif kind == 1:
                    t = _rope(t, c, sn, sp)
                o1[:, _cols(j)] = t.astype(bf16)
            nat = o1[...]
            o4[...] = _dot_nn(perm4, nat).astype(bf16).reshape(4, TM // 4, AW)
            o16[...] = _dot_nn(perm16, nat).astype(bf16).reshape(16, TM // 16, AW)

    nat, d4, d16 = SDS((S, AW), bf16), SDS((4, S // 4, AW), bf16), SDS((16, S // 16, AW), bf16)
    return pl.pallas_call(
        body, grid=(S // TM,),
        in_specs=[_tile(D), _layer(1, D), _layer(PROJ, D), _tile(128), _tile(128), _tile(128)],
        out_specs=[_tile(D), _tile(PW)] + [_tile(AW)] * 3 + [_p4()] * 3 + [_p16()] * 3,
        out_shape=[SDS((S, D), bf16), SDS((S, PW), f32)] + [nat] * 3 + [d4] * 3 + [d16] * 3,
        compiler_params=_CP, name="mix_in_fwd")(x, g, wint, *tabs)


def _mix_in_bwd(dxo, x, g, wint, tabs, dvp, d1, d4, d16, after=()):
    def body(dxo_ref, x_ref, g_ref, w_ref, c_ref, sn_ref, sp_ref, dvp_ref,
             dq1, dk1, dv1, dq4, dk4, dv4, dq16, dk16, dv16,
             dx_ref, dproj_ref, dg_ref):
        c, sn, sp = c_ref[...], sn_ref[...], sp_ref[...]
        dproj_ref[:, :PW] = dvp_ref[...].astype(bf16)
        back4, back16 = _dilation_perm(4, True), _dilation_perm(16, True)
        for kind, (a1, a4, a16) in enumerate(((dq1, dq4, dq16), (dk1, dk4, dk16), (dv1, dv4, dv16))):
            n4 = _dot_nn(back4, a4[...].reshape(TM, AW))
            n16 = _dot_nn(back16, a16[...].reshape(TM, AW))
            for j in range(NG):
                t = a1[:, _cols(j)].astype(f32) + n4[:, _cols(j)] + n16[:, _cols(j)]
                if kind == 0:
                    t = _rope_bwd(t * 0.125, c, sn, sp)
                elif kind == 1:
                    t = _rope_bwd(t, c, sn, sp)
                dproj_ref[:, PW + kind * AW + 128 * j: PW + kind * AW + 128 * (j + 1)] = t.astype(bf16)
        g = g_ref[...]
        r_, xh, _ = _rms(x_ref[...], g)
        dh = _dot_nn(dproj_ref[...], w_ref[...])

        @pl.when(pl.program_id(0) == 0)
        def _():
            dg_ref[...] = jnp.zeros_like(dg_ref)

        dg_ref[...] += jnp.sum(dh * xh, axis=0, keepdims=True)
        dx_ref[...] = dxo_ref[...] + _rms_bwd(dh, r_, xh, g)

    return pl.pallas_call(
        _follow(body, 17, after), grid=(S // TM,),
        in_specs=[_tile(D), _tile(D), _layer(1, D), _layer(PROJ, D), _tile(128), _tile(128), _tile(128),
                  _tile(PW)] + [_tile(AW)] * 3 + [_p4()] * 3 + [_p16()] * 3 + [_ANY] * len(after),
        out_specs=[_tile(D), _tile(PROJ), _const((1, D))],
        out_shape=[SDS((S, D), f32), SDS((S, PROJ), bf16), SDS((1, D), f32)],
        compiler_params=_CP, name="mix_in_bwd")(dxo, x, g, wint, *tabs, dvp, *d1, *d4, *d16, *after)


def _pool_sums(pad_ref, base, rows, adjoint):
    lane_group = lax.broadcasted_iota(jnp.int32, (rows, PW), 1) // 64
    sign = -1 if adjoint else 1

    def sh(o):
        return pad_ref[pl.ds(PAD + base + sign * o, rows), :]

    out = None
    acc = None
    lo, hi = 0, 0
    for gi, w in enumerate(POOL_WINDOWS):
        for o in list(range(-(w // 2), lo)) + list(range(hi, w - w // 2)):
            acc = sh(o) if acc is None else acc + sh(o)
        lo, hi = -(w // 2), w - w // 2
        out = acc if out is None else jnp.where(lane_group >= gi, acc, out)
    return out


def _pool_counts(base, rows):
    pos = base + lax.broadcasted_iota(jnp.int32, (rows, PW), 0)
    lane_group = lax.broadcasted_iota(jnp.int32, (rows, PW), 1) // 64
    cnt = None
    for gi, w in enumerate(POOL_WINDOWS):
        lo = jnp.maximum(pos - w // 2, 0)
        hi = jnp.minimum(pos + w - 1 - w // 2, S - 1)
        c = (hi - lo + 1).astype(f32)
        cnt = c if cnt is None else jnp.where(lane_group >= gi, c, cnt)
    return cnt


def _pool_fwd(vp, wbd, scale):
    ch = 256

    def body(vp_ref, w_ref, sc_ref, y_ref, diff_ref, pad):
        pad[pl.ds(0, PAD), :] = jnp.zeros((PAD, PW), f32)
        pad[pl.ds(PAD + S, PAD), :] = jnp.zeros((PAD, PW), f32)
        pad[pl.ds(PAD, S), :] = vp_ref[...]
        for b in range(S // ch):
            base = b * ch
            pooled = _pool_sums(pad, base, ch, False) / _pool_counts(base, ch)
            diff = (pooled - vp_ref[pl.ds(base, ch), :]).astype(bf16)
            diff_ref[pl.ds(base, ch), :] = diff
            y_ref[pl.ds(base, ch), :] = _dot_nn(diff, w_ref[...]) * sc_ref[...]

    whole = lambda shape: pl.BlockSpec(shape, lambda i: (0,) * len(shape))
    return pl.pallas_call(
        body, grid=(1,),
        in_specs=[whole((S, PW)), whole((PW, PW)), whole((1, PW))],
        out_specs=[whole((S, PW)), whole((S, PW))],
        out_shape=[SDS((S, PW), f32), SDS((S, PW), bf16)],
        scratch_shapes=[pltpu.VMEM((S + 2 * PAD, PW), f32)],
        compiler_params=_CP, name="pool_fwd")(vp, wbd, scale)


def _pool_bwd(dy, diff, wbd, scale, after=()):
    ch = 256

    def body(dy_ref, diff_ref, w_ref, sc_ref, dvp_ref, dw_ref, dsc_ref, pad):
        pad[pl.ds(0, PAD), :] = jnp.zeros((PAD, PW), f32)
        pad[pl.ds(PAD + S, PAD), :] = jnp.zeros((PAD, PW), f32)
        dw = jnp.zeros((PW, PW), f32)
        dsc = jnp.zeros((1, PW), f32)
        for b in range(S // ch):
            base = b * ch
            dy = dy_ref[pl.ds(base, ch), :]
            diff = diff_ref[pl.ds(base, ch), :]
            dsc = dsc + jnp.sum(dy * _dot_nn(diff, w_ref[...]), axis=0, keepdims=True)
            dz = (dy * sc_ref[...]).astype(bf16)
            dw = dw + _dot_tn(diff, dz)
            ddiff = _dot_nt(dz, w_ref[...])
            dvp_ref[pl.ds(base, ch), :] = -ddiff
            pad[pl.ds(PAD + base, ch), :] = ddiff / _pool_counts(base, ch)
        for gi in range(4):
            dw_ref[gi] = dw[64 * gi:64 * (gi + 1), 64 * gi:64 * (gi + 1)]
        dsc_ref[...] = dsc
        for b in range(S // ch):
            base = b * ch
            dvp_ref[pl.ds(base, ch), :] += _pool_sums(pad, base, ch, True)

    whole = lambda shape: pl.BlockSpec(shape, lambda i: (0,) * len(shape))
    return pl.pallas_call(
        _follow(body, 4, after), grid=(1,),
        in_specs=[whole((S, PW)), whole((S, PW)), whole((PW, PW)), whole((1, PW))] + [_ANY] * len(after),
        out_specs=[whole((S, PW)), whole((4, 64, 64)), whole((1, PW))],
        out_shape=[SDS((S, PW), f32), SDS((4, 64, 64), f32), SDS((1, PW), f32)],
        scratch_shapes=[pltpu.VMEM((S + 2 * PAD, PW), f32)],
        compiler_params=_CP, name="pool_bwd")(dy, diff, wbd, scale, *after)


def _attn_blocks(lc):
    bpc = lc // QB
    kw = min(2 * QB, lc)
    blocks = []
    for b in range(S // QB):
        t0 = (b % bpc) * QB
        ks_in = min(max(t0 - HALF, 0), lc - kw)
        blocks.append((b * QB, (b // bpc) * lc + ks_in, t0 - ks_in))
    return kw, blocks


def _attn_bias(bias_ref, kw, shifts):
    r = lax.broadcasted_iota(jnp.int32, (2 * QB, kw), 0) % QB
    c = lax.broadcasted_iota(jnp.int32, (2 * QB, kw), 1)
    for i, shift in enumerate(shifts):
        bias_ref[i] = jnp.where(jnp.abs(r + shift - c) <= HALF, 0.0, MASK_VALUE).astype(f32)


def _head_put(stats, pair, v0, v1, lane):
    return jnp.where(lane == 2 * pair, v0, jnp.where(lane == 2 * pair + 1, v1, stats))


def _head_cols(stats, pair, lane):
    c0 = jnp.sum(jnp.where(lane == 2 * pair, stats, 0.0), axis=-1, keepdims=True)
    c1 = jnp.sum(jnp.where(lane == 2 * pair + 1, stats, 0.0), axis=-1, keepdims=True)
    return jnp.concatenate([c0, c1], axis=0)


def _head_spread(stats, pair, head0):
    return jnp.where(head0, stats[:, 2 * pair:2 * pair + 1], stats[:, 2 * pair + 1:2 * pair + 2])


def _stack_heads(blk, head0):
    zero = jnp.zeros_like(blk)
    return jnp.concatenate([jnp.where(head0, blk, zero), jnp.where(head0, zero, blk)], axis=0)


def _attn_fwd(q, k, v, lc, after=None):
    kw, blocks = _attn_blocks(lc)
    shifts = sorted({b[2] for b in blocks})

    def body(q_ref, k_ref, v_ref, *refs):
        o_ref, lse_ref, bias_ref = refs[-3:]
        lane = lax.broadcasted_iota(jnp.int32, (QB, 128), 1)
        head0 = lane < 64
        pair = pl.program_id(0)
        _attn_bias(bias_ref, kw, shifts)

        @pl.when(pair == 0)
        def _():
            lse_ref[...] = jnp.zeros_like(lse_ref)

        for row0, kstart, shift in blocks:
            q2 = _stack_heads(q_ref[pl.ds(row0, QB), :], head0)
            kb = k_ref[pl.ds(kstart, kw), :]
            vb = v_ref[pl.ds(kstart, kw), :]
            s = _dot_nt(q2, kb) + bias_ref[shifts.index(shift)]
            m = jnp.max(s, axis=-1, keepdims=True)
            p = jnp.exp(s - m)
            den = jnp.sum(p, axis=-1, keepdims=True)
            o2 = _dot_nn(p.astype(bf16), vb) / den
            lse2 = m + jnp.log(den)
            o_ref[pl.ds(row0, QB), :] = jnp.where(head0, o2[:QB], o2[QB:]).astype(bf16)
            lse_ref[pl.ds(row0, QB), :] = _head_put(lse_ref[pl.ds(row0, QB), :], pair, lse2[:QB], lse2[QB:], lane)

    col = pl.BlockSpec((S, 128), lambda p: (0, p))
    extra = () if after is None else (after,)
    return pl.pallas_call(
        body, grid=(NG,), in_specs=[col, col, col] + [_ANY] * len(extra),
        out_specs=[col, pl.BlockSpec((S, 128), lambda p: (0, 0))],
        out_shape=[SDS((S, AW), bf16), SDS((S, 128), f32)],
        scratch_shapes=[pltpu.VMEM((len(shifts), 2 * QB, kw), f32)],
        compiler_params=_CP, name=f"attn_fwd_{lc}")(q, k, v, *extra)


def _attn_bwd(q, k, v, do, lse, delta, lc):
    kw, blocks = _attn_blocks(lc)
    shifts = sorted({b[2] for b in blocks})

    def body(q_ref, k_ref, v_ref, do_ref, lse_ref, dl_ref, dq_ref, dk_out, dv_out, bias_ref, dk_ref, dv_ref):
        lane = lax.broadcasted_iota(jnp.int32, (QB, 128), 1)
        head0 = lane < 64
        pair = pl.program_id(0)
        _attn_bias(bias_ref, kw, shifts)
        dk_ref[...] = jnp.zeros_like(dk_ref)
        dv_ref[...] = jnp.zeros_like(dv_ref)
        for row0, kstart, shift in blocks:
            q2 = _stack_heads(q_ref[pl.ds(row0, QB), :], head0)
            do2 = _stack_heads(do_ref[pl.ds(row0, QB), :], head0)
            lse2 = _head_cols(lse_ref[pl.ds(row0, QB), :], pair, lane)
            dl2 = _head_cols(dl_ref[pl.ds(row0, QB), :], pair, lane)
            kb = k_ref[pl.ds(kstart, kw), :]
            vb = v_ref[pl.ds(kstart, kw), :]
            p = jnp.exp(_dot_nt(q2, kb) + bias_ref[shifts.index(shift)] - lse2)
            ds = (p * (_dot_nt(do2, vb) - dl2)).astype(bf16)
            dq2 = _dot_nn(ds, kb)
            dq_ref[pl.ds(row0, QB), :] = jnp.where(head0, dq2[:QB], dq2[QB:]).astype(bf16)
            dk_ref[pl.ds(kstart, kw), :] += _dot_tn(ds, q2)
            dv_ref[pl.ds(kstart, kw), :] += _dot_tn(p.astype(bf16), do2)
        dk_out[...] = dk_ref[...].astype(bf16)
        dv_out[...] = dv_ref[...].astype(bf16)

    col = pl.BlockSpec((S, 128), lambda p: (0, p))
    stats = pl.BlockSpec((S, 128), lambda p: (0, 0))
    return pl.pallas_call(
        body, grid=(NG,), in_specs=[col] * 4 + [stats] * 2, out_specs=[col] * 3,
        out_shape=[SDS((S, AW), bf16)] * 3,
        scratch_shapes=[pltpu.VMEM((len(shifts), 2 * QB, kw), f32), pltpu.VMEM((S, 128), f32),
                        pltpu.VMEM((S, 128), f32)],
        compiler_params=_CP, name=f"attn_bwd_{lc}")(q, k, v, do, lse, delta)


def _mix_out_fwd(x, ypool, o1, l1, o4, l4, o16, l16, wout):
    def body(x_ref, yp_ref, o1_ref, l1_ref, o4_ref, l4_ref, o16_ref, l16_ref, w_ref,
             xo_ref, mixed_ref, o_ref, lse1_ref, lse4_ref, lse16_ref, sl4, sl16, sl):
        head0 = lax.broadcasted_iota(jnp.int32, (TM, 128), 1) < 64
        for r in range(4):
            sl4[pl.ds(r, TM // 4, stride=4), :] = l4_ref[r]
        for r in range(16):
            sl16[pl.ds(r, TM // 16, stride=16), :] = l16_ref[r]
        n4 = _dot_nn(_dilation_perm(4, True), o4_ref[...].reshape(TM, AW))
        n16 = _dot_nn(_dilation_perm(16, True), o16_ref[...].reshape(TM, AW))
        a, b, c = l1_ref[...], sl4[...], sl16[...]
        m = jnp.maximum(jnp.maximum(a, b), c)
        wa, wb, wc = jnp.exp(a - m), jnp.exp(b - m), jnp.exp(c - m)
        den = wa + wb + wc
        wa, wb, wc = wa / den, wb / den, wc / den
        lse = m + jnp.log(den)
        lse1_ref[...] = lse
        sl[...] = lse
        mixed_ref[:, :PW] = yp_ref[...].astype(bf16)
        for j in range(NG):
            y = (_head_spread(wa, j, head0) * o1_ref[:, _cols(j)].astype(f32)
                 + _head_spread(wb, j, head0) * n4[:, _cols(j)] + _head_spread(wc, j, head0) * n16[:, _cols(j)])
            o_ref[:, _cols(j)] = y
            mixed_ref[:, PW + 128 * j: PW + 128 * (j + 1)] = y.astype(bf16)
        for r in range(4):
            lse4_ref[r] = sl[pl.ds(r, TM // 4, stride=4), :]
        for r in range(16):
            lse16_ref[r] = sl[pl.ds(r, TM // 16, stride=16), :]
        xo_ref[...] = x_ref[...] + _dot_nn(mixed_ref[...], w_ref[...])

    return pl.pallas_call(
        body, grid=(S // TM,),
        in_specs=[_tile(D), _tile(PW), _tile(AW), _tile(128), _p4(), _p4(128), _p16(), _p16(128), _layer(D, D)],
        out_specs=[_tile(D), _tile(D), _tile(AW), _tile(128), _p4(128), _p16(128)],
        out_shape=[SDS((S, D), f32), SDS((S, D), bf16), SDS((S, AW), f32), SDS((S, 128), f32),
                   SDS((4, S // 4, 128), f32), SDS((16, S // 16, 128), f32)],
        scratch_shapes=[pltpu.VMEM((TM, 128), f32)] * 3,
        compiler_params=_CP, name="mix_out_fwd")(x, ypool, o1, l1, o4, l4, o16, l16, wout)


def _mix_out_bwd(dxo, o, wout):
    def body(dxo_ref, o_ref, w_ref, dxb_ref, dyp_ref, do1, do4, do16, dl1, dl4, dl16, sdl):
        dxb = dxo_ref[...].astype(bf16)
        dxb_ref[...] = dxb
        dm = _dot_nt(dxb, w_ref[...])
        dyp_ref[...] = dm[:, :PW]
        lane = lax.broadcasted_iota(jnp.int32, (TM, 128), 1)
        head0 = lane < 64
        dl = jnp.zeros((TM, 128), f32)
        for j in range(NG):
            d = dm[:, PW + 128 * j: PW + 128 * (j + 1)]
            prod = d * o_ref[:, _cols(j)]
            dl = _head_put(dl, j, jnp.sum(jnp.where(head0, prod, 0.0), axis=-1, keepdims=True),
                           jnp.sum(jnp.where(head0, 0.0, prod), axis=-1, keepdims=True), lane)
            do1[:, _cols(j)] = d.astype(bf16)
        dl1[...] = dl
        sdl[...] = dl
        for r in range(4):
            dl4[r] = sdl[pl.ds(r, TM // 4, stride=4), :]
        for r in range(16):
            dl16[r] = sdl[pl.ds(r, TM // 16, stride=16), :]
        nat = do1[...]
        do4[...] = _dot_nn(_dilation_perm(4), nat).astype(bf16).reshape(4, TM // 4, AW)
        do16[...] = _dot_nn(_dilation_perm(16), nat).astype(bf16).reshape(16, TM // 16, AW)

    return pl.pallas_call(
        body, grid=(S // TM,),
        in_specs=[_tile(D), _tile(AW), _layer(D, D)],
        out_specs=[_tile(D), _tile(PW), _tile(AW), _p4(), _p16(), _tile(128), _p4(128), _p16(128)],
        out_shape=[SDS((S, D), bf16), SDS((S, PW), f32),
                   SDS((S, AW), bf16), SDS((4, S // 4, AW), bf16), SDS((16, S // 16, AW), bf16),
                   SDS((S, 128), f32), SDS((4, S // 4, 128), f32), SDS((16, S // 16, 128), f32)],
        scratch_shapes=[pltpu.VMEM((TM, 128), f32)],
        compiler_params=_CP, name="mix_out_bwd")(dxo, o, wout)


def _loss_head(x, g, target):
    def body(x_ref, g_ref, t_ref, dx_ref, loss_ref, dg_ref):
        g = g_ref[...]
        r, xh, y = _rms(x_ref[...], g)
        err = y - t_ref[...]
        dy = err * (1.0 / D)

        @pl.when(pl.program_id(0) == 0)
        def _():
            loss_ref[...] = jnp.zeros_like(loss_ref)
            dg_ref[...] = jnp.zeros_like(dg_ref)

        loss_ref[...] += jnp.broadcast_to(0.5 * jnp.sum(jnp.mean(err * err, axis=-1, keepdims=True)), (1, D))
        dg_ref[...] += jnp.sum(dy * xh, axis=0, keepdims=True)
        dx_ref[...] = _rms_bwd(dy, r, xh, g)

    return pl.pallas_call(
        body, grid=(S // TM,),
        in_specs=[_tile(D), _const((1, D)), _tile(D)],
        out_specs=[_tile(D), _const((1, D)), _const((1, D))],
        out_shape=[SDS((S, D), f32), SDS((1, D), f32), SDS((1, D), f32)],
        compiler_params=_CP, name="loss_head")(x, g, target)


def _peer(k):
    x, y, c = lax.axis_index("x"), lax.axis_index("y"), lax.axis_index("c")
    px = 1 - x if k & 4 else x
    py = 1 - y if k & 2 else y
    pc = 1 - c if k & 1 else c
    return (px, py, pc), 4 * px + 2 * py + pc


def _diag_route():
    x, y, c = lax.axis_index("x"), lax.axis_index("y"), lax.axis_index("c")
    idx_x, idx_y = _peer(4)[1], _peer(2)[1]
    return idx_x + c * (idx_y - idx_x), (x + c * (1 - 2 * x), (1 - y) + c * (2 * y - 1), c)


def _hbm(a):
    return pltpu.with_memory_space_constraint(a, pltpu.HBM)


def _rows(ref, idx):
    r = ref.shape[0] // NDEV
    return ref.at[pl.ds(idx * r, r), :]


def _row_copy(ref, idx, send_sem, recv_sem, to):
    return pltpu.make_async_remote_copy(src_ref=_rows(ref, idx), dst_ref=_rows(ref, idx), send_sem=send_sem,
                                        recv_sem=recv_sem, device_id=to, device_id_type=_MESH)


def _place_own(me, shards, l):
    n = len(shards)

    def body(me_ref, *refs):
        for t in range(n):
            refs[n + t][...] = refs[t][...].astype(bf16)

    grid_spec = pltpu.PrefetchScalarGridSpec(
        num_scalar_prefetch=1, grid=(1,),
        in_specs=[pl.BlockSpec((None, s.shape[1], D), lambda i, me_ref: (l, 0, 0)) for s in shards],
        out_specs=[pl.BlockSpec((s.shape[1], D), lambda i, me_ref: (me_ref[0], 0)) for s in shards])
    return pl.pallas_call(
        body, grid_spec=grid_spec, out_shape=[SDS((NDEV * s.shape[1], D), bf16) for s in shards],
        compiler_params=_CP, name="place_own")(me, *shards)


_TOKEN = SDS((8, 128), f32)
def _ag_start(lands, after, l):
    n = len(lands)
    after = list(after) if isinstance(after, (list, tuple)) else [after]

    def body(*refs):
        zones, send_sems, recv_sems, token = refs[:n], refs[n + len(after)], refs[n + len(after) + 1], refs[-1]
        _, me_idx = _peer(0)
        for k, mask in enumerate((1, 4, 2)):
            for t in range(n):
                _row_copy(zones[t], me_idx, send_sems.at[k * n + t], recv_sems.at[k * n + t], _peer(mask)[0]).start()
        token[...] = jnp.zeros_like(token)

    outs = pl.pallas_call(
        body, name=f"ag_start_{l}", in_specs=[_HBM] * n + [_ANY] * len(after),
        out_specs=(_SEM, _SEM, *[_HBM] * n, pl.BlockSpec(memory_space=pltpu.VMEM)),
        out_shape=(pltpu.SemaphoreType.DMA((3 * n,)), pltpu.SemaphoreType.DMA((3 * n,)),
                   *[pltpu.HBM(a.shape, a.dtype) for a in lands], _TOKEN),
        input_output_aliases={t: 2 + t for t in range(n)}, compiler_params=_CP_SPLIT)(
            *[_hbm(a) for a in lands], *after)
    return outs[0], outs[1], list(outs[2:2 + n]), outs[-1]


def _ag_pass(lands, recv_sems, after, l):
    n = len(lands)
    after = list(after) if isinstance(after, (list, tuple)) else [after]

    def body(*refs):
        zones, recv_sems = refs[:n], refs[n]
        psend, precv, token = refs[n + 1 + len(after)], refs[n + 2 + len(after)], refs[-1]
        me, _ = _peer(0)
        sibling, _ = _peer(1)
        for j, mask in enumerate((4, 2)):
            idx = _peer(mask)[1]
            for t in range(n):
                _row_copy(zones[t], idx, psend.at[j * n + t], recv_sems.at[(1 + j) * n + t], me).wait_recv()
                _row_copy(zones[t], idx, psend.at[j * n + t], precv.at[j * n + t], sibling).start()
        fwd_idx, fwd_dev = _diag_route()
        for t in range(n):
            _row_copy(zones[t], fwd_idx, psend.at[2 * n + t], precv.at[2 * n + t], fwd_dev).start()
        token[...] = jnp.zeros_like(token)

    outs = pl.pallas_call(
        body, name=f"ag_pass_{l}", in_specs=[_HBM] * n + [_SEM] + [_ANY] * len(after),
        out_specs=(_SEM, _SEM, *[_HBM] * n, pl.BlockSpec(memory_space=pltpu.VMEM)),
        out_shape=(pltpu.SemaphoreType.DMA((3 * n,)), pltpu.SemaphoreType.DMA((3 * n,)),
                   *[pltpu.HBM(a.shape, a.dtype) for a in lands], _TOKEN),
        input_output_aliases={t: 2 + t for t in range(n)}, compiler_params=_CP_SPLIT)(*lands, recv_sems, *after)
    return outs[0], outs[1], list(outs[2:2 + n]), outs[-1]


def _ag_last(lands, precv, after, l):
    n = len(lands)
    after = list(after) if isinstance(after, (list, tuple)) else [after]

    def body(*refs):
        zones, precv = refs[:n], refs[n]
        qsend, qrecv, token = refs[n + 1 + len(after)], refs[n + 2 + len(after)], refs[-1]
        me, _ = _peer(0)
        sibling, _ = _peer(1)
        idx = _peer(6)[1]
        for t in range(n):
            _row_copy(zones[t], idx, qsend.at[t], precv.at[2 * n + t], me).wait_recv()
            _row_copy(zones[t], idx, qsend.at[t], qrecv.at[t], sibling).start()
        token[...] = jnp.zeros_like(token)

    outs = pl.pallas_call(
        body, name=f"ag_last_{l}", in_specs=[_HBM] * n + [_SEM] + [_ANY] * len(after),
        out_specs=(_SEM, _SEM, *[_HBM] * n, pl.BlockSpec(memory_space=pltpu.VMEM)),
        out_shape=(pltpu.SemaphoreType.DMA((n,)), pltpu.SemaphoreType.DMA((n,)),
                   *[pltpu.HBM(a.shape, a.dtype) for a in lands], _TOKEN),
        input_output_aliases={t: 2 + t for t in range(n)}, compiler_params=_CP_SPLIT)(*lands, precv, *after)
    return outs[0], outs[1], list(outs[2:2 + n]), outs[-1]


def _ag_wait(lands, send_sems, recv_sems, psend, precv, qsend, qrecv, after, l):
    n = len(lands)
    after = list(after) if isinstance(after, (list, tuple)) else [after]

    def body(*refs):
        zones = refs[:n]
        send_sems, recv_sems, psend, precv, qsend, qrecv = refs[n:n + 6]
        me, me_idx = _peer(0)
        for k in range(3):
            for t in range(n):
                _row_copy(zones[t], me_idx, send_sems.at[k * n + t], recv_sems.at[k * n + t], me).wait_send()
        for t in range(n):
            _row_copy(zones[t], _peer(1)[1], send_sems.at[t], recv_sems.at[t], me).wait_recv()
        fwd_idx, _ = _diag_route()
        for j, (mine, theirs) in enumerate(((_peer(4)[1], _peer(5)[1]), (_peer(2)[1], _peer(3)[1]))):
            for t in range(n):
                _row_copy(zones[t], mine, psend.at[j * n + t], precv.at[j * n + t], me).wait_send()
                _row_copy(zones[t], theirs, psend.at[j * n + t], precv.at[j * n + t], me).wait_recv()
        for t in range(n):
            _row_copy(zones[t], fwd_idx, psend.at[2 * n + t], precv.at[2 * n + t], me).wait_send()
            _row_copy(zones[t], _peer(6)[1], qsend.at[t], qrecv.at[t], me).wait_send()
            _row_copy(zones[t], _peer(7)[1], qsend.at[t], qrecv.at[t], me).wait_recv()

    outs = pl.pallas_call(
        body, name=f"ag_wait_{l}", in_specs=[_HBM] * n + [_SEM] * 6 + [_ANY] * len(after),
        out_specs=tuple([_HBM] * n), out_shape=tuple(pltpu.HBM(a.shape, a.dtype) for a in lands),
        input_output_aliases={t: t for t in range(n)}, compiler_params=_CP_SPLIT)(
            *lands, send_sems, recv_sems, psend, precv, qsend, qrecv, *after)
    return list(outs)


def _xchg_src(ref, slot_ref, idx):
    return _rows(ref, idx) if ref.shape[0] == NDEV * slot_ref.shape[1] else ref


def _rs_start(srcs, slots, after, tag):
    n = len(srcs)
    after = list(after) if isinstance(after, (list, tuple)) else [after]

    def body(*refs):
        src, slot = refs[:n], refs[n:2 * n]
        send_sems, recv_sems, token = refs[2 * n + len(after)], refs[2 * n + len(after) + 1], refs[-1]
        _, me_idx = _peer(0)
        for k in range(1, NDEV):
            dev, idx = _peer(k)
            for t in range(n):
                pltpu.make_async_remote_copy(
                    src_ref=_xchg_src(src[t], slot[t], idx), dst_ref=slot[t].at[me_idx],
                    send_sem=send_sems.at[(k - 1) * n + t], recv_sem=recv_sems.at[(k - 1) * n + t],
                    device_id=dev, device_id_type=_MESH).start()
        token[...] = jnp.zeros_like(token)

    outs = pl.pallas_call(
        body, name=f"rs_start_{tag}", in_specs=[_HBM] * (2 * n) + [_ANY] * len(after),
        out_specs=(_SEM, _SEM, *[_HBM] * (2 * n), pl.BlockSpec(memory_space=pltpu.VMEM)),
        out_shape=(pltpu.SemaphoreType.DMA(((NDEV - 1) * n,)), pltpu.SemaphoreType.DMA(((NDEV - 1) * n,)),
                   *[pltpu.HBM(a.shape, a.dtype) for a in list(srcs) + list(slots)], _TOKEN),
        input_output_aliases={t: 2 + t for t in range(2 * n)}, compiler_params=_CP_SPLIT)(
            *[_hbm(a) for a in list(srcs) + list(slots)], *after)
    return outs[0], outs[1], list(outs[2:2 + n]), list(outs[2 + n:2 + 2 * n]), outs[-1]


def _rs_wait(srcs, slots, send_sems, recv_sems, after, tag):
    n = len(srcs)
    after = list(after) if isinstance(after, (list, tuple)) else [after]

    def body(*refs):
        src, slot, send_sems, recv_sems = refs[:n], refs[n:2 * n], refs[2 * n], refs[2 * n + 1]
        me, _ = _peer(0)
        for k in range(1, NDEV):
            idx = _peer(k)[1]
            for t in range(n):
                cp = pltpu.make_async_remote_copy(
                    src_ref=_xchg_src(src[t], slot[t], idx), dst_ref=slot[t].at[idx],
                    send_sem=send_sems.at[(k - 1) * n + t], recv_sem=recv_sems.at[(k - 1) * n + t],
                    device_id=me, device_id_type=_MESH)
                cp.wait_send()
                cp.wait_recv()

    outs = pl.pallas_call(
        body, name=f"rs_wait_{tag}", in_specs=[_HBM] * (2 * n) + [_SEM, _SEM] + [_ANY] * len(after),
        out_specs=tuple([_HBM] * (2 * n)),
        out_shape=tuple(pltpu.HBM(a.shape, a.dtype) for a in list(srcs) + list(slots)),
        input_output_aliases={t: t for t in range(2 * n)}, compiler_params=_CP_SPLIT)(
            *srcs, *slots, send_sems, recv_sems, *after)
    return list(outs[:n]), list(outs[n:])


def _pair_start(full4s, bufs, after, tag):
    n = len(full4s)
    after = list(after) if isinstance(after, (list, tuple)) else [after]

    def body(*refs):
        full, buf = refs[:n], refs[n:2 * n]
        send_sems, recv_sems, token = refs[2 * n + len(after)], refs[2 * n + len(after) + 1], refs[-1]
        c = lax.axis_index("c")
        for t in range(n):
            pltpu.make_async_remote_copy(src_ref=full[t].at[:, 1 - c], dst_ref=buf[t], send_sem=send_sems.at[t],
                                         recv_sem=recv_sems.at[t], device_id=_peer(1)[0], device_id_type=_MESH).start()
        token[...] = jnp.zeros_like(token)

    outs = pl.pallas_call(
        body, name=f"pair_start_{tag}", in_specs=[_HBM] * (2 * n) + [_ANY] * len(after),
        out_specs=(_SEM, _SEM, *[_HBM] * (2 * n), pl.BlockSpec(memory_space=pltpu.VMEM)),
        out_shape=(pltpu.SemaphoreType.DMA((n,)), pltpu.SemaphoreType.DMA((n,)),
                   *[pltpu.HBM(a.shape, a.dtype) for a in list(full4s) + list(bufs)], _TOKEN),
        input_output_aliases={t: 2 + t for t in range(2 * n)}, compiler_params=_CP_SPLIT)(
            *[_hbm(a) for a in list(full4s) + list(bufs)], *after)
    return outs[0], outs[1], list(outs[2:2 + n]), list(outs[2 + n:2 + 2 * n]), outs[-1]


def _pair_wait(full4s, bufs, send_sems, recv_sems, after, tag):
    n = len(full4s)
    after = list(after) if isinstance(after, (list, tuple)) else [after]

    def body(*refs):
        full, buf, send_sems, recv_sems = refs[:n], refs[n:2 * n], refs[2 * n], refs[2 * n + 1]
        c = lax.axis_index("c")
        for t in range(n):
            cp = pltpu.make_async_remote_copy(src_ref=full[t].at[:, 1 - c], dst_ref=buf[t], send_sem=send_sems.at[t],
                                              recv_sem=recv_sems.at[t], device_id=_peer(0)[0], device_id_type=_MESH)
            cp.wait_send()
            cp.wait_recv()

    outs = pl.pallas_call(
        body, name=f"pair_wait_{tag}", in_specs=[_HBM] * (2 * n) + [_SEM, _SEM] + [_ANY] * len(after),
        out_specs=tuple([_HBM] * (2 * n)),
        out_shape=tuple(pltpu.HBM(a.shape, a.dtype) for a in list(full4s) + list(bufs)),
        input_output_aliases={t: t for t in range(2 * n)}, compiler_params=_CP_SPLIT)(
            *full4s, *bufs, send_sems, recv_sems, *after)
    return list(outs[:n]), list(outs[n:])


def _pair_sum(core, full4s, bufs):
    n = len(full4s)

    def body(core_ref, *refs):
        for t in range(n):
            refs[2 * n + t][...] = (refs[t][...].astype(f32) + refs[n + t][...].astype(f32)).astype(bf16)

    grid_spec = pltpu.PrefetchScalarGridSpec(
        num_scalar_prefetch=1, grid=(4,),
        in_specs=[pl.BlockSpec((None, None) + a.shape[2:], lambda j, core_ref: (j, core_ref[0], 0, 0)) for a in full4s]
        + [pl.BlockSpec((None,) + b.shape[1:], lambda j, core_ref: (j, 0, 0)) for b in bufs],
        out_specs=[pl.BlockSpec((None,) + b.shape[1:], lambda j, core_ref: (j, 0, 0)) for b in bufs])
    return pl.pallas_call(
        body, grid_spec=grid_spec, out_shape=[SDS(b.shape, bf16) for b in bufs],
        compiler_params=_CP, name="pair_sum")(core, *full4s, *bufs)


def _chip_start(sums, slots, after, tag):
    n = len(sums)
    after = list(after) if isinstance(after, (list, tuple)) else [after]

    def body(*refs):
        src, slot = refs[:n], refs[n:2 * n]
        send_sems, recv_sems, token = refs[2 * n + len(after)], refs[2 * n + len(after) + 1], refs[-1]
        my_chip = 2 * lax.axis_index("x") + lax.axis_index("y")
        for k, mask in enumerate((4, 2, 6)):
            dev, _ = _peer(mask)
            for t in range(n):
                pltpu.make_async_remote_copy(
                    src_ref=src[t].at[2 * dev[0] + dev[1]], dst_ref=slot[t].at[my_chip],
                    send_sem=send_sems.at[k * n + t], recv_sem=recv_sems.at[k * n + t],
                    device_id=dev, device_id_type=_MESH).start()
        token[...] = jnp.zeros_like(token)

    outs = pl.pallas_call(
        body, name=f"chip_start_{tag}", in_specs=[_HBM] * (2 * n) + [_ANY] * len(after),
        out_specs=(_SEM, _SEM, *[_HBM] * (2 * n), pl.BlockSpec(memory_space=pltpu.VMEM)),
        out_shape=(pltpu.SemaphoreType.DMA((3 * n,)), pltpu.SemaphoreType.DMA((3 * n,)),
                   *[pltpu.HBM(a.shape, a.dtype) for a in list(sums) + list(slots)], _TOKEN),
        input_output_aliases={t: 2 + t for t in range(2 * n)}, compiler_params=_CP_SPLIT)(
            *[_hbm(a) for a in list(sums) + list(slots)], *after)
    return outs[0], outs[1], list(outs[2:2 + n]), list(outs[2 + n:2 + 2 * n]), outs[-1]


def _chip_wait(sums, slots, send_sems, recv_sems, after, tag):
    n = len(sums)
    after = list(after) if isinstance(after, (list, tuple)) else [after]

    def body(*refs):
        src, slot, send_sems, recv_sems = refs[:n], refs[n:2 * n], refs[2 * n], refs[2 * n + 1]
        for k, mask in enumerate((4, 2, 6)):
            dev, _ = _peer(mask)
            chip = 2 * dev[0] + dev[1]
            for t in range(n):
                cp = pltpu.make_async_remote_copy(
                    src_ref=src[t].at[chip], dst_ref=slot[t].at[chip],
                    send_sem=send_sems.at[k * n + t], recv_sem=recv_sems.at[k * n + t],
                    device_id=_peer(0)[0], device_id_type=_MESH)
                cp.wait_send()
                cp.wait_recv()

    outs = pl.pallas_call(
        body, name=f"chip_wait_{tag}", in_specs=[_HBM] * (2 * n) + [_SEM, _SEM] + [_ANY] * len(after),
        out_specs=tuple([_HBM] * (2 * n)),
        out_shape=tuple(pltpu.HBM(a.shape, a.dtype) for a in list(sums) + list(slots)),
        input_output_aliases={t: t for t in range(2 * n)}, compiler_params=_CP_SPLIT)(
            *sums, *slots, send_sems, recv_sems, *after)
    return list(outs[:n]), list(outs[n:])


def _sum_slots(slots, rb):
    r = slots.shape[1]

    def body(s_ref, o_ref):
        acc = s_ref[0].astype(f32)
        for s in range(1, NDEV):
            acc = acc + s_ref[s].astype(f32)
        o_ref[...] = acc

    return pl.pallas_call(
        body, grid=(r // rb,),
        in_specs=[pl.BlockSpec((NDEV, rb, D), lambda i: (0, i, 0))],
        out_specs=pl.BlockSpec((rb, D), lambda i: (i, 0)),
        out_shape=SDS((r, D), f32), compiler_params=_CP, name="sum_slots")(slots)


def _adamw(w, g, m, v):
    shape = w.shape
    cols = shape[-1]
    rows = w.size // cols
    rb = rows
    for cand in (512, 256, 128, 64, 32, 16, 8):
        if rows % cand == 0 and rows > cand:
            rb = cand
            break

    def body(w_ref, g_ref, m_ref, v_ref, d_ref, mo_ref, vo_ref):
        d_ref[...], mo_ref[...], vo_ref[...] = _adamw_math(w_ref[...], g_ref[...], m_ref[...], v_ref[...])

    spec = pl.BlockSpec((rb, cols), lambda i: (i, 0))
    outs = pl.pallas_call(
        body, grid=(rows // rb,), in_specs=[spec] * 4, out_specs=[spec] * 3,
        out_shape=[SDS((rows, cols), f32)] * 3, compiler_params=_CP, name="adamw")(
            *(a.reshape(rows, cols) for a in (w, g, m, v)))
    return tuple(o.reshape(shape) for o in outs)


def _adamw_math(w, g, m, v):
    m = ADAM_B1 * m + (1.0 - ADAM_B1) * g
    v = ADAM_B2 * v + (1.0 - ADAM_B2) * (g * g)
    m_hat = m / (1.0 - ADAM_B1 ** ADAM_STEP)
    v_hat = v / (1.0 - ADAM_B2 ** ADAM_STEP)
    return -ADAM_LR * (m_hat / (jnp.sqrt(v_hat) + ADAM_EPS) + ADAM_WD * w), m, v


def _reduce_adamw(acc, me, full, slots, w, m, v, l):
    _, r, _ = w.shape
    ns = slots.shape[0]
    rb = r // 4 if r > 128 else r // 2

    def body(me_ref, full_ref, slots_ref, w_ref, m_ref, v_ref, *refs):
        go_ref, d_ref, mo_ref, vo_ref = refs[-4:]
        own = full_ref[...].astype(f32)
        g = None
        for s in range(ns):
            part = jnp.where(me_ref[0] == s, own, slots_ref[s].astype(f32))
            g = part if g is None else g + part
        go_ref[...] = g
        d_ref[...], mo_ref[...], vo_ref[...] = _adamw_math(w_ref[...], g, m_ref[...], v_ref[...])

    steps = r // rb
    lay = pl.BlockSpec((None, rb, D), lambda i, me_ref: (l, i, 0))
    n_acc = 0 if acc is None else 4
    grid_spec = pltpu.PrefetchScalarGridSpec(
        num_scalar_prefetch=1, grid=(steps,),
        in_specs=[pl.BlockSpec((rb, D), lambda i, me_ref: (me_ref[0] * steps + i, 0)),
                  pl.BlockSpec((ns, rb, D), lambda i, me_ref: (0, i, 0)), lay, lay, lay] + [_ANY] * n_acc,
        out_specs=[lay] * 4)
    outs = pl.pallas_call(
        body, grid_spec=grid_spec, out_shape=[SDS(w.shape, f32)] * 4,
        input_output_aliases={6 + j: j for j in range(n_acc)},
        compiler_params=_CP, name="reduce_adamw")(me, full, slots, w, m, v, *(() if acc is None else acc))
    return tuple(outs)


_BIG = ("ffn1_w_gate", "ffn1_w_up", "ffn1_w_down", "w_in", "w_out", "ffn2_w_gate", "ffn2_w_up", "ffn2_w_down")
_TRANSPOSED = ("ffn1_w_gate", "ffn1_w_up", "w_in", "ffn2_w_gate", "ffn2_w_up")

def _block_diag(pool_w):
    out = jnp.zeros((L, PW, PW), pool_w.dtype)
    for gi in range(4):
        out = out.at[:, 64 * gi:64 * (gi + 1), 64 * gi:64 * (gi + 1)].set(pool_w[:, gi])
    return out


def kernel(x, positions, ffn1_norm, ffn1_w_gate, ffn1_w_up, ffn1_w_down, mix_norm, w_in, pool_w, pool_scale, w_out, ffn2_norm, ffn2_w_gate, ffn2_w_up, ffn2_w_down, final_norm, loss_target, m_ffn1_norm, m_ffn1_w_gate, m_ffn1_w_up, m_ffn1_w_down, m_mix_norm, m_w_in, m_pool_w, m_pool_scale, m_w_out, m_ffn2_norm, m_ffn2_w_gate, m_ffn2_w_up, m_ffn2_w_down, m_final_norm, v_ffn1_norm, v_ffn1_w_gate, v_ffn1_w_up, v_ffn1_w_down, v_mix_norm, v_w_in, v_pool_w, v_pool_scale, v_w_out, v_ffn2_norm, v_ffn2_w_gate, v_ffn2_w_up, v_ffn2_w_down, v_final_norm):
    weights = dict(ffn1_norm=ffn1_norm, ffn1_w_gate=ffn1_w_gate, ffn1_w_up=ffn1_w_up, ffn1_w_down=ffn1_w_down,
                   mix_norm=mix_norm, w_in=w_in, pool_w=pool_w, pool_scale=pool_scale, w_out=w_out,
                   ffn2_norm=ffn2_norm, ffn2_w_gate=ffn2_w_gate, ffn2_w_up=ffn2_w_up, ffn2_w_down=ffn2_w_down,
                   final_norm=final_norm)
    moms = dict(ffn1_norm=m_ffn1_norm, ffn1_w_gate=m_ffn1_w_gate, ffn1_w_up=m_ffn1_w_up, ffn1_w_down=m_ffn1_w_down,
                mix_norm=m_mix_norm, w_in=m_w_in, pool_w=m_pool_w, pool_scale=m_pool_scale, w_out=m_w_out,
                ffn2_norm=m_ffn2_norm, ffn2_w_gate=m_ffn2_w_gate, ffn2_w_up=m_ffn2_w_up, ffn2_w_down=m_ffn2_w_down,
                final_norm=m_final_norm)
    vels = dict(ffn1_norm=v_ffn1_norm, ffn1_w_gate=v_ffn1_w_gate, ffn1_w_up=v_ffn1_w_up, ffn1_w_down=v_ffn1_w_down,
                mix_norm=v_mix_norm, w_in=v_w_in, pool_w=v_pool_w, pool_scale=v_pool_scale, w_out=v_w_out,
                ffn2_norm=v_ffn2_norm, ffn2_w_gate=v_ffn2_w_gate, ffn2_w_up=v_ffn2_w_up, ffn2_w_down=v_ffn2_w_down,
                final_norm=v_final_norm)
    names = list(weights)

    me_idx = 4 * lax.axis_index("x") + 2 * lax.axis_index("y") + lax.axis_index("c")
    me_arr = me_idx.reshape(1).astype(jnp.int32)

    as_rows = lambda a, nm: jnp.swapaxes(a, 1, 2) if nm in _TRANSPOSED else a
    w_rows = {nm: as_rows(weights[nm], nm) for nm in _BIG}
    m_rows = {nm: as_rows(moms[nm], nm) for nm in _BIG}
    v_rows = {nm: as_rows(vels[nm], nm) for nm in _BIG}

    def landing_zones(l, which):
        return _place_own(me_arr, [w_rows[_BIG[t]] for t in which], l)

    g_ffn1 = [ffn1_norm[l].reshape(1, D) for l in range(L)]
    g_mix = [mix_norm[l].reshape(1, D) for l in range(L)]
    g_ffn2 = [ffn2_norm[l].reshape(1, D) for l in range(L)]
    wbd_all = _block_diag(pool_w).astype(bf16)
    wbd = [wbd_all[l] for l in range(L)]
    pscale = [pool_scale[l].reshape(1, PW) for l in range(L)]
    tabs = _rope_tables(positions)
    flat = lambda a: a.reshape(S, a.shape[-1])
    r4 = lambda a: a.reshape(4, S // 4, a.shape[-1])
    r16 = lambda a: a.reshape(16, S // 16, a.shape[-1])

    first, rest, whole = (0, 1, 2, 3, 4), (5, 6, 7), tuple(range(8))

    def ag_begin(l, which, after, zones=None):
        tag = f"{l}{'' if which == whole else 'h' if which == first else 'r'}"
        zones = landing_zones(l, which) if zones is None else zones
        send_sems, recv_sems, zones, token = _ag_start(zones, after, tag)
        return dict(tag=tag, zones=zones, s=send_sems, r=recv_sems), token

    def ag_second(ch, after):
        ch["ps"], ch["pr"], ch["zones"], token = _ag_pass(ch["zones"], ch["r"], after, ch["tag"])
        return token

    def ag_third(ch, after):
        ch["qs"], ch["qr"], ch["zones"], token = _ag_last(ch["zones"], ch["pr"], after, ch["tag"])
        return token

    def ag_end(ch, after):
        return _ag_wait(ch["zones"], ch["s"], ch["r"], ch["ps"], ch["pr"], ch["qs"], ch["qr"], after, ch["tag"])

    ch_head, _ = ag_begin(0, first, [])
    zones_rest, zones_next = landing_zones(0, rest), landing_zones(1, whole)
    early_zones = {ll: landing_zones(ll, whole) for ll in range(2, L)}
    fill = [z for zs in (zones_rest, zones_next, *early_zones.values(), tabs, wbd) for z in zs]
    head = ag_end(ch_head, ag_third(ch_head, ag_second(ch_head, fill)))
    ch_rest, tok_rest = ag_begin(0, rest, head[0], zones_rest)
    chains = {}
    chains[1], tok_next = ag_begin(1, whole, head[0], zones_next)
    gathered = [None] * L
    xs = x.reshape(S, D)
    saved = []
    for l in range(L):
        first_after, second_after = (), ()
        if l == 0:
            gt1, ut1, dn1, wint, wout = head
            first_after = (tok_rest, tok_next)
        else:
            gt1, ut1, dn1, wint, wout, gt2, ut2, dn2 = gathered[l]
        x0 = xs
        x1, gate1, up1 = _ffn_fwd(x0, g_ffn1[l], gt1, ut1, dn1, after=first_after)
        hmix, vp, q1, k1, v1, q4, k4, v4, q16, k16, v16 = _mix_in_fwd(x1, g_mix[l], wint, tabs)
        q4, k4, v4, q16, k16, v16 = map(flat, (q4, k4, v4, q16, k16, v16))
        ypool, diff = _pool_fwd(vp, wbd[l], pscale[l])
        after_attn = None
        if l == 0:
            after_attn = ag_second(ch_rest, [ypool, q16])
        o1, l1 = _attn_fwd(q1, k1, v1, S, after=after_attn)
        o4, l4 = _attn_fwd(q4, k4, v4, S // 4, after=after_attn)
        o16, l16 = _attn_fwd(q16, k16, v16, S // 16, after=after_attn)
        if 0 < l < L - 1:
            second_after = (ag_second(chains[l + 1], [o1, o4, o16]),)
        x2, mixed, o, lse1, lse4, lse16 = _mix_out_fwd(x1, ypool, o1, l1, r4(o4), r4(l4), r16(o16), r16(l16), wout)
        if l == 0:
            token = ag_third(ch_rest, x2)
            gt2, ut2, dn2 = ag_end(ch_rest, token)
            gathered[0] = list(head) + [gt2, ut2, dn2]
            second_after = (ag_second(chains[1], gt2),)
        x3, gate2, up2 = _ffn_fwd(x2, g_ffn2[l], gt2, ut2, dn2, after=second_after)
        if l + 1 < L:
            token = ag_third(chains[l + 1], x3)
            if l + 2 < L:
                chains[l + 2], token = ag_begin(l + 2, whole, token, early_zones[l + 2])
            gathered[l + 1] = ag_end(chains[l + 1], token)
        saved.append(dict(x0=x0, x1=x1, x2=x2, gate1=gate1, up1=up1, gate2=gate2, up2=up2, hmix=hmix, diff=diff,
                          qkv=((q1, k1, v1), (q4, k4, v4), (q16, k16, v16)), mixed=mixed, o=o,
                          lse=(lse1, flat(lse4), flat(lse16))))
        xs = x3

    dx, loss_part, d_final = _loss_head(xs, final_norm.reshape(1, D), loss_target.reshape(S, D))

    d_norm = {nm: [None] * L for nm in ("ffn1_norm", "mix_norm", "ffn2_norm")}
    d_poolw, d_pscale = [None] * L, [None] * L
    group_a = ("ffn2_w_gate", "ffn2_w_up", "ffn2_w_down", "w_out")
    group_b = ("ffn1_w_gate", "ffn1_w_up", "ffn1_w_down", "w_in")
    acc = {}

    def exchange(full, group, after, tag):
        srcs = [full[nm] for nm in group]
        slots = [lax.empty((NDEV, g.shape[0] // NDEV, D), bf16) for g in srcs]
        ssem, rsem, srcs, slots, token = _rs_start(srcs, slots, after, tag)
        return (srcs, slots, ssem, rsem, tag), token

    def update(l, group, flight, after):
        srcs, slots, ssem, rsem, tag = flight
        srcs, slots = _rs_wait(srcs, slots, ssem, rsem, after, tag)
        for nm, full_g, slots_g in zip(group, srcs, slots):
            acc[nm] = _reduce_adamw(acc.get(nm), me_arr, full_g, slots_g, w_rows[nm], m_rows[nm], v_rows[nm], l)
        return [acc[nm][0] for nm in group], slots

    core_arr = lax.axis_index("c").reshape(1).astype(jnp.int32)
    chip_arr = (2 * lax.axis_index("x") + lax.axis_index("y")).reshape(1).astype(jnp.int32)

    def exchange_cores(full, group, after, tag):
        full4s = [full[nm].reshape(4, 2, full[nm].shape[0] // NDEV, D) for nm in group]
        bufs = [lax.empty((4,) + a.shape[2:], bf16) for a in full4s]
        ssem, rsem, full4s, bufs, token = _pair_start(full4s, bufs, after, tag)
        return (full4s, bufs, ssem, rsem, tag), token

    def exchange_chips(flight, after):
        full4s, bufs, ssem, rsem, tag = flight
        full4s, bufs = _pair_wait(full4s, bufs, ssem, rsem, after, tag)
        sums = _pair_sum(core_arr, full4s, bufs)
        slots = [lax.empty(a.shape, bf16) for a in sums]
        ssem, rsem, sums, slots, token = _chip_start(sums, slots, bufs[0], tag)
        return (sums, slots, ssem, rsem, tag), token

    def update_chips(l, group, flight, after):
        sums, slots, ssem, rsem, tag = flight
        sums, slots = _chip_wait(sums, slots, ssem, rsem, after, tag)
        for nm, sums_g, slots_g in zip(group, sums, slots):
            own = sums_g.reshape(4 * sums_g.shape[1], D)
            acc[nm] = _reduce_adamw(acc.get(nm), chip_arr, own, slots_g, w_rows[nm], m_rows[nm], v_rows[nm], l)
        return [acc[nm][0] for nm in group]

    flights = {}
    token_b = None
    for l in reversed(range(L)):
        sv = saved[l]
        gt1, ut1, dn1, wint, wout, gt2, ut2, dn2 = gathered[l]
        full = {}
        dx, dgate, dup, h, dy, d_norm["ffn2_norm"][l] = _ffn_bwd_d(
            sv["x2"], g_ffn2[l], sv["gate2"], sv["up2"], dx, gt2, ut2, dn2, after=() if token_b is None else (token_b,))
        full["ffn2_w_gate"], full["ffn2_w_up"], full["ffn2_w_down"] = _ffn_bwd_w(h, dy, sv["gate2"], sv["up2"], dgate, dup)

        dxb, dyp, do1, do4, do16, dl1, dl4, dl16 = _mix_out_bwd(dx, sv["o"], wout)
        full["w_out"] = _wgrad(sv["mixed"], dxb)
        flights[l, "a"], token_a = (exchange_cores if l == 0 else exchange)(full, group_a, dxb, f"a{l}")
        dvp, d_poolw[l], d_pscale[l] = _pool_bwd(dyp, sv["diff"], wbd[l], pscale[l], after=(token_a,))
        dos, dls = (do1, flat(do4), flat(do16)), (dl1, flat(dl4), flat(dl16))
        dqkv = []
        for b, lc in enumerate((S, S // 4, S // 16)):
            qb, kb, vb = sv["qkv"][b]
            dqkv.append(_attn_bwd(qb, kb, vb, dos[b], sv["lse"][b], dls[b], lc))
        d4 = tuple(r4(a) for a in dqkv[1])
        d16 = tuple(r16(a) for a in dqkv[2])
        mix_after = ()
        if l == 0:
            flights[0, "a"], token_a = exchange_chips(flights[0, "a"], [dqkv[0][0], dqkv[1][0], dqkv[2][0]])
            mix_after = (token_a,)
        dx, dproj, d_norm["mix_norm"][l] = _mix_in_bwd(dx, sv["x1"], g_mix[l], wint, tabs, dvp, dqkv[0], d4, d16,
                                                       after=mix_after)
        full["w_in"] = _wgrad(dproj, sv["hmix"])

        dx, dgate, dup, h, dy, d_norm["ffn1_norm"][l] = _ffn_bwd_d(sv["x0"], g_ffn1[l], sv["gate1"], sv["up1"], dx, gt1, ut1, dn1)
        full["ffn1_w_gate"], full["ffn1_w_up"], full["ffn1_w_down"] = _ffn_bwd_w(h, dy, sv["gate1"], sv["up1"], dgate, dup)

        after = dx
        if l + 1 < L and l + 1 >= 2:
            after, _ = update(l + 1, group_a, flights.pop((l + 1, "a")), after)
        if l + 1 < L and l + 1 >= 3:
            after, _ = update(l + 1, group_b, flights.pop((l + 1, "b")), after)
        if l > 0:
            flights[l, "b"], token_b = exchange(full, group_b, after, f"b{l}")

    flights[0, "b"], token_b = exchange_cores(full, group_b, dx, "b0")
    pad8 = lambda a: jnp.pad(a, ((0, 8 - a.shape[0]), (0, 0)))
    misc = jnp.concatenate([d_final, jnp.concatenate(d_pscale, axis=1), loss_part], axis=0)
    small = jnp.concatenate(
        [pad8(jnp.concatenate(d_norm[nm], axis=0)) for nm in ("ffn1_norm", "mix_norm", "ffn2_norm")]
        + [pad8(misc), jnp.stack(d_poolw).reshape(L * 16, D)], axis=0)
    small_slots = lax.dynamic_update_slice(lax.empty((NDEV, SMALL_ROWS, D), f32), small[None], (me_idx, 0, 0))
    pack_sems = _rs_start([small], [small_slots], token_b, "pack")
    flights[0, "b"], token_b = exchange_chips(flights[0, "b"], pack_sems[-1])

    after = token_b
    for key in [(2, "b"), (1, "a"), (1, "b")]:
        after, _ = update(key[0], group_a if key[1] == "a" else group_b, flights.pop(key), after)
    _, pack_slots = _rs_wait(pack_sems[2], pack_sems[3], pack_sems[0], pack_sems[1], after, "pack")
    sm = _sum_slots(pack_slots[0], SMALL_ROWS)
    grads = {}
    grads["ffn1_norm"], grads["mix_norm"], grads["ffn2_norm"] = sm[0:L], sm[8:8 + L], sm[16:16 + L]
    grads["final_norm"] = sm[24]
    grads["pool_scale"] = sm[25].reshape(L, PW)
    grads["pool_w"] = sm[32:32 + L * 16].reshape(L, 4, 64, 64)
    loss = sm[26, 0]
    upd = {nm: _adamw(weights[nm], grads[nm], moms[nm], vels[nm]) for nm in names if nm not in _BIG}
    after = update_chips(0, group_a, flights.pop((0, "a")), [upd[nm][0] for nm in upd])
    update_chips(0, group_b, flights.pop((0, "b")), after)
    for nm in _BIG:
        grads[nm], upd[nm] = as_rows(acc[nm][0], nm), tuple(as_rows(a, nm) for a in acc[nm][1:])
    return (loss, dx.reshape(1, S, D), *[grads[nm] for nm in names], *[upd[nm][0] for nm in names],
            *[upd[nm][1] for nm in names], *[upd[nm][2] for nm in names])
```

```python
import jax
import jax.numpy as jnp
from jax import lax
from jax.experimental import pallas as pl
from jax.experimental.pallas import tpu as pltpu

f32 = jnp.float32
bf16 = jnp.bfloat16
SDS = jax.ShapeDtypeStruct

D = 1024
S = 2048
F = 2816
L = 4
PW = 256
AW = 768
PROJ = PW + 3 * AW
NDEV = 8
TM = 256
QB = 128
HALF = 64
NG = AW // 128
NORM_EPS = 1e-6
MASK_VALUE = -1e30
ROPE_THETA = 500000.0
ADAM_LR, ADAM_B1, ADAM_B2, ADAM_EPS, ADAM_WD, ADAM_STEP = 0.001, 0.9, 0.999, 1e-08, 0.01, 10
POOL_WINDOWS = (2, 4, 8, 16)
PAD = 8
SMALL_ROWS = 96
VMEM_LIMIT = 56 * 1024 * 1024

_CP = pltpu.CompilerParams(vmem_limit_bytes=VMEM_LIMIT)
_ANY = pl.BlockSpec(memory_space=pl.ANY)
_HBM = pl.BlockSpec(memory_space=pltpu.HBM)
_SEM = pl.BlockSpec(memory_space=pltpu.SEMAPHORE)
_MESH = pl.DeviceIdType.MESH
_CP_SPLIT = pltpu.CompilerParams(has_side_effects=pltpu.SideEffectType.DATAFLOW_SIDE_EFFECTING)


def _dot_nn(a, b):
    return lax.dot_general(a, b, (((1,), (0,)), ((), ())), preferred_element_type=f32)


def _dot_nt(a, b):
    return lax.dot_general(a, b, (((1,), (1,)), ((), ())), preferred_element_type=f32)


def _dot_tn(a, b):
    return lax.dot_general(a, b, (((0,), (0,)), ((), ())), preferred_element_type=f32)


def _rms(x, g):
    r = lax.rsqrt(jnp.mean(x * x, axis=-1, keepdims=True) + NORM_EPS)
    xh = x * r
    return r, xh, xh * g


def _rms_bwd(dh, r, xh, g):
    dxh = dh * g
    return r * (dxh - xh * jnp.mean(dxh * xh, axis=-1, keepdims=True))


def _tile(cols, rows=TM):
    return pl.BlockSpec((rows, cols), lambda i: (i, 0))


def _const(shape):
    return pl.BlockSpec(shape, lambda i: (0,) * len(shape))


def _layer(rows, cols):
    return pl.BlockSpec((rows, cols), lambda i: (0, 0), pipeline_mode=pl.Buffered(1))


def _p4(cols=AW):
    return pl.BlockSpec((4, TM // 4, cols), lambda i: (0, i, 0))


def _p16(cols=AW):
    return pl.BlockSpec((16, TM // 16, cols), lambda i: (0, i, 0))


def _cols(j):
    return slice(128 * j, 128 * (j + 1))


def _follow(body, n_in, after):
    k = len(after)
    return body if k == 0 else (lambda *refs: body(*refs[:n_in], *refs[n_in + k:]))


def _ffn_fwd(x, g, gt, ut, dn, after=()):
    def body(x_ref, g_ref, gt_ref, ut_ref, dn_ref, xo_ref, gate_ref, up_ref):
        x = x_ref[...]
        _, _, hn = _rms(x, g_ref[...])
        h = hn.astype(bf16)
        gate = _dot_nt(h, gt_ref[...])
        up = _dot_nt(h, ut_ref[...])
        gate_ref[...] = gate.astype(bf16)
        up_ref[...] = up.astype(bf16)
        a = (gate * jax.nn.sigmoid(gate) * up).astype(bf16)
        xo_ref[...] = x + 0.5 * _dot_nn(a, dn_ref[...])

    rows = 2 * TM
    return pl.pallas_call(
        _follow(body, 5, after), grid=(S // rows,),
        in_specs=[_tile(D, rows), _layer(1, D), _layer(F, D), _layer(F, D), _layer(F, D)] + [_ANY] * len(after),
        out_specs=[_tile(D, rows), _tile(F, rows), _tile(F, rows)],
        out_shape=[SDS((S, D), f32), SDS((S, F), bf16), SDS((S, F), bf16)],
        compiler_params=_CP, name="ffn_fwd")(x, g, gt, ut, dn, *after)


def _ffn_bwd_d(x, g, gate, up, dxo, gt, ut, dn, after=()):
    def body(x_ref, g_ref, gate_ref, up_ref, dxo_ref, gt_ref, ut_ref, dn_ref,
             dx_ref, dgate_ref, dup_ref, h_ref, dy_ref, dg_ref):
        x = x_ref[...]
        g = g_ref[...]
        r, xh, hn = _rms(x, g)
        h_ref[...] = hn.astype(bf16)
        dxo = dxo_ref[...]
        dy = (0.5 * dxo).astype(bf16)
        dy_ref[...] = dy
        da = _dot_nt(dy, dn_ref[...])
        gate = gate_ref[...].astype(f32)
        up = up_ref[...].astype(f32)
        sg = jax.nn.sigmoid(gate)
        dgate = (da * up * (sg * (1.0 + gate * (1.0 - sg)))).astype(bf16)
        dup = (da * (gate * sg)).astype(bf16)
        dgate_ref[...] = dgate
        dup_ref[...] = dup
        dh = _dot_nn(dgate, gt_ref[...]) + _dot_nn(dup, ut_ref[...])

        @pl.when(pl.program_id(0) == 0)
        def _():
            dg_ref[...] = jnp.zeros_like(dg_ref)

        dg_ref[...] += jnp.sum(dh * xh, axis=0, keepdims=True)
        dx_ref[...] = dxo + _rms_bwd(dh, r, xh, g)

    return pl.pallas_call(
        _follow(body, 8, after), grid=(S // TM,),
        in_specs=[_tile(D), _layer(1, D), _tile(F), _tile(F), _tile(D),
                  _layer(F, D), _layer(F, D), _layer(F, D)] + [_ANY] * len(after),
        out_specs=[_tile(D), _tile(F), _tile(F), _tile(D), _tile(D), _const((1, D))],
        out_shape=[SDS((S, D), f32), SDS((S, F), bf16), SDS((S, F), bf16), SDS((S, D), bf16),
                   SDS((S, D), bf16), SDS((1, D), f32)],
        compiler_params=_CP, name="ffn_bwd_d")(x, g, gate, up, dxo, gt, ut, dn, *after)


def _ffn_bwd_w(h, dy, gate, up, dgate, dup):
    fc = 256

    def body(h_ref, dy_ref, gate_ref, up_ref, dgate_ref, dup_ref, dgt_ref, dut_ref, ddn_ref):
        gate = gate_ref[...].astype(f32)
        a = (gate * jax.nn.sigmoid(gate) * up_ref[...].astype(f32)).astype(bf16)
        ddn_ref[...] = _dot_tn(a, dy_ref[...]).astype(bf16)
        h = h_ref[...]
        dgt_ref[...] = _dot_tn(dgate_ref[...], h).astype(bf16)
        dut_ref[...] = _dot_tn(dup_ref[...], h).astype(bf16)

    col = pl.BlockSpec((S, fc), lambda j: (0, j))
    row = pl.BlockSpec((fc, D), lambda j: (j, 0))
    full = pl.BlockSpec((S, D), lambda j: (0, 0))
    return pl.pallas_call(
        body, grid=(F // fc,),
        in_specs=[full, full, col, col, col, col],
        out_specs=[row, row, row],
        out_shape=[SDS((F, D), bf16)] * 3,
        compiler_params=_CP, name="ffn_bwd_w")(h, dy, gate, up, dgate, dup)


def _wgrad(a, b):
    m, n = a.shape[1], b.shape[1]
    mc = 2 * TM

    def body(a_ref, b_ref, o_ref):
        o_ref[...] = _dot_tn(a_ref[...], b_ref[...]).astype(bf16)

    return pl.pallas_call(
        body, grid=(m // mc,),
        in_specs=[pl.BlockSpec((S, mc), lambda j: (0, j)), pl.BlockSpec((S, n), lambda j: (0, 0))],
        out_specs=pl.BlockSpec((mc, n), lambda j: (j, 0)),
        out_shape=SDS((m, n), bf16),
        compiler_params=_CP, name="wgrad")(a, b)


def _rope(t, c, sn, sp):
    return t * c + pltpu.roll(t, 120, 1) * sn + pltpu.roll(t, 8, 1) * sp


def _rope_bwd(d, c, sn, sp):
    return d * c + pltpu.roll(d * sn, 8, 1) + pltpu.roll(d * sp, 120, 1)


def _rope_tables(positions):
    inv_freq = ROPE_THETA ** (-jnp.arange(0, 16, 2, dtype=f32) / 16)
    ang = positions.reshape(S, 1).astype(f32) * inv_freq
    cos, sin = jnp.cos(ang), jnp.sin(ang)
    one = jnp.ones((S, 48), f32)
    zero8 = jnp.zeros((S, 8), f32)
    zero48 = jnp.zeros((S, 48), f32)
    c = jnp.concatenate([cos, cos, one], axis=1)
    sn = jnp.concatenate([-sin, zero8, zero48], axis=1)
    sp = jnp.concatenate([zero8, sin, zero48], axis=1)
    return tuple(jnp.concatenate([t, t], axis=1) for t in (c, sn, sp))


def _dilation_perm(n, back=False):
    per = TM // n
    i = lax.broadcasted_iota(jnp.int32, (TM, TM), 1 if back else 0)
    j = lax.broadcasted_iota(jnp.int32, (TM, TM), 0 if back else 1)
    return jnp.where(j == n * (i % per) + i // per, 1.0, 0.0).astype(bf16)


def _mix_in_fwd(x, g, wint, tabs):
    def body(x_ref, g_ref, w_ref, c_ref, sn_ref, sp_ref,
             h_ref, vp_ref, q1, k1, v1, q4, k4, v4, q16, k16, v16):
        _, _, hn = _rms(x_ref[...], g_ref[...])
        h = hn.astype(bf16)
        h_ref[...] = h
        proj = _dot_nt(h, w_ref[...])
        vp_ref[...] = proj[:, :PW]
        c, sn, sp = c_ref[...], sn_ref[...], sp_ref[...]
        perm4, perm16 = _dilation_perm(4), _dilation_perm(16)
        for kind, (o1, o4, o16) in enumerate(((q1, q4, q16), (k1, k4, k16), (v1, v4, v16))):
            for j in range(NG):
                t = proj[:, PW + kind * AW + 128 * j: PW + kind * AW + 128 * (j + 1)]
                if kind == 0:
                    t = _rope(t, c, sn, sp) * 0.125
                elif kind == 1:
                    t = _rope(t, c, sn, sp)
                o1[:, _cols(j)] = t.astype(bf16)
            nat = o1[...]
            o4[...] = _dot_nn(perm4, nat).astype(bf16).reshape(4, TM // 4, AW)
            o16[...] = _dot_nn(perm16, nat).astype(bf16).reshape(16, TM // 16, AW)

    nat, d4, d16 = SDS((S, AW), bf16), SDS((4, S // 4, AW), bf16), SDS((16, S // 16, AW), bf16)
    return pl.pallas_call(
        body, grid=(S // TM,),
        in_specs=[_tile(D), _layer(1, D), _layer(PROJ, D), _tile(128), _tile(128), _tile(128)],
        out_specs=[_tile(D), _tile(PW)] + [_tile(AW)] * 3 + [_p4()] * 3 + [_p16()] * 3,
        out_shape=[SDS((S, D), bf16), SDS((S, PW), f32)] + [nat] * 3 + [d4] * 3 + [d16] * 3,
        compiler_params=_CP, name="mix_in_fwd")(x, g, wint, *tabs)


def _mix_in_bwd(dxo, x, g, wint, tabs, dvp, d1, d4, d16, after=()):
    def body(dxo_ref, x_ref, g_ref, w_ref, c_ref, sn_ref, sp_ref, dvp_ref,
             dq1, dk1, dv1, dq4, dk4, dv4, dq16, dk16, dv16,
             dx_ref, dproj_ref, dg_ref):
        c, sn, sp = c_ref[...], sn_ref[...], sp_ref[...]
        dproj_ref[:, :PW] = dvp_ref[...].astype(bf16)
        back4, back16 = _dilation_perm(4, True), _dilation_perm(16, True)
        for kind, (a1, a4, a16) in enumerate(((dq1, dq4, dq16), (dk1, dk4, dk16), (dv1, dv4, dv16))):
            n4 = _dot_nn(back4, a4[...].reshape(TM, AW))
            n16 = _dot_nn(back16, a16[...].reshape(TM, AW))
            for j in range(NG):
                t = a1[:, _cols(j)].astype(f32) + n4[:, _cols(j)] + n16[:, _cols(j)]
                if kind == 0:
                    t = _rope_bwd(t * 0.125, c, sn, sp)
                elif kind == 1:
                    t = _rope_bwd(t, c, sn, sp)
                dproj_ref[:, PW + kind * AW + 128 * j: PW + kind * AW + 128 * (j + 1)] = t.astype(bf16)
        g = g_ref[...]
        r_, xh, _ = _rms(x_ref[...], g)
        dh = _dot_nn(dproj_ref[...], w_ref[...])

        @pl.when(pl.program_id(0) == 0)
        def _():
            dg_ref[...] = jnp.zeros_like(dg_ref)

        dg_ref[...] += jnp.sum(dh * xh, axis=0, keepdims=True)
        dx_ref[...] = dxo_ref[...] + _rms_bwd(dh, r_, xh, g)

    return pl.pallas_call(
        _follow(body, 17, after), grid=(S // TM,),
        in_specs=[_tile(D), _tile(D), _layer(1, D), _layer(PROJ, D), _tile(128), _tile(128), _tile(128),
                  _tile(PW)] + [_tile(AW)] * 3 + [_p4()] * 3 + [_p16()] * 3 + [_ANY] * len(after),
        out_specs=[_tile(D), _tile(PROJ), _const((1, D))],
        out_shape=[SDS((S, D), f32), SDS((S, PROJ), bf16), SDS((1, D), f32)],
        compiler_params=_CP, name="mix_in_bwd")(dxo, x, g, wint, *tabs, dvp, *d1, *d4, *d16, *after)


def _pool_sums(pad_ref, base, rows, adjoint):
    lane_group = lax.broadcasted_iota(jnp.int32, (rows, PW), 1) // 64
    sign = -1 if adjoint else 1

    def sh(o):
        return pad_ref[pl.ds(PAD + base + sign * o, rows), :]

    out = None
    acc = None
    lo, hi = 0, 0
    for gi, w in enumerate(POOL_WINDOWS):
        for o in list(range(-(w // 2), lo)) + list(range(hi, w - w // 2)):
            acc = sh(o) if acc is None else acc + sh(o)
        lo, hi = -(w // 2), w - w // 2
        out = acc if out is None else jnp.where(lane_group >= gi, acc, out)
    return out


def _pool_counts(base, rows):
    pos = base + lax.broadcasted_iota(jnp.int32, (rows, PW), 0)
    lane_group = lax.broadcasted_iota(jnp.int32, (rows, PW), 1) // 64
    cnt = None
    for gi, w in enumerate(POOL_WINDOWS):
        lo = jnp.maximum(pos - w // 2, 0)
        hi = jnp.minimum(pos + w - 1 - w // 2, S - 1)
        c = (hi - lo + 1).astype(f32)
        cnt = c if cnt is None else jnp.where(lane_group >= gi, c, cnt)
    return cnt


def _pool_fwd(vp, wbd, scale):
    ch = 256

    def body(vp_ref, w_ref, sc_ref, y_ref, diff_ref, pad):
        pad[pl.ds(0, PAD), :] = jnp.zeros((PAD, PW), f32)
        pad[pl.ds(PAD + S, PAD), :] = jnp.zeros((PAD, PW), f32)
        pad[pl.ds(PAD, S), :] = vp_ref[...]
        for b in range(S // ch):
            base = b * ch
            pooled = _pool_sums(pad, base, ch, False) / _pool_counts(base, ch)
            diff = (pooled - vp_ref[pl.ds(base, ch), :]).astype(bf16)
            diff_ref[pl.ds(base, ch), :] = diff
            y_ref[pl.ds(base, ch), :] = _dot_nn(diff, w_ref[...]) * sc_ref[...]

    whole = lambda shape: pl.BlockSpec(shape, lambda i: (0,) * len(shape))
    return pl.pallas_call(
        body, grid=(1,),
        in_specs=[whole((S, PW)), whole((PW, PW)), whole((1, PW))],
        out_specs=[whole((S, PW)), whole((S, PW))],
        out_shape=[SDS((S, PW), f32), SDS((S, PW), bf16)],
        scratch_shapes=[pltpu.VMEM((S + 2 * PAD, PW), f32)],
        compiler_params=_CP, name="pool_fwd")(vp, wbd, scale)


def _pool_bwd(dy, diff, wbd, scale, after=()):
    ch = 256

    def body(dy_ref, diff_ref, w_ref, sc_ref, dvp_ref, dw_ref, dsc_ref, pad):
        pad[pl.ds(0, PAD), :] = jnp.zeros((PAD, PW), f32)
        pad[pl.ds(PAD + S, PAD), :] = jnp.zeros((PAD, PW), f32)
        dw = jnp.zeros((PW, PW), f32)
        dsc = jnp.zeros((1, PW), f32)
        for b in range(S // ch):
            base = b * ch
            dy = dy_ref[pl.ds(base, ch), :]
            diff = diff_ref[pl.ds(base, ch), :]
            dsc = dsc + jnp.sum(dy * _dot_nn(diff, w_ref[...]), axis=0, keepdims=True)
            dz = (dy * sc_ref[...]).astype(bf16)
            dw = dw + _dot_tn(diff, dz)
            ddiff = _dot_nt(dz, w_ref[...])
            dvp_ref[pl.ds(base, ch), :] = -ddiff
            pad[pl.ds(PAD + base, ch), :] = ddiff / _pool_counts(base, ch)
        for gi in range(4):
            dw_ref[gi] = dw[64 * gi:64 * (gi + 1), 64 * gi:64 * (gi + 1)]
        dsc_ref[...] = dsc
        for b in range(S // ch):
            base = b * ch
            dvp_ref[pl.ds(base, ch), :] += _pool_sums(pad, base, ch, True)

    whole = lambda shape: pl.BlockSpec(shape, lambda i: (0,) * len(shape))
    return pl.pallas_call(
        _follow(body, 4, after), grid=(1,),
        in_specs=[whole((S, PW)), whole((S, PW)), whole((PW, PW)), whole((1, PW))] + [_ANY] * len(after),
        out_specs=[whole((S, PW)), whole((4, 64, 64)), whole((1, PW))],
        out_shape=[SDS((S, PW), f32), SDS((4, 64, 64), f32), SDS((1, PW), f32)],
        scratch_shapes=[pltpu.VMEM((S + 2 * PAD, PW), f32)],
        compiler_params=_CP, name="pool_bwd")(dy, diff, wbd, scale, *after)


def _attn_blocks(lc):
    bpc = lc // QB
    kw = min(2 * QB, lc)
    blocks = []
    for b in range(S // QB):
        t0 = (b % bpc) * QB
        ks_in = min(max(t0 - HALF, 0), lc - kw)
        blocks.append((b * QB, (b // bpc) * lc + ks_in, t0 - ks_in))
    return kw, blocks


def _attn_bias(bias_ref, kw, shifts):
    r = lax.broadcasted_iota(jnp.int32, (2 * QB, kw), 0) % QB
    c = lax.broadcasted_iota(jnp.int32, (2 * QB, kw), 1)
    for i, shift in enumerate(shifts):
        bias_ref[i] = jnp.where(jnp.abs(r + shift - c) <= HALF, 0.0, MASK_VALUE).astype(f32)


def _head_put(stats, pair, v0, v1, lane):
    return jnp.where(lane == 2 * pair, v0, jnp.where(lane == 2 * pair + 1, v1, stats))


def _head_cols(stats, pair, lane):
    c0 = jnp.sum(jnp.where(lane == 2 * pair, stats, 0.0), axis=-1, keepdims=True)
    c1 = jnp.sum(jnp.where(lane == 2 * pair + 1, stats, 0.0), axis=-1, keepdims=True)
    return jnp.concatenate([c0, c1], axis=0)


def _head_spread(stats, pair, head0):
    return jnp.where(head0, stats[:, 2 * pair:2 * pair + 1], stats[:, 2 * pair + 1:2 * pair + 2])


def _stack_heads(blk, head0):
    zero = jnp.zeros_like(blk)
    return jnp.concatenate([jnp.where(head0, blk, zero), jnp.where(head0, zero, blk)], axis=0)


def _attn_fwd(q, k, v, lc, after=None):
    kw, blocks = _attn_blocks(lc)
    shifts = sorted({b[2] for b in blocks})

    def body(q_ref, k_ref, v_ref, *refs):
        o_ref, lse_ref, bias_ref = refs[-3:]
        lane = lax.broadcasted_iota(jnp.int32, (QB, 128), 1)
        head0 = lane < 64
        pair = pl.program_id(0)
        _attn_bias(bias_ref, kw, shifts)

        @pl.when(pair == 0)
        def _():
            lse_ref[...] = jnp.zeros_like(lse_ref)

        for row0, kstart, shift in blocks:
            q2 = _stack_heads(q_ref[pl.ds(row0, QB), :], head0)
            kb = k_ref[pl.ds(kstart, kw), :]
            vb = v_ref[pl.ds(kstart, kw), :]
            s = _dot_nt(q2, kb) + bias_ref[shifts.index(shift)]
            m = jnp.max(s, axis=-1, keepdims=True)
            p = jnp.exp(s - m)
            den = jnp.sum(p, axis=-1, keepdims=True)
            o2 = _dot_nn(p.astype(bf16), vb) / den
            lse2 = m + jnp.log(den)
            o_ref[pl.ds(row0, QB), :] = jnp.where(head0, o2[:QB], o2[QB:]).astype(bf16)
            lse_ref[pl.ds(row0, QB), :] = _head_put(lse_ref[pl.ds(row0, QB), :], pair, lse2[:QB], lse2[QB:], lane)

    col = pl.BlockSpec((S, 128), lambda p: (0, p))
    extra = () if after is None else (after,)
    return pl.pallas_call(
        body, grid=(NG,), in_specs=[col, col, col] + [_ANY] * len(extra),
        out_specs=[col, pl.BlockSpec((S, 128), lambda p: (0, 0))],
        out_shape=[SDS((S, AW), bf16), SDS((S, 128), f32)],
        scratch_shapes=[pltpu.VMEM((len(shifts), 2 * QB, kw), f32)],
        compiler_params=_CP, name=f"attn_fwd_{lc}")(q, k, v, *extra)


def _attn_bwd(q, k, v, do, lse, delta, lc):
    kw, blocks = _attn_blocks(lc)
    shifts = sorted({b[2] for b in blocks})

    def body(q_ref, k_ref, v_ref, do_ref, lse_ref, dl_ref, dq_ref, dk_out, dv_out, bias_ref, dk_ref, dv_ref):
        lane = lax.broadcasted_iota(jnp.int32, (QB, 128), 1)
        head0 = lane < 64
        pair = pl.program_id(0)
        _attn_bias(bias_ref, kw, shifts)
        dk_ref[...] = jnp.zeros_like(dk_ref)
        dv_ref[...] = jnp.zeros_like(dv_ref)
        for row0, kstart, shift in blocks:
            q2 = _stack_heads(q_ref[pl.ds(row0, QB), :], head0)
            do2 = _stack_heads(do_ref[pl.ds(row0, QB), :], head0)
            lse2 = _head_cols(lse_ref[pl.ds(row0, QB), :], pair, lane)
            dl2 = _head_cols(dl_ref[pl.ds(row0, QB), :], pair, lane)
            kb = k_ref[pl.ds(kstart, kw), :]
            vb = v_ref[pl.ds(kstart, kw), :]
            p = jnp.exp(_dot_nt(q2, kb) + bias_ref[shifts.index(shift)] - lse2)
            ds = (p * (_dot_nt(do2, vb) - dl2)).astype(bf16)
            dq2 = _dot_nn(ds, kb)
            dq_ref[pl.ds(row0, QB), :] = jnp.where(head0, dq2[:QB], dq2[QB:]).astype(bf16)
            dk_ref[pl.ds(kstart, kw), :] += _dot_tn(ds, q2)
            dv_ref[pl.ds(kstart, kw), :] += _dot_tn(p.astype(bf16), do2)
        dk_out[...] = dk_ref[...].astype(bf16)
        dv_out[...] = dv_ref[...].astype(bf16)

    col = pl.BlockSpec((S, 128), lambda p: (0, p))
    stats = pl.BlockSpec((S, 128), lambda p: (0, 0))
    return pl.pallas_call(
        body, grid=(NG,), in_specs=[col] * 4 + [stats] * 2, out_specs=[col] * 3,
        out_shape=[SDS((S, AW), bf16)] * 3,
        scratch_shapes=[pltpu.VMEM((len(shifts), 2 * QB, kw), f32), pltpu.VMEM((S, 128), f32),
                        pltpu.VMEM((S, 128), f32)],
        compiler_params=_CP, name=f"attn_bwd_{lc}")(q, k, v, do, lse, delta)


def _mix_out_fwd(x, ypool, o1, l1, o4, l4, o16, l16, wout):
    def body(x_ref, yp_ref, o1_ref, l1_ref, o4_ref, l4_ref, o16_ref, l16_ref, w_ref,
             xo_ref, mixed_ref, o_ref, lse1_ref, lse4_ref, lse16_ref, sl4, sl16, sl):
        head0 = lax.broadcasted_iota(jnp.int32, (TM, 128), 1) < 64
        for r in range(4):
            sl4[pl.ds(r, TM // 4, stride=4), :] = l4_ref[r]
        for r in range(16):
            sl16[pl.ds(r, TM // 16, stride=16), :] = l16_ref[r]
        n4 = _dot_nn(_dilation_perm(4, True), o4_ref[...].reshape(TM, AW))
        n16 = _dot_nn(_dilation_perm(16, True), o16_ref[...].reshape(TM, AW))
        a, b, c = l1_ref[...], sl4[...], sl16[...]
        m = jnp.maximum(jnp.maximum(a, b), c)
        wa, wb, wc = jnp.exp(a - m), jnp.exp(b - m), jnp.exp(c - m)
        den = wa + wb + wc
        wa, wb, wc = wa / den, wb / den, wc / den
        lse = m + jnp.log(den)
        lse1_ref[...] = lse
        sl[...] = lse
        mixed_ref[:, :PW] = yp_ref[...].astype(bf16)
        for j in range(NG):
            y = (_head_spread(wa, j, head0) * o1_ref[:, _cols(j)].astype(f32)
                 + _head_spread(wb, j, head0) * n4[:, _cols(j)] + _head_spread(wc, j, head0) * n16[:, _cols(j)])
            o_ref[:, _cols(j)] = y
            mixed_ref[:, PW + 128 * j: PW + 128 * (j + 1)] = y.astype(bf16)
        for r in range(4):
            lse4_ref[r] = sl[pl.ds(r, TM // 4, stride=4), :]
        for r in range(16):
            lse16_ref[r] = sl[pl.ds(r, TM // 16, stride=16), :]
        xo_ref[...] = x_ref[...] + _dot_nn(mixed_ref[...], w_ref[...])

    return pl.pallas_call(
        body, grid=(S // TM,),
        in_specs=[_tile(D), _tile(PW), _tile(AW), _tile(128), _p4(), _p4(128), _p16(), _p16(128), _layer(D, D)],
        out_specs=[_tile(D), _tile(D), _tile(AW), _tile(128), _p4(128), _p16(128)],
        out_shape=[SDS((S, D), f32), SDS((S, D), bf16), SDS((S, AW), f32), SDS((S, 128), f32),
                   SDS((4, S // 4, 128), f32), SDS((16, S // 16, 128), f32)],
        scratch_shapes=[pltpu.VMEM((TM, 128), f32)] * 3,
        compiler_params=_CP, name="mix_out_fwd")(x, ypool, o1, l1, o4, l4, o16, l16, wout)


def _mix_out_bwd(dxo, o, wout):
    def body(dxo_ref, o_ref, w_ref, dxb_ref, dyp_ref, do1, do4, do16, dl1, dl4, dl16, sdl):
        dxb = dxo_ref[...].astype(bf16)
        dxb_ref[...] = dxb
        dm = _dot_nt(dxb, w_ref[...])
        dyp_ref[...] = dm[:, :PW]
        lane = lax.broadcasted_iota(jnp.int32, (TM, 128), 1)
        head0 = lane < 64
        dl = jnp.zeros((TM, 128), f32)
        for j in range(NG):
            d = dm[:, PW + 128 * j: PW + 128 * (j + 1)]
            prod = d * o_ref[:, _cols(j)]
            dl = _head_put(dl, j, jnp.sum(jnp.where(head0, prod, 0.0), axis=-1, keepdims=True),
                           jnp.sum(jnp.where(head0, 0.0, prod), axis=-1, keepdims=True), lane)
            do1[:, _cols(j)] = d.astype(bf16)
        dl1[...] = dl
        sdl[...] = dl
        for r in range(4):
            dl4[r] = sdl[pl.ds(r, TM // 4, stride=4), :]
        for r in range(16):
            dl16[r] = sdl[pl.ds(r, TM // 16, stride=16), :]
        nat = do1[...]
        do4[...] = _dot_nn(_dilation_perm(4), nat).astype(bf16).reshape(4, TM // 4, AW)
        do16[...] = _dot_nn(_dilation_perm(16), nat).astype(bf16).reshape(16, TM // 16, AW)

    return pl.pallas_call(
        body, grid=(S // TM,),
        in_specs=[_tile(D), _tile(AW), _layer(D, D)],
        out_specs=[_tile(D), _tile(PW), _tile(AW), _p4(), _p16(), _tile(128), _p4(128), _p16(128)],
        out_shape=[SDS((S, D), bf16), SDS((S, PW), f32),
                   SDS((S, AW), bf16), SDS((4, S // 4, AW), bf16), SDS((16, S // 16, AW), bf16),
                   SDS((S, 128), f32), SDS((4, S // 4, 128), f32), SDS((16, S // 16, 128), f32)],
        scratch_shapes=[pltpu.VMEM((TM, 128), f32)],
        compiler_params=_CP, name="mix_out_bwd")(dxo, o, wout)


def _loss_head(x, g, target):
    def body(x_ref, g_ref, t_ref, dx_ref, loss_ref, dg_ref):
        g = g_ref[...]
        r, xh, y = _rms(x_ref[...], g)
        err = y - t_ref[...]
        dy = err * (1.0 / D)

        @pl.when(pl.program_id(0) == 0)
        def _():
            loss_ref[...] = jnp.zeros_like(loss_ref)
            dg_ref[...] = jnp.zeros_like(dg_ref)

        loss_ref[...] += jnp.broadcast_to(0.5 * jnp.sum(jnp.mean(err * err, axis=-1, keepdims=True)), (1, D))
        dg_ref[...] += jnp.sum(dy * xh, axis=0, keepdims=True)
        dx_ref[...] = _rms_bwd(dy, r, xh, g)

    return pl.pallas_call(
        body, grid=(S // TM,),
        in_specs=[_tile(D), _const((1, D)), _tile(D)],
        out_specs=[_tile(D), _const((1, D)), _const((1, D))],
        out_shape=[SDS((S, D), f32), SDS((1, D), f32), SDS((1, D), f32)],
        compiler_params=_CP, name="loss_head")(x, g, target)


def _peer(k):
    x, y, c = lax.axis_index("x"), lax.axis_index("y"), lax.axis_index("c")
    px = 1 - x if k & 4 else x
    py = 1 - y if k & 2 else y
    pc = 1 - c if k & 1 else c
    return (px, py, pc), 4 * px + 2 * py + pc


def _diag_route():
    x, y, c = lax.axis_index("x"), lax.axis_index("y"), lax.axis_index("c")
    idx_x, idx_y = _peer(4)[1], _peer(2)[1]
    return idx_x + c * (idx_y - idx_x), (x + c * (1 - 2 * x), (1 - y) + c * (2 * y - 1), c)


def _hbm(a):
    return pltpu.with_memory_space_constraint(a, pltpu.HBM)


def _rows(ref, idx):
    r = ref.shape[0] // NDEV
    return ref.at[pl.ds(idx * r, r), :]


def _row_copy(ref, idx, send_sem, recv_sem, to):
    return pltpu.make_async_remote_copy(src_ref=_rows(ref, idx), dst_ref=_rows(ref, idx), send_sem=send_sem,
                                        recv_sem=recv_sem, device_id=to, device_id_type=_MESH)


def _place_own(me, shards, l):
    n = len(shards)

    def body(me_ref, *refs):
        for t in range(n):
            refs[n + t][...] = refs[t][...].astype(bf16)

    grid_spec = pltpu.PrefetchScalarGridSpec(
        num_scalar_prefetch=1, grid=(1,),
        in_specs=[pl.BlockSpec((None, s.shape[1], D), lambda i, me_ref: (l, 0, 0)) for s in shards],
        out_specs=[pl.BlockSpec((s.shape[1], D), lambda i, me_ref: (me_ref[0], 0)) for s in shards])
    return pl.pallas_call(
        body, grid_spec=grid_spec, out_shape=[SDS((NDEV * s.shape[1], D), bf16) for s in shards],
        compiler_params=_CP, name="place_own")(me, *shards)


_TOKEN = SDS((8, 128), f32)


def _ag_start_direct(lands, after, l):
    n = len(lands)
    after = list(after) if isinstance(after, (list, tuple)) else [after]

    def body(*refs):
        zones, send_sems, recv_sems, token = refs[:n], refs[n + len(after)], refs[n + len(after) + 1], refs[-1]
        _, me_idx = _peer(0)
        for k, mask in enumerate((1, 4, 2, 6)):
            for t in range(n):
                _row_copy(zones[t], me_idx, send_sems.at[k * n + t], recv_sems.at[k * n + t], _peer(mask)[0]).start()
        token[...] = jnp.zeros_like(token)

    outs = pl.pallas_call(
        body, name=f"agd_start_{l}", in_specs=[_HBM] * n + [_ANY] * len(after),
        out_specs=(_SEM, _SEM, *[_HBM] * n, pl.BlockSpec(memory_space=pltpu.VMEM)),
        out_shape=(pltpu.SemaphoreType.DMA((4 * n,)), pltpu.SemaphoreType.DMA((4 * n,)),
                   *[pltpu.HBM(a.shape, a.dtype) for a in lands], _TOKEN),
        input_output_aliases={t: 2 + t for t in range(n)}, compiler_params=_CP_SPLIT)(
            *[_hbm(a) for a in lands], *after)
    return outs[0], outs[1], list(outs[2:2 + n]), outs[-1]


def _ag_pass_direct(lands, recv_sems, after, l):
    n = len(lands)
    after = list(after) if isinstance(after, (list, tuple)) else [after]

    def body(*refs):
        zones, recv_sems = refs[:n], refs[n]
        psend, precv, token = refs[n + 1 + len(after)], refs[n + 2 + len(after)], refs[-1]
        me, _ = _peer(0)
        sibling, _ = _peer(1)
        for j, mask in enumerate((4, 2, 6)):
            idx = _peer(mask)[1]
            for t in range(n):
                _row_copy(zones[t], idx, psend.at[j * n + t], recv_sems.at[(1 + j) * n + t], me).wait_recv()
                _row_copy(zones[t], idx, psend.at[j * n + t], precv.at[j * n + t], sibling).start()
        token[...] = jnp.zeros_like(token)

    outs = pl.pallas_call(
        body, name=f"agd_pass_{l}", in_specs=[_HBM] * n + [_SEM] + [_ANY] * len(after),
        out_specs=(_SEM, _SEM, *[_HBM] * n, pl.BlockSpec(memory_space=pltpu.VMEM)),
        out_shape=(pltpu.SemaphoreType.DMA((3 * n,)), pltpu.SemaphoreType.DMA((3 * n,)),
                   *[pltpu.HBM(a.shape, a.dtype) for a in lands], _TOKEN),
        input_output_aliases={t: 2 + t for t in range(n)}, compiler_params=_CP_SPLIT)(*lands, recv_sems, *after)
    return outs[0], outs[1], list(outs[2:2 + n]), outs[-1]


def _ag_wait_direct(lands, send_sems, recv_sems, psend, precv, after, l):
    n = len(lands)
    after = list(after) if isinstance(after, (list, tuple)) else [after]

    def body(*refs):
        zones = refs[:n]
        send_sems, recv_sems, psend, precv = refs[n:n + 4]
        me, me_idx = _peer(0)
        for k in range(4):
            for t in range(n):
                _row_copy(zones[t], me_idx, send_sems.at[k * n + t], recv_sems.at[k * n + t], me).wait_send()
        for t in range(n):
            _row_copy(zones[t], _peer(1)[1], send_sems.at[t], recv_sems.at[t], me).wait_recv()
        for j, (mine, theirs) in enumerate(((4, 5), (2, 3), (6, 7))):
            for t in range(n):
                _row_copy(zones[t], _peer(mine)[1], psend.at[j * n + t], precv.at[j * n + t], me).wait_send()
                _row_copy(zones[t], _peer(theirs)[1], psend.at[j * n + t], precv.at[j * n + t], me).wait_recv()

    outs = pl.pallas_call(
        body, name=f"agd_wait_{l}", in_specs=[_HBM] * n + [_SEM] * 4 + [_ANY] * len(after),
        out_specs=tuple([_HBM] * n), out_shape=tuple(pltpu.HBM(a.shape, a.dtype) for a in lands),
        input_output_aliases={t: t for t in range(n)}, compiler_params=_CP_SPLIT)(
            *lands, send_sems, recv_sems, psend, precv, *after)
    return list(outs)


def _ag_start(lands, after, l):
    n = len(lands)
    after = list(after) if isinstance(after, (list, tuple)) else [after]

    def body(*refs):
        zones, send_sems, recv_sems, token = refs[:n], refs[n + len(after)], refs[n + len(after) + 1], refs[-1]
        _, me_idx = _peer(0)
        for k, mask in enumerate((1, 4, 2)):
            for t in range(n):
                _row_copy(zones[t], me_idx, send_sems.at[k * n + t], recv_sems.at[k * n + t], _peer(mask)[0]).start()
        token[...] = jnp.zeros_like(token)

    outs = pl.pallas_call(
        body, name=f"ag_start_{l}", in_specs=[_HBM] * n + [_ANY] * len(after),
        out_specs=(_SEM, _SEM, *[_HBM] * n, pl.BlockSpec(memory_space=pltpu.VMEM)),
        out_shape=(pltpu.SemaphoreType.DMA((3 * n,)), pltpu.SemaphoreType.DMA((3 * n,)),
                   *[pltpu.HBM(a.shape, a.dtype) for a in lands], _TOKEN),
        input_output_aliases={t: 2 + t for t in range(n)}, compiler_params=_CP_SPLIT)(
            *[_hbm(a) for a in lands], *after)
    return outs[0], outs[1], list(outs[2:2 + n]), outs[-1]


def _ag_pass(lands, recv_sems, after, l):
    n = len(lands)
    after = list(after) if isinstance(after, (list, tuple)) else [after]

    def body(*refs):
        zones, recv_sems = refs[:n], refs[n]
        psend, precv, token = refs[n + 1 + len(after)], refs[n + 2 + len(after)], refs[-1]
        me, _ = _peer(0)
        sibling, _ = _peer(1)
        for j, mask in enumerate((4, 2)):
            idx = _peer(mask)[1]
            for t in range(n):
                _row_copy(zones[t], idx, psend.at[j * n + t], recv_sems.at[(1 + j) * n + t], me).wait_recv()
                _row_copy(zones[t], idx, psend.at[j * n + t], precv.at[j * n + t], sibling).start()
        fwd_idx, fwd_dev = _diag_route()
        for t in range(n):
            _row_copy(zones[t], fwd_idx, psend.at[2 * n + t], precv.at[2 * n + t], fwd_dev).start()
        token[...] = jnp.zeros_like(token)

    outs = pl.pallas_call(
        body, name=f"ag_pass_{l}", in_specs=[_HBM] * n + [_SEM] + [_ANY] * len(after),
        out_specs=(_SEM, _SEM, *[_HBM] * n, pl.BlockSpec(memory_space=pltpu.VMEM)),
        out_shape=(pltpu.SemaphoreType.DMA((3 * n,)), pltpu.SemaphoreType.DMA((3 * n,)),
                   *[pltpu.HBM(a.shape, a.dtype) for a in lands], _TOKEN),
        input_output_aliases={t: 2 + t for t in range(n)}, compiler_params=_CP_SPLIT)(*lands, recv_sems, *after)
    return outs[0], outs[1], list(outs[2:2 + n]), outs[-1]


def _ag_last(lands, precv, after, l):
    n = len(lands)
    after = list(after) if isinstance(after, (list, tuple)) else [after]

    def body(*refs):
        zones, precv = refs[:n], refs[n]
        qsend, qrecv, token = refs[n + 1 + len(after)], refs[n + 2 + len(after)], refs[-1]
        me, _ = _peer(0)
        sibling, _ = _peer(1)
        idx = _peer(6)[1]
        for t in range(n):
            _row_copy(zones[t], idx, qsend.at[t], precv.at[2 * n + t], me).wait_recv()
            _row_copy(zones[t], idx, qsend.at[t], qrecv.at[t], sibling).start()
        token[...] = jnp.zeros_like(token)

    outs = pl.pallas_call(
        body, name=f"ag_last_{l}", in_specs=[_HBM] * n + [_SEM] + [_ANY] * len(after),
        out_specs=(_SEM, _SEM, *[_HBM] * n, pl.BlockSpec(memory_space=pltpu.VMEM)),
        out_shape=(pltpu.SemaphoreType.DMA((n,)), pltpu.SemaphoreType.DMA((n,)),
                   *[pltpu.HBM(a.shape, a.dtype) for a in lands], _TOKEN),
        input_output_aliases={t: 2 + t for t in range(n)}, compiler_params=_CP_SPLIT)(*lands, precv, *after)
    return outs[0], outs[1], list(outs[2:2 + n]), outs[-1]


def _ag_wait(lands, send_sems, recv_sems, psend, precv, qsend, qrecv, after, l):
    n = len(lands)
    after = list(after) if isinstance(after, (list, tuple)) else [after]

    def body(*refs):
        zones = refs[:n]
        send_sems, recv_sems, psend, precv, qsend, qrecv = refs[n:n + 6]
        me, me_idx = _peer(0)
        for k in range(3):
            for t in range(n):
                _row_copy(zones[t], me_idx, send_sems.at[k * n + t], recv_sems.at[k * n + t], me).wait_send()
        for t in range(n):
            _row_copy(zones[t], _peer(1)[1], send_sems.at[t], recv_sems.at[t], me).wait_recv()
        fwd_idx, _ = _diag_route()
        for j, (mine, theirs) in enumerate(((_peer(4)[1], _peer(5)[1]), (_peer(2)[1], _peer(3)[1]))):
            for t in range(n):
                _row_copy(zones[t], mine, psend.at[j * n + t], precv.at[j * n + t], me).wait_send()
                _row_copy(zones[t], theirs, psend.at[j * n + t], precv.at[j * n + t], me).wait_recv()
        for t in range(n):
            _row_copy(zones[t], fwd_idx, psend.at[2 * n + t], precv.at[2 * n + t], me).wait_send()
            _row_copy(zones[t], _peer(6)[1], qsend.at[t], qrecv.at[t], me).wait_send()
            _row_copy(zones[t], _peer(7)[1], qsend.at[t], qrecv.at[t], me).wait_recv()

    outs = pl.pallas_call(
        body, name=f"ag_wait_{l}", in_specs=[_HBM] * n + [_SEM] * 6 + [_ANY] * len(after),
        out_specs=tuple([_HBM] * n), out_shape=tuple(pltpu.HBM(a.shape, a.dtype) for a in lands),
        input_output_aliases={t: t for t in range(n)}, compiler_params=_CP_SPLIT)(
            *lands, send_sems, recv_sems, psend, precv, qsend, qrecv, *after)
    return list(outs)


def _xchg_src(ref, slot_ref, idx):
    return _rows(ref, idx) if ref.shape[0] == NDEV * slot_ref.shape[1] else ref


def _rs_start(srcs, slots, after, tag):
    n = len(srcs)
    after = list(after) if isinstance(after, (list, tuple)) else [after]

    def body(*refs):
        src, slot = refs[:n], refs[n:2 * n]
        send_sems, recv_sems, token = refs[2 * n + len(after)], refs[2 * n + len(after) + 1], refs[-1]
        _, me_idx = _peer(0)
        for k in range(1, NDEV):
            dev, idx = _peer(k)
            for t in range(n):
                pltpu.make_async_remote_copy(
                    src_ref=_xchg_src(src[t], slot[t], idx), dst_ref=slot[t].at[me_idx],
                    send_sem=send_sems.at[(k - 1) * n + t], recv_sem=recv_sems.at[(k - 1) * n + t],
                    device_id=dev, device_id_type=_MESH).start()
        token[...] = jnp.zeros_like(token)

    outs = pl.pallas_call(
        body, name=f"rs_start_{tag}", in_specs=[_HBM] * (2 * n) + [_ANY] * len(after),
        out_specs=(_SEM, _SEM, *[_HBM] * (2 * n), pl.BlockSpec(memory_space=pltpu.VMEM)),
        out_shape=(pltpu.SemaphoreType.DMA(((NDEV - 1) * n,)), pltpu.SemaphoreType.DMA(((NDEV - 1) * n,)),
                   *[pltpu.HBM(a.shape, a.dtype) for a in list(srcs) + list(slots)], _TOKEN),
        input_output_aliases={t: 2 + t for t in range(2 * n)}, compiler_params=_CP_SPLIT)(
            *[_hbm(a) for a in list(srcs) + list(slots)], *after)
    return outs[0], outs[1], list(outs[2:2 + n]), list(outs[2 + n:2 + 2 * n]), outs[-1]


def _rs_wait(srcs, slots, send_sems, recv_sems, after, tag):
    n = len(srcs)
    after = list(after) if isinstance(after, (list, tuple)) else [after]

    def body(*refs):
        src, slot, send_sems, recv_sems = refs[:n], refs[n:2 * n], refs[2 * n], refs[2 * n + 1]
        me, _ = _peer(0)
        for k in range(1, NDEV):
            idx = _peer(k)[1]
            for t in range(n):
                cp = pltpu.make_async_remote_copy(
                    src_ref=_xchg_src(src[t], slot[t], idx), dst_ref=slot[t].at[idx],
                    send_sem=send_sems.at[(k - 1) * n + t], recv_sem=recv_sems.at[(k - 1) * n + t],
                    device_id=me, device_id_type=_MESH)
                cp.wait_send()
                cp.wait_recv()

    outs = pl.pallas_call(
        body, name=f"rs_wait_{tag}", in_specs=[_HBM] * (2 * n) + [_SEM, _SEM] + [_ANY] * len(after),
        out_specs=tuple([_HBM] * (2 * n)),
        out_shape=tuple(pltpu.HBM(a.shape, a.dtype) for a in list(srcs) + list(slots)),
        input_output_aliases={t: t for t in range(2 * n)}, compiler_params=_CP_SPLIT)(
            *srcs, *slots, send_sems, recv_sems, *after)
    return list(outs[:n]), list(outs[n:])


def _pair_start(full4s, bufs, after, tag):
    n = len(full4s)
    after = list(after) if isinstance(after, (list, tuple)) else [after]

    def body(*refs):
        full, buf = refs[:n], refs[n:2 * n]
        send_sems, recv_sems, token = refs[2 * n + len(after)], refs[2 * n + len(after) + 1], refs[-1]
        c = lax.axis_index("c")
        for t in range(n):
            pltpu.make_async_remote_copy(src_ref=full[t].at[:, 1 - c], dst_ref=buf[t], send_sem=send_sems.at[t],
                                         recv_sem=recv_sems.at[t], device_id=_peer(1)[0], device_id_type=_MESH).start()
        token[...] = jnp.zeros_like(token)

    outs = pl.pallas_call(
        body, name=f"pair_start_{tag}", in_specs=[_HBM] * (2 * n) + [_ANY] * len(after),
        out_specs=(_SEM, _SEM, *[_HBM] * (2 * n), pl.BlockSpec(memory_space=pltpu.VMEM)),
        out_shape=(pltpu.SemaphoreType.DMA((n,)), pltpu.SemaphoreType.DMA((n,)),
                   *[pltpu.HBM(a.shape, a.dtype) for a in list(full4s) + list(bufs)], _TOKEN),
        input_output_aliases={t: 2 + t for t in range(2 * n)}, compiler_params=_CP_SPLIT)(
            *[_hbm(a) for a in list(full4s) + list(bufs)], *after)
    return outs[0], outs[1], list(outs[2:2 + n]), list(outs[2 + n:2 + 2 * n]), outs[-1]


def _pair_wait(full4s, bufs, send_sems, recv_sems, after, tag):
    n = len(full4s)
    after = list(after) if isinstance(after, (list, tuple)) else [after]

    def body(*refs):
        full, buf, send_sems, recv_sems = refs[:n], refs[n:2 * n], refs[2 * n], refs[2 * n + 1]
        c = lax.axis_index("c")
        for t in range(n):
            cp = pltpu.make_async_remote_copy(src_ref=full[t].at[:, 1 - c], dst_ref=buf[t], send_sem=send_sems.at[t],
                                              recv_sem=recv_sems.at[t], device_id=_peer(0)[0], device_id_type=_MESH)
            cp.wait_send()
            cp.wait_recv()

    outs = pl.pallas_call(
        body, name=f"pair_wait_{tag}", in_specs=[_HBM] * (2 * n) + [_SEM, _SEM] + [_ANY] * len(after),
        out_specs=tuple([_HBM] * (2 * n)),
        out_shape=tuple(pltpu.HBM(a.shape, a.dtype) for a in list(full4s) + list(bufs)),
        input_output_aliases={t: t for t in range(2 * n)}, compiler_params=_CP_SPLIT)(
            *full4s, *bufs, send_sems, recv_sems, *after)
    return list(outs[:n]), list(outs[n:])


def _pair_sum(core, full4s, bufs):
    n = len(full4s)

    def body(core_ref, *refs):
        for t in range(n):
            refs[2 * n + t][...] = (refs[t][...].astype(f32) + refs[n + t][...].astype(f32)).astype(bf16)

    grid_spec = pltpu.PrefetchScalarGridSpec(
        num_scalar_prefetch=1, grid=(4,),
        in_specs=[pl.BlockSpec((None, None) + a.shape[2:], lambda j, core_ref: (j, core_ref[0], 0, 0)) for a in full4s]
        + [pl.BlockSpec((None,) + b.shape[1:], lambda j, core_ref: (j, 0, 0)) for b in bufs],
        out_specs=[pl.BlockSpec((None,) + b.shape[1:], lambda j, core_ref: (j, 0, 0)) for b in bufs])
    return pl.pallas_call(
        body, grid_spec=grid_spec, out_shape=[SDS(b.shape, bf16) for b in bufs],
        compiler_params=_CP, name="pair_sum")(core, *full4s, *bufs)


def _chip_start(sums, slots, after, tag):
    n = len(sums)
    after = list(after) if isinstance(after, (list, tuple)) else [after]

    def body(*refs):
        src, slot = refs[:n], refs[n:2 * n]
        send_sems, recv_sems, token = refs[2 * n + len(after)], refs[2 * n + len(after) + 1], refs[-1]
        my_chip = 2 * lax.axis_index("x") + lax.axis_index("y")
        for k, mask in enumerate((4, 2, 6)):
            dev, _ = _peer(mask)
            for t in range(n):
                pltpu.make_async_remote_copy(
                    src_ref=src[t].at[2 * dev[0] + dev[1]], dst_ref=slot[t].at[my_chip],
                    send_sem=send_sems.at[k * n + t], recv_sem=recv_sems.at[k * n + t],
                    device_id=dev, device_id_type=_MESH).start()
        token[...] = jnp.zeros_like(token)

    outs = pl.pallas_call(
        body, name=f"chip_start_{tag}", in_specs=[_HBM] * (2 * n) + [_ANY] * len(after),
        out_specs=(_SEM, _SEM, *[_HBM] * (2 * n), pl.BlockSpec(memory_space=pltpu.VMEM)),
        out_shape=(pltpu.SemaphoreType.DMA((3 * n,)), pltpu.SemaphoreType.DMA((3 * n,)),
                   *[pltpu.HBM(a.shape, a.dtype) for a in list(sums) + list(slots)], _TOKEN),
        input_output_aliases={t: 2 + t for t in range(2 * n)}, compiler_params=_CP_SPLIT)(
            *[_hbm(a) for a in list(sums) + list(slots)], *after)
    return outs[0], outs[1], list(outs[2:2 + n]), list(outs[2 + n:2 + 2 * n]), outs[-1]


def _chip_wait(sums, slots, send_sems, recv_sems, after, tag):
    n = len(sums)
    after = list(after) if isinstance(after, (list, tuple)) else [after]

    def body(*refs):
        src, slot, send_sems, recv_sems = refs[:n], refs[n:2 * n], refs[2 * n], refs[2 * n + 1]
        for k, mask in enumerate((4, 2, 6)):
            dev, _ = _peer(mask)
            chip = 2 * dev[0] + dev[1]
            for t in range(n):
                cp = pltpu.make_async_remote_copy(
                    src_ref=src[t].at[chip], dst_ref=slot[t].at[chip],
                    send_sem=send_sems.at[k * n + t], recv_sem=recv_sems.at[k * n + t],
                    device_id=_peer(0)[0], device_id_type=_MESH)
                cp.wait_send()
                cp.wait_recv()

    outs = pl.pallas_call(
        body, name=f"chip_wait_{tag}", in_specs=[_HBM] * (2 * n) + [_SEM, _SEM] + [_ANY] * len(after),
        out_specs=tuple([_HBM] * (2 * n)),
        out_shape=tuple(pltpu.HBM(a.shape, a.dtype) for a in list(sums) + list(slots)),
        input_output_aliases={t: t for t in range(2 * n)}, compiler_params=_CP_SPLIT)(
            *sums, *slots, send_sems, recv_sems, *after)
    return list(outs[:n]), list(outs[n:])


def _sum_slots(slots, rb):
    r = slots.shape[1]

    def body(s_ref, o_ref):
        acc = s_ref[0].astype(f32)
        for s in range(1, NDEV):
            acc = acc + s_ref[s].astype(f32)
        o_ref[...] = acc

    return pl.pallas_call(
        body, grid=(r // rb,),
        in_specs=[pl.BlockSpec((NDEV, rb, D), lambda i: (0, i, 0))],
        out_specs=pl.BlockSpec((rb, D), lambda i: (i, 0)),
        out_shape=SDS((r, D), f32), compiler_params=_CP, name="sum_slots")(slots)


def _adamw(w, g, m, v):
    shape = w.shape
    cols = shape[-1]
    rows = w.size // cols
    rb = rows
    for cand in (512, 256, 128, 64, 32, 16, 8):
        if rows % cand == 0 and rows > cand:
            rb = cand
            break

    def body(w_ref, g_ref, m_ref, v_ref, d_ref, mo_ref, vo_ref):
        d_ref[...], mo_ref[...], vo_ref[...] = _adamw_math(w_ref[...], g_ref[...], m_ref[...], v_ref[...])

    spec = pl.BlockSpec((rb, cols), lambda i: (i, 0))
    outs = pl.pallas_call(
        body, grid=(rows // rb,), in_specs=[spec] * 4, out_specs=[spec] * 3,
        out_shape=[SDS((rows, cols), f32)] * 3, compiler_params=_CP, name="adamw")(
            *(a.reshape(rows, cols) for a in (w, g, m, v)))
    return tuple(o.reshape(shape) for o in outs)


def _adamw_math(w, g, m, v):
    m = ADAM_B1 * m + (1.0 - ADAM_B1) * g
    v = ADAM_B2 * v + (1.0 - ADAM_B2) * (g * g)
    m_hat = m / (1.0 - ADAM_B1 ** ADAM_STEP)
    v_hat = v / (1.0 - ADAM_B2 ** ADAM_STEP)
    return -ADAM_LR * (m_hat / (jnp.sqrt(v_hat) + ADAM_EPS) + ADAM_WD * w), m, v


def _reduce_adamw(acc, me, full, slots, w, m, v, l):
    _, r, _ = w.shape
    ns = slots.shape[0]
    rb = r // 2 if r > 128 else r

    def body(me_ref, full_ref, slots_ref, w_ref, m_ref, v_ref, *refs):
        go_ref, d_ref, mo_ref, vo_ref = refs[-4:]
        own = full_ref[...].astype(f32)
        g = None
        for s in range(ns):
            part = jnp.where(me_ref[0] == s, own, slots_ref[s].astype(f32))
            g = part if g is None else g + part
        go_ref[...] = g
        d_ref[...], mo_ref[...], vo_ref[...] = _adamw_math(w_ref[...], g, m_ref[...], v_ref[...])

    steps = r // rb
    lay = pl.BlockSpec((None, rb, D), lambda i, me_ref: (l, i, 0))
    n_acc = 0 if acc is None else 4
    grid_spec = pltpu.PrefetchScalarGridSpec(
        num_scalar_prefetch=1, grid=(steps,),
        in_specs=[pl.BlockSpec((rb, D), lambda i, me_ref: (me_ref[0] * steps + i, 0)),
                  pl.BlockSpec((ns, rb, D), lambda i, me_ref: (0, i, 0)), lay, lay, lay] + [_ANY] * n_acc,
        out_specs=[lay] * 4)
    outs = pl.pallas_call(
        body, grid_spec=grid_spec, out_shape=[SDS(w.shape, f32)] * 4,
        input_output_aliases={6 + j: j for j in range(n_acc)},
        compiler_params=_CP, name="reduce_adamw")(me, full, slots, w, m, v, *(() if acc is None else acc))
    return tuple(outs)


_BIG = ("ffn1_w_gate", "ffn1_w_up", "ffn1_w_down", "w_in", "w_out", "ffn2_w_gate", "ffn2_w_up", "ffn2_w_down")
_TRANSPOSED = ("ffn1_w_gate", "ffn1_w_up", "w_in", "ffn2_w_gate", "ffn2_w_up")

def _block_diag(pool_w):
    out = jnp.zeros((L, PW, PW), pool_w.dtype)
    for gi in range(4):
        out = out.at[:, 64 * gi:64 * (gi + 1), 64 * gi:64 * (gi + 1)].set(pool_w[:, gi])
    return out


def kernel(x, positions, ffn1_norm, ffn1_w_gate, ffn1_w_up, ffn1_w_down, mix_norm, w_in, pool_w, pool_scale, w_out, ffn2_norm, ffn2_w_gate, ffn2_w_up, ffn2_w_down, final_norm, loss_target, m_ffn1_norm, m_ffn1_w_gate, m_ffn1_w_up, m_ffn1_w_down, m_mix_norm, m_w_in, m_pool_w, m_pool_scale, m_w_out, m_ffn2_norm, m_ffn2_w_gate, m_ffn2_w_up, m_ffn2_w_down, m_final_norm, v_ffn1_norm, v_ffn1_w_gate, v_ffn1_w_up, v_ffn1_w_down, v_mix_norm, v_w_in, v_pool_w, v_pool_scale, v_w_out, v_ffn2_norm, v_ffn2_w_gate, v_ffn2_w_up, v_ffn2_w_down, v_final_norm):
    weights = dict(ffn1_norm=ffn1_norm, ffn1_w_gate=ffn1_w_gate, ffn1_w_up=ffn1_w_up, ffn1_w_down=ffn1_w_down,
                   mix_norm=mix_norm, w_in=w_in, pool_w=pool_w, pool_scale=pool_scale, w_out=w_out,
                   ffn2_norm=ffn2_norm, ffn2_w_gate=ffn2_w_gate, ffn2_w_up=ffn2_w_up, ffn2_w_down=ffn2_w_down,
                   final_norm=final_norm)
    moms = dict(ffn1_norm=m_ffn1_norm, ffn1_w_gate=m_ffn1_w_gate, ffn1_w_up=m_ffn1_w_up, ffn1_w_down=m_ffn1_w_down,
                mix_norm=m_mix_norm, w_in=m_w_in, pool_w=m_pool_w, pool_scale=m_pool_scale, w_out=m_w_out,
                ffn2_norm=m_ffn2_norm, ffn2_w_gate=m_ffn2_w_gate, ffn2_w_up=m_ffn2_w_up, ffn2_w_down=m_ffn2_w_down,
                final_norm=m_final_norm)
    vels = dict(ffn1_norm=v_ffn1_norm, ffn1_w_gate=v_ffn1_w_gate, ffn1_w_up=v_ffn1_w_up, ffn1_w_down=v_ffn1_w_down,
                mix_norm=v_mix_norm, w_in=v_w_in, pool_w=v_pool_w, pool_scale=v_pool_scale, w_out=v_w_out,
                ffn2_norm=v_ffn2_norm, ffn2_w_gate=v_ffn2_w_gate, ffn2_w_up=v_ffn2_w_up, ffn2_w_down=v_ffn2_w_down,
                final_norm=v_final_norm)
    names = list(weights)

    me_idx = 4 * lax.axis_index("x") + 2 * lax.axis_index("y") + lax.axis_index("c")
    me_arr = me_idx.reshape(1).astype(jnp.int32)

    as_rows = lambda a, nm: jnp.swapaxes(a, 1, 2) if nm in _TRANSPOSED else a
    w_rows = {nm: as_rows(weights[nm], nm) for nm in _BIG}
    m_rows = {nm: as_rows(moms[nm], nm) for nm in _BIG}
    v_rows = {nm: as_rows(vels[nm], nm) for nm in _BIG}

    def landing_zones(l, which):
        return _place_own(me_arr, [w_rows[_BIG[t]] for t in which], l)

    g_ffn1 = [ffn1_norm[l].reshape(1, D) for l in range(L)]
    g_mix = [mix_norm[l].reshape(1, D) for l in range(L)]
    g_ffn2 = [ffn2_norm[l].reshape(1, D) for l in range(L)]
    wbd_all = _block_diag(pool_w).astype(bf16)
    wbd = [wbd_all[l] for l in range(L)]
    pscale = [pool_scale[l].reshape(1, PW) for l in range(L)]
    tabs = _rope_tables(positions)
    flat = lambda a: a.reshape(S, a.shape[-1])
    r4 = lambda a: a.reshape(4, S // 4, a.shape[-1])
    r16 = lambda a: a.reshape(16, S // 16, a.shape[-1])

    first, rest, whole = (0, 1, 2, 3, 4), (5, 6, 7), tuple(range(8))

    def ag_begin(l, which, after, zones=None):
        tag = f"{l}{'' if which == whole else 'h' if which == first else 'r'}"
        zones = landing_zones(l, which) if zones is None else zones
        send_sems, recv_sems, zones, token = _ag_start(zones, after, tag)
        return dict(tag=tag, zones=zones, s=send_sems, r=recv_sems), token

    def ag_second(ch, after):
        ch["ps"], ch["pr"], ch["zones"], token = _ag_pass(ch["zones"], ch["r"], after, ch["tag"])
        return token

    def ag_third(ch, after):
        ch["qs"], ch["qr"], ch["zones"], token = _ag_last(ch["zones"], ch["pr"], after, ch["tag"])
        return token

    def ag_end(ch, after):
        return _ag_wait(ch["zones"], ch["s"], ch["r"], ch["ps"], ch["pr"], ch["qs"], ch["qr"], after, ch["tag"])

    h_send, h_recv, h_zones, _ = _ag_start_direct(landing_zones(0, first), [], "0h")
    zones_rest, zones_next = landing_zones(0, rest), landing_zones(1, whole)
    early_zones = {ll: landing_zones(ll, whole) for ll in range(2, L)}
    fill = [z for zs in (zones_rest, zones_next, *early_zones.values(), tabs, wbd) for z in zs]
    h_psend, h_precv, h_zones, token = _ag_pass_direct(h_zones, h_recv, fill, "0h")
    head = _ag_wait_direct(h_zones, h_send, h_recv, h_psend, h_precv, token, "0h")
    ch_rest, tok_rest = ag_begin(0, rest, head[0], zones_rest)
    chains = {}
    chains[1], tok_next = ag_begin(1, whole, head[0], zones_next)
    gathered = [None] * L
    xs = x.reshape(S, D)
    saved = []
    for l in range(L):
        first_after, second_after = (), ()
        if l == 0:
            gt1, ut1, dn1, wint, wout = head
            first_after = (tok_rest, tok_next)
        else:
            gt1, ut1, dn1, wint, wout, gt2, ut2, dn2 = gathered[l]
        x0 = xs
        x1, gate1, up1 = _ffn_fwd(x0, g_ffn1[l], gt1, ut1, dn1, after=first_after)
        hmix, vp, q1, k1, v1, q4, k4, v4, q16, k16, v16 = _mix_in_fwd(x1, g_mix[l], wint, tabs)
        q4, k4, v4, q16, k16, v16 = map(flat, (q4, k4, v4, q16, k16, v16))
        ypool, diff = _pool_fwd(vp, wbd[l], pscale[l])
        after_attn = None
        if l == 0:
            after_attn = ag_second(ch_rest, [ypool, q16])
        o1, l1 = _attn_fwd(q1, k1, v1, S, after=after_attn)
        o4, l4 = _attn_fwd(q4, k4, v4, S // 4, after=after_attn)
        o16, l16 = _attn_fwd(q16, k16, v16, S // 16, after=after_attn)
        if 0 < l < L - 1:
            second_after = (ag_second(chains[l + 1], [o1, o4, o16]),)
        x2, mixed, o, lse1, lse4, lse16 = _mix_out_fwd(x1, ypool, o1, l1, r4(o4), r4(l4), r16(o16), r16(l16), wout)
        if l == 0:
            token = ag_third(ch_rest, x2)
            gt2, ut2, dn2 = ag_end(ch_rest, token)
            gathered[0] = list(head) + [gt2, ut2, dn2]
            second_after = (ag_second(chains[1], gt2),)
        x3, gate2, up2 = _ffn_fwd(x2, g_ffn2[l], gt2, ut2, dn2, after=second_after)
        if l + 1 < L:
            token = ag_third(chains[l + 1], x3)
            if l + 2 < L:
                chains[l + 2], token = ag_begin(l + 2, whole, token, early_zones[l + 2])
            gathered[l + 1] = ag_end(chains[l + 1], token)
        saved.append(dict(x0=x0, x1=x1, x2=x2, gate1=gate1, up1=up1, gate2=gate2, up2=up2, hmix=hmix, diff=diff,
                          qkv=((q1, k1, v1), (q4, k4, v4), (q16, k16, v16)), mixed=mixed, o=o,
                          lse=(lse1, flat(lse4), flat(lse16))))
        xs = x3

    dx, loss_part, d_final = _loss_head(xs, final_norm.reshape(1, D), loss_target.reshape(S, D))

    d_norm = {nm: [None] * L for nm in ("ffn1_norm", "mix_norm", "ffn2_norm")}
    d_poolw, d_pscale = [None] * L, [None] * L
    group_a = ("ffn2_w_gate", "ffn2_w_up", "ffn2_w_down", "w_out")
    group_b = ("ffn1_w_gate", "ffn1_w_up", "ffn1_w_down", "w_in")
    acc = {}

    def exchange(full, group, after, tag):
        srcs = [full[nm] for nm in group]
        slots = [lax.empty((NDEV, g.shape[0] // NDEV, D), bf16) for g in srcs]
        ssem, rsem, srcs, slots, token = _rs_start(srcs, slots, after, tag)
        return (srcs, slots, ssem, rsem, tag), token

    def update(l, group, flight, after):
        srcs, slots, ssem, rsem, tag = flight
        srcs, slots = _rs_wait(srcs, slots, ssem, rsem, after, tag)
        for nm, full_g, slots_g in zip(group, srcs, slots):
            acc[nm] = _reduce_adamw(acc.get(nm), me_arr, full_g, slots_g, w_rows[nm], m_rows[nm], v_rows[nm], l)
        return [acc[nm][0] for nm in group], slots

    core_arr = lax.axis_index("c").reshape(1).astype(jnp.int32)
    chip_arr = (2 * lax.axis_index("x") + lax.axis_index("y")).reshape(1).astype(jnp.int32)

    def exchange_cores(full, group, after, tag):
        full4s = [full[nm].reshape(4, 2, full[nm].shape[0] // NDEV, D) for nm in group]
        bufs = [lax.empty((4,) + a.shape[2:], bf16) for a in full4s]
        ssem, rsem, full4s, bufs, token = _pair_start(full4s, bufs, after, tag)
        return (full4s, bufs, ssem, rsem, tag), token

    def exchange_chips(flight, after):
        full4s, bufs, ssem, rsem, tag = flight
        full4s, bufs = _pair_wait(full4s, bufs, ssem, rsem, after, tag)
        sums = _pair_sum(core_arr, full4s, bufs)
        slots = [lax.empty(a.shape, bf16) for a in sums]
        ssem, rsem, sums, slots, token = _chip_start(sums, slots, bufs[0], tag)
        return (sums, slots, ssem, rsem, tag), token

    def update_chips(l, group, flight, after):
        sums, slots, ssem, rsem, tag = flight
        sums, slots = _chip_wait(sums, slots, ssem, rsem, after, tag)
        for nm, sums_g, slots_g in zip(group, sums, slots):
            own = sums_g.reshape(4 * sums_g.shape[1], D)
            acc[nm] = _reduce_adamw(acc.get(nm), chip_arr, own, slots_g, w_rows[nm], m_rows[nm], v_rows[nm], l)
        return [acc[nm][0] for nm in group]

    flights = {}
    token_b = None
    for l in reversed(range(L)):
        sv = saved[l]
        gt1, ut1, dn1, wint, wout, gt2, ut2, dn2 = gathered[l]
        full = {}
        dx, dgate, dup, h, dy, d_norm["ffn2_norm"][l] = _ffn_bwd_d(
            sv["x2"], g_ffn2[l], sv["gate2"], sv["up2"], dx, gt2, ut2, dn2, after=() if token_b is None else (token_b,))
        full["ffn2_w_gate"], full["ffn2_w_up"], full["ffn2_w_down"] = _ffn_bwd_w(h, dy, sv["gate2"], sv["up2"], dgate, dup)

        dxb, dyp, do1, do4, do16, dl1, dl4, dl16 = _mix_out_bwd(dx, sv["o"], wout)
        full["w_out"] = _wgrad(sv["mixed"], dxb)
        flights[l, "a"], token_a = (exchange_cores if l == 0 else exchange)(full, group_a, dxb, f"a{l}")
        dvp, d_poolw[l], d_pscale[l] = _pool_bwd(dyp, sv["diff"], wbd[l], pscale[l], after=(token_a,))
        dos, dls = (do1, flat(do4), flat(do16)), (dl1, flat(dl4), flat(dl16))
        dqkv = []
        for b, lc in enumerate((S, S // 4, S // 16)):
            qb, kb, vb = sv["qkv"][b]
            dqkv.append(_attn_bwd(qb, kb, vb, dos[b], sv["lse"][b], dls[b], lc))
        d4 = tuple(r4(a) for a in dqkv[1])
        d16 = tuple(r16(a) for a in dqkv[2])
        mix_after = ()
        if l == 0:
            flights[0, "a"], token_a = exchange_chips(flights[0, "a"], [dqkv[0][0], dqkv[1][0], dqkv[2][0]])
            mix_after = (token_a,)
        dx, dproj, d_norm["mix_norm"][l] = _mix_in_bwd(dx, sv["x1"], g_mix[l], wint, tabs, dvp, dqkv[0], d4, d16,
                                                       after=mix_after)
        full["w_in"] = _wgrad(dproj, sv["hmix"])

        dx, dgate, dup, h, dy, d_norm["ffn1_norm"][l] = _ffn_bwd_d(sv["x0"], g_ffn1[l], sv["gate1"], sv["up1"], dx, gt1, ut1, dn1)
        full["ffn1_w_gate"], full["ffn1_w_up"], full["ffn1_w_down"] = _ffn_bwd_w(h, dy, sv["gate1"], sv["up1"], dgate, dup)

        after = dx
        if l + 1 < L and l + 1 >= 2:
            after, _ = update(l + 1, group_a, flights.pop((l + 1, "a")), after)
        if l + 1 < L and l + 1 >= 3:
            after, _ = update(l + 1, group_b, flights.pop((l + 1, "b")), after)
        if l > 0:
            flights[l, "b"], token_b = exchange(full, group_b, after, f"b{l}")

    flights[0, "b"], token_b = exchange_cores(full, group_b, dx, "b0")
    pad8 = lambda a: jnp.pad(a, ((0, 8 - a.shape[0]), (0, 0)))
    misc = jnp.concatenate([d_final, jnp.concatenate(d_pscale, axis=1), loss_part], axis=0)
    small = jnp.concatenate(
        [pad8(jnp.concatenate(d_norm[nm], axis=0)) for nm in ("ffn1_norm", "mix_norm", "ffn2_norm")]
        + [pad8(misc), jnp.stack(d_poolw).reshape(L * 16, D)], axis=0)
    small_slots = lax.dynamic_update_slice(lax.empty((NDEV, SMALL_ROWS, D), f32), small[None], (me_idx, 0, 0))
    pack_sems = _rs_start([small], [small_slots], token_b, "pack")
    flights[0, "b"], token_b = exchange_chips(flights[0, "b"], pack_sems[-1])

    after = token_b
    for key in [(2, "b"), (1, "a"), (1, "b")]:
        after, _ = update(key[0], group_a if key[1] == "a" else group_b, flights.pop(key), after)
    _, pack_slots = _rs_wait(pack_sems[2], pack_sems[3], pack_sems[0], pack_sems[1], after, "pack")
    sm = _sum_slots(pack_slots[0], SMALL_ROWS)
    grads = {}
    grads["ffn1_norm"], grads["mix_norm"], grads["ffn2_norm"] = sm[0:L], sm[8:8 + L], sm[16:16 + L]
    grads["final_norm"] = sm[24]
    grads["pool_scale"] = sm[25].reshape(L, PW)
    grads["pool_w"] = sm[32:32 + L * 16].reshape(L, 4, 64, 64)
    loss = sm[26, 0]
    upd = {nm: _adamw(weights[nm], grads[nm], moms[nm], vels[nm]) for nm in names if nm not in _BIG}
    after = update_chips(0, group_a, flights.pop((0, "a")), [upd[nm][0] for nm in upd])
    update_chips(0, group_b, flights.pop((0, "b")), after)
    for nm in _BIG:
        grads[nm], upd[nm] = as_rows(acc[nm][0], nm), tuple(as_rows(a, nm) for a in acc[nm][1:])
    return (loss, dx.reshape(1, S, D), *[grads[nm] for nm in names], *[upd[nm][0] for nm in names],
            *[upd[nm][1] for nm in names], *[upd[nm][2] for nm in names])
```

```python
import jax
import jax.numpy as jnp
from jax import lax
from jax.experimental import pallas as pl
from jax.experimental.pallas import tpu as pltpu

f32 = jnp.float32
bf16 = jnp.bfloat16
SDS = jax.ShapeDtypeStruct

D = 1024
S = 2048
F = 2816
L = 4
PW = 256
AW = 768
PROJ = PW + 3 * AW
NDEV = 8
TM = 256
QB = 128
HALF = 64
NG = AW // 128
NORM_EPS = 1e-6
MASK_VALUE = -1e30
ROPE_THETA = 500000.0
ADAM_LR, ADAM_B1, ADAM_B2, ADAM_EPS, ADAM_WD, ADAM_STEP = 0.001, 0.9, 0.999, 1e-08, 0.01, 10
POOL_WINDOWS = (2, 4, 8, 16)
PAD = 8
SMALL_ROWS = 96
VMEM_LIMIT = 56 * 1024 * 1024

_CP = pltpu.CompilerParams(vmem_limit_bytes=VMEM_LIMIT)
_ANY = pl.BlockSpec(memory_space=pl.ANY)
_HBM = pl.BlockSpec(memory_space=pltpu.HBM)
_SEM = pl.BlockSpec(memory_space=pltpu.SEMAPHORE)
_MESH = pl.DeviceIdType.MESH
_CP_SPLIT = pltpu.CompilerParams(has_side_effects=pltpu.SideEffectType.DATAFLOW_SIDE_EFFECTING)


def _dot_nn(a, b):
    return lax.dot_general(a, b, (((1,), (0,)), ((), ())), preferred_element_type=f32)


def _dot_nt(a, b):
    return lax.dot_general(a, b, (((1,), (1,)), ((), ())), preferred_element_type=f32)


def _dot_tn(a, b):
    return lax.dot_general(a, b, (((0,), (0,)), ((), ())), preferred_element_type=f32)


def _rms(x, g):
    r = lax.rsqrt(jnp.mean(x * x, axis=-1, keepdims=True) + NORM_EPS)
    xh = x * r
    return r, xh, xh * g


def _rms_bwd(dh, r, xh, g):
    dxh = dh * g
    return r * (dxh - xh * jnp.mean(dxh * xh, axis=-1, keepdims=True))


def _tile(cols, rows=TM):
    return pl.BlockSpec((rows, cols), lambda i: (i, 0))


def _const(shape):
    return pl.BlockSpec(shape, lambda i: (0,) * len(shape))


def _layer(rows, cols):
    return pl.BlockSpec((rows, cols), lambda i: (0, 0), pipeline_mode=pl.Buffered(1))


def _p4(cols=AW):
    return pl.BlockSpec((4, TM // 4, cols), lambda i: (0, i, 0))


def _p16(cols=AW):
    return pl.BlockSpec((16, TM // 16, cols), lambda i: (0, i, 0))


def _cols(j):
    return slice(128 * j, 128 * (j + 1))


def _follow(body, n_in, after):
    k = len(after)
    return body if k == 0 else (lambda *refs: body(*refs[:n_in], *refs[n_in + k:]))


def _ffn_fwd(x, g, gt, ut, dn, after=()):
    def body(x_ref, g_ref, gt_ref, ut_ref, dn_ref, xo_ref, gate_ref, up_ref):
        x = x_ref[...]
        _, _, hn = _rms(x, g_ref[...])
        h = hn.astype(bf16)
        gate = _dot_nt(h, gt_ref[...])
        up = _dot_nt(h, ut_ref[...])
        gate_ref[...] = gate.astype(bf16)
        up_ref[...] = up.astype(bf16)
        a = (gate * jax.nn.sigmoid(gate) * up).astype(bf16)
        xo_ref[...] = x + 0.5 * _dot_nn(a, dn_ref[...])

    rows = 2 * TM
    return pl.pallas_call(
        _follow(body, 5, after), grid=(S // rows,),
        in_specs=[_tile(D, rows), _layer(1, D), _layer(F, D), _layer(F, D), _layer(F, D)] + [_ANY] * len(after),
        out_specs=[_tile(D, rows), _tile(F, rows), _tile(F, rows)],
        out_shape=[SDS((S, D), f32), SDS((S, F), bf16), SDS((S, F), bf16)],
        compiler_params=_CP, name="ffn_fwd")(x, g, gt, ut, dn, *after)


def _ffn_bwd_d(x, g, gate, up, dxo, gt, ut, dn, after=()):
    def body(x_ref, g_ref, gate_ref, up_ref, dxo_ref, gt_ref, ut_ref, dn_ref,
             dx_ref, dgate_ref, dup_ref, h_ref, dy_ref, dg_ref):
        x = x_ref[...]
        g = g_ref[...]
        r, xh, hn = _rms(x, g)
        h_ref[...] = hn.astype(bf16)
        dxo = dxo_ref[...]
        dy = (0.5 * dxo).astype(bf16)
        dy_ref[...] = dy
        da = _dot_nt(dy, dn_ref[...])
        gate = gate_ref[...].astype(f32)
        up = up_ref[...].astype(f32)
        sg = jax.nn.sigmoid(gate)
        dgate = (da * up * (sg * (1.0 + gate * (1.0 - sg)))).astype(bf16)
        dup = (da * (gate * sg)).astype(bf16)
        dgate_ref[...] = dgate
        dup_ref[...] = dup
        dh = _dot_nn(dgate, gt_ref[...]) + _dot_nn(dup, ut_ref[...])

        @pl.when(pl.program_id(0) == 0)
        def _():
            dg_ref[...] = jnp.zeros_like(dg_ref)

        dg_ref[...] += jnp.sum(dh * xh, axis=0, keepdims=True)
        dx_ref[...] = dxo + _rms_bwd(dh, r, xh, g)

    return pl.pallas_call(
        _follow(body, 8, after), grid=(S // TM,),
        in_specs=[_tile(D), _layer(1, D), _tile(F), _tile(F), _tile(D),
                  _layer(F, D), _layer(F, D), _layer(F, D)] + [_ANY] * len(after),
        out_specs=[_tile(D), _tile(F), _tile(F), _tile(D), _tile(D), _const((1, D))],
        out_shape=[SDS((S, D), f32), SDS((S, F), bf16), SDS((S, F), bf16), SDS((S, D), bf16),
                   SDS((S, D), bf16), SDS((1, D), f32)],
        compiler_params=_CP, name="ffn_bwd_d")(x, g, gate, up, dxo, gt, ut, dn, *after)


def _ffn_bwd_w(h, dy, gate, up, dgate, dup):
    fc = 256

    def body(h_ref, dy_ref, gate_ref, up_ref, dgate_ref, dup_ref, dgt_ref, dut_ref, ddn_ref):
        gate = gate_ref[...].astype(f32)
        a = (gate * jax.nn.sigmoid(gate) * up_ref[...].astype(f32)).astype(bf16)
        ddn_ref[...] = _dot_tn(a, dy_ref[...]).astype(bf16)
        h = h_ref[...]
        dgt_ref[...] = _dot_tn(dgate_ref[...], h).astype(bf16)
        dut_ref[...] = _dot_tn(dup_ref[...], h).astype(bf16)

    col = pl.BlockSpec((S, fc), lambda j: (0, j))
    row = pl.BlockSpec((fc, D), lambda j: (j, 0))
    full = pl.BlockSpec((S, D), lambda j: (0, 0))
    return pl.pallas_call(
        body, grid=(F // fc,),
        in_specs=[full, full, col, col, col, col],
        out_specs=[row, row, row],
        out_shape=[SDS((F, D), bf16)] * 3,
        compiler_params=_CP, name="ffn_bwd_w")(h, dy, gate, up, dgate, dup)


def _wgrad(a, b):
    m, n = a.shape[1], b.shape[1]
    mc = 2 * TM

    def body(a_ref, b_ref, o_ref):
        o_ref[...] = _dot_tn(a_ref[...], b_ref[...]).astype(bf16)

    return pl.pallas_call(
        body, grid=(m // mc,),
        in_specs=[pl.BlockSpec((S, mc), lambda j: (0, j)), pl.BlockSpec((S, n), lambda j: (0, 0))],
        out_specs=pl.BlockSpec((mc, n), lambda j: (j, 0)),
        out_shape=SDS((m, n), bf16),
        compiler_params=_CP, name="wgrad")(a, b)


def _rope(t, c, sn, sp):
    return t * c + pltpu.roll(t, 120, 1) * sn + pltpu.roll(t, 8, 1) * sp


def _rope_bwd(d, c, sn, sp):
    return d * c + pltpu.roll(d * sn, 8, 1) + pltpu.roll(d * sp, 120, 1)


def _rope_tables(positions):
    inv_freq = ROPE_THETA ** (-jnp.arange(0, 16, 2, dtype=f32) / 16)
    ang = positions.reshape(S, 1).astype(f32) * inv_freq
    cos, sin = jnp.cos(ang), jnp.sin(ang)
    one = jnp.ones((S, 48), f32)
    zero8 = jnp.zeros((S, 8), f32)
    zero48 = jnp.zeros((S, 48), f32)
    c = jnp.concatenate([cos, cos, one], axis=1)
    sn = jnp.concatenate([-sin, zero8, zero48], axis=1)
    sp = jnp.concatenate([zero8, sin, zero48], axis=1)
    return tuple(jnp.concatenate([t, t], axis=1) for t in (c, sn, sp))


def _dilation_perm(n, back=False):
    per = TM // n
    i = lax.broadcasted_iota(jnp.int32, (TM, TM), 1 if back else 0)
    j = lax.broadcasted_iota(jnp.int32, (TM, TM), 0 if back else 1)
    return jnp.where(j == n * (i % per) + i // per, 1.0, 0.0).astype(bf16)


def _mix_in_fwd(x, g, wint, tabs):
    def body(x_ref, g_ref, w_ref, c_ref, sn_ref, sp_ref,
             h_ref, vp_ref, q1, k1, v1, q4, k4, v4, q16, k16, v16):
        _, _, hn = _rms(x_ref[...], g_ref[...])
        h = hn.astype(bf16)
        h_ref[...] = h
        proj = _dot_nt(h, w_ref[...])
        vp_ref[...] = proj[:, :PW]
        c, sn, sp = c_ref[...], sn_ref[...], sp_ref[...]
        perm4, perm16 = _dilation_perm(4), _dilation_perm(16)
        for kind, (o1, o4, o16) in enumerate(((q1, q4, q16), (k1, k4, k16), (v1, v4, v16))):
            for j in range(NG):
                t = proj[:, PW + kind * AW + 128 * j: PW + kind * AW + 128 * (j + 1)]
                if kind == 0:
                    t = _rope(t, c, sn, sp) * 0.125
                elif kind == 1:
                    t = _rope(t, c, sn, sp)
                o1[:, _cols(j)] = t.astype(bf16)
            nat = o1[...]
            o4[...] = _dot_nn(perm4, nat).astype(bf16).reshape(4, TM // 4, AW)
            o16[...] = _dot_nn(perm16, nat).astype(bf16).reshape(16, TM // 16, AW)

    nat, d4, d16 = SDS((S, AW), bf16), SDS((4, S // 4, AW), bf16), SDS((16, S // 16, AW), bf16)
    return pl.pallas_call(
        body, grid=(S // TM,),
        in_specs=[_tile(D), _layer(1, D), _layer(PROJ, D), _tile(128), _tile(128), _tile(128)],
        out_specs=[_tile(D), _tile(PW)] + [_tile(AW)] * 3 + [_p4()] * 3 + [_p16()] * 3,
        out_shape=[SDS((S, D), bf16), SDS((S, PW), f32)] + [nat] * 3 + [d4] * 3 + [d16] * 3,
        compiler_params=_CP, name="mix_in_fwd")(x, g, wint, *tabs)


def _mix_in_bwd(dxo, x, g, wint, tabs, dvp, d1, d4, d16, after=()):
    def body(dxo_ref, x_ref, g_ref, w_ref, c_ref, sn_ref, sp_ref, dvp_ref,
             dq1, dk1, dv1, dq4, dk4, dv4, dq16, dk16, dv16,
             dx_ref, dproj_ref, dg_ref):
        c, sn, sp = c_ref[...], sn_ref[...], sp_ref[...]
        dproj_ref[:, :PW] = dvp_ref[...].astype(bf16)
        back4, back16 = _dilation_perm(4, True), _dilation_perm(16, True)
        for kind, (a1, a4, a16) in enumerate(((dq1, dq4, dq16), (dk1, dk4, dk16), (dv1, dv4, dv16))):
            n4 = _dot_nn(back4, a4[...].reshape(TM, AW))
            n16 = _dot_nn(back16, a16[...].reshape(TM, AW))
            for j in range(NG):
                t = a1[:, _cols(j)].astype(f32) + n4[:, _cols(j)] + n16[:, _cols(j)]
                if kind == 0:
                    t = _rope_bwd(t * 0.125, c, sn, sp)
                elif kind == 1:
                    t = _rope_bwd(t, c, sn, sp)
                dproj_ref[:, PW + kind * AW + 128 * j: PW + kind * AW + 128 * (j + 1)] = t.astype(bf16)
        g = g_ref[...]
        r_, xh, _ = _rms(x_ref[...], g)
        dh = _dot_nn(dproj_ref[...], w_ref[...])

        @pl.when(pl.program_id(0) == 0)
        def _():
            dg_ref[...] = jnp.zeros_like(dg_ref)

        dg_ref[...] += jnp.sum(dh * xh, axis=0, keepdims=True)
        dx_ref[...] = dxo_ref[...] + _rms_bwd(dh, r_, xh, g)

    return pl.pallas_call(
        _follow(body, 17, after), grid=(S // TM,),
        in_specs=[_tile(D), _tile(D), _layer(1, D), _layer(PROJ, D), _tile(128), _tile(128), _tile(128),
                  _tile(PW)] + [_tile(AW)] * 3 + [_p4()] * 3 + [_p16()] * 3 + [_ANY] * len(after),
        out_specs=[_tile(D), _tile(PROJ), _const((1, D))],
        out_shape=[SDS((S, D), f32), SDS((S, PROJ), bf16), SDS((1, D), f32)],
        compiler_params=_CP, name="mix_in_bwd")(dxo, x, g, wint, *tabs, dvp, *d1, *d4, *d16, *after)


def _pool_sums(pad_ref, base, rows, adjoint):
    lane_group = lax.broadcasted_iota(jnp.int32, (rows, PW), 1) // 64
    sign = -1 if adjoint else 1

    def sh(o):
        return pad_ref[pl.ds(PAD + base + sign * o, rows), :]

    out = None
    acc = None
    lo, hi = 0, 0
    for gi, w in enumerate(POOL_WINDOWS):
        for o in list(range(-(w // 2), lo)) + list(range(hi, w - w // 2)):
            acc = sh(o) if acc is None else acc + sh(o)
        lo, hi = -(w // 2), w - w // 2
        out = acc if out is None else jnp.where(lane_group >= gi, acc, out)
    return out


def _pool_counts(base, rows):
    pos = base + lax.broadcasted_iota(jnp.int32, (rows, PW), 0)
    lane_group = lax.broadcasted_iota(jnp.int32, (rows, PW), 1) // 64
    cnt = None
    for gi, w in enumerate(POOL_WINDOWS):
        lo = jnp.maximum(pos - w // 2, 0)
        hi = jnp.minimum(pos + w - 1 - w // 2, S - 1)
        c = (hi - lo + 1).astype(f32)
        cnt = c if cnt is None else jnp.where(lane_group >= gi, c, cnt)
    return cnt


def _pool_fwd(vp, wbd, scale):
    ch = 256

    def body(vp_ref, w_ref, sc_ref, y_ref, diff_ref, pad):
        pad[pl.ds(0, PAD), :] = jnp.zeros((PAD, PW), f32)
        pad[pl.ds(PAD + S, PAD), :] = jnp.zeros((PAD, PW), f32)
        pad[pl.ds(PAD, S), :] = vp_ref[...]
        for b in range(S // ch):
            base = b * ch
            pooled = _pool_sums(pad, base, ch, False) / _pool_counts(base, ch)
            diff = (pooled - vp_ref[pl.ds(base, ch), :]).astype(bf16)
            diff_ref[pl.ds(base, ch), :] = diff
            y_ref[pl.ds(base, ch), :] = _dot_nn(diff, w_ref[...]) * sc_ref[...]

    whole = lambda shape: pl.BlockSpec(shape, lambda i: (0,) * len(shape))
    return pl.pallas_call(
        body, grid=(1,),
        in_specs=[whole((S, PW)), whole((PW, PW)), whole((1, PW))],
        out_specs=[whole((S, PW)), whole((S, PW))],
        out_shape=[SDS((S, PW), f32), SDS((S, PW), bf16)],
        scratch_shapes=[pltpu.VMEM((S + 2 * PAD, PW), f32)],
        compiler_params=_CP, name="pool_fwd")(vp, wbd, scale)


def _pool_bwd(dy, diff, wbd, scale, after=()):
    ch = 256

    def body(dy_ref, diff_ref, w_ref, sc_ref, dvp_ref, dw_ref, dsc_ref, pad):
        pad[pl.ds(0, PAD), :] = jnp.zeros((PAD, PW), f32)
        pad[pl.ds(PAD + S, PAD), :] = jnp.zeros((PAD, PW), f32)
        dw = jnp.zeros((PW, PW), f32)
        dsc = jnp.zeros((1, PW), f32)
        for b in range(S // ch):
            base = b * ch
            dy = dy_ref[pl.ds(base, ch), :]
            diff = diff_ref[pl.ds(base, ch), :]
            dsc = dsc + jnp.sum(dy * _dot_nn(diff, w_ref[...]), axis=0, keepdims=True)
            dz = (dy * sc_ref[...]).astype(bf16)
            dw = dw + _dot_tn(diff, dz)
            ddiff = _dot_nt(dz, w_ref[...])
            dvp_ref[pl.ds(base, ch), :] = -ddiff
            pad[pl.ds(PAD + base, ch), :] = ddiff / _pool_counts(base, ch)
        for gi in range(4):
            dw_ref[gi] = dw[64 * gi:64 * (gi + 1), 64 * gi:64 * (gi + 1)]
        dsc_ref[...] = dsc
        for b in range(S // ch):
            base = b * ch
            dvp_ref[pl.ds(base, ch), :] += _pool_sums(pad, base, ch, True)

    whole = lambda shape: pl.BlockSpec(shape, lambda i: (0,) * len(shape))
    return pl.pallas_call(
        _follow(body, 4, after), grid=(1,),
        in_specs=[whole((S, PW)), whole((S, PW)), whole((PW, PW)), whole((1, PW))] + [_ANY] * len(after),
        out_specs=[whole((S, PW)), whole((4, 64, 64)), whole((1, PW))],
        out_shape=[SDS((S, PW), f32), SDS((4, 64, 64), f32), SDS((1, PW), f32)],
        scratch_shapes=[pltpu.VMEM((S + 2 * PAD, PW), f32)],
        compiler_params=_CP, name="pool_bwd")(dy, diff, wbd, scale, *after)


def _attn_blocks(lc):
    bpc = lc // QB
    kw = min(2 * QB, lc)
    blocks = []
    for b in range(S // QB):
        t0 = (b % bpc) * QB
        ks_in = min(max(t0 - HALF, 0), lc - kw)
        blocks.append((b * QB, (b // bpc) * lc + ks_in, t0 - ks_in))
    return kw, blocks


def _attn_bias(bias_ref, kw, shifts):
    r = lax.broadcasted_iota(jnp.int32, (2 * QB, kw), 0) % QB
    c = lax.broadcasted_iota(jnp.int32, (2 * QB, kw), 1)
    for i, shift in enumerate(shifts):
        bias_ref[i] = jnp.where(jnp.abs(r + shift - c) <= HALF, 0.0, MASK_VALUE).astype(f32)


def _head_put(stats, pair, v0, v1, lane):
    return jnp.where(lane == 2 * pair, v0, jnp.where(lane == 2 * pair + 1, v1, stats))


def _head_cols(stats, pair, lane):
    c0 = jnp.sum(jnp.where(lane == 2 * pair, stats, 0.0), axis=-1, keepdims=True)
    c1 = jnp.sum(jnp.where(lane == 2 * pair + 1, stats, 0.0), axis=-1, keepdims=True)
    return jnp.concatenate([c0, c1], axis=0)


def _head_spread(stats, pair, head0):
    return jnp.where(head0, stats[:, 2 * pair:2 * pair + 1], stats[:, 2 * pair + 1:2 * pair + 2])


def _stack_heads(blk, head0):
    zero = jnp.zeros_like(blk)
    return jnp.concatenate([jnp.where(head0, blk, zero), jnp.where(head0, zero, blk)], axis=0)


def _attn_fwd(q, k, v, lc, after=None):
    kw, blocks = _attn_blocks(lc)
    shifts = sorted({b[2] for b in blocks})

    def body(q_ref, k_ref, v_ref, *refs):
        o_ref, lse_ref, bias_ref = refs[-3:]
        lane = lax.broadcasted_iota(jnp.int32, (QB, 128), 1)
        head0 = lane < 64
        pair = pl.program_id(0)
        _attn_bias(bias_ref, kw, shifts)

        @pl.when(pair == 0)
        def _():
            lse_ref[...] = jnp.zeros_like(lse_ref)

        for row0, kstart, shift in blocks:
            q2 = _stack_heads(q_ref[pl.ds(row0, QB), :], head0)
            kb = k_ref[pl.ds(kstart, kw), :]
            vb = v_ref[pl.ds(kstart, kw), :]
            s = _dot_nt(q2, kb) + bias_ref[shifts.index(shift)]
            m = jnp.max(s, axis=-1, keepdims=True)
            p = jnp.exp(s - m)
            den = jnp.sum(p, axis=-1, keepdims=True)
            o2 = _dot_nn(p.astype(bf16), vb) / den
            lse2 = m + jnp.log(den)
            o_ref[pl.ds(row0, QB), :] = jnp.where(head0, o2[:QB], o2[QB:]).astype(bf16)
            lse_ref[pl.ds(row0, QB), :] = _head_put(lse_ref[pl.ds(row0, QB), :], pair, lse2[:QB], lse2[QB:], lane)

    col = pl.BlockSpec((S, 128), lambda p: (0, p))
    extra = () if after is None else (after,)
    return pl.pallas_call(
        body, grid=(NG,), in_specs=[col, col, col] + [_ANY] * len(extra),
        out_specs=[col, pl.BlockSpec((S, 128), lambda p: (0, 0))],
        out_shape=[SDS((S, AW), bf16), SDS((S, 128), f32)],
        scratch_shapes=[pltpu.VMEM((len(shifts), 2 * QB, kw), f32)],
        compiler_params=_CP, name=f"attn_fwd_{lc}")(q, k, v, *extra)


def _attn_bwd(q, k, v, do, lse, delta, lc):
    kw, blocks = _attn_blocks(lc)
    shifts = sorted({b[2] for b in blocks})

    def body(q_ref, k_ref, v_ref, do_ref, lse_ref, dl_ref, dq_ref, dk_out, dv_out, bias_ref, dk_ref, dv_ref):
        lane = lax.broadcasted_iota(jnp.int32, (QB, 128), 1)
        head0 = lane < 64
        pair = pl.program_id(0)
        _attn_bias(bias_ref, kw, shifts)
        dk_ref[...] = jnp.zeros_like(dk_ref)
        dv_ref[...] = jnp.zeros_like(dv_ref)
        for row0, kstart, shift in blocks:
            q2 = _stack_heads(q_ref[pl.ds(row0, QB), :], head0)
            do2 = _stack_heads(do_ref[pl.ds(row0, QB), :], head0)
            lse2 = _head_cols(lse_ref[pl.ds(row0, QB), :], pair, lane)
            dl2 = _head_cols(dl_ref[pl.ds(row0, QB), :], pair, lane)
            kb = k_ref[pl.ds(kstart, kw), :]
            vb = v_ref[pl.ds(kstart, kw), :]
            p = jnp.exp(_dot_nt(q2, kb) + bias_ref[shifts.index(shift)] - lse2)
            ds = (p * (_dot_nt(do2, vb) - dl2)).astype(bf16)
            dq2 = _dot_nn(ds, kb)
            dq_ref[pl.ds(row0, QB), :] = jnp.where(head0, dq2[:QB], dq2[QB:]).astype(bf16)
            dk_ref[pl.ds(kstart, kw), :] += _dot_tn(ds, q2)
            dv_ref[pl.ds(kstart, kw), :] += _dot_tn(p.astype(bf16), do2)
        dk_out[...] = dk_ref[...].astype(bf16)
        dv_out[...] = dv_ref[...].astype(bf16)

    col = pl.BlockSpec((S, 128), lambda p: (0, p))
    stats = pl.BlockSpec((S, 128), lambda p: (0, 0))
    return pl.pallas_call(
        body, grid=(NG,), in_specs=[col] * 4 + [stats] * 2, out_specs=[col] * 3,
        out_shape=[SDS((S, AW), bf16)] * 3,
        scratch_shapes=[pltpu.VMEM((len(shifts), 2 * QB, kw), f32), pltpu.VMEM((S, 128), f32),
                        pltpu.VMEM((S, 128), f32)],
        compiler_params=_CP, name=f"attn_bwd_{lc}")(q, k, v, do, lse, delta)


def _mix_out_fwd(x, ypool, o1, l1, o4, l4, o16, l16, wout):
    def body(x_ref, yp_ref, o1_ref, l1_ref, o4_ref, l4_ref, o16_ref, l16_ref, w_ref,
             xo_ref, mixed_ref, o_ref, lse1_ref, lse4_ref, lse16_ref, sl4, sl16, sl):
        head0 = lax.broadcasted_iota(jnp.int32, (TM, 128), 1) < 64
        for r in range(4):
            sl4[pl.ds(r, TM // 4, stride=4), :] = l4_ref[r]
        for r in range(16):
            sl16[pl.ds(r, TM // 16, stride=16), :] = l16_ref[r]
        n4 = _dot_nn(_dilation_perm(4, True), o4_ref[...].reshape(TM, AW))
        n16 = _dot_nn(_dilation_perm(16, True), o16_ref[...].reshape(TM, AW))
        a, b, c = l1_ref[...], sl4[...], sl16[...]
        m = jnp.maximum(jnp.maximum(a, b), c)
        wa, wb, wc = jnp.exp(a - m), jnp.exp(b - m), jnp.exp(c - m)
        den = wa + wb + wc
        wa, wb, wc = wa / den, wb / den, wc / den
        lse = m + jnp.log(den)
        lse1_ref[...] = lse
        sl[...] = lse
        mixed_ref[:, :PW] = yp_ref[...].astype(bf16)
        for j in range(NG):
            y = (_head_spread(wa, j, head0) * o1_ref[:, _cols(j)].astype(f32)
                 + _head_spread(wb, j, head0) * n4[:, _cols(j)] + _head_spread(wc, j, head0) * n16[:, _cols(j)])
            o_ref[:, _cols(j)] = y
            mixed_ref[:, PW + 128 * j: PW + 128 * (j + 1)] = y.astype(bf16)
        for r in range(4):
            lse4_ref[r] = sl[pl.ds(r, TM // 4, stride=4), :]
        for r in range(16):
            lse16_ref[r] = sl[pl.ds(r, TM // 16, stride=16), :]
        xo_ref[...] = x_ref[...] + _dot_nn(mixed_ref[...], w_ref[...])

    return pl.pallas_call(
        body, grid=(S // TM,),
        in_specs=[_tile(D), _tile(PW), _tile(AW), _tile(128), _p4(), _p4(128), _p16(), _p16(128), _layer(D, D)],
        out_specs=[_tile(D), _tile(D), _tile(AW), _tile(128), _p4(128), _p16(128)],
        out_shape=[SDS((S, D), f32), SDS((S, D), bf16), SDS((S, AW), f32), SDS((S, 128), f32),
                   SDS((4, S // 4, 128), f32), SDS((16, S // 16, 128), f32)],
        scratch_shapes=[pltpu.VMEM((TM, 128), f32)] * 3,
        compiler_params=_CP, name="mix_out_fwd")(x, ypool, o1, l1, o4, l4, o16, l16, wout)


def _mix_out_bwd(dxo, o, wout):
    def body(dxo_ref, o_ref, w_ref, dxb_ref, dyp_ref, do1, do4, do16, dl1, dl4, dl16, sdl):
        dxb = dxo_ref[...].astype(bf16)
        dxb_ref[...] = dxb
        dm = _dot_nt(dxb, w_ref[...])
        dyp_ref[...] = dm[:, :PW]
        lane = lax.broadcasted_iota(jnp.int32, (TM, 128), 1)
        head0 = lane < 64
        dl = jnp.zeros((TM, 128), f32)
        for j in range(NG):
            d = dm[:, PW + 128 * j: PW + 128 * (j + 1)]
            prod = d * o_ref[:, _cols(j)]
            dl = _head_put(dl, j, jnp.sum(jnp.where(head0, prod, 0.0), axis=-1, keepdims=True),
                           jnp.sum(jnp.where(head0, 0.0, prod), axis=-1, keepdims=True), lane)
            do1[:, _cols(j)] = d.astype(bf16)
        dl1[...] = dl
        sdl[...] = dl
        for r in range(4):
            dl4[r] = sdl[pl.ds(r, TM // 4, stride=4), :]
        for r in range(16):
            dl16[r] = sdl[pl.ds(r, TM // 16, stride=16), :]
        nat = do1[...]
        do4[...] = _dot_nn(_dilation_perm(4), nat).astype(bf16).reshape(4, TM // 4, AW)
        do16[...] = _dot_nn(_dilation_perm(16), nat).astype(bf16).reshape(16, TM // 16, AW)

    return pl.pallas_call(
        body, grid=(S // TM,),
        in_specs=[_tile(D), _tile(AW), _layer(D, D)],
        out_specs=[_tile(D), _tile(PW), _tile(AW), _p4(), _p16(), _tile(128), _p4(128), _p16(128)],
        out_shape=[SDS((S, D), bf16), SDS((S, PW), f32),
                   SDS((S, AW), bf16), SDS((4, S // 4, AW), bf16), SDS((16, S // 16, AW), bf16),
                   SDS((S, 128), f32), SDS((4, S // 4, 128), f32), SDS((16, S // 16, 128), f32)],
        scratch_shapes=[pltpu.VMEM((TM, 128), f32)],
        compiler_params=_CP, name="mix_out_bwd")(dxo, o, wout)


def _loss_head(x, g, target):
    def body(x_ref, g_ref, t_ref, dx_ref, loss_ref, dg_ref):
        g = g_ref[...]
        r, xh, y = _rms(x_ref[...], g)
        err = y - t_ref[...]
        dy = err * (1.0 / D)

        @pl.when(pl.program_id(0) == 0)
        def _():
            loss_ref[...] = jnp.zeros_like(loss_ref)
            dg_ref[...] = jnp.zeros_like(dg_ref)

        loss_ref[...] += jnp.broadcast_to(0.5 * jnp.sum(jnp.mean(err * err, axis=-1, keepdims=True)), (1, D))
        dg_ref[...] += jnp.sum(dy * xh, axis=0, keepdims=True)
        dx_ref[...] = _rms_bwd(dy, r, xh, g)

    return pl.pallas_call(
        body, grid=(S // TM,),
        in_specs=[_tile(D), _const((1, D)), _tile(D)],
        out_specs=[_tile(D), _const((1, D)), _const((1, D))],
        out_shape=[SDS((S, D), f32), SDS((1, D), f32), SDS((1, D), f32)],
        compiler_params=_CP, name="loss_head")(x, g, target)


def _peer(k):
    x, y, c = lax.axis_index("x"), lax.axis_index("y"), lax.axis_index("c")
    px = 1 - x if k & 4 else x
    py = 1 - y if k & 2 else y
    pc = 1 - c if k & 1 else c
    return (px, py, pc), 4 * px + 2 * py + pc


def _diag_route():
    x, y, c = lax.axis_index("x"), lax.axis_index("y"), lax.axis_index("c")
    idx_x, idx_y = _peer(4)[1], _peer(2)[1]
    return idx_x + c * (idx_y - idx_x), (x + c * (1 - 2 * x), (1 - y) + c * (2 * y - 1), c)


def _hbm(a):
    return pltpu.with_memory_space_constraint(a, pltpu.HBM)


def _rows(ref, idx):
    r = ref.shape[0] // NDEV
    return ref.at[pl.ds(idx * r, r), :]


def _row_copy(ref, idx, send_sem, recv_sem, to):
    return pltpu.make_async_remote_copy(src_ref=_rows(ref, idx), dst_ref=_rows(ref, idx), send_sem=send_sem,
                                        recv_sem=recv_sem, device_id=to, device_id_type=_MESH)


def _place_own(me, shards, l):
    n = len(shards)

    def body(me_ref, *refs):
        for t in range(n):
            refs[n + t][...] = refs[t][...].astype(bf16)

    grid_spec = pltpu.PrefetchScalarGridSpec(
        num_scalar_prefetch=1, grid=(1,),
        in_specs=[pl.BlockSpec((None, s.shape[1], D), lambda i, me_ref: (l, 0, 0)) for s in shards],
        out_specs=[pl.BlockSpec((s.shape[1], D), lambda i, me_ref: (me_ref[0], 0)) for s in shards])
    return pl.pallas_call(
        body, grid_spec=grid_spec, out_shape=[SDS((NDEV * s.shape[1], D), bf16) for s in shards],
        compiler_params=_CP, name="place_own")(me, *shards)


_TOKEN = SDS((8, 128), f32)
def _ag_start(lands, after, l):
    n = len(lands)
    after = list(after) if isinstance(after, (list, tuple)) else [after]

    def body(*refs):
        zones, send_sems, recv_sems, token = refs[:n], refs[n + len(after)], refs[n + len(after) + 1], refs[-1]
        _, me_idx = _peer(0)
        for k, mask in enumerate((1, 4, 2)):
            for t in range(n):
                _row_copy(zones[t], me_idx, send_sems.at[k * n + t], recv_sems.at[k * n + t], _peer(mask)[0]).start()
        token[...] = jnp.zeros_like(token)

    outs = pl.pallas_call(
        body, name=f"ag_start_{l}", in_specs=[_HBM] * n + [_ANY] * len(after),
        out_specs=(_SEM, _SEM, *[_HBM] * n, pl.BlockSpec(memory_space=pltpu.VMEM)),
        out_shape=(pltpu.SemaphoreType.DMA((3 * n,)), pltpu.SemaphoreType.DMA((3 * n,)),
                   *[pltpu.HBM(a.shape, a.dtype) for a in lands], _TOKEN),
        input_output_aliases={t: 2 + t for t in range(n)}, compiler_params=_CP_SPLIT)(
            *[_hbm(a) for a in lands], *after)
    return outs[0], outs[1], list(outs[2:2 + n]), outs[-1]


def _ag_pass(lands, recv_sems, after, l):
    n = len(lands)
    after = list(after) if isinstance(after, (list, tuple)) else [after]

    def body(*refs):
        zones, recv_sems = refs[:n], refs[n]
        psend, precv, token = refs[n + 1 + len(after)], refs[n + 2 + len(after)], refs[-1]
        me, _ = _peer(0)
        sibling, _ = _peer(1)
        for j, mask in enumerate((4, 2)):
            idx = _peer(mask)[1]
            for t in range(n):
                _row_copy(zones[t], idx, psend.at[j * n + t], recv_sems.at[(1 + j) * n + t], me).wait_recv()
                _row_copy(zones[t], idx, psend.at[j * n + t], precv.at[j * n + t], sibling).start()
        fwd_idx, fwd_dev = _diag_route()
        for t in range(n):
            _row_copy(zones[t], fwd_idx, psend.at[2 * n + t], precv.at[2 * n + t], fwd_dev).start()
        token[...] = jnp.zeros_like(token)

    outs = pl.pallas_call(
        body, name=f"ag_pass_{l}", in_specs=[_HBM] * n + [_SEM] + [_ANY] * len(after),
        out_specs=(_SEM, _SEM, *[_HBM] * n, pl.BlockSpec(memory_space=pltpu.VMEM)),
        out_shape=(pltpu.SemaphoreType.DMA((3 * n,)), pltpu.SemaphoreType.DMA((3 * n,)),
                   *[pltpu.HBM(a.shape, a.dtype) for a in lands], _TOKEN),
        input_output_aliases={t: 2 + t for t in range(n)}, compiler_params=_CP_SPLIT)(*lands, recv_sems, *after)
    return outs[0], outs[1], list(outs[2:2 + n]), outs[-1]


def _ag_last(lands, precv, after, l):
    n = len(lands)
    after = list(after) if isinstance(after, (list, tuple)) else [after]

    def body(*refs):
        zones, precv = refs[:n], refs[n]
        qsend, qrecv, token = refs[n + 1 + len(after)], refs[n + 2 + len(after)], refs[-1]
        me, _ = _peer(0)
        sibling, _ = _peer(1)
        idx = _peer(6)[1]
        for t in range(n):
            _row_copy(zones[t], idx, qsend.at[t], precv.at[2 * n + t], me).wait_recv()
            _row_copy(zones[t], idx, qsend.at[t], qrecv.at[t], sibling).start()
        token[...] = jnp.zeros_like(token)

    outs = pl.pallas_call(
        body, name=f"ag_last_{l}", in_specs=[_HBM] * n + [_SEM] + [_ANY] * len(after),
        out_specs=(_SEM, _SEM, *[_HBM] * n, pl.BlockSpec(memory_space=pltpu.VMEM)),
        out_shape=(pltpu.SemaphoreType.DMA((n,)), pltpu.SemaphoreType.DMA((n,)),
                   *[pltpu.HBM(a.shape, a.dtype) for a in lands], _TOKEN),
        input_output_aliases={t: 2 + t for t in range(n)}, compiler_params=_CP_SPLIT)(*lands, precv, *after)
    return outs[0], outs[1], list(outs[2:2 + n]), outs[-1]


def _ag_wait(lands, send_sems, recv_sems, psend, precv, qsend, qrecv, after, l):
    n = len(lands)
    after = list(after) if isinstance(after, (list, tuple)) else [after]

    def body(*refs):
        zones = refs[:n]
        send_sems, recv_sems, psend, precv, qsend, qrecv = refs[n:n + 6]
        me, me_idx = _peer(0)
        for k in range(3):
            for t in range(n):
                _row_copy(zones[t], me_idx, send_sems.at[k * n + t], recv_sems.at[k * n + t], me).wait_send()
        for t in range(n):
            _row_copy(zones[t], _peer(1)[1], send_sems.at[t], recv_sems.at[t], me).wait_recv()
        fwd_idx, _ = _diag_route()
        for j, (mine, theirs) in enumerate(((_peer(4)[1], _peer(5)[1]), (_peer(2)[1], _peer(3)[1]))):
            for t in range(n):
                _row_copy(zones[t], mine, psend.at[j * n + t], precv.at[j * n + t], me).wait_send()
                _row_copy(zones[t], theirs, psend.at[j * n + t], precv.at[j * n + t], me).wait_recv()
        for t in range(n):
            _row_copy(zones[t], fwd_idx, psend.at[2 * n + t], precv.at[2 * n + t], me).wait_send()
            _row_copy(zones[t], _peer(6)[1], qsend.at[t], qrecv.at[t], me).wait_send()
            _row_copy(zones[t], _peer(7)[1], qsend.at[t], qrecv.at[t], me).wait_recv()

    outs = pl.pallas_call(
        body, name=f"ag_wait_{l}", in_specs=[_HBM] * n + [_SEM] * 6 + [_ANY] * len(after),
        out_specs=tuple([_HBM] * n), out_shape=tuple(pltpu.HBM(a.shape, a.dtype) for a in lands),
        input_output_aliases={t: t for t in range(n)}, compiler_params=_CP_SPLIT)(
            *lands, send_sems, recv_sems, psend, precv, qsend, qrecv, *after)
    return list(outs)


def _xchg_src(ref, slot_ref, idx):
    return _rows(ref, idx) if ref.shape[0] == NDEV * slot_ref.shape[1] else ref


def _rs_start(srcs, slots, after, tag):
    n = len(srcs)
    after = list(after) if isinstance(after, (list, tuple)) else [after]

    def body(*refs):
        src, slot = refs[:n], refs[n:2 * n]
        send_sems, recv_sems, token = refs[2 * n + len(after)], refs[2 * n + len(after) + 1], refs[-1]
        _, me_idx = _peer(0)
        for k in range(1, NDEV):
            dev, idx = _peer(k)
            for t in range(n):
                pltpu.make_async_remote_copy(
                    src_ref=_xchg_src(src[t], slot[t], idx), dst_ref=slot[t].at[me_idx],
                    send_sem=send_sems.at[(k - 1) * n + t], recv_sem=recv_sems.at[(k - 1) * n + t],
                    device_id=dev, device_id_type=_MESH).start()
        token[...] = jnp.zeros_like(token)

    outs = pl.pallas_call(
        body, name=f"rs_start_{tag}", in_specs=[_HBM] * (2 * n) + [_ANY] * len(after),
        out_specs=(_SEM, _SEM, *[_HBM] * (2 * n), pl.BlockSpec(memory_space=pltpu.VMEM)),
        out_shape=(pltpu.SemaphoreType.DMA(((NDEV - 1) * n,)), pltpu.SemaphoreType.DMA(((NDEV - 1) * n,)),
                   *[pltpu.HBM(a.shape, a.dtype) for a in list(srcs) + list(slots)], _TOKEN),
        input_output_aliases={t: 2 + t for t in range(2 * n)}, compiler_params=_CP_SPLIT)(
            *[_hbm(a) for a in list(srcs) + list(slots)], *after)
    return outs[0], outs[1], list(outs[2:2 + n]), list(outs[2 + n:2 + 2 * n]), outs[-1]


def _rs_wait(srcs, slots, send_sems, recv_sems, after, tag):
    n = len(srcs)
    after = list(after) if isinstance(after, (list, tuple)) else [after]

    def body(*refs):
        src, slot, send_sems, recv_sems = refs[:n], refs[n:2 * n], refs[2 * n], refs[2 * n + 1]
        me, _ = _peer(0)
        for k in range(1, NDEV):
            idx = _peer(k)[1]
            for t in range(n):
                cp = pltpu.make_async_remote_copy(
                    src_ref=_xchg_src(src[t], slot[t], idx), dst_ref=slot[t].at[idx],
                    send_sem=send_sems.at[(k - 1) * n + t], recv_sem=recv_sems.at[(k - 1) * n + t],
                    device_id=me, device_id_type=_MESH)
                cp.wait_send()
                cp.wait_recv()

    outs = pl.pallas_call(
        body, name=f"rs_wait_{tag}", in_specs=[_HBM] * (2 * n) + [_SEM, _SEM] + [_ANY] * len(after),
        out_specs=tuple([_HBM] * (2 * n)),
        out_shape=tuple(pltpu.HBM(a.shape, a.dtype) for a in list(srcs) + list(slots)),
        input_output_aliases={t: t for t in range(2 * n)}, compiler_params=_CP_SPLIT)(
            *srcs, *slots, send_sems, recv_sems, *after)
    return list(outs[:n]), list(outs[n:])


def _pair_start(full4s, bufs, after, tag):
    n = len(full4s)
    after = list(after) if isinstance(after, (list, tuple)) else [after]

    def body(*refs):
        full, buf = refs[:n], refs[n:2 * n]
        send_sems, recv_sems, token = refs[2 * n + len(after)], refs[2 * n + len(after) + 1], refs[-1]
        c = lax.axis_index("c")
        for t in range(n):
            pltpu.make_async_remote_copy(src_ref=full[t].at[:, 1 - c], dst_ref=buf[t], send_sem=send_sems.at[t],
                                         recv_sem=recv_sems.at[t], device_id=_peer(1)[0], device_id_type=_MESH).start()
        token[...] = jnp.zeros_like(token)

    outs = pl.pallas_call(
        body, name=f"pair_start_{tag}", in_specs=[_HBM] * (2 * n) + [_ANY] * len(after),
        out_specs=(_SEM, _SEM, *[_HBM] * (2 * n), pl.BlockSpec(memory_space=pltpu.VMEM)),
        out_shape=(pltpu.SemaphoreType.DMA((n,)), pltpu.SemaphoreType.DMA((n,)),
                   *[pltpu.HBM(a.shape, a.dtype) for a in list(full4s) + list(bufs)], _TOKEN),
        input_output_aliases={t: 2 + t for t in range(2 * n)}, compiler_params=_CP_SPLIT)(
            *[_hbm(a) for a in list(full4s) + list(bufs)], *after)
    return outs[0], outs[1], list(outs[2:2 + n]), list(outs[2 + n:2 + 2 * n]), outs[-1]


def _pair_wait(full4s, bufs, send_sems, recv_sems, after, tag):
    n = len(full4s)
    after = list(after) if isinstance(after, (list, tuple)) else [after]

    def body(*refs):
        full, buf, send_sems, recv_sems = refs[:n], refs[n:2 * n], refs[2 * n], refs[2 * n + 1]
        c = lax.axis_index("c")
        for t in range(n):
            cp = pltpu.make_async_remote_copy(src_ref=full[t].at[:, 1 - c], dst_ref=buf[t], send_sem=send_sems.at[t],
                                              recv_sem=recv_sems.at[t], device_id=_peer(0)[0], device_id_type=_MESH)
            cp.wait_send()
            cp.wait_recv()

    outs = pl.pallas_call(
        body, name=f"pair_wait_{tag}", in_specs=[_HBM] * (2 * n) + [_SEM, _SEM] + [_ANY] * len(after),
        out_specs=tuple([_HBM] * (2 * n)),
        out_shape=tuple(pltpu.HBM(a.shape, a.dtype) for a in list(full4s) + list(bufs)),
        input_output_aliases={t: t for t in range(2 * n)}, compiler_params=_CP_SPLIT)(
            *full4s, *bufs, send_sems, recv_sems, *after)
    return list(outs[:n]), list(outs[n:])


def _pair_sum(core, full4s, bufs):
    n = len(full4s)

    def body(core_ref, *refs):
        for t in range(n):
            refs[2 * n + t][...] = (refs[t][...].astype(f32) + refs[n + t][...].astype(f32)).astype(bf16)

    grid_spec = pltpu.PrefetchScalarGridSpec(
        num_scalar_prefetch=1, grid=(4,),
        in_specs=[pl.BlockSpec((None, None) + a.shape[2:], lambda j, core_ref: (j, core_ref[0], 0, 0)) for a in full4s]
        + [pl.BlockSpec((None,) + b.shape[1:], lambda j, core_ref: (j, 0, 0)) for b in bufs],
        out_specs=[pl.BlockSpec((None,) + b.shape[1:], lambda j, core_ref: (j, 0, 0)) for b in bufs])
    return pl.pallas_call(
        body, grid_spec=grid_spec, out_shape=[SDS(b.shape, bf16) for b in bufs],
        compiler_params=_CP, name="pair_sum")(core, *full4s, *bufs)


def _chip_start(sums, slots, after, tag):
    n = len(sums)
    after = list(after) if isinstance(after, (list, tuple)) else [after]

    def body(*refs):
        src, slot = refs[:n], refs[n:2 * n]
        send_sems, recv_sems, token = refs[2 * n + len(after)], refs[2 * n + len(after) + 1], refs[-1]
        my_chip = 2 * lax.axis_index("x") + lax.axis_index("y")
        for k, mask in enumerate((4, 2, 6)):
            dev, _ = _peer(mask)
            for t in range(n):
                pltpu.make_async_remote_copy(
                    src_ref=src[t].at[2 * dev[0] + dev[1]], dst_ref=slot[t].at[my_chip],
                    send_sem=send_sems.at[k * n + t], recv_sem=recv_sems.at[k * n + t],
                    device_id=dev, device_id_type=_MESH).start()
        token[...] = jnp.zeros_like(token)

    outs = pl.pallas_call(
        body, name=f"chip_start_{tag}", in_specs=[_HBM] * (2 * n) + [_ANY] * len(after),
        out_specs=(_SEM, _SEM, *[_HBM] * (2 * n), pl.BlockSpec(memory_space=pltpu.VMEM)),
        out_shape=(pltpu.SemaphoreType.DMA((3 * n,)), pltpu.SemaphoreType.DMA((3 * n,)),
                   *[pltpu.HBM(a.shape, a.dtype) for a in list(sums) + list(slots)], _TOKEN),
        input_output_aliases={t: 2 + t for t in range(2 * n)}, compiler_params=_CP_SPLIT)(
            *[_hbm(a) for a in list(sums) + list(slots)], *after)
    return outs[0], outs[1], list(outs[2:2 + n]), list(outs[2 + n:2 + 2 * n]), outs[-1]


def _chip_wait(sums, slots, send_sems, recv_sems, after, tag):
    n = len(sums)
    after = list(after) if isinstance(after, (list, tuple)) else [after]

    def body(*refs):
        src, slot, send_sems, recv_sems = refs[:n], refs[n:2 * n], refs[2 * n], refs[2 * n + 1]
        for k, mask in enumerate((4, 2, 6)):
            dev, _ = _peer(mask)
            chip = 2 * dev[0] + dev[1]
            for t in range(n):
                cp = pltpu.make_async_remote_copy(
                    src_ref=src[t].at[chip], dst_ref=slot[t].at[chip],
                    send_sem=send_sems.at[k * n + t], recv_sem=recv_sems.at[k * n + t],
                    device_id=_peer(0)[0], device_id_type=_MESH)
                cp.wait_send()
                cp.wait_recv()

    outs = pl.pallas_call(
        body, name=f"chip_wait_{tag}", in_specs=[_HBM] * (2 * n) + [_SEM, _SEM] + [_ANY] * len(after),
        out_specs=tuple([_HBM] * (2 * n)),
        out_shape=tuple(pltpu.HBM(a.shape, a.dtype) for a in list(sums) + list(slots)),
        input_output_aliases={t: t for t in range(2 * n)}, compiler_params=_CP_SPLIT)(
            *sums, *slots, send_sems, recv_sems, *after)
    return list(outs[:n]), list(outs[n:])


def _sum_slots(slots, rb):
    r = slots.shape[1]

    def body(s_ref, o_ref):
        acc = s_ref[0].astype(f32)
        for s in range(1, NDEV):
            acc = acc + s_ref[s].astype(f32)
        o_ref[...] = acc

    return pl.pallas_call(
        body, grid=(r // rb,),
        in_specs=[pl.BlockSpec((NDEV, rb, D), lambda i: (0, i, 0))],
        out_specs=pl.BlockSpec((rb, D), lambda i: (i, 0)),
        out_shape=SDS((r, D), f32), compiler_params=_CP, name="sum_slots")(slots)


def _adamw(w, g, m, v):
    shape = w.shape
    cols = shape[-1]
    rows = w.size // cols
    rb = rows
    for cand in (512, 256, 128, 64, 32, 16, 8):
        if rows % cand == 0 and rows > cand:
            rb = cand
            break

    def body(w_ref, g_ref, m_ref, v_ref, d_ref, mo_ref, vo_ref):
        d_ref[...], mo_ref[...], vo_ref[...] = _adamw_math(w_ref[...], g_ref[...], m_ref[...], v_ref[...])

    spec = pl.BlockSpec((rb, cols), lambda i: (i, 0))
    outs = pl.pallas_call(
        body, grid=(rows // rb,), in_specs=[spec] * 4, out_specs=[spec] * 3,
        out_shape=[SDS((rows, cols), f32)] * 3, compiler_params=_CP, name="adamw")(
            *(a.reshape(rows, cols) for a in (w, g, m, v)))
    return tuple(o.reshape(shape) for o in outs)


def _adamw_math(w, g, m, v):
    m = ADAM_B1 * m + (1.0 - ADAM_B1) * g
    v = ADAM_B2 * v + (1.0 - ADAM_B2) * (g * g)
    m_hat = m / (1.0 - ADAM_B1 ** ADAM_STEP)
    v_hat = v / (1.0 - ADAM_B2 ** ADAM_STEP)
    return -ADAM_LR * (m_hat / (jnp.sqrt(v_hat) + ADAM_EPS) + ADAM_WD * w), m, v


def _reduce_adamw(acc, me, full, slots, w, m, v, l):
    _, r, _ = w.shape
    ns = slots.shape[0]
    rb = r // 2 if r > 128 else r

    def body(me_ref, full_ref, slots_ref, w_ref, m_ref, v_ref, *refs):
        go_ref, d_ref, mo_ref, vo_ref = refs[-4:]
        own = full_ref[...].astype(f32)
        g = None
        for s in range(ns):
            part = jnp.where(me_ref[0] == s, own, slots_ref[s].astype(f32))
            g = part if g is None else g + part
        go_ref[...] = g
        d_ref[...], mo_ref[...], vo_ref[...] = _adamw_math(w_ref[...], g, m_ref[...], v_ref[...])

    steps = r // rb
    lay = pl.BlockSpec((None, rb, D), lambda i, me_ref: (l, i, 0))
    n_acc = 0 if acc is None else 4
    grid_spec = pltpu.PrefetchScalarGridSpec(
        num_scalar_prefetch=1, grid=(steps,),
        in_specs=[pl.BlockSpec((rb, D), lambda i, me_ref: (me_ref[0] * steps + i, 0)),
                  pl.BlockSpec((ns, rb, D), lambda i, me_ref: (0, i, 0)), lay, lay, lay] + [_ANY] * n_acc,
        out_specs=[lay] * 4)
    outs = pl.pallas_call(
        body, grid_spec=grid_spec, out_shape=[SDS(w.shape, f32)] * 4,
        input_output_aliases={6 + j: j for j in range(n_acc)},
        compiler_params=_CP, name="reduce_adamw")(me, full, slots, w, m, v, *(() if acc is None else acc))
    return tuple(outs)


_BIG = ("ffn1_w_gate", "ffn1_w_up", "ffn1_w_down", "w_in", "w_out", "ffn2_w_gate", "ffn2_w_up", "ffn2_w_down")
_TRANSPOSED = ("ffn1_w_gate", "ffn1_w_up", "w_in", "ffn2_w_gate", "ffn2_w_up")

def _block_diag(pool_w):
    out = jnp.zeros((L, PW, PW), pool_w.dtype)
    for gi in range(4):
        out = out.at[:, 64 * gi:64 * (gi + 1), 64 * gi:64 * (gi + 1)].set(pool_w[:, gi])
    return out


def kernel(x, positions, ffn1_norm, ffn1_w_gate, ffn1_w_up, ffn1_w_down, mix_norm, w_in, pool_w, pool_scale, w_out, ffn2_norm, ffn2_w_gate, ffn2_w_up, ffn2_w_down, final_norm, loss_target, m_ffn1_norm, m_ffn1_w_gate, m_ffn1_w_up, m_ffn1_w_down, m_mix_norm, m_w_in, m_pool_w, m_pool_scale, m_w_out, m_ffn2_norm, m_ffn2_w_gate, m_ffn2_w_up, m_ffn2_w_down, m_final_norm, v_ffn1_norm, v_ffn1_w_gate, v_ffn1_w_up, v_ffn1_w_down, v_mix_norm, v_w_in, v_pool_w, v_pool_scale, v_w_out, v_ffn2_norm, v_ffn2_w_gate, v_ffn2_w_up, v_ffn2_w_down, v_final_norm):
    weights = dict(ffn1_norm=ffn1_norm, ffn1_w_gate=ffn1_w_gate, ffn1_w_up=ffn1_w_up, ffn1_w_down=ffn1_w_down,
                   mix_norm=mix_norm, w_in=w_in, pool_w=pool_w, pool_scale=pool_scale, w_out=w_out,
                   ffn2_norm=ffn2_norm, ffn2_w_gate=ffn2_w_gate, ffn2_w_up=ffn2_w_up, ffn2_w_down=ffn2_w_down,
                   final_norm=final_norm)
    moms = dict(ffn1_norm=m_ffn1_norm, ffn1_w_gate=m_ffn1_w_gate, ffn1_w_up=m_ffn1_w_up, ffn1_w_down=m_ffn1_w_down,
                mix_norm=m_mix_norm, w_in=m_w_in, pool_w=m_pool_w, pool_scale=m_pool_scale, w_out=m_w_out,
                ffn2_norm=m_ffn2_norm, ffn2_w_gate=m_ffn2_w_gate, ffn2_w_up=m_ffn2_w_up, ffn2_w_down=m_ffn2_w_down,
                final_norm=m_final_norm)
    vels = dict(ffn1_norm=v_ffn1_norm, ffn1_w_gate=v_ffn1_w_gate, ffn1_w_up=v_ffn1_w_up, ffn1_w_down=v_ffn1_w_down,
                mix_norm=v_mix_norm, w_in=v_w_in, pool_w=v_pool_w, pool_scale=v_pool_scale, w_out=v_w_out,
                ffn2_norm=v_ffn2_norm, ffn2_w_gate=v_ffn2_w_gate, ffn2_w_up=v_ffn2_w_up, ffn2_w_down=v_ffn2_w_down,
                final_norm=v_final_norm)
    names = list(weights)

    me_idx = 4 * lax.axis_index("x") + 2 * lax.axis_index("y") + lax.axis_index("c")
    me_arr = me_idx.reshape(1).astype(jnp.int32)

    as_rows = lambda a, nm: jnp.swapaxes(a, 1, 2) if nm in _TRANSPOSED else a
    w_rows = {nm: as_rows(weights[nm], nm) for nm in _BIG}
    m_rows = {nm: as_rows(moms[nm], nm) for nm in _BIG}
    v_rows = {nm: as_rows(vels[nm], nm) for nm in _BIG}

    def landing_zones(l, which):
        return _place_own(me_arr, [w_rows[_BIG[t]] for t in which], l)

    g_ffn1 = [ffn1_norm[l].reshape(1, D) for l in range(L)]
    g_mix = [mix_norm[l].reshape(1, D) for l in range(L)]
    g_ffn2 = [ffn2_norm[l].reshape(1, D) for l in range(L)]
    wbd_all = _block_diag(pool_w).astype(bf16)
    wbd = [wbd_all[l] for l in range(L)]
    pscale = [pool_scale[l].reshape(1, PW) for l in range(L)]
    tabs = _rope_tables(positions)
    flat = lambda a: a.reshape(S, a.shape[-1])
    r4 = lambda a: a.reshape(4, S // 4, a.shape[-1])
    r16 = lambda a: a.reshape(16, S // 16, a.shape[-1])

    first, mid, rest, whole = (0, 1, 2), (3, 4), (5, 6, 7), tuple(range(8))

    def ag_begin(l, which, after, zones=None):
        tag = f"{l}{'' if which == whole else 'h' if which == first else 'm' if which == mid else 'r'}"
        zones = landing_zones(l, which) if zones is None else zones
        send_sems, recv_sems, zones, token = _ag_start(zones, after, tag)
        return dict(tag=tag, zones=zones, s=send_sems, r=recv_sems), token

    def ag_second(ch, after):
        ch["ps"], ch["pr"], ch["zones"], token = _ag_pass(ch["zones"], ch["r"], after, ch["tag"])
        return token

    def ag_third(ch, after):
        ch["qs"], ch["qr"], ch["zones"], token = _ag_last(ch["zones"], ch["pr"], after, ch["tag"])
        return token

    def ag_end(ch, after):
        return _ag_wait(ch["zones"], ch["s"], ch["r"], ch["ps"], ch["pr"], ch["qs"], ch["qr"], after, ch["tag"])

    ch_head, token = ag_begin(0, first, [])
    ch_mid, _ = ag_begin(0, mid, token)
    zones_rest, zones_next = landing_zones(0, rest), landing_zones(1, whole)
    early_zones = {ll: landing_zones(ll, whole) for ll in range(2, L)}
    fill = [z for zs in (zones_rest, zones_next, *early_zones.values(), tabs, wbd) for z in zs]
    head = ag_end(ch_head, ag_third(ch_head, ag_second(ch_head, fill)))
    tok_mid = ag_second(ch_mid, head[0])
    ch_rest, tok_rest = ag_begin(0, rest, tok_mid, zones_rest)
    chains = {}
    chains[1], tok_next = ag_begin(1, whole, head[0], zones_next)
    gathered = [None] * L
    xs = x.reshape(S, D)
    saved = []
    for l in range(L):
        first_after, second_after = (), ()
        if l == 0:
            gt1, ut1, dn1 = head
            first_after = (tok_rest, tok_next)
        else:
            gt1, ut1, dn1, wint, wout, gt2, ut2, dn2 = gathered[l]
        x0 = xs
        x1, gate1, up1 = _ffn_fwd(x0, g_ffn1[l], gt1, ut1, dn1, after=first_after)
        if l == 0:
            wint, wout = ag_end(ch_mid, ag_third(ch_mid, x1))
        hmix, vp, q1, k1, v1, q4, k4, v4, q16, k16, v16 = _mix_in_fwd(x1, g_mix[l], wint, tabs)
        q4, k4, v4, q16, k16, v16 = map(flat, (q4, k4, v4, q16, k16, v16))
        ypool, diff = _pool_fwd(vp, wbd[l], pscale[l])
        after_attn = None
        if l == 0:
            after_attn = ag_second(ch_rest, [ypool, q16])
        o1, l1 = _attn_fwd(q1, k1, v1, S, after=after_attn)
        o4, l4 = _attn_fwd(q4, k4, v4, S // 4, after=after_attn)
        o16, l16 = _attn_fwd(q16, k16, v16, S // 16, after=after_attn)
        if 0 < l < L - 1:
            second_after = (ag_second(chains[l + 1], [o1, o4, o16]),)
        x2, mixed, o, lse1, lse4, lse16 = _mix_out_fwd(x1, ypool, o1, l1, r4(o4), r4(l4), r16(o16), r16(l16), wout)
        if l == 0:
            token = ag_third(ch_rest, x2)
            gt2, ut2, dn2 = ag_end(ch_rest, token)
            gathered[0] = list(head) + [wint, wout, gt2, ut2, dn2]
            second_after = (ag_second(chains[1], gt2),)
        x3, gate2, up2 = _ffn_fwd(x2, g_ffn2[l], gt2, ut2, dn2, after=second_after)
        if l + 1 < L:
            token = ag_third(chains[l + 1], x3)
            if l + 2 < L:
                chains[l + 2], token = ag_begin(l + 2, whole, token, early_zones[l + 2])
            gathered[l + 1] = ag_end(chains[l + 1], token)
        saved.append(dict(x0=x0, x1=x1, x2=x2, gate1=gate1, up1=up1, gate2=gate2, up2=up2, hmix=hmix, diff=diff,
                          qkv=((q1, k1, v1), (q4, k4, v4), (q16, k16, v16)), mixed=mixed, o=o,
                          lse=(lse1, flat(lse4), flat(lse16))))
        xs = x3

    dx, loss_part, d_final = _loss_head(xs, final_norm.reshape(1, D), loss_target.reshape(S, D))

    d_norm = {nm: [None] * L for nm in ("ffn1_norm", "mix_norm", "ffn2_norm")}
    d_poolw, d_pscale = [None] * L, [None] * L
    group_a = ("ffn2_w_gate", "ffn2_w_up", "ffn2_w_down", "w_out")
    group_b = ("ffn1_w_gate", "ffn1_w_up", "ffn1_w_down", "w_in")
    acc = {}

    def exchange(full, group, after, tag):
        srcs = [full[nm] for nm in group]
        slots = [lax.empty((NDEV, g.shape[0] // NDEV, D), bf16) for g in srcs]
        ssem, rsem, srcs, slots, token = _rs_start(srcs, slots, after, tag)
        return (srcs, slots, ssem, rsem, tag), token

    def update(l, group, flight, after):
        srcs, slots, ssem, rsem, tag = flight
        srcs, slots = _rs_wait(srcs, slots, ssem, rsem, after, tag)
        for nm, full_g, slots_g in zip(group, srcs, slots):
            acc[nm] = _reduce_adamw(acc.get(nm), me_arr, full_g, slots_g, w_rows[nm], m_rows[nm], v_rows[nm], l)
        return [acc[nm][0] for nm in group], slots

    core_arr = lax.axis_index("c").reshape(1).astype(jnp.int32)
    chip_arr = (2 * lax.axis_index("x") + lax.axis_index("y")).reshape(1).astype(jnp.int32)

    def exchange_cores(full, group, after, tag):
        full4s = [full[nm].reshape(4, 2, full[nm].shape[0] // NDEV, D) for nm in group]
        bufs = [lax.empty((4,) + a.shape[2:], bf16) for a in full4s]
        ssem, rsem, full4s, bufs, token = _pair_start(full4s, bufs, after, tag)
        return (full4s, bufs, ssem, rsem, tag), token

    def exchange_chips(flight, after):
        full4s, bufs, ssem, rsem, tag = flight
        full4s, bufs = _pair_wait(full4s, bufs, ssem, rsem, after, tag)
        sums = _pair_sum(core_arr, full4s, bufs)
        slots = [lax.empty(a.shape, bf16) for a in sums]
        ssem, rsem, sums, slots, token = _chip_start(sums, slots, bufs[0], tag)
        return (sums, slots, ssem, rsem, tag), token

    def update_chips(l, group, flight, after):
        sums, slots, ssem, rsem, tag = flight
        sums, slots = _chip_wait(sums, slots, ssem, rsem, after, tag)
        for nm, sums_g, slots_g in zip(group, sums, slots):
            own = sums_g.reshape(4 * sums_g.shape[1], D)
            acc[nm] = _reduce_adamw(acc.get(nm), chip_arr, own, slots_g, w_rows[nm], m_rows[nm], v_rows[nm], l)
        return [acc[nm][0] for nm in group]

    flights = {}
    token_b = None
    for l in reversed(range(L)):
        sv = saved[l]
        gt1, ut1, dn1, wint, wout, gt2, ut2, dn2 = gathered[l]
        full = {}
        dx, dgate, dup, h, dy, d_norm["ffn2_norm"][l] = _ffn_bwd_d(
            sv["x2"], g_ffn2[l], sv["gate2"], sv["up2"], dx, gt2, ut2, dn2, after=() if token_b is None else (token_b,))
        full["ffn2_w_gate"], full["ffn2_w_up"], full["ffn2_w_down"] = _ffn_bwd_w(h, dy, sv["gate2"], sv["up2"], dgate, dup)

        dxb, dyp, do1, do4, do16, dl1, dl4, dl16 = _mix_out_bwd(dx, sv["o"], wout)
        full["w_out"] = _wgrad(sv["mixed"], dxb)
        flights[l, "a"], token_a = (exchange_cores if l == 0 else exchange)(full, group_a, dxb, f"a{l}")
        dvp, d_poolw[l], d_pscale[l] = _pool_bwd(dyp, sv["diff"], wbd[l], pscale[l], after=(token_a,))
        dos, dls = (do1, flat(do4), flat(do16)), (dl1, flat(dl4), flat(dl16))
        dqkv = []
        for b, lc in enumerate((S, S // 4, S // 16)):
            qb, kb, vb = sv["qkv"][b]
            dqkv.append(_attn_bwd(qb, kb, vb, dos[b], sv["lse"][b], dls[b], lc))
        d4 = tuple(r4(a) for a in dqkv[1])
        d16 = tuple(r16(a) for a in dqkv[2])
        mix_after = ()
        if l == 0:
            flights[0, "a"], token_a = exchange_chips(flights[0, "a"], [dqkv[0][0], dqkv[1][0], dqkv[2][0]])
            mix_after = (token_a,)
        dx, dproj, d_norm["mix_norm"][l] = _mix_in_bwd(dx, sv["x1"], g_mix[l], wint, tabs, dvp, dqkv[0], d4, d16,
                                                       after=mix_after)
        full["w_in"] = _wgrad(dproj, sv["hmix"])

        dx, dgate, dup, h, dy, d_norm["ffn1_norm"][l] = _ffn_bwd_d(sv["x0"], g_ffn1[l], sv["gate1"], sv["up1"], dx, gt1, ut1, dn1)
        full["ffn1_w_gate"], full["ffn1_w_up"], full["ffn1_w_down"] = _ffn_bwd_w(h, dy, sv["gate1"], sv["up1"], dgate, dup)

        after = dx
        if l + 1 < L and l + 1 >= 2:
            after, _ = update(l + 1, group_a, flights.pop((l + 1, "a")), after)
        if l + 1 < L and l + 1 >= 3:
            after, _ = update(l + 1, group_b, flights.pop((l + 1, "b")), after)
        if l > 0:
            flights[l, "b"], token_b = exchange(full, group_b, after, f"b{l}")

    flights[0, "b"], token_b = exchange_cores(full, group_b, dx, "b0")
    pad8 = lambda a: jnp.pad(a, ((0, 8 - a.shape[0]), (0, 0)))
    misc = jnp.concatenate([d_final, jnp.concatenate(d_pscale, axis=1), loss_part], axis=0)
    small = jnp.concatenate(
        [pad8(jnp.concatenate(d_norm[nm], axis=0)) for nm in ("ffn1_norm", "mix_norm", "ffn2_norm")]
        + [pad8(misc), jnp.stack(d_poolw).reshape(L * 16, D)], axis=0)
    small_slots = lax.dynamic_update_slice(lax.empty((NDEV, SMALL_ROWS, D), f32), small[None], (me_idx, 0, 0))
    pack_sems = _rs_start([small], [small_slots], token_b, "pack")
    flights[0, "b"], token_b = exchange_chips(flights[0, "b"], pack_sems[-1])

    after = token_b
    for key in [(2, "b"), (1, "a"), (1, "b")]:
        after, _ = update(key[0], group_a if key[1] == "a" else group_b, flights.pop(key), after)
    _, pack_slots = _rs_wait(pack_sems[2], pack_sems[3], pack_sems[0], pack_sems[1], after, "pack")
    sm = _sum_slots(pack_slots[0], SMALL_ROWS)
    grads = {}
    grads["ffn1_norm"], grads["mix_norm"], grads["ffn2_norm"] = sm[0:L], sm[8:8 + L], sm[16:16 + L]
    grads["final_norm"] = sm[24]
    grads["pool_scale"] = sm[25].reshape(L, PW)
    grads["pool_w"] = sm[32:32 + L * 16].reshape(L, 4, 64, 64)
    loss = sm[26, 0]
    upd = {nm: _adamw(weights[nm], grads[nm], moms[nm], vels[nm]) for nm in names if nm not in _BIG}
    after = update_chips(0, group_a, flights.pop((0, "a")), [upd[nm][0] for nm in upd])
    update_chips(0, group_b, flights.pop((0, "b")), after)
    for nm in _BIG:
        grads[nm], upd[nm] = as_rows(acc[nm][0], nm), tuple(as_rows(a, nm) for a in acc[nm][1:])
    return (loss, dx.reshape(1, S, D), *[grads[nm] for nm in names], *[upd[nm][0] for nm in names],
            *[upd[nm][1] for nm in names], *[upd[nm][2] for nm in names])
```

```python
import jax
import jax.numpy as jnp
from jax import lax
from jax.experimental import pallas as pl
from jax.experimental.pallas import tpu as pltpu

f32 = jnp.float32
bf16 = jnp.bfloat16
SDS = jax.ShapeDtypeStruct

D = 1024
S = 2048
F = 2816
L = 4
PW = 256
AW = 768
PROJ = PW + 3 * AW
NDEV = 8
TM = 256
QB = 128
HALF = 64
NG = AW // 128
NORM_EPS = 1e-6
MASK_VALUE = -1e30
ROPE_THETA = 500000.0
ADAM_LR, ADAM_B1, ADAM_B2, ADAM_EPS, ADAM_WD, ADAM_STEP = 0.001, 0.9, 0.999, 1e-08, 0.01, 10
POOL_WINDOWS = (2, 4, 8, 16)
PAD = 8
SMALL_ROWS = 96
VMEM_LIMIT = 56 * 1024 * 1024

_CP = pltpu.CompilerParams(vmem_limit_bytes=VMEM_LIMIT)
_ANY = pl.BlockSpec(memory_space=pl.ANY)
_HBM = pl.BlockSpec(memory_space=pltpu.HBM)
_SEM = pl.BlockSpec(memory_space=pltpu.SEMAPHORE)
_MESH = pl.DeviceIdType.MESH
_CP_SPLIT = pltpu.CompilerParams(has_side_effects=pltpu.SideEffectType.DATAFLOW_SIDE_EFFECTING)


def _dot_nn(a, b):
    return lax.dot_general(a, b, (((1,), (0,)), ((), ())), preferred_element_type=f32)


def _dot_nt(a, b):
    return lax.dot_general(a, b, (((1,), (1,)), ((), ())), preferred_element_type=f32)


def _dot_tn(a, b):
    return lax.dot_general(a, b, (((0,), (0,)), ((), ())), preferred_element_type=f32)


def _rms(x, g):
    r = lax.rsqrt(jnp.mean(x * x, axis=-1, keepdims=True) + NORM_EPS)
    xh = x * r
    return r, xh, xh * g


def _rms_bwd(dh, r, xh, g):
    dxh = dh * g
    return r * (dxh - xh * jnp.mean(dxh * xh, axis=-1, keepdims=True))


def _tile(cols, rows=TM):
    return pl.BlockSpec((rows, cols), lambda i: (i, 0))


def _const(shape):
    return pl.BlockSpec(shape, lambda i: (0,) * len(shape))


def _layer(rows, cols):
    return pl.BlockSpec((rows, cols), lambda i: (0, 0), pipeline_mode=pl.Buffered(1))


def _p4(cols=AW):
    return pl.BlockSpec((4, TM // 4, cols), lambda i: (0, i, 0))


def _p16(cols=AW):
    return pl.BlockSpec((16, TM // 16, cols), lambda i: (0, i, 0))


def _cols(j):
    return slice(128 * j, 128 * (j + 1))


def _follow(body, n_in, after):
    k = len(after)
    return body if k == 0 else (lambda *refs: body(*refs[:n_in], *refs[n_in + k:]))


def _ffn_fwd(x, g, gt, ut, dn, after=()):
    def body(x_ref, g_ref, gt_ref, ut_ref, dn_ref, xo_ref, gate_ref, up_ref):
        x = x_ref[...]
        _, _, hn = _rms(x, g_ref[...])
        h = hn.astype(bf16)
        gate = _dot_nt(h, gt_ref[...])
        up = _dot_nt(h, ut_ref[...])
        gate_ref[...] = gate.astype(bf16)
        up_ref[...] = up.astype(bf16)
        a = (gate * jax.nn.sigmoid(gate) * up).astype(bf16)
        xo_ref[...] = x + 0.5 * _dot_nn(a, dn_ref[...])

    rows = 2 * TM
    return pl.pallas_call(
        _follow(body, 5, after), grid=(S // rows,),
        in_specs=[_tile(D, rows), _layer(1, D), _layer(F, D), _layer(F, D), _layer(F, D)] + [_ANY] * len(after),
        out_specs=[_tile(D, rows), _tile(F, rows), _tile(F, rows)],
        out_shape=[SDS((S, D), f32), SDS((S, F), bf16), SDS((S, F), bf16)],
        compiler_params=_CP, name="ffn_fwd")(x, g, gt, ut, dn, *after)


def _ffn_bwd_d(x, g, gate, up, dxo, gt, ut, dn, after=()):
    def body(x_ref, g_ref, gate_ref, up_ref, dxo_ref, gt_ref, ut_ref, dn_ref,
             dx_ref, dgate_ref, dup_ref, h_ref, dy_ref, dg_ref):
        x = x_ref[...]
        g = g_ref[...]
        r, xh, hn = _rms(x, g)
        h_ref[...] = hn.astype(bf16)
        dxo = dxo_ref[...]
        dy = (0.5 * dxo).astype(bf16)
        dy_ref[...] = dy
        da = _dot_nt(dy, dn_ref[...])
        gate = gate_ref[...].astype(f32)
        up = up_ref[...].astype(f32)
        sg = jax.nn.sigmoid(gate)
        dgate = (da * up * (sg * (1.0 + gate * (1.0 - sg)))).astype(bf16)
        dup = (da * (gate * sg)).astype(bf16)
        dgate_ref[...] = dgate
        dup_ref[...] = dup
        dh = _dot_nn(dgate, gt_ref[...]) + _dot_nn(dup, ut_ref[...])

        @pl.when(pl.program_id(0) == 0)
        def _():
            dg_ref[...] = jnp.zeros_like(dg_ref)

        dg_ref[...] += jnp.sum(dh * xh, axis=0, keepdims=True)
        dx_ref[...] = dxo + _rms_bwd(dh, r, xh, g)

    return pl.pallas_call(
        _follow(body, 8, after), grid=(S // TM,),
        in_specs=[_tile(D), _layer(1, D), _tile(F), _tile(F), _tile(D),
                  _layer(F, D), _layer(F, D), _layer(F, D)] + [_ANY] * len(after),
        out_specs=[_tile(D), _tile(F), _tile(F), _tile(D), _tile(D), _const((1, D))],
        out_shape=[SDS((S, D), f32), SDS((S, F), bf16), SDS((S, F), bf16), SDS((S, D), bf16),
                   SDS((S, D), bf16), SDS((1, D), f32)],
        compiler_params=_CP, name="ffn_bwd_d")(x, g, gate, up, dxo, gt, ut, dn, *after)


def _ffn_bwd_w(h, dy, gate, up, dgate, dup):
    fc = 256

    def body(h_ref, dy_ref, gate_ref, up_ref, dgate_ref, dup_ref, dgt_ref, dut_ref, ddn_ref):
        gate = gate_ref[...].astype(f32)
        a = (gate * jax.nn.sigmoid(gate) * up_ref[...].astype(f32)).astype(bf16)
        ddn_ref[...] = _dot_tn(a, dy_ref[...]).astype(bf16)
        h = h_ref[...]
        dgt_ref[...] = _dot_tn(dgate_ref[...], h).astype(bf16)
        dut_ref[...] = _dot_tn(dup_ref[...], h).astype(bf16)

    col = pl.BlockSpec((S, fc), lambda j: (0, j))
    row = pl.BlockSpec((fc, D), lambda j: (j, 0))
    full = pl.BlockSpec((S, D), lambda j: (0, 0))
    return pl.pallas_call(
        body, grid=(F // fc,),
        in_specs=[full, full, col, col, col, col],
        out_specs=[row, row, row],
        out_shape=[SDS((F, D), bf16)] * 3,
        compiler_params=_CP, name="ffn_bwd_w")(h, dy, gate, up, dgate, dup)


def _wgrad(a, b):
    m, n = a.shape[1], b.shape[1]
    mc = 2 * TM

    def body(a_ref, b_ref, o_ref):
        o_ref[...] = _dot_tn(a_ref[...], b_ref[...]).astype(bf16)

    return pl.pallas_call(
        body, grid=(m // mc,),
        in_specs=[pl.BlockSpec((S, mc), lambda j: (0, j)), pl.BlockSpec((S, n), lambda j: (0, 0))],
        out_specs=pl.BlockSpec((mc, n), lambda j: (j, 0)),
        out_shape=SDS((m, n), bf16),
        compiler_params=_CP, name="wgrad")(a, b)


def _rope(t, c, sn, sp):
    return t * c + pltpu.roll(t, 120, 1) * sn + pltpu.roll(t, 8, 1) * sp


def _rope_bwd(d, c, sn, sp):
    return d * c + pltpu.roll(d * sn, 8, 1) + pltpu.roll(d * sp, 120, 1)


def _rope_tables(positions):
    inv_freq = ROPE_THETA ** (-jnp.arange(0, 16, 2, dtype=f32) / 16)
    ang = positions.reshape(S, 1).astype(f32) * inv_freq
    cos, sin = jnp.cos(ang), jnp.sin(ang)
    one = jnp.ones((S, 48), f32)
    zero8 = jnp.zeros((S, 8), f32)
    zero48 = jnp.zeros((S, 48), f32)
    c = jnp.concatenate([cos, cos, one], axis=1)
    sn = jnp.concatenate([-sin, zero8, zero48], axis=1)
    sp = jnp.concatenate([zero8, sin, zero48], axis=1)
    return tuple(jnp.concatenate([t, t], axis=1) for t in (c, sn, sp))


def _dilation_perm(n, back=False):
    per = TM // n
    i = lax.broadcasted_iota(jnp.int32, (TM, TM), 1 if back else 0)
    j = lax.broadcasted_iota(jnp.int32, (TM, TM), 0 if back else 1)
    return jnp.where(j == n * (i % per) + i // per, 1.0, 0.0).astype(bf16)


def _mix_in_fwd(x, g, wint, tabs):
    def body(x_ref, g_ref, w_ref, c_ref, sn_ref, sp_ref,
             h_ref, vp_ref, q1, k1, v1, q4, k4, v4, q16, k16, v16):
        _, _, hn = _rms(x_ref[...], g_ref[...])
        h = hn.astype(bf16)
        h_ref[...] = h
        proj = _dot_nt(h, w_ref[...])
        vp_ref[...] = proj[:, :PW]
        c, sn, sp = c_ref[...], sn_ref[...], sp_ref[...]
        perm4, perm16 = _dilation_perm(4), _dilation_perm(16)
        for kind, (o1, o4, o16) in enumerate(((q1, q4, q16), (k1, k4, k16), (v1, v4, v16))):
            for j in range(NG):
                t = proj[:, PW + kind * AW + 128 * j: PW + kind * AW + 128 * (j + 1)]
                if kind == 0:
                    t = _rope(t, c, sn, sp) * 0.125
                elif kind == 1:
                    t = _rope(t, c, sn, sp)
                o1[:, _cols(j)] = t.astype(bf16)
            nat = o1[...]
            o4[...] = _dot_nn(perm4, nat).astype(bf16).reshape(4, TM // 4, AW)
            o16[...] = _dot_nn(perm16, nat).astype(bf16).reshape(16, TM // 16, AW)

    nat, d4, d16 = SDS((S, AW), bf16), SDS((4, S // 4, AW), bf16), SDS((16, S // 16, AW), bf16)
    return pl.pallas_call(
        body, grid=(S // TM,),
        in_specs=[_tile(D), _layer(1, D), _layer(PROJ, D), _tile(128), _tile(128), _tile(128)],
        out_specs=[_tile(D), _tile(PW)] + [_tile(AW)] * 3 + [_p4()] * 3 + [_p16()] * 3,
        out_shape=[SDS((S, D), bf16), SDS((S, PW), f32)] + [nat] * 3 + [d4] * 3 + [d16] * 3,
        compiler_params=_CP, name="mix_in_fwd")(x, g, wint, *tabs)


def _mix_in_bwd(dxo, x, g, wint, tabs, dvp, d1, d4, d16, after=()):
    def body(dxo_ref, x_ref, g_ref, w_ref, c_ref, sn_ref, sp_ref, dvp_ref,
             dq1, dk1, dv1, dq4, dk4, dv4, dq16, dk16, dv16,
             dx_ref, dproj_ref, dg_ref):
        c, sn, sp = c_ref[...], sn_ref[...], sp_ref[...]
        dproj_ref[:, :PW] = dvp_ref[...].astype(bf16)
        back4, back16 = _dilation_perm(4, True), _dilation_perm(16, True)
        for kind, (a1, a4, a16) in enumerate(((dq1, dq4, dq16), (dk1, dk4, dk16), (dv1, dv4, dv16))):
            n4 = _dot_nn(back4, a4[...].reshape(TM, AW))
            n16 = _dot_nn(back16, a16[...].reshape(TM, AW))
            for j in range(NG):
                t = a1[:, _cols(j)].astype(f32) + n4[:, _cols(j)] + n16[:, _cols(j)]
                if kind == 0:
                    t = _rope_bwd(t * 0.125, c, sn, sp)
                elif kind == 1:
                    t = _rope_bwd(t, c, sn, sp)
                dproj_ref[:, PW + kind * AW + 128 * j: PW + kind * AW + 128 * (j + 1)] = t.astype(bf16)
        g = g_ref[...]
        r_, xh, _ = _rms(x_ref[...], g)
        dh = _dot_nn(dproj_ref[...], w_ref[...])

        @pl.when(pl.program_id(0) == 0)
        def _():
            dg_ref[...] = jnp.zeros_like(dg_ref)

        dg_ref[...] += jnp.sum(dh * xh, axis=0, keepdims=True)
        dx_ref[...] = dxo_ref[...] + _rms_bwd(dh, r_, xh, g)

    return pl.pallas_call(
        _follow(body, 17, after), grid=(S // TM,),
        in_specs=[_tile(D), _tile(D), _layer(1, D), _layer(PROJ, D), _tile(128), _tile(128), _tile(128),
                  _tile(PW)] + [_tile(AW)] * 3 + [_p4()] * 3 + [_p16()] * 3 + [_ANY] * len(after),
        out_specs=[_tile(D), _tile(PROJ), _const((1, D))],
        out_shape=[SDS((S, D), f32), SDS((S, PROJ), bf16), SDS((1, D), f32)],
        compiler_params=_CP, name="mix_in_bwd")(dxo, x, g, wint, *tabs, dvp, *d1, *d4, *d16, *after)


def _pool_sums(pad_ref, base, rows, adjoint):
    lane_group = lax.broadcasted_iota(jnp.int32, (rows, PW), 1) // 64
    sign = -1 if adjoint else 1

    def sh(o):
        return pad_ref[pl.ds(PAD + base + sign * o, rows), :]

    out = None
    acc = None
    lo, hi = 0, 0
    for gi, w in enumerate(POOL_WINDOWS):
        for o in list(range(-(w // 2), lo)) + list(range(hi, w - w // 2)):
            acc = sh(o) if acc is None else acc + sh(o)
        lo, hi = -(w // 2), w - w // 2
        out = acc if out is None else jnp.where(lane_group >= gi, acc, out)
    return out


def _pool_counts(base, rows):
    pos = base + lax.broadcasted_iota(jnp.int32, (rows, PW), 0)
    lane_group = lax.broadcasted_iota(jnp.int32, (rows, PW), 1) // 64
    cnt = None
    for gi, w in enumerate(POOL_WINDOWS):
        lo = jnp.maximum(pos - w // 2, 0)
        hi = jnp.minimum(pos + w - 1 - w // 2, S - 1)
        c = (hi - lo + 1).astype(f32)
        cnt = c if cnt is None else jnp.where(lane_group >= gi, c, cnt)
    return cnt


def _pool_fwd(vp, wbd, scale):
    ch = 256

    def body(vp_ref, w_ref, sc_ref, y_ref, diff_ref, pad):
        pad[pl.ds(0, PAD), :] = jnp.zeros((PAD, PW), f32)
        pad[pl.ds(PAD + S, PAD), :] = jnp.zeros((PAD, PW), f32)
        pad[pl.ds(PAD, S), :] = vp_ref[...]
        for b in range(S // ch):
            base = b * ch
            pooled = _pool_sums(pad, base, ch, False) / _pool_counts(base, ch)
            diff = (pooled - vp_ref[pl.ds(base, ch), :]).astype(bf16)
            diff_ref[pl.ds(base, ch), :] = diff
            y_ref[pl.ds(base, ch), :] = _dot_nn(diff, w_ref[...]) * sc_ref[...]

    whole = lambda shape: pl.BlockSpec(shape, lambda i: (0,) * len(shape))
    return pl.pallas_call(
        body, grid=(1,),
        in_specs=[whole((S, PW)), whole((PW, PW)), whole((1, PW))],
        out_specs=[whole((S, PW)), whole((S, PW))],
        out_shape=[SDS((S, PW), f32), SDS((S, PW), bf16)],
        scratch_shapes=[pltpu.VMEM((S + 2 * PAD, PW), f32)],
        compiler_params=_CP, name="pool_fwd")(vp, wbd, scale)


def _pool_bwd(dy, diff, wbd, scale, after=()):
    ch = 256

    def body(dy_ref, diff_ref, w_ref, sc_ref, dvp_ref, dw_ref, dsc_ref, pad):
        pad[pl.ds(0, PAD), :] = jnp.zeros((PAD, PW), f32)
        pad[pl.ds(PAD + S, PAD), :] = jnp.zeros((PAD, PW), f32)
        dw = jnp.zeros((PW, PW), f32)
        dsc = jnp.zeros((1, PW), f32)
        for b in range(S // ch):
            base = b * ch
            dy = dy_ref[pl.ds(base, ch), :]
            diff = diff_ref[pl.ds(base, ch), :]
            dsc = dsc + jnp.sum(dy * _dot_nn(diff, w_ref[...]), axis=0, keepdims=True)
            dz = (dy * sc_ref[...]).astype(bf16)
            dw = dw + _dot_tn(diff, dz)
            ddiff = _dot_nt(dz, w_ref[...])
            dvp_ref[pl.ds(base, ch), :] = -ddiff
            pad[pl.ds(PAD + base, ch), :] = ddiff / _pool_counts(base, ch)
        for gi in range(4):
            dw_ref[gi] = dw[64 * gi:64 * (gi + 1), 64 * gi:64 * (gi + 1)]
        dsc_ref[...] = dsc
        for b in range(S // ch):
            base = b * ch
            dvp_ref[pl.ds(base, ch), :] += _pool_sums(pad, base, ch, True)

    whole = lambda shape: pl.BlockSpec(shape, lambda i: (0,) * len(shape))
    return pl.pallas_call(
        _follow(body, 4, after), grid=(1,),
        in_specs=[whole((S, PW)), whole((S, PW)), whole((PW, PW)), whole((1, PW))] + [_ANY] * len(after),
        out_specs=[whole((S, PW)), whole((4, 64, 64)), whole((1, PW))],
        out_shape=[SDS((S, PW), f32), SDS((4, 64, 64), f32), SDS((1, PW), f32)],
        scratch_shapes=[pltpu.VMEM((S + 2 * PAD, PW), f32)],
        compiler_params=_CP, name="pool_bwd")(dy, diff, wbd, scale, *after)


def _attn_blocks(lc):
    bpc = lc // QB
    kw = min(2 * QB, lc)
    blocks = []
    for b in range(S // QB):
        t0 = (b % bpc) * QB
        ks_in = min(max(t0 - HALF, 0), lc - kw)
        blocks.append((b * QB, (b // bpc) * lc + ks_in, t0 - ks_in))
    return kw, blocks


def _attn_bias(bias_ref, kw, shifts):
    r = lax.broadcasted_iota(jnp.int32, (2 * QB, kw), 0) % QB
    c = lax.broadcasted_iota(jnp.int32, (2 * QB, kw), 1)
    for i, shift in enumerate(shifts):
        bias_ref[i] = jnp.where(jnp.abs(r + shift - c) <= HALF, 0.0, MASK_VALUE).astype(f32)


def _head_put(stats, pair, v0, v1, lane):
    return jnp.where(lane == 2 * pair, v0, jnp.where(lane == 2 * pair + 1, v1, stats))


def _head_cols(stats, pair, lane):
    c0 = jnp.sum(jnp.where(lane == 2 * pair, stats, 0.0), axis=-1, keepdims=True)
    c1 = jnp.sum(jnp.where(lane == 2 * pair + 1, stats, 0.0), axis=-1, keepdims=True)
    return jnp.concatenate([c0, c1], axis=0)


def _head_spread(stats, pair, head0):
    return jnp.where(head0, stats[:, 2 * pair:2 * pair + 1], stats[:, 2 * pair + 1:2 * pair + 2])


def _stack_heads(blk, head0):
    zero = jnp.zeros_like(blk)
    return jnp.concatenate([jnp.where(head0, blk, zero), jnp.where(head0, zero, blk)], axis=0)


def _attn_fwd(q, k, v, lc, after=None):
    kw, blocks = _attn_blocks(lc)
    shifts = sorted({b[2] for b in blocks})

    def body(q_ref, k_ref, v_ref, *refs):
        o_ref, lse_ref, bias_ref = refs[-3:]
        lane = lax.broadcasted_iota(jnp.int32, (QB, 128), 1)
        head0 = lane < 64
        pair = pl.program_id(0)
        _attn_bias(bias_ref, kw, shifts)

        @pl.when(pair == 0)
        def _():
            lse_ref[...] = jnp.zeros_like(lse_ref)

        for row0, kstart, shift in blocks:
            q2 = _stack_heads(q_ref[pl.ds(row0, QB), :], head0)
            kb = k_ref[pl.ds(kstart, kw), :]
            vb = v_ref[pl.ds(kstart, kw), :]
            s = _dot_nt(q2, kb) + bias_ref[shifts.index(shift)]
            m = jnp.max(s, axis=-1, keepdims=True)
            p = jnp.exp(s - m)
            den = jnp.sum(p, axis=-1, keepdims=True)
            o2 = _dot_nn(p.astype(bf16), vb) / den
            lse2 = m + jnp.log(den)
            o_ref[pl.ds(row0, QB), :] = jnp.where(head0, o2[:QB], o2[QB:]).astype(bf16)
            lse_ref[pl.ds(row0, QB), :] = _head_put(lse_ref[pl.ds(row0, QB), :], pair, lse2[:QB], lse2[QB:], lane)

    col = pl.BlockSpec((S, 128), lambda p: (0, p))
    extra = () if after is None else (after,)
    return pl.pallas_call(
        body, grid=(NG,), in_specs=[col, col, col] + [_ANY] * len(extra),
        out_specs=[col, pl.BlockSpec((S, 128), lambda p: (0, 0))],
        out_shape=[SDS((S, AW), bf16), SDS((S, 128), f32)],
        scratch_shapes=[pltpu.VMEM((len(shifts), 2 * QB, kw), f32)],
        compiler_params=_CP, name=f"attn_fwd_{lc}")(q, k, v, *extra)


def _attn_bwd(q, k, v, do, lse, delta, lc):
    kw, blocks = _attn_blocks(lc)
    shifts = sorted({b[2] for b in blocks})

    def body(q_ref, k_ref, v_ref, do_ref, lse_ref, dl_ref, dq_ref, dk_out, dv_out, bias_ref, dk_ref, dv_ref):
        lane = lax.broadcasted_iota(jnp.int32, (QB, 128), 1)
        head0 = lane < 64
        pair = pl.program_id(0)
        _attn_bias(bias_ref, kw, shifts)
        dk_ref[...] = jnp.zeros_like(dk_ref)
        dv_ref[...] = jnp.zeros_like(dv_ref)
        for row0, kstart, shift in blocks:
            q2 = _stack_heads(q_ref[pl.ds(row0, QB), :], head0)
            do2 = _stack_heads(do_ref[pl.ds(row0, QB), :], head0)
            lse2 = _head_cols(lse_ref[pl.ds(row0, QB), :], pair, lane)
            dl2 = _head_cols(dl_ref[pl.ds(row0, QB), :], pair, lane)
            kb = k_ref[pl.ds(kstart, kw), :]
            vb = v_ref[pl.ds(kstart, kw), :]
            p = jnp.exp(_dot_nt(q2, kb) + bias_ref[shifts.index(shift)] - lse2)
            ds = (p * (_dot_nt(do2, vb) - dl2)).astype(bf16)
            dq2 = _dot_nn(ds, kb)
            dq_ref[pl.ds(row0, QB), :] = jnp.where(head0, dq2[:QB], dq2[QB:]).astype(bf16)
            dk_ref[pl.ds(kstart, kw), :] += _dot_tn(ds, q2)
            dv_ref[pl.ds(kstart, kw), :] += _dot_tn(p.astype(bf16), do2)
        dk_out[...] = dk_ref[...].astype(bf16)
        dv_out[...] = dv_ref[...].astype(bf16)

    col = pl.BlockSpec((S, 128), lambda p: (0, p))
    stats = pl.BlockSpec((S, 128), lambda p: (0, 0))
    return pl.pallas_call(
        body, grid=(NG,), in_specs=[col] * 4 + [stats] * 2, out_specs=[col] * 3,
        out_shape=[SDS((S, AW), bf16)] * 3,
        scratch_shapes=[pltpu.VMEM((len(shifts), 2 * QB, kw), f32), pltpu.VMEM((S, 128), f32),
                        pltpu.VMEM((S, 128), f32)],
        compiler_params=_CP, name=f"attn_bwd_{lc}")(q, k, v, do, lse, delta)


def _mix_out_fwd(x, ypool, o1, l1, o4, l4, o16, l16, wout):
    def body(x_ref, yp_ref, o1_ref, l1_ref, o4_ref, l4_ref, o16_ref, l16_ref, w_ref,
             xo_ref, mixed_ref, o_ref, lse1_ref, lse4_ref, lse16_ref, sl4, sl16, sl):
        head0 = lax.broadcasted_iota(jnp.int32, (TM, 128), 1) < 64
        for r in range(4):
            sl4[pl.ds(r, TM // 4, stride=4), :] = l4_ref[r]
        for r in range(16):
            sl16[pl.ds(r, TM // 16, stride=16), :] = l16_ref[r]
        n4 = _dot_nn(_dilation_perm(4, True), o4_ref[...].reshape(TM, AW))
        n16 = _dot_nn(_dilation_perm(16, True), o16_ref[...].reshape(TM, AW))
        a, b, c = l1_ref[...], sl4[...], sl16[...]
        m = jnp.maximum(jnp.maximum(a, b), c)
        wa, wb, wc = jnp.exp(a - m), jnp.exp(b - m), jnp.exp(c - m)
        den = wa + wb + wc
        wa, wb, wc = wa / den, wb / den, wc / den
        lse = m + jnp.log(den)
        lse1_ref[...] = lse
        sl[...] = lse
        mixed_ref[:, :PW] = yp_ref[...].astype(bf16)
        for j in range(NG):
            y = (_head_spread(wa, j, head0) * o1_ref[:, _cols(j)].astype(f32)
                 + _head_spread(wb, j, head0) * n4[:, _cols(j)] + _head_spread(wc, j, head0) * n16[:, _cols(j)])
            o_ref[:, _cols(j)] = y
            mixed_ref[:, PW + 128 * j: PW + 128 * (j + 1)] = y.astype(bf16)
        for r in range(4):
            lse4_ref[r] = sl[pl.ds(r, TM // 4, stride=4), :]
        for r in range(16):
            lse16_ref[r] = sl[pl.ds(r, TM // 16, stride=16), :]
        xo_ref[...] = x_ref[...] + _dot_nn(mixed_ref[...], w_ref[...])

    return pl.pallas_call(
        body, grid=(S // TM,),
        in_specs=[_tile(D), _tile(PW), _tile(AW), _tile(128), _p4(), _p4(128), _p16(), _p16(128), _layer(D, D)],
        out_specs=[_tile(D), _tile(D), _tile(AW), _tile(128), _p4(128), _p16(128)],
        out_shape=[SDS((S, D), f32), SDS((S, D), bf16), SDS((S, AW), f32), SDS((S, 128), f32),
                   SDS((4, S // 4, 128), f32), SDS((16, S // 16, 128), f32)],
        scratch_shapes=[pltpu.VMEM((TM, 128), f32)] * 3,
        compiler_params=_CP, name="mix_out_fwd")(x, ypool, o1, l1, o4, l4, o16, l16, wout)


def _mix_out_bwd(dxo, o, wout):
    def body(dxo_ref, o_ref, w_ref, dxb_ref, dyp_ref, do1, do4, do16, dl1, dl4, dl16, sdl):
        dxb = dxo_ref[...].astype(bf16)
        dxb_ref[...] = dxb
        dm = _dot_nt(dxb, w_ref[...])
        dyp_ref[...] = dm[:, :PW]
        lane = lax.broadcasted_iota(jnp.int32, (TM, 128), 1)
        head0 = lane < 64
        dl = jnp.zeros((TM, 128), f32)
        for j in range(NG):
            d = dm[:, PW + 128 * j: PW + 128 * (j + 1)]
            prod = d * o_ref[:, _cols(j)]
            dl = _head_put(dl, j, jnp.sum(jnp.where(head0, prod, 0.0), axis=-1, keepdims=True),
                           jnp.sum(jnp.where(head0, 0.0, prod), axis=-1, keepdims=True), lane)
            do1[:, _cols(j)] = d.astype(bf16)
        dl1[...] = dl
        sdl[...] = dl
        for r in range(4):
            dl4[r] = sdl[pl.ds(r, TM // 4, stride=4), :]
        for r in range(16):
            dl16[r] = sdl[pl.ds(r, TM // 16, stride=16), :]
        nat = do1[...]
        do4[...] = _dot_nn(_dilation_perm(4), nat).astype(bf16).reshape(4, TM // 4, AW)
        do16[...] = _dot_nn(_dilation_perm(16), nat).astype(bf16).reshape(16, TM // 16, AW)

    return pl.pallas_call(
        body, grid=(S // TM,),
        in_specs=[_tile(D), _tile(AW), _layer(D, D)],
        out_specs=[_tile(D), _tile(PW), _tile(AW), _p4(), _p16(), _tile(128), _p4(128), _p16(128)],
        out_shape=[SDS((S, D), bf16), SDS((S, PW), f32),
                   SDS((S, AW), bf16), SDS((4, S // 4, AW), bf16), SDS((16, S // 16, AW), bf16),
                   SDS((S, 128), f32), SDS((4, S // 4, 128), f32), SDS((16, S // 16, 128), f32)],
        scratch_shapes=[pltpu.VMEM((TM, 128), f32)],
        compiler_params=_CP, name="mix_out_bwd")(dxo, o, wout)


def _loss_head(x, g, target):
    def body(x_ref, g_ref, t_ref, dx_ref, loss_ref, dg_ref):
        g = g_ref[...]
        r, xh, y = _rms(x_ref[...], g)
        err = y - t_ref[...]
        dy = err * (1.0 / D)

        @pl.when(pl.program_id(0) == 0)
        def _():
            loss_ref[...] = jnp.zeros_like(loss_ref)
            dg_ref[...] = jnp.zeros_like(dg_ref)

        loss_ref[...] += jnp.broadcast_to(0.5 * jnp.sum(jnp.mean(err * err, axis=-1, keepdims=True)), (1, D))
        dg_ref[...] += jnp.sum(dy * xh, axis=0, keepdims=True)
        dx_ref[...] = _rms_bwd(dy, r, xh, g)

    return pl.pallas_call(
        body, grid=(S // TM,),
        in_specs=[_tile(D), _const((1, D)), _tile(D)],
        out_specs=[_tile(D), _const((1, D)), _const((1, D))],
        out_shape=[SDS((S, D), f32), SDS((1, D), f32), SDS((1, D), f32)],
        compiler_params=_CP, name="loss_head")(x, g, target)


def _peer(k):
    x, y, c = lax.axis_index("x"), lax.axis_index("y"), lax.axis_index("c")
    px = 1 - x if k & 4 else x
    py = 1 - y if k & 2 else y
    pc = 1 - c if k & 1 else c
    return (px, py, pc), 4 * px + 2 * py + pc


def _diag_route():
    x, y, c = lax.axis_index("x"), lax.axis_index("y"), lax.axis_index("c")
    idx_x, idx_y = _peer(4)[1], _peer(2)[1]
    return idx_x + c * (idx_y - idx_x), (x + c * (1 - 2 * x), (1 - y) + c * (2 * y - 1), c)


def _hbm(a):
    return pltpu.with_memory_space_constraint(a, pltpu.HBM)


def _rows(ref, idx):
    r = ref.shape[0] // NDEV
    return ref.at[pl.ds(idx * r, r), :]


def _row_copy(ref, idx, send_sem, recv_sem, to):
    return pltpu.make_async_remote_copy(src_ref=_rows(ref, idx), dst_ref=_rows(ref, idx), send_sem=send_sem,
                                        recv_sem=recv_sem, device_id=to, device_id_type=_MESH)


def _place_own(me, shards, l):
    n = len(shards)

    def body(me_ref, *refs):
        for t in range(n):
            refs[n + t][...] = refs[t][...].astype(bf16)

    grid_spec = pltpu.PrefetchScalarGridSpec(
        num_scalar_prefetch=1, grid=(1,),
        in_specs=[pl.BlockSpec((None, s.shape[1], D), lambda i, me_ref: (l, 0, 0)) for s in shards],
        out_specs=[pl.BlockSpec((s.shape[1], D), lambda i, me_ref: (me_ref[0], 0)) for s in shards])
    return pl.pallas_call(
        body, grid_spec=grid_spec, out_shape=[SDS((NDEV * s.shape[1], D), bf16) for s in shards],
        compiler_params=_CP, name="place_own")(me, *shards)


_TOKEN = SDS((8, 128), f32)
def _ag_start(lands, after, l):
    n = len(lands)
    after = list(after) if isinstance(after, (list, tuple)) else [after]

    def body(*refs):
        zones, send_sems, recv_sems, token = refs[:n], refs[n + len(after)], refs[n + len(after) + 1], refs[-1]
        _, me_idx = _peer(0)
        for k, mask in enumerate((1, 4, 2)):
            for t in range(n):
                _row_copy(zones[t], me_idx, send_sems.at[k * n + t], recv_sems.at[k * n + t], _peer(mask)[0]).start()
        token[...] = jnp.zeros_like(token)

    outs = pl.pallas_call(
        body, name=f"ag_start_{l}", in_specs=[_HBM] * n + [_ANY] * len(after),
        out_specs=(_SEM, _SEM, *[_HBM] * n, pl.BlockSpec(memory_space=pltpu.VMEM)),
        out_shape=(pltpu.SemaphoreType.DMA((3 * n,)), pltpu.SemaphoreType.DMA((3 * n,)),
                   *[pltpu.HBM(a.shape, a.dtype) for a in lands], _TOKEN),
        input_output_aliases={t: 2 + t for t in range(n)}, compiler_params=_CP_SPLIT)(
            *[_hbm(a) for a in lands], *after)
    return outs[0], outs[1], list(outs[2:2 + n]), outs[-1]


def _ag_pass(lands, recv_sems, after, l):
    n = len(lands)
    after = list(after) if isinstance(after, (list, tuple)) else [after]

    def body(*refs):
        zones, recv_sems = refs[:n], refs[n]
        psend, precv, token = refs[n + 1 + len(after)], refs[n + 2 + len(after)], refs[-1]
        me, _ = _peer(0)
        sibling, _ = _peer(1)
        for j, mask in enumerate((4, 2)):
            idx = _peer(mask)[1]
            for t in range(n):
                _row_copy(zones[t], idx, psend.at[j * n + t], recv_sems.at[(1 + j) * n + t], me).wait_recv()
                _row_copy(zones[t], idx, psend.at[j * n + t], precv.at[j * n + t], sibling).start()
        fwd_idx, fwd_dev = _diag_route()
        for t in range(n):
            _row_copy(zones[t], fwd_idx, psend.at[2 * n + t], precv.at[2 * n + t], fwd_dev).start()
        token[...] = jnp.zeros_like(token)

    outs = pl.pallas_call(
        body, name=f"ag_pass_{l}", in_specs=[_HBM] * n + [_SEM] + [_ANY] * len(after),
        out_specs=(_SEM, _SEM, *[_HBM] * n, pl.BlockSpec(memory_space=pltpu.VMEM)),
        out_shape=(pltpu.SemaphoreType.DMA((3 * n,)), pltpu.SemaphoreType.DMA((3 * n,)),
                   *[pltpu.HBM(a.shape, a.dtype) for a in lands], _TOKEN),
        input_output_aliases={t: 2 + t for t in range(n)}, compiler_params=_CP_SPLIT)(*lands, recv_sems, *after)
    return outs[0], outs[1], list(outs[2:2 + n]), outs[-1]


def _ag_last(lands, precv, after, l):
    n = len(lands)
    after = list(after) if isinstance(after, (list, tuple)) else [after]

    def body(*refs):
        zones, precv = refs[:n], refs[n]
        qsend, qrecv, token = refs[n + 1 + len(after)], refs[n + 2 + len(after)], refs[-1]
        me, _ = _peer(0)
        sibling, _ = _peer(1)
        idx = _peer(6)[1]
        for t in range(n):
            _row_copy(zones[t], idx, qsend.at[t], precv.at[2 * n + t], me).wait_recv()
            _row_copy(zones[t], idx, qsend.at[t], qrecv.at[t], sibling).start()
        token[...] = jnp.zeros_like(token)

    outs = pl.pallas_call(
        body, name=f"ag_last_{l}", in_specs=[_HBM] * n + [_SEM] + [_ANY] * len(after),
        out_specs=(_SEM, _SEM, *[_HBM] * n, pl.BlockSpec(memory_space=pltpu.VMEM)),
        out_shape=(pltpu.SemaphoreType.DMA((n,)), pltpu.SemaphoreType.DMA((n,)),
                   *[pltpu.HBM(a.shape, a.dtype) for a in lands], _TOKEN),
        input_output_aliases={t: 2 + t for t in range(n)}, compiler_params=_CP_SPLIT)(*lands, precv, *after)
    return outs[0], outs[1], list(outs[2:2 + n]), outs[-1]


def _ag_wait(lands, send_sems, recv_sems, psend, precv, qsend, qrecv, after, l):
    n = len(lands)
    after = list(after) if isinstance(after, (list, tuple)) else [after]

    def body(*refs):
        zones = refs[:n]
        send_sems, recv_sems, psend, precv, qsend, qrecv = refs[n:n + 6]
        me, me_idx = _peer(0)
        for k in range(3):
            for t in range(n):
                _row_copy(zones[t], me_idx, send_sems.at[k * n + t], recv_sems.at[k * n + t], me).wait_send()
        for t in range(n):
            _row_copy(zones[t], _peer(1)[1], send_sems.at[t], recv_sems.at[t], me).wait_recv()
        fwd_idx, _ = _diag_route()
        for j, (mine, theirs) in enumerate(((_peer(4)[1], _peer(5)[1]), (_peer(2)[1], _peer(3)[1]))):
            for t in range(n):
                _row_copy(zones[t], mine, psend.at[j * n + t], precv.at[j * n + t], me).wait_send()
                _row_copy(zones[t], theirs, psend.at[j * n + t], precv.at[j * n + t], me).wait_recv()
        for t in range(n):
            _row_copy(zones[t], fwd_idx, psend.at[2 * n + t], precv.at[2 * n + t], me).wait_send()
            _row_copy(zones[t], _peer(6)[1], qsend.at[t], qrecv.at[t], me).wait_send()
            _row_copy(zones[t], _peer(7)[1], qsend.at[t], qrecv.at[t], me).wait_recv()

    outs = pl.pallas_call(
        body, name=f"ag_wait_{l}", in_specs=[_HBM] * n + [_SEM] * 6 + [_ANY] * len(after),
        out_specs=tuple([_HBM] * n), out_shape=tuple(pltpu.HBM(a.shape, a.dtype) for a in lands),
        input_output_aliases={t: t for t in range(n)}, compiler_params=_CP_SPLIT)(
            *lands, send_sems, recv_sems, psend, precv, qsend, qrecv, *after)
    return list(outs)


def _xchg_src(ref, slot_ref, idx):
    return _rows(ref, idx) if ref.shape[0] == NDEV * slot_ref.shape[1] else ref


def _rs_start(srcs, slots, after, tag):
    n = len(srcs)
    after = list(after) if isinstance(after, (list, tuple)) else [after]

    def body(*refs):
        src, slot = refs[:n], refs[n:2 * n]
        send_sems, recv_sems, token = refs[2 * n + len(after)], refs[2 * n + len(after) + 1], refs[-1]
        _, me_idx = _peer(0)
        for k in range(1, NDEV):
            dev, idx = _peer(k)
            for t in range(n):
                pltpu.make_async_remote_copy(
                    src_ref=_xchg_src(src[t], slot[t], idx), dst_ref=slot[t].at[me_idx],
                    send_sem=send_sems.at[(k - 1) * n + t], recv_sem=recv_sems.at[(k - 1) * n + t],
                    device_id=dev, device_id_type=_MESH).start()
        token[...] = jnp.zeros_like(token)

    outs = pl.pallas_call(
        body, name=f"rs_start_{tag}", in_specs=[_HBM] * (2 * n) + [_ANY] * len(after),
        out_specs=(_SEM, _SEM, *[_HBM] * (2 * n), pl.BlockSpec(memory_space=pltpu.VMEM)),
        out_shape=(pltpu.SemaphoreType.DMA(((NDEV - 1) * n,)), pltpu.SemaphoreType.DMA(((NDEV - 1) * n,)),
                   *[pltpu.HBM(a.shape, a.dtype) for a in list(srcs) + list(slots)], _TOKEN),
        input_output_aliases={t: 2 + t for t in range(2 * n)}, compiler_params=_CP_SPLIT)(
            *[_hbm(a) for a in list(srcs) + list(slots)], *after)
    return outs[0], outs[1], list(outs[2:2 + n]), list(outs[2 + n:2 + 2 * n]), outs[-1]


def _rs_wait(srcs, slots, send_sems, recv_sems, after, tag):
    n = len(srcs)
    after = list(after) if isinstance(after, (list, tuple)) else [after]

    def body(*refs):
        src, slot, send_sems, recv_sems = refs[:n], refs[n:2 * n], refs[2 * n], refs[2 * n + 1]
        me, _ = _peer(0)
        for k in range(1, NDEV):
            idx = _peer(k)[1]
            for t in range(n):
                cp = pltpu.make_async_remote_copy(
                    src_ref=_xchg_src(src[t], slot[t], idx), dst_ref=slot[t].at[idx],
                    send_sem=send_sems.at[(k - 1) * n + t], recv_sem=recv_sems.at[(k - 1) * n + t],
                    device_id=me, device_id_type=_MESH)
                cp.wait_send()
                cp.wait_recv()

    outs = pl.pallas_call(
        body, name=f"rs_wait_{tag}", in_specs=[_HBM] * (2 * n) + [_SEM, _SEM] + [_ANY] * len(after),
        out_specs=tuple([_HBM] * (2 * n)),
        out_shape=tuple(pltpu.HBM(a.shape, a.dtype) for a in list(srcs) + list(slots)),
        input_output_aliases={t: t for t in range(2 * n)}, compiler_params=_CP_SPLIT)(
            *srcs, *slots, send_sems, recv_sems, *after)
    return list(outs[:n]), list(outs[n:])


def _pair_start(full4s, bufs, after, tag):
    n = len(full4s)
    after = list(after) if isinstance(after, (list, tuple)) else [after]

    def body(*refs):
        full, buf = refs[:n], refs[n:2 * n]
        send_sems, recv_sems, token = refs[2 * n + len(after)], refs[2 * n + len(after) + 1], refs[-1]
        c = lax.axis_index("c")
        for t in range(n):
            pltpu.make_async_remote_copy(src_ref=full[t].at[:, 1 - c], dst_ref=buf[t], send_sem=send_sems.at[t],
                                         recv_sem=recv_sems.at[t], device_id=_peer(1)[0], device_id_type=_MESH).start()
        token[...] = jnp.zeros_like(token)

    outs = pl.pallas_call(
        body, name=f"pair_start_{tag}", in_specs=[_HBM] * (2 * n) + [_ANY] * len(after),
        out_specs=(_SEM, _SEM, *[_HBM] * (2 * n), pl.BlockSpec(memory_space=pltpu.VMEM)),
        out_shape=(pltpu.SemaphoreType.DMA((n,)), pltpu.SemaphoreType.DMA((n,)),
                   *[pltpu.HBM(a.shape, a.dtype) for a in list(full4s) + list(bufs)], _TOKEN),
        input_output_aliases={t: 2 + t for t in range(2 * n)}, compiler_params=_CP_SPLIT)(
            *[_hbm(a) for a in list(full4s) + list(bufs)], *after)
    return outs[0], outs[1], list(outs[2:2 + n]), list(outs[2 + n:2 + 2 * n]), outs[-1]


def _pair_wait(full4s, bufs, send_sems, recv_sems, after, tag):
    n = len(full4s)
    after = list(after) if isinstance(after, (list, tuple)) else [after]

    def body(*refs):
        full, buf, send_sems, recv_sems = refs[:n], refs[n:2 * n], refs[2 * n], refs[2 * n + 1]
        c = lax.axis_index("c")
        for t in range(n):
            cp = pltpu.make_async_remote_copy(src_ref=full[t].at[:, 1 - c], dst_ref=buf[t], send_sem=send_sems.at[t],
                                              recv_sem=recv_sems.at[t], device_id=_peer(0)[0], device_id_type=_MESH)
            cp.wait_send()
            cp.wait_recv()

    outs = pl.pallas_call(
        body, name=f"pair_wait_{tag}", in_specs=[_HBM] * (2 * n) + [_SEM, _SEM] + [_ANY] * len(after),
        out_specs=tuple([_HBM] * (2 * n)),
        out_shape=tuple(pltpu.HBM(a.shape, a.dtype) for a in list(full4s) + list(bufs)),
        input_output_aliases={t: t for t in range(2 * n)}, compiler_params=_CP_SPLIT)(
            *full4s, *bufs, send_sems, recv_sems, *after)
    return list(outs[:n]), list(outs[n:])


def _pair_sum(core, full4s, bufs):
    n = len(full4s)

    def body(core_ref, *refs):
        for t in range(n):
            refs[2 * n + t][...] = (refs[t][...].astype(f32) + refs[n + t][...].astype(f32)).astype(bf16)

    grid_spec = pltpu.PrefetchScalarGridSpec(
        num_scalar_prefetch=1, grid=(4,),
        in_specs=[pl.BlockSpec((None, None) + a.shape[2:], lambda j, core_ref: (j, core_ref[0], 0, 0)) for a in full4s]
        + [pl.BlockSpec((None,) + b.shape[1:], lambda j, core_ref: (j, 0, 0)) for b in bufs],
        out_specs=[pl.BlockSpec((None,) + b.shape[1:], lambda j, core_ref: (j, 0, 0)) for b in bufs])
    return pl.pallas_call(
        body, grid_spec=grid_spec, out_shape=[SDS(b.shape, bf16) for b in bufs],
        compiler_params=_CP, name="pair_sum")(core, *full4s, *bufs)


def _chip_start(sums, slots, after, tag):
    n = len(sums)
    after = list(after) if isinstance(after, (list, tuple)) else [after]

    def body(*refs):
        src, slot = refs[:n], refs[n:2 * n]
        send_sems, recv_sems, token = refs[2 * n + len(after)], refs[2 * n + len(after) + 1], refs[-1]
        my_chip = 2 * lax.axis_index("x") + lax.axis_index("y")
        for k, mask in enumerate((4, 2, 6)):
            dev, _ = _peer(mask)
            for t in range(n):
                pltpu.make_async_remote_copy(
                    src_ref=src[t].at[2 * dev[0] + dev[1]], dst_ref=slot[t].at[my_chip],
                    send_sem=send_sems.at[k * n + t], recv_sem=recv_sems.at[k * n + t],
                    device_id=dev, device_id_type=_MESH).start()
        token[...] = jnp.zeros_like(token)

    outs = pl.pallas_call(
        body, name=f"chip_start_{tag}", in_specs=[_HBM] * (2 * n) + [_ANY] * len(after),
        out_specs=(_SEM, _SEM, *[_HBM] * (2 * n), pl.BlockSpec(memory_space=pltpu.VMEM)),
        out_shape=(pltpu.SemaphoreType.DMA((3 * n,)), pltpu.SemaphoreType.DMA((3 * n,)),
                   *[pltpu.HBM(a.shape, a.dtype) for a in list(sums) + list(slots)], _TOKEN),
        input_output_aliases={t: 2 + t for t in range(2 * n)}, compiler_params=_CP_SPLIT)(
            *[_hbm(a) for a in list(sums) + list(slots)], *after)
    return outs[0], outs[1], list(outs[2:2 + n]), list(outs[2 + n:2 + 2 * n]), outs[-1]


def _chip_wait(sums, slots, send_sems, recv_sems, after, tag):
    n = len(sums)
    after = list(after) if isinstance(after, (list, tuple)) else [after]

    def body(*refs):
        src, slot, send_sems, recv_sems = refs[:n], refs[n:2 * n], refs[2 * n], refs[2 * n + 1]
        for k, mask in enumerate((4, 2, 6)):
            dev, _ = _peer(mask)
            chip = 2 * dev[0] + dev[1]
            for t in range(n):
                cp = pltpu.make_async_remote_copy(
                    src_ref=src[t].at[chip], dst_ref=slot[t].at[chip],
                    send_sem=send_sems.at[k * n + t], recv_sem=recv_sems.at[k * n + t],
                    device_id=_peer(0)[0], device_id_type=_MESH)
                cp.wait_send()
                cp.wait_recv()

    outs = pl.pallas_call(
        body, name=f"chip_wait_{tag}", in_specs=[_HBM] * (2 * n) + [_SEM, _SEM] + [_ANY] * len(after),
        out_specs=tuple([_HBM] * (2 * n)),
        out_shape=tuple(pltpu.HBM(a.shape, a.dtype) for a in list(sums) + list(slots)),
        input_output_aliases={t: t for t in range(2 * n)}, compiler_params=_CP_SPLIT)(
            *sums, *slots, send_sems, recv_sems, *after)
    return list(outs[:n]), list(outs[n:])


def _sum_slots(slots, rb):
    r = slots.shape[1]

    def body(s_ref, o_ref):
        acc = s_ref[0].astype(f32)
        for s in range(1, NDEV):
            acc = acc + s_ref[s].astype(f32)
        o_ref[...] = acc

    return pl.pallas_call(
        body, grid=(r // rb,),
        in_specs=[pl.BlockSpec((NDEV, rb, D), lambda i: (0, i, 0))],
        out_specs=pl.BlockSpec((rb, D), lambda i: (i, 0)),
        out_shape=SDS((r, D), f32), compiler_params=_CP, name="sum_slots")(slots)


def _adamw(w, g, m, v):
    shape = w.shape
    cols = shape[-1]
    rows = w.size // cols
    rb = rows
    for cand in (512, 256, 128, 64, 32, 16, 8):
        if rows % cand == 0 and rows > cand:
            rb = cand
            break

    def body(w_ref, g_ref, m_ref, v_ref, d_ref, mo_ref, vo_ref):
        d_ref[...], mo_ref[...], vo_ref[...] = _adamw_math(w_ref[...], g_ref[...], m_ref[...], v_ref[...])

    spec = pl.BlockSpec((rb, cols), lambda i: (i, 0))
    outs = pl.pallas_call(
        body, grid=(rows // rb,), in_specs=[spec] * 4, out_specs=[spec] * 3,
        out_shape=[SDS((rows, cols), f32)] * 3, compiler_params=_CP, name="adamw")(
            *(a.reshape(rows, cols) for a in (w, g, m, v)))
    return tuple(o.reshape(shape) for o in outs)


def _adamw_math(w, g, m, v):
    m = ADAM_B1 * m + (1.0 - ADAM_B1) * g
    v = ADAM_B2 * v + (1.0 - ADAM_B2) * (g * g)
    m_hat = m / (1.0 - ADAM_B1 ** ADAM_STEP)
    v_hat = v / (1.0 - ADAM_B2 ** ADAM_STEP)
    return -ADAM_LR * (m_hat / (jnp.sqrt(v_hat) + ADAM_EPS) + ADAM_WD * w), m, v


def _reduce_adamw(acc, me, full, slots, w, m, v, l):
    _, r, _ = w.shape
    ns = slots.shape[0]
    rb = r // 2 if r > 128 else r

    def body(me_ref, full_ref, slots_ref, w_ref, m_ref, v_ref, *refs):
        go_ref, d_ref, mo_ref, vo_ref = refs[-4:]
        own = full_ref[...].astype(f32)
        g = None
        for s in range(ns):
            part = jnp.where(me_ref[0] == s, own, slots_ref[s].astype(f32))
            g = part if g is None else g + part
        go_ref[...] = g
        d_ref[...], mo_ref[...], vo_ref[...] = _adamw_math(w_ref[...], g, m_ref[...], v_ref[...])

    steps = r // rb
    lay = pl.BlockSpec((None, rb, D), lambda i, me_ref: (l, i, 0))
    n_acc = 0 if acc is None else 4
    grid_spec = pltpu.PrefetchScalarGridSpec(
        num_scalar_prefetch=1, grid=(steps,),
        in_specs=[pl.BlockSpec((rb, D), lambda i, me_ref: (me_ref[0] * steps + i, 0)),
                  pl.BlockSpec((ns, rb, D), lambda i, me_ref: (0, i, 0)), lay, lay, lay] + [_ANY] * n_acc,
        out_specs=[lay] * 4)
    outs = pl.pallas_call(
        body, grid_spec=grid_spec, out_shape=[SDS(w.shape, f32)] * 4,
        input_output_aliases={6 + j: j for j in range(n_acc)},
        compiler_params=_CP, name="reduce_adamw")(me, full, slots, w, m, v, *(() if acc is None else acc))
    return tuple(outs)


_BIG = ("ffn1_w_gate", "ffn1_w_up", "ffn1_w_down", "w_in", "w_out", "ffn2_w_gate", "ffn2_w_up", "ffn2_w_down")
_TRANSPOSED = ("ffn1_w_gate", "ffn1_w_up", "w_in", "ffn2_w_gate", "ffn2_w_up")

def _block_diag(pool_w):
    out = jnp.zeros((L, PW, PW), pool_w.dtype)
    for gi in range(4):
        out = out.at[:, 64 * gi:64 * (gi + 1), 64 * gi:64 * (gi + 1)].set(pool_w[:, gi])
    return out


def kernel(x, positions, ffn1_norm, ffn1_w_gate, ffn1_w_up, ffn1_w_down, mix_norm, w_in, pool_w, pool_scale, w_out, ffn2_norm, ffn2_w_gate, ffn2_w_up, ffn2_w_down, final_norm, loss_target, m_ffn1_norm, m_ffn1_w_gate, m_ffn1_w_up, m_ffn1_w_down, m_mix_norm, m_w_in, m_pool_w, m_pool_scale, m_w_out, m_ffn2_norm, m_ffn2_w_gate, m_ffn2_w_up, m_ffn2_w_down, m_final_norm, v_ffn1_norm, v_ffn1_w_gate, v_ffn1_w_up, v_ffn1_w_down, v_mix_norm, v_w_in, v_pool_w, v_pool_scale, v_w_out, v_ffn2_norm, v_ffn2_w_gate, v_ffn2_w_up, v_ffn2_w_down, v_final_norm):
    weights = dict(ffn1_norm=ffn1_norm, ffn1_w_gate=ffn1_w_gate, ffn1_w_up=ffn1_w_up, ffn1_w_down=ffn1_w_down,
                   mix_norm=mix_norm, w_in=w_in, pool_w=pool_w, pool_scale=pool_scale, w_out=w_out,
                   ffn2_norm=ffn2_norm, ffn2_w_gate=ffn2_w_gate, ffn2_w_up=ffn2_w_up, ffn2_w_down=ffn2_w_down,
                   final_norm=final_norm)
    moms = dict(ffn1_norm=m_ffn1_norm, ffn1_w_gate=m_ffn1_w_gate, ffn1_w_up=m_ffn1_w_up, ffn1_w_down=m_ffn1_w_down,
                mix_norm=m_mix_norm, w_in=m_w_in, pool_w=m_pool_w, pool_scale=m_pool_scale, w_out=m_w_out,
                ffn2_norm=m_ffn2_norm, ffn2_w_gate=m_ffn2_w_gate, ffn2_w_up=m_ffn2_w_up, ffn2_w_down=m_ffn2_w_down,
                final_norm=m_final_norm)
    vels = dict(ffn1_norm=v_ffn1_norm, ffn1_w_gate=v_ffn1_w_gate, ffn1_w_up=v_ffn1_w_up, ffn1_w_down=v_ffn1_w_down,
                mix_norm=v_mix_norm, w_in=v_w_in, pool_w=v_pool_w, pool_scale=v_pool_scale, w_out=v_w_out,
                ffn2_norm=v_ffn2_norm, ffn2_w_gate=v_ffn2_w_gate, ffn2_w_up=v_ffn2_w_up, ffn2_w_down=v_ffn2_w_down,
                final_norm=v_final_norm)
    names = list(weights)

    me_idx = 4 * lax.axis_index("x") + 2 * lax.axis_index("y") + lax.axis_index("c")
    me_arr = me_idx.reshape(1).astype(jnp.int32)

    as_rows = lambda a, nm: jnp.swapaxes(a, 1, 2) if nm in _TRANSPOSED else a
    w_rows = {nm: as_rows(weights[nm], nm) for nm in _BIG}
    m_rows = {nm: as_rows(moms[nm], nm) for nm in _BIG}
    v_rows = {nm: as_rows(vels[nm], nm) for nm in _BIG}

    def landing_zones(l, which):
        return _place_own(me_arr, [w_rows[_BIG[t]] for t in which], l)

    g_ffn1 = [ffn1_norm[l].reshape(1, D) for l in range(L)]
    g_mix = [mix_norm[l].reshape(1, D) for l in range(L)]
    g_ffn2 = [ffn2_norm[l].reshape(1, D) for l in range(L)]
    wbd_all = _block_diag(pool_w).astype(bf16)
    wbd = [wbd_all[l] for l in range(L)]
    pscale = [pool_scale[l].reshape(1, PW) for l in range(L)]
    tabs = _rope_tables(positions)
    flat = lambda a: a.reshape(S, a.shape[-1])
    r4 = lambda a: a.reshape(4, S // 4, a.shape[-1])
    r16 = lambda a: a.reshape(16, S // 16, a.shape[-1])

    first, mid, rest, whole = (0, 1, 2), (3, 4), (5, 6, 7), tuple(range(8))

    def ag_begin(l, which, after, zones=None):
        tag = f"{l}{'' if which == whole else 'h' if which == first else 'm' if which == mid else 'r'}"
        zones = landing_zones(l, which) if zones is None else zones
        send_sems, recv_sems, zones, token = _ag_start(zones, after, tag)
        return dict(tag=tag, zones=zones, s=send_sems, r=recv_sems), token

    def ag_second(ch, after):
        ch["ps"], ch["pr"], ch["zones"], token = _ag_pass(ch["zones"], ch["r"], after, ch["tag"])
        return token

    def ag_third(ch, after):
        ch["qs"], ch["qr"], ch["zones"], token = _ag_last(ch["zones"], ch["pr"], after, ch["tag"])
        return token

    def ag_end(ch, after):
        return _ag_wait(ch["zones"], ch["s"], ch["r"], ch["ps"], ch["pr"], ch["qs"], ch["qr"], after, ch["tag"])

    ch_head, token = ag_begin(0, first, [])
    ch_mid, token = ag_begin(0, mid, token)
    zones_rest, zones_next = landing_zones(0, rest), landing_zones(1, whole)
    early_zones = {ll: landing_zones(ll, whole) for ll in range(2, L)}
    fill = [z for zs in (zones_rest, zones_next, *early_zones.values(), tabs, wbd, [token]) for z in zs]
    head = ag_end(ch_head, ag_third(ch_head, ag_second(ch_head, fill)))
    tok_mid = ag_second(ch_mid, head[0])
    ch_rest, tok_rest = ag_begin(0, rest, tok_mid, zones_rest)
    chains = {}
    chains[1], tok_next = ag_begin(1, whole, head[0], zones_next)
    gathered = [None] * L
    xs = x.reshape(S, D)
    saved = []
    for l in range(L):
        first_after, second_after = (), ()
        if l == 0:
            gt1, ut1, dn1 = head
            first_after = (tok_rest, tok_next)
        else:
            gt1, ut1, dn1, wint, wout, gt2, ut2, dn2 = gathered[l]
        x0 = xs
        x1, gate1, up1 = _ffn_fwd(x0, g_ffn1[l], gt1, ut1, dn1, after=first_after)
        if l == 0:
            wint, wout = ag_end(ch_mid, ag_third(ch_mid, x1))
        hmix, vp, q1, k1, v1, q4, k4, v4, q16, k16, v16 = _mix_in_fwd(x1, g_mix[l], wint, tabs)
        q4, k4, v4, q16, k16, v16 = map(flat, (q4, k4, v4, q16, k16, v16))
        ypool, diff = _pool_fwd(vp, wbd[l], pscale[l])
        after_attn = None
        if l == 0:
            after_attn = ag_second(ch_rest, [ypool, q16])
        o1, l1 = _attn_fwd(q1, k1, v1, S, after=after_attn)
        o4, l4 = _attn_fwd(q4, k4, v4, S // 4, after=after_attn)
        o16, l16 = _attn_fwd(q16, k16, v16, S // 16, after=after_attn)
        if 0 < l < L - 1:
            second_after = (ag_second(chains[l + 1], [o1, o4, o16]),)
        x2, mixed, o, lse1, lse4, lse16 = _mix_out_fwd(x1, ypool, o1, l1, r4(o4), r4(l4), r16(o16), r16(l16), wout)
        if l == 0:
            token = ag_third(ch_rest, x2)
            gt2, ut2, dn2 = ag_end(ch_rest, token)
            gathered[0] = list(head) + [wint, wout, gt2, ut2, dn2]
            second_after = (ag_second(chains[1], gt2),)
        x3, gate2, up2 = _ffn_fwd(x2, g_ffn2[l], gt2, ut2, dn2, after=second_after)
        if l + 1 < L:
            token = ag_third(chains[l + 1], x3)
            if l + 2 < L:
                chains[l + 2], token = ag_begin(l + 2, whole, token, early_zones[l + 2])
            gathered[l + 1] = ag_end(chains[l + 1], token)
        saved.append(dict(x0=x0, x1=x1, x2=x2, gate1=gate1, up1=up1, gate2=gate2, up2=up2, hmix=hmix, diff=diff,
                          qkv=((q1, k1, v1), (q4, k4, v4), (q16, k16, v16)), mixed=mixed, o=o,
                          lse=(lse1, flat(lse4), flat(lse16))))
        xs = x3

    dx, loss_part, d_final = _loss_head(xs, final_norm.reshape(1, D), loss_target.reshape(S, D))

    d_norm = {nm: [None] * L for nm in ("ffn1_norm", "mix_norm", "ffn2_norm")}
    d_poolw, d_pscale = [None] * L, [None] * L
    group_a = ("ffn2_w_gate", "ffn2_w_up", "ffn2_w_down", "w_out")
    group_b = ("ffn1_w_gate", "ffn1_w_up", "ffn1_w_down", "w_in")
    acc = {}

    def exchange(full, group, after, tag):
        srcs = [full[nm] for nm in group]
        slots = [lax.empty((NDEV, g.shape[0] // NDEV, D), bf16) for g in srcs]
        ssem, rsem, srcs, slots, token = _rs_start(srcs, slots, after, tag)
        return (srcs, slots, ssem, rsem, tag), token

    def update(l, group, flight, after):
        srcs, slots, ssem, rsem, tag = flight
        srcs, slots = _rs_wait(srcs, slots, ssem, rsem, after, tag)
        for nm, full_g, slots_g in zip(group, srcs, slots):
            acc[nm] = _reduce_adamw(acc.get(nm), me_arr, full_g, slots_g, w_rows[nm], m_rows[nm], v_rows[nm], l)
        return [acc[nm][0] for nm in group], slots

    core_arr = lax.axis_index("c").reshape(1).astype(jnp.int32)
    chip_arr = (2 * lax.axis_index("x") + lax.axis_index("y")).reshape(1).astype(jnp.int32)

    def exchange_cores(full, group, after, tag):
        full4s = [full[nm].reshape(4, 2, full[nm].shape[0] // NDEV, D) for nm in group]
        bufs = [lax.empty((4,) + a.shape[2:], bf16) for a in full4s]
        ssem, rsem, full4s, bufs, token = _pair_start(full4s, bufs, after, tag)
        return (full4s, bufs, ssem, rsem, tag), token

    def exchange_chips(flight, after):
        full4s, bufs, ssem, rsem, tag = flight
        full4s, bufs = _pair_wait(full4s, bufs, ssem, rsem, after, tag)
        sums = _pair_sum(core_arr, full4s, bufs)
        slots = [lax.empty(a.shape, bf16) for a in sums]
        ssem, rsem, sums, slots, token = _chip_start(sums, slots, bufs[0], tag)
        return (sums, slots, ssem, rsem, tag), token

    def update_chips(l, group, flight, after):
        sums, slots, ssem, rsem, tag = flight
        sums, slots = _chip_wait(sums, slots, ssem, rsem, after, tag)
        for nm, sums_g, slots_g in zip(group, sums, slots):
            own = sums_g.reshape(4 * sums_g.shape[1], D)
            acc[nm] = _reduce_adamw(acc.get(nm), chip_arr, own, slots_g, w_rows[nm], m_rows[nm], v_rows[nm], l)
        return [acc[nm][0] for nm in group]

    flights = {}
    token_b = None
    for l in reversed(range(L)):
        sv = saved[l]
        gt1, ut1, dn1, wint, wout, gt2, ut2, dn2 = gathered[l]
        full = {}
        dx, dgate, dup, h, dy, d_norm["ffn2_norm"][l] = _ffn_bwd_d(
            sv["x2"], g_ffn2[l], sv["gate2"], sv["up2"], dx, gt2, ut2, dn2, after=() if token_b is None else (token_b,))
        full["ffn2_w_gate"], full["ffn2_w_up"], full["ffn2_w_down"] = _ffn_bwd_w(h, dy, sv["gate2"], sv["up2"], dgate, dup)

        dxb, dyp, do1, do4, do16, dl1, dl4, dl16 = _mix_out_bwd(dx, sv["o"], wout)
        full["w_out"] = _wgrad(sv["mixed"], dxb)
        flights[l, "a"], token_a = (exchange_cores if l == 0 else exchange)(full, group_a, dxb, f"a{l}")
        dvp, d_poolw[l], d_pscale[l] = _pool_bwd(dyp, sv["diff"], wbd[l], pscale[l], after=(token_a,))
        dos, dls = (do1, flat(do4), flat(do16)), (dl1, flat(dl4), flat(dl16))
        dqkv = []
        for b, lc in enumerate((S, S // 4, S // 16)):
            qb, kb, vb = sv["qkv"][b]
            dqkv.append(_attn_bwd(qb, kb, vb, dos[b], sv["lse"][b], dls[b], lc))
        d4 = tuple(r4(a) for a in dqkv[1])
        d16 = tuple(r16(a) for a in dqkv[2])
        mix_after = ()
        if l == 0:
            flights[0, "a"], token_a = exchange_chips(flights[0, "a"], [dqkv[0][0], dqkv[1][0], dqkv[2][0]])
            mix_after = (token_a,)
        dx, dproj, d_norm["mix_norm"][l] = _mix_in_bwd(dx, sv["x1"], g_mix[l], wint, tabs, dvp, dqkv[0], d4, d16,
                                                       after=mix_after)
        full["w_in"] = _wgrad(dproj, sv["hmix"])

        dx, dgate, dup, h, dy, d_norm["ffn1_norm"][l] = _ffn_bwd_d(sv["x0"], g_ffn1[l], sv["gate1"], sv["up1"], dx, gt1, ut1, dn1)
        full["ffn1_w_gate"], full["ffn1_w_up"], full["ffn1_w_down"] = _ffn_bwd_w(h, dy, sv["gate1"], sv["up1"], dgate, dup)

        after = dx
        if l + 1 < L and l + 1 >= 2:
            after, _ = update(l + 1, group_a, flights.pop((l + 1, "a")), after)
        if l + 1 < L and l + 1 >= 3:
            after, _ = update(l + 1, group_b, flights.pop((l + 1, "b")), after)
        if l > 0:
            flights[l, "b"], token_b = exchange(full, group_b, after, f"b{l}")

    flights[0, "b"], token_b = exchange_cores(full, group_b, dx, "b0")
    pad8 = lambda a: jnp.pad(a, ((0, 8 - a.shape[0]), (0, 0)))
    misc = jnp.concatenate([d_final, jnp.concatenate(d_pscale, axis=1), loss_part], axis=0)
    small = jnp.concatenate(
        [pad8(jnp.concatenate(d_norm[nm], axis=0)) for nm in ("ffn1_norm", "mix_norm", "ffn2_norm")]
        + [pad8(misc), jnp.stack(d_poolw).reshape(L * 16, D)], axis=0)
    small_slots = lax.dynamic_update_slice(lax.empty((NDEV, SMALL_ROWS, D), f32), small[None], (me_idx, 0, 0))
    pack_sems = _rs_start([small], [small_slots], token_b, "pack")
    flights[0, "b"], token_b = exchange_chips(flights[0, "b"], pack_sems[-1])

    after = token_b
    for key in [(2, "b"), (1, "a"), (1, "b")]:
        after, _ = update(key[0], group_a if key[1] == "a" else group_b, flights.pop(key), after)
    _, pack_slots = _rs_wait(pack_sems[2], pack_sems[3], pack_sems[0], pack_sems[1], after, "pack")
    sm = _sum_slots(pack_slots[0], SMALL_ROWS)
    grads = {}
    grads["ffn1_norm"], grads["mix_norm"], grads["ffn2_norm"] = sm[0:L], sm[8:8 + L], sm[16:16 + L]
    grads["final_norm"] = sm[24]
    grads["pool_scale"] = sm[25].reshape(L, PW)
    grads["pool_w"] = sm[32:32 + L * 16].reshape(L, 4, 64, 64)
    loss = sm[26, 0]
    upd = {nm: _adamw(weights[nm], grads[nm], moms[nm], vels[nm]) for nm in names if nm not in _BIG}
    after = update_chips(0, group_a, flights.pop((0, "a")), [upd[nm][0] for nm in upd])
    update_chips(0, group_b, flights.pop((0, "b")), after)
    for nm in _BIG:
        grads[nm], upd[nm] = as_rows(acc[nm][0], nm), tuple(as_rows(a, nm) for a in acc[nm][1:])
    return (loss, dx.reshape(1, S, D), *[grads[nm] for nm in names], *[upd[nm][0] for nm in names],
            *[upd[nm][1] for nm in names], *[upd[nm][2] for nm in names])
```

```python
import jax
import jax.numpy as jnp
from jax import lax
from jax.experimental import pallas as pl
from jax.experimental.pallas import tpu as pltpu

f32 = jnp.float32
bf16 = jnp.bfloat16
SDS = jax.ShapeDtypeStruct

D = 1024
S = 2048
F = 2816
L = 4
PW = 256
AW = 768
PROJ = PW + 3 * AW
NDEV = 8
TM = 256
QB = 128
HALF = 64
NG = AW // 128
NORM_EPS = 1e-6
MASK_VALUE = -1e30
ROPE_THETA = 500000.0
ADAM_LR, ADAM_B1, ADAM_B2, ADAM_EPS, ADAM_WD, ADAM_STEP = 0.001, 0.9, 0.999, 1e-08, 0.01, 10
POOL_WINDOWS = (2, 4, 8, 16)
PAD = 8
SMALL_ROWS = 96
VMEM_LIMIT = 56 * 1024 * 1024

_CP = pltpu.CompilerParams(vmem_limit_bytes=VMEM_LIMIT)
_ANY = pl.BlockSpec(memory_space=pl.ANY)
_HBM = pl.BlockSpec(memory_space=pltpu.HBM)
_SEM = pl.BlockSpec(memory_space=pltpu.SEMAPHORE)
_MESH = pl.DeviceIdType.MESH
_CP_SPLIT = pltpu.CompilerParams(has_side_effects=pltpu.SideEffectType.DATAFLOW_SIDE_EFFECTING)


def _dot_nn(a, b):
    return lax.dot_general(a, b, (((1,), (0,)), ((), ())), preferred_element_type=f32)


def _dot_nt(a, b):
    return lax.dot_general(a, b, (((1,), (1,)), ((), ())), preferred_element_type=f32)


def _dot_tn(a, b):
    return lax.dot_general(a, b, (((0,), (0,)), ((), ())), preferred_element_type=f32)


def _rms(x, g):
    r = lax.rsqrt(jnp.mean(x * x, axis=-1, keepdims=True) + NORM_EPS)
    xh = x * r
    return r, xh, xh * g


def _rms_bwd(dh, r, xh, g):
    dxh = dh * g
    return r * (dxh - xh * jnp.mean(dxh * xh, axis=-1, keepdims=True))


def _tile(cols, rows=TM):
    return pl.BlockSpec((rows, cols), lambda i: (i, 0))


def _const(shape):
    return pl.BlockSpec(shape, lambda i: (0,) * len(shape))


def _layer(rows, cols):
    return pl.BlockSpec((rows, cols), lambda i: (0, 0), pipeline_mode=pl.Buffered(1))


def _p4(cols=AW):
    return pl.BlockSpec((4, TM // 4, cols), lambda i: (0, i, 0))


def _p16(cols=AW):
    return pl.BlockSpec((16, TM // 16, cols), lambda i: (0, i, 0))


def _cols(j):
    return slice(128 * j, 128 * (j + 1))


def _follow(body, n_in, after):
    k = len(after)
    return body if k == 0 else (lambda *refs: body(*refs[:n_in], *refs[n_in + k:]))


def _ffn_fwd(x, g, gt, ut, dn, after=()):
    def body(x_ref, g_ref, gt_ref, ut_ref, dn_ref, xo_ref, gate_ref, up_ref):
        x = x_ref[...]
        _, _, hn = _rms(x, g_ref[...])
        h = hn.astype(bf16)
        gate = _dot_nt(h, gt_ref[...])
        up = _dot_nt(h, ut_ref[...])
        gate_ref[...] = gate.astype(bf16)
        up_ref[...] = up.astype(bf16)
        a = (gate * jax.nn.sigmoid(gate) * up).astype(bf16)
        xo_ref[...] = x + 0.5 * _dot_nn(a, dn_ref[...])

    rows = 2 * TM
    return pl.pallas_call(
        _follow(body, 5, after), grid=(S // rows,),
        in_specs=[_tile(D, rows), _layer(1, D), _layer(F, D), _layer(F, D), _layer(F, D)] + [_ANY] * len(after),
        out_specs=[_tile(D, rows), _tile(F, rows), _tile(F, rows)],
        out_shape=[SDS((S, D), f32), SDS((S, F), bf16), SDS((S, F), bf16)],
        compiler_params=_CP, name="ffn_fwd")(x, g, gt, ut, dn, *after)


def _ffn_bwd_d(x, g, gate, up, dxo, gt, ut, dn, after=()):
    def body(x_ref, g_ref, gate_ref, up_ref, dxo_ref, gt_ref, ut_ref, dn_ref,
             dx_ref, dgate_ref, dup_ref, h_ref, dy_ref, dg_ref):
        x = x_ref[...]
        g = g_ref[...]
        r, xh, hn = _rms(x, g)
        h_ref[...] = hn.astype(bf16)
        dxo = dxo_ref[...]
        dy = (0.5 * dxo).astype(bf16)
        dy_ref[...] = dy
        da = _dot_nt(dy, dn_ref[...])
        gate = gate_ref[...].astype(f32)
        up = up_ref[...].astype(f32)
        sg = jax.nn.sigmoid(gate)
        dgate = (da * up * (sg * (1.0 + gate * (1.0 - sg)))).astype(bf16)
        dup = (da * (gate * sg)).astype(bf16)
        dgate_ref[...] = dgate
        dup_ref[...] = dup
        dh = _dot_nn(dgate, gt_ref[...]) + _dot_nn(dup, ut_ref[...])

        @pl.when(pl.program_id(0) == 0)
        def _():
            dg_ref[...] = jnp.zeros_like(dg_ref)

        dg_ref[...] += jnp.sum(dh * xh, axis=0, keepdims=True)
        dx_ref[...] = dxo + _rms_bwd(dh, r, xh, g)

    return pl.pallas_call(
        _follow(body, 8, after), grid=(S // TM,),
        in_specs=[_tile(D), _layer(1, D), _tile(F), _tile(F), _tile(D),
                  _layer(F, D), _layer(F, D), _layer(F, D)] + [_ANY] * len(after),
        out_specs=[_tile(D), _tile(F), _tile(F), _tile(D), _tile(D), _const((1, D))],
        out_shape=[SDS((S, D), f32), SDS((S, F), bf16), SDS((S, F), bf16), SDS((S, D), bf16),
                   SDS((S, D), bf16), SDS((1, D), f32)],
        compiler_params=_CP, name="ffn_bwd_d")(x, g, gate, up, dxo, gt, ut, dn, *after)


def _ffn_bwd_w(h, dy, gate, up, dgate, dup):
    fc = 256

    def body(h_ref, dy_ref, gate_ref, up_ref, dgate_ref, dup_ref, dgt_ref, dut_ref, ddn_ref):
        gate = gate_ref[...].astype(f32)
        a = (gate * jax.nn.sigmoid(gate) * up_ref[...].astype(f32)).astype(bf16)
        ddn_ref[...] = _dot_tn(a, dy_ref[...]).astype(bf16)
        h = h_ref[...]
        dgt_ref[...] = _dot_tn(dgate_ref[...], h).astype(bf16)
        dut_ref[...] = _dot_tn(dup_ref[...], h).astype(bf16)

    col = pl.BlockSpec((S, fc), lambda j: (0, j))
    row = pl.BlockSpec((fc, D), lambda j: (j, 0))
    full = pl.BlockSpec((S, D), lambda j: (0, 0))
    return pl.pallas_call(
        body, grid=(F // fc,),
        in_specs=[full, full, col, col, col, col],
        out_specs=[row, row, row],
        out_shape=[SDS((F, D), bf16)] * 3,
        compiler_params=_CP, name="ffn_bwd_w")(h, dy, gate, up, dgate, dup)


def _wgrad(a, b):
    m, n = a.shape[1], b.shape[1]
    mc = 2 * TM

    def body(a_ref, b_ref, o_ref):
        o_ref[...] = _dot_tn(a_ref[...], b_ref[...]).astype(bf16)

    return pl.pallas_call(
        body, grid=(m // mc,),
        in_specs=[pl.BlockSpec((S, mc), lambda j: (0, j)), pl.BlockSpec((S, n), lambda j: (0, 0))],
        out_specs=pl.BlockSpec((mc, n), lambda j: (j, 0)),
        out_shape=SDS((m, n), bf16),
        compiler_params=_CP, name="wgrad")(a, b)


def _rope(t, c, sn, sp):
    return t * c + pltpu.roll(t, 120, 1) * sn + pltpu.roll(t, 8, 1) * sp


def _rope_bwd(d, c, sn, sp):
    return d * c + pltpu.roll(d * sn, 8, 1) + pltpu.roll(d * sp, 120, 1)


def _rope_tables(positions):
    inv_freq = ROPE_THETA ** (-jnp.arange(0, 16, 2, dtype=f32) / 16)
    ang = positions.reshape(S, 1).astype(f32) * inv_freq
    cos, sin = jnp.cos(ang), jnp.sin(ang)
    one = jnp.ones((S, 48), f32)
    zero8 = jnp.zeros((S, 8), f32)
    zero48 = jnp.zeros((S, 48), f32)
    c = jnp.concatenate([cos, cos, one], axis=1)
    sn = jnp.concatenate([-sin, zero8, zero48], axis=1)
    sp = jnp.concatenate([zero8, sin, zero48], axis=1)
    return tuple(jnp.concatenate([t, t], axis=1) for t in (c, sn, sp))


def _dilation_perm(n, back=False):
    per = TM // n
    i = lax.broadcasted_iota(jnp.int32, (TM, TM), 1 if back else 0)
    j = lax.broadcasted_iota(jnp.int32, (TM, TM), 0 if back else 1)
    return jnp.where(j == n * (i % per) + i // per, 1.0, 0.0).astype(bf16)


def _mix_in_fwd(x, g, wint, tabs):
    def body(x_ref, g_ref, w_ref, c_ref, sn_ref, sp_ref,
             h_ref, vp_ref, q1, k1, v1, q4, k4, v4, q16, k16, v16):
        _, _, hn = _rms(x_ref[...], g_ref[...])
        h = hn.astype(bf16)
        h_ref[...] = h
        proj = _dot_nt(h, w_ref[...])
        vp_ref[...] = proj[:, :PW]
        c, sn, sp = c_ref[...], sn_ref[...], sp_ref[...]
        perm4, perm16 = _dilation_perm(4), _dilation_perm(16)
        for kind, (o1, o4, o16) in enumerate(((q1, q4, q16), (k1, k4, k16), (v1, v4, v16))):
            for j in range(NG):
                t = proj[:, PW + kind * AW + 128 * j: PW + kind * AW + 128 * (j + 1)]
                if kind == 0:
                    t = _rope(t, c, sn, sp) * 0.125
                elif kind == 1:
                    t = _rope(t, c, sn, sp)
                o1[:, _cols(j)] = t.astype(bf16)
            nat = o1[...]
            o4[...] = _dot_nn(perm4, nat).astype(bf16).reshape(4, TM // 4, AW)
            o16[...] = _dot_nn(perm16, nat).astype(bf16).reshape(16, TM // 16, AW)

    nat, d4, d16 = SDS((S, AW), bf16), SDS((4, S // 4, AW), bf16), SDS((16, S // 16, AW), bf16)
    return pl.pallas_call(
        body, grid=(S // TM,),
        in_specs=[_tile(D), _layer(1, D), _layer(PROJ, D), _tile(128), _tile(128), _tile(128)],
        out_specs=[_tile(D), _tile(PW)] + [_tile(AW)] * 3 + [_p4()] * 3 + [_p16()] * 3,
        out_shape=[SDS((S, D), bf16), SDS((S, PW), f32)] + [nat] * 3 + [d4] * 3 + [d16] * 3,
        compiler_params=_CP, name="mix_in_fwd")(x, g, wint, *tabs)


def _mix_in_bwd(dxo, x, g, wint, tabs, dvp, d1, d4, d16, after=()):
    def body(dxo_ref, x_ref, g_ref, w_ref, c_ref, sn_ref, sp_ref, dvp_ref,
             dq1, dk1, dv1, dq4, dk4, dv4, dq16, dk16, dv16,
             dx_ref, dproj_ref, dg_ref):
        c, sn, sp = c_ref[...], sn_ref[...], sp_ref[...]
        dproj_ref[:, :PW] = dvp_ref[...].astype(bf16)
        back4, back16 = _dilation_perm(4, True), _dilation_perm(16, True)
        for kind, (a1, a4, a16) in enumerate(((dq1, dq4, dq16), (dk1, dk4, dk16), (dv1, dv4, dv16))):
            n4 = _dot_nn(back4, a4[...].reshape(TM, AW))
            n16 = _dot_nn(back16, a16[...].reshape(TM, AW))
            for j in range(NG):
                t = a1[:, _cols(j)].astype(f32) + n4[:, _cols(j)] + n16[:, _cols(j)]
                if kind == 0:
                    t = _rope_bwd(t * 0.125, c, sn, sp)
                elif kind == 1:
                    t = _rope_bwd(t, c, sn, sp)
                dproj_ref[:, PW + kind * AW + 128 * j: PW + kind * AW + 128 * (j + 1)] = t.astype(bf16)
        g = g_ref[...]
        r_, xh, _ = _rms(x_ref[...], g)
        dh = _dot_nn(dproj_ref[...], w_ref[...])

        @pl.when(pl.program_id(0) == 0)
        def _():
            dg_ref[...] = jnp.zeros_like(dg_ref)

        dg_ref[...] += jnp.sum(dh * xh, axis=0, keepdims=True)
        dx_ref[...] = dxo_ref[...] + _rms_bwd(dh, r_, xh, g)

    return pl.pallas_call(
        _follow(body, 17, after), grid=(S // TM,),
        in_specs=[_tile(D), _tile(D), _layer(1, D), _layer(PROJ, D), _tile(128), _tile(128), _tile(128),
                  _tile(PW)] + [_tile(AW)] * 3 + [_p4()] * 3 + [_p16()] * 3 + [_ANY] * len(after),
        out_specs=[_tile(D), _tile(PROJ), _const((1, D))],
        out_shape=[SDS((S, D), f32), SDS((S, PROJ), bf16), SDS((1, D), f32)],
        compiler_params=_CP, name="mix_in_bwd")(dxo, x, g, wint, *tabs, dvp, *d1, *d4, *d16, *after)


def _pool_sums(pad_ref, base, rows, adjoint):
    lane_group = lax.broadcasted_iota(jnp.int32, (rows, PW), 1) // 64
    sign = -1 if adjoint else 1

    def sh(o):
        return pad_ref[pl.ds(PAD + base + sign * o, rows), :]

    out = None
    acc = None
    lo, hi = 0, 0
    for gi, w in enumerate(POOL_WINDOWS):
        for o in list(range(-(w // 2), lo)) + list(range(hi, w - w // 2)):
            acc = sh(o) if acc is None else acc + sh(o)
        lo, hi = -(w // 2), w - w // 2
        out = acc if out is None else jnp.where(lane_group >= gi, acc, out)
    return out


def _pool_counts(base, rows):
    pos = base + lax.broadcasted_iota(jnp.int32, (rows, PW), 0)
    lane_group = lax.broadcasted_iota(jnp.int32, (rows, PW), 1) // 64
    cnt = None
    for gi, w in enumerate(POOL_WINDOWS):
        lo = jnp.maximum(pos - w // 2, 0)
        hi = jnp.minimum(pos + w - 1 - w // 2, S - 1)
        c = (hi - lo + 1).astype(f32)
        cnt = c if cnt is None else jnp.where(lane_group >= gi, c, cnt)
    return cnt


def _pool_fwd(vp, wbd, scale):
    ch = 256

    def body(vp_ref, w_ref, sc_ref, y_ref, diff_ref, pad):
        pad[pl.ds(0, PAD), :] = jnp.zeros((PAD, PW), f32)
        pad[pl.ds(PAD + S, PAD), :] = jnp.zeros((PAD, PW), f32)
        pad[pl.ds(PAD, S), :] = vp_ref[...]
        for b in range(S // ch):
            base = b * ch
            pooled = _pool_sums(pad, base, ch, False) / _pool_counts(base, ch)
            diff = (pooled - vp_ref[pl.ds(base, ch), :]).astype(bf16)
            diff_ref[pl.ds(base, ch), :] = diff
            y_ref[pl.ds(base, ch), :] = _dot_nn(diff, w_ref[...]) * sc_ref[...]

    whole = lambda shape: pl.BlockSpec(shape, lambda i: (0,) * len(shape))
    return pl.pallas_call(
        body, grid=(1,),
        in_specs=[whole((S, PW)), whole((PW, PW)), whole((1, PW))],
        out_specs=[whole((S, PW)), whole((S, PW))],
        out_shape=[SDS((S, PW), f32), SDS((S, PW), bf16)],
        scratch_shapes=[pltpu.VMEM((S + 2 * PAD, PW), f32)],
        compiler_params=_CP, name="pool_fwd")(vp, wbd, scale)


def _pool_bwd(dy, diff, wbd, scale, after=()):
    ch = 256

    def body(dy_ref, diff_ref, w_ref, sc_ref, dvp_ref, dw_ref, dsc_ref, pad):
        pad[pl.ds(0, PAD), :] = jnp.zeros((PAD, PW), f32)
        pad[pl.ds(PAD + S, PAD), :] = jnp.zeros((PAD, PW), f32)
        dw = jnp.zeros((PW, PW), f32)
        dsc = jnp.zeros((1, PW), f32)
        for b in range(S // ch):
            base = b * ch
            dy = dy_ref[pl.ds(base, ch), :]
            diff = diff_ref[pl.ds(base, ch), :]
            dsc = dsc + jnp.sum(dy * _dot_nn(diff, w_ref[...]), axis=0, keepdims=True)
            dz = (dy * sc_ref[...]).astype(bf16)
            dw = dw + _dot_tn(diff, dz)
            ddiff = _dot_nt(dz, w_ref[...])
            dvp_ref[pl.ds(base, ch), :] = -ddiff
            pad[pl.ds(PAD + base, ch), :] = ddiff / _pool_counts(base, ch)
        for gi in range(4):
            dw_ref[gi] = dw[64 * gi:64 * (gi + 1), 64 * gi:64 * (gi + 1)]
        dsc_ref[...] = dsc
        for b in range(S // ch):
            base = b * ch
            dvp_ref[pl.ds(base, ch), :] += _pool_sums(pad, base, ch, True)

    whole = lambda shape: pl.BlockSpec(shape, lambda i: (0,) * len(shape))
    return pl.pallas_call(
        _follow(body, 4, after), grid=(1,),
        in_specs=[whole((S, PW)), whole((S, PW)), whole((PW, PW)), whole((1, PW))] + [_ANY] * len(after),
        out_specs=[whole((S, PW)), whole((4, 64, 64)), whole((1, PW))],
        out_shape=[SDS((S, PW), f32), SDS((4, 64, 64), f32), SDS((1, PW), f32)],
        scratch_shapes=[pltpu.VMEM((S + 2 * PAD, PW), f32)],
        compiler_params=_CP, name="pool_bwd")(dy, diff, wbd, scale, *after)


def _attn_blocks(lc):
    bpc = lc // QB
    kw = min(2 * QB, lc)
    blocks = []
    for b in range(S // QB):
        t0 = (b % bpc) * QB
        ks_in = min(max(t0 - HALF, 0), lc - kw)
        blocks.append((b * QB, (b // bpc) * lc + ks_in, t0 - ks_in))
    return kw, blocks


def _attn_bias(bias_ref, kw, shifts):
    r = lax.broadcasted_iota(jnp.int32, (2 * QB, kw), 0) % QB
    c = lax.broadcasted_iota(jnp.int32, (2 * QB, kw), 1)
    for i, shift in enumerate(shifts):
        bias_ref[i] = jnp.where(jnp.abs(r + shift - c) <= HALF, 0.0, MASK_VALUE).astype(f32)


def _head_put(stats, pair, v0, v1, lane):
    return jnp.where(lane == 2 * pair, v0, jnp.where(lane == 2 * pair + 1, v1, stats))


def _head_cols(stats, pair, lane):
    c0 = jnp.sum(jnp.where(lane == 2 * pair, stats, 0.0), axis=-1, keepdims=True)
    c1 = jnp.sum(jnp.where(lane == 2 * pair + 1, stats, 0.0), axis=-1, keepdims=True)
    return jnp.concatenate([c0, c1], axis=0)


def _head_spread(stats, pair, head0):
    return jnp.where(head0, stats[:, 2 * pair:2 * pair + 1], stats[:, 2 * pair + 1:2 * pair + 2])


def _stack_heads(blk, head0):
    zero = jnp.zeros_like(blk)
    return jnp.concatenate([jnp.where(head0, blk, zero), jnp.where(head0, zero, blk)], axis=0)


def _attn_fwd(q, k, v, lc, after=None):
    kw, blocks = _attn_blocks(lc)
    shifts = sorted({b[2] for b in blocks})

    def body(q_ref, k_ref, v_ref, *refs):
        o_ref, lse_ref, bias_ref = refs[-3:]
        lane = lax.broadcasted_iota(jnp.int32, (QB, 128), 1)
        head0 = lane < 64
        pair = pl.program_id(0)
        _attn_bias(bias_ref, kw, shifts)

        @pl.when(pair == 0)
        def _():
            lse_ref[...] = jnp.zeros_like(lse_ref)

        for row0, kstart, shift in blocks:
            q2 = _stack_heads(q_ref[pl.ds(row0, QB), :], head0)
            kb = k_ref[pl.ds(kstart, kw), :]
            vb = v_ref[pl.ds(kstart, kw), :]
            s = _dot_nt(q2, kb) + bias_ref[shifts.index(shift)]
            m = jnp.max(s, axis=-1, keepdims=True)
            p = jnp.exp(s - m)
            den = jnp.sum(p, axis=-1, keepdims=True)
            o2 = _dot_nn(p.astype(bf16), vb) / den
            lse2 = m + jnp.log(den)
            o_ref[pl.ds(row0, QB), :] = jnp.where(head0, o2[:QB], o2[QB:]).astype(bf16)
            lse_ref[pl.ds(row0, QB), :] = _head_put(lse_ref[pl.ds(row0, QB), :], pair, lse2[:QB], lse2[QB:], lane)

    col = pl.BlockSpec((S, 128), lambda p: (0, p))
    extra = () if after is None else (after,)
    return pl.pallas_call(
        body, grid=(NG,), in_specs=[col, col, col] + [_ANY] * len(extra),
        out_specs=[col, pl.BlockSpec((S, 128), lambda p: (0, 0))],
        out_shape=[SDS((S, AW), bf16), SDS((S, 128), f32)],
        scratch_shapes=[pltpu.VMEM((len(shifts), 2 * QB, kw), f32)],
        compiler_params=_CP, name=f"attn_fwd_{lc}")(q, k, v, *extra)


def _attn_bwd(q, k, v, do, lse, delta, lc):
    kw, blocks = _attn_blocks(lc)
    shifts = sorted({b[2] for b in blocks})

    def body(q_ref, k_ref, v_ref, do_ref, lse_ref, dl_ref, dq_ref, dk_out, dv_out, bias_ref, dk_ref, dv_ref):
        lane = lax.broadcasted_iota(jnp.int32, (QB, 128), 1)
        head0 = lane < 64
        pair = pl.program_id(0)
        _attn_bias(bias_ref, kw, shifts)
        dk_ref[...] = jnp.zeros_like(dk_ref)
        dv_ref[...] = jnp.zeros_like(dv_ref)
        for row0, kstart, shift in blocks:
            q2 = _stack_heads(q_ref[pl.ds(row0, QB), :], head0)
            do2 = _stack_heads(do_ref[pl.ds(row0, QB), :], head0)
            lse2 = _head_cols(lse_ref[pl.ds(row0, QB), :], pair, lane)
            dl2 = _head_cols(dl_ref[pl.ds(row0, QB), :], pair, lane)
            kb = k_ref[pl.ds(kstart, kw), :]
            vb = v_ref[pl.ds(kstart, kw), :]
            p = jnp.exp(_dot_nt(q2, kb) + bias_ref[shifts.index(shift)] - lse2)
            ds = (p * (_dot_nt(do2, vb) - dl2)).astype(bf16)
            dq2 = _dot_nn(ds, kb)
            dq_ref[pl.ds(row0, QB), :] = jnp.where(head0, dq2[:QB], dq2[QB:]).astype(bf16)
            dk_ref[pl.ds(kstart, kw), :] += _dot_tn(ds, q2)
            dv_ref[pl.ds(kstart, kw), :] += _dot_tn(p.astype(bf16), do2)
        dk_out[...] = dk_ref[...].astype(bf16)
        dv_out[...] = dv_ref[...].astype(bf16)

    col = pl.BlockSpec((S, 128), lambda p: (0, p))
    stats = pl.BlockSpec((S, 128), lambda p: (0, 0))
    return pl.pallas_call(
        body, grid=(NG,), in_specs=[col] * 4 + [stats] * 2, out_specs=[col] * 3,
        out_shape=[SDS((S, AW), bf16)] * 3,
        scratch_shapes=[pltpu.VMEM((len(shifts), 2 * QB, kw), f32), pltpu.VMEM((S, 128), f32),
                        pltpu.VMEM((S, 128), f32)],
        compiler_params=_CP, name=f"attn_bwd_{lc}")(q, k, v, do, lse, delta)


def _mix_out_fwd(x, ypool, o1, l1, o4, l4, o16, l16, wout):
    def body(x_ref, yp_ref, o1_ref, l1_ref, o4_ref, l4_ref, o16_ref, l16_ref, w_ref,
             xo_ref, mixed_ref, o_ref, lse1_ref, lse4_ref, lse16_ref, sl4, sl16, sl):
        head0 = lax.broadcasted_iota(jnp.int32, (TM, 128), 1) < 64
        for r in range(4):
            sl4[pl.ds(r, TM // 4, stride=4), :] = l4_ref[r]
        for r in range(16):
            sl16[pl.ds(r, TM // 16, stride=16), :] = l16_ref[r]
        n4 = _dot_nn(_dilation_perm(4, True), o4_ref[...].reshape(TM, AW))
        n16 = _dot_nn(_dilation_perm(16, True), o16_ref[...].reshape(TM, AW))
        a, b, c = l1_ref[...], sl4[...], sl16[...]
        m = jnp.maximum(jnp.maximum(a, b), c)
        wa, wb, wc = jnp.exp(a - m), jnp.exp(b - m), jnp.exp(c - m)
        den = wa + wb + wc
        wa, wb, wc = wa / den, wb / den, wc / den
        lse = m + jnp.log(den)
        lse1_ref[...] = lse
        sl[...] = lse
        mixed_ref[:, :PW] = yp_ref[...].astype(bf16)
        for j in range(NG):
            y = (_head_spread(wa, j, head0) * o1_ref[:, _cols(j)].astype(f32)
                 + _head_spread(wb, j, head0) * n4[:, _cols(j)] + _head_spread(wc, j, head0) * n16[:, _cols(j)])
            o_ref[:, _cols(j)] = y
            mixed_ref[:, PW + 128 * j: PW + 128 * (j + 1)] = y.astype(bf16)
        for r in range(4):
            lse4_ref[r] = sl[pl.ds(r, TM // 4, stride=4), :]
        for r in range(16):
            lse16_ref[r] = sl[pl.ds(r, TM // 16, stride=16), :]
        xo_ref[...] = x_ref[...] + _dot_nn(mixed_ref[...], w_ref[...])

    return pl.pallas_call(
        body, grid=(S // TM,),
        in_specs=[_tile(D), _tile(PW), _tile(AW), _tile(128), _p4(), _p4(128), _p16(), _p16(128), _layer(D, D)],
        out_specs=[_tile(D), _tile(D), _tile(AW), _tile(128), _p4(128), _p16(128)],
        out_shape=[SDS((S, D), f32), SDS((S, D), bf16), SDS((S, AW), f32), SDS((S, 128), f32),
                   SDS((4, S // 4, 128), f32), SDS((16, S // 16, 128), f32)],
        scratch_shapes=[pltpu.VMEM((TM, 128), f32)] * 3,
        compiler_params=_CP, name="mix_out_fwd")(x, ypool, o1, l1, o4, l4, o16, l16, wout)


def _mix_out_bwd(dxo, o, wout):
    def body(dxo_ref, o_ref, w_ref, dxb_ref, dyp_ref, do1, do4, do16, dl1, dl4, dl16, sdl):
        dxb = dxo_ref[...].astype(bf16)
        dxb_ref[...] = dxb
        dm = _dot_nt(dxb, w_ref[...])
        dyp_ref[...] = dm[:, :PW]
        lane = lax.broadcasted_iota(jnp.int32, (TM, 128), 1)
        head0 = lane < 64
        dl = jnp.zeros((TM, 128), f32)
        for j in range(NG):
            d = dm[:, PW + 128 * j: PW + 128 * (j + 1)]
            prod = d * o_ref[:, _cols(j)]
            dl = _head_put(dl, j, jnp.sum(jnp.where(head0, prod, 0.0), axis=-1, keepdims=True),
                           jnp.sum(jnp.where(head0, 0.0, prod), axis=-1, keepdims=True), lane)
            do1[:, _cols(j)] = d.astype(bf16)
        dl1[...] = dl
        sdl[...] = dl
        for r in range(4):
            dl4[r] = sdl[pl.ds(r, TM // 4, stride=4), :]
        for r in range(16):
            dl16[r] = sdl[pl.ds(r, TM // 16, stride=16), :]
        nat = do1[...]
        do4[...] = _dot_nn(_dilation_perm(4), nat).astype(bf16).reshape(4, TM // 4, AW)
        do16[...] = _dot_nn(_dilation_perm(16), nat).astype(bf16).reshape(16, TM // 16, AW)

    return pl.pallas_call(
        body, grid=(S // TM,),
        in_specs=[_tile(D), _tile(AW), _layer(D, D)],
        out_specs=[_tile(D), _tile(PW), _tile(AW), _p4(), _p16(), _tile(128), _p4(128), _p16(128)],
        out_shape=[SDS((S, D), bf16), SDS((S, PW), f32),
                   SDS((S, AW), bf16), SDS((4, S // 4, AW), bf16), SDS((16, S // 16, AW), bf16),
                   SDS((S, 128), f32), SDS((4, S // 4, 128), f32), SDS((16, S // 16, 128), f32)],
        scratch_shapes=[pltpu.VMEM((TM, 128), f32)],
        compiler_params=_CP, name="mix_out_bwd")(dxo, o, wout)


def _loss_head(x, g, target):
    def body(x_ref, g_ref, t_ref, dx_ref, loss_ref, dg_ref):
        g = g_ref[...]
        r, xh, y = _rms(x_ref[...], g)
        err = y - t_ref[...]
        dy = err * (1.0 / D)

        @pl.when(pl.program_id(0) == 0)
        def _():
            loss_ref[...] = jnp.zeros_like(loss_ref)
            dg_ref[...] = jnp.zeros_like(dg_ref)

        loss_ref[...] += jnp.broadcast_to(0.5 * jnp.sum(jnp.mean(err * err, axis=-1, keepdims=True)), (1, D))
        dg_ref[...] += jnp.sum(dy * xh, axis=0, keepdims=True)
        dx_ref[...] = _rms_bwd(dy, r, xh, g)

    return pl.pallas_call(
        body, grid=(S // TM,),
        in_specs=[_tile(D), _const((1, D)), _tile(D)],
        out_specs=[_tile(D), _const((1, D)), _const((1, D))],
        out_shape=[SDS((S, D), f32), SDS((1, D), f32), SDS((1, D), f32)],
        compiler_params=_CP, name="loss_head")(x, g, target)


def _peer(k):
    x, y, c = lax.axis_index("x"), lax.axis_index("y"), lax.axis_index("c")
    px = 1 - x if k & 4 else x
    py = 1 - y if k & 2 else y
    pc = 1 - c if k & 1 else c
    return (px, py, pc), 4 * px + 2 * py + pc


def _diag_route():
    x, y, c = lax.axis_index("x"), lax.axis_index("y"), lax.axis_index("c")
    idx_x, idx_y = _peer(4)[1], _peer(2)[1]
    return idx_x + c * (idx_y - idx_x), (x + c * (1 - 2 * x), (1 - y) + c * (2 * y - 1), c)


def _hbm(a):
    return pltpu.with_memory_space_constraint(a, pltpu.HBM)


def _rows(ref, idx):
    r = ref.shape[0] // NDEV
    return ref.at[pl.ds(idx * r, r), :]


def _row_copy(ref, idx, send_sem, recv_sem, to):
    return pltpu.make_async_remote_copy(src_ref=_rows(ref, idx), dst_ref=_rows(ref, idx), send_sem=send_sem,
                                        recv_sem=recv_sem, device_id=to, device_id_type=_MESH)


def _place_own(me, shards, l):
    n = len(shards)

    def body(me_ref, *refs):
        for t in range(n):
            refs[n + t][...] = refs[t][...].astype(bf16)

    grid_spec = pltpu.PrefetchScalarGridSpec(
        num_scalar_prefetch=1, grid=(1,),
        in_specs=[pl.BlockSpec((None, s.shape[1], D), lambda i, me_ref: (l, 0, 0)) for s in shards],
        out_specs=[pl.BlockSpec((s.shape[1], D), lambda i, me_ref: (me_ref[0], 0)) for s in shards])
    return pl.pallas_call(
        body, grid_spec=grid_spec, out_shape=[SDS((NDEV * s.shape[1], D), bf16) for s in shards],
        compiler_params=_CP, name="place_own")(me, *shards)


_TOKEN = SDS((8, 128), f32)
def _ag_start(lands, after, l):
    n = len(lands)
    after = list(after) if isinstance(after, (list, tuple)) else [after]

    def body(*refs):
        zones, send_sems, recv_sems, token = refs[:n], refs[n + len(after)], refs[n + len(after) + 1], refs[-1]
        _, me_idx = _peer(0)
        for k, mask in enumerate((1, 4, 2)):
            for t in range(n):
                _row_copy(zones[t], me_idx, send_sems.at[k * n + t], recv_sems.at[k * n + t], _peer(mask)[0]).start()
        token[...] = jnp.zeros_like(token)

    outs = pl.pallas_call(
        body, name=f"ag_start_{l}", in_specs=[_HBM] * n + [_ANY] * len(after),
        out_specs=(_SEM, _SEM, *[_HBM] * n, pl.BlockSpec(memory_space=pltpu.VMEM)),
        out_shape=(pltpu.SemaphoreType.DMA((3 * n,)), pltpu.SemaphoreType.DMA((3 * n,)),
                   *[pltpu.HBM(a.shape, a.dtype) for a in lands], _TOKEN),
        input_output_aliases={t: 2 + t for t in range(n)}, compiler_params=_CP_SPLIT)(
            *[_hbm(a) for a in lands], *after)
    return outs[0], outs[1], list(outs[2:2 + n]), outs[-1]


def _ag_pass(lands, recv_sems, after, l):
    n = len(lands)
    after = list(after) if isinstance(after, (list, tuple)) else [after]

    def body(*refs):
        zones, recv_sems = refs[:n], refs[n]
        psend, precv, token = refs[n + 1 + len(after)], refs[n + 2 + len(after)], refs[-1]
        me, _ = _peer(0)
        sibling, _ = _peer(1)
        for j, mask in enumerate((4, 2)):
            idx = _peer(mask)[1]
            for t in range(n):
                _row_copy(zones[t], idx, psend.at[j * n + t], recv_sems.at[(1 + j) * n + t], me).wait_recv()
                _row_copy(zones[t], idx, psend.at[j * n + t], precv.at[j * n + t], sibling).start()
        fwd_idx, fwd_dev = _diag_route()
        for t in range(n):
            _row_copy(zones[t], fwd_idx, psend.at[2 * n + t], precv.at[2 * n + t], fwd_dev).start()
        token[...] = jnp.zeros_like(token)

    outs = pl.pallas_call(
        body, name=f"ag_pass_{l}", in_specs=[_HBM] * n + [_SEM] + [_ANY] * len(after),
        out_specs=(_SEM, _SEM, *[_HBM] * n, pl.BlockSpec(memory_space=pltpu.VMEM)),
        out_shape=(pltpu.SemaphoreType.DMA((3 * n,)), pltpu.SemaphoreType.DMA((3 * n,)),
                   *[pltpu.HBM(a.shape, a.dtype) for a in lands], _TOKEN),
        input_output_aliases={t: 2 + t for t in range(n)}, compiler_params=_CP_SPLIT)(*lands, recv_sems, *after)
    return outs[0], outs[1], list(outs[2:2 + n]), outs[-1]


def _ag_last(lands, precv, after, l):
    n = len(lands)
    after = list(after) if isinstance(after, (list, tuple)) else [after]

    def body(*refs):
        zones, precv = refs[:n], refs[n]
        qsend, qrecv, token = refs[n + 1 + len(after)], refs[n + 2 + len(after)], refs[-1]
        me, _ = _peer(0)
        sibling, _ = _peer(1)
        idx = _peer(6)[1]
        for t in range(n):
            _row_copy(zones[t], idx, qsend.at[t], precv.at[2 * n + t], me).wait_recv()
            _row_copy(zones[t], idx, qsend.at[t], qrecv.at[t], sibling).start()
        token[...] = jnp.zeros_like(token)

    outs = pl.pallas_call(
        body, name=f"ag_last_{l}", in_specs=[_HBM] * n + [_SEM] + [_ANY] * len(after),
        out_specs=(_SEM, _SEM, *[_HBM] * n, pl.BlockSpec(memory_space=pltpu.VMEM)),
        out_shape=(pltpu.SemaphoreType.DMA((n,)), pltpu.SemaphoreType.DMA((n,)),
                   *[pltpu.HBM(a.shape, a.dtype) for a in lands], _TOKEN),
        input_output_aliases={t: 2 + t for t in range(n)}, compiler_params=_CP_SPLIT)(*lands, precv, *after)
    return outs[0], outs[1], list(outs[2:2 + n]), outs[-1]


def _ag_wait(lands, send_sems, recv_sems, psend, precv, qsend, qrecv, after, l):
    n = len(lands)
    after = list(after) if isinstance(after, (list, tuple)) else [after]

    def body(*refs):
        zones = refs[:n]
        send_sems, recv_sems, psend, precv, qsend, qrecv = refs[n:n + 6]
        me, me_idx = _peer(0)
        for k in range(3):
            for t in range(n):
                _row_copy(zones[t], me_idx, send_sems.at[k * n + t], recv_sems.at[k * n + t], me).wait_send()
        for t in range(n):
            _row_copy(zones[t], _peer(1)[1], send_sems.at[t], recv_sems.at[t], me).wait_recv()
        fwd_idx, _ = _diag_route()
        for j, (mine, theirs) in enumerate(((_peer(4)[1], _peer(5)[1]), (_peer(2)[1], _peer(3)[1]))):
            for t in range(n):
                _row_copy(zones[t], mine, psend.at[j * n + t], precv.at[j * n + t], me).wait_send()
                _row_copy(zones[t], theirs, psend.at[j * n + t], precv.at[j * n + t], me).wait_recv()
        for t in range(n):
            _row_copy(zones[t], fwd_idx, psend.at[2 * n + t], precv.at[2 * n + t], me).wait_send()
            _row_copy(zones[t], _peer(6)[1], qsend.at[t], qrecv.at[t], me).wait_send()
            _row_copy(zones[t], _peer(7)[1], qsend.at[t], qrecv.at[t], me).wait_recv()

    outs = pl.pallas_call(
        body, name=f"ag_wait_{l}", in_specs=[_HBM] * n + [_SEM] * 6 + [_ANY] * len(after),
        out_specs=tuple([_HBM] * n), out_shape=tuple(pltpu.HBM(a.shape, a.dtype) for a in lands),
        input_output_aliases={t: t for t in range(n)}, compiler_params=_CP_SPLIT)(
            *lands, send_sems, recv_sems, psend, precv, qsend, qrecv, *after)
    return list(outs)


def _xchg_src(ref, slot_ref, idx):
    return _rows(ref, idx) if ref.shape[0] == NDEV * slot_ref.shape[1] else ref


def _rs_start(srcs, slots, after, tag):
    n = len(srcs)
    after = list(after) if isinstance(after, (list, tuple)) else [after]

    def body(*refs):
        src, slot = refs[:n], refs[n:2 * n]
        send_sems, recv_sems, token = refs[2 * n + len(after)], refs[2 * n + len(after) + 1], refs[-1]
        _, me_idx = _peer(0)
        for k in range(1, NDEV):
            dev, idx = _peer(k)
            for t in range(n):
                pltpu.make_async_remote_copy(
                    src_ref=_xchg_src(src[t], slot[t], idx), dst_ref=slot[t].at[me_idx],
                    send_sem=send_sems.at[(k - 1) * n + t], recv_sem=recv_sems.at[(k - 1) * n + t],
                    device_id=dev, device_id_type=_MESH).start()
        token[...] = jnp.zeros_like(token)

    outs = pl.pallas_call(
        body, name=f"rs_start_{tag}", in_specs=[_HBM] * (2 * n) + [_ANY] * len(after),
        out_specs=(_SEM, _SEM, *[_HBM] * (2 * n), pl.BlockSpec(memory_space=pltpu.VMEM)),
        out_shape=(pltpu.SemaphoreType.DMA(((NDEV - 1) * n,)), pltpu.SemaphoreType.DMA(((NDEV - 1) * n,)),
                   *[pltpu.HBM(a.shape, a.dtype) for a in list(srcs) + list(slots)], _TOKEN),
        input_output_aliases={t: 2 + t for t in range(2 * n)}, compiler_params=_CP_SPLIT)(
            *[_hbm(a) for a in list(srcs) + list(slots)], *after)
    return outs[0], outs[1], list(outs[2:2 + n]), list(outs[2 + n:2 + 2 * n]), outs[-1]


def _rs_wait(srcs, slots, send_sems, recv_sems, after, tag):
    n = len(srcs)
    after = list(after) if isinstance(after, (list, tuple)) else [after]

    def body(*refs):
        src, slot, send_sems, recv_sems = refs[:n], refs[n:2 * n], refs[2 * n], refs[2 * n + 1]
        me, _ = _peer(0)
        for k in range(1, NDEV):
            idx = _peer(k)[1]
            for t in range(n):
                cp = pltpu.make_async_remote_copy(
                    src_ref=_xchg_src(src[t], slot[t], idx), dst_ref=slot[t].at[idx],
                    send_sem=send_sems.at[(k - 1) * n + t], recv_sem=recv_sems.at[(k - 1) * n + t],
                    device_id=me, device_id_type=_MESH)
                cp.wait_send()
                cp.wait_recv()

    outs = pl.pallas_call(
        body, name=f"rs_wait_{tag}", in_specs=[_HBM] * (2 * n) + [_SEM, _SEM] + [_ANY] * len(after),
        out_specs=tuple([_HBM] * (2 * n)),
        out_shape=tuple(pltpu.HBM(a.shape, a.dtype) for a in list(srcs) + list(slots)),
        input_output_aliases={t: t for t in range(2 * n)}, compiler_params=_CP_SPLIT)(
            *srcs, *slots, send_sems, recv_sems, *after)
    return list(outs[:n]), list(outs[n:])


def _pair_start(full4s, bufs, after, tag):
    n = len(full4s)
    after = list(after) if isinstance(after, (list, tuple)) else [after]

    def body(*refs):
        full, buf = refs[:n], refs[n:2 * n]
        send_sems, recv_sems, token = refs[2 * n + len(after)], refs[2 * n + len(after) + 1], refs[-1]
        c = lax.axis_index("c")
        for t in range(n):
            pltpu.make_async_remote_copy(src_ref=full[t].at[:, 1 - c], dst_ref=buf[t], send_sem=send_sems.at[t],
                                         recv_sem=recv_sems.at[t], device_id=_peer(1)[0], device_id_type=_MESH).start()
        token[...] = jnp.zeros_like(token)

    outs = pl.pallas_call(
        body, name=f"pair_start_{tag}", in_specs=[_HBM] * (2 * n) + [_ANY] * len(after),
        out_specs=(_SEM, _SEM, *[_HBM] * (2 * n), pl.BlockSpec(memory_space=pltpu.VMEM)),
        out_shape=(pltpu.SemaphoreType.DMA((n,)), pltpu.SemaphoreType.DMA((n,)),
                   *[pltpu.HBM(a.shape, a.dtype) for a in list(full4s) + list(bufs)], _TOKEN),
        input_output_aliases={t: 2 + t for t in range(2 * n)}, compiler_params=_CP_SPLIT)(
            *[_hbm(a) for a in list(full4s) + list(bufs)], *after)
    return outs[0], outs[1], list(outs[2:2 + n]), list(outs[2 + n:2 + 2 * n]), outs[-1]


def _pair_wait(full4s, bufs, send_sems, recv_sems, after, tag):
    n = len(full4s)
    after = list(after) if isinstance(after, (list, tuple)) else [after]

    def body(*refs):
        full, buf, send_sems, recv_sems = refs[:n], refs[n:2 * n], refs[2 * n], refs[2 * n + 1]
        c = lax.axis_index("c")
        for t in range(n):
            cp = pltpu.make_async_remote_copy(src_ref=full[t].at[:, 1 - c], dst_ref=buf[t], send_sem=send_sems.at[t],
                                              recv_sem=recv_sems.at[t], device_id=_peer(0)[0], device_id_type=_MESH)
            cp.wait_send()
            cp.wait_recv()

    outs = pl.pallas_call(
        body, name=f"pair_wait_{tag}", in_specs=[_HBM] * (2 * n) + [_SEM, _SEM] + [_ANY] * len(after),
        out_specs=tuple([_HBM] * (2 * n)),
        out_shape=tuple(pltpu.HBM(a.shape, a.dtype) for a in list(full4s) + list(bufs)),
        input_output_aliases={t: t for t in range(2 * n)}, compiler_params=_CP_SPLIT)(
            *full4s, *bufs, send_sems, recv_sems, *after)
    return list(outs[:n]), list(outs[n:])


def _pair_sum(core, full4s, bufs):
    n = len(full4s)

    def body(core_ref, *refs):
        for t in range(n):
            refs[2 * n + t][...] = (refs[t][...].astype(f32) + refs[n + t][...].astype(f32)).astype(bf16)

    grid_spec = pltpu.PrefetchScalarGridSpec(
        num_scalar_prefetch=1, grid=(4,),
        in_specs=[pl.BlockSpec((None, None) + a.shape[2:], lambda j, core_ref: (j, core_ref[0], 0, 0)) for a in full4s]
        + [pl.BlockSpec((None,) + b.shape[1:], lambda j, core_ref: (j, 0, 0)) for b in bufs],
        out_specs=[pl.BlockSpec((None,) + b.shape[1:], lambda j, core_ref: (j, 0, 0)) for b in bufs])
    return pl.pallas_call(
        body, grid_spec=grid_spec, out_shape=[SDS(b.shape, bf16) for b in bufs],
        compiler_params=_CP, name="pair_sum")(core, *full4s, *bufs)


def _chip_start(sums, slots, after, tag):
    n = len(sums)
    after = list(after) if isinstance(after, (list, tuple)) else [after]

    def body(*refs):
        src, slot = refs[:n], refs[n:2 * n]
        send_sems, recv_sems, token = refs[2 * n + len(after)], refs[2 * n + len(after) + 1], refs[-1]
        my_chip = 2 * lax.axis_index("x") + lax.axis_index("y")
        for k, mask in enumerate((4, 2, 6)):
            dev, _ = _peer(mask)
            for t in range(n):
                pltpu.make_async_remote_copy(
                    src_ref=src[t].at[2 * dev[0] + dev[1]], dst_ref=slot[t].at[my_chip],
                    send_sem=send_sems.at[k * n + t], recv_sem=recv_sems.at[k * n + t],
                    device_id=dev, device_id_type=_MESH).start()
        token[...] = jnp.zeros_like(token)

    outs = pl.pallas_call(
        body, name=f"chip_start_{tag}", in_specs=[_HBM] * (2 * n) + [_ANY] * len(after),
        out_specs=(_SEM, _SEM, *[_HBM] * (2 * n), pl.BlockSpec(memory_space=pltpu.VMEM)),
        out_shape=(pltpu.SemaphoreType.DMA((3 * n,)), pltpu.SemaphoreType.DMA((3 * n,)),
                   *[pltpu.HBM(a.shape, a.dtype) for a in list(sums) + list(slots)], _TOKEN),
        input_output_aliases={t: 2 + t for t in range(2 * n)}, compiler_params=_CP_SPLIT)(
            *[_hbm(a) for a in list(sums) + list(slots)], *after)
    return outs[0], outs[1], list(outs[2:2 + n]), list(outs[2 + n:2 + 2 * n]), outs[-1]


def _chip_wait(sums, slots, send_sems, recv_sems, after, tag):
    n = len(sums)
    after = list(after) if isinstance(after, (list, tuple)) else [after]

    def body(*refs):
        src, slot, send_sems, recv_sems = refs[:n], refs[n:2 * n], refs[2 * n], refs[2 * n + 1]
        for k, mask in enumerate((4, 2, 6)):
            dev, _ = _peer(mask)
            chip = 2 * dev[0] + dev[1]
            for t in range(n):
                cp = pltpu.make_async_remote_copy(
                    src_ref=src[t].at[chip], dst_ref=slot[t].at[chip],
                    send_sem=send_sems.at[k * n + t], recv_sem=recv_sems.at[k * n + t],
                    device_id=_peer(0)[0], device_id_type=_MESH)
                cp.wait_send()
                cp.wait_recv()

    outs = pl.pallas_call(
        body, name=f"chip_wait_{tag}", in_specs=[_HBM] * (2 * n) + [_SEM, _SEM] + [_ANY] * len(after),
        out_specs=tuple([_HBM] * (2 * n)),
        out_shape=tuple(pltpu.HBM(a.shape, a.dtype) for a in list(sums) + list(slots)),
        input_output_aliases={t: t for t in range(2 * n)}, compiler_params=_CP_SPLIT)(
            *sums, *slots, send_sems, recv_sems, *after)
    return list(outs[:n]), list(outs[n:])


def _sum_slots(slots, rb):
    r = slots.shape[1]

    def body(s_ref, o_ref):
        acc = s_ref[0].astype(f32)
        for s in range(1, NDEV):
            acc = acc + s_ref[s].astype(f32)
        o_ref[...] = acc

    return pl.pallas_call(
        body, grid=(r // rb,),
        in_specs=[pl.BlockSpec((NDEV, rb, D), lambda i: (0, i, 0))],
        out_specs=pl.BlockSpec((rb, D), lambda i: (i, 0)),
        out_shape=SDS((r, D), f32), compiler_params=_CP, name="sum_slots")(slots)


def _adamw(w, g, m, v):
    shape = w.shape
    cols = shape[-1]
    rows = w.size // cols
    rb = rows
    for cand in (512, 256, 128, 64, 32, 16, 8):
        if rows % cand == 0 and rows > cand:
            rb = cand
            break

    def body(w_ref, g_ref, m_ref, v_ref, d_ref, mo_ref, vo_ref):
        d_ref[...], mo_ref[...], vo_ref[...] = _adamw_math(w_ref[...], g_ref[...], m_ref[...], v_ref[...])

    spec = pl.BlockSpec((rb, cols), lambda i: (i, 0))
    outs = pl.pallas_call(
        body, grid=(rows // rb,), in_specs=[spec] * 4, out_specs=[spec] * 3,
        out_shape=[SDS((rows, cols), f32)] * 3, compiler_params=_CP, name="adamw")(
            *(a.reshape(rows, cols) for a in (w, g, m, v)))
    return tuple(o.reshape(shape) for o in outs)


def _adamw_math(w, g, m, v):
    m = ADAM_B1 * m + (1.0 - ADAM_B1) * g
    v = ADAM_B2 * v + (1.0 - ADAM_B2) * (g * g)
    m_hat = m / (1.0 - ADAM_B1 ** ADAM_STEP)
    v_hat = v / (1.0 - ADAM_B2 ** ADAM_STEP)
    return -ADAM_LR * (m_hat / (jnp.sqrt(v_hat) + ADAM_EPS) + ADAM_WD * w), m, v


def _reduce_adamw(acc, me, full, slots, w, m, v, l):
    _, r, _ = w.shape
    ns = slots.shape[0]
    rb = r // 2 if r > 128 else r

    def body(me_ref, full_ref, slots_ref, w_ref, m_ref, v_ref, *refs):
        go_ref, d_ref, mo_ref, vo_ref = refs[-4:]
        own = full_ref[...].astype(f32)
        g = None
        for s in range(ns):
            part = jnp.where(me_ref[0] == s, own, slots_ref[s].astype(f32))
            g = part if g is None else g + part
        go_ref[...] = g
        d_ref[...], mo_ref[...], vo_ref[...] = _adamw_math(w_ref[...], g, m_ref[...], v_ref[...])

    steps = r // rb
    lay = pl.BlockSpec((None, rb, D), lambda i, me_ref: (l, i, 0))
    n_acc = 0 if acc is None else 4
    grid_spec = pltpu.PrefetchScalarGridSpec(
        num_scalar_prefetch=1, grid=(steps,),
        in_specs=[pl.BlockSpec((rb, D), lambda i, me_ref: (me_ref[0] * steps + i, 0)),
                  pl.BlockSpec((ns, rb, D), lambda i, me_ref: (0, i, 0)), lay, lay, lay] + [_ANY] * n_acc,
        out_specs=[lay] * 4)
    outs = pl.pallas_call(
        body, grid_spec=grid_spec, out_shape=[SDS(w.shape, f32)] * 4,
        input_output_aliases={6 + j: j for j in range(n_acc)},
        compiler_params=_CP, name="reduce_adamw")(me, full, slots, w, m, v, *(() if acc is None else acc))
    return tuple(outs)


_BIG = ("ffn1_w_gate", "ffn1_w_up", "ffn1_w_down", "w_in", "w_out", "ffn2_w_gate", "ffn2_w_up", "ffn2_w_down")
_TRANSPOSED = ("ffn1_w_gate", "ffn1_w_up", "w_in", "ffn2_w_gate", "ffn2_w_up")

def _block_diag(pool_w):
    out = jnp.zeros((L, PW, PW), pool_w.dtype)
    for gi in range(4):
        out = out.at[:, 64 * gi:64 * (gi + 1), 64 * gi:64 * (gi + 1)].set(pool_w[:, gi])
    return out


def kernel(x, positions, ffn1_norm, ffn1_w_gate, ffn1_w_up, ffn1_w_down, mix_norm, w_in, pool_w, pool_scale, w_out, ffn2_norm, ffn2_w_gate, ffn2_w_up, ffn2_w_down, final_norm, loss_target, m_ffn1_norm, m_ffn1_w_gate, m_ffn1_w_up, m_ffn1_w_down, m_mix_norm, m_w_in, m_pool_w, m_pool_scale, m_w_out, m_ffn2_norm, m_ffn2_w_gate, m_ffn2_w_up, m_ffn2_w_down, m_final_norm, v_ffn1_norm, v_ffn1_w_gate, v_ffn1_w_up, v_ffn1_w_down, v_mix_norm, v_w_in, v_pool_w, v_pool_scale, v_w_out, v_ffn2_norm, v_ffn2_w_gate, v_ffn2_w_up, v_ffn2_w_down, v_final_norm):
    weights = dict(ffn1_norm=ffn1_norm, ffn1_w_gate=ffn1_w_gate, ffn1_w_up=ffn1_w_up, ffn1_w_down=ffn1_w_down,
                   mix_norm=mix_norm, w_in=w_in, pool_w=pool_w, pool_scale=pool_scale, w_out=w_out,
                   ffn2_norm=ffn2_norm, ffn2_w_gate=ffn2_w_gate, ffn2_w_up=ffn2_w_up, ffn2_w_down=ffn2_w_down,
                   final_norm=final_norm)
    moms = dict(ffn1_norm=m_ffn1_norm, ffn1_w_gate=m_ffn1_w_gate, ffn1_w_up=m_ffn1_w_up, ffn1_w_down=m_ffn1_w_down,
                mix_norm=m_mix_norm, w_in=m_w_in, pool_w=m_pool_w, pool_scale=m_pool_scale, w_out=m_w_out,
                ffn2_norm=m_ffn2_norm, ffn2_w_gate=m_ffn2_w_gate, ffn2_w_up=m_ffn2_w_up, ffn2_w_down=m_ffn2_w_down,
                final_norm=m_final_norm)
    vels = dict(ffn1_norm=v_ffn1_norm, ffn1_w_gate=v_ffn1_w_gate, ffn1_w_up=v_ffn1_w_up, ffn1_w_down=v_ffn1_w_down,
                mix_norm=v_mix_norm, w_in=v_w_in, pool_w=v_pool_w, pool_scale=v_pool_scale, w_out=v_w_out,
                ffn2_norm=v_ffn2_norm, ffn2_w_gate=v_ffn2_w_gate, ffn2_w_up=v_ffn2_w_up, ffn2_w_down=v_ffn2_w_down,
                final_norm=v_final_norm)
    names = list(weights)

    me_idx = 4 * lax.axis_index("x") + 2 * lax.axis_index("y") + lax.axis_index("c")
    me_arr = me_idx.reshape(1).astype(jnp.int32)

    as_rows = lambda a, nm: jnp.swapaxes(a, 1, 2) if nm in _TRANSPOSED else a
    w_rows = {nm: as_rows(weights[nm], nm) for nm in _BIG}
    m_rows = {nm: as_rows(moms[nm], nm) for nm in _BIG}
    v_rows = {nm: as_rows(vels[nm], nm) for nm in _BIG}

    def landing_zones(l, which):
        return _place_own(me_arr, [w_rows[_BIG[t]] for t in which], l)

    g_ffn1 = [ffn1_norm[l].reshape(1, D) for l in range(L)]
    g_mix = [mix_norm[l].reshape(1, D) for l in range(L)]
    g_ffn2 = [ffn2_norm[l].reshape(1, D) for l in range(L)]
    wbd_all = _block_diag(pool_w).astype(bf16)
    wbd = [wbd_all[l] for l in range(L)]
    pscale = [pool_scale[l].reshape(1, PW) for l in range(L)]
    tabs = _rope_tables(positions)
    flat = lambda a: a.reshape(S, a.shape[-1])
    r4 = lambda a: a.reshape(4, S // 4, a.shape[-1])
    r16 = lambda a: a.reshape(16, S // 16, a.shape[-1])

    first, mid, rest, whole = (0, 1, 2), (3, 4), (5, 6, 7), tuple(range(8))

    def ag_begin(l, which, after, zones=None):
        tag = f"{l}{'' if which == whole else 'h' if which == first else 'm' if which == mid else 'r'}"
        zones = landing_zones(l, which) if zones is None else zones
        send_sems, recv_sems, zones, token = _ag_start(zones, after, tag)
        return dict(tag=tag, zones=zones, s=send_sems, r=recv_sems), token

    def ag_second(ch, after):
        ch["ps"], ch["pr"], ch["zones"], token = _ag_pass(ch["zones"], ch["r"], after, ch["tag"])
        return token

    def ag_third(ch, after):
        ch["qs"], ch["qr"], ch["zones"], token = _ag_last(ch["zones"], ch["pr"], after, ch["tag"])
        return token

    def ag_end(ch, after):
        return _ag_wait(ch["zones"], ch["s"], ch["r"], ch["ps"], ch["pr"], ch["qs"], ch["qr"], after, ch["tag"])

    ch_head, token = ag_begin(0, first, [])
    ch_mid, token = ag_begin(0, mid, token)
    zones_rest, zones_next = landing_zones(0, rest), landing_zones(1, whole)
    early_zones = {ll: landing_zones(ll, whole) for ll in range(2, L)}
    fill = [z for zs in (zones_rest, zones_next, *early_zones.values(), tabs, wbd, [token]) for z in zs]
    head = ag_end(ch_head, ag_third(ch_head, ag_second(ch_head, fill)))
    tok_mid = ag_second(ch_mid, head[0])
    ch_rest, tok_rest = ag_begin(0, rest, tok_mid, zones_rest)
    chains = {}
    chains[1], tok_next = ag_begin(1, whole, head[0], zones_next)
    gathered = [None] * L
    xs = x.reshape(S, D)
    saved = []
    for l in range(L):
        first_after, second_after = (), ()
        if l == 0:
            gt1, ut1, dn1 = head
            first_after = (tok_rest, tok_next)
        else:
            gt1, ut1, dn1, wint, wout, gt2, ut2, dn2 = gathered[l]
        x0 = xs
        x1, gate1, up1 = _ffn_fwd(x0, g_ffn1[l], gt1, ut1, dn1, after=first_after)
        if l == 0:
            wint, wout = ag_end(ch_mid, ag_second(ch_rest, ag_third(ch_mid, x1)))
        hmix, vp, q1, k1, v1, q4, k4, v4, q16, k16, v16 = _mix_in_fwd(x1, g_mix[l], wint, tabs)
        q4, k4, v4, q16, k16, v16 = map(flat, (q4, k4, v4, q16, k16, v16))
        ypool, diff = _pool_fwd(vp, wbd[l], pscale[l])
        o1, l1 = _attn_fwd(q1, k1, v1, S)
        o4, l4 = _attn_fwd(q4, k4, v4, S // 4)
        o16, l16 = _attn_fwd(q16, k16, v16, S // 16)
        if l < L - 1:
            second_after = (ag_second(chains[l + 1], [o1, o4, o16]),)
        x2, mixed, o, lse1, lse4, lse16 = _mix_out_fwd(x1, ypool, o1, l1, r4(o4), r4(l4), r16(o16), r16(l16), wout)
        if l == 0:
            token = ag_third(ch_rest, x2)
            gt2, ut2, dn2 = ag_end(ch_rest, token)
            gathered[0] = list(head) + [wint, wout, gt2, ut2, dn2]
        x3, gate2, up2 = _ffn_fwd(x2, g_ffn2[l], gt2, ut2, dn2, after=second_after)
        if l + 1 < L:
            token = ag_third(chains[l + 1], x3)
            if l + 2 < L:
                chains[l + 2], token = ag_begin(l + 2, whole, token, early_zones[l + 2])
            gathered[l + 1] = ag_end(chains[l + 1], token)
        saved.append(dict(x0=x0, x1=x1, x2=x2, gate1=gate1, up1=up1, gate2=gate2, up2=up2, hmix=hmix, diff=diff,
                          qkv=((q1, k1, v1), (q4, k4, v4), (q16, k16, v16)), mixed=mixed, o=o,
                          lse=(lse1, flat(lse4), flat(lse16))))
        xs = x3

    dx, loss_part, d_final = _loss_head(xs, final_norm.reshape(1, D), loss_target.reshape(S, D))

    d_norm = {nm: [None] * L for nm in ("ffn1_norm", "mix_norm", "ffn2_norm")}
    d_poolw, d_pscale = [None] * L, [None] * L
    group_a = ("ffn2_w_gate", "ffn2_w_up", "ffn2_w_down", "w_out")
    group_b = ("ffn1_w_gate", "ffn1_w_up", "ffn1_w_down", "w_in")
    acc = {}

    def exchange(full, group, after, tag):
        srcs = [full[nm] for nm in group]
        slots = [lax.empty((NDEV, g.shape[0] // NDEV, D), bf16) for g in srcs]
        ssem, rsem, srcs, slots, token = _rs_start(srcs, slots, after, tag)
        return (srcs, slots, ssem, rsem, tag), token

    def update(l, group, flight, after):
        srcs, slots, ssem, rsem, tag = flight
        srcs, slots = _rs_wait(srcs, slots, ssem, rsem, after, tag)
        for nm, full_g, slots_g in zip(group, srcs, slots):
            acc[nm] = _reduce_adamw(acc.get(nm), me_arr, full_g, slots_g, w_rows[nm], m_rows[nm], v_rows[nm], l)
        return [acc[nm][0] for nm in group], slots

    core_arr = lax.axis_index("c").reshape(1).astype(jnp.int32)
    chip_arr = (2 * lax.axis_index("x") + lax.axis_index("y")).reshape(1).astype(jnp.int32)

    def exchange_cores(full, group, after, tag):
        full4s = [full[nm].reshape(4, 2, full[nm].shape[0] // NDEV, D) for nm in group]
        bufs = [lax.empty((4,) + a.shape[2:], bf16) for a in full4s]
        ssem, rsem, full4s, bufs, token = _pair_start(full4s, bufs, after, tag)
        return (full4s, bufs, ssem, rsem, tag), token

    def exchange_chips(flight, after):
        full4s, bufs, ssem, rsem, tag = flight
        full4s, bufs = _pair_wait(full4s, bufs, ssem, rsem, after, tag)
        sums = _pair_sum(core_arr, full4s, bufs)
        slots = [lax.empty(a.shape, bf16) for a in sums]
        ssem, rsem, sums, slots, token = _chip_start(sums, slots, bufs[0], tag)
        return (sums, slots, ssem, rsem, tag), token

    def update_chips(l, group, flight, after):
        sums, slots, ssem, rsem, tag = flight
        sums, slots = _chip_wait(sums, slots, ssem, rsem, after, tag)
        for nm, sums_g, slots_g in zip(group, sums, slots):
            own = sums_g.reshape(4 * sums_g.shape[1], D)
            acc[nm] = _reduce_adamw(acc.get(nm), chip_arr, own, slots_g, w_rows[nm], m_rows[nm], v_rows[nm], l)
        return [acc[nm][0] for nm in group]

    flights = {}
    token_b = None
    for l in reversed(range(L)):
        sv = saved[l]
        gt1, ut1, dn1, wint, wout, gt2, ut2, dn2 = gathered[l]
        full = {}
        dx, dgate, dup, h, dy, d_norm["ffn2_norm"][l] = _ffn_bwd_d(
            sv["x2"], g_ffn2[l], sv["gate2"], sv["up2"], dx, gt2, ut2, dn2, after=() if token_b is None else (token_b,))
        full["ffn2_w_gate"], full["ffn2_w_up"], full["ffn2_w_down"] = _ffn_bwd_w(h, dy, sv["gate2"], sv["up2"], dgate, dup)

        dxb, dyp, do1, do4, do16, dl1, dl4, dl16 = _mix_out_bwd(dx, sv["o"], wout)
        full["w_out"] = _wgrad(sv["mixed"], dxb)
        flights[l, "a"], token_a = (exchange_cores if l == 0 else exchange)(full, group_a, dxb, f"a{l}")
        dvp, d_poolw[l], d_pscale[l] = _pool_bwd(dyp, sv["diff"], wbd[l], pscale[l], after=(token_a,))
        dos, dls = (do1, flat(do4), flat(do16)), (dl1, flat(dl4), flat(dl16))
        dqkv = []
        for b, lc in enumerate((S, S // 4, S // 16)):
            qb, kb, vb = sv["qkv"][b]
            dqkv.append(_attn_bwd(qb, kb, vb, dos[b], sv["lse"][b], dls[b], lc))
        d4 = tuple(r4(a) for a in dqkv[1])
        d16 = tuple(r16(a) for a in dqkv[2])
        mix_after = ()
        if l == 0:
            flights[0, "a"], token_a = exchange_chips(flights[0, "a"], [dqkv[0][0], dqkv[1][0], dqkv[2][0]])
            mix_after = (token_a,)
        dx, dproj, d_norm["mix_norm"][l] = _mix_in_bwd(dx, sv["x1"], g_mix[l], wint, tabs, dvp, dqkv[0], d4, d16,
                                                       after=mix_after)
        full["w_in"] = _wgrad(dproj, sv["hmix"])

        dx, dgate, dup, h, dy, d_norm["ffn1_norm"][l] = _ffn_bwd_d(sv["x0"], g_ffn1[l], sv["gate1"], sv["up1"], dx, gt1, ut1, dn1)
        full["ffn1_w_gate"], full["ffn1_w_up"], full["ffn1_w_down"] = _ffn_bwd_w(h, dy, sv["gate1"], sv["up1"], dgate, dup)

        after = dx
        if l + 1 < L and l + 1 >= 2:
            after, _ = update(l + 1, group_a, flights.pop((l + 1, "a")), after)
        if l + 1 < L and l + 1 >= 3:
            after, _ = update(l + 1, group_b, flights.pop((l + 1, "b")), after)
        if l > 0:
            flights[l, "b"], token_b = exchange(full, group_b, after, f"b{l}")

    flights[0, "b"], token_b = exchange_cores(full, group_b, dx, "b0")
    pad8 = lambda a: jnp.pad(a, ((0, 8 - a.shape[0]), (0, 0)))
    misc = jnp.concatenate([d_final, jnp.concatenate(d_pscale, axis=1), loss_part], axis=0)
    small = jnp.concatenate(
        [pad8(jnp.concatenate(d_norm[nm], axis=0)) for nm in ("ffn1_norm", "mix_norm", "ffn2_norm")]
        + [pad8(misc), jnp.stack(d_poolw).reshape(L * 16, D)], axis=0)
    small_slots = lax.dynamic_update_slice(lax.empty((NDEV, SMALL_ROWS, D), f32), small[None], (me_idx, 0, 0))
    pack_sems = _rs_start([small], [small_slots], token_b, "pack")
    flights[0, "b"], token_b = exchange_chips(flights[0, "b"], pack_sems[-1])

    after = token_b
    for key in [(2, "b"), (1, "a"), (1, "b")]:
        after, _ = update(key[0], group_a if key[1] == "a" else group_b, flights.pop(key), after)
    _, pack_slots = _rs_wait(pack_sems[2], pack_sems[3], pack_sems[0], pack_sems[1], after, "pack")
    sm = _sum_slots(pack_slots[0], SMALL_ROWS)
    grads = {}
    grads["ffn1_norm"], grads["mix_norm"], grads["ffn2_norm"] = sm[0:L], sm[8:8 + L], sm[16:16 + L]
    grads["final_norm"] = sm[24]
    grads["pool_scale"] = sm[25].reshape(L, PW)
    grads["pool_w"] = sm[32:32 + L * 16].reshape(L, 4, 64, 64)
    loss = sm[26, 0]
    upd = {nm: _adamw(weights[nm], grads[nm], moms[nm], vels[nm]) for nm in names if nm not in _BIG}
    after = update_chips(0, group_a, flights.pop((0, "a")), [upd[nm][0] for nm in upd])
    update_chips(0, group_b, flights.pop((0, "b")), after)
    for nm in _BIG:
        grads[nm], upd[nm] = as_rows(acc[nm][0], nm), tuple(as_rows(a, nm) for a in acc[nm][1:])
    return (loss, dx.reshape(1, S, D), *[grads[nm] for nm in names], *[upd[nm][0] for nm in names],
            *[upd[nm][1] for nm in names], *[upd[nm][2] for nm in names])
```
